```python
import jax, jax.numpy as jnp
from jax import lax
import numpy as np

D_MODEL = 1024
BATCH = 16
SEQ = 2048
DEPTH = 2

CHUNK = 64
N_MIXERS = 2
EXPAND = 2
D_INNER = EXPAND * D_MODEL
SG_BLOCK = 128
SG_GROUPS = 8
SG_GROUP_DIM = D_INNER // SG_GROUPS
HG_HEADS = 16
HG_HEAD_DIM = D_INNER // HG_HEADS
N_LAYERS_A = (DEPTH + 1) // 2
N_LAYERS_B = DEPTH // 2
EPS = 1e-6

kernel_name = "hybrid_gmlp_hgrn2_adaln_trunk"


def rms_norm(x, gain):
    xf = x.astype(jnp.float32)
    y = xf * lax.rsqrt(jnp.mean(xf * xf, axis=-1, keepdims=True) + EPS)
    return (y * gain.astype(jnp.float32)).astype(x.dtype)


def layer_norm(x, gain, bias):
    xf = x.astype(jnp.float32)
    mu = jnp.mean(xf, axis=-1, keepdims=True)
    var = jnp.mean(jnp.square(xf - mu), axis=-1, keepdims=True)
    y = (xf - mu) * lax.rsqrt(var + EPS) * gain.astype(jnp.float32) + bias.astype(jnp.float32)
    return y.astype(x.dtype)


def spatial_gating_mixer(h, w_in, ln_gain, ln_bias, w_s, b_s, w_out):
    bsz, seq, _ = h.shape
    proj = h @ w_in
    uv, g = proj[..., : 2 * D_INNER], proj[..., 2 * D_INNER:]
    uv = jax.nn.gelu(uv)
    u, v = uv[..., :D_INNER], uv[..., D_INNER:]
    v = layer_norm(v, ln_gain, ln_bias)
    nb = seq // SG_BLOCK
    v = v.reshape(bsz, nb, SG_BLOCK, SG_GROUPS, SG_GROUP_DIM)
    pos = jnp.arange(SG_BLOCK)
    mask = (pos[None, :] // CHUNK) <= (pos[:, None] // CHUNK)
    ws = jnp.where(mask[None], w_s, jnp.zeros((), w_s.dtype))
    s = jnp.einsum('gts,bnsgd->bntgd', ws, v) + b_s.T[None, None, :, :, None]
    s = s.reshape(bsz, seq, D_INNER)
    y = u * s * jax.nn.silu(g)
    return y @ w_out


def hgrn2_mixer(h, w_in, lower_bound, gn_gain, w_out):
    bsz, seq, _ = h.shape
    f32 = jnp.float32
    proj = h @ w_in
    q = proj[..., :D_INNER]
    f = proj[..., D_INNER: 2 * D_INNER]
    i = proj[..., 2 * D_INNER: 3 * D_INNER]
    g = proj[..., 3 * D_INNER:]
    q = jax.nn.silu(q.astype(f32))
    lb = lower_bound.astype(f32)
    f = lb + (1.0 - lb) * jax.nn.sigmoid(f.astype(f32))
    k = 1.0 - f
    log_f = jnp.log(f)
    nc = seq // CHUNK

    def heads(z):
        return z.reshape(bsz, nc, CHUNK, HG_HEADS, HG_HEAD_DIM).transpose(0, 3, 1, 2, 4)

    q, k, v, log_f = heads(q), heads(k), heads(i.astype(f32)), heads(log_f)
    a = jnp.cumsum(log_f, axis=3)
    a_ref = a[:, :, :, CHUNK // 2 - 1: CHUNK // 2, :]
    a_last = a[:, :, :, CHUNK - 1:, :]
    q_in = q * jnp.exp(a - a_ref)
    k_in = k * jnp.exp(a_ref - a)
    scores = jnp.einsum('bhnck,bhnsk->bhncs', q_in, k_in)
    causal = jnp.tril(jnp.ones((CHUNK, CHUNK), dtype=bool))
    scores = jnp.where(causal, scores, jnp.zeros((), f32))
    o_intra = jnp.einsum('bhncs,bhnsv->bhncv', scores, v)
    q_out = q * jnp.exp(a)
    k_out = k * jnp.exp(a_last - a)
    decay = jnp.exp(a_last[:, :, :, 0, :])

    def step(state, xs):
        q_c, k_c, v_c, d_c = xs
        o_c = jnp.einsum('bhck,bhkv->bhcv', q_c, state)
        state = d_c[..., None] * state + jnp.einsum('bhck,bhcv->bhkv', k_c, v_c)
        return state, o_c

    xs = (jnp.moveaxis(q_out, 2, 0), jnp.moveaxis(k_out, 2, 0),
          jnp.moveaxis(v, 2, 0), jnp.moveaxis(decay, 2, 0))
    init = jnp.zeros((bsz, HG_HEADS, HG_HEAD_DIM, HG_HEAD_DIM), f32)
    _, o_inter = lax.scan(step, init, xs)
    o = o_intra + jnp.moveaxis(o_inter, 0, 2)
    o = o.transpose(0, 2, 3, 1, 4)
    o = rms_norm(o, gn_gain)
    o = o.reshape(bsz, seq, D_INNER).astype(h.dtype)
    return (o * jax.nn.silu(g)) @ w_out


def _fwd_setup_inputs(seed: int = 0) -> dict:
    key = jax.random.key(seed)
    ks = jax.random.split(key, 20)
    nrm = jax.random.normal
    f32 = jnp.float32
    D, DI = D_MODEL, D_INNER
    return {
        "x": nrm(ks[0], (BATCH, SEQ, D), f32),
        "c": nrm(ks[1], (BATCH, D), f32),
        "norm_gain": 1.0 + 0.02 * nrm(ks[2], (DEPTH, D), f32),
        "w_ada": 0.5 * D ** -0.5 * nrm(ks[3], (DEPTH, D, 3 * D), f32),
        "b_ada": 0.02 * nrm(ks[4], (DEPTH, 3 * D), f32),
        "a_w_in": D ** -0.5 * nrm(ks[5], (N_LAYERS_A, D, 3 * DI), f32),
        "a_ln_gain": 1.0 + 0.02 * nrm(ks[6], (N_LAYERS_A, DI), f32),
        "a_ln_bias": 0.02 * nrm(ks[7], (N_LAYERS_A, DI), f32),
        "a_w_s": SG_BLOCK ** -0.5 * nrm(ks[8], (N_LAYERS_A, SG_GROUPS, SG_BLOCK, SG_BLOCK), f32),
        "a_b_s": 1.0 + 0.02 * nrm(ks[9], (N_LAYERS_A, SG_GROUPS, SG_BLOCK), f32),
        "a_w_out": DI ** -0.5 * nrm(ks[10], (N_LAYERS_A, DI, D), f32),
        "b_w_in": D ** -0.5 * nrm(ks[11], (N_LAYERS_B, D, 4 * DI), f32),
        "b_lower_bounds": 0.1 * nrm(ks[12], (DEPTH, DI), f32),
        "b_gn_gain": 1.0 + 0.02 * nrm(ks[13], (N_LAYERS_B, HG_HEAD_DIM), f32),
        "b_w_out": DI ** -0.5 * nrm(ks[14], (N_LAYERS_B, DI, D), f32),
        "final_gain": 1.0 + 0.02 * nrm(ks[15], (D,), f32),
    }


def _fwd_reference(x, c, norm_gain, w_ada, b_ada, a_w_in, a_ln_gain, a_ln_bias, a_w_s, a_b_s,
              a_w_out, b_w_in, b_lower_bounds, b_gn_gain, b_w_out, final_gain):
    p = jax.nn.softmax(b_lower_bounds.astype(jnp.float32), axis=0)
    cum = jnp.cumsum(p, axis=0)
    lower_bounds = cum - cum[0:1]
    c_act = jax.nn.silu(c)
    for layer in range(DEPTH):
        mod = c_act @ w_ada[layer] + b_ada[layer]
        shift = mod[:, None, :D_MODEL]
        scale = mod[:, None, D_MODEL: 2 * D_MODEL]
        gate = mod[:, None, 2 * D_MODEL:]
        h = rms_norm(x, norm_gain[layer]) * (1.0 + scale) + shift
        j = layer // N_MIXERS
        if layer % N_MIXERS == 0:
            y = spatial_gating_mixer(h, a_w_in[j], a_ln_gain[j], a_ln_bias[j],
                                     a_w_s[j], a_b_s[j], a_w_out[j])
        else:
            y = hgrn2_mixer(h, b_w_in[j], lower_bounds[layer], b_gn_gain[j], b_w_out[j])
        x = x + gate * y
    return rms_norm(x, final_gain)


import jax as _jax
import jax.numpy as _jnp

TWIN_FORMAT = 'train_step'
FWD_PARAMS = ['x', 'c', 'norm_gain', 'w_ada', 'b_ada', 'a_w_in', 'a_ln_gain', 'a_ln_bias', 'a_w_s', 'a_b_s', 'a_w_out', 'b_w_in', 'b_lower_bounds', 'b_gn_gain', 'b_w_out', 'final_gain']
TWIN_WEIGHTS = ['norm_gain', 'w_ada', 'b_ada', 'a_w_in', 'a_ln_gain', 'a_ln_bias', 'a_w_s', 'a_b_s', 'a_w_out', 'b_w_in', 'b_lower_bounds', 'b_gn_gain', 'b_w_out', 'final_gain']
TWIN_DIFF_INPUT = 'x'
TWIN_INPUTS = ['x', 'c', 'norm_gain', 'w_ada', 'b_ada', 'a_w_in', 'a_ln_gain', 'a_ln_bias', 'a_w_s', 'a_b_s', 'a_w_out', 'b_w_in', 'b_lower_bounds', 'b_gn_gain', 'b_w_out', 'final_gain', 'loss_target', 'm_norm_gain', 'm_w_ada', 'm_b_ada', 'm_a_w_in', 'm_a_ln_gain', 'm_a_ln_bias', 'm_a_w_s', 'm_a_b_s', 'm_a_w_out', 'm_b_w_in', 'm_b_lower_bounds', 'm_b_gn_gain', 'm_b_w_out', 'm_final_gain', 'v_norm_gain', 'v_w_ada', 'v_b_ada', 'v_a_w_in', 'v_a_ln_gain', 'v_a_ln_bias', 'v_a_w_s', 'v_a_b_s', 'v_a_w_out', 'v_b_w_in', 'v_b_lower_bounds', 'v_b_gn_gain', 'v_b_w_out', 'v_final_gain']
TWIN_OUTPUTS = ['loss', 'grad_x', 'grad_norm_gain', 'grad_w_ada', 'grad_b_ada', 'grad_a_w_in', 'grad_a_ln_gain', 'grad_a_ln_bias', 'grad_a_w_s', 'grad_a_b_s', 'grad_a_w_out', 'grad_b_w_in', 'grad_b_lower_bounds', 'grad_b_gn_gain', 'grad_b_w_out', 'grad_final_gain', 'delta_norm_gain', 'delta_w_ada', 'delta_b_ada', 'delta_a_w_in', 'delta_a_ln_gain', 'delta_a_ln_bias', 'delta_a_w_s', 'delta_a_b_s', 'delta_a_w_out', 'delta_b_w_in', 'delta_b_lower_bounds', 'delta_b_gn_gain', 'delta_b_w_out', 'delta_final_gain', 'new_m_norm_gain', 'new_m_w_ada', 'new_m_b_ada', 'new_m_a_w_in', 'new_m_a_ln_gain', 'new_m_a_ln_bias', 'new_m_a_w_s', 'new_m_a_b_s', 'new_m_a_w_out', 'new_m_b_w_in', 'new_m_b_lower_bounds', 'new_m_b_gn_gain', 'new_m_b_w_out', 'new_m_final_gain', 'new_v_norm_gain', 'new_v_w_ada', 'new_v_b_ada', 'new_v_a_w_in', 'new_v_a_ln_gain', 'new_v_a_ln_bias', 'new_v_a_w_s', 'new_v_a_b_s', 'new_v_a_w_out', 'new_v_b_w_in', 'new_v_b_lower_bounds', 'new_v_b_gn_gain', 'new_v_b_w_out', 'new_v_final_gain']
TWIN_LEAF_KINDS = {'loss': 'loss', 'grad_x': 'grad_x', 'grad_norm_gain': 'grad_w', 'grad_w_ada': 'grad_w', 'grad_b_ada': 'grad_w', 'grad_a_w_in': 'grad_w', 'grad_a_ln_gain': 'grad_w', 'grad_a_ln_bias': 'grad_w', 'grad_a_w_s': 'grad_w', 'grad_a_b_s': 'grad_w', 'grad_a_w_out': 'grad_w', 'grad_b_w_in': 'grad_w', 'grad_b_lower_bounds': 'grad_w', 'grad_b_gn_gain': 'grad_w', 'grad_b_w_out': 'grad_w', 'grad_final_gain': 'grad_w', 'delta_norm_gain': 'delta_w', 'delta_w_ada': 'delta_w', 'delta_b_ada': 'delta_w', 'delta_a_w_in': 'delta_w', 'delta_a_ln_gain': 'delta_w', 'delta_a_ln_bias': 'delta_w', 'delta_a_w_s': 'delta_w', 'delta_a_b_s': 'delta_w', 'delta_a_w_out': 'delta_w', 'delta_b_w_in': 'delta_w', 'delta_b_lower_bounds': 'delta_w', 'delta_b_gn_gain': 'delta_w', 'delta_b_w_out': 'delta_w', 'delta_final_gain': 'delta_w', 'new_m_norm_gain': 'new_m', 'new_m_w_ada': 'new_m', 'new_m_b_ada': 'new_m', 'new_m_a_w_in': 'new_m', 'new_m_a_ln_gain': 'new_m', 'new_m_a_ln_bias': 'new_m', 'new_m_a_w_s': 'new_m', 'new_m_a_b_s': 'new_m', 'new_m_a_w_out': 'new_m', 'new_m_b_w_in': 'new_m', 'new_m_b_lower_bounds': 'new_m', 'new_m_b_gn_gain': 'new_m', 'new_m_b_w_out': 'new_m', 'new_m_final_gain': 'new_m', 'new_v_norm_gain': 'new_v', 'new_v_w_ada': 'new_v', 'new_v_b_ada': 'new_v', 'new_v_a_w_in': 'new_v', 'new_v_a_ln_gain': 'new_v', 'new_v_a_ln_bias': 'new_v', 'new_v_a_w_s': 'new_v', 'new_v_a_b_s': 'new_v', 'new_v_a_w_out': 'new_v', 'new_v_b_w_in': 'new_v', 'new_v_b_lower_bounds': 'new_v', 'new_v_b_gn_gain': 'new_v', 'new_v_b_w_out': 'new_v', 'new_v_final_gain': 'new_v'}


def _forward(args):
    return _fwd_reference(*[args[k] for k in FWD_PARAMS])


def _output_shape():
    out = _jax.eval_shape(lambda: _forward(_fwd_setup_inputs(0)))
    return out.shape, out.dtype

N_MICROBATCH = 1
ADAM_LR = 0.001
ADAM_B1 = 0.9
ADAM_B2 = 0.999
ADAM_EPS = 1e-08
ADAM_WD = 0.01
ADAM_STEP = 10
PER_EXAMPLE_BATCH_AXIS = {'x': 0, 'c': 0, 'loss_target': 0}
SHARED_INPUTS = []
_WEIGHT_DTYPES = {'norm_gain': _jnp.float32, 'w_ada': _jnp.float32, 'b_ada': _jnp.float32, 'a_w_in': _jnp.float32, 'a_ln_gain': _jnp.float32, 'a_ln_bias': _jnp.float32, 'a_w_s': _jnp.float32, 'a_b_s': _jnp.float32, 'a_w_out': _jnp.float32, 'b_w_in': _jnp.float32, 'b_lower_bounds': _jnp.float32, 'b_gn_gain': _jnp.float32, 'b_w_out': _jnp.float32, 'final_gain': _jnp.float32}
MOMENT_SCALE = {'norm_gain': 4.983153e-02, 'w_ada': 5.123898e-02, 'b_ada': 8.404101e-02, 'a_w_in': 2.296712e-02, 'a_ln_gain': 1.682496e-02, 'a_ln_bias': 1.611198e-02, 'a_w_s': 2.236159e-02, 'a_b_s': 2.627610e-02, 'a_w_out': 3.418074e-02, 'b_w_in': 1.789024e-02, 'b_lower_bounds': 1.991411e-03, 'b_gn_gain': 1.085465e-01, 'b_w_out': 3.475366e-02, 'final_gain': 3.197523e+01}


def _to_microbatches(a, axis):
    t = _jnp.moveaxis(a, axis, 0)
    t = t.reshape((N_MICROBATCH, t.shape[0] // N_MICROBATCH) + t.shape[1:])
    return _jnp.moveaxis(t, 1, axis + 1)


def setup_inputs(seed: int = 0) -> dict:
    inp = _fwd_setup_inputs(seed)
    key = _jax.random.fold_in(_jax.random.key(seed), 7919)
    shape, _ = _output_shape()
    out = dict(inp)
    out["loss_target"] = _jax.random.normal(_jax.random.fold_in(key, 0), shape, _jnp.float32)
    for i, name in enumerate(TWIN_WEIGHTS):
        w = inp[name].astype(_jnp.float32)
        if MOMENT_SCALE is None:
            s = _jnp.sqrt(_jnp.mean(_jnp.square(w)) + 1e-30)
        else:
            s = MOMENT_SCALE[name]
        km, kv = _jax.random.split(_jax.random.fold_in(key, i + 1))
        out[name] = w
        out["m_" + name] = s * _jax.random.normal(km, w.shape, _jnp.float32)
        out["v_" + name] = (s * s) * _jax.random.uniform(kv, w.shape, _jnp.float32, 0.5, 1.5)
    if N_MICROBATCH > 1:
        for name, axis in PER_EXAMPLE_BATCH_AXIS.items():
            out[name] = _to_microbatches(out[name], axis)
    return {'x': out['x'], 'c': out['c'], 'norm_gain': out['norm_gain'], 'w_ada': out['w_ada'], 'b_ada': out['b_ada'], 'a_w_in': out['a_w_in'], 'a_ln_gain': out['a_ln_gain'], 'a_ln_bias': out['a_ln_bias'], 'a_w_s': out['a_w_s'], 'a_b_s': out['a_b_s'], 'a_w_out': out['a_w_out'], 'b_w_in': out['b_w_in'], 'b_lower_bounds': out['b_lower_bounds'], 'b_gn_gain': out['b_gn_gain'], 'b_w_out': out['b_w_out'], 'final_gain': out['final_gain'], 'loss_target': out['loss_target'], 'm_norm_gain': out['m_norm_gain'], 'm_w_ada': out['m_w_ada'], 'm_b_ada': out['m_b_ada'], 'm_a_w_in': out['m_a_w_in'], 'm_a_ln_gain': out['m_a_ln_gain'], 'm_a_ln_bias': out['m_a_ln_bias'], 'm_a_w_s': out['m_a_w_s'], 'm_a_b_s': out['m_a_b_s'], 'm_a_w_out': out['m_a_w_out'], 'm_b_w_in': out['m_b_w_in'], 'm_b_lower_bounds': out['m_b_lower_bounds'], 'm_b_gn_gain': out['m_b_gn_gain'], 'm_b_w_out': out['m_b_w_out'], 'm_final_gain': out['m_final_gain'], 'v_norm_gain': out['v_norm_gain'], 'v_w_ada': out['v_w_ada'], 'v_b_ada': out['v_b_ada'], 'v_a_w_in': out['v_a_w_in'], 'v_a_ln_gain': out['v_a_ln_gain'], 'v_a_ln_bias': out['v_a_ln_bias'], 'v_a_w_s': out['v_a_w_s'], 'v_a_b_s': out['v_a_b_s'], 'v_a_w_out': out['v_a_w_out'], 'v_b_w_in': out['v_b_w_in'], 'v_b_lower_bounds': out['v_b_lower_bounds'], 'v_b_gn_gain': out['v_b_gn_gain'], 'v_b_w_out': out['v_b_w_out'], 'v_final_gain': out['v_final_gain']}


def _loss(weights, diff, rest, loss_target):
    with _jax.named_scope("forward"):
        args = {**rest, TWIN_DIFF_INPUT: diff, **{k: w.astype(_WEIGHT_DTYPES[k]) for k, w in weights.items()}}
        y = _forward(args)
    with _jax.named_scope("loss_head"):
        err = _jnp.square(y.astype(_jnp.float32) - loss_target)
        return 0.5 * _jnp.sum(_jnp.mean(err, axis=-1)) if err.ndim else 0.5 * err


def _adamw(w, g, m, v):
    m = ADAM_B1 * m + (1.0 - ADAM_B1) * g
    v = ADAM_B2 * v + (1.0 - ADAM_B2) * _jnp.square(g)
    m_hat = m / (1.0 - ADAM_B1 ** ADAM_STEP)
    v_hat = v / (1.0 - ADAM_B2 ** ADAM_STEP)
    delta = -ADAM_LR * (m_hat / (_jnp.sqrt(v_hat) + ADAM_EPS) + ADAM_WD * w)
    return delta, m, v


def reference(x, c, norm_gain, w_ada, b_ada, a_w_in, a_ln_gain, a_ln_bias, a_w_s, a_b_s, a_w_out, b_w_in, b_lower_bounds, b_gn_gain, b_w_out, final_gain, loss_target, m_norm_gain, m_w_ada, m_b_ada, m_a_w_in, m_a_ln_gain, m_a_ln_bias, m_a_w_s, m_a_b_s, m_a_w_out, m_b_w_in, m_b_lower_bounds, m_b_gn_gain, m_b_w_out, m_final_gain, v_norm_gain, v_w_ada, v_b_ada, v_a_w_in, v_a_ln_gain, v_a_ln_bias, v_a_w_s, v_a_b_s, v_a_w_out, v_b_w_in, v_b_lower_bounds, v_b_gn_gain, v_b_w_out, v_final_gain):
    given = dict(x=x, c=c, norm_gain=norm_gain, w_ada=w_ada, b_ada=b_ada, a_w_in=a_w_in, a_ln_gain=a_ln_gain, a_ln_bias=a_ln_bias, a_w_s=a_w_s, a_b_s=a_b_s, a_w_out=a_w_out, b_w_in=b_w_in, b_lower_bounds=b_lower_bounds, b_gn_gain=b_gn_gain, b_w_out=b_w_out, final_gain=final_gain, loss_target=loss_target, m_norm_gain=m_norm_gain, m_w_ada=m_w_ada, m_b_ada=m_b_ada, m_a_w_in=m_a_w_in, m_a_ln_gain=m_a_ln_gain, m_a_ln_bias=m_a_ln_bias, m_a_w_s=m_a_w_s, m_a_b_s=m_a_b_s, m_a_w_out=m_a_w_out, m_b_w_in=m_b_w_in, m_b_lower_bounds=m_b_lower_bounds, m_b_gn_gain=m_b_gn_gain, m_b_w_out=m_b_w_out, m_final_gain=m_final_gain, v_norm_gain=v_norm_gain, v_w_ada=v_w_ada, v_b_ada=v_b_ada, v_a_w_in=v_a_w_in, v_a_ln_gain=v_a_ln_gain, v_a_ln_bias=v_a_ln_bias, v_a_w_s=v_a_w_s, v_a_b_s=v_a_b_s, v_a_w_out=v_a_w_out, v_b_w_in=v_b_w_in, v_b_lower_bounds=v_b_lower_bounds, v_b_gn_gain=v_b_gn_gain, v_b_w_out=v_b_w_out, v_final_gain=v_final_gain)
    weights = {n: given[n] for n in TWIN_WEIGHTS}
    shared = {n: given[n] for n in SHARED_INPUTS}
    per_example = {n: given[n] for n in ['x', 'c']}
    grad_fn = _jax.value_and_grad(_loss, argnums=(0, 1))

    def one_microbatch(ex, loss_target):
        ex = dict(ex)
        diff = ex.pop(TWIN_DIFF_INPUT)
        return grad_fn(weights, diff, {**shared, **ex}, loss_target)

    if N_MICROBATCH == 1:
        loss, (grad_w, grad_x) = one_microbatch(per_example, given["loss_target"])
    else:
        def body(carry, xs):
            loss_sum, grad_sum = carry
            l_k, (gw_k, gx_k) = one_microbatch(xs[0], xs[1])
            with _jax.named_scope("update"):
                return (loss_sum + l_k, _jax.tree.map(_jnp.add, grad_sum, gw_k)), gx_k

        init = (_jnp.zeros((), _jnp.float32), _jax.tree.map(_jnp.zeros_like, weights))
        (loss, grad_w), grad_x = _jax.lax.scan(body, init, (per_example, given["loss_target"]))
    with _jax.named_scope("update"):
        delta_w, new_m, new_v = {}, {}, {}
        for n in TWIN_WEIGHTS:
            delta_w[n], new_m[n], new_v[n] = _adamw(weights[n], grad_w[n], given["m_" + n], given["v_" + n])
    return (loss, grad_x, *[grad_w[n] for n in TWIN_WEIGHTS], *[delta_w[n] for n in TWIN_WEIGHTS],
            *[new_m[n] for n in TWIN_WEIGHTS], *[new_v[n] for n in TWIN_WEIGHTS])
```

```python
import functools

import jax
import jax.numpy as jnp
from jax import lax
from jax.experimental import pallas as pl
from jax.experimental.pallas import tpu as pltpu

F32 = jnp.float32
BF16 = jnp.bfloat16
EPS = 1e-6
CHUNK = 64
SG_BLOCK = 128
SG_GROUPS = 8
HEAD_DIM = 128
N_CHIPS = 4
N_DEV = 8
LANES = 128
ADAM_LR = 0.001
ADAM_B1 = 0.9
ADAM_B2 = 0.999
ADAM_EPS = 1e-08
ADAM_WD = 0.01
ADAM_STEP = 10
GELU_C0 = 0.7978845608028654
GELU_C1 = 0.044715
MESH = pl.DeviceIdType.MESH
VMEM_LIMIT = 56 * 1024 * 1024


def _call(body, **kw):
    return pl.pallas_call(body, **kw)


def _params(**kw):
    return pltpu.CompilerParams(vmem_limit_bytes=VMEM_LIMIT, **kw)


def _sigmoid(x):
    return 1.0 / (1.0 + jnp.exp(-x))


def _silu_and_grad(x):
    s = _sigmoid(x)
    return x * s, s * (1.0 + x * (1.0 - s))


def _gelu(x):
    return 0.5 * x * (1.0 + jnp.tanh(GELU_C0 * (x + GELU_C1 * x * x * x)))


def _gelu_and_grad(x):
    t = jnp.tanh(GELU_C0 * (x + GELU_C1 * x * x * x))
    g = 0.5 * x * (1.0 + t)
    dg = 0.5 * (1.0 + t) + 0.5 * x * (1.0 - t * t) * (GELU_C0 * (1.0 + 3.0 * GELU_C1 * x * x))
    return g, dg


def _dot(a, b, dims, precision=None):
    return lax.dot_general(a, b, (dims, ((), ())), precision=precision, preferred_element_type=F32)


NN = ((1,), (0,))
NT = ((1,), (1,))
TN = ((0,), (0,))


def _adamw(w, g, m, v):
    m = ADAM_B1 * m + (1.0 - ADAM_B1) * g
    v = ADAM_B2 * v + (1.0 - ADAM_B2) * (g * g)
    m_hat = m / (1.0 - ADAM_B1 ** ADAM_STEP)
    v_hat = v / (1.0 - ADAM_B2 ** ADAM_STEP)
    delta = -ADAM_LR * (m_hat / (jnp.sqrt(v_hat) + ADAM_EPS) + ADAM_WD * w)
    return delta, m, v


def _chunk_mask():
    r = lax.broadcasted_iota(jnp.int32, (SG_BLOCK, SG_BLOCK), 0)
    c = lax.broadcasted_iota(jnp.int32, (SG_BLOCK, SG_BLOCK), 1)
    return (c // CHUNK) <= (r // CHUNK)


def _place():
    return lax.axis_index("x"), lax.axis_index("y"), lax.axis_index("c")


def _other_chips(x, y):
    return [(1 - x, y), (x, 1 - y), (1 - x, 1 - y)]


def allgather_small(v, name):
    m_per, n = v.shape

    def body(x_ref, out_ref, send_sems, recv_sems, local_sem):
        x, y, c = _place()
        me, sibling = (x, y, c), (x, y, 1 - c)
        chips = _other_chips(x, y)

        def rows(px, py, pc):
            return out_ref.at[pl.ds((4 * px + 2 * py + pc) * m_per, m_per), :]

        def copy(k, block, to, src=None):
            return pltpu.make_async_remote_copy(
                src_ref=rows(*block) if src is None else src, dst_ref=rows(*block),
                send_sem=send_sems.at[k], recv_sem=recv_sems.at[k], device_id=to, device_id_type=MESH)

        mine = pltpu.make_async_copy(x_ref, rows(*me), local_sem)
        mine.start()
        first = [copy(0, me, sibling, src=x_ref)]
        first += [copy(1 + j, me, (*chip, c), src=x_ref) for j, chip in enumerate(chips)]
        for cp in first:
            cp.start()
        passed = [copy(4 + j, (*chip, c), sibling) for j, chip in enumerate(chips)]
        for j, chip in enumerate(chips):
            copy(1 + j, (*chip, c), me).wait_recv()
            passed[j].start()
        copy(0, sibling, me).wait_recv()
        for j, chip in enumerate(chips):
            copy(4 + j, (*chip, 1 - c), me).wait_recv()
        for cp in first + passed:
            cp.wait_send()
        mine.wait()

    return _call(
        body, name=name,
        out_shape=jax.ShapeDtypeStruct((N_DEV * m_per, n), v.dtype),
        in_specs=[pl.BlockSpec(memory_space=pltpu.VMEM)],
        out_specs=pl.BlockSpec(memory_space=pltpu.VMEM),
        scratch_shapes=[pltpu.SemaphoreType.DMA((7,)), pltpu.SemaphoreType.DMA((7,)), pltpu.SemaphoreType.DMA],
    )(v)


def _hbm_spec():
    return pl.BlockSpec(memory_space=pltpu.HBM)


def allgather_chips(shards, name):
    n = len(shards)

    def body(*refs):
        ins, outs = refs[:n], refs[n:2 * n]
        send_sems, recv_sems, local_sems = refs[2 * n:]
        x, y, c = _place()
        chips = _other_chips(x, y)
        started = []
        for w in range(n):
            loc = pltpu.make_async_copy(ins[w], outs[w].at[2 * x + y], local_sems.at[w])
            loc.start()
            started.append(loc)
        sends = []
        for w in range(n):
            for j, (px, py) in enumerate(chips):
                cp = pltpu.make_async_remote_copy(
                    src_ref=ins[w], dst_ref=outs[w].at[2 * x + y],
                    send_sem=send_sems.at[3 * w + j], recv_sem=recv_sems.at[3 * w + j],
                    device_id=(px, py, c), device_id_type=MESH)
                cp.start()
                sends.append(cp)
        for w in range(n):
            for j, (px, py) in enumerate(chips):
                pltpu.make_async_remote_copy(
                    src_ref=ins[w], dst_ref=outs[w].at[2 * px + py],
                    send_sem=send_sems.at[3 * w + j], recv_sem=recv_sems.at[3 * w + j],
                    device_id=(px, py, c), device_id_type=MESH).wait_recv()
        for cp in sends:
            cp.wait_send()
        for loc in started:
            loc.wait()

    return _call(
        body, name=name,
        out_shape=[jax.ShapeDtypeStruct((N_CHIPS,) + s.shape, s.dtype) for s in shards],
        in_specs=[_hbm_spec()] * n, out_specs=[_hbm_spec()] * n,
        scratch_shapes=[pltpu.SemaphoreType.DMA((3 * n,)), pltpu.SemaphoreType.DMA((3 * n,)),
                        pltpu.SemaphoreType.DMA((n,))],
    )(*shards)


def exchange_chips(parts, name):
    n = len(parts)

    def body(*refs):
        ins, outs = refs[:n], refs[n:2 * n]
        send_sems, recv_sems, local_sems = refs[2 * n:]
        x, y, c = _place()
        chips = _other_chips(x, y)
        started = []
        for w in range(n):
            loc = pltpu.make_async_copy(ins[w].at[2 * x + y], outs[w].at[3], local_sems.at[w])
            loc.start()
            started.append(loc)
        sends = []
        for w in range(n):
            for j, (px, py) in enumerate(chips):
                cp = pltpu.make_async_remote_copy(
                    src_ref=ins[w].at[2 * px + py], dst_ref=outs[w].at[j],
                    send_sem=send_sems.at[3 * w + j], recv_sem=recv_sems.at[3 * w + j],
                    device_id=(px, py, c), device_id_type=MESH)
                cp.start()
                sends.append(cp)
        for cp in sends:
            cp.wait_recv()
        for cp in sends:
            cp.wait_send()
        for loc in started:
            loc.wait()

    return _call(
        body, name=name,
        out_shape=[jax.ShapeDtypeStruct(p.shape, p.dtype) for p in parts],
        in_specs=[_hbm_spec()] * n, out_specs=[_hbm_spec()] * n,
        scratch_shapes=[pltpu.SemaphoreType.DMA((3 * n,)), pltpu.SemaphoreType.DMA((3 * n,)),
                        pltpu.SemaphoreType.DMA((n,))],
    )(*parts)


def swap_sibling(arrs, name):
    n = len(arrs)

    def body(*refs):
        ins, outs = refs[:n], refs[n:2 * n]
        send_sems, recv_sems = refs[2 * n:]
        x, y, c = _place()
        cps = []
        for w in range(n):
            cp = pltpu.make_async_remote_copy(
                src_ref=ins[w], dst_ref=outs[w], send_sem=send_sems.at[w], recv_sem=recv_sems.at[w],
                device_id=(x, y, 1 - c), device_id_type=MESH)
            cp.start()
            cps.append(cp)
        for cp in cps:
            cp.wait_recv()
        for cp in cps:
            cp.wait_send()

    return _call(
        body, name=name,
        out_shape=[jax.ShapeDtypeStruct(a.shape, a.dtype) for a in arrs],
        in_specs=[_hbm_spec()] * n, out_specs=[_hbm_spec()] * n,
        scratch_shapes=[pltpu.SemaphoreType.DMA((n,)), pltpu.SemaphoreType.DMA((n,))],
    )(*arrs)


def cast_bf16(w, name):
    r, c = w.shape
    tr = min(256, r)

    def body(w_ref, o_ref):
        o_ref[...] = w_ref[...].astype(BF16)

    return _call(
        body, name=name, grid=(r // tr,),
        out_shape=jax.ShapeDtypeStruct((r, c), BF16),
        in_specs=[pl.BlockSpec((tr, c), lambda i: (i, 0))],
        out_specs=pl.BlockSpec((tr, c), lambda i: (i, 0)),
        compiler_params=_params(),
    )(w)


def sum_slots(recv, name):
    _, r, c = recv.shape
    tr = min(256, r)

    def body(r_ref, o_ref):
        acc = r_ref[3].astype(F32) + r_ref[0].astype(F32)
        acc = acc + r_ref[1].astype(F32)
        o_ref[...] = acc + r_ref[2].astype(F32)

    return _call(
        body, name=name, grid=(r // tr,),
        out_shape=jax.ShapeDtypeStruct((r, c), F32),
        in_specs=[pl.BlockSpec((N_CHIPS, tr, c), lambda i: (0, i, 0))],
        out_specs=pl.BlockSpec((tr, c), lambda i: (i, 0)),
        compiler_params=_params(),
    )(recv)


def adamw_pair(pa, pb, w, m, v, name):
    r, c = w.shape
    tr = min(128, r)

    def body(pa_ref, pb_ref, w_ref, m_ref, v_ref, g_ref, d_ref, nm_ref, nv_ref):
        g = pa_ref[...] + pb_ref[...]
        d, nm, nv = _adamw(w_ref[...], g, m_ref[...], v_ref[...])
        g_ref[...] = g
        d_ref[...] = d
        nm_ref[...] = nm
        nv_ref[...] = nv

    spec = pl.BlockSpec((tr, c), lambda i: (i, 0))
    return _call(
        body, name=name, grid=(r // tr,),
        out_shape=[jax.ShapeDtypeStruct((r, c), F32)] * 4,
        in_specs=[spec] * 5, out_specs=[spec] * 4,
        compiler_params=_params(),
    )(pa, pb, w, m, v)


def small_update(gathered, w, m, v, name):
    def body(g_ref, w_ref, m_ref, v_ref, go_ref, d_ref, nm_ref, nv_ref):
        g = g_ref[0]
        for k in range(1, N_DEV):
            g = g + g_ref[k]
        d, nm, nv = _adamw(w_ref[...], g, m_ref[...], v_ref[...])
        go_ref[...] = g
        d_ref[...] = d
        nm_ref[...] = nm
        nv_ref[...] = nv

    return _call(
        body, name=name,
        out_shape=[jax.ShapeDtypeStruct(w.shape, F32)] * 4,
        compiler_params=_params(),
    )(gathered, w, m, v)


def ada_fwd(c_all, w_ada, b_cols, name):
    n_l, d, cols = w_ada.shape
    nb = c_all.shape[0]
    tn = 256

    def body(c_ref, w_ref, b_ref, o_ref):
        cv = c_ref[...]
        ca = (cv * _sigmoid(cv)).astype(BF16)
        o_ref[...] = _dot(ca, w_ref[...].astype(BF16), NN) + b_ref[...]

    return _call(
        body, name=name, grid=(n_l, cols // tn),
        out_shape=jax.ShapeDtypeStruct((n_l, nb, cols), F32),
        in_specs=[pl.BlockSpec((nb, d), lambda l, j: (0, 0)),
                  pl.BlockSpec((None, d, tn), lambda l, j: (l, 0, j)),
                  pl.BlockSpec((None, 1, tn), lambda l, j: (l, 0, j))],
        out_specs=pl.BlockSpec((None, nb, tn), lambda l, j: (l, 0, j)),
        compiler_params=_params(),
    )(c_all, w_ada, b_cols)


def ada_bwd(c_all, dmod_cols, w, m, v, name):
    n_l, d, cols = w.shape
    nb = c_all.shape[0]
    tn = 256

    def body(c_ref, dm_ref, w_ref, m_ref, v_ref, g_ref, d_ref, nm_ref, nv_ref):
        cv = c_ref[...]
        ca = (cv * _sigmoid(cv)).astype(BF16)
        g = _dot(ca, dm_ref[...].astype(BF16), TN)
        dl, nm, nv = _adamw(w_ref[...], g, m_ref[...], v_ref[...])
        g_ref[...] = g
        d_ref[...] = dl
        nm_ref[...] = nm
        nv_ref[...] = nv

    wspec = pl.BlockSpec((None, d, tn), lambda l, j: (l, 0, j))
    return _call(
        body, name=name, grid=(n_l, cols // tn),
        out_shape=[jax.ShapeDtypeStruct((n_l, d, cols), F32)] * 4,
        in_specs=[pl.BlockSpec((nb, d), lambda l, j: (0, 0)),
                  pl.BlockSpec((None, nb, tn), lambda l, j: (l, 0, j)),
                  wspec, wspec, wspec],
        out_specs=[wspec] * 4,
        compiler_params=_params(),
    )(c_all, dmod_cols, w, m, v)


def bias_update(dmod_all, w, m, v, name):
    def body(dm_ref, w_ref, m_ref, v_ref, g_ref, d_ref, nm_ref, nv_ref):
        g = jnp.sum(dm_ref[...], axis=0, keepdims=True)
        dl, nm, nv = _adamw(w_ref[...], g, m_ref[...], v_ref[...])
        g_ref[...] = g
        d_ref[...] = dl
        nm_ref[...] = nm
        nv_ref[...] = nv

    return _call(
        body, name=name,
        out_shape=[jax.ShapeDtypeStruct(w.shape, F32)] * 4,
        compiler_params=_params(),
    )(dmod_all, w, m, v)


def inproj_fwd(x, mod, ng, wg, seq, sectioned, name):
    m_rows, d = x.shape
    nsh, _, ns = wg.shape
    n = nsh * ns
    tm, tn = min(512, seq), 512
    per = ns // tn

    def body(x_ref, mod_ref, ng_ref, w_ref, proj_ref, h_ref):
        @pl.when(pl.program_id(1) == 0)
        def _():
            xv = x_ref[...]
            r = lax.rsqrt(jnp.mean(xv * xv, axis=-1, keepdims=True) + EPS)
            md = mod_ref[0]
            h = (xv * r * ng_ref[...]) * (1.0 + md[:, d:2 * d]) + md[:, :d]
            h_ref[...] = h.astype(BF16)
        proj_ref[...] = _dot(h_ref[...], w_ref[...], NN)

    if sectioned:
        proj_shape = (nsh, m_rows, ns)
        proj_spec = pl.BlockSpec((None, tm, tn), lambda i, j: (j // per, i, j % per))
    else:
        proj_shape = (m_rows, n)
        proj_spec = pl.BlockSpec((tm, tn), lambda i, j: (i, j))
    return _call(
        body, name=name, grid=(m_rows // tm, n // tn),
        out_shape=[jax.ShapeDtypeStruct(proj_shape, F32), jax.ShapeDtypeStruct((m_rows, d), BF16)],
        in_specs=[pl.BlockSpec((tm, d), lambda i, j: (i, 0)),
                  pl.BlockSpec((1, 1, 3 * d), lambda i, j: ((i * tm) // seq, 0, 0)),
                  pl.BlockSpec((1, d), lambda i, j: (0, 0)),
                  pl.BlockSpec((None, d, tn), lambda i, j: (j // per, 0, j % per))],
        out_specs=[proj_spec, pl.BlockSpec((tm, d), lambda i, j: (i, 0))],
        compiler_params=_params(),
    )(x, mod, ng, wg)


def outproj_fwd(y, w, x, mod, seq, name):
    m_rows, di = y.shape
    d = w.shape[1]
    tm = min(512, seq)

    def body(y_ref, w_ref, x_ref, mod_ref, xn_ref, out_ref):
        acc = _dot(y_ref[...], w_ref[...], NN)
        out_ref[...] = acc
        xn_ref[...] = x_ref[...] + mod_ref[0][:, 2 * d:] * acc

    row = pl.BlockSpec((tm, d), lambda i: (i, 0))
    return _call(
        body, name=name, grid=(m_rows // tm,),
        out_shape=[jax.ShapeDtypeStruct((m_rows, d), F32)] * 2,
        in_specs=[pl.BlockSpec((tm, di), lambda i: (i, 0)),
                  pl.BlockSpec((di, d), lambda i: (0, 0)),
                  row,
                  pl.BlockSpec((1, 1, 3 * d), lambda i: ((i * tm) // seq, 0, 0))],
        out_specs=[row, row],
        compiler_params=_params(),
    )(y, w, x, mod)


def outproj_bwd(dxo, out, mod, w, seq, name):
    m_rows, d = dxo.shape
    di = w.shape[0]
    nb = m_rows // seq
    tm, tn = min(512, seq), 512

    def body(dxo_ref, out_ref, mod_ref, w_ref, dy_ref, dout_ref, dgate_ref):
        i = pl.program_id(0)

        @pl.when(pl.program_id(1) == 0)
        def _():
            dx = dxo_ref[...]
            dout_ref[...] = (mod_ref[0][:, 2 * d:] * dx).astype(BF16)
            part = jnp.sum(dx * out_ref[...], axis=0, keepdims=True)

            @pl.when((i * tm) % seq == 0)
            def _():
                dgate_ref[0] = part

            @pl.when((i * tm) % seq != 0)
            def _():
                dgate_ref[0] = dgate_ref[0] + part

        dy_ref[...] = _dot(dout_ref[...], w_ref[...], NT)

    row = pl.BlockSpec((tm, d), lambda i, j: (i, 0))
    return _call(
        body, name=name, grid=(m_rows // tm, di // tn),
        out_shape=[jax.ShapeDtypeStruct((m_rows, di), F32), jax.ShapeDtypeStruct((m_rows, d), BF16),
                   jax.ShapeDtypeStruct((nb, 1, d), F32)],
        in_specs=[row, row,
                  pl.BlockSpec((1, 1, 3 * d), lambda i, j: ((i * tm) // seq, 0, 0)),
                  pl.BlockSpec((tn, d), lambda i, j: (j, 0))],
        out_specs=[pl.BlockSpec((tm, tn), lambda i, j: (i, j)), row,
                   pl.BlockSpec((1, 1, d), lambda i, j: ((i * tm) // seq, 0, 0))],
        compiler_params=_params(),
    )(dxo, out, mod, w)


def grad_w_out(y, dout, name):
    m_rows, di = y.shape
    d = dout.shape[1]
    tm, tk = min(512, m_rows), 512
    n_m = m_rows // tm

    def body(y_ref, do_ref, o_ref, acc_ref):
        mi = pl.program_id(1)
        part = _dot(y_ref[...], do_ref[...], TN)

        @pl.when(mi == 0)
        def _():
            acc_ref[...] = part

        @pl.when(mi != 0)
        def _():
            acc_ref[...] = acc_ref[...] + part

        @pl.when(mi == n_m - 1)
        def _():
            o_ref[...] = acc_ref[...].astype(BF16)

    return _call(
        body, name=name, grid=(di // tk, n_m),
        out_shape=jax.ShapeDtypeStruct((di, d), BF16),
        in_specs=[pl.BlockSpec((tm, tk), lambda j, mi: (mi, j)),
                  pl.BlockSpec((tm, d), lambda j, mi: (mi, 0))],
        out_specs=pl.BlockSpec((tk, d), lambda j, mi: (j, 0)),
        scratch_shapes=[pltpu.VMEM((tk, d), F32)],
        compiler_params=_params(),
    )(y, dout)


def grad_w_in(h, dproj, nsh, sectioned, name):
    m_rows, d = h.shape
    n = dproj.shape[0] * dproj.shape[2] if sectioned else dproj.shape[1]
    ns = n // nsh
    tm, tn = min(512, m_rows), 512
    per = ns // tn
    n_m = m_rows // tm

    def body(h_ref, dp_ref, o_ref, acc_ref):
        mi = pl.program_id(1)
        part = _dot(h_ref[...], dp_ref[...], TN)

        @pl.when(mi == 0)
        def _():
            acc_ref[...] = part

        @pl.when(mi != 0)
        def _():
            acc_ref[...] = acc_ref[...] + part

        @pl.when(mi == n_m - 1)
        def _():
            o_ref[...] = acc_ref[...].astype(BF16)

    if sectioned:
        dp_spec = pl.BlockSpec((None, tm, tn), lambda j, mi: (j // per, mi, j % per))
    else:
        dp_spec = pl.BlockSpec((tm, tn), lambda j, mi: (mi, j))
    return _call(
        body, name=name, grid=(n // tn, n_m),
        out_shape=jax.ShapeDtypeStruct((nsh, d, ns), BF16),
        in_specs=[pl.BlockSpec((tm, d), lambda j, mi: (mi, 0)), dp_spec],
        out_specs=pl.BlockSpec((None, d, tn), lambda j, mi: (j // per, 0, j % per)),
        scratch_shapes=[pltpu.VMEM((d, tn), F32)],
        compiler_params=_params(),
    )(h, dproj)


def inproj_bwd(dproj, wg, x, dxo, mod, ng, seq, sectioned, name):
    m_rows, d = x.shape
    nsh, _, ns = wg.shape
    n = nsh * ns
    nb = m_rows // seq
    tm, tk = min(512, seq), 512
    per = ns // tk
    n_k = n // tk

    def body(dp_ref, w_ref, x_ref, dxo_ref, mod_ref, ng_ref, dxi_ref, dsh_ref, dsc_ref, dng_ref, acc_ref):
        i, k = pl.program_id(0), pl.program_id(1)
        part = _dot(dp_ref[...], w_ref[...], NT)

        @pl.when(k == 0)
        def _():
            acc_ref[...] = part

        @pl.when(k != 0)
        def _():
            acc_ref[...] = acc_ref[...] + part

        @pl.when(k == n_k - 1)
        def _():
            dh = acc_ref[...]
            xv = x_ref[...]
            r = lax.rsqrt(jnp.mean(xv * xv, axis=-1, keepdims=True) + EPS)
            xn = xv * r
            md = mod_ref[0]
            gain = ng_ref[...]
            p_shift = jnp.sum(dh, axis=0, keepdims=True)
            p_scale = jnp.sum(dh * (xn * gain), axis=0, keepdims=True)
            drn = dh * (1.0 + md[:, d:2 * d])
            p_ng = jnp.sum(drn * xn, axis=0, keepdims=True)
            dxn = drn * gain
            dx = r * (dxn - xn * jnp.mean(dxn * xn, axis=-1, keepdims=True))
            dxi_ref[...] = dxo_ref[...] + dx

            @pl.when((i * tm) % seq == 0)
            def _():
                dsh_ref[0] = p_shift
                dsc_ref[0] = p_scale

            @pl.when((i * tm) % seq != 0)
            def _():
                dsh_ref[0] = dsh_ref[0] + p_shift
                dsc_ref[0] = dsc_ref[0] + p_scale

            @pl.when(i == 0)
            def _():
                dng_ref[...] = p_ng

            @pl.when(i != 0)
            def _():
                dng_ref[...] = dng_ref[...] + p_ng

    if sectioned:
        dp_spec = pl.BlockSpec((None, tm, tk), lambda i, k: (k // per, i, k % per))
    else:
        dp_spec = pl.BlockSpec((tm, tk), lambda i, k: (i, k))
    row = pl.BlockSpec((tm, d), lambda i, k: (i, 0))
    per_seq = pl.BlockSpec((1, 1, d), lambda i, k: ((i * tm) // seq, 0, 0))
    return _call(
        body, name=name, grid=(m_rows // tm, n_k),
        out_shape=[jax.ShapeDtypeStruct((m_rows, d), F32), jax.ShapeDtypeStruct((nb, 1, d), F32),
                   jax.ShapeDtypeStruct((nb, 1, d), F32), jax.ShapeDtypeStruct((1, d), F32)],
        in_specs=[dp_spec,
                  pl.BlockSpec((None, d, tk), lambda i, k: (k // per, 0, k % per)),
                  row, row,
                  pl.BlockSpec((1, 1, 3 * d), lambda i, k: ((i * tm) // seq, 0, 0)),
                  pl.BlockSpec((1, d), lambda i, k: (0, 0))],
        out_specs=[row, per_seq, per_seq, pl.BlockSpec((1, d), lambda i, k: (0, 0))],
        scratch_shapes=[pltpu.VMEM((tm, d), F32)],
        compiler_params=_params(),
    )(dproj, wg, x, dxo, mod, ng)


def _sgu_stats(proj_ref, vg_ref, di, gd):
    s1 = jnp.zeros((SG_BLOCK, 1), F32)
    for g in range(SG_GROUPS):
        vg = _gelu(proj_ref[:, di + g * gd:di + (g + 1) * gd])
        vg_ref[:, g * gd:(g + 1) * gd] = vg
        s1 = s1 + jnp.sum(vg, axis=1, keepdims=True)
    mu = s1 / di
    s2 = jnp.zeros((SG_BLOCK, 1), F32)
    for g in range(SG_GROUPS):
        dv = vg_ref[:, g * gd:(g + 1) * gd] - mu
        s2 = s2 + jnp.sum(dv * dv, axis=1, keepdims=True)
    return mu, lax.rsqrt(s2 / di + EPS)


def sgu_fwd(proj, ln_gain, ln_bias, ws, bs, name):
    m_rows, n3 = proj.shape
    di = n3 // 3
    gd = di // SG_GROUPS

    def body(proj_ref, lg_ref, lb_ref, ws_ref, bs_ref, y_ref, wsm_ref, vg_ref):
        @pl.when(pl.program_id(0) == 0)
        def _():
            mask = _chunk_mask()
            for g in range(SG_GROUPS):
                wsm_ref[g] = jnp.where(mask, ws_ref[g], 0.0).astype(BF16)

        mu, rstd = _sgu_stats(proj_ref, vg_ref, di, gd)
        for g in range(SG_GROUPS):
            cs = slice(g * gd, (g + 1) * gd)
            vln = (vg_ref[:, cs] - mu) * rstd * lg_ref[:, cs] + lb_ref[:, cs]
            s = _dot(wsm_ref[g], vln.astype(BF16), NN) + bs_ref[g]
            u = _gelu(proj_ref[:, cs])
            gp = proj_ref[:, 2 * di + g * gd:2 * di + (g + 1) * gd]
            y_ref[:, cs] = (u * s * (gp * _sigmoid(gp))).astype(BF16)

    full = lambda shape: pl.BlockSpec(shape, lambda i: (0,) * len(shape))
    return _call(
        body, name=name, grid=(m_rows // SG_BLOCK,),
        out_shape=jax.ShapeDtypeStruct((m_rows, di), BF16),
        in_specs=[pl.BlockSpec((SG_BLOCK, n3), lambda i: (i, 0)),
                  full((1, di)), full((1, di)),
                  full((SG_GROUPS, SG_BLOCK, SG_BLOCK)), full((SG_GROUPS, SG_BLOCK, 1))],
        out_specs=pl.BlockSpec((SG_BLOCK, di), lambda i: (i, 0)),
        scratch_shapes=[pltpu.VMEM((SG_GROUPS, SG_BLOCK, SG_BLOCK), BF16), pltpu.VMEM((SG_BLOCK, di), F32)],
        compiler_params=_params(),
    )(proj, ln_gain, ln_bias, ws, bs)


def sgu_bwd(proj, dy, ln_gain, ln_bias, ws, bs, name):
    m_rows, n3 = proj.shape
    di = n3 // 3
    gd = di // SG_GROUPS
    n_i = m_rows // SG_BLOCK

    def body(proj_ref, dy_ref, lg_ref, lb_ref, ws_ref, bs_ref,
             dp_ref, dws_ref, dbs_ref, dlg_ref, dlb_ref, wsm_ref, vg_ref, dvh_ref):
        i = pl.program_id(0)

        @pl.when(i == 0)
        def _():
            mask = _chunk_mask()
            for g in range(SG_GROUPS):
                wsm_ref[g] = jnp.where(mask, ws_ref[g], 0.0).astype(BF16)
            dws_ref[...] = jnp.zeros_like(dws_ref)
            dbs_ref[...] = jnp.zeros_like(dbs_ref)
            dlg_ref[...] = jnp.zeros_like(dlg_ref)
            dlb_ref[...] = jnp.zeros_like(dlb_ref)

        mu, rstd = _sgu_stats(proj_ref, vg_ref, di, gd)
        m1 = jnp.zeros((SG_BLOCK, 1), F32)
        m2 = jnp.zeros((SG_BLOCK, 1), F32)
        for g in range(SG_GROUPS):
            cs = slice(g * gd, (g + 1) * gd)
            gs = slice(2 * di + g * gd, 2 * di + (g + 1) * gd)
            gain = lg_ref[:, cs]
            vhat = (vg_ref[:, cs] - mu) * rstd
            vln_b = (vhat * gain + lb_ref[:, cs]).astype(BF16)
            s = _dot(wsm_ref[g], vln_b, NN) + bs_ref[g]
            u, du = _gelu_and_grad(proj_ref[:, cs])
            sg, dsg = _silu_and_grad(proj_ref[:, gs])
            dyv = dy_ref[:, cs]
            dp_ref[:, cs] = (dyv * s * sg * du).astype(BF16)
            dp_ref[:, gs] = (dyv * u * s * dsg).astype(BF16)
            ds = dyv * u * sg
            ds_b = ds.astype(BF16)
            dws_ref[g] = dws_ref[g] + _dot(ds_b, vln_b, NT)
            dbs_ref[g] = dbs_ref[g] + jnp.sum(ds, axis=1, keepdims=True)
            dvln = _dot(wsm_ref[g], ds_b, TN)
            dlg_ref[:, cs] = dlg_ref[:, cs] + jnp.sum(dvln * vhat, axis=0, keepdims=True)
            dlb_ref[:, cs] = dlb_ref[:, cs] + jnp.sum(dvln, axis=0, keepdims=True)
            dvh = dvln * gain
            dvh_ref[:, cs] = dvh
            m1 = m1 + jnp.sum(dvh, axis=1, keepdims=True)
            m2 = m2 + jnp.sum(dvh * vhat, axis=1, keepdims=True)
        m1 = m1 / di
        m2 = m2 / di
        for g in range(SG_GROUPS):
            cs = slice(g * gd, (g + 1) * gd)
            vs = slice(di + g * gd, di + (g + 1) * gd)
            vhat = (vg_ref[:, cs] - mu) * rstd
            dvg = rstd * (dvh_ref[:, cs] - m1 - vhat * m2)
            _, dgel = _gelu_and_grad(proj_ref[:, vs])
            dp_ref[:, vs] = (dvg * dgel).astype(BF16)

        @pl.when(i == n_i - 1)
        def _():
            mask = _chunk_mask()
            for g in range(SG_GROUPS):
                dws_ref[g] = jnp.where(mask, dws_ref[g], 0.0)

    full = lambda shape: pl.BlockSpec(shape, lambda i: (0,) * len(shape))
    return _call(
        body, name=name, grid=(n_i,),
        out_shape=[jax.ShapeDtypeStruct((m_rows, n3), BF16),
                   jax.ShapeDtypeStruct((SG_GROUPS, SG_BLOCK, SG_BLOCK), F32),
                   jax.ShapeDtypeStruct((SG_GROUPS, SG_BLOCK, 1), F32),
                   jax.ShapeDtypeStruct((1, di), F32), jax.ShapeDtypeStruct((1, di), F32)],
        in_specs=[pl.BlockSpec((SG_BLOCK, n3), lambda i: (i, 0)),
                  pl.BlockSpec((SG_BLOCK, di), lambda i: (i, 0)),
                  full((1, di)), full((1, di)),
                  full((SG_GROUPS, SG_BLOCK, SG_BLOCK)), full((SG_GROUPS, SG_BLOCK, 1))],
        out_specs=[pl.BlockSpec((SG_BLOCK, n3), lambda i: (i, 0)),
                   full((SG_GROUPS, SG_BLOCK, SG_BLOCK)), full((SG_GROUPS, SG_BLOCK, 1)),
                   full((1, di)), full((1, di))],
        scratch_shapes=[pltpu.VMEM((SG_GROUPS, SG_BLOCK, SG_BLOCK), BF16),
                        pltpu.VMEM((SG_BLOCK, di), F32), pltpu.VMEM((SG_BLOCK, di), F32)],
        compiler_params=_params(),
    )(proj, dy, ln_gain, ln_bias, ws, bs)


def _lower_bound(lbraw):
    mx = jnp.maximum(lbraw[0:1, :], lbraw[1:2, :])
    e0 = jnp.exp(lbraw[0:1, :] - mx)
    e1 = jnp.exp(lbraw[1:2, :] - mx)
    p0 = e0 / (e0 + e1)
    p1 = e1 / (e0 + e1)
    return (p0 + p1) - p0, p0, p1


def _tri(lower):
    r = lax.broadcasted_iota(jnp.int32, (CHUNK, CHUNK), 0)
    c = lax.broadcasted_iota(jnp.int32, (CHUNK, CHUNK), 1)
    return ((r >= c) if lower else (c >= r)).astype(F32)


def _row(a, idx):
    r = lax.broadcasted_iota(jnp.int32, a.shape, 0)
    return jnp.sum(jnp.where(r == idx, a, 0.0), axis=0, keepdims=True)


def _hgrn_gates(qp, fp, lb, tri):
    sgm = _sigmoid(fp)
    f = lb + (1.0 - lb) * sgm
    k = 1.0 - f
    a = _dot(tri, jnp.log(f), NN, precision=lax.Precision.HIGHEST)
    a_mid = _row(a, CHUNK // 2 - 1)
    a_last = _row(a, CHUNK - 1)
    q, dq = _silu_and_grad(qp)
    e1, e2, e3, e4 = jnp.exp(a - a_mid), jnp.exp(a_mid - a), jnp.exp(a), jnp.exp(a_last - a)
    return dict(sgm=sgm, f=f, k=k, q=q, dq=dq, e1=e1, e2=e2, e3=e3, e4=e4, dec=jnp.exp(a_last),
                q_in=q * e1, k_in=k * e2, q_out=q * e3, k_out=k * e4)


def _causal():
    r = lax.broadcasted_iota(jnp.int32, (CHUNK, CHUNK), 0)
    c = lax.broadcasted_iota(jnp.int32, (CHUNK, CHUNK), 1)
    return r >= c


def hgrn_fwd(proj4, lbraw, gn, seq, name):
    _, m_rows, di = proj4.shape
    nb, nh, nc = m_rows // seq, di // HEAD_DIM, seq // CHUNK

    def body(p_ref, lb_ref, gn_ref, y_ref):
        lb, _, _ = _lower_bound(lb_ref[...])
        tri = _tri(True)
        causal = _causal()
        gain = gn_ref[...]

        def chunk(n, st):
            rs = pl.ds(pl.multiple_of(n * CHUNK, CHUNK), CHUNK)
            t = _hgrn_gates(p_ref[0, rs, :], p_ref[1, rs, :], lb, tri)
            v_b = p_ref[2, rs, :].astype(BF16)
            sc = jnp.where(causal, _dot(t["q_in"].astype(BF16), t["k_in"].astype(BF16), NT), 0.0)
            o = _dot(sc.astype(BF16), v_b, NN) + _dot(t["q_out"].astype(BF16), st.astype(BF16), NT)
            r = lax.rsqrt(jnp.mean(o * o, axis=-1, keepdims=True) + EPS)
            gp = p_ref[3, rs, :]
            y_ref[rs, :] = ((o * r * gain) * (gp * _sigmoid(gp))).astype(BF16)
            return st * t["dec"] + _dot(v_b, t["k_out"].astype(BF16), TN)

        lax.fori_loop(0, nc, chunk, jnp.zeros((HEAD_DIM, HEAD_DIM), F32))

    return _call(
        body, name=name, grid=(nb, nh),
        out_shape=jax.ShapeDtypeStruct((m_rows, di), BF16),
        in_specs=[pl.BlockSpec((4, seq, HEAD_DIM), lambda b, h: (0, b, h)),
                  pl.BlockSpec((2, HEAD_DIM), lambda b, h: (0, h)),
                  pl.BlockSpec((1, HEAD_DIM), lambda b, h: (0, 0))],
        out_specs=pl.BlockSpec((seq, HEAD_DIM), lambda b, h: (b, h)),
        compiler_params=_params(),
    )(proj4, lbraw, gn)


def hgrn_bwd(proj4, dy, lbraw, gn, seq, name):
    _, m_rows, di = proj4.shape
    nb, nh, nc = m_rows // seq, di // HEAD_DIM, seq // CHUNK

    def body(p_ref, dy_ref, lb_ref, gn_ref, dp_ref, dlb_ref, dgn_ref, st_ref, lbacc_ref):
        h, b = pl.program_id(0), pl.program_id(1)
        lb, p0, p1 = _lower_bound(lb_ref[...])
        tri, triu = _tri(True), _tri(False)
        causal = _causal()
        gain = gn_ref[...]

        @pl.when((h == 0) & (b == 0))
        def _():
            dgn_ref[...] = jnp.zeros_like(dgn_ref)

        @pl.when(b == 0)
        def _():
            lbacc_ref[...] = jnp.zeros_like(lbacc_ref)

        def sweep(n, st):
            rs = pl.ds(pl.multiple_of(n * CHUNK, CHUNK), CHUNK)
            st_ref[n] = st
            t = _hgrn_gates(p_ref[0, rs, :], p_ref[1, rs, :], lb, tri)
            return st * t["dec"] + _dot(p_ref[2, rs, :].astype(BF16), t["k_out"].astype(BF16), TN)

        lax.fori_loop(0, nc, sweep, jnp.zeros((HEAD_DIM, HEAD_DIM), F32))

        def back(j, dst):
            n = nc - 1 - j
            rs = pl.ds(pl.multiple_of(n * CHUNK, CHUNK), CHUNK)
            t = _hgrn_gates(p_ref[0, rs, :], p_ref[1, rs, :], lb, tri)
            st = st_ref[n]
            st_b, dst_b = st.astype(BF16), dst.astype(BF16)
            v_b = p_ref[2, rs, :].astype(BF16)
            q_in_b, k_in_b = t["q_in"].astype(BF16), t["k_in"].astype(BF16)
            q_out_b, k_out_b = t["q_out"].astype(BF16), t["k_out"].astype(BF16)
            sc_b = jnp.where(causal, _dot(q_in_b, k_in_b, NT), 0.0).astype(BF16)
            o = _dot(sc_b, v_b, NN) + _dot(q_out_b, st_b, NT)
            r = lax.rsqrt(jnp.mean(o * o, axis=-1, keepdims=True) + EPS)
            ohat = o * r
            sg, dsg = _silu_and_grad(p_ref[3, rs, :])
            dyv = dy_ref[rs, :]
            dp_ref[3, rs, :] = (dyv * (ohat * gain) * dsg).astype(BF16)
            d_on = dyv * sg
            dgn_ref[...] = dgn_ref[...] + jnp.sum(d_on * ohat, axis=0, keepdims=True)
            dohat = d_on * gain
            do = r * (dohat - ohat * jnp.mean(dohat * ohat, axis=-1, keepdims=True))
            do_b = do.astype(BF16)
            dq_out = _dot(do_b, st_b, NN)
            dk_out = _dot(v_b, dst_b, NN)
            dv = _dot(k_out_b, dst_b, NT) + _dot(sc_b, do_b, TN)
            dp_ref[2, rs, :] = dv.astype(BF16)
            dsc_b = jnp.where(causal, _dot(do_b, v_b, NT), 0.0).astype(BF16)
            dq_in = _dot(dsc_b, k_in_b, NN)
            dk_in = _dot(dsc_b, q_in_b, TN)
            ddec = jnp.sum(dst * st, axis=0, keepdims=True)
            dq = dq_in * t["e1"] + dq_out * t["e3"]
            dk = dk_in * t["e2"] + dk_out * t["e4"]
            w_in = dq_in * t["q_in"] - dk_in * t["k_in"]
            w_out = dk_out * t["k_out"]
            da = w_in + dq_out * t["q_out"] - w_out
            da_mid = -jnp.sum(w_in, axis=0, keepdims=True)
            da_last = jnp.sum(w_out, axis=0, keepdims=True) + ddec * t["dec"]
            rid = lax.broadcasted_iota(jnp.int32, da.shape, 0)
            da = da + jnp.where(rid == CHUNK // 2 - 1, da_mid, 0.0) + jnp.where(rid == CHUNK - 1, da_last, 0.0)
            dlf = _dot(triu, da, NN, precision=lax.Precision.HIGHEST)
            df = dlf / t["f"] - dk
            sgm = t["sgm"]
            dp_ref[1, rs, :] = (df * (1.0 - lb) * sgm * (1.0 - sgm)).astype(BF16)
            lbacc_ref[...] = lbacc_ref[...] + jnp.sum(df * (1.0 - sgm), axis=0, keepdims=True)
            dp_ref[0, rs, :] = (dq * t["dq"]).astype(BF16)
            return dst * t["dec"] + _dot(do_b, q_out_b, TN)

        lax.fori_loop(0, nc, back, jnp.zeros((HEAD_DIM, HEAD_DIM), F32))

        @pl.when(b == nb - 1)
        def _():
            acc = lbacc_ref[...]
            dlb_ref[0:1, :] = -acc * p0 * p1
            dlb_ref[1:2, :] = acc * p1 * (1.0 - p1)

    return _call(
        body, name=name, grid=(nh, nb),
        out_shape=[jax.ShapeDtypeStruct((4, m_rows, di), BF16), jax.ShapeDtypeStruct((2, di), F32),
                   jax.ShapeDtypeStruct((1, HEAD_DIM), F32)],
        in_specs=[pl.BlockSpec((4, seq, HEAD_DIM), lambda h, b: (0, b, h)),
                  pl.BlockSpec((seq, HEAD_DIM), lambda h, b: (b, h)),
                  pl.BlockSpec((2, HEAD_DIM), lambda h, b: (0, h)),
                  pl.BlockSpec((1, HEAD_DIM), lambda h, b: (0, 0))],
        out_specs=[pl.BlockSpec((4, seq, HEAD_DIM), lambda h, b: (0, b, h)),
                   pl.BlockSpec((2, HEAD_DIM), lambda h, b: (0, h)),
                   pl.BlockSpec((1, HEAD_DIM), lambda h, b: (0, 0))],
        scratch_shapes=[pltpu.VMEM((nc, HEAD_DIM, HEAD_DIM), F32), pltpu.VMEM((1, HEAD_DIM), F32)],
        compiler_params=_params(),
    )(proj4, dy, lbraw, gn)


def final_loss(x, fg, target, name):
    m_rows, d = x.shape
    tm = min(512, m_rows)

    def body(x_ref, fg_ref, t_ref, loss_ref, dx_ref, dfg_ref):
        i = pl.program_id(0)
        xv = x_ref[...]
        gain = fg_ref[...]
        r = lax.rsqrt(jnp.mean(xv * xv, axis=-1, keepdims=True) + EPS)
        xn = xv * r
        e = xn * gain - t_ref[...]
        part = 0.5 * jnp.sum(jnp.mean(e * e, axis=-1, keepdims=True), axis=0, keepdims=True)
        dyv = e / d
        p_fg = jnp.sum(dyv * xn, axis=0, keepdims=True)
        dxn = dyv * gain
        dx_ref[...] = r * (dxn - xn * jnp.mean(dxn * xn, axis=-1, keepdims=True))

        @pl.when(i == 0)
        def _():
            loss_ref[...] = part
            dfg_ref[...] = p_fg

        @pl.when(i != 0)
        def _():
            loss_ref[...] = loss_ref[...] + part
            dfg_ref[...] = dfg_ref[...] + p_fg

    row = pl.BlockSpec((tm, d), lambda i: (i, 0))
    return _call(
        body, name=name, grid=(m_rows // tm,),
        out_shape=[jax.ShapeDtypeStruct((1, 1), F32), jax.ShapeDtypeStruct((m_rows, d), F32),
                   jax.ShapeDtypeStruct((1, d), F32)],
        in_specs=[row, pl.BlockSpec((1, d), lambda i: (0, 0)), row],
        out_specs=[pl.BlockSpec((1, 1), lambda i: (0, 0)), row, pl.BlockSpec((1, d), lambda i: (0, 0))],
        compiler_params=_params(),
    )(x, fg, target)


def _pack(parts):
    flat = jnp.concatenate([p.reshape(-1) for p in parts])
    pad = (-flat.shape[0]) % (8 * LANES)
    return jnp.pad(flat, (0, pad)).reshape(-1, LANES)


def _unpack(packed, like):
    flat = packed.reshape(-1)
    out, off = [], 0
    for a in like:
        out.append(flat[off:off + a.size].reshape(a.shape))
        off += a.size
    return out


def kernel(x, c, norm_gain, w_ada, b_ada, a_w_in, a_ln_gain, a_ln_bias, a_w_s, a_b_s, a_w_out, b_w_in, b_lower_bounds, b_gn_gain, b_w_out, final_gain, loss_target, m_norm_gain, m_w_ada, m_b_ada, m_a_w_in, m_a_ln_gain, m_a_ln_bias, m_a_w_s, m_a_b_s, m_a_w_out, m_b_w_in, m_b_lower_bounds, m_b_gn_gain, m_b_w_out, m_final_gain, v_norm_gain, v_w_ada, v_b_ada, v_a_w_in, v_a_ln_gain, v_a_ln_bias, v_a_w_s, v_a_b_s, v_a_w_out, v_b_w_in, v_b_lower_bounds, v_b_gn_gain, v_b_w_out, v_final_gain):
    nb, seq, d = x.shape
    m_rows = nb * seq
    n_l = w_ada.shape[0]
    ada_cols = w_ada.shape[2]
    px, py, pc = _place()
    chip = 2 * px + py
    dev = 2 * chip + pc

    c_all = allgather_small(c.reshape(-1, LANES), "gather_c").reshape(N_DEV * nb, d)
    b_cols = lax.dynamic_slice_in_dim(b_ada, chip * ada_cols, ada_cols, axis=1).reshape(n_l, 1, ada_cols)
    mod_cols = ada_fwd(c_all, w_ada, b_cols, "ada_fwd")
    mod_g = allgather_small(mod_cols.reshape(-1, LANES), "gather_mod")
    mod_g = mod_g.reshape(N_CHIPS, 2, n_l, N_DEV * nb, ada_cols)[:, 0]
    mod_all = jnp.transpose(mod_g, (1, 2, 0, 3)).reshape(n_l, N_DEV * nb, 3 * d)
    mod_mine = lax.dynamic_slice_in_dim(mod_all, dev * nb, nb, axis=1)
    mod0 = mod_mine[0].reshape(nb, 1, 3 * d)
    mod1 = mod_mine[1].reshape(nb, 1, 3 * d)

    shards = [cast_bf16(a_w_in[0], "cast_a_in"), cast_bf16(a_w_out[0], "cast_a_out"),
              cast_bf16(b_w_in[0], "cast_b_in"), cast_bf16(b_w_out[0], "cast_b_out")]
    wa_in, wa_out, wb_in, wb_out = allgather_chips(shards, "gather_weights")
    di = wa_out.shape[1] * N_CHIPS
    wa_out = wa_out.reshape(di, d)
    wb_out = wb_out.reshape(di, d)

    x0 = x.reshape(m_rows, d)
    tgt = loss_target.reshape(m_rows, d)
    ng0, ng1 = norm_gain[0:1], norm_gain[1:2]
    bs_col = a_b_s[0].reshape(SG_GROUPS, SG_BLOCK, 1)
    proj_a, h_a = inproj_fwd(x0, mod0, ng0, wa_in, seq, False, "a_inproj")
    y_a = sgu_fwd(proj_a, a_ln_gain, a_ln_bias, a_w_s[0], bs_col, "a_sgu")
    x1, out_a = outproj_fwd(y_a, wa_out, x0, mod0, seq, "a_outproj")
    proj_b, h_b = inproj_fwd(x1, mod1, ng1, wb_in, seq, True, "b_inproj")
    y_b = hgrn_fwd(proj_b, b_lower_bounds, b_gn_gain, seq, "b_hgrn")
    x2, out_b = outproj_fwd(y_b, wb_out, x1, mod1, seq, "b_outproj")
    loss_part, dx2, dfg = final_loss(x2, final_gain.reshape(1, d), tgt, "loss_head")
    loss = lax.psum(loss_part[0, 0], ("x", "y", "c"))

    dy_b, dout_b, dgate1 = outproj_bwd(dx2, out_b, mod1, wb_out, seq, "b_outproj_bwd")
    gwb_out = grad_w_out(y_b, dout_b, "b_grad_w_out")
    dproj_b, dlb, dgn = hgrn_bwd(proj_b, dy_b, b_lower_bounds, b_gn_gain, seq, "b_hgrn_bwd")
    gwb_in = grad_w_in(h_b, dproj_b, N_CHIPS, True, "b_grad_w_in")
    dx1, dshift1, dscale1, dng1 = inproj_bwd(dproj_b, wb_in, x1, dx2, mod1, ng1, seq, True, "b_inproj_bwd")

    dy_a, dout_a, dgate0 = outproj_bwd(dx1, out_a, mod0, wa_out, seq, "a_outproj_bwd")
    gwa_out = grad_w_out(y_a, dout_a, "a_grad_w_out")
    dproj_a, dws, dbs, dlg, dlbias = sgu_bwd(proj_a, dy_a, a_ln_gain, a_ln_bias, a_w_s[0], bs_col, "a_sgu_bwd")
    gwa_in = grad_w_in(h_a, dproj_a, N_CHIPS, False, "a_grad_w_in")
    dx0, dshift0, dscale0, dng0 = inproj_bwd(dproj_a, wa_in, x0, dx1, mod0, ng0, seq, False, "a_inproj_bwd")
    grad_x = dx0.reshape(nb, seq, d)

    parts = [gwa_in, gwa_out.reshape(N_CHIPS, di // N_CHIPS, d), gwb_in, gwb_out.reshape(N_CHIPS, di // N_CHIPS, d)]
    recv = exchange_chips(parts, "exchange_grads")
    sums = [sum_slots(r, "sum_" + nm) for r, nm in zip(recv, ("a_in", "a_out", "b_in", "b_out"))]
    theirs = swap_sibling(sums, "swap_sums")
    big = []
    for mine, other, w, m, v, nm in zip(
            sums, theirs, (a_w_in, a_w_out, b_w_in, b_w_out), (m_a_w_in, m_a_w_out, m_b_w_in, m_b_w_out),
            (v_a_w_in, v_a_w_out, v_b_w_in, v_b_w_out), ("a_in", "a_out", "b_in", "b_out")):
        res = adamw_pair(mine, other, w[0], m[0], v[0], "adamw_" + nm)
        big.append([r.reshape(w.shape) for r in res])
    (ga_in, da_in, ma_in, va_in), (ga_out, da_out, ma_out, va_out), \
        (gb_in, db_in, mb_in, vb_in), (gb_out, db_out, mb_out, vb_out) = big

    dmod = jnp.concatenate([dshift0, dscale0, dgate0, dshift1, dscale1, dgate1], axis=2)
    dmod_all = allgather_small(dmod.reshape(-1, LANES), "gather_dmod").reshape(N_DEV * nb, n_l, 3 * d)
    dmod_cols = lax.dynamic_slice_in_dim(dmod_all, chip * ada_cols, ada_cols, axis=2)
    dmod_cols = jnp.transpose(dmod_cols, (1, 0, 2))
    g_wada, d_wada, m_wada, v_wada = ada_bwd(c_all, dmod_cols, w_ada, m_w_ada, v_w_ada, "ada_bwd")
    flat = lambda a: a.reshape(1, -1)
    g_bada, d_bada, m_bada, v_bada = [
        r.reshape(b_ada.shape) for r in
        bias_update(dmod_all.reshape(N_DEV * nb, n_l * 3 * d), flat(b_ada), flat(m_b_ada), flat(v_b_ada), "bias_update")]

    small_w = [norm_gain, a_ln_gain, a_ln_bias, a_w_s, a_b_s, b_lower_bounds, b_gn_gain, final_gain]
    small_m = [m_norm_gain, m_a_ln_gain, m_a_ln_bias, m_a_w_s, m_a_b_s, m_b_lower_bounds, m_b_gn_gain, m_final_gain]
    small_v = [v_norm_gain, v_a_ln_gain, v_a_ln_bias, v_a_w_s, v_a_b_s, v_b_lower_bounds, v_b_gn_gain, v_final_gain]
    small_g = [jnp.concatenate([dng0, dng1], axis=0), dlg, dlbias, dws, dbs, dlb, dgn, dfg]
    packed_g = _pack(small_g)
    rows = packed_g.shape[0]
    gathered = allgather_small(packed_g, "gather_small").reshape(N_DEV, rows, LANES)
    res = small_update(gathered, _pack(small_w), _pack(small_m), _pack(small_v), "small_update")
    sg, sd, sm, sv = [_unpack(r, small_w) for r in res]

    def order(ng, wada, bada, ain, sm_rest, aout, bin_, bout):
        lg, lbi, ws_, bs_, lbd, gn_, fg_ = sm_rest
        return [ng, wada, bada, ain, lg, lbi, ws_, bs_, aout, bin_, lbd, gn_, bout, fg_]

    grads = order(sg[0], g_wada, g_bada, ga_in, sg[1:], ga_out, gb_in, gb_out)
    deltas = order(sd[0], d_wada, d_bada, da_in, sd[1:], da_out, db_in, db_out)
    new_m = order(sm[0], m_wada, m_bada, ma_in, sm[1:], ma_out, mb_in, mb_out)
    new_v = order(sv[0], v_wada, v_bada, va_in, sv[1:], va_out, vb_in, vb_out)
    return (loss, grad_x, *grads, *deltas, *new_m, *new_v)
```

```python
import functools

import jax
import jax.numpy as jnp
from jax import lax
from jax.experimental import pallas as pl
from jax.experimental.pallas import tpu as pltpu

F32 = jnp.float32
BF16 = jnp.bfloat16
EPS = 1e-6
CHUNK = 64
SG_BLOCK = 128
SG_GROUPS = 8
HEAD_DIM = 128
HG_WIDE = 8
HG_ROWS = 256
N_CHIPS = 4
N_DEV = 8
LANES = 128
ADAM_LR = 0.001
ADAM_B1 = 0.9
ADAM_B2 = 0.999
ADAM_EPS = 1e-08
ADAM_WD = 0.01
ADAM_STEP = 10
GELU_C0 = 0.7978845608028654
GELU_C1 = 0.044715
MESH = pl.DeviceIdType.MESH
VMEM_LIMIT = 56 * 1024 * 1024


def _call(body, **kw):
    return pl.pallas_call(body, **kw)


def _params(**kw):
    return pltpu.CompilerParams(vmem_limit_bytes=VMEM_LIMIT, **kw)


def _sigmoid(x):
    return 1.0 / (1.0 + jnp.exp(-x))


def _silu_and_grad(x):
    s = _sigmoid(x)
    return x * s, s * (1.0 + x * (1.0 - s))


def _gelu(x):
    return 0.5 * x * (1.0 + jnp.tanh(GELU_C0 * (x + GELU_C1 * x * x * x)))


def _gelu_and_grad(x):
    t = jnp.tanh(GELU_C0 * (x + GELU_C1 * x * x * x))
    g = 0.5 * x * (1.0 + t)
    dg = 0.5 * (1.0 + t) + 0.5 * x * (1.0 - t * t) * (GELU_C0 * (1.0 + 3.0 * GELU_C1 * x * x))
    return g, dg


def _dot(a, b, dims, precision=None):
    return lax.dot_general(a, b, (dims, ((), ())), precision=precision, preferred_element_type=F32)


NN = ((1,), (0,))
NT = ((1,), (1,))
TN = ((0,), (0,))


def _adamw(w, g, m, v):
    m = ADAM_B1 * m + (1.0 - ADAM_B1) * g
    v = ADAM_B2 * v + (1.0 - ADAM_B2) * (g * g)
    m_hat = m / (1.0 - ADAM_B1 ** ADAM_STEP)
    v_hat = v / (1.0 - ADAM_B2 ** ADAM_STEP)
    delta = -ADAM_LR * (m_hat / (jnp.sqrt(v_hat) + ADAM_EPS) + ADAM_WD * w)
    return delta, m, v


def _chunk_mask():
    r = lax.broadcasted_iota(jnp.int32, (SG_BLOCK, SG_BLOCK), 0)
    c = lax.broadcasted_iota(jnp.int32, (SG_BLOCK, SG_BLOCK), 1)
    return (c // CHUNK) <= (r // CHUNK)


def _place():
    return lax.axis_index("x"), lax.axis_index("y"), lax.axis_index("c")


def _other_chips(x, y):
    return [(1 - x, y), (x, 1 - y), (1 - x, 1 - y)]


def allgather_small(v, name):
    m_per, n = v.shape

    def body(x_ref, out_ref, send_sems, recv_sems, local_sem):
        x, y, c = _place()
        me, sibling = (x, y, c), (x, y, 1 - c)
        chips = _other_chips(x, y)

        def rows(px, py, pc):
            return out_ref.at[pl.ds((4 * px + 2 * py + pc) * m_per, m_per), :]

        def copy(k, block, to, src=None):
            return pltpu.make_async_remote_copy(
                src_ref=rows(*block) if src is None else src, dst_ref=rows(*block),
                send_sem=send_sems.at[k], recv_sem=recv_sems.at[k], device_id=to, device_id_type=MESH)

        mine = pltpu.make_async_copy(x_ref, rows(*me), local_sem)
        mine.start()
        first = [copy(0, me, sibling, src=x_ref)]
        first += [copy(1 + j, me, (*chip, c), src=x_ref) for j, chip in enumerate(chips)]
        for cp in first:
            cp.start()
        passed = [copy(4 + j, (*chip, c), sibling) for j, chip in enumerate(chips)]
        for j, chip in enumerate(chips):
            copy(1 + j, (*chip, c), me).wait_recv()
            passed[j].start()
        copy(0, sibling, me).wait_recv()
        for j, chip in enumerate(chips):
            copy(4 + j, (*chip, 1 - c), me).wait_recv()
        for cp in first + passed:
            cp.wait_send()
        mine.wait()

    return _call(
        body, name=name,
        out_shape=jax.ShapeDtypeStruct((N_DEV * m_per, n), v.dtype),
        in_specs=[pl.BlockSpec(memory_space=pltpu.VMEM)],
        out_specs=pl.BlockSpec(memory_space=pltpu.VMEM),
        scratch_shapes=[pltpu.SemaphoreType.DMA((7,)), pltpu.SemaphoreType.DMA((7,)), pltpu.SemaphoreType.DMA],
    )(v)


def _hbm_spec():
    return pl.BlockSpec(memory_space=pltpu.HBM)


def allgather_chips(shards, name):
    n = len(shards)

    def body(*refs):
        ins, outs = refs[:n], refs[n:2 * n]
        send_sems, recv_sems, local_sems = refs[2 * n:]
        x, y, c = _place()
        chips = _other_chips(x, y)
        started = []
        for w in range(n):
            loc = pltpu.make_async_copy(ins[w], outs[w].at[2 * x + y], local_sems.at[w])
            loc.start()
            started.append(loc)
        sends = []
        for w in range(n):
            for j, (px, py) in enumerate(chips):
                cp = pltpu.make_async_remote_copy(
                    src_ref=ins[w], dst_ref=outs[w].at[2 * x + y],
                    send_sem=send_sems.at[3 * w + j], recv_sem=recv_sems.at[3 * w + j],
                    device_id=(px, py, c), device_id_type=MESH)
                cp.start()
                sends.append(cp)
        for w in range(n):
            for j, (px, py) in enumerate(chips):
                pltpu.make_async_remote_copy(
                    src_ref=ins[w], dst_ref=outs[w].at[2 * px + py],
                    send_sem=send_sems.at[3 * w + j], recv_sem=recv_sems.at[3 * w + j],
                    device_id=(px, py, c), device_id_type=MESH).wait_recv()
        for cp in sends:
            cp.wait_send()
        for loc in started:
            loc.wait()

    return _call(
        body, name=name,
        out_shape=[jax.ShapeDtypeStruct((N_CHIPS,) + s.shape, s.dtype) for s in shards],
        in_specs=[_hbm_spec()] * n, out_specs=[_hbm_spec()] * n,
        scratch_shapes=[pltpu.SemaphoreType.DMA((3 * n,)), pltpu.SemaphoreType.DMA((3 * n,)),
                        pltpu.SemaphoreType.DMA((n,))],
    )(*shards)


def exchange_chips(parts, name):
    n = len(parts)

    def body(*refs):
        ins, outs = refs[:n], refs[n:2 * n]
        send_sems, recv_sems, local_sems = refs[2 * n:]
        x, y, c = _place()
        chips = _other_chips(x, y)
        started = []
        for w in range(n):
            loc = pltpu.make_async_copy(ins[w].at[2 * x + y], outs[w].at[3], local_sems.at[w])
            loc.start()
            started.append(loc)
        sends = []
        for w in range(n):
            for j, (px, py) in enumerate(chips):
                cp = pltpu.make_async_remote_copy(
                    src_ref=ins[w].at[2 * px + py], dst_ref=outs[w].at[j],
                    send_sem=send_sems.at[3 * w + j], recv_sem=recv_sems.at[3 * w + j],
                    device_id=(px, py, c), device_id_type=MESH)
                cp.start()
                sends.append(cp)
        for cp in sends:
            cp.wait_recv()
        for cp in sends:
            cp.wait_send()
        for loc in started:
            loc.wait()

    return _call(
        body, name=name,
        out_shape=[jax.ShapeDtypeStruct(p.shape, p.dtype) for p in parts],
        in_specs=[_hbm_spec()] * n, out_specs=[_hbm_spec()] * n,
        scratch_shapes=[pltpu.SemaphoreType.DMA((3 * n,)), pltpu.SemaphoreType.DMA((3 * n,)),
                        pltpu.SemaphoreType.DMA((n,))],
    )(*parts)


def swap_sibling(arrs, name):
    n = len(arrs)

    def body(*refs):
        ins, outs = refs[:n], refs[n:2 * n]
        send_sems, recv_sems = refs[2 * n:]
        x, y, c = _place()
        cps = []
        for w in range(n):
            cp = pltpu.make_async_remote_copy(
                src_ref=ins[w], dst_ref=outs[w], send_sem=send_sems.at[w], recv_sem=recv_sems.at[w],
                device_id=(x, y, 1 - c), device_id_type=MESH)
            cp.start()
            cps.append(cp)
        for cp in cps:
            cp.wait_recv()
        for cp in cps:
            cp.wait_send()

    return _call(
        body, name=name,
        out_shape=[jax.ShapeDtypeStruct(a.shape, a.dtype) for a in arrs],
        in_specs=[_hbm_spec()] * n, out_specs=[_hbm_spec()] * n,
        scratch_shapes=[pltpu.SemaphoreType.DMA((n,)), pltpu.SemaphoreType.DMA((n,))],
    )(*arrs)


def cast_bf16(w, name):
    r, c = w.shape
    tr = min(256, r)

    def body(w_ref, o_ref):
        o_ref[...] = w_ref[...].astype(BF16)

    return _call(
        body, name=name, grid=(r // tr,),
        out_shape=jax.ShapeDtypeStruct((r, c), BF16),
        in_specs=[pl.BlockSpec((tr, c), lambda i: (i, 0))],
        out_specs=pl.BlockSpec((tr, c), lambda i: (i, 0)),
        compiler_params=_params(),
    )(w)


def sum_slots(recv, name):
    _, r, c = recv.shape
    tr = min(256, r)

    def body(r_ref, o_ref):
        acc = r_ref[3].astype(F32) + r_ref[0].astype(F32)
        acc = acc + r_ref[1].astype(F32)
        o_ref[...] = acc + r_ref[2].astype(F32)

    return _call(
        body, name=name, grid=(r // tr,),
        out_shape=jax.ShapeDtypeStruct((r, c), F32),
        in_specs=[pl.BlockSpec((N_CHIPS, tr, c), lambda i: (0, i, 0))],
        out_specs=pl.BlockSpec((tr, c), lambda i: (i, 0)),
        compiler_params=_params(),
    )(recv)


def adamw_pair(pa, pb, w, m, v, name):
    r, c = w.shape
    tr = min(128, r)

    def body(pa_ref, pb_ref, w_ref, m_ref, v_ref, g_ref, d_ref, nm_ref, nv_ref):
        g = pa_ref[...] + pb_ref[...]
        d, nm, nv = _adamw(w_ref[...], g, m_ref[...], v_ref[...])
        g_ref[...] = g
        d_ref[...] = d
        nm_ref[...] = nm
        nv_ref[...] = nv

    spec = pl.BlockSpec((tr, c), lambda i: (i, 0))
    return _call(
        body, name=name, grid=(r // tr,),
        out_shape=[jax.ShapeDtypeStruct((r, c), F32)] * 4,
        in_specs=[spec] * 5, out_specs=[spec] * 4,
        compiler_params=_params(),
    )(pa, pb, w, m, v)


def small_update(gathered, w, m, v, name):
    def body(g_ref, w_ref, m_ref, v_ref, go_ref, d_ref, nm_ref, nv_ref):
        g = g_ref[0]
        for k in range(1, N_DEV):
            g = g + g_ref[k]
        d, nm, nv = _adamw(w_ref[...], g, m_ref[...], v_ref[...])
        go_ref[...] = g
        d_ref[...] = d
        nm_ref[...] = nm
        nv_ref[...] = nv

    return _call(
        body, name=name,
        out_shape=[jax.ShapeDtypeStruct(w.shape, F32)] * 4,
        compiler_params=_params(),
    )(gathered, w, m, v)


def ada_fwd(c_all, w_ada, b_cols, name):
    n_l, d, cols = w_ada.shape
    nb = c_all.shape[0]
    tn = 256

    def body(c_ref, w_ref, b_ref, o_ref):
        cv = c_ref[...]
        ca = (cv * _sigmoid(cv)).astype(BF16)
        o_ref[...] = _dot(ca, w_ref[...].astype(BF16), NN) + b_ref[...]

    return _call(
        body, name=name, grid=(n_l, cols // tn),
        out_shape=jax.ShapeDtypeStruct((n_l, nb, cols), F32),
        in_specs=[pl.BlockSpec((nb, d), lambda l, j: (0, 0)),
                  pl.BlockSpec((None, d, tn), lambda l, j: (l, 0, j)),
                  pl.BlockSpec((None, 1, tn), lambda l, j: (l, 0, j))],
        out_specs=pl.BlockSpec((None, nb, tn), lambda l, j: (l, 0, j)),
        compiler_params=_params(),
    )(c_all, w_ada, b_cols)


def ada_bwd(c_all, dmod_cols, w, m, v, name):
    n_l, d, cols = w.shape
    nb = c_all.shape[0]
    tn = 256

    def body(c_ref, dm_ref, w_ref, m_ref, v_ref, g_ref, d_ref, nm_ref, nv_ref):
        cv = c_ref[...]
        ca = (cv * _sigmoid(cv)).astype(BF16)
        g = _dot(ca, dm_ref[...].astype(BF16), TN)
        dl, nm, nv = _adamw(w_ref[...], g, m_ref[...], v_ref[...])
        g_ref[...] = g
        d_ref[...] = dl
        nm_ref[...] = nm
        nv_ref[...] = nv

    wspec = pl.BlockSpec((None, d, tn), lambda l, j: (l, 0, j))
    return _call(
        body, name=name, grid=(n_l, cols // tn),
        out_shape=[jax.ShapeDtypeStruct((n_l, d, cols), F32)] * 4,
        in_specs=[pl.BlockSpec((nb, d), lambda l, j: (0, 0)),
                  pl.BlockSpec((None, nb, tn), lambda l, j: (l, 0, j)),
                  wspec, wspec, wspec],
        out_specs=[wspec] * 4,
        compiler_params=_params(),
    )(c_all, dmod_cols, w, m, v)


def bias_update(dmod_all, w, m, v, name):
    def body(dm_ref, w_ref, m_ref, v_ref, g_ref, d_ref, nm_ref, nv_ref):
        g = jnp.sum(dm_ref[...], axis=0, keepdims=True)
        dl, nm, nv = _adamw(w_ref[...], g, m_ref[...], v_ref[...])
        g_ref[...] = g
        d_ref[...] = dl
        nm_ref[...] = nm
        nv_ref[...] = nv

    return _call(
        body, name=name,
        out_shape=[jax.ShapeDtypeStruct(w.shape, F32)] * 4,
        compiler_params=_params(),
    )(dmod_all, w, m, v)


def inproj_fwd(x, mod, ng, wg, seq, sectioned, name):
    m_rows, d = x.shape
    nsh, _, ns = wg.shape
    n = nsh * ns
    tm, tn = min(512, seq), 512
    per = ns // tn

    def body(x_ref, mod_ref, ng_ref, w_ref, proj_ref, h_ref):
        @pl.when(pl.program_id(1) == 0)
        def _():
            xv = x_ref[...]
            r = lax.rsqrt(jnp.mean(xv * xv, axis=-1, keepdims=True) + EPS)
            md = mod_ref[0]
            h = (xv * r * ng_ref[...]) * (1.0 + md[:, d:2 * d]) + md[:, :d]
            h_ref[...] = h.astype(BF16)
        proj_ref[...] = _dot(h_ref[...], w_ref[...], NN)

    if sectioned:
        proj_shape = (nsh, m_rows, ns)
        proj_spec = pl.BlockSpec((None, tm, tn), lambda i, j: (j // per, i, j % per))
    else:
        proj_shape = (m_rows, n)
        proj_spec = pl.BlockSpec((tm, tn), lambda i, j: (i, j))
    return _call(
        body, name=name, grid=(m_rows // tm, n // tn),
        out_shape=[jax.ShapeDtypeStruct(proj_shape, F32), jax.ShapeDtypeStruct((m_rows, d), BF16)],
        in_specs=[pl.BlockSpec((tm, d), lambda i, j: (i, 0)),
                  pl.BlockSpec((1, 1, 3 * d), lambda i, j: ((i * tm) // seq, 0, 0)),
                  pl.BlockSpec((1, d), lambda i, j: (0, 0)),
                  pl.BlockSpec((None, d, tn), lambda i, j: (j // per, 0, j % per))],
        out_specs=[proj_spec, pl.BlockSpec((tm, d), lambda i, j: (i, 0))],
        compiler_params=_params(),
    )(x, mod, ng, wg)


def outproj_fwd(y, w, x, mod, seq, name):
    m_rows, di = y.shape
    d = w.shape[1]
    tm = min(512, seq)

    def body(y_ref, w_ref, x_ref, mod_ref, xn_ref, out_ref):
        acc = _dot(y_ref[...], w_ref[...], NN)
        out_ref[...] = acc
        xn_ref[...] = x_ref[...] + mod_ref[0][:, 2 * d:] * acc

    row = pl.BlockSpec((tm, d), lambda i: (i, 0))
    return _call(
        body, name=name, grid=(m_rows // tm,),
        out_shape=[jax.ShapeDtypeStruct((m_rows, d), F32)] * 2,
        in_specs=[pl.BlockSpec((tm, di), lambda i: (i, 0)),
                  pl.BlockSpec((di, d), lambda i: (0, 0)),
                  row,
                  pl.BlockSpec((1, 1, 3 * d), lambda i: ((i * tm) // seq, 0, 0))],
        out_specs=[row, row],
        compiler_params=_params(),
    )(y, w, x, mod)


def outproj_bwd(dxo, out, mod, w, seq, name):
    m_rows, d = dxo.shape
    di = w.shape[0]
    nb = m_rows // seq
    tm, tn = min(512, seq), 512

    def body(dxo_ref, out_ref, mod_ref, w_ref, dy_ref, dout_ref, dgate_ref):
        i = pl.program_id(0)

        @pl.when(pl.program_id(1) == 0)
        def _():
            dx = dxo_ref[...]
            dout_ref[...] = (mod_ref[0][:, 2 * d:] * dx).astype(BF16)
            part = jnp.sum(dx * out_ref[...], axis=0, keepdims=True)

            @pl.when((i * tm) % seq == 0)
            def _():
                dgate_ref[0] = part

            @pl.when((i * tm) % seq != 0)
            def _():
                dgate_ref[0] = dgate_ref[0] + part

        dy_ref[...] = _dot(dout_ref[...], w_ref[...], NT)

    row = pl.BlockSpec((tm, d), lambda i, j: (i, 0))
    return _call(
        body, name=name, grid=(m_rows // tm, di // tn),
        out_shape=[jax.ShapeDtypeStruct((m_rows, di), F32), jax.ShapeDtypeStruct((m_rows, d), BF16),
                   jax.ShapeDtypeStruct((nb, 1, d), F32)],
        in_specs=[row, row,
                  pl.BlockSpec((1, 1, 3 * d), lambda i, j: ((i * tm) // seq, 0, 0)),
                  pl.BlockSpec((tn, d), lambda i, j: (j, 0))],
        out_specs=[pl.BlockSpec((tm, tn), lambda i, j: (i, j)), row,
                   pl.BlockSpec((1, 1, d), lambda i, j: ((i * tm) // seq, 0, 0))],
        compiler_params=_params(),
    )(dxo, out, mod, w)


def grad_w_out(y, dout, name):
    m_rows, di = y.shape
    d = dout.shape[1]
    tm, tk = min(512, m_rows), 512
    n_m = m_rows // tm

    def body(y_ref, do_ref, o_ref, acc_ref):
        mi = pl.program_id(1)
        part = _dot(y_ref[...], do_ref[...], TN)

        @pl.when(mi == 0)
        def _():
            acc_ref[...] = part

        @pl.when(mi != 0)
        def _():
            acc_ref[...] = acc_ref[...] + part

        @pl.when(mi == n_m - 1)
        def _():
            o_ref[...] = acc_ref[...].astype(BF16)

    return _call(
        body, name=name, grid=(di // tk, n_m),
        out_shape=jax.ShapeDtypeStruct((di, d), BF16),
        in_specs=[pl.BlockSpec((tm, tk), lambda j, mi: (mi, j)),
                  pl.BlockSpec((tm, d), lambda j, mi: (mi, 0))],
        out_specs=pl.BlockSpec((tk, d), lambda j, mi: (j, 0)),
        scratch_shapes=[pltpu.VMEM((tk, d), F32)],
        compiler_params=_params(),
    )(y, dout)


def grad_w_in(h, dproj, nsh, sectioned, name):
    m_rows, d = h.shape
    n = dproj.shape[0] * dproj.shape[2] if sectioned else dproj.shape[1]
    ns = n // nsh
    tm, tn = min(512, m_rows), 512
    per = ns // tn
    n_m = m_rows // tm

    def body(h_ref, dp_ref, o_ref, acc_ref):
        mi = pl.program_id(1)
        part = _dot(h_ref[...], dp_ref[...], TN)

        @pl.when(mi == 0)
        def _():
            acc_ref[...] = part

        @pl.when(mi != 0)
        def _():
            acc_ref[...] = acc_ref[...] + part

        @pl.when(mi == n_m - 1)
        def _():
            o_ref[...] = acc_ref[...].astype(BF16)

    if sectioned:
        dp_spec = pl.BlockSpec((None, tm, tn), lambda j, mi: (j // per, mi, j % per))
    else:
        dp_spec = pl.BlockSpec((tm, tn), lambda j, mi: (mi, j))
    return _call(
        body, name=name, grid=(n // tn, n_m),
        out_shape=jax.ShapeDtypeStruct((nsh, d, ns), BF16),
        in_specs=[pl.BlockSpec((tm, d), lambda j, mi: (mi, 0)), dp_spec],
        out_specs=pl.BlockSpec((None, d, tn), lambda j, mi: (j // per, 0, j % per)),
        scratch_shapes=[pltpu.VMEM((d, tn), F32)],
        compiler_params=_params(),
    )(h, dproj)


def inproj_bwd(dproj, wg, x, dxo, mod, ng, seq, sectioned, name):
    m_rows, d = x.shape
    nsh, _, ns = wg.shape
    n = nsh * ns
    nb = m_rows // seq
    tm, tk = min(512, seq), 512
    per = ns // tk
    n_k = n // tk

    def body(dp_ref, w_ref, x_ref, dxo_ref, mod_ref, ng_ref, dxi_ref, dsh_ref, dsc_ref, dng_ref, acc_ref):
        i, k = pl.program_id(0), pl.program_id(1)
        part = _dot(dp_ref[...], w_ref[...], NT)

        @pl.when(k == 0)
        def _():
            acc_ref[...] = part

        @pl.when(k != 0)
        def _():
            acc_ref[...] = acc_ref[...] + part

        @pl.when(k == n_k - 1)
        def _():
            dh = acc_ref[...]
            xv = x_ref[...]
            r = lax.rsqrt(jnp.mean(xv * xv, axis=-1, keepdims=True) + EPS)
            xn = xv * r
            md = mod_ref[0]
            gain = ng_ref[...]
            p_shift = jnp.sum(dh, axis=0, keepdims=True)
            p_scale = jnp.sum(dh * (xn * gain), axis=0, keepdims=True)
            drn = dh * (1.0 + md[:, d:2 * d])
            p_ng = jnp.sum(drn * xn, axis=0, keepdims=True)
            dxn = drn * gain
            dx = r * (dxn - xn * jnp.mean(dxn * xn, axis=-1, keepdims=True))
            dxi_ref[...] = dxo_ref[...] + dx

            @pl.when((i * tm) % seq == 0)
            def _():
                dsh_ref[0] = p_shift
                dsc_ref[0] = p_scale

            @pl.when((i * tm) % seq != 0)
            def _():
                dsh_ref[0] = dsh_ref[0] + p_shift
                dsc_ref[0] = dsc_ref[0] + p_scale

            @pl.when(i == 0)
            def _():
                dng_ref[...] = p_ng

            @pl.when(i != 0)
            def _():
                dng_ref[...] = dng_ref[...] + p_ng

    if sectioned:
        dp_spec = pl.BlockSpec((None, tm, tk), lambda i, k: (k // per, i, k % per))
    else:
        dp_spec = pl.BlockSpec((tm, tk), lambda i, k: (i, k))
    row = pl.BlockSpec((tm, d), lambda i, k: (i, 0))
    per_seq = pl.BlockSpec((1, 1, d), lambda i, k: ((i * tm) // seq, 0, 0))
    return _call(
        body, name=name, grid=(m_rows // tm, n_k),
        out_shape=[jax.ShapeDtypeStruct((m_rows, d), F32), jax.ShapeDtypeStruct((nb, 1, d), F32),
                   jax.ShapeDtypeStruct((nb, 1, d), F32), jax.ShapeDtypeStruct((1, d), F32)],
        in_specs=[dp_spec,
                  pl.BlockSpec((None, d, tk), lambda i, k: (k // per, 0, k % per)),
                  row, row,
                  pl.BlockSpec((1, 1, 3 * d), lambda i, k: ((i * tm) // seq, 0, 0)),
                  pl.BlockSpec((1, d), lambda i, k: (0, 0))],
        out_specs=[row, per_seq, per_seq, pl.BlockSpec((1, d), lambda i, k: (0, 0))],
        scratch_shapes=[pltpu.VMEM((tm, d), F32)],
        compiler_params=_params(),
    )(dproj, wg, x, dxo, mod, ng)


def _sgu_stats(proj_ref, vg_ref, di, gd):
    s1 = jnp.zeros((SG_BLOCK, 1), F32)
    for g in range(SG_GROUPS):
        vg = _gelu(proj_ref[:, di + g * gd:di + (g + 1) * gd])
        vg_ref[:, g * gd:(g + 1) * gd] = vg
        s1 = s1 + jnp.sum(vg, axis=1, keepdims=True)
    mu = s1 / di
    s2 = jnp.zeros((SG_BLOCK, 1), F32)
    for g in range(SG_GROUPS):
        dv = vg_ref[:, g * gd:(g + 1) * gd] - mu
        s2 = s2 + jnp.sum(dv * dv, axis=1, keepdims=True)
    return mu, lax.rsqrt(s2 / di + EPS)


def sgu_fwd(proj, ln_gain, ln_bias, ws, bs, name):
    m_rows, n3 = proj.shape
    di = n3 // 3
    gd = di // SG_GROUPS

    def body(proj_ref, lg_ref, lb_ref, ws_ref, bs_ref, y_ref, wsm_ref, vg_ref):
        @pl.when(pl.program_id(0) == 0)
        def _():
            mask = _chunk_mask()
            for g in range(SG_GROUPS):
                wsm_ref[g] = jnp.where(mask, ws_ref[g], 0.0).astype(BF16)

        mu, rstd = _sgu_stats(proj_ref, vg_ref, di, gd)
        for g in range(SG_GROUPS):
            cs = slice(g * gd, (g + 1) * gd)
            vln = (vg_ref[:, cs] - mu) * rstd * lg_ref[:, cs] + lb_ref[:, cs]
            s = _dot(wsm_ref[g], vln.astype(BF16), NN) + bs_ref[g]
            u = _gelu(proj_ref[:, cs])
            gp = proj_ref[:, 2 * di + g * gd:2 * di + (g + 1) * gd]
            y_ref[:, cs] = (u * s * (gp * _sigmoid(gp))).astype(BF16)

    full = lambda shape: pl.BlockSpec(shape, lambda i: (0,) * len(shape))
    return _call(
        body, name=name, grid=(m_rows // SG_BLOCK,),
        out_shape=jax.ShapeDtypeStruct((m_rows, di), BF16),
        in_specs=[pl.BlockSpec((SG_BLOCK, n3), lambda i: (i, 0)),
                  full((1, di)), full((1, di)),
                  full((SG_GROUPS, SG_BLOCK, SG_BLOCK)), full((SG_GROUPS, SG_BLOCK, 1))],
        out_specs=pl.BlockSpec((SG_BLOCK, di), lambda i: (i, 0)),
        scratch_shapes=[pltpu.VMEM((SG_GROUPS, SG_BLOCK, SG_BLOCK), BF16), pltpu.VMEM((SG_BLOCK, di), F32)],
        compiler_params=_params(),
    )(proj, ln_gain, ln_bias, ws, bs)


def sgu_bwd(proj, dy, ln_gain, ln_bias, ws, bs, name):
    m_rows, n3 = proj.shape
    di = n3 // 3
    gd = di // SG_GROUPS
    n_i = m_rows // SG_BLOCK

    def body(proj_ref, dy_ref, lg_ref, lb_ref, ws_ref, bs_ref,
             dp_ref, dws_ref, dbs_ref, dlg_ref, dlb_ref, wsm_ref, vg_ref, dvh_ref):
        i = pl.program_id(0)

        @pl.when(i == 0)
        def _():
            mask = _chunk_mask()
            for g in range(SG_GROUPS):
                wsm_ref[g] = jnp.where(mask, ws_ref[g], 0.0).astype(BF16)
            dws_ref[...] = jnp.zeros_like(dws_ref)
            dbs_ref[...] = jnp.zeros_like(dbs_ref)
            dlg_ref[...] = jnp.zeros_like(dlg_ref)
            dlb_ref[...] = jnp.zeros_like(dlb_ref)

        mu, rstd = _sgu_stats(proj_ref, vg_ref, di, gd)
        m1 = jnp.zeros((SG_BLOCK, 1), F32)
        m2 = jnp.zeros((SG_BLOCK, 1), F32)
        for g in range(SG_GROUPS):
            cs = slice(g * gd, (g + 1) * gd)
            gs = slice(2 * di + g * gd, 2 * di + (g + 1) * gd)
            gain = lg_ref[:, cs]
            vhat = (vg_ref[:, cs] - mu) * rstd
            vln_b = (vhat * gain + lb_ref[:, cs]).astype(BF16)
            s = _dot(wsm_ref[g], vln_b, NN) + bs_ref[g]
            u, du = _gelu_and_grad(proj_ref[:, cs])
            sg, dsg = _silu_and_grad(proj_ref[:, gs])
            dyv = dy_ref[:, cs]
            dp_ref[:, cs] = (dyv * s * sg * du).astype(BF16)
            dp_ref[:, gs] = (dyv * u * s * dsg).astype(BF16)
            ds = dyv * u * sg
            ds_b = ds.astype(BF16)
            dws_ref[g] = dws_ref[g] + _dot(ds_b, vln_b, NT)
            dbs_ref[g] = dbs_ref[g] + jnp.sum(ds, axis=1, keepdims=True)
            dvln = _dot(wsm_ref[g], ds_b, TN)
            dlg_ref[:, cs] = dlg_ref[:, cs] + jnp.sum(dvln * vhat, axis=0, keepdims=True)
            dlb_ref[:, cs] = dlb_ref[:, cs] + jnp.sum(dvln, axis=0, keepdims=True)
            dvh = dvln * gain
            dvh_ref[:, cs] = dvh
            m1 = m1 + jnp.sum(dvh, axis=1, keepdims=True)
            m2 = m2 + jnp.sum(dvh * vhat, axis=1, keepdims=True)
        m1 = m1 / di
        m2 = m2 / di
        for g in range(SG_GROUPS):
            cs = slice(g * gd, (g + 1) * gd)
            vs = slice(di + g * gd, di + (g + 1) * gd)
            vhat = (vg_ref[:, cs] - mu) * rstd
            dvg = rstd * (dvh_ref[:, cs] - m1 - vhat * m2)
            _, dgel = _gelu_and_grad(proj_ref[:, vs])
            dp_ref[:, vs] = (dvg * dgel).astype(BF16)

        @pl.when(i == n_i - 1)
        def _():
            mask = _chunk_mask()
            for g in range(SG_GROUPS):
                dws_ref[g] = jnp.where(mask, dws_ref[g], 0.0)

    full = lambda shape: pl.BlockSpec(shape, lambda i: (0,) * len(shape))
    return _call(
        body, name=name, grid=(n_i,),
        out_shape=[jax.ShapeDtypeStruct((m_rows, n3), BF16),
                   jax.ShapeDtypeStruct((SG_GROUPS, SG_BLOCK, SG_BLOCK), F32),
                   jax.ShapeDtypeStruct((SG_GROUPS, SG_BLOCK, 1), F32),
                   jax.ShapeDtypeStruct((1, di), F32), jax.ShapeDtypeStruct((1, di), F32)],
        in_specs=[pl.BlockSpec((SG_BLOCK, n3), lambda i: (i, 0)),
                  pl.BlockSpec((SG_BLOCK, di), lambda i: (i, 0)),
                  full((1, di)), full((1, di)),
                  full((SG_GROUPS, SG_BLOCK, SG_BLOCK)), full((SG_GROUPS, SG_BLOCK, 1))],
        out_specs=[pl.BlockSpec((SG_BLOCK, n3), lambda i: (i, 0)),
                   full((SG_GROUPS, SG_BLOCK, SG_BLOCK)), full((SG_GROUPS, SG_BLOCK, 1)),
                   full((1, di)), full((1, di))],
        scratch_shapes=[pltpu.VMEM((SG_GROUPS, SG_BLOCK, SG_BLOCK), BF16),
                        pltpu.VMEM((SG_BLOCK, di), F32), pltpu.VMEM((SG_BLOCK, di), F32)],
        compiler_params=_params(),
    )(proj, dy, ln_gain, ln_bias, ws, bs)


def _lower_bound(lbraw):
    mx = jnp.maximum(lbraw[0:1, :], lbraw[1:2, :])
    e0 = jnp.exp(lbraw[0:1, :] - mx)
    e1 = jnp.exp(lbraw[1:2, :] - mx)
    p0 = e0 / (e0 + e1)
    p1 = e1 / (e0 + e1)
    return (p0 + p1) - p0, p0, p1


def _tri(lower):
    r = lax.broadcasted_iota(jnp.int32, (CHUNK, CHUNK), 0)
    c = lax.broadcasted_iota(jnp.int32, (CHUNK, CHUNK), 1)
    return ((r >= c) if lower else (c >= r)).astype(F32)


def _row(a, idx):
    r = lax.broadcasted_iota(jnp.int32, a.shape, 0)
    return jnp.sum(jnp.where(r == idx, a, 0.0), axis=0, keepdims=True)


def _hgrn_gates(qp, fp, lb, tri):
    sgm = _sigmoid(fp)
    f = lb + (1.0 - lb) * sgm
    k = 1.0 - f
    a = _dot(tri, jnp.log(f), NN, precision=lax.Precision.HIGHEST)
    a_mid = _row(a, CHUNK // 2 - 1)
    a_last = _row(a, CHUNK - 1)
    q, dq = _silu_and_grad(qp)
    e1, e2, e3, e4 = jnp.exp(a - a_mid), jnp.exp(a_mid - a), jnp.exp(a), jnp.exp(a_last - a)
    return dict(sgm=sgm, f=f, k=k, q=q, dq=dq, e1=e1, e2=e2, e3=e3, e4=e4, dec=jnp.exp(a_last),
                q_in=q * e1, k_in=k * e2, q_out=q * e3, k_out=k * e4)


def _causal():
    r = lax.broadcasted_iota(jnp.int32, (CHUNK, CHUNK), 0)
    c = lax.broadcasted_iota(jnp.int32, (CHUNK, CHUNK), 1)
    return r >= c


def hgrn_fwd(proj4, lbraw, gn, seq, name):
    _, m_rows, di = proj4.shape
    nb, nh, nc = m_rows // seq, di // HEAD_DIM, seq // CHUNK
    rows = min(HG_ROWS, seq)
    wide = HG_WIDE * HEAD_DIM
    ns, cpb = seq // rows, rows // CHUNK

    def body(p_ref, lb_ref, gn_ref, y_ref, sts_ref, st_ref):
        @pl.when(pl.program_id(2) == 0)
        def _():
            st_ref[...] = jnp.zeros_like(st_ref)

        tri = _tri(True)
        causal = _causal()
        gain = gn_ref[...]
        lbs = [_lower_bound(lb_ref[:, j * HEAD_DIM:(j + 1) * HEAD_DIM])[0] for j in range(HG_WIDE)]

        def chunk(n, carry):
            rs = pl.ds(pl.multiple_of(n * CHUNK, CHUNK), CHUNK)
            for j in range(HG_WIDE):
                cs = slice(j * HEAD_DIM, (j + 1) * HEAD_DIM)
                st = st_ref[j]
                sts_ref[n, :, cs] = st
                t = _hgrn_gates(p_ref[0, rs, cs], p_ref[1, rs, cs], lbs[j], tri)
                v_b = p_ref[2, rs, cs].astype(BF16)
                sc = jnp.where(causal, _dot(t["q_in"].astype(BF16), t["k_in"].astype(BF16), NT), 0.0)
                o = _dot(sc.astype(BF16), v_b, NN) + _dot(t["q_out"].astype(BF16), st.astype(BF16), NT)
                r = lax.rsqrt(jnp.mean(o * o, axis=-1, keepdims=True) + EPS)
                gp = p_ref[3, rs, cs]
                y_ref[rs, cs] = ((o * r * gain) * (gp * _sigmoid(gp))).astype(BF16)
                st_ref[j] = st * t["dec"] + _dot(v_b, t["k_out"].astype(BF16), TN)
            return carry

        lax.fori_loop(0, cpb, chunk, 0)

    return _call(
        body, name=name, grid=(nh // HG_WIDE, nb, ns),
        out_shape=[jax.ShapeDtypeStruct((m_rows, di), BF16),
                   jax.ShapeDtypeStruct((nb * nc, HEAD_DIM, di), F32)],
        in_specs=[pl.BlockSpec((4, rows, wide), lambda hg, b, s: (0, b * ns + s, hg)),
                  pl.BlockSpec((2, wide), lambda hg, b, s: (0, hg)),
                  pl.BlockSpec((1, HEAD_DIM), lambda hg, b, s: (0, 0))],
        out_specs=[pl.BlockSpec((rows, wide), lambda hg, b, s: (b * ns + s, hg)),
                   pl.BlockSpec((cpb, HEAD_DIM, wide), lambda hg, b, s: (b * ns + s, 0, hg))],
        scratch_shapes=[pltpu.VMEM((HG_WIDE, HEAD_DIM, HEAD_DIM), F32)],
        compiler_params=_params(),
    )(proj4, lbraw, gn)


def hgrn_bwd(proj4, dy, sts, lbraw, gn, seq, name):
    _, m_rows, di = proj4.shape
    nb, nh, nc = m_rows // seq, di // HEAD_DIM, seq // CHUNK
    rows = min(HG_ROWS, seq)
    wide = HG_WIDE * HEAD_DIM
    ns, cpb = seq // rows, rows // CHUNK
    n_hg = nh // HG_WIDE

    def body(p_ref, dy_ref, sts_ref, lb_ref, gn_ref, dp_ref, dlb_ref, dgn_ref, dst_ref, lbacc_ref, gnacc_ref):
        hg, b, s = pl.program_id(0), pl.program_id(1), pl.program_id(2)
        tri, triu = _tri(True), _tri(False)
        causal = _causal()
        gain = gn_ref[...]
        first = (b == 0) & (s == 0)

        @pl.when((hg == 0) & first)
        def _():
            gnacc_ref[...] = jnp.zeros_like(gnacc_ref)

        @pl.when(first)
        def _():
            lbacc_ref[...] = jnp.zeros_like(lbacc_ref)

        @pl.when(s == 0)
        def _():
            dst_ref[...] = jnp.zeros_like(dst_ref)

        def back(i, carry):
            n = cpb - 1 - i
            rs = pl.ds(pl.multiple_of(n * CHUNK, CHUNK), CHUNK)
            for j in range(HG_WIDE):
                _hgrn_bwd_chunk(j, n, rs, p_ref, dy_ref, sts_ref, lb_ref, dp_ref, dst_ref, lbacc_ref, gnacc_ref,
                                tri, triu, causal, gain)
            return carry

        lax.fori_loop(0, cpb, back, 0)

        @pl.when((b == nb - 1) & (s == ns - 1))
        def _():
            for j in range(HG_WIDE):
                cs = slice(j * HEAD_DIM, (j + 1) * HEAD_DIM)
                _, p0, p1 = _lower_bound(lb_ref[:, cs])
                acc = lbacc_ref[:, cs]
                dlb_ref[0:1, cs] = -acc * p0 * p1
                dlb_ref[1:2, cs] = acc * p1 * (1.0 - p1)

        @pl.when((hg == n_hg - 1) & (b == nb - 1) & (s == ns - 1))
        def _():
            tot = gnacc_ref[:, 0:HEAD_DIM]
            for j in range(1, HG_WIDE):
                tot = tot + gnacc_ref[:, j * HEAD_DIM:(j + 1) * HEAD_DIM]
            dgn_ref[...] = tot

    blk = lambda hg, b, s: b * ns + (ns - 1 - s)
    return _call(
        body, name=name, grid=(n_hg, nb, ns),
        out_shape=[jax.ShapeDtypeStruct((4, m_rows, di), BF16), jax.ShapeDtypeStruct((2, di), F32),
                   jax.ShapeDtypeStruct((1, HEAD_DIM), F32)],
        in_specs=[pl.BlockSpec((4, rows, wide), lambda hg, b, s: (0, blk(hg, b, s), hg)),
                  pl.BlockSpec((rows, wide), lambda hg, b, s: (blk(hg, b, s), hg)),
                  pl.BlockSpec((cpb, HEAD_DIM, wide), lambda hg, b, s: (blk(hg, b, s), 0, hg)),
                  pl.BlockSpec((2, wide), lambda hg, b, s: (0, hg)),
                  pl.BlockSpec((1, HEAD_DIM), lambda hg, b, s: (0, 0))],
        out_specs=[pl.BlockSpec((4, rows, wide), lambda hg, b, s: (0, blk(hg, b, s), hg)),
                   pl.BlockSpec((2, wide), lambda hg, b, s: (0, hg)),
                   pl.BlockSpec((1, HEAD_DIM), lambda hg, b, s: (0, 0))],
        scratch_shapes=[pltpu.VMEM((HG_WIDE, HEAD_DIM, HEAD_DIM), F32), pltpu.VMEM((1, wide), F32),
                        pltpu.VMEM((1, wide), F32)],
        compiler_params=_params(),
    )(proj4, dy, sts, lbraw, gn)


def _hgrn_bwd_chunk(j, n, rs, p_ref, dy_ref, sts_ref, lb_ref, dp_ref, dst_ref, lbacc_ref, gnacc_ref,
                    tri, triu, causal, gain):
    cs = slice(j * HEAD_DIM, (j + 1) * HEAD_DIM)
    lb = _lower_bound(lb_ref[:, cs])[0]
    t = _hgrn_gates(p_ref[0, rs, cs], p_ref[1, rs, cs], lb, tri)
    st = sts_ref[n, :, cs]
    dst = dst_ref[j]
    st_b, dst_b = st.astype(BF16), dst.astype(BF16)
    v_b = p_ref[2, rs, cs].astype(BF16)
    q_in_b, k_in_b = t["q_in"].astype(BF16), t["k_in"].astype(BF16)
    q_out_b, k_out_b = t["q_out"].astype(BF16), t["k_out"].astype(BF16)
    sc_b = jnp.where(causal, _dot(q_in_b, k_in_b, NT), 0.0).astype(BF16)
    o = _dot(sc_b, v_b, NN) + _dot(q_out_b, st_b, NT)
    r = lax.rsqrt(jnp.mean(o * o, axis=-1, keepdims=True) + EPS)
    ohat = o * r
    sg, dsg = _silu_and_grad(p_ref[3, rs, cs])
    dyv = dy_ref[rs, cs]
    dp_ref[3, rs, cs] = (dyv * (ohat * gain) * dsg).astype(BF16)
    d_on = dyv * sg
    gnacc_ref[:, cs] = gnacc_ref[:, cs] + jnp.sum(d_on * ohat, axis=0, keepdims=True)
    dohat = d_on * gain
    do = r * (dohat - ohat * jnp.mean(dohat * ohat, axis=-1, keepdims=True))
    do_b = do.astype(BF16)
    dq_out = _dot(do_b, st_b, NN)
    dk_out = _dot(v_b, dst_b, NN)
    dv = _dot(k_out_b, dst_b, NT) + _dot(sc_b, do_b, TN)
    dp_ref[2, rs, cs] = dv.astype(BF16)
    dsc_b = jnp.where(causal, _dot(do_b, v_b, NT), 0.0).astype(BF16)
    dq_in = _dot(dsc_b, k_in_b, NN)
    dk_in = _dot(dsc_b, q_in_b, TN)
    ddec = jnp.sum(dst * st, axis=0, keepdims=True)
    dq = dq_in * t["e1"] + dq_out * t["e3"]
    dk = dk_in * t["e2"] + dk_out * t["e4"]
    w_in = dq_in * t["q_in"] - dk_in * t["k_in"]
    w_out = dk_out * t["k_out"]
    da = w_in + dq_out * t["q_out"] - w_out
    da_mid = -jnp.sum(w_in, axis=0, keepdims=True)
    da_last = jnp.sum(w_out, axis=0, keepdims=True) + ddec * t["dec"]
    rid = lax.broadcasted_iota(jnp.int32, da.shape, 0)
    da = da + jnp.where(rid == CHUNK // 2 - 1, da_mid, 0.0) + jnp.where(rid == CHUNK - 1, da_last, 0.0)
    dlf = _dot(triu, da, NN, precision=lax.Precision.HIGHEST)
    df = dlf / t["f"] - dk
    sgm = t["sgm"]
    dp_ref[1, rs, cs] = (df * (1.0 - lb) * sgm * (1.0 - sgm)).astype(BF16)
    lbacc_ref[:, cs] = lbacc_ref[:, cs] + jnp.sum(df * (1.0 - sgm), axis=0, keepdims=True)
    dp_ref[0, rs, cs] = (dq * t["dq"]).astype(BF16)
    dst_ref[j] = dst * t["dec"] + _dot(do_b, q_out_b, TN)


def final_loss(x, fg, target, name):
    m_rows, d = x.shape
    tm = min(512, m_rows)

    def body(x_ref, fg_ref, t_ref, loss_ref, dx_ref, dfg_ref):
        i = pl.program_id(0)
        xv = x_ref[...]
        gain = fg_ref[...]
        r = lax.rsqrt(jnp.mean(xv * xv, axis=-1, keepdims=True) + EPS)
        xn = xv * r
        e = xn * gain - t_ref[...]
        part = 0.5 * jnp.sum(jnp.mean(e * e, axis=-1, keepdims=True), axis=0, keepdims=True)
        dyv = e / d
        p_fg = jnp.sum(dyv * xn, axis=0, keepdims=True)
        dxn = dyv * gain
        dx_ref[...] = r * (dxn - xn * jnp.mean(dxn * xn, axis=-1, keepdims=True))

        @pl.when(i == 0)
        def _():
            loss_ref[...] = part
            dfg_ref[...] = p_fg

        @pl.when(i != 0)
        def _():
            loss_ref[...] = loss_ref[...] + part
            dfg_ref[...] = dfg_ref[...] + p_fg

    row = pl.BlockSpec((tm, d), lambda i: (i, 0))
    return _call(
        body, name=name, grid=(m_rows // tm,),
        out_shape=[jax.ShapeDtypeStruct((1, 1), F32), jax.ShapeDtypeStruct((m_rows, d), F32),
                   jax.ShapeDtypeStruct((1, d), F32)],
        in_specs=[row, pl.BlockSpec((1, d), lambda i: (0, 0)), row],
        out_specs=[pl.BlockSpec((1, 1), lambda i: (0, 0)), row, pl.BlockSpec((1, d), lambda i: (0, 0))],
        compiler_params=_params(),
    )(x, fg, target)


def _pack(parts):
    flat = jnp.concatenate([p.reshape(-1) for p in parts])
    pad = (-flat.shape[0]) % (8 * LANES)
    return jnp.pad(flat, (0, pad)).reshape(-1, LANES)


def _unpack(packed, like):
    flat = packed.reshape(-1)
    out, off = [], 0
    for a in like:
        out.append(flat[off:off + a.size].reshape(a.shape))
        off += a.size
    return out


def kernel(x, c, norm_gain, w_ada, b_ada, a_w_in, a_ln_gain, a_ln_bias, a_w_s, a_b_s, a_w_out, b_w_in, b_lower_bounds, b_gn_gain, b_w_out, final_gain, loss_target, m_norm_gain, m_w_ada, m_b_ada, m_a_w_in, m_a_ln_gain, m_a_ln_bias, m_a_w_s, m_a_b_s, m_a_w_out, m_b_w_in, m_b_lower_bounds, m_b_gn_gain, m_b_w_out, m_final_gain, v_norm_gain, v_w_ada, v_b_ada, v_a_w_in, v_a_ln_gain, v_a_ln_bias, v_a_w_s, v_a_b_s, v_a_w_out, v_b_w_in, v_b_lower_bounds, v_b_gn_gain, v_b_w_out, v_final_gain):
    nb, seq, d = x.shape
    m_rows = nb * seq
    n_l = w_ada.shape[0]
    ada_cols = w_ada.shape[2]
    px, py, pc = _place()
    chip = 2 * px + py
    dev = 2 * chip + pc

    c_all = allgather_small(c.reshape(-1, LANES), "gather_c").reshape(N_DEV * nb, d)
    b_cols = lax.dynamic_slice_in_dim(b_ada, chip * ada_cols, ada_cols, axis=1).reshape(n_l, 1, ada_cols)
    mod_cols = ada_fwd(c_all, w_ada, b_cols, "ada_fwd")
    mod_g = allgather_small(mod_cols.reshape(-1, LANES), "gather_mod")
    mod_g = mod_g.reshape(N_CHIPS, 2, n_l, N_DEV * nb, ada_cols)[:, 0]
    mod_all = jnp.transpose(mod_g, (1, 2, 0, 3)).reshape(n_l, N_DEV * nb, 3 * d)
    mod_mine = lax.dynamic_slice_in_dim(mod_all, dev * nb, nb, axis=1)
    mod0 = mod_mine[0].reshape(nb, 1, 3 * d)
    mod1 = mod_mine[1].reshape(nb, 1, 3 * d)

    shards = [cast_bf16(a_w_in[0], "cast_a_in"), cast_bf16(a_w_out[0], "cast_a_out"),
              cast_bf16(b_w_in[0], "cast_b_in"), cast_bf16(b_w_out[0], "cast_b_out")]
    wa_in, wa_out, wb_in, wb_out = allgather_chips(shards, "gather_weights")
    di = wa_out.shape[1] * N_CHIPS
    wa_out = wa_out.reshape(di, d)
    wb_out = wb_out.reshape(di, d)

    x0 = x.reshape(m_rows, d)
    tgt = loss_target.reshape(m_rows, d)
    ng0, ng1 = norm_gain[0:1], norm_gain[1:2]
    bs_col = a_b_s[0].reshape(SG_GROUPS, SG_BLOCK, 1)
    proj_a, h_a = inproj_fwd(x0, mod0, ng0, wa_in, seq, False, "a_inproj")
    y_a = sgu_fwd(proj_a, a_ln_gain, a_ln_bias, a_w_s[0], bs_col, "a_sgu")
    x1, out_a = outproj_fwd(y_a, wa_out, x0, mod0, seq, "a_outproj")
    proj_b, h_b = inproj_fwd(x1, mod1, ng1, wb_in, seq, True, "b_inproj")
    y_b, sts_b = hgrn_fwd(proj_b, b_lower_bounds, b_gn_gain, seq, "b_hgrn")
    x2, out_b = outproj_fwd(y_b, wb_out, x1, mod1, seq, "b_outproj")
    loss_part, dx2, dfg = final_loss(x2, final_gain.reshape(1, d), tgt, "loss_head")
    loss = lax.psum(loss_part[0, 0], ("x", "y", "c"))

    dy_b, dout_b, dgate1 = outproj_bwd(dx2, out_b, mod1, wb_out, seq, "b_outproj_bwd")
    gwb_out = grad_w_out(y_b, dout_b, "b_grad_w_out")
    dproj_b, dlb, dgn = hgrn_bwd(proj_b, dy_b, sts_b, b_lower_bounds, b_gn_gain, seq, "b_hgrn_bwd")
    gwb_in = grad_w_in(h_b, dproj_b, N_CHIPS, True, "b_grad_w_in")
    dx1, dshift1, dscale1, dng1 = inproj_bwd(dproj_b, wb_in, x1, dx2, mod1, ng1, seq, True, "b_inproj_bwd")

    dy_a, dout_a, dgate0 = outproj_bwd(dx1, out_a, mod0, wa_out, seq, "a_outproj_bwd")
    gwa_out = grad_w_out(y_a, dout_a, "a_grad_w_out")
    dproj_a, dws, dbs, dlg, dlbias = sgu_bwd(proj_a, dy_a, a_ln_gain, a_ln_bias, a_w_s[0], bs_col, "a_sgu_bwd")
    gwa_in = grad_w_in(h_a, dproj_a, N_CHIPS, False, "a_grad_w_in")
    dx0, dshift0, dscale0, dng0 = inproj_bwd(dproj_a, wa_in, x0, dx1, mod0, ng0, seq, False, "a_inproj_bwd")
    grad_x = dx0.reshape(nb, seq, d)

    parts = [gwa_in, gwa_out.reshape(N_CHIPS, di // N_CHIPS, d), gwb_in, gwb_out.reshape(N_CHIPS, di // N_CHIPS, d)]
    recv = exchange_chips(parts, "exchange_grads")
    sums = [sum_slots(r, "sum_" + nm) for r, nm in zip(recv, ("a_in", "a_out", "b_in", "b_out"))]
    theirs = swap_sibling(sums, "swap_sums")
    big = []
    for mine, other, w, m, v, nm in zip(
            sums, theirs, (a_w_in, a_w_out, b_w_in, b_w_out), (m_a_w_in, m_a_w_out, m_b_w_in, m_b_w_out),
            (v_a_w_in, v_a_w_out, v_b_w_in, v_b_w_out), ("a_in", "a_out", "b_in", "b_out")):
        res = adamw_pair(mine, other, w[0], m[0], v[0], "adamw_" + nm)
        big.append([r.reshape(w.shape) for r in res])
    (ga_in, da_in, ma_in, va_in), (ga_out, da_out, ma_out, va_out), \
        (gb_in, db_in, mb_in, vb_in), (gb_out, db_out, mb_out, vb_out) = big

    dmod = jnp.concatenate([dshift0, dscale0, dgate0, dshift1, dscale1, dgate1], axis=2)
    dmod_all = allgather_small(dmod.reshape(-1, LANES), "gather_dmod").reshape(N_DEV * nb, n_l, 3 * d)
    dmod_cols = lax.dynamic_slice_in_dim(dmod_all, chip * ada_cols, ada_cols, axis=2)
    dmod_cols = jnp.transpose(dmod_cols, (1, 0, 2))
    g_wada, d_wada, m_wada, v_wada = ada_bwd(c_all, dmod_cols, w_ada, m_w_ada, v_w_ada, "ada_bwd")
    flat = lambda a: a.reshape(1, -1)
    g_bada, d_bada, m_bada, v_bada = [
        r.reshape(b_ada.shape) for r in
        bias_update(dmod_all.reshape(N_DEV * nb, n_l * 3 * d), flat(b_ada), flat(m_b_ada), flat(v_b_ada), "bias_update")]

    small_w = [norm_gain, a_ln_gain, a_ln_bias, a_w_s, a_b_s, b_lower_bounds, b_gn_gain, final_gain]
    small_m = [m_norm_gain, m_a_ln_gain, m_a_ln_bias, m_a_w_s, m_a_b_s, m_b_lower_bounds, m_b_gn_gain, m_final_gain]
    small_v = [v_norm_gain, v_a_ln_gain, v_a_ln_bias, v_a_w_s, v_a_b_s, v_b_lower_bounds, v_b_gn_gain, v_final_gain]
    small_g = [jnp.concatenate([dng0, dng1], axis=0), dlg, dlbias, dws, dbs, dlb, dgn, dfg]
    packed_g = _pack(small_g)
    rows = packed_g.shape[0]
    gathered = allgather_small(packed_g, "gather_small").reshape(N_DEV, rows, LANES)
    res = small_update(gathered, _pack(small_w), _pack(small_m), _pack(small_v), "small_update")
    sg, sd, sm, sv = [_unpack(r, small_w) for r in res]

    def order(ng, wada, bada, ain, sm_rest, aout, bin_, bout):
        lg, lbi, ws_, bs_, lbd, gn_, fg_ = sm_rest
        return [ng, wada, bada, ain, lg, lbi, ws_, bs_, aout, bin_, lbd, gn_, bout, fg_]

    grads = order(sg[0], g_wada, g_bada, ga_in, sg[1:], ga_out, gb_in, gb_out)
    deltas = order(sd[0], d_wada, d_bada, da_in, sd[1:], da_out, db_in, db_out)
    new_m = order(sm[0], m_wada, m_bada, ma_in, sm[1:], ma_out, mb_in, mb_out)
    new_v = order(sv[0], v_wada, v_bada, va_in, sv[1:], va_out, vb_in, vb_out)
    return (loss, grad_x, *grads, *deltas, *new_m, *new_v)
```

```python
import functools

import jax
import jax.numpy as jnp
from jax import lax
from jax.experimental import pallas as pl
from jax.experimental.pallas import tpu as pltpu

F32 = jnp.float32
BF16 = jnp.bfloat16
EPS = 1e-6
CHUNK = 64
SG_BLOCK = 128
SG_GROUPS = 8
HEAD_DIM = 128
HG_WIDE = 8
HG_ROWS = 128
N_CHIPS = 4
N_DEV = 8
LANES = 128
ADAM_LR = 0.001
ADAM_B1 = 0.9
ADAM_B2 = 0.999
ADAM_EPS = 1e-08
ADAM_WD = 0.01
ADAM_STEP = 10
GELU_C0 = 0.7978845608028654
GELU_C1 = 0.044715
MESH = pl.DeviceIdType.MESH
VMEM_LIMIT = 56 * 1024 * 1024


def _call(body, **kw):
    return pl.pallas_call(body, **kw)


def _params(**kw):
    return pltpu.CompilerParams(vmem_limit_bytes=VMEM_LIMIT, **kw)


def _sigmoid(x):
    return 1.0 / (1.0 + jnp.exp(-x))


def _silu_and_grad(x):
    s = _sigmoid(x)
    return x * s, s * (1.0 + x * (1.0 - s))


def _gelu(x):
    return 0.5 * x * (1.0 + jnp.tanh(GELU_C0 * (x + GELU_C1 * x * x * x)))


def _gelu_and_grad(x):
    t = jnp.tanh(GELU_C0 * (x + GELU_C1 * x * x * x))
    g = 0.5 * x * (1.0 + t)
    dg = 0.5 * (1.0 + t) + 0.5 * x * (1.0 - t * t) * (GELU_C0 * (1.0 + 3.0 * GELU_C1 * x * x))
    return g, dg


def _dot(a, b, dims, precision=None):
    return lax.dot_general(a, b, (dims, ((), ())), precision=precision, preferred_element_type=F32)


NN = ((1,), (0,))
NT = ((1,), (1,))
TN = ((0,), (0,))


def _adamw(w, g, m, v):
    m = ADAM_B1 * m + (1.0 - ADAM_B1) * g
    v = ADAM_B2 * v + (1.0 - ADAM_B2) * (g * g)
    m_hat = m / (1.0 - ADAM_B1 ** ADAM_STEP)
    v_hat = v / (1.0 - ADAM_B2 ** ADAM_STEP)
    delta = -ADAM_LR * (m_hat / (jnp.sqrt(v_hat) + ADAM_EPS) + ADAM_WD * w)
    return delta, m, v


def _chunk_mask():
    r = lax.broadcasted_iota(jnp.int32, (SG_BLOCK, SG_BLOCK), 0)
    c = lax.broadcasted_iota(jnp.int32, (SG_BLOCK, SG_BLOCK), 1)
    return (c // CHUNK) <= (r // CHUNK)


def _place():
    return lax.axis_index("x"), lax.axis_index("y"), lax.axis_index("c")


def _other_chips(x, y):
    return [(1 - x, y), (x, 1 - y), (1 - x, 1 - y)]


def allgather_small(v, name):
    m_per, n = v.shape

    def body(x_ref, out_ref, send_sems, recv_sems, local_sem):
        x, y, c = _place()
        me, sibling = (x, y, c), (x, y, 1 - c)
        chips = _other_chips(x, y)

        def rows(px, py, pc):
            return out_ref.at[pl.ds((4 * px + 2 * py + pc) * m_per, m_per), :]

        def copy(k, block, to, src=None):
            return pltpu.make_async_remote_copy(
                src_ref=rows(*block) if src is None else src, dst_ref=rows(*block),
                send_sem=send_sems.at[k], recv_sem=recv_sems.at[k], device_id=to, device_id_type=MESH)

        mine = pltpu.make_async_copy(x_ref, rows(*me), local_sem)
        mine.start()
        first = [copy(0, me, sibling, src=x_ref)]
        first += [copy(1 + j, me, (*chip, c), src=x_ref) for j, chip in enumerate(chips)]
        for cp in first:
            cp.start()
        passed = [copy(4 + j, (*chip, c), sibling) for j, chip in enumerate(chips)]
        for j, chip in enumerate(chips):
            copy(1 + j, (*chip, c), me).wait_recv()
            passed[j].start()
        copy(0, sibling, me).wait_recv()
        for j, chip in enumerate(chips):
            copy(4 + j, (*chip, 1 - c), me).wait_recv()
        for cp in first + passed:
            cp.wait_send()
        mine.wait()

    return _call(
        body, name=name,
        out_shape=jax.ShapeDtypeStruct((N_DEV * m_per, n), v.dtype),
        in_specs=[pl.BlockSpec(memory_space=pltpu.VMEM)],
        out_specs=pl.BlockSpec(memory_space=pltpu.VMEM),
        scratch_shapes=[pltpu.SemaphoreType.DMA((7,)), pltpu.SemaphoreType.DMA((7,)), pltpu.SemaphoreType.DMA],
    )(v)


def _hbm_spec():
    return pl.BlockSpec(memory_space=pltpu.HBM)


def allgather_chips(shards, name):
    n = len(shards)

    def body(*refs):
        ins, outs = refs[:n], refs[n:2 * n]
        send_sems, recv_sems, local_sems = refs[2 * n:]
        x, y, c = _place()
        chips = _other_chips(x, y)
        started = []
        for w in range(n):
            loc = pltpu.make_async_copy(ins[w], outs[w].at[2 * x + y], local_sems.at[w])
            loc.start()
            started.append(loc)
        sends = []
        for w in range(n):
            for j, (px, py) in enumerate(chips):
                cp = pltpu.make_async_remote_copy(
                    src_ref=ins[w], dst_ref=outs[w].at[2 * x + y],
                    send_sem=send_sems.at[3 * w + j], recv_sem=recv_sems.at[3 * w + j],
                    device_id=(px, py, c), device_id_type=MESH)
                cp.start()
                sends.append(cp)
        for w in range(n):
            for j, (px, py) in enumerate(chips):
                pltpu.make_async_remote_copy(
                    src_ref=ins[w], dst_ref=outs[w].at[2 * px + py],
                    send_sem=send_sems.at[3 * w + j], recv_sem=recv_sems.at[3 * w + j],
                    device_id=(px, py, c), device_id_type=MESH).wait_recv()
        for cp in sends:
            cp.wait_send()
        for loc in started:
            loc.wait()

    return _call(
        body, name=name,
        out_shape=[jax.ShapeDtypeStruct((N_CHIPS,) + s.shape, s.dtype) for s in shards],
        in_specs=[_hbm_spec()] * n, out_specs=[_hbm_spec()] * n,
        scratch_shapes=[pltpu.SemaphoreType.DMA((3 * n,)), pltpu.SemaphoreType.DMA((3 * n,)),
                        pltpu.SemaphoreType.DMA((n,))],
    )(*shards)


def exchange_chips(parts, name):
    n = len(parts)

    def body(*refs):
        ins, outs = refs[:n], refs[n:2 * n]
        send_sems, recv_sems, local_sems = refs[2 * n:]
        x, y, c = _place()
        chips = _other_chips(x, y)
        started = []
        for w in range(n):
            loc = pltpu.make_async_copy(ins[w].at[2 * x + y], outs[w].at[3], local_sems.at[w])
            loc.start()
            started.append(loc)
        sends = []
        for w in range(n):
            for j, (px, py) in enumerate(chips):
                cp = pltpu.make_async_remote_copy(
                    src_ref=ins[w].at[2 * px + py], dst_ref=outs[w].at[j],
                    send_sem=send_sems.at[3 * w + j], recv_sem=recv_sems.at[3 * w + j],
                    device_id=(px, py, c), device_id_type=MESH)
                cp.start()
                sends.append(cp)
        for cp in sends:
            cp.wait_recv()
        for cp in sends:
            cp.wait_send()
        for loc in started:
            loc.wait()

    return _call(
        body, name=name,
        out_shape=[jax.ShapeDtypeStruct(p.shape, p.dtype) for p in parts],
        in_specs=[_hbm_spec()] * n, out_specs=[_hbm_spec()] * n,
        scratch_shapes=[pltpu.SemaphoreType.DMA((3 * n,)), pltpu.SemaphoreType.DMA((3 * n,)),
                        pltpu.SemaphoreType.DMA((n,))],
    )(*parts)


def swap_sibling(arrs, name):
    n = len(arrs)

    def body(*refs):
        ins, outs = refs[:n], refs[n:2 * n]
        send_sems, recv_sems = refs[2 * n:]
        x, y, c = _place()
        cps = []
        for w in range(n):
            cp = pltpu.make_async_remote_copy(
                src_ref=ins[w], dst_ref=outs[w], send_sem=send_sems.at[w], recv_sem=recv_sems.at[w],
                device_id=(x, y, 1 - c), device_id_type=MESH)
            cp.start()
            cps.append(cp)
        for cp in cps:
            cp.wait_recv()
        for cp in cps:
            cp.wait_send()

    return _call(
        body, name=name,
        out_shape=[jax.ShapeDtypeStruct(a.shape, a.dtype) for a in arrs],
        in_specs=[_hbm_spec()] * n, out_specs=[_hbm_spec()] * n,
        scratch_shapes=[pltpu.SemaphoreType.DMA((n,)), pltpu.SemaphoreType.DMA((n,))],
    )(*arrs)


def cast_bf16(w, name):
    r, c = w.shape
    tr = min(256, r)

    def body(w_ref, o_ref):
        o_ref[...] = w_ref[...].astype(BF16)

    return _call(
        body, name=name, grid=(r // tr,),
        out_shape=jax.ShapeDtypeStruct((r, c), BF16),
        in_specs=[pl.BlockSpec((tr, c), lambda i: (i, 0))],
        out_specs=pl.BlockSpec((tr, c), lambda i: (i, 0)),
        compiler_params=_params(),
    )(w)


def sum_slots(recv, name):
    _, r, c = recv.shape
    tr = min(256, r)

    def body(r_ref, o_ref):
        acc = r_ref[3].astype(F32) + r_ref[0].astype(F32)
        acc = acc + r_ref[1].astype(F32)
        o_ref[...] = acc + r_ref[2].astype(F32)

    return _call(
        body, name=name, grid=(r // tr,),
        out_shape=jax.ShapeDtypeStruct((r, c), F32),
        in_specs=[pl.BlockSpec((N_CHIPS, tr, c), lambda i: (0, i, 0))],
        out_specs=pl.BlockSpec((tr, c), lambda i: (i, 0)),
        compiler_params=_params(),
    )(recv)


def adamw_pair(pa, pb, w, m, v, name):
    r, c = w.shape
    tr = min(128, r)

    def body(pa_ref, pb_ref, w_ref, m_ref, v_ref, g_ref, d_ref, nm_ref, nv_ref):
        g = pa_ref[...] + pb_ref[...]
        d, nm, nv = _adamw(w_ref[...], g, m_ref[...], v_ref[...])
        g_ref[...] = g
        d_ref[...] = d
        nm_ref[...] = nm
        nv_ref[...] = nv

    spec = pl.BlockSpec((tr, c), lambda i: (i, 0))
    return _call(
        body, name=name, grid=(r // tr,),
        out_shape=[jax.ShapeDtypeStruct((r, c), F32)] * 4,
        in_specs=[spec] * 5, out_specs=[spec] * 4,
        compiler_params=_params(),
    )(pa, pb, w, m, v)


def small_update(gathered, w, m, v, name):
    def body(g_ref, w_ref, m_ref, v_ref, go_ref, d_ref, nm_ref, nv_ref):
        g = g_ref[0]
        for k in range(1, N_DEV):
            g = g + g_ref[k]
        d, nm, nv = _adamw(w_ref[...], g, m_ref[...], v_ref[...])
        go_ref[...] = g
        d_ref[...] = d
        nm_ref[...] = nm
        nv_ref[...] = nv

    return _call(
        body, name=name,
        out_shape=[jax.ShapeDtypeStruct(w.shape, F32)] * 4,
        compiler_params=_params(),
    )(gathered, w, m, v)


def ada_fwd(c_all, w_ada, b_cols, name):
    n_l, d, cols = w_ada.shape
    nb = c_all.shape[0]
    tn = 256

    def body(c_ref, w_ref, b_ref, o_ref):
        cv = c_ref[...]
        ca = (cv * _sigmoid(cv)).astype(BF16)
        o_ref[...] = _dot(ca, w_ref[...].astype(BF16), NN) + b_ref[...]

    return _call(
        body, name=name, grid=(n_l, cols // tn),
        out_shape=jax.ShapeDtypeStruct((n_l, nb, cols), F32),
        in_specs=[pl.BlockSpec((nb, d), lambda l, j: (0, 0)),
                  pl.BlockSpec((None, d, tn), lambda l, j: (l, 0, j)),
                  pl.BlockSpec((None, 1, tn), lambda l, j: (l, 0, j))],
        out_specs=pl.BlockSpec((None, nb, tn), lambda l, j: (l, 0, j)),
        compiler_params=_params(),
    )(c_all, w_ada, b_cols)


def ada_bwd(c_all, dmod_cols, w, m, v, name):
    n_l, d, cols = w.shape
    nb = c_all.shape[0]
    tn = 256

    def body(c_ref, dm_ref, w_ref, m_ref, v_ref, g_ref, d_ref, nm_ref, nv_ref):
        cv = c_ref[...]
        ca = (cv * _sigmoid(cv)).astype(BF16)
        g = _dot(ca, dm_ref[...].astype(BF16), TN)
        dl, nm, nv = _adamw(w_ref[...], g, m_ref[...], v_ref[...])
        g_ref[...] = g
        d_ref[...] = dl
        nm_ref[...] = nm
        nv_ref[...] = nv

    wspec = pl.BlockSpec((None, d, tn), lambda l, j: (l, 0, j))
    return _call(
        body, name=name, grid=(n_l, cols // tn),
        out_shape=[jax.ShapeDtypeStruct((n_l, d, cols), F32)] * 4,
        in_specs=[pl.BlockSpec((nb, d), lambda l, j: (0, 0)),
                  pl.BlockSpec((None, nb, tn), lambda l, j: (l, 0, j)),
                  wspec, wspec, wspec],
        out_specs=[wspec] * 4,
        compiler_params=_params(),
    )(c_all, dmod_cols, w, m, v)


def bias_update(dmod_all, w, m, v, name):
    def body(dm_ref, w_ref, m_ref, v_ref, g_ref, d_ref, nm_ref, nv_ref):
        g = jnp.sum(dm_ref[...], axis=0, keepdims=True)
        dl, nm, nv = _adamw(w_ref[...], g, m_ref[...], v_ref[...])
        g_ref[...] = g
        d_ref[...] = dl
        nm_ref[...] = nm
        nv_ref[...] = nv

    return _call(
        body, name=name,
        out_shape=[jax.ShapeDtypeStruct(w.shape, F32)] * 4,
        compiler_params=_params(),
    )(dmod_all, w, m, v)


def inproj_fwd(x, mod, ng, wg, seq, sectioned, name):
    m_rows, d = x.shape
    nsh, _, ns = wg.shape
    n = nsh * ns
    tm, tn = min(512, seq), 512
    per = ns // tn

    def body(x_ref, mod_ref, ng_ref, w_ref, proj_ref, h_ref):
        @pl.when(pl.program_id(1) == 0)
        def _():
            xv = x_ref[...]
            r = lax.rsqrt(jnp.mean(xv * xv, axis=-1, keepdims=True) + EPS)
            md = mod_ref[0]
            h = (xv * r * ng_ref[...]) * (1.0 + md[:, d:2 * d]) + md[:, :d]
            h_ref[...] = h.astype(BF16)
        proj_ref[...] = _dot(h_ref[...], w_ref[...], NN)

    if sectioned:
        proj_shape = (nsh, m_rows, ns)
        proj_spec = pl.BlockSpec((None, tm, tn), lambda i, j: (j // per, i, j % per))
    else:
        proj_shape = (m_rows, n)
        proj_spec = pl.BlockSpec((tm, tn), lambda i, j: (i, j))
    return _call(
        body, name=name, grid=(m_rows // tm, n // tn),
        out_shape=[jax.ShapeDtypeStruct(proj_shape, F32), jax.ShapeDtypeStruct((m_rows, d), BF16)],
        in_specs=[pl.BlockSpec((tm, d), lambda i, j: (i, 0)),
                  pl.BlockSpec((1, 1, 3 * d), lambda i, j: ((i * tm) // seq, 0, 0)),
                  pl.BlockSpec((1, d), lambda i, j: (0, 0)),
                  pl.BlockSpec((None, d, tn), lambda i, j: (j // per, 0, j % per))],
        out_specs=[proj_spec, pl.BlockSpec((tm, d), lambda i, j: (i, 0))],
        compiler_params=_params(),
    )(x, mod, ng, wg)


def outproj_fwd(y, w, x, mod, seq, name):
    m_rows, di = y.shape
    d = w.shape[1]
    tm = min(512, seq)

    def body(y_ref, w_ref, x_ref, mod_ref, xn_ref, out_ref):
        acc = _dot(y_ref[...], w_ref[...], NN)
        out_ref[...] = acc
        xn_ref[...] = x_ref[...] + mod_ref[0][:, 2 * d:] * acc

    row = pl.BlockSpec((tm, d), lambda i: (i, 0))
    return _call(
        body, name=name, grid=(m_rows // tm,),
        out_shape=[jax.ShapeDtypeStruct((m_rows, d), F32)] * 2,
        in_specs=[pl.BlockSpec((tm, di), lambda i: (i, 0)),
                  pl.BlockSpec((di, d), lambda i: (0, 0)),
                  row,
                  pl.BlockSpec((1, 1, 3 * d), lambda i: ((i * tm) // seq, 0, 0))],
        out_specs=[row, row],
        compiler_params=_params(),
    )(y, w, x, mod)


def outproj_bwd(dxo, out, mod, w, seq, name):
    m_rows, d = dxo.shape
    di = w.shape[0]
    nb = m_rows // seq
    tm, tn = min(512, seq), 512

    def body(dxo_ref, out_ref, mod_ref, w_ref, dy_ref, dout_ref, dgate_ref):
        i = pl.program_id(0)

        @pl.when(pl.program_id(1) == 0)
        def _():
            dx = dxo_ref[...]
            dout_ref[...] = (mod_ref[0][:, 2 * d:] * dx).astype(BF16)
            part = jnp.sum(dx * out_ref[...], axis=0, keepdims=True)

            @pl.when((i * tm) % seq == 0)
            def _():
                dgate_ref[0] = part

            @pl.when((i * tm) % seq != 0)
            def _():
                dgate_ref[0] = dgate_ref[0] + part

        dy_ref[...] = _dot(dout_ref[...], w_ref[...], NT)

    row = pl.BlockSpec((tm, d), lambda i, j: (i, 0))
    return _call(
        body, name=name, grid=(m_rows // tm, di // tn),
        out_shape=[jax.ShapeDtypeStruct((m_rows, di), F32), jax.ShapeDtypeStruct((m_rows, d), BF16),
                   jax.ShapeDtypeStruct((nb, 1, d), F32)],
        in_specs=[row, row,
                  pl.BlockSpec((1, 1, 3 * d), lambda i, j: ((i * tm) // seq, 0, 0)),
                  pl.BlockSpec((tn, d), lambda i, j: (j, 0))],
        out_specs=[pl.BlockSpec((tm, tn), lambda i, j: (i, j)), row,
                   pl.BlockSpec((1, 1, d), lambda i, j: ((i * tm) // seq, 0, 0))],
        compiler_params=_params(),
    )(dxo, out, mod, w)


def grad_w_out(y, dout, name):
    m_rows, di = y.shape
    d = dout.shape[1]
    tm, tk = min(512, m_rows), 512
    n_m = m_rows // tm

    def body(y_ref, do_ref, o_ref, acc_ref):
        mi = pl.program_id(1)
        part = _dot(y_ref[...], do_ref[...], TN)

        @pl.when(mi == 0)
        def _():
            acc_ref[...] = part

        @pl.when(mi != 0)
        def _():
            acc_ref[...] = acc_ref[...] + part

        @pl.when(mi == n_m - 1)
        def _():
            o_ref[...] = acc_ref[...].astype(BF16)

    return _call(
        body, name=name, grid=(di // tk, n_m),
        out_shape=jax.ShapeDtypeStruct((di, d), BF16),
        in_specs=[pl.BlockSpec((tm, tk), lambda j, mi: (mi, j)),
                  pl.BlockSpec((tm, d), lambda j, mi: (mi, 0))],
        out_specs=pl.BlockSpec((tk, d), lambda j, mi: (j, 0)),
        scratch_shapes=[pltpu.VMEM((tk, d), F32)],
        compiler_params=_params(),
    )(y, dout)


def grad_w_in(h, dproj, nsh, sectioned, name):
    m_rows, d = h.shape
    n = dproj.shape[0] * dproj.shape[2] if sectioned else dproj.shape[1]
    ns = n // nsh
    tm, tn = min(512, m_rows), 512
    per = ns // tn
    n_m = m_rows // tm

    def body(h_ref, dp_ref, o_ref, acc_ref):
        mi = pl.program_id(1)
        part = _dot(h_ref[...], dp_ref[...], TN)

        @pl.when(mi == 0)
        def _():
            acc_ref[...] = part

        @pl.when(mi != 0)
        def _():
            acc_ref[...] = acc_ref[...] + part

        @pl.when(mi == n_m - 1)
        def _():
            o_ref[...] = acc_ref[...].astype(BF16)

    if sectioned:
        dp_spec = pl.BlockSpec((None, tm, tn), lambda j, mi: (j // per, mi, j % per))
    else:
        dp_spec = pl.BlockSpec((tm, tn), lambda j, mi: (mi, j))
    return _call(
        body, name=name, grid=(n // tn, n_m),
        out_shape=jax.ShapeDtypeStruct((nsh, d, ns), BF16),
        in_specs=[pl.BlockSpec((tm, d), lambda j, mi: (mi, 0)), dp_spec],
        out_specs=pl.BlockSpec((None, d, tn), lambda j, mi: (j // per, 0, j % per)),
        scratch_shapes=[pltpu.VMEM((d, tn), F32)],
        compiler_params=_params(),
    )(h, dproj)


def inproj_bwd(dproj, wg, x, dxo, mod, ng, seq, sectioned, name):
    m_rows, d = x.shape
    nsh, _, ns = wg.shape
    n = nsh * ns
    nb = m_rows // seq
    tm, tk = min(512, seq), 512
    per = ns // tk
    n_k = n // tk

    def body(dp_ref, w_ref, x_ref, dxo_ref, mod_ref, ng_ref, dxi_ref, dsh_ref, dsc_ref, dng_ref, acc_ref):
        i, k = pl.program_id(0), pl.program_id(1)
        part = _dot(dp_ref[...], w_ref[...], NT)

        @pl.when(k == 0)
        def _():
            acc_ref[...] = part

        @pl.when(k != 0)
        def _():
            acc_ref[...] = acc_ref[...] + part

        @pl.when(k == n_k - 1)
        def _():
            dh = acc_ref[...]
            xv = x_ref[...]
            r = lax.rsqrt(jnp.mean(xv * xv, axis=-1, keepdims=True) + EPS)
            xn = xv * r
            md = mod_ref[0]
            gain = ng_ref[...]
            p_shift = jnp.sum(dh, axis=0, keepdims=True)
            p_scale = jnp.sum(dh * (xn * gain), axis=0, keepdims=True)
            drn = dh * (1.0 + md[:, d:2 * d])
            p_ng = jnp.sum(drn * xn, axis=0, keepdims=True)
            dxn = drn * gain
            dx = r * (dxn - xn * jnp.mean(dxn * xn, axis=-1, keepdims=True))
            dxi_ref[...] = dxo_ref[...] + dx

            @pl.when((i * tm) % seq == 0)
            def _():
                dsh_ref[0] = p_shift
                dsc_ref[0] = p_scale

            @pl.when((i * tm) % seq != 0)
            def _():
                dsh_ref[0] = dsh_ref[0] + p_shift
                dsc_ref[0] = dsc_ref[0] + p_scale

            @pl.when(i == 0)
            def _():
                dng_ref[...] = p_ng

            @pl.when(i != 0)
            def _():
                dng_ref[...] = dng_ref[...] + p_ng

    if sectioned:
        dp_spec = pl.BlockSpec((None, tm, tk), lambda i, k: (k // per, i, k % per))
    else:
        dp_spec = pl.BlockSpec((tm, tk), lambda i, k: (i, k))
    row = pl.BlockSpec((tm, d), lambda i, k: (i, 0))
    per_seq = pl.BlockSpec((1, 1, d), lambda i, k: ((i * tm) // seq, 0, 0))
    return _call(
        body, name=name, grid=(m_rows // tm, n_k),
        out_shape=[jax.ShapeDtypeStruct((m_rows, d), F32), jax.ShapeDtypeStruct((nb, 1, d), F32),
                   jax.ShapeDtypeStruct((nb, 1, d), F32), jax.ShapeDtypeStruct((1, d), F32)],
        in_specs=[dp_spec,
                  pl.BlockSpec((None, d, tk), lambda i, k: (k // per, 0, k % per)),
                  row, row,
                  pl.BlockSpec((1, 1, 3 * d), lambda i, k: ((i * tm) // seq, 0, 0)),
                  pl.BlockSpec((1, d), lambda i, k: (0, 0))],
        out_specs=[row, per_seq, per_seq, pl.BlockSpec((1, d), lambda i, k: (0, 0))],
        scratch_shapes=[pltpu.VMEM((tm, d), F32)],
        compiler_params=_params(),
    )(dproj, wg, x, dxo, mod, ng)


def _sgu_stats(proj_ref, vg_ref, di, gd):
    s1 = jnp.zeros((SG_BLOCK, 1), F32)
    for g in range(SG_GROUPS):
        vg = _gelu(proj_ref[:, di + g * gd:di + (g + 1) * gd])
        vg_ref[:, g * gd:(g + 1) * gd] = vg
        s1 = s1 + jnp.sum(vg, axis=1, keepdims=True)
    mu = s1 / di
    s2 = jnp.zeros((SG_BLOCK, 1), F32)
    for g in range(SG_GROUPS):
        dv = vg_ref[:, g * gd:(g + 1) * gd] - mu
        s2 = s2 + jnp.sum(dv * dv, axis=1, keepdims=True)
    return mu, lax.rsqrt(s2 / di + EPS)


def sgu_fwd(proj, ln_gain, ln_bias, ws, bs, name):
    m_rows, n3 = proj.shape
    di = n3 // 3
    gd = di // SG_GROUPS

    def body(proj_ref, lg_ref, lb_ref, ws_ref, bs_ref, y_ref, wsm_ref, vg_ref):
        @pl.when(pl.program_id(0) == 0)
        def _():
            mask = _chunk_mask()
            for g in range(SG_GROUPS):
                wsm_ref[g] = jnp.where(mask, ws_ref[g], 0.0).astype(BF16)

        mu, rstd = _sgu_stats(proj_ref, vg_ref, di, gd)
        for g in range(SG_GROUPS):
            cs = slice(g * gd, (g + 1) * gd)
            vln = (vg_ref[:, cs] - mu) * rstd * lg_ref[:, cs] + lb_ref[:, cs]
            s = _dot(wsm_ref[g], vln.astype(BF16), NN) + bs_ref[g]
            u = _gelu(proj_ref[:, cs])
            gp = proj_ref[:, 2 * di + g * gd:2 * di + (g + 1) * gd]
            y_ref[:, cs] = (u * s * (gp * _sigmoid(gp))).astype(BF16)

    full = lambda shape: pl.BlockSpec(shape, lambda i: (0,) * len(shape))
    return _call(
        body, name=name, grid=(m_rows // SG_BLOCK,),
        out_shape=jax.ShapeDtypeStruct((m_rows, di), BF16),
        in_specs=[pl.BlockSpec((SG_BLOCK, n3), lambda i: (i, 0)),
                  full((1, di)), full((1, di)),
                  full((SG_GROUPS, SG_BLOCK, SG_BLOCK)), full((SG_GROUPS, SG_BLOCK, 1))],
        out_specs=pl.BlockSpec((SG_BLOCK, di), lambda i: (i, 0)),
        scratch_shapes=[pltpu.VMEM((SG_GROUPS, SG_BLOCK, SG_BLOCK), BF16), pltpu.VMEM((SG_BLOCK, di), F32)],
        compiler_params=_params(),
    )(proj, ln_gain, ln_bias, ws, bs)


def sgu_bwd(proj, dy, ln_gain, ln_bias, ws, bs, name):
    m_rows, n3 = proj.shape
    di = n3 // 3
    gd = di // SG_GROUPS
    n_i = m_rows // SG_BLOCK

    def body(proj_ref, dy_ref, lg_ref, lb_ref, ws_ref, bs_ref,
             dp_ref, dws_ref, dbs_ref, dlg_ref, dlb_ref, wsm_ref, vg_ref, dvh_ref):
        i = pl.program_id(0)

        @pl.when(i == 0)
        def _():
            mask = _chunk_mask()
            for g in range(SG_GROUPS):
                wsm_ref[g] = jnp.where(mask, ws_ref[g], 0.0).astype(BF16)
            dws_ref[...] = jnp.zeros_like(dws_ref)
            dbs_ref[...] = jnp.zeros_like(dbs_ref)
            dlg_ref[...] = jnp.zeros_like(dlg_ref)
            dlb_ref[...] = jnp.zeros_like(dlb_ref)

        mu, rstd = _sgu_stats(proj_ref, vg_ref, di, gd)
        m1 = jnp.zeros((SG_BLOCK, 1), F32)
        m2 = jnp.zeros((SG_BLOCK, 1), F32)
        for g in range(SG_GROUPS):
            cs = slice(g * gd, (g + 1) * gd)
            gs = slice(2 * di + g * gd, 2 * di + (g + 1) * gd)
            gain = lg_ref[:, cs]
            vhat = (vg_ref[:, cs] - mu) * rstd
            vln_b = (vhat * gain + lb_ref[:, cs]).astype(BF16)
            s = _dot(wsm_ref[g], vln_b, NN) + bs_ref[g]
            u, du = _gelu_and_grad(proj_ref[:, cs])
            sg, dsg = _silu_and_grad(proj_ref[:, gs])
            dyv = dy_ref[:, cs]
            dp_ref[:, cs] = (dyv * s * sg * du).astype(BF16)
            dp_ref[:, gs] = (dyv * u * s * dsg).astype(BF16)
            ds = dyv * u * sg
            ds_b = ds.astype(BF16)
            dws_ref[g] = dws_ref[g] + _dot(ds_b, vln_b, NT)
            dbs_ref[g] = dbs_ref[g] + jnp.sum(ds, axis=1, keepdims=True)
            dvln = _dot(wsm_ref[g], ds_b, TN)
            dlg_ref[:, cs] = dlg_ref[:, cs] + jnp.sum(dvln * vhat, axis=0, keepdims=True)
            dlb_ref[:, cs] = dlb_ref[:, cs] + jnp.sum(dvln, axis=0, keepdims=True)
            dvh = dvln * gain
            dvh_ref[:, cs] = dvh
            m1 = m1 + jnp.sum(dvh, axis=1, keepdims=True)
            m2 = m2 + jnp.sum(dvh * vhat, axis=1, keepdims=True)
        m1 = m1 / di
        m2 = m2 / di
        for g in range(SG_GROUPS):
            cs = slice(g * gd, (g + 1) * gd)
            vs = slice(di + g * gd, di + (g + 1) * gd)
            vhat = (vg_ref[:, cs] - mu) * rstd
            dvg = rstd * (dvh_ref[:, cs] - m1 - vhat * m2)
            _, dgel = _gelu_and_grad(proj_ref[:, vs])
            dp_ref[:, vs] = (dvg * dgel).astype(BF16)

        @pl.when(i == n_i - 1)
        def _():
            mask = _chunk_mask()
            for g in range(SG_GROUPS):
                dws_ref[g] = jnp.where(mask, dws_ref[g], 0.0)

    full = lambda shape: pl.BlockSpec(shape, lambda i: (0,) * len(shape))
    return _call(
        body, name=name, grid=(n_i,),
        out_shape=[jax.ShapeDtypeStruct((m_rows, n3), BF16),
                   jax.ShapeDtypeStruct((SG_GROUPS, SG_BLOCK, SG_BLOCK), F32),
                   jax.ShapeDtypeStruct((SG_GROUPS, SG_BLOCK, 1), F32),
                   jax.ShapeDtypeStruct((1, di), F32), jax.ShapeDtypeStruct((1, di), F32)],
        in_specs=[pl.BlockSpec((SG_BLOCK, n3), lambda i: (i, 0)),
                  pl.BlockSpec((SG_BLOCK, di), lambda i: (i, 0)),
                  full((1, di)), full((1, di)),
                  full((SG_GROUPS, SG_BLOCK, SG_BLOCK)), full((SG_GROUPS, SG_BLOCK, 1))],
        out_specs=[pl.BlockSpec((SG_BLOCK, n3), lambda i: (i, 0)),
                   full((SG_GROUPS, SG_BLOCK, SG_BLOCK)), full((SG_GROUPS, SG_BLOCK, 1)),
                   full((1, di)), full((1, di))],
        scratch_shapes=[pltpu.VMEM((SG_GROUPS, SG_BLOCK, SG_BLOCK), BF16),
                        pltpu.VMEM((SG_BLOCK, di), F32), pltpu.VMEM((SG_BLOCK, di), F32)],
        compiler_params=_params(),
    )(proj, dy, ln_gain, ln_bias, ws, bs)


def _lower_bound(lbraw):
    mx = jnp.maximum(lbraw[0:1, :], lbraw[1:2, :])
    e0 = jnp.exp(lbraw[0:1, :] - mx)
    e1 = jnp.exp(lbraw[1:2, :] - mx)
    p0 = e0 / (e0 + e1)
    p1 = e1 / (e0 + e1)
    return (p0 + p1) - p0, p0, p1


def _tri(lower):
    r = lax.broadcasted_iota(jnp.int32, (CHUNK, CHUNK), 0)
    c = lax.broadcasted_iota(jnp.int32, (CHUNK, CHUNK), 1)
    return ((r >= c) if lower else (c >= r)).astype(F32)


def _row(a, idx):
    r = lax.broadcasted_iota(jnp.int32, a.shape, 0)
    return jnp.sum(jnp.where(r == idx, a, 0.0), axis=0, keepdims=True)


def _hgrn_gates(qp, fp, lb, tri):
    sgm = _sigmoid(fp)
    f = lb + (1.0 - lb) * sgm
    k = 1.0 - f
    a = _dot(tri, jnp.log(f), NN, precision=lax.Precision.HIGHEST)
    a_mid = _row(a, CHUNK // 2 - 1)
    a_last = _row(a, CHUNK - 1)
    q, dq = _silu_and_grad(qp)
    e1, e2, e3, e4 = jnp.exp(a - a_mid), jnp.exp(a_mid - a), jnp.exp(a), jnp.exp(a_last - a)
    return dict(sgm=sgm, f=f, k=k, q=q, dq=dq, e1=e1, e2=e2, e3=e3, e4=e4, dec=jnp.exp(a_last),
                q_in=q * e1, k_in=k * e2, q_out=q * e3, k_out=k * e4)


def _causal():
    r = lax.broadcasted_iota(jnp.int32, (CHUNK, CHUNK), 0)
    c = lax.broadcasted_iota(jnp.int32, (CHUNK, CHUNK), 1)
    return r >= c


def hgrn_fwd(proj4, lbraw, gn, seq, name):
    _, m_rows, di = proj4.shape
    nb, nh, nc = m_rows // seq, di // HEAD_DIM, seq // CHUNK
    rows = min(HG_ROWS, seq)
    wide = HG_WIDE * HEAD_DIM
    ns, cpb = seq // rows, rows // CHUNK

    def body(p_ref, lb_ref, gn_ref, y_ref, sts_ref, st_ref):
        @pl.when(pl.program_id(2) == 0)
        def _():
            st_ref[...] = jnp.zeros_like(st_ref)

        tri = _tri(True)
        causal = _causal()
        gain = gn_ref[...]
        lbs = [_lower_bound(lb_ref[:, j * HEAD_DIM:(j + 1) * HEAD_DIM])[0] for j in range(HG_WIDE)]

        units = [(n, j) for n in range(cpb) for j in range(HG_WIDE)]
        rs = lambda n: slice(n * CHUNK, (n + 1) * CHUNK)
        cs = lambda j: slice(j * HEAD_DIM, (j + 1) * HEAD_DIM)
        gates, v_b, sc_b, kv, o_in, o_x = {}, {}, {}, {}, {}, {}
        for n, j in units:
            gates[n, j] = _hgrn_gates(p_ref[0, rs(n), cs(j)], p_ref[1, rs(n), cs(j)], lbs[j], tri)
            v_b[n, j] = p_ref[2, rs(n), cs(j)].astype(BF16)
        for u in units:
            t = gates[u]
            sc_b[u] = jnp.where(causal, _dot(t["q_in"].astype(BF16), t["k_in"].astype(BF16), NT), 0.0).astype(BF16)
            kv[u] = _dot(v_b[u], t["k_out"].astype(BF16), TN)
        for u in units:
            o_in[u] = _dot(sc_b[u], v_b[u], NN)
        for j in range(HG_WIDE):
            st = st_ref[j]
            for n in range(cpb):
                sts_ref[n, :, cs(j)] = st
                o_x[n, j] = _dot(gates[n, j]["q_out"].astype(BF16), st.astype(BF16), NT)
                st = st * gates[n, j]["dec"] + kv[n, j]
            st_ref[j] = st
        for n, j in units:
            o = o_in[n, j] + o_x[n, j]
            r = lax.rsqrt(jnp.mean(o * o, axis=-1, keepdims=True) + EPS)
            gp = p_ref[3, rs(n), cs(j)]
            y_ref[rs(n), cs(j)] = ((o * r * gain) * (gp * _sigmoid(gp))).astype(BF16)

    return _call(
        body, name=name, grid=(nh // HG_WIDE, nb, ns),
        out_shape=[jax.ShapeDtypeStruct((m_rows, di), BF16),
                   jax.ShapeDtypeStruct((nb * nc, HEAD_DIM, di), F32)],
        in_specs=[pl.BlockSpec((4, rows, wide), lambda hg, b, s: (0, b * ns + s, hg)),
                  pl.BlockSpec((2, wide), lambda hg, b, s: (0, hg)),
                  pl.BlockSpec((1, HEAD_DIM), lambda hg, b, s: (0, 0))],
        out_specs=[pl.BlockSpec((rows, wide), lambda hg, b, s: (b * ns + s, hg)),
                   pl.BlockSpec((cpb, HEAD_DIM, wide), lambda hg, b, s: (b * ns + s, 0, hg))],
        scratch_shapes=[pltpu.VMEM((HG_WIDE, HEAD_DIM, HEAD_DIM), F32)],
        compiler_params=_params(),
    )(proj4, lbraw, gn)


def hgrn_bwd(proj4, dy, sts, lbraw, gn, seq, name):
    _, m_rows, di = proj4.shape
    nb, nh, nc = m_rows // seq, di // HEAD_DIM, seq // CHUNK
    rows = min(HG_ROWS, seq)
    wide = HG_WIDE * HEAD_DIM
    ns, cpb = seq // rows, rows // CHUNK
    n_hg = nh // HG_WIDE

    def body(p_ref, dy_ref, sts_ref, lb_ref, gn_ref, dp_ref, dlb_ref, dgn_ref, dst_ref, lbacc_ref, gnacc_ref):
        hg, b, s = pl.program_id(0), pl.program_id(1), pl.program_id(2)
        tri, triu = _tri(True), _tri(False)
        causal = _causal()
        gain = gn_ref[...]
        first = (b == 0) & (s == 0)

        @pl.when((hg == 0) & first)
        def _():
            gnacc_ref[...] = jnp.zeros_like(gnacc_ref)

        @pl.when(first)
        def _():
            lbacc_ref[...] = jnp.zeros_like(lbacc_ref)

        @pl.when(s == 0)
        def _():
            dst_ref[...] = jnp.zeros_like(dst_ref)

        units = [(n, j) for n in range(cpb) for j in range(HG_WIDE)]
        rs = lambda n: slice(n * CHUNK, (n + 1) * CHUNK)
        cs = lambda j: slice(j * HEAD_DIM, (j + 1) * HEAD_DIM)
        lbs = [_lower_bound(lb_ref[:, cs(j)])[0] for j in range(HG_WIDE)]
        gates, v_b, st_b, sc_b, o, do_b = {}, {}, {}, {}, {}, {}
        dq_out, dsc_b, dv, g_st, dq_in, dk_in, dst_at, dk_out, ddec = {}, {}, {}, {}, {}, {}, {}, {}, {}
        for n, j in units:
            gates[n, j] = _hgrn_gates(p_ref[0, rs(n), cs(j)], p_ref[1, rs(n), cs(j)], lbs[j], tri)
            v_b[n, j] = p_ref[2, rs(n), cs(j)].astype(BF16)
            st_b[n, j] = sts_ref[n, :, cs(j)].astype(BF16)
        for u in units:
            t = gates[u]
            sc_b[u] = jnp.where(causal, _dot(t["q_in"].astype(BF16), t["k_in"].astype(BF16), NT), 0.0).astype(BF16)
        for u in units:
            o[u] = _dot(sc_b[u], v_b[u], NN) + _dot(gates[u]["q_out"].astype(BF16), st_b[u], NT)
        for n, j in units:
            ov = o[n, j]
            r = lax.rsqrt(jnp.mean(ov * ov, axis=-1, keepdims=True) + EPS)
            ohat = ov * r
            sg, dsg = _silu_and_grad(p_ref[3, rs(n), cs(j)])
            dyv = dy_ref[rs(n), cs(j)]
            dp_ref[3, rs(n), cs(j)] = (dyv * (ohat * gain) * dsg).astype(BF16)
            d_on = dyv * sg
            gnacc_ref[:, cs(j)] = gnacc_ref[:, cs(j)] + jnp.sum(d_on * ohat, axis=0, keepdims=True)
            dohat = d_on * gain
            do_b[n, j] = (r * (dohat - ohat * jnp.mean(dohat * ohat, axis=-1, keepdims=True))).astype(BF16)
        for u in units:
            dq_out[u] = _dot(do_b[u], st_b[u], NN)
            dsc_b[u] = jnp.where(causal, _dot(do_b[u], v_b[u], NT), 0.0).astype(BF16)
            dv[u] = _dot(sc_b[u], do_b[u], TN)
            g_st[u] = _dot(do_b[u], gates[u]["q_out"].astype(BF16), TN)
        for u in units:
            dq_in[u] = _dot(dsc_b[u], gates[u]["k_in"].astype(BF16), NN)
            dk_in[u] = _dot(dsc_b[u], gates[u]["q_in"].astype(BF16), TN)
        for j in range(HG_WIDE):
            dst = dst_ref[j]
            for n in reversed(range(cpb)):
                dst_at[n, j] = dst
                dst = dst * gates[n, j]["dec"] + g_st[n, j]
            dst_ref[j] = dst
        for n, j in units:
            dst = dst_at[n, j]
            dst_b = dst.astype(BF16)
            dk_out[n, j] = _dot(v_b[n, j], dst_b, NN)
            dv[n, j] = dv[n, j] + _dot(gates[n, j]["k_out"].astype(BF16), dst_b, NT)
            ddec[n, j] = jnp.sum(dst * sts_ref[n, :, cs(j)], axis=0, keepdims=True)
        for n, j in units:
            t = gates[n, j]
            dp_ref[2, rs(n), cs(j)] = dv[n, j].astype(BF16)
            dq = dq_in[n, j] * t["e1"] + dq_out[n, j] * t["e3"]
            dk = dk_in[n, j] * t["e2"] + dk_out[n, j] * t["e4"]
            w_in = dq_in[n, j] * t["q_in"] - dk_in[n, j] * t["k_in"]
            w_out = dk_out[n, j] * t["k_out"]
            da = w_in + dq_out[n, j] * t["q_out"] - w_out
            da_mid = -jnp.sum(w_in, axis=0, keepdims=True)
            da_last = jnp.sum(w_out, axis=0, keepdims=True) + ddec[n, j] * t["dec"]
            rid = lax.broadcasted_iota(jnp.int32, da.shape, 0)
            da = da + jnp.where(rid == CHUNK // 2 - 1, da_mid, 0.0) + jnp.where(rid == CHUNK - 1, da_last, 0.0)
            dlf = _dot(triu, da, NN, precision=lax.Precision.HIGHEST)
            df = dlf / t["f"] - dk
            sgm = t["sgm"]
            dp_ref[1, rs(n), cs(j)] = (df * (1.0 - lbs[j]) * sgm * (1.0 - sgm)).astype(BF16)
            lbacc_ref[:, cs(j)] = lbacc_ref[:, cs(j)] + jnp.sum(df * (1.0 - sgm), axis=0, keepdims=True)
            dp_ref[0, rs(n), cs(j)] = (dq * t["dq"]).astype(BF16)

        @pl.when((b == nb - 1) & (s == ns - 1))
        def _():
            for j in range(HG_WIDE):
                cs = slice(j * HEAD_DIM, (j + 1) * HEAD_DIM)
                _, p0, p1 = _lower_bound(lb_ref[:, cs])
                acc = lbacc_ref[:, cs]
                dlb_ref[0:1, cs] = -acc * p0 * p1
                dlb_ref[1:2, cs] = acc * p1 * (1.0 - p1)

        @pl.when((hg == n_hg - 1) & (b == nb - 1) & (s == ns - 1))
        def _():
            tot = gnacc_ref[:, 0:HEAD_DIM]
            for j in range(1, HG_WIDE):
                tot = tot + gnacc_ref[:, j * HEAD_DIM:(j + 1) * HEAD_DIM]
            dgn_ref[...] = tot

    blk = lambda hg, b, s: b * ns + (ns - 1 - s)
    return _call(
        body, name=name, grid=(n_hg, nb, ns),
        out_shape=[jax.ShapeDtypeStruct((4, m_rows, di), BF16), jax.ShapeDtypeStruct((2, di), F32),
                   jax.ShapeDtypeStruct((1, HEAD_DIM), F32)],
        in_specs=[pl.BlockSpec((4, rows, wide), lambda hg, b, s: (0, blk(hg, b, s), hg)),
                  pl.BlockSpec((rows, wide), lambda hg, b, s: (blk(hg, b, s), hg)),
                  pl.BlockSpec((cpb, HEAD_DIM, wide), lambda hg, b, s: (blk(hg, b, s), 0, hg)),
                  pl.BlockSpec((2, wide), lambda hg, b, s: (0, hg)),
                  pl.BlockSpec((1, HEAD_DIM), lambda hg, b, s: (0, 0))],
        out_specs=[pl.BlockSpec((4, rows, wide), lambda hg, b, s: (0, blk(hg, b, s), hg)),
                   pl.BlockSpec((2, wide), lambda hg, b, s: (0, hg)),
                   pl.BlockSpec((1, HEAD_DIM), lambda hg, b, s: (0, 0))],
        scratch_shapes=[pltpu.VMEM((HG_WIDE, HEAD_DIM, HEAD_DIM), F32), pltpu.VMEM((1, wide), F32),
                        pltpu.VMEM((1, wide), F32)],
        compiler_params=_params(),
    )(proj4, dy, sts, lbraw, gn)


def final_loss(x, fg, target, name):
    m_rows, d = x.shape
    tm = min(512, m_rows)

    def body(x_ref, fg_ref, t_ref, loss_ref, dx_ref, dfg_ref):
        i = pl.program_id(0)
        xv = x_ref[...]
        gain = fg_ref[...]
        r = lax.rsqrt(jnp.mean(xv * xv, axis=-1, keepdims=True) + EPS)
        xn = xv * r
        e = xn * gain - t_ref[...]
        part = 0.5 * jnp.sum(jnp.mean(e * e, axis=-1, keepdims=True), axis=0, keepdims=True)
        dyv = e / d
        p_fg = jnp.sum(dyv * xn, axis=0, keepdims=True)
        dxn = dyv * gain
        dx_ref[...] = r * (dxn - xn * jnp.mean(dxn * xn, axis=-1, keepdims=True))

        @pl.when(i == 0)
        def _():
            loss_ref[...] = part
            dfg_ref[...] = p_fg

        @pl.when(i != 0)
        def _():
            loss_ref[...] = loss_ref[...] + part
            dfg_ref[...] = dfg_ref[...] + p_fg

    row = pl.BlockSpec((tm, d), lambda i: (i, 0))
    return _call(
        body, name=name, grid=(m_rows // tm,),
        out_shape=[jax.ShapeDtypeStruct((1, 1), F32), jax.ShapeDtypeStruct((m_rows, d), F32),
                   jax.ShapeDtypeStruct((1, d), F32)],
        in_specs=[row, pl.BlockSpec((1, d), lambda i: (0, 0)), row],
        out_specs=[pl.BlockSpec((1, 1), lambda i: (0, 0)), row, pl.BlockSpec((1, d), lambda i: (0, 0))],
        compiler_params=_params(),
    )(x, fg, target)


def _pack(parts):
    flat = jnp.concatenate([p.reshape(-1) for p in parts])
    pad = (-flat.shape[0]) % (8 * LANES)
    return jnp.pad(flat, (0, pad)).reshape(-1, LANES)


def _unpack(packed, like):
    flat = packed.reshape(-1)
    out, off = [], 0
    for a in like:
        out.append(flat[off:off + a.size].reshape(a.shape))
        off += a.size
    return out


def kernel(x, c, norm_gain, w_ada, b_ada, a_w_in, a_ln_gain, a_ln_bias, a_w_s, a_b_s, a_w_out, b_w_in, b_lower_bounds, b_gn_gain, b_w_out, final_gain, loss_target, m_norm_gain, m_w_ada, m_b_ada, m_a_w_in, m_a_ln_gain, m_a_ln_bias, m_a_w_s, m_a_b_s, m_a_w_out, m_b_w_in, m_b_lower_bounds, m_b_gn_gain, m_b_w_out, m_final_gain, v_norm_gain, v_w_ada, v_b_ada, v_a_w_in, v_a_ln_gain, v_a_ln_bias, v_a_w_s, v_a_b_s, v_a_w_out, v_b_w_in, v_b_lower_bounds, v_b_gn_gain, v_b_w_out, v_final_gain):
    nb, seq, d = x.shape
    m_rows = nb * seq
    n_l = w_ada.shape[0]
    ada_cols = w_ada.shape[2]
    px, py, pc = _place()
    chip = 2 * px + py
    dev = 2 * chip + pc

    c_all = allgather_small(c.reshape(-1, LANES), "gather_c").reshape(N_DEV * nb, d)
    b_cols = lax.dynamic_slice_in_dim(b_ada, chip * ada_cols, ada_cols, axis=1).reshape(n_l, 1, ada_cols)
    mod_cols = ada_fwd(c_all, w_ada, b_cols, "ada_fwd")
    mod_g = allgather_small(mod_cols.reshape(-1, LANES), "gather_mod")
    mod_g = mod_g.reshape(N_CHIPS, 2, n_l, N_DEV * nb, ada_cols)[:, 0]
    mod_all = jnp.transpose(mod_g, (1, 2, 0, 3)).reshape(n_l, N_DEV * nb, 3 * d)
    mod_mine = lax.dynamic_slice_in_dim(mod_all, dev * nb, nb, axis=1)
    mod0 = mod_mine[0].reshape(nb, 1, 3 * d)
    mod1 = mod_mine[1].reshape(nb, 1, 3 * d)

    shards = [cast_bf16(a_w_in[0], "cast_a_in"), cast_bf16(a_w_out[0], "cast_a_out"),
              cast_bf16(b_w_in[0], "cast_b_in"), cast_bf16(b_w_out[0], "cast_b_out")]
    wa_in, wa_out, wb_in, wb_out = allgather_chips(shards, "gather_weights")
    di = wa_out.shape[1] * N_CHIPS
    wa_out = wa_out.reshape(di, d)
    wb_out = wb_out.reshape(di, d)

    x0 = x.reshape(m_rows, d)
    tgt = loss_target.reshape(m_rows, d)
    ng0, ng1 = norm_gain[0:1], norm_gain[1:2]
    bs_col = a_b_s[0].reshape(SG_GROUPS, SG_BLOCK, 1)
    proj_a, h_a = inproj_fwd(x0, mod0, ng0, wa_in, seq, False, "a_inproj")
    y_a = sgu_fwd(proj_a, a_ln_gain, a_ln_bias, a_w_s[0], bs_col, "a_sgu")
    x1, out_a = outproj_fwd(y_a, wa_out, x0, mod0, seq, "a_outproj")
    proj_b, h_b = inproj_fwd(x1, mod1, ng1, wb_in, seq, True, "b_inproj")
    y_b, sts_b = hgrn_fwd(proj_b, b_lower_bounds, b_gn_gain, seq, "b_hgrn")
    x2, out_b = outproj_fwd(y_b, wb_out, x1, mod1, seq, "b_outproj")
    loss_part, dx2, dfg = final_loss(x2, final_gain.reshape(1, d), tgt, "loss_head")
    loss = lax.psum(loss_part[0, 0], ("x", "y", "c"))

    dy_b, dout_b, dgate1 = outproj_bwd(dx2, out_b, mod1, wb_out, seq, "b_outproj_bwd")
    gwb_out = grad_w_out(y_b, dout_b, "b_grad_w_out")
    dproj_b, dlb, dgn = hgrn_bwd(proj_b, dy_b, sts_b, b_lower_bounds, b_gn_gain, seq, "b_hgrn_bwd")
    gwb_in = grad_w_in(h_b, dproj_b, N_CHIPS, True, "b_grad_w_in")
    dx1, dshift1, dscale1, dng1 = inproj_bwd(dproj_b, wb_in, x1, dx2, mod1, ng1, seq, True, "b_inproj_bwd")

    dy_a, dout_a, dgate0 = outproj_bwd(dx1, out_a, mod0, wa_out, seq, "a_outproj_bwd")
    gwa_out = grad_w_out(y_a, dout_a, "a_grad_w_out")
    dproj_a, dws, dbs, dlg, dlbias = sgu_bwd(proj_a, dy_a, a_ln_gain, a_ln_bias, a_w_s[0], bs_col, "a_sgu_bwd")
    gwa_in = grad_w_in(h_a, dproj_a, N_CHIPS, False, "a_grad_w_in")
    dx0, dshift0, dscale0, dng0 = inproj_bwd(dproj_a, wa_in, x0, dx1, mod0, ng0, seq, False, "a_inproj_bwd")
    grad_x = dx0.reshape(nb, seq, d)

    parts = [gwa_in, gwa_out.reshape(N_CHIPS, di // N_CHIPS, d), gwb_in, gwb_out.reshape(N_CHIPS, di // N_CHIPS, d)]
    recv = exchange_chips(parts, "exchange_grads")
    sums = [sum_slots(r, "sum_" + nm) for r, nm in zip(recv, ("a_in", "a_out", "b_in", "b_out"))]
    theirs = swap_sibling(sums, "swap_sums")
    big = []
    for mine, other, w, m, v, nm in zip(
            sums, theirs, (a_w_in, a_w_out, b_w_in, b_w_out), (m_a_w_in, m_a_w_out, m_b_w_in, m_b_w_out),
            (v_a_w_in, v_a_w_out, v_b_w_in, v_b_w_out), ("a_in", "a_out", "b_in", "b_out")):
        res = adamw_pair(mine, other, w[0], m[0], v[0], "adamw_" + nm)
        big.append([r.reshape(w.shape) for r in res])
    (ga_in, da_in, ma_in, va_in), (ga_out, da_out, ma_out, va_out), \
        (gb_in, db_in, mb_in, vb_in), (gb_out, db_out, mb_out, vb_out) = big

    dmod = jnp.concatenate([dshift0, dscale0, dgate0, dshift1, dscale1, dgate1], axis=2)
    dmod_all = allgather_small(dmod.reshape(-1, LANES), "gather_dmod").reshape(N_DEV * nb, n_l, 3 * d)
    dmod_cols = lax.dynamic_slice_in_dim(dmod_all, chip * ada_cols, ada_cols, axis=2)
    dmod_cols = jnp.transpose(dmod_cols, (1, 0, 2))
    g_wada, d_wada, m_wada, v_wada = ada_bwd(c_all, dmod_cols, w_ada, m_w_ada, v_w_ada, "ada_bwd")
    flat = lambda a: a.reshape(1, -1)
    g_bada, d_bada, m_bada, v_bada = [
        r.reshape(b_ada.shape) for r in
        bias_update(dmod_all.reshape(N_DEV * nb, n_l * 3 * d), flat(b_ada), flat(m_b_ada), flat(v_b_ada), "bias_update")]

    small_w = [norm_gain, a_ln_gain, a_ln_bias, a_w_s, a_b_s, b_lower_bounds, b_gn_gain, final_gain]
    small_m = [m_norm_gain, m_a_ln_gain, m_a_ln_bias, m_a_w_s, m_a_b_s, m_b_lower_bounds, m_b_gn_gain, m_final_gain]
    small_v = [v_norm_gain, v_a_ln_gain, v_a_ln_bias, v_a_w_s, v_a_b_s, v_b_lower_bounds, v_b_gn_gain, v_final_gain]
    small_g = [jnp.concatenate([dng0, dng1], axis=0), dlg, dlbias, dws, dbs, dlb, dgn, dfg]
    packed_g = _pack(small_g)
    rows = packed_g.shape[0]
    gathered = allgather_small(packed_g, "gather_small").reshape(N_DEV, rows, LANES)
    res = small_update(gathered, _pack(small_w), _pack(small_m), _pack(small_v), "small_update")
    sg, sd, sm, sv = [_unpack(r, small_w) for r in res]

    def order(ng, wada, bada, ain, sm_rest, aout, bin_, bout):
        lg, lbi, ws_, bs_, lbd, gn_, fg_ = sm_rest
        return [ng, wada, bada, ain, lg, lbi, ws_, bs_, aout, bin_, lbd, gn_, bout, fg_]

    grads = order(sg[0], g_wada, g_bada, ga_in, sg[1:], ga_out, gb_in, gb_out)
    deltas = order(sd[0], d_wada, d_bada, da_in, sd[1:], da_out, db_in, db_out)
    new_m = order(sm[0], m_wada, m_bada, ma_in, sm[1:], ma_out, mb_in, mb_out)
    new_v = order(sv[0], v_wada, v_bada, va_in, sv[1:], va_out, vb_in, vb_out)
    return (loss, grad_x, *grads, *deltas, *new_m, *new_v)
```

```python
import functools

import jax
import jax.numpy as jnp
from jax import lax
from jax.experimental import pallas as pl
from jax.experimental.pallas import tpu as pltpu

F32 = jnp.float32
BF16 = jnp.bfloat16
EPS = 1e-6
CHUNK = 64
SG_BLOCK = 128
SG_GROUPS = 8
HEAD_DIM = 128
HG_WIDE = 8
HG_ROWS = 128
N_CHIPS = 4
N_DEV = 8
LANES = 128
ADAM_LR = 0.001
ADAM_B1 = 0.9
ADAM_B2 = 0.999
ADAM_EPS = 1e-08
ADAM_WD = 0.01
ADAM_STEP = 10
GELU_C0 = 0.7978845608028654
GELU_C1 = 0.044715
MESH = pl.DeviceIdType.MESH
VMEM_LIMIT = 56 * 1024 * 1024


def _call(body, **kw):
    return pl.pallas_call(body, **kw)


def _params(**kw):
    return pltpu.CompilerParams(vmem_limit_bytes=VMEM_LIMIT, **kw)


def _sigmoid(x):
    return 1.0 / (1.0 + jnp.exp(-x))


def _silu_and_grad(x):
    s = _sigmoid(x)
    return x * s, s * (1.0 + x * (1.0 - s))


def _gelu(x):
    return 0.5 * x * (1.0 + jnp.tanh(GELU_C0 * (x + GELU_C1 * x * x * x)))


def _gelu_and_grad(x):
    t = jnp.tanh(GELU_C0 * (x + GELU_C1 * x * x * x))
    g = 0.5 * x * (1.0 + t)
    dg = 0.5 * (1.0 + t) + 0.5 * x * (1.0 - t * t) * (GELU_C0 * (1.0 + 3.0 * GELU_C1 * x * x))
    return g, dg


def _dot(a, b, dims, precision=None):
    return lax.dot_general(a, b, (dims, ((), ())), precision=precision, preferred_element_type=F32)


NN = ((1,), (0,))
NT = ((1,), (1,))
TN = ((0,), (0,))


def _adamw(w, g, m, v):
    m = ADAM_B1 * m + (1.0 - ADAM_B1) * g
    v = ADAM_B2 * v + (1.0 - ADAM_B2) * (g * g)
    m_hat = m / (1.0 - ADAM_B1 ** ADAM_STEP)
    v_hat = v / (1.0 - ADAM_B2 ** ADAM_STEP)
    delta = -ADAM_LR * (m_hat / (jnp.sqrt(v_hat) + ADAM_EPS) + ADAM_WD * w)
    return delta, m, v


def _chunk_mask():
    r = lax.broadcasted_iota(jnp.int32, (SG_BLOCK, SG_BLOCK), 0)
    c = lax.broadcasted_iota(jnp.int32, (SG_BLOCK, SG_BLOCK), 1)
    return (c // CHUNK) <= (r // CHUNK)


def _place():
    return lax.axis_index("x"), lax.axis_index("y"), lax.axis_index("c")


def _other_chips(x, y):
    return [(1 - x, y), (x, 1 - y), (1 - x, 1 - y)]


def allgather_small(v, name):
    m_per, n = v.shape

    def body(x_ref, out_ref, send_sems, recv_sems, local_sem):
        x, y, c = _place()
        me, sibling = (x, y, c), (x, y, 1 - c)
        chips = _other_chips(x, y)

        def rows(px, py, pc):
            return out_ref.at[pl.ds((4 * px + 2 * py + pc) * m_per, m_per), :]

        def copy(k, block, to, src=None):
            return pltpu.make_async_remote_copy(
                src_ref=rows(*block) if src is None else src, dst_ref=rows(*block),
                send_sem=send_sems.at[k], recv_sem=recv_sems.at[k], device_id=to, device_id_type=MESH)

        mine = pltpu.make_async_copy(x_ref, rows(*me), local_sem)
        mine.start()
        first = [copy(0, me, sibling, src=x_ref)]
        first += [copy(1 + j, me, (*chip, c), src=x_ref) for j, chip in enumerate(chips)]
        for cp in first:
            cp.start()
        passed = [copy(4 + j, (*chip, c), sibling) for j, chip in enumerate(chips)]
        for j, chip in enumerate(chips):
            copy(1 + j, (*chip, c), me).wait_recv()
            passed[j].start()
        copy(0, sibling, me).wait_recv()
        for j, chip in enumerate(chips):
            copy(4 + j, (*chip, 1 - c), me).wait_recv()
        for cp in first + passed:
            cp.wait_send()
        mine.wait()

    return _call(
        body, name=name,
        out_shape=jax.ShapeDtypeStruct((N_DEV * m_per, n), v.dtype),
        in_specs=[pl.BlockSpec(memory_space=pltpu.VMEM)],
        out_specs=pl.BlockSpec(memory_space=pltpu.VMEM),
        scratch_shapes=[pltpu.SemaphoreType.DMA((7,)), pltpu.SemaphoreType.DMA((7,)), pltpu.SemaphoreType.DMA],
    )(v)


def _hbm_spec():
    return pl.BlockSpec(memory_space=pltpu.HBM)


def _sem_spec():
    return pl.BlockSpec(memory_space=pltpu.SEMAPHORE)


def _split_params():
    return pltpu.CompilerParams(has_side_effects=pltpu.SideEffectType.DATAFLOW_SIDE_EFFECTING)


def _hbm(a):
    return pltpu.with_memory_space_constraint(a, pltpu.HBM)


def gather_inplace(land, name):
    def body(land_in, land_ref, send_sems, recv_sems):
        del land_in
        x, y, c = _place()
        sends = []
        for j, (px, py) in enumerate(_other_chips(x, y)):
            cp = pltpu.make_async_remote_copy(
                src_ref=land_ref.at[2 * x + y], dst_ref=land_ref.at[2 * x + y],
                send_sem=send_sems.at[j], recv_sem=recv_sems.at[j], device_id=(px, py, c), device_id_type=MESH)
            cp.start()
            sends.append(cp)
        for j, (px, py) in enumerate(_other_chips(x, y)):
            pltpu.make_async_remote_copy(
                src_ref=land_ref.at[2 * x + y], dst_ref=land_ref.at[2 * px + py],
                send_sem=send_sems.at[j], recv_sem=recv_sems.at[j], device_id=(px, py, c),
                device_id_type=MESH).wait_recv()
        for cp in sends:
            cp.wait_send()

    return _call(
        body, name=name,
        out_shape=jax.ShapeDtypeStruct(land.shape, land.dtype),
        in_specs=[_hbm_spec()], out_specs=_hbm_spec(), input_output_aliases={0: 0},
        scratch_shapes=[pltpu.SemaphoreType.DMA((3,)), pltpu.SemaphoreType.DMA((3,))],
    )(land)


def gather_start(land, name):
    def body(land_ref, send_sems, recv_sems, land_thru, token):
        del land_thru
        x, y, c = _place()
        for j, (px, py) in enumerate(_other_chips(x, y)):
            pltpu.make_async_remote_copy(
                src_ref=land_ref.at[2 * x + y], dst_ref=land_ref.at[2 * x + y],
                send_sem=send_sems.at[j], recv_sem=recv_sems.at[j], device_id=(px, py, c),
                device_id_type=MESH).start()
        token[...] = jnp.zeros_like(token)

    return _call(
        body, name=name,
        out_shape=(pltpu.SemaphoreType.DMA((3,)), pltpu.SemaphoreType.DMA((3,)),
                   pltpu.HBM(land.shape, land.dtype), jax.ShapeDtypeStruct((8, LANES), F32)),
        in_specs=(_hbm_spec(),),
        out_specs=(_sem_spec(), _sem_spec(), _hbm_spec(), pl.BlockSpec(memory_space=pltpu.VMEM)),
        input_output_aliases={0: 2}, compiler_params=_split_params(),
    )(_hbm(land))


def gather_wait(send_sems, recv_sems, land, after, name):
    def body(land_ref, send_sems, recv_sems, after_ref, land_out):
        del after_ref, land_out
        x, y, c = _place()
        for j, (px, py) in enumerate(_other_chips(x, y)):
            cp = pltpu.make_async_remote_copy(
                src_ref=land_ref.at[2 * x + y], dst_ref=land_ref.at[2 * px + py],
                send_sem=send_sems.at[j], recv_sem=recv_sems.at[j], device_id=(px, py, c), device_id_type=MESH)
            cp.wait_send()
            cp.wait_recv()

    return _call(
        body, name=name,
        out_shape=pltpu.HBM(land.shape, land.dtype),
        in_specs=(_hbm_spec(), _sem_spec(), _sem_spec(), pl.BlockSpec(memory_space=pl.ANY)),
        out_specs=_hbm_spec(), input_output_aliases={0: 0}, compiler_params=_split_params(),
    )(land, send_sems, recv_sems, after)


def exchange_start(parts, name):
    _, r, c_ = parts.shape

    def body(parts_ref, land_ref, send_sems, recv_sems, parts_thru, land_thru, token):
        del parts_thru, land_thru
        x, y, c = _place()
        for j, (px, py) in enumerate(_other_chips(x, y)):
            pltpu.make_async_remote_copy(
                src_ref=parts_ref.at[2 * px + py], dst_ref=land_ref.at[j],
                send_sem=send_sems.at[j], recv_sem=recv_sems.at[j], device_id=(px, py, c),
                device_id_type=MESH).start()
        token[...] = jnp.zeros_like(token)

    return _call(
        body, name=name,
        out_shape=(pltpu.SemaphoreType.DMA((3,)), pltpu.SemaphoreType.DMA((3,)),
                   pltpu.HBM(parts.shape, parts.dtype), pltpu.HBM((3, r, c_), parts.dtype),
                   jax.ShapeDtypeStruct((8, LANES), F32)),
        in_specs=(_hbm_spec(), _hbm_spec()),
        out_specs=(_sem_spec(), _sem_spec(), _hbm_spec(), _hbm_spec(), pl.BlockSpec(memory_space=pltpu.VMEM)),
        input_output_aliases={0: 2, 1: 3}, compiler_params=_split_params(),
    )(_hbm(parts), _hbm(lax.empty((3, r, c_), parts.dtype)))


def exchange_wait(send_sems, recv_sems, parts, land, after, name):
    def body(parts_ref, land_ref, send_sems, recv_sems, after_ref, parts_out, land_out):
        del after_ref, parts_out, land_out
        x, y, c = _place()
        for j, (px, py) in enumerate(_other_chips(x, y)):
            cp = pltpu.make_async_remote_copy(
                src_ref=parts_ref.at[2 * px + py], dst_ref=land_ref.at[j],
                send_sem=send_sems.at[j], recv_sem=recv_sems.at[j], device_id=(px, py, c), device_id_type=MESH)
            cp.wait_send()
            cp.wait_recv()

    return _call(
        body, name=name,
        out_shape=(pltpu.HBM(parts.shape, parts.dtype), pltpu.HBM(land.shape, land.dtype)),
        in_specs=(_hbm_spec(), _hbm_spec(), _sem_spec(), _sem_spec(), pl.BlockSpec(memory_space=pl.ANY)),
        out_specs=(_hbm_spec(), _hbm_spec()), input_output_aliases={0: 0, 1: 1},
        compiler_params=_split_params(),
    )(parts, land, send_sems, recv_sems, after)


def cast_into_slot(w, chip, name):
    r, c = w.shape
    tr = min(256, r)

    def body(s_ref, w_ref, o_ref):
        del s_ref
        o_ref[...] = w_ref[...].astype(BF16)

    return _call(
        body, name=name,
        grid_spec=pltpu.PrefetchScalarGridSpec(
            num_scalar_prefetch=1, grid=(r // tr,),
            in_specs=[pl.BlockSpec((tr, c), lambda i, s: (i, 0))],
            out_specs=pl.BlockSpec((None, tr, c), lambda i, s: (s[0], i, 0))),
        out_shape=jax.ShapeDtypeStruct((N_CHIPS, r, c), BF16),
        compiler_params=_params(),
    )(chip.reshape(1).astype(jnp.int32), w)


def sum_parts(parts, land, chip, name):
    _, r, c = parts.shape
    tr = min(256, r)

    def body(s_ref, p_ref, l_ref, o_ref):
        del s_ref
        acc = p_ref[...].astype(F32) + l_ref[0].astype(F32)
        acc = acc + l_ref[1].astype(F32)
        o_ref[...] = acc + l_ref[2].astype(F32)

    return _call(
        body, name=name,
        grid_spec=pltpu.PrefetchScalarGridSpec(
            num_scalar_prefetch=1, grid=(r // tr,),
            in_specs=[pl.BlockSpec((None, tr, c), lambda i, s: (s[0], i, 0)),
                      pl.BlockSpec((3, tr, c), lambda i, s: (0, i, 0))],
            out_specs=pl.BlockSpec((tr, c), lambda i, s: (i, 0))),
        out_shape=jax.ShapeDtypeStruct((r, c), F32),
        compiler_params=_params(),
    )(chip.reshape(1).astype(jnp.int32), parts, land)


def swap_sibling(arrs, name):
    n = len(arrs)

    def body(*refs):
        ins, outs = refs[:n], refs[n:2 * n]
        send_sems, recv_sems = refs[2 * n:]
        x, y, c = _place()
        cps = []
        for w in range(n):
            cp = pltpu.make_async_remote_copy(
                src_ref=ins[w], dst_ref=outs[w], send_sem=send_sems.at[w], recv_sem=recv_sems.at[w],
                device_id=(x, y, 1 - c), device_id_type=MESH)
            cp.start()
            cps.append(cp)
        for cp in cps:
            cp.wait_recv()
        for cp in cps:
            cp.wait_send()

    return _call(
        body, name=name,
        out_shape=[jax.ShapeDtypeStruct(a.shape, a.dtype) for a in arrs],
        in_specs=[_hbm_spec()] * n, out_specs=[_hbm_spec()] * n,
        scratch_shapes=[pltpu.SemaphoreType.DMA((n,)), pltpu.SemaphoreType.DMA((n,))],
    )(*arrs)


def adamw_pair(pa, pb, w, m, v, name):
    r, c = w.shape
    tr = min(128, r)

    def body(pa_ref, pb_ref, w_ref, m_ref, v_ref, g_ref, d_ref, nm_ref, nv_ref):
        g = pa_ref[...] + pb_ref[...]
        d, nm, nv = _adamw(w_ref[...], g, m_ref[...], v_ref[...])
        g_ref[...] = g
        d_ref[...] = d
        nm_ref[...] = nm
        nv_ref[...] = nv

    spec = pl.BlockSpec((tr, c), lambda i: (i, 0))
    return _call(
        body, name=name, grid=(r // tr,),
        out_shape=[jax.ShapeDtypeStruct((r, c), F32)] * 4,
        in_specs=[spec] * 5, out_specs=[spec] * 4,
        compiler_params=_params(),
    )(pa, pb, w, m, v)


def small_update(gathered, w, m, v, name):
    def body(g_ref, w_ref, m_ref, v_ref, go_ref, d_ref, nm_ref, nv_ref):
        g = g_ref[0]
        for k in range(1, N_DEV):
            g = g + g_ref[k]
        d, nm, nv = _adamw(w_ref[...], g, m_ref[...], v_ref[...])
        go_ref[...] = g
        d_ref[...] = d
        nm_ref[...] = nm
        nv_ref[...] = nv

    return _call(
        body, name=name,
        out_shape=[jax.ShapeDtypeStruct(w.shape, F32)] * 4,
        compiler_params=_params(),
    )(gathered, w, m, v)


def ada_fwd(c_all, w_ada, b_cols, name):
    n_l, d, cols = w_ada.shape
    nb = c_all.shape[0]
    tn = 256

    def body(c_ref, w_ref, b_ref, o_ref):
        cv = c_ref[...]
        ca = (cv * _sigmoid(cv)).astype(BF16)
        o_ref[...] = _dot(ca, w_ref[...].astype(BF16), NN) + b_ref[...]

    return _call(
        body, name=name, grid=(n_l, cols // tn),
        out_shape=jax.ShapeDtypeStruct((n_l, nb, cols), F32),
        in_specs=[pl.BlockSpec((nb, d), lambda l, j: (0, 0)),
                  pl.BlockSpec((None, d, tn), lambda l, j: (l, 0, j)),
                  pl.BlockSpec((None, 1, tn), lambda l, j: (l, 0, j))],
        out_specs=pl.BlockSpec((None, nb, tn), lambda l, j: (l, 0, j)),
        compiler_params=_params(),
    )(c_all, w_ada, b_cols)


def ada_bwd(c_all, dmod_cols, w, m, v, name):
    n_l, d, cols = w.shape
    nb = c_all.shape[0]
    tn = 256

    def body(c_ref, dm_ref, w_ref, m_ref, v_ref, g_ref, d_ref, nm_ref, nv_ref):
        cv = c_ref[...]
        ca = (cv * _sigmoid(cv)).astype(BF16)
        g = _dot(ca, dm_ref[...].astype(BF16), TN)
        dl, nm, nv = _adamw(w_ref[...], g, m_ref[...], v_ref[...])
        g_ref[...] = g
        d_ref[...] = dl
        nm_ref[...] = nm
        nv_ref[...] = nv

    wspec = pl.BlockSpec((None, d, tn), lambda l, j: (l, 0, j))
    return _call(
        body, name=name, grid=(n_l, cols // tn),
        out_shape=[jax.ShapeDtypeStruct((n_l, d, cols), F32)] * 4,
        in_specs=[pl.BlockSpec((nb, d), lambda l, j: (0, 0)),
                  pl.BlockSpec((None, nb, tn), lambda l, j: (l, 0, j)),
                  wspec, wspec, wspec],
        out_specs=[wspec] * 4,
        compiler_params=_params(),
    )(c_all, dmod_cols, w, m, v)


def bias_update(dmod_all, w, m, v, name):
    def body(dm_ref, w_ref, m_ref, v_ref, g_ref, d_ref, nm_ref, nv_ref):
        g = jnp.sum(dm_ref[...], axis=0, keepdims=True)
        dl, nm, nv = _adamw(w_ref[...], g, m_ref[...], v_ref[...])
        g_ref[...] = g
        d_ref[...] = dl
        nm_ref[...] = nm
        nv_ref[...] = nv

    return _call(
        body, name=name,
        out_shape=[jax.ShapeDtypeStruct(w.shape, F32)] * 4,
        compiler_params=_params(),
    )(dmod_all, w, m, v)


def inproj_fwd(x, mod, ng, wg, seq, sectioned, name):
    m_rows, d = x.shape
    nsh, _, ns = wg.shape
    n = nsh * ns
    tm, tn = min(512, seq), 512
    per = ns // tn

    def body(x_ref, mod_ref, ng_ref, w_ref, proj_ref, h_ref):
        @pl.when(pl.program_id(1) == 0)
        def _():
            xv = x_ref[...]
            r = lax.rsqrt(jnp.mean(xv * xv, axis=-1, keepdims=True) + EPS)
            md = mod_ref[0]
            h = (xv * r * ng_ref[...]) * (1.0 + md[:, d:2 * d]) + md[:, :d]
            h_ref[...] = h.astype(BF16)
        proj_ref[...] = _dot(h_ref[...], w_ref[...], NN)

    if sectioned:
        proj_shape = (nsh, m_rows, ns)
        proj_spec = pl.BlockSpec((None, tm, tn), lambda i, j: (j // per, i, j % per))
    else:
        proj_shape = (m_rows, n)
        proj_spec = pl.BlockSpec((tm, tn), lambda i, j: (i, j))
    return _call(
        body, name=name, grid=(m_rows // tm, n // tn),
        out_shape=[jax.ShapeDtypeStruct(proj_shape, F32), jax.ShapeDtypeStruct((m_rows, d), BF16)],
        in_specs=[pl.BlockSpec((tm, d), lambda i, j: (i, 0)),
                  pl.BlockSpec((1, 1, 3 * d), lambda i, j: ((i * tm) // seq, 0, 0)),
                  pl.BlockSpec((1, d), lambda i, j: (0, 0)),
                  pl.BlockSpec((None, d, tn), lambda i, j: (j // per, 0, j % per))],
        out_specs=[proj_spec, pl.BlockSpec((tm, d), lambda i, j: (i, 0))],
        compiler_params=_params(),
    )(x, mod, ng, wg)


def outproj_fwd(y, w, x, mod, seq, name):
    m_rows, di = y.shape
    d = w.shape[1]
    tm = min(512, seq)

    def body(y_ref, w_ref, x_ref, mod_ref, xn_ref, out_ref):
        acc = _dot(y_ref[...], w_ref[...], NN)
        out_ref[...] = acc
        xn_ref[...] = x_ref[...] + mod_ref[0][:, 2 * d:] * acc

    row = pl.BlockSpec((tm, d), lambda i: (i, 0))
    return _call(
        body, name=name, grid=(m_rows // tm,),
        out_shape=[jax.ShapeDtypeStruct((m_rows, d), F32)] * 2,
        in_specs=[pl.BlockSpec((tm, di), lambda i: (i, 0)),
                  pl.BlockSpec((di, d), lambda i: (0, 0)),
                  row,
                  pl.BlockSpec((1, 1, 3 * d), lambda i: ((i * tm) // seq, 0, 0))],
        out_specs=[row, row],
        compiler_params=_params(),
    )(y, w, x, mod)


def outproj_bwd(dxo, out, mod, w, seq, name):
    m_rows, d = dxo.shape
    di = w.shape[0]
    nb = m_rows // seq
    tm, tn = min(512, seq), 512

    def body(dxo_ref, out_ref, mod_ref, w_ref, dy_ref, dout_ref, dgate_ref):
        i = pl.program_id(0)

        @pl.when(pl.program_id(1) == 0)
        def _():
            dx = dxo_ref[...]
            dout_ref[...] = (mod_ref[0][:, 2 * d:] * dx).astype(BF16)
            part = jnp.sum(dx * out_ref[...], axis=0, keepdims=True)

            @pl.when((i * tm) % seq == 0)
            def _():
                dgate_ref[0] = part

            @pl.when((i * tm) % seq != 0)
            def _():
                dgate_ref[0] = dgate_ref[0] + part

        dy_ref[...] = _dot(dout_ref[...], w_ref[...], NT)

    row = pl.BlockSpec((tm, d), lambda i, j: (i, 0))
    return _call(
        body, name=name, grid=(m_rows // tm, di // tn),
        out_shape=[jax.ShapeDtypeStruct((m_rows, di), F32), jax.ShapeDtypeStruct((m_rows, d), BF16),
                   jax.ShapeDtypeStruct((nb, 1, d), F32)],
        in_specs=[row, row,
                  pl.BlockSpec((1, 1, 3 * d), lambda i, j: ((i * tm) // seq, 0, 0)),
                  pl.BlockSpec((tn, d), lambda i, j: (j, 0))],
        out_specs=[pl.BlockSpec((tm, tn), lambda i, j: (i, j)), row,
                   pl.BlockSpec((1, 1, d), lambda i, j: ((i * tm) // seq, 0, 0))],
        compiler_params=_params(),
    )(dxo, out, mod, w)


def grad_w_out(y, dout, name):
    m_rows, di = y.shape
    d = dout.shape[1]
    tm, tk = min(512, m_rows), 512
    n_m = m_rows // tm

    def body(y_ref, do_ref, o_ref, acc_ref):
        mi = pl.program_id(1)
        part = _dot(y_ref[...], do_ref[...], TN)

        @pl.when(mi == 0)
        def _():
            acc_ref[...] = part

        @pl.when(mi != 0)
        def _():
            acc_ref[...] = acc_ref[...] + part

        @pl.when(mi == n_m - 1)
        def _():
            o_ref[...] = acc_ref[...].astype(BF16)

    return _call(
        body, name=name, grid=(di // tk, n_m),
        out_shape=jax.ShapeDtypeStruct((di, d), BF16),
        in_specs=[pl.BlockSpec((tm, tk), lambda j, mi: (mi, j)),
                  pl.BlockSpec((tm, d), lambda j, mi: (mi, 0))],
        out_specs=pl.BlockSpec((tk, d), lambda j, mi: (j, 0)),
        scratch_shapes=[pltpu.VMEM((tk, d), F32)],
        compiler_params=_params(),
    )(y, dout)


def grad_w_in(h, dproj, nsh, sectioned, name):
    m_rows, d = h.shape
    n = dproj.shape[0] * dproj.shape[2] if sectioned else dproj.shape[1]
    ns = n // nsh
    tm, tn = min(512, m_rows), 512
    per = ns // tn
    n_m = m_rows // tm

    def body(h_ref, dp_ref, o_ref, acc_ref):
        mi = pl.program_id(1)
        part = _dot(h_ref[...], dp_ref[...], TN)

        @pl.when(mi == 0)
        def _():
            acc_ref[...] = part

        @pl.when(mi != 0)
        def _():
            acc_ref[...] = acc_ref[...] + part

        @pl.when(mi == n_m - 1)
        def _():
            o_ref[...] = acc_ref[...].astype(BF16)

    if sectioned:
        dp_spec = pl.BlockSpec((None, tm, tn), lambda j, mi: (j // per, mi, j % per))
    else:
        dp_spec = pl.BlockSpec((tm, tn), lambda j, mi: (mi, j))
    return _call(
        body, name=name, grid=(n // tn, n_m),
        out_shape=jax.ShapeDtypeStruct((nsh, d, ns), BF16),
        in_specs=[pl.BlockSpec((tm, d), lambda j, mi: (mi, 0)), dp_spec],
        out_specs=pl.BlockSpec((None, d, tn), lambda j, mi: (j // per, 0, j % per)),
        scratch_shapes=[pltpu.VMEM((d, tn), F32)],
        compiler_params=_params(),
    )(h, dproj)


def inproj_bwd(dproj, wg, x, dxo, mod, ng, seq, sectioned, name):
    m_rows, d = x.shape
    nsh, _, ns = wg.shape
    n = nsh * ns
    nb = m_rows // seq
    tm, tk = min(512, seq), 512
    per = ns // tk
    n_k = n // tk

    def body(dp_ref, w_ref, x_ref, dxo_ref, mod_ref, ng_ref, dxi_ref, dsh_ref, dsc_ref, dng_ref, acc_ref):
        i, k = pl.program_id(0), pl.program_id(1)
        part = _dot(dp_ref[...], w_ref[...], NT)

        @pl.when(k == 0)
        def _():
            acc_ref[...] = part

        @pl.when(k != 0)
        def _():
            acc_ref[...] = acc_ref[...] + part

        @pl.when(k == n_k - 1)
        def _():
            dh = acc_ref[...]
            xv = x_ref[...]
            r = lax.rsqrt(jnp.mean(xv * xv, axis=-1, keepdims=True) + EPS)
            xn = xv * r
            md = mod_ref[0]
            gain = ng_ref[...]
            p_shift = jnp.sum(dh, axis=0, keepdims=True)
            p_scale = jnp.sum(dh * (xn * gain), axis=0, keepdims=True)
            drn = dh * (1.0 + md[:, d:2 * d])
            p_ng = jnp.sum(drn * xn, axis=0, keepdims=True)
            dxn = drn * gain
            dx = r * (dxn - xn * jnp.mean(dxn * xn, axis=-1, keepdims=True))
            dxi_ref[...] = dxo_ref[...] + dx

            @pl.when((i * tm) % seq == 0)
            def _():
                dsh_ref[0] = p_shift
                dsc_ref[0] = p_scale

            @pl.when((i * tm) % seq != 0)
            def _():
                dsh_ref[0] = dsh_ref[0] + p_shift
                dsc_ref[0] = dsc_ref[0] + p_scale

            @pl.when(i == 0)
            def _():
                dng_ref[...] = p_ng

            @pl.when(i != 0)
            def _():
                dng_ref[...] = dng_ref[...] + p_ng

    if sectioned:
        dp_spec = pl.BlockSpec((None, tm, tk), lambda i, k: (k // per, i, k % per))
    else:
        dp_spec = pl.BlockSpec((tm, tk), lambda i, k: (i, k))
    row = pl.BlockSpec((tm, d), lambda i, k: (i, 0))
    per_seq = pl.BlockSpec((1, 1, d), lambda i, k: ((i * tm) // seq, 0, 0))
    return _call(
        body, name=name, grid=(m_rows // tm, n_k),
        out_shape=[jax.ShapeDtypeStruct((m_rows, d), F32), jax.ShapeDtypeStruct((nb, 1, d), F32),
                   jax.ShapeDtypeStruct((nb, 1, d), F32), jax.ShapeDtypeStruct((1, d), F32)],
        in_specs=[dp_spec,
                  pl.BlockSpec((None, d, tk), lambda i, k: (k // per, 0, k % per)),
                  row, row,
                  pl.BlockSpec((1, 1, 3 * d), lambda i, k: ((i * tm) // seq, 0, 0)),
                  pl.BlockSpec((1, d), lambda i, k: (0, 0))],
        out_specs=[row, per_seq, per_seq, pl.BlockSpec((1, d), lambda i, k: (0, 0))],
        scratch_shapes=[pltpu.VMEM((tm, d), F32)],
        compiler_params=_params(),
    )(dproj, wg, x, dxo, mod, ng)


def _sgu_stats(proj_ref, vg_ref, di, gd):
    s1 = jnp.zeros((SG_BLOCK, 1), F32)
    for g in range(SG_GROUPS):
        vg = _gelu(proj_ref[:, di + g * gd:di + (g + 1) * gd])
        vg_ref[:, g * gd:(g + 1) * gd] = vg
        s1 = s1 + jnp.sum(vg, axis=1, keepdims=True)
    mu = s1 / di
    s2 = jnp.zeros((SG_BLOCK, 1), F32)
    for g in range(SG_GROUPS):
        dv = vg_ref[:, g * gd:(g + 1) * gd] - mu
        s2 = s2 + jnp.sum(dv * dv, axis=1, keepdims=True)
    return mu, lax.rsqrt(s2 / di + EPS)


def sgu_fwd(proj, ln_gain, ln_bias, ws, bs, name):
    m_rows, n3 = proj.shape
    di = n3 // 3
    gd = di // SG_GROUPS

    def body(proj_ref, lg_ref, lb_ref, ws_ref, bs_ref, y_ref, wsm_ref, vg_ref):
        @pl.when(pl.program_id(0) == 0)
        def _():
            mask = _chunk_mask()
            for g in range(SG_GROUPS):
                wsm_ref[g] = jnp.where(mask, ws_ref[g], 0.0).astype(BF16)

        mu, rstd = _sgu_stats(proj_ref, vg_ref, di, gd)
        for g in range(SG_GROUPS):
            cs = slice(g * gd, (g + 1) * gd)
            vln = (vg_ref[:, cs] - mu) * rstd * lg_ref[:, cs] + lb_ref[:, cs]
            s = _dot(wsm_ref[g], vln.astype(BF16), NN) + bs_ref[g]
            u = _gelu(proj_ref[:, cs])
            gp = proj_ref[:, 2 * di + g * gd:2 * di + (g + 1) * gd]
            y_ref[:, cs] = (u * s * (gp * _sigmoid(gp))).astype(BF16)

    full = lambda shape: pl.BlockSpec(shape, lambda i: (0,) * len(shape))
    return _call(
        body, name=name, grid=(m_rows // SG_BLOCK,),
        out_shape=jax.ShapeDtypeStruct((m_rows, di), BF16),
        in_specs=[pl.BlockSpec((SG_BLOCK, n3), lambda i: (i, 0)),
                  full((1, di)), full((1, di)),
                  full((SG_GROUPS, SG_BLOCK, SG_BLOCK)), full((SG_GROUPS, SG_BLOCK, 1))],
        out_specs=pl.BlockSpec((SG_BLOCK, di), lambda i: (i, 0)),
        scratch_shapes=[pltpu.VMEM((SG_GROUPS, SG_BLOCK, SG_BLOCK), BF16), pltpu.VMEM((SG_BLOCK, di), F32)],
        compiler_params=_params(),
    )(proj, ln_gain, ln_bias, ws, bs)


def sgu_bwd(proj, dy, ln_gain, ln_bias, ws, bs, name):
    m_rows, n3 = proj.shape
    di = n3 // 3
    gd = di // SG_GROUPS
    n_i = m_rows // SG_BLOCK

    def body(proj_ref, dy_ref, lg_ref, lb_ref, ws_ref, bs_ref,
             dp_ref, dws_ref, dbs_ref, dlg_ref, dlb_ref, wsm_ref, vg_ref, dvh_ref):
        i = pl.program_id(0)

        @pl.when(i == 0)
        def _():
            mask = _chunk_mask()
            for g in range(SG_GROUPS):
                wsm_ref[g] = jnp.where(mask, ws_ref[g], 0.0).astype(BF16)
            dws_ref[...] = jnp.zeros_like(dws_ref)
            dbs_ref[...] = jnp.zeros_like(dbs_ref)
            dlg_ref[...] = jnp.zeros_like(dlg_ref)
            dlb_ref[...] = jnp.zeros_like(dlb_ref)

        mu, rstd = _sgu_stats(proj_ref, vg_ref, di, gd)
        m1 = jnp.zeros((SG_BLOCK, 1), F32)
        m2 = jnp.zeros((SG_BLOCK, 1), F32)
        for g in range(SG_GROUPS):
            cs = slice(g * gd, (g + 1) * gd)
            gs = slice(2 * di + g * gd, 2 * di + (g + 1) * gd)
            gain = lg_ref[:, cs]
            vhat = (vg_ref[:, cs] - mu) * rstd
            vln_b = (vhat * gain + lb_ref[:, cs]).astype(BF16)
            s = _dot(wsm_ref[g], vln_b, NN) + bs_ref[g]
            u, du = _gelu_and_grad(proj_ref[:, cs])
            sg, dsg = _silu_and_grad(proj_ref[:, gs])
            dyv = dy_ref[:, cs]
            dp_ref[:, cs] = (dyv * s * sg * du).astype(BF16)
            dp_ref[:, gs] = (dyv * u * s * dsg).astype(BF16)
            ds = dyv * u * sg
            ds_b = ds.astype(BF16)
            dws_ref[g] = dws_ref[g] + _dot(ds_b, vln_b, NT)
            dbs_ref[g] = dbs_ref[g] + jnp.sum(ds, axis=1, keepdims=True)
            dvln = _dot(wsm_ref[g], ds_b, TN)
            dlg_ref[:, cs] = dlg_ref[:, cs] + jnp.sum(dvln * vhat, axis=0, keepdims=True)
            dlb_ref[:, cs] = dlb_ref[:, cs] + jnp.sum(dvln, axis=0, keepdims=True)
            dvh = dvln * gain
            dvh_ref[:, cs] = dvh
            m1 = m1 + jnp.sum(dvh, axis=1, keepdims=True)
            m2 = m2 + jnp.sum(dvh * vhat, axis=1, keepdims=True)
        m1 = m1 / di
        m2 = m2 / di
        for g in range(SG_GROUPS):
            cs = slice(g * gd, (g + 1) * gd)
            vs = slice(di + g * gd, di + (g + 1) * gd)
            vhat = (vg_ref[:, cs] - mu) * rstd
            dvg = rstd * (dvh_ref[:, cs] - m1 - vhat * m2)
            _, dgel = _gelu_and_grad(proj_ref[:, vs])
            dp_ref[:, vs] = (dvg * dgel).astype(BF16)

        @pl.when(i == n_i - 1)
        def _():
            mask = _chunk_mask()
            for g in range(SG_GROUPS):
                dws_ref[g] = jnp.where(mask, dws_ref[g], 0.0)

    full = lambda shape: pl.BlockSpec(shape, lambda i: (0,) * len(shape))
    return _call(
        body, name=name, grid=(n_i,),
        out_shape=[jax.ShapeDtypeStruct((m_rows, n3), BF16),
                   jax.ShapeDtypeStruct((SG_GROUPS, SG_BLOCK, SG_BLOCK), F32),
                   jax.ShapeDtypeStruct((SG_GROUPS, SG_BLOCK, 1), F32),
                   jax.ShapeDtypeStruct((1, di), F32), jax.ShapeDtypeStruct((1, di), F32)],
        in_specs=[pl.BlockSpec((SG_BLOCK, n3), lambda i: (i, 0)),
                  pl.BlockSpec((SG_BLOCK, di), lambda i: (i, 0)),
                  full((1, di)), full((1, di)),
                  full((SG_GROUPS, SG_BLOCK, SG_BLOCK)), full((SG_GROUPS, SG_BLOCK, 1))],
        out_specs=[pl.BlockSpec((SG_BLOCK, n3), lambda i: (i, 0)),
                   full((SG_GROUPS, SG_BLOCK, SG_BLOCK)), full((SG_GROUPS, SG_BLOCK, 1)),
                   full((1, di)), full((1, di))],
        scratch_shapes=[pltpu.VMEM((SG_GROUPS, SG_BLOCK, SG_BLOCK), BF16),
                        pltpu.VMEM((SG_BLOCK, di), F32), pltpu.VMEM((SG_BLOCK, di), F32)],
        compiler_params=_params(),
    )(proj, dy, ln_gain, ln_bias, ws, bs)


def _lower_bound(lbraw):
    mx = jnp.maximum(lbraw[0:1, :], lbraw[1:2, :])
    e0 = jnp.exp(lbraw[0:1, :] - mx)
    e1 = jnp.exp(lbraw[1:2, :] - mx)
    p0 = e0 / (e0 + e1)
    p1 = e1 / (e0 + e1)
    return (p0 + p1) - p0, p0, p1


def _tri(lower):
    r = lax.broadcasted_iota(jnp.int32, (CHUNK, CHUNK), 0)
    c = lax.broadcasted_iota(jnp.int32, (CHUNK, CHUNK), 1)
    return ((r >= c) if lower else (c >= r)).astype(F32)


def _row(a, idx):
    r = lax.broadcasted_iota(jnp.int32, a.shape, 0)
    return jnp.sum(jnp.where(r == idx, a, 0.0), axis=0, keepdims=True)


def _hgrn_gates(qp, fp, lb, tri):
    sgm = _sigmoid(fp)
    f = lb + (1.0 - lb) * sgm
    k = 1.0 - f
    a = _dot(tri, jnp.log(f), NN, precision=lax.Precision.HIGHEST)
    a_mid = _row(a, CHUNK // 2 - 1)
    a_last = _row(a, CHUNK - 1)
    q, dq = _silu_and_grad(qp)
    e1, e2, e3, e4 = jnp.exp(a - a_mid), jnp.exp(a_mid - a), jnp.exp(a), jnp.exp(a_last - a)
    return dict(sgm=sgm, f=f, k=k, q=q, dq=dq, e1=e1, e2=e2, e3=e3, e4=e4, dec=jnp.exp(a_last),
                q_in=q * e1, k_in=k * e2, q_out=q * e3, k_out=k * e4)


def _causal():
    r = lax.broadcasted_iota(jnp.int32, (CHUNK, CHUNK), 0)
    c = lax.broadcasted_iota(jnp.int32, (CHUNK, CHUNK), 1)
    return r >= c


def hgrn_fwd(proj4, lbraw, gn, seq, name):
    _, m_rows, di = proj4.shape
    nb, nh, nc = m_rows // seq, di // HEAD_DIM, seq // CHUNK
    rows = min(HG_ROWS, seq)
    wide = HG_WIDE * HEAD_DIM
    ns, cpb = seq // rows, rows // CHUNK

    def body(p_ref, lb_ref, gn_ref, y_ref, sts_ref, st_ref):
        @pl.when(pl.program_id(2) == 0)
        def _():
            st_ref[...] = jnp.zeros_like(st_ref)

        tri = _tri(True)
        causal = _causal()
        gain = gn_ref[...]
        lbs = [_lower_bound(lb_ref[:, j * HEAD_DIM:(j + 1) * HEAD_DIM])[0] for j in range(HG_WIDE)]

        units = [(n, j) for n in range(cpb) for j in range(HG_WIDE)]
        rs = lambda n: slice(n * CHUNK, (n + 1) * CHUNK)
        cs = lambda j: slice(j * HEAD_DIM, (j + 1) * HEAD_DIM)
        gates, v_b, sc_b, kv, o_in, o_x = {}, {}, {}, {}, {}, {}
        for n, j in units:
            gates[n, j] = _hgrn_gates(p_ref[0, rs(n), cs(j)], p_ref[1, rs(n), cs(j)], lbs[j], tri)
            v_b[n, j] = p_ref[2, rs(n), cs(j)].astype(BF16)
        for u in units:
            t = gates[u]
            sc_b[u] = jnp.where(causal, _dot(t["q_in"].astype(BF16), t["k_in"].astype(BF16), NT), 0.0).astype(BF16)
            kv[u] = _dot(v_b[u], t["k_out"].astype(BF16), TN)
        for u in units:
            o_in[u] = _dot(sc_b[u], v_b[u], NN)
        for j in range(HG_WIDE):
            st = st_ref[j]
            for n in range(cpb):
                sts_ref[n, :, cs(j)] = st
                o_x[n, j] = _dot(gates[n, j]["q_out"].astype(BF16), st.astype(BF16), NT)
                st = st * gates[n, j]["dec"] + kv[n, j]
            st_ref[j] = st
        for n, j in units:
            o = o_in[n, j] + o_x[n, j]
            r = lax.rsqrt(jnp.mean(o * o, axis=-1, keepdims=True) + EPS)
            gp = p_ref[3, rs(n), cs(j)]
            y_ref[rs(n), cs(j)] = ((o * r * gain) * (gp * _sigmoid(gp))).astype(BF16)

    return _call(
        body, name=name, grid=(nh // HG_WIDE, nb, ns),
        out_shape=[jax.ShapeDtypeStruct((m_rows, di), BF16),
                   jax.ShapeDtypeStruct((nb * nc, HEAD_DIM, di), F32)],
        in_specs=[pl.BlockSpec((4, rows, wide), lambda hg, b, s: (0, b * ns + s, hg)),
                  pl.BlockSpec((2, wide), lambda hg, b, s: (0, hg)),
                  pl.BlockSpec((1, HEAD_DIM), lambda hg, b, s: (0, 0))],
        out_specs=[pl.BlockSpec((rows, wide), lambda hg, b, s: (b * ns + s, hg)),
                   pl.BlockSpec((cpb, HEAD_DIM, wide), lambda hg, b, s: (b * ns + s, 0, hg))],
        scratch_shapes=[pltpu.VMEM((HG_WIDE, HEAD_DIM, HEAD_DIM), F32)],
        compiler_params=_params(),
    )(proj4, lbraw, gn)


def hgrn_bwd(proj4, dy, sts, lbraw, gn, seq, name):
    _, m_rows, di = proj4.shape
    nb, nh, nc = m_rows // seq, di // HEAD_DIM, seq // CHUNK
    rows = min(HG_ROWS, seq)
    wide = HG_WIDE * HEAD_DIM
    ns, cpb = seq // rows, rows // CHUNK
    n_hg = nh // HG_WIDE

    def body(p_ref, dy_ref, sts_ref, lb_ref, gn_ref, dp_ref, dlb_ref, dgn_ref, dst_ref, lbacc_ref, gnacc_ref):
        hg, b, s = pl.program_id(0), pl.program_id(1), pl.program_id(2)
        tri, triu = _tri(True), _tri(False)
        causal = _causal()
        gain = gn_ref[...]
        first = (b == 0) & (s == 0)

        @pl.when((hg == 0) & first)
        def _():
            gnacc_ref[...] = jnp.zeros_like(gnacc_ref)

        @pl.when(first)
        def _():
            lbacc_ref[...] = jnp.zeros_like(lbacc_ref)

        @pl.when(s == 0)
        def _():
            dst_ref[...] = jnp.zeros_like(dst_ref)

        units = [(n, j) for n in range(cpb) for j in range(HG_WIDE)]
        rs = lambda n: slice(n * CHUNK, (n + 1) * CHUNK)
        cs = lambda j: slice(j * HEAD_DIM, (j + 1) * HEAD_DIM)
        lbs = [_lower_bound(lb_ref[:, cs(j)])[0] for j in range(HG_WIDE)]
        gates, v_b, st_b, sc_b, o, do_b = {}, {}, {}, {}, {}, {}
        dq_out, dsc_b, dv, g_st, dq_in, dk_in, dst_at, dk_out, ddec = {}, {}, {}, {}, {}, {}, {}, {}, {}
        for n, j in units:
            gates[n, j] = _hgrn_gates(p_ref[0, rs(n), cs(j)], p_ref[1, rs(n), cs(j)], lbs[j], tri)
            v_b[n, j] = p_ref[2, rs(n), cs(j)].astype(BF16)
            st_b[n, j] = sts_ref[n, :, cs(j)].astype(BF16)
        for u in units:
            t = gates[u]
            sc_b[u] = jnp.where(causal, _dot(t["q_in"].astype(BF16), t["k_in"].astype(BF16), NT), 0.0).astype(BF16)
        for u in units:
            o[u] = _dot(sc_b[u], v_b[u], NN) + _dot(gates[u]["q_out"].astype(BF16), st_b[u], NT)
        for n, j in units:
            ov = o[n, j]
            r = lax.rsqrt(jnp.mean(ov * ov, axis=-1, keepdims=True) + EPS)
            ohat = ov * r
            sg, dsg = _silu_and_grad(p_ref[3, rs(n), cs(j)])
            dyv = dy_ref[rs(n), cs(j)]
            dp_ref[3, rs(n), cs(j)] = (dyv * (ohat * gain) * dsg).astype(BF16)
            d_on = dyv * sg
            gnacc_ref[:, cs(j)] = gnacc_ref[:, cs(j)] + jnp.sum(d_on * ohat, axis=0, keepdims=True)
            dohat = d_on * gain
            do_b[n, j] = (r * (dohat - ohat * jnp.mean(dohat * ohat, axis=-1, keepdims=True))).astype(BF16)
        for u in units:
            dq_out[u] = _dot(do_b[u], st_b[u], NN)
            dsc_b[u] = jnp.where(causal, _dot(do_b[u], v_b[u], NT), 0.0).astype(BF16)
            dv[u] = _dot(sc_b[u], do_b[u], TN)
            g_st[u] = _dot(do_b[u], gates[u]["q_out"].astype(BF16), TN)
        for u in units:
            dq_in[u] = _dot(dsc_b[u], gates[u]["k_in"].astype(BF16), NN)
            dk_in[u] = _dot(dsc_b[u], gates[u]["q_in"].astype(BF16), TN)
        for j in range(HG_WIDE):
            dst = dst_ref[j]
            for n in reversed(range(cpb)):
                dst_at[n, j] = dst
                dst = dst * gates[n, j]["dec"] + g_st[n, j]
            dst_ref[j] = dst
        for n, j in units:
            dst = dst_at[n, j]
            dst_b = dst.astype(BF16)
            dk_out[n, j] = _dot(v_b[n, j], dst_b, NN)
            dv[n, j] = dv[n, j] + _dot(gates[n, j]["k_out"].astype(BF16), dst_b, NT)
            ddec[n, j] = jnp.sum(dst * sts_ref[n, :, cs(j)], axis=0, keepdims=True)
        for n, j in units:
            t = gates[n, j]
            dp_ref[2, rs(n), cs(j)] = dv[n, j].astype(BF16)
            dq = dq_in[n, j] * t["e1"] + dq_out[n, j] * t["e3"]
            dk = dk_in[n, j] * t["e2"] + dk_out[n, j] * t["e4"]
            w_in = dq_in[n, j] * t["q_in"] - dk_in[n, j] * t["k_in"]
            w_out = dk_out[n, j] * t["k_out"]
            da = w_in + dq_out[n, j] * t["q_out"] - w_out
            da_mid = -jnp.sum(w_in, axis=0, keepdims=True)
            da_last = jnp.sum(w_out, axis=0, keepdims=True) + ddec[n, j] * t["dec"]
            rid = lax.broadcasted_iota(jnp.int32, da.shape, 0)
            da = da + jnp.where(rid == CHUNK // 2 - 1, da_mid, 0.0) + jnp.where(rid == CHUNK - 1, da_last, 0.0)
            dlf = _dot(triu, da, NN, precision=lax.Precision.HIGHEST)
            df = dlf / t["f"] - dk
            sgm = t["sgm"]
            dp_ref[1, rs(n), cs(j)] = (df * (1.0 - lbs[j]) * sgm * (1.0 - sgm)).astype(BF16)
            lbacc_ref[:, cs(j)] = lbacc_ref[:, cs(j)] + jnp.sum(df * (1.0 - sgm), axis=0, keepdims=True)
            dp_ref[0, rs(n), cs(j)] = (dq * t["dq"]).astype(BF16)

        @pl.when((b == nb - 1) & (s == ns - 1))
        def _():
            for j in range(HG_WIDE):
                cs = slice(j * HEAD_DIM, (j + 1) * HEAD_DIM)
                _, p0, p1 = _lower_bound(lb_ref[:, cs])
                acc = lbacc_ref[:, cs]
                dlb_ref[0:1, cs] = -acc * p0 * p1
                dlb_ref[1:2, cs] = acc * p1 * (1.0 - p1)

        @pl.when((hg == n_hg - 1) & (b == nb - 1) & (s == ns - 1))
        def _():
            tot = gnacc_ref[:, 0:HEAD_DIM]
            for j in range(1, HG_WIDE):
                tot = tot + gnacc_ref[:, j * HEAD_DIM:(j + 1) * HEAD_DIM]
            dgn_ref[...] = tot

    blk = lambda hg, b, s: b * ns + (ns - 1 - s)
    return _call(
        body, name=name, grid=(n_hg, nb, ns),
        out_shape=[jax.ShapeDtypeStruct((4, m_rows, di), BF16), jax.ShapeDtypeStruct((2, di), F32),
                   jax.ShapeDtypeStruct((1, HEAD_DIM), F32)],
        in_specs=[pl.BlockSpec((4, rows, wide), lambda hg, b, s: (0, blk(hg, b, s), hg)),
                  pl.BlockSpec((rows, wide), lambda hg, b, s: (blk(hg, b, s), hg)),
                  pl.BlockSpec((cpb, HEAD_DIM, wide), lambda hg, b, s: (blk(hg, b, s), 0, hg)),
                  pl.BlockSpec((2, wide), lambda hg, b, s: (0, hg)),
                  pl.BlockSpec((1, HEAD_DIM), lambda hg, b, s: (0, 0))],
        out_specs=[pl.BlockSpec((4, rows, wide), lambda hg, b, s: (0, blk(hg, b, s), hg)),
                   pl.BlockSpec((2, wide), lambda hg, b, s: (0, hg)),
                   pl.BlockSpec((1, HEAD_DIM), lambda hg, b, s: (0, 0))],
        scratch_shapes=[pltpu.VMEM((HG_WIDE, HEAD_DIM, HEAD_DIM), F32), pltpu.VMEM((1, wide), F32),
                        pltpu.VMEM((1, wide), F32)],
        compiler_params=_params(),
    )(proj4, dy, sts, lbraw, gn)


def final_loss(x, fg, target, name):
    m_rows, d = x.shape
    tm = min(512, m_rows)

    def body(x_ref, fg_ref, t_ref, loss_ref, dx_ref, dfg_ref):
        i = pl.program_id(0)
        xv = x_ref[...]
        gain = fg_ref[...]
        r = lax.rsqrt(jnp.mean(xv * xv, axis=-1, keepdims=True) + EPS)
        xn = xv * r
        e = xn * gain - t_ref[...]
        part = 0.5 * jnp.sum(jnp.mean(e * e, axis=-1, keepdims=True), axis=0, keepdims=True)
        dyv = e / d
        p_fg = jnp.sum(dyv * xn, axis=0, keepdims=True)
        dxn = dyv * gain
        dx_ref[...] = r * (dxn - xn * jnp.mean(dxn * xn, axis=-1, keepdims=True))

        @pl.when(i == 0)
        def _():
            loss_ref[...] = part
            dfg_ref[...] = p_fg

        @pl.when(i != 0)
        def _():
            loss_ref[...] = loss_ref[...] + part
            dfg_ref[...] = dfg_ref[...] + p_fg

    row = pl.BlockSpec((tm, d), lambda i: (i, 0))
    return _call(
        body, name=name, grid=(m_rows // tm,),
        out_shape=[jax.ShapeDtypeStruct((1, 1), F32), jax.ShapeDtypeStruct((m_rows, d), F32),
                   jax.ShapeDtypeStruct((1, d), F32)],
        in_specs=[row, pl.BlockSpec((1, d), lambda i: (0, 0)), row],
        out_specs=[pl.BlockSpec((1, 1), lambda i: (0, 0)), row, pl.BlockSpec((1, d), lambda i: (0, 0))],
        compiler_params=_params(),
    )(x, fg, target)


def _pack(parts):
    flat = jnp.concatenate([p.reshape(-1) for p in parts])
    pad = (-flat.shape[0]) % (8 * LANES)
    return jnp.pad(flat, (0, pad)).reshape(-1, LANES)


def _unpack(packed, like):
    flat = packed.reshape(-1)
    out, off = [], 0
    for a in like:
        out.append(flat[off:off + a.size].reshape(a.shape))
        off += a.size
    return out


def kernel(x, c, norm_gain, w_ada, b_ada, a_w_in, a_ln_gain, a_ln_bias, a_w_s, a_b_s, a_w_out, b_w_in, b_lower_bounds, b_gn_gain, b_w_out, final_gain, loss_target, m_norm_gain, m_w_ada, m_b_ada, m_a_w_in, m_a_ln_gain, m_a_ln_bias, m_a_w_s, m_a_b_s, m_a_w_out, m_b_w_in, m_b_lower_bounds, m_b_gn_gain, m_b_w_out, m_final_gain, v_norm_gain, v_w_ada, v_b_ada, v_a_w_in, v_a_ln_gain, v_a_ln_bias, v_a_w_s, v_a_b_s, v_a_w_out, v_b_w_in, v_b_lower_bounds, v_b_gn_gain, v_b_w_out, v_final_gain):
    nb, seq, d = x.shape
    m_rows = nb * seq
    n_l = w_ada.shape[0]
    ada_cols = w_ada.shape[2]
    px, py, pc = _place()
    chip = 2 * px + py
    dev = 2 * chip + pc

    c_all = allgather_small(c.reshape(-1, LANES), "gather_c").reshape(N_DEV * nb, d)
    b_cols = lax.dynamic_slice_in_dim(b_ada, chip * ada_cols, ada_cols, axis=1).reshape(n_l, 1, ada_cols)
    mod_cols = ada_fwd(c_all, w_ada, b_cols, "ada_fwd")
    mod_g = allgather_small(mod_cols.reshape(-1, LANES), "gather_mod")
    mod_g = mod_g.reshape(N_CHIPS, 2, n_l, N_DEV * nb, ada_cols)[:, 0]
    mod_all = jnp.transpose(mod_g, (1, 2, 0, 3)).reshape(n_l, N_DEV * nb, 3 * d)
    mod_mine = lax.dynamic_slice_in_dim(mod_all, dev * nb, nb, axis=1)
    mod0 = mod_mine[0].reshape(nb, 1, 3 * d)
    mod1 = mod_mine[1].reshape(nb, 1, 3 * d)

    wa_in = gather_inplace(cast_into_slot(a_w_in[0], chip, "cast_a_in"), "gather_a_in")
    s_ao = gather_start(cast_into_slot(a_w_out[0], chip, "cast_a_out"), "gather_a_out_start")
    s_bi = gather_start(cast_into_slot(b_w_in[0], chip, "cast_b_in"), "gather_b_in_start")
    s_bo = gather_start(cast_into_slot(b_w_out[0], chip, "cast_b_out"), "gather_b_out_start")
    di = a_w_out.shape[1] * N_CHIPS

    x0 = x.reshape(m_rows, d)
    tgt = loss_target.reshape(m_rows, d)
    ng0 = norm_gain[0:1] + (s_ao[3][0, 0] + s_bi[3][0, 0] + s_bo[3][0, 0])
    ng1 = norm_gain[1:2]
    bs_col = a_b_s[0].reshape(SG_GROUPS, SG_BLOCK, 1)
    proj_a, h_a = inproj_fwd(x0, mod0, ng0, wa_in, seq, False, "a_inproj")
    y_a = sgu_fwd(proj_a, a_ln_gain, a_ln_bias, a_w_s[0], bs_col, "a_sgu")
    wa_out = gather_wait(*s_ao[:3], y_a, "gather_a_out_wait").reshape(di, d)
    x1, out_a = outproj_fwd(y_a, wa_out, x0, mod0, seq, "a_outproj")
    wb_in = gather_wait(*s_bi[:3], out_a, "gather_b_in_wait")
    proj_b, h_b = inproj_fwd(x1, mod1, ng1, wb_in, seq, True, "b_inproj")
    y_b, sts_b = hgrn_fwd(proj_b, b_lower_bounds, b_gn_gain, seq, "b_hgrn")
    wb_out = gather_wait(*s_bo[:3], y_b, "gather_b_out_wait").reshape(di, d)
    x2, out_b = outproj_fwd(y_b, wb_out, x1, mod1, seq, "b_outproj")
    loss_part, dx2, dfg = final_loss(x2, final_gain.reshape(1, d), tgt, "loss_head")
    loss = lax.psum(loss_part[0, 0], ("x", "y", "c"))

    shard_rows = di // N_CHIPS
    dy_b, dout_b, dgate1 = outproj_bwd(dx2, out_b, mod1, wb_out, seq, "b_outproj_bwd")
    gwb_out = grad_w_out(y_b, dout_b, "b_grad_w_out").reshape(N_CHIPS, shard_rows, d)
    e_bo = exchange_start(gwb_out, "exchange_b_out_start")
    dproj_b, dlb, dgn = hgrn_bwd(proj_b, dy_b, sts_b, b_lower_bounds, b_gn_gain + e_bo[4][0, 0], seq, "b_hgrn_bwd")
    e_bi = exchange_start(grad_w_in(h_b, dproj_b, N_CHIPS, True, "b_grad_w_in"), "exchange_b_in_start")
    dx1, dshift1, dscale1, dng1 = inproj_bwd(
        dproj_b, wb_in, x1, dx2, mod1, ng1 + e_bi[4][0, 0], seq, True, "b_inproj_bwd")

    dy_a, dout_a, dgate0 = outproj_bwd(dx1, out_a, mod0, wa_out, seq, "a_outproj_bwd")
    gwa_out = grad_w_out(y_a, dout_a, "a_grad_w_out").reshape(N_CHIPS, shard_rows, d)
    e_ao = exchange_start(gwa_out, "exchange_a_out_start")
    dproj_a, dws, dbs, dlg, dlbias = sgu_bwd(
        proj_a, dy_a, a_ln_gain + e_ao[4][0, 0], a_ln_bias, a_w_s[0], bs_col, "a_sgu_bwd")
    e_ai = exchange_start(grad_w_in(h_a, dproj_a, N_CHIPS, False, "a_grad_w_in"), "exchange_a_in_start")
    dx0, dshift0, dscale0, dng0 = inproj_bwd(
        dproj_a, wa_in, x0, dx1, mod0, norm_gain[0:1] + e_ai[4][0, 0], seq, False, "a_inproj_bwd")
    grad_x = dx0.reshape(nb, seq, d)

    def finish(ex, after, w, m, v, nm):
        parts_thru, land = exchange_wait(ex[0], ex[1], ex[2], ex[3], after, "exchange_" + nm + "_wait")
        mine = sum_parts(parts_thru, land, chip, "sum_" + nm)
        (other,) = swap_sibling([mine], "swap_" + nm)
        res = adamw_pair(mine, other, w[0], m[0], v[0], "adamw_" + nm)
        return [r.reshape(w.shape) for r in res]

    gb_out, db_out, mb_out, vb_out = finish(e_bo, dx0, b_w_out, m_b_w_out, v_b_w_out, "b_out")
    gb_in, db_in, mb_in, vb_in = finish(e_bi, gb_out, b_w_in, m_b_w_in, v_b_w_in, "b_in")
    ga_out, da_out, ma_out, va_out = finish(e_ao, gb_in, a_w_out, m_a_w_out, v_a_w_out, "a_out")

    dmod = jnp.concatenate([dshift0, dscale0, dgate0, dshift1, dscale1, dgate1], axis=2)
    dmod_all = allgather_small(dmod.reshape(-1, LANES), "gather_dmod").reshape(N_DEV * nb, n_l, 3 * d)
    dmod_cols = lax.dynamic_slice_in_dim(dmod_all, chip * ada_cols, ada_cols, axis=2)
    dmod_cols = jnp.transpose(dmod_cols, (1, 0, 2))
    g_wada, d_wada, m_wada, v_wada = ada_bwd(c_all, dmod_cols, w_ada, m_w_ada, v_w_ada, "ada_bwd")
    flat = lambda a: a.reshape(1, -1)
    g_bada, d_bada, m_bada, v_bada = [
        r.reshape(b_ada.shape) for r in
        bias_update(dmod_all.reshape(N_DEV * nb, n_l * 3 * d), flat(b_ada), flat(m_b_ada), flat(v_b_ada), "bias_update")]

    small_w = [norm_gain, a_ln_gain, a_ln_bias, a_w_s, a_b_s, b_lower_bounds, b_gn_gain, final_gain]
    small_m = [m_norm_gain, m_a_ln_gain, m_a_ln_bias, m_a_w_s, m_a_b_s, m_b_lower_bounds, m_b_gn_gain, m_final_gain]
    small_v = [v_norm_gain, v_a_ln_gain, v_a_ln_bias, v_a_w_s, v_a_b_s, v_b_lower_bounds, v_b_gn_gain, v_final_gain]
    small_g = [jnp.concatenate([dng0, dng1], axis=0), dlg, dlbias, dws, dbs, dlb, dgn, dfg]
    packed_g = _pack(small_g)
    rows = packed_g.shape[0]
    gathered = allgather_small(packed_g, "gather_small").reshape(N_DEV, rows, LANES)
    res = small_update(gathered, _pack(small_w), _pack(small_m), _pack(small_v), "small_update")
    sg, sd, sm, sv = [_unpack(r, small_w) for r in res]
    ga_in, da_in, ma_in, va_in = finish(e_ai, res[0], a_w_in, m_a_w_in, v_a_w_in, "a_in")

    def order(ng, wada, bada, ain, sm_rest, aout, bin_, bout):
        lg, lbi, ws_, bs_, lbd, gn_, fg_ = sm_rest
        return [ng, wada, bada, ain, lg, lbi, ws_, bs_, aout, bin_, lbd, gn_, bout, fg_]

    grads = order(sg[0], g_wada, g_bada, ga_in, sg[1:], ga_out, gb_in, gb_out)
    deltas = order(sd[0], d_wada, d_bada, da_in, sd[1:], da_out, db_in, db_out)
    new_m = order(sm[0], m_wada, m_bada, ma_in, sm[1:], ma_out, mb_in, mb_out)
    new_v = order(sv[0], v_wada, v_bada, va_in, sv[1:], va_out, vb_in, vb_out)
    return (loss, grad_x, *grads, *deltas, *new_m, *new_v)
```

```python
import functools

import jax
import jax.numpy as jnp
from jax import lax
from jax.experimental import pallas as pl
from jax.experimental.pallas import tpu as pltpu

F32 = jnp.float32
BF16 = jnp.bfloat16
EPS = 1e-6
CHUNK = 64
SG_BLOCK = 128
SG_GROUPS = 8
HEAD_DIM = 128
HG_WIDE = 8
HG_ROWS = 128
N_CHIPS = 4
N_DEV = 8
LANES = 128
ADAM_LR = 0.001
ADAM_B1 = 0.9
ADAM_B2 = 0.999
ADAM_EPS = 1e-08
ADAM_WD = 0.01
ADAM_STEP = 10
GELU_C0 = 0.7978845608028654
GELU_C1 = 0.044715
MESH = pl.DeviceIdType.MESH
VMEM_LIMIT = 56 * 1024 * 1024


ROW_TILE = 1024


def _col_tile(n):
    return next(t for t in (1024, 768, 512, 256) if n % t == 0)


def _call(body, **kw):
    return pl.pallas_call(body, **kw)


def _params(**kw):
    return pltpu.CompilerParams(vmem_limit_bytes=VMEM_LIMIT, **kw)


def _sigmoid(x):
    return 1.0 / (1.0 + jnp.exp(-x))


def _silu_and_grad(x):
    s = _sigmoid(x)
    return x * s, s * (1.0 + x * (1.0 - s))


def _gelu(x):
    return 0.5 * x * (1.0 + jnp.tanh(GELU_C0 * (x + GELU_C1 * x * x * x)))


def _gelu_and_grad(x):
    t = jnp.tanh(GELU_C0 * (x + GELU_C1 * x * x * x))
    g = 0.5 * x * (1.0 + t)
    dg = 0.5 * (1.0 + t) + 0.5 * x * (1.0 - t * t) * (GELU_C0 * (1.0 + 3.0 * GELU_C1 * x * x))
    return g, dg


def _dot(a, b, dims, precision=None):
    return lax.dot_general(a, b, (dims, ((), ())), precision=precision, preferred_element_type=F32)


NN = ((1,), (0,))
NT = ((1,), (1,))
TN = ((0,), (0,))


def _adamw(w, g, m, v):
    m = ADAM_B1 * m + (1.0 - ADAM_B1) * g
    v = ADAM_B2 * v + (1.0 - ADAM_B2) * (g * g)
    m_hat = m / (1.0 - ADAM_B1 ** ADAM_STEP)
    v_hat = v / (1.0 - ADAM_B2 ** ADAM_STEP)
    delta = -ADAM_LR * (m_hat / (jnp.sqrt(v_hat) + ADAM_EPS) + ADAM_WD * w)
    return delta, m, v


def _chunk_mask():
    r = lax.broadcasted_iota(jnp.int32, (SG_BLOCK, SG_BLOCK), 0)
    c = lax.broadcasted_iota(jnp.int32, (SG_BLOCK, SG_BLOCK), 1)
    return (c // CHUNK) <= (r // CHUNK)


def _place():
    return lax.axis_index("x"), lax.axis_index("y"), lax.axis_index("c")


def _other_chips(x, y):
    return [(1 - x, y), (x, 1 - y), (1 - x, 1 - y)]


def allgather_small(v, name):
    m_per, n = v.shape

    def body(x_ref, out_ref, send_sems, recv_sems, local_sem):
        x, y, c = _place()
        me, sibling = (x, y, c), (x, y, 1 - c)
        chips = _other_chips(x, y)

        def rows(px, py, pc):
            return out_ref.at[pl.ds((4 * px + 2 * py + pc) * m_per, m_per), :]

        def copy(k, block, to, src=None):
            return pltpu.make_async_remote_copy(
                src_ref=rows(*block) if src is None else src, dst_ref=rows(*block),
                send_sem=send_sems.at[k], recv_sem=recv_sems.at[k], device_id=to, device_id_type=MESH)

        mine = pltpu.make_async_copy(x_ref, rows(*me), local_sem)
        mine.start()
        first = [copy(0, me, sibling, src=x_ref)]
        first += [copy(1 + j, me, (*chip, c), src=x_ref) for j, chip in enumerate(chips)]
        for cp in first:
            cp.start()
        passed = [copy(4 + j, (*chip, c), sibling) for j, chip in enumerate(chips)]
        for j, chip in enumerate(chips):
            copy(1 + j, (*chip, c), me).wait_recv()
            passed[j].start()
        copy(0, sibling, me).wait_recv()
        for j, chip in enumerate(chips):
            copy(4 + j, (*chip, 1 - c), me).wait_recv()
        for cp in first + passed:
            cp.wait_send()
        mine.wait()

    return _call(
        body, name=name,
        out_shape=jax.ShapeDtypeStruct((N_DEV * m_per, n), v.dtype),
        in_specs=[pl.BlockSpec(memory_space=pltpu.VMEM)],
        out_specs=pl.BlockSpec(memory_space=pltpu.VMEM),
        scratch_shapes=[pltpu.SemaphoreType.DMA((7,)), pltpu.SemaphoreType.DMA((7,)), pltpu.SemaphoreType.DMA],
    )(v)


def _hbm_spec():
    return pl.BlockSpec(memory_space=pltpu.HBM)


def _sem_spec():
    return pl.BlockSpec(memory_space=pltpu.SEMAPHORE)


def _split_params():
    return pltpu.CompilerParams(has_side_effects=pltpu.SideEffectType.DATAFLOW_SIDE_EFFECTING)


def _hbm(a):
    return pltpu.with_memory_space_constraint(a, pltpu.HBM)


def gather_inplace(land, name):
    half = land.shape[1] // 2

    def body(land_in, land_ref, token, send_sems, recv_sems):
        del land_in
        x, y, c = _place()
        chips = _other_chips(x, y)

        def copy(k, chip_idx, core_half, to):
            rows = land_ref.at[chip_idx, pl.ds(core_half * half, half), :]
            return pltpu.make_async_remote_copy(
                src_ref=rows, dst_ref=rows, send_sem=send_sems.at[k], recv_sem=recv_sems.at[k],
                device_id=to, device_id_type=MESH)

        first = [copy(j, 2 * x + y, c, (px, py, c)) for j, (px, py) in enumerate(chips)]
        for cp in first:
            cp.start()
        passed = [copy(3 + j, 2 * px + py, c, (x, y, 1 - c)) for j, (px, py) in enumerate(chips)]
        for j, (px, py) in enumerate(chips):
            copy(j, 2 * px + py, c, (px, py, c)).wait_recv()
            passed[j].start()
        for j, (px, py) in enumerate(chips):
            copy(3 + j, 2 * px + py, 1 - c, (x, y, 1 - c)).wait_recv()
        for cp in first + passed:
            cp.wait_send()
        token[...] = jnp.zeros_like(token)

    return _call(
        body, name=name,
        out_shape=(jax.ShapeDtypeStruct(land.shape, land.dtype), jax.ShapeDtypeStruct((8, LANES), F32)),
        in_specs=[_hbm_spec()], out_specs=(_hbm_spec(), pl.BlockSpec(memory_space=pltpu.VMEM)),
        input_output_aliases={0: 0},
        scratch_shapes=[pltpu.SemaphoreType.DMA((6,)), pltpu.SemaphoreType.DMA((6,))],
    )(land)


def gather_start(land, name):
    def body(land_ref, send_sems, recv_sems, land_thru, token):
        del land_thru
        x, y, c = _place()
        for j, (px, py) in enumerate(_other_chips(x, y)):
            pltpu.make_async_remote_copy(
                src_ref=land_ref.at[2 * x + y], dst_ref=land_ref.at[2 * x + y],
                send_sem=send_sems.at[j], recv_sem=recv_sems.at[j], device_id=(px, py, c),
                device_id_type=MESH).start()
        token[...] = jnp.zeros_like(token)

    return _call(
        body, name=name,
        out_shape=(pltpu.SemaphoreType.DMA((3,)), pltpu.SemaphoreType.DMA((3,)),
                   pltpu.HBM(land.shape, land.dtype), jax.ShapeDtypeStruct((8, LANES), F32)),
        in_specs=(_hbm_spec(),),
        out_specs=(_sem_spec(), _sem_spec(), _hbm_spec(), pl.BlockSpec(memory_space=pltpu.VMEM)),
        input_output_aliases={0: 2}, compiler_params=_split_params(),
    )(_hbm(land))


def gather_wait(send_sems, recv_sems, land, after, name):
    def body(land_ref, send_sems, recv_sems, after_ref, land_out):
        del after_ref, land_out
        x, y, c = _place()
        for j, (px, py) in enumerate(_other_chips(x, y)):
            cp = pltpu.make_async_remote_copy(
                src_ref=land_ref.at[2 * x + y], dst_ref=land_ref.at[2 * px + py],
                send_sem=send_sems.at[j], recv_sem=recv_sems.at[j], device_id=(px, py, c), device_id_type=MESH)
            cp.wait_send()
            cp.wait_recv()

    return _call(
        body, name=name,
        out_shape=pltpu.HBM(land.shape, land.dtype),
        in_specs=(_hbm_spec(), _sem_spec(), _sem_spec(), pl.BlockSpec(memory_space=pl.ANY)),
        out_specs=_hbm_spec(), input_output_aliases={0: 0}, compiler_params=_split_params(),
    )(land, send_sems, recv_sems, after)


def exchange_start(parts, name):
    _, r, c_ = parts.shape

    def body(parts_ref, land_ref, send_sems, recv_sems, parts_thru, land_thru, token):
        del parts_thru, land_thru
        x, y, c = _place()
        for j, (px, py) in enumerate(_other_chips(x, y)):
            pltpu.make_async_remote_copy(
                src_ref=parts_ref.at[2 * px + py], dst_ref=land_ref.at[j],
                send_sem=send_sems.at[j], recv_sem=recv_sems.at[j], device_id=(px, py, c),
                device_id_type=MESH).start()
        token[...] = jnp.zeros_like(token)

    return _call(
        body, name=name,
        out_shape=(pltpu.SemaphoreType.DMA((3,)), pltpu.SemaphoreType.DMA((3,)),
                   pltpu.HBM(parts.shape, parts.dtype), pltpu.HBM((3, r, c_), parts.dtype),
                   jax.ShapeDtypeStruct((8, LANES), F32)),
        in_specs=(_hbm_spec(), _hbm_spec()),
        out_specs=(_sem_spec(), _sem_spec(), _hbm_spec(), _hbm_spec(), pl.BlockSpec(memory_space=pltpu.VMEM)),
        input_output_aliases={0: 2, 1: 3}, compiler_params=_split_params(),
    )(_hbm(parts), _hbm(lax.empty((3, r, c_), parts.dtype)))


def exchange_wait(send_sems, recv_sems, parts, land, after, name):
    def body(parts_ref, land_ref, send_sems, recv_sems, after_ref, parts_out, land_out):
        del after_ref, parts_out, land_out
        x, y, c = _place()
        for j, (px, py) in enumerate(_other_chips(x, y)):
            cp = pltpu.make_async_remote_copy(
                src_ref=parts_ref.at[2 * px + py], dst_ref=land_ref.at[j],
                send_sem=send_sems.at[j], recv_sem=recv_sems.at[j], device_id=(px, py, c), device_id_type=MESH)
            cp.wait_send()
            cp.wait_recv()

    return _call(
        body, name=name,
        out_shape=(pltpu.HBM(parts.shape, parts.dtype), pltpu.HBM(land.shape, land.dtype)),
        in_specs=(_hbm_spec(), _hbm_spec(), _sem_spec(), _sem_spec(), pl.BlockSpec(memory_space=pl.ANY)),
        out_specs=(_hbm_spec(), _hbm_spec()), input_output_aliases={0: 0, 1: 1},
        compiler_params=_split_params(),
    )(parts, land, send_sems, recv_sems, after)


def cast_into_slot(w, chip, after, name):
    r, c = w.shape
    tr = min(256, r)

    def body(s_ref, w_ref, after_ref, o_ref):
        del s_ref, after_ref
        o_ref[...] = w_ref[...].astype(BF16)

    return _call(
        body, name=name,
        grid_spec=pltpu.PrefetchScalarGridSpec(
            num_scalar_prefetch=1, grid=(r // tr,),
            in_specs=[pl.BlockSpec((tr, c), lambda i, s: (i, 0)), pl.BlockSpec(memory_space=pl.ANY)],
            out_specs=pl.BlockSpec((None, tr, c), lambda i, s: (s[0], i, 0))),
        out_shape=jax.ShapeDtypeStruct((N_CHIPS, r, c), BF16),
        compiler_params=_params(),
    )(chip.reshape(1).astype(jnp.int32), w, after)


def sum_parts(parts, land, chip, name):
    _, r, c = parts.shape
    tr = min(256, r)

    def body(s_ref, p_ref, l_ref, o_ref):
        del s_ref
        acc = p_ref[...].astype(F32) + l_ref[0].astype(F32)
        acc = acc + l_ref[1].astype(F32)
        o_ref[...] = acc + l_ref[2].astype(F32)

    return _call(
        body, name=name,
        grid_spec=pltpu.PrefetchScalarGridSpec(
            num_scalar_prefetch=1, grid=(r // tr,),
            in_specs=[pl.BlockSpec((None, tr, c), lambda i, s: (s[0], i, 0)),
                      pl.BlockSpec((3, tr, c), lambda i, s: (0, i, 0))],
            out_specs=pl.BlockSpec((tr, c), lambda i, s: (i, 0))),
        out_shape=jax.ShapeDtypeStruct((r, c), F32),
        compiler_params=_params(),
    )(chip.reshape(1).astype(jnp.int32), parts, land)


def swap_sibling(arrs, name):
    n = len(arrs)

    def body(*refs):
        ins, outs = refs[:n], refs[n:2 * n]
        send_sems, recv_sems = refs[2 * n:]
        x, y, c = _place()
        cps = []
        for w in range(n):
            cp = pltpu.make_async_remote_copy(
                src_ref=ins[w], dst_ref=outs[w], send_sem=send_sems.at[w], recv_sem=recv_sems.at[w],
                device_id=(x, y, 1 - c), device_id_type=MESH)
            cp.start()
            cps.append(cp)
        for cp in cps:
            cp.wait_recv()
        for cp in cps:
            cp.wait_send()

    return _call(
        body, name=name,
        out_shape=[jax.ShapeDtypeStruct(a.shape, a.dtype) for a in arrs],
        in_specs=[_hbm_spec()] * n, out_specs=[_hbm_spec()] * n,
        scratch_shapes=[pltpu.SemaphoreType.DMA((n,)), pltpu.SemaphoreType.DMA((n,))],
    )(*arrs)


def adamw_pair(pa, pb, w, m, v, name):
    r, c = w.shape
    tr = min(128, r)

    def body(pa_ref, pb_ref, w_ref, m_ref, v_ref, g_ref, d_ref, nm_ref, nv_ref):
        g = pa_ref[...] + pb_ref[...]
        d, nm, nv = _adamw(w_ref[...], g, m_ref[...], v_ref[...])
        g_ref[...] = g
        d_ref[...] = d
        nm_ref[...] = nm
        nv_ref[...] = nv

    spec = pl.BlockSpec((tr, c), lambda i: (i, 0))
    return _call(
        body, name=name, grid=(r // tr,),
        out_shape=[jax.ShapeDtypeStruct((r, c), F32)] * 4,
        in_specs=[spec] * 5, out_specs=[spec] * 4,
        compiler_params=_params(),
    )(pa, pb, w, m, v)


def small_update(gathered, w, m, v, name):
    def body(g_ref, w_ref, m_ref, v_ref, go_ref, d_ref, nm_ref, nv_ref):
        g = g_ref[0]
        for k in range(1, N_DEV):
            g = g + g_ref[k]
        d, nm, nv = _adamw(w_ref[...], g, m_ref[...], v_ref[...])
        go_ref[...] = g
        d_ref[...] = d
        nm_ref[...] = nm
        nv_ref[...] = nv

    return _call(
        body, name=name,
        out_shape=[jax.ShapeDtypeStruct(w.shape, F32)] * 4,
        compiler_params=_params(),
    )(gathered, w, m, v)


def ada_fwd(c_all, w_ada, b_cols, name):
    n_l, d, cols = w_ada.shape
    nb = c_all.shape[0]
    tn = 256

    def body(c_ref, w_ref, b_ref, o_ref):
        cv = c_ref[...]
        ca = (cv * _sigmoid(cv)).astype(BF16)
        o_ref[...] = _dot(ca, w_ref[...].astype(BF16), NN) + b_ref[...]

    return _call(
        body, name=name, grid=(n_l, cols // tn),
        out_shape=jax.ShapeDtypeStruct((n_l, nb, cols), F32),
        in_specs=[pl.BlockSpec((nb, d), lambda l, j: (0, 0)),
                  pl.BlockSpec((None, d, tn), lambda l, j: (l, 0, j)),
                  pl.BlockSpec((None, 1, tn), lambda l, j: (l, 0, j))],
        out_specs=pl.BlockSpec((None, nb, tn), lambda l, j: (l, 0, j)),
        compiler_params=_params(),
    )(c_all, w_ada, b_cols)


def ada_bwd(c_all, dmod_cols, w, m, v, name):
    n_l, d, cols = w.shape
    nb = c_all.shape[0]
    tn = 256

    def body(c_ref, dm_ref, w_ref, m_ref, v_ref, g_ref, d_ref, nm_ref, nv_ref):
        cv = c_ref[...]
        ca = (cv * _sigmoid(cv)).astype(BF16)
        g = _dot(ca, dm_ref[...].astype(BF16), TN)
        dl, nm, nv = _adamw(w_ref[...], g, m_ref[...], v_ref[...])
        g_ref[...] = g
        d_ref[...] = dl
        nm_ref[...] = nm
        nv_ref[...] = nv

    wspec = pl.BlockSpec((None, d, tn), lambda l, j: (l, 0, j))
    return _call(
        body, name=name, grid=(n_l, cols // tn),
        out_shape=[jax.ShapeDtypeStruct((n_l, d, cols), F32)] * 4,
        in_specs=[pl.BlockSpec((nb, d), lambda l, j: (0, 0)),
                  pl.BlockSpec((None, nb, tn), lambda l, j: (l, 0, j)),
                  wspec, wspec, wspec],
        out_specs=[wspec] * 4,
        compiler_params=_params(),
    )(c_all, dmod_cols, w, m, v)


def bias_update(dmod_all, w, m, v, name):
    def body(dm_ref, w_ref, m_ref, v_ref, g_ref, d_ref, nm_ref, nv_ref):
        g = jnp.sum(dm_ref[...], axis=0, keepdims=True)
        dl, nm, nv = _adamw(w_ref[...], g, m_ref[...], v_ref[...])
        g_ref[...] = g
        d_ref[...] = dl
        nm_ref[...] = nm
        nv_ref[...] = nv

    return _call(
        body, name=name,
        out_shape=[jax.ShapeDtypeStruct(w.shape, F32)] * 4,
        compiler_params=_params(),
    )(dmod_all, w, m, v)


def inproj_fwd(x, mod, ng, wg, seq, sectioned, name):
    m_rows, d = x.shape
    nsh, _, ns = wg.shape
    n = nsh * ns
    tm, tn = min(ROW_TILE, seq), 512
    per = ns // tn

    def body(x_ref, mod_ref, ng_ref, w_ref, proj_ref, h_ref):
        @pl.when(pl.program_id(1) == 0)
        def _():
            xv = x_ref[...]
            r = lax.rsqrt(jnp.mean(xv * xv, axis=-1, keepdims=True) + EPS)
            md = mod_ref[0]
            h = (xv * r * ng_ref[...]) * (1.0 + md[:, d:2 * d]) + md[:, :d]
            h_ref[...] = h.astype(BF16)
        proj_ref[...] = _dot(h_ref[...], w_ref[...], NN)

    if sectioned:
        proj_shape = (nsh, m_rows, ns)
        proj_spec = pl.BlockSpec((None, tm, tn), lambda i, j: (j // per, i, j % per))
    else:
        proj_shape = (m_rows, n)
        proj_spec = pl.BlockSpec((tm, tn), lambda i, j: (i, j))
    return _call(
        body, name=name, grid=(m_rows // tm, n // tn),
        out_shape=[jax.ShapeDtypeStruct(proj_shape, F32), jax.ShapeDtypeStruct((m_rows, d), BF16)],
        in_specs=[pl.BlockSpec((tm, d), lambda i, j: (i, 0)),
                  pl.BlockSpec((1, 1, 3 * d), lambda i, j: ((i * tm) // seq, 0, 0)),
                  pl.BlockSpec((1, d), lambda i, j: (0, 0)),
                  pl.BlockSpec((None, d, tn), lambda i, j: (j // per, 0, j % per))],
        out_specs=[proj_spec, pl.BlockSpec((tm, d), lambda i, j: (i, 0))],
        compiler_params=_params(),
    )(x, mod, ng, wg)


def outproj_fwd(y, w, x, mod, seq, name):
    m_rows, di = y.shape
    d = w.shape[1]
    tm = min(ROW_TILE, seq)

    def body(y_ref, w_ref, x_ref, mod_ref, xn_ref, out_ref):
        acc = _dot(y_ref[...], w_ref[...], NN)
        out_ref[...] = acc
        xn_ref[...] = x_ref[...] + mod_ref[0][:, 2 * d:] * acc

    row = pl.BlockSpec((tm, d), lambda i: (i, 0))
    return _call(
        body, name=name, grid=(m_rows // tm,),
        out_shape=[jax.ShapeDtypeStruct((m_rows, d), F32)] * 2,
        in_specs=[pl.BlockSpec((tm, di), lambda i: (i, 0)),
                  pl.BlockSpec((di, d), lambda i: (0, 0)),
                  row,
                  pl.BlockSpec((1, 1, 3 * d), lambda i: ((i * tm) // seq, 0, 0))],
        out_specs=[row, row],
        compiler_params=_params(),
    )(y, w, x, mod)


def outproj_bwd(dxo, out, mod, w, seq, name):
    m_rows, d = dxo.shape
    di = w.shape[0]
    nb = m_rows // seq
    tm, tn = min(ROW_TILE, seq), 512

    def body(dxo_ref, out_ref, mod_ref, w_ref, dy_ref, dout_ref, dgate_ref):
        i = pl.program_id(0)

        @pl.when(pl.program_id(1) == 0)
        def _():
            dx = dxo_ref[...]
            dout_ref[...] = (mod_ref[0][:, 2 * d:] * dx).astype(BF16)
            part = jnp.sum(dx * out_ref[...], axis=0, keepdims=True)

            @pl.when((i * tm) % seq == 0)
            def _():
                dgate_ref[0] = part

            @pl.when((i * tm) % seq != 0)
            def _():
                dgate_ref[0] = dgate_ref[0] + part

        dy_ref[...] = _dot(dout_ref[...], w_ref[...], NT)

    row = pl.BlockSpec((tm, d), lambda i, j: (i, 0))
    return _call(
        body, name=name, grid=(m_rows // tm, di // tn),
        out_shape=[jax.ShapeDtypeStruct((m_rows, di), F32), jax.ShapeDtypeStruct((m_rows, d), BF16),
                   jax.ShapeDtypeStruct((nb, 1, d), F32)],
        in_specs=[row, row,
                  pl.BlockSpec((1, 1, 3 * d), lambda i, j: ((i * tm) // seq, 0, 0)),
                  pl.BlockSpec((tn, d), lambda i, j: (j, 0))],
        out_specs=[pl.BlockSpec((tm, tn), lambda i, j: (i, j)), row,
                   pl.BlockSpec((1, 1, d), lambda i, j: ((i * tm) // seq, 0, 0))],
        compiler_params=_params(),
    )(dxo, out, mod, w)


def grad_w_out(y, dout, name):
    m_rows, di = y.shape
    d = dout.shape[1]
    tm, tk = min(512, m_rows), _col_tile(di)
    n_m = m_rows // tm

    def body(y_ref, do_ref, o_ref, acc_ref):
        mi = pl.program_id(1)
        part = _dot(y_ref[...], do_ref[...], TN)

        @pl.when(mi == 0)
        def _():
            acc_ref[...] = part

        @pl.when(mi != 0)
        def _():
            acc_ref[...] = acc_ref[...] + part

        @pl.when(mi == n_m - 1)
        def _():
            o_ref[...] = acc_ref[...].astype(BF16)

    return _call(
        body, name=name, grid=(di // tk, n_m),
        out_shape=jax.ShapeDtypeStruct((di, d), BF16),
        in_specs=[pl.BlockSpec((tm, tk), lambda j, mi: (mi, j)),
                  pl.BlockSpec((tm, d), lambda j, mi: (mi, 0))],
        out_specs=pl.BlockSpec((tk, d), lambda j, mi: (j, 0)),
        scratch_shapes=[pltpu.VMEM((tk, d), F32)],
        compiler_params=_params(),
    )(y, dout)


def grad_w_in(h, dproj, nsh, sectioned, name):
    m_rows, d = h.shape
    n = dproj.shape[0] * dproj.shape[2] if sectioned else dproj.shape[1]
    ns = n // nsh
    tm, tn = min(512, m_rows), _col_tile(ns)
    per = ns // tn
    n_m = m_rows // tm

    def body(h_ref, dp_ref, o_ref, acc_ref):
        mi = pl.program_id(1)
        part = _dot(h_ref[...], dp_ref[...], TN)

        @pl.when(mi == 0)
        def _():
            acc_ref[...] = part

        @pl.when(mi != 0)
        def _():
            acc_ref[...] = acc_ref[...] + part

        @pl.when(mi == n_m - 1)
        def _():
            o_ref[...] = acc_ref[...].astype(BF16)

    if sectioned:
        dp_spec = pl.BlockSpec((None, tm, tn), lambda j, mi: (j // per, mi, j % per))
    else:
        dp_spec = pl.BlockSpec((tm, tn), lambda j, mi: (mi, j))
    return _call(
        body, name=name, grid=(n // tn, n_m),
        out_shape=jax.ShapeDtypeStruct((nsh, d, ns), BF16),
        in_specs=[pl.BlockSpec((tm, d), lambda j, mi: (mi, 0)), dp_spec],
        out_specs=pl.BlockSpec((None, d, tn), lambda j, mi: (j // per, 0, j % per)),
        scratch_shapes=[pltpu.VMEM((d, tn), F32)],
        compiler_params=_params(),
    )(h, dproj)


def inproj_bwd(dproj, wg, x, dxo, mod, ng, seq, sectioned, name):
    m_rows, d = x.shape
    nsh, _, ns = wg.shape
    n = nsh * ns
    nb = m_rows // seq
    tm, tk = min(ROW_TILE, seq), 512
    per = ns // tk
    n_k = n // tk

    def body(dp_ref, w_ref, x_ref, dxo_ref, mod_ref, ng_ref, dxi_ref, dsh_ref, dsc_ref, dng_ref, acc_ref):
        i, k = pl.program_id(0), pl.program_id(1)
        part = _dot(dp_ref[...], w_ref[...], NT)

        @pl.when(k == 0)
        def _():
            acc_ref[...] = part

        @pl.when(k != 0)
        def _():
            acc_ref[...] = acc_ref[...] + part

        @pl.when(k == n_k - 1)
        def _():
            dh = acc_ref[...]
            xv = x_ref[...]
            r = lax.rsqrt(jnp.mean(xv * xv, axis=-1, keepdims=True) + EPS)
            xn = xv * r
            md = mod_ref[0]
            gain = ng_ref[...]
            p_shift = jnp.sum(dh, axis=0, keepdims=True)
            p_scale = jnp.sum(dh * (xn * gain), axis=0, keepdims=True)
            drn = dh * (1.0 + md[:, d:2 * d])
            p_ng = jnp.sum(drn * xn, axis=0, keepdims=True)
            dxn = drn * gain
            dx = r * (dxn - xn * jnp.mean(dxn * xn, axis=-1, keepdims=True))
            dxi_ref[...] = dxo_ref[...] + dx

            @pl.when((i * tm) % seq == 0)
            def _():
                dsh_ref[0] = p_shift
                dsc_ref[0] = p_scale

            @pl.when((i * tm) % seq != 0)
            def _():
                dsh_ref[0] = dsh_ref[0] + p_shift
                dsc_ref[0] = dsc_ref[0] + p_scale

            @pl.when(i == 0)
            def _():
                dng_ref[...] = p_ng

            @pl.when(i != 0)
            def _():
                dng_ref[...] = dng_ref[...] + p_ng

    if sectioned:
        dp_spec = pl.BlockSpec((None, tm, tk), lambda i, k: (k // per, i, k % per))
    else:
        dp_spec = pl.BlockSpec((tm, tk), lambda i, k: (i, k))
    row = pl.BlockSpec((tm, d), lambda i, k: (i, 0))
    per_seq = pl.BlockSpec((1, 1, d), lambda i, k: ((i * tm) // seq, 0, 0))
    return _call(
        body, name=name, grid=(m_rows // tm, n_k),
        out_shape=[jax.ShapeDtypeStruct((m_rows, d), F32), jax.ShapeDtypeStruct((nb, 1, d), F32),
                   jax.ShapeDtypeStruct((nb, 1, d), F32), jax.ShapeDtypeStruct((1, d), F32)],
        in_specs=[dp_spec,
                  pl.BlockSpec((None, d, tk), lambda i, k: (k // per, 0, k % per)),
                  row, row,
                  pl.BlockSpec((1, 1, 3 * d), lambda i, k: ((i * tm) // seq, 0, 0)),
                  pl.BlockSpec((1, d), lambda i, k: (0, 0))],
        out_specs=[row, per_seq, per_seq, pl.BlockSpec((1, d), lambda i, k: (0, 0))],
        scratch_shapes=[pltpu.VMEM((tm, d), F32)],
        compiler_params=_params(),
    )(dproj, wg, x, dxo, mod, ng)


def _sgu_stats(proj_ref, vg_ref, di, gd):
    s1 = jnp.zeros((SG_BLOCK, 1), F32)
    for g in range(SG_GROUPS):
        vg = _gelu(proj_ref[:, di + g * gd:di + (g + 1) * gd])
        vg_ref[:, g * gd:(g + 1) * gd] = vg
        s1 = s1 + jnp.sum(vg, axis=1, keepdims=True)
    mu = s1 / di
    s2 = jnp.zeros((SG_BLOCK, 1), F32)
    for g in range(SG_GROUPS):
        dv = vg_ref[:, g * gd:(g + 1) * gd] - mu
        s2 = s2 + jnp.sum(dv * dv, axis=1, keepdims=True)
    return mu, lax.rsqrt(s2 / di + EPS)


def sgu_fwd(proj, ln_gain, ln_bias, ws, bs, name):
    m_rows, n3 = proj.shape
    di = n3 // 3
    gd = di // SG_GROUPS

    def body(proj_ref, lg_ref, lb_ref, ws_ref, bs_ref, y_ref, wsm_ref, vg_ref):
        @pl.when(pl.program_id(0) == 0)
        def _():
            mask = _chunk_mask()
            for g in range(SG_GROUPS):
                wsm_ref[g] = jnp.where(mask, ws_ref[g], 0.0).astype(BF16)

        mu, rstd = _sgu_stats(proj_ref, vg_ref, di, gd)
        for g in range(SG_GROUPS):
            cs = slice(g * gd, (g + 1) * gd)
            vln = (vg_ref[:, cs] - mu) * rstd * lg_ref[:, cs] + lb_ref[:, cs]
            s = _dot(wsm_ref[g], vln.astype(BF16), NN) + bs_ref[g]
            u = _gelu(proj_ref[:, cs])
            gp = proj_ref[:, 2 * di + g * gd:2 * di + (g + 1) * gd]
            y_ref[:, cs] = (u * s * (gp * _sigmoid(gp))).astype(BF16)

    full = lambda shape: pl.BlockSpec(shape, lambda i: (0,) * len(shape))
    return _call(
        body, name=name, grid=(m_rows // SG_BLOCK,),
        out_shape=jax.ShapeDtypeStruct((m_rows, di), BF16),
        in_specs=[pl.BlockSpec((SG_BLOCK, n3), lambda i: (i, 0)),
                  full((1, di)), full((1, di)),
                  full((SG_GROUPS, SG_BLOCK, SG_BLOCK)), full((SG_GROUPS, SG_BLOCK, 1))],
        out_specs=pl.BlockSpec((SG_BLOCK, di), lambda i: (i, 0)),
        scratch_shapes=[pltpu.VMEM((SG_GROUPS, SG_BLOCK, SG_BLOCK), BF16), pltpu.VMEM((SG_BLOCK, di), F32)],
        compiler_params=_params(),
    )(proj, ln_gain, ln_bias, ws, bs)


def sgu_bwd(proj, dy, ln_gain, ln_bias, ws, bs, name):
    m_rows, n3 = proj.shape
    di = n3 // 3
    gd = di // SG_GROUPS
    n_i = m_rows // SG_BLOCK

    def body(proj_ref, dy_ref, lg_ref, lb_ref, ws_ref, bs_ref,
             dp_ref, dws_ref, dbs_ref, dlg_ref, dlb_ref, wsm_ref, vg_ref, dvh_ref):
        i = pl.program_id(0)

        @pl.when(i == 0)
        def _():
            mask = _chunk_mask()
            for g in range(SG_GROUPS):
                wsm_ref[g] = jnp.where(mask, ws_ref[g], 0.0).astype(BF16)
            dws_ref[...] = jnp.zeros_like(dws_ref)
            dbs_ref[...] = jnp.zeros_like(dbs_ref)
            dlg_ref[...] = jnp.zeros_like(dlg_ref)
            dlb_ref[...] = jnp.zeros_like(dlb_ref)

        mu, rstd = _sgu_stats(proj_ref, vg_ref, di, gd)
        m1 = jnp.zeros((SG_BLOCK, 1), F32)
        m2 = jnp.zeros((SG_BLOCK, 1), F32)
        for g in range(SG_GROUPS):
            cs = slice(g * gd, (g + 1) * gd)
            gs = slice(2 * di + g * gd, 2 * di + (g + 1) * gd)
            gain = lg_ref[:, cs]
            vhat = (vg_ref[:, cs] - mu) * rstd
            vln_b = (vhat * gain + lb_ref[:, cs]).astype(BF16)
            s = _dot(wsm_ref[g], vln_b, NN) + bs_ref[g]
            u, du = _gelu_and_grad(proj_ref[:, cs])
            sg, dsg = _silu_and_grad(proj_ref[:, gs])
            dyv = dy_ref[:, cs]
            dp_ref[:, cs] = (dyv * s * sg * du).astype(BF16)
            dp_ref[:, gs] = (dyv * u * s * dsg).astype(BF16)
            ds = dyv * u * sg
            ds_b = ds.astype(BF16)
            dws_ref[g] = dws_ref[g] + _dot(ds_b, vln_b, NT)
            dbs_ref[g] = dbs_ref[g] + jnp.sum(ds, axis=1, keepdims=True)
            dvln = _dot(wsm_ref[g], ds_b, TN)
            dlg_ref[:, cs] = dlg_ref[:, cs] + jnp.sum(dvln * vhat, axis=0, keepdims=True)
            dlb_ref[:, cs] = dlb_ref[:, cs] + jnp.sum(dvln, axis=0, keepdims=True)
            dvh = dvln * gain
            dvh_ref[:, cs] = dvh
            m1 = m1 + jnp.sum(dvh, axis=1, keepdims=True)
            m2 = m2 + jnp.sum(dvh * vhat, axis=1, keepdims=True)
        m1 = m1 / di
        m2 = m2 / di
        for g in range(SG_GROUPS):
            cs = slice(g * gd, (g + 1) * gd)
            vs = slice(di + g * gd, di + (g + 1) * gd)
            vhat = (vg_ref[:, cs] - mu) * rstd
            dvg = rstd * (dvh_ref[:, cs] - m1 - vhat * m2)
            _, dgel = _gelu_and_grad(proj_ref[:, vs])
            dp_ref[:, vs] = (dvg * dgel).astype(BF16)

        @pl.when(i == n_i - 1)
        def _():
            mask = _chunk_mask()
            for g in range(SG_GROUPS):
                dws_ref[g] = jnp.where(mask, dws_ref[g], 0.0)

    full = lambda shape: pl.BlockSpec(shape, lambda i: (0,) * len(shape))
    return _call(
        body, name=name, grid=(n_i,),
        out_shape=[jax.ShapeDtypeStruct((m_rows, n3), BF16),
                   jax.ShapeDtypeStruct((SG_GROUPS, SG_BLOCK, SG_BLOCK), F32),
                   jax.ShapeDtypeStruct((SG_GROUPS, SG_BLOCK, 1), F32),
                   jax.ShapeDtypeStruct((1, di), F32), jax.ShapeDtypeStruct((1, di), F32)],
        in_specs=[pl.BlockSpec((SG_BLOCK, n3), lambda i: (i, 0)),
                  pl.BlockSpec((SG_BLOCK, di), lambda i: (i, 0)),
                  full((1, di)), full((1, di)),
                  full((SG_GROUPS, SG_BLOCK, SG_BLOCK)), full((SG_GROUPS, SG_BLOCK, 1))],
        out_specs=[pl.BlockSpec((SG_BLOCK, n3), lambda i: (i, 0)),
                   full((SG_GROUPS, SG_BLOCK, SG_BLOCK)), full((SG_GROUPS, SG_BLOCK, 1)),
                   full((1, di)), full((1, di))],
        scratch_shapes=[pltpu.VMEM((SG_GROUPS, SG_BLOCK, SG_BLOCK), BF16),
                        pltpu.VMEM((SG_BLOCK, di), F32), pltpu.VMEM((SG_BLOCK, di), F32)],
        compiler_params=_params(),
    )(proj, dy, ln_gain, ln_bias, ws, bs)


def _lower_bound(lbraw):
    mx = jnp.maximum(lbraw[0:1, :], lbraw[1:2, :])
    e0 = jnp.exp(lbraw[0:1, :] - mx)
    e1 = jnp.exp(lbraw[1:2, :] - mx)
    p0 = e0 / (e0 + e1)
    p1 = e1 / (e0 + e1)
    return (p0 + p1) - p0, p0, p1


def _tri(lower):
    r = lax.broadcasted_iota(jnp.int32, (CHUNK, CHUNK), 0)
    c = lax.broadcasted_iota(jnp.int32, (CHUNK, CHUNK), 1)
    return ((r >= c) if lower else (c >= r)).astype(F32)


def _row(a, idx):
    r = lax.broadcasted_iota(jnp.int32, a.shape, 0)
    return jnp.sum(jnp.where(r == idx, a, 0.0), axis=0, keepdims=True)


def _hgrn_gates(qp, fp, lb, tri):
    sgm = _sigmoid(fp)
    f = lb + (1.0 - lb) * sgm
    k = 1.0 - f
    a = _dot(tri, jnp.log(f), NN, precision=lax.Precision.HIGHEST)
    a_mid = _row(a, CHUNK // 2 - 1)
    a_last = _row(a, CHUNK - 1)
    q, dq = _silu_and_grad(qp)
    e1, e2, e3, e4 = jnp.exp(a - a_mid), jnp.exp(a_mid - a), jnp.exp(a), jnp.exp(a_last - a)
    return dict(sgm=sgm, f=f, k=k, q=q, dq=dq, e1=e1, e2=e2, e3=e3, e4=e4, dec=jnp.exp(a_last),
                q_in=q * e1, k_in=k * e2, q_out=q * e3, k_out=k * e4)


def _causal():
    r = lax.broadcasted_iota(jnp.int32, (CHUNK, CHUNK), 0)
    c = lax.broadcasted_iota(jnp.int32, (CHUNK, CHUNK), 1)
    return r >= c


def hgrn_fwd(proj4, lbraw, gn, seq, name):
    _, m_rows, di = proj4.shape
    nb, nh, nc = m_rows // seq, di // HEAD_DIM, seq // CHUNK
    rows = min(HG_ROWS, seq)
    wide = HG_WIDE * HEAD_DIM
    ns, cpb = seq // rows, rows // CHUNK

    def body(p_ref, lb_ref, gn_ref, y_ref, sts_ref, st_ref):
        @pl.when(pl.program_id(2) == 0)
        def _():
            st_ref[...] = jnp.zeros_like(st_ref)

        tri = _tri(True)
        causal = _causal()
        gain = gn_ref[...]
        lbs = [_lower_bound(lb_ref[:, j * HEAD_DIM:(j + 1) * HEAD_DIM])[0] for j in range(HG_WIDE)]

        units = [(n, j) for n in range(cpb) for j in range(HG_WIDE)]
        rs = lambda n: slice(n * CHUNK, (n + 1) * CHUNK)
        cs = lambda j: slice(j * HEAD_DIM, (j + 1) * HEAD_DIM)
        gates, v_b, sc_b, kv, o_in, o_x = {}, {}, {}, {}, {}, {}
        for n, j in units:
            gates[n, j] = _hgrn_gates(p_ref[0, rs(n), cs(j)], p_ref[1, rs(n), cs(j)], lbs[j], tri)
            v_b[n, j] = p_ref[2, rs(n), cs(j)].astype(BF16)
        for u in units:
            t = gates[u]
            sc_b[u] = jnp.where(causal, _dot(t["q_in"].astype(BF16), t["k_in"].astype(BF16), NT), 0.0).astype(BF16)
            kv[u] = _dot(v_b[u], t["k_out"].astype(BF16), TN)
        for u in units:
            o_in[u] = _dot(sc_b[u], v_b[u], NN)
        for j in range(HG_WIDE):
            st = st_ref[j]
            for n in range(cpb):
                sts_ref[n, :, cs(j)] = st
                o_x[n, j] = _dot(gates[n, j]["q_out"].astype(BF16), st.astype(BF16), NT)
                st = st * gates[n, j]["dec"] + kv[n, j]
            st_ref[j] = st
        for n, j in units:
            o = o_in[n, j] + o_x[n, j]
            r = lax.rsqrt(jnp.mean(o * o, axis=-1, keepdims=True) + EPS)
            gp = p_ref[3, rs(n), cs(j)]
            y_ref[rs(n), cs(j)] = ((o * r * gain) * (gp * _sigmoid(gp))).astype(BF16)

    return _call(
        body, name=name, grid=(nh // HG_WIDE, nb, ns),
        out_shape=[jax.ShapeDtypeStruct((m_rows, di), BF16),
                   jax.ShapeDtypeStruct((nb * nc, HEAD_DIM, di), F32)],
        in_specs=[pl.BlockSpec((4, rows, wide), lambda hg, b, s: (0, b * ns + s, hg)),
                  pl.BlockSpec((2, wide), lambda hg, b, s: (0, hg)),
                  pl.BlockSpec((1, HEAD_DIM), lambda hg, b, s: (0, 0))],
        out_specs=[pl.BlockSpec((rows, wide), lambda hg, b, s: (b * ns + s, hg)),
                   pl.BlockSpec((cpb, HEAD_DIM, wide), lambda hg, b, s: (b * ns + s, 0, hg))],
        scratch_shapes=[pltpu.VMEM((HG_WIDE, HEAD_DIM, HEAD_DIM), F32)],
        compiler_params=_params(),
    )(proj4, lbraw, gn)


def hgrn_bwd(proj4, dy, sts, lbraw, gn, seq, name):
    _, m_rows, di = proj4.shape
    nb, nh, nc = m_rows // seq, di // HEAD_DIM, seq // CHUNK
    rows = min(HG_ROWS, seq)
    wide = HG_WIDE * HEAD_DIM
    ns, cpb = seq // rows, rows // CHUNK
    n_hg = nh // HG_WIDE

    def body(p_ref, dy_ref, sts_ref, lb_ref, gn_ref, dp_ref, dlb_ref, dgn_ref, dst_ref, lbacc_ref, gnacc_ref):
        hg, b, s = pl.program_id(0), pl.program_id(1), pl.program_id(2)
        tri, triu = _tri(True), _tri(False)
        causal = _causal()
        gain = gn_ref[...]
        first = (b == 0) & (s == 0)

        @pl.when((hg == 0) & first)
        def _():
            gnacc_ref[...] = jnp.zeros_like(gnacc_ref)

        @pl.when(first)
        def _():
            lbacc_ref[...] = jnp.zeros_like(lbacc_ref)

        @pl.when(s == 0)
        def _():
            dst_ref[...] = jnp.zeros_like(dst_ref)

        units = [(n, j) for n in range(cpb) for j in range(HG_WIDE)]
        rs = lambda n: slice(n * CHUNK, (n + 1) * CHUNK)
        cs = lambda j: slice(j * HEAD_DIM, (j + 1) * HEAD_DIM)
        lbs = [_lower_bound(lb_ref[:, cs(j)])[0] for j in range(HG_WIDE)]
        gates, v_b, st_b, sc_b, o, do_b = {}, {}, {}, {}, {}, {}
        dq_out, dsc_b, dv, g_st, dq_in, dk_in, dst_at, dk_out, ddec = {}, {}, {}, {}, {}, {}, {}, {}, {}
        for n, j in units:
            gates[n, j] = _hgrn_gates(p_ref[0, rs(n), cs(j)], p_ref[1, rs(n), cs(j)], lbs[j], tri)
            v_b[n, j] = p_ref[2, rs(n), cs(j)].astype(BF16)
            st_b[n, j] = sts_ref[n, :, cs(j)].astype(BF16)
        for u in units:
            t = gates[u]
            sc_b[u] = jnp.where(causal, _dot(t["q_in"].astype(BF16), t["k_in"].astype(BF16), NT), 0.0).astype(BF16)
        for u in units:
            o[u] = _dot(sc_b[u], v_b[u], NN) + _dot(gates[u]["q_out"].astype(BF16), st_b[u], NT)
        for n, j in units:
            ov = o[n, j]
            r = lax.rsqrt(jnp.mean(ov * ov, axis=-1, keepdims=True) + EPS)
            ohat = ov * r
            sg, dsg = _silu_and_grad(p_ref[3, rs(n), cs(j)])
            dyv = dy_ref[rs(n), cs(j)]
            dp_ref[3, rs(n), cs(j)] = (dyv * (ohat * gain) * dsg).astype(BF16)
            d_on = dyv * sg
            gnacc_ref[:, cs(j)] = gnacc_ref[:, cs(j)] + jnp.sum(d_on * ohat, axis=0, keepdims=True)
            dohat = d_on * gain
            do_b[n, j] = (r * (dohat - ohat * jnp.mean(dohat * ohat, axis=-1, keepdims=True))).astype(BF16)
        for u in units:
            dq_out[u] = _dot(do_b[u], st_b[u], NN)
            dsc_b[u] = jnp.where(causal, _dot(do_b[u], v_b[u], NT), 0.0).astype(BF16)
            dv[u] = _dot(sc_b[u], do_b[u], TN)
            g_st[u] = _dot(do_b[u], gates[u]["q_out"].astype(BF16), TN)
        for u in units:
            dq_in[u] = _dot(dsc_b[u], gates[u]["k_in"].astype(BF16), NN)
            dk_in[u] = _dot(dsc_b[u], gates[u]["q_in"].astype(BF16), TN)
        for j in range(HG_WIDE):
            dst = dst_ref[j]
            for n in reversed(range(cpb)):
                dst_at[n, j] = dst
                dst = dst * gates[n, j]["dec"] + g_st[n, j]
            dst_ref[j] = dst
        for n, j in units:
            dst = dst_at[n, j]
            dst_b = dst.astype(BF16)
            dk_out[n, j] = _dot(v_b[n, j], dst_b, NN)
            dv[n, j] = dv[n, j] + _dot(gates[n, j]["k_out"].astype(BF16), dst_b, NT)
            ddec[n, j] = jnp.sum(dst * sts_ref[n, :, cs(j)], axis=0, keepdims=True)
        for n, j in units:
            t = gates[n, j]
            dp_ref[2, rs(n), cs(j)] = dv[n, j].astype(BF16)
            dq = dq_in[n, j] * t["e1"] + dq_out[n, j] * t["e3"]
            dk = dk_in[n, j] * t["e2"] + dk_out[n, j] * t["e4"]
            w_in = dq_in[n, j] * t["q_in"] - dk_in[n, j] * t["k_in"]
            w_out = dk_out[n, j] * t["k_out"]
            da = w_in + dq_out[n, j] * t["q_out"] - w_out
            da_mid = -jnp.sum(w_in, axis=0, keepdims=True)
            da_last = jnp.sum(w_out, axis=0, keepdims=True) + ddec[n, j] * t["dec"]
            rid = lax.broadcasted_iota(jnp.int32, da.shape, 0)
            da = da + jnp.where(rid == CHUNK // 2 - 1, da_mid, 0.0) + jnp.where(rid == CHUNK - 1, da_last, 0.0)
            dlf = _dot(triu, da, NN, precision=lax.Precision.HIGHEST)
            df = dlf / t["f"] - dk
            sgm = t["sgm"]
            dp_ref[1, rs(n), cs(j)] = (df * (1.0 - lbs[j]) * sgm * (1.0 - sgm)).astype(BF16)
            lbacc_ref[:, cs(j)] = lbacc_ref[:, cs(j)] + jnp.sum(df * (1.0 - sgm), axis=0, keepdims=True)
            dp_ref[0, rs(n), cs(j)] = (dq * t["dq"]).astype(BF16)

        @pl.when((b == nb - 1) & (s == ns - 1))
        def _():
            for j in range(HG_WIDE):
                cs = slice(j * HEAD_DIM, (j + 1) * HEAD_DIM)
                _, p0, p1 = _lower_bound(lb_ref[:, cs])
                acc = lbacc_ref[:, cs]
                dlb_ref[0:1, cs] = -acc * p0 * p1
                dlb_ref[1:2, cs] = acc * p1 * (1.0 - p1)

        @pl.when((hg == n_hg - 1) & (b == nb - 1) & (s == ns - 1))
        def _():
            tot = gnacc_ref[:, 0:HEAD_DIM]
            for j in range(1, HG_WIDE):
                tot = tot + gnacc_ref[:, j * HEAD_DIM:(j + 1) * HEAD_DIM]
            dgn_ref[...] = tot

    blk = lambda hg, b, s: b * ns + (ns - 1 - s)
    return _call(
        body, name=name, grid=(n_hg, nb, ns),
        out_shape=[jax.ShapeDtypeStruct((4, m_rows, di), BF16), jax.ShapeDtypeStruct((2, di), F32),
                   jax.ShapeDtypeStruct((1, HEAD_DIM), F32)],
        in_specs=[pl.BlockSpec((4, rows, wide), lambda hg, b, s: (0, blk(hg, b, s), hg)),
                  pl.BlockSpec((rows, wide), lambda hg, b, s: (blk(hg, b, s), hg)),
                  pl.BlockSpec((cpb, HEAD_DIM, wide), lambda hg, b, s: (blk(hg, b, s), 0, hg)),
                  pl.BlockSpec((2, wide), lambda hg, b, s: (0, hg)),
                  pl.BlockSpec((1, HEAD_DIM), lambda hg, b, s: (0, 0))],
        out_specs=[pl.BlockSpec((4, rows, wide), lambda hg, b, s: (0, blk(hg, b, s), hg)),
                   pl.BlockSpec((2, wide), lambda hg, b, s: (0, hg)),
                   pl.BlockSpec((1, HEAD_DIM), lambda hg, b, s: (0, 0))],
        scratch_shapes=[pltpu.VMEM((HG_WIDE, HEAD_DIM, HEAD_DIM), F32), pltpu.VMEM((1, wide), F32),
                        pltpu.VMEM((1, wide), F32)],
        compiler_params=_params(),
    )(proj4, dy, sts, lbraw, gn)


def final_loss(x, fg, target, name):
    m_rows, d = x.shape
    tm = min(512, m_rows)

    def body(x_ref, fg_ref, t_ref, loss_ref, dx_ref, dfg_ref):
        i = pl.program_id(0)
        xv = x_ref[...]
        gain = fg_ref[...]
        r = lax.rsqrt(jnp.mean(xv * xv, axis=-1, keepdims=True) + EPS)
        xn = xv * r
        e = xn * gain - t_ref[...]
        part = 0.5 * jnp.sum(jnp.mean(e * e, axis=-1, keepdims=True), axis=0, keepdims=True)
        dyv = e / d
        p_fg = jnp.sum(dyv * xn, axis=0, keepdims=True)
        dxn = dyv * gain
        dx_ref[...] = r * (dxn - xn * jnp.mean(dxn * xn, axis=-1, keepdims=True))

        @pl.when(i == 0)
        def _():
            loss_ref[...] = part
            dfg_ref[...] = p_fg

        @pl.when(i != 0)
        def _():
            loss_ref[...] = loss_ref[...] + part
            dfg_ref[...] = dfg_ref[...] + p_fg

    row = pl.BlockSpec((tm, d), lambda i: (i, 0))
    return _call(
        body, name=name, grid=(m_rows // tm,),
        out_shape=[jax.ShapeDtypeStruct((1, 1), F32), jax.ShapeDtypeStruct((m_rows, d), F32),
                   jax.ShapeDtypeStruct((1, d), F32)],
        in_specs=[row, pl.BlockSpec((1, d), lambda i: (0, 0)), row],
        out_specs=[pl.BlockSpec((1, 1), lambda i: (0, 0)), row, pl.BlockSpec((1, d), lambda i: (0, 0))],
        compiler_params=_params(),
    )(x, fg, target)


def _pack(parts):
    flat = jnp.concatenate([p.reshape(-1) for p in parts])
    pad = (-flat.shape[0]) % (8 * LANES)
    return jnp.pad(flat, (0, pad)).reshape(-1, LANES)


def _unpack(packed, like):
    flat = packed.reshape(-1)
    out, off = [], 0
    for a in like:
        out.append(flat[off:off + a.size].reshape(a.shape))
        off += a.size
    return out


def kernel(x, c, norm_gain, w_ada, b_ada, a_w_in, a_ln_gain, a_ln_bias, a_w_s, a_b_s, a_w_out, b_w_in, b_lower_bounds, b_gn_gain, b_w_out, final_gain, loss_target, m_norm_gain, m_w_ada, m_b_ada, m_a_w_in, m_a_ln_gain, m_a_ln_bias, m_a_w_s, m_a_b_s, m_a_w_out, m_b_w_in, m_b_lower_bounds, m_b_gn_gain, m_b_w_out, m_final_gain, v_norm_gain, v_w_ada, v_b_ada, v_a_w_in, v_a_ln_gain, v_a_ln_bias, v_a_w_s, v_a_b_s, v_a_w_out, v_b_w_in, v_b_lower_bounds, v_b_gn_gain, v_b_w_out, v_final_gain):
    nb, seq, d = x.shape
    m_rows = nb * seq
    n_l = w_ada.shape[0]
    ada_cols = w_ada.shape[2]
    px, py, pc = _place()
    chip = 2 * px + py
    dev = 2 * chip + pc

    c_all = allgather_small(c.reshape(-1, LANES), "gather_c").reshape(N_DEV * nb, d)
    b_cols = lax.dynamic_slice_in_dim(b_ada, chip * ada_cols, ada_cols, axis=1).reshape(n_l, 1, ada_cols)
    mod_cols = ada_fwd(c_all, w_ada, b_cols, "ada_fwd")
    mod_g = allgather_small(mod_cols.reshape(-1, LANES), "gather_mod")
    mod_g = mod_g.reshape(N_CHIPS, 2, n_l, N_DEV * nb, ada_cols)[:, 0]
    mod_all = jnp.transpose(mod_g, (1, 2, 0, 3)).reshape(n_l, N_DEV * nb, 3 * d)
    mod_mine = lax.dynamic_slice_in_dim(mod_all, dev * nb, nb, axis=1)
    mod0 = mod_mine[0].reshape(nb, 1, 3 * d)
    mod1 = mod_mine[1].reshape(nb, 1, 3 * d)

    wa_in, tok_a_in = gather_inplace(cast_into_slot(a_w_in[0], chip, c, "cast_a_in"), "gather_a_in")
    s_ao = gather_start(cast_into_slot(a_w_out[0], chip, tok_a_in, "cast_a_out"), "gather_a_out_start")
    s_bi = gather_start(cast_into_slot(b_w_in[0], chip, s_ao[3], "cast_b_in"), "gather_b_in_start")
    s_bo = gather_start(cast_into_slot(b_w_out[0], chip, s_bi[3], "cast_b_out"), "gather_b_out_start")
    di = a_w_out.shape[1] * N_CHIPS

    x0 = x.reshape(m_rows, d)
    tgt = loss_target.reshape(m_rows, d)
    ng0 = norm_gain[0:1] + (s_ao[3][0, 0] + s_bi[3][0, 0] + s_bo[3][0, 0])
    ng1 = norm_gain[1:2]
    bs_col = a_b_s[0].reshape(SG_GROUPS, SG_BLOCK, 1)
    proj_a, h_a = inproj_fwd(x0, mod0, ng0, wa_in, seq, False, "a_inproj")
    y_a = sgu_fwd(proj_a, a_ln_gain, a_ln_bias, a_w_s[0], bs_col, "a_sgu")
    wa_out = gather_wait(*s_ao[:3], y_a, "gather_a_out_wait").reshape(di, d)
    x1, out_a = outproj_fwd(y_a, wa_out, x0, mod0, seq, "a_outproj")
    wb_in = gather_wait(*s_bi[:3], out_a, "gather_b_in_wait")
    proj_b, h_b = inproj_fwd(x1, mod1, ng1, wb_in, seq, True, "b_inproj")
    y_b, sts_b = hgrn_fwd(proj_b, b_lower_bounds, b_gn_gain, seq, "b_hgrn")
    wb_out = gather_wait(*s_bo[:3], y_b, "gather_b_out_wait").reshape(di, d)
    x2, out_b = outproj_fwd(y_b, wb_out, x1, mod1, seq, "b_outproj")
    loss_part, dx2, dfg = final_loss(x2, final_gain.reshape(1, d), tgt, "loss_head")
    loss = lax.psum(loss_part[0, 0], ("x", "y", "c"))

    shard_rows = di // N_CHIPS
    dy_b, dout_b, dgate1 = outproj_bwd(dx2, out_b, mod1, wb_out, seq, "b_outproj_bwd")
    gwb_out = grad_w_out(y_b, dout_b, "b_grad_w_out").reshape(N_CHIPS, shard_rows, d)
    e_bo = exchange_start(gwb_out, "exchange_b_out_start")
    dproj_b, dlb, dgn = hgrn_bwd(proj_b, dy_b, sts_b, b_lower_bounds, b_gn_gain + e_bo[4][0, 0], seq, "b_hgrn_bwd")
    e_bi = exchange_start(grad_w_in(h_b, dproj_b, N_CHIPS, True, "b_grad_w_in"), "exchange_b_in_start")
    dx1, dshift1, dscale1, dng1 = inproj_bwd(
        dproj_b, wb_in, x1, dx2, mod1, ng1 + e_bi[4][0, 0], seq, True, "b_inproj_bwd")

    dy_a, dout_a, dgate0 = outproj_bwd(dx1, out_a, mod0, wa_out, seq, "a_outproj_bwd")
    gwa_out = grad_w_out(y_a, dout_a, "a_grad_w_out").reshape(N_CHIPS, shard_rows, d)
    e_ao = exchange_start(gwa_out, "exchange_a_out_start")
    dproj_a, dws, dbs, dlg, dlbias = sgu_bwd(
        proj_a, dy_a, a_ln_gain + e_ao[4][0, 0], a_ln_bias, a_w_s[0], bs_col, "a_sgu_bwd")
    e_ai = exchange_start(grad_w_in(h_a, dproj_a, N_CHIPS, False, "a_grad_w_in"), "exchange_a_in_start")
    dx0, dshift0, dscale0, dng0 = inproj_bwd(
        dproj_a, wa_in, x0, dx1, mod0, norm_gain[0:1] + e_ai[4][0, 0], seq, False, "a_inproj_bwd")
    grad_x = dx0.reshape(nb, seq, d)

    def finish(ex, after, w, m, v, nm):
        parts_thru, land = exchange_wait(ex[0], ex[1], ex[2], ex[3], after, "exchange_" + nm + "_wait")
        mine = sum_parts(parts_thru, land, chip, "sum_" + nm)
        (other,) = swap_sibling([mine], "swap_" + nm)
        res = adamw_pair(mine, other, w[0], m[0], v[0], "adamw_" + nm)
        return [r.reshape(w.shape) for r in res]

    gb_out, db_out, mb_out, vb_out = finish(e_bo, dx0, b_w_out, m_b_w_out, v_b_w_out, "b_out")
    gb_in, db_in, mb_in, vb_in = finish(e_bi, gb_out, b_w_in, m_b_w_in, v_b_w_in, "b_in")
    ga_out, da_out, ma_out, va_out = finish(e_ao, gb_in, a_w_out, m_a_w_out, v_a_w_out, "a_out")

    dmod = jnp.concatenate([dshift0, dscale0, dgate0, dshift1, dscale1, dgate1], axis=2)
    dmod_all = allgather_small(dmod.reshape(-1, LANES), "gather_dmod").reshape(N_DEV * nb, n_l, 3 * d)
    dmod_cols = lax.dynamic_slice_in_dim(dmod_all, chip * ada_cols, ada_cols, axis=2)
    dmod_cols = jnp.transpose(dmod_cols, (1, 0, 2))
    g_wada, d_wada, m_wada, v_wada = ada_bwd(c_all, dmod_cols, w_ada, m_w_ada, v_w_ada, "ada_bwd")
    flat = lambda a: a.reshape(1, -1)
    g_bada, d_bada, m_bada, v_bada = [
        r.reshape(b_ada.shape) for r in
        bias_update(dmod_all.reshape(N_DEV * nb, n_l * 3 * d), flat(b_ada), flat(m_b_ada), flat(v_b_ada), "bias_update")]

    small_w = [norm_gain, a_ln_gain, a_ln_bias, a_w_s, a_b_s, b_lower_bounds, b_gn_gain, final_gain]
    small_m = [m_norm_gain, m_a_ln_gain, m_a_ln_bias, m_a_w_s, m_a_b_s, m_b_lower_bounds, m_b_gn_gain, m_final_gain]
    small_v = [v_norm_gain, v_a_ln_gain, v_a_ln_bias, v_a_w_s, v_a_b_s, v_b_lower_bounds, v_b_gn_gain, v_final_gain]
    small_g = [jnp.concatenate([dng0, dng1], axis=0), dlg, dlbias, dws, dbs, dlb, dgn, dfg]
    packed_g = _pack(small_g)
    rows = packed_g.shape[0]
    gathered = allgather_small(packed_g, "gather_small").reshape(N_DEV, rows, LANES)
    res = small_update(gathered, _pack(small_w), _pack(small_m), _pack(small_v), "small_update")
    sg, sd, sm, sv = [_unpack(r, small_w) for r in res]
    ga_in, da_in, ma_in, va_in = finish(e_ai, res[0], a_w_in, m_a_w_in, v_a_w_in, "a_in")

    def order(ng, wada, bada, ain, sm_rest, aout, bin_, bout):
        lg, lbi, ws_, bs_, lbd, gn_, fg_ = sm_rest
        return [ng, wada, bada, ain, lg, lbi, ws_, bs_, aout, bin_, lbd, gn_, bout, fg_]

    grads = order(sg[0], g_wada, g_bada, ga_in, sg[1:], ga_out, gb_in, gb_out)
    deltas = order(sd[0], d_wada, d_bada, da_in, sd[1:], da_out, db_in, db_out)
    new_m = order(sm[0], m_wada, m_bada, ma_in, sm[1:], ma_out, mb_in, mb_out)
    new_v = order(sv[0], v_wada, v_bada, va_in, sv[1:], va_out, vb_in, vb_out)
    return (loss, grad_x, *grads, *deltas, *new_m, *new_v)
```

```python
import functools

import jax
import jax.numpy as jnp
from jax import lax
from jax.experimental import pallas as pl
from jax.experimental.pallas import tpu as pltpu

F32 = jnp.float32
BF16 = jnp.bfloat16
EPS = 1e-6
CHUNK = 64
SG_BLOCK = 128
SG_GROUPS = 8
HEAD_DIM = 128
HG_WIDE = 8
HG_ROWS = 128
N_CHIPS = 4
N_DEV = 8
LANES = 128
ADAM_LR = 0.001
ADAM_B1 = 0.9
ADAM_B2 = 0.999
ADAM_EPS = 1e-08
ADAM_WD = 0.01
ADAM_STEP = 10
GELU_C0 = 0.7978845608028654
GELU_C1 = 0.044715
MESH = pl.DeviceIdType.MESH
VMEM_LIMIT = 56 * 1024 * 1024


ROW_TILE = 1024


def _col_tile(n):
    return next(t for t in (1024, 768, 512, 256) if n % t == 0)


def _call(body, **kw):
    return pl.pallas_call(body, **kw)


def _params(**kw):
    return pltpu.CompilerParams(vmem_limit_bytes=VMEM_LIMIT, **kw)


def _sigmoid(x):
    return 1.0 / (1.0 + jnp.exp(-x))


def _silu_and_grad(x):
    s = _sigmoid(x)
    return x * s, s * (1.0 + x * (1.0 - s))


def _gelu(x):
    return 0.5 * x * (1.0 + jnp.tanh(GELU_C0 * (x + GELU_C1 * x * x * x)))


def _gelu_and_grad(x):
    t = jnp.tanh(GELU_C0 * (x + GELU_C1 * x * x * x))
    g = 0.5 * x * (1.0 + t)
    dg = 0.5 * (1.0 + t) + 0.5 * x * (1.0 - t * t) * (GELU_C0 * (1.0 + 3.0 * GELU_C1 * x * x))
    return g, dg


def _dot(a, b, dims, precision=None):
    return lax.dot_general(a, b, (dims, ((), ())), precision=precision, preferred_element_type=F32)


NN = ((1,), (0,))
NT = ((1,), (1,))
TN = ((0,), (0,))


def _adamw(w, g, m, v):
    m = ADAM_B1 * m + (1.0 - ADAM_B1) * g
    v = ADAM_B2 * v + (1.0 - ADAM_B2) * (g * g)
    m_hat = m / (1.0 - ADAM_B1 ** ADAM_STEP)
    v_hat = v / (1.0 - ADAM_B2 ** ADAM_STEP)
    delta = -ADAM_LR * (m_hat / (jnp.sqrt(v_hat) + ADAM_EPS) + ADAM_WD * w)
    return delta, m, v


def _chunk_mask():
    r = lax.broadcasted_iota(jnp.int32, (SG_BLOCK, SG_BLOCK), 0)
    c = lax.broadcasted_iota(jnp.int32, (SG_BLOCK, SG_BLOCK), 1)
    return (c // CHUNK) <= (r // CHUNK)


def _place():
    return lax.axis_index("x"), lax.axis_index("y"), lax.axis_index("c")


def _other_chips(x, y):
    return [(1 - x, y), (x, 1 - y), (1 - x, 1 - y)]


def allgather_small(v, name):
    m_per, n = v.shape

    def body(x_ref, out_ref, send_sems, recv_sems, local_sem):
        x, y, c = _place()
        me, sibling = (x, y, c), (x, y, 1 - c)
        chips = _other_chips(x, y)

        def rows(px, py, pc):
            return out_ref.at[pl.ds((4 * px + 2 * py + pc) * m_per, m_per), :]

        def copy(k, block, to, src=None):
            return pltpu.make_async_remote_copy(
                src_ref=rows(*block) if src is None else src, dst_ref=rows(*block),
                send_sem=send_sems.at[k], recv_sem=recv_sems.at[k], device_id=to, device_id_type=MESH)

        mine = pltpu.make_async_copy(x_ref, rows(*me), local_sem)
        mine.start()
        first = [copy(0, me, sibling, src=x_ref)]
        first += [copy(1 + j, me, (*chip, c), src=x_ref) for j, chip in enumerate(chips)]
        for cp in first:
            cp.start()
        passed = [copy(4 + j, (*chip, c), sibling) for j, chip in enumerate(chips)]
        for j, chip in enumerate(chips):
            copy(1 + j, (*chip, c), me).wait_recv()
            passed[j].start()
        copy(0, sibling, me).wait_recv()
        for j, chip in enumerate(chips):
            copy(4 + j, (*chip, 1 - c), me).wait_recv()
        for cp in first + passed:
            cp.wait_send()
        mine.wait()

    return _call(
        body, name=name,
        out_shape=jax.ShapeDtypeStruct((N_DEV * m_per, n), v.dtype),
        in_specs=[pl.BlockSpec(memory_space=pltpu.VMEM)],
        out_specs=pl.BlockSpec(memory_space=pltpu.VMEM),
        scratch_shapes=[pltpu.SemaphoreType.DMA((7,)), pltpu.SemaphoreType.DMA((7,)), pltpu.SemaphoreType.DMA],
    )(v)


def _hbm_spec():
    return pl.BlockSpec(memory_space=pltpu.HBM)


def _sem_spec():
    return pl.BlockSpec(memory_space=pltpu.SEMAPHORE)


def _split_params():
    return pltpu.CompilerParams(has_side_effects=pltpu.SideEffectType.DATAFLOW_SIDE_EFFECTING)


def _hbm(a):
    return pltpu.with_memory_space_constraint(a, pltpu.HBM)


def gather_inplace(land, name):
    half = land.shape[1] // 2

    def body(land_in, land_ref, token, send_sems, recv_sems):
        del land_in
        x, y, c = _place()
        chips = _other_chips(x, y)

        def copy(k, chip_idx, core_half, to):
            rows = land_ref.at[chip_idx, pl.ds(core_half * half, half), :]
            return pltpu.make_async_remote_copy(
                src_ref=rows, dst_ref=rows, send_sem=send_sems.at[k], recv_sem=recv_sems.at[k],
                device_id=to, device_id_type=MESH)

        first = [copy(j, 2 * x + y, c, (px, py, c)) for j, (px, py) in enumerate(chips)]
        for cp in first:
            cp.start()
        passed = [copy(3 + j, 2 * px + py, c, (x, y, 1 - c)) for j, (px, py) in enumerate(chips)]
        for j, (px, py) in enumerate(chips):
            copy(j, 2 * px + py, c, (px, py, c)).wait_recv()
            passed[j].start()
        for j, (px, py) in enumerate(chips):
            copy(3 + j, 2 * px + py, 1 - c, (x, y, 1 - c)).wait_recv()
        for cp in first + passed:
            cp.wait_send()
        token[...] = jnp.zeros_like(token)

    return _call(
        body, name=name,
        out_shape=(jax.ShapeDtypeStruct(land.shape, land.dtype), jax.ShapeDtypeStruct((8, LANES), F32)),
        in_specs=[_hbm_spec()], out_specs=(_hbm_spec(), pl.BlockSpec(memory_space=pltpu.VMEM)),
        input_output_aliases={0: 0},
        scratch_shapes=[pltpu.SemaphoreType.DMA((6,)), pltpu.SemaphoreType.DMA((6,))],
    )(land)


def gather_start(land, name):
    def body(land_ref, send_sems, recv_sems, land_thru, token):
        del land_thru
        x, y, c = _place()
        for j, (px, py) in enumerate(_other_chips(x, y)):
            pltpu.make_async_remote_copy(
                src_ref=land_ref.at[2 * x + y], dst_ref=land_ref.at[2 * x + y],
                send_sem=send_sems.at[j], recv_sem=recv_sems.at[j], device_id=(px, py, c),
                device_id_type=MESH).start()
        token[...] = jnp.zeros_like(token)

    return _call(
        body, name=name,
        out_shape=(pltpu.SemaphoreType.DMA((3,)), pltpu.SemaphoreType.DMA((3,)),
                   pltpu.HBM(land.shape, land.dtype), jax.ShapeDtypeStruct((8, LANES), F32)),
        in_specs=(_hbm_spec(),),
        out_specs=(_sem_spec(), _sem_spec(), _hbm_spec(), pl.BlockSpec(memory_space=pltpu.VMEM)),
        input_output_aliases={0: 2}, compiler_params=_split_params(),
    )(_hbm(land))


def gather_wait(send_sems, recv_sems, land, after, name):
    def body(land_ref, send_sems, recv_sems, after_ref, land_out):
        del after_ref, land_out
        x, y, c = _place()
        for j, (px, py) in enumerate(_other_chips(x, y)):
            cp = pltpu.make_async_remote_copy(
                src_ref=land_ref.at[2 * x + y], dst_ref=land_ref.at[2 * px + py],
                send_sem=send_sems.at[j], recv_sem=recv_sems.at[j], device_id=(px, py, c), device_id_type=MESH)
            cp.wait_send()
            cp.wait_recv()

    return _call(
        body, name=name,
        out_shape=pltpu.HBM(land.shape, land.dtype),
        in_specs=(_hbm_spec(), _sem_spec(), _sem_spec(), pl.BlockSpec(memory_space=pl.ANY)),
        out_specs=_hbm_spec(), input_output_aliases={0: 0}, compiler_params=_split_params(),
    )(land, send_sems, recv_sems, after)


def exchange_start(parts, name):
    _, r, c_ = parts.shape

    def body(parts_ref, land_ref, send_sems, recv_sems, parts_thru, land_thru, token):
        del parts_thru, land_thru
        x, y, c = _place()
        for j, (px, py) in enumerate(_other_chips(x, y)):
            pltpu.make_async_remote_copy(
                src_ref=parts_ref.at[2 * px + py], dst_ref=land_ref.at[j],
                send_sem=send_sems.at[j], recv_sem=recv_sems.at[j], device_id=(px, py, c),
                device_id_type=MESH).start()
        token[...] = jnp.zeros_like(token)

    return _call(
        body, name=name,
        out_shape=(pltpu.SemaphoreType.DMA((3,)), pltpu.SemaphoreType.DMA((3,)),
                   pltpu.HBM(parts.shape, parts.dtype), pltpu.HBM((3, r, c_), parts.dtype),
                   jax.ShapeDtypeStruct((8, LANES), F32)),
        in_specs=(_hbm_spec(), _hbm_spec()),
        out_specs=(_sem_spec(), _sem_spec(), _hbm_spec(), _hbm_spec(), pl.BlockSpec(memory_space=pltpu.VMEM)),
        input_output_aliases={0: 2, 1: 3}, compiler_params=_split_params(),
    )(_hbm(parts), _hbm(lax.empty((3, r, c_), parts.dtype)))


def exchange_wait(send_sems, recv_sems, parts, land, after, name):
    def body(parts_ref, land_ref, send_sems, recv_sems, after_ref, parts_out, land_out):
        del after_ref, parts_out, land_out
        x, y, c = _place()
        for j, (px, py) in enumerate(_other_chips(x, y)):
            cp = pltpu.make_async_remote_copy(
                src_ref=parts_ref.at[2 * px + py], dst_ref=land_ref.at[j],
                send_sem=send_sems.at[j], recv_sem=recv_sems.at[j], device_id=(px, py, c), device_id_type=MESH)
            cp.wait_send()
            cp.wait_recv()

    return _call(
        body, name=name,
        out_shape=(pltpu.HBM(parts.shape, parts.dtype), pltpu.HBM(land.shape, land.dtype)),
        in_specs=(_hbm_spec(), _hbm_spec(), _sem_spec(), _sem_spec(), pl.BlockSpec(memory_space=pl.ANY)),
        out_specs=(_hbm_spec(), _hbm_spec()), input_output_aliases={0: 0, 1: 1},
        compiler_params=_split_params(),
    )(parts, land, send_sems, recv_sems, after)


def cast_into_slot(w, chip, after, name):
    r, c = w.shape
    tr = min(256, r)

    def body(s_ref, w_ref, after_ref, o_ref):
        del s_ref, after_ref
        o_ref[...] = w_ref[...].astype(BF16)

    return _call(
        body, name=name,
        grid_spec=pltpu.PrefetchScalarGridSpec(
            num_scalar_prefetch=1, grid=(r // tr,),
            in_specs=[pl.BlockSpec((tr, c), lambda i, s: (i, 0)), pl.BlockSpec(memory_space=pl.ANY)],
            out_specs=pl.BlockSpec((None, tr, c), lambda i, s: (s[0], i, 0))),
        out_shape=jax.ShapeDtypeStruct((N_CHIPS, r, c), BF16),
        compiler_params=_params(),
    )(chip.reshape(1).astype(jnp.int32), w, after)


def sum_parts(parts, land, chip, name):
    _, r, c = parts.shape
    tr = min(256, r)

    def body(s_ref, p_ref, l_ref, o_ref):
        del s_ref
        acc = p_ref[...].astype(F32) + l_ref[0].astype(F32)
        acc = acc + l_ref[1].astype(F32)
        o_ref[...] = acc + l_ref[2].astype(F32)

    return _call(
        body, name=name,
        grid_spec=pltpu.PrefetchScalarGridSpec(
            num_scalar_prefetch=1, grid=(r // tr,),
            in_specs=[pl.BlockSpec((None, tr, c), lambda i, s: (s[0], i, 0)),
                      pl.BlockSpec((3, tr, c), lambda i, s: (0, i, 0))],
            out_specs=pl.BlockSpec((tr, c), lambda i, s: (i, 0))),
        out_shape=jax.ShapeDtypeStruct((r, c), F32),
        compiler_params=_params(),
    )(chip.reshape(1).astype(jnp.int32), parts, land)


def swap_sibling(arrs, name):
    n = len(arrs)

    def body(*refs):
        ins, outs = refs[:n], refs[n:2 * n]
        send_sems, recv_sems = refs[2 * n:]
        x, y, c = _place()
        cps = []
        for w in range(n):
            cp = pltpu.make_async_remote_copy(
                src_ref=ins[w], dst_ref=outs[w], send_sem=send_sems.at[w], recv_sem=recv_sems.at[w],
                device_id=(x, y, 1 - c), device_id_type=MESH)
            cp.start()
            cps.append(cp)
        for cp in cps:
            cp.wait_recv()
        for cp in cps:
            cp.wait_send()

    return _call(
        body, name=name,
        out_shape=[jax.ShapeDtypeStruct(a.shape, a.dtype) for a in arrs],
        in_specs=[_hbm_spec()] * n, out_specs=[_hbm_spec()] * n,
        scratch_shapes=[pltpu.SemaphoreType.DMA((n,)), pltpu.SemaphoreType.DMA((n,))],
    )(*arrs)


def adamw_pair(pa, pb, w, m, v, name):
    r, c = w.shape
    tr = min(128, r)

    def body(pa_ref, pb_ref, w_ref, m_ref, v_ref, g_ref, d_ref, nm_ref, nv_ref):
        g = pa_ref[...] + pb_ref[...]
        d, nm, nv = _adamw(w_ref[...], g, m_ref[...], v_ref[...])
        g_ref[...] = g
        d_ref[...] = d
        nm_ref[...] = nm
        nv_ref[...] = nv

    spec = pl.BlockSpec((tr, c), lambda i: (i, 0))
    return _call(
        body, name=name, grid=(r // tr,),
        out_shape=[jax.ShapeDtypeStruct((r, c), F32)] * 4,
        in_specs=[spec] * 5, out_specs=[spec] * 4,
        compiler_params=_params(),
    )(pa, pb, w, m, v)


def small_update(gathered, w, m, v, name):
    def body(g_ref, w_ref, m_ref, v_ref, go_ref, d_ref, nm_ref, nv_ref):
        g = g_ref[0]
        for k in range(1, N_DEV):
            g = g + g_ref[k]
        d, nm, nv = _adamw(w_ref[...], g, m_ref[...], v_ref[...])
        go_ref[...] = g
        d_ref[...] = d
        nm_ref[...] = nm
        nv_ref[...] = nv

    return _call(
        body, name=name,
        out_shape=[jax.ShapeDtypeStruct(w.shape, F32)] * 4,
        compiler_params=_params(),
    )(gathered, w, m, v)


def ada_fwd(c_all, w_ada, b_cols, name):
    n_l, d, cols = w_ada.shape
    nb = c_all.shape[0]
    tn = 256

    def body(c_ref, w_ref, b_ref, o_ref):
        cv = c_ref[...]
        ca = (cv * _sigmoid(cv)).astype(BF16)
        o_ref[...] = _dot(ca, w_ref[...].astype(BF16), NN) + b_ref[...]

    return _call(
        body, name=name, grid=(n_l, cols // tn),
        out_shape=jax.ShapeDtypeStruct((n_l, nb, cols), F32),
        in_specs=[pl.BlockSpec((nb, d), lambda l, j: (0, 0)),
                  pl.BlockSpec((None, d, tn), lambda l, j: (l, 0, j)),
                  pl.BlockSpec((None, 1, tn), lambda l, j: (l, 0, j))],
        out_specs=pl.BlockSpec((None, nb, tn), lambda l, j: (l, 0, j)),
        compiler_params=_params(),
    )(c_all, w_ada, b_cols)


def ada_bwd(c_all, dmod_cols, w, m, v, name):
    n_l, d, cols = w.shape
    nb = c_all.shape[0]
    tn = 256

    def body(c_ref, dm_ref, w_ref, m_ref, v_ref, g_ref, d_ref, nm_ref, nv_ref):
        cv = c_ref[...]
        ca = (cv * _sigmoid(cv)).astype(BF16)
        g = _dot(ca, dm_ref[...].astype(BF16), TN)
        dl, nm, nv = _adamw(w_ref[...], g, m_ref[...], v_ref[...])
        g_ref[...] = g
        d_ref[...] = dl
        nm_ref[...] = nm
        nv_ref[...] = nv

    wspec = pl.BlockSpec((None, d, tn), lambda l, j: (l, 0, j))
    return _call(
        body, name=name, grid=(n_l, cols // tn),
        out_shape=[jax.ShapeDtypeStruct((n_l, d, cols), F32)] * 4,
        in_specs=[pl.BlockSpec((nb, d), lambda l, j: (0, 0)),
                  pl.BlockSpec((None, nb, tn), lambda l, j: (l, 0, j)),
                  wspec, wspec, wspec],
        out_specs=[wspec] * 4,
        compiler_params=_params(),
    )(c_all, dmod_cols, w, m, v)


def bias_update(dmod_all, w, m, v, name):
    def body(dm_ref, w_ref, m_ref, v_ref, g_ref, d_ref, nm_ref, nv_ref):
        g = jnp.sum(dm_ref[...], axis=0, keepdims=True)
        dl, nm, nv = _adamw(w_ref[...], g, m_ref[...], v_ref[...])
        g_ref[...] = g
        d_ref[...] = dl
        nm_ref[...] = nm
        nv_ref[...] = nv

    return _call(
        body, name=name,
        out_shape=[jax.ShapeDtypeStruct(w.shape, F32)] * 4,
        compiler_params=_params(),
    )(dmod_all, w, m, v)


def inproj_fwd(x, mod, ng, wg, seq, sectioned, name):
    m_rows, d = x.shape
    nsh, _, ns = wg.shape
    n = nsh * ns
    tm, tn = min(ROW_TILE, seq), 512
    per = ns // tn

    def body(x_ref, mod_ref, ng_ref, w_ref, proj_ref, h_ref):
        @pl.when(pl.program_id(1) == 0)
        def _():
            xv = x_ref[...]
            r = lax.rsqrt(jnp.mean(xv * xv, axis=-1, keepdims=True) + EPS)
            md = mod_ref[0]
            h = (xv * r * ng_ref[...]) * (1.0 + md[:, d:2 * d]) + md[:, :d]
            h_ref[...] = h.astype(BF16)
        proj_ref[...] = _dot(h_ref[...], w_ref[...], NN)

    if sectioned:
        proj_shape = (nsh, m_rows, ns)
        proj_spec = pl.BlockSpec((None, tm, tn), lambda i, j: (j // per, i, j % per))
    else:
        proj_shape = (m_rows, n)
        proj_spec = pl.BlockSpec((tm, tn), lambda i, j: (i, j))
    return _call(
        body, name=name, grid=(m_rows // tm, n // tn),
        out_shape=[jax.ShapeDtypeStruct(proj_shape, F32), jax.ShapeDtypeStruct((m_rows, d), BF16)],
        in_specs=[pl.BlockSpec((tm, d), lambda i, j: (i, 0)),
                  pl.BlockSpec((1, 1, 3 * d), lambda i, j: ((i * tm) // seq, 0, 0)),
                  pl.BlockSpec((1, d), lambda i, j: (0, 0)),
                  pl.BlockSpec((None, d, tn), lambda i, j: (j // per, 0, j % per))],
        out_specs=[proj_spec, pl.BlockSpec((tm, d), lambda i, j: (i, 0))],
        compiler_params=_params(),
    )(x, mod, ng, wg)


def outproj_fwd(y, w, x, mod, seq, name):
    m_rows, di = y.shape
    d = w.shape[1]
    tm = min(ROW_TILE, seq)

    def body(y_ref, w_ref, x_ref, mod_ref, xn_ref, out_ref):
        acc = _dot(y_ref[...], w_ref[...], NN)
        out_ref[...] = acc
        xn_ref[...] = x_ref[...] + mod_ref[0][:, 2 * d:] * acc

    row = pl.BlockSpec((tm, d), lambda i: (i, 0))
    return _call(
        body, name=name, grid=(m_rows // tm,),
        out_shape=[jax.ShapeDtypeStruct((m_rows, d), F32)] * 2,
        in_specs=[pl.BlockSpec((tm, di), lambda i: (i, 0)),
                  pl.BlockSpec((di, d), lambda i: (0, 0)),
                  row,
                  pl.BlockSpec((1, 1, 3 * d), lambda i: ((i * tm) // seq, 0, 0))],
        out_specs=[row, row],
        compiler_params=_params(),
    )(y, w, x, mod)


def outproj_bwd(dxo, out, mod, w, seq, name):
    m_rows, d = dxo.shape
    di = w.shape[0]
    nb = m_rows // seq
    tm, tn = min(ROW_TILE, seq), 512

    def body(dxo_ref, out_ref, mod_ref, w_ref, dy_ref, dout_ref, dgate_ref):
        i = pl.program_id(0)

        @pl.when(pl.program_id(1) == 0)
        def _():
            dx = dxo_ref[...]
            dout_ref[...] = (mod_ref[0][:, 2 * d:] * dx).astype(BF16)
            part = jnp.sum(dx * out_ref[...], axis=0, keepdims=True)

            @pl.when((i * tm) % seq == 0)
            def _():
                dgate_ref[0] = part

            @pl.when((i * tm) % seq != 0)
            def _():
                dgate_ref[0] = dgate_ref[0] + part

        dy_ref[...] = _dot(dout_ref[...], w_ref[...], NT)

    row = pl.BlockSpec((tm, d), lambda i, j: (i, 0))
    return _call(
        body, name=name, grid=(m_rows // tm, di // tn),
        out_shape=[jax.ShapeDtypeStruct((m_rows, di), F32), jax.ShapeDtypeStruct((m_rows, d), BF16),
                   jax.ShapeDtypeStruct((nb, 1, d), F32)],
        in_specs=[row, row,
                  pl.BlockSpec((1, 1, 3 * d), lambda i, j: ((i * tm) // seq, 0, 0)),
                  pl.BlockSpec((tn, d), lambda i, j: (j, 0))],
        out_specs=[pl.BlockSpec((tm, tn), lambda i, j: (i, j)), row,
                   pl.BlockSpec((1, 1, d), lambda i, j: ((i * tm) // seq, 0, 0))],
        compiler_params=_params(),
    )(dxo, out, mod, w)


def grad_w_out(y, dout, name):
    m_rows, di = y.shape
    d = dout.shape[1]
    tm, tk = min(512, m_rows), _col_tile(di)
    n_m = m_rows // tm

    def body(y_ref, do_ref, o_ref, acc_ref):
        mi = pl.program_id(1)
        part = _dot(y_ref[...], do_ref[...], TN)

        @pl.when(mi == 0)
        def _():
            acc_ref[...] = part

        @pl.when(mi != 0)
        def _():
            acc_ref[...] = acc_ref[...] + part

        @pl.when(mi == n_m - 1)
        def _():
            o_ref[...] = acc_ref[...].astype(BF16)

    return _call(
        body, name=name, grid=(di // tk, n_m),
        out_shape=jax.ShapeDtypeStruct((di, d), BF16),
        in_specs=[pl.BlockSpec((tm, tk), lambda j, mi: (mi, j)),
                  pl.BlockSpec((tm, d), lambda j, mi: (mi, 0))],
        out_specs=pl.BlockSpec((tk, d), lambda j, mi: (j, 0)),
        scratch_shapes=[pltpu.VMEM((tk, d), F32)],
        compiler_params=_params(),
    )(y, dout)


def grad_w_in(h, dproj, nsh, sectioned, name):
    m_rows, d = h.shape
    n = dproj.shape[0] * dproj.shape[2] if sectioned else dproj.shape[1]
    ns = n // nsh
    tm, tn = min(512, m_rows), _col_tile(ns)
    per = ns // tn
    n_m = m_rows // tm

    def body(h_ref, dp_ref, o_ref, acc_ref):
        mi = pl.program_id(1)
        part = _dot(h_ref[...], dp_ref[...], TN)

        @pl.when(mi == 0)
        def _():
            acc_ref[...] = part

        @pl.when(mi != 0)
        def _():
            acc_ref[...] = acc_ref[...] + part

        @pl.when(mi == n_m - 1)
        def _():
            o_ref[...] = acc_ref[...].astype(BF16)

    if sectioned:
        dp_spec = pl.BlockSpec((None, tm, tn), lambda j, mi: (j // per, mi, j % per))
    else:
        dp_spec = pl.BlockSpec((tm, tn), lambda j, mi: (mi, j))
    return _call(
        body, name=name, grid=(n // tn, n_m),
        out_shape=jax.ShapeDtypeStruct((nsh, d, ns), BF16),
        in_specs=[pl.BlockSpec((tm, d), lambda j, mi: (mi, 0)), dp_spec],
        out_specs=pl.BlockSpec((None, d, tn), lambda j, mi: (j // per, 0, j % per)),
        scratch_shapes=[pltpu.VMEM((d, tn), F32)],
        compiler_params=_params(),
    )(h, dproj)


def inproj_bwd(dproj, wg, x, dxo, mod, ng, seq, sectioned, name):
    m_rows, d = x.shape
    nsh, _, ns = wg.shape
    n = nsh * ns
    nb = m_rows // seq
    tm, tk = min(ROW_TILE, seq), 512
    per = ns // tk
    n_k = n // tk

    def body(dp_ref, w_ref, x_ref, dxo_ref, mod_ref, ng_ref, dxi_ref, dsh_ref, dsc_ref, dng_ref, acc_ref):
        i, k = pl.program_id(0), pl.program_id(1)
        part = _dot(dp_ref[...], w_ref[...], NT)

        @pl.when(k == 0)
        def _():
            acc_ref[...] = part

        @pl.when(k != 0)
        def _():
            acc_ref[...] = acc_ref[...] + part

        @pl.when(k == n_k - 1)
        def _():
            dh = acc_ref[...]
            xv = x_ref[...]
            r = lax.rsqrt(jnp.mean(xv * xv, axis=-1, keepdims=True) + EPS)
            xn = xv * r
            md = mod_ref[0]
            gain = ng_ref[...]
            p_shift = jnp.sum(dh, axis=0, keepdims=True)
            p_scale = jnp.sum(dh * (xn * gain), axis=0, keepdims=True)
            drn = dh * (1.0 + md[:, d:2 * d])
            p_ng = jnp.sum(drn * xn, axis=0, keepdims=True)
            dxn = drn * gain
            dx = r * (dxn - xn * jnp.mean(dxn * xn, axis=-1, keepdims=True))
            dxi_ref[...] = dxo_ref[...] + dx

            @pl.when((i * tm) % seq == 0)
            def _():
                dsh_ref[0] = p_shift
                dsc_ref[0] = p_scale

            @pl.when((i * tm) % seq != 0)
            def _():
                dsh_ref[0] = dsh_ref[0] + p_shift
                dsc_ref[0] = dsc_ref[0] + p_scale

            @pl.when(i == 0)
            def _():
                dng_ref[...] = p_ng

            @pl.when(i != 0)
            def _():
                dng_ref[...] = dng_ref[...] + p_ng

    if sectioned:
        dp_spec = pl.BlockSpec((None, tm, tk), lambda i, k: (k // per, i, k % per))
    else:
        dp_spec = pl.BlockSpec((tm, tk), lambda i, k: (i, k))
    row = pl.BlockSpec((tm, d), lambda i, k: (i, 0))
    per_seq = pl.BlockSpec((1, 1, d), lambda i, k: ((i * tm) // seq, 0, 0))
    return _call(
        body, name=name, grid=(m_rows // tm, n_k),
        out_shape=[jax.ShapeDtypeStruct((m_rows, d), F32), jax.ShapeDtypeStruct((nb, 1, d), F32),
                   jax.ShapeDtypeStruct((nb, 1, d), F32), jax.ShapeDtypeStruct((1, d), F32)],
        in_specs=[dp_spec,
                  pl.BlockSpec((None, d, tk), lambda i, k: (k // per, 0, k % per)),
                  row, row,
                  pl.BlockSpec((1, 1, 3 * d), lambda i, k: ((i * tm) // seq, 0, 0)),
                  pl.BlockSpec((1, d), lambda i, k: (0, 0))],
        out_specs=[row, per_seq, per_seq, pl.BlockSpec((1, d), lambda i, k: (0, 0))],
        scratch_shapes=[pltpu.VMEM((tm, d), F32)],
        compiler_params=_params(),
    )(dproj, wg, x, dxo, mod, ng)


def _sgu_stats(proj_ref, vg_ref, di, gd):
    s1 = jnp.zeros((SG_BLOCK, 1), F32)
    for g in range(SG_GROUPS):
        vg = _gelu(proj_ref[:, di + g * gd:di + (g + 1) * gd])
        vg_ref[:, g * gd:(g + 1) * gd] = vg
        s1 = s1 + jnp.sum(vg, axis=1, keepdims=True)
    mu = s1 / di
    s2 = jnp.zeros((SG_BLOCK, 1), F32)
    for g in range(SG_GROUPS):
        dv = vg_ref[:, g * gd:(g + 1) * gd] - mu
        s2 = s2 + jnp.sum(dv * dv, axis=1, keepdims=True)
    return mu, lax.rsqrt(s2 / di + EPS)


def sgu_fwd(proj, ln_gain, ln_bias, ws, bs, name):
    m_rows, n3 = proj.shape
    di = n3 // 3
    gd = di // SG_GROUPS

    def body(proj_ref, lg_ref, lb_ref, ws_ref, bs_ref, y_ref, wsm_ref, vg_ref):
        @pl.when(pl.program_id(0) == 0)
        def _():
            mask = _chunk_mask()
            for g in range(SG_GROUPS):
                wsm_ref[g] = jnp.where(mask, ws_ref[g], 0.0).astype(BF16)

        mu, rstd = _sgu_stats(proj_ref, vg_ref, di, gd)
        for g in range(SG_GROUPS):
            cs = slice(g * gd, (g + 1) * gd)
            vln = (vg_ref[:, cs] - mu) * rstd * lg_ref[:, cs] + lb_ref[:, cs]
            s = _dot(wsm_ref[g], vln.astype(BF16), NN) + bs_ref[g]
            u = _gelu(proj_ref[:, cs])
            gp = proj_ref[:, 2 * di + g * gd:2 * di + (g + 1) * gd]
            y_ref[:, cs] = (u * s * (gp * _sigmoid(gp))).astype(BF16)

    full = lambda shape: pl.BlockSpec(shape, lambda i: (0,) * len(shape))
    return _call(
        body, name=name, grid=(m_rows // SG_BLOCK,),
        out_shape=jax.ShapeDtypeStruct((m_rows, di), BF16),
        in_specs=[pl.BlockSpec((SG_BLOCK, n3), lambda i: (i, 0)),
                  full((1, di)), full((1, di)),
                  full((SG_GROUPS, SG_BLOCK, SG_BLOCK)), full((SG_GROUPS, SG_BLOCK, 1))],
        out_specs=pl.BlockSpec((SG_BLOCK, di), lambda i: (i, 0)),
        scratch_shapes=[pltpu.VMEM((SG_GROUPS, SG_BLOCK, SG_BLOCK), BF16), pltpu.VMEM((SG_BLOCK, di), F32)],
        compiler_params=_params(),
    )(proj, ln_gain, ln_bias, ws, bs)


def sgu_bwd(proj, dy, ln_gain, ln_bias, ws, bs, name):
    m_rows, n3 = proj.shape
    di = n3 // 3
    gd = di // SG_GROUPS
    n_i = m_rows // SG_BLOCK

    def body(proj_ref, dy_ref, lg_ref, lb_ref, ws_ref, bs_ref,
             dp_ref, dws_ref, dbs_ref, dlg_ref, dlb_ref, wsm_ref, vg_ref, dvh_ref):
        i = pl.program_id(0)

        @pl.when(i == 0)
        def _():
            mask = _chunk_mask()
            for g in range(SG_GROUPS):
                wsm_ref[g] = jnp.where(mask, ws_ref[g], 0.0).astype(BF16)
            dws_ref[...] = jnp.zeros_like(dws_ref)
            dbs_ref[...] = jnp.zeros_like(dbs_ref)
            dlg_ref[...] = jnp.zeros_like(dlg_ref)
            dlb_ref[...] = jnp.zeros_like(dlb_ref)

        mu, rstd = _sgu_stats(proj_ref, vg_ref, di, gd)
        m1 = jnp.zeros((SG_BLOCK, 1), F32)
        m2 = jnp.zeros((SG_BLOCK, 1), F32)
        for g in range(SG_GROUPS):
            cs = slice(g * gd, (g + 1) * gd)
            gs = slice(2 * di + g * gd, 2 * di + (g + 1) * gd)
            gain = lg_ref[:, cs]
            vhat = (vg_ref[:, cs] - mu) * rstd
            vln_b = (vhat * gain + lb_ref[:, cs]).astype(BF16)
            s = _dot(wsm_ref[g], vln_b, NN) + bs_ref[g]
            u, du = _gelu_and_grad(proj_ref[:, cs])
            sg, dsg = _silu_and_grad(proj_ref[:, gs])
            dyv = dy_ref[:, cs]
            dp_ref[:, cs] = (dyv * s * sg * du).astype(BF16)
            dp_ref[:, gs] = (dyv * u * s * dsg).astype(BF16)
            ds = dyv * u * sg
            ds_b = ds.astype(BF16)
            dws_ref[g] = dws_ref[g] + _dot(ds_b, vln_b, NT)
            dbs_ref[g] = dbs_ref[g] + jnp.sum(ds, axis=1, keepdims=True)
            dvln = _dot(wsm_ref[g], ds_b, TN)
            dlg_ref[:, cs] = dlg_ref[:, cs] + jnp.sum(dvln * vhat, axis=0, keepdims=True)
            dlb_ref[:, cs] = dlb_ref[:, cs] + jnp.sum(dvln, axis=0, keepdims=True)
            dvh = dvln * gain
            dvh_ref[:, cs] = dvh
            m1 = m1 + jnp.sum(dvh, axis=1, keepdims=True)
            m2 = m2 + jnp.sum(dvh * vhat, axis=1, keepdims=True)
        m1 = m1 / di
        m2 = m2 / di
        for g in range(SG_GROUPS):
            cs = slice(g * gd, (g + 1) * gd)
            vs = slice(di + g * gd, di + (g + 1) * gd)
            vhat = (vg_ref[:, cs] - mu) * rstd
            dvg = rstd * (dvh_ref[:, cs] - m1 - vhat * m2)
            _, dgel = _gelu_and_grad(proj_ref[:, vs])
            dp_ref[:, vs] = (dvg * dgel).astype(BF16)

        @pl.when(i == n_i - 1)
        def _():
            mask = _chunk_mask()
            for g in range(SG_GROUPS):
                dws_ref[g] = jnp.where(mask, dws_ref[g], 0.0)

    full = lambda shape: pl.BlockSpec(shape, lambda i: (0,) * len(shape))
    return _call(
        body, name=name, grid=(n_i,),
        out_shape=[jax.ShapeDtypeStruct((m_rows, n3), BF16),
                   jax.ShapeDtypeStruct((SG_GROUPS, SG_BLOCK, SG_BLOCK), F32),
                   jax.ShapeDtypeStruct((SG_GROUPS, SG_BLOCK, 1), F32),
                   jax.ShapeDtypeStruct((1, di), F32), jax.ShapeDtypeStruct((1, di), F32)],
        in_specs=[pl.BlockSpec((SG_BLOCK, n3), lambda i: (i, 0)),
                  pl.BlockSpec((SG_BLOCK, di), lambda i: (i, 0)),
                  full((1, di)), full((1, di)),
                  full((SG_GROUPS, SG_BLOCK, SG_BLOCK)), full((SG_GROUPS, SG_BLOCK, 1))],
        out_specs=[pl.BlockSpec((SG_BLOCK, n3), lambda i: (i, 0)),
                   full((SG_GROUPS, SG_BLOCK, SG_BLOCK)), full((SG_GROUPS, SG_BLOCK, 1)),
                   full((1, di)), full((1, di))],
        scratch_shapes=[pltpu.VMEM((SG_GROUPS, SG_BLOCK, SG_BLOCK), BF16),
                        pltpu.VMEM((SG_BLOCK, di), F32), pltpu.VMEM((SG_BLOCK, di), F32)],
        compiler_params=_params(),
    )(proj, dy, ln_gain, ln_bias, ws, bs)


def _lower_bound(lbraw):
    mx = jnp.maximum(lbraw[0:1, :], lbraw[1:2, :])
    e0 = jnp.exp(lbraw[0:1, :] - mx)
    e1 = jnp.exp(lbraw[1:2, :] - mx)
    p0 = e0 / (e0 + e1)
    p1 = e1 / (e0 + e1)
    return (p0 + p1) - p0, p0, p1


def _tri(lower):
    r = lax.broadcasted_iota(jnp.int32, (CHUNK, CHUNK), 0)
    c = lax.broadcasted_iota(jnp.int32, (CHUNK, CHUNK), 1)
    return ((r >= c) if lower else (c >= r)).astype(F32)


def _row(a, idx):
    r = lax.broadcasted_iota(jnp.int32, a.shape, 0)
    return jnp.sum(jnp.where(r == idx, a, 0.0), axis=0, keepdims=True)


def _hgrn_gates(qp, fp, lb, tri):
    sgm = _sigmoid(fp)
    f = lb + (1.0 - lb) * sgm
    k = 1.0 - f
    a = _dot(tri, jnp.log(f), NN, precision=lax.Precision.HIGHEST)
    a_mid = _row(a, CHUNK // 2 - 1)
    a_last = _row(a, CHUNK - 1)
    q, dq = _silu_and_grad(qp)
    e1, e2, e3, e4 = jnp.exp(a - a_mid), jnp.exp(a_mid - a), jnp.exp(a), jnp.exp(a_last - a)
    return dict(sgm=sgm, f=f, k=k, q=q, dq=dq, e1=e1, e2=e2, e3=e3, e4=e4, dec=jnp.exp(a_last),
                q_in=q * e1, k_in=k * e2, q_out=q * e3, k_out=k * e4)


def _causal():
    r = lax.broadcasted_iota(jnp.int32, (CHUNK, CHUNK), 0)
    c = lax.broadcasted_iota(jnp.int32, (CHUNK, CHUNK), 1)
    return r >= c


def hgrn_fwd(proj4, lbraw, gn, seq, name):
    _, m_rows, di = proj4.shape
    nb, nh, nc = m_rows // seq, di // HEAD_DIM, seq // CHUNK
    rows = min(HG_ROWS, seq)
    wide = HG_WIDE * HEAD_DIM
    ns, cpb = seq // rows, rows // CHUNK

    def body(p_ref, lb_ref, gn_ref, y_ref, sts_ref, st_ref):
        @pl.when(pl.program_id(2) == 0)
        def _():
            st_ref[...] = jnp.zeros_like(st_ref)

        tri = _tri(True)
        causal = _causal()
        gain = gn_ref[...]
        lbs = [_lower_bound(lb_ref[:, j * HEAD_DIM:(j + 1) * HEAD_DIM])[0] for j in range(HG_WIDE)]

        units = [(n, j) for n in range(cpb) for j in range(HG_WIDE)]
        rs = lambda n: slice(n * CHUNK, (n + 1) * CHUNK)
        cs = lambda j: slice(j * HEAD_DIM, (j + 1) * HEAD_DIM)
        gates, v_b, sc_b, kv, o_in, o_x = {}, {}, {}, {}, {}, {}
        for n, j in units:
            gates[n, j] = _hgrn_gates(p_ref[0, rs(n), cs(j)], p_ref[1, rs(n), cs(j)], lbs[j], tri)
            v_b[n, j] = p_ref[2, rs(n), cs(j)].astype(BF16)
        for u in units:
            t = gates[u]
            sc_b[u] = jnp.where(causal, _dot(t["q_in"].astype(BF16), t["k_in"].astype(BF16), NT), 0.0).astype(BF16)
            kv[u] = _dot(v_b[u], t["k_out"].astype(BF16), TN)
        for u in units:
            o_in[u] = _dot(sc_b[u], v_b[u], NN)
        for j in range(HG_WIDE):
            st = st_ref[j]
            for n in range(cpb):
                sts_ref[n, :, cs(j)] = st
                o_x[n, j] = _dot(gates[n, j]["q_out"].astype(BF16), st.astype(BF16), NT)
                st = st * gates[n, j]["dec"] + kv[n, j]
            st_ref[j] = st
        for n, j in units:
            o = o_in[n, j] + o_x[n, j]
            r = lax.rsqrt(jnp.mean(o * o, axis=-1, keepdims=True) + EPS)
            gp = p_ref[3, rs(n), cs(j)]
            y_ref[rs(n), cs(j)] = ((o * r * gain) * (gp * _sigmoid(gp))).astype(BF16)

    return _call(
        body, name=name, grid=(nh // HG_WIDE, nb, ns),
        out_shape=[jax.ShapeDtypeStruct((m_rows, di), BF16),
                   jax.ShapeDtypeStruct((nb * nc, HEAD_DIM, di), F32)],
        in_specs=[pl.BlockSpec((4, rows, wide), lambda hg, b, s: (0, b * ns + s, hg)),
                  pl.BlockSpec((2, wide), lambda hg, b, s: (0, hg)),
                  pl.BlockSpec((1, HEAD_DIM), lambda hg, b, s: (0, 0))],
        out_specs=[pl.BlockSpec((rows, wide), lambda hg, b, s: (b * ns + s, hg)),
                   pl.BlockSpec((cpb, HEAD_DIM, wide), lambda hg, b, s: (b * ns + s, 0, hg))],
        scratch_shapes=[pltpu.VMEM((HG_WIDE, HEAD_DIM, HEAD_DIM), F32)],
        compiler_params=_params(),
    )(proj4, lbraw, gn)


def hgrn_bwd(proj4, dy, sts, lbraw, gn, seq, name):
    _, m_rows, di = proj4.shape
    nb, nh, nc = m_rows // seq, di // HEAD_DIM, seq // CHUNK
    rows = min(HG_ROWS, seq)
    wide = HG_WIDE * HEAD_DIM
    ns, cpb = seq // rows, rows // CHUNK
    n_hg = nh // HG_WIDE

    def body(p_ref, dy_ref, sts_ref, lb_ref, gn_ref, dp_ref, dlb_ref, dgn_ref, dst_ref, lbacc_ref, gnacc_ref):
        hg, b, s = pl.program_id(0), pl.program_id(1), pl.program_id(2)
        tri, triu = _tri(True), _tri(False)
        causal = _causal()
        gain = gn_ref[...]
        first = (b == 0) & (s == 0)

        @pl.when((hg == 0) & first)
        def _():
            gnacc_ref[...] = jnp.zeros_like(gnacc_ref)

        @pl.when(first)
        def _():
            lbacc_ref[...] = jnp.zeros_like(lbacc_ref)

        @pl.when(s == 0)
        def _():
            dst_ref[...] = jnp.zeros_like(dst_ref)

        units = [(n, j) for n in range(cpb) for j in range(HG_WIDE)]
        rs = lambda n: slice(n * CHUNK, (n + 1) * CHUNK)
        cs = lambda j: slice(j * HEAD_DIM, (j + 1) * HEAD_DIM)
        lbs = [_lower_bound(lb_ref[:, cs(j)])[0] for j in range(HG_WIDE)]
        gates, v_b, st_b, sc_b, o, do_b = {}, {}, {}, {}, {}, {}
        dq_out, dsc_b, dv, g_st, dq_in, dk_in, dst_at, dk_out, ddec = {}, {}, {}, {}, {}, {}, {}, {}, {}
        for n, j in units:
            gates[n, j] = _hgrn_gates(p_ref[0, rs(n), cs(j)], p_ref[1, rs(n), cs(j)], lbs[j], tri)
            v_b[n, j] = p_ref[2, rs(n), cs(j)].astype(BF16)
            st_b[n, j] = sts_ref[n, :, cs(j)].astype(BF16)
        for u in units:
            t = gates[u]
            sc_b[u] = jnp.where(causal, _dot(t["q_in"].astype(BF16), t["k_in"].astype(BF16), NT), 0.0).astype(BF16)
        for u in units:
            o[u] = _dot(sc_b[u], v_b[u], NN) + _dot(gates[u]["q_out"].astype(BF16), st_b[u], NT)
        for n, j in units:
            ov = o[n, j]
            r = lax.rsqrt(jnp.mean(ov * ov, axis=-1, keepdims=True) + EPS)
            ohat = ov * r
            sg, dsg = _silu_and_grad(p_ref[3, rs(n), cs(j)])
            dyv = dy_ref[rs(n), cs(j)]
            dp_ref[3, rs(n), cs(j)] = (dyv * (ohat * gain) * dsg).astype(BF16)
            d_on = dyv * sg
            gnacc_ref[:, cs(j)] = gnacc_ref[:, cs(j)] + jnp.sum(d_on * ohat, axis=0, keepdims=True)
            dohat = d_on * gain
            do_b[n, j] = (r * (dohat - ohat * jnp.mean(dohat * ohat, axis=-1, keepdims=True))).astype(BF16)
        for u in units:
            dq_out[u] = _dot(do_b[u], st_b[u], NN)
            dsc_b[u] = jnp.where(causal, _dot(do_b[u], v_b[u], NT), 0.0).astype(BF16)
            dv[u] = _dot(sc_b[u], do_b[u], TN)
            g_st[u] = _dot(do_b[u], gates[u]["q_out"].astype(BF16), TN)
        for u in units:
            dq_in[u] = _dot(dsc_b[u], gates[u]["k_in"].astype(BF16), NN)
            dk_in[u] = _dot(dsc_b[u], gates[u]["q_in"].astype(BF16), TN)
        for j in range(HG_WIDE):
            dst = dst_ref[j]
            for n in reversed(range(cpb)):
                dst_at[n, j] = dst
                dst = dst * gates[n, j]["dec"] + g_st[n, j]
            dst_ref[j] = dst
        for n, j in units:
            dst = dst_at[n, j]
            dst_b = dst.astype(BF16)
            dk_out[n, j] = _dot(v_b[n, j], dst_b, NN)
            dv[n, j] = dv[n, j] + _dot(gates[n, j]["k_out"].astype(BF16), dst_b, NT)
            ddec[n, j] = jnp.sum(dst * sts_ref[n, :, cs(j)], axis=0, keepdims=True)
        for n, j in units:
            t = gates[n, j]
            dp_ref[2, rs(n), cs(j)] = dv[n, j].astype(BF16)
            dq = dq_in[n, j] * t["e1"] + dq_out[n, j] * t["e3"]
            dk = dk_in[n, j] * t["e2"] + dk_out[n, j] * t["e4"]
            w_in = dq_in[n, j] * t["q_in"] - dk_in[n, j] * t["k_in"]
            w_out = dk_out[n, j] * t["k_out"]
            da = w_in + dq_out[n, j] * t["q_out"] - w_out
            da_mid = -jnp.sum(w_in, axis=0, keepdims=True)
            da_last = jnp.sum(w_out, axis=0, keepdims=True) + ddec[n, j] * t["dec"]
            rid = lax.broadcasted_iota(jnp.int32, da.shape, 0)
            da = da + jnp.where(rid == CHUNK // 2 - 1, da_mid, 0.0) + jnp.where(rid == CHUNK - 1, da_last, 0.0)
            dlf = _dot(triu, da, NN, precision=lax.Precision.HIGHEST)
            df = dlf / t["f"] - dk
            sgm = t["sgm"]
            dp_ref[1, rs(n), cs(j)] = (df * (1.0 - lbs[j]) * sgm * (1.0 - sgm)).astype(BF16)
            lbacc_ref[:, cs(j)] = lbacc_ref[:, cs(j)] + jnp.sum(df * (1.0 - sgm), axis=0, keepdims=True)
            dp_ref[0, rs(n), cs(j)] = (dq * t["dq"]).astype(BF16)

        @pl.when((b == nb - 1) & (s == ns - 1))
        def _():
            for j in range(HG_WIDE):
                cs = slice(j * HEAD_DIM, (j + 1) * HEAD_DIM)
                _, p0, p1 = _lower_bound(lb_ref[:, cs])
                acc = lbacc_ref[:, cs]
                dlb_ref[0:1, cs] = -acc * p0 * p1
                dlb_ref[1:2, cs] = acc * p1 * (1.0 - p1)

        @pl.when((hg == n_hg - 1) & (b == nb - 1) & (s == ns - 1))
        def _():
            tot = gnacc_ref[:, 0:HEAD_DIM]
            for j in range(1, HG_WIDE):
                tot = tot + gnacc_ref[:, j * HEAD_DIM:(j + 1) * HEAD_DIM]
            dgn_ref[...] = tot

    blk = lambda hg, b, s: b * ns + (ns - 1 - s)
    return _call(
        body, name=name, grid=(n_hg, nb, ns),
        out_shape=[jax.ShapeDtypeStruct((4, m_rows, di), BF16), jax.ShapeDtypeStruct((2, di), F32),
                   jax.ShapeDtypeStruct((1, HEAD_DIM), F32)],
        in_specs=[pl.BlockSpec((4, rows, wide), lambda hg, b, s: (0, blk(hg, b, s), hg)),
                  pl.BlockSpec((rows, wide), lambda hg, b, s: (blk(hg, b, s), hg)),
                  pl.BlockSpec((cpb, HEAD_DIM, wide), lambda hg, b, s: (blk(hg, b, s), 0, hg)),
                  pl.BlockSpec((2, wide), lambda hg, b, s: (0, hg)),
                  pl.BlockSpec((1, HEAD_DIM), lambda hg, b, s: (0, 0))],
        out_specs=[pl.BlockSpec((4, rows, wide), lambda hg, b, s: (0, blk(hg, b, s), hg)),
                   pl.BlockSpec((2, wide), lambda hg, b, s: (0, hg)),
                   pl.BlockSpec((1, HEAD_DIM), lambda hg, b, s: (0, 0))],
        scratch_shapes=[pltpu.VMEM((HG_WIDE, HEAD_DIM, HEAD_DIM), F32), pltpu.VMEM((1, wide), F32),
                        pltpu.VMEM((1, wide), F32)],
        compiler_params=_params(),
    )(proj4, dy, sts, lbraw, gn)


def final_loss(x, fg, target, name):
    m_rows, d = x.shape
    tm = min(512, m_rows)

    def body(x_ref, fg_ref, t_ref, loss_ref, dx_ref, dfg_ref):
        i = pl.program_id(0)
        xv = x_ref[...]
        gain = fg_ref[...]
        r = lax.rsqrt(jnp.mean(xv * xv, axis=-1, keepdims=True) + EPS)
        xn = xv * r
        e = xn * gain - t_ref[...]
        part = 0.5 * jnp.sum(jnp.mean(e * e, axis=-1, keepdims=True), axis=0, keepdims=True)
        dyv = e / d
        p_fg = jnp.sum(dyv * xn, axis=0, keepdims=True)
        dxn = dyv * gain
        dx_ref[...] = r * (dxn - xn * jnp.mean(dxn * xn, axis=-1, keepdims=True))

        @pl.when(i == 0)
        def _():
            loss_ref[...] = part
            dfg_ref[...] = p_fg

        @pl.when(i != 0)
        def _():
            loss_ref[...] = loss_ref[...] + part
            dfg_ref[...] = dfg_ref[...] + p_fg

    row = pl.BlockSpec((tm, d), lambda i: (i, 0))
    return _call(
        body, name=name, grid=(m_rows // tm,),
        out_shape=[jax.ShapeDtypeStruct((1, 1), F32), jax.ShapeDtypeStruct((m_rows, d), F32),
                   jax.ShapeDtypeStruct((1, d), F32)],
        in_specs=[row, pl.BlockSpec((1, d), lambda i: (0, 0)), row],
        out_specs=[pl.BlockSpec((1, 1), lambda i: (0, 0)), row, pl.BlockSpec((1, d), lambda i: (0, 0))],
        compiler_params=_params(),
    )(x, fg, target)


def _pack(parts):
    flat = jnp.concatenate([p.reshape(-1) for p in parts])
    pad = (-flat.shape[0]) % (8 * LANES)
    return jnp.pad(flat, (0, pad)).reshape(-1, LANES)


def _unpack(packed, like):
    flat = packed.reshape(-1)
    out, off = [], 0
    for a in like:
        out.append(flat[off:off + a.size].reshape(a.shape))
        off += a.size
    return out


def kernel(x, c, norm_gain, w_ada, b_ada, a_w_in, a_ln_gain, a_ln_bias, a_w_s, a_b_s, a_w_out, b_w_in, b_lower_bounds, b_gn_gain, b_w_out, final_gain, loss_target, m_norm_gain, m_w_ada, m_b_ada, m_a_w_in, m_a_ln_gain, m_a_ln_bias, m_a_w_s, m_a_b_s, m_a_w_out, m_b_w_in, m_b_lower_bounds, m_b_gn_gain, m_b_w_out, m_final_gain, v_norm_gain, v_w_ada, v_b_ada, v_a_w_in, v_a_ln_gain, v_a_ln_bias, v_a_w_s, v_a_b_s, v_a_w_out, v_b_w_in, v_b_lower_bounds, v_b_gn_gain, v_b_w_out, v_final_gain):
    nb, seq, d = x.shape
    m_rows = nb * seq
    n_l = w_ada.shape[0]
    ada_cols = w_ada.shape[2]
    px, py, pc = _place()
    chip = 2 * px + py
    dev = 2 * chip + pc

    c_all = allgather_small(c.reshape(-1, LANES), "gather_c").reshape(N_DEV * nb, d)
    b_cols = lax.dynamic_slice_in_dim(b_ada, chip * ada_cols, ada_cols, axis=1).reshape(n_l, 1, ada_cols)
    mod_cols = ada_fwd(c_all, w_ada, b_cols, "ada_fwd")
    mod_g = allgather_small(mod_cols.reshape(-1, LANES), "gather_mod")
    mod_g = mod_g.reshape(N_CHIPS, 2, n_l, N_DEV * nb, ada_cols)[:, 0]
    mod_all = jnp.transpose(mod_g, (1, 2, 0, 3)).reshape(n_l, N_DEV * nb, 3 * d)
    mod_mine = lax.dynamic_slice_in_dim(mod_all, dev * nb, nb, axis=1)
    mod0 = mod_mine[0].reshape(nb, 1, 3 * d)
    mod1 = mod_mine[1].reshape(nb, 1, 3 * d)

    wa_in, tok_a_in = gather_inplace(cast_into_slot(a_w_in[0], chip, mod_mine, "cast_a_in"), "gather_a_in")
    s_ao = gather_start(cast_into_slot(a_w_out[0], chip, tok_a_in, "cast_a_out"), "gather_a_out_start")
    s_bi = gather_start(cast_into_slot(b_w_in[0], chip, s_ao[3], "cast_b_in"), "gather_b_in_start")
    s_bo = gather_start(cast_into_slot(b_w_out[0], chip, s_bi[3], "cast_b_out"), "gather_b_out_start")
    di = a_w_out.shape[1] * N_CHIPS

    x0 = x.reshape(m_rows, d)
    tgt = loss_target.reshape(m_rows, d)
    ng0 = norm_gain[0:1] + (s_ao[3][0, 0] + s_bi[3][0, 0] + s_bo[3][0, 0])
    ng1 = norm_gain[1:2]
    bs_col = a_b_s[0].reshape(SG_GROUPS, SG_BLOCK, 1)
    proj_a, h_a = inproj_fwd(x0, mod0, ng0, wa_in, seq, False, "a_inproj")
    y_a = sgu_fwd(proj_a, a_ln_gain, a_ln_bias, a_w_s[0], bs_col, "a_sgu")
    wa_out = gather_wait(*s_ao[:3], y_a, "gather_a_out_wait").reshape(di, d)
    x1, out_a = outproj_fwd(y_a, wa_out, x0, mod0, seq, "a_outproj")
    wb_in = gather_wait(*s_bi[:3], out_a, "gather_b_in_wait")
    proj_b, h_b = inproj_fwd(x1, mod1, ng1, wb_in, seq, True, "b_inproj")
    y_b, sts_b = hgrn_fwd(proj_b, b_lower_bounds, b_gn_gain, seq, "b_hgrn")
    wb_out = gather_wait(*s_bo[:3], y_b, "gather_b_out_wait").reshape(di, d)
    x2, out_b = outproj_fwd(y_b, wb_out, x1, mod1, seq, "b_outproj")
    loss_part, dx2, dfg = final_loss(x2, final_gain.reshape(1, d), tgt, "loss_head")
    loss = lax.psum(loss_part[0, 0], ("x", "y", "c"))

    shard_rows = di // N_CHIPS
    dy_b, dout_b, dgate1 = outproj_bwd(dx2, out_b, mod1, wb_out, seq, "b_outproj_bwd")
    gwb_out = grad_w_out(y_b, dout_b, "b_grad_w_out").reshape(N_CHIPS, shard_rows, d)
    e_bo = exchange_start(gwb_out, "exchange_b_out_start")
    dproj_b, dlb, dgn = hgrn_bwd(proj_b, dy_b, sts_b, b_lower_bounds, b_gn_gain + e_bo[4][0, 0], seq, "b_hgrn_bwd")
    e_bi = exchange_start(grad_w_in(h_b, dproj_b, N_CHIPS, True, "b_grad_w_in"), "exchange_b_in_start")
    dx1, dshift1, dscale1, dng1 = inproj_bwd(
        dproj_b, wb_in, x1, dx2, mod1, ng1 + e_bi[4][0, 0], seq, True, "b_inproj_bwd")

    dy_a, dout_a, dgate0 = outproj_bwd(dx1, out_a, mod0, wa_out, seq, "a_outproj_bwd")
    gwa_out = grad_w_out(y_a, dout_a, "a_grad_w_out").reshape(N_CHIPS, shard_rows, d)
    e_ao = exchange_start(gwa_out, "exchange_a_out_start")
    dproj_a, dws, dbs, dlg, dlbias = sgu_bwd(
        proj_a, dy_a, a_ln_gain + e_ao[4][0, 0], a_ln_bias, a_w_s[0], bs_col, "a_sgu_bwd")
    e_ai = exchange_start(grad_w_in(h_a, dproj_a, N_CHIPS, False, "a_grad_w_in"), "exchange_a_in_start")
    dx0, dshift0, dscale0, dng0 = inproj_bwd(
        dproj_a, wa_in, x0, dx1, mod0, norm_gain[0:1] + e_ai[4][0, 0], seq, False, "a_inproj_bwd")
    grad_x = dx0.reshape(nb, seq, d)

    def finish(ex, after, w, m, v, nm):
        parts_thru, land = exchange_wait(ex[0], ex[1], ex[2], ex[3], after, "exchange_" + nm + "_wait")
        mine = sum_parts(parts_thru, land, chip, "sum_" + nm)
        (other,) = swap_sibling([mine], "swap_" + nm)
        res = adamw_pair(mine, other, w[0], m[0], v[0], "adamw_" + nm)
        return [r.reshape(w.shape) for r in res]

    gb_out, db_out, mb_out, vb_out = finish(e_bo, dx0, b_w_out, m_b_w_out, v_b_w_out, "b_out")
    gb_in, db_in, mb_in, vb_in = finish(e_bi, gb_out, b_w_in, m_b_w_in, v_b_w_in, "b_in")
    ga_out, da_out, ma_out, va_out = finish(e_ao, gb_in, a_w_out, m_a_w_out, v_a_w_out, "a_out")

    dmod = jnp.concatenate([dshift0, dscale0, dgate0, dshift1, dscale1, dgate1], axis=2)
    dmod_all = allgather_small(dmod.reshape(-1, LANES), "gather_dmod").reshape(N_DEV * nb, n_l, 3 * d)
    dmod_cols = lax.dynamic_slice_in_dim(dmod_all, chip * ada_cols, ada_cols, axis=2)
    dmod_cols = jnp.transpose(dmod_cols, (1, 0, 2))
    g_wada, d_wada, m_wada, v_wada = ada_bwd(c_all, dmod_cols, w_ada, m_w_ada, v_w_ada, "ada_bwd")
    flat = lambda a: a.reshape(1, -1)
    g_bada, d_bada, m_bada, v_bada = [
        r.reshape(b_ada.shape) for r in
        bias_update(dmod_all.reshape(N_DEV * nb, n_l * 3 * d), flat(b_ada), flat(m_b_ada), flat(v_b_ada), "bias_update")]

    small_w = [norm_gain, a_ln_gain, a_ln_bias, a_w_s, a_b_s, b_lower_bounds, b_gn_gain, final_gain]
    small_m = [m_norm_gain, m_a_ln_gain, m_a_ln_bias, m_a_w_s, m_a_b_s, m_b_lower_bounds, m_b_gn_gain, m_final_gain]
    small_v = [v_norm_gain, v_a_ln_gain, v_a_ln_bias, v_a_w_s, v_a_b_s, v_b_lower_bounds, v_b_gn_gain, v_final_gain]
    small_g = [jnp.concatenate([dng0, dng1], axis=0), dlg, dlbias, dws, dbs, dlb, dgn, dfg]
    packed_g = _pack(small_g)
    rows = packed_g.shape[0]
    gathered = allgather_small(packed_g, "gather_small").reshape(N_DEV, rows, LANES)
    res = small_update(gathered, _pack(small_w), _pack(small_m), _pack(small_v), "small_update")
    sg, sd, sm, sv = [_unpack(r, small_w) for r in res]
    ga_in, da_in, ma_in, va_in = finish(e_ai, res[0], a_w_in, m_a_w_in, v_a_w_in, "a_in")

    def order(ng, wada, bada, ain, sm_rest, aout, bin_, bout):
        lg, lbi, ws_, bs_, lbd, gn_, fg_ = sm_rest
        return [ng, wada, bada, ain, lg, lbi, ws_, bs_, aout, bin_, lbd, gn_, bout, fg_]

    grads = order(sg[0], g_wada, g_bada, ga_in, sg[1:], ga_out, gb_in, gb_out)
    deltas = order(sd[0], d_wada, d_bada, da_in, sd[1:], da_out, db_in, db_out)
    new_m = order(sm[0], m_wada, m_bada, ma_in, sm[1:], ma_out, mb_in, mb_out)
    new_v = order(sv[0], v_wada, v_bada, va_in, sv[1:], va_out, vb_in, vb_out)
    return (loss, grad_x, *grads, *deltas, *new_m, *new_v)
```

```python
import functools

import jax
import jax.numpy as jnp
from jax import lax
from jax.experimental import pallas as pl
from jax.experimental.pallas import tpu as pltpu

F32 = jnp.float32
BF16 = jnp.bfloat16
EPS = 1e-6
CHUNK = 64
SG_BLOCK = 128
SG_GROUPS = 8
HEAD_DIM = 128
HG_WIDE = 8
HG_ROWS = 128
N_CHIPS = 4
N_DEV = 8
LANES = 128
ADAM_LR = 0.001
ADAM_B1 = 0.9
ADAM_B2 = 0.999
ADAM_EPS = 1e-08
ADAM_WD = 0.01
ADAM_STEP = 10
GELU_C0 = 0.7978845608028654
GELU_C1 = 0.044715
MESH = pl.DeviceIdType.MESH
VMEM_LIMIT = 56 * 1024 * 1024


ROW_TILE = 1024


def _col_tile(n):
    return next(t for t in (1024, 768, 512, 256) if n % t == 0)


def _call(body, **kw):
    return pl.pallas_call(body, **kw)


def _params(**kw):
    return pltpu.CompilerParams(vmem_limit_bytes=VMEM_LIMIT, **kw)


def _sigmoid(x):
    return 0.5 * jnp.tanh(0.5 * x) + 0.5


def _sigmoid_small(x):
    return 1.0 / (1.0 + jnp.exp(-x))


def _silu_and_grad(x):
    s = _sigmoid(x)
    return x * s, s * (1.0 + x * (1.0 - s))


def _gelu(x):
    return 0.5 * x * (1.0 + jnp.tanh(GELU_C0 * (x + GELU_C1 * x * x * x)))


def _gelu_and_grad(x):
    t = jnp.tanh(GELU_C0 * (x + GELU_C1 * x * x * x))
    g = 0.5 * x * (1.0 + t)
    dg = 0.5 * (1.0 + t) + 0.5 * x * (1.0 - t * t) * (GELU_C0 * (1.0 + 3.0 * GELU_C1 * x * x))
    return g, dg


def _dot(a, b, dims, precision=None):
    return lax.dot_general(a, b, (dims, ((), ())), precision=precision, preferred_element_type=F32)


NN = ((1,), (0,))
NT = ((1,), (1,))
TN = ((0,), (0,))


def _adamw(w, g, m, v):
    m = ADAM_B1 * m + (1.0 - ADAM_B1) * g
    v = ADAM_B2 * v + (1.0 - ADAM_B2) * (g * g)
    m_hat = m / (1.0 - ADAM_B1 ** ADAM_STEP)
    v_hat = v / (1.0 - ADAM_B2 ** ADAM_STEP)
    delta = -ADAM_LR * (m_hat / (jnp.sqrt(v_hat) + ADAM_EPS) + ADAM_WD * w)
    return delta, m, v


def _chunk_mask():
    r = lax.broadcasted_iota(jnp.int32, (SG_BLOCK, SG_BLOCK), 0)
    c = lax.broadcasted_iota(jnp.int32, (SG_BLOCK, SG_BLOCK), 1)
    return (c // CHUNK) <= (r // CHUNK)


def _place():
    return lax.axis_index("x"), lax.axis_index("y"), lax.axis_index("c")


def _other_chips(x, y):
    return [(1 - x, y), (x, 1 - y), (1 - x, 1 - y)]


def allgather_small(v, name):
    m_per, n = v.shape

    def body(x_ref, out_ref, send_sems, recv_sems, local_sem):
        x, y, c = _place()
        me, sibling = (x, y, c), (x, y, 1 - c)
        chips = _other_chips(x, y)

        def rows(px, py, pc):
            return out_ref.at[pl.ds((4 * px + 2 * py + pc) * m_per, m_per), :]

        def copy(k, block, to, src=None):
            return pltpu.make_async_remote_copy(
                src_ref=rows(*block) if src is None else src, dst_ref=rows(*block),
                send_sem=send_sems.at[k], recv_sem=recv_sems.at[k], device_id=to, device_id_type=MESH)

        mine = pltpu.make_async_copy(x_ref, rows(*me), local_sem)
        mine.start()
        first = [copy(0, me, sibling, src=x_ref)]
        first += [copy(1 + j, me, (*chip, c), src=x_ref) for j, chip in enumerate(chips)]
        for cp in first:
            cp.start()
        passed = [copy(4 + j, (*chip, c), sibling) for j, chip in enumerate(chips)]
        for j, chip in enumerate(chips):
            copy(1 + j, (*chip, c), me).wait_recv()
            passed[j].start()
        copy(0, sibling, me).wait_recv()
        for j, chip in enumerate(chips):
            copy(4 + j, (*chip, 1 - c), me).wait_recv()
        for cp in first + passed:
            cp.wait_send()
        mine.wait()

    return _call(
        body, name=name,
        out_shape=jax.ShapeDtypeStruct((N_DEV * m_per, n), v.dtype),
        in_specs=[pl.BlockSpec(memory_space=pltpu.VMEM)],
        out_specs=pl.BlockSpec(memory_space=pltpu.VMEM),
        scratch_shapes=[pltpu.SemaphoreType.DMA((7,)), pltpu.SemaphoreType.DMA((7,)), pltpu.SemaphoreType.DMA],
    )(v)


def _hbm_spec():
    return pl.BlockSpec(memory_space=pltpu.HBM)


def _sem_spec():
    return pl.BlockSpec(memory_space=pltpu.SEMAPHORE)


def _split_params():
    return pltpu.CompilerParams(has_side_effects=pltpu.SideEffectType.DATAFLOW_SIDE_EFFECTING)


def _hbm(a):
    return pltpu.with_memory_space_constraint(a, pltpu.HBM)


def gather_inplace(land, name):
    half = land.shape[1] // 2

    def body(land_in, land_ref, token, send_sems, recv_sems):
        del land_in
        x, y, c = _place()
        chips = _other_chips(x, y)

        def copy(k, chip_idx, core_half, to):
            rows = land_ref.at[chip_idx, pl.ds(core_half * half, half), :]
            return pltpu.make_async_remote_copy(
                src_ref=rows, dst_ref=rows, send_sem=send_sems.at[k], recv_sem=recv_sems.at[k],
                device_id=to, device_id_type=MESH)

        first = [copy(j, 2 * x + y, c, (px, py, c)) for j, (px, py) in enumerate(chips)]
        for cp in first:
            cp.start()
        passed = [copy(3 + j, 2 * px + py, c, (x, y, 1 - c)) for j, (px, py) in enumerate(chips)]
        for j, (px, py) in enumerate(chips):
            copy(j, 2 * px + py, c, (px, py, c)).wait_recv()
            passed[j].start()
        for j, (px, py) in enumerate(chips):
            copy(3 + j, 2 * px + py, 1 - c, (x, y, 1 - c)).wait_recv()
        for cp in first + passed:
            cp.wait_send()
        token[...] = jnp.zeros_like(token)

    return _call(
        body, name=name,
        out_shape=(jax.ShapeDtypeStruct(land.shape, land.dtype), jax.ShapeDtypeStruct((8, LANES), F32)),
        in_specs=[_hbm_spec()], out_specs=(_hbm_spec(), pl.BlockSpec(memory_space=pltpu.VMEM)),
        input_output_aliases={0: 0},
        scratch_shapes=[pltpu.SemaphoreType.DMA((6,)), pltpu.SemaphoreType.DMA((6,))],
    )(land)


def gather_start(land, name):
    def body(land_ref, send_sems, recv_sems, land_thru, token):
        del land_thru
        x, y, c = _place()
        for j, (px, py) in enumerate(_other_chips(x, y)):
            pltpu.make_async_remote_copy(
                src_ref=land_ref.at[2 * x + y], dst_ref=land_ref.at[2 * x + y],
                send_sem=send_sems.at[j], recv_sem=recv_sems.at[j], device_id=(px, py, c),
                device_id_type=MESH).start()
        token[...] = jnp.zeros_like(token)

    return _call(
        body, name=name,
        out_shape=(pltpu.SemaphoreType.DMA((3,)), pltpu.SemaphoreType.DMA((3,)),
                   pltpu.HBM(land.shape, land.dtype), jax.ShapeDtypeStruct((8, LANES), F32)),
        in_specs=(_hbm_spec(),),
        out_specs=(_sem_spec(), _sem_spec(), _hbm_spec(), pl.BlockSpec(memory_space=pltpu.VMEM)),
        input_output_aliases={0: 2}, compiler_params=_split_params(),
    )(_hbm(land))


def gather_wait(send_sems, recv_sems, land, after, name):
    def body(land_ref, send_sems, recv_sems, after_ref, land_out):
        del after_ref, land_out
        x, y, c = _place()
        for j, (px, py) in enumerate(_other_chips(x, y)):
            cp = pltpu.make_async_remote_copy(
                src_ref=land_ref.at[2 * x + y], dst_ref=land_ref.at[2 * px + py],
                send_sem=send_sems.at[j], recv_sem=recv_sems.at[j], device_id=(px, py, c), device_id_type=MESH)
            cp.wait_send()
            cp.wait_recv()

    return _call(
        body, name=name,
        out_shape=pltpu.HBM(land.shape, land.dtype),
        in_specs=(_hbm_spec(), _sem_spec(), _sem_spec(), pl.BlockSpec(memory_space=pl.ANY)),
        out_specs=_hbm_spec(), input_output_aliases={0: 0}, compiler_params=_split_params(),
    )(land, send_sems, recv_sems, after)


def exchange_start(parts, name):
    _, r, c_ = parts.shape

    def body(parts_ref, land_ref, send_sems, recv_sems, parts_thru, land_thru, token):
        del parts_thru, land_thru
        x, y, c = _place()
        for j, (px, py) in enumerate(_other_chips(x, y)):
            pltpu.make_async_remote_copy(
                src_ref=parts_ref.at[2 * px + py], dst_ref=land_ref.at[j],
                send_sem=send_sems.at[j], recv_sem=recv_sems.at[j], device_id=(px, py, c),
                device_id_type=MESH).start()
        token[...] = jnp.zeros_like(token)

    return _call(
        body, name=name,
        out_shape=(pltpu.SemaphoreType.DMA((3,)), pltpu.SemaphoreType.DMA((3,)),
                   pltpu.HBM(parts.shape, parts.dtype), pltpu.HBM((3, r, c_), parts.dtype),
                   jax.ShapeDtypeStruct((8, LANES), F32)),
        in_specs=(_hbm_spec(), _hbm_spec()),
        out_specs=(_sem_spec(), _sem_spec(), _hbm_spec(), _hbm_spec(), pl.BlockSpec(memory_space=pltpu.VMEM)),
        input_output_aliases={0: 2, 1: 3}, compiler_params=_split_params(),
    )(_hbm(parts), _hbm(lax.empty((3, r, c_), parts.dtype)))


def exchange_wait(send_sems, recv_sems, parts, land, after, name):
    def body(parts_ref, land_ref, send_sems, recv_sems, after_ref, parts_out, land_out):
        del after_ref, parts_out, land_out
        x, y, c = _place()
        for j, (px, py) in enumerate(_other_chips(x, y)):
            cp = pltpu.make_async_remote_copy(
                src_ref=parts_ref.at[2 * px + py], dst_ref=land_ref.at[j],
                send_sem=send_sems.at[j], recv_sem=recv_sems.at[j], device_id=(px, py, c), device_id_type=MESH)
            cp.wait_send()
            cp.wait_recv()

    return _call(
        body, name=name,
        out_shape=(pltpu.HBM(parts.shape, parts.dtype), pltpu.HBM(land.shape, land.dtype)),
        in_specs=(_hbm_spec(), _hbm_spec(), _sem_spec(), _sem_spec(), pl.BlockSpec(memory_space=pl.ANY)),
        out_specs=(_hbm_spec(), _hbm_spec()), input_output_aliases={0: 0, 1: 1},
        compiler_params=_split_params(),
    )(parts, land, send_sems, recv_sems, after)


def cast_into_slot(w, chip, after, name):
    r, c = w.shape
    tr = min(256, r)

    def body(s_ref, w_ref, after_ref, o_ref):
        del s_ref, after_ref
        o_ref[...] = w_ref[...].astype(BF16)

    return _call(
        body, name=name,
        grid_spec=pltpu.PrefetchScalarGridSpec(
            num_scalar_prefetch=1, grid=(r // tr,),
            in_specs=[pl.BlockSpec((tr, c), lambda i, s: (i, 0)), pl.BlockSpec(memory_space=pl.ANY)],
            out_specs=pl.BlockSpec((None, tr, c), lambda i, s: (s[0], i, 0))),
        out_shape=jax.ShapeDtypeStruct((N_CHIPS, r, c), BF16),
        compiler_params=_params(),
    )(chip.reshape(1).astype(jnp.int32), w, after)


def sum_parts(parts, land, chip, name):
    _, r, c = parts.shape
    tr = min(256, r)

    def body(s_ref, p_ref, l_ref, o_ref):
        del s_ref
        acc = p_ref[...].astype(F32) + l_ref[0].astype(F32)
        acc = acc + l_ref[1].astype(F32)
        o_ref[...] = acc + l_ref[2].astype(F32)

    return _call(
        body, name=name,
        grid_spec=pltpu.PrefetchScalarGridSpec(
            num_scalar_prefetch=1, grid=(r // tr,),
            in_specs=[pl.BlockSpec((None, tr, c), lambda i, s: (s[0], i, 0)),
                      pl.BlockSpec((3, tr, c), lambda i, s: (0, i, 0))],
            out_specs=pl.BlockSpec((tr, c), lambda i, s: (i, 0))),
        out_shape=jax.ShapeDtypeStruct((r, c), F32),
        compiler_params=_params(),
    )(chip.reshape(1).astype(jnp.int32), parts, land)


def swap_sibling(arrs, name):
    n = len(arrs)

    def body(*refs):
        ins, outs = refs[:n], refs[n:2 * n]
        send_sems, recv_sems = refs[2 * n:]
        x, y, c = _place()
        cps = []
        for w in range(n):
            cp = pltpu.make_async_remote_copy(
                src_ref=ins[w], dst_ref=outs[w], send_sem=send_sems.at[w], recv_sem=recv_sems.at[w],
                device_id=(x, y, 1 - c), device_id_type=MESH)
            cp.start()
            cps.append(cp)
        for cp in cps:
            cp.wait_recv()
        for cp in cps:
            cp.wait_send()

    return _call(
        body, name=name,
        out_shape=[jax.ShapeDtypeStruct(a.shape, a.dtype) for a in arrs],
        in_specs=[_hbm_spec()] * n, out_specs=[_hbm_spec()] * n,
        scratch_shapes=[pltpu.SemaphoreType.DMA((n,)), pltpu.SemaphoreType.DMA((n,))],
    )(*arrs)


def adamw_pair(pa, pb, w, m, v, name):
    r, c = w.shape
    tr = min(128, r)

    def body(pa_ref, pb_ref, w_ref, m_ref, v_ref, g_ref, d_ref, nm_ref, nv_ref):
        g = pa_ref[...] + pb_ref[...]
        d, nm, nv = _adamw(w_ref[...], g, m_ref[...], v_ref[...])
        g_ref[...] = g
        d_ref[...] = d
        nm_ref[...] = nm
        nv_ref[...] = nv

    spec = pl.BlockSpec((tr, c), lambda i: (i, 0))
    return _call(
        body, name=name, grid=(r // tr,),
        out_shape=[jax.ShapeDtypeStruct((r, c), F32)] * 4,
        in_specs=[spec] * 5, out_specs=[spec] * 4,
        compiler_params=_params(),
    )(pa, pb, w, m, v)


def small_update(gathered, w, m, v, name):
    def body(g_ref, w_ref, m_ref, v_ref, go_ref, d_ref, nm_ref, nv_ref):
        g = g_ref[0]
        for k in range(1, N_DEV):
            g = g + g_ref[k]
        d, nm, nv = _adamw(w_ref[...], g, m_ref[...], v_ref[...])
        go_ref[...] = g
        d_ref[...] = d
        nm_ref[...] = nm
        nv_ref[...] = nv

    return _call(
        body, name=name,
        out_shape=[jax.ShapeDtypeStruct(w.shape, F32)] * 4,
        compiler_params=_params(),
    )(gathered, w, m, v)


def ada_fwd(c_all, w_ada, b_cols, name):
    n_l, d, cols = w_ada.shape
    nb = c_all.shape[0]
    tn = 256

    def body(c_ref, w_ref, b_ref, o_ref):
        cv = c_ref[...]
        ca = (cv * _sigmoid(cv)).astype(BF16)
        o_ref[...] = _dot(ca, w_ref[...].astype(BF16), NN) + b_ref[...]

    return _call(
        body, name=name, grid=(n_l, cols // tn),
        out_shape=jax.ShapeDtypeStruct((n_l, nb, cols), F32),
        in_specs=[pl.BlockSpec((nb, d), lambda l, j: (0, 0)),
                  pl.BlockSpec((None, d, tn), lambda l, j: (l, 0, j)),
                  pl.BlockSpec((None, 1, tn), lambda l, j: (l, 0, j))],
        out_specs=pl.BlockSpec((None, nb, tn), lambda l, j: (l, 0, j)),
        compiler_params=_params(),
    )(c_all, w_ada, b_cols)


def ada_bwd(c_all, dmod_cols, w, m, v, name):
    n_l, d, cols = w.shape
    nb = c_all.shape[0]
    tn = 256

    def body(c_ref, dm_ref, w_ref, m_ref, v_ref, g_ref, d_ref, nm_ref, nv_ref):
        cv = c_ref[...]
        ca = (cv * _sigmoid(cv)).astype(BF16)
        g = _dot(ca, dm_ref[...].astype(BF16), TN)
        dl, nm, nv = _adamw(w_ref[...], g, m_ref[...], v_ref[...])
        g_ref[...] = g
        d_ref[...] = dl
        nm_ref[...] = nm
        nv_ref[...] = nv

    wspec = pl.BlockSpec((None, d, tn), lambda l, j: (l, 0, j))
    return _call(
        body, name=name, grid=(n_l, cols // tn),
        out_shape=[jax.ShapeDtypeStruct((n_l, d, cols), F32)] * 4,
        in_specs=[pl.BlockSpec((nb, d), lambda l, j: (0, 0)),
                  pl.BlockSpec((None, nb, tn), lambda l, j: (l, 0, j)),
                  wspec, wspec, wspec],
        out_specs=[wspec] * 4,
        compiler_params=_params(),
    )(c_all, dmod_cols, w, m, v)


def bias_update(dmod_all, w, m, v, name):
    def body(dm_ref, w_ref, m_ref, v_ref, g_ref, d_ref, nm_ref, nv_ref):
        g = jnp.sum(dm_ref[...], axis=0, keepdims=True)
        dl, nm, nv = _adamw(w_ref[...], g, m_ref[...], v_ref[...])
        g_ref[...] = g
        d_ref[...] = dl
        nm_ref[...] = nm
        nv_ref[...] = nv

    return _call(
        body, name=name,
        out_shape=[jax.ShapeDtypeStruct(w.shape, F32)] * 4,
        compiler_params=_params(),
    )(dmod_all, w, m, v)


def inproj_fwd(x, mod, ng, wg, seq, sectioned, name):
    m_rows, d = x.shape
    nsh, _, ns = wg.shape
    n = nsh * ns
    tm, tn = min(ROW_TILE, seq), 512
    per = ns // tn

    def body(x_ref, mod_ref, ng_ref, w_ref, proj_ref, h_ref):
        @pl.when(pl.program_id(1) == 0)
        def _():
            xv = x_ref[...]
            r = lax.rsqrt(jnp.mean(xv * xv, axis=-1, keepdims=True) + EPS)
            md = mod_ref[0]
            h = (xv * r * ng_ref[...]) * (1.0 + md[:, d:2 * d]) + md[:, :d]
            h_ref[...] = h.astype(BF16)
        proj_ref[...] = _dot(h_ref[...], w_ref[...], NN)

    if sectioned:
        proj_shape = (nsh, m_rows, ns)
        proj_spec = pl.BlockSpec((None, tm, tn), lambda i, j: (j // per, i, j % per))
    else:
        proj_shape = (m_rows, n)
        proj_spec = pl.BlockSpec((tm, tn), lambda i, j: (i, j))
    return _call(
        body, name=name, grid=(m_rows // tm, n // tn),
        out_shape=[jax.ShapeDtypeStruct(proj_shape, F32), jax.ShapeDtypeStruct((m_rows, d), BF16)],
        in_specs=[pl.BlockSpec((tm, d), lambda i, j: (i, 0)),
                  pl.BlockSpec((1, 1, 3 * d), lambda i, j: ((i * tm) // seq, 0, 0)),
                  pl.BlockSpec((1, d), lambda i, j: (0, 0)),
                  pl.BlockSpec((None, d, tn), lambda i, j: (j // per, 0, j % per))],
        out_specs=[proj_spec, pl.BlockSpec((tm, d), lambda i, j: (i, 0))],
        compiler_params=_params(),
    )(x, mod, ng, wg)


def outproj_fwd(y, w, x, mod, seq, name):
    m_rows, di = y.shape
    d = w.shape[1]
    tm = min(ROW_TILE, seq)

    def body(y_ref, w_ref, x_ref, mod_ref, xn_ref, out_ref):
        acc = _dot(y_ref[...], w_ref[...], NN)
        out_ref[...] = acc
        xn_ref[...] = x_ref[...] + mod_ref[0][:, 2 * d:] * acc

    row = pl.BlockSpec((tm, d), lambda i: (i, 0))
    return _call(
        body, name=name, grid=(m_rows // tm,),
        out_shape=[jax.ShapeDtypeStruct((m_rows, d), F32)] * 2,
        in_specs=[pl.BlockSpec((tm, di), lambda i: (i, 0)),
                  pl.BlockSpec((di, d), lambda i: (0, 0)),
                  row,
                  pl.BlockSpec((1, 1, 3 * d), lambda i: ((i * tm) // seq, 0, 0))],
        out_specs=[row, row],
        compiler_params=_params(),
    )(y, w, x, mod)


def outproj_bwd(dxo, out, mod, w, seq, name):
    m_rows, d = dxo.shape
    di = w.shape[0]
    nb = m_rows // seq
    tm, tn = min(ROW_TILE, seq), 512

    def body(dxo_ref, out_ref, mod_ref, w_ref, dy_ref, dout_ref, dgate_ref):
        i = pl.program_id(0)

        @pl.when(pl.program_id(1) == 0)
        def _():
            dx = dxo_ref[...]
            dout_ref[...] = (mod_ref[0][:, 2 * d:] * dx).astype(BF16)
            part = jnp.sum(dx * out_ref[...], axis=0, keepdims=True)

            @pl.when((i * tm) % seq == 0)
            def _():
                dgate_ref[0] = part

            @pl.when((i * tm) % seq != 0)
            def _():
                dgate_ref[0] = dgate_ref[0] + part

        dy_ref[...] = _dot(dout_ref[...], w_ref[...], NT)

    row = pl.BlockSpec((tm, d), lambda i, j: (i, 0))
    return _call(
        body, name=name, grid=(m_rows // tm, di // tn),
        out_shape=[jax.ShapeDtypeStruct((m_rows, di), F32), jax.ShapeDtypeStruct((m_rows, d), BF16),
                   jax.ShapeDtypeStruct((nb, 1, d), F32)],
        in_specs=[row, row,
                  pl.BlockSpec((1, 1, 3 * d), lambda i, j: ((i * tm) // seq, 0, 0)),
                  pl.BlockSpec((tn, d), lambda i, j: (j, 0))],
        out_specs=[pl.BlockSpec((tm, tn), lambda i, j: (i, j)), row,
                   pl.BlockSpec((1, 1, d), lambda i, j: ((i * tm) // seq, 0, 0))],
        compiler_params=_params(),
    )(dxo, out, mod, w)


def grad_w_out(y, dout, name):
    m_rows, di = y.shape
    d = dout.shape[1]
    tm, tk = min(512, m_rows), _col_tile(di)
    n_m = m_rows // tm

    def body(y_ref, do_ref, o_ref, acc_ref):
        mi = pl.program_id(1)
        part = _dot(y_ref[...], do_ref[...], TN)

        @pl.when(mi == 0)
        def _():
            acc_ref[...] = part

        @pl.when(mi != 0)
        def _():
            acc_ref[...] = acc_ref[...] + part

        @pl.when(mi == n_m - 1)
        def _():
            o_ref[...] = acc_ref[...].astype(BF16)

    return _call(
        body, name=name, grid=(di // tk, n_m),
        out_shape=jax.ShapeDtypeStruct((di, d), BF16),
        in_specs=[pl.BlockSpec((tm, tk), lambda j, mi: (mi, j)),
                  pl.BlockSpec((tm, d), lambda j, mi: (mi, 0))],
        out_specs=pl.BlockSpec((tk, d), lambda j, mi: (j, 0)),
        scratch_shapes=[pltpu.VMEM((tk, d), F32)],
        compiler_params=_params(),
    )(y, dout)


def grad_w_in(h, dproj, nsh, sectioned, name):
    m_rows, d = h.shape
    n = dproj.shape[0] * dproj.shape[2] if sectioned else dproj.shape[1]
    ns = n // nsh
    tm, tn = min(512, m_rows), _col_tile(ns)
    per = ns // tn
    n_m = m_rows // tm

    def body(h_ref, dp_ref, o_ref, acc_ref):
        mi = pl.program_id(1)
        part = _dot(h_ref[...], dp_ref[...], TN)

        @pl.when(mi == 0)
        def _():
            acc_ref[...] = part

        @pl.when(mi != 0)
        def _():
            acc_ref[...] = acc_ref[...] + part

        @pl.when(mi == n_m - 1)
        def _():
            o_ref[...] = acc_ref[...].astype(BF16)

    if sectioned:
        dp_spec = pl.BlockSpec((None, tm, tn), lambda j, mi: (j // per, mi, j % per))
    else:
        dp_spec = pl.BlockSpec((tm, tn), lambda j, mi: (mi, j))
    return _call(
        body, name=name, grid=(n // tn, n_m),
        out_shape=jax.ShapeDtypeStruct((nsh, d, ns), BF16),
        in_specs=[pl.BlockSpec((tm, d), lambda j, mi: (mi, 0)), dp_spec],
        out_specs=pl.BlockSpec((None, d, tn), lambda j, mi: (j // per, 0, j % per)),
        scratch_shapes=[pltpu.VMEM((d, tn), F32)],
        compiler_params=_params(),
    )(h, dproj)


def inproj_bwd(dproj, wg, x, dxo, mod, ng, seq, sectioned, name):
    m_rows, d = x.shape
    nsh, _, ns = wg.shape
    n = nsh * ns
    nb = m_rows // seq
    tm, tk = min(ROW_TILE, seq), 512
    per = ns // tk
    n_k = n // tk

    def body(dp_ref, w_ref, x_ref, dxo_ref, mod_ref, ng_ref, dxi_ref, dsh_ref, dsc_ref, dng_ref, acc_ref):
        i, k = pl.program_id(0), pl.program_id(1)
        part = _dot(dp_ref[...], w_ref[...], NT)

        @pl.when(k == 0)
        def _():
            acc_ref[...] = part

        @pl.when(k != 0)
        def _():
            acc_ref[...] = acc_ref[...] + part

        @pl.when(k == n_k - 1)
        def _():
            dh = acc_ref[...]
            xv = x_ref[...]
            r = lax.rsqrt(jnp.mean(xv * xv, axis=-1, keepdims=True) + EPS)
            xn = xv * r
            md = mod_ref[0]
            gain = ng_ref[...]
            p_shift = jnp.sum(dh, axis=0, keepdims=True)
            p_scale = jnp.sum(dh * (xn * gain), axis=0, keepdims=True)
            drn = dh * (1.0 + md[:, d:2 * d])
            p_ng = jnp.sum(drn * xn, axis=0, keepdims=True)
            dxn = drn * gain
            dx = r * (dxn - xn * jnp.mean(dxn * xn, axis=-1, keepdims=True))
            dxi_ref[...] = dxo_ref[...] + dx

            @pl.when((i * tm) % seq == 0)
            def _():
                dsh_ref[0] = p_shift
                dsc_ref[0] = p_scale

            @pl.when((i * tm) % seq != 0)
            def _():
                dsh_ref[0] = dsh_ref[0] + p_shift
                dsc_ref[0] = dsc_ref[0] + p_scale

            @pl.when(i == 0)
            def _():
                dng_ref[...] = p_ng

            @pl.when(i != 0)
            def _():
                dng_ref[...] = dng_ref[...] + p_ng

    if sectioned:
        dp_spec = pl.BlockSpec((None, tm, tk), lambda i, k: (k // per, i, k % per))
    else:
        dp_spec = pl.BlockSpec((tm, tk), lambda i, k: (i, k))
    row = pl.BlockSpec((tm, d), lambda i, k: (i, 0))
    per_seq = pl.BlockSpec((1, 1, d), lambda i, k: ((i * tm) // seq, 0, 0))
    return _call(
        body, name=name, grid=(m_rows // tm, n_k),
        out_shape=[jax.ShapeDtypeStruct((m_rows, d), F32), jax.ShapeDtypeStruct((nb, 1, d), F32),
                   jax.ShapeDtypeStruct((nb, 1, d), F32), jax.ShapeDtypeStruct((1, d), F32)],
        in_specs=[dp_spec,
                  pl.BlockSpec((None, d, tk), lambda i, k: (k // per, 0, k % per)),
                  row, row,
                  pl.BlockSpec((1, 1, 3 * d), lambda i, k: ((i * tm) // seq, 0, 0)),
                  pl.BlockSpec((1, d), lambda i, k: (0, 0))],
        out_specs=[row, per_seq, per_seq, pl.BlockSpec((1, d), lambda i, k: (0, 0))],
        scratch_shapes=[pltpu.VMEM((tm, d), F32)],
        compiler_params=_params(),
    )(dproj, wg, x, dxo, mod, ng)


def _sgu_stats(proj_ref, vg_ref, di, gd):
    s1 = jnp.zeros((SG_BLOCK, 1), F32)
    for g in range(SG_GROUPS):
        vg = _gelu(proj_ref[:, di + g * gd:di + (g + 1) * gd])
        vg_ref[:, g * gd:(g + 1) * gd] = vg
        s1 = s1 + jnp.sum(vg, axis=1, keepdims=True)
    mu = s1 / di
    s2 = jnp.zeros((SG_BLOCK, 1), F32)
    for g in range(SG_GROUPS):
        dv = vg_ref[:, g * gd:(g + 1) * gd] - mu
        s2 = s2 + jnp.sum(dv * dv, axis=1, keepdims=True)
    return mu, lax.rsqrt(s2 / di + EPS)


def sgu_fwd(proj, ln_gain, ln_bias, ws, bs, name):
    m_rows, n3 = proj.shape
    di = n3 // 3
    gd = di // SG_GROUPS

    def body(proj_ref, lg_ref, lb_ref, ws_ref, bs_ref, y_ref, wsm_ref, vg_ref):
        @pl.when(pl.program_id(0) == 0)
        def _():
            mask = _chunk_mask()
            for g in range(SG_GROUPS):
                wsm_ref[g] = jnp.where(mask, ws_ref[g], 0.0).astype(BF16)

        mu, rstd = _sgu_stats(proj_ref, vg_ref, di, gd)
        for g in range(SG_GROUPS):
            cs = slice(g * gd, (g + 1) * gd)
            vln = (vg_ref[:, cs] - mu) * rstd * lg_ref[:, cs] + lb_ref[:, cs]
            s = _dot(wsm_ref[g], vln.astype(BF16), NN) + bs_ref[g]
            u = _gelu(proj_ref[:, cs])
            gp = proj_ref[:, 2 * di + g * gd:2 * di + (g + 1) * gd]
            y_ref[:, cs] = (u * s * (gp * _sigmoid(gp))).astype(BF16)

    full = lambda shape: pl.BlockSpec(shape, lambda i: (0,) * len(shape))
    return _call(
        body, name=name, grid=(m_rows // SG_BLOCK,),
        out_shape=jax.ShapeDtypeStruct((m_rows, di), BF16),
        in_specs=[pl.BlockSpec((SG_BLOCK, n3), lambda i: (i, 0)),
                  full((1, di)), full((1, di)),
                  full((SG_GROUPS, SG_BLOCK, SG_BLOCK)), full((SG_GROUPS, SG_BLOCK, 1))],
        out_specs=pl.BlockSpec((SG_BLOCK, di), lambda i: (i, 0)),
        scratch_shapes=[pltpu.VMEM((SG_GROUPS, SG_BLOCK, SG_BLOCK), BF16), pltpu.VMEM((SG_BLOCK, di), F32)],
        compiler_params=_params(),
    )(proj, ln_gain, ln_bias, ws, bs)


def sgu_bwd(proj, dy, ln_gain, ln_bias, ws, bs, name):
    m_rows, n3 = proj.shape
    di = n3 // 3
    gd = di // SG_GROUPS
    n_i = m_rows // SG_BLOCK

    def body(proj_ref, dy_ref, lg_ref, lb_ref, ws_ref, bs_ref,
             dp_ref, dws_ref, dbs_ref, dlg_ref, dlb_ref, wsm_ref, vg_ref, dvh_ref):
        i = pl.program_id(0)

        @pl.when(i == 0)
        def _():
            mask = _chunk_mask()
            for g in range(SG_GROUPS):
                wsm_ref[g] = jnp.where(mask, ws_ref[g], 0.0).astype(BF16)
            dws_ref[...] = jnp.zeros_like(dws_ref)
            dbs_ref[...] = jnp.zeros_like(dbs_ref)
            dlg_ref[...] = jnp.zeros_like(dlg_ref)
            dlb_ref[...] = jnp.zeros_like(dlb_ref)

        mu, rstd = _sgu_stats(proj_ref, vg_ref, di, gd)
        m1 = jnp.zeros((SG_BLOCK, 1), F32)
        m2 = jnp.zeros((SG_BLOCK, 1), F32)
        for g in range(SG_GROUPS):
            cs = slice(g * gd, (g + 1) * gd)
            gs = slice(2 * di + g * gd, 2 * di + (g + 1) * gd)
            gain = lg_ref[:, cs]
            vhat = (vg_ref[:, cs] - mu) * rstd
            vln_b = (vhat * gain + lb_ref[:, cs]).astype(BF16)
            s = _dot(wsm_ref[g], vln_b, NN) + bs_ref[g]
            u, du = _gelu_and_grad(proj_ref[:, cs])
            sg, dsg = _silu_and_grad(proj_ref[:, gs])
            dyv = dy_ref[:, cs]
            dp_ref[:, cs] = (dyv * s * sg * du).astype(BF16)
            dp_ref[:, gs] = (dyv * u * s * dsg).astype(BF16)
            ds = dyv * u * sg
            ds_b = ds.astype(BF16)
            dws_ref[g] = dws_ref[g] + _dot(ds_b, vln_b, NT)
            dbs_ref[g] = dbs_ref[g] + jnp.sum(ds, axis=1, keepdims=True)
            dvln = _dot(wsm_ref[g], ds_b, TN)
            dlg_ref[:, cs] = dlg_ref[:, cs] + jnp.sum(dvln * vhat, axis=0, keepdims=True)
            dlb_ref[:, cs] = dlb_ref[:, cs] + jnp.sum(dvln, axis=0, keepdims=True)
            dvh = dvln * gain
            dvh_ref[:, cs] = dvh
            m1 = m1 + jnp.sum(dvh, axis=1, keepdims=True)
            m2 = m2 + jnp.sum(dvh * vhat, axis=1, keepdims=True)
        m1 = m1 / di
        m2 = m2 / di
        for g in range(SG_GROUPS):
            cs = slice(g * gd, (g + 1) * gd)
            vs = slice(di + g * gd, di + (g + 1) * gd)
            vhat = (vg_ref[:, cs] - mu) * rstd
            dvg = rstd * (dvh_ref[:, cs] - m1 - vhat * m2)
            _, dgel = _gelu_and_grad(proj_ref[:, vs])
            dp_ref[:, vs] = (dvg * dgel).astype(BF16)

        @pl.when(i == n_i - 1)
        def _():
            mask = _chunk_mask()
            for g in range(SG_GROUPS):
                dws_ref[g] = jnp.where(mask, dws_ref[g], 0.0)

    full = lambda shape: pl.BlockSpec(shape, lambda i: (0,) * len(shape))
    return _call(
        body, name=name, grid=(n_i,),
        out_shape=[jax.ShapeDtypeStruct((m_rows, n3), BF16),
                   jax.ShapeDtypeStruct((SG_GROUPS, SG_BLOCK, SG_BLOCK), F32),
                   jax.ShapeDtypeStruct((SG_GROUPS, SG_BLOCK, 1), F32),
                   jax.ShapeDtypeStruct((1, di), F32), jax.ShapeDtypeStruct((1, di), F32)],
        in_specs=[pl.BlockSpec((SG_BLOCK, n3), lambda i: (i, 0)),
                  pl.BlockSpec((SG_BLOCK, di), lambda i: (i, 0)),
                  full((1, di)), full((1, di)),
                  full((SG_GROUPS, SG_BLOCK, SG_BLOCK)), full((SG_GROUPS, SG_BLOCK, 1))],
        out_specs=[pl.BlockSpec((SG_BLOCK, n3), lambda i: (i, 0)),
                   full((SG_GROUPS, SG_BLOCK, SG_BLOCK)), full((SG_GROUPS, SG_BLOCK, 1)),
                   full((1, di)), full((1, di))],
        scratch_shapes=[pltpu.VMEM((SG_GROUPS, SG_BLOCK, SG_BLOCK), BF16),
                        pltpu.VMEM((SG_BLOCK, di), F32), pltpu.VMEM((SG_BLOCK, di), F32)],
        compiler_params=_params(),
    )(proj, dy, ln_gain, ln_bias, ws, bs)


def _lower_bound(lbraw):
    mx = jnp.maximum(lbraw[0:1, :], lbraw[1:2, :])
    e0 = jnp.exp(lbraw[0:1, :] - mx)
    e1 = jnp.exp(lbraw[1:2, :] - mx)
    p0 = e0 / (e0 + e1)
    p1 = e1 / (e0 + e1)
    return (p0 + p1) - p0, p0, p1


def _tri(lower):
    r = lax.broadcasted_iota(jnp.int32, (CHUNK, CHUNK), 0)
    c = lax.broadcasted_iota(jnp.int32, (CHUNK, CHUNK), 1)
    return ((r >= c) if lower else (c >= r)).astype(F32)


def _row(a, idx):
    r = lax.broadcasted_iota(jnp.int32, a.shape, 0)
    return jnp.sum(jnp.where(r == idx, a, 0.0), axis=0, keepdims=True)


def _hgrn_gates(qp, fp, lb, tri):
    sgm = _sigmoid_small(fp)
    f = lb + (1.0 - lb) * sgm
    k = 1.0 - f
    a = _dot(tri, jnp.log(f), NN, precision=lax.Precision.HIGHEST)
    a_mid = _row(a, CHUNK // 2 - 1)
    a_last = _row(a, CHUNK - 1)
    q, dq = _silu_and_grad(qp)
    e1, e2, e3, e4 = jnp.exp(a - a_mid), jnp.exp(a_mid - a), jnp.exp(a), jnp.exp(a_last - a)
    return dict(sgm=sgm, f=f, k=k, q=q, dq=dq, e1=e1, e2=e2, e3=e3, e4=e4, dec=jnp.exp(a_last),
                q_in=q * e1, k_in=k * e2, q_out=q * e3, k_out=k * e4)


def _causal():
    r = lax.broadcasted_iota(jnp.int32, (CHUNK, CHUNK), 0)
    c = lax.broadcasted_iota(jnp.int32, (CHUNK, CHUNK), 1)
    return r >= c


def hgrn_fwd(proj4, lbraw, gn, seq, name):
    _, m_rows, di = proj4.shape
    nb, nh, nc = m_rows // seq, di // HEAD_DIM, seq // CHUNK
    rows = min(HG_ROWS, seq)
    wide = HG_WIDE * HEAD_DIM
    ns, cpb = seq // rows, rows // CHUNK

    def body(p_ref, lb_ref, gn_ref, y_ref, sts_ref, st_ref):
        @pl.when(pl.program_id(2) == 0)
        def _():
            st_ref[...] = jnp.zeros_like(st_ref)

        tri = _tri(True)
        causal = _causal()
        gain = gn_ref[...]
        lbs = [_lower_bound(lb_ref[:, j * HEAD_DIM:(j + 1) * HEAD_DIM])[0] for j in range(HG_WIDE)]

        units = [(n, j) for n in range(cpb) for j in range(HG_WIDE)]
        rs = lambda n: slice(n * CHUNK, (n + 1) * CHUNK)
        cs = lambda j: slice(j * HEAD_DIM, (j + 1) * HEAD_DIM)
        gates, v_b, sc_b, kv, o_in, o_x = {}, {}, {}, {}, {}, {}
        for n, j in units:
            gates[n, j] = _hgrn_gates(p_ref[0, rs(n), cs(j)], p_ref[1, rs(n), cs(j)], lbs[j], tri)
            v_b[n, j] = p_ref[2, rs(n), cs(j)].astype(BF16)
        for u in units:
            t = gates[u]
            sc_b[u] = jnp.where(causal, _dot(t["q_in"].astype(BF16), t["k_in"].astype(BF16), NT), 0.0).astype(BF16)
            kv[u] = _dot(v_b[u], t["k_out"].astype(BF16), TN)
        for u in units:
            o_in[u] = _dot(sc_b[u], v_b[u], NN)
        for j in range(HG_WIDE):
            st = st_ref[j]
            for n in range(cpb):
                sts_ref[n, :, cs(j)] = st
                o_x[n, j] = _dot(gates[n, j]["q_out"].astype(BF16), st.astype(BF16), NT)
                st = st * gates[n, j]["dec"] + kv[n, j]
            st_ref[j] = st
        for n, j in units:
            o = o_in[n, j] + o_x[n, j]
            r = lax.rsqrt(jnp.mean(o * o, axis=-1, keepdims=True) + EPS)
            gp = p_ref[3, rs(n), cs(j)]
            y_ref[rs(n), cs(j)] = ((o * r * gain) * (gp * _sigmoid(gp))).astype(BF16)

    return _call(
        body, name=name, grid=(nh // HG_WIDE, nb, ns),
        out_shape=[jax.ShapeDtypeStruct((m_rows, di), BF16),
                   jax.ShapeDtypeStruct((nb * nc, HEAD_DIM, di), F32)],
        in_specs=[pl.BlockSpec((4, rows, wide), lambda hg, b, s: (0, b * ns + s, hg)),
                  pl.BlockSpec((2, wide), lambda hg, b, s: (0, hg)),
                  pl.BlockSpec((1, HEAD_DIM), lambda hg, b, s: (0, 0))],
        out_specs=[pl.BlockSpec((rows, wide), lambda hg, b, s: (b * ns + s, hg)),
                   pl.BlockSpec((cpb, HEAD_DIM, wide), lambda hg, b, s: (b * ns + s, 0, hg))],
        scratch_shapes=[pltpu.VMEM((HG_WIDE, HEAD_DIM, HEAD_DIM), F32)],
        compiler_params=_params(),
    )(proj4, lbraw, gn)


def hgrn_bwd(proj4, dy, sts, lbraw, gn, seq, name):
    _, m_rows, di = proj4.shape
    nb, nh, nc = m_rows // seq, di // HEAD_DIM, seq // CHUNK
    rows = min(HG_ROWS, seq)
    wide = HG_WIDE * HEAD_DIM
    ns, cpb = seq // rows, rows // CHUNK
    n_hg = nh // HG_WIDE

    def body(p_ref, dy_ref, sts_ref, lb_ref, gn_ref, dp_ref, dlb_ref, dgn_ref, dst_ref, lbacc_ref, gnacc_ref):
        hg, b, s = pl.program_id(0), pl.program_id(1), pl.program_id(2)
        tri, triu = _tri(True), _tri(False)
        causal = _causal()
        gain = gn_ref[...]
        first = (b == 0) & (s == 0)

        @pl.when((hg == 0) & first)
        def _():
            gnacc_ref[...] = jnp.zeros_like(gnacc_ref)

        @pl.when(first)
        def _():
            lbacc_ref[...] = jnp.zeros_like(lbacc_ref)

        @pl.when(s == 0)
        def _():
            dst_ref[...] = jnp.zeros_like(dst_ref)

        units = [(n, j) for n in range(cpb) for j in range(HG_WIDE)]
        rs = lambda n: slice(n * CHUNK, (n + 1) * CHUNK)
        cs = lambda j: slice(j * HEAD_DIM, (j + 1) * HEAD_DIM)
        lbs = [_lower_bound(lb_ref[:, cs(j)])[0] for j in range(HG_WIDE)]
        gates, v_b, st_b, sc_b, o, do_b = {}, {}, {}, {}, {}, {}
        dq_out, dsc_b, dv, g_st, dq_in, dk_in, dst_at, dk_out, ddec = {}, {}, {}, {}, {}, {}, {}, {}, {}
        for n, j in units:
            gates[n, j] = _hgrn_gates(p_ref[0, rs(n), cs(j)], p_ref[1, rs(n), cs(j)], lbs[j], tri)
            v_b[n, j] = p_ref[2, rs(n), cs(j)].astype(BF16)
            st_b[n, j] = sts_ref[n, :, cs(j)].astype(BF16)
        for u in units:
            t = gates[u]
            sc_b[u] = jnp.where(causal, _dot(t["q_in"].astype(BF16), t["k_in"].astype(BF16), NT), 0.0).astype(BF16)
        for u in units:
            o[u] = _dot(sc_b[u], v_b[u], NN) + _dot(gates[u]["q_out"].astype(BF16), st_b[u], NT)
        for n, j in units:
            ov = o[n, j]
            r = lax.rsqrt(jnp.mean(ov * ov, axis=-1, keepdims=True) + EPS)
            ohat = ov * r
            sg, dsg = _silu_and_grad(p_ref[3, rs(n), cs(j)])
            dyv = dy_ref[rs(n), cs(j)]
            dp_ref[3, rs(n), cs(j)] = (dyv * (ohat * gain) * dsg).astype(BF16)
            d_on = dyv * sg
            gnacc_ref[:, cs(j)] = gnacc_ref[:, cs(j)] + jnp.sum(d_on * ohat, axis=0, keepdims=True)
            dohat = d_on * gain
            do_b[n, j] = (r * (dohat - ohat * jnp.mean(dohat * ohat, axis=-1, keepdims=True))).astype(BF16)
        for u in units:
            dq_out[u] = _dot(do_b[u], st_b[u], NN)
            dsc_b[u] = jnp.where(causal, _dot(do_b[u], v_b[u], NT), 0.0).astype(BF16)
            dv[u] = _dot(sc_b[u], do_b[u], TN)
            g_st[u] = _dot(do_b[u], gates[u]["q_out"].astype(BF16), TN)
        for u in units:
            dq_in[u] = _dot(dsc_b[u], gates[u]["k_in"].astype(BF16), NN)
            dk_in[u] = _dot(dsc_b[u], gates[u]["q_in"].astype(BF16), TN)
        for j in range(HG_WIDE):
            dst = dst_ref[j]
            for n in reversed(range(cpb)):
                dst_at[n, j] = dst
                dst = dst * gates[n, j]["dec"] + g_st[n, j]
            dst_ref[j] = dst
        for n, j in units:
            dst = dst_at[n, j]
            dst_b = dst.astype(BF16)
            dk_out[n, j] = _dot(v_b[n, j], dst_b, NN)
            dv[n, j] = dv[n, j] + _dot(gates[n, j]["k_out"].astype(BF16), dst_b, NT)
            ddec[n, j] = jnp.sum(dst * sts_ref[n, :, cs(j)], axis=0, keepdims=True)
        for n, j in units:
            t = gates[n, j]
            dp_ref[2, rs(n), cs(j)] = dv[n, j].astype(BF16)
            dq = dq_in[n, j] * t["e1"] + dq_out[n, j] * t["e3"]
            dk = dk_in[n, j] * t["e2"] + dk_out[n, j] * t["e4"]
            w_in = dq_in[n, j] * t["q_in"] - dk_in[n, j] * t["k_in"]
            w_out = dk_out[n, j] * t["k_out"]
            da = w_in + dq_out[n, j] * t["q_out"] - w_out
            da_mid = -jnp.sum(w_in, axis=0, keepdims=True)
            da_last = jnp.sum(w_out, axis=0, keepdims=True) + ddec[n, j] * t["dec"]
            rid = lax.broadcasted_iota(jnp.int32, da.shape, 0)
            da = da + jnp.where(rid == CHUNK // 2 - 1, da_mid, 0.0) + jnp.where(rid == CHUNK - 1, da_last, 0.0)
            dlf = _dot(triu, da, NN, precision=lax.Precision.HIGHEST)
            df = dlf / t["f"] - dk
            sgm = t["sgm"]
            dp_ref[1, rs(n), cs(j)] = (df * (1.0 - lbs[j]) * sgm * (1.0 - sgm)).astype(BF16)
            lbacc_ref[:, cs(j)] = lbacc_ref[:, cs(j)] + jnp.sum(df * (1.0 - sgm), axis=0, keepdims=True)
            dp_ref[0, rs(n), cs(j)] = (dq * t["dq"]).astype(BF16)

        @pl.when((b == nb - 1) & (s == ns - 1))
        def _():
            for j in range(HG_WIDE):
                cs = slice(j * HEAD_DIM, (j + 1) * HEAD_DIM)
                _, p0, p1 = _lower_bound(lb_ref[:, cs])
                acc = lbacc_ref[:, cs]
                dlb_ref[0:1, cs] = -acc * p0 * p1
                dlb_ref[1:2, cs] = acc * p1 * (1.0 - p1)

        @pl.when((hg == n_hg - 1) & (b == nb - 1) & (s == ns - 1))
        def _():
            tot = gnacc_ref[:, 0:HEAD_DIM]
            for j in range(1, HG_WIDE):
                tot = tot + gnacc_ref[:, j * HEAD_DIM:(j + 1) * HEAD_DIM]
            dgn_ref[...] = tot

    blk = lambda hg, b, s: b * ns + (ns - 1 - s)
    return _call(
        body, name=name, grid=(n_hg, nb, ns),
        out_shape=[jax.ShapeDtypeStruct((4, m_rows, di), BF16), jax.ShapeDtypeStruct((2, di), F32),
                   jax.ShapeDtypeStruct((1, HEAD_DIM), F32)],
        in_specs=[pl.BlockSpec((4, rows, wide), lambda hg, b, s: (0, blk(hg, b, s), hg)),
                  pl.BlockSpec((rows, wide), lambda hg, b, s: (blk(hg, b, s), hg)),
                  pl.BlockSpec((cpb, HEAD_DIM, wide), lambda hg, b, s: (blk(hg, b, s), 0, hg)),
                  pl.BlockSpec((2, wide), lambda hg, b, s: (0, hg)),
                  pl.BlockSpec((1, HEAD_DIM), lambda hg, b, s: (0, 0))],
        out_specs=[pl.BlockSpec((4, rows, wide), lambda hg, b, s: (0, blk(hg, b, s), hg)),
                   pl.BlockSpec((2, wide), lambda hg, b, s: (0, hg)),
                   pl.BlockSpec((1, HEAD_DIM), lambda hg, b, s: (0, 0))],
        scratch_shapes=[pltpu.VMEM((HG_WIDE, HEAD_DIM, HEAD_DIM), F32), pltpu.VMEM((1, wide), F32),
                        pltpu.VMEM((1, wide), F32)],
        compiler_params=_params(),
    )(proj4, dy, sts, lbraw, gn)


def final_loss(x, fg, target, name):
    m_rows, d = x.shape
    tm = min(512, m_rows)

    def body(x_ref, fg_ref, t_ref, loss_ref, dx_ref, dfg_ref):
        i = pl.program_id(0)
        xv = x_ref[...]
        gain = fg_ref[...]
        r = lax.rsqrt(jnp.mean(xv * xv, axis=-1, keepdims=True) + EPS)
        xn = xv * r
        e = xn * gain - t_ref[...]
        part = 0.5 * jnp.sum(jnp.mean(e * e, axis=-1, keepdims=True), axis=0, keepdims=True)
        dyv = e / d
        p_fg = jnp.sum(dyv * xn, axis=0, keepdims=True)
        dxn = dyv * gain
        dx_ref[...] = r * (dxn - xn * jnp.mean(dxn * xn, axis=-1, keepdims=True))

        @pl.when(i == 0)
        def _():
            loss_ref[...] = part
            dfg_ref[...] = p_fg

        @pl.when(i != 0)
        def _():
            loss_ref[...] = loss_ref[...] + part
            dfg_ref[...] = dfg_ref[...] + p_fg

    row = pl.BlockSpec((tm, d), lambda i: (i, 0))
    return _call(
        body, name=name, grid=(m_rows // tm,),
        out_shape=[jax.ShapeDtypeStruct((1, 1), F32), jax.ShapeDtypeStruct((m_rows, d), F32),
                   jax.ShapeDtypeStruct((1, d), F32)],
        in_specs=[row, pl.BlockSpec((1, d), lambda i: (0, 0)), row],
        out_specs=[pl.BlockSpec((1, 1), lambda i: (0, 0)), row, pl.BlockSpec((1, d), lambda i: (0, 0))],
        compiler_params=_params(),
    )(x, fg, target)


def _pack(parts):
    flat = jnp.concatenate([p.reshape(-1) for p in parts])
    pad = (-flat.shape[0]) % (8 * LANES)
    return jnp.pad(flat, (0, pad)).reshape(-1, LANES)


def _unpack(packed, like):
    flat = packed.reshape(-1)
    out, off = [], 0
    for a in like:
        out.append(flat[off:off + a.size].reshape(a.shape))
        off += a.size
    return out


def kernel(x, c, norm_gain, w_ada, b_ada, a_w_in, a_ln_gain, a_ln_bias, a_w_s, a_b_s, a_w_out, b_w_in, b_lower_bounds, b_gn_gain, b_w_out, final_gain, loss_target, m_norm_gain, m_w_ada, m_b_ada, m_a_w_in, m_a_ln_gain, m_a_ln_bias, m_a_w_s, m_a_b_s, m_a_w_out, m_b_w_in, m_b_lower_bounds, m_b_gn_gain, m_b_w_out, m_final_gain, v_norm_gain, v_w_ada, v_b_ada, v_a_w_in, v_a_ln_gain, v_a_ln_bias, v_a_w_s, v_a_b_s, v_a_w_out, v_b_w_in, v_b_lower_bounds, v_b_gn_gain, v_b_w_out, v_final_gain):
    nb, seq, d = x.shape
    m_rows = nb * seq
    n_l = w_ada.shape[0]
    ada_cols = w_ada.shape[2]
    px, py, pc = _place()
    chip = 2 * px + py
    dev = 2 * chip + pc

    c_all = allgather_small(c.reshape(-1, LANES), "gather_c").reshape(N_DEV * nb, d)
    b_cols = lax.dynamic_slice_in_dim(b_ada, chip * ada_cols, ada_cols, axis=1).reshape(n_l, 1, ada_cols)
    mod_cols = ada_fwd(c_all, w_ada, b_cols, "ada_fwd")
    mod_g = allgather_small(mod_cols.reshape(-1, LANES), "gather_mod")
    mod_g = mod_g.reshape(N_CHIPS, 2, n_l, N_DEV * nb, ada_cols)[:, 0]
    mod_all = jnp.transpose(mod_g, (1, 2, 0, 3)).reshape(n_l, N_DEV * nb, 3 * d)
    mod_mine = lax.dynamic_slice_in_dim(mod_all, dev * nb, nb, axis=1)
    mod0 = mod_mine[0].reshape(nb, 1, 3 * d)
    mod1 = mod_mine[1].reshape(nb, 1, 3 * d)

    wa_in, tok_a_in = gather_inplace(cast_into_slot(a_w_in[0], chip, mod_mine, "cast_a_in"), "gather_a_in")
    s_ao = gather_start(cast_into_slot(a_w_out[0], chip, tok_a_in, "cast_a_out"), "gather_a_out_start")
    s_bi = gather_start(cast_into_slot(b_w_in[0], chip, s_ao[3], "cast_b_in"), "gather_b_in_start")
    s_bo = gather_start(cast_into_slot(b_w_out[0], chip, s_bi[3], "cast_b_out"), "gather_b_out_start")
    di = a_w_out.shape[1] * N_CHIPS

    x0 = x.reshape(m_rows, d)
    tgt = loss_target.reshape(m_rows, d)
    ng0 = norm_gain[0:1] + (s_ao[3][0, 0] + s_bi[3][0, 0] + s_bo[3][0, 0])
    ng1 = norm_gain[1:2]
    bs_col = a_b_s[0].reshape(SG_GROUPS, SG_BLOCK, 1)
    proj_a, h_a = inproj_fwd(x0, mod0, ng0, wa_in, seq, False, "a_inproj")
    y_a = sgu_fwd(proj_a, a_ln_gain, a_ln_bias, a_w_s[0], bs_col, "a_sgu")
    wa_out = gather_wait(*s_ao[:3], y_a, "gather_a_out_wait").reshape(di, d)
    x1, out_a = outproj_fwd(y_a, wa_out, x0, mod0, seq, "a_outproj")
    wb_in = gather_wait(*s_bi[:3], out_a, "gather_b_in_wait")
    proj_b, h_b = inproj_fwd(x1, mod1, ng1, wb_in, seq, True, "b_inproj")
    y_b, sts_b = hgrn_fwd(proj_b, b_lower_bounds, b_gn_gain, seq, "b_hgrn")
    wb_out = gather_wait(*s_bo[:3], y_b, "gather_b_out_wait").reshape(di, d)
    x2, out_b = outproj_fwd(y_b, wb_out, x1, mod1, seq, "b_outproj")
    loss_part, dx2, dfg = final_loss(x2, final_gain.reshape(1, d), tgt, "loss_head")

    shard_rows = di // N_CHIPS
    dy_b, dout_b, dgate1 = outproj_bwd(dx2, out_b, mod1, wb_out, seq, "b_outproj_bwd")
    gwb_out = grad_w_out(y_b, dout_b, "b_grad_w_out").reshape(N_CHIPS, shard_rows, d)
    e_bo = exchange_start(gwb_out, "exchange_b_out_start")
    dproj_b, dlb, dgn = hgrn_bwd(proj_b, dy_b, sts_b, b_lower_bounds, b_gn_gain + e_bo[4][0, 0], seq, "b_hgrn_bwd")
    e_bi = exchange_start(grad_w_in(h_b, dproj_b, N_CHIPS, True, "b_grad_w_in"), "exchange_b_in_start")
    dx1, dshift1, dscale1, dng1 = inproj_bwd(
        dproj_b, wb_in, x1, dx2, mod1, ng1 + e_bi[4][0, 0], seq, True, "b_inproj_bwd")

    dy_a, dout_a, dgate0 = outproj_bwd(dx1, out_a, mod0, wa_out, seq, "a_outproj_bwd")
    gwa_out = grad_w_out(y_a, dout_a, "a_grad_w_out").reshape(N_CHIPS, shard_rows, d)
    e_ao = exchange_start(gwa_out, "exchange_a_out_start")
    dproj_a, dws, dbs, dlg, dlbias = sgu_bwd(
        proj_a, dy_a, a_ln_gain + e_ao[4][0, 0], a_ln_bias, a_w_s[0], bs_col, "a_sgu_bwd")
    e_ai = exchange_start(grad_w_in(h_a, dproj_a, N_CHIPS, False, "a_grad_w_in"), "exchange_a_in_start")
    dx0, dshift0, dscale0, dng0 = inproj_bwd(
        dproj_a, wa_in, x0, dx1, mod0, norm_gain[0:1] + e_ai[4][0, 0], seq, False, "a_inproj_bwd")
    grad_x = dx0.reshape(nb, seq, d)

    names = ("b_out", "b_in", "a_out", "a_in")
    mine, after = [], dx0
    for ex, nm in zip((e_bo, e_bi, e_ao, e_ai), names):
        parts_thru, land = exchange_wait(ex[0], ex[1], ex[2], ex[3], after, "exchange_" + nm + "_wait")
        mine.append(sum_parts(parts_thru, land, chip, "sum_" + nm))
        after = mine[-1]
    theirs = swap_sibling(mine, "swap_sums")
    big = []
    for pa, pb, w, m, v, nm in zip(mine, theirs, (b_w_out, b_w_in, a_w_out, a_w_in),
                                   (m_b_w_out, m_b_w_in, m_a_w_out, m_a_w_in),
                                   (v_b_w_out, v_b_w_in, v_a_w_out, v_a_w_in), names):
        big.append([r.reshape(w.shape) for r in adamw_pair(pa, pb, w[0], m[0], v[0], "adamw_" + nm)])
    (gb_out, db_out, mb_out, vb_out), (gb_in, db_in, mb_in, vb_in), \
        (ga_out, da_out, ma_out, va_out), (ga_in, da_in, ma_in, va_in) = big

    dmod = jnp.concatenate([dshift0, dscale0, dgate0, dshift1, dscale1, dgate1], axis=2)
    n_dmod = dmod.size
    small_w = [norm_gain, a_ln_gain, a_ln_bias, a_w_s, a_b_s, b_lower_bounds, b_gn_gain, final_gain]
    small_m = [m_norm_gain, m_a_ln_gain, m_a_ln_bias, m_a_w_s, m_a_b_s, m_b_lower_bounds, m_b_gn_gain, m_final_gain]
    small_v = [v_norm_gain, v_a_ln_gain, v_a_ln_bias, v_a_w_s, v_a_b_s, v_b_lower_bounds, v_b_gn_gain, v_final_gain]
    small_g = [jnp.concatenate([dng0, dng1], axis=0), dlg, dlbias, dws, dbs, dlb, dgn, dfg]
    blank = [jnp.zeros((n_dmod,), F32)]
    one = [jnp.zeros((1,), F32)]
    packed_g = _pack([dmod] + small_g + [loss_part])
    rows = packed_g.shape[0]
    gathered = allgather_small(packed_g, "gather_small").reshape(N_DEV, rows, LANES)
    res = small_update(gathered, _pack(blank + small_w + one), _pack(blank + small_m + one),
                       _pack(blank + small_v + one), "small_update")
    like = blank + small_w + one
    sg, sd, sm, sv = [_unpack(r, like)[1:] for r in res]
    loss = sg[-1][0]

    dmod_all = gathered[:, :n_dmod // LANES].reshape(N_DEV * nb, n_l, 3 * d)
    dmod_cols = lax.dynamic_slice_in_dim(dmod_all, chip * ada_cols, ada_cols, axis=2)
    dmod_cols = jnp.transpose(dmod_cols, (1, 0, 2))
    g_wada, d_wada, m_wada, v_wada = ada_bwd(c_all, dmod_cols, w_ada, m_w_ada, v_w_ada, "ada_bwd")
    flat = lambda a: a.reshape(1, -1)
    g_bada, d_bada, m_bada, v_bada = [
        r.reshape(b_ada.shape) for r in
        bias_update(dmod_all.reshape(N_DEV * nb, n_l * 3 * d), flat(b_ada), flat(m_b_ada), flat(v_b_ada), "bias_update")]

    def order(ng, wada, bada, ain, sm_rest, aout, bin_, bout):
        lg, lbi, ws_, bs_, lbd, gn_, fg_ = sm_rest
        return [ng, wada, bada, ain, lg, lbi, ws_, bs_, aout, bin_, lbd, gn_, bout, fg_]

    grads = order(sg[0], g_wada, g_bada, ga_in, sg[1:8], ga_out, gb_in, gb_out)
    deltas = order(sd[0], d_wada, d_bada, da_in, sd[1:8], da_out, db_in, db_out)
    new_m = order(sm[0], m_wada, m_bada, ma_in, sm[1:8], ma_out, mb_in, mb_out)
    new_v = order(sv[0], v_wada, v_bada, va_in, sv[1:8], va_out, vb_in, vb_out)
    return (loss, grad_x, *grads, *deltas, *new_m, *new_v)
```

```python
import functools

import jax
import jax.numpy as jnp
from jax import lax
from jax.experimental import pallas as pl
from jax.experimental.pallas import tpu as pltpu

F32 = jnp.float32
BF16 = jnp.bfloat16
EPS = 1e-6
CHUNK = 64
SG_BLOCK = 128
SG_GROUPS = 8
HEAD_DIM = 128
HG_WIDE = 8
HG_ROWS = 128
N_CHIPS = 4
N_DEV = 8
LANES = 128
ADAM_LR = 0.001
ADAM_B1 = 0.9
ADAM_B2 = 0.999
ADAM_EPS = 1e-08
ADAM_WD = 0.01
ADAM_STEP = 10
GELU_C0 = 0.7978845608028654
GELU_C1 = 0.044715
MESH = pl.DeviceIdType.MESH
VMEM_LIMIT = 56 * 1024 * 1024


ROW_TILE = 1024


def _col_tile(n):
    return next(t for t in (1024, 768, 512, 256) if n % t == 0)


def _call(body, **kw):
    return pl.pallas_call(body, **kw)


def _params(**kw):
    return pltpu.CompilerParams(vmem_limit_bytes=VMEM_LIMIT, **kw)


def _sigmoid(x):
    return 0.5 * jnp.tanh(0.5 * x) + 0.5


def _sigmoid_small(x):
    return 1.0 / (1.0 + jnp.exp(-x))


def _silu_and_grad(x):
    s = _sigmoid(x)
    return x * s, s * (1.0 + x * (1.0 - s))


def _gelu(x):
    return 0.5 * x * (1.0 + jnp.tanh(GELU_C0 * (x + GELU_C1 * x * x * x)))


def _gelu_and_grad(x):
    t = jnp.tanh(GELU_C0 * (x + GELU_C1 * x * x * x))
    g = 0.5 * x * (1.0 + t)
    dg = 0.5 * (1.0 + t) + 0.5 * x * (1.0 - t * t) * (GELU_C0 * (1.0 + 3.0 * GELU_C1 * x * x))
    return g, dg


def _dot(a, b, dims, precision=None):
    return lax.dot_general(a, b, (dims, ((), ())), precision=precision, preferred_element_type=F32)


NN = ((1,), (0,))
NT = ((1,), (1,))
TN = ((0,), (0,))


def _adamw(w, g, m, v):
    m = ADAM_B1 * m + (1.0 - ADAM_B1) * g
    v = ADAM_B2 * v + (1.0 - ADAM_B2) * (g * g)
    m_hat = m / (1.0 - ADAM_B1 ** ADAM_STEP)
    v_hat = v / (1.0 - ADAM_B2 ** ADAM_STEP)
    delta = -ADAM_LR * (m_hat / (jnp.sqrt(v_hat) + ADAM_EPS) + ADAM_WD * w)
    return delta, m, v


def _chunk_mask():
    r = lax.broadcasted_iota(jnp.int32, (SG_BLOCK, SG_BLOCK), 0)
    c = lax.broadcasted_iota(jnp.int32, (SG_BLOCK, SG_BLOCK), 1)
    return (c // CHUNK) <= (r // CHUNK)


def _place():
    return lax.axis_index("x"), lax.axis_index("y"), lax.axis_index("c")


def _other_chips(x, y):
    return [(1 - x, y), (x, 1 - y), (1 - x, 1 - y)]


def allgather_small(v, name):
    m_per, n = v.shape

    def body(x_ref, out_ref, send_sems, recv_sems, local_sem):
        x, y, c = _place()
        me, sibling = (x, y, c), (x, y, 1 - c)
        chips = _other_chips(x, y)

        def rows(px, py, pc):
            return out_ref.at[pl.ds((4 * px + 2 * py + pc) * m_per, m_per), :]

        def copy(k, block, to, src=None):
            return pltpu.make_async_remote_copy(
                src_ref=rows(*block) if src is None else src, dst_ref=rows(*block),
                send_sem=send_sems.at[k], recv_sem=recv_sems.at[k], device_id=to, device_id_type=MESH)

        mine = pltpu.make_async_copy(x_ref, rows(*me), local_sem)
        mine.start()
        first = [copy(0, me, sibling, src=x_ref)]
        first += [copy(1 + j, me, (*chip, c), src=x_ref) for j, chip in enumerate(chips)]
        for cp in first:
            cp.start()
        passed = [copy(4 + j, (*chip, c), sibling) for j, chip in enumerate(chips)]
        for j, chip in enumerate(chips):
            copy(1 + j, (*chip, c), me).wait_recv()
            passed[j].start()
        copy(0, sibling, me).wait_recv()
        for j, chip in enumerate(chips):
            copy(4 + j, (*chip, 1 - c), me).wait_recv()
        for cp in first + passed:
            cp.wait_send()
        mine.wait()

    return _call(
        body, name=name,
        out_shape=jax.ShapeDtypeStruct((N_DEV * m_per, n), v.dtype),
        in_specs=[pl.BlockSpec(memory_space=pltpu.VMEM)],
        out_specs=pl.BlockSpec(memory_space=pltpu.VMEM),
        scratch_shapes=[pltpu.SemaphoreType.DMA((7,)), pltpu.SemaphoreType.DMA((7,)), pltpu.SemaphoreType.DMA],
    )(v)


def _hbm_spec():
    return pl.BlockSpec(memory_space=pltpu.HBM)


def _sem_spec():
    return pl.BlockSpec(memory_space=pltpu.SEMAPHORE)


def _split_params():
    return pltpu.CompilerParams(has_side_effects=pltpu.SideEffectType.DATAFLOW_SIDE_EFFECTING)


def _hbm(a):
    return pltpu.with_memory_space_constraint(a, pltpu.HBM)


def gather_inplace(land, name):
    half = land.shape[1] // 2

    def body(land_in, land_ref, token, send_sems, recv_sems):
        del land_in
        x, y, c = _place()
        chips = _other_chips(x, y)

        def copy(k, chip_idx, core_half, to):
            rows = land_ref.at[chip_idx, pl.ds(core_half * half, half), :]
            return pltpu.make_async_remote_copy(
                src_ref=rows, dst_ref=rows, send_sem=send_sems.at[k], recv_sem=recv_sems.at[k],
                device_id=to, device_id_type=MESH)

        first = [copy(j, 2 * x + y, c, (px, py, c)) for j, (px, py) in enumerate(chips)]
        for cp in first:
            cp.start()
        passed = [copy(3 + j, 2 * px + py, c, (x, y, 1 - c)) for j, (px, py) in enumerate(chips)]
        for j, (px, py) in enumerate(chips):
            copy(j, 2 * px + py, c, (px, py, c)).wait_recv()
            passed[j].start()
        for j, (px, py) in enumerate(chips):
            copy(3 + j, 2 * px + py, 1 - c, (x, y, 1 - c)).wait_recv()
        for cp in first + passed:
            cp.wait_send()
        token[...] = jnp.zeros_like(token)

    return _call(
        body, name=name,
        out_shape=(jax.ShapeDtypeStruct(land.shape, land.dtype), jax.ShapeDtypeStruct((8, LANES), F32)),
        in_specs=[_hbm_spec()], out_specs=(_hbm_spec(), pl.BlockSpec(memory_space=pltpu.VMEM)),
        input_output_aliases={0: 0},
        scratch_shapes=[pltpu.SemaphoreType.DMA((6,)), pltpu.SemaphoreType.DMA((6,))],
    )(land)


def gather_start(land, name):
    def body(land_ref, send_sems, recv_sems, land_thru, token):
        del land_thru
        x, y, c = _place()
        for j, (px, py) in enumerate(_other_chips(x, y)):
            pltpu.make_async_remote_copy(
                src_ref=land_ref.at[2 * x + y], dst_ref=land_ref.at[2 * x + y],
                send_sem=send_sems.at[j], recv_sem=recv_sems.at[j], device_id=(px, py, c),
                device_id_type=MESH).start()
        token[...] = jnp.zeros_like(token)

    return _call(
        body, name=name,
        out_shape=(pltpu.SemaphoreType.DMA((3,)), pltpu.SemaphoreType.DMA((3,)),
                   pltpu.HBM(land.shape, land.dtype), jax.ShapeDtypeStruct((8, LANES), F32)),
        in_specs=(_hbm_spec(),),
        out_specs=(_sem_spec(), _sem_spec(), _hbm_spec(), pl.BlockSpec(memory_space=pltpu.VMEM)),
        input_output_aliases={0: 2}, compiler_params=_split_params(),
    )(_hbm(land))


def gather_wait(send_sems, recv_sems, land, after, name):
    def body(land_ref, send_sems, recv_sems, after_ref, land_out):
        del after_ref, land_out
        x, y, c = _place()
        for j, (px, py) in enumerate(_other_chips(x, y)):
            cp = pltpu.make_async_remote_copy(
                src_ref=land_ref.at[2 * x + y], dst_ref=land_ref.at[2 * px + py],
                send_sem=send_sems.at[j], recv_sem=recv_sems.at[j], device_id=(px, py, c), device_id_type=MESH)
            cp.wait_send()
            cp.wait_recv()

    return _call(
        body, name=name,
        out_shape=pltpu.HBM(land.shape, land.dtype),
        in_specs=(_hbm_spec(), _sem_spec(), _sem_spec(), pl.BlockSpec(memory_space=pl.ANY)),
        out_specs=_hbm_spec(), input_output_aliases={0: 0}, compiler_params=_split_params(),
    )(land, send_sems, recv_sems, after)


def _flips():
    return [(fx, fy, fc) for fx in (0, 1) for fy in (0, 1) for fc in (0, 1) if (fx, fy, fc) != (0, 0, 0)]


def _flipped(x, y, c, flip):
    fx, fy, fc = flip
    return (1 - x if fx else x, 1 - y if fy else y, 1 - c if fc else c)


def gather_all_start(land, name):
    def body(land_ref, send_sems, recv_sems, land_thru, token):
        del land_thru
        x, y, c = _place()
        for k, flip in enumerate(_flips()):
            pltpu.make_async_remote_copy(
                src_ref=land_ref.at[4 * x + 2 * y + c], dst_ref=land_ref.at[4 * x + 2 * y + c],
                send_sem=send_sems.at[k], recv_sem=recv_sems.at[k], device_id=_flipped(x, y, c, flip),
                device_id_type=MESH).start()
        token[...] = jnp.zeros_like(token)

    return _call(
        body, name=name,
        out_shape=(pltpu.SemaphoreType.DMA((7,)), pltpu.SemaphoreType.DMA((7,)),
                   pltpu.HBM(land.shape, land.dtype), jax.ShapeDtypeStruct((8, LANES), F32)),
        in_specs=(_hbm_spec(),),
        out_specs=(_sem_spec(), _sem_spec(), _hbm_spec(), pl.BlockSpec(memory_space=pltpu.VMEM)),
        input_output_aliases={0: 2}, compiler_params=_split_params(),
    )(_hbm(land))


def gather_all_wait(send_sems, recv_sems, land, after, name):
    def body(land_ref, send_sems, recv_sems, after_ref, land_out):
        del after_ref, land_out
        x, y, c = _place()
        for k, flip in enumerate(_flips()):
            px, py, pc = _flipped(x, y, c, flip)
            cp = pltpu.make_async_remote_copy(
                src_ref=land_ref.at[4 * x + 2 * y + c], dst_ref=land_ref.at[4 * px + 2 * py + pc],
                send_sem=send_sems.at[k], recv_sem=recv_sems.at[k], device_id=(px, py, pc), device_id_type=MESH)
            cp.wait_send()
            cp.wait_recv()

    return _call(
        body, name=name,
        out_shape=pltpu.HBM(land.shape, land.dtype),
        in_specs=(_hbm_spec(), _sem_spec(), _sem_spec(), pl.BlockSpec(memory_space=pl.ANY)),
        out_specs=_hbm_spec(), input_output_aliases={0: 0}, compiler_params=_split_params(),
    )(land, send_sems, recv_sems, after)


def exchange_start(parts, name):
    _, r, c_ = parts.shape

    def body(parts_ref, land_ref, send_sems, recv_sems, parts_thru, land_thru, token):
        del parts_thru, land_thru
        x, y, c = _place()
        for j, (px, py) in enumerate(_other_chips(x, y)):
            pltpu.make_async_remote_copy(
                src_ref=parts_ref.at[2 * px + py], dst_ref=land_ref.at[j],
                send_sem=send_sems.at[j], recv_sem=recv_sems.at[j], device_id=(px, py, c),
                device_id_type=MESH).start()
        token[...] = jnp.zeros_like(token)

    return _call(
        body, name=name,
        out_shape=(pltpu.SemaphoreType.DMA((3,)), pltpu.SemaphoreType.DMA((3,)),
                   pltpu.HBM(parts.shape, parts.dtype), pltpu.HBM((3, r, c_), parts.dtype),
                   jax.ShapeDtypeStruct((8, LANES), F32)),
        in_specs=(_hbm_spec(), _hbm_spec()),
        out_specs=(_sem_spec(), _sem_spec(), _hbm_spec(), _hbm_spec(), pl.BlockSpec(memory_space=pltpu.VMEM)),
        input_output_aliases={0: 2, 1: 3}, compiler_params=_split_params(),
    )(_hbm(parts), _hbm(lax.empty((3, r, c_), parts.dtype)))


def exchange_wait(send_sems, recv_sems, parts, land, after, name):
    def body(parts_ref, land_ref, send_sems, recv_sems, after_ref, parts_out, land_out):
        del after_ref, parts_out, land_out
        x, y, c = _place()
        for j, (px, py) in enumerate(_other_chips(x, y)):
            cp = pltpu.make_async_remote_copy(
                src_ref=parts_ref.at[2 * px + py], dst_ref=land_ref.at[j],
                send_sem=send_sems.at[j], recv_sem=recv_sems.at[j], device_id=(px, py, c), device_id_type=MESH)
            cp.wait_send()
            cp.wait_recv()

    return _call(
        body, name=name,
        out_shape=(pltpu.HBM(parts.shape, parts.dtype), pltpu.HBM(land.shape, land.dtype)),
        in_specs=(_hbm_spec(), _hbm_spec(), _sem_spec(), _sem_spec(), pl.BlockSpec(memory_space=pl.ANY)),
        out_specs=(_hbm_spec(), _hbm_spec()), input_output_aliases={0: 0, 1: 1},
        compiler_params=_split_params(),
    )(parts, land, send_sems, recv_sems, after)


def cast_into_slot(w, chip, after, name):
    r, c = w.shape
    tr = min(256, r)

    def body(s_ref, w_ref, after_ref, o_ref):
        del s_ref, after_ref
        o_ref[...] = w_ref[...].astype(BF16)

    return _call(
        body, name=name,
        grid_spec=pltpu.PrefetchScalarGridSpec(
            num_scalar_prefetch=1, grid=(r // tr,),
            in_specs=[pl.BlockSpec((tr, c), lambda i, s: (i, 0)), pl.BlockSpec(memory_space=pl.ANY)],
            out_specs=pl.BlockSpec((None, tr, c), lambda i, s: (s[0], i, 0))),
        out_shape=jax.ShapeDtypeStruct((N_CHIPS, r, c), BF16),
        compiler_params=_params(),
    )(chip.reshape(1).astype(jnp.int32), w, after)


def sum_parts(parts, land, chip, name):
    _, r, c = parts.shape
    tr = min(256, r)

    def body(s_ref, p_ref, l_ref, o_ref):
        del s_ref
        acc = p_ref[...].astype(F32) + l_ref[0].astype(F32)
        acc = acc + l_ref[1].astype(F32)
        o_ref[...] = acc + l_ref[2].astype(F32)

    return _call(
        body, name=name,
        grid_spec=pltpu.PrefetchScalarGridSpec(
            num_scalar_prefetch=1, grid=(r // tr,),
            in_specs=[pl.BlockSpec((None, tr, c), lambda i, s: (s[0], i, 0)),
                      pl.BlockSpec((3, tr, c), lambda i, s: (0, i, 0))],
            out_specs=pl.BlockSpec((tr, c), lambda i, s: (i, 0))),
        out_shape=jax.ShapeDtypeStruct((r, c), F32),
        compiler_params=_params(),
    )(chip.reshape(1).astype(jnp.int32), parts, land)


def swap_sibling(arrs, name):
    n = len(arrs)

    def body(*refs):
        ins, outs = refs[:n], refs[n:2 * n]
        send_sems, recv_sems = refs[2 * n:]
        x, y, c = _place()
        cps = []
        for w in range(n):
            cp = pltpu.make_async_remote_copy(
                src_ref=ins[w], dst_ref=outs[w], send_sem=send_sems.at[w], recv_sem=recv_sems.at[w],
                device_id=(x, y, 1 - c), device_id_type=MESH)
            cp.start()
            cps.append(cp)
        for cp in cps:
            cp.wait_recv()
        for cp in cps:
            cp.wait_send()

    return _call(
        body, name=name,
        out_shape=[jax.ShapeDtypeStruct(a.shape, a.dtype) for a in arrs],
        in_specs=[_hbm_spec()] * n, out_specs=[_hbm_spec()] * n,
        scratch_shapes=[pltpu.SemaphoreType.DMA((n,)), pltpu.SemaphoreType.DMA((n,))],
    )(*arrs)


def adamw_pair(pa, pb, w, m, v, name):
    r, c = w.shape
    tr = min(128, r)

    def body(pa_ref, pb_ref, w_ref, m_ref, v_ref, g_ref, d_ref, nm_ref, nv_ref):
        g = pa_ref[...] + pb_ref[...]
        d, nm, nv = _adamw(w_ref[...], g, m_ref[...], v_ref[...])
        g_ref[...] = g
        d_ref[...] = d
        nm_ref[...] = nm
        nv_ref[...] = nv

    spec = pl.BlockSpec((tr, c), lambda i: (i, 0))
    return _call(
        body, name=name, grid=(r // tr,),
        out_shape=[jax.ShapeDtypeStruct((r, c), F32)] * 4,
        in_specs=[spec] * 5, out_specs=[spec] * 4,
        compiler_params=_params(),
    )(pa, pb, w, m, v)


def small_update(gathered, w, m, v, name):
    def body(g_ref, w_ref, m_ref, v_ref, go_ref, d_ref, nm_ref, nv_ref):
        g = g_ref[0]
        for k in range(1, N_DEV):
            g = g + g_ref[k]
        d, nm, nv = _adamw(w_ref[...], g, m_ref[...], v_ref[...])
        go_ref[...] = g
        d_ref[...] = d
        nm_ref[...] = nm
        nv_ref[...] = nv

    return _call(
        body, name=name,
        out_shape=[jax.ShapeDtypeStruct(w.shape, F32)] * 4,
        compiler_params=_params(),
    )(gathered, w, m, v)


def ada_fwd(c_all, w_ada, b_cols, name):
    n_l, d, cols = w_ada.shape
    nb = c_all.shape[0]
    tn = 256

    def body(c_ref, w_ref, b_ref, o_ref):
        cv = c_ref[...]
        ca = (cv * _sigmoid(cv)).astype(BF16)
        o_ref[...] = _dot(ca, w_ref[...].astype(BF16), NN) + b_ref[...]

    return _call(
        body, name=name, grid=(n_l, cols // tn),
        out_shape=jax.ShapeDtypeStruct((n_l, nb, cols), F32),
        in_specs=[pl.BlockSpec((nb, d), lambda l, j: (0, 0)),
                  pl.BlockSpec((None, d, tn), lambda l, j: (l, 0, j)),
                  pl.BlockSpec((None, 1, tn), lambda l, j: (l, 0, j))],
        out_specs=pl.BlockSpec((None, nb, tn), lambda l, j: (l, 0, j)),
        compiler_params=_params(),
    )(c_all, w_ada, b_cols)


def ada_bwd(c_all, dmod_cols, w, m, v, name):
    n_l, d, cols = w.shape
    nb = c_all.shape[0]
    tn = 256

    def body(c_ref, dm_ref, w_ref, m_ref, v_ref, g_ref, d_ref, nm_ref, nv_ref):
        cv = c_ref[...]
        ca = (cv * _sigmoid(cv)).astype(BF16)
        g = _dot(ca, dm_ref[...].astype(BF16), TN)
        dl, nm, nv = _adamw(w_ref[...], g, m_ref[...], v_ref[...])
        g_ref[...] = g
        d_ref[...] = dl
        nm_ref[...] = nm
        nv_ref[...] = nv

    wspec = pl.BlockSpec((None, d, tn), lambda l, j: (l, 0, j))
    return _call(
        body, name=name, grid=(n_l, cols // tn),
        out_shape=[jax.ShapeDtypeStruct((n_l, d, cols), F32)] * 4,
        in_specs=[pl.BlockSpec((nb, d), lambda l, j: (0, 0)),
                  pl.BlockSpec((None, nb, tn), lambda l, j: (l, 0, j)),
                  wspec, wspec, wspec],
        out_specs=[wspec] * 4,
        compiler_params=_params(),
    )(c_all, dmod_cols, w, m, v)


def bias_update(dmod_all, w, m, v, name):
    def body(dm_ref, w_ref, m_ref, v_ref, g_ref, d_ref, nm_ref, nv_ref):
        g = jnp.sum(dm_ref[...], axis=0, keepdims=True)
        dl, nm, nv = _adamw(w_ref[...], g, m_ref[...], v_ref[...])
        g_ref[...] = g
        d_ref[...] = dl
        nm_ref[...] = nm
        nv_ref[...] = nv

    return _call(
        body, name=name,
        out_shape=[jax.ShapeDtypeStruct(w.shape, F32)] * 4,
        compiler_params=_params(),
    )(dmod_all, w, m, v)


def inproj_fwd(x, mod, ng, wg, seq, sectioned, name):
    m_rows, d = x.shape
    nsh, _, ns = wg.shape
    n = nsh * ns
    tm, tn = min(ROW_TILE, seq), 512
    per = ns // tn

    def body(x_ref, mod_ref, ng_ref, w_ref, proj_ref, h_ref):
        @pl.when(pl.program_id(1) == 0)
        def _():
            xv = x_ref[...]
            r = lax.rsqrt(jnp.mean(xv * xv, axis=-1, keepdims=True) + EPS)
            md = mod_ref[0]
            h = (xv * r * ng_ref[...]) * (1.0 + md[:, d:2 * d]) + md[:, :d]
            h_ref[...] = h.astype(BF16)
        proj_ref[...] = _dot(h_ref[...], w_ref[...], NN)

    if sectioned:
        proj_shape = (nsh, m_rows, ns)
        proj_spec = pl.BlockSpec((None, tm, tn), lambda i, j: (j // per, i, j % per))
    else:
        proj_shape = (m_rows, n)
        proj_spec = pl.BlockSpec((tm, tn), lambda i, j: (i, j))
    return _call(
        body, name=name, grid=(m_rows // tm, n // tn),
        out_shape=[jax.ShapeDtypeStruct(proj_shape, F32), jax.ShapeDtypeStruct((m_rows, d), BF16)],
        in_specs=[pl.BlockSpec((tm, d), lambda i, j: (i, 0)),
                  pl.BlockSpec((1, 1, 3 * d), lambda i, j: ((i * tm) // seq, 0, 0)),
                  pl.BlockSpec((1, d), lambda i, j: (0, 0)),
                  pl.BlockSpec((None, d, tn), lambda i, j: (j // per, 0, j % per))],
        out_specs=[proj_spec, pl.BlockSpec((tm, d), lambda i, j: (i, 0))],
        compiler_params=_params(),
    )(x, mod, ng, wg)


def outproj_fwd(y, w, x, mod, seq, name):
    m_rows, di = y.shape
    d = w.shape[1]
    tm = min(ROW_TILE, seq)

    def body(y_ref, w_ref, x_ref, mod_ref, xn_ref, out_ref):
        acc = _dot(y_ref[...], w_ref[...], NN)
        out_ref[...] = acc
        xn_ref[...] = x_ref[...] + mod_ref[0][:, 2 * d:] * acc

    row = pl.BlockSpec((tm, d), lambda i: (i, 0))
    return _call(
        body, name=name, grid=(m_rows // tm,),
        out_shape=[jax.ShapeDtypeStruct((m_rows, d), F32)] * 2,
        in_specs=[pl.BlockSpec((tm, di), lambda i: (i, 0)),
                  pl.BlockSpec((di, d), lambda i: (0, 0)),
                  row,
                  pl.BlockSpec((1, 1, 3 * d), lambda i: ((i * tm) // seq, 0, 0))],
        out_specs=[row, row],
        compiler_params=_params(),
    )(y, w, x, mod)


def outproj_bwd(dxo, out, mod, w, seq, name):
    m_rows, d = dxo.shape
    di = w.shape[0]
    nb = m_rows // seq
    tm, tn = min(ROW_TILE, seq), 512

    def body(dxo_ref, out_ref, mod_ref, w_ref, dy_ref, dout_ref, dgate_ref):
        i = pl.program_id(0)

        @pl.when(pl.program_id(1) == 0)
        def _():
            dx = dxo_ref[...]
            dout_ref[...] = (mod_ref[0][:, 2 * d:] * dx).astype(BF16)
            part = jnp.sum(dx * out_ref[...], axis=0, keepdims=True)

            @pl.when((i * tm) % seq == 0)
            def _():
                dgate_ref[0] = part

            @pl.when((i * tm) % seq != 0)
            def _():
                dgate_ref[0] = dgate_ref[0] + part

        dy_ref[...] = _dot(dout_ref[...], w_ref[...], NT)

    row = pl.BlockSpec((tm, d), lambda i, j: (i, 0))
    return _call(
        body, name=name, grid=(m_rows // tm, di // tn),
        out_shape=[jax.ShapeDtypeStruct((m_rows, di), F32), jax.ShapeDtypeStruct((m_rows, d), BF16),
                   jax.ShapeDtypeStruct((nb, 1, d), F32)],
        in_specs=[row, row,
                  pl.BlockSpec((1, 1, 3 * d), lambda i, j: ((i * tm) // seq, 0, 0)),
                  pl.BlockSpec((tn, d), lambda i, j: (j, 0))],
        out_specs=[pl.BlockSpec((tm, tn), lambda i, j: (i, j)), row,
                   pl.BlockSpec((1, 1, d), lambda i, j: ((i * tm) // seq, 0, 0))],
        compiler_params=_params(),
    )(dxo, out, mod, w)


def grad_w_out(y, dout, name):
    m_rows, di = y.shape
    d = dout.shape[1]
    tm, tk = min(512, m_rows), _col_tile(di)
    n_m = m_rows // tm

    def body(y_ref, do_ref, o_ref, acc_ref):
        mi = pl.program_id(1)
        @pl.when(mi == 0)
        def _():
            acc_ref[...] = jnp.zeros_like(acc_ref)

        acc_ref[...] += _dot(y_ref[...], do_ref[...], TN)

        @pl.when(mi == n_m - 1)
        def _():
            o_ref[...] = acc_ref[...].astype(BF16)

    return _call(
        body, name=name, grid=(di // tk, n_m),
        out_shape=jax.ShapeDtypeStruct((di, d), BF16),
        in_specs=[pl.BlockSpec((tm, tk), lambda j, mi: (mi, j)),
                  pl.BlockSpec((tm, d), lambda j, mi: (mi, 0))],
        out_specs=pl.BlockSpec((tk, d), lambda j, mi: (j, 0)),
        scratch_shapes=[pltpu.VMEM((tk, d), F32)],
        compiler_params=_params(),
    )(y, dout)


def grad_w_in(h, dproj, nsh, sectioned, name):
    m_rows, d = h.shape
    n = dproj.shape[0] * dproj.shape[2] if sectioned else dproj.shape[1]
    ns = n // nsh
    tm, tn = min(512, m_rows), _col_tile(ns)
    per = ns // tn
    n_m = m_rows // tm

    def body(h_ref, dp_ref, o_ref, acc_ref):
        mi = pl.program_id(1)
        @pl.when(mi == 0)
        def _():
            acc_ref[...] = jnp.zeros_like(acc_ref)

        acc_ref[...] += _dot(h_ref[...], dp_ref[...], TN)

        @pl.when(mi == n_m - 1)
        def _():
            o_ref[...] = acc_ref[...].astype(BF16)

    if sectioned:
        dp_spec = pl.BlockSpec((None, tm, tn), lambda j, mi: (j // per, mi, j % per))
    else:
        dp_spec = pl.BlockSpec((tm, tn), lambda j, mi: (mi, j))
    return _call(
        body, name=name, grid=(n // tn, n_m),
        out_shape=jax.ShapeDtypeStruct((nsh, d, ns), BF16),
        in_specs=[pl.BlockSpec((tm, d), lambda j, mi: (mi, 0)), dp_spec],
        out_specs=pl.BlockSpec((None, d, tn), lambda j, mi: (j // per, 0, j % per)),
        scratch_shapes=[pltpu.VMEM((d, tn), F32)],
        compiler_params=_params(),
    )(h, dproj)


def inproj_bwd(dproj, wg, x, dxo, mod, ng, seq, sectioned, name):
    m_rows, d = x.shape
    nsh, _, ns = wg.shape
    n = nsh * ns
    nb = m_rows // seq
    tm, tk = min(ROW_TILE, seq), 512
    per = ns // tk
    n_k = n // tk

    def body(dp_ref, w_ref, x_ref, dxo_ref, mod_ref, ng_ref, dxi_ref, dsh_ref, dsc_ref, dng_ref, acc_ref):
        i, k = pl.program_id(0), pl.program_id(1)
        @pl.when(k == 0)
        def _():
            acc_ref[...] = jnp.zeros_like(acc_ref)

        acc_ref[...] += _dot(dp_ref[...], w_ref[...], NT)

        @pl.when(k == n_k - 1)
        def _():
            dh = acc_ref[...]
            xv = x_ref[...]
            r = lax.rsqrt(jnp.mean(xv * xv, axis=-1, keepdims=True) + EPS)
            xn = xv * r
            md = mod_ref[0]
            gain = ng_ref[...]
            p_shift = jnp.sum(dh, axis=0, keepdims=True)
            p_scale = jnp.sum(dh * (xn * gain), axis=0, keepdims=True)
            drn = dh * (1.0 + md[:, d:2 * d])
            p_ng = jnp.sum(drn * xn, axis=0, keepdims=True)
            dxn = drn * gain
            dx = r * (dxn - xn * jnp.mean(dxn * xn, axis=-1, keepdims=True))
            dxi_ref[...] = dxo_ref[...] + dx

            @pl.when((i * tm) % seq == 0)
            def _():
                dsh_ref[0] = p_shift
                dsc_ref[0] = p_scale

            @pl.when((i * tm) % seq != 0)
            def _():
                dsh_ref[0] = dsh_ref[0] + p_shift
                dsc_ref[0] = dsc_ref[0] + p_scale

            @pl.when(i == 0)
            def _():
                dng_ref[...] = p_ng

            @pl.when(i != 0)
            def _():
                dng_ref[...] = dng_ref[...] + p_ng

    if sectioned:
        dp_spec = pl.BlockSpec((None, tm, tk), lambda i, k: (k // per, i, k % per))
    else:
        dp_spec = pl.BlockSpec((tm, tk), lambda i, k: (i, k))
    row = pl.BlockSpec((tm, d), lambda i, k: (i, 0))
    per_seq = pl.BlockSpec((1, 1, d), lambda i, k: ((i * tm) // seq, 0, 0))
    return _call(
        body, name=name, grid=(m_rows // tm, n_k),
        out_shape=[jax.ShapeDtypeStruct((m_rows, d), F32), jax.ShapeDtypeStruct((nb, 1, d), F32),
                   jax.ShapeDtypeStruct((nb, 1, d), F32), jax.ShapeDtypeStruct((1, d), F32)],
        in_specs=[dp_spec,
                  pl.BlockSpec((None, d, tk), lambda i, k: (k // per, 0, k % per)),
                  row, row,
                  pl.BlockSpec((1, 1, 3 * d), lambda i, k: ((i * tm) // seq, 0, 0)),
                  pl.BlockSpec((1, d), lambda i, k: (0, 0))],
        out_specs=[row, per_seq, per_seq, pl.BlockSpec((1, d), lambda i, k: (0, 0))],
        scratch_shapes=[pltpu.VMEM((tm, d), F32)],
        compiler_params=_params(),
    )(dproj, wg, x, dxo, mod, ng)


def _sgu_stats(proj_ref, vg_ref, di, gd):
    s1 = jnp.zeros((SG_BLOCK, 1), F32)
    for g in range(SG_GROUPS):
        vg = _gelu(proj_ref[:, di + g * gd:di + (g + 1) * gd])
        vg_ref[:, g * gd:(g + 1) * gd] = vg
        s1 = s1 + jnp.sum(vg, axis=1, keepdims=True)
    mu = s1 / di
    s2 = jnp.zeros((SG_BLOCK, 1), F32)
    for g in range(SG_GROUPS):
        dv = vg_ref[:, g * gd:(g + 1) * gd] - mu
        s2 = s2 + jnp.sum(dv * dv, axis=1, keepdims=True)
    return mu, lax.rsqrt(s2 / di + EPS)


def sgu_fwd(proj, ln_gain, ln_bias, ws, bs, name):
    m_rows, n3 = proj.shape
    di = n3 // 3
    gd = di // SG_GROUPS

    def body(proj_ref, lg_ref, lb_ref, ws_ref, bs_ref, y_ref, wsm_ref, vg_ref):
        @pl.when(pl.program_id(0) == 0)
        def _():
            mask = _chunk_mask()
            for g in range(SG_GROUPS):
                wsm_ref[g] = jnp.where(mask, ws_ref[g], 0.0).astype(BF16)

        mu, rstd = _sgu_stats(proj_ref, vg_ref, di, gd)
        for g in range(SG_GROUPS):
            cs = slice(g * gd, (g + 1) * gd)
            vln = (vg_ref[:, cs] - mu) * rstd * lg_ref[:, cs] + lb_ref[:, cs]
            s = _dot(wsm_ref[g], vln.astype(BF16), NN) + bs_ref[g]
            u = _gelu(proj_ref[:, cs])
            gp = proj_ref[:, 2 * di + g * gd:2 * di + (g + 1) * gd]
            y_ref[:, cs] = (u * s * (gp * _sigmoid(gp))).astype(BF16)

    full = lambda shape: pl.BlockSpec(shape, lambda i: (0,) * len(shape))
    return _call(
        body, name=name, grid=(m_rows // SG_BLOCK,),
        out_shape=jax.ShapeDtypeStruct((m_rows, di), BF16),
        in_specs=[pl.BlockSpec((SG_BLOCK, n3), lambda i: (i, 0)),
                  full((1, di)), full((1, di)),
                  full((SG_GROUPS, SG_BLOCK, SG_BLOCK)), full((SG_GROUPS, SG_BLOCK, 1))],
        out_specs=pl.BlockSpec((SG_BLOCK, di), lambda i: (i, 0)),
        scratch_shapes=[pltpu.VMEM((SG_GROUPS, SG_BLOCK, SG_BLOCK), BF16), pltpu.VMEM((SG_BLOCK, di), F32)],
        compiler_params=_params(),
    )(proj, ln_gain, ln_bias, ws, bs)


def sgu_bwd(proj, dy, ln_gain, ln_bias, ws, bs, name):
    m_rows, n3 = proj.shape
    di = n3 // 3
    gd = di // SG_GROUPS
    n_i = m_rows // SG_BLOCK

    def body(proj_ref, dy_ref, lg_ref, lb_ref, ws_ref, bs_ref,
             dp_ref, dws_ref, dbs_ref, dlg_ref, dlb_ref, wsm_ref, vg_ref, dvh_ref):
        i = pl.program_id(0)

        @pl.when(i == 0)
        def _():
            mask = _chunk_mask()
            for g in range(SG_GROUPS):
                wsm_ref[g] = jnp.where(mask, ws_ref[g], 0.0).astype(BF16)
            dws_ref[...] = jnp.zeros_like(dws_ref)
            dbs_ref[...] = jnp.zeros_like(dbs_ref)
            dlg_ref[...] = jnp.zeros_like(dlg_ref)
            dlb_ref[...] = jnp.zeros_like(dlb_ref)

        mu, rstd = _sgu_stats(proj_ref, vg_ref, di, gd)
        m1 = jnp.zeros((SG_BLOCK, 1), F32)
        m2 = jnp.zeros((SG_BLOCK, 1), F32)
        for g in range(SG_GROUPS):
            cs = slice(g * gd, (g + 1) * gd)
            gs = slice(2 * di + g * gd, 2 * di + (g + 1) * gd)
            gain = lg_ref[:, cs]
            vhat = (vg_ref[:, cs] - mu) * rstd
            vln_b = (vhat * gain + lb_ref[:, cs]).astype(BF16)
            s = _dot(wsm_ref[g], vln_b, NN) + bs_ref[g]
            u, du = _gelu_and_grad(proj_ref[:, cs])
            sg, dsg = _silu_and_grad(proj_ref[:, gs])
            dyv = dy_ref[:, cs]
            dp_ref[:, cs] = (dyv * s * sg * du).astype(BF16)
            dp_ref[:, gs] = (dyv * u * s * dsg).astype(BF16)
            ds = dyv * u * sg
            ds_b = ds.astype(BF16)
            dws_ref[g] = dws_ref[g] + _dot(ds_b, vln_b, NT)
            dbs_ref[g] = dbs_ref[g] + jnp.sum(ds, axis=1, keepdims=True)
            dvln = _dot(wsm_ref[g], ds_b, TN)
            dlg_ref[:, cs] = dlg_ref[:, cs] + jnp.sum(dvln * vhat, axis=0, keepdims=True)
            dlb_ref[:, cs] = dlb_ref[:, cs] + jnp.sum(dvln, axis=0, keepdims=True)
            dvh = dvln * gain
            dvh_ref[:, cs] = dvh
            m1 = m1 + jnp.sum(dvh, axis=1, keepdims=True)
            m2 = m2 + jnp.sum(dvh * vhat, axis=1, keepdims=True)
        m1 = m1 / di
        m2 = m2 / di
        for g in range(SG_GROUPS):
            cs = slice(g * gd, (g + 1) * gd)
            vs = slice(di + g * gd, di + (g + 1) * gd)
            vhat = (vg_ref[:, cs] - mu) * rstd
            dvg = rstd * (dvh_ref[:, cs] - m1 - vhat * m2)
            _, dgel = _gelu_and_grad(proj_ref[:, vs])
            dp_ref[:, vs] = (dvg * dgel).astype(BF16)

        @pl.when(i == n_i - 1)
        def _():
            mask = _chunk_mask()
            for g in range(SG_GROUPS):
                dws_ref[g] = jnp.where(mask, dws_ref[g], 0.0)

    full = lambda shape: pl.BlockSpec(shape, lambda i: (0,) * len(shape))
    return _call(
        body, name=name, grid=(n_i,),
        out_shape=[jax.ShapeDtypeStruct((m_rows, n3), BF16),
                   jax.ShapeDtypeStruct((SG_GROUPS, SG_BLOCK, SG_BLOCK), F32),
                   jax.ShapeDtypeStruct((SG_GROUPS, SG_BLOCK, 1), F32),
                   jax.ShapeDtypeStruct((1, di), F32), jax.ShapeDtypeStruct((1, di), F32)],
        in_specs=[pl.BlockSpec((SG_BLOCK, n3), lambda i: (i, 0)),
                  pl.BlockSpec((SG_BLOCK, di), lambda i: (i, 0)),
                  full((1, di)), full((1, di)),
                  full((SG_GROUPS, SG_BLOCK, SG_BLOCK)), full((SG_GROUPS, SG_BLOCK, 1))],
        out_specs=[pl.BlockSpec((SG_BLOCK, n3), lambda i: (i, 0)),
                   full((SG_GROUPS, SG_BLOCK, SG_BLOCK)), full((SG_GROUPS, SG_BLOCK, 1)),
                   full((1, di)), full((1, di))],
        scratch_shapes=[pltpu.VMEM((SG_GROUPS, SG_BLOCK, SG_BLOCK), BF16),
                        pltpu.VMEM((SG_BLOCK, di), F32), pltpu.VMEM((SG_BLOCK, di), F32)],
        compiler_params=_params(),
    )(proj, dy, ln_gain, ln_bias, ws, bs)


def _lower_bound(lbraw):
    mx = jnp.maximum(lbraw[0:1, :], lbraw[1:2, :])
    e0 = jnp.exp(lbraw[0:1, :] - mx)
    e1 = jnp.exp(lbraw[1:2, :] - mx)
    p0 = e0 / (e0 + e1)
    p1 = e1 / (e0 + e1)
    return (p0 + p1) - p0, p0, p1


def _tri(lower):
    r = lax.broadcasted_iota(jnp.int32, (CHUNK, CHUNK), 0)
    c = lax.broadcasted_iota(jnp.int32, (CHUNK, CHUNK), 1)
    return ((r >= c) if lower else (c >= r)).astype(F32)


def _row(a, idx):
    r = lax.broadcasted_iota(jnp.int32, a.shape, 0)
    return jnp.sum(jnp.where(r == idx, a, 0.0), axis=0, keepdims=True)


def _hgrn_gates(qp, fp, lb, tri):
    sgm = _sigmoid_small(fp)
    f = lb + (1.0 - lb) * sgm
    k = 1.0 - f
    a = _dot(tri, jnp.log(f), NN, precision=lax.Precision.HIGHEST)
    a_mid = _row(a, CHUNK // 2 - 1)
    a_last = _row(a, CHUNK - 1)
    q, dq = _silu_and_grad(qp)
    e1, e2, e3, e4 = jnp.exp(a - a_mid), jnp.exp(a_mid - a), jnp.exp(a), jnp.exp(a_last - a)
    return dict(sgm=sgm, f=f, k=k, q=q, dq=dq, e1=e1, e2=e2, e3=e3, e4=e4, dec=jnp.exp(a_last),
                q_in=q * e1, k_in=k * e2, q_out=q * e3, k_out=k * e4)


def _causal():
    r = lax.broadcasted_iota(jnp.int32, (CHUNK, CHUNK), 0)
    c = lax.broadcasted_iota(jnp.int32, (CHUNK, CHUNK), 1)
    return r >= c


def hgrn_fwd(proj4, lbraw, gn, seq, name):
    _, m_rows, di = proj4.shape
    nb, nh, nc = m_rows // seq, di // HEAD_DIM, seq // CHUNK
    rows = min(HG_ROWS, seq)
    wide = HG_WIDE * HEAD_DIM
    ns, cpb = seq // rows, rows // CHUNK

    def body(p_ref, lb_ref, gn_ref, y_ref, sts_ref, st_ref):
        @pl.when(pl.program_id(2) == 0)
        def _():
            st_ref[...] = jnp.zeros_like(st_ref)

        tri = _tri(True)
        causal = _causal()
        gain = gn_ref[...]
        lbs = [_lower_bound(lb_ref[:, j * HEAD_DIM:(j + 1) * HEAD_DIM])[0] for j in range(HG_WIDE)]

        units = [(n, j) for n in range(cpb) for j in range(HG_WIDE)]
        rs = lambda n: slice(n * CHUNK, (n + 1) * CHUNK)
        cs = lambda j: slice(j * HEAD_DIM, (j + 1) * HEAD_DIM)
        gates, v_b, sc_b, kv, o_in, o_x = {}, {}, {}, {}, {}, {}
        for n, j in units:
            gates[n, j] = _hgrn_gates(p_ref[0, rs(n), cs(j)], p_ref[1, rs(n), cs(j)], lbs[j], tri)
            v_b[n, j] = p_ref[2, rs(n), cs(j)].astype(BF16)
        for u in units:
            t = gates[u]
            sc_b[u] = jnp.where(causal, _dot(t["q_in"].astype(BF16), t["k_in"].astype(BF16), NT), 0.0).astype(BF16)
            kv[u] = _dot(v_b[u], t["k_out"].astype(BF16), TN)
        for u in units:
            o_in[u] = _dot(sc_b[u], v_b[u], NN)
        for j in range(HG_WIDE):
            st = st_ref[j]
            for n in range(cpb):
                sts_ref[n, :, cs(j)] = st
                o_x[n, j] = _dot(gates[n, j]["q_out"].astype(BF16), st.astype(BF16), NT)
                st = st * gates[n, j]["dec"] + kv[n, j]
            st_ref[j] = st
        for n, j in units:
            o = o_in[n, j] + o_x[n, j]
            r = lax.rsqrt(jnp.mean(o * o, axis=-1, keepdims=True) + EPS)
            gp = p_ref[3, rs(n), cs(j)]
            y_ref[rs(n), cs(j)] = ((o * r * gain) * (gp * _sigmoid(gp))).astype(BF16)

    return _call(
        body, name=name, grid=(nh // HG_WIDE, nb, ns),
        out_shape=[jax.ShapeDtypeStruct((m_rows, di), BF16),
                   jax.ShapeDtypeStruct((nb * nc, HEAD_DIM, di), F32)],
        in_specs=[pl.BlockSpec((4, rows, wide), lambda hg, b, s: (0, b * ns + s, hg)),
                  pl.BlockSpec((2, wide), lambda hg, b, s: (0, hg)),
                  pl.BlockSpec((1, HEAD_DIM), lambda hg, b, s: (0, 0))],
        out_specs=[pl.BlockSpec((rows, wide), lambda hg, b, s: (b * ns + s, hg)),
                   pl.BlockSpec((cpb, HEAD_DIM, wide), lambda hg, b, s: (b * ns + s, 0, hg))],
        scratch_shapes=[pltpu.VMEM((HG_WIDE, HEAD_DIM, HEAD_DIM), F32)],
        compiler_params=_params(),
    )(proj4, lbraw, gn)


def hgrn_bwd(proj4, dy, sts, lbraw, gn, seq, name):
    _, m_rows, di = proj4.shape
    nb, nh, nc = m_rows // seq, di // HEAD_DIM, seq // CHUNK
    rows = min(HG_ROWS, seq)
    wide = HG_WIDE * HEAD_DIM
    ns, cpb = seq // rows, rows // CHUNK
    n_hg = nh // HG_WIDE

    def body(p_ref, dy_ref, sts_ref, lb_ref, gn_ref, dp_ref, dlb_ref, dgn_ref, dst_ref, lbacc_ref, gnacc_ref):
        hg, b, s = pl.program_id(0), pl.program_id(1), pl.program_id(2)
        tri, triu = _tri(True), _tri(False)
        causal = _causal()
        gain = gn_ref[...]
        first = (b == 0) & (s == 0)

        @pl.when((hg == 0) & first)
        def _():
            gnacc_ref[...] = jnp.zeros_like(gnacc_ref)

        @pl.when(first)
        def _():
            lbacc_ref[...] = jnp.zeros_like(lbacc_ref)

        @pl.when(s == 0)
        def _():
            dst_ref[...] = jnp.zeros_like(dst_ref)

        units = [(n, j) for n in range(cpb) for j in range(HG_WIDE)]
        rs = lambda n: slice(n * CHUNK, (n + 1) * CHUNK)
        cs = lambda j: slice(j * HEAD_DIM, (j + 1) * HEAD_DIM)
        lbs = [_lower_bound(lb_ref[:, cs(j)])[0] for j in range(HG_WIDE)]
        gates, v_b, st_b, sc_b, o, do_b = {}, {}, {}, {}, {}, {}
        dq_out, dsc_b, dv, g_st, dq_in, dk_in, dst_at, dk_out, ddec = {}, {}, {}, {}, {}, {}, {}, {}, {}
        for n, j in units:
            gates[n, j] = _hgrn_gates(p_ref[0, rs(n), cs(j)], p_ref[1, rs(n), cs(j)], lbs[j], tri)
            v_b[n, j] = p_ref[2, rs(n), cs(j)].astype(BF16)
            st_b[n, j] = sts_ref[n, :, cs(j)].astype(BF16)
        for u in units:
            t = gates[u]
            sc_b[u] = jnp.where(causal, _dot(t["q_in"].astype(BF16), t["k_in"].astype(BF16), NT), 0.0).astype(BF16)
        for u in units:
            o[u] = _dot(sc_b[u], v_b[u], NN) + _dot(gates[u]["q_out"].astype(BF16), st_b[u], NT)
        for n, j in units:
            ov = o[n, j]
            r = lax.rsqrt(jnp.mean(ov * ov, axis=-1, keepdims=True) + EPS)
            ohat = ov * r
            sg, dsg = _silu_and_grad(p_ref[3, rs(n), cs(j)])
            dyv = dy_ref[rs(n), cs(j)]
            dp_ref[3, rs(n), cs(j)] = (dyv * (ohat * gain) * dsg).astype(BF16)
            d_on = dyv * sg
            gnacc_ref[:, cs(j)] = gnacc_ref[:, cs(j)] + jnp.sum(d_on * ohat, axis=0, keepdims=True)
            dohat = d_on * gain
            do_b[n, j] = (r * (dohat - ohat * jnp.mean(dohat * ohat, axis=-1, keepdims=True))).astype(BF16)
        for u in units:
            dq_out[u] = _dot(do_b[u], st_b[u], NN)
            dsc_b[u] = jnp.where(causal, _dot(do_b[u], v_b[u], NT), 0.0).astype(BF16)
            dv[u] = _dot(sc_b[u], do_b[u], TN)
            g_st[u] = _dot(do_b[u], gates[u]["q_out"].astype(BF16), TN)
        for u in units:
            dq_in[u] = _dot(dsc_b[u], gates[u]["k_in"].astype(BF16), NN)
            dk_in[u] = _dot(dsc_b[u], gates[u]["q_in"].astype(BF16), TN)
        for j in range(HG_WIDE):
            dst = dst_ref[j]
            for n in reversed(range(cpb)):
                dst_at[n, j] = dst
                dst = dst * gates[n, j]["dec"] + g_st[n, j]
            dst_ref[j] = dst
        for n, j in units:
            dst = dst_at[n, j]
            dst_b = dst.astype(BF16)
            dk_out[n, j] = _dot(v_b[n, j], dst_b, NN)
            dv[n, j] = dv[n, j] + _dot(gates[n, j]["k_out"].astype(BF16), dst_b, NT)
            ddec[n, j] = jnp.sum(dst * sts_ref[n, :, cs(j)], axis=0, keepdims=True)
        for n, j in units:
            t = gates[n, j]
            dp_ref[2, rs(n), cs(j)] = dv[n, j].astype(BF16)
            dq = dq_in[n, j] * t["e1"] + dq_out[n, j] * t["e3"]
            dk = dk_in[n, j] * t["e2"] + dk_out[n, j] * t["e4"]
            w_in = dq_in[n, j] * t["q_in"] - dk_in[n, j] * t["k_in"]
            w_out = dk_out[n, j] * t["k_out"]
            da = w_in + dq_out[n, j] * t["q_out"] - w_out
            da_mid = -jnp.sum(w_in, axis=0, keepdims=True)
            da_last = jnp.sum(w_out, axis=0, keepdims=True) + ddec[n, j] * t["dec"]
            rid = lax.broadcasted_iota(jnp.int32, da.shape, 0)
            da = da + jnp.where(rid == CHUNK // 2 - 1, da_mid, 0.0) + jnp.where(rid == CHUNK - 1, da_last, 0.0)
            dlf = _dot(triu, da, NN, precision=lax.Precision.HIGHEST)
            df = dlf / t["f"] - dk
            sgm = t["sgm"]
            dp_ref[1, rs(n), cs(j)] = (df * (1.0 - lbs[j]) * sgm * (1.0 - sgm)).astype(BF16)
            lbacc_ref[:, cs(j)] = lbacc_ref[:, cs(j)] + jnp.sum(df * (1.0 - sgm), axis=0, keepdims=True)
            dp_ref[0, rs(n), cs(j)] = (dq * t["dq"]).astype(BF16)

        @pl.when((b == nb - 1) & (s == ns - 1))
        def _():
            for j in range(HG_WIDE):
                cs = slice(j * HEAD_DIM, (j + 1) * HEAD_DIM)
                _, p0, p1 = _lower_bound(lb_ref[:, cs])
                acc = lbacc_ref[:, cs]
                dlb_ref[0:1, cs] = -acc * p0 * p1
                dlb_ref[1:2, cs] = acc * p1 * (1.0 - p1)

        @pl.when((hg == n_hg - 1) & (b == nb - 1) & (s == ns - 1))
        def _():
            tot = gnacc_ref[:, 0:HEAD_DIM]
            for j in range(1, HG_WIDE):
                tot = tot + gnacc_ref[:, j * HEAD_DIM:(j + 1) * HEAD_DIM]
            dgn_ref[...] = tot

    blk = lambda hg, b, s: b * ns + (ns - 1 - s)
    return _call(
        body, name=name, grid=(n_hg, nb, ns),
        out_shape=[jax.ShapeDtypeStruct((4, m_rows, di), BF16), jax.ShapeDtypeStruct((2, di), F32),
                   jax.ShapeDtypeStruct((1, HEAD_DIM), F32)],
        in_specs=[pl.BlockSpec((4, rows, wide), lambda hg, b, s: (0, blk(hg, b, s), hg)),
                  pl.BlockSpec((rows, wide), lambda hg, b, s: (blk(hg, b, s), hg)),
                  pl.BlockSpec((cpb, HEAD_DIM, wide), lambda hg, b, s: (blk(hg, b, s), 0, hg)),
                  pl.BlockSpec((2, wide), lambda hg, b, s: (0, hg)),
                  pl.BlockSpec((1, HEAD_DIM), lambda hg, b, s: (0, 0))],
        out_specs=[pl.BlockSpec((4, rows, wide), lambda hg, b, s: (0, blk(hg, b, s), hg)),
                   pl.BlockSpec((2, wide), lambda hg, b, s: (0, hg)),
                   pl.BlockSpec((1, HEAD_DIM), lambda hg, b, s: (0, 0))],
        scratch_shapes=[pltpu.VMEM((HG_WIDE, HEAD_DIM, HEAD_DIM), F32), pltpu.VMEM((1, wide), F32),
                        pltpu.VMEM((1, wide), F32)],
        compiler_params=_params(),
    )(proj4, dy, sts, lbraw, gn)


def final_loss(x, fg, target, name):
    m_rows, d = x.shape
    tm = min(512, m_rows)

    def body(x_ref, fg_ref, t_ref, loss_ref, dx_ref, dfg_ref):
        i = pl.program_id(0)
        xv = x_ref[...]
        gain = fg_ref[...]
        r = lax.rsqrt(jnp.mean(xv * xv, axis=-1, keepdims=True) + EPS)
        xn = xv * r
        e = xn * gain - t_ref[...]
        part = 0.5 * jnp.sum(jnp.mean(e * e, axis=-1, keepdims=True), axis=0, keepdims=True)
        dyv = e / d
        p_fg = jnp.sum(dyv * xn, axis=0, keepdims=True)
        dxn = dyv * gain
        dx_ref[...] = r * (dxn - xn * jnp.mean(dxn * xn, axis=-1, keepdims=True))

        @pl.when(i == 0)
        def _():
            loss_ref[...] = part
            dfg_ref[...] = p_fg

        @pl.when(i != 0)
        def _():
            loss_ref[...] = loss_ref[...] + part
            dfg_ref[...] = dfg_ref[...] + p_fg

    row = pl.BlockSpec((tm, d), lambda i: (i, 0))
    return _call(
        body, name=name, grid=(m_rows // tm,),
        out_shape=[jax.ShapeDtypeStruct((1, 1), F32), jax.ShapeDtypeStruct((m_rows, d), F32),
                   jax.ShapeDtypeStruct((1, d), F32)],
        in_specs=[row, pl.BlockSpec((1, d), lambda i: (0, 0)), row],
        out_specs=[pl.BlockSpec((1, 1), lambda i: (0, 0)), row, pl.BlockSpec((1, d), lambda i: (0, 0))],
        compiler_params=_params(),
    )(x, fg, target)


def _pack(parts):
    flat = jnp.concatenate([p.reshape(-1) for p in parts])
    pad = (-flat.shape[0]) % (8 * LANES)
    return jnp.pad(flat, (0, pad)).reshape(-1, LANES)


def _unpack(packed, like):
    flat = packed.reshape(-1)
    out, off = [], 0
    for a in like:
        out.append(flat[off:off + a.size].reshape(a.shape))
        off += a.size
    return out


def kernel(x, c, norm_gain, w_ada, b_ada, a_w_in, a_ln_gain, a_ln_bias, a_w_s, a_b_s, a_w_out, b_w_in, b_lower_bounds, b_gn_gain, b_w_out, final_gain, loss_target, m_norm_gain, m_w_ada, m_b_ada, m_a_w_in, m_a_ln_gain, m_a_ln_bias, m_a_w_s, m_a_b_s, m_a_w_out, m_b_w_in, m_b_lower_bounds, m_b_gn_gain, m_b_w_out, m_final_gain, v_norm_gain, v_w_ada, v_b_ada, v_a_w_in, v_a_ln_gain, v_a_ln_bias, v_a_w_s, v_a_b_s, v_a_w_out, v_b_w_in, v_b_lower_bounds, v_b_gn_gain, v_b_w_out, v_final_gain):
    nb, seq, d = x.shape
    m_rows = nb * seq
    n_l = w_ada.shape[0]
    ada_cols = w_ada.shape[2]
    px, py, pc = _place()
    chip = 2 * px + py
    dev = 2 * chip + pc

    c_all = allgather_small(c.reshape(-1, LANES), "gather_c").reshape(N_DEV * nb, d)
    b_cols = lax.dynamic_slice_in_dim(b_ada, chip * ada_cols, ada_cols, axis=1).reshape(n_l, 1, ada_cols)
    mod_cols = ada_fwd(c_all, w_ada, b_cols, "ada_fwd")
    mod_g = allgather_small(mod_cols.reshape(-1, LANES), "gather_mod")
    mod_g = mod_g.reshape(N_CHIPS, 2, n_l, N_DEV * nb, ada_cols)[:, 0]
    mod_all = jnp.transpose(mod_g, (1, 2, 0, 3)).reshape(n_l, N_DEV * nb, 3 * d)
    mod_mine = lax.dynamic_slice_in_dim(mod_all, dev * nb, nb, axis=1)
    mod0 = mod_mine[0].reshape(nb, 1, 3 * d)
    mod1 = mod_mine[1].reshape(nb, 1, 3 * d)

    wa_in, tok_a_in = gather_inplace(cast_into_slot(a_w_in[0], chip, mod_mine, "cast_a_in"), "gather_a_in")
    s_ao = gather_start(cast_into_slot(a_w_out[0], chip, tok_a_in, "cast_a_out"), "gather_a_out_start")
    s_bi = gather_start(cast_into_slot(b_w_in[0], chip, s_ao[3], "cast_b_in"), "gather_b_in_start")
    s_bo = gather_start(cast_into_slot(b_w_out[0], chip, s_bi[3], "cast_b_out"), "gather_b_out_start")
    di = a_w_out.shape[1] * N_CHIPS

    x0 = x.reshape(m_rows, d)
    tgt = loss_target.reshape(m_rows, d)
    ng0 = norm_gain[0:1] + (s_ao[3][0, 0] + s_bi[3][0, 0] + s_bo[3][0, 0])
    ng1 = norm_gain[1:2]
    bs_col = a_b_s[0].reshape(SG_GROUPS, SG_BLOCK, 1)
    proj_a, h_a = inproj_fwd(x0, mod0, ng0, wa_in, seq, False, "a_inproj")
    y_a = sgu_fwd(proj_a, a_ln_gain, a_ln_bias, a_w_s[0], bs_col, "a_sgu")
    wa_out = gather_wait(*s_ao[:3], y_a, "gather_a_out_wait").reshape(di, d)
    x1, out_a = outproj_fwd(y_a, wa_out, x0, mod0, seq, "a_outproj")
    wb_in = gather_wait(*s_bi[:3], out_a, "gather_b_in_wait")
    proj_b, h_b = inproj_fwd(x1, mod1, ng1, wb_in, seq, True, "b_inproj")
    y_b, sts_b = hgrn_fwd(proj_b, b_lower_bounds, b_gn_gain, seq, "b_hgrn")
    wb_out = gather_wait(*s_bo[:3], y_b, "gather_b_out_wait").reshape(di, d)
    x2, out_b = outproj_fwd(y_b, wb_out, x1, mod1, seq, "b_outproj")
    loss_part, dx2, dfg = final_loss(x2, final_gain.reshape(1, d), tgt, "loss_head")

    shard_rows = di // N_CHIPS
    dy_b, dout_b, dgate1 = outproj_bwd(dx2, out_b, mod1, wb_out, seq, "b_outproj_bwd")
    gwb_out = grad_w_out(y_b, dout_b, "b_grad_w_out").reshape(N_CHIPS, shard_rows, d)
    e_bo = exchange_start(gwb_out, "exchange_b_out_start")
    dproj_b, dlb, dgn = hgrn_bwd(proj_b, dy_b, sts_b, b_lower_bounds, b_gn_gain + e_bo[4][0, 0], seq, "b_hgrn_bwd")
    e_bi = exchange_start(grad_w_in(h_b, dproj_b, N_CHIPS, True, "b_grad_w_in"), "exchange_b_in_start")
    dx1, dshift1, dscale1, dng1 = inproj_bwd(
        dproj_b, wb_in, x1, dx2, mod1, ng1 + e_bi[4][0, 0], seq, True, "b_inproj_bwd")

    dy_a, dout_a, dgate0 = outproj_bwd(dx1, out_a, mod0, wa_out, seq, "a_outproj_bwd")
    gwa_out = grad_w_out(y_a, dout_a, "a_grad_w_out").reshape(N_CHIPS, shard_rows, d)
    e_ao = exchange_start(gwa_out, "exchange_a_out_start")
    dproj_a, dws, dbs, dlg, dlbias = sgu_bwd(
        proj_a, dy_a, a_ln_gain + e_ao[4][0, 0], a_ln_bias, a_w_s[0], bs_col, "a_sgu_bwd")
    e_ai = exchange_start(grad_w_in(h_a, dproj_a, N_CHIPS, False, "a_grad_w_in"), "exchange_a_in_start")
    dx0, dshift0, dscale0, dng0 = inproj_bwd(
        dproj_a, wa_in, x0, dx1, mod0, norm_gain[0:1] + e_ai[4][0, 0], seq, False, "a_inproj_bwd")
    grad_x = dx0.reshape(nb, seq, d)

    dmod = jnp.concatenate([dshift0, dscale0, dgate0, dshift1, dscale1, dgate1], axis=2)
    n_dmod = dmod.size
    small_g = [jnp.concatenate([dng0, dng1], axis=0), dlg, dlbias, dws, dbs, dlb, dgn, dfg]
    packed_g = _pack([dmod] + small_g + [loss_part])
    rows = packed_g.shape[0]
    s_small = gather_all_start(
        lax.dynamic_update_slice(jnp.zeros((N_DEV, rows, LANES), F32), packed_g[None], (dev, 0, 0)),
        "gather_small_start")

    names = ("b_out", "b_in", "a_out", "a_in")
    mine, after = [], s_small[3]
    for ex, nm in zip((e_bo, e_bi, e_ao, e_ai), names):
        parts_thru, land = exchange_wait(ex[0], ex[1], ex[2], ex[3], after, "exchange_" + nm + "_wait")
        mine.append(sum_parts(parts_thru, land, chip, "sum_" + nm))
        after = mine[-1]
    theirs = swap_sibling(mine, "swap_sums")
    big = []
    for pa, pb, w, m, v, nm in zip(mine, theirs, (b_w_out, b_w_in, a_w_out, a_w_in),
                                   (m_b_w_out, m_b_w_in, m_a_w_out, m_a_w_in),
                                   (v_b_w_out, v_b_w_in, v_a_w_out, v_a_w_in), names):
        big.append([r.reshape(w.shape) for r in adamw_pair(pa, pb, w[0], m[0], v[0], "adamw_" + nm)])
    (gb_out, db_out, mb_out, vb_out), (gb_in, db_in, mb_in, vb_in), \
        (ga_out, da_out, ma_out, va_out), (ga_in, da_in, ma_in, va_in) = big

    small_w = [norm_gain, a_ln_gain, a_ln_bias, a_w_s, a_b_s, b_lower_bounds, b_gn_gain, final_gain]
    small_m = [m_norm_gain, m_a_ln_gain, m_a_ln_bias, m_a_w_s, m_a_b_s, m_b_lower_bounds, m_b_gn_gain, m_final_gain]
    small_v = [v_norm_gain, v_a_ln_gain, v_a_ln_bias, v_a_w_s, v_a_b_s, v_b_lower_bounds, v_b_gn_gain, v_final_gain]
    blank = [jnp.zeros((n_dmod,), F32)]
    one = [jnp.zeros((1,), F32)]
    gathered = gather_all_wait(s_small[0], s_small[1], s_small[2], ga_in, "gather_small_wait")
    res = small_update(gathered, _pack(blank + small_w + one), _pack(blank + small_m + one),
                       _pack(blank + small_v + one), "small_update")
    like = blank + small_w + one
    sg, sd, sm, sv = [_unpack(r, like)[1:] for r in res]
    loss = sg[-1][0]

    dmod_all = gathered[:, :n_dmod // LANES].reshape(N_DEV * nb, n_l, 3 * d)
    dmod_cols = lax.dynamic_slice_in_dim(dmod_all, chip * ada_cols, ada_cols, axis=2)
    dmod_cols = jnp.transpose(dmod_cols, (1, 0, 2))
    g_wada, d_wada, m_wada, v_wada = ada_bwd(c_all, dmod_cols, w_ada, m_w_ada, v_w_ada, "ada_bwd")
    flat = lambda a: a.reshape(1, -1)
    g_bada, d_bada, m_bada, v_bada = [
        r.reshape(b_ada.shape) for r in
        bias_update(dmod_all.reshape(N_DEV * nb, n_l * 3 * d), flat(b_ada), flat(m_b_ada), flat(v_b_ada), "bias_update")]

    def order(ng, wada, bada, ain, sm_rest, aout, bin_, bout):
        lg, lbi, ws_, bs_, lbd, gn_, fg_ = sm_rest
        return [ng, wada, bada, ain, lg, lbi, ws_, bs_, aout, bin_, lbd, gn_, bout, fg_]

    grads = order(sg[0], g_wada, g_bada, ga_in, sg[1:8], ga_out, gb_in, gb_out)
    deltas = order(sd[0], d_wada, d_bada, da_in, sd[1:8], da_out, db_in, db_out)
    new_m = order(sm[0], m_wada, m_bada, ma_in, sm[1:8], ma_out, mb_in, mb_out)
    new_v = order(sv[0], v_wada, v_bada, va_in, sv[1:8], va_out, vb_in, vb_out)
    return (loss, grad_x, *grads, *deltas, *new_m, *new_v)
```

```python
import functools

import jax
import jax.numpy as jnp
from jax import lax
from jax.experimental import pallas as pl
from jax.experimental.pallas import tpu as pltpu

F32 = jnp.float32
BF16 = jnp.bfloat16
EPS = 1e-6
CHUNK = 64
SG_BLOCK = 128
SG_GROUPS = 8
HEAD_DIM = 128
HG_WIDE = 8
HG_ROWS = 128
N_CHIPS = 4
N_DEV = 8
LANES = 128
ADAM_LR = 0.001
ADAM_B1 = 0.9
ADAM_B2 = 0.999
ADAM_EPS = 1e-08
ADAM_WD = 0.01
ADAM_STEP = 10
GELU_C0 = 0.7978845608028654
GELU_C1 = 0.044715
MESH = pl.DeviceIdType.MESH
VMEM_LIMIT = 56 * 1024 * 1024


ROW_TILE = 1024


def _col_tile(n):
    return next(t for t in (1024, 768, 512, 256) if n % t == 0)


def _call(body, **kw):
    return pl.pallas_call(body, **kw)


def _params(**kw):
    return pltpu.CompilerParams(vmem_limit_bytes=VMEM_LIMIT, **kw)


def _sigmoid(x):
    return 0.5 * jnp.tanh(0.5 * x) + 0.5


def _sigmoid_small(x):
    return 1.0 / (1.0 + jnp.exp(-x))


def _silu_and_grad(x):
    s = _sigmoid(x)
    return x * s, s * (1.0 + x * (1.0 - s))


def _gelu(x):
    return 0.5 * x * (1.0 + jnp.tanh(GELU_C0 * (x + GELU_C1 * x * x * x)))


def _gelu_and_grad(x):
    t = jnp.tanh(GELU_C0 * (x + GELU_C1 * x * x * x))
    g = 0.5 * x * (1.0 + t)
    dg = 0.5 * (1.0 + t) + 0.5 * x * (1.0 - t * t) * (GELU_C0 * (1.0 + 3.0 * GELU_C1 * x * x))
    return g, dg


def _dot(a, b, dims, precision=None):
    return lax.dot_general(a, b, (dims, ((), ())), precision=precision, preferred_element_type=F32)


NN = ((1,), (0,))
NT = ((1,), (1,))
TN = ((0,), (0,))


def _adamw(w, g, m, v):
    m = ADAM_B1 * m + (1.0 - ADAM_B1) * g
    v = ADAM_B2 * v + (1.0 - ADAM_B2) * (g * g)
    m_hat = m / (1.0 - ADAM_B1 ** ADAM_STEP)
    v_hat = v / (1.0 - ADAM_B2 ** ADAM_STEP)
    delta = -ADAM_LR * (m_hat / (jnp.sqrt(v_hat) + ADAM_EPS) + ADAM_WD * w)
    return delta, m, v


def _chunk_mask():
    r = lax.broadcasted_iota(jnp.int32, (SG_BLOCK, SG_BLOCK), 0)
    c = lax.broadcasted_iota(jnp.int32, (SG_BLOCK, SG_BLOCK), 1)
    return (c // CHUNK) <= (r // CHUNK)


def _place():
    return lax.axis_index("x"), lax.axis_index("y"), lax.axis_index("c")


def _other_chips(x, y):
    return [(1 - x, y), (x, 1 - y), (1 - x, 1 - y)]


def allgather_small(v, name):
    m_per, n = v.shape

    def body(x_ref, out_ref, send_sems, recv_sems, local_sem):
        x, y, c = _place()
        me, sibling = (x, y, c), (x, y, 1 - c)
        chips = _other_chips(x, y)

        def rows(px, py, pc):
            return out_ref.at[pl.ds((4 * px + 2 * py + pc) * m_per, m_per), :]

        def copy(k, block, to, src=None):
            return pltpu.make_async_remote_copy(
                src_ref=rows(*block) if src is None else src, dst_ref=rows(*block),
                send_sem=send_sems.at[k], recv_sem=recv_sems.at[k], device_id=to, device_id_type=MESH)

        mine = pltpu.make_async_copy(x_ref, rows(*me), local_sem)
        mine.start()
        first = [copy(0, me, sibling, src=x_ref)]
        first += [copy(1 + j, me, (*chip, c), src=x_ref) for j, chip in enumerate(chips)]
        for cp in first:
            cp.start()
        passed = [copy(4 + j, (*chip, c), sibling) for j, chip in enumerate(chips)]
        for j, chip in enumerate(chips):
            copy(1 + j, (*chip, c), me).wait_recv()
            passed[j].start()
        copy(0, sibling, me).wait_recv()
        for j, chip in enumerate(chips):
            copy(4 + j, (*chip, 1 - c), me).wait_recv()
        for cp in first + passed:
            cp.wait_send()
        mine.wait()

    return _call(
        body, name=name,
        out_shape=jax.ShapeDtypeStruct((N_DEV * m_per, n), v.dtype),
        in_specs=[pl.BlockSpec(memory_space=pltpu.VMEM)],
        out_specs=pl.BlockSpec(memory_space=pltpu.VMEM),
        scratch_shapes=[pltpu.SemaphoreType.DMA((7,)), pltpu.SemaphoreType.DMA((7,)), pltpu.SemaphoreType.DMA],
    )(v)


def _hbm_spec():
    return pl.BlockSpec(memory_space=pltpu.HBM)


def _sem_spec():
    return pl.BlockSpec(memory_space=pltpu.SEMAPHORE)


def _split_params():
    return pltpu.CompilerParams(has_side_effects=pltpu.SideEffectType.DATAFLOW_SIDE_EFFECTING)


def _hbm(a):
    return pltpu.with_memory_space_constraint(a, pltpu.HBM)


def gather_inplace(land, name):
    half = land.shape[1] // 2

    def body(land_in, land_ref, token, send_sems, recv_sems):
        del land_in
        x, y, c = _place()
        chips = _other_chips(x, y)

        def copy(k, chip_idx, core_half, to):
            rows = land_ref.at[chip_idx, pl.ds(core_half * half, half), :]
            return pltpu.make_async_remote_copy(
                src_ref=rows, dst_ref=rows, send_sem=send_sems.at[k], recv_sem=recv_sems.at[k],
                device_id=to, device_id_type=MESH)

        first = [copy(j, 2 * x + y, c, (px, py, c)) for j, (px, py) in enumerate(chips)]
        for cp in first:
            cp.start()
        passed = [copy(3 + j, 2 * px + py, c, (x, y, 1 - c)) for j, (px, py) in enumerate(chips)]
        for j, (px, py) in enumerate(chips):
            copy(j, 2 * px + py, c, (px, py, c)).wait_recv()
            passed[j].start()
        for j, (px, py) in enumerate(chips):
            copy(3 + j, 2 * px + py, 1 - c, (x, y, 1 - c)).wait_recv()
        for cp in first + passed:
            cp.wait_send()
        token[...] = jnp.zeros_like(token)

    return _call(
        body, name=name,
        out_shape=(jax.ShapeDtypeStruct(land.shape, land.dtype), jax.ShapeDtypeStruct((8, LANES), F32)),
        in_specs=[_hbm_spec()], out_specs=(_hbm_spec(), pl.BlockSpec(memory_space=pltpu.VMEM)),
        input_output_aliases={0: 0},
        scratch_shapes=[pltpu.SemaphoreType.DMA((6,)), pltpu.SemaphoreType.DMA((6,))],
    )(land)


def gather_start(land, name):
    def body(land_ref, send_sems, recv_sems, land_thru, token):
        del land_thru
        x, y, c = _place()
        for j, (px, py) in enumerate(_other_chips(x, y)):
            pltpu.make_async_remote_copy(
                src_ref=land_ref.at[2 * x + y], dst_ref=land_ref.at[2 * x + y],
                send_sem=send_sems.at[j], recv_sem=recv_sems.at[j], device_id=(px, py, c),
                device_id_type=MESH).start()
        token[...] = jnp.zeros_like(token)

    return _call(
        body, name=name,
        out_shape=(pltpu.SemaphoreType.DMA((3,)), pltpu.SemaphoreType.DMA((3,)),
                   pltpu.HBM(land.shape, land.dtype), jax.ShapeDtypeStruct((8, LANES), F32)),
        in_specs=(_hbm_spec(),),
        out_specs=(_sem_spec(), _sem_spec(), _hbm_spec(), pl.BlockSpec(memory_space=pltpu.VMEM)),
        input_output_aliases={0: 2}, compiler_params=_split_params(),
    )(_hbm(land))


def gather_wait(send_sems, recv_sems, land, after, name):
    def body(land_ref, send_sems, recv_sems, after_ref, land_out):
        del after_ref, land_out
        x, y, c = _place()
        for j, (px, py) in enumerate(_other_chips(x, y)):
            cp = pltpu.make_async_remote_copy(
                src_ref=land_ref.at[2 * x + y], dst_ref=land_ref.at[2 * px + py],
                send_sem=send_sems.at[j], recv_sem=recv_sems.at[j], device_id=(px, py, c), device_id_type=MESH)
            cp.wait_send()
            cp.wait_recv()

    return _call(
        body, name=name,
        out_shape=pltpu.HBM(land.shape, land.dtype),
        in_specs=(_hbm_spec(), _sem_spec(), _sem_spec(), pl.BlockSpec(memory_space=pl.ANY)),
        out_specs=_hbm_spec(), input_output_aliases={0: 0}, compiler_params=_split_params(),
    )(land, send_sems, recv_sems, after)


def _flips():
    return [(fx, fy, fc) for fx in (0, 1) for fy in (0, 1) for fc in (0, 1) if (fx, fy, fc) != (0, 0, 0)]


def _flipped(x, y, c, flip):
    fx, fy, fc = flip
    return (1 - x if fx else x, 1 - y if fy else y, 1 - c if fc else c)


def gather_all_start(land, name):
    def body(land_ref, send_sems, recv_sems, land_thru, token):
        del land_thru
        x, y, c = _place()
        for k, flip in enumerate(_flips()):
            pltpu.make_async_remote_copy(
                src_ref=land_ref.at[4 * x + 2 * y + c], dst_ref=land_ref.at[4 * x + 2 * y + c],
                send_sem=send_sems.at[k], recv_sem=recv_sems.at[k], device_id=_flipped(x, y, c, flip),
                device_id_type=MESH).start()
        token[...] = jnp.zeros_like(token)

    return _call(
        body, name=name,
        out_shape=(pltpu.SemaphoreType.DMA((7,)), pltpu.SemaphoreType.DMA((7,)),
                   pltpu.HBM(land.shape, land.dtype), jax.ShapeDtypeStruct((8, LANES), F32)),
        in_specs=(_hbm_spec(),),
        out_specs=(_sem_spec(), _sem_spec(), _hbm_spec(), pl.BlockSpec(memory_space=pltpu.VMEM)),
        input_output_aliases={0: 2}, compiler_params=_split_params(),
    )(_hbm(land))


def gather_all_wait(send_sems, recv_sems, land, after, name):
    def body(land_ref, send_sems, recv_sems, after_ref, land_out):
        del after_ref, land_out
        x, y, c = _place()
        for k, flip in enumerate(_flips()):
            px, py, pc = _flipped(x, y, c, flip)
            cp = pltpu.make_async_remote_copy(
                src_ref=land_ref.at[4 * x + 2 * y + c], dst_ref=land_ref.at[4 * px + 2 * py + pc],
                send_sem=send_sems.at[k], recv_sem=recv_sems.at[k], device_id=(px, py, pc), device_id_type=MESH)
            cp.wait_send()
            cp.wait_recv()

    return _call(
        body, name=name,
        out_shape=pltpu.HBM(land.shape, land.dtype),
        in_specs=(_hbm_spec(), _sem_spec(), _sem_spec(), pl.BlockSpec(memory_space=pl.ANY)),
        out_specs=_hbm_spec(), input_output_aliases={0: 0}, compiler_params=_split_params(),
    )(land, send_sems, recv_sems, after)


def exchange_start(parts, name):
    _, r, c_ = parts.shape

    def body(parts_ref, land_ref, send_sems, recv_sems, parts_thru, land_thru, token):
        del parts_thru, land_thru
        x, y, c = _place()
        for j, (px, py) in enumerate(_other_chips(x, y)):
            pltpu.make_async_remote_copy(
                src_ref=parts_ref.at[2 * px + py], dst_ref=land_ref.at[j],
                send_sem=send_sems.at[j], recv_sem=recv_sems.at[j], device_id=(px, py, c),
                device_id_type=MESH).start()
        token[...] = jnp.zeros_like(token)

    return _call(
        body, name=name,
        out_shape=(pltpu.SemaphoreType.DMA((3,)), pltpu.SemaphoreType.DMA((3,)),
                   pltpu.HBM(parts.shape, parts.dtype), pltpu.HBM((3, r, c_), parts.dtype),
                   jax.ShapeDtypeStruct((8, LANES), F32)),
        in_specs=(_hbm_spec(), _hbm_spec()),
        out_specs=(_sem_spec(), _sem_spec(), _hbm_spec(), _hbm_spec(), pl.BlockSpec(memory_space=pltpu.VMEM)),
        input_output_aliases={0: 2, 1: 3}, compiler_params=_split_params(),
    )(_hbm(parts), _hbm(lax.empty((3, r, c_), parts.dtype)))


def exchange_wait(send_sems, recv_sems, parts, land, after, name):
    def body(parts_ref, land_ref, send_sems, recv_sems, after_ref, parts_out, land_out):
        del after_ref, parts_out, land_out
        x, y, c = _place()
        for j, (px, py) in enumerate(_other_chips(x, y)):
            cp = pltpu.make_async_remote_copy(
                src_ref=parts_ref.at[2 * px + py], dst_ref=land_ref.at[j],
                send_sem=send_sems.at[j], recv_sem=recv_sems.at[j], device_id=(px, py, c), device_id_type=MESH)
            cp.wait_send()
            cp.wait_recv()

    return _call(
        body, name=name,
        out_shape=(pltpu.HBM(parts.shape, parts.dtype), pltpu.HBM(land.shape, land.dtype)),
        in_specs=(_hbm_spec(), _hbm_spec(), _sem_spec(), _sem_spec(), pl.BlockSpec(memory_space=pl.ANY)),
        out_specs=(_hbm_spec(), _hbm_spec()), input_output_aliases={0: 0, 1: 1},
        compiler_params=_split_params(),
    )(parts, land, send_sems, recv_sems, after)


def cast_into_slot(w, chip, after, name):
    r, c = w.shape
    tr = min(256, r)

    def body(s_ref, w_ref, after_ref, o_ref):
        del s_ref, after_ref
        o_ref[...] = w_ref[...].astype(BF16)

    return _call(
        body, name=name,
        grid_spec=pltpu.PrefetchScalarGridSpec(
            num_scalar_prefetch=1, grid=(r // tr,),
            in_specs=[pl.BlockSpec((tr, c), lambda i, s: (i, 0)), pl.BlockSpec(memory_space=pl.ANY)],
            out_specs=pl.BlockSpec((None, tr, c), lambda i, s: (s[0], i, 0))),
        out_shape=jax.ShapeDtypeStruct((N_CHIPS, r, c), BF16),
        compiler_params=_params(),
    )(chip.reshape(1).astype(jnp.int32), w, after)


def sum_parts(parts, land, chip, name):
    _, r, c = parts.shape
    tr = min(256, r)

    def body(s_ref, p_ref, l_ref, o_ref):
        del s_ref
        acc = p_ref[...].astype(F32) + l_ref[0].astype(F32)
        acc = acc + l_ref[1].astype(F32)
        o_ref[...] = acc + l_ref[2].astype(F32)

    return _call(
        body, name=name,
        grid_spec=pltpu.PrefetchScalarGridSpec(
            num_scalar_prefetch=1, grid=(r // tr,),
            in_specs=[pl.BlockSpec((None, tr, c), lambda i, s: (s[0], i, 0)),
                      pl.BlockSpec((3, tr, c), lambda i, s: (0, i, 0))],
            out_specs=pl.BlockSpec((tr, c), lambda i, s: (i, 0))),
        out_shape=jax.ShapeDtypeStruct((r, c), F32),
        compiler_params=_params(),
    )(chip.reshape(1).astype(jnp.int32), parts, land)


def swap_sibling(arrs, name):
    n = len(arrs)

    def body(*refs):
        ins, outs = refs[:n], refs[n:2 * n]
        send_sems, recv_sems = refs[2 * n:]
        x, y, c = _place()
        cps = []
        for w in range(n):
            cp = pltpu.make_async_remote_copy(
                src_ref=ins[w], dst_ref=outs[w], send_sem=send_sems.at[w], recv_sem=recv_sems.at[w],
                device_id=(x, y, 1 - c), device_id_type=MESH)
            cp.start()
            cps.append(cp)
        for cp in cps:
            cp.wait_recv()
        for cp in cps:
            cp.wait_send()

    return _call(
        body, name=name,
        out_shape=[jax.ShapeDtypeStruct(a.shape, a.dtype) for a in arrs],
        in_specs=[_hbm_spec()] * n, out_specs=[_hbm_spec()] * n,
        scratch_shapes=[pltpu.SemaphoreType.DMA((n,)), pltpu.SemaphoreType.DMA((n,))],
    )(*arrs)


def adamw_pair(pa, pb, w, m, v, name):
    r, c = w.shape
    tr = min(128, r)

    def body(pa_ref, pb_ref, w_ref, m_ref, v_ref, g_ref, d_ref, nm_ref, nv_ref):
        g = pa_ref[...] + pb_ref[...]
        d, nm, nv = _adamw(w_ref[...], g, m_ref[...], v_ref[...])
        g_ref[...] = g
        d_ref[...] = d
        nm_ref[...] = nm
        nv_ref[...] = nv

    spec = pl.BlockSpec((tr, c), lambda i: (i, 0))
    return _call(
        body, name=name, grid=(r // tr,),
        out_shape=[jax.ShapeDtypeStruct((r, c), F32)] * 4,
        in_specs=[spec] * 5, out_specs=[spec] * 4,
        compiler_params=_params(),
    )(pa, pb, w, m, v)


def small_update(gathered, first_row, ws, ms, vs, name):
    n_w = len(ws)
    total_rows = gathered.shape[1]

    def body(*refs):
        g_ref = refs[0]
        w_refs, m_refs, v_refs = refs[1:1 + n_w], refs[1 + n_w:1 + 2 * n_w], refs[1 + 2 * n_w:1 + 3 * n_w]
        tail_ref = refs[1 + 3 * n_w]
        outs = refs[2 + 3 * n_w:2 + 7 * n_w]
        sum_ref = refs[2 + 7 * n_w]
        acc = g_ref[0]
        for k in range(1, N_DEV):
            acc = acc + g_ref[k]
        sum_ref[...] = acc
        row = first_row
        for p in range(n_w):
            a, b = ws[p].shape
            per = b // LANES
            g_out, d_out, m_out, v_out = outs[4 * p:4 * p + 4]
            if per == 1:
                g_out[...] = sum_ref[row:row + a, :]
            else:
                for i in range(a):
                    for jc in range(per):
                        g_out[i:i + 1, jc * LANES:(jc + 1) * LANES] = sum_ref[row + i * per + jc:row + i * per + jc + 1, :]
            row += a * per
            dl, nm, nv = _adamw(w_refs[p][...], g_out[...], m_refs[p][...], v_refs[p][...])
            d_out[...] = dl
            m_out[...] = nm
            v_out[...] = nv
        tail_ref[...] = sum_ref[row:row + 1, :]

    out_shape = [jax.ShapeDtypeStruct((1, LANES), F32)]
    for w in ws:
        out_shape += [jax.ShapeDtypeStruct(w.shape, F32)] * 4
    res = _call(
        body, name=name, out_shape=out_shape,
        scratch_shapes=[pltpu.VMEM((total_rows, LANES), F32)],
        compiler_params=_params(),
    )(gathered, *ws, *ms, *vs)
    return res[0], [res[1 + 4 * p:5 + 4 * p] for p in range(n_w)]


def ada_fwd(c_all, w_ada, b_cols, name):
    n_l, d, cols = w_ada.shape
    nb = c_all.shape[0]
    tn = 256

    def body(c_ref, w_ref, b_ref, o_ref):
        cv = c_ref[...]
        ca = (cv * _sigmoid(cv)).astype(BF16)
        o_ref[...] = _dot(ca, w_ref[...].astype(BF16), NN) + b_ref[...]

    return _call(
        body, name=name, grid=(n_l, cols // tn),
        out_shape=jax.ShapeDtypeStruct((n_l, nb, cols), F32),
        in_specs=[pl.BlockSpec((nb, d), lambda l, j: (0, 0)),
                  pl.BlockSpec((None, d, tn), lambda l, j: (l, 0, j)),
                  pl.BlockSpec((None, 1, tn), lambda l, j: (l, 0, j))],
        out_specs=pl.BlockSpec((None, nb, tn), lambda l, j: (l, 0, j)),
        compiler_params=_params(),
    )(c_all, w_ada, b_cols)


def ada_bwd(c_all, dmod_cols, w, m, v, name):
    n_l, d, cols = w.shape
    nb = c_all.shape[0]
    tn = 256

    def body(c_ref, dm_ref, w_ref, m_ref, v_ref, g_ref, d_ref, nm_ref, nv_ref):
        cv = c_ref[...]
        ca = (cv * _sigmoid(cv)).astype(BF16)
        g = _dot(ca, dm_ref[...].astype(BF16), TN)
        dl, nm, nv = _adamw(w_ref[...], g, m_ref[...], v_ref[...])
        g_ref[...] = g
        d_ref[...] = dl
        nm_ref[...] = nm
        nv_ref[...] = nv

    wspec = pl.BlockSpec((None, d, tn), lambda l, j: (l, 0, j))
    return _call(
        body, name=name, grid=(n_l, cols // tn),
        out_shape=[jax.ShapeDtypeStruct((n_l, d, cols), F32)] * 4,
        in_specs=[pl.BlockSpec((nb, d), lambda l, j: (0, 0)),
                  pl.BlockSpec((None, nb, tn), lambda l, j: (l, 0, j)),
                  wspec, wspec, wspec],
        out_specs=[wspec] * 4,
        compiler_params=_params(),
    )(c_all, dmod_cols, w, m, v)


def bias_update(dmod_all, w, m, v, name):
    def body(dm_ref, w_ref, m_ref, v_ref, g_ref, d_ref, nm_ref, nv_ref):
        g = jnp.sum(dm_ref[...], axis=0, keepdims=True)
        dl, nm, nv = _adamw(w_ref[...], g, m_ref[...], v_ref[...])
        g_ref[...] = g
        d_ref[...] = dl
        nm_ref[...] = nm
        nv_ref[...] = nv

    return _call(
        body, name=name,
        out_shape=[jax.ShapeDtypeStruct(w.shape, F32)] * 4,
        compiler_params=_params(),
    )(dmod_all, w, m, v)


def inproj_fwd(x, mod, ng, wg, seq, sectioned, name):
    m_rows, d = x.shape
    nsh, _, ns = wg.shape
    n = nsh * ns
    tm, tn = min(ROW_TILE, seq), ns
    per = ns // tn

    def body(x_ref, mod_ref, ng_ref, w_ref, proj_ref, h_ref):
        @pl.when(pl.program_id(1) == 0)
        def _():
            xv = x_ref[...]
            r = lax.rsqrt(jnp.mean(xv * xv, axis=-1, keepdims=True) + EPS)
            md = mod_ref[0]
            h = (xv * r * ng_ref[...]) * (1.0 + md[:, d:2 * d]) + md[:, :d]
            h_ref[...] = h.astype(BF16)
        proj_ref[...] = _dot(h_ref[...], w_ref[...], NN)

    if sectioned:
        proj_shape = (nsh, m_rows, ns)
        proj_spec = pl.BlockSpec((None, tm, tn), lambda i, j: (j // per, i, j % per))
    else:
        proj_shape = (m_rows, n)
        proj_spec = pl.BlockSpec((tm, tn), lambda i, j: (i, j))
    return _call(
        body, name=name, grid=(m_rows // tm, n // tn),
        out_shape=[jax.ShapeDtypeStruct(proj_shape, F32), jax.ShapeDtypeStruct((m_rows, d), BF16)],
        in_specs=[pl.BlockSpec((tm, d), lambda i, j: (i, 0)),
                  pl.BlockSpec((1, 1, 3 * d), lambda i, j: ((i * tm) // seq, 0, 0)),
                  pl.BlockSpec((1, d), lambda i, j: (0, 0)),
                  pl.BlockSpec((None, d, tn), lambda i, j: (j // per, 0, j % per))],
        out_specs=[proj_spec, pl.BlockSpec((tm, d), lambda i, j: (i, 0))],
        compiler_params=_params(),
    )(x, mod, ng, wg)


def outproj_fwd(y, w, x, mod, seq, name):
    m_rows, di = y.shape
    d = w.shape[1]
    tm = min(ROW_TILE, seq)

    def body(y_ref, w_ref, x_ref, mod_ref, xn_ref, out_ref):
        acc = _dot(y_ref[...], w_ref[...], NN)
        out_ref[...] = acc
        xn_ref[...] = x_ref[...] + mod_ref[0][:, 2 * d:] * acc

    row = pl.BlockSpec((tm, d), lambda i: (i, 0))
    return _call(
        body, name=name, grid=(m_rows // tm,),
        out_shape=[jax.ShapeDtypeStruct((m_rows, d), F32)] * 2,
        in_specs=[pl.BlockSpec((tm, di), lambda i: (i, 0)),
                  pl.BlockSpec((di, d), lambda i: (0, 0)),
                  row,
                  pl.BlockSpec((1, 1, 3 * d), lambda i: ((i * tm) // seq, 0, 0))],
        out_specs=[row, row],
        compiler_params=_params(),
    )(y, w, x, mod)


def outproj_bwd(dxo, out, mod, w, seq, name):
    m_rows, d = dxo.shape
    di = w.shape[0]
    nb = m_rows // seq
    tm, tn = min(ROW_TILE, seq), _col_tile(di)

    def body(dxo_ref, out_ref, mod_ref, w_ref, dy_ref, dout_ref, dgate_ref):
        i = pl.program_id(0)

        @pl.when(pl.program_id(1) == 0)
        def _():
            dx = dxo_ref[...]
            dout_ref[...] = (mod_ref[0][:, 2 * d:] * dx).astype(BF16)
            part = jnp.sum(dx * out_ref[...], axis=0, keepdims=True)

            @pl.when((i * tm) % seq == 0)
            def _():
                dgate_ref[0] = part

            @pl.when((i * tm) % seq != 0)
            def _():
                dgate_ref[0] = dgate_ref[0] + part

        dy_ref[...] = _dot(dout_ref[...], w_ref[...], NT)

    row = pl.BlockSpec((tm, d), lambda i, j: (i, 0))
    return _call(
        body, name=name, grid=(m_rows // tm, di // tn),
        out_shape=[jax.ShapeDtypeStruct((m_rows, di), F32), jax.ShapeDtypeStruct((m_rows, d), BF16),
                   jax.ShapeDtypeStruct((nb, 1, d), F32)],
        in_specs=[row, row,
                  pl.BlockSpec((1, 1, 3 * d), lambda i, j: ((i * tm) // seq, 0, 0)),
                  pl.BlockSpec((tn, d), lambda i, j: (j, 0))],
        out_specs=[pl.BlockSpec((tm, tn), lambda i, j: (i, j)), row,
                   pl.BlockSpec((1, 1, d), lambda i, j: ((i * tm) // seq, 0, 0))],
        compiler_params=_params(),
    )(dxo, out, mod, w)


def grad_w_out(y, dout, name):
    m_rows, di = y.shape
    d = dout.shape[1]
    tm, tk = min(ROW_TILE, m_rows), _col_tile(di)
    n_m = m_rows // tm

    def body(y_ref, do_ref, o_ref, acc_ref):
        mi = pl.program_id(1)
        @pl.when(mi == 0)
        def _():
            acc_ref[...] = jnp.zeros_like(acc_ref)

        acc_ref[...] += _dot(y_ref[...], do_ref[...], TN)

        @pl.when(mi == n_m - 1)
        def _():
            o_ref[...] = acc_ref[...].astype(BF16)

    return _call(
        body, name=name, grid=(di // tk, n_m),
        out_shape=jax.ShapeDtypeStruct((di, d), BF16),
        in_specs=[pl.BlockSpec((tm, tk), lambda j, mi: (mi, j)),
                  pl.BlockSpec((tm, d), lambda j, mi: (mi, 0))],
        out_specs=pl.BlockSpec((tk, d), lambda j, mi: (j, 0)),
        scratch_shapes=[pltpu.VMEM((tk, d), F32)],
        compiler_params=_params(),
    )(y, dout)


def grad_w_in(h, dproj, nsh, sectioned, name):
    m_rows, d = h.shape
    n = dproj.shape[0] * dproj.shape[2] if sectioned else dproj.shape[1]
    ns = n // nsh
    tm, tn = min(ROW_TILE, m_rows), _col_tile(ns)
    per = ns // tn
    n_m = m_rows // tm

    def body(h_ref, dp_ref, o_ref, acc_ref):
        mi = pl.program_id(1)
        @pl.when(mi == 0)
        def _():
            acc_ref[...] = jnp.zeros_like(acc_ref)

        acc_ref[...] += _dot(h_ref[...], dp_ref[...], TN)

        @pl.when(mi == n_m - 1)
        def _():
            o_ref[...] = acc_ref[...].astype(BF16)

    if sectioned:
        dp_spec = pl.BlockSpec((None, tm, tn), lambda j, mi: (j // per, mi, j % per))
    else:
        dp_spec = pl.BlockSpec((tm, tn), lambda j, mi: (mi, j))
    return _call(
        body, name=name, grid=(n // tn, n_m),
        out_shape=jax.ShapeDtypeStruct((nsh, d, ns), BF16),
        in_specs=[pl.BlockSpec((tm, d), lambda j, mi: (mi, 0)), dp_spec],
        out_specs=pl.BlockSpec((None, d, tn), lambda j, mi: (j // per, 0, j % per)),
        scratch_shapes=[pltpu.VMEM((d, tn), F32)],
        compiler_params=_params(),
    )(h, dproj)


def inproj_bwd(dproj, wg, x, dxo, mod, ng, seq, sectioned, name):
    m_rows, d = x.shape
    nsh, _, ns = wg.shape
    n = nsh * ns
    nb = m_rows // seq
    tm, tk = min(ROW_TILE, seq), _col_tile(ns)
    per = ns // tk
    n_k = n // tk

    def body(dp_ref, w_ref, x_ref, dxo_ref, mod_ref, ng_ref, dxi_ref, dsh_ref, dsc_ref, dng_ref, acc_ref):
        i, k = pl.program_id(0), pl.program_id(1)
        @pl.when(k == 0)
        def _():
            acc_ref[...] = jnp.zeros_like(acc_ref)

        acc_ref[...] += _dot(dp_ref[...], w_ref[...], NT)

        @pl.when(k == n_k - 1)
        def _():
            dh = acc_ref[...]
            xv = x_ref[...]
            r = lax.rsqrt(jnp.mean(xv * xv, axis=-1, keepdims=True) + EPS)
            xn = xv * r
            md = mod_ref[0]
            gain = ng_ref[...]
            p_shift = jnp.sum(dh, axis=0, keepdims=True)
            p_scale = jnp.sum(dh * (xn * gain), axis=0, keepdims=True)
            drn = dh * (1.0 + md[:, d:2 * d])
            p_ng = jnp.sum(drn * xn, axis=0, keepdims=True)
            dxn = drn * gain
            dx = r * (dxn - xn * jnp.mean(dxn * xn, axis=-1, keepdims=True))
            dxi_ref[...] = dxo_ref[...] + dx

            @pl.when((i * tm) % seq == 0)
            def _():
                dsh_ref[0] = p_shift
                dsc_ref[0] = p_scale

            @pl.when((i * tm) % seq != 0)
            def _():
                dsh_ref[0] = dsh_ref[0] + p_shift
                dsc_ref[0] = dsc_ref[0] + p_scale

            @pl.when(i == 0)
            def _():
                dng_ref[...] = p_ng

            @pl.when(i != 0)
            def _():
                dng_ref[...] = dng_ref[...] + p_ng

    if sectioned:
        dp_spec = pl.BlockSpec((None, tm, tk), lambda i, k: (k // per, i, k % per))
    else:
        dp_spec = pl.BlockSpec((tm, tk), lambda i, k: (i, k))
    row = pl.BlockSpec((tm, d), lambda i, k: (i, 0))
    per_seq = pl.BlockSpec((1, 1, d), lambda i, k: ((i * tm) // seq, 0, 0))
    return _call(
        body, name=name, grid=(m_rows // tm, n_k),
        out_shape=[jax.ShapeDtypeStruct((m_rows, d), F32), jax.ShapeDtypeStruct((nb, 1, d), F32),
                   jax.ShapeDtypeStruct((nb, 1, d), F32), jax.ShapeDtypeStruct((1, d), F32)],
        in_specs=[dp_spec,
                  pl.BlockSpec((None, d, tk), lambda i, k: (k // per, 0, k % per)),
                  row, row,
                  pl.BlockSpec((1, 1, 3 * d), lambda i, k: ((i * tm) // seq, 0, 0)),
                  pl.BlockSpec((1, d), lambda i, k: (0, 0))],
        out_specs=[row, per_seq, per_seq, pl.BlockSpec((1, d), lambda i, k: (0, 0))],
        scratch_shapes=[pltpu.VMEM((tm, d), F32)],
        compiler_params=_params(),
    )(dproj, wg, x, dxo, mod, ng)


def _sgu_stats(proj_ref, vg_ref, di, gd):
    s1 = jnp.zeros((SG_BLOCK, 1), F32)
    for g in range(SG_GROUPS):
        vg = _gelu(proj_ref[:, di + g * gd:di + (g + 1) * gd])
        vg_ref[:, g * gd:(g + 1) * gd] = vg
        s1 = s1 + jnp.sum(vg, axis=1, keepdims=True)
    mu = s1 / di
    s2 = jnp.zeros((SG_BLOCK, 1), F32)
    for g in range(SG_GROUPS):
        dv = vg_ref[:, g * gd:(g + 1) * gd] - mu
        s2 = s2 + jnp.sum(dv * dv, axis=1, keepdims=True)
    return mu, lax.rsqrt(s2 / di + EPS)


def sgu_fwd(proj, ln_gain, ln_bias, ws, bs, name):
    m_rows, n3 = proj.shape
    di = n3 // 3
    gd = di // SG_GROUPS

    def body(proj_ref, lg_ref, lb_ref, ws_ref, bs_ref, y_ref, wsm_ref, vg_ref):
        @pl.when(pl.program_id(0) == 0)
        def _():
            mask = _chunk_mask()
            for g in range(SG_GROUPS):
                wsm_ref[g] = jnp.where(mask, ws_ref[g], 0.0).astype(BF16)

        mu, rstd = _sgu_stats(proj_ref, vg_ref, di, gd)
        for g in range(SG_GROUPS):
            cs = slice(g * gd, (g + 1) * gd)
            vln = (vg_ref[:, cs] - mu) * rstd * lg_ref[:, cs] + lb_ref[:, cs]
            s = _dot(wsm_ref[g], vln.astype(BF16), NN) + bs_ref[g]
            u = _gelu(proj_ref[:, cs])
            gp = proj_ref[:, 2 * di + g * gd:2 * di + (g + 1) * gd]
            y_ref[:, cs] = (u * s * (gp * _sigmoid(gp))).astype(BF16)

    full = lambda shape: pl.BlockSpec(shape, lambda i: (0,) * len(shape))
    return _call(
        body, name=name, grid=(m_rows // SG_BLOCK,),
        out_shape=jax.ShapeDtypeStruct((m_rows, di), BF16),
        in_specs=[pl.BlockSpec((SG_BLOCK, n3), lambda i: (i, 0)),
                  full((1, di)), full((1, di)),
                  full((SG_GROUPS, SG_BLOCK, SG_BLOCK)), full((SG_GROUPS, SG_BLOCK, 1))],
        out_specs=pl.BlockSpec((SG_BLOCK, di), lambda i: (i, 0)),
        scratch_shapes=[pltpu.VMEM((SG_GROUPS, SG_BLOCK, SG_BLOCK), BF16), pltpu.VMEM((SG_BLOCK, di), F32)],
        compiler_params=_params(),
    )(proj, ln_gain, ln_bias, ws, bs)


def sgu_bwd(proj, dy, ln_gain, ln_bias, ws, bs, name):
    m_rows, n3 = proj.shape
    di = n3 // 3
    gd = di // SG_GROUPS
    n_i = m_rows // SG_BLOCK

    def body(proj_ref, dy_ref, lg_ref, lb_ref, ws_ref, bs_ref,
             dp_ref, dws_ref, dbs_ref, dlg_ref, dlb_ref, wsm_ref, vg_ref, dvh_ref):
        i = pl.program_id(0)

        @pl.when(i == 0)
        def _():
            mask = _chunk_mask()
            for g in range(SG_GROUPS):
                wsm_ref[g] = jnp.where(mask, ws_ref[g], 0.0).astype(BF16)
            dws_ref[...] = jnp.zeros_like(dws_ref)
            dbs_ref[...] = jnp.zeros_like(dbs_ref)
            dlg_ref[...] = jnp.zeros_like(dlg_ref)
            dlb_ref[...] = jnp.zeros_like(dlb_ref)

        mu, rstd = _sgu_stats(proj_ref, vg_ref, di, gd)
        m1 = jnp.zeros((SG_BLOCK, 1), F32)
        m2 = jnp.zeros((SG_BLOCK, 1), F32)
        for g in range(SG_GROUPS):
            cs = slice(g * gd, (g + 1) * gd)
            gs = slice(2 * di + g * gd, 2 * di + (g + 1) * gd)
            gain = lg_ref[:, cs]
            vhat = (vg_ref[:, cs] - mu) * rstd
            vln_b = (vhat * gain + lb_ref[:, cs]).astype(BF16)
            s = _dot(wsm_ref[g], vln_b, NN) + bs_ref[g]
            u, du = _gelu_and_grad(proj_ref[:, cs])
            sg, dsg = _silu_and_grad(proj_ref[:, gs])
            dyv = dy_ref[:, cs]
            dp_ref[:, cs] = (dyv * s * sg * du).astype(BF16)
            dp_ref[:, gs] = (dyv * u * s * dsg).astype(BF16)
            ds = dyv * u * sg
            ds_b = ds.astype(BF16)
            dws_ref[g] = dws_ref[g] + _dot(ds_b, vln_b, NT)
            dbs_ref[g] = dbs_ref[g] + jnp.sum(ds, axis=1, keepdims=True)
            dvln = _dot(wsm_ref[g], ds_b, TN)
            dlg_ref[:, cs] = dlg_ref[:, cs] + jnp.sum(dvln * vhat, axis=0, keepdims=True)
            dlb_ref[:, cs] = dlb_ref[:, cs] + jnp.sum(dvln, axis=0, keepdims=True)
            dvh = dvln * gain
            dvh_ref[:, cs] = dvh
            m1 = m1 + jnp.sum(dvh, axis=1, keepdims=True)
            m2 = m2 + jnp.sum(dvh * vhat, axis=1, keepdims=True)
        m1 = m1 / di
        m2 = m2 / di
        for g in range(SG_GROUPS):
            cs = slice(g * gd, (g + 1) * gd)
            vs = slice(di + g * gd, di + (g + 1) * gd)
            vhat = (vg_ref[:, cs] - mu) * rstd
            dvg = rstd * (dvh_ref[:, cs] - m1 - vhat * m2)
            _, dgel = _gelu_and_grad(proj_ref[:, vs])
            dp_ref[:, vs] = (dvg * dgel).astype(BF16)

        @pl.when(i == n_i - 1)
        def _():
            mask = _chunk_mask()
            for g in range(SG_GROUPS):
                dws_ref[g] = jnp.where(mask, dws_ref[g], 0.0)

    full = lambda shape: pl.BlockSpec(shape, lambda i: (0,) * len(shape))
    return _call(
        body, name=name, grid=(n_i,),
        out_shape=[jax.ShapeDtypeStruct((m_rows, n3), BF16),
                   jax.ShapeDtypeStruct((SG_GROUPS, SG_BLOCK, SG_BLOCK), F32),
                   jax.ShapeDtypeStruct((SG_GROUPS, SG_BLOCK, 1), F32),
                   jax.ShapeDtypeStruct((1, di), F32), jax.ShapeDtypeStruct((1, di), F32)],
        in_specs=[pl.BlockSpec((SG_BLOCK, n3), lambda i: (i, 0)),
                  pl.BlockSpec((SG_BLOCK, di), lambda i: (i, 0)),
                  full((1, di)), full((1, di)),
                  full((SG_GROUPS, SG_BLOCK, SG_BLOCK)), full((SG_GROUPS, SG_BLOCK, 1))],
        out_specs=[pl.BlockSpec((SG_BLOCK, n3), lambda i: (i, 0)),
                   full((SG_GROUPS, SG_BLOCK, SG_BLOCK)), full((SG_GROUPS, SG_BLOCK, 1)),
                   full((1, di)), full((1, di))],
        scratch_shapes=[pltpu.VMEM((SG_GROUPS, SG_BLOCK, SG_BLOCK), BF16),
                        pltpu.VMEM((SG_BLOCK, di), F32), pltpu.VMEM((SG_BLOCK, di), F32)],
        compiler_params=_params(),
    )(proj, dy, ln_gain, ln_bias, ws, bs)


def _lower_bound(lbraw):
    mx = jnp.maximum(lbraw[0:1, :], lbraw[1:2, :])
    e0 = jnp.exp(lbraw[0:1, :] - mx)
    e1 = jnp.exp(lbraw[1:2, :] - mx)
    p0 = e0 / (e0 + e1)
    p1 = e1 / (e0 + e1)
    return (p0 + p1) - p0, p0, p1


def _tri(lower):
    r = lax.broadcasted_iota(jnp.int32, (CHUNK, CHUNK), 0)
    c = lax.broadcasted_iota(jnp.int32, (CHUNK, CHUNK), 1)
    return ((r >= c) if lower else (c >= r)).astype(F32)


def _row(a, idx):
    r = lax.broadcasted_iota(jnp.int32, a.shape, 0)
    return jnp.sum(jnp.where(r == idx, a, 0.0), axis=0, keepdims=True)


def _hgrn_gates(qp, fp, lb, tri):
    sgm = _sigmoid_small(fp)
    f = lb + (1.0 - lb) * sgm
    k = 1.0 - f
    a = _dot(tri, jnp.log(f), NN, precision=lax.Precision.HIGHEST)
    a_mid = _row(a, CHUNK // 2 - 1)
    a_last = _row(a, CHUNK - 1)
    q, dq = _silu_and_grad(qp)
    e1, e2, e3, e4 = jnp.exp(a - a_mid), jnp.exp(a_mid - a), jnp.exp(a), jnp.exp(a_last - a)
    return dict(sgm=sgm, f=f, k=k, q=q, dq=dq, e1=e1, e2=e2, e3=e3, e4=e4, dec=jnp.exp(a_last),
                q_in=q * e1, k_in=k * e2, q_out=q * e3, k_out=k * e4)


def _causal():
    r = lax.broadcasted_iota(jnp.int32, (CHUNK, CHUNK), 0)
    c = lax.broadcasted_iota(jnp.int32, (CHUNK, CHUNK), 1)
    return r >= c


def hgrn_fwd(proj4, lbraw, gn, seq, name):
    _, m_rows, di = proj4.shape
    nb, nh, nc = m_rows // seq, di // HEAD_DIM, seq // CHUNK
    rows = min(HG_ROWS, seq)
    wide = HG_WIDE * HEAD_DIM
    ns, cpb = seq // rows, rows // CHUNK

    def body(p_ref, lb_ref, gn_ref, y_ref, sts_ref, st_ref):
        @pl.when(pl.program_id(2) == 0)
        def _():
            st_ref[...] = jnp.zeros_like(st_ref)

        tri = _tri(True)
        causal = _causal()
        gain = gn_ref[...]
        lbs = [_lower_bound(lb_ref[:, j * HEAD_DIM:(j + 1) * HEAD_DIM])[0] for j in range(HG_WIDE)]

        units = [(n, j) for n in range(cpb) for j in range(HG_WIDE)]
        rs = lambda n: slice(n * CHUNK, (n + 1) * CHUNK)
        cs = lambda j: slice(j * HEAD_DIM, (j + 1) * HEAD_DIM)
        gates, v_b, sc_b, kv, o_in, o_x = {}, {}, {}, {}, {}, {}
        for n, j in units:
            gates[n, j] = _hgrn_gates(p_ref[0, rs(n), cs(j)], p_ref[1, rs(n), cs(j)], lbs[j], tri)
            v_b[n, j] = p_ref[2, rs(n), cs(j)].astype(BF16)
        for u in units:
            t = gates[u]
            sc_b[u] = jnp.where(causal, _dot(t["q_in"].astype(BF16), t["k_in"].astype(BF16), NT), 0.0).astype(BF16)
            kv[u] = _dot(v_b[u], t["k_out"].astype(BF16), TN)
        for u in units:
            o_in[u] = _dot(sc_b[u], v_b[u], NN)
        for j in range(HG_WIDE):
            st = st_ref[j]
            for n in range(cpb):
                sts_ref[n, :, cs(j)] = st
                o_x[n, j] = _dot(gates[n, j]["q_out"].astype(BF16), st.astype(BF16), NT)
                st = st * gates[n, j]["dec"] + kv[n, j]
            st_ref[j] = st
        for n, j in units:
            o = o_in[n, j] + o_x[n, j]
            r = lax.rsqrt(jnp.mean(o * o, axis=-1, keepdims=True) + EPS)
            gp = p_ref[3, rs(n), cs(j)]
            y_ref[rs(n), cs(j)] = ((o * r * gain) * (gp * _sigmoid(gp))).astype(BF16)

    return _call(
        body, name=name, grid=(nh // HG_WIDE, nb, ns),
        out_shape=[jax.ShapeDtypeStruct((m_rows, di), BF16),
                   jax.ShapeDtypeStruct((nb * nc, HEAD_DIM, di), F32)],
        in_specs=[pl.BlockSpec((4, rows, wide), lambda hg, b, s: (0, b * ns + s, hg)),
                  pl.BlockSpec((2, wide), lambda hg, b, s: (0, hg)),
                  pl.BlockSpec((1, HEAD_DIM), lambda hg, b, s: (0, 0))],
        out_specs=[pl.BlockSpec((rows, wide), lambda hg, b, s: (b * ns + s, hg)),
                   pl.BlockSpec((cpb, HEAD_DIM, wide), lambda hg, b, s: (b * ns + s, 0, hg))],
        scratch_shapes=[pltpu.VMEM((HG_WIDE, HEAD_DIM, HEAD_DIM), F32)],
        compiler_params=_params(),
    )(proj4, lbraw, gn)


def hgrn_bwd(proj4, dy, sts, lbraw, gn, seq, name):
    _, m_rows, di = proj4.shape
    nb, nh, nc = m_rows // seq, di // HEAD_DIM, seq // CHUNK
    rows = min(HG_ROWS, seq)
    wide = HG_WIDE * HEAD_DIM
    ns, cpb = seq // rows, rows // CHUNK
    n_hg = nh // HG_WIDE

    def body(p_ref, dy_ref, sts_ref, lb_ref, gn_ref, dp_ref, dlb_ref, dgn_ref, dst_ref, lbacc_ref, gnacc_ref):
        hg, b, s = pl.program_id(0), pl.program_id(1), pl.program_id(2)
        tri, triu = _tri(True), _tri(False)
        causal = _causal()
        gain = gn_ref[...]
        first = (b == 0) & (s == 0)

        @pl.when((hg == 0) & first)
        def _():
            gnacc_ref[...] = jnp.zeros_like(gnacc_ref)

        @pl.when(first)
        def _():
            lbacc_ref[...] = jnp.zeros_like(lbacc_ref)

        @pl.when(s == 0)
        def _():
            dst_ref[...] = jnp.zeros_like(dst_ref)

        units = [(n, j) for n in range(cpb) for j in range(HG_WIDE)]
        rs = lambda n: slice(n * CHUNK, (n + 1) * CHUNK)
        cs = lambda j: slice(j * HEAD_DIM, (j + 1) * HEAD_DIM)
        lbs = [_lower_bound(lb_ref[:, cs(j)])[0] for j in range(HG_WIDE)]
        gates, v_b, st_b, sc_b, o, do_b = {}, {}, {}, {}, {}, {}
        dq_out, dsc_b, dv, g_st, dq_in, dk_in, dst_at, dk_out, ddec = {}, {}, {}, {}, {}, {}, {}, {}, {}
        for n, j in units:
            gates[n, j] = _hgrn_gates(p_ref[0, rs(n), cs(j)], p_ref[1, rs(n), cs(j)], lbs[j], tri)
            v_b[n, j] = p_ref[2, rs(n), cs(j)].astype(BF16)
            st_b[n, j] = sts_ref[n, :, cs(j)].astype(BF16)
        for u in units:
            t = gates[u]
            sc_b[u] = jnp.where(causal, _dot(t["q_in"].astype(BF16), t["k_in"].astype(BF16), NT), 0.0).astype(BF16)
        for u in units:
            o[u] = _dot(sc_b[u], v_b[u], NN) + _dot(gates[u]["q_out"].astype(BF16), st_b[u], NT)
        for n, j in units:
            ov = o[n, j]
            r = lax.rsqrt(jnp.mean(ov * ov, axis=-1, keepdims=True) + EPS)
            ohat = ov * r
            sg, dsg = _silu_and_grad(p_ref[3, rs(n), cs(j)])
            dyv = dy_ref[rs(n), cs(j)]
            dp_ref[3, rs(n), cs(j)] = (dyv * (ohat * gain) * dsg).astype(BF16)
            d_on = dyv * sg
            gnacc_ref[:, cs(j)] = gnacc_ref[:, cs(j)] + jnp.sum(d_on * ohat, axis=0, keepdims=True)
            dohat = d_on * gain
            do_b[n, j] = (r * (dohat - ohat * jnp.mean(dohat * ohat, axis=-1, keepdims=True))).astype(BF16)
        for u in units:
            dq_out[u] = _dot(do_b[u], st_b[u], NN)
            dsc_b[u] = jnp.where(causal, _dot(do_b[u], v_b[u], NT), 0.0).astype(BF16)
            dv[u] = _dot(sc_b[u], do_b[u], TN)
            g_st[u] = _dot(do_b[u], gates[u]["q_out"].astype(BF16), TN)
        for u in units:
            dq_in[u] = _dot(dsc_b[u], gates[u]["k_in"].astype(BF16), NN)
            dk_in[u] = _dot(dsc_b[u], gates[u]["q_in"].astype(BF16), TN)
        for j in range(HG_WIDE):
            dst = dst_ref[j]
            for n in reversed(range(cpb)):
                dst_at[n, j] = dst
                dst = dst * gates[n, j]["dec"] + g_st[n, j]
            dst_ref[j] = dst
        for n, j in units:
            dst = dst_at[n, j]
            dst_b = dst.astype(BF16)
            dk_out[n, j] = _dot(v_b[n, j], dst_b, NN)
            dv[n, j] = dv[n, j] + _dot(gates[n, j]["k_out"].astype(BF16), dst_b, NT)
            ddec[n, j] = jnp.sum(dst * sts_ref[n, :, cs(j)], axis=0, keepdims=True)
        for n, j in units:
            t = gates[n, j]
            dp_ref[2, rs(n), cs(j)] = dv[n, j].astype(BF16)
            dq = dq_in[n, j] * t["e1"] + dq_out[n, j] * t["e3"]
            dk = dk_in[n, j] * t["e2"] + dk_out[n, j] * t["e4"]
            w_in = dq_in[n, j] * t["q_in"] - dk_in[n, j] * t["k_in"]
            w_out = dk_out[n, j] * t["k_out"]
            da = w_in + dq_out[n, j] * t["q_out"] - w_out
            da_mid = -jnp.sum(w_in, axis=0, keepdims=True)
            da_last = jnp.sum(w_out, axis=0, keepdims=True) + ddec[n, j] * t["dec"]
            rid = lax.broadcasted_iota(jnp.int32, da.shape, 0)
            da = da + jnp.where(rid == CHUNK // 2 - 1, da_mid, 0.0) + jnp.where(rid == CHUNK - 1, da_last, 0.0)
            dlf = _dot(triu, da, NN, precision=lax.Precision.HIGHEST)
            df = dlf / t["f"] - dk
            sgm = t["sgm"]
            dp_ref[1, rs(n), cs(j)] = (df * (1.0 - lbs[j]) * sgm * (1.0 - sgm)).astype(BF16)
            lbacc_ref[:, cs(j)] = lbacc_ref[:, cs(j)] + jnp.sum(df * (1.0 - sgm), axis=0, keepdims=True)
            dp_ref[0, rs(n), cs(j)] = (dq * t["dq"]).astype(BF16)

        @pl.when((b == nb - 1) & (s == ns - 1))
        def _():
            for j in range(HG_WIDE):
                cs = slice(j * HEAD_DIM, (j + 1) * HEAD_DIM)
                _, p0, p1 = _lower_bound(lb_ref[:, cs])
                acc = lbacc_ref[:, cs]
                dlb_ref[0:1, cs] = -acc * p0 * p1
                dlb_ref[1:2, cs] = acc * p1 * (1.0 - p1)

        @pl.when((hg == n_hg - 1) & (b == nb - 1) & (s == ns - 1))
        def _():
            tot = gnacc_ref[:, 0:HEAD_DIM]
            for j in range(1, HG_WIDE):
                tot = tot + gnacc_ref[:, j * HEAD_DIM:(j + 1) * HEAD_DIM]
            dgn_ref[...] = tot

    blk = lambda hg, b, s: b * ns + (ns - 1 - s)
    return _call(
        body, name=name, grid=(n_hg, nb, ns),
        out_shape=[jax.ShapeDtypeStruct((4, m_rows, di), BF16), jax.ShapeDtypeStruct((2, di), F32),
                   jax.ShapeDtypeStruct((1, HEAD_DIM), F32)],
        in_specs=[pl.BlockSpec((4, rows, wide), lambda hg, b, s: (0, blk(hg, b, s), hg)),
                  pl.BlockSpec((rows, wide), lambda hg, b, s: (blk(hg, b, s), hg)),
                  pl.BlockSpec((cpb, HEAD_DIM, wide), lambda hg, b, s: (blk(hg, b, s), 0, hg)),
                  pl.BlockSpec((2, wide), lambda hg, b, s: (0, hg)),
                  pl.BlockSpec((1, HEAD_DIM), lambda hg, b, s: (0, 0))],
        out_specs=[pl.BlockSpec((4, rows, wide), lambda hg, b, s: (0, blk(hg, b, s), hg)),
                   pl.BlockSpec((2, wide), lambda hg, b, s: (0, hg)),
                   pl.BlockSpec((1, HEAD_DIM), lambda hg, b, s: (0, 0))],
        scratch_shapes=[pltpu.VMEM((HG_WIDE, HEAD_DIM, HEAD_DIM), F32), pltpu.VMEM((1, wide), F32),
                        pltpu.VMEM((1, wide), F32)],
        compiler_params=_params(),
    )(proj4, dy, sts, lbraw, gn)


def final_loss(x, fg, target, name):
    m_rows, d = x.shape
    tm = min(512, m_rows)

    def body(x_ref, fg_ref, t_ref, loss_ref, dx_ref, dfg_ref):
        i = pl.program_id(0)
        xv = x_ref[...]
        gain = fg_ref[...]
        r = lax.rsqrt(jnp.mean(xv * xv, axis=-1, keepdims=True) + EPS)
        xn = xv * r
        e = xn * gain - t_ref[...]
        part = 0.5 * jnp.sum(jnp.mean(e * e, axis=-1, keepdims=True), axis=0, keepdims=True)
        dyv = e / d
        p_fg = jnp.sum(dyv * xn, axis=0, keepdims=True)
        dxn = dyv * gain
        dx_ref[...] = r * (dxn - xn * jnp.mean(dxn * xn, axis=-1, keepdims=True))

        @pl.when(i == 0)
        def _():
            loss_ref[...] = part
            dfg_ref[...] = p_fg

        @pl.when(i != 0)
        def _():
            loss_ref[...] = loss_ref[...] + part
            dfg_ref[...] = dfg_ref[...] + p_fg

    row = pl.BlockSpec((tm, d), lambda i: (i, 0))
    return _call(
        body, name=name, grid=(m_rows // tm,),
        out_shape=[jax.ShapeDtypeStruct((1, 1), F32), jax.ShapeDtypeStruct((m_rows, d), F32),
                   jax.ShapeDtypeStruct((1, d), F32)],
        in_specs=[row, pl.BlockSpec((1, d), lambda i: (0, 0)), row],
        out_specs=[pl.BlockSpec((1, 1), lambda i: (0, 0)), row, pl.BlockSpec((1, d), lambda i: (0, 0))],
        compiler_params=_params(),
    )(x, fg, target)


def _pack(parts):
    flat = jnp.concatenate([p.reshape(-1) for p in parts])
    pad = (-flat.shape[0]) % (8 * LANES)
    return jnp.pad(flat, (0, pad)).reshape(-1, LANES)


def kernel(x, c, norm_gain, w_ada, b_ada, a_w_in, a_ln_gain, a_ln_bias, a_w_s, a_b_s, a_w_out, b_w_in, b_lower_bounds, b_gn_gain, b_w_out, final_gain, loss_target, m_norm_gain, m_w_ada, m_b_ada, m_a_w_in, m_a_ln_gain, m_a_ln_bias, m_a_w_s, m_a_b_s, m_a_w_out, m_b_w_in, m_b_lower_bounds, m_b_gn_gain, m_b_w_out, m_final_gain, v_norm_gain, v_w_ada, v_b_ada, v_a_w_in, v_a_ln_gain, v_a_ln_bias, v_a_w_s, v_a_b_s, v_a_w_out, v_b_w_in, v_b_lower_bounds, v_b_gn_gain, v_b_w_out, v_final_gain):
    nb, seq, d = x.shape
    m_rows = nb * seq
    n_l = w_ada.shape[0]
    ada_cols = w_ada.shape[2]
    px, py, pc = _place()
    chip = 2 * px + py
    dev = 2 * chip + pc

    c_all = allgather_small(c.reshape(-1, LANES), "gather_c").reshape(N_DEV * nb, d)
    b_cols = lax.dynamic_slice_in_dim(b_ada, chip * ada_cols, ada_cols, axis=1).reshape(n_l, 1, ada_cols)
    mod_cols = ada_fwd(c_all, w_ada, b_cols, "ada_fwd")
    mod_g = allgather_small(mod_cols.reshape(-1, LANES), "gather_mod")
    mod_g = mod_g.reshape(N_CHIPS, 2, n_l, N_DEV * nb, ada_cols)[:, 0]
    mod_all = jnp.transpose(mod_g, (1, 2, 0, 3)).reshape(n_l, N_DEV * nb, 3 * d)
    mod_mine = lax.dynamic_slice_in_dim(mod_all, dev * nb, nb, axis=1)
    mod0 = mod_mine[0].reshape(nb, 1, 3 * d)
    mod1 = mod_mine[1].reshape(nb, 1, 3 * d)

    wa_in, tok_a_in = gather_inplace(cast_into_slot(a_w_in[0], chip, mod_mine, "cast_a_in"), "gather_a_in")
    s_ao = gather_start(cast_into_slot(a_w_out[0], chip, tok_a_in, "cast_a_out"), "gather_a_out_start")
    s_bi = gather_start(cast_into_slot(b_w_in[0], chip, s_ao[3], "cast_b_in"), "gather_b_in_start")
    s_bo = gather_start(cast_into_slot(b_w_out[0], chip, s_bi[3], "cast_b_out"), "gather_b_out_start")
    di = a_w_out.shape[1] * N_CHIPS

    x0 = x.reshape(m_rows, d)
    tgt = loss_target.reshape(m_rows, d)
    ng0 = norm_gain[0:1] + (s_ao[3][0, 0] + s_bi[3][0, 0] + s_bo[3][0, 0])
    ng1 = norm_gain[1:2]
    bs_col = a_b_s[0].reshape(SG_GROUPS, SG_BLOCK, 1)
    proj_a, h_a = inproj_fwd(x0, mod0, ng0, wa_in, seq, False, "a_inproj")
    y_a = sgu_fwd(proj_a, a_ln_gain, a_ln_bias, a_w_s[0], bs_col, "a_sgu")
    wa_out = gather_wait(*s_ao[:3], y_a, "gather_a_out_wait").reshape(di, d)
    x1, out_a = outproj_fwd(y_a, wa_out, x0, mod0, seq, "a_outproj")
    wb_in = gather_wait(*s_bi[:3], out_a, "gather_b_in_wait")
    proj_b, h_b = inproj_fwd(x1, mod1, ng1, wb_in, seq, True, "b_inproj")
    y_b, sts_b = hgrn_fwd(proj_b, b_lower_bounds, b_gn_gain, seq, "b_hgrn")
    wb_out = gather_wait(*s_bo[:3], y_b, "gather_b_out_wait").reshape(di, d)
    x2, out_b = outproj_fwd(y_b, wb_out, x1, mod1, seq, "b_outproj")
    loss_part, dx2, dfg = final_loss(x2, final_gain.reshape(1, d), tgt, "loss_head")

    shard_rows = di // N_CHIPS
    dy_b, dout_b, dgate1 = outproj_bwd(dx2, out_b, mod1, wb_out, seq, "b_outproj_bwd")
    gwb_out = grad_w_out(y_b, dout_b, "b_grad_w_out").reshape(N_CHIPS, shard_rows, d)
    e_bo = exchange_start(gwb_out, "exchange_b_out_start")
    dproj_b, dlb, dgn = hgrn_bwd(proj_b, dy_b, sts_b, b_lower_bounds, b_gn_gain + e_bo[4][0, 0], seq, "b_hgrn_bwd")
    e_bi = exchange_start(grad_w_in(h_b, dproj_b, N_CHIPS, True, "b_grad_w_in"), "exchange_b_in_start")
    dx1, dshift1, dscale1, dng1 = inproj_bwd(
        dproj_b, wb_in, x1, dx2, mod1, ng1 + e_bi[4][0, 0], seq, True, "b_inproj_bwd")

    dy_a, dout_a, dgate0 = outproj_bwd(dx1, out_a, mod0, wa_out, seq, "a_outproj_bwd")
    gwa_out = grad_w_out(y_a, dout_a, "a_grad_w_out").reshape(N_CHIPS, shard_rows, d)
    e_ao = exchange_start(gwa_out, "exchange_a_out_start")
    dproj_a, dws, dbs, dlg, dlbias = sgu_bwd(
        proj_a, dy_a, a_ln_gain + e_ao[4][0, 0], a_ln_bias, a_w_s[0], bs_col, "a_sgu_bwd")
    e_ai = exchange_start(grad_w_in(h_a, dproj_a, N_CHIPS, False, "a_grad_w_in"), "exchange_a_in_start")
    dx0, dshift0, dscale0, dng0 = inproj_bwd(
        dproj_a, wa_in, x0, dx1, mod0, norm_gain[0:1] + e_ai[4][0, 0], seq, False, "a_inproj_bwd")
    grad_x = dx0.reshape(nb, seq, d)

    dmod = jnp.concatenate([dshift0, dscale0, dgate0, dshift1, dscale1, dgate1], axis=2)
    n_dmod = dmod.size
    small_g = [jnp.concatenate([dng0, dng1], axis=0), dlg, dlbias, dws, dbs, dlb, dfg, dgn]
    packed_g = _pack([dmod] + small_g + [loss_part])
    rows = packed_g.shape[0]
    s_small = gather_all_start(
        lax.dynamic_update_slice(jnp.zeros((N_DEV, rows, LANES), F32), packed_g[None], (dev, 0, 0)),
        "gather_small_start")

    names = ("b_out", "b_in", "a_out", "a_in")
    mine, after = [], s_small[3]
    for ex, nm in zip((e_bo, e_bi, e_ao, e_ai), names):
        parts_thru, land = exchange_wait(ex[0], ex[1], ex[2], ex[3], after, "exchange_" + nm + "_wait")
        mine.append(sum_parts(parts_thru, land, chip, "sum_" + nm))
        after = mine[-1]
    theirs = swap_sibling(mine, "swap_sums")
    big = []
    for pa, pb, w, m, v, nm in zip(mine, theirs, (b_w_out, b_w_in, a_w_out, a_w_in),
                                   (m_b_w_out, m_b_w_in, m_a_w_out, m_a_w_in),
                                   (v_b_w_out, v_b_w_in, v_a_w_out, v_a_w_in), names):
        big.append([r.reshape(w.shape) for r in adamw_pair(pa, pb, w[0], m[0], v[0], "adamw_" + nm)])
    (gb_out, db_out, mb_out, vb_out), (gb_in, db_in, mb_in, vb_in), \
        (ga_out, da_out, ma_out, va_out), (ga_in, da_in, ma_in, va_in) = big

    small_w = [norm_gain, a_ln_gain, a_ln_bias, a_w_s, a_b_s, b_lower_bounds, final_gain, b_gn_gain]
    small_m = [m_norm_gain, m_a_ln_gain, m_a_ln_bias, m_a_w_s, m_a_b_s, m_b_lower_bounds, m_final_gain, m_b_gn_gain]
    small_v = [v_norm_gain, v_a_ln_gain, v_a_ln_bias, v_a_w_s, v_a_b_s, v_b_lower_bounds, v_final_gain, v_b_gn_gain]
    rows_of = lambda a: a.reshape(-1, a.shape[-1])
    gathered = gather_all_wait(s_small[0], s_small[1], s_small[2], ga_in, "gather_small_wait")
    tail, small_res = small_update(
        gathered, n_dmod // LANES, [rows_of(a) for a in small_w], [rows_of(a) for a in small_m],
        [rows_of(a) for a in small_v], "small_update")
    loss = tail[0, 0]
    sg, sd, sm, sv = [[small_res[p][kind].reshape(w.shape) for p, w in enumerate(small_w)] for kind in range(4)]

    dmod_all = gathered[:, :n_dmod // LANES].reshape(N_DEV * nb, n_l, 3 * d)
    dmod_cols = lax.dynamic_slice_in_dim(dmod_all, chip * ada_cols, ada_cols, axis=2)
    dmod_cols = jnp.transpose(dmod_cols, (1, 0, 2))
    g_wada, d_wada, m_wada, v_wada = ada_bwd(c_all, dmod_cols, w_ada, m_w_ada, v_w_ada, "ada_bwd")
    flat = lambda a: a.reshape(1, -1)
    g_bada, d_bada, m_bada, v_bada = [
        r.reshape(b_ada.shape) for r in
        bias_update(dmod_all.reshape(N_DEV * nb, n_l * 3 * d), flat(b_ada), flat(m_b_ada), flat(v_b_ada), "bias_update")]

    def order(ng, wada, bada, ain, sm_rest, aout, bin_, bout):
        lg, lbi, ws_, bs_, lbd, fg_, gn_ = sm_rest
        return [ng, wada, bada, ain, lg, lbi, ws_, bs_, aout, bin_, lbd, gn_, bout, fg_]

    grads = order(sg[0], g_wada, g_bada, ga_in, sg[1:8], ga_out, gb_in, gb_out)
    deltas = order(sd[0], d_wada, d_bada, da_in, sd[1:8], da_out, db_in, db_out)
    new_m = order(sm[0], m_wada, m_bada, ma_in, sm[1:8], ma_out, mb_in, mb_out)
    new_v = order(sv[0], v_wada, v_bada, va_in, sv[1:8], va_out, vb_in, vb_out)
    return (loss, grad_x, *grads, *deltas, *new_m, *new_v)
```

```python
import functools

import jax
import jax.numpy as jnp
from jax import lax
from jax.experimental import pallas as pl
from jax.experimental.pallas import tpu as pltpu

F32 = jnp.float32
BF16 = jnp.bfloat16
EPS = 1e-6
CHUNK = 64
SG_BLOCK = 128
SG_GROUPS = 8
HEAD_DIM = 128
HG_WIDE = 8
HG_ROWS = 128
N_CHIPS = 4
N_DEV = 8
LANES = 128
ADAM_LR = 0.001
ADAM_B1 = 0.9
ADAM_B2 = 0.999
ADAM_EPS = 1e-08
ADAM_WD = 0.01
ADAM_STEP = 10
GELU_C0 = 0.7978845608028654
GELU_C1 = 0.044715
MESH = pl.DeviceIdType.MESH
VMEM_LIMIT = 56 * 1024 * 1024


ROW_TILE = 1024


def _col_tile(n):
    return next(t for t in (1024, 768, 512, 256) if n % t == 0)


def _call(body, **kw):
    return pl.pallas_call(body, **kw)


def _params(**kw):
    return pltpu.CompilerParams(vmem_limit_bytes=VMEM_LIMIT, **kw)


def _sigmoid(x):
    return 0.5 * jnp.tanh(0.5 * x) + 0.5


def _sigmoid_small(x):
    return 1.0 / (1.0 + jnp.exp(-x))


def _silu_and_grad(x):
    s = _sigmoid(x)
    return x * s, s * (1.0 + x * (1.0 - s))


def _gelu(x):
    return 0.5 * x * (1.0 + jnp.tanh(GELU_C0 * (x + GELU_C1 * x * x * x)))


def _gelu_and_grad(x):
    t = jnp.tanh(GELU_C0 * (x + GELU_C1 * x * x * x))
    g = 0.5 * x * (1.0 + t)
    dg = 0.5 * (1.0 + t) + 0.5 * x * (1.0 - t * t) * (GELU_C0 * (1.0 + 3.0 * GELU_C1 * x * x))
    return g, dg


def _dot(a, b, dims, precision=None):
    return lax.dot_general(a, b, (dims, ((), ())), precision=precision, preferred_element_type=F32)


NN = ((1,), (0,))
NT = ((1,), (1,))
TN = ((0,), (0,))


def _adamw(w, g, m, v):
    m = ADAM_B1 * m + (1.0 - ADAM_B1) * g
    v = ADAM_B2 * v + (1.0 - ADAM_B2) * (g * g)
    m_hat = m / (1.0 - ADAM_B1 ** ADAM_STEP)
    v_hat = v / (1.0 - ADAM_B2 ** ADAM_STEP)
    delta = -ADAM_LR * (m_hat / (jnp.sqrt(v_hat) + ADAM_EPS) + ADAM_WD * w)
    return delta, m, v


def _chunk_mask():
    r = lax.broadcasted_iota(jnp.int32, (SG_BLOCK, SG_BLOCK), 0)
    c = lax.broadcasted_iota(jnp.int32, (SG_BLOCK, SG_BLOCK), 1)
    return (c // CHUNK) <= (r // CHUNK)


def _place():
    return lax.axis_index("x"), lax.axis_index("y"), lax.axis_index("c")


def _other_chips(x, y):
    return [(1 - x, y), (x, 1 - y), (1 - x, 1 - y)]


def allgather_small(v, name):
    m_per, n = v.shape

    def body(x_ref, out_ref, send_sems, recv_sems, local_sem):
        x, y, c = _place()
        me, sibling = (x, y, c), (x, y, 1 - c)
        chips = _other_chips(x, y)

        def rows(px, py, pc):
            return out_ref.at[pl.ds((4 * px + 2 * py + pc) * m_per, m_per), :]

        def copy(k, block, to, src=None):
            return pltpu.make_async_remote_copy(
                src_ref=rows(*block) if src is None else src, dst_ref=rows(*block),
                send_sem=send_sems.at[k], recv_sem=recv_sems.at[k], device_id=to, device_id_type=MESH)

        mine = pltpu.make_async_copy(x_ref, rows(*me), local_sem)
        mine.start()
        first = [copy(0, me, sibling, src=x_ref)]
        first += [copy(1 + j, me, (*chip, c), src=x_ref) for j, chip in enumerate(chips)]
        for cp in first:
            cp.start()
        passed = [copy(4 + j, (*chip, c), sibling) for j, chip in enumerate(chips)]
        for j, chip in enumerate(chips):
            copy(1 + j, (*chip, c), me).wait_recv()
            passed[j].start()
        copy(0, sibling, me).wait_recv()
        for j, chip in enumerate(chips):
            copy(4 + j, (*chip, 1 - c), me).wait_recv()
        for cp in first + passed:
            cp.wait_send()
        mine.wait()

    return _call(
        body, name=name,
        out_shape=jax.ShapeDtypeStruct((N_DEV * m_per, n), v.dtype),
        in_specs=[pl.BlockSpec(memory_space=pltpu.VMEM)],
        out_specs=pl.BlockSpec(memory_space=pltpu.VMEM),
        scratch_shapes=[pltpu.SemaphoreType.DMA((7,)), pltpu.SemaphoreType.DMA((7,)), pltpu.SemaphoreType.DMA],
    )(v)


def _hbm_spec():
    return pl.BlockSpec(memory_space=pltpu.HBM)


def _sem_spec():
    return pl.BlockSpec(memory_space=pltpu.SEMAPHORE)


def _split_params():
    return pltpu.CompilerParams(has_side_effects=pltpu.SideEffectType.DATAFLOW_SIDE_EFFECTING)


def _hbm(a):
    return pltpu.with_memory_space_constraint(a, pltpu.HBM)


def gather_inplace(lands, name):
    n = len(lands)

    def body(*refs):
        land_refs, token = refs[n:2 * n], refs[2 * n]
        send_sems, recv_sems = refs[2 * n + 1:]
        x, y, c = _place()
        chips = _other_chips(x, y)

        def copy(w, k, chip_idx, core_half, to):
            half = lands[w].shape[1] // 2
            rows = land_refs[w].at[chip_idx, pl.ds(core_half * half, half), :]
            return pltpu.make_async_remote_copy(
                src_ref=rows, dst_ref=rows, send_sem=send_sems.at[6 * w + k], recv_sem=recv_sems.at[6 * w + k],
                device_id=to, device_id_type=MESH)

        first = [copy(w, j, 2 * x + y, c, (px, py, c)) for w in range(n) for j, (px, py) in enumerate(chips)]
        for cp in first:
            cp.start()
        passed = []
        for w in range(n):
            for j, (px, py) in enumerate(chips):
                copy(w, j, 2 * px + py, c, (px, py, c)).wait_recv()
                passed.append(copy(w, 3 + j, 2 * px + py, c, (x, y, 1 - c)))
                passed[-1].start()
        for w in range(n):
            for j, (px, py) in enumerate(chips):
                copy(w, 3 + j, 2 * px + py, 1 - c, (x, y, 1 - c)).wait_recv()
        for cp in first + passed:
            cp.wait_send()
        token[...] = jnp.zeros_like(token)

    res = _call(
        body, name=name,
        out_shape=[jax.ShapeDtypeStruct(a.shape, a.dtype) for a in lands] + [jax.ShapeDtypeStruct((8, LANES), F32)],
        in_specs=[_hbm_spec()] * n, out_specs=[_hbm_spec()] * n + [pl.BlockSpec(memory_space=pltpu.VMEM)],
        input_output_aliases={w: w for w in range(n)},
        scratch_shapes=[pltpu.SemaphoreType.DMA((6 * n,)), pltpu.SemaphoreType.DMA((6 * n,))],
    )(*lands)
    return res[:n], res[n]


def gather_start(land, name):
    def body(land_ref, send_sems, recv_sems, land_thru, token):
        del land_thru
        x, y, c = _place()
        for j, (px, py) in enumerate(_other_chips(x, y)):
            pltpu.make_async_remote_copy(
                src_ref=land_ref.at[2 * x + y], dst_ref=land_ref.at[2 * x + y],
                send_sem=send_sems.at[j], recv_sem=recv_sems.at[j], device_id=(px, py, c),
                device_id_type=MESH).start()
        token[...] = jnp.zeros_like(token)

    return _call(
        body, name=name,
        out_shape=(pltpu.SemaphoreType.DMA((3,)), pltpu.SemaphoreType.DMA((3,)),
                   pltpu.HBM(land.shape, land.dtype), jax.ShapeDtypeStruct((8, LANES), F32)),
        in_specs=(_hbm_spec(),),
        out_specs=(_sem_spec(), _sem_spec(), _hbm_spec(), pl.BlockSpec(memory_space=pltpu.VMEM)),
        input_output_aliases={0: 2}, compiler_params=_split_params(),
    )(_hbm(land))


def gather_wait(send_sems, recv_sems, land, after, name):
    def body(land_ref, send_sems, recv_sems, after_ref, land_out):
        del after_ref, land_out
        x, y, c = _place()
        for j, (px, py) in enumerate(_other_chips(x, y)):
            cp = pltpu.make_async_remote_copy(
                src_ref=land_ref.at[2 * x + y], dst_ref=land_ref.at[2 * px + py],
                send_sem=send_sems.at[j], recv_sem=recv_sems.at[j], device_id=(px, py, c), device_id_type=MESH)
            cp.wait_send()
            cp.wait_recv()

    return _call(
        body, name=name,
        out_shape=pltpu.HBM(land.shape, land.dtype),
        in_specs=(_hbm_spec(), _sem_spec(), _sem_spec(), pl.BlockSpec(memory_space=pl.ANY)),
        out_specs=_hbm_spec(), input_output_aliases={0: 0}, compiler_params=_split_params(),
    )(land, send_sems, recv_sems, after)


def _flips():
    return [(fx, fy, fc) for fx in (0, 1) for fy in (0, 1) for fc in (0, 1) if (fx, fy, fc) != (0, 0, 0)]


def _flipped(x, y, c, flip):
    fx, fy, fc = flip
    return (1 - x if fx else x, 1 - y if fy else y, 1 - c if fc else c)


def gather_all_start(land, name):
    def body(land_ref, send_sems, recv_sems, land_thru, token):
        del land_thru
        x, y, c = _place()
        for k, flip in enumerate(_flips()):
            pltpu.make_async_remote_copy(
                src_ref=land_ref.at[4 * x + 2 * y + c], dst_ref=land_ref.at[4 * x + 2 * y + c],
                send_sem=send_sems.at[k], recv_sem=recv_sems.at[k], device_id=_flipped(x, y, c, flip),
                device_id_type=MESH).start()
        token[...] = jnp.zeros_like(token)

    return _call(
        body, name=name,
        out_shape=(pltpu.SemaphoreType.DMA((7,)), pltpu.SemaphoreType.DMA((7,)),
                   pltpu.HBM(land.shape, land.dtype), jax.ShapeDtypeStruct((8, LANES), F32)),
        in_specs=(_hbm_spec(),),
        out_specs=(_sem_spec(), _sem_spec(), _hbm_spec(), pl.BlockSpec(memory_space=pltpu.VMEM)),
        input_output_aliases={0: 2}, compiler_params=_split_params(),
    )(_hbm(land))


def gather_all_wait(send_sems, recv_sems, land, after, name):
    def body(land_ref, send_sems, recv_sems, after_ref, land_out):
        del after_ref, land_out
        x, y, c = _place()
        for k, flip in enumerate(_flips()):
            px, py, pc = _flipped(x, y, c, flip)
            cp = pltpu.make_async_remote_copy(
                src_ref=land_ref.at[4 * x + 2 * y + c], dst_ref=land_ref.at[4 * px + 2 * py + pc],
                send_sem=send_sems.at[k], recv_sem=recv_sems.at[k], device_id=(px, py, pc), device_id_type=MESH)
            cp.wait_send()
            cp.wait_recv()

    return _call(
        body, name=name,
        out_shape=pltpu.HBM(land.shape, land.dtype),
        in_specs=(_hbm_spec(), _sem_spec(), _sem_spec(), pl.BlockSpec(memory_space=pl.ANY)),
        out_specs=_hbm_spec(), input_output_aliases={0: 0}, compiler_params=_split_params(),
    )(land, send_sems, recv_sems, after)


def exchange_start(parts, name):
    _, r, c_ = parts.shape

    def body(parts_ref, land_ref, send_sems, recv_sems, parts_thru, land_thru, token):
        del parts_thru, land_thru
        x, y, c = _place()
        for j, (px, py) in enumerate(_other_chips(x, y)):
            pltpu.make_async_remote_copy(
                src_ref=parts_ref.at[2 * px + py], dst_ref=land_ref.at[j],
                send_sem=send_sems.at[j], recv_sem=recv_sems.at[j], device_id=(px, py, c),
                device_id_type=MESH).start()
        token[...] = jnp.zeros_like(token)

    return _call(
        body, name=name,
        out_shape=(pltpu.SemaphoreType.DMA((3,)), pltpu.SemaphoreType.DMA((3,)),
                   pltpu.HBM(parts.shape, parts.dtype), pltpu.HBM((3, r, c_), parts.dtype),
                   jax.ShapeDtypeStruct((8, LANES), F32)),
        in_specs=(_hbm_spec(), _hbm_spec()),
        out_specs=(_sem_spec(), _sem_spec(), _hbm_spec(), _hbm_spec(), pl.BlockSpec(memory_space=pltpu.VMEM)),
        input_output_aliases={0: 2, 1: 3}, compiler_params=_split_params(),
    )(_hbm(parts), _hbm(lax.empty((3, r, c_), parts.dtype)))


def exchange_wait(send_sems, recv_sems, parts, land, after, name):
    def body(parts_ref, land_ref, send_sems, recv_sems, after_ref, parts_out, land_out):
        del after_ref, parts_out, land_out
        x, y, c = _place()
        for j, (px, py) in enumerate(_other_chips(x, y)):
            cp = pltpu.make_async_remote_copy(
                src_ref=parts_ref.at[2 * px + py], dst_ref=land_ref.at[j],
                send_sem=send_sems.at[j], recv_sem=recv_sems.at[j], device_id=(px, py, c), device_id_type=MESH)
            cp.wait_send()
            cp.wait_recv()

    return _call(
        body, name=name,
        out_shape=(pltpu.HBM(parts.shape, parts.dtype), pltpu.HBM(land.shape, land.dtype)),
        in_specs=(_hbm_spec(), _hbm_spec(), _sem_spec(), _sem_spec(), pl.BlockSpec(memory_space=pl.ANY)),
        out_specs=(_hbm_spec(), _hbm_spec()), input_output_aliases={0: 0, 1: 1},
        compiler_params=_split_params(),
    )(parts, land, send_sems, recv_sems, after)


def cast_into_slot(w, chip, after, name):
    r, c = w.shape
    tr = min(256, r)

    def body(s_ref, w_ref, after_ref, o_ref):
        del s_ref, after_ref
        o_ref[...] = w_ref[...].astype(BF16)

    return _call(
        body, name=name,
        grid_spec=pltpu.PrefetchScalarGridSpec(
            num_scalar_prefetch=1, grid=(r // tr,),
            in_specs=[pl.BlockSpec((tr, c), lambda i, s: (i, 0)), pl.BlockSpec(memory_space=pl.ANY)],
            out_specs=pl.BlockSpec((None, tr, c), lambda i, s: (s[0], i, 0))),
        out_shape=jax.ShapeDtypeStruct((N_CHIPS, r, c), BF16),
        compiler_params=_params(),
    )(chip.reshape(1).astype(jnp.int32), w, after)


def sum_parts(parts, land, chip, name):
    _, r, c = parts.shape
    tr = min(256, r)

    def body(s_ref, p_ref, l_ref, o_ref):
        del s_ref
        acc = p_ref[...].astype(F32) + l_ref[0].astype(F32)
        acc = acc + l_ref[1].astype(F32)
        o_ref[...] = acc + l_ref[2].astype(F32)

    return _call(
        body, name=name,
        grid_spec=pltpu.PrefetchScalarGridSpec(
            num_scalar_prefetch=1, grid=(r // tr,),
            in_specs=[pl.BlockSpec((None, tr, c), lambda i, s: (s[0], i, 0)),
                      pl.BlockSpec((3, tr, c), lambda i, s: (0, i, 0))],
            out_specs=pl.BlockSpec((tr, c), lambda i, s: (i, 0))),
        out_shape=jax.ShapeDtypeStruct((r, c), F32),
        compiler_params=_params(),
    )(chip.reshape(1).astype(jnp.int32), parts, land)


def swap_sibling(arrs, name):
    n = len(arrs)

    def body(*refs):
        ins, outs = refs[:n], refs[n:2 * n]
        send_sems, recv_sems = refs[2 * n:]
        x, y, c = _place()
        cps = []
        for w in range(n):
            cp = pltpu.make_async_remote_copy(
                src_ref=ins[w], dst_ref=outs[w], send_sem=send_sems.at[w], recv_sem=recv_sems.at[w],
                device_id=(x, y, 1 - c), device_id_type=MESH)
            cp.start()
            cps.append(cp)
        for cp in cps:
            cp.wait_recv()
        for cp in cps:
            cp.wait_send()

    return _call(
        body, name=name,
        out_shape=[jax.ShapeDtypeStruct(a.shape, a.dtype) for a in arrs],
        in_specs=[_hbm_spec()] * n, out_specs=[_hbm_spec()] * n,
        scratch_shapes=[pltpu.SemaphoreType.DMA((n,)), pltpu.SemaphoreType.DMA((n,))],
    )(*arrs)


def adamw_pair(pa, pb, w, m, v, name):
    r, c = w.shape
    tr = min(128, r)

    def body(pa_ref, pb_ref, w_ref, m_ref, v_ref, g_ref, d_ref, nm_ref, nv_ref):
        g = pa_ref[...] + pb_ref[...]
        d, nm, nv = _adamw(w_ref[...], g, m_ref[...], v_ref[...])
        g_ref[...] = g
        d_ref[...] = d
        nm_ref[...] = nm
        nv_ref[...] = nv

    spec = pl.BlockSpec((tr, c), lambda i: (i, 0))
    return _call(
        body, name=name, grid=(r // tr,),
        out_shape=[jax.ShapeDtypeStruct((r, c), F32)] * 4,
        in_specs=[spec] * 5, out_specs=[spec] * 4,
        compiler_params=_params(),
    )(pa, pb, w, m, v)


def small_update(gathered, first_row, ws, ms, vs, name):
    n_w = len(ws)
    total_rows = gathered.shape[1]

    def body(*refs):
        g_ref = refs[0]
        w_refs, m_refs, v_refs = refs[1:1 + n_w], refs[1 + n_w:1 + 2 * n_w], refs[1 + 2 * n_w:1 + 3 * n_w]
        tail_ref = refs[1 + 3 * n_w]
        outs = refs[2 + 3 * n_w:2 + 7 * n_w]
        sum_ref = refs[2 + 7 * n_w]
        acc = g_ref[0]
        for k in range(1, N_DEV):
            acc = acc + g_ref[k]
        sum_ref[...] = acc
        row = first_row
        for p in range(n_w):
            a, b = ws[p].shape
            per = b // LANES
            g_out, d_out, m_out, v_out = outs[4 * p:4 * p + 4]
            if per == 1:
                g_out[...] = sum_ref[row:row + a, :]
            else:
                for i in range(a):
                    for jc in range(per):
                        g_out[i:i + 1, jc * LANES:(jc + 1) * LANES] = sum_ref[row + i * per + jc:row + i * per + jc + 1, :]
            row += a * per
            dl, nm, nv = _adamw(w_refs[p][...], g_out[...], m_refs[p][...], v_refs[p][...])
            d_out[...] = dl
            m_out[...] = nm
            v_out[...] = nv
        tail_ref[...] = sum_ref[row:row + 1, :]

    out_shape = [jax.ShapeDtypeStruct((1, LANES), F32)]
    for w in ws:
        out_shape += [jax.ShapeDtypeStruct(w.shape, F32)] * 4
    res = _call(
        body, name=name, out_shape=out_shape,
        scratch_shapes=[pltpu.VMEM((total_rows, LANES), F32)],
        compiler_params=_params(),
    )(gathered, *ws, *ms, *vs)
    return res[0], [res[1 + 4 * p:5 + 4 * p] for p in range(n_w)]


def ada_fwd(c_all, w_ada, b_cols, name):
    n_l, d, cols = w_ada.shape
    nb = c_all.shape[0]
    tn = 256

    def body(c_ref, w_ref, b_ref, o_ref):
        cv = c_ref[...]
        ca = (cv * _sigmoid(cv)).astype(BF16)
        o_ref[...] = _dot(ca, w_ref[...].astype(BF16), NN) + b_ref[...]

    return _call(
        body, name=name, grid=(n_l, cols // tn),
        out_shape=jax.ShapeDtypeStruct((n_l, nb, cols), F32),
        in_specs=[pl.BlockSpec((nb, d), lambda l, j: (0, 0)),
                  pl.BlockSpec((None, d, tn), lambda l, j: (l, 0, j)),
                  pl.BlockSpec((None, 1, tn), lambda l, j: (l, 0, j))],
        out_specs=pl.BlockSpec((None, nb, tn), lambda l, j: (l, 0, j)),
        compiler_params=_params(),
    )(c_all, w_ada, b_cols)


def ada_bwd(c_all, dmod_cols, w, m, v, name):
    n_l, d, cols = w.shape
    nb = c_all.shape[0]
    tn = 256

    def body(c_ref, dm_ref, w_ref, m_ref, v_ref, g_ref, d_ref, nm_ref, nv_ref):
        cv = c_ref[...]
        ca = (cv * _sigmoid(cv)).astype(BF16)
        g = _dot(ca, dm_ref[...].astype(BF16), TN)
        dl, nm, nv = _adamw(w_ref[...], g, m_ref[...], v_ref[...])
        g_ref[...] = g
        d_ref[...] = dl
        nm_ref[...] = nm
        nv_ref[...] = nv

    wspec = pl.BlockSpec((None, d, tn), lambda l, j: (l, 0, j))
    return _call(
        body, name=name, grid=(n_l, cols // tn),
        out_shape=[jax.ShapeDtypeStruct((n_l, d, cols), F32)] * 4,
        in_specs=[pl.BlockSpec((nb, d), lambda l, j: (0, 0)),
                  pl.BlockSpec((None, nb, tn), lambda l, j: (l, 0, j)),
                  wspec, wspec, wspec],
        out_specs=[wspec] * 4,
        compiler_params=_params(),
    )(c_all, dmod_cols, w, m, v)


def bias_update(dmod_all, w, m, v, name):
    def body(dm_ref, w_ref, m_ref, v_ref, g_ref, d_ref, nm_ref, nv_ref):
        g = jnp.sum(dm_ref[...], axis=0, keepdims=True)
        dl, nm, nv = _adamw(w_ref[...], g, m_ref[...], v_ref[...])
        g_ref[...] = g
        d_ref[...] = dl
        nm_ref[...] = nm
        nv_ref[...] = nv

    return _call(
        body, name=name,
        out_shape=[jax.ShapeDtypeStruct(w.shape, F32)] * 4,
        compiler_params=_params(),
    )(dmod_all, w, m, v)


def inproj_fwd(x, mod, ng, wg, seq, sectioned, name):
    m_rows, d = x.shape
    nsh, _, ns = wg.shape
    n = nsh * ns
    tm, tn = min(ROW_TILE, seq), ns
    per = ns // tn

    def body(x_ref, mod_ref, ng_ref, w_ref, proj_ref, h_ref):
        @pl.when(pl.program_id(1) == 0)
        def _():
            xv = x_ref[...]
            r = lax.rsqrt(jnp.mean(xv * xv, axis=-1, keepdims=True) + EPS)
            md = mod_ref[0]
            h = (xv * r * ng_ref[...]) * (1.0 + md[:, d:2 * d]) + md[:, :d]
            h_ref[...] = h.astype(BF16)
        proj_ref[...] = _dot(h_ref[...], w_ref[...], NN)

    if sectioned:
        proj_shape = (nsh, m_rows, ns)
        proj_spec = pl.BlockSpec((None, tm, tn), lambda i, j: (j // per, i, j % per))
    else:
        proj_shape = (m_rows, n)
        proj_spec = pl.BlockSpec((tm, tn), lambda i, j: (i, j))
    return _call(
        body, name=name, grid=(m_rows // tm, n // tn),
        out_shape=[jax.ShapeDtypeStruct(proj_shape, F32), jax.ShapeDtypeStruct((m_rows, d), BF16)],
        in_specs=[pl.BlockSpec((tm, d), lambda i, j: (i, 0)),
                  pl.BlockSpec((1, 1, 3 * d), lambda i, j: ((i * tm) // seq, 0, 0)),
                  pl.BlockSpec((1, d), lambda i, j: (0, 0)),
                  pl.BlockSpec((None, d, tn), lambda i, j: (j // per, 0, j % per))],
        out_specs=[proj_spec, pl.BlockSpec((tm, d), lambda i, j: (i, 0))],
        compiler_params=_params(),
    )(x, mod, ng, wg)


def outproj_fwd(y, w, x, mod, seq, name):
    m_rows, di = y.shape
    d = w.shape[1]
    tm = min(ROW_TILE, seq)

    def body(y_ref, w_ref, x_ref, mod_ref, xn_ref, out_ref):
        acc = _dot(y_ref[...], w_ref[...], NN)
        out_ref[...] = acc
        xn_ref[...] = x_ref[...] + mod_ref[0][:, 2 * d:] * acc

    row = pl.BlockSpec((tm, d), lambda i: (i, 0))
    return _call(
        body, name=name, grid=(m_rows // tm,),
        out_shape=[jax.ShapeDtypeStruct((m_rows, d), F32)] * 2,
        in_specs=[pl.BlockSpec((tm, di), lambda i: (i, 0)),
                  pl.BlockSpec((di, d), lambda i: (0, 0)),
                  row,
                  pl.BlockSpec((1, 1, 3 * d), lambda i: ((i * tm) // seq, 0, 0))],
        out_specs=[row, row],
        compiler_params=_params(),
    )(y, w, x, mod)


def outproj_bwd(dxo, out, mod, w, seq, name):
    m_rows, d = dxo.shape
    di = w.shape[0]
    nb = m_rows // seq
    tm, tn = min(ROW_TILE, seq), _col_tile(di)

    def body(dxo_ref, out_ref, mod_ref, w_ref, dy_ref, dout_ref, dgate_ref):
        i = pl.program_id(0)

        @pl.when(pl.program_id(1) == 0)
        def _():
            dx = dxo_ref[...]
            dout_ref[...] = (mod_ref[0][:, 2 * d:] * dx).astype(BF16)
            part = jnp.sum(dx * out_ref[...], axis=0, keepdims=True)

            @pl.when((i * tm) % seq == 0)
            def _():
                dgate_ref[0] = part

            @pl.when((i * tm) % seq != 0)
            def _():
                dgate_ref[0] = dgate_ref[0] + part

        dy_ref[...] = _dot(dout_ref[...], w_ref[...], NT)

    row = pl.BlockSpec((tm, d), lambda i, j: (i, 0))
    return _call(
        body, name=name, grid=(m_rows // tm, di // tn),
        out_shape=[jax.ShapeDtypeStruct((m_rows, di), F32), jax.ShapeDtypeStruct((m_rows, d), BF16),
                   jax.ShapeDtypeStruct((nb, 1, d), F32)],
        in_specs=[row, row,
                  pl.BlockSpec((1, 1, 3 * d), lambda i, j: ((i * tm) // seq, 0, 0)),
                  pl.BlockSpec((tn, d), lambda i, j: (j, 0))],
        out_specs=[pl.BlockSpec((tm, tn), lambda i, j: (i, j)), row,
                   pl.BlockSpec((1, 1, d), lambda i, j: ((i * tm) // seq, 0, 0))],
        compiler_params=_params(),
    )(dxo, out, mod, w)


def grad_w_out(y, dout, name):
    m_rows, di = y.shape
    d = dout.shape[1]
    tm, tk = min(ROW_TILE, m_rows), _col_tile(di)
    n_m = m_rows // tm

    def body(y_ref, do_ref, o_ref, acc_ref):
        mi = pl.program_id(1)
        @pl.when(mi == 0)
        def _():
            acc_ref[...] = jnp.zeros_like(acc_ref)

        acc_ref[...] += _dot(y_ref[...], do_ref[...], TN)

        @pl.when(mi == n_m - 1)
        def _():
            o_ref[...] = acc_ref[...].astype(BF16)

    return _call(
        body, name=name, grid=(di // tk, n_m),
        out_shape=jax.ShapeDtypeStruct((di, d), BF16),
        in_specs=[pl.BlockSpec((tm, tk), lambda j, mi: (mi, j)),
                  pl.BlockSpec((tm, d), lambda j, mi: (mi, 0))],
        out_specs=pl.BlockSpec((tk, d), lambda j, mi: (j, 0)),
        scratch_shapes=[pltpu.VMEM((tk, d), F32)],
        compiler_params=_params(),
    )(y, dout)


def grad_w_in(h, dproj, nsh, sectioned, name):
    m_rows, d = h.shape
    n = dproj.shape[0] * dproj.shape[2] if sectioned else dproj.shape[1]
    ns = n // nsh
    tm, tn = min(ROW_TILE, m_rows), _col_tile(ns)
    per = ns // tn
    n_m = m_rows // tm

    def body(h_ref, dp_ref, o_ref, acc_ref):
        mi = pl.program_id(1)
        @pl.when(mi == 0)
        def _():
            acc_ref[...] = jnp.zeros_like(acc_ref)

        acc_ref[...] += _dot(h_ref[...], dp_ref[...], TN)

        @pl.when(mi == n_m - 1)
        def _():
            o_ref[...] = acc_ref[...].astype(BF16)

    if sectioned:
        dp_spec = pl.BlockSpec((None, tm, tn), lambda j, mi: (j // per, mi, j % per))
    else:
        dp_spec = pl.BlockSpec((tm, tn), lambda j, mi: (mi, j))
    return _call(
        body, name=name, grid=(n // tn, n_m),
        out_shape=jax.ShapeDtypeStruct((nsh, d, ns), BF16),
        in_specs=[pl.BlockSpec((tm, d), lambda j, mi: (mi, 0)), dp_spec],
        out_specs=pl.BlockSpec((None, d, tn), lambda j, mi: (j // per, 0, j % per)),
        scratch_shapes=[pltpu.VMEM((d, tn), F32)],
        compiler_params=_params(),
    )(h, dproj)


def inproj_bwd(dproj, wg, x, dxo, mod, ng, seq, sectioned, name):
    m_rows, d = x.shape
    nsh, _, ns = wg.shape
    n = nsh * ns
    nb = m_rows // seq
    tm, tk = min(ROW_TILE, seq), _col_tile(ns)
    per = ns // tk
    n_k = n // tk

    def body(dp_ref, w_ref, x_ref, dxo_ref, mod_ref, ng_ref, dxi_ref, dsh_ref, dsc_ref, dng_ref, acc_ref):
        i, k = pl.program_id(0), pl.program_id(1)
        @pl.when(k == 0)
        def _():
            acc_ref[...] = jnp.zeros_like(acc_ref)

        acc_ref[...] += _dot(dp_ref[...], w_ref[...], NT)

        @pl.when(k == n_k - 1)
        def _():
            dh = acc_ref[...]
            xv = x_ref[...]
            r = lax.rsqrt(jnp.mean(xv * xv, axis=-1, keepdims=True) + EPS)
            xn = xv * r
            md = mod_ref[0]
            gain = ng_ref[...]
            p_shift = jnp.sum(dh, axis=0, keepdims=True)
            p_scale = jnp.sum(dh * (xn * gain), axis=0, keepdims=True)
            drn = dh * (1.0 + md[:, d:2 * d])
            p_ng = jnp.sum(drn * xn, axis=0, keepdims=True)
            dxn = drn * gain
            dx = r * (dxn - xn * jnp.mean(dxn * xn, axis=-1, keepdims=True))
            dxi_ref[...] = dxo_ref[...] + dx

            @pl.when((i * tm) % seq == 0)
            def _():
                dsh_ref[0] = p_shift
                dsc_ref[0] = p_scale

            @pl.when((i * tm) % seq != 0)
            def _():
                dsh_ref[0] = dsh_ref[0] + p_shift
                dsc_ref[0] = dsc_ref[0] + p_scale

            @pl.when(i == 0)
            def _():
                dng_ref[...] = p_ng

            @pl.when(i != 0)
            def _():
                dng_ref[...] = dng_ref[...] + p_ng

    if sectioned:
        dp_spec = pl.BlockSpec((None, tm, tk), lambda i, k: (k // per, i, k % per))
    else:
        dp_spec = pl.BlockSpec((tm, tk), lambda i, k: (i, k))
    row = pl.BlockSpec((tm, d), lambda i, k: (i, 0))
    per_seq = pl.BlockSpec((1, 1, d), lambda i, k: ((i * tm) // seq, 0, 0))
    return _call(
        body, name=name, grid=(m_rows // tm, n_k),
        out_shape=[jax.ShapeDtypeStruct((m_rows, d), F32), jax.ShapeDtypeStruct((nb, 1, d), F32),
                   jax.ShapeDtypeStruct((nb, 1, d), F32), jax.ShapeDtypeStruct((1, d), F32)],
        in_specs=[dp_spec,
                  pl.BlockSpec((None, d, tk), lambda i, k: (k // per, 0, k % per)),
                  row, row,
                  pl.BlockSpec((1, 1, 3 * d), lambda i, k: ((i * tm) // seq, 0, 0)),
                  pl.BlockSpec((1, d), lambda i, k: (0, 0))],
        out_specs=[row, per_seq, per_seq, pl.BlockSpec((1, d), lambda i, k: (0, 0))],
        scratch_shapes=[pltpu.VMEM((tm, d), F32)],
        compiler_params=_params(),
    )(dproj, wg, x, dxo, mod, ng)


def _sgu_stats(proj_ref, vg_ref, di, gd):
    s1 = jnp.zeros((SG_BLOCK, 1), F32)
    for g in range(SG_GROUPS):
        vg = _gelu(proj_ref[:, di + g * gd:di + (g + 1) * gd])
        vg_ref[:, g * gd:(g + 1) * gd] = vg
        s1 = s1 + jnp.sum(vg, axis=1, keepdims=True)
    mu = s1 / di
    s2 = jnp.zeros((SG_BLOCK, 1), F32)
    for g in range(SG_GROUPS):
        dv = vg_ref[:, g * gd:(g + 1) * gd] - mu
        s2 = s2 + jnp.sum(dv * dv, axis=1, keepdims=True)
    return mu, lax.rsqrt(s2 / di + EPS)


def sgu_fwd(proj, ln_gain, ln_bias, ws, bs, name):
    m_rows, n3 = proj.shape
    di = n3 // 3
    gd = di // SG_GROUPS

    def body(proj_ref, lg_ref, lb_ref, ws_ref, bs_ref, y_ref, wsm_ref, vg_ref):
        @pl.when(pl.program_id(0) == 0)
        def _():
            mask = _chunk_mask()
            for g in range(SG_GROUPS):
                wsm_ref[g] = jnp.where(mask, ws_ref[g], 0.0).astype(BF16)

        mu, rstd = _sgu_stats(proj_ref, vg_ref, di, gd)
        for g in range(SG_GROUPS):
            cs = slice(g * gd, (g + 1) * gd)
            vln = (vg_ref[:, cs] - mu) * rstd * lg_ref[:, cs] + lb_ref[:, cs]
            s = _dot(wsm_ref[g], vln.astype(BF16), NN) + bs_ref[g]
            u = _gelu(proj_ref[:, cs])
            gp = proj_ref[:, 2 * di + g * gd:2 * di + (g + 1) * gd]
            y_ref[:, cs] = (u * s * (gp * _sigmoid(gp))).astype(BF16)

    full = lambda shape: pl.BlockSpec(shape, lambda i: (0,) * len(shape))
    return _call(
        body, name=name, grid=(m_rows // SG_BLOCK,),
        out_shape=jax.ShapeDtypeStruct((m_rows, di), BF16),
        in_specs=[pl.BlockSpec((SG_BLOCK, n3), lambda i: (i, 0)),
                  full((1, di)), full((1, di)),
                  full((SG_GROUPS, SG_BLOCK, SG_BLOCK)), full((SG_GROUPS, SG_BLOCK, 1))],
        out_specs=pl.BlockSpec((SG_BLOCK, di), lambda i: (i, 0)),
        scratch_shapes=[pltpu.VMEM((SG_GROUPS, SG_BLOCK, SG_BLOCK), BF16), pltpu.VMEM((SG_BLOCK, di), F32)],
        compiler_params=_params(),
    )(proj, ln_gain, ln_bias, ws, bs)


def sgu_bwd(proj, dy, ln_gain, ln_bias, ws, bs, name):
    m_rows, n3 = proj.shape
    di = n3 // 3
    gd = di // SG_GROUPS
    n_i = m_rows // SG_BLOCK

    def body(proj_ref, dy_ref, lg_ref, lb_ref, ws_ref, bs_ref,
             dp_ref, dws_ref, dbs_ref, dlg_ref, dlb_ref, wsm_ref, vg_ref, dvh_ref):
        i = pl.program_id(0)

        @pl.when(i == 0)
        def _():
            mask = _chunk_mask()
            for g in range(SG_GROUPS):
                wsm_ref[g] = jnp.where(mask, ws_ref[g], 0.0).astype(BF16)
            dws_ref[...] = jnp.zeros_like(dws_ref)
            dbs_ref[...] = jnp.zeros_like(dbs_ref)
            dlg_ref[...] = jnp.zeros_like(dlg_ref)
            dlb_ref[...] = jnp.zeros_like(dlb_ref)

        mu, rstd = _sgu_stats(proj_ref, vg_ref, di, gd)
        m1 = jnp.zeros((SG_BLOCK, 1), F32)
        m2 = jnp.zeros((SG_BLOCK, 1), F32)
        for g in range(SG_GROUPS):
            cs = slice(g * gd, (g + 1) * gd)
            gs = slice(2 * di + g * gd, 2 * di + (g + 1) * gd)
            gain = lg_ref[:, cs]
            vhat = (vg_ref[:, cs] - mu) * rstd
            vln_b = (vhat * gain + lb_ref[:, cs]).astype(BF16)
            s = _dot(wsm_ref[g], vln_b, NN) + bs_ref[g]
            u, du = _gelu_and_grad(proj_ref[:, cs])
            sg, dsg = _silu_and_grad(proj_ref[:, gs])
            dyv = dy_ref[:, cs]
            dp_ref[:, cs] = (dyv * s * sg * du).astype(BF16)
            dp_ref[:, gs] = (dyv * u * s * dsg).astype(BF16)
            ds = dyv * u * sg
            ds_b = ds.astype(BF16)
            dws_ref[g] = dws_ref[g] + _dot(ds_b, vln_b, NT)
            dbs_ref[g] = dbs_ref[g] + jnp.sum(ds, axis=1, keepdims=True)
            dvln = _dot(wsm_ref[g], ds_b, TN)
            dlg_ref[:, cs] = dlg_ref[:, cs] + jnp.sum(dvln * vhat, axis=0, keepdims=True)
            dlb_ref[:, cs] = dlb_ref[:, cs] + jnp.sum(dvln, axis=0, keepdims=True)
            dvh = dvln * gain
            dvh_ref[:, cs] = dvh
            m1 = m1 + jnp.sum(dvh, axis=1, keepdims=True)
            m2 = m2 + jnp.sum(dvh * vhat, axis=1, keepdims=True)
        m1 = m1 / di
        m2 = m2 / di
        for g in range(SG_GROUPS):
            cs = slice(g * gd, (g + 1) * gd)
            vs = slice(di + g * gd, di + (g + 1) * gd)
            vhat = (vg_ref[:, cs] - mu) * rstd
            dvg = rstd * (dvh_ref[:, cs] - m1 - vhat * m2)
            _, dgel = _gelu_and_grad(proj_ref[:, vs])
            dp_ref[:, vs] = (dvg * dgel).astype(BF16)

        @pl.when(i == n_i - 1)
        def _():
            mask = _chunk_mask()
            for g in range(SG_GROUPS):
                dws_ref[g] = jnp.where(mask, dws_ref[g], 0.0)

    full = lambda shape: pl.BlockSpec(shape, lambda i: (0,) * len(shape))
    return _call(
        body, name=name, grid=(n_i,),
        out_shape=[jax.ShapeDtypeStruct((m_rows, n3), BF16),
                   jax.ShapeDtypeStruct((SG_GROUPS, SG_BLOCK, SG_BLOCK), F32),
                   jax.ShapeDtypeStruct((SG_GROUPS, SG_BLOCK, 1), F32),
                   jax.ShapeDtypeStruct((1, di), F32), jax.ShapeDtypeStruct((1, di), F32)],
        in_specs=[pl.BlockSpec((SG_BLOCK, n3), lambda i: (i, 0)),
                  pl.BlockSpec((SG_BLOCK, di), lambda i: (i, 0)),
                  full((1, di)), full((1, di)),
                  full((SG_GROUPS, SG_BLOCK, SG_BLOCK)), full((SG_GROUPS, SG_BLOCK, 1))],
        out_specs=[pl.BlockSpec((SG_BLOCK, n3), lambda i: (i, 0)),
                   full((SG_GROUPS, SG_BLOCK, SG_BLOCK)), full((SG_GROUPS, SG_BLOCK, 1)),
                   full((1, di)), full((1, di))],
        scratch_shapes=[pltpu.VMEM((SG_GROUPS, SG_BLOCK, SG_BLOCK), BF16),
                        pltpu.VMEM((SG_BLOCK, di), F32), pltpu.VMEM((SG_BLOCK, di), F32)],
        compiler_params=_params(),
    )(proj, dy, ln_gain, ln_bias, ws, bs)


def _lower_bound(lbraw):
    mx = jnp.maximum(lbraw[0:1, :], lbraw[1:2, :])
    e0 = jnp.exp(lbraw[0:1, :] - mx)
    e1 = jnp.exp(lbraw[1:2, :] - mx)
    p0 = e0 / (e0 + e1)
    p1 = e1 / (e0 + e1)
    return (p0 + p1) - p0, p0, p1


def _tri(lower):
    r = lax.broadcasted_iota(jnp.int32, (CHUNK, CHUNK), 0)
    c = lax.broadcasted_iota(jnp.int32, (CHUNK, CHUNK), 1)
    return ((r >= c) if lower else (c >= r)).astype(F32)


def _row(a, idx):
    r = lax.broadcasted_iota(jnp.int32, a.shape, 0)
    return jnp.sum(jnp.where(r == idx, a, 0.0), axis=0, keepdims=True)


def _hgrn_gates(qp, fp, lb, tri):
    sgm = _sigmoid_small(fp)
    f = lb + (1.0 - lb) * sgm
    k = 1.0 - f
    a = _dot(tri, jnp.log(f), NN, precision=lax.Precision.HIGHEST)
    a_mid = _row(a, CHUNK // 2 - 1)
    a_last = _row(a, CHUNK - 1)
    q, dq = _silu_and_grad(qp)
    e1, e2, e3, e4 = jnp.exp(a - a_mid), jnp.exp(a_mid - a), jnp.exp(a), jnp.exp(a_last - a)
    return dict(sgm=sgm, f=f, k=k, q=q, dq=dq, e1=e1, e2=e2, e3=e3, e4=e4, dec=jnp.exp(a_last),
                q_in=q * e1, k_in=k * e2, q_out=q * e3, k_out=k * e4)


def _causal():
    r = lax.broadcasted_iota(jnp.int32, (CHUNK, CHUNK), 0)
    c = lax.broadcasted_iota(jnp.int32, (CHUNK, CHUNK), 1)
    return r >= c


def hgrn_fwd(proj4, lbraw, gn, seq, name):
    _, m_rows, di = proj4.shape
    nb, nh, nc = m_rows // seq, di // HEAD_DIM, seq // CHUNK
    rows = min(HG_ROWS, seq)
    wide = HG_WIDE * HEAD_DIM
    ns, cpb = seq // rows, rows // CHUNK

    def body(p_ref, lb_ref, gn_ref, y_ref, sts_ref, st_ref):
        @pl.when(pl.program_id(2) == 0)
        def _():
            st_ref[...] = jnp.zeros_like(st_ref)

        tri = _tri(True)
        causal = _causal()
        gain = gn_ref[...]
        lbs = [_lower_bound(lb_ref[:, j * HEAD_DIM:(j + 1) * HEAD_DIM])[0] for j in range(HG_WIDE)]

        units = [(n, j) for n in range(cpb) for j in range(HG_WIDE)]
        rs = lambda n: slice(n * CHUNK, (n + 1) * CHUNK)
        cs = lambda j: slice(j * HEAD_DIM, (j + 1) * HEAD_DIM)
        gates, v_b, sc_b, kv, o_in, o_x = {}, {}, {}, {}, {}, {}
        for n, j in units:
            gates[n, j] = _hgrn_gates(p_ref[0, rs(n), cs(j)], p_ref[1, rs(n), cs(j)], lbs[j], tri)
            v_b[n, j] = p_ref[2, rs(n), cs(j)].astype(BF16)
        for u in units:
            t = gates[u]
            sc_b[u] = jnp.where(causal, _dot(t["q_in"].astype(BF16), t["k_in"].astype(BF16), NT), 0.0).astype(BF16)
            kv[u] = _dot(v_b[u], t["k_out"].astype(BF16), TN)
        for u in units:
            o_in[u] = _dot(sc_b[u], v_b[u], NN)
        for j in range(HG_WIDE):
            st = st_ref[j]
            for n in range(cpb):
                sts_ref[n, :, cs(j)] = st
                o_x[n, j] = _dot(gates[n, j]["q_out"].astype(BF16), st.astype(BF16), NT)
                st = st * gates[n, j]["dec"] + kv[n, j]
            st_ref[j] = st
        for n, j in units:
            o = o_in[n, j] + o_x[n, j]
            r = lax.rsqrt(jnp.mean(o * o, axis=-1, keepdims=True) + EPS)
            gp = p_ref[3, rs(n), cs(j)]
            y_ref[rs(n), cs(j)] = ((o * r * gain) * (gp * _sigmoid(gp))).astype(BF16)

    return _call(
        body, name=name, grid=(nh // HG_WIDE, nb, ns),
        out_shape=[jax.ShapeDtypeStruct((m_rows, di), BF16),
                   jax.ShapeDtypeStruct((nb * nc, HEAD_DIM, di), F32)],
        in_specs=[pl.BlockSpec((4, rows, wide), lambda hg, b, s: (0, b * ns + s, hg)),
                  pl.BlockSpec((2, wide), lambda hg, b, s: (0, hg)),
                  pl.BlockSpec((1, HEAD_DIM), lambda hg, b, s: (0, 0))],
        out_specs=[pl.BlockSpec((rows, wide), lambda hg, b, s: (b * ns + s, hg)),
                   pl.BlockSpec((cpb, HEAD_DIM, wide), lambda hg, b, s: (b * ns + s, 0, hg))],
        scratch_shapes=[pltpu.VMEM((HG_WIDE, HEAD_DIM, HEAD_DIM), F32)],
        compiler_params=_params(),
    )(proj4, lbraw, gn)


def hgrn_bwd(proj4, dy, sts, lbraw, gn, seq, name):
    _, m_rows, di = proj4.shape
    nb, nh, nc = m_rows // seq, di // HEAD_DIM, seq // CHUNK
    rows = min(HG_ROWS, seq)
    wide = HG_WIDE * HEAD_DIM
    ns, cpb = seq // rows, rows // CHUNK
    n_hg = nh // HG_WIDE

    def body(p_ref, dy_ref, sts_ref, lb_ref, gn_ref, dp_ref, dlb_ref, dgn_ref, dst_ref, lbacc_ref, gnacc_ref):
        hg, b, s = pl.program_id(0), pl.program_id(1), pl.program_id(2)
        tri, triu = _tri(True), _tri(False)
        causal = _causal()
        gain = gn_ref[...]
        first = (b == 0) & (s == 0)

        @pl.when((hg == 0) & first)
        def _():
            gnacc_ref[...] = jnp.zeros_like(gnacc_ref)

        @pl.when(first)
        def _():
            lbacc_ref[...] = jnp.zeros_like(lbacc_ref)

        @pl.when(s == 0)
        def _():
            dst_ref[...] = jnp.zeros_like(dst_ref)

        units = [(n, j) for n in range(cpb) for j in range(HG_WIDE)]
        rs = lambda n: slice(n * CHUNK, (n + 1) * CHUNK)
        cs = lambda j: slice(j * HEAD_DIM, (j + 1) * HEAD_DIM)
        lbs = [_lower_bound(lb_ref[:, cs(j)])[0] for j in range(HG_WIDE)]
        gates, v_b, st_b, sc_b, o, do_b = {}, {}, {}, {}, {}, {}
        dq_out, dsc_b, dv, g_st, dq_in, dk_in, dst_at, dk_out, ddec = {}, {}, {}, {}, {}, {}, {}, {}, {}
        for n, j in units:
            gates[n, j] = _hgrn_gates(p_ref[0, rs(n), cs(j)], p_ref[1, rs(n), cs(j)], lbs[j], tri)
            v_b[n, j] = p_ref[2, rs(n), cs(j)].astype(BF16)
            st_b[n, j] = sts_ref[n, :, cs(j)].astype(BF16)
        for u in units:
            t = gates[u]
            sc_b[u] = jnp.where(causal, _dot(t["q_in"].astype(BF16), t["k_in"].astype(BF16), NT), 0.0).astype(BF16)
        for u in units:
            o[u] = _dot(sc_b[u], v_b[u], NN) + _dot(gates[u]["q_out"].astype(BF16), st_b[u], NT)
        for n, j in units:
            ov = o[n, j]
            r = lax.rsqrt(jnp.mean(ov * ov, axis=-1, keepdims=True) + EPS)
            ohat = ov * r
            sg, dsg = _silu_and_grad(p_ref[3, rs(n), cs(j)])
            dyv = dy_ref[rs(n), cs(j)]
            dp_ref[3, rs(n), cs(j)] = (dyv * (ohat * gain) * dsg).astype(BF16)
            d_on = dyv * sg
            gnacc_ref[:, cs(j)] = gnacc_ref[:, cs(j)] + jnp.sum(d_on * ohat, axis=0, keepdims=True)
            dohat = d_on * gain
            do_b[n, j] = (r * (dohat - ohat * jnp.mean(dohat * ohat, axis=-1, keepdims=True))).astype(BF16)
        for u in units:
            dq_out[u] = _dot(do_b[u], st_b[u], NN)
            dsc_b[u] = jnp.where(causal, _dot(do_b[u], v_b[u], NT), 0.0).astype(BF16)
            dv[u] = _dot(sc_b[u], do_b[u], TN)
            g_st[u] = _dot(do_b[u], gates[u]["q_out"].astype(BF16), TN)
        for u in units:
            dq_in[u] = _dot(dsc_b[u], gates[u]["k_in"].astype(BF16), NN)
            dk_in[u] = _dot(dsc_b[u], gates[u]["q_in"].astype(BF16), TN)
        for j in range(HG_WIDE):
            dst = dst_ref[j]
            for n in reversed(range(cpb)):
                dst_at[n, j] = dst
                dst = dst * gates[n, j]["dec"] + g_st[n, j]
            dst_ref[j] = dst
        for n, j in units:
            dst = dst_at[n, j]
            dst_b = dst.astype(BF16)
            dk_out[n, j] = _dot(v_b[n, j], dst_b, NN)
            dv[n, j] = dv[n, j] + _dot(gates[n, j]["k_out"].astype(BF16), dst_b, NT)
            ddec[n, j] = jnp.sum(dst * sts_ref[n, :, cs(j)], axis=0, keepdims=True)
        for n, j in units:
            t = gates[n, j]
            dp_ref[2, rs(n), cs(j)] = dv[n, j].astype(BF16)
            dq = dq_in[n, j] * t["e1"] + dq_out[n, j] * t["e3"]
            dk = dk_in[n, j] * t["e2"] + dk_out[n, j] * t["e4"]
            w_in = dq_in[n, j] * t["q_in"] - dk_in[n, j] * t["k_in"]
            w_out = dk_out[n, j] * t["k_out"]
            da = w_in + dq_out[n, j] * t["q_out"] - w_out
            da_mid = -jnp.sum(w_in, axis=0, keepdims=True)
            da_last = jnp.sum(w_out, axis=0, keepdims=True) + ddec[n, j] * t["dec"]
            rid = lax.broadcasted_iota(jnp.int32, da.shape, 0)
            da = da + jnp.where(rid == CHUNK // 2 - 1, da_mid, 0.0) + jnp.where(rid == CHUNK - 1, da_last, 0.0)
            dlf = _dot(triu, da, NN, precision=lax.Precision.HIGHEST)
            df = dlf / t["f"] - dk
            sgm = t["sgm"]
            dp_ref[1, rs(n), cs(j)] = (df * (1.0 - lbs[j]) * sgm * (1.0 - sgm)).astype(BF16)
            lbacc_ref[:, cs(j)] = lbacc_ref[:, cs(j)] + jnp.sum(df * (1.0 - sgm), axis=0, keepdims=True)
            dp_ref[0, rs(n), cs(j)] = (dq * t["dq"]).astype(BF16)

        @pl.when((b == nb - 1) & (s == ns - 1))
        def _():
            for j in range(HG_WIDE):
                cs = slice(j * HEAD_DIM, (j + 1) * HEAD_DIM)
                _, p0, p1 = _lower_bound(lb_ref[:, cs])
                acc = lbacc_ref[:, cs]
                dlb_ref[0:1, cs] = -acc * p0 * p1
                dlb_ref[1:2, cs] = acc * p1 * (1.0 - p1)

        @pl.when((hg == n_hg - 1) & (b == nb - 1) & (s == ns - 1))
        def _():
            tot = gnacc_ref[:, 0:HEAD_DIM]
            for j in range(1, HG_WIDE):
                tot = tot + gnacc_ref[:, j * HEAD_DIM:(j + 1) * HEAD_DIM]
            dgn_ref[...] = tot

    blk = lambda hg, b, s: b * ns + (ns - 1 - s)
    return _call(
        body, name=name, grid=(n_hg, nb, ns),
        out_shape=[jax.ShapeDtypeStruct((4, m_rows, di), BF16), jax.ShapeDtypeStruct((2, di), F32),
                   jax.ShapeDtypeStruct((1, HEAD_DIM), F32)],
        in_specs=[pl.BlockSpec((4, rows, wide), lambda hg, b, s: (0, blk(hg, b, s), hg)),
                  pl.BlockSpec((rows, wide), lambda hg, b, s: (blk(hg, b, s), hg)),
                  pl.BlockSpec((cpb, HEAD_DIM, wide), lambda hg, b, s: (blk(hg, b, s), 0, hg)),
                  pl.BlockSpec((2, wide), lambda hg, b, s: (0, hg)),
                  pl.BlockSpec((1, HEAD_DIM), lambda hg, b, s: (0, 0))],
        out_specs=[pl.BlockSpec((4, rows, wide), lambda hg, b, s: (0, blk(hg, b, s), hg)),
                   pl.BlockSpec((2, wide), lambda hg, b, s: (0, hg)),
                   pl.BlockSpec((1, HEAD_DIM), lambda hg, b, s: (0, 0))],
        scratch_shapes=[pltpu.VMEM((HG_WIDE, HEAD_DIM, HEAD_DIM), F32), pltpu.VMEM((1, wide), F32),
                        pltpu.VMEM((1, wide), F32)],
        compiler_params=_params(),
    )(proj4, dy, sts, lbraw, gn)


def final_loss(x, fg, target, name):
    m_rows, d = x.shape
    tm = min(512, m_rows)

    def body(x_ref, fg_ref, t_ref, loss_ref, dx_ref, dfg_ref):
        i = pl.program_id(0)
        xv = x_ref[...]
        gain = fg_ref[...]
        r = lax.rsqrt(jnp.mean(xv * xv, axis=-1, keepdims=True) + EPS)
        xn = xv * r
        e = xn * gain - t_ref[...]
        part = 0.5 * jnp.sum(jnp.mean(e * e, axis=-1, keepdims=True), axis=0, keepdims=True)
        dyv = e / d
        p_fg = jnp.sum(dyv * xn, axis=0, keepdims=True)
        dxn = dyv * gain
        dx_ref[...] = r * (dxn - xn * jnp.mean(dxn * xn, axis=-1, keepdims=True))

        @pl.when(i == 0)
        def _():
            loss_ref[...] = part
            dfg_ref[...] = p_fg

        @pl.when(i != 0)
        def _():
            loss_ref[...] = loss_ref[...] + part
            dfg_ref[...] = dfg_ref[...] + p_fg

    row = pl.BlockSpec((tm, d), lambda i: (i, 0))
    return _call(
        body, name=name, grid=(m_rows // tm,),
        out_shape=[jax.ShapeDtypeStruct((1, 1), F32), jax.ShapeDtypeStruct((m_rows, d), F32),
                   jax.ShapeDtypeStruct((1, d), F32)],
        in_specs=[row, pl.BlockSpec((1, d), lambda i: (0, 0)), row],
        out_specs=[pl.BlockSpec((1, 1), lambda i: (0, 0)), row, pl.BlockSpec((1, d), lambda i: (0, 0))],
        compiler_params=_params(),
    )(x, fg, target)


def _pack(parts):
    flat = jnp.concatenate([p.reshape(-1) for p in parts])
    pad = (-flat.shape[0]) % (8 * LANES)
    return jnp.pad(flat, (0, pad)).reshape(-1, LANES)


def kernel(x, c, norm_gain, w_ada, b_ada, a_w_in, a_ln_gain, a_ln_bias, a_w_s, a_b_s, a_w_out, b_w_in, b_lower_bounds, b_gn_gain, b_w_out, final_gain, loss_target, m_norm_gain, m_w_ada, m_b_ada, m_a_w_in, m_a_ln_gain, m_a_ln_bias, m_a_w_s, m_a_b_s, m_a_w_out, m_b_w_in, m_b_lower_bounds, m_b_gn_gain, m_b_w_out, m_final_gain, v_norm_gain, v_w_ada, v_b_ada, v_a_w_in, v_a_ln_gain, v_a_ln_bias, v_a_w_s, v_a_b_s, v_a_w_out, v_b_w_in, v_b_lower_bounds, v_b_gn_gain, v_b_w_out, v_final_gain):
    nb, seq, d = x.shape
    m_rows = nb * seq
    n_l = w_ada.shape[0]
    ada_cols = w_ada.shape[2]
    px, py, pc = _place()
    chip = 2 * px + py
    dev = 2 * chip + pc

    c_all = allgather_small(c.reshape(-1, LANES), "gather_c").reshape(N_DEV * nb, d)
    b_cols = lax.dynamic_slice_in_dim(b_ada, chip * ada_cols, ada_cols, axis=1).reshape(n_l, 1, ada_cols)
    mod_cols = ada_fwd(c_all, w_ada, b_cols, "ada_fwd")
    mod_g = allgather_small(mod_cols.reshape(-1, LANES), "gather_mod")
    mod_g = mod_g.reshape(N_CHIPS, 2, n_l, N_DEV * nb, ada_cols)[:, 0]
    mod_all = jnp.transpose(mod_g, (1, 2, 0, 3)).reshape(n_l, N_DEV * nb, 3 * d)
    mod_mine = lax.dynamic_slice_in_dim(mod_all, dev * nb, nb, axis=1)
    mod0 = mod_mine[0].reshape(nb, 1, 3 * d)
    mod1 = mod_mine[1].reshape(nb, 1, 3 * d)

    (wa_in, wa_out), tok_a = gather_inplace(
        [cast_into_slot(a_w_in[0], chip, mod_mine, "cast_a_in"), cast_into_slot(a_w_out[0], chip, mod_mine, "cast_a_out")],
        "gather_a")
    s_bi = gather_start(cast_into_slot(b_w_in[0], chip, tok_a, "cast_b_in"), "gather_b_in_start")
    s_bo = gather_start(cast_into_slot(b_w_out[0], chip, s_bi[3], "cast_b_out"), "gather_b_out_start")
    di = a_w_out.shape[1] * N_CHIPS
    wa_out = wa_out.reshape(di, d)

    x0 = x.reshape(m_rows, d)
    tgt = loss_target.reshape(m_rows, d)
    ng0 = norm_gain[0:1] + (s_bi[3][0, 0] + s_bo[3][0, 0])
    ng1 = norm_gain[1:2]
    bs_col = a_b_s[0].reshape(SG_GROUPS, SG_BLOCK, 1)
    proj_a, h_a = inproj_fwd(x0, mod0, ng0, wa_in, seq, False, "a_inproj")
    y_a = sgu_fwd(proj_a, a_ln_gain, a_ln_bias, a_w_s[0], bs_col, "a_sgu")
    x1, out_a = outproj_fwd(y_a, wa_out, x0, mod0, seq, "a_outproj")
    wb_in = gather_wait(*s_bi[:3], out_a, "gather_b_in_wait")
    proj_b, h_b = inproj_fwd(x1, mod1, ng1, wb_in, seq, True, "b_inproj")
    y_b, sts_b = hgrn_fwd(proj_b, b_lower_bounds, b_gn_gain, seq, "b_hgrn")
    wb_out = gather_wait(*s_bo[:3], y_b, "gather_b_out_wait").reshape(di, d)
    x2, out_b = outproj_fwd(y_b, wb_out, x1, mod1, seq, "b_outproj")
    loss_part, dx2, dfg = final_loss(x2, final_gain.reshape(1, d), tgt, "loss_head")

    shard_rows = di // N_CHIPS
    dy_b, dout_b, dgate1 = outproj_bwd(dx2, out_b, mod1, wb_out, seq, "b_outproj_bwd")
    gwb_out = grad_w_out(y_b, dout_b, "b_grad_w_out").reshape(N_CHIPS, shard_rows, d)
    e_bo = exchange_start(gwb_out, "exchange_b_out_start")
    dproj_b, dlb, dgn = hgrn_bwd(proj_b, dy_b, sts_b, b_lower_bounds, b_gn_gain + e_bo[4][0, 0], seq, "b_hgrn_bwd")
    e_bi = exchange_start(grad_w_in(h_b, dproj_b, N_CHIPS, True, "b_grad_w_in"), "exchange_b_in_start")
    dx1, dshift1, dscale1, dng1 = inproj_bwd(
        dproj_b, wb_in, x1, dx2, mod1, ng1 + e_bi[4][0, 0], seq, True, "b_inproj_bwd")

    dy_a, dout_a, dgate0 = outproj_bwd(dx1, out_a, mod0, wa_out, seq, "a_outproj_bwd")
    gwa_out = grad_w_out(y_a, dout_a, "a_grad_w_out").reshape(N_CHIPS, shard_rows, d)
    e_ao = exchange_start(gwa_out, "exchange_a_out_start")
    dproj_a, dws, dbs, dlg, dlbias = sgu_bwd(
        proj_a, dy_a, a_ln_gain + e_ao[4][0, 0], a_ln_bias, a_w_s[0], bs_col, "a_sgu_bwd")
    e_ai = exchange_start(grad_w_in(h_a, dproj_a, N_CHIPS, False, "a_grad_w_in"), "exchange_a_in_start")
    dx0, dshift0, dscale0, dng0 = inproj_bwd(
        dproj_a, wa_in, x0, dx1, mod0, norm_gain[0:1] + e_ai[4][0, 0], seq, False, "a_inproj_bwd")
    grad_x = dx0.reshape(nb, seq, d)

    dmod = jnp.concatenate([dshift0, dscale0, dgate0, dshift1, dscale1, dgate1], axis=2)
    n_dmod = dmod.size
    small_g = [jnp.concatenate([dng0, dng1], axis=0), dlg, dlbias, dws, dbs, dlb, dfg, dgn]
    packed_g = _pack([dmod] + small_g + [loss_part])
    rows = packed_g.shape[0]
    s_small = gather_all_start(
        lax.dynamic_update_slice(jnp.zeros((N_DEV, rows, LANES), F32), packed_g[None], (dev, 0, 0)),
        "gather_small_start")

    def finish(group, after):
        mine = []
        for ex, _, _, _, nm in group:
            parts_thru, land = exchange_wait(ex[0], ex[1], ex[2], ex[3], after, "exchange_" + nm + "_wait")
            mine.append(sum_parts(parts_thru, land, chip, "sum_" + nm))
            after = mine[-1]
        theirs = swap_sibling(mine, "swap_" + group[0][4])
        return [[r.reshape(w.shape) for r in adamw_pair(pa, pb, w[0], m[0], v[0], "adamw_" + nm)]
                for pa, pb, (_, w, m, v, nm) in zip(mine, theirs, group)]

    (gb_out, db_out, mb_out, vb_out), (gb_in, db_in, mb_in, vb_in), (ga_out, da_out, ma_out, va_out) = finish(
        [(e_bo, b_w_out, m_b_w_out, v_b_w_out, "b_out"), (e_bi, b_w_in, m_b_w_in, v_b_w_in, "b_in"),
         (e_ao, a_w_out, m_a_w_out, v_a_w_out, "a_out")], s_small[3])
    ((ga_in, da_in, ma_in, va_in),) = finish([(e_ai, a_w_in, m_a_w_in, v_a_w_in, "a_in")], ga_out)

    small_w = [norm_gain, a_ln_gain, a_ln_bias, a_w_s, a_b_s, b_lower_bounds, final_gain, b_gn_gain]
    small_m = [m_norm_gain, m_a_ln_gain, m_a_ln_bias, m_a_w_s, m_a_b_s, m_b_lower_bounds, m_final_gain, m_b_gn_gain]
    small_v = [v_norm_gain, v_a_ln_gain, v_a_ln_bias, v_a_w_s, v_a_b_s, v_b_lower_bounds, v_final_gain, v_b_gn_gain]
    rows_of = lambda a: a.reshape(-1, a.shape[-1])
    gathered = gather_all_wait(s_small[0], s_small[1], s_small[2], ga_in, "gather_small_wait")
    tail, small_res = small_update(
        gathered, n_dmod // LANES, [rows_of(a) for a in small_w], [rows_of(a) for a in small_m],
        [rows_of(a) for a in small_v], "small_update")
    loss = tail[0, 0]
    sg, sd, sm, sv = [[small_res[p][kind].reshape(w.shape) for p, w in enumerate(small_w)] for kind in range(4)]

    dmod_all = gathered[:, :n_dmod // LANES].reshape(N_DEV * nb, n_l, 3 * d)
    dmod_cols = lax.dynamic_slice_in_dim(dmod_all, chip * ada_cols, ada_cols, axis=2)
    dmod_cols = jnp.transpose(dmod_cols, (1, 0, 2))
    g_wada, d_wada, m_wada, v_wada = ada_bwd(c_all, dmod_cols, w_ada, m_w_ada, v_w_ada, "ada_bwd")
    flat = lambda a: a.reshape(1, -1)
    g_bada, d_bada, m_bada, v_bada = [
        r.reshape(b_ada.shape) for r in
        bias_update(dmod_all.reshape(N_DEV * nb, n_l * 3 * d), flat(b_ada), flat(m_b_ada), flat(v_b_ada), "bias_update")]

    def order(ng, wada, bada, ain, sm_rest, aout, bin_, bout):
        lg, lbi, ws_, bs_, lbd, fg_, gn_ = sm_rest
        return [ng, wada, bada, ain, lg, lbi, ws_, bs_, aout, bin_, lbd, gn_, bout, fg_]

    grads = order(sg[0], g_wada, g_bada, ga_in, sg[1:8], ga_out, gb_in, gb_out)
    deltas = order(sd[0], d_wada, d_bada, da_in, sd[1:8], da_out, db_in, db_out)
    new_m = order(sm[0], m_wada, m_bada, ma_in, sm[1:8], ma_out, mb_in, mb_out)
    new_v = order(sv[0], v_wada, v_bada, va_in, sv[1:8], va_out, vb_in, vb_out)
    return (loss, grad_x, *grads, *deltas, *new_m, *new_v)
```

```python
import functools

import jax
import jax.numpy as jnp
from jax import lax
from jax.experimental import pallas as pl
from jax.experimental.pallas import tpu as pltpu

F32 = jnp.float32
BF16 = jnp.bfloat16
EPS = 1e-6
CHUNK = 64
SG_BLOCK = 128
SG_GROUPS = 8
HEAD_DIM = 128
HG_WIDE = 8
HG_ROWS = 128
N_CHIPS = 4
N_DEV = 8
LANES = 128
ADAM_LR = 0.001
ADAM_B1 = 0.9
ADAM_B2 = 0.999
ADAM_EPS = 1e-08
ADAM_WD = 0.01
ADAM_STEP = 10
GELU_C0 = 0.7978845608028654
GELU_C1 = 0.044715
MESH = pl.DeviceIdType.MESH
VMEM_LIMIT = 56 * 1024 * 1024


ROW_TILE = 1024


def _col_tile(n):
    return next(t for t in (1024, 768, 512, 256) if n % t == 0)


def _call(body, **kw):
    return pl.pallas_call(body, **kw)


def _params(**kw):
    return pltpu.CompilerParams(vmem_limit_bytes=VMEM_LIMIT, **kw)


def _sigmoid(x):
    return 0.5 * jnp.tanh(0.5 * x) + 0.5


def _sigmoid_small(x):
    return 1.0 / (1.0 + jnp.exp(-x))


def _silu_and_grad(x):
    s = _sigmoid(x)
    return x * s, s * (1.0 + x * (1.0 - s))


def _gelu(x):
    return 0.5 * x * (1.0 + jnp.tanh(GELU_C0 * (x + GELU_C1 * x * x * x)))


def _gelu_and_grad(x):
    t = jnp.tanh(GELU_C0 * (x + GELU_C1 * x * x * x))
    g = 0.5 * x * (1.0 + t)
    dg = 0.5 * (1.0 + t) + 0.5 * x * (1.0 - t * t) * (GELU_C0 * (1.0 + 3.0 * GELU_C1 * x * x))
    return g, dg


def _dot(a, b, dims, precision=None):
    return lax.dot_general(a, b, (dims, ((), ())), precision=precision, preferred_element_type=F32)


NN = ((1,), (0,))
NT = ((1,), (1,))
TN = ((0,), (0,))


def _adamw(w, g, m, v):
    m = ADAM_B1 * m + (1.0 - ADAM_B1) * g
    v = ADAM_B2 * v + (1.0 - ADAM_B2) * (g * g)
    m_hat = m / (1.0 - ADAM_B1 ** ADAM_STEP)
    v_hat = v / (1.0 - ADAM_B2 ** ADAM_STEP)
    delta = -ADAM_LR * (m_hat / (jnp.sqrt(v_hat) + ADAM_EPS) + ADAM_WD * w)
    return delta, m, v


def _chunk_mask():
    r = lax.broadcasted_iota(jnp.int32, (SG_BLOCK, SG_BLOCK), 0)
    c = lax.broadcasted_iota(jnp.int32, (SG_BLOCK, SG_BLOCK), 1)
    return (c // CHUNK) <= (r // CHUNK)


def _place():
    return lax.axis_index("x"), lax.axis_index("y"), lax.axis_index("c")


def _other_chips(x, y):
    return [(1 - x, y), (x, 1 - y), (1 - x, 1 - y)]


def allgather_small(v, name):
    m_per, n = v.shape

    def body(x_ref, out_ref, send_sems, recv_sems, local_sem):
        x, y, c = _place()
        me, sibling = (x, y, c), (x, y, 1 - c)
        chips = _other_chips(x, y)

        def rows(px, py, pc):
            return out_ref.at[pl.ds((4 * px + 2 * py + pc) * m_per, m_per), :]

        def copy(k, block, to, src=None):
            return pltpu.make_async_remote_copy(
                src_ref=rows(*block) if src is None else src, dst_ref=rows(*block),
                send_sem=send_sems.at[k], recv_sem=recv_sems.at[k], device_id=to, device_id_type=MESH)

        mine = pltpu.make_async_copy(x_ref, rows(*me), local_sem)
        mine.start()
        first = [copy(0, me, sibling, src=x_ref)]
        first += [copy(1 + j, me, (*chip, c), src=x_ref) for j, chip in enumerate(chips)]
        for cp in first:
            cp.start()
        passed = [copy(4 + j, (*chip, c), sibling) for j, chip in enumerate(chips)]
        for j, chip in enumerate(chips):
            copy(1 + j, (*chip, c), me).wait_recv()
            passed[j].start()
        copy(0, sibling, me).wait_recv()
        for j, chip in enumerate(chips):
            copy(4 + j, (*chip, 1 - c), me).wait_recv()
        for cp in first + passed:
            cp.wait_send()
        mine.wait()

    return _call(
        body, name=name,
        out_shape=jax.ShapeDtypeStruct((N_DEV * m_per, n), v.dtype),
        in_specs=[pl.BlockSpec(memory_space=pltpu.VMEM)],
        out_specs=pl.BlockSpec(memory_space=pltpu.VMEM),
        scratch_shapes=[pltpu.SemaphoreType.DMA((7,)), pltpu.SemaphoreType.DMA((7,)), pltpu.SemaphoreType.DMA],
    )(v)


def _hbm_spec():
    return pl.BlockSpec(memory_space=pltpu.HBM)


def _sem_spec():
    return pl.BlockSpec(memory_space=pltpu.SEMAPHORE)


def _split_params():
    return pltpu.CompilerParams(has_side_effects=pltpu.SideEffectType.DATAFLOW_SIDE_EFFECTING)


def _hbm(a):
    return pltpu.with_memory_space_constraint(a, pltpu.HBM)


def gather_inplace(lands, name):
    n = len(lands)

    def body(*refs):
        land_refs, token = refs[n:2 * n], refs[2 * n]
        send_sems, recv_sems = refs[2 * n + 1:]
        x, y, c = _place()
        chips = _other_chips(x, y)

        def copy(w, k, chip_idx, core_half, to):
            half = lands[w].shape[1] // 2
            rows = land_refs[w].at[chip_idx, pl.ds(core_half * half, half), :]
            return pltpu.make_async_remote_copy(
                src_ref=rows, dst_ref=rows, send_sem=send_sems.at[6 * w + k], recv_sem=recv_sems.at[6 * w + k],
                device_id=to, device_id_type=MESH)

        first = [copy(w, j, 2 * x + y, c, (px, py, c)) for w in range(n) for j, (px, py) in enumerate(chips)]
        for cp in first:
            cp.start()
        passed = []
        for w in range(n):
            for j, (px, py) in enumerate(chips):
                copy(w, j, 2 * px + py, c, (px, py, c)).wait_recv()
                passed.append(copy(w, 3 + j, 2 * px + py, c, (x, y, 1 - c)))
                passed[-1].start()
        for w in range(n):
            for j, (px, py) in enumerate(chips):
                copy(w, 3 + j, 2 * px + py, 1 - c, (x, y, 1 - c)).wait_recv()
        for cp in first + passed:
            cp.wait_send()
        token[...] = jnp.zeros_like(token)

    res = _call(
        body, name=name,
        out_shape=[jax.ShapeDtypeStruct(a.shape, a.dtype) for a in lands] + [jax.ShapeDtypeStruct((8, LANES), F32)],
        in_specs=[_hbm_spec()] * n, out_specs=[_hbm_spec()] * n + [pl.BlockSpec(memory_space=pltpu.VMEM)],
        input_output_aliases={w: w for w in range(n)},
        scratch_shapes=[pltpu.SemaphoreType.DMA((6 * n,)), pltpu.SemaphoreType.DMA((6 * n,))],
    )(*lands)
    return res[:n], res[n]


def gather_start(land, name):
    def body(land_ref, send_sems, recv_sems, land_thru, token):
        del land_thru
        x, y, c = _place()
        for j, (px, py) in enumerate(_other_chips(x, y)):
            pltpu.make_async_remote_copy(
                src_ref=land_ref.at[2 * x + y], dst_ref=land_ref.at[2 * x + y],
                send_sem=send_sems.at[j], recv_sem=recv_sems.at[j], device_id=(px, py, c),
                device_id_type=MESH).start()
        token[...] = jnp.zeros_like(token)

    return _call(
        body, name=name,
        out_shape=(pltpu.SemaphoreType.DMA((3,)), pltpu.SemaphoreType.DMA((3,)),
                   pltpu.HBM(land.shape, land.dtype), jax.ShapeDtypeStruct((8, LANES), F32)),
        in_specs=(_hbm_spec(),),
        out_specs=(_sem_spec(), _sem_spec(), _hbm_spec(), pl.BlockSpec(memory_space=pltpu.VMEM)),
        input_output_aliases={0: 2}, compiler_params=_split_params(),
    )(_hbm(land))


def gather_wait(send_sems, recv_sems, land, after, name):
    def body(land_ref, send_sems, recv_sems, after_ref, land_out):
        del after_ref, land_out
        x, y, c = _place()
        for j, (px, py) in enumerate(_other_chips(x, y)):
            cp = pltpu.make_async_remote_copy(
                src_ref=land_ref.at[2 * x + y], dst_ref=land_ref.at[2 * px + py],
                send_sem=send_sems.at[j], recv_sem=recv_sems.at[j], device_id=(px, py, c), device_id_type=MESH)
            cp.wait_send()
            cp.wait_recv()

    return _call(
        body, name=name,
        out_shape=pltpu.HBM(land.shape, land.dtype),
        in_specs=(_hbm_spec(), _sem_spec(), _sem_spec(), pl.BlockSpec(memory_space=pl.ANY)),
        out_specs=_hbm_spec(), input_output_aliases={0: 0}, compiler_params=_split_params(),
    )(land, send_sems, recv_sems, after)


def _flips():
    return [(fx, fy, fc) for fx in (0, 1) for fy in (0, 1) for fc in (0, 1) if (fx, fy, fc) != (0, 0, 0)]


def _flipped(x, y, c, flip):
    fx, fy, fc = flip
    return (1 - x if fx else x, 1 - y if fy else y, 1 - c if fc else c)


def gather_all_start(land, name):
    def body(land_ref, send_sems, recv_sems, land_thru, token):
        del land_thru
        x, y, c = _place()
        for k, flip in enumerate(_flips()):
            pltpu.make_async_remote_copy(
                src_ref=land_ref.at[4 * x + 2 * y + c], dst_ref=land_ref.at[4 * x + 2 * y + c],
                send_sem=send_sems.at[k], recv_sem=recv_sems.at[k], device_id=_flipped(x, y, c, flip),
                device_id_type=MESH).start()
        token[...] = jnp.zeros_like(token)

    return _call(
        body, name=name,
        out_shape=(pltpu.SemaphoreType.DMA((7,)), pltpu.SemaphoreType.DMA((7,)),
                   pltpu.HBM(land.shape, land.dtype), jax.ShapeDtypeStruct((8, LANES), F32)),
        in_specs=(_hbm_spec(),),
        out_specs=(_sem_spec(), _sem_spec(), _hbm_spec(), pl.BlockSpec(memory_space=pltpu.VMEM)),
        input_output_aliases={0: 2}, compiler_params=_split_params(),
    )(_hbm(land))


def gather_all_wait(send_sems, recv_sems, land, after, name):
    def body(land_ref, send_sems, recv_sems, after_ref, land_out):
        del after_ref, land_out
        x, y, c = _place()
        for k, flip in enumerate(_flips()):
            px, py, pc = _flipped(x, y, c, flip)
            cp = pltpu.make_async_remote_copy(
                src_ref=land_ref.at[4 * x + 2 * y + c], dst_ref=land_ref.at[4 * px + 2 * py + pc],
                send_sem=send_sems.at[k], recv_sem=recv_sems.at[k], device_id=(px, py, pc), device_id_type=MESH)
            cp.wait_send()
            cp.wait_recv()

    return _call(
        body, name=name,
        out_shape=pltpu.HBM(land.shape, land.dtype),
        in_specs=(_hbm_spec(), _sem_spec(), _sem_spec(), pl.BlockSpec(memory_space=pl.ANY)),
        out_specs=_hbm_spec(), input_output_aliases={0: 0}, compiler_params=_split_params(),
    )(land, send_sems, recv_sems, after)


def exchange_start(parts, name):
    _, r, c_ = parts.shape

    def body(parts_ref, land_ref, send_sems, recv_sems, parts_thru, land_thru, token):
        del parts_thru, land_thru
        x, y, c = _place()
        for j, (px, py) in enumerate(_other_chips(x, y)):
            pltpu.make_async_remote_copy(
                src_ref=parts_ref.at[2 * px + py], dst_ref=land_ref.at[j],
                send_sem=send_sems.at[j], recv_sem=recv_sems.at[j], device_id=(px, py, c),
                device_id_type=MESH).start()
        token[...] = jnp.zeros_like(token)

    return _call(
        body, name=name,
        out_shape=(pltpu.SemaphoreType.DMA((3,)), pltpu.SemaphoreType.DMA((3,)),
                   pltpu.HBM(parts.shape, parts.dtype), pltpu.HBM((3, r, c_), parts.dtype),
                   jax.ShapeDtypeStruct((8, LANES), F32)),
        in_specs=(_hbm_spec(), _hbm_spec()),
        out_specs=(_sem_spec(), _sem_spec(), _hbm_spec(), _hbm_spec(), pl.BlockSpec(memory_space=pltpu.VMEM)),
        input_output_aliases={0: 2, 1: 3}, compiler_params=_split_params(),
    )(_hbm(parts), _hbm(lax.empty((3, r, c_), parts.dtype)))


def exchange_wait(send_sems, recv_sems, parts, land, after, name):
    def body(parts_ref, land_ref, send_sems, recv_sems, after_ref, parts_out, land_out):
        del after_ref, parts_out, land_out
        x, y, c = _place()
        for j, (px, py) in enumerate(_other_chips(x, y)):
            cp = pltpu.make_async_remote_copy(
                src_ref=parts_ref.at[2 * px + py], dst_ref=land_ref.at[j],
                send_sem=send_sems.at[j], recv_sem=recv_sems.at[j], device_id=(px, py, c), device_id_type=MESH)
            cp.wait_send()
            cp.wait_recv()

    return _call(
        body, name=name,
        out_shape=(pltpu.HBM(parts.shape, parts.dtype), pltpu.HBM(land.shape, land.dtype)),
        in_specs=(_hbm_spec(), _hbm_spec(), _sem_spec(), _sem_spec(), pl.BlockSpec(memory_space=pl.ANY)),
        out_specs=(_hbm_spec(), _hbm_spec()), input_output_aliases={0: 0, 1: 1},
        compiler_params=_split_params(),
    )(parts, land, send_sems, recv_sems, after)


def cast_into_slot(w, chip, after, name):
    r, c = w.shape
    tr = min(256, r)

    def body(s_ref, w_ref, after_ref, o_ref):
        del s_ref, after_ref
        o_ref[...] = w_ref[...].astype(BF16)

    return _call(
        body, name=name,
        grid_spec=pltpu.PrefetchScalarGridSpec(
            num_scalar_prefetch=1, grid=(r // tr,),
            in_specs=[pl.BlockSpec((tr, c), lambda i, s: (i, 0)), pl.BlockSpec(memory_space=pl.ANY)],
            out_specs=pl.BlockSpec((None, tr, c), lambda i, s: (s[0], i, 0))),
        out_shape=jax.ShapeDtypeStruct((N_CHIPS, r, c), BF16),
        compiler_params=_params(),
    )(chip.reshape(1).astype(jnp.int32), w, after)


def sum_parts(parts, land, chip, name):
    _, r, c = parts.shape
    tr = min(256, r)

    def body(s_ref, p_ref, l_ref, o_ref):
        del s_ref
        acc = p_ref[...].astype(F32) + l_ref[0].astype(F32)
        acc = acc + l_ref[1].astype(F32)
        o_ref[...] = acc + l_ref[2].astype(F32)

    return _call(
        body, name=name,
        grid_spec=pltpu.PrefetchScalarGridSpec(
            num_scalar_prefetch=1, grid=(r // tr,),
            in_specs=[pl.BlockSpec((None, tr, c), lambda i, s: (s[0], i, 0)),
                      pl.BlockSpec((3, tr, c), lambda i, s: (0, i, 0))],
            out_specs=pl.BlockSpec((tr, c), lambda i, s: (i, 0))),
        out_shape=jax.ShapeDtypeStruct((r, c), F32),
        compiler_params=_params(),
    )(chip.reshape(1).astype(jnp.int32), parts, land)


def swap_sibling(arrs, name):
    n = len(arrs)

    def body(*refs):
        ins, outs = refs[:n], refs[n:2 * n]
        send_sems, recv_sems = refs[2 * n:]
        x, y, c = _place()
        cps = []
        for w in range(n):
            cp = pltpu.make_async_remote_copy(
                src_ref=ins[w], dst_ref=outs[w], send_sem=send_sems.at[w], recv_sem=recv_sems.at[w],
                device_id=(x, y, 1 - c), device_id_type=MESH)
            cp.start()
            cps.append(cp)
        for cp in cps:
            cp.wait_recv()
        for cp in cps:
            cp.wait_send()

    return _call(
        body, name=name,
        out_shape=[jax.ShapeDtypeStruct(a.shape, a.dtype) for a in arrs],
        in_specs=[_hbm_spec()] * n, out_specs=[_hbm_spec()] * n,
        scratch_shapes=[pltpu.SemaphoreType.DMA((n,)), pltpu.SemaphoreType.DMA((n,))],
    )(*arrs)


def adamw_pair(pa, pb, w, m, v, name):
    r, c = w.shape
    tr = min(128, r)

    def body(pa_ref, pb_ref, w_ref, m_ref, v_ref, g_ref, d_ref, nm_ref, nv_ref):
        g = pa_ref[...] + pb_ref[...]
        d, nm, nv = _adamw(w_ref[...], g, m_ref[...], v_ref[...])
        g_ref[...] = g
        d_ref[...] = d
        nm_ref[...] = nm
        nv_ref[...] = nv

    spec = pl.BlockSpec((tr, c), lambda i: (i, 0))
    return _call(
        body, name=name, grid=(r // tr,),
        out_shape=[jax.ShapeDtypeStruct((r, c), F32)] * 4,
        in_specs=[spec] * 5, out_specs=[spec] * 4,
        compiler_params=_params(),
    )(pa, pb, w, m, v)


def small_update(gathered, first_row, ws, ms, vs, name):
    n_w = len(ws)
    total_rows = gathered.shape[1]

    def body(*refs):
        g_ref = refs[0]
        w_refs, m_refs, v_refs = refs[1:1 + n_w], refs[1 + n_w:1 + 2 * n_w], refs[1 + 2 * n_w:1 + 3 * n_w]
        tail_ref = refs[1 + 3 * n_w]
        outs = refs[2 + 3 * n_w:2 + 7 * n_w]
        sum_ref = refs[2 + 7 * n_w]
        acc = g_ref[0]
        for k in range(1, N_DEV):
            acc = acc + g_ref[k]
        sum_ref[...] = acc
        row = first_row
        for p in range(n_w):
            a, b = ws[p].shape
            per = b // LANES
            g_out, d_out, m_out, v_out = outs[4 * p:4 * p + 4]
            if per == 1:
                g_out[...] = sum_ref[row:row + a, :]
            else:
                for i in range(a):
                    for jc in range(per):
                        g_out[i:i + 1, jc * LANES:(jc + 1) * LANES] = sum_ref[row + i * per + jc:row + i * per + jc + 1, :]
            row += a * per
            dl, nm, nv = _adamw(w_refs[p][...], g_out[...], m_refs[p][...], v_refs[p][...])
            d_out[...] = dl
            m_out[...] = nm
            v_out[...] = nv
        tail_ref[...] = sum_ref[row:row + 1, :]

    out_shape = [jax.ShapeDtypeStruct((1, LANES), F32)]
    for w in ws:
        out_shape += [jax.ShapeDtypeStruct(w.shape, F32)] * 4
    res = _call(
        body, name=name, out_shape=out_shape,
        scratch_shapes=[pltpu.VMEM((total_rows, LANES), F32)],
        compiler_params=_params(),
    )(gathered, *ws, *ms, *vs)
    return res[0], [res[1 + 4 * p:5 + 4 * p] for p in range(n_w)]


def ada_fwd(c_all, w_ada, b_cols, name):
    n_l, d, cols = w_ada.shape
    nb = c_all.shape[0]
    tn = 256

    def body(c_ref, w_ref, b_ref, o_ref):
        cv = c_ref[...]
        ca = (cv * _sigmoid(cv)).astype(BF16)
        o_ref[...] = _dot(ca, w_ref[...].astype(BF16), NN) + b_ref[...]

    return _call(
        body, name=name, grid=(n_l, cols // tn),
        out_shape=jax.ShapeDtypeStruct((n_l, nb, cols), F32),
        in_specs=[pl.BlockSpec((nb, d), lambda l, j: (0, 0)),
                  pl.BlockSpec((None, d, tn), lambda l, j: (l, 0, j)),
                  pl.BlockSpec((None, 1, tn), lambda l, j: (l, 0, j))],
        out_specs=pl.BlockSpec((None, nb, tn), lambda l, j: (l, 0, j)),
        compiler_params=_params(),
    )(c_all, w_ada, b_cols)


def ada_bwd(c_all, dmod_cols, w, m, v, name):
    n_l, d, cols = w.shape
    nb = c_all.shape[0]
    tn = 256

    def body(c_ref, dm_ref, w_ref, m_ref, v_ref, g_ref, d_ref, nm_ref, nv_ref):
        cv = c_ref[...]
        ca = (cv * _sigmoid(cv)).astype(BF16)
        g = _dot(ca, dm_ref[...].astype(BF16), TN)
        dl, nm, nv = _adamw(w_ref[...], g, m_ref[...], v_ref[...])
        g_ref[...] = g
        d_ref[...] = dl
        nm_ref[...] = nm
        nv_ref[...] = nv

    wspec = pl.BlockSpec((None, d, tn), lambda l, j: (l, 0, j))
    return _call(
        body, name=name, grid=(n_l, cols // tn),
        out_shape=[jax.ShapeDtypeStruct((n_l, d, cols), F32)] * 4,
        in_specs=[pl.BlockSpec((nb, d), lambda l, j: (0, 0)),
                  pl.BlockSpec((None, nb, tn), lambda l, j: (l, 0, j)),
                  wspec, wspec, wspec],
        out_specs=[wspec] * 4,
        compiler_params=_params(),
    )(c_all, dmod_cols, w, m, v)


def bias_update(dmod_all, w, m, v, name):
    def body(dm_ref, w_ref, m_ref, v_ref, g_ref, d_ref, nm_ref, nv_ref):
        g = jnp.sum(dm_ref[...], axis=0, keepdims=True)
        dl, nm, nv = _adamw(w_ref[...], g, m_ref[...], v_ref[...])
        g_ref[...] = g
        d_ref[...] = dl
        nm_ref[...] = nm
        nv_ref[...] = nv

    return _call(
        body, name=name,
        out_shape=[jax.ShapeDtypeStruct(w.shape, F32)] * 4,
        compiler_params=_params(),
    )(dmod_all, w, m, v)


def inproj_fwd(x, mod, ng, wg, seq, sectioned, proj_dtype, name):
    m_rows, d = x.shape
    nsh, _, ns = wg.shape
    n = nsh * ns
    tm, tn = min(ROW_TILE, seq), ns
    per = ns // tn

    def body(x_ref, mod_ref, ng_ref, w_ref, proj_ref, h_ref):
        @pl.when(pl.program_id(1) == 0)
        def _():
            xv = x_ref[...]
            r = lax.rsqrt(jnp.mean(xv * xv, axis=-1, keepdims=True) + EPS)
            md = mod_ref[0]
            h = (xv * r * ng_ref[...]) * (1.0 + md[:, d:2 * d]) + md[:, :d]
            h_ref[...] = h.astype(BF16)
        proj_ref[...] = _dot(h_ref[...], w_ref[...], NN).astype(proj_dtype)

    if sectioned:
        proj_shape = (nsh, m_rows, ns)
        proj_spec = pl.BlockSpec((None, tm, tn), lambda i, j: (j // per, i, j % per))
    else:
        proj_shape = (m_rows, n)
        proj_spec = pl.BlockSpec((tm, tn), lambda i, j: (i, j))
    return _call(
        body, name=name, grid=(m_rows // tm, n // tn),
        out_shape=[jax.ShapeDtypeStruct(proj_shape, proj_dtype), jax.ShapeDtypeStruct((m_rows, d), BF16)],
        in_specs=[pl.BlockSpec((tm, d), lambda i, j: (i, 0)),
                  pl.BlockSpec((1, 1, 3 * d), lambda i, j: ((i * tm) // seq, 0, 0)),
                  pl.BlockSpec((1, d), lambda i, j: (0, 0)),
                  pl.BlockSpec((None, d, tn), lambda i, j: (j // per, 0, j % per))],
        out_specs=[proj_spec, pl.BlockSpec((tm, d), lambda i, j: (i, 0))],
        compiler_params=_params(),
    )(x, mod, ng, wg)


def outproj_fwd(y, w, x, mod, seq, name):
    m_rows, di = y.shape
    d = w.shape[1]
    tm = min(ROW_TILE, seq)

    def body(y_ref, w_ref, x_ref, mod_ref, xn_ref, out_ref):
        acc = _dot(y_ref[...], w_ref[...], NN)
        out_ref[...] = acc.astype(BF16)
        xn_ref[...] = x_ref[...] + mod_ref[0][:, 2 * d:] * acc

    row = pl.BlockSpec((tm, d), lambda i: (i, 0))
    return _call(
        body, name=name, grid=(m_rows // tm,),
        out_shape=[jax.ShapeDtypeStruct((m_rows, d), F32), jax.ShapeDtypeStruct((m_rows, d), BF16)],
        in_specs=[pl.BlockSpec((tm, di), lambda i: (i, 0)),
                  pl.BlockSpec((di, d), lambda i: (0, 0)),
                  row,
                  pl.BlockSpec((1, 1, 3 * d), lambda i: ((i * tm) // seq, 0, 0))],
        out_specs=[row, row],
        compiler_params=_params(),
    )(y, w, x, mod)


def outproj_bwd(dxo, out, mod, w, seq, name):
    m_rows, d = dxo.shape
    di = w.shape[0]
    nb = m_rows // seq
    tm, tn = min(ROW_TILE, seq), _col_tile(di)

    def body(dxo_ref, out_ref, mod_ref, w_ref, dy_ref, dout_ref, dgate_ref):
        i = pl.program_id(0)

        @pl.when(pl.program_id(1) == 0)
        def _():
            dx = dxo_ref[...]
            dout_ref[...] = (mod_ref[0][:, 2 * d:] * dx).astype(BF16)
            part = jnp.sum(dx * out_ref[...].astype(F32), axis=0, keepdims=True)

            @pl.when((i * tm) % seq == 0)
            def _():
                dgate_ref[0] = part

            @pl.when((i * tm) % seq != 0)
            def _():
                dgate_ref[0] = dgate_ref[0] + part

        dy_ref[...] = _dot(dout_ref[...], w_ref[...], NT).astype(BF16)

    row = pl.BlockSpec((tm, d), lambda i, j: (i, 0))
    return _call(
        body, name=name, grid=(m_rows // tm, di // tn),
        out_shape=[jax.ShapeDtypeStruct((m_rows, di), BF16), jax.ShapeDtypeStruct((m_rows, d), BF16),
                   jax.ShapeDtypeStruct((nb, 1, d), F32)],
        in_specs=[row, row,
                  pl.BlockSpec((1, 1, 3 * d), lambda i, j: ((i * tm) // seq, 0, 0)),
                  pl.BlockSpec((tn, d), lambda i, j: (j, 0))],
        out_specs=[pl.BlockSpec((tm, tn), lambda i, j: (i, j)), row,
                   pl.BlockSpec((1, 1, d), lambda i, j: ((i * tm) // seq, 0, 0))],
        compiler_params=_params(),
    )(dxo, out, mod, w)


def grad_w_out(y, dout, name):
    m_rows, di = y.shape
    d = dout.shape[1]
    tm, tk = min(ROW_TILE, m_rows), _col_tile(di)
    n_m = m_rows // tm

    def body(y_ref, do_ref, o_ref, acc_ref):
        mi = pl.program_id(1)

        @pl.when(mi == 0)
        def _():
            acc_ref[...] = jnp.zeros_like(acc_ref)

        acc_ref[...] += _dot(y_ref[...], do_ref[...], TN)

        @pl.when(mi == n_m - 1)
        def _():
            o_ref[...] = acc_ref[...].astype(BF16)

    return _call(
        body, name=name, grid=(di // tk, n_m),
        out_shape=jax.ShapeDtypeStruct((di, d), BF16),
        in_specs=[pl.BlockSpec((tm, tk), lambda j, mi: (mi, j)),
                  pl.BlockSpec((tm, d), lambda j, mi: (mi, 0))],
        out_specs=pl.BlockSpec((tk, d), lambda j, mi: (j, 0)),
        scratch_shapes=[pltpu.VMEM((tk, d), F32)],
        compiler_params=_params(),
    )(y, dout)


def grad_w_in(h, dproj, nsh, sectioned, name):
    m_rows, d = h.shape
    n = dproj.shape[0] * dproj.shape[2] if sectioned else dproj.shape[1]
    ns = n // nsh
    tm, tn = min(ROW_TILE, m_rows), _col_tile(ns)
    per = ns // tn
    n_m = m_rows // tm

    def body(h_ref, dp_ref, o_ref, acc_ref):
        mi = pl.program_id(1)
        @pl.when(mi == 0)
        def _():
            acc_ref[...] = jnp.zeros_like(acc_ref)

        acc_ref[...] += _dot(h_ref[...], dp_ref[...], TN)

        @pl.when(mi == n_m - 1)
        def _():
            o_ref[...] = acc_ref[...].astype(BF16)

    if sectioned:
        dp_spec = pl.BlockSpec((None, tm, tn), lambda j, mi: (j // per, mi, j % per))
    else:
        dp_spec = pl.BlockSpec((tm, tn), lambda j, mi: (mi, j))
    return _call(
        body, name=name, grid=(n // tn, n_m),
        out_shape=jax.ShapeDtypeStruct((nsh, d, ns), BF16),
        in_specs=[pl.BlockSpec((tm, d), lambda j, mi: (mi, 0)), dp_spec],
        out_specs=pl.BlockSpec((None, d, tn), lambda j, mi: (j // per, 0, j % per)),
        scratch_shapes=[pltpu.VMEM((d, tn), F32)],
        compiler_params=_params(),
    )(h, dproj)


def inproj_bwd(dproj, wg, x, dxo, mod, ng, seq, sectioned, name):
    m_rows, d = x.shape
    nsh, _, ns = wg.shape
    n = nsh * ns
    nb = m_rows // seq
    tm, tk = min(ROW_TILE, seq), _col_tile(ns)
    per = ns // tk
    n_k = n // tk

    def body(dp_ref, w_ref, x_ref, dxo_ref, mod_ref, ng_ref, dxi_ref, dsh_ref, dsc_ref, dng_ref, acc_ref):
        i, k = pl.program_id(0), pl.program_id(1)
        @pl.when(k == 0)
        def _():
            acc_ref[...] = jnp.zeros_like(acc_ref)

        acc_ref[...] += _dot(dp_ref[...], w_ref[...], NT)

        @pl.when(k == n_k - 1)
        def _():
            dh = acc_ref[...]
            xv = x_ref[...]
            r = lax.rsqrt(jnp.mean(xv * xv, axis=-1, keepdims=True) + EPS)
            xn = xv * r
            md = mod_ref[0]
            gain = ng_ref[...]
            p_shift = jnp.sum(dh, axis=0, keepdims=True)
            p_scale = jnp.sum(dh * (xn * gain), axis=0, keepdims=True)
            drn = dh * (1.0 + md[:, d:2 * d])
            p_ng = jnp.sum(drn * xn, axis=0, keepdims=True)
            dxn = drn * gain
            dx = r * (dxn - xn * jnp.mean(dxn * xn, axis=-1, keepdims=True))
            dxi_ref[...] = dxo_ref[...] + dx

            @pl.when((i * tm) % seq == 0)
            def _():
                dsh_ref[0] = p_shift
                dsc_ref[0] = p_scale

            @pl.when((i * tm) % seq != 0)
            def _():
                dsh_ref[0] = dsh_ref[0] + p_shift
                dsc_ref[0] = dsc_ref[0] + p_scale

            @pl.when(i == 0)
            def _():
                dng_ref[...] = p_ng

            @pl.when(i != 0)
            def _():
                dng_ref[...] = dng_ref[...] + p_ng

    if sectioned:
        dp_spec = pl.BlockSpec((None, tm, tk), lambda i, k: (k // per, i, k % per))
    else:
        dp_spec = pl.BlockSpec((tm, tk), lambda i, k: (i, k))
    row = pl.BlockSpec((tm, d), lambda i, k: (i, 0))
    per_seq = pl.BlockSpec((1, 1, d), lambda i, k: ((i * tm) // seq, 0, 0))
    return _call(
        body, name=name, grid=(m_rows // tm, n_k),
        out_shape=[jax.ShapeDtypeStruct((m_rows, d), F32), jax.ShapeDtypeStruct((nb, 1, d), F32),
                   jax.ShapeDtypeStruct((nb, 1, d), F32), jax.ShapeDtypeStruct((1, d), F32)],
        in_specs=[dp_spec,
                  pl.BlockSpec((None, d, tk), lambda i, k: (k // per, 0, k % per)),
                  row, row,
                  pl.BlockSpec((1, 1, 3 * d), lambda i, k: ((i * tm) // seq, 0, 0)),
                  pl.BlockSpec((1, d), lambda i, k: (0, 0))],
        out_specs=[row, per_seq, per_seq, pl.BlockSpec((1, d), lambda i, k: (0, 0))],
        scratch_shapes=[pltpu.VMEM((tm, d), F32)],
        compiler_params=_params(),
    )(dproj, wg, x, dxo, mod, ng)


def _sgu_stats(proj_ref, vg_ref, di, gd):
    s1 = jnp.zeros((SG_BLOCK, 1), F32)
    for g in range(SG_GROUPS):
        vg = _gelu(proj_ref[:, di + g * gd:di + (g + 1) * gd].astype(F32))
        vg_ref[:, g * gd:(g + 1) * gd] = vg
        s1 = s1 + jnp.sum(vg, axis=1, keepdims=True)
    mu = s1 / di
    s2 = jnp.zeros((SG_BLOCK, 1), F32)
    for g in range(SG_GROUPS):
        dv = vg_ref[:, g * gd:(g + 1) * gd] - mu
        s2 = s2 + jnp.sum(dv * dv, axis=1, keepdims=True)
    return mu, lax.rsqrt(s2 / di + EPS)


def sgu_fwd(proj, ln_gain, ln_bias, ws, bs, name):
    m_rows, n3 = proj.shape
    di = n3 // 3
    gd = di // SG_GROUPS

    def body(proj_ref, lg_ref, lb_ref, ws_ref, bs_ref, y_ref, wsm_ref, vg_ref):
        @pl.when(pl.program_id(0) == 0)
        def _():
            mask = _chunk_mask()
            for g in range(SG_GROUPS):
                wsm_ref[g] = jnp.where(mask, ws_ref[g], 0.0).astype(BF16)

        mu, rstd = _sgu_stats(proj_ref, vg_ref, di, gd)
        for g in range(SG_GROUPS):
            cs = slice(g * gd, (g + 1) * gd)
            vln = (vg_ref[:, cs] - mu) * rstd * lg_ref[:, cs] + lb_ref[:, cs]
            s = _dot(wsm_ref[g], vln.astype(BF16), NN) + bs_ref[g]
            u = _gelu(proj_ref[:, cs].astype(F32))
            gp = proj_ref[:, 2 * di + g * gd:2 * di + (g + 1) * gd].astype(F32)
            y_ref[:, cs] = (u * s * (gp * _sigmoid(gp))).astype(BF16)

    full = lambda shape: pl.BlockSpec(shape, lambda i: (0,) * len(shape))
    return _call(
        body, name=name, grid=(m_rows // SG_BLOCK,),
        out_shape=jax.ShapeDtypeStruct((m_rows, di), BF16),
        in_specs=[pl.BlockSpec((SG_BLOCK, n3), lambda i: (i, 0)),
                  full((1, di)), full((1, di)),
                  full((SG_GROUPS, SG_BLOCK, SG_BLOCK)), full((SG_GROUPS, SG_BLOCK, 1))],
        out_specs=pl.BlockSpec((SG_BLOCK, di), lambda i: (i, 0)),
        scratch_shapes=[pltpu.VMEM((SG_GROUPS, SG_BLOCK, SG_BLOCK), BF16), pltpu.VMEM((SG_BLOCK, di), F32)],
        compiler_params=_params(),
    )(proj, ln_gain, ln_bias, ws, bs)


def sgu_bwd(proj, dy, ln_gain, ln_bias, ws, bs, name):
    m_rows, n3 = proj.shape
    di = n3 // 3
    gd = di // SG_GROUPS
    n_i = m_rows // SG_BLOCK

    def body(proj_ref, dy_ref, lg_ref, lb_ref, ws_ref, bs_ref,
             dp_ref, dws_ref, dbs_ref, dlg_ref, dlb_ref, wsm_ref, vg_ref, dvh_ref):
        i = pl.program_id(0)

        def before():
            @pl.when(i == 0)
            def _():
                mask = _chunk_mask()
                for g in range(SG_GROUPS):
                    wsm_ref[g] = jnp.where(mask, ws_ref[g], 0.0).astype(BF16)
                dws_ref[...] = jnp.zeros_like(dws_ref)
                dbs_ref[...] = jnp.zeros_like(dbs_ref)
                dlg_ref[...] = jnp.zeros_like(dlg_ref)
                dlb_ref[...] = jnp.zeros_like(dlb_ref)

        def after():
            @pl.when(i == n_i - 1)
            def _():
                mask = _chunk_mask()
                for g in range(SG_GROUPS):
                    dws_ref[g] = jnp.where(mask, dws_ref[g], 0.0)

        before()
        mu, rstd = _sgu_stats(proj_ref, vg_ref, di, gd)
        m1 = jnp.zeros((SG_BLOCK, 1), F32)
        m2 = jnp.zeros((SG_BLOCK, 1), F32)
        for g in range(SG_GROUPS):
            cs = slice(g * gd, (g + 1) * gd)
            gs = slice(2 * di + g * gd, 2 * di + (g + 1) * gd)
            gain = lg_ref[:, cs]
            vhat = (vg_ref[:, cs] - mu) * rstd
            vln_b = (vhat * gain + lb_ref[:, cs]).astype(BF16)
            s = _dot(wsm_ref[g], vln_b, NN) + bs_ref[g]
            u, du = _gelu_and_grad(proj_ref[:, cs].astype(F32))
            sg, dsg = _silu_and_grad(proj_ref[:, gs].astype(F32))
            dyv = dy_ref[:, cs].astype(F32)
            dp_ref[:, cs] = (dyv * s * sg * du).astype(BF16)
            dp_ref[:, gs] = (dyv * u * s * dsg).astype(BF16)
            ds = dyv * u * sg
            ds_b = ds.astype(BF16)
            dws_ref[g] = dws_ref[g] + _dot(ds_b, vln_b, NT)
            dbs_ref[g] = dbs_ref[g] + jnp.sum(ds, axis=1, keepdims=True)
            dvln = _dot(wsm_ref[g], ds_b, TN)
            dlg_ref[:, cs] = dlg_ref[:, cs] + jnp.sum(dvln * vhat, axis=0, keepdims=True)
            dlb_ref[:, cs] = dlb_ref[:, cs] + jnp.sum(dvln, axis=0, keepdims=True)
            dvh = dvln * gain
            dvh_ref[:, cs] = dvh
            m1 = m1 + jnp.sum(dvh, axis=1, keepdims=True)
            m2 = m2 + jnp.sum(dvh * vhat, axis=1, keepdims=True)
        m1 = m1 / di
        m2 = m2 / di
        for g in range(SG_GROUPS):
            cs = slice(g * gd, (g + 1) * gd)
            vs = slice(di + g * gd, di + (g + 1) * gd)
            vhat = (vg_ref[:, cs] - mu) * rstd
            dvg = rstd * (dvh_ref[:, cs] - m1 - vhat * m2)
            _, dgel = _gelu_and_grad(proj_ref[:, vs].astype(F32))
            dp_ref[:, vs] = (dvg * dgel).astype(BF16)

        after()

    full = lambda shape: pl.BlockSpec(shape, lambda i: (0,) * len(shape))
    return _call(
        body, name=name, grid=(n_i,),
        out_shape=[jax.ShapeDtypeStruct((m_rows, n3), BF16),
                   jax.ShapeDtypeStruct((SG_GROUPS, SG_BLOCK, SG_BLOCK), F32),
                   jax.ShapeDtypeStruct((SG_GROUPS, SG_BLOCK, 1), F32),
                   jax.ShapeDtypeStruct((1, di), F32), jax.ShapeDtypeStruct((1, di), F32)],
        in_specs=[pl.BlockSpec((SG_BLOCK, n3), lambda i: (i, 0)),
                  pl.BlockSpec((SG_BLOCK, di), lambda i: (i, 0)),
                  full((1, di)), full((1, di)),
                  full((SG_GROUPS, SG_BLOCK, SG_BLOCK)), full((SG_GROUPS, SG_BLOCK, 1))],
        out_specs=[pl.BlockSpec((SG_BLOCK, n3), lambda i: (i, 0)),
                   full((SG_GROUPS, SG_BLOCK, SG_BLOCK)), full((SG_GROUPS, SG_BLOCK, 1)),
                   full((1, di)), full((1, di))],
        scratch_shapes=[pltpu.VMEM((SG_GROUPS, SG_BLOCK, SG_BLOCK), BF16),
                        pltpu.VMEM((SG_BLOCK, di), F32), pltpu.VMEM((SG_BLOCK, di), F32)],
        compiler_params=_params(),
    )(proj, dy, ln_gain, ln_bias, ws, bs)


def _lower_bound(lbraw):
    mx = jnp.maximum(lbraw[0:1, :], lbraw[1:2, :])
    e0 = jnp.exp(lbraw[0:1, :] - mx)
    e1 = jnp.exp(lbraw[1:2, :] - mx)
    p0 = e0 / (e0 + e1)
    p1 = e1 / (e0 + e1)
    return (p0 + p1) - p0, p0, p1


def _tri(lower):
    r = lax.broadcasted_iota(jnp.int32, (CHUNK, CHUNK), 0)
    c = lax.broadcasted_iota(jnp.int32, (CHUNK, CHUNK), 1)
    return ((r >= c) if lower else (c >= r)).astype(F32)


def _row(a, idx):
    r = lax.broadcasted_iota(jnp.int32, a.shape, 0)
    return jnp.sum(jnp.where(r == idx, a, 0.0), axis=0, keepdims=True)


def _hgrn_decay(fp, lb, tri):
    sgm = _sigmoid_small(fp)
    return sgm, _dot(tri, jnp.log(lb + (1.0 - lb) * sgm), NN, precision=lax.Precision.HIGHEST)


def _hgrn_gates(qp, sgm, a, lb):
    f = lb + (1.0 - lb) * sgm
    k = 1.0 - f
    a_mid = _row(a, CHUNK // 2 - 1)
    a_last = _row(a, CHUNK - 1)
    q, dq = _silu_and_grad(qp)
    e1, e2, e3, e4 = jnp.exp(a - a_mid), jnp.exp(a_mid - a), jnp.exp(a), jnp.exp(a_last - a)
    return dict(sgm=sgm, f=f, k=k, q=q, dq=dq, e1=e1, e2=e2, e3=e3, e4=e4, dec=jnp.exp(a_last),
                q_in=q * e1, k_in=k * e2, q_out=q * e3, k_out=k * e4)


def _causal():
    r = lax.broadcasted_iota(jnp.int32, (CHUNK, CHUNK), 0)
    c = lax.broadcasted_iota(jnp.int32, (CHUNK, CHUNK), 1)
    return r >= c


def hgrn_fwd(proj4, lbraw, gn, seq, name):
    _, m_rows, di = proj4.shape
    nb, nh, nc = m_rows // seq, di // HEAD_DIM, seq // CHUNK
    rows = min(HG_ROWS, seq)
    wide = HG_WIDE * HEAD_DIM
    ns, cpb = seq // rows, rows // CHUNK

    def body(p_ref, lb_ref, gn_ref, y_ref, sts_ref, dec_ref, st_ref):
        @pl.when(pl.program_id(2) == 0)
        def _():
            st_ref[...] = jnp.zeros_like(st_ref)

        tri = _tri(True)
        causal = _causal()
        gain = gn_ref[...]
        lbs = [_lower_bound(lb_ref[:, j * HEAD_DIM:(j + 1) * HEAD_DIM])[0] for j in range(HG_WIDE)]

        units = [(n, j) for n in range(cpb) for j in range(HG_WIDE)]
        rs = lambda n: slice(n * CHUNK, (n + 1) * CHUNK)
        cs = lambda j: slice(j * HEAD_DIM, (j + 1) * HEAD_DIM)
        gates, v_b, sc_b, kv, o_in, o_x = {}, {}, {}, {}, {}, {}
        for n, j in units:
            sgm, a = _hgrn_decay(p_ref[1, rs(n), cs(j)], lbs[j], tri)
            dec_ref[0, rs(n), cs(j)] = sgm
            dec_ref[1, rs(n), cs(j)] = a
            gates[n, j] = _hgrn_gates(p_ref[0, rs(n), cs(j)], sgm, a, lbs[j])
            v_b[n, j] = p_ref[2, rs(n), cs(j)].astype(BF16)
        for u in units:
            t = gates[u]
            sc_b[u] = jnp.where(causal, _dot(t["q_in"].astype(BF16), t["k_in"].astype(BF16), NT), 0.0).astype(BF16)
            kv[u] = _dot(v_b[u], t["k_out"].astype(BF16), TN)
        for u in units:
            o_in[u] = _dot(sc_b[u], v_b[u], NN)
        for j in range(HG_WIDE):
            st = st_ref[j]
            for n in range(cpb):
                sts_ref[n, :, cs(j)] = st
                o_x[n, j] = _dot(gates[n, j]["q_out"].astype(BF16), st.astype(BF16), NT)
                st = st * gates[n, j]["dec"] + kv[n, j]
            st_ref[j] = st
        for n, j in units:
            o = o_in[n, j] + o_x[n, j]
            r = lax.rsqrt(jnp.mean(o * o, axis=-1, keepdims=True) + EPS)
            gp = p_ref[3, rs(n), cs(j)]
            y_ref[rs(n), cs(j)] = ((o * r * gain) * (gp * _sigmoid(gp))).astype(BF16)

    return _call(
        body, name=name, grid=(nh // HG_WIDE, nb, ns),
        out_shape=[jax.ShapeDtypeStruct((m_rows, di), BF16),
                   jax.ShapeDtypeStruct((nb * nc, HEAD_DIM, di), F32),
                   jax.ShapeDtypeStruct((2, m_rows, di), F32)],
        in_specs=[pl.BlockSpec((4, rows, wide), lambda hg, b, s: (0, b * ns + s, hg)),
                  pl.BlockSpec((2, wide), lambda hg, b, s: (0, hg)),
                  pl.BlockSpec((1, HEAD_DIM), lambda hg, b, s: (0, 0))],
        out_specs=[pl.BlockSpec((rows, wide), lambda hg, b, s: (b * ns + s, hg)),
                   pl.BlockSpec((cpb, HEAD_DIM, wide), lambda hg, b, s: (b * ns + s, 0, hg)),
                   pl.BlockSpec((2, rows, wide), lambda hg, b, s: (0, b * ns + s, hg))],
        scratch_shapes=[pltpu.VMEM((HG_WIDE, HEAD_DIM, HEAD_DIM), F32)],
        compiler_params=_params(),
    )(proj4, lbraw, gn)


def hgrn_bwd(proj4, dy, sts, decay, lbraw, gn, seq, name):
    _, m_rows, di = proj4.shape
    nb, nh, nc = m_rows // seq, di // HEAD_DIM, seq // CHUNK
    rows = min(HG_ROWS, seq)
    wide = HG_WIDE * HEAD_DIM
    ns, cpb = seq // rows, rows // CHUNK
    n_hg = nh // HG_WIDE

    def body(p_ref, dy_ref, sts_ref, dec_ref, lb_ref, gn_ref,
             dp_ref, dlb_ref, dgn_ref, dst_ref, lbacc_ref, gnacc_ref):
        hg, b, s = pl.program_id(0), pl.program_id(1), pl.program_id(2)
        triu = _tri(False)
        causal = _causal()
        gain = gn_ref[...]
        first = (b == 0) & (s == 0)
        cs = lambda j: slice(j * HEAD_DIM, (j + 1) * HEAD_DIM)

        def before():
            @pl.when((hg == 0) & first)
            def _():
                gnacc_ref[...] = jnp.zeros_like(gnacc_ref)

            @pl.when(first)
            def _():
                lbacc_ref[...] = jnp.zeros_like(lbacc_ref)

            @pl.when(s == 0)
            def _():
                dst_ref[...] = jnp.zeros_like(dst_ref)

        def after():
            @pl.when((b == nb - 1) & (s == ns - 1))
            def _():
                for j in range(HG_WIDE):
                    _, p0, p1 = _lower_bound(lb_ref[:, cs(j)])
                    acc = lbacc_ref[:, cs(j)]
                    dlb_ref[0:1, cs(j)] = -acc * p0 * p1
                    dlb_ref[1:2, cs(j)] = acc * p1 * (1.0 - p1)

            @pl.when((hg == n_hg - 1) & (b == nb - 1) & (s == ns - 1))
            def _():
                tot = gnacc_ref[:, 0:HEAD_DIM]
                for j in range(1, HG_WIDE):
                    tot = tot + gnacc_ref[:, cs(j)]
                dgn_ref[...] = tot

        before()

        units = [(n, j) for n in range(cpb) for j in range(HG_WIDE)]
        rs = lambda n: slice(n * CHUNK, (n + 1) * CHUNK)
        lbs = [_lower_bound(lb_ref[:, cs(j)])[0] for j in range(HG_WIDE)]
        gates, v_b, st_b, sc_b, o, do_b = {}, {}, {}, {}, {}, {}
        dq_out, dsc_b, dv, g_st, dq_in, dk_in, dst_at, dk_out, ddec = {}, {}, {}, {}, {}, {}, {}, {}, {}
        for n, j in units:
            gates[n, j] = _hgrn_gates(p_ref[0, rs(n), cs(j)], dec_ref[0, rs(n), cs(j)], dec_ref[1, rs(n), cs(j)], lbs[j])
            v_b[n, j] = p_ref[2, rs(n), cs(j)].astype(BF16)
            st_b[n, j] = sts_ref[n, :, cs(j)].astype(BF16)
        for u in units:
            t = gates[u]
            sc_b[u] = jnp.where(causal, _dot(t["q_in"].astype(BF16), t["k_in"].astype(BF16), NT), 0.0).astype(BF16)
        for u in units:
            o[u] = _dot(sc_b[u], v_b[u], NN) + _dot(gates[u]["q_out"].astype(BF16), st_b[u], NT)
        for n, j in units:
            ov = o[n, j]
            r = lax.rsqrt(jnp.mean(ov * ov, axis=-1, keepdims=True) + EPS)
            ohat = ov * r
            sg, dsg = _silu_and_grad(p_ref[3, rs(n), cs(j)])
            dyv = dy_ref[rs(n), cs(j)].astype(F32)
            dp_ref[3, rs(n), cs(j)] = (dyv * (ohat * gain) * dsg).astype(BF16)
            d_on = dyv * sg
            gnacc_ref[:, cs(j)] = gnacc_ref[:, cs(j)] + jnp.sum(d_on * ohat, axis=0, keepdims=True)
            dohat = d_on * gain
            do_b[n, j] = (r * (dohat - ohat * jnp.mean(dohat * ohat, axis=-1, keepdims=True))).astype(BF16)
        for u in units:
            dq_out[u] = _dot(do_b[u], st_b[u], NN)
            dsc_b[u] = jnp.where(causal, _dot(do_b[u], v_b[u], NT), 0.0).astype(BF16)
            dv[u] = _dot(sc_b[u], do_b[u], TN)
            g_st[u] = _dot(do_b[u], gates[u]["q_out"].astype(BF16), TN)
        for u in units:
            dq_in[u] = _dot(dsc_b[u], gates[u]["k_in"].astype(BF16), NN)
            dk_in[u] = _dot(dsc_b[u], gates[u]["q_in"].astype(BF16), TN)
        for j in range(HG_WIDE):
            dst = dst_ref[j]
            for n in reversed(range(cpb)):
                dst_at[n, j] = dst
                dst = dst * gates[n, j]["dec"] + g_st[n, j]
            dst_ref[j] = dst
        for n, j in units:
            dst = dst_at[n, j]
            dst_b = dst.astype(BF16)
            dk_out[n, j] = _dot(v_b[n, j], dst_b, NN)
            dv[n, j] = dv[n, j] + _dot(gates[n, j]["k_out"].astype(BF16), dst_b, NT)
            ddec[n, j] = jnp.sum(dst * sts_ref[n, :, cs(j)], axis=0, keepdims=True)
        for n, j in units:
            t = gates[n, j]
            dp_ref[2, rs(n), cs(j)] = dv[n, j].astype(BF16)
            dq = dq_in[n, j] * t["e1"] + dq_out[n, j] * t["e3"]
            dk = dk_in[n, j] * t["e2"] + dk_out[n, j] * t["e4"]
            w_in = dq_in[n, j] * t["q_in"] - dk_in[n, j] * t["k_in"]
            w_out = dk_out[n, j] * t["k_out"]
            da = w_in + dq_out[n, j] * t["q_out"] - w_out
            da_mid = -jnp.sum(w_in, axis=0, keepdims=True)
            da_last = jnp.sum(w_out, axis=0, keepdims=True) + ddec[n, j] * t["dec"]
            rid = lax.broadcasted_iota(jnp.int32, da.shape, 0)
            da = da + jnp.where(rid == CHUNK // 2 - 1, da_mid, 0.0) + jnp.where(rid == CHUNK - 1, da_last, 0.0)
            dlf = _dot(triu, da, NN, precision=lax.Precision.HIGHEST)
            df = dlf / t["f"] - dk
            sgm = t["sgm"]
            dp_ref[1, rs(n), cs(j)] = (df * (1.0 - lbs[j]) * sgm * (1.0 - sgm)).astype(BF16)
            lbacc_ref[:, cs(j)] = lbacc_ref[:, cs(j)] + jnp.sum(df * (1.0 - sgm), axis=0, keepdims=True)
            dp_ref[0, rs(n), cs(j)] = (dq * t["dq"]).astype(BF16)

        after()

    blk = lambda hg, b, s: b * ns + (ns - 1 - s)
    return _call(
        body, name=name, grid=(n_hg, nb, ns),
        out_shape=[jax.ShapeDtypeStruct((4, m_rows, di), BF16), jax.ShapeDtypeStruct((2, di), F32),
                   jax.ShapeDtypeStruct((1, HEAD_DIM), F32)],
        in_specs=[pl.BlockSpec((4, rows, wide), lambda hg, b, s: (0, blk(hg, b, s), hg)),
                  pl.BlockSpec((rows, wide), lambda hg, b, s: (blk(hg, b, s), hg)),
                  pl.BlockSpec((cpb, HEAD_DIM, wide), lambda hg, b, s: (blk(hg, b, s), 0, hg)),
                  pl.BlockSpec((2, rows, wide), lambda hg, b, s: (0, blk(hg, b, s), hg)),
                  pl.BlockSpec((2, wide), lambda hg, b, s: (0, hg)),
                  pl.BlockSpec((1, HEAD_DIM), lambda hg, b, s: (0, 0))],
        out_specs=[pl.BlockSpec((4, rows, wide), lambda hg, b, s: (0, blk(hg, b, s), hg)),
                   pl.BlockSpec((2, wide), lambda hg, b, s: (0, hg)),
                   pl.BlockSpec((1, HEAD_DIM), lambda hg, b, s: (0, 0))],
        scratch_shapes=[pltpu.VMEM((HG_WIDE, HEAD_DIM, HEAD_DIM), F32), pltpu.VMEM((1, wide), F32),
                        pltpu.VMEM((1, wide), F32)],
        compiler_params=_params(),
    )(proj4, dy, sts, decay, lbraw, gn)


def final_loss(x, fg, target, name):
    m_rows, d = x.shape
    tm = min(512, m_rows)

    def body(x_ref, fg_ref, t_ref, loss_ref, dx_ref, dfg_ref):
        i = pl.program_id(0)
        xv = x_ref[...]
        gain = fg_ref[...]
        r = lax.rsqrt(jnp.mean(xv * xv, axis=-1, keepdims=True) + EPS)
        xn = xv * r
        e = xn * gain - t_ref[...]
        part = 0.5 * jnp.sum(jnp.mean(e * e, axis=-1, keepdims=True), axis=0, keepdims=True)
        dyv = e / d
        p_fg = jnp.sum(dyv * xn, axis=0, keepdims=True)
        dxn = dyv * gain
        dx_ref[...] = r * (dxn - xn * jnp.mean(dxn * xn, axis=-1, keepdims=True))

        @pl.when(i == 0)
        def _():
            loss_ref[...] = part
            dfg_ref[...] = p_fg

        @pl.when(i != 0)
        def _():
            loss_ref[...] = loss_ref[...] + part
            dfg_ref[...] = dfg_ref[...] + p_fg

    row = pl.BlockSpec((tm, d), lambda i: (i, 0))
    return _call(
        body, name=name, grid=(m_rows // tm,),
        out_shape=[jax.ShapeDtypeStruct((1, 1), F32), jax.ShapeDtypeStruct((m_rows, d), F32),
                   jax.ShapeDtypeStruct((1, d), F32)],
        in_specs=[row, pl.BlockSpec((1, d), lambda i: (0, 0)), row],
        out_specs=[pl.BlockSpec((1, 1), lambda i: (0, 0)), row, pl.BlockSpec((1, d), lambda i: (0, 0))],
        compiler_params=_params(),
    )(x, fg, target)


def _pack(parts):
    flat = jnp.concatenate([p.reshape(-1) for p in parts])
    pad = (-flat.shape[0]) % (8 * LANES)
    return jnp.pad(flat, (0, pad)).reshape(-1, LANES)


def kernel(x, c, norm_gain, w_ada, b_ada, a_w_in, a_ln_gain, a_ln_bias, a_w_s, a_b_s, a_w_out, b_w_in, b_lower_bounds, b_gn_gain, b_w_out, final_gain, loss_target, m_norm_gain, m_w_ada, m_b_ada, m_a_w_in, m_a_ln_gain, m_a_ln_bias, m_a_w_s, m_a_b_s, m_a_w_out, m_b_w_in, m_b_lower_bounds, m_b_gn_gain, m_b_w_out, m_final_gain, v_norm_gain, v_w_ada, v_b_ada, v_a_w_in, v_a_ln_gain, v_a_ln_bias, v_a_w_s, v_a_b_s, v_a_w_out, v_b_w_in, v_b_lower_bounds, v_b_gn_gain, v_b_w_out, v_final_gain):
    nb, seq, d = x.shape
    m_rows = nb * seq
    n_l = w_ada.shape[0]
    ada_cols = w_ada.shape[2]
    px, py, pc = _place()
    chip = 2 * px + py
    dev = 2 * chip + pc

    c_all = allgather_small(c.reshape(-1, LANES), "gather_c").reshape(N_DEV * nb, d)
    b_cols = lax.dynamic_slice_in_dim(b_ada, chip * ada_cols, ada_cols, axis=1).reshape(n_l, 1, ada_cols)
    mod_cols = ada_fwd(c_all, w_ada, b_cols, "ada_fwd")
    mod_g = allgather_small(mod_cols.reshape(-1, LANES), "gather_mod")
    mod_g = mod_g.reshape(N_CHIPS, 2, n_l, N_DEV * nb, ada_cols)[:, 0]
    mod_all = jnp.transpose(mod_g, (1, 2, 0, 3)).reshape(n_l, N_DEV * nb, 3 * d)
    mod_mine = lax.dynamic_slice_in_dim(mod_all, dev * nb, nb, axis=1)
    mod0 = mod_mine[0].reshape(nb, 1, 3 * d)
    mod1 = mod_mine[1].reshape(nb, 1, 3 * d)

    (wa_in, wa_out), tok_a = gather_inplace(
        [cast_into_slot(a_w_in[0], chip, mod_mine, "cast_a_in"), cast_into_slot(a_w_out[0], chip, mod_mine, "cast_a_out")],
        "gather_a")
    s_bi = gather_start(cast_into_slot(b_w_in[0], chip, tok_a, "cast_b_in"), "gather_b_in_start")
    s_bo = gather_start(cast_into_slot(b_w_out[0], chip, s_bi[3], "cast_b_out"), "gather_b_out_start")
    di = a_w_out.shape[1] * N_CHIPS
    wa_out = wa_out.reshape(di, d)

    x0 = x.reshape(m_rows, d)
    tgt = loss_target.reshape(m_rows, d)
    ng0 = norm_gain[0:1] + (s_bi[3][0, 0] + s_bo[3][0, 0])
    ng1 = norm_gain[1:2]
    bs_col = a_b_s[0].reshape(SG_GROUPS, SG_BLOCK, 1)
    proj_a, h_a = inproj_fwd(x0, mod0, ng0, wa_in, seq, False, BF16, "a_inproj")
    y_a = sgu_fwd(proj_a, a_ln_gain, a_ln_bias, a_w_s[0], bs_col, "a_sgu")
    x1, out_a = outproj_fwd(y_a, wa_out, x0, mod0, seq, "a_outproj")
    wb_in = gather_wait(*s_bi[:3], out_a, "gather_b_in_wait")
    proj_b, h_b = inproj_fwd(x1, mod1, ng1, wb_in, seq, True, F32, "b_inproj")
    y_b, sts_b, decay_b = hgrn_fwd(proj_b, b_lower_bounds, b_gn_gain, seq, "b_hgrn")
    wb_out = gather_wait(*s_bo[:3], y_b, "gather_b_out_wait").reshape(di, d)
    x2, out_b = outproj_fwd(y_b, wb_out, x1, mod1, seq, "b_outproj")
    loss_part, dx2, dfg = final_loss(x2, final_gain.reshape(1, d), tgt, "loss_head")

    shard_rows = di // N_CHIPS
    dy_b, dout_b, dgate1 = outproj_bwd(dx2, out_b, mod1, wb_out, seq, "b_outproj_bwd")
    gwb_out = grad_w_out(y_b, dout_b, "b_grad_w_out").reshape(N_CHIPS, shard_rows, d)
    e_bo = exchange_start(gwb_out, "exchange_b_out_start")
    dproj_b, dlb, dgn = hgrn_bwd(
        proj_b, dy_b, sts_b, decay_b, b_lower_bounds, b_gn_gain + e_bo[4][0, 0], seq, "b_hgrn_bwd")
    e_bi = exchange_start(grad_w_in(h_b, dproj_b, N_CHIPS, True, "b_grad_w_in"), "exchange_b_in_start")
    dx1, dshift1, dscale1, dng1 = inproj_bwd(
        dproj_b, wb_in, x1, dx2, mod1, ng1 + e_bi[4][0, 0], seq, True, "b_inproj_bwd")

    dy_a, dout_a, dgate0 = outproj_bwd(dx1, out_a, mod0, wa_out, seq, "a_outproj_bwd")
    gwa_out = grad_w_out(y_a, dout_a, "a_grad_w_out").reshape(N_CHIPS, shard_rows, d)
    e_ao = exchange_start(gwa_out, "exchange_a_out_start")
    dproj_a, dws, dbs, dlg, dlbias = sgu_bwd(
        proj_a, dy_a, a_ln_gain + e_ao[4][0, 0], a_ln_bias, a_w_s[0], bs_col, "a_sgu_bwd")
    e_ai = exchange_start(grad_w_in(h_a, dproj_a, N_CHIPS, False, "a_grad_w_in"), "exchange_a_in_start")
    dx0, dshift0, dscale0, dng0 = inproj_bwd(
        dproj_a, wa_in, x0, dx1, mod0, norm_gain[0:1] + e_ai[4][0, 0], seq, False, "a_inproj_bwd")
    grad_x = dx0.reshape(nb, seq, d)

    dmod = jnp.concatenate([dshift0, dscale0, dgate0, dshift1, dscale1, dgate1], axis=2)
    n_dmod = dmod.size
    small_g = [jnp.concatenate([dng0, dng1], axis=0), dlg, dlbias, dws, dbs, dlb, dfg, dgn]
    packed_g = _pack([dmod] + small_g + [loss_part])
    rows = packed_g.shape[0]
    s_small = gather_all_start(
        lax.dynamic_update_slice(jnp.zeros((N_DEV, rows, LANES), F32), packed_g[None], (dev, 0, 0)),
        "gather_small_start")

    def finish(group, after):
        mine = []
        for ex, _, _, _, nm in group:
            parts_thru, land = exchange_wait(ex[0], ex[1], ex[2], ex[3], after, "exchange_" + nm + "_wait")
            mine.append(sum_parts(parts_thru, land, chip, "sum_" + nm))
            after = mine[-1]
        theirs = swap_sibling(mine, "swap_" + group[0][4])
        return [[r.reshape(w.shape) for r in adamw_pair(pa, pb, w[0], m[0], v[0], "adamw_" + nm)]
                for pa, pb, (_, w, m, v, nm) in zip(mine, theirs, group)]

    (gb_out, db_out, mb_out, vb_out), (gb_in, db_in, mb_in, vb_in), (ga_out, da_out, ma_out, va_out) = finish(
        [(e_bo, b_w_out, m_b_w_out, v_b_w_out, "b_out"), (e_bi, b_w_in, m_b_w_in, v_b_w_in, "b_in"),
         (e_ao, a_w_out, m_a_w_out, v_a_w_out, "a_out")], s_small[3])
    ((ga_in, da_in, ma_in, va_in),) = finish([(e_ai, a_w_in, m_a_w_in, v_a_w_in, "a_in")], ga_out)

    small_w = [norm_gain, a_ln_gain, a_ln_bias, a_w_s, a_b_s, b_lower_bounds, final_gain, b_gn_gain]
    small_m = [m_norm_gain, m_a_ln_gain, m_a_ln_bias, m_a_w_s, m_a_b_s, m_b_lower_bounds, m_final_gain, m_b_gn_gain]
    small_v = [v_norm_gain, v_a_ln_gain, v_a_ln_bias, v_a_w_s, v_a_b_s, v_b_lower_bounds, v_final_gain, v_b_gn_gain]
    rows_of = lambda a: a.reshape(-1, a.shape[-1])
    gathered = gather_all_wait(s_small[0], s_small[1], s_small[2], ga_in, "gather_small_wait")
    tail, small_res = small_update(
        gathered, n_dmod // LANES, [rows_of(a) for a in small_w], [rows_of(a) for a in small_m],
        [rows_of(a) for a in small_v], "small_update")
    loss = tail[0, 0]
    sg, sd, sm, sv = [[small_res[p][kind].reshape(w.shape) for p, w in enumerate(small_w)] for kind in range(4)]

    dmod_all = gathered[:, :n_dmod // LANES].reshape(N_DEV * nb, n_l, 3 * d)
    dmod_cols = lax.dynamic_slice_in_dim(dmod_all, chip * ada_cols, ada_cols, axis=2)
    dmod_cols = jnp.transpose(dmod_cols, (1, 0, 2))
    g_wada, d_wada, m_wada, v_wada = ada_bwd(c_all, dmod_cols, w_ada, m_w_ada, v_w_ada, "ada_bwd")
    flat = lambda a: a.reshape(1, -1)
    g_bada, d_bada, m_bada, v_bada = [
        r.reshape(b_ada.shape) for r in
        bias_update(dmod_all.reshape(N_DEV * nb, n_l * 3 * d), flat(b_ada), flat(m_b_ada), flat(v_b_ada), "bias_update")]

    def order(ng, wada, bada, ain, sm_rest, aout, bin_, bout):
        lg, lbi, ws_, bs_, lbd, fg_, gn_ = sm_rest
        return [ng, wada, bada, ain, lg, lbi, ws_, bs_, aout, bin_, lbd, gn_, bout, fg_]

    grads = order(sg[0], g_wada, g_bada, ga_in, sg[1:8], ga_out, gb_in, gb_out)
    deltas = order(sd[0], d_wada, d_bada, da_in, sd[1:8], da_out, db_in, db_out)
    new_m = order(sm[0], m_wada, m_bada, ma_in, sm[1:8], ma_out, mb_in, mb_out)
    new_v = order(sv[0], v_wada, v_bada, va_in, sv[1:8], va_out, vb_in, vb_out)
    return (loss, grad_x, *grads, *deltas, *new_m, *new_v)
```

```python
import functools

import jax
import jax.numpy as jnp
from jax import lax
from jax.experimental import pallas as pl
from jax.experimental.pallas import tpu as pltpu

F32 = jnp.float32
BF16 = jnp.bfloat16
EPS = 1e-6
CHUNK = 64
SG_BLOCK = 128
SG_GROUPS = 8
HEAD_DIM = 128
HG_WIDE = 8
HG_ROWS = 256
N_CHIPS = 4
N_DEV = 8
LANES = 128
ADAM_LR = 0.001
ADAM_B1 = 0.9
ADAM_B2 = 0.999
ADAM_EPS = 1e-08
ADAM_WD = 0.01
ADAM_STEP = 10
GELU_C0 = 0.7978845608028654
GELU_C1 = 0.044715
MESH = pl.DeviceIdType.MESH
VMEM_LIMIT = 56 * 1024 * 1024


ROW_TILE = 1024


def _col_tile(n):
    return next(t for t in (1024, 768, 512, 256) if n % t == 0)


def _call(body, **kw):
    return pl.pallas_call(body, **kw)


def _params(**kw):
    return pltpu.CompilerParams(vmem_limit_bytes=VMEM_LIMIT, **kw)


def _sigmoid(x):
    return 0.5 * jnp.tanh(0.5 * x) + 0.5


def _sigmoid_small(x):
    return 1.0 / (1.0 + jnp.exp(-x))


def _silu_and_grad(x):
    s = _sigmoid(x)
    return x * s, s * (1.0 + x * (1.0 - s))


def _gelu(x):
    return 0.5 * x * (1.0 + jnp.tanh(GELU_C0 * (x + GELU_C1 * x * x * x)))


def _gelu_and_grad(x):
    t = jnp.tanh(GELU_C0 * (x + GELU_C1 * x * x * x))
    g = 0.5 * x * (1.0 + t)
    dg = 0.5 * (1.0 + t) + 0.5 * x * (1.0 - t * t) * (GELU_C0 * (1.0 + 3.0 * GELU_C1 * x * x))
    return g, dg


def _dot(a, b, dims, precision=None):
    return lax.dot_general(a, b, (dims, ((), ())), precision=precision, preferred_element_type=F32)


NN = ((1,), (0,))
NT = ((1,), (1,))
TN = ((0,), (0,))


def _adamw(w, g, m, v):
    m = ADAM_B1 * m + (1.0 - ADAM_B1) * g
    v = ADAM_B2 * v + (1.0 - ADAM_B2) * (g * g)
    m_hat = m / (1.0 - ADAM_B1 ** ADAM_STEP)
    v_hat = v / (1.0 - ADAM_B2 ** ADAM_STEP)
    delta = -ADAM_LR * (m_hat / (jnp.sqrt(v_hat) + ADAM_EPS) + ADAM_WD * w)
    return delta, m, v


def _chunk_mask():
    r = lax.broadcasted_iota(jnp.int32, (SG_BLOCK, SG_BLOCK), 0)
    c = lax.broadcasted_iota(jnp.int32, (SG_BLOCK, SG_BLOCK), 1)
    return (c // CHUNK) <= (r // CHUNK)


def _place():
    return lax.axis_index("x"), lax.axis_index("y"), lax.axis_index("c")


def _other_chips(x, y):
    return [(1 - x, y), (x, 1 - y), (1 - x, 1 - y)]


def allgather_small(v, name):
    m_per, n = v.shape

    def body(x_ref, out_ref, send_sems, recv_sems, local_sem):
        x, y, c = _place()
        me, sibling = (x, y, c), (x, y, 1 - c)
        chips = _other_chips(x, y)

        def rows(px, py, pc):
            return out_ref.at[pl.ds((4 * px + 2 * py + pc) * m_per, m_per), :]

        def copy(k, block, to, src=None):
            return pltpu.make_async_remote_copy(
                src_ref=rows(*block) if src is None else src, dst_ref=rows(*block),
                send_sem=send_sems.at[k], recv_sem=recv_sems.at[k], device_id=to, device_id_type=MESH)

        mine = pltpu.make_async_copy(x_ref, rows(*me), local_sem)
        mine.start()
        first = [copy(0, me, sibling, src=x_ref)]
        first += [copy(1 + j, me, (*chip, c), src=x_ref) for j, chip in enumerate(chips)]
        for cp in first:
            cp.start()
        passed = [copy(4 + j, (*chip, c), sibling) for j, chip in enumerate(chips)]
        for j, chip in enumerate(chips):
            copy(1 + j, (*chip, c), me).wait_recv()
            passed[j].start()
        copy(0, sibling, me).wait_recv()
        for j, chip in enumerate(chips):
            copy(4 + j, (*chip, 1 - c), me).wait_recv()
        for cp in first + passed:
            cp.wait_send()
        mine.wait()

    return _call(
        body, name=name,
        out_shape=jax.ShapeDtypeStruct((N_DEV * m_per, n), v.dtype),
        in_specs=[pl.BlockSpec(memory_space=pltpu.VMEM)],
        out_specs=pl.BlockSpec(memory_space=pltpu.VMEM),
        scratch_shapes=[pltpu.SemaphoreType.DMA((7,)), pltpu.SemaphoreType.DMA((7,)), pltpu.SemaphoreType.DMA],
    )(v)


def _hbm_spec():
    return pl.BlockSpec(memory_space=pltpu.HBM)


def _sem_spec():
    return pl.BlockSpec(memory_space=pltpu.SEMAPHORE)


def _split_params():
    return pltpu.CompilerParams(has_side_effects=pltpu.SideEffectType.DATAFLOW_SIDE_EFFECTING)


def _hbm(a):
    return pltpu.with_memory_space_constraint(a, pltpu.HBM)


def gather_inplace(lands, name):
    n = len(lands)

    def body(*refs):
        land_refs, token = refs[n:2 * n], refs[2 * n]
        send_sems, recv_sems = refs[2 * n + 1:]
        x, y, c = _place()
        chips = _other_chips(x, y)

        def copy(w, k, chip_idx, core_half, to):
            half = lands[w].shape[1] // 2
            rows = land_refs[w].at[chip_idx, pl.ds(core_half * half, half), :]
            return pltpu.make_async_remote_copy(
                src_ref=rows, dst_ref=rows, send_sem=send_sems.at[6 * w + k], recv_sem=recv_sems.at[6 * w + k],
                device_id=to, device_id_type=MESH)

        first = [copy(w, j, 2 * x + y, c, (px, py, c)) for w in range(n) for j, (px, py) in enumerate(chips)]
        for cp in first:
            cp.start()
        passed = []
        for w in range(n):
            for j, (px, py) in enumerate(chips):
                copy(w, j, 2 * px + py, c, (px, py, c)).wait_recv()
                passed.append(copy(w, 3 + j, 2 * px + py, c, (x, y, 1 - c)))
                passed[-1].start()
        for w in range(n):
            for j, (px, py) in enumerate(chips):
                copy(w, 3 + j, 2 * px + py, 1 - c, (x, y, 1 - c)).wait_recv()
        for cp in first + passed:
            cp.wait_send()
        token[...] = jnp.zeros_like(token)

    res = _call(
        body, name=name,
        out_shape=[jax.ShapeDtypeStruct(a.shape, a.dtype) for a in lands] + [jax.ShapeDtypeStruct((8, LANES), F32)],
        in_specs=[_hbm_spec()] * n, out_specs=[_hbm_spec()] * n + [pl.BlockSpec(memory_space=pltpu.VMEM)],
        input_output_aliases={w: w for w in range(n)},
        scratch_shapes=[pltpu.SemaphoreType.DMA((6 * n,)), pltpu.SemaphoreType.DMA((6 * n,))],
    )(*lands)
    return res[:n], res[n]


def gather_start(land, name):
    def body(land_ref, send_sems, recv_sems, land_thru, token):
        del land_thru
        x, y, c = _place()
        for j, (px, py) in enumerate(_other_chips(x, y)):
            pltpu.make_async_remote_copy(
                src_ref=land_ref.at[2 * x + y], dst_ref=land_ref.at[2 * x + y],
                send_sem=send_sems.at[j], recv_sem=recv_sems.at[j], device_id=(px, py, c),
                device_id_type=MESH).start()
        token[...] = jnp.zeros_like(token)

    return _call(
        body, name=name,
        out_shape=(pltpu.SemaphoreType.DMA((3,)), pltpu.SemaphoreType.DMA((3,)),
                   pltpu.HBM(land.shape, land.dtype), jax.ShapeDtypeStruct((8, LANES), F32)),
        in_specs=(_hbm_spec(),),
        out_specs=(_sem_spec(), _sem_spec(), _hbm_spec(), pl.BlockSpec(memory_space=pltpu.VMEM)),
        input_output_aliases={0: 2}, compiler_params=_split_params(),
    )(_hbm(land))


def gather_wait(send_sems, recv_sems, land, after, name):
    def body(land_ref, send_sems, recv_sems, after_ref, land_out):
        del after_ref, land_out
        x, y, c = _place()
        for j, (px, py) in enumerate(_other_chips(x, y)):
            cp = pltpu.make_async_remote_copy(
                src_ref=land_ref.at[2 * x + y], dst_ref=land_ref.at[2 * px + py],
                send_sem=send_sems.at[j], recv_sem=recv_sems.at[j], device_id=(px, py, c), device_id_type=MESH)
            cp.wait_send()
            cp.wait_recv()

    return _call(
        body, name=name,
        out_shape=pltpu.HBM(land.shape, land.dtype),
        in_specs=(_hbm_spec(), _sem_spec(), _sem_spec(), pl.BlockSpec(memory_space=pl.ANY)),
        out_specs=_hbm_spec(), input_output_aliases={0: 0}, compiler_params=_split_params(),
    )(land, send_sems, recv_sems, after)


def _flips():
    return [(fx, fy, fc) for fx in (0, 1) for fy in (0, 1) for fc in (0, 1) if (fx, fy, fc) != (0, 0, 0)]


def _flipped(x, y, c, flip):
    fx, fy, fc = flip
    return (1 - x if fx else x, 1 - y if fy else y, 1 - c if fc else c)


def gather_all_start(land, name):
    def body(land_ref, send_sems, recv_sems, land_thru, token):
        del land_thru
        x, y, c = _place()
        for k, flip in enumerate(_flips()):
            pltpu.make_async_remote_copy(
                src_ref=land_ref.at[4 * x + 2 * y + c], dst_ref=land_ref.at[4 * x + 2 * y + c],
                send_sem=send_sems.at[k], recv_sem=recv_sems.at[k], device_id=_flipped(x, y, c, flip),
                device_id_type=MESH).start()
        token[...] = jnp.zeros_like(token)

    return _call(
        body, name=name,
        out_shape=(pltpu.SemaphoreType.DMA((7,)), pltpu.SemaphoreType.DMA((7,)),
                   pltpu.HBM(land.shape, land.dtype), jax.ShapeDtypeStruct((8, LANES), F32)),
        in_specs=(_hbm_spec(),),
        out_specs=(_sem_spec(), _sem_spec(), _hbm_spec(), pl.BlockSpec(memory_space=pltpu.VMEM)),
        input_output_aliases={0: 2}, compiler_params=_split_params(),
    )(_hbm(land))


def gather_all_wait(send_sems, recv_sems, land, after, name):
    def body(land_ref, send_sems, recv_sems, after_ref, land_out):
        del after_ref, land_out
        x, y, c = _place()
        for k, flip in enumerate(_flips()):
            px, py, pc = _flipped(x, y, c, flip)
            cp = pltpu.make_async_remote_copy(
                src_ref=land_ref.at[4 * x + 2 * y + c], dst_ref=land_ref.at[4 * px + 2 * py + pc],
                send_sem=send_sems.at[k], recv_sem=recv_sems.at[k], device_id=(px, py, pc), device_id_type=MESH)
            cp.wait_send()
            cp.wait_recv()

    return _call(
        body, name=name,
        out_shape=pltpu.HBM(land.shape, land.dtype),
        in_specs=(_hbm_spec(), _sem_spec(), _sem_spec(), pl.BlockSpec(memory_space=pl.ANY)),
        out_specs=_hbm_spec(), input_output_aliases={0: 0}, compiler_params=_split_params(),
    )(land, send_sems, recv_sems, after)


def exchange_start(parts, name):
    _, r, c_ = parts.shape

    def body(parts_ref, land_ref, send_sems, recv_sems, parts_thru, land_thru, token):
        del parts_thru, land_thru
        x, y, c = _place()
        for j, (px, py) in enumerate(_other_chips(x, y)):
            pltpu.make_async_remote_copy(
                src_ref=parts_ref.at[2 * px + py], dst_ref=land_ref.at[j],
                send_sem=send_sems.at[j], recv_sem=recv_sems.at[j], device_id=(px, py, c),
                device_id_type=MESH).start()
        token[...] = jnp.zeros_like(token)

    return _call(
        body, name=name,
        out_shape=(pltpu.SemaphoreType.DMA((3,)), pltpu.SemaphoreType.DMA((3,)),
                   pltpu.HBM(parts.shape, parts.dtype), pltpu.HBM((3, r, c_), parts.dtype),
                   jax.ShapeDtypeStruct((8, LANES), F32)),
        in_specs=(_hbm_spec(), _hbm_spec()),
        out_specs=(_sem_spec(), _sem_spec(), _hbm_spec(), _hbm_spec(), pl.BlockSpec(memory_space=pltpu.VMEM)),
        input_output_aliases={0: 2, 1: 3}, compiler_params=_split_params(),
    )(_hbm(parts), _hbm(lax.empty((3, r, c_), parts.dtype)))


def exchange_wait(send_sems, recv_sems, parts, land, after, name):
    def body(parts_ref, land_ref, send_sems, recv_sems, after_ref, parts_out, land_out):
        del after_ref, parts_out, land_out
        x, y, c = _place()
        for j, (px, py) in enumerate(_other_chips(x, y)):
            cp = pltpu.make_async_remote_copy(
                src_ref=parts_ref.at[2 * px + py], dst_ref=land_ref.at[j],
                send_sem=send_sems.at[j], recv_sem=recv_sems.at[j], device_id=(px, py, c), device_id_type=MESH)
            cp.wait_send()
            cp.wait_recv()

    return _call(
        body, name=name,
        out_shape=(pltpu.HBM(parts.shape, parts.dtype), pltpu.HBM(land.shape, land.dtype)),
        in_specs=(_hbm_spec(), _hbm_spec(), _sem_spec(), _sem_spec(), pl.BlockSpec(memory_space=pl.ANY)),
        out_specs=(_hbm_spec(), _hbm_spec()), input_output_aliases={0: 0, 1: 1},
        compiler_params=_split_params(),
    )(parts, land, send_sems, recv_sems, after)


def cast_into_slot(w, chip, after, name):
    r, c = w.shape
    tr = min(256, r)

    def body(s_ref, w_ref, after_ref, o_ref):
        del s_ref, after_ref
        o_ref[...] = w_ref[...].astype(BF16)

    return _call(
        body, name=name,
        grid_spec=pltpu.PrefetchScalarGridSpec(
            num_scalar_prefetch=1, grid=(r // tr,),
            in_specs=[pl.BlockSpec((tr, c), lambda i, s: (i, 0)), pl.BlockSpec(memory_space=pl.ANY)],
            out_specs=pl.BlockSpec((None, tr, c), lambda i, s: (s[0], i, 0))),
        out_shape=jax.ShapeDtypeStruct((N_CHIPS, r, c), BF16),
        compiler_params=_params(),
    )(chip.reshape(1).astype(jnp.int32), w, after)


def sum_parts(parts, land, chip, name):
    _, r, c = parts.shape
    tr = min(256, r)

    def body(s_ref, p_ref, l_ref, o_ref):
        del s_ref
        acc = p_ref[...].astype(F32) + l_ref[0].astype(F32)
        acc = acc + l_ref[1].astype(F32)
        o_ref[...] = acc + l_ref[2].astype(F32)

    return _call(
        body, name=name,
        grid_spec=pltpu.PrefetchScalarGridSpec(
            num_scalar_prefetch=1, grid=(r // tr,),
            in_specs=[pl.BlockSpec((None, tr, c), lambda i, s: (s[0], i, 0)),
                      pl.BlockSpec((3, tr, c), lambda i, s: (0, i, 0))],
            out_specs=pl.BlockSpec((tr, c), lambda i, s: (i, 0))),
        out_shape=jax.ShapeDtypeStruct((r, c), F32),
        compiler_params=_params(),
    )(chip.reshape(1).astype(jnp.int32), parts, land)


def swap_sibling(arrs, name):
    n = len(arrs)

    def body(*refs):
        ins, outs = refs[:n], refs[n:2 * n]
        send_sems, recv_sems = refs[2 * n:]
        x, y, c = _place()
        cps = []
        for w in range(n):
            cp = pltpu.make_async_remote_copy(
                src_ref=ins[w], dst_ref=outs[w], send_sem=send_sems.at[w], recv_sem=recv_sems.at[w],
                device_id=(x, y, 1 - c), device_id_type=MESH)
            cp.start()
            cps.append(cp)
        for cp in cps:
            cp.wait_recv()
        for cp in cps:
            cp.wait_send()

    return _call(
        body, name=name,
        out_shape=[jax.ShapeDtypeStruct(a.shape, a.dtype) for a in arrs],
        in_specs=[_hbm_spec()] * n, out_specs=[_hbm_spec()] * n,
        scratch_shapes=[pltpu.SemaphoreType.DMA((n,)), pltpu.SemaphoreType.DMA((n,))],
    )(*arrs)


def adamw_pair(pa, pb, w, m, v, name):
    r, c = w.shape
    tr = min(128, r)

    def body(pa_ref, pb_ref, w_ref, m_ref, v_ref, g_ref, d_ref, nm_ref, nv_ref):
        g = pa_ref[...] + pb_ref[...]
        d, nm, nv = _adamw(w_ref[...], g, m_ref[...], v_ref[...])
        g_ref[...] = g
        d_ref[...] = d
        nm_ref[...] = nm
        nv_ref[...] = nv

    spec = pl.BlockSpec((tr, c), lambda i: (i, 0))
    return _call(
        body, name=name, grid=(r // tr,),
        out_shape=[jax.ShapeDtypeStruct((r, c), F32)] * 4,
        in_specs=[spec] * 5, out_specs=[spec] * 4,
        compiler_params=_params(),
    )(pa, pb, w, m, v)


def small_update(gathered, first_row, ws, ms, vs, name):
    n_w = len(ws)
    total_rows = gathered.shape[1]

    def body(*refs):
        g_ref = refs[0]
        w_refs, m_refs, v_refs = refs[1:1 + n_w], refs[1 + n_w:1 + 2 * n_w], refs[1 + 2 * n_w:1 + 3 * n_w]
        tail_ref = refs[1 + 3 * n_w]
        outs = refs[2 + 3 * n_w:2 + 7 * n_w]
        sum_ref = refs[2 + 7 * n_w]
        acc = g_ref[0]
        for k in range(1, N_DEV):
            acc = acc + g_ref[k]
        sum_ref[...] = acc
        row = first_row
        for p in range(n_w):
            a, b = ws[p].shape
            per = b // LANES
            g_out, d_out, m_out, v_out = outs[4 * p:4 * p + 4]
            if per == 1:
                g_out[...] = sum_ref[row:row + a, :]
            else:
                for i in range(a):
                    for jc in range(per):
                        g_out[i:i + 1, jc * LANES:(jc + 1) * LANES] = sum_ref[row + i * per + jc:row + i * per + jc + 1, :]
            row += a * per
            dl, nm, nv = _adamw(w_refs[p][...], g_out[...], m_refs[p][...], v_refs[p][...])
            d_out[...] = dl
            m_out[...] = nm
            v_out[...] = nv
        tail_ref[...] = sum_ref[row:row + 1, :]

    out_shape = [jax.ShapeDtypeStruct((1, LANES), F32)]
    for w in ws:
        out_shape += [jax.ShapeDtypeStruct(w.shape, F32)] * 4
    res = _call(
        body, name=name, out_shape=out_shape,
        scratch_shapes=[pltpu.VMEM((total_rows, LANES), F32)],
        compiler_params=_params(),
    )(gathered, *ws, *ms, *vs)
    return res[0], [res[1 + 4 * p:5 + 4 * p] for p in range(n_w)]


def ada_fwd(c_all, w_ada, b_cols, name):
    n_l, d, cols = w_ada.shape
    nb = c_all.shape[0]
    tn = 256

    def body(c_ref, w_ref, b_ref, o_ref):
        cv = c_ref[...]
        ca = (cv * _sigmoid(cv)).astype(BF16)
        o_ref[...] = _dot(ca, w_ref[...].astype(BF16), NN) + b_ref[...]

    return _call(
        body, name=name, grid=(n_l, cols // tn),
        out_shape=jax.ShapeDtypeStruct((n_l, nb, cols), F32),
        in_specs=[pl.BlockSpec((nb, d), lambda l, j: (0, 0)),
                  pl.BlockSpec((None, d, tn), lambda l, j: (l, 0, j)),
                  pl.BlockSpec((None, 1, tn), lambda l, j: (l, 0, j))],
        out_specs=pl.BlockSpec((None, nb, tn), lambda l, j: (l, 0, j)),
        compiler_params=_params(),
    )(c_all, w_ada, b_cols)


def ada_bwd(c_all, dmod_cols, w, m, v, name):
    n_l, d, cols = w.shape
    nb = c_all.shape[0]
    tn = 256

    def body(c_ref, dm_ref, w_ref, m_ref, v_ref, g_ref, d_ref, nm_ref, nv_ref):
        cv = c_ref[...]
        ca = (cv * _sigmoid(cv)).astype(BF16)
        g = _dot(ca, dm_ref[...].astype(BF16), TN)
        dl, nm, nv = _adamw(w_ref[...], g, m_ref[...], v_ref[...])
        g_ref[...] = g
        d_ref[...] = dl
        nm_ref[...] = nm
        nv_ref[...] = nv

    wspec = pl.BlockSpec((None, d, tn), lambda l, j: (l, 0, j))
    return _call(
        body, name=name, grid=(n_l, cols // tn),
        out_shape=[jax.ShapeDtypeStruct((n_l, d, cols), F32)] * 4,
        in_specs=[pl.BlockSpec((nb, d), lambda l, j: (0, 0)),
                  pl.BlockSpec((None, nb, tn), lambda l, j: (l, 0, j)),
                  wspec, wspec, wspec],
        out_specs=[wspec] * 4,
        compiler_params=_params(),
    )(c_all, dmod_cols, w, m, v)


def bias_update(dmod_all, w, m, v, name):
    def body(dm_ref, w_ref, m_ref, v_ref, g_ref, d_ref, nm_ref, nv_ref):
        g = jnp.sum(dm_ref[...], axis=0, keepdims=True)
        dl, nm, nv = _adamw(w_ref[...], g, m_ref[...], v_ref[...])
        g_ref[...] = g
        d_ref[...] = dl
        nm_ref[...] = nm
        nv_ref[...] = nv

    return _call(
        body, name=name,
        out_shape=[jax.ShapeDtypeStruct(w.shape, F32)] * 4,
        compiler_params=_params(),
    )(dmod_all, w, m, v)


def inproj_fwd(x, mod, ng, wg, seq, sectioned, name):
    m_rows, d = x.shape
    nsh, _, ns = wg.shape
    n = nsh * ns
    tm, tn = min(ROW_TILE, seq), ns
    per = ns // tn

    def body(x_ref, mod_ref, ng_ref, w_ref, proj_ref, h_ref):
        @pl.when(pl.program_id(1) == 0)
        def _():
            xv = x_ref[...]
            r = lax.rsqrt(jnp.mean(xv * xv, axis=-1, keepdims=True) + EPS)
            md = mod_ref[0]
            h = (xv * r * ng_ref[...]) * (1.0 + md[:, d:2 * d]) + md[:, :d]
            h_ref[...] = h.astype(BF16)
        proj_ref[...] = _dot(h_ref[...], w_ref[...], NN)

    if sectioned:
        proj_shape = (nsh, m_rows, ns)
        proj_spec = pl.BlockSpec((None, tm, tn), lambda i, j: (j // per, i, j % per))
    else:
        proj_shape = (m_rows, n)
        proj_spec = pl.BlockSpec((tm, tn), lambda i, j: (i, j))
    return _call(
        body, name=name, grid=(m_rows // tm, n // tn),
        out_shape=[jax.ShapeDtypeStruct(proj_shape, F32), jax.ShapeDtypeStruct((m_rows, d), BF16)],
        in_specs=[pl.BlockSpec((tm, d), lambda i, j: (i, 0)),
                  pl.BlockSpec((1, 1, 3 * d), lambda i, j: ((i * tm) // seq, 0, 0)),
                  pl.BlockSpec((1, d), lambda i, j: (0, 0)),
                  pl.BlockSpec((None, d, tn), lambda i, j: (j // per, 0, j % per))],
        out_specs=[proj_spec, pl.BlockSpec((tm, d), lambda i, j: (i, 0))],
        compiler_params=_params(),
    )(x, mod, ng, wg)


def outproj_fwd(y, w, x, mod, seq, name):
    m_rows, di = y.shape
    d = w.shape[1]
    tm = min(ROW_TILE, seq)

    def body(y_ref, w_ref, x_ref, mod_ref, xn_ref, out_ref):
        acc = _dot(y_ref[...], w_ref[...], NN)
        out_ref[...] = acc.astype(BF16)
        xn_ref[...] = x_ref[...] + mod_ref[0][:, 2 * d:] * acc

    row = pl.BlockSpec((tm, d), lambda i: (i, 0))
    return _call(
        body, name=name, grid=(m_rows // tm,),
        out_shape=[jax.ShapeDtypeStruct((m_rows, d), F32), jax.ShapeDtypeStruct((m_rows, d), BF16)],
        in_specs=[pl.BlockSpec((tm, di), lambda i: (i, 0)),
                  pl.BlockSpec((di, d), lambda i: (0, 0)),
                  row,
                  pl.BlockSpec((1, 1, 3 * d), lambda i: ((i * tm) // seq, 0, 0))],
        out_specs=[row, row],
        compiler_params=_params(),
    )(y, w, x, mod)


def outproj_bwd(dxo, out, mod, w, seq, name):
    m_rows, d = dxo.shape
    di = w.shape[0]
    nb = m_rows // seq
    tm, tn = min(ROW_TILE, seq), _col_tile(di)

    def body(dxo_ref, out_ref, mod_ref, w_ref, dy_ref, dout_ref, dgate_ref):
        i = pl.program_id(0)

        @pl.when(pl.program_id(1) == 0)
        def _():
            dx = dxo_ref[...]
            dout_ref[...] = (mod_ref[0][:, 2 * d:] * dx).astype(BF16)
            part = jnp.sum(dx * out_ref[...].astype(F32), axis=0, keepdims=True)

            @pl.when((i * tm) % seq == 0)
            def _():
                dgate_ref[0] = part

            @pl.when((i * tm) % seq != 0)
            def _():
                dgate_ref[0] = dgate_ref[0] + part

        dy_ref[...] = _dot(dout_ref[...], w_ref[...], NT).astype(BF16)

    row = pl.BlockSpec((tm, d), lambda i, j: (i, 0))
    return _call(
        body, name=name, grid=(m_rows // tm, di // tn),
        out_shape=[jax.ShapeDtypeStruct((m_rows, di), BF16), jax.ShapeDtypeStruct((m_rows, d), BF16),
                   jax.ShapeDtypeStruct((nb, 1, d), F32)],
        in_specs=[row, row,
                  pl.BlockSpec((1, 1, 3 * d), lambda i, j: ((i * tm) // seq, 0, 0)),
                  pl.BlockSpec((tn, d), lambda i, j: (j, 0))],
        out_specs=[pl.BlockSpec((tm, tn), lambda i, j: (i, j)), row,
                   pl.BlockSpec((1, 1, d), lambda i, j: ((i * tm) // seq, 0, 0))],
        compiler_params=_params(),
    )(dxo, out, mod, w)


def grad_w_out(y, dout, name):
    m_rows, di = y.shape
    d = dout.shape[1]
    tm, tk = min(ROW_TILE, m_rows), _col_tile(di)
    n_m = m_rows // tm

    def body(y_ref, do_ref, o_ref, acc_ref):
        mi = pl.program_id(1)

        @pl.when(mi == 0)
        def _():
            acc_ref[...] = jnp.zeros_like(acc_ref)

        acc_ref[...] += _dot(y_ref[...], do_ref[...], TN)

        @pl.when(mi == n_m - 1)
        def _():
            o_ref[...] = acc_ref[...].astype(BF16)

    return _call(
        body, name=name, grid=(di // tk, n_m),
        out_shape=jax.ShapeDtypeStruct((di, d), BF16),
        in_specs=[pl.BlockSpec((tm, tk), lambda j, mi: (mi, j)),
                  pl.BlockSpec((tm, d), lambda j, mi: (mi, 0))],
        out_specs=pl.BlockSpec((tk, d), lambda j, mi: (j, 0)),
        scratch_shapes=[pltpu.VMEM((tk, d), F32)],
        compiler_params=_params(),
    )(y, dout)


def grad_w_in(h, dproj, nsh, sectioned, name):
    m_rows, d = h.shape
    n = dproj.shape[0] * dproj.shape[2] if sectioned else dproj.shape[1]
    ns = n // nsh
    tm, tn = min(ROW_TILE, m_rows), _col_tile(ns)
    per = ns // tn
    n_m = m_rows // tm

    def body(h_ref, dp_ref, o_ref, acc_ref):
        mi = pl.program_id(1)
        @pl.when(mi == 0)
        def _():
            acc_ref[...] = jnp.zeros_like(acc_ref)

        acc_ref[...] += _dot(h_ref[...], dp_ref[...], TN)

        @pl.when(mi == n_m - 1)
        def _():
            o_ref[...] = acc_ref[...].astype(BF16)

    if sectioned:
        dp_spec = pl.BlockSpec((None, tm, tn), lambda j, mi: (j // per, mi, j % per))
    else:
        dp_spec = pl.BlockSpec((tm, tn), lambda j, mi: (mi, j))
    return _call(
        body, name=name, grid=(n // tn, n_m),
        out_shape=jax.ShapeDtypeStruct((nsh, d, ns), BF16),
        in_specs=[pl.BlockSpec((tm, d), lambda j, mi: (mi, 0)), dp_spec],
        out_specs=pl.BlockSpec((None, d, tn), lambda j, mi: (j // per, 0, j % per)),
        scratch_shapes=[pltpu.VMEM((d, tn), F32)],
        compiler_params=_params(),
    )(h, dproj)


def inproj_bwd(dproj, wg, x, dxo, mod, ng, seq, sectioned, name):
    m_rows, d = x.shape
    nsh, _, ns = wg.shape
    n = nsh * ns
    nb = m_rows // seq
    tm, tk = min(ROW_TILE, seq), _col_tile(ns)
    per = ns // tk
    n_k = n // tk

    def body(dp_ref, w_ref, x_ref, dxo_ref, mod_ref, ng_ref, dxi_ref, dsh_ref, dsc_ref, dng_ref, acc_ref):
        i, k = pl.program_id(0), pl.program_id(1)
        @pl.when(k == 0)
        def _():
            acc_ref[...] = jnp.zeros_like(acc_ref)

        acc_ref[...] += _dot(dp_ref[...], w_ref[...], NT)

        @pl.when(k == n_k - 1)
        def _():
            dh = acc_ref[...]
            xv = x_ref[...]
            r = lax.rsqrt(jnp.mean(xv * xv, axis=-1, keepdims=True) + EPS)
            xn = xv * r
            md = mod_ref[0]
            gain = ng_ref[...]
            p_shift = jnp.sum(dh, axis=0, keepdims=True)
            p_scale = jnp.sum(dh * (xn * gain), axis=0, keepdims=True)
            drn = dh * (1.0 + md[:, d:2 * d])
            p_ng = jnp.sum(drn * xn, axis=0, keepdims=True)
            dxn = drn * gain
            dx = r * (dxn - xn * jnp.mean(dxn * xn, axis=-1, keepdims=True))
            dxi_ref[...] = dxo_ref[...] + dx

            @pl.when((i * tm) % seq == 0)
            def _():
                dsh_ref[0] = p_shift
                dsc_ref[0] = p_scale

            @pl.when((i * tm) % seq != 0)
            def _():
                dsh_ref[0] = dsh_ref[0] + p_shift
                dsc_ref[0] = dsc_ref[0] + p_scale

            @pl.when(i == 0)
            def _():
                dng_ref[...] = p_ng

            @pl.when(i != 0)
            def _():
                dng_ref[...] = dng_ref[...] + p_ng

    if sectioned:
        dp_spec = pl.BlockSpec((None, tm, tk), lambda i, k: (k // per, i, k % per))
    else:
        dp_spec = pl.BlockSpec((tm, tk), lambda i, k: (i, k))
    row = pl.BlockSpec((tm, d), lambda i, k: (i, 0))
    per_seq = pl.BlockSpec((1, 1, d), lambda i, k: ((i * tm) // seq, 0, 0))
    return _call(
        body, name=name, grid=(m_rows // tm, n_k),
        out_shape=[jax.ShapeDtypeStruct((m_rows, d), F32), jax.ShapeDtypeStruct((nb, 1, d), F32),
                   jax.ShapeDtypeStruct((nb, 1, d), F32), jax.ShapeDtypeStruct((1, d), F32)],
        in_specs=[dp_spec,
                  pl.BlockSpec((None, d, tk), lambda i, k: (k // per, 0, k % per)),
                  row, row,
                  pl.BlockSpec((1, 1, 3 * d), lambda i, k: ((i * tm) // seq, 0, 0)),
                  pl.BlockSpec((1, d), lambda i, k: (0, 0))],
        out_specs=[row, per_seq, per_seq, pl.BlockSpec((1, d), lambda i, k: (0, 0))],
        scratch_shapes=[pltpu.VMEM((tm, d), F32)],
        compiler_params=_params(),
    )(dproj, wg, x, dxo, mod, ng)


def _sgu_stats(proj_ref, vg_ref, di, gd):
    s1 = jnp.zeros((SG_BLOCK, 1), F32)
    for g in range(SG_GROUPS):
        vg = _gelu(proj_ref[:, di + g * gd:di + (g + 1) * gd])
        vg_ref[:, g * gd:(g + 1) * gd] = vg
        s1 = s1 + jnp.sum(vg, axis=1, keepdims=True)
    mu = s1 / di
    s2 = jnp.zeros((SG_BLOCK, 1), F32)
    for g in range(SG_GROUPS):
        dv = vg_ref[:, g * gd:(g + 1) * gd] - mu
        s2 = s2 + jnp.sum(dv * dv, axis=1, keepdims=True)
    return mu, lax.rsqrt(s2 / di + EPS)


def sgu_fwd(proj, ln_gain, ln_bias, ws, bs, name):
    m_rows, n3 = proj.shape
    di = n3 // 3
    gd = di // SG_GROUPS

    def body(proj_ref, lg_ref, lb_ref, ws_ref, bs_ref, y_ref, wsm_ref, vg_ref):
        @pl.when(pl.program_id(0) == 0)
        def _():
            mask = _chunk_mask()
            for g in range(SG_GROUPS):
                wsm_ref[g] = jnp.where(mask, ws_ref[g], 0.0).astype(BF16)

        mu, rstd = _sgu_stats(proj_ref, vg_ref, di, gd)
        for g in range(SG_GROUPS):
            cs = slice(g * gd, (g + 1) * gd)
            vln = (vg_ref[:, cs] - mu) * rstd * lg_ref[:, cs] + lb_ref[:, cs]
            s = _dot(wsm_ref[g], vln.astype(BF16), NN) + bs_ref[g]
            u = _gelu(proj_ref[:, cs])
            gp = proj_ref[:, 2 * di + g * gd:2 * di + (g + 1) * gd]
            y_ref[:, cs] = (u * s * (gp * _sigmoid(gp))).astype(BF16)

    full = lambda shape: pl.BlockSpec(shape, lambda i: (0,) * len(shape))
    return _call(
        body, name=name, grid=(m_rows // SG_BLOCK,),
        out_shape=jax.ShapeDtypeStruct((m_rows, di), BF16),
        in_specs=[pl.BlockSpec((SG_BLOCK, n3), lambda i: (i, 0)),
                  full((1, di)), full((1, di)),
                  full((SG_GROUPS, SG_BLOCK, SG_BLOCK)), full((SG_GROUPS, SG_BLOCK, 1))],
        out_specs=pl.BlockSpec((SG_BLOCK, di), lambda i: (i, 0)),
        scratch_shapes=[pltpu.VMEM((SG_GROUPS, SG_BLOCK, SG_BLOCK), BF16), pltpu.VMEM((SG_BLOCK, di), F32)],
        compiler_params=_params(),
    )(proj, ln_gain, ln_bias, ws, bs)


def sgu_bwd(proj, dy, ln_gain, ln_bias, ws, bs, name):
    m_rows, n3 = proj.shape
    di = n3 // 3
    gd = di // SG_GROUPS
    n_i = m_rows // SG_BLOCK

    def body(proj_ref, dy_ref, lg_ref, lb_ref, ws_ref, bs_ref,
             dp_ref, dws_ref, dbs_ref, dlg_ref, dlb_ref, wsm_ref, vg_ref, dvh_ref):
        i = pl.program_id(0)

        def before():
            @pl.when(i == 0)
            def _():
                mask = _chunk_mask()
                for g in range(SG_GROUPS):
                    wsm_ref[g] = jnp.where(mask, ws_ref[g], 0.0).astype(BF16)
                dws_ref[...] = jnp.zeros_like(dws_ref)
                dbs_ref[...] = jnp.zeros_like(dbs_ref)
                dlg_ref[...] = jnp.zeros_like(dlg_ref)
                dlb_ref[...] = jnp.zeros_like(dlb_ref)

        def after():
            @pl.when(i == n_i - 1)
            def _():
                mask = _chunk_mask()
                for g in range(SG_GROUPS):
                    dws_ref[g] = jnp.where(mask, dws_ref[g], 0.0)

        before()
        mu, rstd = _sgu_stats(proj_ref, vg_ref, di, gd)
        m1 = jnp.zeros((SG_BLOCK, 1), F32)
        m2 = jnp.zeros((SG_BLOCK, 1), F32)
        for g in range(SG_GROUPS):
            cs = slice(g * gd, (g + 1) * gd)
            gs = slice(2 * di + g * gd, 2 * di + (g + 1) * gd)
            gain = lg_ref[:, cs]
            vhat = (vg_ref[:, cs] - mu) * rstd
            vln_b = (vhat * gain + lb_ref[:, cs]).astype(BF16)
            s = _dot(wsm_ref[g], vln_b, NN) + bs_ref[g]
            u, du = _gelu_and_grad(proj_ref[:, cs])
            sg, dsg = _silu_and_grad(proj_ref[:, gs])
            dyv = dy_ref[:, cs].astype(F32)
            dp_ref[:, cs] = (dyv * s * sg * du).astype(BF16)
            dp_ref[:, gs] = (dyv * u * s * dsg).astype(BF16)
            ds = dyv * u * sg
            ds_b = ds.astype(BF16)
            dws_ref[g] = dws_ref[g] + _dot(ds_b, vln_b, NT)
            dbs_ref[g] = dbs_ref[g] + jnp.sum(ds, axis=1, keepdims=True)
            dvln = _dot(wsm_ref[g], ds_b, TN)
            dlg_ref[:, cs] = dlg_ref[:, cs] + jnp.sum(dvln * vhat, axis=0, keepdims=True)
            dlb_ref[:, cs] = dlb_ref[:, cs] + jnp.sum(dvln, axis=0, keepdims=True)
            dvh = dvln * gain
            dvh_ref[:, cs] = dvh
            m1 = m1 + jnp.sum(dvh, axis=1, keepdims=True)
            m2 = m2 + jnp.sum(dvh * vhat, axis=1, keepdims=True)
        m1 = m1 / di
        m2 = m2 / di
        for g in range(SG_GROUPS):
            cs = slice(g * gd, (g + 1) * gd)
            vs = slice(di + g * gd, di + (g + 1) * gd)
            vhat = (vg_ref[:, cs] - mu) * rstd
            dvg = rstd * (dvh_ref[:, cs] - m1 - vhat * m2)
            _, dgel = _gelu_and_grad(proj_ref[:, vs])
            dp_ref[:, vs] = (dvg * dgel).astype(BF16)

        after()

    full = lambda shape: pl.BlockSpec(shape, lambda i: (0,) * len(shape))
    return _call(
        body, name=name, grid=(n_i,),
        out_shape=[jax.ShapeDtypeStruct((m_rows, n3), BF16),
                   jax.ShapeDtypeStruct((SG_GROUPS, SG_BLOCK, SG_BLOCK), F32),
                   jax.ShapeDtypeStruct((SG_GROUPS, SG_BLOCK, 1), F32),
                   jax.ShapeDtypeStruct((1, di), F32), jax.ShapeDtypeStruct((1, di), F32)],
        in_specs=[pl.BlockSpec((SG_BLOCK, n3), lambda i: (i, 0)),
                  pl.BlockSpec((SG_BLOCK, di), lambda i: (i, 0)),
                  full((1, di)), full((1, di)),
                  full((SG_GROUPS, SG_BLOCK, SG_BLOCK)), full((SG_GROUPS, SG_BLOCK, 1))],
        out_specs=[pl.BlockSpec((SG_BLOCK, n3), lambda i: (i, 0)),
                   full((SG_GROUPS, SG_BLOCK, SG_BLOCK)), full((SG_GROUPS, SG_BLOCK, 1)),
                   full((1, di)), full((1, di))],
        scratch_shapes=[pltpu.VMEM((SG_GROUPS, SG_BLOCK, SG_BLOCK), BF16),
                        pltpu.VMEM((SG_BLOCK, di), F32), pltpu.VMEM((SG_BLOCK, di), F32)],
        compiler_params=_params(),
    )(proj, dy, ln_gain, ln_bias, ws, bs)


def _lower_bound(lbraw):
    mx = jnp.maximum(lbraw[0:1, :], lbraw[1:2, :])
    e0 = jnp.exp(lbraw[0:1, :] - mx)
    e1 = jnp.exp(lbraw[1:2, :] - mx)
    p0 = e0 / (e0 + e1)
    p1 = e1 / (e0 + e1)
    return (p0 + p1) - p0, p0, p1


def _tri(lower):
    r = lax.broadcasted_iota(jnp.int32, (CHUNK, CHUNK), 0)
    c = lax.broadcasted_iota(jnp.int32, (CHUNK, CHUNK), 1)
    return ((r >= c) if lower else (c >= r)).astype(F32)


def _row(a, idx):
    r = lax.broadcasted_iota(jnp.int32, a.shape, 0)
    return jnp.sum(jnp.where(r == idx, a, 0.0), axis=0, keepdims=True)


def _hgrn_gates(qp, fp, lb, tri):
    sgm = _sigmoid_small(fp)
    f = lb + (1.0 - lb) * sgm
    k = 1.0 - f
    a = _dot(tri, jnp.log(f), NN, precision=lax.Precision.HIGHEST)
    a_mid = _row(a, CHUNK // 2 - 1)
    a_last = _row(a, CHUNK - 1)
    q, dq = _silu_and_grad(qp)
    e1, e2, e3, e4 = jnp.exp(a - a_mid), jnp.exp(a_mid - a), jnp.exp(a), jnp.exp(a_last - a)
    return dict(sgm=sgm, f=f, k=k, q=q, dq=dq, e1=e1, e2=e2, e3=e3, e4=e4, dec=jnp.exp(a_last),
                q_in=q * e1, k_in=k * e2, q_out=q * e3, k_out=k * e4)


def _causal():
    r = lax.broadcasted_iota(jnp.int32, (CHUNK, CHUNK), 0)
    c = lax.broadcasted_iota(jnp.int32, (CHUNK, CHUNK), 1)
    return r >= c


def hgrn_fwd(proj4, lbraw, gn, seq, name):
    _, m_rows, di = proj4.shape
    nb, nh, nc = m_rows // seq, di // HEAD_DIM, seq // CHUNK
    rows = min(HG_ROWS, seq)
    wide = HG_WIDE * HEAD_DIM
    ns, cpb = seq // rows, rows // CHUNK

    def body(p_ref, lb_ref, gn_ref, y_ref, sts_ref, st_ref):
        @pl.when(pl.program_id(2) == 0)
        def _():
            st_ref[...] = jnp.zeros_like(st_ref)

        tri = _tri(True)
        causal = _causal()
        gain = gn_ref[...]
        lbs = [_lower_bound(lb_ref[:, j * HEAD_DIM:(j + 1) * HEAD_DIM])[0] for j in range(HG_WIDE)]

        units = [(n, j) for n in range(cpb) for j in range(HG_WIDE)]
        rs = lambda n: slice(n * CHUNK, (n + 1) * CHUNK)
        cs = lambda j: slice(j * HEAD_DIM, (j + 1) * HEAD_DIM)
        gates, v_b, sc_b, kv, o_in, o_x = {}, {}, {}, {}, {}, {}
        for n, j in units:
            gates[n, j] = _hgrn_gates(p_ref[0, rs(n), cs(j)], p_ref[1, rs(n), cs(j)], lbs[j], tri)
            v_b[n, j] = p_ref[2, rs(n), cs(j)].astype(BF16)
        for u in units:
            t = gates[u]
            sc_b[u] = jnp.where(causal, _dot(t["q_in"].astype(BF16), t["k_in"].astype(BF16), NT), 0.0).astype(BF16)
            kv[u] = _dot(v_b[u], t["k_out"].astype(BF16), TN)
        for u in units:
            o_in[u] = _dot(sc_b[u], v_b[u], NN)
        for j in range(HG_WIDE):
            st = st_ref[j]
            for n in range(cpb):
                sts_ref[n, :, cs(j)] = st
                o_x[n, j] = _dot(gates[n, j]["q_out"].astype(BF16), st.astype(BF16), NT)
                st = st * gates[n, j]["dec"] + kv[n, j]
            st_ref[j] = st
        for n, j in units:
            o = o_in[n, j] + o_x[n, j]
            r = lax.rsqrt(jnp.mean(o * o, axis=-1, keepdims=True) + EPS)
            gp = p_ref[3, rs(n), cs(j)]
            y_ref[rs(n), cs(j)] = ((o * r * gain) * (gp * _sigmoid(gp))).astype(BF16)

    return _call(
        body, name=name, grid=(nh // HG_WIDE, nb, ns),
        out_shape=[jax.ShapeDtypeStruct((m_rows, di), BF16),
                   jax.ShapeDtypeStruct((nb * nc, HEAD_DIM, di), F32)],
        in_specs=[pl.BlockSpec((4, rows, wide), lambda hg, b, s: (0, b * ns + s, hg)),
                  pl.BlockSpec((2, wide), lambda hg, b, s: (0, hg)),
                  pl.BlockSpec((1, HEAD_DIM), lambda hg, b, s: (0, 0))],
        out_specs=[pl.BlockSpec((rows, wide), lambda hg, b, s: (b * ns + s, hg)),
                   pl.BlockSpec((cpb, HEAD_DIM, wide), lambda hg, b, s: (b * ns + s, 0, hg))],
        scratch_shapes=[pltpu.VMEM((HG_WIDE, HEAD_DIM, HEAD_DIM), F32)],
        compiler_params=_params(),
    )(proj4, lbraw, gn)


def hgrn_bwd(proj4, dy, sts, lbraw, gn, seq, name):
    _, m_rows, di = proj4.shape
    nb, nh, nc = m_rows // seq, di // HEAD_DIM, seq // CHUNK
    rows = min(HG_ROWS, seq)
    wide = HG_WIDE * HEAD_DIM
    ns, cpb = seq // rows, rows // CHUNK
    n_hg = nh // HG_WIDE

    def body(p_ref, dy_ref, sts_ref, lb_ref, gn_ref, dp_ref, dlb_ref, dgn_ref, dst_ref, lbacc_ref, gnacc_ref):
        hg, b, s = pl.program_id(0), pl.program_id(1), pl.program_id(2)
        tri, triu = _tri(True), _tri(False)
        causal = _causal()
        gain = gn_ref[...]
        first = (b == 0) & (s == 0)
        cs = lambda j: slice(j * HEAD_DIM, (j + 1) * HEAD_DIM)

        def before():
            @pl.when((hg == 0) & first)
            def _():
                gnacc_ref[...] = jnp.zeros_like(gnacc_ref)

            @pl.when(first)
            def _():
                lbacc_ref[...] = jnp.zeros_like(lbacc_ref)

            @pl.when(s == 0)
            def _():
                dst_ref[...] = jnp.zeros_like(dst_ref)

        def after():
            @pl.when((b == nb - 1) & (s == ns - 1))
            def _():
                for j in range(HG_WIDE):
                    _, p0, p1 = _lower_bound(lb_ref[:, cs(j)])
                    acc = lbacc_ref[:, cs(j)]
                    dlb_ref[0:1, cs(j)] = -acc * p0 * p1
                    dlb_ref[1:2, cs(j)] = acc * p1 * (1.0 - p1)

            @pl.when((hg == n_hg - 1) & (b == nb - 1) & (s == ns - 1))
            def _():
                tot = gnacc_ref[:, 0:HEAD_DIM]
                for j in range(1, HG_WIDE):
                    tot = tot + gnacc_ref[:, cs(j)]
                dgn_ref[...] = tot

        before()

        units = [(n, j) for n in range(cpb) for j in range(HG_WIDE)]
        rs = lambda n: slice(n * CHUNK, (n + 1) * CHUNK)
        lbs = [_lower_bound(lb_ref[:, cs(j)])[0] for j in range(HG_WIDE)]
        gates, v_b, st_b, sc_b, o, do_b = {}, {}, {}, {}, {}, {}
        dq_out, dsc_b, dv, g_st, dq_in, dk_in, dst_at, dk_out, ddec = {}, {}, {}, {}, {}, {}, {}, {}, {}
        for n, j in units:
            gates[n, j] = _hgrn_gates(p_ref[0, rs(n), cs(j)], p_ref[1, rs(n), cs(j)], lbs[j], tri)
            v_b[n, j] = p_ref[2, rs(n), cs(j)].astype(BF16)
            st_b[n, j] = sts_ref[n, :, cs(j)].astype(BF16)
        for u in units:
            t = gates[u]
            sc_b[u] = jnp.where(causal, _dot(t["q_in"].astype(BF16), t["k_in"].astype(BF16), NT), 0.0).astype(BF16)
        for u in units:
            o[u] = _dot(sc_b[u], v_b[u], NN) + _dot(gates[u]["q_out"].astype(BF16), st_b[u], NT)
        for n, j in units:
            ov = o[n, j]
            r = lax.rsqrt(jnp.mean(ov * ov, axis=-1, keepdims=True) + EPS)
            ohat = ov * r
            sg, dsg = _silu_and_grad(p_ref[3, rs(n), cs(j)])
            dyv = dy_ref[rs(n), cs(j)].astype(F32)
            dp_ref[3, rs(n), cs(j)] = (dyv * (ohat * gain) * dsg).astype(BF16)
            d_on = dyv * sg
            gnacc_ref[:, cs(j)] = gnacc_ref[:, cs(j)] + jnp.sum(d_on * ohat, axis=0, keepdims=True)
            dohat = d_on * gain
            do_b[n, j] = (r * (dohat - ohat * jnp.mean(dohat * ohat, axis=-1, keepdims=True))).astype(BF16)
        for u in units:
            dq_out[u] = _dot(do_b[u], st_b[u], NN)
            dsc_b[u] = jnp.where(causal, _dot(do_b[u], v_b[u], NT), 0.0).astype(BF16)
            dv[u] = _dot(sc_b[u], do_b[u], TN)
            g_st[u] = _dot(do_b[u], gates[u]["q_out"].astype(BF16), TN)
        for u in units:
            dq_in[u] = _dot(dsc_b[u], gates[u]["k_in"].astype(BF16), NN)
            dk_in[u] = _dot(dsc_b[u], gates[u]["q_in"].astype(BF16), TN)
        for j in range(HG_WIDE):
            dst = dst_ref[j]
            for n in reversed(range(cpb)):
                dst_at[n, j] = dst
                dst = dst * gates[n, j]["dec"] + g_st[n, j]
            dst_ref[j] = dst
        for n, j in units:
            dst = dst_at[n, j]
            dst_b = dst.astype(BF16)
            dk_out[n, j] = _dot(v_b[n, j], dst_b, NN)
            dv[n, j] = dv[n, j] + _dot(gates[n, j]["k_out"].astype(BF16), dst_b, NT)
            ddec[n, j] = jnp.sum(dst * sts_ref[n, :, cs(j)], axis=0, keepdims=True)
        for n, j in units:
            t = gates[n, j]
            dp_ref[2, rs(n), cs(j)] = dv[n, j].astype(BF16)
            dq = dq_in[n, j] * t["e1"] + dq_out[n, j] * t["e3"]
            dk = dk_in[n, j] * t["e2"] + dk_out[n, j] * t["e4"]
            w_in = dq_in[n, j] * t["q_in"] - dk_in[n, j] * t["k_in"]
            w_out = dk_out[n, j] * t["k_out"]
            da = w_in + dq_out[n, j] * t["q_out"] - w_out
            da_mid = -jnp.sum(w_in, axis=0, keepdims=True)
            da_last = jnp.sum(w_out, axis=0, keepdims=True) + ddec[n, j] * t["dec"]
            rid = lax.broadcasted_iota(jnp.int32, da.shape, 0)
            da = da + jnp.where(rid == CHUNK // 2 - 1, da_mid, 0.0) + jnp.where(rid == CHUNK - 1, da_last, 0.0)
            dlf = _dot(triu, da, NN, precision=lax.Precision.HIGHEST)
            df = dlf / t["f"] - dk
            sgm = t["sgm"]
            dp_ref[1, rs(n), cs(j)] = (df * (1.0 - lbs[j]) * sgm * (1.0 - sgm)).astype(BF16)
            lbacc_ref[:, cs(j)] = lbacc_ref[:, cs(j)] + jnp.sum(df * (1.0 - sgm), axis=0, keepdims=True)
            dp_ref[0, rs(n), cs(j)] = (dq * t["dq"]).astype(BF16)

        after()

    blk = lambda hg, b, s: b * ns + (ns - 1 - s)
    return _call(
        body, name=name, grid=(n_hg, nb, ns),
        out_shape=[jax.ShapeDtypeStruct((4, m_rows, di), BF16), jax.ShapeDtypeStruct((2, di), F32),
                   jax.ShapeDtypeStruct((1, HEAD_DIM), F32)],
        in_specs=[pl.BlockSpec((4, rows, wide), lambda hg, b, s: (0, blk(hg, b, s), hg)),
                  pl.BlockSpec((rows, wide), lambda hg, b, s: (blk(hg, b, s), hg)),
                  pl.BlockSpec((cpb, HEAD_DIM, wide), lambda hg, b, s: (blk(hg, b, s), 0, hg)),
                  pl.BlockSpec((2, wide), lambda hg, b, s: (0, hg)),
                  pl.BlockSpec((1, HEAD_DIM), lambda hg, b, s: (0, 0))],
        out_specs=[pl.BlockSpec((4, rows, wide), lambda hg, b, s: (0, blk(hg, b, s), hg)),
                   pl.BlockSpec((2, wide), lambda hg, b, s: (0, hg)),
                   pl.BlockSpec((1, HEAD_DIM), lambda hg, b, s: (0, 0))],
        scratch_shapes=[pltpu.VMEM((HG_WIDE, HEAD_DIM, HEAD_DIM), F32), pltpu.VMEM((1, wide), F32),
                        pltpu.VMEM((1, wide), F32)],
        compiler_params=_params(),
    )(proj4, dy, sts, lbraw, gn)


def outproj_loss(y, w, x, mod, fg, target, seq, name):
    m_rows, di = y.shape
    d = w.shape[1]
    tm = min(512, seq)

    def body(y_ref, w_ref, x_ref, mod_ref, fg_ref, t_ref, out_ref, loss_ref, dx_ref, dfg_ref):
        i = pl.program_id(0)
        acc = _dot(y_ref[...], w_ref[...], NN)
        out_ref[...] = acc.astype(BF16)
        xv = x_ref[...] + mod_ref[0][:, 2 * d:] * acc
        gain = fg_ref[...]
        r = lax.rsqrt(jnp.mean(xv * xv, axis=-1, keepdims=True) + EPS)
        xn = xv * r
        e = xn * gain - t_ref[...]
        part = 0.5 * jnp.sum(jnp.mean(e * e, axis=-1, keepdims=True), axis=0, keepdims=True)
        dyv = e / d
        p_fg = jnp.sum(dyv * xn, axis=0, keepdims=True)
        dxn = dyv * gain
        dx_ref[...] = r * (dxn - xn * jnp.mean(dxn * xn, axis=-1, keepdims=True))

        @pl.when(i == 0)
        def _():
            loss_ref[...] = part
            dfg_ref[...] = p_fg

        @pl.when(i != 0)
        def _():
            loss_ref[...] = loss_ref[...] + part
            dfg_ref[...] = dfg_ref[...] + p_fg

    row = pl.BlockSpec((tm, d), lambda i: (i, 0))
    return _call(
        body, name=name, grid=(m_rows // tm,),
        out_shape=[jax.ShapeDtypeStruct((m_rows, d), BF16), jax.ShapeDtypeStruct((1, 1), F32),
                   jax.ShapeDtypeStruct((m_rows, d), F32), jax.ShapeDtypeStruct((1, d), F32)],
        in_specs=[pl.BlockSpec((tm, di), lambda i: (i, 0)),
                  pl.BlockSpec((di, d), lambda i: (0, 0)),
                  row,
                  pl.BlockSpec((1, 1, 3 * d), lambda i: ((i * tm) // seq, 0, 0)),
                  pl.BlockSpec((1, d), lambda i: (0, 0)), row],
        out_specs=[row, pl.BlockSpec((1, 1), lambda i: (0, 0)), row, pl.BlockSpec((1, d), lambda i: (0, 0))],
        compiler_params=_params(),
    )(y, w, x, mod, fg, target)


def _pack(parts):
    flat = jnp.concatenate([p.reshape(-1) for p in parts])
    pad = (-flat.shape[0]) % (8 * LANES)
    return jnp.pad(flat, (0, pad)).reshape(-1, LANES)


def kernel(x, c, norm_gain, w_ada, b_ada, a_w_in, a_ln_gain, a_ln_bias, a_w_s, a_b_s, a_w_out, b_w_in, b_lower_bounds, b_gn_gain, b_w_out, final_gain, loss_target, m_norm_gain, m_w_ada, m_b_ada, m_a_w_in, m_a_ln_gain, m_a_ln_bias, m_a_w_s, m_a_b_s, m_a_w_out, m_b_w_in, m_b_lower_bounds, m_b_gn_gain, m_b_w_out, m_final_gain, v_norm_gain, v_w_ada, v_b_ada, v_a_w_in, v_a_ln_gain, v_a_ln_bias, v_a_w_s, v_a_b_s, v_a_w_out, v_b_w_in, v_b_lower_bounds, v_b_gn_gain, v_b_w_out, v_final_gain):
    nb, seq, d = x.shape
    m_rows = nb * seq
    n_l = w_ada.shape[0]
    ada_cols = w_ada.shape[2]
    px, py, pc = _place()
    chip = 2 * px + py
    dev = 2 * chip + pc

    c_all = allgather_small(c.reshape(-1, LANES), "gather_c").reshape(N_DEV * nb, d)
    b_cols = lax.dynamic_slice_in_dim(b_ada, chip * ada_cols, ada_cols, axis=1).reshape(n_l, 1, ada_cols)
    mod_cols = ada_fwd(c_all, w_ada, b_cols, "ada_fwd")
    mod_g = allgather_small(mod_cols.reshape(-1, LANES), "gather_mod")
    mod_g = mod_g.reshape(N_CHIPS, 2, n_l, N_DEV * nb, ada_cols)[:, 0]
    mod_all = jnp.transpose(mod_g, (1, 2, 0, 3)).reshape(n_l, N_DEV * nb, 3 * d)
    mod_mine = lax.dynamic_slice_in_dim(mod_all, dev * nb, nb, axis=1)
    mod0 = mod_mine[0].reshape(nb, 1, 3 * d)
    mod1 = mod_mine[1].reshape(nb, 1, 3 * d)

    (wa_in, wa_out), tok_a = gather_inplace(
        [cast_into_slot(a_w_in[0], chip, mod_mine, "cast_a_in"), cast_into_slot(a_w_out[0], chip, mod_mine, "cast_a_out")],
        "gather_a")
    s_bi = gather_start(cast_into_slot(b_w_in[0], chip, tok_a, "cast_b_in"), "gather_b_in_start")
    s_bo = gather_start(cast_into_slot(b_w_out[0], chip, s_bi[3], "cast_b_out"), "gather_b_out_start")
    di = a_w_out.shape[1] * N_CHIPS
    wa_out = wa_out.reshape(di, d)

    x0 = x.reshape(m_rows, d)
    tgt = loss_target.reshape(m_rows, d)
    ng0 = norm_gain[0:1] + (s_bi[3][0, 0] + s_bo[3][0, 0])
    ng1 = norm_gain[1:2]
    bs_col = a_b_s[0].reshape(SG_GROUPS, SG_BLOCK, 1)
    proj_a, h_a = inproj_fwd(x0, mod0, ng0, wa_in, seq, False, "a_inproj")
    y_a = sgu_fwd(proj_a, a_ln_gain, a_ln_bias, a_w_s[0], bs_col, "a_sgu")
    x1, out_a = outproj_fwd(y_a, wa_out, x0, mod0, seq, "a_outproj")
    wb_in = gather_wait(*s_bi[:3], out_a, "gather_b_in_wait")
    proj_b, h_b = inproj_fwd(x1, mod1, ng1, wb_in, seq, True, "b_inproj")
    y_b, sts_b = hgrn_fwd(proj_b, b_lower_bounds, b_gn_gain, seq, "b_hgrn")
    wb_out = gather_wait(*s_bo[:3], y_b, "gather_b_out_wait").reshape(di, d)
    out_b, loss_part, dx2, dfg = outproj_loss(
        y_b, wb_out, x1, mod1, final_gain.reshape(1, d), tgt, seq, "b_outproj_loss")

    shard_rows = di // N_CHIPS
    dy_b, dout_b, dgate1 = outproj_bwd(dx2, out_b, mod1, wb_out, seq, "b_outproj_bwd")
    gwb_out = grad_w_out(y_b, dout_b, "b_grad_w_out").reshape(N_CHIPS, shard_rows, d)
    e_bo = exchange_start(gwb_out, "exchange_b_out_start")
    dproj_b, dlb, dgn = hgrn_bwd(
        proj_b, dy_b, sts_b, b_lower_bounds, b_gn_gain + e_bo[4][0, 0], seq, "b_hgrn_bwd")
    e_bi = exchange_start(grad_w_in(h_b, dproj_b, N_CHIPS, True, "b_grad_w_in"), "exchange_b_in_start")
    dx1, dshift1, dscale1, dng1 = inproj_bwd(
        dproj_b, wb_in, x1, dx2, mod1, ng1 + e_bi[4][0, 0], seq, True, "b_inproj_bwd")

    dy_a, dout_a, dgate0 = outproj_bwd(dx1, out_a, mod0, wa_out, seq, "a_outproj_bwd")
    gwa_out = grad_w_out(y_a, dout_a, "a_grad_w_out").reshape(N_CHIPS, shard_rows, d)
    e_ao = exchange_start(gwa_out, "exchange_a_out_start")
    dproj_a, dws, dbs, dlg, dlbias = sgu_bwd(
        proj_a, dy_a, a_ln_gain + e_ao[4][0, 0], a_ln_bias, a_w_s[0], bs_col, "a_sgu_bwd")
    e_ai = exchange_start(grad_w_in(h_a, dproj_a, N_CHIPS, False, "a_grad_w_in"), "exchange_a_in_start")
    dx0, dshift0, dscale0, dng0 = inproj_bwd(
        dproj_a, wa_in, x0, dx1, mod0, norm_gain[0:1] + e_ai[4][0, 0], seq, False, "a_inproj_bwd")
    grad_x = dx0.reshape(nb, seq, d)

    dmod = jnp.concatenate([dshift0, dscale0, dgate0, dshift1, dscale1, dgate1], axis=2)
    n_dmod = dmod.size
    small_g = [jnp.concatenate([dng0, dng1], axis=0), dlg, dlbias, dws, dbs, dlb, dfg, dgn]
    packed_g = _pack([dmod] + small_g + [loss_part])
    rows = packed_g.shape[0]
    s_small = gather_all_start(
        lax.dynamic_update_slice(jnp.zeros((N_DEV, rows, LANES), F32), packed_g[None], (dev, 0, 0)),
        "gather_small_start")

    def finish(group, after):
        mine = []
        for ex, _, _, _, nm in group:
            parts_thru, land = exchange_wait(ex[0], ex[1], ex[2], ex[3], after, "exchange_" + nm + "_wait")
            mine.append(sum_parts(parts_thru, land, chip, "sum_" + nm))
            after = mine[-1]
        theirs = swap_sibling(mine, "swap_" + group[0][4])
        return [[r.reshape(w.shape) for r in adamw_pair(pa, pb, w[0], m[0], v[0], "adamw_" + nm)]
                for pa, pb, (_, w, m, v, nm) in zip(mine, theirs, group)]

    (gb_out, db_out, mb_out, vb_out), (gb_in, db_in, mb_in, vb_in), (ga_out, da_out, ma_out, va_out) = finish(
        [(e_bo, b_w_out, m_b_w_out, v_b_w_out, "b_out"), (e_bi, b_w_in, m_b_w_in, v_b_w_in, "b_in"),
         (e_ao, a_w_out, m_a_w_out, v_a_w_out, "a_out")], s_small[3])
    ((ga_in, da_in, ma_in, va_in),) = finish([(e_ai, a_w_in, m_a_w_in, v_a_w_in, "a_in")], ga_out)

    small_w = [norm_gain, a_ln_gain, a_ln_bias, a_w_s, a_b_s, b_lower_bounds, final_gain, b_gn_gain]
    small_m = [m_norm_gain, m_a_ln_gain, m_a_ln_bias, m_a_w_s, m_a_b_s, m_b_lower_bounds, m_final_gain, m_b_gn_gain]
    small_v = [v_norm_gain, v_a_ln_gain, v_a_ln_bias, v_a_w_s, v_a_b_s, v_b_lower_bounds, v_final_gain, v_b_gn_gain]
    rows_of = lambda a: a.reshape(-1, a.shape[-1])
    gathered = gather_all_wait(s_small[0], s_small[1], s_small[2], ga_in, "gather_small_wait")
    tail, small_res = small_update(
        gathered, n_dmod // LANES, [rows_of(a) for a in small_w], [rows_of(a) for a in small_m],
        [rows_of(a) for a in small_v], "small_update")
    loss = tail[0, 0]
    sg, sd, sm, sv = [[small_res[p][kind].reshape(w.shape) for p, w in enumerate(small_w)] for kind in range(4)]

    dmod_all = gathered[:, :n_dmod // LANES].reshape(N_DEV * nb, n_l, 3 * d)
    dmod_cols = lax.dynamic_slice_in_dim(dmod_all, chip * ada_cols, ada_cols, axis=2)
    dmod_cols = jnp.transpose(dmod_cols, (1, 0, 2))
    g_wada, d_wada, m_wada, v_wada = ada_bwd(c_all, dmod_cols, w_ada, m_w_ada, v_w_ada, "ada_bwd")
    flat = lambda a: a.reshape(1, -1)
    g_bada, d_bada, m_bada, v_bada = [
        r.reshape(b_ada.shape) for r in
        bias_update(dmod_all.reshape(N_DEV * nb, n_l * 3 * d), flat(b_ada), flat(m_b_ada), flat(v_b_ada), "bias_update")]

    def order(ng, wada, bada, ain, sm_rest, aout, bin_, bout):
        lg, lbi, ws_, bs_, lbd, fg_, gn_ = sm_rest
        return [ng, wada, bada, ain, lg, lbi, ws_, bs_, aout, bin_, lbd, gn_, bout, fg_]

    grads = order(sg[0], g_wada, g_bada, ga_in, sg[1:8], ga_out, gb_in, gb_out)
    deltas = order(sd[0], d_wada, d_bada, da_in, sd[1:8], da_out, db_in, db_out)
    new_m = order(sm[0], m_wada, m_bada, ma_in, sm[1:8], ma_out, mb_in, mb_out)
    new_v = order(sv[0], v_wada, v_bada, va_in, sv[1:8], va_out, vb_in, vb_out)
    return (loss, grad_x, *grads, *deltas, *new_m, *new_v)
```

```python
import functools

import jax
import jax.numpy as jnp
from jax import lax
from jax.experimental import pallas as pl
from jax.experimental.pallas import tpu as pltpu

F32 = jnp.float32
BF16 = jnp.bfloat16
EPS = 1e-6
CHUNK = 64
SG_BLOCK = 128
SG_GROUPS = 8
HEAD_DIM = 128
HG_WIDE = 8
HG_ROWS = 256
N_CHIPS = 4
N_DEV = 8
LANES = 128
ADAM_LR = 0.001
ADAM_B1 = 0.9
ADAM_B2 = 0.999
ADAM_EPS = 1e-08
ADAM_WD = 0.01
ADAM_STEP = 10
GELU_C0 = 0.7978845608028654
GELU_C1 = 0.044715
MESH = pl.DeviceIdType.MESH
VMEM_LIMIT = 56 * 1024 * 1024


ROW_TILE = 1024


def _col_tile(n):
    return next(t for t in (1024, 768, 512, 256) if n % t == 0)


def _call(body, **kw):
    return pl.pallas_call(body, **kw)


def _params(**kw):
    return pltpu.CompilerParams(vmem_limit_bytes=VMEM_LIMIT, **kw)


def _sigmoid(x):
    return 0.5 * jnp.tanh(0.5 * x) + 0.5


def _sigmoid_small(x):
    return 1.0 / (1.0 + jnp.exp(-x))


def _silu_and_grad(x):
    s = _sigmoid(x)
    return x * s, s * (1.0 + x * (1.0 - s))


def _gelu(x):
    return 0.5 * x * (1.0 + jnp.tanh(GELU_C0 * (x + GELU_C1 * x * x * x)))


def _gelu_and_grad(x):
    t = jnp.tanh(GELU_C0 * (x + GELU_C1 * x * x * x))
    g = 0.5 * x * (1.0 + t)
    dg = 0.5 * (1.0 + t) + 0.5 * x * (1.0 - t * t) * (GELU_C0 * (1.0 + 3.0 * GELU_C1 * x * x))
    return g, dg


def _dot(a, b, dims, precision=None):
    return lax.dot_general(a, b, (dims, ((), ())), precision=precision, preferred_element_type=F32)


NN = ((1,), (0,))
NT = ((1,), (1,))
TN = ((0,), (0,))


def _adamw(w, g, m, v):
    m = ADAM_B1 * m + (1.0 - ADAM_B1) * g
    v = ADAM_B2 * v + (1.0 - ADAM_B2) * (g * g)
    m_hat = m / (1.0 - ADAM_B1 ** ADAM_STEP)
    v_hat = v / (1.0 - ADAM_B2 ** ADAM_STEP)
    delta = -ADAM_LR * (m_hat / (jnp.sqrt(v_hat) + ADAM_EPS) + ADAM_WD * w)
    return delta, m, v


def _chunk_mask():
    r = lax.broadcasted_iota(jnp.int32, (SG_BLOCK, SG_BLOCK), 0)
    c = lax.broadcasted_iota(jnp.int32, (SG_BLOCK, SG_BLOCK), 1)
    return (c // CHUNK) <= (r // CHUNK)


def _place():
    return lax.axis_index("x"), lax.axis_index("y"), lax.axis_index("c")


def _other_chips(x, y):
    return [(1 - x, y), (x, 1 - y), (1 - x, 1 - y)]


def allgather_small(v, name):
    m_per, n = v.shape

    def body(x_ref, out_ref, send_sems, recv_sems, local_sem):
        x, y, c = _place()
        me, sibling = (x, y, c), (x, y, 1 - c)
        chips = _other_chips(x, y)

        def rows(px, py, pc):
            return out_ref.at[pl.ds((4 * px + 2 * py + pc) * m_per, m_per), :]

        def copy(k, block, to, src=None):
            return pltpu.make_async_remote_copy(
                src_ref=rows(*block) if src is None else src, dst_ref=rows(*block),
                send_sem=send_sems.at[k], recv_sem=recv_sems.at[k], device_id=to, device_id_type=MESH)

        mine = pltpu.make_async_copy(x_ref, rows(*me), local_sem)
        mine.start()
        first = [copy(0, me, sibling, src=x_ref)]
        first += [copy(1 + j, me, (*chip, c), src=x_ref) for j, chip in enumerate(chips)]
        for cp in first:
            cp.start()
        passed = [copy(4 + j, (*chip, c), sibling) for j, chip in enumerate(chips)]
        for j, chip in enumerate(chips):
            copy(1 + j, (*chip, c), me).wait_recv()
            passed[j].start()
        copy(0, sibling, me).wait_recv()
        for j, chip in enumerate(chips):
            copy(4 + j, (*chip, 1 - c), me).wait_recv()
        for cp in first + passed:
            cp.wait_send()
        mine.wait()

    return _call(
        body, name=name,
        out_shape=jax.ShapeDtypeStruct((N_DEV * m_per, n), v.dtype),
        in_specs=[pl.BlockSpec(memory_space=pltpu.VMEM)],
        out_specs=pl.BlockSpec(memory_space=pltpu.VMEM),
        scratch_shapes=[pltpu.SemaphoreType.DMA((7,)), pltpu.SemaphoreType.DMA((7,)), pltpu.SemaphoreType.DMA],
    )(v)


def _hbm_spec():
    return pl.BlockSpec(memory_space=pltpu.HBM)


def _sem_spec():
    return pl.BlockSpec(memory_space=pltpu.SEMAPHORE)


def _split_params():
    return pltpu.CompilerParams(has_side_effects=pltpu.SideEffectType.DATAFLOW_SIDE_EFFECTING)


def _hbm(a):
    return pltpu.with_memory_space_constraint(a, pltpu.HBM)


def gather_inplace(lands, name):
    n = len(lands)

    def body(*refs):
        land_refs, token = refs[n:2 * n], refs[2 * n]
        send_sems, recv_sems = refs[2 * n + 1:]
        x, y, c = _place()
        chips = _other_chips(x, y)

        def copy(w, k, chip_idx, core_half, to):
            half = lands[w].shape[1] // 2
            rows = land_refs[w].at[chip_idx, pl.ds(core_half * half, half), :]
            return pltpu.make_async_remote_copy(
                src_ref=rows, dst_ref=rows, send_sem=send_sems.at[6 * w + k], recv_sem=recv_sems.at[6 * w + k],
                device_id=to, device_id_type=MESH)

        first = [copy(w, j, 2 * x + y, c, (px, py, c)) for w in range(n) for j, (px, py) in enumerate(chips)]
        for cp in first:
            cp.start()
        passed = []
        for w in range(n):
            for j, (px, py) in enumerate(chips):
                copy(w, j, 2 * px + py, c, (px, py, c)).wait_recv()
                passed.append(copy(w, 3 + j, 2 * px + py, c, (x, y, 1 - c)))
                passed[-1].start()
        for w in range(n):
            for j, (px, py) in enumerate(chips):
                copy(w, 3 + j, 2 * px + py, 1 - c, (x, y, 1 - c)).wait_recv()
        for cp in first + passed:
            cp.wait_send()
        token[...] = jnp.zeros_like(token)

    res = _call(
        body, name=name,
        out_shape=[jax.ShapeDtypeStruct(a.shape, a.dtype) for a in lands] + [jax.ShapeDtypeStruct((8, LANES), F32)],
        in_specs=[_hbm_spec()] * n, out_specs=[_hbm_spec()] * n + [pl.BlockSpec(memory_space=pltpu.VMEM)],
        input_output_aliases={w: w for w in range(n)},
        scratch_shapes=[pltpu.SemaphoreType.DMA((6 * n,)), pltpu.SemaphoreType.DMA((6 * n,))],
    )(*lands)
    return res[:n], res[n]


def gather_start(land, name):
    def body(land_ref, send_sems, recv_sems, land_thru, token):
        del land_thru
        x, y, c = _place()
        for j, (px, py) in enumerate(_other_chips(x, y)):
            pltpu.make_async_remote_copy(
                src_ref=land_ref.at[2 * x + y], dst_ref=land_ref.at[2 * x + y],
                send_sem=send_sems.at[j], recv_sem=recv_sems.at[j], device_id=(px, py, c),
                device_id_type=MESH).start()
        token[...] = jnp.zeros_like(token)

    return _call(
        body, name=name,
        out_shape=(pltpu.SemaphoreType.DMA((3,)), pltpu.SemaphoreType.DMA((3,)),
                   pltpu.HBM(land.shape, land.dtype), jax.ShapeDtypeStruct((8, LANES), F32)),
        in_specs=(_hbm_spec(),),
        out_specs=(_sem_spec(), _sem_spec(), _hbm_spec(), pl.BlockSpec(memory_space=pltpu.VMEM)),
        input_output_aliases={0: 2}, compiler_params=_split_params(),
    )(_hbm(land))


def gather_wait(send_sems, recv_sems, land, after, name):
    def body(land_ref, send_sems, recv_sems, after_ref, land_out):
        del after_ref, land_out
        x, y, c = _place()
        for j, (px, py) in enumerate(_other_chips(x, y)):
            cp = pltpu.make_async_remote_copy(
                src_ref=land_ref.at[2 * x + y], dst_ref=land_ref.at[2 * px + py],
                send_sem=send_sems.at[j], recv_sem=recv_sems.at[j], device_id=(px, py, c), device_id_type=MESH)
            cp.wait_send()
            cp.wait_recv()

    return _call(
        body, name=name,
        out_shape=pltpu.HBM(land.shape, land.dtype),
        in_specs=(_hbm_spec(), _sem_spec(), _sem_spec(), pl.BlockSpec(memory_space=pl.ANY)),
        out_specs=_hbm_spec(), input_output_aliases={0: 0}, compiler_params=_split_params(),
    )(land, send_sems, recv_sems, after)


def _flips():
    return [(fx, fy, fc) for fx in (0, 1) for fy in (0, 1) for fc in (0, 1) if (fx, fy, fc) != (0, 0, 0)]


def _flipped(x, y, c, flip):
    fx, fy, fc = flip
    return (1 - x if fx else x, 1 - y if fy else y, 1 - c if fc else c)


def gather_all_start(land, name):
    def body(land_ref, send_sems, recv_sems, land_thru, token):
        del land_thru
        x, y, c = _place()
        for k, flip in enumerate(_flips()):
            pltpu.make_async_remote_copy(
                src_ref=land_ref.at[4 * x + 2 * y + c], dst_ref=land_ref.at[4 * x + 2 * y + c],
                send_sem=send_sems.at[k], recv_sem=recv_sems.at[k], device_id=_flipped(x, y, c, flip),
                device_id_type=MESH).start()
        token[...] = jnp.zeros_like(token)

    return _call(
        body, name=name,
        out_shape=(pltpu.SemaphoreType.DMA((7,)), pltpu.SemaphoreType.DMA((7,)),
                   pltpu.HBM(land.shape, land.dtype), jax.ShapeDtypeStruct((8, LANES), F32)),
        in_specs=(_hbm_spec(),),
        out_specs=(_sem_spec(), _sem_spec(), _hbm_spec(), pl.BlockSpec(memory_space=pltpu.VMEM)),
        input_output_aliases={0: 2}, compiler_params=_split_params(),
    )(_hbm(land))


def gather_all_wait(send_sems, recv_sems, land, after, name):
    def body(land_ref, send_sems, recv_sems, after_ref, land_out):
        del after_ref, land_out
        x, y, c = _place()
        for k, flip in enumerate(_flips()):
            px, py, pc = _flipped(x, y, c, flip)
            cp = pltpu.make_async_remote_copy(
                src_ref=land_ref.at[4 * x + 2 * y + c], dst_ref=land_ref.at[4 * px + 2 * py + pc],
                send_sem=send_sems.at[k], recv_sem=recv_sems.at[k], device_id=(px, py, pc), device_id_type=MESH)
            cp.wait_send()
            cp.wait_recv()

    return _call(
        body, name=name,
        out_shape=pltpu.HBM(land.shape, land.dtype),
        in_specs=(_hbm_spec(), _sem_spec(), _sem_spec(), pl.BlockSpec(memory_space=pl.ANY)),
        out_specs=_hbm_spec(), input_output_aliases={0: 0}, compiler_params=_split_params(),
    )(land, send_sems, recv_sems, after)


def exchange_start(parts, name):
    _, r, c_ = parts.shape

    def body(parts_ref, land_ref, send_sems, recv_sems, parts_thru, land_thru, token):
        del parts_thru, land_thru
        x, y, c = _place()
        for j, (px, py) in enumerate(_other_chips(x, y)):
            pltpu.make_async_remote_copy(
                src_ref=parts_ref.at[2 * px + py], dst_ref=land_ref.at[j],
                send_sem=send_sems.at[j], recv_sem=recv_sems.at[j], device_id=(px, py, c),
                device_id_type=MESH).start()
        token[...] = jnp.zeros_like(token)

    return _call(
        body, name=name,
        out_shape=(pltpu.SemaphoreType.DMA((3,)), pltpu.SemaphoreType.DMA((3,)),
                   pltpu.HBM(parts.shape, parts.dtype), pltpu.HBM((3, r, c_), parts.dtype),
                   jax.ShapeDtypeStruct((8, LANES), F32)),
        in_specs=(_hbm_spec(), _hbm_spec()),
        out_specs=(_sem_spec(), _sem_spec(), _hbm_spec(), _hbm_spec(), pl.BlockSpec(memory_space=pltpu.VMEM)),
        input_output_aliases={0: 2, 1: 3}, compiler_params=_split_params(),
    )(_hbm(parts), _hbm(lax.empty((3, r, c_), parts.dtype)))


def exchange_wait(send_sems, recv_sems, parts, land, after, name):
    def body(parts_ref, land_ref, send_sems, recv_sems, after_ref, parts_out, land_out):
        del after_ref, parts_out, land_out
        x, y, c = _place()
        for j, (px, py) in enumerate(_other_chips(x, y)):
            cp = pltpu.make_async_remote_copy(
                src_ref=parts_ref.at[2 * px + py], dst_ref=land_ref.at[j],
                send_sem=send_sems.at[j], recv_sem=recv_sems.at[j], device_id=(px, py, c), device_id_type=MESH)
            cp.wait_send()
            cp.wait_recv()

    return _call(
        body, name=name,
        out_shape=(pltpu.HBM(parts.shape, parts.dtype), pltpu.HBM(land.shape, land.dtype)),
        in_specs=(_hbm_spec(), _hbm_spec(), _sem_spec(), _sem_spec(), pl.BlockSpec(memory_space=pl.ANY)),
        out_specs=(_hbm_spec(), _hbm_spec()), input_output_aliases={0: 0, 1: 1},
        compiler_params=_split_params(),
    )(parts, land, send_sems, recv_sems, after)


def cast_into_slot(w, chip, after, name):
    r, c = w.shape
    tr = min(256, r)

    def body(s_ref, w_ref, after_ref, o_ref):
        del s_ref, after_ref
        o_ref[...] = w_ref[...].astype(BF16)

    return _call(
        body, name=name,
        grid_spec=pltpu.PrefetchScalarGridSpec(
            num_scalar_prefetch=1, grid=(r // tr,),
            in_specs=[pl.BlockSpec((tr, c), lambda i, s: (i, 0)), pl.BlockSpec(memory_space=pl.ANY)],
            out_specs=pl.BlockSpec((None, tr, c), lambda i, s: (s[0], i, 0))),
        out_shape=jax.ShapeDtypeStruct((N_CHIPS, r, c), BF16),
        compiler_params=_params(),
    )(chip.reshape(1).astype(jnp.int32), w, after)


def sum_parts(parts, land, chip, name):
    _, r, c = parts.shape
    tr = min(256, r)

    def body(s_ref, p_ref, l_ref, o_ref):
        del s_ref
        acc = p_ref[...].astype(F32) + l_ref[0].astype(F32)
        acc = acc + l_ref[1].astype(F32)
        o_ref[...] = (acc + l_ref[2].astype(F32)).astype(BF16)

    return _call(
        body, name=name,
        grid_spec=pltpu.PrefetchScalarGridSpec(
            num_scalar_prefetch=1, grid=(r // tr,),
            in_specs=[pl.BlockSpec((None, tr, c), lambda i, s: (s[0], i, 0)),
                      pl.BlockSpec((3, tr, c), lambda i, s: (0, i, 0))],
            out_specs=pl.BlockSpec((tr, c), lambda i, s: (i, 0))),
        out_shape=jax.ShapeDtypeStruct((r, c), BF16),
        compiler_params=_params(),
    )(chip.reshape(1).astype(jnp.int32), parts, land)


def swap_sibling(arrs, name):
    n = len(arrs)

    def body(*refs):
        ins, outs = refs[:n], refs[n:2 * n]
        send_sems, recv_sems = refs[2 * n:]
        x, y, c = _place()
        cps = []
        for w in range(n):
            cp = pltpu.make_async_remote_copy(
                src_ref=ins[w], dst_ref=outs[w], send_sem=send_sems.at[w], recv_sem=recv_sems.at[w],
                device_id=(x, y, 1 - c), device_id_type=MESH)
            cp.start()
            cps.append(cp)
        for cp in cps:
            cp.wait_recv()
        for cp in cps:
            cp.wait_send()

    return _call(
        body, name=name,
        out_shape=[jax.ShapeDtypeStruct(a.shape, a.dtype) for a in arrs],
        in_specs=[_hbm_spec()] * n, out_specs=[_hbm_spec()] * n,
        scratch_shapes=[pltpu.SemaphoreType.DMA((n,)), pltpu.SemaphoreType.DMA((n,))],
    )(*arrs)


def adamw_pair(pa, pb, w, m, v, name):
    r, c = w.shape
    tr = min(128, r)

    def body(pa_ref, pb_ref, w_ref, m_ref, v_ref, g_ref, d_ref, nm_ref, nv_ref):
        g = pa_ref[...].astype(F32) + pb_ref[...].astype(F32)
        d, nm, nv = _adamw(w_ref[...], g, m_ref[...], v_ref[...])
        g_ref[...] = g
        d_ref[...] = d
        nm_ref[...] = nm
        nv_ref[...] = nv

    spec = pl.BlockSpec((tr, c), lambda i: (i, 0))
    return _call(
        body, name=name, grid=(r // tr,),
        out_shape=[jax.ShapeDtypeStruct((r, c), F32)] * 4,
        in_specs=[spec] * 5, out_specs=[spec] * 4,
        compiler_params=_params(),
    )(pa, pb, w, m, v)


def small_update(gathered, first_row, ws, ms, vs, name):
    n_w = len(ws)
    total_rows = gathered.shape[1]

    def body(*refs):
        g_ref = refs[0]
        w_refs, m_refs, v_refs = refs[1:1 + n_w], refs[1 + n_w:1 + 2 * n_w], refs[1 + 2 * n_w:1 + 3 * n_w]
        tail_ref = refs[1 + 3 * n_w]
        outs = refs[2 + 3 * n_w:2 + 7 * n_w]
        sum_ref = refs[2 + 7 * n_w]
        acc = g_ref[0]
        for k in range(1, N_DEV):
            acc = acc + g_ref[k]
        sum_ref[...] = acc
        row = first_row
        for p in range(n_w):
            a, b = ws[p].shape
            per = b // LANES
            g_out, d_out, m_out, v_out = outs[4 * p:4 * p + 4]
            if per == 1:
                g_out[...] = sum_ref[row:row + a, :]
            else:
                for i in range(a):
                    for jc in range(per):
                        g_out[i:i + 1, jc * LANES:(jc + 1) * LANES] = sum_ref[row + i * per + jc:row + i * per + jc + 1, :]
            row += a * per
            dl, nm, nv = _adamw(w_refs[p][...], g_out[...], m_refs[p][...], v_refs[p][...])
            d_out[...] = dl
            m_out[...] = nm
            v_out[...] = nv
        tail_ref[...] = sum_ref[row:row + 1, :]

    out_shape = [jax.ShapeDtypeStruct((1, LANES), F32)]
    for w in ws:
        out_shape += [jax.ShapeDtypeStruct(w.shape, F32)] * 4
    res = _call(
        body, name=name, out_shape=out_shape,
        scratch_shapes=[pltpu.VMEM((total_rows, LANES), F32)],
        compiler_params=_params(),
    )(gathered, *ws, *ms, *vs)
    return res[0], [res[1 + 4 * p:5 + 4 * p] for p in range(n_w)]


def ada_fwd(c_all, w_ada, b_cols, name):
    n_l, d, cols = w_ada.shape
    nb = c_all.shape[0]
    tn = 256

    def body(c_ref, w_ref, b_ref, o_ref):
        cv = c_ref[...]
        ca = (cv * _sigmoid(cv)).astype(BF16)
        o_ref[...] = _dot(ca, w_ref[...].astype(BF16), NN) + b_ref[...]

    return _call(
        body, name=name, grid=(n_l, cols // tn),
        out_shape=jax.ShapeDtypeStruct((n_l, nb, cols), F32),
        in_specs=[pl.BlockSpec((nb, d), lambda l, j: (0, 0)),
                  pl.BlockSpec((None, d, tn), lambda l, j: (l, 0, j)),
                  pl.BlockSpec((None, 1, tn), lambda l, j: (l, 0, j))],
        out_specs=pl.BlockSpec((None, nb, tn), lambda l, j: (l, 0, j)),
        compiler_params=_params(),
    )(c_all, w_ada, b_cols)


def ada_bwd(c_all, dmod_cols, w, m, v, name):
    n_l, d, cols = w.shape
    nb = c_all.shape[0]
    tn = 256

    def body(c_ref, dm_ref, w_ref, m_ref, v_ref, g_ref, d_ref, nm_ref, nv_ref):
        cv = c_ref[...]
        ca = (cv * _sigmoid(cv)).astype(BF16)
        g = _dot(ca, dm_ref[...].astype(BF16), TN)
        dl, nm, nv = _adamw(w_ref[...], g, m_ref[...], v_ref[...])
        g_ref[...] = g
        d_ref[...] = dl
        nm_ref[...] = nm
        nv_ref[...] = nv

    wspec = pl.BlockSpec((None, d, tn), lambda l, j: (l, 0, j))
    return _call(
        body, name=name, grid=(n_l, cols // tn),
        out_shape=[jax.ShapeDtypeStruct((n_l, d, cols), F32)] * 4,
        in_specs=[pl.BlockSpec((nb, d), lambda l, j: (0, 0)),
                  pl.BlockSpec((None, nb, tn), lambda l, j: (l, 0, j)),
                  wspec, wspec, wspec],
        out_specs=[wspec] * 4,
        compiler_params=_params(),
    )(c_all, dmod_cols, w, m, v)


def bias_update(dmod_all, w, m, v, name):
    def body(dm_ref, w_ref, m_ref, v_ref, g_ref, d_ref, nm_ref, nv_ref):
        g = jnp.sum(dm_ref[...], axis=0, keepdims=True)
        dl, nm, nv = _adamw(w_ref[...], g, m_ref[...], v_ref[...])
        g_ref[...] = g
        d_ref[...] = dl
        nm_ref[...] = nm
        nv_ref[...] = nv

    return _call(
        body, name=name,
        out_shape=[jax.ShapeDtypeStruct(w.shape, F32)] * 4,
        compiler_params=_params(),
    )(dmod_all, w, m, v)


def inproj_fwd(x, mod, ng, wg, seq, sectioned, name):
    m_rows, d = x.shape
    nsh, _, ns = wg.shape
    n = nsh * ns
    tm, tn = min(ROW_TILE, seq), ns
    per = ns // tn

    def body(x_ref, mod_ref, ng_ref, w_ref, proj_ref, h_ref):
        @pl.when(pl.program_id(1) == 0)
        def _():
            xv = x_ref[...]
            r = lax.rsqrt(jnp.mean(xv * xv, axis=-1, keepdims=True) + EPS)
            md = mod_ref[0]
            h = (xv * r * ng_ref[...]) * (1.0 + md[:, d:2 * d]) + md[:, :d]
            h_ref[...] = h.astype(BF16)
        proj_ref[...] = _dot(h_ref[...], w_ref[...], NN)

    if sectioned:
        proj_shape = (nsh, m_rows, ns)
        proj_spec = pl.BlockSpec((None, tm, tn), lambda i, j: (j // per, i, j % per))
    else:
        proj_shape = (m_rows, n)
        proj_spec = pl.BlockSpec((tm, tn), lambda i, j: (i, j))
    return _call(
        body, name=name, grid=(m_rows // tm, n // tn),
        out_shape=[jax.ShapeDtypeStruct(proj_shape, F32), jax.ShapeDtypeStruct((m_rows, d), BF16)],
        in_specs=[pl.BlockSpec((tm, d), lambda i, j: (i, 0)),
                  pl.BlockSpec((1, 1, 3 * d), lambda i, j: ((i * tm) // seq, 0, 0)),
                  pl.BlockSpec((1, d), lambda i, j: (0, 0)),
                  pl.BlockSpec((None, d, tn), lambda i, j: (j // per, 0, j % per))],
        out_specs=[proj_spec, pl.BlockSpec((tm, d), lambda i, j: (i, 0))],
        compiler_params=_params(),
    )(x, mod, ng, wg)


def outproj_fwd(y, w, x, mod, seq, name):
    m_rows, di = y.shape
    d = w.shape[1]
    tm = min(ROW_TILE, seq)

    def body(y_ref, w_ref, x_ref, mod_ref, xn_ref, out_ref):
        acc = _dot(y_ref[...], w_ref[...], NN)
        out_ref[...] = acc.astype(BF16)
        xn_ref[...] = x_ref[...] + mod_ref[0][:, 2 * d:] * acc

    row = pl.BlockSpec((tm, d), lambda i: (i, 0))
    return _call(
        body, name=name, grid=(m_rows // tm,),
        out_shape=[jax.ShapeDtypeStruct((m_rows, d), F32), jax.ShapeDtypeStruct((m_rows, d), BF16)],
        in_specs=[pl.BlockSpec((tm, di), lambda i: (i, 0)),
                  pl.BlockSpec((di, d), lambda i: (0, 0)),
                  row,
                  pl.BlockSpec((1, 1, 3 * d), lambda i: ((i * tm) // seq, 0, 0))],
        out_specs=[row, row],
        compiler_params=_params(),
    )(y, w, x, mod)


def outproj_bwd(dxo, out, mod, w, seq, name):
    m_rows, d = dxo.shape
    di = w.shape[0]
    nb = m_rows // seq
    tm, tn = min(ROW_TILE, seq), _col_tile(di)

    def body(dxo_ref, out_ref, mod_ref, w_ref, dy_ref, dout_ref, dgate_ref):
        i = pl.program_id(0)

        @pl.when(pl.program_id(1) == 0)
        def _():
            dx = dxo_ref[...]
            dout_ref[...] = (mod_ref[0][:, 2 * d:] * dx).astype(BF16)
            part = jnp.sum(dx * out_ref[...].astype(F32), axis=0, keepdims=True)

            @pl.when((i * tm) % seq == 0)
            def _():
                dgate_ref[0] = part

            @pl.when((i * tm) % seq != 0)
            def _():
                dgate_ref[0] = dgate_ref[0] + part

        dy_ref[...] = _dot(dout_ref[...], w_ref[...], NT).astype(BF16)

    row = pl.BlockSpec((tm, d), lambda i, j: (i, 0))
    return _call(
        body, name=name, grid=(m_rows // tm, di // tn),
        out_shape=[jax.ShapeDtypeStruct((m_rows, di), BF16), jax.ShapeDtypeStruct((m_rows, d), BF16),
                   jax.ShapeDtypeStruct((nb, 1, d), F32)],
        in_specs=[row, row,
                  pl.BlockSpec((1, 1, 3 * d), lambda i, j: ((i * tm) // seq, 0, 0)),
                  pl.BlockSpec((tn, d), lambda i, j: (j, 0))],
        out_specs=[pl.BlockSpec((tm, tn), lambda i, j: (i, j)), row,
                   pl.BlockSpec((1, 1, d), lambda i, j: ((i * tm) // seq, 0, 0))],
        compiler_params=_params(),
    )(dxo, out, mod, w)


def grad_w_out(y, dout, name):
    m_rows, di = y.shape
    d = dout.shape[1]
    tm, tk = min(ROW_TILE, m_rows), _col_tile(di)
    n_m = m_rows // tm

    def body(y_ref, do_ref, o_ref, acc_ref):
        mi = pl.program_id(1)

        @pl.when(mi == 0)
        def _():
            acc_ref[...] = jnp.zeros_like(acc_ref)

        acc_ref[...] += _dot(y_ref[...], do_ref[...], TN)

        @pl.when(mi == n_m - 1)
        def _():
            o_ref[...] = acc_ref[...].astype(BF16)

    return _call(
        body, name=name, grid=(di // tk, n_m),
        out_shape=jax.ShapeDtypeStruct((di, d), BF16),
        in_specs=[pl.BlockSpec((tm, tk), lambda j, mi: (mi, j)),
                  pl.BlockSpec((tm, d), lambda j, mi: (mi, 0))],
        out_specs=pl.BlockSpec((tk, d), lambda j, mi: (j, 0)),
        scratch_shapes=[pltpu.VMEM((tk, d), F32)],
        compiler_params=_params(),
    )(y, dout)


def grad_w_in(h, dproj, nsh, sectioned, name):
    m_rows, d = h.shape
    n = dproj.shape[0] * dproj.shape[2] if sectioned else dproj.shape[1]
    ns = n // nsh
    tm, tn = min(ROW_TILE, m_rows), _col_tile(ns)
    per = ns // tn
    n_m = m_rows // tm

    def body(h_ref, dp_ref, o_ref, acc_ref):
        mi = pl.program_id(1)
        @pl.when(mi == 0)
        def _():
            acc_ref[...] = jnp.zeros_like(acc_ref)

        acc_ref[...] += _dot(h_ref[...], dp_ref[...], TN)

        @pl.when(mi == n_m - 1)
        def _():
            o_ref[...] = acc_ref[...].astype(BF16)

    if sectioned:
        dp_spec = pl.BlockSpec((None, tm, tn), lambda j, mi: (j // per, mi, j % per))
    else:
        dp_spec = pl.BlockSpec((tm, tn), lambda j, mi: (mi, j))
    return _call(
        body, name=name, grid=(n // tn, n_m),
        out_shape=jax.ShapeDtypeStruct((nsh, d, ns), BF16),
        in_specs=[pl.BlockSpec((tm, d), lambda j, mi: (mi, 0)), dp_spec],
        out_specs=pl.BlockSpec((None, d, tn), lambda j, mi: (j // per, 0, j % per)),
        scratch_shapes=[pltpu.VMEM((d, tn), F32)],
        compiler_params=_params(),
    )(h, dproj)


def inproj_bwd(dproj, wg, x, dxo, mod, ng, seq, sectioned, name):
    m_rows, d = x.shape
    nsh, _, ns = wg.shape
    n = nsh * ns
    nb = m_rows // seq
    tm, tk = min(ROW_TILE, seq), _col_tile(ns)
    per = ns // tk
    n_k = n // tk

    def body(dp_ref, w_ref, x_ref, dxo_ref, mod_ref, ng_ref, dxi_ref, dsh_ref, dsc_ref, dng_ref, acc_ref):
        i, k = pl.program_id(0), pl.program_id(1)
        @pl.when(k == 0)
        def _():
            acc_ref[...] = jnp.zeros_like(acc_ref)

        acc_ref[...] += _dot(dp_ref[...], w_ref[...], NT)

        @pl.when(k == n_k - 1)
        def _():
            dh = acc_ref[...]
            xv = x_ref[...]
            r = lax.rsqrt(jnp.mean(xv * xv, axis=-1, keepdims=True) + EPS)
            xn = xv * r
            md = mod_ref[0]
            gain = ng_ref[...]
            p_shift = jnp.sum(dh, axis=0, keepdims=True)
            p_scale = jnp.sum(dh * (xn * gain), axis=0, keepdims=True)
            drn = dh * (1.0 + md[:, d:2 * d])
            p_ng = jnp.sum(drn * xn, axis=0, keepdims=True)
            dxn = drn * gain
            dx = r * (dxn - xn * jnp.mean(dxn * xn, axis=-1, keepdims=True))
            dxi_ref[...] = dxo_ref[...] + dx

            @pl.when((i * tm) % seq == 0)
            def _():
                dsh_ref[0] = p_shift
                dsc_ref[0] = p_scale

            @pl.when((i * tm) % seq != 0)
            def _():
                dsh_ref[0] = dsh_ref[0] + p_shift
                dsc_ref[0] = dsc_ref[0] + p_scale

            @pl.when(i == 0)
            def _():
                dng_ref[...] = p_ng

            @pl.when(i != 0)
            def _():
                dng_ref[...] = dng_ref[...] + p_ng

    if sectioned:
        dp_spec = pl.BlockSpec((None, tm, tk), lambda i, k: (k // per, i, k % per))
    else:
        dp_spec = pl.BlockSpec((tm, tk), lambda i, k: (i, k))
    row = pl.BlockSpec((tm, d), lambda i, k: (i, 0))
    per_seq = pl.BlockSpec((1, 1, d), lambda i, k: ((i * tm) // seq, 0, 0))
    return _call(
        body, name=name, grid=(m_rows // tm, n_k),
        out_shape=[jax.ShapeDtypeStruct((m_rows, d), F32), jax.ShapeDtypeStruct((nb, 1, d), F32),
                   jax.ShapeDtypeStruct((nb, 1, d), F32), jax.ShapeDtypeStruct((1, d), F32)],
        in_specs=[dp_spec,
                  pl.BlockSpec((None, d, tk), lambda i, k: (k // per, 0, k % per)),
                  row, row,
                  pl.BlockSpec((1, 1, 3 * d), lambda i, k: ((i * tm) // seq, 0, 0)),
                  pl.BlockSpec((1, d), lambda i, k: (0, 0))],
        out_specs=[row, per_seq, per_seq, pl.BlockSpec((1, d), lambda i, k: (0, 0))],
        scratch_shapes=[pltpu.VMEM((tm, d), F32)],
        compiler_params=_params(),
    )(dproj, wg, x, dxo, mod, ng)


def _sgu_stats(proj_ref, vg_ref, di, gd, dgel_ref=None):
    s1 = jnp.zeros((SG_BLOCK, 1), F32)
    for g in range(SG_GROUPS):
        v_pre = proj_ref[:, di + g * gd:di + (g + 1) * gd]
        if dgel_ref is None:
            vg = _gelu(v_pre)
        else:
            vg, dgel_ref[:, g * gd:(g + 1) * gd] = _gelu_and_grad(v_pre)
        vg_ref[:, g * gd:(g + 1) * gd] = vg
        s1 = s1 + jnp.sum(vg, axis=1, keepdims=True)
    mu = s1 / di
    s2 = jnp.zeros((SG_BLOCK, 1), F32)
    for g in range(SG_GROUPS):
        dv = vg_ref[:, g * gd:(g + 1) * gd] - mu
        s2 = s2 + jnp.sum(dv * dv, axis=1, keepdims=True)
    return mu, lax.rsqrt(s2 / di + EPS)


def sgu_fwd(proj, ln_gain, ln_bias, ws, bs, name):
    m_rows, n3 = proj.shape
    di = n3 // 3
    gd = di // SG_GROUPS

    def body(proj_ref, lg_ref, lb_ref, ws_ref, bs_ref, y_ref, wsm_ref, vg_ref):
        @pl.when(pl.program_id(0) == 0)
        def _():
            mask = _chunk_mask()
            for g in range(SG_GROUPS):
                wsm_ref[g] = jnp.where(mask, ws_ref[g], 0.0).astype(BF16)

        mu, rstd = _sgu_stats(proj_ref, vg_ref, di, gd)
        for g in range(SG_GROUPS):
            cs = slice(g * gd, (g + 1) * gd)
            vln = (vg_ref[:, cs] - mu) * rstd * lg_ref[:, cs] + lb_ref[:, cs]
            s = _dot(wsm_ref[g], vln.astype(BF16), NN) + bs_ref[g]
            u = _gelu(proj_ref[:, cs])
            gp = proj_ref[:, 2 * di + g * gd:2 * di + (g + 1) * gd]
            y_ref[:, cs] = (u * s * (gp * _sigmoid(gp))).astype(BF16)

    full = lambda shape: pl.BlockSpec(shape, lambda i: (0,) * len(shape))
    return _call(
        body, name=name, grid=(m_rows // SG_BLOCK,),
        out_shape=jax.ShapeDtypeStruct((m_rows, di), BF16),
        in_specs=[pl.BlockSpec((SG_BLOCK, n3), lambda i: (i, 0)),
                  full((1, di)), full((1, di)),
                  full((SG_GROUPS, SG_BLOCK, SG_BLOCK)), full((SG_GROUPS, SG_BLOCK, 1))],
        out_specs=pl.BlockSpec((SG_BLOCK, di), lambda i: (i, 0)),
        scratch_shapes=[pltpu.VMEM((SG_GROUPS, SG_BLOCK, SG_BLOCK), BF16), pltpu.VMEM((SG_BLOCK, di), F32)],
        compiler_params=_params(),
    )(proj, ln_gain, ln_bias, ws, bs)


def sgu_bwd(proj, dy, ln_gain, ln_bias, ws, bs, name):
    m_rows, n3 = proj.shape
    di = n3 // 3
    gd = di // SG_GROUPS
    n_i = m_rows // SG_BLOCK

    def body(proj_ref, dy_ref, lg_ref, lb_ref, ws_ref, bs_ref,
             dp_ref, dws_ref, dbs_ref, dlg_ref, dlb_ref, wsm_ref, vg_ref, dvh_ref, dgel_ref):
        i = pl.program_id(0)

        def before():
            @pl.when(i == 0)
            def _():
                mask = _chunk_mask()
                for g in range(SG_GROUPS):
                    wsm_ref[g] = jnp.where(mask, ws_ref[g], 0.0).astype(BF16)
                dws_ref[...] = jnp.zeros_like(dws_ref)
                dbs_ref[...] = jnp.zeros_like(dbs_ref)
                dlg_ref[...] = jnp.zeros_like(dlg_ref)
                dlb_ref[...] = jnp.zeros_like(dlb_ref)

        def after():
            @pl.when(i == n_i - 1)
            def _():
                mask = _chunk_mask()
                for g in range(SG_GROUPS):
                    dws_ref[g] = jnp.where(mask, dws_ref[g], 0.0)

        before()
        mu, rstd = _sgu_stats(proj_ref, vg_ref, di, gd, dgel_ref)
        m1 = jnp.zeros((SG_BLOCK, 1), F32)
        m2 = jnp.zeros((SG_BLOCK, 1), F32)
        for g in range(SG_GROUPS):
            cs = slice(g * gd, (g + 1) * gd)
            gs = slice(2 * di + g * gd, 2 * di + (g + 1) * gd)
            gain = lg_ref[:, cs]
            vhat = (vg_ref[:, cs] - mu) * rstd
            vln_b = (vhat * gain + lb_ref[:, cs]).astype(BF16)
            s = _dot(wsm_ref[g], vln_b, NN) + bs_ref[g]
            u, du = _gelu_and_grad(proj_ref[:, cs])
            sg, dsg = _silu_and_grad(proj_ref[:, gs])
            dyv = dy_ref[:, cs].astype(F32)
            dp_ref[:, cs] = (dyv * s * sg * du).astype(BF16)
            dp_ref[:, gs] = (dyv * u * s * dsg).astype(BF16)
            ds = dyv * u * sg
            ds_b = ds.astype(BF16)
            dws_ref[g] = dws_ref[g] + _dot(ds_b, vln_b, NT)
            dbs_ref[g] = dbs_ref[g] + jnp.sum(ds, axis=1, keepdims=True)
            dvln = _dot(wsm_ref[g], ds_b, TN)
            dlg_ref[:, cs] = dlg_ref[:, cs] + jnp.sum(dvln * vhat, axis=0, keepdims=True)
            dlb_ref[:, cs] = dlb_ref[:, cs] + jnp.sum(dvln, axis=0, keepdims=True)
            dvh = dvln * gain
            dvh_ref[:, cs] = dvh
            m1 = m1 + jnp.sum(dvh, axis=1, keepdims=True)
            m2 = m2 + jnp.sum(dvh * vhat, axis=1, keepdims=True)
        m1 = m1 / di
        m2 = m2 / di
        for g in range(SG_GROUPS):
            cs = slice(g * gd, (g + 1) * gd)
            vs = slice(di + g * gd, di + (g + 1) * gd)
            vhat = (vg_ref[:, cs] - mu) * rstd
            dvg = rstd * (dvh_ref[:, cs] - m1 - vhat * m2)
            dp_ref[:, vs] = (dvg * dgel_ref[:, cs]).astype(BF16)

        after()

    full = lambda shape: pl.BlockSpec(shape, lambda i: (0,) * len(shape))
    return _call(
        body, name=name, grid=(n_i,),
        out_shape=[jax.ShapeDtypeStruct((m_rows, n3), BF16),
                   jax.ShapeDtypeStruct((SG_GROUPS, SG_BLOCK, SG_BLOCK), F32),
                   jax.ShapeDtypeStruct((SG_GROUPS, SG_BLOCK, 1), F32),
                   jax.ShapeDtypeStruct((1, di), F32), jax.ShapeDtypeStruct((1, di), F32)],
        in_specs=[pl.BlockSpec((SG_BLOCK, n3), lambda i: (i, 0)),
                  pl.BlockSpec((SG_BLOCK, di), lambda i: (i, 0)),
                  full((1, di)), full((1, di)),
                  full((SG_GROUPS, SG_BLOCK, SG_BLOCK)), full((SG_GROUPS, SG_BLOCK, 1))],
        out_specs=[pl.BlockSpec((SG_BLOCK, n3), lambda i: (i, 0)),
                   full((SG_GROUPS, SG_BLOCK, SG_BLOCK)), full((SG_GROUPS, SG_BLOCK, 1)),
                   full((1, di)), full((1, di))],
        scratch_shapes=[pltpu.VMEM((SG_GROUPS, SG_BLOCK, SG_BLOCK), BF16),
                        pltpu.VMEM((SG_BLOCK, di), F32), pltpu.VMEM((SG_BLOCK, di), F32),
                        pltpu.VMEM((SG_BLOCK, di), F32)],
        compiler_params=_params(),
    )(proj, dy, ln_gain, ln_bias, ws, bs)


def _lower_bound(lbraw):
    mx = jnp.maximum(lbraw[0:1, :], lbraw[1:2, :])
    e0 = jnp.exp(lbraw[0:1, :] - mx)
    e1 = jnp.exp(lbraw[1:2, :] - mx)
    p0 = e0 / (e0 + e1)
    p1 = e1 / (e0 + e1)
    return (p0 + p1) - p0, p0, p1


def _tri(lower):
    r = lax.broadcasted_iota(jnp.int32, (CHUNK, CHUNK), 0)
    c = lax.broadcasted_iota(jnp.int32, (CHUNK, CHUNK), 1)
    return ((r >= c) if lower else (c >= r)).astype(F32)


def _row(a, idx):
    r = lax.broadcasted_iota(jnp.int32, a.shape, 0)
    return jnp.sum(jnp.where(r == idx, a, 0.0), axis=0, keepdims=True)


def _hgrn_gates(qp, fp, lb, tri):
    sgm = _sigmoid_small(fp)
    f = lb + (1.0 - lb) * sgm
    k = 1.0 - f
    a = _dot(tri, jnp.log(f), NN, precision=lax.Precision.HIGHEST)
    a_mid = _row(a, CHUNK // 2 - 1)
    a_last = _row(a, CHUNK - 1)
    q, dq = _silu_and_grad(qp)
    e1, e2, e3, e4 = jnp.exp(a - a_mid), jnp.exp(a_mid - a), jnp.exp(a), jnp.exp(a_last - a)
    return dict(sgm=sgm, f=f, k=k, q=q, dq=dq, e1=e1, e2=e2, e3=e3, e4=e4, dec=jnp.exp(a_last),
                q_in=q * e1, k_in=k * e2, q_out=q * e3, k_out=k * e4)


def _causal():
    r = lax.broadcasted_iota(jnp.int32, (CHUNK, CHUNK), 0)
    c = lax.broadcasted_iota(jnp.int32, (CHUNK, CHUNK), 1)
    return r >= c


def hgrn_fwd(proj4, lbraw, gn, seq, name):
    _, m_rows, di = proj4.shape
    nb, nh, nc = m_rows // seq, di // HEAD_DIM, seq // CHUNK
    rows = min(HG_ROWS, seq)
    wide = HG_WIDE * HEAD_DIM
    ns, cpb = seq // rows, rows // CHUNK

    def body(p_ref, lb_ref, gn_ref, y_ref, sts_ref, st_ref):
        @pl.when(pl.program_id(2) == 0)
        def _():
            st_ref[...] = jnp.zeros_like(st_ref)

        tri = _tri(True)
        causal = _causal()
        gain = gn_ref[...]
        lbs = [_lower_bound(lb_ref[:, j * HEAD_DIM:(j + 1) * HEAD_DIM])[0] for j in range(HG_WIDE)]

        units = [(n, j) for n in range(cpb) for j in range(HG_WIDE)]
        rs = lambda n: slice(n * CHUNK, (n + 1) * CHUNK)
        cs = lambda j: slice(j * HEAD_DIM, (j + 1) * HEAD_DIM)
        gates, v_b, sc_b, kv, o_in, o_x = {}, {}, {}, {}, {}, {}
        for n, j in units:
            gates[n, j] = _hgrn_gates(p_ref[0, rs(n), cs(j)], p_ref[1, rs(n), cs(j)], lbs[j], tri)
            v_b[n, j] = p_ref[2, rs(n), cs(j)].astype(BF16)
        for u in units:
            t = gates[u]
            sc_b[u] = jnp.where(causal, _dot(t["q_in"].astype(BF16), t["k_in"].astype(BF16), NT), 0.0).astype(BF16)
            kv[u] = _dot(v_b[u], t["k_out"].astype(BF16), TN)
        for u in units:
            o_in[u] = _dot(sc_b[u], v_b[u], NN)
        for j in range(HG_WIDE):
            st = st_ref[j]
            for n in range(cpb):
                sts_ref[n, :, cs(j)] = st
                o_x[n, j] = _dot(gates[n, j]["q_out"].astype(BF16), st.astype(BF16), NT)
                st = st * gates[n, j]["dec"] + kv[n, j]
            st_ref[j] = st
        for n, j in units:
            o = o_in[n, j] + o_x[n, j]
            r = lax.rsqrt(jnp.mean(o * o, axis=-1, keepdims=True) + EPS)
            gp = p_ref[3, rs(n), cs(j)]
            y_ref[rs(n), cs(j)] = ((o * r * gain) * (gp * _sigmoid(gp))).astype(BF16)

    return _call(
        body, name=name, grid=(nh // HG_WIDE, nb, ns),
        out_shape=[jax.ShapeDtypeStruct((m_rows, di), BF16),
                   jax.ShapeDtypeStruct((nb * nc, HEAD_DIM, di), F32)],
        in_specs=[pl.BlockSpec((4, rows, wide), lambda hg, b, s: (0, b * ns + s, hg)),
                  pl.BlockSpec((2, wide), lambda hg, b, s: (0, hg)),
                  pl.BlockSpec((1, HEAD_DIM), lambda hg, b, s: (0, 0))],
        out_specs=[pl.BlockSpec((rows, wide), lambda hg, b, s: (b * ns + s, hg)),
                   pl.BlockSpec((cpb, HEAD_DIM, wide), lambda hg, b, s: (b * ns + s, 0, hg))],
        scratch_shapes=[pltpu.VMEM((HG_WIDE, HEAD_DIM, HEAD_DIM), F32)],
        compiler_params=_params(),
    )(proj4, lbraw, gn)


def hgrn_bwd(proj4, dy, sts, lbraw, gn, seq, name):
    _, m_rows, di = proj4.shape
    nb, nh, nc = m_rows // seq, di // HEAD_DIM, seq // CHUNK
    rows = min(HG_ROWS, seq)
    wide = HG_WIDE * HEAD_DIM
    ns, cpb = seq // rows, rows // CHUNK
    n_hg = nh // HG_WIDE

    def body(p_ref, dy_ref, sts_ref, lb_ref, gn_ref, dp_ref, dlb_ref, dgn_ref, dst_ref, lbacc_ref, gnacc_ref):
        hg, b, s = pl.program_id(0), pl.program_id(1), pl.program_id(2)
        tri, triu = _tri(True), _tri(False)
        causal = _causal()
        gain = gn_ref[...]
        first = (b == 0) & (s == 0)
        cs = lambda j: slice(j * HEAD_DIM, (j + 1) * HEAD_DIM)

        def before():
            @pl.when((hg == 0) & first)
            def _():
                gnacc_ref[...] = jnp.zeros_like(gnacc_ref)

            @pl.when(first)
            def _():
                lbacc_ref[...] = jnp.zeros_like(lbacc_ref)

            @pl.when(s == 0)
            def _():
                dst_ref[...] = jnp.zeros_like(dst_ref)

        def after():
            @pl.when((b == nb - 1) & (s == ns - 1))
            def _():
                for j in range(HG_WIDE):
                    _, p0, p1 = _lower_bound(lb_ref[:, cs(j)])
                    acc = lbacc_ref[:, cs(j)]
                    dlb_ref[0:1, cs(j)] = -acc * p0 * p1
                    dlb_ref[1:2, cs(j)] = acc * p1 * (1.0 - p1)

            @pl.when((hg == n_hg - 1) & (b == nb - 1) & (s == ns - 1))
            def _():
                tot = gnacc_ref[:, 0:HEAD_DIM]
                for j in range(1, HG_WIDE):
                    tot = tot + gnacc_ref[:, cs(j)]
                dgn_ref[...] = tot

        before()

        units = [(n, j) for n in range(cpb) for j in range(HG_WIDE)]
        rs = lambda n: slice(n * CHUNK, (n + 1) * CHUNK)
        lbs = [_lower_bound(lb_ref[:, cs(j)])[0] for j in range(HG_WIDE)]
        gates, v_b, st_b, sc_b, o, do_b = {}, {}, {}, {}, {}, {}
        dq_out, dsc_b, dv, g_st, dq_in, dk_in, dst_at, dk_out, ddec = {}, {}, {}, {}, {}, {}, {}, {}, {}
        for n, j in units:
            gates[n, j] = _hgrn_gates(p_ref[0, rs(n), cs(j)], p_ref[1, rs(n), cs(j)], lbs[j], tri)
            v_b[n, j] = p_ref[2, rs(n), cs(j)].astype(BF16)
            st_b[n, j] = sts_ref[n, :, cs(j)].astype(BF16)
        for u in units:
            t = gates[u]
            sc_b[u] = jnp.where(causal, _dot(t["q_in"].astype(BF16), t["k_in"].astype(BF16), NT), 0.0).astype(BF16)
        for u in units:
            o[u] = _dot(sc_b[u], v_b[u], NN) + _dot(gates[u]["q_out"].astype(BF16), st_b[u], NT)
        for n, j in units:
            ov = o[n, j]
            r = lax.rsqrt(jnp.mean(ov * ov, axis=-1, keepdims=True) + EPS)
            ohat = ov * r
            sg, dsg = _silu_and_grad(p_ref[3, rs(n), cs(j)])
            dyv = dy_ref[rs(n), cs(j)].astype(F32)
            dp_ref[3, rs(n), cs(j)] = (dyv * (ohat * gain) * dsg).astype(BF16)
            d_on = dyv * sg
            gnacc_ref[:, cs(j)] = gnacc_ref[:, cs(j)] + jnp.sum(d_on * ohat, axis=0, keepdims=True)
            dohat = d_on * gain
            do_b[n, j] = (r * (dohat - ohat * jnp.mean(dohat * ohat, axis=-1, keepdims=True))).astype(BF16)
        for u in units:
            dq_out[u] = _dot(do_b[u], st_b[u], NN)
            dsc_b[u] = jnp.where(causal, _dot(do_b[u], v_b[u], NT), 0.0).astype(BF16)
            dv[u] = _dot(sc_b[u], do_b[u], TN)
            g_st[u] = _dot(do_b[u], gates[u]["q_out"].astype(BF16), TN)
        for u in units:
            dq_in[u] = _dot(dsc_b[u], gates[u]["k_in"].astype(BF16), NN)
            dk_in[u] = _dot(dsc_b[u], gates[u]["q_in"].astype(BF16), TN)
        for j in range(HG_WIDE):
            dst = dst_ref[j]
            for n in reversed(range(cpb)):
                dst_at[n, j] = dst
                dst = dst * gates[n, j]["dec"] + g_st[n, j]
            dst_ref[j] = dst
        for n, j in units:
            dst = dst_at[n, j]
            dst_b = dst.astype(BF16)
            dk_out[n, j] = _dot(v_b[n, j], dst_b, NN)
            dv[n, j] = dv[n, j] + _dot(gates[n, j]["k_out"].astype(BF16), dst_b, NT)
            ddec[n, j] = jnp.sum(dst * sts_ref[n, :, cs(j)], axis=0, keepdims=True)
        for n, j in units:
            t = gates[n, j]
            dp_ref[2, rs(n), cs(j)] = dv[n, j].astype(BF16)
            dq = dq_in[n, j] * t["e1"] + dq_out[n, j] * t["e3"]
            dk = dk_in[n, j] * t["e2"] + dk_out[n, j] * t["e4"]
            w_in = dq_in[n, j] * t["q_in"] - dk_in[n, j] * t["k_in"]
            w_out = dk_out[n, j] * t["k_out"]
            da = w_in + dq_out[n, j] * t["q_out"] - w_out
            da_mid = -jnp.sum(w_in, axis=0, keepdims=True)
            da_last = jnp.sum(w_out, axis=0, keepdims=True) + ddec[n, j] * t["dec"]
            rid = lax.broadcasted_iota(jnp.int32, da.shape, 0)
            da = da + jnp.where(rid == CHUNK // 2 - 1, da_mid, 0.0) + jnp.where(rid == CHUNK - 1, da_last, 0.0)
            dlf = _dot(triu, da, NN, precision=lax.Precision.HIGHEST)
            df = dlf / t["f"] - dk
            sgm = t["sgm"]
            dp_ref[1, rs(n), cs(j)] = (df * (1.0 - lbs[j]) * sgm * (1.0 - sgm)).astype(BF16)
            lbacc_ref[:, cs(j)] = lbacc_ref[:, cs(j)] + jnp.sum(df * (1.0 - sgm), axis=0, keepdims=True)
            dp_ref[0, rs(n), cs(j)] = (dq * t["dq"]).astype(BF16)

        after()

    blk = lambda hg, b, s: b * ns + (ns - 1 - s)
    return _call(
        body, name=name, grid=(n_hg, nb, ns),
        out_shape=[jax.ShapeDtypeStruct((4, m_rows, di), BF16), jax.ShapeDtypeStruct((2, di), F32),
                   jax.ShapeDtypeStruct((1, HEAD_DIM), F32)],
        in_specs=[pl.BlockSpec((4, rows, wide), lambda hg, b, s: (0, blk(hg, b, s), hg)),
                  pl.BlockSpec((rows, wide), lambda hg, b, s: (blk(hg, b, s), hg)),
                  pl.BlockSpec((cpb, HEAD_DIM, wide), lambda hg, b, s: (blk(hg, b, s), 0, hg)),
                  pl.BlockSpec((2, wide), lambda hg, b, s: (0, hg)),
                  pl.BlockSpec((1, HEAD_DIM), lambda hg, b, s: (0, 0))],
        out_specs=[pl.BlockSpec((4, rows, wide), lambda hg, b, s: (0, blk(hg, b, s), hg)),
                   pl.BlockSpec((2, wide), lambda hg, b, s: (0, hg)),
                   pl.BlockSpec((1, HEAD_DIM), lambda hg, b, s: (0, 0))],
        scratch_shapes=[pltpu.VMEM((HG_WIDE, HEAD_DIM, HEAD_DIM), F32), pltpu.VMEM((1, wide), F32),
                        pltpu.VMEM((1, wide), F32)],
        compiler_params=_params(),
    )(proj4, dy, sts, lbraw, gn)


def outproj_loss(y, w, x, mod, fg, target, seq, name):
    m_rows, di = y.shape
    d = w.shape[1]
    tm = min(512, seq)

    def body(y_ref, w_ref, x_ref, mod_ref, fg_ref, t_ref, out_ref, loss_ref, dx_ref, dfg_ref):
        i = pl.program_id(0)
        acc = _dot(y_ref[...], w_ref[...], NN)
        out_ref[...] = acc.astype(BF16)
        xv = x_ref[...] + mod_ref[0][:, 2 * d:] * acc
        gain = fg_ref[...]
        r = lax.rsqrt(jnp.mean(xv * xv, axis=-1, keepdims=True) + EPS)
        xn = xv * r
        e = xn * gain - t_ref[...]
        part = 0.5 * jnp.sum(jnp.mean(e * e, axis=-1, keepdims=True), axis=0, keepdims=True)
        dyv = e / d
        p_fg = jnp.sum(dyv * xn, axis=0, keepdims=True)
        dxn = dyv * gain
        dx_ref[...] = r * (dxn - xn * jnp.mean(dxn * xn, axis=-1, keepdims=True))

        @pl.when(i == 0)
        def _():
            loss_ref[...] = part
            dfg_ref[...] = p_fg

        @pl.when(i != 0)
        def _():
            loss_ref[...] = loss_ref[...] + part
            dfg_ref[...] = dfg_ref[...] + p_fg

    row = pl.BlockSpec((tm, d), lambda i: (i, 0))
    return _call(
        body, name=name, grid=(m_rows // tm,),
        out_shape=[jax.ShapeDtypeStruct((m_rows, d), BF16), jax.ShapeDtypeStruct((1, 1), F32),
                   jax.ShapeDtypeStruct((m_rows, d), F32), jax.ShapeDtypeStruct((1, d), F32)],
        in_specs=[pl.BlockSpec((tm, di), lambda i: (i, 0)),
                  pl.BlockSpec((di, d), lambda i: (0, 0)),
                  row,
                  pl.BlockSpec((1, 1, 3 * d), lambda i: ((i * tm) // seq, 0, 0)),
                  pl.BlockSpec((1, d), lambda i: (0, 0)), row],
        out_specs=[row, pl.BlockSpec((1, 1), lambda i: (0, 0)), row, pl.BlockSpec((1, d), lambda i: (0, 0))],
        compiler_params=_params(),
    )(y, w, x, mod, fg, target)


def _pack(parts):
    flat = jnp.concatenate([p.reshape(-1) for p in parts])
    pad = (-flat.shape[0]) % (8 * LANES)
    return jnp.pad(flat, (0, pad)).reshape(-1, LANES)


def kernel(x, c, norm_gain, w_ada, b_ada, a_w_in, a_ln_gain, a_ln_bias, a_w_s, a_b_s, a_w_out, b_w_in, b_lower_bounds, b_gn_gain, b_w_out, final_gain, loss_target, m_norm_gain, m_w_ada, m_b_ada, m_a_w_in, m_a_ln_gain, m_a_ln_bias, m_a_w_s, m_a_b_s, m_a_w_out, m_b_w_in, m_b_lower_bounds, m_b_gn_gain, m_b_w_out, m_final_gain, v_norm_gain, v_w_ada, v_b_ada, v_a_w_in, v_a_ln_gain, v_a_ln_bias, v_a_w_s, v_a_b_s, v_a_w_out, v_b_w_in, v_b_lower_bounds, v_b_gn_gain, v_b_w_out, v_final_gain):
    nb, seq, d = x.shape
    m_rows = nb * seq
    n_l = w_ada.shape[0]
    ada_cols = w_ada.shape[2]
    px, py, pc = _place()
    chip = 2 * px + py
    dev = 2 * chip + pc

    c_all = allgather_small(c.reshape(-1, LANES), "gather_c").reshape(N_DEV * nb, d)
    b_cols = lax.dynamic_slice_in_dim(b_ada, chip * ada_cols, ada_cols, axis=1).reshape(n_l, 1, ada_cols)
    mod_cols = ada_fwd(c_all, w_ada, b_cols, "ada_fwd")
    mod_g = allgather_small(mod_cols.reshape(-1, LANES), "gather_mod")
    mod_g = mod_g.reshape(N_CHIPS, 2, n_l, N_DEV * nb, ada_cols)[:, 0]
    mod_all = jnp.transpose(mod_g, (1, 2, 0, 3)).reshape(n_l, N_DEV * nb, 3 * d)
    mod_mine = lax.dynamic_slice_in_dim(mod_all, dev * nb, nb, axis=1)
    mod0 = mod_mine[0].reshape(nb, 1, 3 * d)
    mod1 = mod_mine[1].reshape(nb, 1, 3 * d)

    (wa_in, wa_out), tok_a = gather_inplace(
        [cast_into_slot(a_w_in[0], chip, mod_mine, "cast_a_in"), cast_into_slot(a_w_out[0], chip, mod_mine, "cast_a_out")],
        "gather_a")
    s_bi = gather_start(cast_into_slot(b_w_in[0], chip, tok_a, "cast_b_in"), "gather_b_in_start")
    s_bo = gather_start(cast_into_slot(b_w_out[0], chip, s_bi[3], "cast_b_out"), "gather_b_out_start")
    di = a_w_out.shape[1] * N_CHIPS
    wa_out = wa_out.reshape(di, d)

    x0 = x.reshape(m_rows, d)
    tgt = loss_target.reshape(m_rows, d)
    ng0 = norm_gain[0:1] + (s_bi[3][0, 0] + s_bo[3][0, 0])
    ng1 = norm_gain[1:2]
    bs_col = a_b_s[0].reshape(SG_GROUPS, SG_BLOCK, 1)
    proj_a, h_a = inproj_fwd(x0, mod0, ng0, wa_in, seq, False, "a_inproj")
    y_a = sgu_fwd(proj_a, a_ln_gain, a_ln_bias, a_w_s[0], bs_col, "a_sgu")
    x1, out_a = outproj_fwd(y_a, wa_out, x0, mod0, seq, "a_outproj")
    wb_in = gather_wait(*s_bi[:3], out_a, "gather_b_in_wait")
    proj_b, h_b = inproj_fwd(x1, mod1, ng1, wb_in, seq, True, "b_inproj")
    y_b, sts_b = hgrn_fwd(proj_b, b_lower_bounds, b_gn_gain, seq, "b_hgrn")
    wb_out = gather_wait(*s_bo[:3], y_b, "gather_b_out_wait").reshape(di, d)
    out_b, loss_part, dx2, dfg = outproj_loss(
        y_b, wb_out, x1, mod1, final_gain.reshape(1, d), tgt, seq, "b_outproj_loss")

    shard_rows = di // N_CHIPS
    dy_b, dout_b, dgate1 = outproj_bwd(dx2, out_b, mod1, wb_out, seq, "b_outproj_bwd")
    gwb_out = grad_w_out(y_b, dout_b, "b_grad_w_out").reshape(N_CHIPS, shard_rows, d)
    e_bo = exchange_start(gwb_out, "exchange_b_out_start")
    dproj_b, dlb, dgn = hgrn_bwd(
        proj_b, dy_b, sts_b, b_lower_bounds, b_gn_gain + e_bo[4][0, 0], seq, "b_hgrn_bwd")
    e_bi = exchange_start(grad_w_in(h_b, dproj_b, N_CHIPS, True, "b_grad_w_in"), "exchange_b_in_start")
    dx1, dshift1, dscale1, dng1 = inproj_bwd(
        dproj_b, wb_in, x1, dx2, mod1, ng1 + e_bi[4][0, 0], seq, True, "b_inproj_bwd")

    dy_a, dout_a, dgate0 = outproj_bwd(dx1, out_a, mod0, wa_out, seq, "a_outproj_bwd")
    gwa_out = grad_w_out(y_a, dout_a, "a_grad_w_out").reshape(N_CHIPS, shard_rows, d)
    e_ao = exchange_start(gwa_out, "exchange_a_out_start")
    dproj_a, dws, dbs, dlg, dlbias = sgu_bwd(
        proj_a, dy_a, a_ln_gain + e_ao[4][0, 0], a_ln_bias, a_w_s[0], bs_col, "a_sgu_bwd")
    e_ai = exchange_start(grad_w_in(h_a, dproj_a, N_CHIPS, False, "a_grad_w_in"), "exchange_a_in_start")
    dx0, dshift0, dscale0, dng0 = inproj_bwd(
        dproj_a, wa_in, x0, dx1, mod0, norm_gain[0:1] + e_ai[4][0, 0], seq, False, "a_inproj_bwd")
    grad_x = dx0.reshape(nb, seq, d)

    dmod = jnp.concatenate([dshift0, dscale0, dgate0, dshift1, dscale1, dgate1], axis=2)
    n_dmod = dmod.size
    small_g = [jnp.concatenate([dng0, dng1], axis=0), dlg, dlbias, dws, dbs, dlb, dfg, dgn]
    packed_g = _pack([dmod] + small_g + [loss_part])
    rows = packed_g.shape[0]
    s_small = gather_all_start(
        lax.dynamic_update_slice(jnp.zeros((N_DEV, rows, LANES), F32), packed_g[None], (dev, 0, 0)),
        "gather_small_start")

    def finish(group, after):
        mine = []
        for ex, _, _, _, nm in group:
            parts_thru, land = exchange_wait(ex[0], ex[1], ex[2], ex[3], after, "exchange_" + nm + "_wait")
            mine.append(sum_parts(parts_thru, land, chip, "sum_" + nm))
            after = mine[-1]
        theirs = swap_sibling(mine, "swap_" + group[0][4])
        return [[r.reshape(w.shape) for r in adamw_pair(pa, pb, w[0], m[0], v[0], "adamw_" + nm)]
                for pa, pb, (_, w, m, v, nm) in zip(mine, theirs, group)]

    (gb_out, db_out, mb_out, vb_out), (gb_in, db_in, mb_in, vb_in), (ga_out, da_out, ma_out, va_out) = finish(
        [(e_bo, b_w_out, m_b_w_out, v_b_w_out, "b_out"), (e_bi, b_w_in, m_b_w_in, v_b_w_in, "b_in"),
         (e_ao, a_w_out, m_a_w_out, v_a_w_out, "a_out")], s_small[3])
    ((ga_in, da_in, ma_in, va_in),) = finish([(e_ai, a_w_in, m_a_w_in, v_a_w_in, "a_in")], ga_out)

    small_w = [norm_gain, a_ln_gain, a_ln_bias, a_w_s, a_b_s, b_lower_bounds, final_gain, b_gn_gain]
    small_m = [m_norm_gain, m_a_ln_gain, m_a_ln_bias, m_a_w_s, m_a_b_s, m_b_lower_bounds, m_final_gain, m_b_gn_gain]
    small_v = [v_norm_gain, v_a_ln_gain, v_a_ln_bias, v_a_w_s, v_a_b_s, v_b_lower_bounds, v_final_gain, v_b_gn_gain]
    rows_of = lambda a: a.reshape(-1, a.shape[-1])
    gathered = gather_all_wait(s_small[0], s_small[1], s_small[2], ga_in, "gather_small_wait")
    tail, small_res = small_update(
        gathered, n_dmod // LANES, [rows_of(a) for a in small_w], [rows_of(a) for a in small_m],
        [rows_of(a) for a in small_v], "small_update")
    loss = tail[0, 0]
    sg, sd, sm, sv = [[small_res[p][kind].reshape(w.shape) for p, w in enumerate(small_w)] for kind in range(4)]

    dmod_all = gathered[:, :n_dmod // LANES].reshape(N_DEV * nb, n_l, 3 * d)
    dmod_cols = lax.dynamic_slice_in_dim(dmod_all, chip * ada_cols, ada_cols, axis=2)
    dmod_cols = jnp.transpose(dmod_cols, (1, 0, 2))
    g_wada, d_wada, m_wada, v_wada = ada_bwd(c_all, dmod_cols, w_ada, m_w_ada, v_w_ada, "ada_bwd")
    flat = lambda a: a.reshape(1, -1)
    g_bada, d_bada, m_bada, v_bada = [
        r.reshape(b_ada.shape) for r in
        bias_update(dmod_all.reshape(N_DEV * nb, n_l * 3 * d), flat(b_ada), flat(m_b_ada), flat(v_b_ada), "bias_update")]

    def order(ng, wada, bada, ain, sm_rest, aout, bin_, bout):
        lg, lbi, ws_, bs_, lbd, fg_, gn_ = sm_rest
        return [ng, wada, bada, ain, lg, lbi, ws_, bs_, aout, bin_, lbd, gn_, bout, fg_]

    grads = order(sg[0], g_wada, g_bada, ga_in, sg[1:8], ga_out, gb_in, gb_out)
    deltas = order(sd[0], d_wada, d_bada, da_in, sd[1:8], da_out, db_in, db_out)
    new_m = order(sm[0], m_wada, m_bada, ma_in, sm[1:8], ma_out, mb_in, mb_out)
    new_v = order(sv[0], v_wada, v_bada, va_in, sv[1:8], va_out, vb_in, vb_out)
    return (loss, grad_x, *grads, *deltas, *new_m, *new_v)
```

```python
import functools

import jax
import jax.numpy as jnp
from jax import lax
from jax.experimental import pallas as pl
from jax.experimental.pallas import tpu as pltpu

F32 = jnp.float32
BF16 = jnp.bfloat16
EPS = 1e-6
CHUNK = 64
SG_BLOCK = 128
SG_GROUPS = 8
HEAD_DIM = 128
HG_WIDE = 8
HG_ROWS = 256
N_CHIPS = 4
N_DEV = 8
LANES = 128
ADAM_LR = 0.001
ADAM_B1 = 0.9
ADAM_B2 = 0.999
ADAM_EPS = 1e-08
ADAM_WD = 0.01
ADAM_STEP = 10
GELU_C0 = 0.7978845608028654
GELU_C1 = 0.044715
MESH = pl.DeviceIdType.MESH
VMEM_LIMIT = 56 * 1024 * 1024


ROW_TILE = 1024


def _col_tile(n):
    return next(t for t in (1024, 768, 512, 256) if n % t == 0)


def _call(body, **kw):
    return pl.pallas_call(body, **kw)


def _params(**kw):
    return pltpu.CompilerParams(vmem_limit_bytes=VMEM_LIMIT, **kw)


def _sigmoid(x):
    return 0.5 * jnp.tanh(0.5 * x) + 0.5


def _sigmoid_small(x):
    return 1.0 / (1.0 + jnp.exp(-x))


def _silu_and_grad(x):
    s = _sigmoid(x)
    return x * s, s * (1.0 + x * (1.0 - s))


def _gelu(x):
    return 0.5 * x * (1.0 + jnp.tanh(GELU_C0 * (x + GELU_C1 * x * x * x)))


def _gelu_and_grad(x):
    t = jnp.tanh(GELU_C0 * (x + GELU_C1 * x * x * x))
    g = 0.5 * x * (1.0 + t)
    dg = 0.5 * (1.0 + t) + 0.5 * x * (1.0 - t * t) * (GELU_C0 * (1.0 + 3.0 * GELU_C1 * x * x))
    return g, dg


def _dot(a, b, dims, precision=None):
    return lax.dot_general(a, b, (dims, ((), ())), precision=precision, preferred_element_type=F32)


NN = ((1,), (0,))
NT = ((1,), (1,))
TN = ((0,), (0,))


def _adamw(w, g, m, v):
    m = ADAM_B1 * m + (1.0 - ADAM_B1) * g
    v = ADAM_B2 * v + (1.0 - ADAM_B2) * (g * g)
    m_hat = m / (1.0 - ADAM_B1 ** ADAM_STEP)
    v_hat = v / (1.0 - ADAM_B2 ** ADAM_STEP)
    delta = -ADAM_LR * (m_hat / (jnp.sqrt(v_hat) + ADAM_EPS) + ADAM_WD * w)
    return delta, m, v


def _chunk_mask():
    r = lax.broadcasted_iota(jnp.int32, (SG_BLOCK, SG_BLOCK), 0)
    c = lax.broadcasted_iota(jnp.int32, (SG_BLOCK, SG_BLOCK), 1)
    return (c // CHUNK) <= (r // CHUNK)


def _place():
    return lax.axis_index("x"), lax.axis_index("y"), lax.axis_index("c")


def _other_chips(x, y):
    return [(1 - x, y), (x, 1 - y), (1 - x, 1 - y)]


def allgather_small(v, name):
    m_per, n = v.shape

    def body(x_ref, out_ref, send_sems, recv_sems, local_sem):
        x, y, c = _place()
        me, sibling = (x, y, c), (x, y, 1 - c)
        chips = _other_chips(x, y)

        def rows(px, py, pc):
            return out_ref.at[pl.ds((4 * px + 2 * py + pc) * m_per, m_per), :]

        def copy(k, block, to, src=None):
            return pltpu.make_async_remote_copy(
                src_ref=rows(*block) if src is None else src, dst_ref=rows(*block),
                send_sem=send_sems.at[k], recv_sem=recv_sems.at[k], device_id=to, device_id_type=MESH)

        mine = pltpu.make_async_copy(x_ref, rows(*me), local_sem)
        mine.start()
        first = [copy(0, me, sibling, src=x_ref)]
        first += [copy(1 + j, me, (*chip, c), src=x_ref) for j, chip in enumerate(chips)]
        for cp in first:
            cp.start()
        passed = [copy(4 + j, (*chip, c), sibling) for j, chip in enumerate(chips)]
        for j, chip in enumerate(chips):
            copy(1 + j, (*chip, c), me).wait_recv()
            passed[j].start()
        copy(0, sibling, me).wait_recv()
        for j, chip in enumerate(chips):
            copy(4 + j, (*chip, 1 - c), me).wait_recv()
        for cp in first + passed:
            cp.wait_send()
        mine.wait()

    return _call(
        body, name=name,
        out_shape=jax.ShapeDtypeStruct((N_DEV * m_per, n), v.dtype),
        in_specs=[pl.BlockSpec(memory_space=pltpu.VMEM)],
        out_specs=pl.BlockSpec(memory_space=pltpu.VMEM),
        scratch_shapes=[pltpu.SemaphoreType.DMA((7,)), pltpu.SemaphoreType.DMA((7,)), pltpu.SemaphoreType.DMA],
    )(v)


def _hbm_spec():
    return pl.BlockSpec(memory_space=pltpu.HBM)


def _sem_spec():
    return pl.BlockSpec(memory_space=pltpu.SEMAPHORE)


def _split_params():
    return pltpu.CompilerParams(has_side_effects=pltpu.SideEffectType.DATAFLOW_SIDE_EFFECTING)


def _hbm(a):
    return pltpu.with_memory_space_constraint(a, pltpu.HBM)


def gather_inplace(lands, name):
    n = len(lands)

    def body(*refs):
        land_refs, token = refs[n:2 * n], refs[2 * n]
        send_sems, recv_sems = refs[2 * n + 1:]
        x, y, c = _place()
        chips = _other_chips(x, y)

        def copy(w, k, chip_idx, core_half, to):
            half = lands[w].shape[1] // 2
            rows = land_refs[w].at[chip_idx, pl.ds(core_half * half, half), :]
            return pltpu.make_async_remote_copy(
                src_ref=rows, dst_ref=rows, send_sem=send_sems.at[6 * w + k], recv_sem=recv_sems.at[6 * w + k],
                device_id=to, device_id_type=MESH)

        first = [copy(w, j, 2 * x + y, c, (px, py, c)) for w in range(n) for j, (px, py) in enumerate(chips)]
        for cp in first:
            cp.start()
        passed = []
        for w in range(n):
            for j, (px, py) in enumerate(chips):
                copy(w, j, 2 * px + py, c, (px, py, c)).wait_recv()
                passed.append(copy(w, 3 + j, 2 * px + py, c, (x, y, 1 - c)))
                passed[-1].start()
        for w in range(n):
            for j, (px, py) in enumerate(chips):
                copy(w, 3 + j, 2 * px + py, 1 - c, (x, y, 1 - c)).wait_recv()
        for cp in first + passed:
            cp.wait_send()
        token[...] = jnp.zeros_like(token)

    res = _call(
        body, name=name,
        out_shape=[jax.ShapeDtypeStruct(a.shape, a.dtype) for a in lands] + [jax.ShapeDtypeStruct((8, LANES), F32)],
        in_specs=[_hbm_spec()] * n, out_specs=[_hbm_spec()] * n + [pl.BlockSpec(memory_space=pltpu.VMEM)],
        input_output_aliases={w: w for w in range(n)},
        scratch_shapes=[pltpu.SemaphoreType.DMA((6 * n,)), pltpu.SemaphoreType.DMA((6 * n,))],
    )(*lands)
    return res[:n], res[n]


def gather_start(land, name):
    def body(land_ref, send_sems, recv_sems, land_thru, token):
        del land_thru
        x, y, c = _place()
        for j, (px, py) in enumerate(_other_chips(x, y)):
            pltpu.make_async_remote_copy(
                src_ref=land_ref.at[2 * x + y], dst_ref=land_ref.at[2 * x + y],
                send_sem=send_sems.at[j], recv_sem=recv_sems.at[j], device_id=(px, py, c),
                device_id_type=MESH).start()
        token[...] = jnp.zeros_like(token)

    return _call(
        body, name=name,
        out_shape=(pltpu.SemaphoreType.DMA((3,)), pltpu.SemaphoreType.DMA((3,)),
                   pltpu.HBM(land.shape, land.dtype), jax.ShapeDtypeStruct((8, LANES), F32)),
        in_specs=(_hbm_spec(),),
        out_specs=(_sem_spec(), _sem_spec(), _hbm_spec(), pl.BlockSpec(memory_space=pltpu.VMEM)),
        input_output_aliases={0: 2}, compiler_params=_split_params(),
    )(_hbm(land))


def gather_wait(send_sems, recv_sems, land, after, name):
    def body(land_ref, send_sems, recv_sems, after_ref, land_out):
        del after_ref, land_out
        x, y, c = _place()
        for j, (px, py) in enumerate(_other_chips(x, y)):
            cp = pltpu.make_async_remote_copy(
                src_ref=land_ref.at[2 * x + y], dst_ref=land_ref.at[2 * px + py],
                send_sem=send_sems.at[j], recv_sem=recv_sems.at[j], device_id=(px, py, c), device_id_type=MESH)
            cp.wait_send()
            cp.wait_recv()

    return _call(
        body, name=name,
        out_shape=pltpu.HBM(land.shape, land.dtype),
        in_specs=(_hbm_spec(), _sem_spec(), _sem_spec(), pl.BlockSpec(memory_space=pl.ANY)),
        out_specs=_hbm_spec(), input_output_aliases={0: 0}, compiler_params=_split_params(),
    )(land, send_sems, recv_sems, after)


def _flips():
    return [(fx, fy, fc) for fx in (0, 1) for fy in (0, 1) for fc in (0, 1) if (fx, fy, fc) != (0, 0, 0)]


def _flipped(x, y, c, flip):
    fx, fy, fc = flip
    return (1 - x if fx else x, 1 - y if fy else y, 1 - c if fc else c)


def gather_all_start(land, name):
    def body(land_ref, send_sems, recv_sems, land_thru, token):
        del land_thru
        x, y, c = _place()
        for k, flip in enumerate(_flips()):
            pltpu.make_async_remote_copy(
                src_ref=land_ref.at[4 * x + 2 * y + c], dst_ref=land_ref.at[4 * x + 2 * y + c],
                send_sem=send_sems.at[k], recv_sem=recv_sems.at[k], device_id=_flipped(x, y, c, flip),
                device_id_type=MESH).start()
        token[...] = jnp.zeros_like(token)

    return _call(
        body, name=name,
        out_shape=(pltpu.SemaphoreType.DMA((7,)), pltpu.SemaphoreType.DMA((7,)),
                   pltpu.HBM(land.shape, land.dtype), jax.ShapeDtypeStruct((8, LANES), F32)),
        in_specs=(_hbm_spec(),),
        out_specs=(_sem_spec(), _sem_spec(), _hbm_spec(), pl.BlockSpec(memory_space=pltpu.VMEM)),
        input_output_aliases={0: 2}, compiler_params=_split_params(),
    )(_hbm(land))


def gather_all_wait(send_sems, recv_sems, land, after, name):
    def body(land_ref, send_sems, recv_sems, after_ref, land_out):
        del after_ref, land_out
        x, y, c = _place()
        for k, flip in enumerate(_flips()):
            px, py, pc = _flipped(x, y, c, flip)
            cp = pltpu.make_async_remote_copy(
                src_ref=land_ref.at[4 * x + 2 * y + c], dst_ref=land_ref.at[4 * px + 2 * py + pc],
                send_sem=send_sems.at[k], recv_sem=recv_sems.at[k], device_id=(px, py, pc), device_id_type=MESH)
            cp.wait_send()
            cp.wait_recv()

    return _call(
        body, name=name,
        out_shape=pltpu.HBM(land.shape, land.dtype),
        in_specs=(_hbm_spec(), _sem_spec(), _sem_spec(), pl.BlockSpec(memory_space=pl.ANY)),
        out_specs=_hbm_spec(), input_output_aliases={0: 0}, compiler_params=_split_params(),
    )(land, send_sems, recv_sems, after)


def exchange_start(parts, name):
    _, r, c_ = parts.shape

    def body(parts_ref, land_ref, send_sems, recv_sems, parts_thru, land_thru, token):
        del parts_thru, land_thru
        x, y, c = _place()
        for j, (px, py) in enumerate(_other_chips(x, y)):
            pltpu.make_async_remote_copy(
                src_ref=parts_ref.at[2 * px + py], dst_ref=land_ref.at[j],
                send_sem=send_sems.at[j], recv_sem=recv_sems.at[j], device_id=(px, py, c),
                device_id_type=MESH).start()
        token[...] = jnp.zeros_like(token)

    return _call(
        body, name=name,
        out_shape=(pltpu.SemaphoreType.DMA((3,)), pltpu.SemaphoreType.DMA((3,)),
                   pltpu.HBM(parts.shape, parts.dtype), pltpu.HBM((3, r, c_), parts.dtype),
                   jax.ShapeDtypeStruct((8, LANES), F32)),
        in_specs=(_hbm_spec(), _hbm_spec()),
        out_specs=(_sem_spec(), _sem_spec(), _hbm_spec(), _hbm_spec(), pl.BlockSpec(memory_space=pltpu.VMEM)),
        input_output_aliases={0: 2, 1: 3}, compiler_params=_split_params(),
    )(_hbm(parts), _hbm(lax.empty((3, r, c_), parts.dtype)))


def exchange_wait(send_sems, recv_sems, parts, land, after, name):
    def body(parts_ref, land_ref, send_sems, recv_sems, after_ref, parts_out, land_out):
        del after_ref, parts_out, land_out
        x, y, c = _place()
        for j, (px, py) in enumerate(_other_chips(x, y)):
            cp = pltpu.make_async_remote_copy(
                src_ref=parts_ref.at[2 * px + py], dst_ref=land_ref.at[j],
                send_sem=send_sems.at[j], recv_sem=recv_sems.at[j], device_id=(px, py, c), device_id_type=MESH)
            cp.wait_send()
            cp.wait_recv()

    return _call(
        body, name=name,
        out_shape=(pltpu.HBM(parts.shape, parts.dtype), pltpu.HBM(land.shape, land.dtype)),
        in_specs=(_hbm_spec(), _hbm_spec(), _sem_spec(), _sem_spec(), pl.BlockSpec(memory_space=pl.ANY)),
        out_specs=(_hbm_spec(), _hbm_spec()), input_output_aliases={0: 0, 1: 1},
        compiler_params=_split_params(),
    )(parts, land, send_sems, recv_sems, after)


def cast_into_slot(w, chip, after, name):
    r, c = w.shape
    tr = min(256, r)

    def body(s_ref, w_ref, after_ref, o_ref):
        del s_ref, after_ref
        o_ref[...] = w_ref[...].astype(BF16)

    return _call(
        body, name=name,
        grid_spec=pltpu.PrefetchScalarGridSpec(
            num_scalar_prefetch=1, grid=(r // tr,),
            in_specs=[pl.BlockSpec((tr, c), lambda i, s: (i, 0)), pl.BlockSpec(memory_space=pl.ANY)],
            out_specs=pl.BlockSpec((None, tr, c), lambda i, s: (s[0], i, 0))),
        out_shape=jax.ShapeDtypeStruct((N_CHIPS, r, c), BF16),
        compiler_params=_params(),
    )(chip.reshape(1).astype(jnp.int32), w, after)


def sum_parts(parts, land, chip, name):
    _, r, c = parts.shape
    tr = min(256, r)

    def body(s_ref, p_ref, l_ref, o_ref):
        del s_ref
        acc = p_ref[...].astype(F32) + l_ref[0].astype(F32)
        acc = acc + l_ref[1].astype(F32)
        o_ref[...] = (acc + l_ref[2].astype(F32)).astype(BF16)

    return _call(
        body, name=name,
        grid_spec=pltpu.PrefetchScalarGridSpec(
            num_scalar_prefetch=1, grid=(r // tr,),
            in_specs=[pl.BlockSpec((None, tr, c), lambda i, s: (s[0], i, 0)),
                      pl.BlockSpec((3, tr, c), lambda i, s: (0, i, 0))],
            out_specs=pl.BlockSpec((tr, c), lambda i, s: (i, 0))),
        out_shape=jax.ShapeDtypeStruct((r, c), BF16),
        compiler_params=_params(),
    )(chip.reshape(1).astype(jnp.int32), parts, land)


def swap_sibling(arrs, name):
    n = len(arrs)

    def body(*refs):
        ins, outs = refs[:n], refs[n:2 * n]
        send_sems, recv_sems = refs[2 * n:]
        x, y, c = _place()
        cps = []
        for w in range(n):
            cp = pltpu.make_async_remote_copy(
                src_ref=ins[w], dst_ref=outs[w], send_sem=send_sems.at[w], recv_sem=recv_sems.at[w],
                device_id=(x, y, 1 - c), device_id_type=MESH)
            cp.start()
            cps.append(cp)
        for cp in cps:
            cp.wait_recv()
        for cp in cps:
            cp.wait_send()

    return _call(
        body, name=name,
        out_shape=[jax.ShapeDtypeStruct(a.shape, a.dtype) for a in arrs],
        in_specs=[_hbm_spec()] * n, out_specs=[_hbm_spec()] * n,
        scratch_shapes=[pltpu.SemaphoreType.DMA((n,)), pltpu.SemaphoreType.DMA((n,))],
    )(*arrs)


def adamw_pair(pa, pb, w, m, v, name):
    r, c = w.shape
    tr = min(128, r)

    def body(pa_ref, pb_ref, w_ref, m_ref, v_ref, g_ref, d_ref, nm_ref, nv_ref):
        g = pa_ref[...].astype(F32) + pb_ref[...].astype(F32)
        d, nm, nv = _adamw(w_ref[...], g, m_ref[...], v_ref[...])
        g_ref[...] = g
        d_ref[...] = d
        nm_ref[...] = nm
        nv_ref[...] = nv

    spec = pl.BlockSpec((tr, c), lambda i: (i, 0))
    return _call(
        body, name=name, grid=(r // tr,),
        out_shape=[jax.ShapeDtypeStruct((r, c), F32)] * 4,
        in_specs=[spec] * 5, out_specs=[spec] * 4,
        compiler_params=_params(),
    )(pa, pb, w, m, v)


def small_update(gathered, first_row, ws, ms, vs, name):
    n_w = len(ws)
    total_rows = gathered.shape[1]

    def body(*refs):
        g_ref = refs[0]
        w_refs, m_refs, v_refs = refs[1:1 + n_w], refs[1 + n_w:1 + 2 * n_w], refs[1 + 2 * n_w:1 + 3 * n_w]
        tail_ref = refs[1 + 3 * n_w]
        outs = refs[2 + 3 * n_w:2 + 7 * n_w]
        sum_ref = refs[2 + 7 * n_w]
        acc = g_ref[0]
        for k in range(1, N_DEV):
            acc = acc + g_ref[k]
        sum_ref[...] = acc
        row = first_row
        for p in range(n_w):
            a, b = ws[p].shape
            per = b // LANES
            g_out, d_out, m_out, v_out = outs[4 * p:4 * p + 4]
            if per == 1:
                g_out[...] = sum_ref[row:row + a, :]
            else:
                for i in range(a):
                    for jc in range(per):
                        g_out[i:i + 1, jc * LANES:(jc + 1) * LANES] = sum_ref[row + i * per + jc:row + i * per + jc + 1, :]
            row += a * per
            dl, nm, nv = _adamw(w_refs[p][...], g_out[...], m_refs[p][...], v_refs[p][...])
            d_out[...] = dl
            m_out[...] = nm
            v_out[...] = nv
        tail_ref[...] = sum_ref[row:row + 1, :]

    out_shape = [jax.ShapeDtypeStruct((1, LANES), F32)]
    for w in ws:
        out_shape += [jax.ShapeDtypeStruct(w.shape, F32)] * 4
    res = _call(
        body, name=name, out_shape=out_shape,
        scratch_shapes=[pltpu.VMEM((total_rows, LANES), F32)],
        compiler_params=_params(),
    )(gathered, *ws, *ms, *vs)
    return res[0], [res[1 + 4 * p:5 + 4 * p] for p in range(n_w)]


def ada_fwd(c_all, w_ada, b_cols, name):
    n_l, d, cols = w_ada.shape
    nb = c_all.shape[0]
    tn = 256

    def body(c_ref, w_ref, b_ref, o_ref):
        cv = c_ref[...]
        ca = (cv * _sigmoid(cv)).astype(BF16)
        o_ref[...] = _dot(ca, w_ref[...].astype(BF16), NN) + b_ref[...]

    return _call(
        body, name=name, grid=(n_l, cols // tn),
        out_shape=jax.ShapeDtypeStruct((n_l, nb, cols), F32),
        in_specs=[pl.BlockSpec((nb, d), lambda l, j: (0, 0)),
                  pl.BlockSpec((None, d, tn), lambda l, j: (l, 0, j)),
                  pl.BlockSpec((None, 1, tn), lambda l, j: (l, 0, j))],
        out_specs=pl.BlockSpec((None, nb, tn), lambda l, j: (l, 0, j)),
        compiler_params=_params(),
    )(c_all, w_ada, b_cols)


def ada_bwd(c_all, dmod_cols, w, m, v, name):
    n_l, d, cols = w.shape
    nb = c_all.shape[0]
    tn = 256

    def body(c_ref, dm_ref, w_ref, m_ref, v_ref, g_ref, d_ref, nm_ref, nv_ref):
        cv = c_ref[...]
        ca = (cv * _sigmoid(cv)).astype(BF16)
        g = _dot(ca, dm_ref[...].astype(BF16), TN)
        dl, nm, nv = _adamw(w_ref[...], g, m_ref[...], v_ref[...])
        g_ref[...] = g
        d_ref[...] = dl
        nm_ref[...] = nm
        nv_ref[...] = nv

    wspec = pl.BlockSpec((None, d, tn), lambda l, j: (l, 0, j))
    return _call(
        body, name=name, grid=(n_l, cols // tn),
        out_shape=[jax.ShapeDtypeStruct((n_l, d, cols), F32)] * 4,
        in_specs=[pl.BlockSpec((nb, d), lambda l, j: (0, 0)),
                  pl.BlockSpec((None, nb, tn), lambda l, j: (l, 0, j)),
                  wspec, wspec, wspec],
        out_specs=[wspec] * 4,
        compiler_params=_params(),
    )(c_all, dmod_cols, w, m, v)


def bias_update(dmod_all, w, m, v, name):
    def body(dm_ref, w_ref, m_ref, v_ref, g_ref, d_ref, nm_ref, nv_ref):
        g = jnp.sum(dm_ref[...], axis=0, keepdims=True)
        dl, nm, nv = _adamw(w_ref[...], g, m_ref[...], v_ref[...])
        g_ref[...] = g
        d_ref[...] = dl
        nm_ref[...] = nm
        nv_ref[...] = nv

    return _call(
        body, name=name,
        out_shape=[jax.ShapeDtypeStruct(w.shape, F32)] * 4,
        compiler_params=_params(),
    )(dmod_all, w, m, v)


def inproj_fwd(x, mod, ng, wg, seq, sectioned, name):
    m_rows, d = x.shape
    nsh, _, ns = wg.shape
    n = nsh * ns
    tm, tn = min(2 * ROW_TILE, seq), _col_tile(ns)
    per = ns // tn

    def body(x_ref, mod_ref, ng_ref, w_ref, proj_ref, h_ref):
        @pl.when(pl.program_id(1) == 0)
        def _():
            xv = x_ref[...]
            r = lax.rsqrt(jnp.mean(xv * xv, axis=-1, keepdims=True) + EPS)
            md = mod_ref[0]
            h = (xv * r * ng_ref[...]) * (1.0 + md[:, d:2 * d]) + md[:, :d]
            h_ref[...] = h.astype(BF16)
        proj_ref[...] = _dot(h_ref[...], w_ref[...], NN)

    if sectioned:
        proj_shape = (nsh, m_rows, ns)
        proj_spec = pl.BlockSpec((None, tm, tn), lambda i, j: (j // per, i, j % per))
    else:
        proj_shape = (m_rows, n)
        proj_spec = pl.BlockSpec((tm, tn), lambda i, j: (i, j))
    return _call(
        body, name=name, grid=(m_rows // tm, n // tn),
        out_shape=[jax.ShapeDtypeStruct(proj_shape, F32), jax.ShapeDtypeStruct((m_rows, d), BF16)],
        in_specs=[pl.BlockSpec((tm, d), lambda i, j: (i, 0)),
                  pl.BlockSpec((1, 1, 3 * d), lambda i, j: ((i * tm) // seq, 0, 0)),
                  pl.BlockSpec((1, d), lambda i, j: (0, 0)),
                  pl.BlockSpec((None, d, tn), lambda i, j: (j // per, 0, j % per))],
        out_specs=[proj_spec, pl.BlockSpec((tm, d), lambda i, j: (i, 0))],
        compiler_params=_params(),
    )(x, mod, ng, wg)


def outproj_fwd(y, w, x, mod, seq, name):
    m_rows, di = y.shape
    d = w.shape[1]
    tm = min(ROW_TILE, seq)

    def body(y_ref, w_ref, x_ref, mod_ref, xn_ref, out_ref):
        acc = _dot(y_ref[...], w_ref[...], NN)
        out_ref[...] = acc.astype(BF16)
        xn_ref[...] = x_ref[...] + mod_ref[0][:, 2 * d:] * acc

    row = pl.BlockSpec((tm, d), lambda i: (i, 0))
    return _call(
        body, name=name, grid=(m_rows // tm,),
        out_shape=[jax.ShapeDtypeStruct((m_rows, d), F32), jax.ShapeDtypeStruct((m_rows, d), BF16)],
        in_specs=[pl.BlockSpec((tm, di), lambda i: (i, 0)),
                  pl.BlockSpec((di, d), lambda i: (0, 0)),
                  row,
                  pl.BlockSpec((1, 1, 3 * d), lambda i: ((i * tm) // seq, 0, 0))],
        out_specs=[row, row],
        compiler_params=_params(),
    )(y, w, x, mod)


def outproj_bwd(dxo, out, mod, w, seq, name):
    m_rows, d = dxo.shape
    di = w.shape[0]
    nb = m_rows // seq
    tm, tn = min(ROW_TILE, seq), _col_tile(di)

    def body(dxo_ref, out_ref, mod_ref, w_ref, dy_ref, dout_ref, dgate_ref):
        i = pl.program_id(0)

        @pl.when(pl.program_id(1) == 0)
        def _():
            dx = dxo_ref[...]
            dout_ref[...] = (mod_ref[0][:, 2 * d:] * dx).astype(BF16)
            part = jnp.sum(dx * out_ref[...].astype(F32), axis=0, keepdims=True)

            @pl.when((i * tm) % seq == 0)
            def _():
                dgate_ref[0] = part

            @pl.when((i * tm) % seq != 0)
            def _():
                dgate_ref[0] = dgate_ref[0] + part

        dy_ref[...] = _dot(dout_ref[...], w_ref[...], NT).astype(BF16)

    row = pl.BlockSpec((tm, d), lambda i, j: (i, 0))
    return _call(
        body, name=name, grid=(m_rows // tm, di // tn),
        out_shape=[jax.ShapeDtypeStruct((m_rows, di), BF16), jax.ShapeDtypeStruct((m_rows, d), BF16),
                   jax.ShapeDtypeStruct((nb, 1, d), F32)],
        in_specs=[row, row,
                  pl.BlockSpec((1, 1, 3 * d), lambda i, j: ((i * tm) // seq, 0, 0)),
                  pl.BlockSpec((tn, d), lambda i, j: (j, 0))],
        out_specs=[pl.BlockSpec((tm, tn), lambda i, j: (i, j)), row,
                   pl.BlockSpec((1, 1, d), lambda i, j: ((i * tm) // seq, 0, 0))],
        compiler_params=_params(),
    )(dxo, out, mod, w)


def grad_w_out(y, dout, name):
    m_rows, di = y.shape
    d = dout.shape[1]
    tm, tk = min(ROW_TILE, m_rows), _col_tile(di)
    n_m = m_rows // tm

    def body(y_ref, do_ref, o_ref, acc_ref):
        mi = pl.program_id(1)

        @pl.when(mi == 0)
        def _():
            acc_ref[...] = jnp.zeros_like(acc_ref)

        acc_ref[...] += _dot(y_ref[...], do_ref[...], TN)

        @pl.when(mi == n_m - 1)
        def _():
            o_ref[...] = acc_ref[...].astype(BF16)

    return _call(
        body, name=name, grid=(di // tk, n_m),
        out_shape=jax.ShapeDtypeStruct((di, d), BF16),
        in_specs=[pl.BlockSpec((tm, tk), lambda j, mi: (mi, j)),
                  pl.BlockSpec((tm, d), lambda j, mi: (mi, 0))],
        out_specs=pl.BlockSpec((tk, d), lambda j, mi: (j, 0)),
        scratch_shapes=[pltpu.VMEM((tk, d), F32)],
        compiler_params=_params(),
    )(y, dout)


def grad_w_in(h, dproj, nsh, sectioned, name):
    m_rows, d = h.shape
    n = dproj.shape[0] * dproj.shape[2] if sectioned else dproj.shape[1]
    ns = n // nsh
    tm, tn = min(ROW_TILE, m_rows), ns
    per = ns // tn
    n_m = m_rows // tm

    def body(h_ref, dp_ref, o_ref, acc_ref):
        mi = pl.program_id(1)
        @pl.when(mi == 0)
        def _():
            acc_ref[...] = jnp.zeros_like(acc_ref)

        acc_ref[...] += _dot(h_ref[...], dp_ref[...], TN)

        @pl.when(mi == n_m - 1)
        def _():
            o_ref[...] = acc_ref[...].astype(BF16)

    if sectioned:
        dp_spec = pl.BlockSpec((None, tm, tn), lambda j, mi: (j // per, mi, j % per))
    else:
        dp_spec = pl.BlockSpec((tm, tn), lambda j, mi: (mi, j))
    return _call(
        body, name=name, grid=(n // tn, n_m),
        out_shape=jax.ShapeDtypeStruct((nsh, d, ns), BF16),
        in_specs=[pl.BlockSpec((tm, d), lambda j, mi: (mi, 0)), dp_spec],
        out_specs=pl.BlockSpec((None, d, tn), lambda j, mi: (j // per, 0, j % per)),
        scratch_shapes=[pltpu.VMEM((d, tn), F32)],
        compiler_params=_params(),
    )(h, dproj)


def inproj_bwd(dproj, wg, x, dxo, mod, ng, seq, sectioned, name):
    m_rows, d = x.shape
    nsh, _, ns = wg.shape
    n = nsh * ns
    nb = m_rows // seq
    tm, tk = min(ROW_TILE, seq), ns
    per = ns // tk
    n_k = n // tk

    def body(dp_ref, w_ref, x_ref, dxo_ref, mod_ref, ng_ref, dxi_ref, dsh_ref, dsc_ref, dng_ref, acc_ref):
        i, k = pl.program_id(0), pl.program_id(1)
        @pl.when(k == 0)
        def _():
            acc_ref[...] = jnp.zeros_like(acc_ref)

        acc_ref[...] += _dot(dp_ref[...], w_ref[...], NT)

        @pl.when(k == n_k - 1)
        def _():
            dh = acc_ref[...]
            xv = x_ref[...]
            r = lax.rsqrt(jnp.mean(xv * xv, axis=-1, keepdims=True) + EPS)
            xn = xv * r
            md = mod_ref[0]
            gain = ng_ref[...]
            p_shift = jnp.sum(dh, axis=0, keepdims=True)
            p_scale = jnp.sum(dh * (xn * gain), axis=0, keepdims=True)
            drn = dh * (1.0 + md[:, d:2 * d])
            p_ng = jnp.sum(drn * xn, axis=0, keepdims=True)
            dxn = drn * gain
            dx = r * (dxn - xn * jnp.mean(dxn * xn, axis=-1, keepdims=True))
            dxi_ref[...] = dxo_ref[...] + dx

            @pl.when((i * tm) % seq == 0)
            def _():
                dsh_ref[0] = p_shift
                dsc_ref[0] = p_scale

            @pl.when((i * tm) % seq != 0)
            def _():
                dsh_ref[0] = dsh_ref[0] + p_shift
                dsc_ref[0] = dsc_ref[0] + p_scale

            @pl.when(i == 0)
            def _():
                dng_ref[...] = p_ng

            @pl.when(i != 0)
            def _():
                dng_ref[...] = dng_ref[...] + p_ng

    if sectioned:
        dp_spec = pl.BlockSpec((None, tm, tk), lambda i, k: (k // per, i, k % per))
    else:
        dp_spec = pl.BlockSpec((tm, tk), lambda i, k: (i, k))
    row = pl.BlockSpec((tm, d), lambda i, k: (i, 0))
    per_seq = pl.BlockSpec((1, 1, d), lambda i, k: ((i * tm) // seq, 0, 0))
    return _call(
        body, name=name, grid=(m_rows // tm, n_k),
        out_shape=[jax.ShapeDtypeStruct((m_rows, d), F32), jax.ShapeDtypeStruct((nb, 1, d), F32),
                   jax.ShapeDtypeStruct((nb, 1, d), F32), jax.ShapeDtypeStruct((1, d), F32)],
        in_specs=[dp_spec,
                  pl.BlockSpec((None, d, tk), lambda i, k: (k // per, 0, k % per)),
                  row, row,
                  pl.BlockSpec((1, 1, 3 * d), lambda i, k: ((i * tm) // seq, 0, 0)),
                  pl.BlockSpec((1, d), lambda i, k: (0, 0))],
        out_specs=[row, per_seq, per_seq, pl.BlockSpec((1, d), lambda i, k: (0, 0))],
        scratch_shapes=[pltpu.VMEM((tm, d), F32)],
        compiler_params=_params(),
    )(dproj, wg, x, dxo, mod, ng)


def _sgu_stats(proj_ref, vg_ref, di, gd, dgel_ref=None):
    s1 = jnp.zeros((SG_BLOCK, 1), F32)
    for g in range(SG_GROUPS):
        v_pre = proj_ref[:, di + g * gd:di + (g + 1) * gd]
        if dgel_ref is None:
            vg = _gelu(v_pre)
        else:
            vg, dgel_ref[:, g * gd:(g + 1) * gd] = _gelu_and_grad(v_pre)
        vg_ref[:, g * gd:(g + 1) * gd] = vg
        s1 = s1 + jnp.sum(vg, axis=1, keepdims=True)
    mu = s1 / di
    s2 = jnp.zeros((SG_BLOCK, 1), F32)
    for g in range(SG_GROUPS):
        dv = vg_ref[:, g * gd:(g + 1) * gd] - mu
        s2 = s2 + jnp.sum(dv * dv, axis=1, keepdims=True)
    return mu, lax.rsqrt(s2 / di + EPS)


def sgu_fwd(proj, ln_gain, ln_bias, ws, bs, name):
    m_rows, n3 = proj.shape
    di = n3 // 3
    gd = di // SG_GROUPS

    def body(proj_ref, lg_ref, lb_ref, ws_ref, bs_ref, y_ref, wsm_ref, vg_ref):
        @pl.when(pl.program_id(0) == 0)
        def _():
            mask = _chunk_mask()
            for g in range(SG_GROUPS):
                wsm_ref[g] = jnp.where(mask, ws_ref[g], 0.0).astype(BF16)

        mu, rstd = _sgu_stats(proj_ref, vg_ref, di, gd)
        for g in range(SG_GROUPS):
            cs = slice(g * gd, (g + 1) * gd)
            vln = (vg_ref[:, cs] - mu) * rstd * lg_ref[:, cs] + lb_ref[:, cs]
            s = _dot(wsm_ref[g], vln.astype(BF16), NN) + bs_ref[g]
            u = _gelu(proj_ref[:, cs])
            gp = proj_ref[:, 2 * di + g * gd:2 * di + (g + 1) * gd]
            y_ref[:, cs] = (u * s * (gp * _sigmoid(gp))).astype(BF16)

    full = lambda shape: pl.BlockSpec(shape, lambda i: (0,) * len(shape))
    return _call(
        body, name=name, grid=(m_rows // SG_BLOCK,),
        out_shape=jax.ShapeDtypeStruct((m_rows, di), BF16),
        in_specs=[pl.BlockSpec((SG_BLOCK, n3), lambda i: (i, 0)),
                  full((1, di)), full((1, di)),
                  full((SG_GROUPS, SG_BLOCK, SG_BLOCK)), full((SG_GROUPS, SG_BLOCK, 1))],
        out_specs=pl.BlockSpec((SG_BLOCK, di), lambda i: (i, 0)),
        scratch_shapes=[pltpu.VMEM((SG_GROUPS, SG_BLOCK, SG_BLOCK), BF16), pltpu.VMEM((SG_BLOCK, di), F32)],
        compiler_params=_params(),
    )(proj, ln_gain, ln_bias, ws, bs)


def sgu_bwd(proj, dy, ln_gain, ln_bias, ws, bs, name):
    m_rows, n3 = proj.shape
    di = n3 // 3
    gd = di // SG_GROUPS
    n_i = m_rows // SG_BLOCK

    def body(proj_ref, dy_ref, lg_ref, lb_ref, ws_ref, bs_ref,
             dp_ref, dws_ref, dbs_ref, dlg_ref, dlb_ref, wsm_ref, vg_ref, dvh_ref, dgel_ref):
        i = pl.program_id(0)

        def before():
            @pl.when(i == 0)
            def _():
                mask = _chunk_mask()
                for g in range(SG_GROUPS):
                    wsm_ref[g] = jnp.where(mask, ws_ref[g], 0.0).astype(BF16)
                dws_ref[...] = jnp.zeros_like(dws_ref)
                dbs_ref[...] = jnp.zeros_like(dbs_ref)
                dlg_ref[...] = jnp.zeros_like(dlg_ref)
                dlb_ref[...] = jnp.zeros_like(dlb_ref)

        def after():
            @pl.when(i == n_i - 1)
            def _():
                mask = _chunk_mask()
                for g in range(SG_GROUPS):
                    dws_ref[g] = jnp.where(mask, dws_ref[g], 0.0)

        before()
        mu, rstd = _sgu_stats(proj_ref, vg_ref, di, gd, dgel_ref)
        m1 = jnp.zeros((SG_BLOCK, 1), F32)
        m2 = jnp.zeros((SG_BLOCK, 1), F32)
        for g in range(SG_GROUPS):
            cs = slice(g * gd, (g + 1) * gd)
            gs = slice(2 * di + g * gd, 2 * di + (g + 1) * gd)
            gain = lg_ref[:, cs]
            vhat = (vg_ref[:, cs] - mu) * rstd
            vln_b = (vhat * gain + lb_ref[:, cs]).astype(BF16)
            s = _dot(wsm_ref[g], vln_b, NN) + bs_ref[g]
            u, du = _gelu_and_grad(proj_ref[:, cs])
            sg, dsg = _silu_and_grad(proj_ref[:, gs])
            dyv = dy_ref[:, cs].astype(F32)
            dp_ref[:, cs] = (dyv * s * sg * du).astype(BF16)
            dp_ref[:, gs] = (dyv * u * s * dsg).astype(BF16)
            ds = dyv * u * sg
            ds_b = ds.astype(BF16)
            dws_ref[g] = dws_ref[g] + _dot(ds_b, vln_b, NT)
            dbs_ref[g] = dbs_ref[g] + jnp.sum(ds, axis=1, keepdims=True)
            dvln = _dot(wsm_ref[g], ds_b, TN)
            dlg_ref[:, cs] = dlg_ref[:, cs] + jnp.sum(dvln * vhat, axis=0, keepdims=True)
            dlb_ref[:, cs] = dlb_ref[:, cs] + jnp.sum(dvln, axis=0, keepdims=True)
            dvh = dvln * gain
            dvh_ref[:, cs] = dvh
            m1 = m1 + jnp.sum(dvh, axis=1, keepdims=True)
            m2 = m2 + jnp.sum(dvh * vhat, axis=1, keepdims=True)
        m1 = m1 / di
        m2 = m2 / di
        for g in range(SG_GROUPS):
            cs = slice(g * gd, (g + 1) * gd)
            vs = slice(di + g * gd, di + (g + 1) * gd)
            vhat = (vg_ref[:, cs] - mu) * rstd
            dvg = rstd * (dvh_ref[:, cs] - m1 - vhat * m2)
            dp_ref[:, vs] = (dvg * dgel_ref[:, cs]).astype(BF16)

        after()

    full = lambda shape: pl.BlockSpec(shape, lambda i: (0,) * len(shape))
    return _call(
        body, name=name, grid=(n_i,),
        out_shape=[jax.ShapeDtypeStruct((m_rows, n3), BF16),
                   jax.ShapeDtypeStruct((SG_GROUPS, SG_BLOCK, SG_BLOCK), F32),
                   jax.ShapeDtypeStruct((SG_GROUPS, SG_BLOCK, 1), F32),
                   jax.ShapeDtypeStruct((1, di), F32), jax.ShapeDtypeStruct((1, di), F32)],
        in_specs=[pl.BlockSpec((SG_BLOCK, n3), lambda i: (i, 0)),
                  pl.BlockSpec((SG_BLOCK, di), lambda i: (i, 0)),
                  full((1, di)), full((1, di)),
                  full((SG_GROUPS, SG_BLOCK, SG_BLOCK)), full((SG_GROUPS, SG_BLOCK, 1))],
        out_specs=[pl.BlockSpec((SG_BLOCK, n3), lambda i: (i, 0)),
                   full((SG_GROUPS, SG_BLOCK, SG_BLOCK)), full((SG_GROUPS, SG_BLOCK, 1)),
                   full((1, di)), full((1, di))],
        scratch_shapes=[pltpu.VMEM((SG_GROUPS, SG_BLOCK, SG_BLOCK), BF16),
                        pltpu.VMEM((SG_BLOCK, di), F32), pltpu.VMEM((SG_BLOCK, di), F32),
                        pltpu.VMEM((SG_BLOCK, di), F32)],
        compiler_params=_params(),
    )(proj, dy, ln_gain, ln_bias, ws, bs)


def _lower_bound(lbraw):
    mx = jnp.maximum(lbraw[0:1, :], lbraw[1:2, :])
    e0 = jnp.exp(lbraw[0:1, :] - mx)
    e1 = jnp.exp(lbraw[1:2, :] - mx)
    p0 = e0 / (e0 + e1)
    p1 = e1 / (e0 + e1)
    return (p0 + p1) - p0, p0, p1


def _tri(lower):
    r = lax.broadcasted_iota(jnp.int32, (CHUNK, CHUNK), 0)
    c = lax.broadcasted_iota(jnp.int32, (CHUNK, CHUNK), 1)
    return ((r >= c) if lower else (c >= r)).astype(F32)


def _row(a, idx):
    r = lax.broadcasted_iota(jnp.int32, a.shape, 0)
    return jnp.sum(jnp.where(r == idx, a, 0.0), axis=0, keepdims=True)


def _hgrn_gates(qp, fp, lb, tri):
    sgm = _sigmoid_small(fp)
    f = lb + (1.0 - lb) * sgm
    k = 1.0 - f
    a = _dot(tri, jnp.log(f), NN, precision=lax.Precision.HIGHEST)
    a_mid = _row(a, CHUNK // 2 - 1)
    a_last = _row(a, CHUNK - 1)
    q, dq = _silu_and_grad(qp)
    e1, e2, e3, e4 = jnp.exp(a - a_mid), jnp.exp(a_mid - a), jnp.exp(a), jnp.exp(a_last - a)
    return dict(sgm=sgm, f=f, k=k, q=q, dq=dq, e1=e1, e2=e2, e3=e3, e4=e4, dec=jnp.exp(a_last),
                q_in=q * e1, k_in=k * e2, q_out=q * e3, k_out=k * e4)


def _causal():
    r = lax.broadcasted_iota(jnp.int32, (CHUNK, CHUNK), 0)
    c = lax.broadcasted_iota(jnp.int32, (CHUNK, CHUNK), 1)
    return r >= c


def hgrn_fwd(proj4, lbraw, gn, seq, name):
    _, m_rows, di = proj4.shape
    nb, nh, nc = m_rows // seq, di // HEAD_DIM, seq // CHUNK
    rows = min(HG_ROWS, seq)
    wide = HG_WIDE * HEAD_DIM
    ns, cpb = seq // rows, rows // CHUNK

    def body(p_ref, lb_ref, gn_ref, y_ref, sts_ref, st_ref):
        @pl.when(pl.program_id(2) == 0)
        def _():
            st_ref[...] = jnp.zeros_like(st_ref)

        tri = _tri(True)
        causal = _causal()
        gain = gn_ref[...]
        lbs = [_lower_bound(lb_ref[:, j * HEAD_DIM:(j + 1) * HEAD_DIM])[0] for j in range(HG_WIDE)]

        units = [(n, j) for n in range(cpb) for j in range(HG_WIDE)]
        rs = lambda n: slice(n * CHUNK, (n + 1) * CHUNK)
        cs = lambda j: slice(j * HEAD_DIM, (j + 1) * HEAD_DIM)
        gates, v_b, sc_b, kv, o_in, o_x = {}, {}, {}, {}, {}, {}
        for n, j in units:
            gates[n, j] = _hgrn_gates(p_ref[0, rs(n), cs(j)], p_ref[1, rs(n), cs(j)], lbs[j], tri)
            v_b[n, j] = p_ref[2, rs(n), cs(j)].astype(BF16)
        for u in units:
            t = gates[u]
            sc_b[u] = jnp.where(causal, _dot(t["q_in"].astype(BF16), t["k_in"].astype(BF16), NT), 0.0).astype(BF16)
            kv[u] = _dot(v_b[u], t["k_out"].astype(BF16), TN)
        for u in units:
            o_in[u] = _dot(sc_b[u], v_b[u], NN)
        for j in range(HG_WIDE):
            st = st_ref[j]
            for n in range(cpb):
                sts_ref[n, :, cs(j)] = st
                o_x[n, j] = _dot(gates[n, j]["q_out"].astype(BF16), st.astype(BF16), NT)
                st = st * gates[n, j]["dec"] + kv[n, j]
            st_ref[j] = st
        for n, j in units:
            o = o_in[n, j] + o_x[n, j]
            r = lax.rsqrt(jnp.mean(o * o, axis=-1, keepdims=True) + EPS)
            gp = p_ref[3, rs(n), cs(j)]
            y_ref[rs(n), cs(j)] = ((o * r * gain) * (gp * _sigmoid(gp))).astype(BF16)

    return _call(
        body, name=name, grid=(nh // HG_WIDE, nb, ns),
        out_shape=[jax.ShapeDtypeStruct((m_rows, di), BF16),
                   jax.ShapeDtypeStruct((nb * nc, HEAD_DIM, di), F32)],
        in_specs=[pl.BlockSpec((4, rows, wide), lambda hg, b, s: (0, b * ns + s, hg)),
                  pl.BlockSpec((2, wide), lambda hg, b, s: (0, hg)),
                  pl.BlockSpec((1, HEAD_DIM), lambda hg, b, s: (0, 0))],
        out_specs=[pl.BlockSpec((rows, wide), lambda hg, b, s: (b * ns + s, hg)),
                   pl.BlockSpec((cpb, HEAD_DIM, wide), lambda hg, b, s: (b * ns + s, 0, hg))],
        scratch_shapes=[pltpu.VMEM((HG_WIDE, HEAD_DIM, HEAD_DIM), F32)],
        compiler_params=_params(),
    )(proj4, lbraw, gn)


def hgrn_bwd(proj4, dy, sts, lbraw, gn, seq, name):
    _, m_rows, di = proj4.shape
    nb, nh, nc = m_rows // seq, di // HEAD_DIM, seq // CHUNK
    rows = min(HG_ROWS, seq)
    wide = HG_WIDE * HEAD_DIM
    ns, cpb = seq // rows, rows // CHUNK
    n_hg = nh // HG_WIDE

    def body(p_ref, dy_ref, sts_ref, lb_ref, gn_ref, dp_ref, dlb_ref, dgn_ref, dst_ref, lbacc_ref, gnacc_ref):
        hg, b, s = pl.program_id(0), pl.program_id(1), pl.program_id(2)
        tri, triu = _tri(True), _tri(False)
        causal = _causal()
        gain = gn_ref[...]
        first = (b == 0) & (s == 0)
        cs = lambda j: slice(j * HEAD_DIM, (j + 1) * HEAD_DIM)

        def before():
            @pl.when((hg == 0) & first)
            def _():
                gnacc_ref[...] = jnp.zeros_like(gnacc_ref)

            @pl.when(first)
            def _():
                lbacc_ref[...] = jnp.zeros_like(lbacc_ref)

            @pl.when(s == 0)
            def _():
                dst_ref[...] = jnp.zeros_like(dst_ref)

        def after():
            @pl.when((b == nb - 1) & (s == ns - 1))
            def _():
                for j in range(HG_WIDE):
                    _, p0, p1 = _lower_bound(lb_ref[:, cs(j)])
                    acc = lbacc_ref[:, cs(j)]
                    dlb_ref[0:1, cs(j)] = -acc * p0 * p1
                    dlb_ref[1:2, cs(j)] = acc * p1 * (1.0 - p1)

            @pl.when((hg == n_hg - 1) & (b == nb - 1) & (s == ns - 1))
            def _():
                tot = gnacc_ref[:, 0:HEAD_DIM]
                for j in range(1, HG_WIDE):
                    tot = tot + gnacc_ref[:, cs(j)]
                dgn_ref[...] = tot

        before()

        units = [(n, j) for n in range(cpb) for j in range(HG_WIDE)]
        rs = lambda n: slice(n * CHUNK, (n + 1) * CHUNK)
        lbs = [_lower_bound(lb_ref[:, cs(j)])[0] for j in range(HG_WIDE)]
        gates, v_b, st_b, sc_b, o, do_b = {}, {}, {}, {}, {}, {}
        dq_out, dsc_b, dv, g_st, dq_in, dk_in, dst_at, dk_out, ddec = {}, {}, {}, {}, {}, {}, {}, {}, {}
        for n, j in units:
            gates[n, j] = _hgrn_gates(p_ref[0, rs(n), cs(j)], p_ref[1, rs(n), cs(j)], lbs[j], tri)
            v_b[n, j] = p_ref[2, rs(n), cs(j)].astype(BF16)
            st_b[n, j] = sts_ref[n, :, cs(j)].astype(BF16)
        for u in units:
            t = gates[u]
            sc_b[u] = jnp.where(causal, _dot(t["q_in"].astype(BF16), t["k_in"].astype(BF16), NT), 0.0).astype(BF16)
        for u in units:
            o[u] = _dot(sc_b[u], v_b[u], NN) + _dot(gates[u]["q_out"].astype(BF16), st_b[u], NT)
        for n, j in units:
            ov = o[n, j]
            r = lax.rsqrt(jnp.mean(ov * ov, axis=-1, keepdims=True) + EPS)
            ohat = ov * r
            sg, dsg = _silu_and_grad(p_ref[3, rs(n), cs(j)])
            dyv = dy_ref[rs(n), cs(j)].astype(F32)
            dp_ref[3, rs(n), cs(j)] = (dyv * (ohat * gain) * dsg).astype(BF16)
            d_on = dyv * sg
            gnacc_ref[:, cs(j)] = gnacc_ref[:, cs(j)] + jnp.sum(d_on * ohat, axis=0, keepdims=True)
            dohat = d_on * gain
            do_b[n, j] = (r * (dohat - ohat * jnp.mean(dohat * ohat, axis=-1, keepdims=True))).astype(BF16)
        for u in units:
            dq_out[u] = _dot(do_b[u], st_b[u], NN)
            dsc_b[u] = jnp.where(causal, _dot(do_b[u], v_b[u], NT), 0.0).astype(BF16)
            dv[u] = _dot(sc_b[u], do_b[u], TN)
            g_st[u] = _dot(do_b[u], gates[u]["q_out"].astype(BF16), TN)
        for u in units:
            dq_in[u] = _dot(dsc_b[u], gates[u]["k_in"].astype(BF16), NN)
            dk_in[u] = _dot(dsc_b[u], gates[u]["q_in"].astype(BF16), TN)
        for j in range(HG_WIDE):
            dst = dst_ref[j]
            for n in reversed(range(cpb)):
                dst_at[n, j] = dst
                dst = dst * gates[n, j]["dec"] + g_st[n, j]
            dst_ref[j] = dst
        for n, j in units:
            dst = dst_at[n, j]
            dst_b = dst.astype(BF16)
            dk_out[n, j] = _dot(v_b[n, j], dst_b, NN)
            dv[n, j] = dv[n, j] + _dot(gates[n, j]["k_out"].astype(BF16), dst_b, NT)
            ddec[n, j] = jnp.sum(dst * sts_ref[n, :, cs(j)], axis=0, keepdims=True)
        for n, j in units:
            t = gates[n, j]
            dp_ref[2, rs(n), cs(j)] = dv[n, j].astype(BF16)
            dq = dq_in[n, j] * t["e1"] + dq_out[n, j] * t["e3"]
            dk = dk_in[n, j] * t["e2"] + dk_out[n, j] * t["e4"]
            w_in = dq_in[n, j] * t["q_in"] - dk_in[n, j] * t["k_in"]
            w_out = dk_out[n, j] * t["k_out"]
            da = w_in + dq_out[n, j] * t["q_out"] - w_out
            da_mid = -jnp.sum(w_in, axis=0, keepdims=True)
            da_last = jnp.sum(w_out, axis=0, keepdims=True) + ddec[n, j] * t["dec"]
            rid = lax.broadcasted_iota(jnp.int32, da.shape, 0)
            da = da + jnp.where(rid == CHUNK // 2 - 1, da_mid, 0.0) + jnp.where(rid == CHUNK - 1, da_last, 0.0)
            dlf = _dot(triu, da, NN, precision=lax.Precision.HIGHEST)
            df = dlf / t["f"] - dk
            sgm = t["sgm"]
            dp_ref[1, rs(n), cs(j)] = (df * (1.0 - lbs[j]) * sgm * (1.0 - sgm)).astype(BF16)
            lbacc_ref[:, cs(j)] = lbacc_ref[:, cs(j)] + jnp.sum(df * (1.0 - sgm), axis=0, keepdims=True)
            dp_ref[0, rs(n), cs(j)] = (dq * t["dq"]).astype(BF16)

        after()

    blk = lambda hg, b, s: b * ns + (ns - 1 - s)
    return _call(
        body, name=name, grid=(n_hg, nb, ns),
        out_shape=[jax.ShapeDtypeStruct((4, m_rows, di), BF16), jax.ShapeDtypeStruct((2, di), F32),
                   jax.ShapeDtypeStruct((1, HEAD_DIM), F32)],
        in_specs=[pl.BlockSpec((4, rows, wide), lambda hg, b, s: (0, blk(hg, b, s), hg)),
                  pl.BlockSpec((rows, wide), lambda hg, b, s: (blk(hg, b, s), hg)),
                  pl.BlockSpec((cpb, HEAD_DIM, wide), lambda hg, b, s: (blk(hg, b, s), 0, hg)),
                  pl.BlockSpec((2, wide), lambda hg, b, s: (0, hg)),
                  pl.BlockSpec((1, HEAD_DIM), lambda hg, b, s: (0, 0))],
        out_specs=[pl.BlockSpec((4, rows, wide), lambda hg, b, s: (0, blk(hg, b, s), hg)),
                   pl.BlockSpec((2, wide), lambda hg, b, s: (0, hg)),
                   pl.BlockSpec((1, HEAD_DIM), lambda hg, b, s: (0, 0))],
        scratch_shapes=[pltpu.VMEM((HG_WIDE, HEAD_DIM, HEAD_DIM), F32), pltpu.VMEM((1, wide), F32),
                        pltpu.VMEM((1, wide), F32)],
        compiler_params=_params(),
    )(proj4, dy, sts, lbraw, gn)


def outproj_loss(y, w, x, mod, fg, target, seq, name):
    m_rows, di = y.shape
    d = w.shape[1]
    tm = min(512, seq)

    def body(y_ref, w_ref, x_ref, mod_ref, fg_ref, t_ref, out_ref, loss_ref, dx_ref, dfg_ref):
        i = pl.program_id(0)
        acc = _dot(y_ref[...], w_ref[...], NN)
        out_ref[...] = acc.astype(BF16)
        xv = x_ref[...] + mod_ref[0][:, 2 * d:] * acc
        gain = fg_ref[...]
        r = lax.rsqrt(jnp.mean(xv * xv, axis=-1, keepdims=True) + EPS)
        xn = xv * r
        e = xn * gain - t_ref[...]
        part = 0.5 * jnp.sum(jnp.mean(e * e, axis=-1, keepdims=True), axis=0, keepdims=True)
        dyv = e / d
        p_fg = jnp.sum(dyv * xn, axis=0, keepdims=True)
        dxn = dyv * gain
        dx_ref[...] = r * (dxn - xn * jnp.mean(dxn * xn, axis=-1, keepdims=True))

        @pl.when(i == 0)
        def _():
            loss_ref[...] = part
            dfg_ref[...] = p_fg

        @pl.when(i != 0)
        def _():
            loss_ref[...] = loss_ref[...] + part
            dfg_ref[...] = dfg_ref[...] + p_fg

    row = pl.BlockSpec((tm, d), lambda i: (i, 0))
    return _call(
        body, name=name, grid=(m_rows // tm,),
        out_shape=[jax.ShapeDtypeStruct((m_rows, d), BF16), jax.ShapeDtypeStruct((1, 1), F32),
                   jax.ShapeDtypeStruct((m_rows, d), F32), jax.ShapeDtypeStruct((1, d), F32)],
        in_specs=[pl.BlockSpec((tm, di), lambda i: (i, 0)),
                  pl.BlockSpec((di, d), lambda i: (0, 0)),
                  row,
                  pl.BlockSpec((1, 1, 3 * d), lambda i: ((i * tm) // seq, 0, 0)),
                  pl.BlockSpec((1, d), lambda i: (0, 0)), row],
        out_specs=[row, pl.BlockSpec((1, 1), lambda i: (0, 0)), row, pl.BlockSpec((1, d), lambda i: (0, 0))],
        compiler_params=_params(),
    )(y, w, x, mod, fg, target)


def _pack(parts):
    flat = jnp.concatenate([p.reshape(-1) for p in parts])
    pad = (-flat.shape[0]) % (8 * LANES)
    return jnp.pad(flat, (0, pad)).reshape(-1, LANES)


def kernel(x, c, norm_gain, w_ada, b_ada, a_w_in, a_ln_gain, a_ln_bias, a_w_s, a_b_s, a_w_out, b_w_in, b_lower_bounds, b_gn_gain, b_w_out, final_gain, loss_target, m_norm_gain, m_w_ada, m_b_ada, m_a_w_in, m_a_ln_gain, m_a_ln_bias, m_a_w_s, m_a_b_s, m_a_w_out, m_b_w_in, m_b_lower_bounds, m_b_gn_gain, m_b_w_out, m_final_gain, v_norm_gain, v_w_ada, v_b_ada, v_a_w_in, v_a_ln_gain, v_a_ln_bias, v_a_w_s, v_a_b_s, v_a_w_out, v_b_w_in, v_b_lower_bounds, v_b_gn_gain, v_b_w_out, v_final_gain):
    nb, seq, d = x.shape
    m_rows = nb * seq
    n_l = w_ada.shape[0]
    ada_cols = w_ada.shape[2]
    px, py, pc = _place()
    chip = 2 * px + py
    dev = 2 * chip + pc

    c_all = allgather_small(c.reshape(-1, LANES), "gather_c").reshape(N_DEV * nb, d)
    b_cols = lax.dynamic_slice_in_dim(b_ada, chip * ada_cols, ada_cols, axis=1).reshape(n_l, 1, ada_cols)
    mod_cols = ada_fwd(c_all, w_ada, b_cols, "ada_fwd")
    mod_g = allgather_small(mod_cols.reshape(-1, LANES), "gather_mod")
    mod_g = mod_g.reshape(N_CHIPS, 2, n_l, N_DEV * nb, ada_cols)[:, 0]
    mod_all = jnp.transpose(mod_g, (1, 2, 0, 3)).reshape(n_l, N_DEV * nb, 3 * d)
    mod_mine = lax.dynamic_slice_in_dim(mod_all, dev * nb, nb, axis=1)
    mod0 = mod_mine[0].reshape(nb, 1, 3 * d)
    mod1 = mod_mine[1].reshape(nb, 1, 3 * d)

    (wa_in, wa_out), tok_a = gather_inplace(
        [cast_into_slot(a_w_in[0], chip, mod_mine, "cast_a_in"), cast_into_slot(a_w_out[0], chip, mod_mine, "cast_a_out")],
        "gather_a")
    s_bi = gather_start(cast_into_slot(b_w_in[0], chip, tok_a, "cast_b_in"), "gather_b_in_start")
    s_bo = gather_start(cast_into_slot(b_w_out[0], chip, s_bi[3], "cast_b_out"), "gather_b_out_start")
    di = a_w_out.shape[1] * N_CHIPS
    wa_out = wa_out.reshape(di, d)

    x0 = x.reshape(m_rows, d)
    tgt = loss_target.reshape(m_rows, d)
    ng0 = norm_gain[0:1] + (s_bi[3][0, 0] + s_bo[3][0, 0])
    ng1 = norm_gain[1:2]
    bs_col = a_b_s[0].reshape(SG_GROUPS, SG_BLOCK, 1)
    proj_a, h_a = inproj_fwd(x0, mod0, ng0, wa_in, seq, False, "a_inproj")
    y_a = sgu_fwd(proj_a, a_ln_gain, a_ln_bias, a_w_s[0], bs_col, "a_sgu")
    x1, out_a = outproj_fwd(y_a, wa_out, x0, mod0, seq, "a_outproj")
    wb_in = gather_wait(*s_bi[:3], out_a, "gather_b_in_wait")
    proj_b, h_b = inproj_fwd(x1, mod1, ng1, wb_in, seq, True, "b_inproj")
    y_b, sts_b = hgrn_fwd(proj_b, b_lower_bounds, b_gn_gain, seq, "b_hgrn")
    wb_out = gather_wait(*s_bo[:3], y_b, "gather_b_out_wait").reshape(di, d)
    out_b, loss_part, dx2, dfg = outproj_loss(
        y_b, wb_out, x1, mod1, final_gain.reshape(1, d), tgt, seq, "b_outproj_loss")

    shard_rows = di // N_CHIPS
    dy_b, dout_b, dgate1 = outproj_bwd(dx2, out_b, mod1, wb_out, seq, "b_outproj_bwd")
    gwb_out = grad_w_out(y_b, dout_b, "b_grad_w_out").reshape(N_CHIPS, shard_rows, d)
    e_bo = exchange_start(gwb_out, "exchange_b_out_start")
    dproj_b, dlb, dgn = hgrn_bwd(
        proj_b, dy_b, sts_b, b_lower_bounds, b_gn_gain + e_bo[4][0, 0], seq, "b_hgrn_bwd")
    e_bi = exchange_start(grad_w_in(h_b, dproj_b, N_CHIPS, True, "b_grad_w_in"), "exchange_b_in_start")
    dx1, dshift1, dscale1, dng1 = inproj_bwd(
        dproj_b, wb_in, x1, dx2, mod1, ng1 + e_bi[4][0, 0], seq, True, "b_inproj_bwd")

    dy_a, dout_a, dgate0 = outproj_bwd(dx1, out_a, mod0, wa_out, seq, "a_outproj_bwd")
    gwa_out = grad_w_out(y_a, dout_a, "a_grad_w_out").reshape(N_CHIPS, shard_rows, d)
    e_ao = exchange_start(gwa_out, "exchange_a_out_start")
    dproj_a, dws, dbs, dlg, dlbias = sgu_bwd(
        proj_a, dy_a, a_ln_gain + e_ao[4][0, 0], a_ln_bias, a_w_s[0], bs_col, "a_sgu_bwd")
    e_ai = exchange_start(grad_w_in(h_a, dproj_a, N_CHIPS, False, "a_grad_w_in"), "exchange_a_in_start")
    dx0, dshift0, dscale0, dng0 = inproj_bwd(
        dproj_a, wa_in, x0, dx1, mod0, norm_gain[0:1] + e_ai[4][0, 0], seq, False, "a_inproj_bwd")
    grad_x = dx0.reshape(nb, seq, d)

    dmod = jnp.concatenate([dshift0, dscale0, dgate0, dshift1, dscale1, dgate1], axis=2)
    n_dmod = dmod.size
    small_g = [jnp.concatenate([dng0, dng1], axis=0), dlg, dlbias, dws, dbs, dlb, dfg, dgn]
    packed_g = _pack([dmod] + small_g + [loss_part])
    rows = packed_g.shape[0]
    s_small = gather_all_start(
        lax.dynamic_update_slice(jnp.zeros((N_DEV, rows, LANES), F32), packed_g[None], (dev, 0, 0)),
        "gather_small_start")

    def finish(group, after):
        mine = []
        for ex, _, _, _, nm in group:
            parts_thru, land = exchange_wait(ex[0], ex[1], ex[2], ex[3], after, "exchange_" + nm + "_wait")
            mine.append(sum_parts(parts_thru, land, chip, "sum_" + nm))
            after = mine[-1]
        theirs = swap_sibling(mine, "swap_" + group[0][4])
        return [[r.reshape(w.shape) for r in adamw_pair(pa, pb, w[0], m[0], v[0], "adamw_" + nm)]
                for pa, pb, (_, w, m, v, nm) in zip(mine, theirs, group)]

    (gb_out, db_out, mb_out, vb_out), (gb_in, db_in, mb_in, vb_in), (ga_out, da_out, ma_out, va_out) = finish(
        [(e_bo, b_w_out, m_b_w_out, v_b_w_out, "b_out"), (e_bi, b_w_in, m_b_w_in, v_b_w_in, "b_in"),
         (e_ao, a_w_out, m_a_w_out, v_a_w_out, "a_out")], s_small[3])
    ((ga_in, da_in, ma_in, va_in),) = finish([(e_ai, a_w_in, m_a_w_in, v_a_w_in, "a_in")], ga_out)

    small_w = [norm_gain, a_ln_gain, a_ln_bias, a_w_s, a_b_s, b_lower_bounds, final_gain, b_gn_gain]
    small_m = [m_norm_gain, m_a_ln_gain, m_a_ln_bias, m_a_w_s, m_a_b_s, m_b_lower_bounds, m_final_gain, m_b_gn_gain]
    small_v = [v_norm_gain, v_a_ln_gain, v_a_ln_bias, v_a_w_s, v_a_b_s, v_b_lower_bounds, v_final_gain, v_b_gn_gain]
    rows_of = lambda a: a.reshape(-1, a.shape[-1])
    gathered = gather_all_wait(s_small[0], s_small[1], s_small[2], ga_in, "gather_small_wait")
    tail, small_res = small_update(
        gathered, n_dmod // LANES, [rows_of(a) for a in small_w], [rows_of(a) for a in small_m],
        [rows_of(a) for a in small_v], "small_update")
    loss = tail[0, 0]
    sg, sd, sm, sv = [[small_res[p][kind].reshape(w.shape) for p, w in enumerate(small_w)] for kind in range(4)]

    dmod_all = gathered[:, :n_dmod // LANES].reshape(N_DEV * nb, n_l, 3 * d)
    dmod_cols = lax.dynamic_slice_in_dim(dmod_all, chip * ada_cols, ada_cols, axis=2)
    dmod_cols = jnp.transpose(dmod_cols, (1, 0, 2))
    g_wada, d_wada, m_wada, v_wada = ada_bwd(c_all, dmod_cols, w_ada, m_w_ada, v_w_ada, "ada_bwd")
    flat = lambda a: a.reshape(1, -1)
    g_bada, d_bada, m_bada, v_bada = [
        r.reshape(b_ada.shape) for r in
        bias_update(dmod_all.reshape(N_DEV * nb, n_l * 3 * d), flat(b_ada), flat(m_b_ada), flat(v_b_ada), "bias_update")]

    def order(ng, wada, bada, ain, sm_rest, aout, bin_, bout):
        lg, lbi, ws_, bs_, lbd, fg_, gn_ = sm_rest
        return [ng, wada, bada, ain, lg, lbi, ws_, bs_, aout, bin_, lbd, gn_, bout, fg_]

    grads = order(sg[0], g_wada, g_bada, ga_in, sg[1:8], ga_out, gb_in, gb_out)
    deltas = order(sd[0], d_wada, d_bada, da_in, sd[1:8], da_out, db_in, db_out)
    new_m = order(sm[0], m_wada, m_bada, ma_in, sm[1:8], ma_out, mb_in, mb_out)
    new_v = order(sv[0], v_wada, v_bada, va_in, sv[1:8], va_out, vb_in, vb_out)
    return (loss, grad_x, *grads, *deltas, *new_m, *new_v)
```

```python
import jax
import jax.numpy as jnp
from jax import lax
from jax.experimental import pallas as pl
from jax.experimental.pallas import tpu as pltpu

F32 = jnp.float32
BF16 = jnp.bfloat16
EPS = 1e-6
CHUNK = 64
SG_BLOCK = 128
SG_GROUPS = 8
HEAD_DIM = 128
HG_WIDE = 8
HG_ROWS = 256
N_CHIPS = 4
N_DEV = 8
GATHER_PIECES = 4
LANES = 128
ADAM_LR = 0.001
ADAM_B1 = 0.9
ADAM_B2 = 0.999
ADAM_EPS = 1e-08
ADAM_WD = 0.01
ADAM_STEP = 10
GELU_C0 = 0.7978845608028654
GELU_C1 = 0.044715
MESH = pl.DeviceIdType.MESH
VMEM_LIMIT = 56 * 1024 * 1024


ROW_TILE = 1024


def _col_tile(n):
    return next(t for t in (1024, 768, 512, 256) if n % t == 0)


def _call(body, **kw):
    return pl.pallas_call(body, **kw)


def _params(**kw):
    return pltpu.CompilerParams(vmem_limit_bytes=VMEM_LIMIT, **kw)


def _sigmoid(x):
    return 0.5 * jnp.tanh(0.5 * x) + 0.5


def _sigmoid_small(x):
    return 1.0 / (1.0 + jnp.exp(-x))


def _silu_and_grad(x):
    s = _sigmoid(x)
    return x * s, s * (1.0 + x * (1.0 - s))


def _gelu(x):
    return 0.5 * x * (1.0 + jnp.tanh(GELU_C0 * (x + GELU_C1 * x * x * x)))


def _gelu_and_grad(x):
    t = jnp.tanh(GELU_C0 * (x + GELU_C1 * x * x * x))
    g = 0.5 * x * (1.0 + t)
    dg = 0.5 * (1.0 + t) + 0.5 * x * (1.0 - t * t) * (GELU_C0 * (1.0 + 3.0 * GELU_C1 * x * x))
    return g, dg


def _dot(a, b, dims, precision=None):
    return lax.dot_general(a, b, (dims, ((), ())), precision=precision, preferred_element_type=F32)


NN = ((1,), (0,))
NT = ((1,), (1,))
TN = ((0,), (0,))


def _adamw(w, g, m, v):
    m = ADAM_B1 * m + (1.0 - ADAM_B1) * g
    v = ADAM_B2 * v + (1.0 - ADAM_B2) * (g * g)
    m_hat = m / (1.0 - ADAM_B1 ** ADAM_STEP)
    v_hat = v / (1.0 - ADAM_B2 ** ADAM_STEP)
    delta = -ADAM_LR * (m_hat / (jnp.sqrt(v_hat) + ADAM_EPS) + ADAM_WD * w)
    return delta, m, v


def _chunk_mask():
    r = lax.broadcasted_iota(jnp.int32, (SG_BLOCK, SG_BLOCK), 0)
    c = lax.broadcasted_iota(jnp.int32, (SG_BLOCK, SG_BLOCK), 1)
    return (c // CHUNK) <= (r // CHUNK)


def _place():
    return lax.axis_index("x"), lax.axis_index("y"), lax.axis_index("c")


def _other_chips(x, y):
    return [(1 - x, y), (x, 1 - y), (1 - x, 1 - y)]


def allgather_small(v, name):
    m_per, n = v.shape

    def body(x_ref, out_ref, send_sems, recv_sems, local_sem):
        x, y, c = _place()
        me, sibling = (x, y, c), (x, y, 1 - c)
        chips = _other_chips(x, y)

        def rows(px, py, pc):
            return out_ref.at[pl.ds((4 * px + 2 * py + pc) * m_per, m_per), :]

        def copy(k, block, to, src=None):
            return pltpu.make_async_remote_copy(
                src_ref=rows(*block) if src is None else src, dst_ref=rows(*block),
                send_sem=send_sems.at[k], recv_sem=recv_sems.at[k], device_id=to, device_id_type=MESH)

        mine = pltpu.make_async_copy(x_ref, rows(*me), local_sem)
        mine.start()
        first = [copy(0, me, sibling, src=x_ref)]
        first += [copy(1 + j, me, (*chip, c), src=x_ref) for j, chip in enumerate(chips)]
        for cp in first:
            cp.start()
        passed = [copy(4 + j, (*chip, c), sibling) for j, chip in enumerate(chips)]
        for j, chip in enumerate(chips):
            copy(1 + j, (*chip, c), me).wait_recv()
            passed[j].start()
        copy(0, sibling, me).wait_recv()
        for j, chip in enumerate(chips):
            copy(4 + j, (*chip, 1 - c), me).wait_recv()
        for cp in first + passed:
            cp.wait_send()
        mine.wait()

    return _call(
        body, name=name,
        out_shape=jax.ShapeDtypeStruct((N_DEV * m_per, n), v.dtype),
        in_specs=[pl.BlockSpec(memory_space=pltpu.VMEM)],
        out_specs=pl.BlockSpec(memory_space=pltpu.VMEM),
        scratch_shapes=[pltpu.SemaphoreType.DMA((7,)), pltpu.SemaphoreType.DMA((7,)), pltpu.SemaphoreType.DMA],
    )(v)


def _hbm_spec():
    return pl.BlockSpec(memory_space=pltpu.HBM)


def _sem_spec():
    return pl.BlockSpec(memory_space=pltpu.SEMAPHORE)


def _split_params():
    return pltpu.CompilerParams(has_side_effects=pltpu.SideEffectType.DATAFLOW_SIDE_EFFECTING)


def _hbm(a):
    return pltpu.with_memory_space_constraint(a, pltpu.HBM)


def gather_inplace(lands, name):
    n = len(lands)
    per_land = 6 * GATHER_PIECES

    def body(*refs):
        land_refs, token = refs[n:2 * n], refs[2 * n]
        send_sems, recv_sems = refs[2 * n + 1:]
        x, y, c = _place()
        chips = _other_chips(x, y)

        def copy(w, k, q, chip_idx, core_half, to):
            piece = lands[w].shape[1] // (2 * GATHER_PIECES)
            rows = land_refs[w].at[chip_idx, pl.ds((core_half * GATHER_PIECES + q) * piece, piece), :]
            sem = per_land * w + GATHER_PIECES * k + q
            return pltpu.make_async_remote_copy(
                src_ref=rows, dst_ref=rows, send_sem=send_sems.at[sem], recv_sem=recv_sems.at[sem],
                device_id=to, device_id_type=MESH)

        order = [(q, w, j, px, py) for q in range(GATHER_PIECES) for w in range(n) for j, (px, py) in enumerate(chips)]
        first = [copy(w, j, q, 2 * x + y, c, (px, py, c)) for q, w, j, px, py in order]
        for cp in first:
            cp.start()
        passed = []
        for q, w, j, px, py in order:
            copy(w, j, q, 2 * px + py, c, (px, py, c)).wait_recv()
            passed.append(copy(w, 3 + j, q, 2 * px + py, c, (x, y, 1 - c)))
            passed[-1].start()
        for q, w, j, px, py in order:
            copy(w, 3 + j, q, 2 * px + py, 1 - c, (x, y, 1 - c)).wait_recv()
        for cp in first + passed:
            cp.wait_send()
        token[...] = jnp.zeros_like(token)

    res = _call(
        body, name=name,
        out_shape=[jax.ShapeDtypeStruct(a.shape, a.dtype) for a in lands] + [jax.ShapeDtypeStruct((8, LANES), F32)],
        in_specs=[_hbm_spec()] * n, out_specs=[_hbm_spec()] * n + [pl.BlockSpec(memory_space=pltpu.VMEM)],
        input_output_aliases={w: w for w in range(n)},
        scratch_shapes=[pltpu.SemaphoreType.DMA((per_land * n,)), pltpu.SemaphoreType.DMA((per_land * n,))],
    )(*lands)
    return res[:n], res[n]


def gather_start(land, name):
    def body(land_ref, send_sems, recv_sems, land_thru, token):
        del land_thru
        x, y, c = _place()
        for j, (px, py) in enumerate(_other_chips(x, y)):
            pltpu.make_async_remote_copy(
                src_ref=land_ref.at[2 * x + y], dst_ref=land_ref.at[2 * x + y],
                send_sem=send_sems.at[j], recv_sem=recv_sems.at[j], device_id=(px, py, c),
                device_id_type=MESH).start()
        token[...] = jnp.zeros_like(token)

    return _call(
        body, name=name,
        out_shape=(pltpu.SemaphoreType.DMA((3,)), pltpu.SemaphoreType.DMA((3,)),
                   pltpu.HBM(land.shape, land.dtype), jax.ShapeDtypeStruct((8, LANES), F32)),
        in_specs=(_hbm_spec(),),
        out_specs=(_sem_spec(), _sem_spec(), _hbm_spec(), pl.BlockSpec(memory_space=pltpu.VMEM)),
        input_output_aliases={0: 2}, compiler_params=_split_params(),
    )(_hbm(land))


def gather_wait(send_sems, recv_sems, land, after, name):
    def body(land_ref, send_sems, recv_sems, after_ref, land_out):
        del after_ref, land_out
        x, y, c = _place()
        for j, (px, py) in enumerate(_other_chips(x, y)):
            cp = pltpu.make_async_remote_copy(
                src_ref=land_ref.at[2 * x + y], dst_ref=land_ref.at[2 * px + py],
                send_sem=send_sems.at[j], recv_sem=recv_sems.at[j], device_id=(px, py, c), device_id_type=MESH)
            cp.wait_send()
            cp.wait_recv()

    return _call(
        body, name=name,
        out_shape=pltpu.HBM(land.shape, land.dtype),
        in_specs=(_hbm_spec(), _sem_spec(), _sem_spec(), pl.BlockSpec(memory_space=pl.ANY)),
        out_specs=_hbm_spec(), input_output_aliases={0: 0}, compiler_params=_split_params(),
    )(land, send_sems, recv_sems, after)


def _flips():
    return [(fx, fy, fc) for fx in (0, 1) for fy in (0, 1) for fc in (0, 1) if (fx, fy, fc) != (0, 0, 0)]


def _flipped(x, y, c, flip):
    fx, fy, fc = flip
    return (1 - x if fx else x, 1 - y if fy else y, 1 - c if fc else c)


def gather_all_start(land, name):
    def body(land_ref, send_sems, recv_sems, land_thru, token):
        del land_thru
        x, y, c = _place()
        for k, flip in enumerate(_flips()):
            pltpu.make_async_remote_copy(
                src_ref=land_ref.at[4 * x + 2 * y + c], dst_ref=land_ref.at[4 * x + 2 * y + c],
                send_sem=send_sems.at[k], recv_sem=recv_sems.at[k], device_id=_flipped(x, y, c, flip),
                device_id_type=MESH).start()
        token[...] = jnp.zeros_like(token)

    return _call(
        body, name=name,
        out_shape=(pltpu.SemaphoreType.DMA((7,)), pltpu.SemaphoreType.DMA((7,)),
                   pltpu.HBM(land.shape, land.dtype), jax.ShapeDtypeStruct((8, LANES), F32)),
        in_specs=(_hbm_spec(),),
        out_specs=(_sem_spec(), _sem_spec(), _hbm_spec(), pl.BlockSpec(memory_space=pltpu.VMEM)),
        input_output_aliases={0: 2}, compiler_params=_split_params(),
    )(_hbm(land))


def gather_all_wait(send_sems, recv_sems, land, after, name):
    def body(land_ref, send_sems, recv_sems, after_ref, land_out):
        del after_ref, land_out
        x, y, c = _place()
        for k, flip in enumerate(_flips()):
            px, py, pc = _flipped(x, y, c, flip)
            cp = pltpu.make_async_remote_copy(
                src_ref=land_ref.at[4 * x + 2 * y + c], dst_ref=land_ref.at[4 * px + 2 * py + pc],
                send_sem=send_sems.at[k], recv_sem=recv_sems.at[k], device_id=(px, py, pc), device_id_type=MESH)
            cp.wait_send()
            cp.wait_recv()

    return _call(
        body, name=name,
        out_shape=pltpu.HBM(land.shape, land.dtype),
        in_specs=(_hbm_spec(), _sem_spec(), _sem_spec(), pl.BlockSpec(memory_space=pl.ANY)),
        out_specs=_hbm_spec(), input_output_aliases={0: 0}, compiler_params=_split_params(),
    )(land, send_sems, recv_sems, after)


def exchange_start(parts, name):
    _, r, c_ = parts.shape

    def body(parts_ref, land_ref, send_sems, recv_sems, parts_thru, land_thru, token):
        del parts_thru, land_thru
        x, y, c = _place()
        for j, (px, py) in enumerate(_other_chips(x, y)):
            pltpu.make_async_remote_copy(
                src_ref=parts_ref.at[2 * px + py], dst_ref=land_ref.at[j],
                send_sem=send_sems.at[j], recv_sem=recv_sems.at[j], device_id=(px, py, c),
                device_id_type=MESH).start()
        token[...] = jnp.zeros_like(token)

    return _call(
        body, name=name,
        out_shape=(pltpu.SemaphoreType.DMA((3,)), pltpu.SemaphoreType.DMA((3,)),
                   pltpu.HBM(parts.shape, parts.dtype), pltpu.HBM((3, r, c_), parts.dtype),
                   jax.ShapeDtypeStruct((8, LANES), F32)),
        in_specs=(_hbm_spec(), _hbm_spec()),
        out_specs=(_sem_spec(), _sem_spec(), _hbm_spec(), _hbm_spec(), pl.BlockSpec(memory_space=pltpu.VMEM)),
        input_output_aliases={0: 2, 1: 3}, compiler_params=_split_params(),
    )(_hbm(parts), _hbm(lax.empty((3, r, c_), parts.dtype)))


def exchange_wait(send_sems, recv_sems, parts, land, after, name):
    def body(parts_ref, land_ref, send_sems, recv_sems, after_ref, parts_out, land_out):
        del after_ref, parts_out, land_out
        x, y, c = _place()
        for j, (px, py) in enumerate(_other_chips(x, y)):
            cp = pltpu.make_async_remote_copy(
                src_ref=parts_ref.at[2 * px + py], dst_ref=land_ref.at[j],
                send_sem=send_sems.at[j], recv_sem=recv_sems.at[j], device_id=(px, py, c), device_id_type=MESH)
            cp.wait_send()
            cp.wait_recv()

    return _call(
        body, name=name,
        out_shape=(pltpu.HBM(parts.shape, parts.dtype), pltpu.HBM(land.shape, land.dtype)),
        in_specs=(_hbm_spec(), _hbm_spec(), _sem_spec(), _sem_spec(), pl.BlockSpec(memory_space=pl.ANY)),
        out_specs=(_hbm_spec(), _hbm_spec()), input_output_aliases={0: 0, 1: 1},
        compiler_params=_split_params(),
    )(parts, land, send_sems, recv_sems, after)


def cast_into_slot(w, chip, after, name):
    r, c = w.shape
    tr = min(256, r)

    def body(s_ref, w_ref, after_ref, o_ref):
        del s_ref, after_ref
        o_ref[...] = w_ref[...].astype(BF16)

    return _call(
        body, name=name,
        grid_spec=pltpu.PrefetchScalarGridSpec(
            num_scalar_prefetch=1, grid=(r // tr,),
            in_specs=[pl.BlockSpec((tr, c), lambda i, s: (i, 0)), pl.BlockSpec(memory_space=pl.ANY)],
            out_specs=pl.BlockSpec((None, tr, c), lambda i, s: (s[0], i, 0))),
        out_shape=jax.ShapeDtypeStruct((N_CHIPS, r, c), BF16),
        compiler_params=_params(),
    )(chip.reshape(1).astype(jnp.int32), w, after)


def sum_parts(parts, land, chip, name):
    _, r, c = parts.shape
    tr = min(256, r)

    def body(s_ref, p_ref, l_ref, o_ref):
        del s_ref
        acc = p_ref[...].astype(F32) + l_ref[0].astype(F32)
        acc = acc + l_ref[1].astype(F32)
        o_ref[...] = (acc + l_ref[2].astype(F32)).astype(BF16)

    return _call(
        body, name=name,
        grid_spec=pltpu.PrefetchScalarGridSpec(
            num_scalar_prefetch=1, grid=(r // tr,),
            in_specs=[pl.BlockSpec((None, tr, c), lambda i, s: (s[0], i, 0)),
                      pl.BlockSpec((3, tr, c), lambda i, s: (0, i, 0))],
            out_specs=pl.BlockSpec((tr, c), lambda i, s: (i, 0))),
        out_shape=jax.ShapeDtypeStruct((r, c), BF16),
        compiler_params=_params(),
    )(chip.reshape(1).astype(jnp.int32), parts, land)


def swap_sibling(arrs, name):
    n = len(arrs)

    def body(*refs):
        ins, outs = refs[:n], refs[n:2 * n]
        send_sems, recv_sems = refs[2 * n:]
        x, y, c = _place()
        cps = []
        for w in range(n):
            cp = pltpu.make_async_remote_copy(
                src_ref=ins[w], dst_ref=outs[w], send_sem=send_sems.at[w], recv_sem=recv_sems.at[w],
                device_id=(x, y, 1 - c), device_id_type=MESH)
            cp.start()
            cps.append(cp)
        for cp in cps:
            cp.wait_recv()
        for cp in cps:
            cp.wait_send()

    return _call(
        body, name=name,
        out_shape=[jax.ShapeDtypeStruct(a.shape, a.dtype) for a in arrs],
        in_specs=[_hbm_spec()] * n, out_specs=[_hbm_spec()] * n,
        scratch_shapes=[pltpu.SemaphoreType.DMA((n,)), pltpu.SemaphoreType.DMA((n,))],
    )(*arrs)


def adamw_pair(pa, pb, w, m, v, name):
    r, c = w.shape
    tr = min(128, r)

    def body(pa_ref, pb_ref, w_ref, m_ref, v_ref, g_ref, d_ref, nm_ref, nv_ref):
        g = pa_ref[...].astype(F32) + pb_ref[...].astype(F32)
        d, nm, nv = _adamw(w_ref[...], g, m_ref[...], v_ref[...])
        g_ref[...] = g
        d_ref[...] = d
        nm_ref[...] = nm
        nv_ref[...] = nv

    spec = pl.BlockSpec((tr, c), lambda i: (i, 0))
    return _call(
        body, name=name, grid=(r // tr,),
        out_shape=[jax.ShapeDtypeStruct((r, c), F32)] * 4,
        in_specs=[spec] * 5, out_specs=[spec] * 4,
        compiler_params=_params(),
    )(pa, pb, w, m, v)


def small_update(gathered, first_row, ws, ms, vs, name):
    n_w = len(ws)
    total_rows = gathered.shape[1]

    def body(*refs):
        g_ref = refs[0]
        w_refs, m_refs, v_refs = refs[1:1 + n_w], refs[1 + n_w:1 + 2 * n_w], refs[1 + 2 * n_w:1 + 3 * n_w]
        tail_ref = refs[1 + 3 * n_w]
        outs = refs[2 + 3 * n_w:2 + 7 * n_w]
        sum_ref = refs[2 + 7 * n_w]
        acc = g_ref[0]
        for k in range(1, N_DEV):
            acc = acc + g_ref[k]
        sum_ref[...] = acc
        row = first_row
        for p in range(n_w):
            a, b = ws[p].shape
            per = b // LANES
            g_out, d_out, m_out, v_out = outs[4 * p:4 * p + 4]
            if per == 1:
                g_out[...] = sum_ref[row:row + a, :]
            else:
                for i in range(a):
                    for jc in range(per):
                        g_out[i:i + 1, jc * LANES:(jc + 1) * LANES] = sum_ref[row + i * per + jc:row + i * per + jc + 1, :]
            row += a * per
            dl, nm, nv = _adamw(w_refs[p][...], g_out[...], m_refs[p][...], v_refs[p][...])
            d_out[...] = dl
            m_out[...] = nm
            v_out[...] = nv
        tail_ref[...] = sum_ref[row:row + 1, :]

    out_shape = [jax.ShapeDtypeStruct((1, LANES), F32)]
    for w in ws:
        out_shape += [jax.ShapeDtypeStruct(w.shape, F32)] * 4
    res = _call(
        body, name=name, out_shape=out_shape,
        scratch_shapes=[pltpu.VMEM((total_rows, LANES), F32)],
        compiler_params=_params(),
    )(gathered, *ws, *ms, *vs)
    return res[0], [res[1 + 4 * p:5 + 4 * p] for p in range(n_w)]


def ada_fwd(c_all, w_ada, b_cols, name):
    n_l, d, cols = w_ada.shape
    nb = c_all.shape[0]
    tn = 256

    def body(c_ref, w_ref, b_ref, o_ref):
        cv = c_ref[...]
        ca = (cv * _sigmoid(cv)).astype(BF16)
        o_ref[...] = _dot(ca, w_ref[...].astype(BF16), NN) + b_ref[...]

    return _call(
        body, name=name, grid=(n_l, cols // tn),
        out_shape=jax.ShapeDtypeStruct((n_l, nb, cols), F32),
        in_specs=[pl.BlockSpec((nb, d), lambda l, j: (0, 0)),
                  pl.BlockSpec((None, d, tn), lambda l, j: (l, 0, j)),
                  pl.BlockSpec((None, 1, tn), lambda l, j: (l, 0, j))],
        out_specs=pl.BlockSpec((None, nb, tn), lambda l, j: (l, 0, j)),
        compiler_params=_params(),
    )(c_all, w_ada, b_cols)


def ada_bwd(c_all, dmod_cols, w, m, v, name):
    n_l, d, cols = w.shape
    nb = c_all.shape[0]
    tn = 256

    def body(c_ref, dm_ref, w_ref, m_ref, v_ref, g_ref, d_ref, nm_ref, nv_ref):
        cv = c_ref[...]
        ca = (cv * _sigmoid(cv)).astype(BF16)
        g = _dot(ca, dm_ref[...].astype(BF16), TN)
        dl, nm, nv = _adamw(w_ref[...], g, m_ref[...], v_ref[...])
        g_ref[...] = g
        d_ref[...] = dl
        nm_ref[...] = nm
        nv_ref[...] = nv

    wspec = pl.BlockSpec((None, d, tn), lambda l, j: (l, 0, j))
    return _call(
        body, name=name, grid=(n_l, cols // tn),
        out_shape=[jax.ShapeDtypeStruct((n_l, d, cols), F32)] * 4,
        in_specs=[pl.BlockSpec((nb, d), lambda l, j: (0, 0)),
                  pl.BlockSpec((None, nb, tn), lambda l, j: (l, 0, j)),
                  wspec, wspec, wspec],
        out_specs=[wspec] * 4,
        compiler_params=_params(),
    )(c_all, dmod_cols, w, m, v)


def bias_update(dmod_all, w, m, v, name):
    def body(dm_ref, w_ref, m_ref, v_ref, g_ref, d_ref, nm_ref, nv_ref):
        g = jnp.sum(dm_ref[...], axis=0, keepdims=True)
        dl, nm, nv = _adamw(w_ref[...], g, m_ref[...], v_ref[...])
        g_ref[...] = g
        d_ref[...] = dl
        nm_ref[...] = nm
        nv_ref[...] = nv

    return _call(
        body, name=name,
        out_shape=[jax.ShapeDtypeStruct(w.shape, F32)] * 4,
        compiler_params=_params(),
    )(dmod_all, w, m, v)


def inproj_fwd(x, mod, ng, wg, seq, sectioned, name):
    m_rows, d = x.shape
    nsh, _, ns = wg.shape
    n = nsh * ns
    tm, tn = min(2 * ROW_TILE, seq), _col_tile(ns)
    per = ns // tn

    def body(x_ref, mod_ref, ng_ref, w_ref, proj_ref, h_ref):
        @pl.when(pl.program_id(1) == 0)
        def _():
            xv = x_ref[...]
            r = lax.rsqrt(jnp.mean(xv * xv, axis=-1, keepdims=True) + EPS)
            md = mod_ref[0]
            h = (xv * r * ng_ref[...]) * (1.0 + md[:, d:2 * d]) + md[:, :d]
            h_ref[...] = h.astype(BF16)
        proj_ref[...] = _dot(h_ref[...], w_ref[...], NN)

    if sectioned:
        proj_shape = (nsh, m_rows, ns)
        proj_spec = pl.BlockSpec((None, tm, tn), lambda i, j: (j // per, i, j % per))
    else:
        proj_shape = (m_rows, n)
        proj_spec = pl.BlockSpec((tm, tn), lambda i, j: (i, j))
    return _call(
        body, name=name, grid=(m_rows // tm, n // tn),
        out_shape=[jax.ShapeDtypeStruct(proj_shape, F32), jax.ShapeDtypeStruct((m_rows, d), BF16)],
        in_specs=[pl.BlockSpec((tm, d), lambda i, j: (i, 0)),
                  pl.BlockSpec((1, 1, 3 * d), lambda i, j: ((i * tm) // seq, 0, 0)),
                  pl.BlockSpec((1, d), lambda i, j: (0, 0)),
                  pl.BlockSpec((None, d, tn), lambda i, j: (j // per, 0, j % per))],
        out_specs=[proj_spec, pl.BlockSpec((tm, d), lambda i, j: (i, 0))],
        compiler_params=_params(),
    )(x, mod, ng, wg)


def outproj_fwd(y, w, x, mod, seq, name):
    m_rows, di = y.shape
    d = w.shape[1]
    tm = min(ROW_TILE, seq)

    def body(y_ref, w_ref, x_ref, mod_ref, xn_ref, out_ref):
        acc = _dot(y_ref[...], w_ref[...], NN)
        out_ref[...] = acc.astype(BF16)
        xn_ref[...] = x_ref[...] + mod_ref[0][:, 2 * d:] * acc

    row = pl.BlockSpec((tm, d), lambda i: (i, 0))
    return _call(
        body, name=name, grid=(m_rows // tm,),
        out_shape=[jax.ShapeDtypeStruct((m_rows, d), F32), jax.ShapeDtypeStruct((m_rows, d), BF16)],
        in_specs=[pl.BlockSpec((tm, di), lambda i: (i, 0)),
                  pl.BlockSpec((di, d), lambda i: (0, 0)),
                  row,
                  pl.BlockSpec((1, 1, 3 * d), lambda i: ((i * tm) // seq, 0, 0))],
        out_specs=[row, row],
        compiler_params=_params(),
    )(y, w, x, mod)


def outproj_bwd(dxo, out, mod, w, seq, name):
    m_rows, d = dxo.shape
    di = w.shape[0]
    nb = m_rows // seq
    tm, tn = min(ROW_TILE, seq), _col_tile(di)

    def body(dxo_ref, out_ref, mod_ref, w_ref, dy_ref, dout_ref, dgate_ref):
        i = pl.program_id(0)

        @pl.when(pl.program_id(1) == 0)
        def _():
            dx = dxo_ref[...]
            dout_ref[...] = (mod_ref[0][:, 2 * d:] * dx).astype(BF16)
            part = jnp.sum(dx * out_ref[...].astype(F32), axis=0, keepdims=True)

            @pl.when((i * tm) % seq == 0)
            def _():
                dgate_ref[0] = part

            @pl.when((i * tm) % seq != 0)
            def _():
                dgate_ref[0] = dgate_ref[0] + part

        dy_ref[...] = _dot(dout_ref[...], w_ref[...], NT).astype(BF16)

    row = pl.BlockSpec((tm, d), lambda i, j: (i, 0))
    return _call(
        body, name=name, grid=(m_rows // tm, di // tn),
        out_shape=[jax.ShapeDtypeStruct((m_rows, di), BF16), jax.ShapeDtypeStruct((m_rows, d), BF16),
                   jax.ShapeDtypeStruct((nb, 1, d), F32)],
        in_specs=[row, row,
                  pl.BlockSpec((1, 1, 3 * d), lambda i, j: ((i * tm) // seq, 0, 0)),
                  pl.BlockSpec((tn, d), lambda i, j: (j, 0))],
        out_specs=[pl.BlockSpec((tm, tn), lambda i, j: (i, j)), row,
                   pl.BlockSpec((1, 1, d), lambda i, j: ((i * tm) // seq, 0, 0))],
        compiler_params=_params(),
    )(dxo, out, mod, w)


def grad_w_out(y, dout, name):
    m_rows, di = y.shape
    d = dout.shape[1]
    tm, tk = min(ROW_TILE, m_rows), _col_tile(di)
    n_m = m_rows // tm

    def body(y_ref, do_ref, o_ref, acc_ref):
        mi = pl.program_id(1)

        @pl.when(mi == 0)
        def _():
            acc_ref[...] = jnp.zeros_like(acc_ref)

        acc_ref[...] += _dot(y_ref[...], do_ref[...], TN)

        @pl.when(mi == n_m - 1)
        def _():
            o_ref[...] = acc_ref[...].astype(BF16)

    return _call(
        body, name=name, grid=(di // tk, n_m),
        out_shape=jax.ShapeDtypeStruct((di, d), BF16),
        in_specs=[pl.BlockSpec((tm, tk), lambda j, mi: (mi, j)),
                  pl.BlockSpec((tm, d), lambda j, mi: (mi, 0))],
        out_specs=pl.BlockSpec((tk, d), lambda j, mi: (j, 0)),
        scratch_shapes=[pltpu.VMEM((tk, d), F32)],
        compiler_params=_params(),
    )(y, dout)


def grad_w_in(h, dproj, nsh, sectioned, name):
    m_rows, d = h.shape
    n = dproj.shape[0] * dproj.shape[2] if sectioned else dproj.shape[1]
    ns = n // nsh
    tm, tn = min(ROW_TILE, m_rows), ns
    per = ns // tn
    n_m = m_rows // tm

    def body(h_ref, dp_ref, o_ref, acc_ref):
        mi = pl.program_id(1)
        @pl.when(mi == 0)
        def _():
            acc_ref[...] = jnp.zeros_like(acc_ref)

        acc_ref[...] += _dot(h_ref[...], dp_ref[...], TN)

        @pl.when(mi == n_m - 1)
        def _():
            o_ref[...] = acc_ref[...].astype(BF16)

    if sectioned:
        dp_spec = pl.BlockSpec((None, tm, tn), lambda j, mi: (j // per, mi, j % per))
    else:
        dp_spec = pl.BlockSpec((tm, tn), lambda j, mi: (mi, j))
    return _call(
        body, name=name, grid=(n // tn, n_m),
        out_shape=jax.ShapeDtypeStruct((nsh, d, ns), BF16),
        in_specs=[pl.BlockSpec((tm, d), lambda j, mi: (mi, 0)), dp_spec],
        out_specs=pl.BlockSpec((None, d, tn), lambda j, mi: (j // per, 0, j % per)),
        scratch_shapes=[pltpu.VMEM((d, tn), F32)],
        compiler_params=_params(),
    )(h, dproj)


def inproj_bwd(dproj, wg, x, dxo, mod, ng, seq, sectioned, name):
    m_rows, d = x.shape
    nsh, _, ns = wg.shape
    n = nsh * ns
    nb = m_rows // seq
    tm, tk = min(ROW_TILE, seq), ns
    per = ns // tk
    n_k = n // tk

    def body(dp_ref, w_ref, x_ref, dxo_ref, mod_ref, ng_ref, dxi_ref, dsh_ref, dsc_ref, dng_ref, acc_ref):
        i, k = pl.program_id(0), pl.program_id(1)
        @pl.when(k == 0)
        def _():
            acc_ref[...] = jnp.zeros_like(acc_ref)

        acc_ref[...] += _dot(dp_ref[...], w_ref[...], NT)

        @pl.when(k == n_k - 1)
        def _():
            dh = acc_ref[...]
            xv = x_ref[...]
            r = lax.rsqrt(jnp.mean(xv * xv, axis=-1, keepdims=True) + EPS)
            xn = xv * r
            md = mod_ref[0]
            gain = ng_ref[...]
            p_shift = jnp.sum(dh, axis=0, keepdims=True)
            p_scale = jnp.sum(dh * (xn * gain), axis=0, keepdims=True)
            drn = dh * (1.0 + md[:, d:2 * d])
            p_ng = jnp.sum(drn * xn, axis=0, keepdims=True)
            dxn = drn * gain
            dx = r * (dxn - xn * jnp.mean(dxn * xn, axis=-1, keepdims=True))
            dxi_ref[...] = dxo_ref[...] + dx

            @pl.when((i * tm) % seq == 0)
            def _():
                dsh_ref[0] = p_shift
                dsc_ref[0] = p_scale

            @pl.when((i * tm) % seq != 0)
            def _():
                dsh_ref[0] = dsh_ref[0] + p_shift
                dsc_ref[0] = dsc_ref[0] + p_scale

            @pl.when(i == 0)
            def _():
                dng_ref[...] = p_ng

            @pl.when(i != 0)
            def _():
                dng_ref[...] = dng_ref[...] + p_ng

    if sectioned:
        dp_spec = pl.BlockSpec((None, tm, tk), lambda i, k: (k // per, i, k % per))
    else:
        dp_spec = pl.BlockSpec((tm, tk), lambda i, k: (i, k))
    row = pl.BlockSpec((tm, d), lambda i, k: (i, 0))
    per_seq = pl.BlockSpec((1, 1, d), lambda i, k: ((i * tm) // seq, 0, 0))
    return _call(
        body, name=name, grid=(m_rows // tm, n_k),
        out_shape=[jax.ShapeDtypeStruct((m_rows, d), F32), jax.ShapeDtypeStruct((nb, 1, d), F32),
                   jax.ShapeDtypeStruct((nb, 1, d), F32), jax.ShapeDtypeStruct((1, d), F32)],
        in_specs=[dp_spec,
                  pl.BlockSpec((None, d, tk), lambda i, k: (k // per, 0, k % per)),
                  row, row,
                  pl.BlockSpec((1, 1, 3 * d), lambda i, k: ((i * tm) // seq, 0, 0)),
                  pl.BlockSpec((1, d), lambda i, k: (0, 0))],
        out_specs=[row, per_seq, per_seq, pl.BlockSpec((1, d), lambda i, k: (0, 0))],
        scratch_shapes=[pltpu.VMEM((tm, d), F32)],
        compiler_params=_params(),
    )(dproj, wg, x, dxo, mod, ng)


def _sgu_stats(proj_ref, vg_ref, di, gd, dgel_ref=None):
    s1 = jnp.zeros((SG_BLOCK, 1), F32)
    for g in range(SG_GROUPS):
        v_pre = proj_ref[:, di + g * gd:di + (g + 1) * gd]
        if dgel_ref is None:
            vg = _gelu(v_pre)
        else:
            vg, dgel_ref[:, g * gd:(g + 1) * gd] = _gelu_and_grad(v_pre)
        vg_ref[:, g * gd:(g + 1) * gd] = vg
        s1 = s1 + jnp.sum(vg, axis=1, keepdims=True)
    mu = s1 / di
    s2 = jnp.zeros((SG_BLOCK, 1), F32)
    for g in range(SG_GROUPS):
        dv = vg_ref[:, g * gd:(g + 1) * gd] - mu
        s2 = s2 + jnp.sum(dv * dv, axis=1, keepdims=True)
    return mu, lax.rsqrt(s2 / di + EPS)


def sgu_fwd(proj, ln_gain, ln_bias, ws, bs, name):
    m_rows, n3 = proj.shape
    di = n3 // 3
    gd = di // SG_GROUPS

    def body(proj_ref, lg_ref, lb_ref, ws_ref, bs_ref, y_ref, wsm_ref, vg_ref):
        @pl.when(pl.program_id(0) == 0)
        def _():
            mask = _chunk_mask()
            for g in range(SG_GROUPS):
                wsm_ref[g] = jnp.where(mask, ws_ref[g], 0.0).astype(BF16)

        mu, rstd = _sgu_stats(proj_ref, vg_ref, di, gd)
        for g in range(SG_GROUPS):
            cs = slice(g * gd, (g + 1) * gd)
            vln = (vg_ref[:, cs] - mu) * rstd * lg_ref[:, cs] + lb_ref[:, cs]
            s = _dot(wsm_ref[g], vln.astype(BF16), NN) + bs_ref[g]
            u = _gelu(proj_ref[:, cs])
            gp = proj_ref[:, 2 * di + g * gd:2 * di + (g + 1) * gd]
            y_ref[:, cs] = (u * s * (gp * _sigmoid(gp))).astype(BF16)

    full = lambda shape: pl.BlockSpec(shape, lambda i: (0,) * len(shape))
    return _call(
        body, name=name, grid=(m_rows // SG_BLOCK,),
        out_shape=jax.ShapeDtypeStruct((m_rows, di), BF16),
        in_specs=[pl.BlockSpec((SG_BLOCK, n3), lambda i: (i, 0)),
                  full((1, di)), full((1, di)),
                  full((SG_GROUPS, SG_BLOCK, SG_BLOCK)), full((SG_GROUPS, SG_BLOCK, 1))],
        out_specs=pl.BlockSpec((SG_BLOCK, di), lambda i: (i, 0)),
        scratch_shapes=[pltpu.VMEM((SG_GROUPS, SG_BLOCK, SG_BLOCK), BF16), pltpu.VMEM((SG_BLOCK, di), F32)],
        compiler_params=_params(),
    )(proj, ln_gain, ln_bias, ws, bs)


def sgu_bwd(proj, dy, ln_gain, ln_bias, ws, bs, name):
    m_rows, n3 = proj.shape
    di = n3 // 3
    gd = di // SG_GROUPS
    n_i = m_rows // SG_BLOCK

    def body(proj_ref, dy_ref, lg_ref, lb_ref, ws_ref, bs_ref,
             dp_ref, dws_ref, dbs_ref, dlg_ref, dlb_ref, wsm_ref, vg_ref, dvh_ref, dgel_ref):
        i = pl.program_id(0)

        def before():
            @pl.when(i == 0)
            def _():
                mask = _chunk_mask()
                for g in range(SG_GROUPS):
                    wsm_ref[g] = jnp.where(mask, ws_ref[g], 0.0).astype(BF16)
                dws_ref[...] = jnp.zeros_like(dws_ref)
                dbs_ref[...] = jnp.zeros_like(dbs_ref)
                dlg_ref[...] = jnp.zeros_like(dlg_ref)
                dlb_ref[...] = jnp.zeros_like(dlb_ref)

        def after():
            @pl.when(i == n_i - 1)
            def _():
                mask = _chunk_mask()
                for g in range(SG_GROUPS):
                    dws_ref[g] = jnp.where(mask, dws_ref[g], 0.0)

        before()
        mu, rstd = _sgu_stats(proj_ref, vg_ref, di, gd, dgel_ref)
        m1 = jnp.zeros((SG_BLOCK, 1), F32)
        m2 = jnp.zeros((SG_BLOCK, 1), F32)
        for g in range(SG_GROUPS):
            cs = slice(g * gd, (g + 1) * gd)
            gs = slice(2 * di + g * gd, 2 * di + (g + 1) * gd)
            gain = lg_ref[:, cs]
            vhat = (vg_ref[:, cs] - mu) * rstd
            vln_b = (vhat * gain + lb_ref[:, cs]).astype(BF16)
            s = _dot(wsm_ref[g], vln_b, NN) + bs_ref[g]
            u, du = _gelu_and_grad(proj_ref[:, cs])
            sg, dsg = _silu_and_grad(proj_ref[:, gs])
            dyv = dy_ref[:, cs].astype(F32)
            dp_ref[:, cs] = (dyv * s * sg * du).astype(BF16)
            dp_ref[:, gs] = (dyv * u * s * dsg).astype(BF16)
            ds = dyv * u * sg
            ds_b = ds.astype(BF16)
            dws_ref[g] = dws_ref[g] + _dot(ds_b, vln_b, NT)
            dbs_ref[g] = dbs_ref[g] + jnp.sum(ds, axis=1, keepdims=True)
            dvln = _dot(wsm_ref[g], ds_b, TN)
            dlg_ref[:, cs] = dlg_ref[:, cs] + jnp.sum(dvln * vhat, axis=0, keepdims=True)
            dlb_ref[:, cs] = dlb_ref[:, cs] + jnp.sum(dvln, axis=0, keepdims=True)
            dvh = dvln * gain
            dvh_ref[:, cs] = dvh
            m1 = m1 + jnp.sum(dvh, axis=1, keepdims=True)
            m2 = m2 + jnp.sum(dvh * vhat, axis=1, keepdims=True)
        m1 = m1 / di
        m2 = m2 / di
        for g in range(SG_GROUPS):
            cs = slice(g * gd, (g + 1) * gd)
            vs = slice(di + g * gd, di + (g + 1) * gd)
            vhat = (vg_ref[:, cs] - mu) * rstd
            dvg = rstd * (dvh_ref[:, cs] - m1 - vhat * m2)
            dp_ref[:, vs] = (dvg * dgel_ref[:, cs]).astype(BF16)

        after()

    full = lambda shape: pl.BlockSpec(shape, lambda i: (0,) * len(shape))
    return _call(
        body, name=name, grid=(n_i,),
        out_shape=[jax.ShapeDtypeStruct((m_rows, n3), BF16),
                   jax.ShapeDtypeStruct((SG_GROUPS, SG_BLOCK, SG_BLOCK), F32),
                   jax.ShapeDtypeStruct((SG_GROUPS, SG_BLOCK, 1), F32),
                   jax.ShapeDtypeStruct((1, di), F32), jax.ShapeDtypeStruct((1, di), F32)],
        in_specs=[pl.BlockSpec((SG_BLOCK, n3), lambda i: (i, 0)),
                  pl.BlockSpec((SG_BLOCK, di), lambda i: (i, 0)),
                  full((1, di)), full((1, di)),
                  full((SG_GROUPS, SG_BLOCK, SG_BLOCK)), full((SG_GROUPS, SG_BLOCK, 1))],
        out_specs=[pl.BlockSpec((SG_BLOCK, n3), lambda i: (i, 0)),
                   full((SG_GROUPS, SG_BLOCK, SG_BLOCK)), full((SG_GROUPS, SG_BLOCK, 1)),
                   full((1, di)), full((1, di))],
        scratch_shapes=[pltpu.VMEM((SG_GROUPS, SG_BLOCK, SG_BLOCK), BF16),
                        pltpu.VMEM((SG_BLOCK, di), F32), pltpu.VMEM((SG_BLOCK, di), F32),
                        pltpu.VMEM((SG_BLOCK, di), F32)],
        compiler_params=_params(),
    )(proj, dy, ln_gain, ln_bias, ws, bs)


def _lower_bound(lbraw):
    mx = jnp.maximum(lbraw[0:1, :], lbraw[1:2, :])
    e0 = jnp.exp(lbraw[0:1, :] - mx)
    e1 = jnp.exp(lbraw[1:2, :] - mx)
    p0 = e0 / (e0 + e1)
    p1 = e1 / (e0 + e1)
    return (p0 + p1) - p0, p0, p1


def _tri(lower):
    r = lax.broadcasted_iota(jnp.int32, (CHUNK, CHUNK), 0)
    c = lax.broadcasted_iota(jnp.int32, (CHUNK, CHUNK), 1)
    return ((r >= c) if lower else (c >= r)).astype(F32)


def _row(a, idx):
    r = lax.broadcasted_iota(jnp.int32, a.shape, 0)
    return jnp.sum(jnp.where(r == idx, a, 0.0), axis=0, keepdims=True)


def _hgrn_gates(qp, fp, lb, tri):
    sgm = _sigmoid_small(fp)
    f = lb + (1.0 - lb) * sgm
    k = 1.0 - f
    a = _dot(tri, jnp.log(f), NN, precision=lax.Precision.HIGHEST)
    a_mid = _row(a, CHUNK // 2 - 1)
    a_last = _row(a, CHUNK - 1)
    q, dq = _silu_and_grad(qp)
    e1, e2, e3, e4 = jnp.exp(a - a_mid), jnp.exp(a_mid - a), jnp.exp(a), jnp.exp(a_last - a)
    return dict(sgm=sgm, f=f, k=k, q=q, dq=dq, e1=e1, e2=e2, e3=e3, e4=e4, dec=jnp.exp(a_last),
                q_in=q * e1, k_in=k * e2, q_out=q * e3, k_out=k * e4)


def _causal():
    r = lax.broadcasted_iota(jnp.int32, (CHUNK, CHUNK), 0)
    c = lax.broadcasted_iota(jnp.int32, (CHUNK, CHUNK), 1)
    return r >= c


def hgrn_fwd(proj4, lbraw, gn, seq, name):
    _, m_rows, di = proj4.shape
    nb, nh, nc = m_rows // seq, di // HEAD_DIM, seq // CHUNK
    rows = min(HG_ROWS, seq)
    wide = HG_WIDE * HEAD_DIM
    ns, cpb = seq // rows, rows // CHUNK

    def body(p_ref, lb_ref, gn_ref, y_ref, sts_ref, st_ref):
        @pl.when(pl.program_id(2) == 0)
        def _():
            st_ref[...] = jnp.zeros_like(st_ref)

        tri = _tri(True)
        causal = _causal()
        gain = gn_ref[...]
        lbs = [_lower_bound(lb_ref[:, j * HEAD_DIM:(j + 1) * HEAD_DIM])[0] for j in range(HG_WIDE)]

        units = [(n, j) for n in range(cpb) for j in range(HG_WIDE)]
        rs = lambda n: slice(n * CHUNK, (n + 1) * CHUNK)
        cs = lambda j: slice(j * HEAD_DIM, (j + 1) * HEAD_DIM)
        gates, v_b, sc_b, kv, o_in, o_x = {}, {}, {}, {}, {}, {}
        for n, j in units:
            gates[n, j] = _hgrn_gates(p_ref[0, rs(n), cs(j)], p_ref[1, rs(n), cs(j)], lbs[j], tri)
            v_b[n, j] = p_ref[2, rs(n), cs(j)].astype(BF16)
        for u in units:
            t = gates[u]
            sc_b[u] = jnp.where(causal, _dot(t["q_in"].astype(BF16), t["k_in"].astype(BF16), NT), 0.0).astype(BF16)
            kv[u] = _dot(v_b[u], t["k_out"].astype(BF16), TN)
        for u in units:
            o_in[u] = _dot(sc_b[u], v_b[u], NN)
        for j in range(HG_WIDE):
            st = st_ref[j]
            for n in range(cpb):
                sts_ref[n, :, cs(j)] = st
                o_x[n, j] = _dot(gates[n, j]["q_out"].astype(BF16), st.astype(BF16), NT)
                st = st * gates[n, j]["dec"] + kv[n, j]
            st_ref[j] = st
        for n, j in units:
            o = o_in[n, j] + o_x[n, j]
            r = lax.rsqrt(jnp.mean(o * o, axis=-1, keepdims=True) + EPS)
            gp = p_ref[3, rs(n), cs(j)]
            y_ref[rs(n), cs(j)] = ((o * r * gain) * (gp * _sigmoid(gp))).astype(BF16)

    return _call(
        body, name=name, grid=(nh // HG_WIDE, nb, ns),
        out_shape=[jax.ShapeDtypeStruct((m_rows, di), BF16),
                   jax.ShapeDtypeStruct((nb * nc, HEAD_DIM, di), F32)],
        in_specs=[pl.BlockSpec((4, rows, wide), lambda hg, b, s: (0, b * ns + s, hg)),
                  pl.BlockSpec((2, wide), lambda hg, b, s: (0, hg)),
                  pl.BlockSpec((1, HEAD_DIM), lambda hg, b, s: (0, 0))],
        out_specs=[pl.BlockSpec((rows, wide), lambda hg, b, s: (b * ns + s, hg)),
                   pl.BlockSpec((cpb, HEAD_DIM, wide), lambda hg, b, s: (b * ns + s, 0, hg))],
        scratch_shapes=[pltpu.VMEM((HG_WIDE, HEAD_DIM, HEAD_DIM), F32)],
        compiler_params=_params(),
    )(proj4, lbraw, gn)


def hgrn_bwd(proj4, dy, sts, lbraw, gn, seq, name):
    _, m_rows, di = proj4.shape
    nb, nh, nc = m_rows // seq, di // HEAD_DIM, seq // CHUNK
    rows = min(HG_ROWS, seq)
    wide = HG_WIDE * HEAD_DIM
    ns, cpb = seq // rows, rows // CHUNK
    n_hg = nh // HG_WIDE

    def body(p_ref, dy_ref, sts_ref, lb_ref, gn_ref, dp_ref, dlb_ref, dgn_ref, dst_ref, lbacc_ref, gnacc_ref):
        hg, b, s = pl.program_id(0), pl.program_id(1), pl.program_id(2)
        tri, triu = _tri(True), _tri(False)
        causal = _causal()
        gain = gn_ref[...]
        first = (b == 0) & (s == 0)
        cs = lambda j: slice(j * HEAD_DIM, (j + 1) * HEAD_DIM)

        def before():
            @pl.when((hg == 0) & first)
            def _():
                gnacc_ref[...] = jnp.zeros_like(gnacc_ref)

            @pl.when(first)
            def _():
                lbacc_ref[...] = jnp.zeros_like(lbacc_ref)

            @pl.when(s == 0)
            def _():
                dst_ref[...] = jnp.zeros_like(dst_ref)

        def after():
            @pl.when((b == nb - 1) & (s == ns - 1))
            def _():
                for j in range(HG_WIDE):
                    _, p0, p1 = _lower_bound(lb_ref[:, cs(j)])
                    acc = lbacc_ref[:, cs(j)]
                    dlb_ref[0:1, cs(j)] = -acc * p0 * p1
                    dlb_ref[1:2, cs(j)] = acc * p1 * (1.0 - p1)

            @pl.when((hg == n_hg - 1) & (b == nb - 1) & (s == ns - 1))
            def _():
                tot = gnacc_ref[:, 0:HEAD_DIM]
                for j in range(1, HG_WIDE):
                    tot = tot + gnacc_ref[:, cs(j)]
                dgn_ref[...] = tot

        before()

        units = [(n, j) for n in range(cpb) for j in range(HG_WIDE)]
        rs = lambda n: slice(n * CHUNK, (n + 1) * CHUNK)
        lbs = [_lower_bound(lb_ref[:, cs(j)])[0] for j in range(HG_WIDE)]
        gates, v_b, st_b, sc_b, o, do_b = {}, {}, {}, {}, {}, {}
        dq_out, dsc_b, dv, g_st, dq_in, dk_in, dst_at, dk_out, ddec = {}, {}, {}, {}, {}, {}, {}, {}, {}
        for n, j in units:
            gates[n, j] = _hgrn_gates(p_ref[0, rs(n), cs(j)], p_ref[1, rs(n), cs(j)], lbs[j], tri)
            v_b[n, j] = p_ref[2, rs(n), cs(j)].astype(BF16)
            st_b[n, j] = sts_ref[n, :, cs(j)].astype(BF16)
        for u in units:
            t = gates[u]
            sc_b[u] = jnp.where(causal, _dot(t["q_in"].astype(BF16), t["k_in"].astype(BF16), NT), 0.0).astype(BF16)
        for u in units:
            o[u] = _dot(sc_b[u], v_b[u], NN) + _dot(gates[u]["q_out"].astype(BF16), st_b[u], NT)
        for n, j in units:
            ov = o[n, j]
            r = lax.rsqrt(jnp.mean(ov * ov, axis=-1, keepdims=True) + EPS)
            ohat = ov * r
            sg, dsg = _silu_and_grad(p_ref[3, rs(n), cs(j)])
            dyv = dy_ref[rs(n), cs(j)].astype(F32)
            dp_ref[3, rs(n), cs(j)] = (dyv * (ohat * gain) * dsg).astype(BF16)
            d_on = dyv * sg
            gnacc_ref[:, cs(j)] = gnacc_ref[:, cs(j)] + jnp.sum(d_on * ohat, axis=0, keepdims=True)
            dohat = d_on * gain
            do_b[n, j] = (r * (dohat - ohat * jnp.mean(dohat * ohat, axis=-1, keepdims=True))).astype(BF16)
        for u in units:
            dq_out[u] = _dot(do_b[u], st_b[u], NN)
            dsc_b[u] = jnp.where(causal, _dot(do_b[u], v_b[u], NT), 0.0).astype(BF16)
            dv[u] = _dot(sc_b[u], do_b[u], TN)
            g_st[u] = _dot(do_b[u], gates[u]["q_out"].astype(BF16), TN)
        for u in units:
            dq_in[u] = _dot(dsc_b[u], gates[u]["k_in"].astype(BF16), NN)
            dk_in[u] = _dot(dsc_b[u], gates[u]["q_in"].astype(BF16), TN)
        for j in range(HG_WIDE):
            dst = dst_ref[j]
            for n in reversed(range(cpb)):
                dst_at[n, j] = dst
                dst = dst * gates[n, j]["dec"] + g_st[n, j]
            dst_ref[j] = dst
        for n, j in units:
            dst = dst_at[n, j]
            dst_b = dst.astype(BF16)
            dk_out[n, j] = _dot(v_b[n, j], dst_b, NN)
            dv[n, j] = dv[n, j] + _dot(gates[n, j]["k_out"].astype(BF16), dst_b, NT)
            ddec[n, j] = jnp.sum(dst * sts_ref[n, :, cs(j)], axis=0, keepdims=True)
        for n, j in units:
            t = gates[n, j]
            dp_ref[2, rs(n), cs(j)] = dv[n, j].astype(BF16)
            dq = dq_in[n, j] * t["e1"] + dq_out[n, j] * t["e3"]
            dk = dk_in[n, j] * t["e2"] + dk_out[n, j] * t["e4"]
            w_in = dq_in[n, j] * t["q_in"] - dk_in[n, j] * t["k_in"]
            w_out = dk_out[n, j] * t["k_out"]
            da = w_in + dq_out[n, j] * t["q_out"] - w_out
            da_mid = -jnp.sum(w_in, axis=0, keepdims=True)
            da_last = jnp.sum(w_out, axis=0, keepdims=True) + ddec[n, j] * t["dec"]
            rid = lax.broadcasted_iota(jnp.int32, da.shape, 0)
            da = da + jnp.where(rid == CHUNK // 2 - 1, da_mid, 0.0) + jnp.where(rid == CHUNK - 1, da_last, 0.0)
            dlf = _dot(triu, da, NN, precision=lax.Precision.HIGHEST)
            df = dlf / t["f"] - dk
            sgm = t["sgm"]
            dp_ref[1, rs(n), cs(j)] = (df * (1.0 - lbs[j]) * sgm * (1.0 - sgm)).astype(BF16)
            lbacc_ref[:, cs(j)] = lbacc_ref[:, cs(j)] + jnp.sum(df * (1.0 - sgm), axis=0, keepdims=True)
            dp_ref[0, rs(n), cs(j)] = (dq * t["dq"]).astype(BF16)

        after()

    blk = lambda hg, b, s: b * ns + (ns - 1 - s)
    return _call(
        body, name=name, grid=(n_hg, nb, ns),
        out_shape=[jax.ShapeDtypeStruct((4, m_rows, di), BF16), jax.ShapeDtypeStruct((2, di), F32),
                   jax.ShapeDtypeStruct((1, HEAD_DIM), F32)],
        in_specs=[pl.BlockSpec((4, rows, wide), lambda hg, b, s: (0, blk(hg, b, s), hg)),
                  pl.BlockSpec((rows, wide), lambda hg, b, s: (blk(hg, b, s), hg)),
                  pl.BlockSpec((cpb, HEAD_DIM, wide), lambda hg, b, s: (blk(hg, b, s), 0, hg)),
                  pl.BlockSpec((2, wide), lambda hg, b, s: (0, hg)),
                  pl.BlockSpec((1, HEAD_DIM), lambda hg, b, s: (0, 0))],
        out_specs=[pl.BlockSpec((4, rows, wide), lambda hg, b, s: (0, blk(hg, b, s), hg)),
                   pl.BlockSpec((2, wide), lambda hg, b, s: (0, hg)),
                   pl.BlockSpec((1, HEAD_DIM), lambda hg, b, s: (0, 0))],
        scratch_shapes=[pltpu.VMEM((HG_WIDE, HEAD_DIM, HEAD_DIM), F32), pltpu.VMEM((1, wide), F32),
                        pltpu.VMEM((1, wide), F32)],
        compiler_params=_params(),
    )(proj4, dy, sts, lbraw, gn)


def outproj_loss(y, w, x, mod, fg, target, seq, name):
    m_rows, di = y.shape
    d = w.shape[1]
    tm = min(512, seq)

    def body(y_ref, w_ref, x_ref, mod_ref, fg_ref, t_ref, out_ref, loss_ref, dx_ref, dfg_ref):
        i = pl.program_id(0)
        acc = _dot(y_ref[...], w_ref[...], NN)
        out_ref[...] = acc.astype(BF16)
        xv = x_ref[...] + mod_ref[0][:, 2 * d:] * acc
        gain = fg_ref[...]
        r = lax.rsqrt(jnp.mean(xv * xv, axis=-1, keepdims=True) + EPS)
        xn = xv * r
        e = xn * gain - t_ref[...]
        part = 0.5 * jnp.sum(jnp.mean(e * e, axis=-1, keepdims=True), axis=0, keepdims=True)
        dyv = e / d
        p_fg = jnp.sum(dyv * xn, axis=0, keepdims=True)
        dxn = dyv * gain
        dx_ref[...] = r * (dxn - xn * jnp.mean(dxn * xn, axis=-1, keepdims=True))

        @pl.when(i == 0)
        def _():
            loss_ref[...] = part
            dfg_ref[...] = p_fg

        @pl.when(i != 0)
        def _():
            loss_ref[...] = loss_ref[...] + part
            dfg_ref[...] = dfg_ref[...] + p_fg

    row = pl.BlockSpec((tm, d), lambda i: (i, 0))
    return _call(
        body, name=name, grid=(m_rows // tm,),
        out_shape=[jax.ShapeDtypeStruct((m_rows, d), BF16), jax.ShapeDtypeStruct((1, 1), F32),
                   jax.ShapeDtypeStruct((m_rows, d), F32), jax.ShapeDtypeStruct((1, d), F32)],
        in_specs=[pl.BlockSpec((tm, di), lambda i: (i, 0)),
                  pl.BlockSpec((di, d), lambda i: (0, 0)),
                  row,
                  pl.BlockSpec((1, 1, 3 * d), lambda i: ((i * tm) // seq, 0, 0)),
                  pl.BlockSpec((1, d), lambda i: (0, 0)), row],
        out_specs=[row, pl.BlockSpec((1, 1), lambda i: (0, 0)), row, pl.BlockSpec((1, d), lambda i: (0, 0))],
        compiler_params=_params(),
    )(y, w, x, mod, fg, target)


def _pack(parts):
    flat = jnp.concatenate([p.reshape(-1) for p in parts])
    pad = (-flat.shape[0]) % (8 * LANES)
    return jnp.pad(flat, (0, pad)).reshape(-1, LANES)


def kernel(x, c, norm_gain, w_ada, b_ada, a_w_in, a_ln_gain, a_ln_bias, a_w_s, a_b_s, a_w_out, b_w_in, b_lower_bounds, b_gn_gain, b_w_out, final_gain, loss_target, m_norm_gain, m_w_ada, m_b_ada, m_a_w_in, m_a_ln_gain, m_a_ln_bias, m_a_w_s, m_a_b_s, m_a_w_out, m_b_w_in, m_b_lower_bounds, m_b_gn_gain, m_b_w_out, m_final_gain, v_norm_gain, v_w_ada, v_b_ada, v_a_w_in, v_a_ln_gain, v_a_ln_bias, v_a_w_s, v_a_b_s, v_a_w_out, v_b_w_in, v_b_lower_bounds, v_b_gn_gain, v_b_w_out, v_final_gain):
    nb, seq, d = x.shape
    m_rows = nb * seq
    n_l = w_ada.shape[0]
    ada_cols = w_ada.shape[2]
    px, py, pc = _place()
    chip = 2 * px + py
    dev = 2 * chip + pc

    c_all = allgather_small(c.reshape(-1, LANES), "gather_c").reshape(N_DEV * nb, d)
    b_cols = lax.dynamic_slice_in_dim(b_ada, chip * ada_cols, ada_cols, axis=1).reshape(n_l, 1, ada_cols)
    mod_cols = ada_fwd(c_all, w_ada, b_cols, "ada_fwd")
    mod_g = allgather_small(mod_cols.reshape(-1, LANES), "gather_mod")
    mod_g = mod_g.reshape(N_CHIPS, 2, n_l, N_DEV * nb, ada_cols)[:, 0]
    mod_all = jnp.transpose(mod_g, (1, 2, 0, 3)).reshape(n_l, N_DEV * nb, 3 * d)
    mod_mine = lax.dynamic_slice_in_dim(mod_all, dev * nb, nb, axis=1)
    mod0 = mod_mine[0].reshape(nb, 1, 3 * d)
    mod1 = mod_mine[1].reshape(nb, 1, 3 * d)

    (wa_in, wa_out), tok_a = gather_inplace(
        [cast_into_slot(a_w_in[0], chip, mod_mine, "cast_a_in"), cast_into_slot(a_w_out[0], chip, mod_mine, "cast_a_out")],
        "gather_a")
    s_bi = gather_start(cast_into_slot(b_w_in[0], chip, tok_a, "cast_b_in"), "gather_b_in_start")
    s_bo = gather_start(cast_into_slot(b_w_out[0], chip, s_bi[3], "cast_b_out"), "gather_b_out_start")
    di = a_w_out.shape[1] * N_CHIPS
    wa_out = wa_out.reshape(di, d)

    x0 = x.reshape(m_rows, d)
    tgt = loss_target.reshape(m_rows, d)
    ng0 = norm_gain[0:1] + (s_bi[3][0, 0] + s_bo[3][0, 0])
    ng1 = norm_gain[1:2]
    bs_col = a_b_s[0].reshape(SG_GROUPS, SG_BLOCK, 1)
    proj_a, h_a = inproj_fwd(x0, mod0, ng0, wa_in, seq, False, "a_inproj")
    y_a = sgu_fwd(proj_a, a_ln_gain, a_ln_bias, a_w_s[0], bs_col, "a_sgu")
    x1, out_a = outproj_fwd(y_a, wa_out, x0, mod0, seq, "a_outproj")
    wb_in = gather_wait(*s_bi[:3], out_a, "gather_b_in_wait")
    proj_b, h_b = inproj_fwd(x1, mod1, ng1, wb_in, seq, True, "b_inproj")
    y_b, sts_b = hgrn_fwd(proj_b, b_lower_bounds, b_gn_gain, seq, "b_hgrn")
    wb_out = gather_wait(*s_bo[:3], y_b, "gather_b_out_wait").reshape(di, d)
    out_b, loss_part, dx2, dfg = outproj_loss(
        y_b, wb_out, x1, mod1, final_gain.reshape(1, d), tgt, seq, "b_outproj_loss")

    shard_rows = di // N_CHIPS
    dy_b, dout_b, dgate1 = outproj_bwd(dx2, out_b, mod1, wb_out, seq, "b_outproj_bwd")
    gwb_out = grad_w_out(y_b, dout_b, "b_grad_w_out").reshape(N_CHIPS, shard_rows, d)
    e_bo = exchange_start(gwb_out, "exchange_b_out_start")
    dproj_b, dlb, dgn = hgrn_bwd(
        proj_b, dy_b, sts_b, b_lower_bounds, b_gn_gain + e_bo[4][0, 0], seq, "b_hgrn_bwd")
    e_bi = exchange_start(grad_w_in(h_b, dproj_b, N_CHIPS, True, "b_grad_w_in"), "exchange_b_in_start")
    dx1, dshift1, dscale1, dng1 = inproj_bwd(
        dproj_b, wb_in, x1, dx2, mod1, ng1 + e_bi[4][0, 0], seq, True, "b_inproj_bwd")

    dy_a, dout_a, dgate0 = outproj_bwd(dx1, out_a, mod0, wa_out, seq, "a_outproj_bwd")
    gwa_out = grad_w_out(y_a, dout_a, "a_grad_w_out").reshape(N_CHIPS, shard_rows, d)
    e_ao = exchange_start(gwa_out, "exchange_a_out_start")
    dproj_a, dws, dbs, dlg, dlbias = sgu_bwd(
        proj_a, dy_a, a_ln_gain + e_ao[4][0, 0], a_ln_bias, a_w_s[0], bs_col, "a_sgu_bwd")
    e_ai = exchange_start(grad_w_in(h_a, dproj_a, N_CHIPS, False, "a_grad_w_in"), "exchange_a_in_start")
    dx0, dshift0, dscale0, dng0 = inproj_bwd(
        dproj_a, wa_in, x0, dx1, mod0, norm_gain[0:1] + e_ai[4][0, 0], seq, False, "a_inproj_bwd")
    grad_x = dx0.reshape(nb, seq, d)

    dmod = jnp.concatenate([dshift0, dscale0, dgate0, dshift1, dscale1, dgate1], axis=2)
    n_dmod = dmod.size
    small_g = [jnp.concatenate([dng0, dng1], axis=0), dlg, dlbias, dws, dbs, dlb, dfg, dgn]
    packed_g = _pack([dmod] + small_g + [loss_part])
    rows = packed_g.shape[0]
    s_small = gather_all_start(
        lax.dynamic_update_slice(jnp.zeros((N_DEV, rows, LANES), F32), packed_g[None], (dev, 0, 0)),
        "gather_small_start")

    def finish(group, after):
        mine = []
        for ex, _, _, _, nm in group:
            parts_thru, land = exchange_wait(ex[0], ex[1], ex[2], ex[3], after, "exchange_" + nm + "_wait")
            mine.append(sum_parts(parts_thru, land, chip, "sum_" + nm))
            after = mine[-1]
        theirs = swap_sibling(mine, "swap_" + group[0][4])
        return [[r.reshape(w.shape) for r in adamw_pair(pa, pb, w[0], m[0], v[0], "adamw_" + nm)]
                for pa, pb, (_, w, m, v, nm) in zip(mine, theirs, group)]

    (gb_out, db_out, mb_out, vb_out), (gb_in, db_in, mb_in, vb_in), (ga_out, da_out, ma_out, va_out) = finish(
        [(e_bo, b_w_out, m_b_w_out, v_b_w_out, "b_out"), (e_bi, b_w_in, m_b_w_in, v_b_w_in, "b_in"),
         (e_ao, a_w_out, m_a_w_out, v_a_w_out, "a_out")], s_small[3])
    ((ga_in, da_in, ma_in, va_in),) = finish([(e_ai, a_w_in, m_a_w_in, v_a_w_in, "a_in")], ga_out)

    small_w = [norm_gain, a_ln_gain, a_ln_bias, a_w_s, a_b_s, b_lower_bounds, final_gain, b_gn_gain]
    small_m = [m_norm_gain, m_a_ln_gain, m_a_ln_bias, m_a_w_s, m_a_b_s, m_b_lower_bounds, m_final_gain, m_b_gn_gain]
    small_v = [v_norm_gain, v_a_ln_gain, v_a_ln_bias, v_a_w_s, v_a_b_s, v_b_lower_bounds, v_final_gain, v_b_gn_gain]
    rows_of = lambda a: a.reshape(-1, a.shape[-1])
    gathered = gather_all_wait(s_small[0], s_small[1], s_small[2], ga_in, "gather_small_wait")
    tail, small_res = small_update(
        gathered, n_dmod // LANES, [rows_of(a) for a in small_w], [rows_of(a) for a in small_m],
        [rows_of(a) for a in small_v], "small_update")
    loss = tail[0, 0]
    sg, sd, sm, sv = [[small_res[p][kind].reshape(w.shape) for p, w in enumerate(small_w)] for kind in range(4)]

    dmod_all = gathered[:, :n_dmod // LANES].reshape(N_DEV * nb, n_l, 3 * d)
    dmod_cols = lax.dynamic_slice_in_dim(dmod_all, chip * ada_cols, ada_cols, axis=2)
    dmod_cols = jnp.transpose(dmod_cols, (1, 0, 2))
    g_wada, d_wada, m_wada, v_wada = ada_bwd(c_all, dmod_cols, w_ada, m_w_ada, v_w_ada, "ada_bwd")
    flat = lambda a: a.reshape(1, -1)
    g_bada, d_bada, m_bada, v_bada = [
        r.reshape(b_ada.shape) for r in
        bias_update(dmod_all.reshape(N_DEV * nb, n_l * 3 * d), flat(b_ada), flat(m_b_ada), flat(v_b_ada), "bias_update")]

    def order(ng, wada, bada, ain, sm_rest, aout, bin_, bout):
        lg, lbi, ws_, bs_, lbd, fg_, gn_ = sm_rest
        return [ng, wada, bada, ain, lg, lbi, ws_, bs_, aout, bin_, lbd, gn_, bout, fg_]

    grads = order(sg[0], g_wada, g_bada, ga_in, sg[1:8], ga_out, gb_in, gb_out)
    deltas = order(sd[0], d_wada, d_bada, da_in, sd[1:8], da_out, db_in, db_out)
    new_m = order(sm[0], m_wada, m_bada, ma_in, sm[1:8], ma_out, mb_in, mb_out)
    new_v = order(sv[0], v_wada, v_bada, va_in, sv[1:8], va_out, vb_in, vb_out)
    return (loss, grad_x, *grads, *deltas, *new_m, *new_v)
```

```python
import math

import jax
import jax.numpy as jnp
from jax import lax
from jax.experimental import pallas as pl
from jax.experimental.pallas import tpu as pltpu

F32 = jnp.float32
BF16 = jnp.bfloat16
EPS = 1e-6
CHUNK = 64
SG_BLOCK = 128
SG_GROUPS = 8
HEAD_DIM = 128
HG_WIDE = 8
HG_ROWS = 256
N_CHIPS = 4
N_DEV = 8
GATHER_PIECES = 4
LANES = 128
ADAM_LR = 0.001
ADAM_B1 = 0.9
ADAM_B2 = 0.999
ADAM_EPS = 1e-08
ADAM_WD = 0.01
ADAM_STEP = 10
GELU_C0 = 0.7978845608028654
GELU_C1 = 0.044715
MESH = pl.DeviceIdType.MESH
VMEM_LIMIT = 56 * 1024 * 1024


ROW_TILE = 1024


def _col_tile(n):
    return next(t for t in (1024, 768, 512, 256) if n % t == 0)


def _run(body, **kw):
    return pl.pallas_call(body, **kw)


def _call(body, **kw):
    if _TRACE["defer"]:
        return _Deferred(body, kw)
    return _run(body, **kw)


_TRACE = {"defer": False, "ids": None, "phase": None}


def _pid(axis):
    return pl.program_id(axis) if _TRACE["ids"] is None else _TRACE["ids"][axis]


def _phase(k):
    return _TRACE["phase"] is None or _TRACE["phase"] == k


class _Deferred:
    def __init__(self, body, kw):
        self.body, self.kw, self.args = body, kw, None

    def __call__(self, *args):
        self.args = args
        return self


def _deferred(build, *args, **kw):
    _TRACE["defer"] = True
    try:
        return build(*args, **kw)
    finally:
        _TRACE["defer"] = False


def _as_list(v):
    return list(v) if isinstance(v, (list, tuple)) else [v]


def fused(parts, name):
    grids = [tuple(p.kw["grid"]) for p in parts]
    steps = [math.prod(g) for g in grids]
    assert len(set(steps)) == 1, steps

    def unravel(t, grid):
        ids, stride = [], 1
        for g in reversed(grid):
            ids.append((t // stride) % g)
            stride *= g
        return tuple(reversed(ids))

    def respec(spec, grid):
        if spec.index_map is None:
            return spec
        return pl.BlockSpec(spec.block_shape, lambda t, m=spec.index_map, g=grid: m(*unravel(t, g)))

    ins = [[respec(s, g) for s in _as_list(p.kw["in_specs"])] for p, g in zip(parts, grids)]
    outs = [[respec(s, g) for s in _as_list(p.kw["out_specs"])] for p, g in zip(parts, grids)]
    shapes = [_as_list(p.kw["out_shape"]) for p in parts]
    scratch = [_as_list(p.kw.get("scratch_shapes", [])) for p in parts]
    n_in, n_out, n_scr = [len(v) for v in ins], [len(v) for v in outs], [len(v) for v in scratch]

    def body(*refs):
        t = pl.program_id(0)
        for phase in range(3):
            off_in, off_out, off_scr = 0, sum(n_in), sum(n_in) + sum(n_out)
            for k, p in enumerate(parts):
                mine = (refs[off_in:off_in + n_in[k]] + refs[off_out:off_out + n_out[k]]
                        + refs[off_scr:off_scr + n_scr[k]])
                off_in, off_out, off_scr = off_in + n_in[k], off_out + n_out[k], off_scr + n_scr[k]
                _TRACE["ids"], _TRACE["phase"] = unravel(t, grids[k]), phase
                try:
                    p.body(*mine)
                finally:
                    _TRACE["ids"], _TRACE["phase"] = None, None

    res = _run(
        body, name=name, grid=(steps[0],),
        in_specs=sum(ins, []), out_specs=sum(outs, []), out_shape=sum(shapes, []),
        scratch_shapes=sum(scratch, []), compiler_params=_params(),
    )(*sum([list(p.args) for p in parts], []))
    split, off = [], 0
    for k in range(len(parts)):
        split.append(list(res[off:off + n_out[k]]))
        off += n_out[k]
    return split


def _params(**kw):
    return pltpu.CompilerParams(vmem_limit_bytes=VMEM_LIMIT, **kw)


def _sigmoid(x):
    return 0.5 * jnp.tanh(0.5 * x) + 0.5


def _sigmoid_small(x):
    return 1.0 / (1.0 + jnp.exp(-x))


def _silu_and_grad(x):
    s = _sigmoid(x)
    return x * s, s * (1.0 + x * (1.0 - s))


def _gelu(x):
    return 0.5 * x * (1.0 + jnp.tanh(GELU_C0 * (x + GELU_C1 * x * x * x)))


def _gelu_and_grad(x):
    t = jnp.tanh(GELU_C0 * (x + GELU_C1 * x * x * x))
    g = 0.5 * x * (1.0 + t)
    dg = 0.5 * (1.0 + t) + 0.5 * x * (1.0 - t * t) * (GELU_C0 * (1.0 + 3.0 * GELU_C1 * x * x))
    return g, dg


def _dot(a, b, dims, precision=None):
    return lax.dot_general(a, b, (dims, ((), ())), precision=precision, preferred_element_type=F32)


NN = ((1,), (0,))
NT = ((1,), (1,))
TN = ((0,), (0,))


def _adamw(w, g, m, v):
    m = ADAM_B1 * m + (1.0 - ADAM_B1) * g
    v = ADAM_B2 * v + (1.0 - ADAM_B2) * (g * g)
    m_hat = m / (1.0 - ADAM_B1 ** ADAM_STEP)
    v_hat = v / (1.0 - ADAM_B2 ** ADAM_STEP)
    delta = -ADAM_LR * (m_hat / (jnp.sqrt(v_hat) + ADAM_EPS) + ADAM_WD * w)
    return delta, m, v


def _chunk_mask():
    r = lax.broadcasted_iota(jnp.int32, (SG_BLOCK, SG_BLOCK), 0)
    c = lax.broadcasted_iota(jnp.int32, (SG_BLOCK, SG_BLOCK), 1)
    return (c // CHUNK) <= (r // CHUNK)


def _place():
    return lax.axis_index("x"), lax.axis_index("y"), lax.axis_index("c")


def _other_chips(x, y):
    return [(1 - x, y), (x, 1 - y), (1 - x, 1 - y)]


def allgather_small(v, name):
    m_per, n = v.shape

    def body(x_ref, out_ref, send_sems, recv_sems, local_sem):
        x, y, c = _place()
        me, sibling = (x, y, c), (x, y, 1 - c)
        chips = _other_chips(x, y)

        def rows(px, py, pc):
            return out_ref.at[pl.ds((4 * px + 2 * py + pc) * m_per, m_per), :]

        def copy(k, block, to, src=None):
            return pltpu.make_async_remote_copy(
                src_ref=rows(*block) if src is None else src, dst_ref=rows(*block),
                send_sem=send_sems.at[k], recv_sem=recv_sems.at[k], device_id=to, device_id_type=MESH)

        mine = pltpu.make_async_copy(x_ref, rows(*me), local_sem)
        mine.start()
        first = [copy(0, me, sibling, src=x_ref)]
        first += [copy(1 + j, me, (*chip, c), src=x_ref) for j, chip in enumerate(chips)]
        for cp in first:
            cp.start()
        passed = [copy(4 + j, (*chip, c), sibling) for j, chip in enumerate(chips)]
        for j, chip in enumerate(chips):
            copy(1 + j, (*chip, c), me).wait_recv()
            passed[j].start()
        copy(0, sibling, me).wait_recv()
        for j, chip in enumerate(chips):
            copy(4 + j, (*chip, 1 - c), me).wait_recv()
        for cp in first + passed:
            cp.wait_send()
        mine.wait()

    return _call(
        body, name=name,
        out_shape=jax.ShapeDtypeStruct((N_DEV * m_per, n), v.dtype),
        in_specs=[pl.BlockSpec(memory_space=pltpu.VMEM)],
        out_specs=pl.BlockSpec(memory_space=pltpu.VMEM),
        scratch_shapes=[pltpu.SemaphoreType.DMA((7,)), pltpu.SemaphoreType.DMA((7,)), pltpu.SemaphoreType.DMA],
    )(v)


def _hbm_spec():
    return pl.BlockSpec(memory_space=pltpu.HBM)


def _sem_spec():
    return pl.BlockSpec(memory_space=pltpu.SEMAPHORE)


def _split_params():
    return pltpu.CompilerParams(has_side_effects=pltpu.SideEffectType.DATAFLOW_SIDE_EFFECTING)


def _hbm(a):
    return pltpu.with_memory_space_constraint(a, pltpu.HBM)


def gather_inplace(lands, name):
    n = len(lands)
    per_land = 6 * GATHER_PIECES

    def body(*refs):
        land_refs, token = refs[n:2 * n], refs[2 * n]
        send_sems, recv_sems = refs[2 * n + 1:]
        x, y, c = _place()
        chips = _other_chips(x, y)

        def copy(w, k, q, chip_idx, core_half, to):
            piece = lands[w].shape[1] // (2 * GATHER_PIECES)
            rows = land_refs[w].at[chip_idx, pl.ds((core_half * GATHER_PIECES + q) * piece, piece), :]
            sem = per_land * w + GATHER_PIECES * k + q
            return pltpu.make_async_remote_copy(
                src_ref=rows, dst_ref=rows, send_sem=send_sems.at[sem], recv_sem=recv_sems.at[sem],
                device_id=to, device_id_type=MESH)

        order = [(q, w, j, px, py) for q in range(GATHER_PIECES) for w in range(n) for j, (px, py) in enumerate(chips)]
        first = [copy(w, j, q, 2 * x + y, c, (px, py, c)) for q, w, j, px, py in order]
        for cp in first:
            cp.start()
        passed = []
        for q, w, j, px, py in order:
            copy(w, j, q, 2 * px + py, c, (px, py, c)).wait_recv()
            passed.append(copy(w, 3 + j, q, 2 * px + py, c, (x, y, 1 - c)))
            passed[-1].start()
        for q, w, j, px, py in order:
            copy(w, 3 + j, q, 2 * px + py, 1 - c, (x, y, 1 - c)).wait_recv()
        for cp in first + passed:
            cp.wait_send()
        token[...] = jnp.zeros_like(token)

    res = _call(
        body, name=name,
        out_shape=[jax.ShapeDtypeStruct(a.shape, a.dtype) for a in lands] + [jax.ShapeDtypeStruct((8, LANES), F32)],
        in_specs=[_hbm_spec()] * n, out_specs=[_hbm_spec()] * n + [pl.BlockSpec(memory_space=pltpu.VMEM)],
        input_output_aliases={w: w for w in range(n)},
        scratch_shapes=[pltpu.SemaphoreType.DMA((per_land * n,)), pltpu.SemaphoreType.DMA((per_land * n,))],
    )(*lands)
    return res[:n], res[n]


def gather_start(land, name):
    def body(land_ref, send_sems, recv_sems, land_thru, token):
        del land_thru
        x, y, c = _place()
        for j, (px, py) in enumerate(_other_chips(x, y)):
            pltpu.make_async_remote_copy(
                src_ref=land_ref.at[2 * x + y], dst_ref=land_ref.at[2 * x + y],
                send_sem=send_sems.at[j], recv_sem=recv_sems.at[j], device_id=(px, py, c),
                device_id_type=MESH).start()
        token[...] = jnp.zeros_like(token)

    return _call(
        body, name=name,
        out_shape=(pltpu.SemaphoreType.DMA((3,)), pltpu.SemaphoreType.DMA((3,)),
                   pltpu.HBM(land.shape, land.dtype), jax.ShapeDtypeStruct((8, LANES), F32)),
        in_specs=(_hbm_spec(),),
        out_specs=(_sem_spec(), _sem_spec(), _hbm_spec(), pl.BlockSpec(memory_space=pltpu.VMEM)),
        input_output_aliases={0: 2}, compiler_params=_split_params(),
    )(_hbm(land))


def gather_wait(send_sems, recv_sems, land, after, name):
    def body(land_ref, send_sems, recv_sems, after_ref, land_out):
        del after_ref, land_out
        x, y, c = _place()
        for j, (px, py) in enumerate(_other_chips(x, y)):
            cp = pltpu.make_async_remote_copy(
                src_ref=land_ref.at[2 * x + y], dst_ref=land_ref.at[2 * px + py],
                send_sem=send_sems.at[j], recv_sem=recv_sems.at[j], device_id=(px, py, c), device_id_type=MESH)
            cp.wait_send()
            cp.wait_recv()

    return _call(
        body, name=name,
        out_shape=pltpu.HBM(land.shape, land.dtype),
        in_specs=(_hbm_spec(), _sem_spec(), _sem_spec(), pl.BlockSpec(memory_space=pl.ANY)),
        out_specs=_hbm_spec(), input_output_aliases={0: 0}, compiler_params=_split_params(),
    )(land, send_sems, recv_sems, after)


def _flips():
    return [(fx, fy, fc) for fx in (0, 1) for fy in (0, 1) for fc in (0, 1) if (fx, fy, fc) != (0, 0, 0)]


def _flipped(x, y, c, flip):
    fx, fy, fc = flip
    return (1 - x if fx else x, 1 - y if fy else y, 1 - c if fc else c)


def gather_all_start(land, name):
    def body(land_ref, send_sems, recv_sems, land_thru, token):
        del land_thru
        x, y, c = _place()
        for k, flip in enumerate(_flips()):
            pltpu.make_async_remote_copy(
                src_ref=land_ref.at[4 * x + 2 * y + c], dst_ref=land_ref.at[4 * x + 2 * y + c],
                send_sem=send_sems.at[k], recv_sem=recv_sems.at[k], device_id=_flipped(x, y, c, flip),
                device_id_type=MESH).start()
        token[...] = jnp.zeros_like(token)

    return _call(
        body, name=name,
        out_shape=(pltpu.SemaphoreType.DMA((7,)), pltpu.SemaphoreType.DMA((7,)),
                   pltpu.HBM(land.shape, land.dtype), jax.ShapeDtypeStruct((8, LANES), F32)),
        in_specs=(_hbm_spec(),),
        out_specs=(_sem_spec(), _sem_spec(), _hbm_spec(), pl.BlockSpec(memory_space=pltpu.VMEM)),
        input_output_aliases={0: 2}, compiler_params=_split_params(),
    )(_hbm(land))


def gather_all_wait(send_sems, recv_sems, land, after, name):
    def body(land_ref, send_sems, recv_sems, after_ref, land_out):
        del after_ref, land_out
        x, y, c = _place()
        for k, flip in enumerate(_flips()):
            px, py, pc = _flipped(x, y, c, flip)
            cp = pltpu.make_async_remote_copy(
                src_ref=land_ref.at[4 * x + 2 * y + c], dst_ref=land_ref.at[4 * px + 2 * py + pc],
                send_sem=send_sems.at[k], recv_sem=recv_sems.at[k], device_id=(px, py, pc), device_id_type=MESH)
            cp.wait_send()
            cp.wait_recv()

    return _call(
        body, name=name,
        out_shape=pltpu.HBM(land.shape, land.dtype),
        in_specs=(_hbm_spec(), _sem_spec(), _sem_spec(), pl.BlockSpec(memory_space=pl.ANY)),
        out_specs=_hbm_spec(), input_output_aliases={0: 0}, compiler_params=_split_params(),
    )(land, send_sems, recv_sems, after)


def exchange_start(parts, name):
    _, r, c_ = parts.shape

    def body(parts_ref, land_ref, send_sems, recv_sems, parts_thru, land_thru, token):
        del parts_thru, land_thru
        x, y, c = _place()
        for j, (px, py) in enumerate(_other_chips(x, y)):
            pltpu.make_async_remote_copy(
                src_ref=parts_ref.at[2 * px + py], dst_ref=land_ref.at[j],
                send_sem=send_sems.at[j], recv_sem=recv_sems.at[j], device_id=(px, py, c),
                device_id_type=MESH).start()
        token[...] = jnp.zeros_like(token)

    return _call(
        body, name=name,
        out_shape=(pltpu.SemaphoreType.DMA((3,)), pltpu.SemaphoreType.DMA((3,)),
                   pltpu.HBM(parts.shape, parts.dtype), pltpu.HBM((3, r, c_), parts.dtype),
                   jax.ShapeDtypeStruct((8, LANES), F32)),
        in_specs=(_hbm_spec(), _hbm_spec()),
        out_specs=(_sem_spec(), _sem_spec(), _hbm_spec(), _hbm_spec(), pl.BlockSpec(memory_space=pltpu.VMEM)),
        input_output_aliases={0: 2, 1: 3}, compiler_params=_split_params(),
    )(_hbm(parts), _hbm(lax.empty((3, r, c_), parts.dtype)))


def exchange_wait(send_sems, recv_sems, parts, land, after, name):
    def body(parts_ref, land_ref, send_sems, recv_sems, after_ref, parts_out, land_out):
        del after_ref, parts_out, land_out
        x, y, c = _place()
        for j, (px, py) in enumerate(_other_chips(x, y)):
            cp = pltpu.make_async_remote_copy(
                src_ref=parts_ref.at[2 * px + py], dst_ref=land_ref.at[j],
                send_sem=send_sems.at[j], recv_sem=recv_sems.at[j], device_id=(px, py, c), device_id_type=MESH)
            cp.wait_send()
            cp.wait_recv()

    return _call(
        body, name=name,
        out_shape=(pltpu.HBM(parts.shape, parts.dtype), pltpu.HBM(land.shape, land.dtype)),
        in_specs=(_hbm_spec(), _hbm_spec(), _sem_spec(), _sem_spec(), pl.BlockSpec(memory_space=pl.ANY)),
        out_specs=(_hbm_spec(), _hbm_spec()), input_output_aliases={0: 0, 1: 1},
        compiler_params=_split_params(),
    )(parts, land, send_sems, recv_sems, after)


def cast_into_slot(w, chip, after, name):
    r, c = w.shape
    tr = min(256, r)

    def body(s_ref, w_ref, after_ref, o_ref):
        del s_ref, after_ref
        o_ref[...] = w_ref[...].astype(BF16)

    return _call(
        body, name=name,
        grid_spec=pltpu.PrefetchScalarGridSpec(
            num_scalar_prefetch=1, grid=(r // tr,),
            in_specs=[pl.BlockSpec((tr, c), lambda i, s: (i, 0)), pl.BlockSpec(memory_space=pl.ANY)],
            out_specs=pl.BlockSpec((None, tr, c), lambda i, s: (s[0], i, 0))),
        out_shape=jax.ShapeDtypeStruct((N_CHIPS, r, c), BF16),
        compiler_params=_params(),
    )(chip.reshape(1).astype(jnp.int32), w, after)


def sum_parts(parts, land, chip, name):
    _, r, c = parts.shape
    tr = min(256, r)

    def body(s_ref, p_ref, l_ref, o_ref):
        del s_ref
        acc = p_ref[...].astype(F32) + l_ref[0].astype(F32)
        acc = acc + l_ref[1].astype(F32)
        o_ref[...] = (acc + l_ref[2].astype(F32)).astype(BF16)

    return _call(
        body, name=name,
        grid_spec=pltpu.PrefetchScalarGridSpec(
            num_scalar_prefetch=1, grid=(r // tr,),
            in_specs=[pl.BlockSpec((None, tr, c), lambda i, s: (s[0], i, 0)),
                      pl.BlockSpec((3, tr, c), lambda i, s: (0, i, 0))],
            out_specs=pl.BlockSpec((tr, c), lambda i, s: (i, 0))),
        out_shape=jax.ShapeDtypeStruct((r, c), BF16),
        compiler_params=_params(),
    )(chip.reshape(1).astype(jnp.int32), parts, land)


def swap_sibling(arrs, name):
    n = len(arrs)

    def body(*refs):
        ins, outs = refs[:n], refs[n:2 * n]
        send_sems, recv_sems = refs[2 * n:]
        x, y, c = _place()
        cps = []
        for w in range(n):
            cp = pltpu.make_async_remote_copy(
                src_ref=ins[w], dst_ref=outs[w], send_sem=send_sems.at[w], recv_sem=recv_sems.at[w],
                device_id=(x, y, 1 - c), device_id_type=MESH)
            cp.start()
            cps.append(cp)
        for cp in cps:
            cp.wait_recv()
        for cp in cps:
            cp.wait_send()

    return _call(
        body, name=name,
        out_shape=[jax.ShapeDtypeStruct(a.shape, a.dtype) for a in arrs],
        in_specs=[_hbm_spec()] * n, out_specs=[_hbm_spec()] * n,
        scratch_shapes=[pltpu.SemaphoreType.DMA((n,)), pltpu.SemaphoreType.DMA((n,))],
    )(*arrs)


def adamw_pair(pa, pb, w, m, v, name, steps=None):
    r, c = w.shape
    tr = min(128, r) if steps is None else r // steps

    def body(pa_ref, pb_ref, w_ref, m_ref, v_ref, g_ref, d_ref, nm_ref, nv_ref):
        if not _phase(1):
            return
        g = pa_ref[...].astype(F32) + pb_ref[...].astype(F32)
        d, nm, nv = _adamw(w_ref[...], g, m_ref[...], v_ref[...])
        g_ref[...] = g
        d_ref[...] = d
        nm_ref[...] = nm
        nv_ref[...] = nv

    spec = pl.BlockSpec((tr, c), lambda i: (i, 0))
    return _call(
        body, name=name, grid=(r // tr,),
        out_shape=[jax.ShapeDtypeStruct((r, c), F32)] * 4,
        in_specs=[spec] * 5, out_specs=[spec] * 4,
        compiler_params=_params(),
    )(pa, pb, w, m, v)


def small_update(gathered, first_row, ws, ms, vs, name):
    n_w = len(ws)
    total_rows = gathered.shape[1]

    def body(*refs):
        g_ref = refs[0]
        w_refs, m_refs, v_refs = refs[1:1 + n_w], refs[1 + n_w:1 + 2 * n_w], refs[1 + 2 * n_w:1 + 3 * n_w]
        tail_ref = refs[1 + 3 * n_w]
        outs = refs[2 + 3 * n_w:2 + 7 * n_w]
        sum_ref = refs[2 + 7 * n_w]
        acc = g_ref[0]
        for k in range(1, N_DEV):
            acc = acc + g_ref[k]
        sum_ref[...] = acc
        row = first_row
        for p in range(n_w):
            a, b = ws[p].shape
            per = b // LANES
            g_out, d_out, m_out, v_out = outs[4 * p:4 * p + 4]
            if per == 1:
                g_out[...] = sum_ref[row:row + a, :]
            else:
                for i in range(a):
                    for jc in range(per):
                        g_out[i:i + 1, jc * LANES:(jc + 1) * LANES] = sum_ref[row + i * per + jc:row + i * per + jc + 1, :]
            row += a * per
            dl, nm, nv = _adamw(w_refs[p][...], g_out[...], m_refs[p][...], v_refs[p][...])
            d_out[...] = dl
            m_out[...] = nm
            v_out[...] = nv
        tail_ref[...] = sum_ref[row:row + 1, :]

    out_shape = [jax.ShapeDtypeStruct((1, LANES), F32)]
    for w in ws:
        out_shape += [jax.ShapeDtypeStruct(w.shape, F32)] * 4
    res = _call(
        body, name=name, out_shape=out_shape,
        scratch_shapes=[pltpu.VMEM((total_rows, LANES), F32)],
        compiler_params=_params(),
    )(gathered, *ws, *ms, *vs)
    return res[0], [res[1 + 4 * p:5 + 4 * p] for p in range(n_w)]


def ada_fwd(c_all, w_ada, b_cols, name):
    n_l, d, cols = w_ada.shape
    nb = c_all.shape[0]
    tn = 256

    def body(c_ref, w_ref, b_ref, o_ref):
        cv = c_ref[...]
        ca = (cv * _sigmoid(cv)).astype(BF16)
        o_ref[...] = _dot(ca, w_ref[...].astype(BF16), NN) + b_ref[...]

    return _call(
        body, name=name, grid=(n_l, cols // tn),
        out_shape=jax.ShapeDtypeStruct((n_l, nb, cols), F32),
        in_specs=[pl.BlockSpec((nb, d), lambda l, j: (0, 0)),
                  pl.BlockSpec((None, d, tn), lambda l, j: (l, 0, j)),
                  pl.BlockSpec((None, 1, tn), lambda l, j: (l, 0, j))],
        out_specs=pl.BlockSpec((None, nb, tn), lambda l, j: (l, 0, j)),
        compiler_params=_params(),
    )(c_all, w_ada, b_cols)


def ada_bwd(c_all, dmod_cols, w, m, v, name):
    n_l, d, cols = w.shape
    nb = c_all.shape[0]
    tn = 256

    def body(c_ref, dm_ref, w_ref, m_ref, v_ref, g_ref, d_ref, nm_ref, nv_ref):
        cv = c_ref[...]
        ca = (cv * _sigmoid(cv)).astype(BF16)
        g = _dot(ca, dm_ref[...].astype(BF16), TN)
        dl, nm, nv = _adamw(w_ref[...], g, m_ref[...], v_ref[...])
        g_ref[...] = g
        d_ref[...] = dl
        nm_ref[...] = nm
        nv_ref[...] = nv

    wspec = pl.BlockSpec((None, d, tn), lambda l, j: (l, 0, j))
    return _call(
        body, name=name, grid=(n_l, cols // tn),
        out_shape=[jax.ShapeDtypeStruct((n_l, d, cols), F32)] * 4,
        in_specs=[pl.BlockSpec((nb, d), lambda l, j: (0, 0)),
                  pl.BlockSpec((None, nb, tn), lambda l, j: (l, 0, j)),
                  wspec, wspec, wspec],
        out_specs=[wspec] * 4,
        compiler_params=_params(),
    )(c_all, dmod_cols, w, m, v)


def bias_update(dmod_all, w, m, v, name):
    def body(dm_ref, w_ref, m_ref, v_ref, g_ref, d_ref, nm_ref, nv_ref):
        g = jnp.sum(dm_ref[...], axis=0, keepdims=True)
        dl, nm, nv = _adamw(w_ref[...], g, m_ref[...], v_ref[...])
        g_ref[...] = g
        d_ref[...] = dl
        nm_ref[...] = nm
        nv_ref[...] = nv

    return _call(
        body, name=name,
        out_shape=[jax.ShapeDtypeStruct(w.shape, F32)] * 4,
        compiler_params=_params(),
    )(dmod_all, w, m, v)


def inproj_fwd(x, mod, ng, wg, seq, sectioned, name):
    m_rows, d = x.shape
    nsh, _, ns = wg.shape
    n = nsh * ns
    tm, tn = min(2 * ROW_TILE, seq), _col_tile(ns)
    per = ns // tn

    def body(x_ref, mod_ref, ng_ref, w_ref, proj_ref, h_ref):
        @pl.when(_pid(1) == 0)
        def _():
            xv = x_ref[...]
            r = lax.rsqrt(jnp.mean(xv * xv, axis=-1, keepdims=True) + EPS)
            md = mod_ref[0]
            h = (xv * r * ng_ref[...]) * (1.0 + md[:, d:2 * d]) + md[:, :d]
            h_ref[...] = h.astype(BF16)
        proj_ref[...] = _dot(h_ref[...], w_ref[...], NN)

    if sectioned:
        proj_shape = (nsh, m_rows, ns)
        proj_spec = pl.BlockSpec((None, tm, tn), lambda i, j: (j // per, i, j % per))
    else:
        proj_shape = (m_rows, n)
        proj_spec = pl.BlockSpec((tm, tn), lambda i, j: (i, j))
    return _call(
        body, name=name, grid=(m_rows // tm, n // tn),
        out_shape=[jax.ShapeDtypeStruct(proj_shape, F32), jax.ShapeDtypeStruct((m_rows, d), BF16)],
        in_specs=[pl.BlockSpec((tm, d), lambda i, j: (i, 0)),
                  pl.BlockSpec((1, 1, 3 * d), lambda i, j: ((i * tm) // seq, 0, 0)),
                  pl.BlockSpec((1, d), lambda i, j: (0, 0)),
                  pl.BlockSpec((None, d, tn), lambda i, j: (j // per, 0, j % per))],
        out_specs=[proj_spec, pl.BlockSpec((tm, d), lambda i, j: (i, 0))],
        compiler_params=_params(),
    )(x, mod, ng, wg)


def outproj_fwd(y, w, x, mod, seq, name):
    m_rows, di = y.shape
    d = w.shape[1]
    tm = min(ROW_TILE, seq)

    def body(y_ref, w_ref, x_ref, mod_ref, xn_ref, out_ref):
        acc = _dot(y_ref[...], w_ref[...], NN)
        out_ref[...] = acc.astype(BF16)
        xn_ref[...] = x_ref[...] + mod_ref[0][:, 2 * d:] * acc

    row = pl.BlockSpec((tm, d), lambda i: (i, 0))
    return _call(
        body, name=name, grid=(m_rows // tm,),
        out_shape=[jax.ShapeDtypeStruct((m_rows, d), F32), jax.ShapeDtypeStruct((m_rows, d), BF16)],
        in_specs=[pl.BlockSpec((tm, di), lambda i: (i, 0)),
                  pl.BlockSpec((di, d), lambda i: (0, 0)),
                  row,
                  pl.BlockSpec((1, 1, 3 * d), lambda i: ((i * tm) // seq, 0, 0))],
        out_specs=[row, row],
        compiler_params=_params(),
    )(y, w, x, mod)


def outproj_bwd(dxo, out, mod, w, seq, name):
    m_rows, d = dxo.shape
    di = w.shape[0]
    nb = m_rows // seq
    tm, tn = min(ROW_TILE, seq), _col_tile(di)

    def body(dxo_ref, out_ref, mod_ref, w_ref, dy_ref, dout_ref, dgate_ref):
        i = _pid(0)

        @pl.when(_pid(1) == 0)
        def _():
            dx = dxo_ref[...]
            dout_ref[...] = (mod_ref[0][:, 2 * d:] * dx).astype(BF16)
            part = jnp.sum(dx * out_ref[...].astype(F32), axis=0, keepdims=True)

            @pl.when((i * tm) % seq == 0)
            def _():
                dgate_ref[0] = part

            @pl.when((i * tm) % seq != 0)
            def _():
                dgate_ref[0] = dgate_ref[0] + part

        dy_ref[...] = _dot(dout_ref[...], w_ref[...], NT).astype(BF16)

    row = pl.BlockSpec((tm, d), lambda i, j: (i, 0))
    return _call(
        body, name=name, grid=(m_rows // tm, di // tn),
        out_shape=[jax.ShapeDtypeStruct((m_rows, di), BF16), jax.ShapeDtypeStruct((m_rows, d), BF16),
                   jax.ShapeDtypeStruct((nb, 1, d), F32)],
        in_specs=[row, row,
                  pl.BlockSpec((1, 1, 3 * d), lambda i, j: ((i * tm) // seq, 0, 0)),
                  pl.BlockSpec((tn, d), lambda i, j: (j, 0))],
        out_specs=[pl.BlockSpec((tm, tn), lambda i, j: (i, j)), row,
                   pl.BlockSpec((1, 1, d), lambda i, j: ((i * tm) // seq, 0, 0))],
        compiler_params=_params(),
    )(dxo, out, mod, w)


def grad_w_out(y, dout, name):
    m_rows, di = y.shape
    d = dout.shape[1]
    tm, tk = min(ROW_TILE, m_rows), _col_tile(di)
    n_m = m_rows // tm

    def body(y_ref, do_ref, o_ref, acc_ref):
        mi = _pid(1)

        @pl.when(mi == 0)
        def _():
            acc_ref[...] = jnp.zeros_like(acc_ref)

        acc_ref[...] += _dot(y_ref[...], do_ref[...], TN)

        @pl.when(mi == n_m - 1)
        def _():
            o_ref[...] = acc_ref[...].astype(BF16)

    return _call(
        body, name=name, grid=(di // tk, n_m),
        out_shape=jax.ShapeDtypeStruct((di, d), BF16),
        in_specs=[pl.BlockSpec((tm, tk), lambda j, mi: (mi, j)),
                  pl.BlockSpec((tm, d), lambda j, mi: (mi, 0))],
        out_specs=pl.BlockSpec((tk, d), lambda j, mi: (j, 0)),
        scratch_shapes=[pltpu.VMEM((tk, d), F32)],
        compiler_params=_params(),
    )(y, dout)


def grad_w_in(h, dproj, nsh, sectioned, name):
    m_rows, d = h.shape
    n = dproj.shape[0] * dproj.shape[2] if sectioned else dproj.shape[1]
    ns = n // nsh
    tm, tn = min(ROW_TILE, m_rows), ns
    per = ns // tn
    n_m = m_rows // tm

    def body(h_ref, dp_ref, o_ref, acc_ref):
        mi = _pid(1)
        if _phase(0):
            @pl.when(mi == 0)
            def _():
                acc_ref[...] = jnp.zeros_like(acc_ref)

        if _phase(1):
            acc_ref[...] += _dot(h_ref[...], dp_ref[...], TN)

        if _phase(2):
            @pl.when(mi == n_m - 1)
            def _():
                o_ref[...] = acc_ref[...].astype(BF16)

    if sectioned:
        dp_spec = pl.BlockSpec((None, tm, tn), lambda j, mi: (j // per, mi, j % per))
    else:
        dp_spec = pl.BlockSpec((tm, tn), lambda j, mi: (mi, j))
    return _call(
        body, name=name, grid=(n // tn, n_m),
        out_shape=jax.ShapeDtypeStruct((nsh, d, ns), BF16),
        in_specs=[pl.BlockSpec((tm, d), lambda j, mi: (mi, 0)), dp_spec],
        out_specs=pl.BlockSpec((None, d, tn), lambda j, mi: (j // per, 0, j % per)),
        scratch_shapes=[pltpu.VMEM((d, tn), F32)],
        compiler_params=_params(),
    )(h, dproj)


def inproj_bwd(dproj, wg, x, dxo, mod, ng, seq, sectioned, name):
    m_rows, d = x.shape
    nsh, _, ns = wg.shape
    n = nsh * ns
    nb = m_rows // seq
    tm, tk = min(ROW_TILE, seq), ns
    per = ns // tk
    n_k = n // tk

    def body(dp_ref, w_ref, x_ref, dxo_ref, mod_ref, ng_ref, dxi_ref, dsh_ref, dsc_ref, dng_ref, acc_ref):
        i, k = _pid(0), _pid(1)
        if _phase(0):
            @pl.when(k == 0)
            def _():
                acc_ref[...] = jnp.zeros_like(acc_ref)

        if _phase(1):
            acc_ref[...] += _dot(dp_ref[...], w_ref[...], NT)

        if not _phase(2):
            return

        @pl.when(k == n_k - 1)
        def _():
            dh = acc_ref[...]
            xv = x_ref[...]
            r = lax.rsqrt(jnp.mean(xv * xv, axis=-1, keepdims=True) + EPS)
            xn = xv * r
            md = mod_ref[0]
            gain = ng_ref[...]
            p_shift = jnp.sum(dh, axis=0, keepdims=True)
            p_scale = jnp.sum(dh * (xn * gain), axis=0, keepdims=True)
            drn = dh * (1.0 + md[:, d:2 * d])
            p_ng = jnp.sum(drn * xn, axis=0, keepdims=True)
            dxn = drn * gain
            dx = r * (dxn - xn * jnp.mean(dxn * xn, axis=-1, keepdims=True))
            dxi_ref[...] = dxo_ref[...] + dx

            @pl.when((i * tm) % seq == 0)
            def _():
                dsh_ref[0] = p_shift
                dsc_ref[0] = p_scale

            @pl.when((i * tm) % seq != 0)
            def _():
                dsh_ref[0] = dsh_ref[0] + p_shift
                dsc_ref[0] = dsc_ref[0] + p_scale

            @pl.when(i == 0)
            def _():
                dng_ref[...] = p_ng

            @pl.when(i != 0)
            def _():
                dng_ref[...] = dng_ref[...] + p_ng

    if sectioned:
        dp_spec = pl.BlockSpec((None, tm, tk), lambda i, k: (k // per, i, k % per))
    else:
        dp_spec = pl.BlockSpec((tm, tk), lambda i, k: (i, k))
    row = pl.BlockSpec((tm, d), lambda i, k: (i, 0))
    per_seq = pl.BlockSpec((1, 1, d), lambda i, k: ((i * tm) // seq, 0, 0))
    return _call(
        body, name=name, grid=(m_rows // tm, n_k),
        out_shape=[jax.ShapeDtypeStruct((m_rows, d), F32), jax.ShapeDtypeStruct((nb, 1, d), F32),
                   jax.ShapeDtypeStruct((nb, 1, d), F32), jax.ShapeDtypeStruct((1, d), F32)],
        in_specs=[dp_spec,
                  pl.BlockSpec((None, d, tk), lambda i, k: (k // per, 0, k % per)),
                  row, row,
                  pl.BlockSpec((1, 1, 3 * d), lambda i, k: ((i * tm) // seq, 0, 0)),
                  pl.BlockSpec((1, d), lambda i, k: (0, 0))],
        out_specs=[row, per_seq, per_seq, pl.BlockSpec((1, d), lambda i, k: (0, 0))],
        scratch_shapes=[pltpu.VMEM((tm, d), F32)],
        compiler_params=_params(),
    )(dproj, wg, x, dxo, mod, ng)


def _sgu_stats(proj_ref, vg_ref, di, gd, dgel_ref=None):
    s1 = jnp.zeros((SG_BLOCK, 1), F32)
    for g in range(SG_GROUPS):
        v_pre = proj_ref[:, di + g * gd:di + (g + 1) * gd]
        if dgel_ref is None:
            vg = _gelu(v_pre)
        else:
            vg, dgel_ref[:, g * gd:(g + 1) * gd] = _gelu_and_grad(v_pre)
        vg_ref[:, g * gd:(g + 1) * gd] = vg
        s1 = s1 + jnp.sum(vg, axis=1, keepdims=True)
    mu = s1 / di
    s2 = jnp.zeros((SG_BLOCK, 1), F32)
    for g in range(SG_GROUPS):
        dv = vg_ref[:, g * gd:(g + 1) * gd] - mu
        s2 = s2 + jnp.sum(dv * dv, axis=1, keepdims=True)
    return mu, lax.rsqrt(s2 / di + EPS)


def sgu_fwd(proj, ln_gain, ln_bias, ws, bs, name):
    m_rows, n3 = proj.shape
    di = n3 // 3
    gd = di // SG_GROUPS

    def body(proj_ref, lg_ref, lb_ref, ws_ref, bs_ref, y_ref, wsm_ref, vg_ref):
        @pl.when(_pid(0) == 0)
        def _():
            mask = _chunk_mask()
            for g in range(SG_GROUPS):
                wsm_ref[g] = jnp.where(mask, ws_ref[g], 0.0).astype(BF16)

        mu, rstd = _sgu_stats(proj_ref, vg_ref, di, gd)
        for g in range(SG_GROUPS):
            cs = slice(g * gd, (g + 1) * gd)
            vln = (vg_ref[:, cs] - mu) * rstd * lg_ref[:, cs] + lb_ref[:, cs]
            s = _dot(wsm_ref[g], vln.astype(BF16), NN) + bs_ref[g]
            u = _gelu(proj_ref[:, cs])
            gp = proj_ref[:, 2 * di + g * gd:2 * di + (g + 1) * gd]
            y_ref[:, cs] = (u * s * (gp * _sigmoid(gp))).astype(BF16)

    full = lambda shape: pl.BlockSpec(shape, lambda i: (0,) * len(shape))
    return _call(
        body, name=name, grid=(m_rows // SG_BLOCK,),
        out_shape=jax.ShapeDtypeStruct((m_rows, di), BF16),
        in_specs=[pl.BlockSpec((SG_BLOCK, n3), lambda i: (i, 0)),
                  full((1, di)), full((1, di)),
                  full((SG_GROUPS, SG_BLOCK, SG_BLOCK)), full((SG_GROUPS, SG_BLOCK, 1))],
        out_specs=pl.BlockSpec((SG_BLOCK, di), lambda i: (i, 0)),
        scratch_shapes=[pltpu.VMEM((SG_GROUPS, SG_BLOCK, SG_BLOCK), BF16), pltpu.VMEM((SG_BLOCK, di), F32)],
        compiler_params=_params(),
    )(proj, ln_gain, ln_bias, ws, bs)


def sgu_bwd(proj, dy, ln_gain, ln_bias, ws, bs, name):
    m_rows, n3 = proj.shape
    di = n3 // 3
    gd = di // SG_GROUPS
    n_i = m_rows // SG_BLOCK

    def body(proj_ref, dy_ref, lg_ref, lb_ref, ws_ref, bs_ref,
             dp_ref, dws_ref, dbs_ref, dlg_ref, dlb_ref, wsm_ref, vg_ref, dvh_ref, dgel_ref):
        i = _pid(0)

        def before():
            @pl.when(i == 0)
            def _():
                mask = _chunk_mask()
                for g in range(SG_GROUPS):
                    wsm_ref[g] = jnp.where(mask, ws_ref[g], 0.0).astype(BF16)
                dws_ref[...] = jnp.zeros_like(dws_ref)
                dbs_ref[...] = jnp.zeros_like(dbs_ref)
                dlg_ref[...] = jnp.zeros_like(dlg_ref)
                dlb_ref[...] = jnp.zeros_like(dlb_ref)

        def after():
            @pl.when(i == n_i - 1)
            def _():
                mask = _chunk_mask()
                for g in range(SG_GROUPS):
                    dws_ref[g] = jnp.where(mask, dws_ref[g], 0.0)

        before()
        mu, rstd = _sgu_stats(proj_ref, vg_ref, di, gd, dgel_ref)
        m1 = jnp.zeros((SG_BLOCK, 1), F32)
        m2 = jnp.zeros((SG_BLOCK, 1), F32)
        for g in range(SG_GROUPS):
            cs = slice(g * gd, (g + 1) * gd)
            gs = slice(2 * di + g * gd, 2 * di + (g + 1) * gd)
            gain = lg_ref[:, cs]
            vhat = (vg_ref[:, cs] - mu) * rstd
            vln_b = (vhat * gain + lb_ref[:, cs]).astype(BF16)
            s = _dot(wsm_ref[g], vln_b, NN) + bs_ref[g]
            u, du = _gelu_and_grad(proj_ref[:, cs])
            sg, dsg = _silu_and_grad(proj_ref[:, gs])
            dyv = dy_ref[:, cs].astype(F32)
            dp_ref[:, cs] = (dyv * s * sg * du).astype(BF16)
            dp_ref[:, gs] = (dyv * u * s * dsg).astype(BF16)
            ds = dyv * u * sg
            ds_b = ds.astype(BF16)
            dws_ref[g] = dws_ref[g] + _dot(ds_b, vln_b, NT)
            dbs_ref[g] = dbs_ref[g] + jnp.sum(ds, axis=1, keepdims=True)
            dvln = _dot(wsm_ref[g], ds_b, TN)
            dlg_ref[:, cs] = dlg_ref[:, cs] + jnp.sum(dvln * vhat, axis=0, keepdims=True)
            dlb_ref[:, cs] = dlb_ref[:, cs] + jnp.sum(dvln, axis=0, keepdims=True)
            dvh = dvln * gain
            dvh_ref[:, cs] = dvh
            m1 = m1 + jnp.sum(dvh, axis=1, keepdims=True)
            m2 = m2 + jnp.sum(dvh * vhat, axis=1, keepdims=True)
        m1 = m1 / di
        m2 = m2 / di
        for g in range(SG_GROUPS):
            cs = slice(g * gd, (g + 1) * gd)
            vs = slice(di + g * gd, di + (g + 1) * gd)
            vhat = (vg_ref[:, cs] - mu) * rstd
            dvg = rstd * (dvh_ref[:, cs] - m1 - vhat * m2)
            dp_ref[:, vs] = (dvg * dgel_ref[:, cs]).astype(BF16)

        after()

    full = lambda shape: pl.BlockSpec(shape, lambda i: (0,) * len(shape))
    return _call(
        body, name=name, grid=(n_i,),
        out_shape=[jax.ShapeDtypeStruct((m_rows, n3), BF16),
                   jax.ShapeDtypeStruct((SG_GROUPS, SG_BLOCK, SG_BLOCK), F32),
                   jax.ShapeDtypeStruct((SG_GROUPS, SG_BLOCK, 1), F32),
                   jax.ShapeDtypeStruct((1, di), F32), jax.ShapeDtypeStruct((1, di), F32)],
        in_specs=[pl.BlockSpec((SG_BLOCK, n3), lambda i: (i, 0)),
                  pl.BlockSpec((SG_BLOCK, di), lambda i: (i, 0)),
                  full((1, di)), full((1, di)),
                  full((SG_GROUPS, SG_BLOCK, SG_BLOCK)), full((SG_GROUPS, SG_BLOCK, 1))],
        out_specs=[pl.BlockSpec((SG_BLOCK, n3), lambda i: (i, 0)),
                   full((SG_GROUPS, SG_BLOCK, SG_BLOCK)), full((SG_GROUPS, SG_BLOCK, 1)),
                   full((1, di)), full((1, di))],
        scratch_shapes=[pltpu.VMEM((SG_GROUPS, SG_BLOCK, SG_BLOCK), BF16),
                        pltpu.VMEM((SG_BLOCK, di), F32), pltpu.VMEM((SG_BLOCK, di), F32),
                        pltpu.VMEM((SG_BLOCK, di), F32)],
        compiler_params=_params(),
    )(proj, dy, ln_gain, ln_bias, ws, bs)


def _lower_bound(lbraw):
    mx = jnp.maximum(lbraw[0:1, :], lbraw[1:2, :])
    e0 = jnp.exp(lbraw[0:1, :] - mx)
    e1 = jnp.exp(lbraw[1:2, :] - mx)
    p0 = e0 / (e0 + e1)
    p1 = e1 / (e0 + e1)
    return (p0 + p1) - p0, p0, p1


def _tri(lower):
    r = lax.broadcasted_iota(jnp.int32, (CHUNK, CHUNK), 0)
    c = lax.broadcasted_iota(jnp.int32, (CHUNK, CHUNK), 1)
    return ((r >= c) if lower else (c >= r)).astype(F32)


def _row(a, idx):
    r = lax.broadcasted_iota(jnp.int32, a.shape, 0)
    return jnp.sum(jnp.where(r == idx, a, 0.0), axis=0, keepdims=True)


def _hgrn_gates(qp, fp, lb, tri):
    sgm = _sigmoid_small(fp)
    f = lb + (1.0 - lb) * sgm
    k = 1.0 - f
    a = _dot(tri, jnp.log(f), NN, precision=lax.Precision.HIGHEST)
    a_mid = _row(a, CHUNK // 2 - 1)
    a_last = _row(a, CHUNK - 1)
    q, dq = _silu_and_grad(qp)
    e1, e2, e3, e4 = jnp.exp(a - a_mid), jnp.exp(a_mid - a), jnp.exp(a), jnp.exp(a_last - a)
    return dict(sgm=sgm, f=f, k=k, q=q, dq=dq, e1=e1, e2=e2, e3=e3, e4=e4, dec=jnp.exp(a_last),
                q_in=q * e1, k_in=k * e2, q_out=q * e3, k_out=k * e4)


def _causal():
    r = lax.broadcasted_iota(jnp.int32, (CHUNK, CHUNK), 0)
    c = lax.broadcasted_iota(jnp.int32, (CHUNK, CHUNK), 1)
    return r >= c


def hgrn_fwd(proj4, lbraw, gn, seq, name):
    _, m_rows, di = proj4.shape
    nb, nh, nc = m_rows // seq, di // HEAD_DIM, seq // CHUNK
    rows = min(HG_ROWS, seq)
    wide = HG_WIDE * HEAD_DIM
    ns, cpb = seq // rows, rows // CHUNK

    def body(p_ref, lb_ref, gn_ref, y_ref, sts_ref, st_ref):
        @pl.when(_pid(2) == 0)
        def _():
            st_ref[...] = jnp.zeros_like(st_ref)

        tri = _tri(True)
        causal = _causal()
        gain = gn_ref[...]
        lbs = [_lower_bound(lb_ref[:, j * HEAD_DIM:(j + 1) * HEAD_DIM])[0] for j in range(HG_WIDE)]

        units = [(n, j) for n in range(cpb) for j in range(HG_WIDE)]
        rs = lambda n: slice(n * CHUNK, (n + 1) * CHUNK)
        cs = lambda j: slice(j * HEAD_DIM, (j + 1) * HEAD_DIM)
        gates, v_b, sc_b, kv, o_in, o_x = {}, {}, {}, {}, {}, {}
        for n, j in units:
            gates[n, j] = _hgrn_gates(p_ref[0, rs(n), cs(j)], p_ref[1, rs(n), cs(j)], lbs[j], tri)
            v_b[n, j] = p_ref[2, rs(n), cs(j)].astype(BF16)
        for u in units:
            t = gates[u]
            sc_b[u] = jnp.where(causal, _dot(t["q_in"].astype(BF16), t["k_in"].astype(BF16), NT), 0.0).astype(BF16)
            kv[u] = _dot(v_b[u], t["k_out"].astype(BF16), TN)
        for u in units:
            o_in[u] = _dot(sc_b[u], v_b[u], NN)
        for j in range(HG_WIDE):
            st = st_ref[j]
            for n in range(cpb):
                sts_ref[n, :, cs(j)] = st
                o_x[n, j] = _dot(gates[n, j]["q_out"].astype(BF16), st.astype(BF16), NT)
                st = st * gates[n, j]["dec"] + kv[n, j]
            st_ref[j] = st
        for n, j in units:
            o = o_in[n, j] + o_x[n, j]
            r = lax.rsqrt(jnp.mean(o * o, axis=-1, keepdims=True) + EPS)
            gp = p_ref[3, rs(n), cs(j)]
            y_ref[rs(n), cs(j)] = ((o * r * gain) * (gp * _sigmoid(gp))).astype(BF16)

    return _call(
        body, name=name, grid=(nh // HG_WIDE, nb, ns),
        out_shape=[jax.ShapeDtypeStruct((m_rows, di), BF16),
                   jax.ShapeDtypeStruct((nb * nc, HEAD_DIM, di), F32)],
        in_specs=[pl.BlockSpec((4, rows, wide), lambda hg, b, s: (0, b * ns + s, hg)),
                  pl.BlockSpec((2, wide), lambda hg, b, s: (0, hg)),
                  pl.BlockSpec((1, HEAD_DIM), lambda hg, b, s: (0, 0))],
        out_specs=[pl.BlockSpec((rows, wide), lambda hg, b, s: (b * ns + s, hg)),
                   pl.BlockSpec((cpb, HEAD_DIM, wide), lambda hg, b, s: (b * ns + s, 0, hg))],
        scratch_shapes=[pltpu.VMEM((HG_WIDE, HEAD_DIM, HEAD_DIM), F32)],
        compiler_params=_params(),
    )(proj4, lbraw, gn)


def hgrn_bwd(proj4, dy, sts, lbraw, gn, seq, name):
    _, m_rows, di = proj4.shape
    nb, nh, nc = m_rows // seq, di // HEAD_DIM, seq // CHUNK
    rows = min(HG_ROWS, seq)
    wide = HG_WIDE * HEAD_DIM
    ns, cpb = seq // rows, rows // CHUNK
    n_hg = nh // HG_WIDE

    def body(p_ref, dy_ref, sts_ref, lb_ref, gn_ref, dp_ref, dlb_ref, dgn_ref, dst_ref, lbacc_ref, gnacc_ref):
        hg, b, s = _pid(0), _pid(1), _pid(2)
        tri, triu = _tri(True), _tri(False)
        causal = _causal()
        gain = gn_ref[...]
        first = (b == 0) & (s == 0)
        cs = lambda j: slice(j * HEAD_DIM, (j + 1) * HEAD_DIM)

        def before():
            @pl.when((hg == 0) & first)
            def _():
                gnacc_ref[...] = jnp.zeros_like(gnacc_ref)

            @pl.when(first)
            def _():
                lbacc_ref[...] = jnp.zeros_like(lbacc_ref)

            @pl.when(s == 0)
            def _():
                dst_ref[...] = jnp.zeros_like(dst_ref)

        def after():
            @pl.when((b == nb - 1) & (s == ns - 1))
            def _():
                for j in range(HG_WIDE):
                    _, p0, p1 = _lower_bound(lb_ref[:, cs(j)])
                    acc = lbacc_ref[:, cs(j)]
                    dlb_ref[0:1, cs(j)] = -acc * p0 * p1
                    dlb_ref[1:2, cs(j)] = acc * p1 * (1.0 - p1)

            @pl.when((hg == n_hg - 1) & (b == nb - 1) & (s == ns - 1))
            def _():
                tot = gnacc_ref[:, 0:HEAD_DIM]
                for j in range(1, HG_WIDE):
                    tot = tot + gnacc_ref[:, cs(j)]
                dgn_ref[...] = tot

        before()

        units = [(n, j) for n in range(cpb) for j in range(HG_WIDE)]
        rs = lambda n: slice(n * CHUNK, (n + 1) * CHUNK)
        lbs = [_lower_bound(lb_ref[:, cs(j)])[0] for j in range(HG_WIDE)]
        gates, v_b, st_b, sc_b, o, do_b = {}, {}, {}, {}, {}, {}
        dq_out, dsc_b, dv, g_st, dq_in, dk_in, dst_at, dk_out, ddec = {}, {}, {}, {}, {}, {}, {}, {}, {}
        for n, j in units:
            gates[n, j] = _hgrn_gates(p_ref[0, rs(n), cs(j)], p_ref[1, rs(n), cs(j)], lbs[j], tri)
            v_b[n, j] = p_ref[2, rs(n), cs(j)].astype(BF16)
            st_b[n, j] = sts_ref[n, :, cs(j)].astype(BF16)
        for u in units:
            t = gates[u]
            sc_b[u] = jnp.where(causal, _dot(t["q_in"].astype(BF16), t["k_in"].astype(BF16), NT), 0.0).astype(BF16)
        for u in units:
            o[u] = _dot(sc_b[u], v_b[u], NN) + _dot(gates[u]["q_out"].astype(BF16), st_b[u], NT)
        for n, j in units:
            ov = o[n, j]
            r = lax.rsqrt(jnp.mean(ov * ov, axis=-1, keepdims=True) + EPS)
            ohat = ov * r
            sg, dsg = _silu_and_grad(p_ref[3, rs(n), cs(j)])
            dyv = dy_ref[rs(n), cs(j)].astype(F32)
            dp_ref[3, rs(n), cs(j)] = (dyv * (ohat * gain) * dsg).astype(BF16)
            d_on = dyv * sg
            gnacc_ref[:, cs(j)] = gnacc_ref[:, cs(j)] + jnp.sum(d_on * ohat, axis=0, keepdims=True)
            dohat = d_on * gain
            do_b[n, j] = (r * (dohat - ohat * jnp.mean(dohat * ohat, axis=-1, keepdims=True))).astype(BF16)
        for u in units:
            dq_out[u] = _dot(do_b[u], st_b[u], NN)
            dsc_b[u] = jnp.where(causal, _dot(do_b[u], v_b[u], NT), 0.0).astype(BF16)
            dv[u] = _dot(sc_b[u], do_b[u], TN)
            g_st[u] = _dot(do_b[u], gates[u]["q_out"].astype(BF16), TN)
        for u in units:
            dq_in[u] = _dot(dsc_b[u], gates[u]["k_in"].astype(BF16), NN)
            dk_in[u] = _dot(dsc_b[u], gates[u]["q_in"].astype(BF16), TN)
        for j in range(HG_WIDE):
            dst = dst_ref[j]
            for n in reversed(range(cpb)):
                dst_at[n, j] = dst
                dst = dst * gates[n, j]["dec"] + g_st[n, j]
            dst_ref[j] = dst
        for n, j in units:
            dst = dst_at[n, j]
            dst_b = dst.astype(BF16)
            dk_out[n, j] = _dot(v_b[n, j], dst_b, NN)
            dv[n, j] = dv[n, j] + _dot(gates[n, j]["k_out"].astype(BF16), dst_b, NT)
            ddec[n, j] = jnp.sum(dst * sts_ref[n, :, cs(j)], axis=0, keepdims=True)
        for n, j in units:
            t = gates[n, j]
            dp_ref[2, rs(n), cs(j)] = dv[n, j].astype(BF16)
            dq = dq_in[n, j] * t["e1"] + dq_out[n, j] * t["e3"]
            dk = dk_in[n, j] * t["e2"] + dk_out[n, j] * t["e4"]
            w_in = dq_in[n, j] * t["q_in"] - dk_in[n, j] * t["k_in"]
            w_out = dk_out[n, j] * t["k_out"]
            da = w_in + dq_out[n, j] * t["q_out"] - w_out
            da_mid = -jnp.sum(w_in, axis=0, keepdims=True)
            da_last = jnp.sum(w_out, axis=0, keepdims=True) + ddec[n, j] * t["dec"]
            rid = lax.broadcasted_iota(jnp.int32, da.shape, 0)
            da = da + jnp.where(rid == CHUNK // 2 - 1, da_mid, 0.0) + jnp.where(rid == CHUNK - 1, da_last, 0.0)
            dlf = _dot(triu, da, NN, precision=lax.Precision.HIGHEST)
            df = dlf / t["f"] - dk
            sgm = t["sgm"]
            dp_ref[1, rs(n), cs(j)] = (df * (1.0 - lbs[j]) * sgm * (1.0 - sgm)).astype(BF16)
            lbacc_ref[:, cs(j)] = lbacc_ref[:, cs(j)] + jnp.sum(df * (1.0 - sgm), axis=0, keepdims=True)
            dp_ref[0, rs(n), cs(j)] = (dq * t["dq"]).astype(BF16)

        after()

    blk = lambda hg, b, s: b * ns + (ns - 1 - s)
    return _call(
        body, name=name, grid=(n_hg, nb, ns),
        out_shape=[jax.ShapeDtypeStruct((4, m_rows, di), BF16), jax.ShapeDtypeStruct((2, di), F32),
                   jax.ShapeDtypeStruct((1, HEAD_DIM), F32)],
        in_specs=[pl.BlockSpec((4, rows, wide), lambda hg, b, s: (0, blk(hg, b, s), hg)),
                  pl.BlockSpec((rows, wide), lambda hg, b, s: (blk(hg, b, s), hg)),
                  pl.BlockSpec((cpb, HEAD_DIM, wide), lambda hg, b, s: (blk(hg, b, s), 0, hg)),
                  pl.BlockSpec((2, wide), lambda hg, b, s: (0, hg)),
                  pl.BlockSpec((1, HEAD_DIM), lambda hg, b, s: (0, 0))],
        out_specs=[pl.BlockSpec((4, rows, wide), lambda hg, b, s: (0, blk(hg, b, s), hg)),
                   pl.BlockSpec((2, wide), lambda hg, b, s: (0, hg)),
                   pl.BlockSpec((1, HEAD_DIM), lambda hg, b, s: (0, 0))],
        scratch_shapes=[pltpu.VMEM((HG_WIDE, HEAD_DIM, HEAD_DIM), F32), pltpu.VMEM((1, wide), F32),
                        pltpu.VMEM((1, wide), F32)],
        compiler_params=_params(),
    )(proj4, dy, sts, lbraw, gn)


def outproj_loss(y, w, x, mod, fg, target, seq, name):
    m_rows, di = y.shape
    d = w.shape[1]
    tm = min(512, seq)

    def body(y_ref, w_ref, x_ref, mod_ref, fg_ref, t_ref, out_ref, loss_ref, dx_ref, dfg_ref):
        i = _pid(0)
        acc = _dot(y_ref[...], w_ref[...], NN)
        out_ref[...] = acc.astype(BF16)
        xv = x_ref[...] + mod_ref[0][:, 2 * d:] * acc
        gain = fg_ref[...]
        r = lax.rsqrt(jnp.mean(xv * xv, axis=-1, keepdims=True) + EPS)
        xn = xv * r
        e = xn * gain - t_ref[...]
        part = 0.5 * jnp.sum(jnp.mean(e * e, axis=-1, keepdims=True), axis=0, keepdims=True)
        dyv = e / d
        p_fg = jnp.sum(dyv * xn, axis=0, keepdims=True)
        dxn = dyv * gain
        dx_ref[...] = r * (dxn - xn * jnp.mean(dxn * xn, axis=-1, keepdims=True))

        @pl.when(i == 0)
        def _():
            loss_ref[...] = part
            dfg_ref[...] = p_fg

        @pl.when(i != 0)
        def _():
            loss_ref[...] = loss_ref[...] + part
            dfg_ref[...] = dfg_ref[...] + p_fg

    row = pl.BlockSpec((tm, d), lambda i: (i, 0))
    return _call(
        body, name=name, grid=(m_rows // tm,),
        out_shape=[jax.ShapeDtypeStruct((m_rows, d), BF16), jax.ShapeDtypeStruct((1, 1), F32),
                   jax.ShapeDtypeStruct((m_rows, d), F32), jax.ShapeDtypeStruct((1, d), F32)],
        in_specs=[pl.BlockSpec((tm, di), lambda i: (i, 0)),
                  pl.BlockSpec((di, d), lambda i: (0, 0)),
                  row,
                  pl.BlockSpec((1, 1, 3 * d), lambda i: ((i * tm) // seq, 0, 0)),
                  pl.BlockSpec((1, d), lambda i: (0, 0)), row],
        out_specs=[row, pl.BlockSpec((1, 1), lambda i: (0, 0)), row, pl.BlockSpec((1, d), lambda i: (0, 0))],
        compiler_params=_params(),
    )(y, w, x, mod, fg, target)


def _pack(parts):
    flat = jnp.concatenate([p.reshape(-1) for p in parts])
    pad = (-flat.shape[0]) % (8 * LANES)
    return jnp.pad(flat, (0, pad)).reshape(-1, LANES)


def kernel(x, c, norm_gain, w_ada, b_ada, a_w_in, a_ln_gain, a_ln_bias, a_w_s, a_b_s, a_w_out, b_w_in, b_lower_bounds, b_gn_gain, b_w_out, final_gain, loss_target, m_norm_gain, m_w_ada, m_b_ada, m_a_w_in, m_a_ln_gain, m_a_ln_bias, m_a_w_s, m_a_b_s, m_a_w_out, m_b_w_in, m_b_lower_bounds, m_b_gn_gain, m_b_w_out, m_final_gain, v_norm_gain, v_w_ada, v_b_ada, v_a_w_in, v_a_ln_gain, v_a_ln_bias, v_a_w_s, v_a_b_s, v_a_w_out, v_b_w_in, v_b_lower_bounds, v_b_gn_gain, v_b_w_out, v_final_gain):
    nb, seq, d = x.shape
    m_rows = nb * seq
    n_l = w_ada.shape[0]
    ada_cols = w_ada.shape[2]
    px, py, pc = _place()
    chip = 2 * px + py
    dev = 2 * chip + pc

    c_all = allgather_small(c.reshape(-1, LANES), "gather_c").reshape(N_DEV * nb, d)
    b_cols = lax.dynamic_slice_in_dim(b_ada, chip * ada_cols, ada_cols, axis=1).reshape(n_l, 1, ada_cols)
    mod_cols = ada_fwd(c_all, w_ada, b_cols, "ada_fwd")
    mod_g = allgather_small(mod_cols.reshape(-1, LANES), "gather_mod")
    mod_g = mod_g.reshape(N_CHIPS, 2, n_l, N_DEV * nb, ada_cols)[:, 0]
    mod_all = jnp.transpose(mod_g, (1, 2, 0, 3)).reshape(n_l, N_DEV * nb, 3 * d)
    mod_mine = lax.dynamic_slice_in_dim(mod_all, dev * nb, nb, axis=1)
    mod0 = mod_mine[0].reshape(nb, 1, 3 * d)
    mod1 = mod_mine[1].reshape(nb, 1, 3 * d)

    (wa_in, wa_out), tok_a = gather_inplace(
        [cast_into_slot(a_w_in[0], chip, mod_mine, "cast_a_in"), cast_into_slot(a_w_out[0], chip, mod_mine, "cast_a_out")],
        "gather_a")
    s_bi = gather_start(cast_into_slot(b_w_in[0], chip, tok_a, "cast_b_in"), "gather_b_in_start")
    s_bo = gather_start(cast_into_slot(b_w_out[0], chip, s_bi[3], "cast_b_out"), "gather_b_out_start")
    di = a_w_out.shape[1] * N_CHIPS
    wa_out = wa_out.reshape(di, d)

    x0 = x.reshape(m_rows, d)
    tgt = loss_target.reshape(m_rows, d)
    ng0 = norm_gain[0:1] + (s_bi[3][0, 0] + s_bo[3][0, 0])
    ng1 = norm_gain[1:2]
    bs_col = a_b_s[0].reshape(SG_GROUPS, SG_BLOCK, 1)
    proj_a, h_a = inproj_fwd(x0, mod0, ng0, wa_in, seq, False, "a_inproj")
    y_a = sgu_fwd(proj_a, a_ln_gain, a_ln_bias, a_w_s[0], bs_col, "a_sgu")
    x1, out_a = outproj_fwd(y_a, wa_out, x0, mod0, seq, "a_outproj")
    wb_in = gather_wait(*s_bi[:3], out_a, "gather_b_in_wait")
    proj_b, h_b = inproj_fwd(x1, mod1, ng1, wb_in, seq, True, "b_inproj")
    y_b, sts_b = hgrn_fwd(proj_b, b_lower_bounds, b_gn_gain, seq, "b_hgrn")
    wb_out = gather_wait(*s_bo[:3], y_b, "gather_b_out_wait").reshape(di, d)
    out_b, loss_part, dx2, dfg = outproj_loss(
        y_b, wb_out, x1, mod1, final_gain.reshape(1, d), tgt, seq, "b_outproj_loss")

    shard_rows = di // N_CHIPS
    dy_b, dout_b, dgate1 = outproj_bwd(dx2, out_b, mod1, wb_out, seq, "b_outproj_bwd")
    gwb_out = grad_w_out(y_b, dout_b, "b_grad_w_out").reshape(N_CHIPS, shard_rows, d)
    e_bo = exchange_start(gwb_out, "exchange_b_out_start")
    dproj_b, dlb, dgn = hgrn_bwd(
        proj_b, dy_b, sts_b, b_lower_bounds, b_gn_gain + e_bo[4][0, 0], seq, "b_hgrn_bwd")
    e_bi = exchange_start(grad_w_in(h_b, dproj_b, N_CHIPS, True, "b_grad_w_in"), "exchange_b_in_start")
    dx1, dshift1, dscale1, dng1 = inproj_bwd(
        dproj_b, wb_in, x1, dx2, mod1, ng1 + e_bi[4][0, 0], seq, True, "b_inproj_bwd")

    dy_a, dout_a, dgate0 = outproj_bwd(dx1, out_a, mod0, wa_out, seq, "a_outproj_bwd")
    gwa_out = grad_w_out(y_a, dout_a, "a_grad_w_out").reshape(N_CHIPS, shard_rows, d)
    e_ao = exchange_start(gwa_out, "exchange_a_out_start")
    dproj_a, dws, dbs, dlg, dlbias = sgu_bwd(
        proj_a, dy_a, a_ln_gain + e_ao[4][0, 0], a_ln_bias, a_w_s[0], bs_col, "a_sgu_bwd")
    def landed(ex, after, nm):
        parts_thru, land = exchange_wait(ex[0], ex[1], ex[2], ex[3], after, "exchange_" + nm + "_wait")
        return sum_parts(parts_thru, land, chip, "sum_" + nm)

    sum_bo = landed(e_bo, dproj_a, "b_out")
    sum_bi = landed(e_bi, sum_bo, "b_in")
    other_bo, other_bi = swap_sibling([sum_bo, sum_bi], "swap_b")
    gw_part = _deferred(grad_w_in, h_a, dproj_a, N_CHIPS, False, "a_grad_w_in")
    up_part = _deferred(adamw_pair, sum_bi, other_bi, b_w_in[0], m_b_w_in[0], v_b_w_in[0], "adamw_b_in",
                        steps=math.prod(gw_part.kw["grid"]))
    (gwa_in,), upd_bi = fused([gw_part, up_part], "a_grad_w_in_adamw_b_in")
    gb_in, db_in, mb_in, vb_in = [r.reshape(b_w_in.shape) for r in upd_bi]
    e_ai = exchange_start(gwa_in, "exchange_a_in_start")
    sum_ao = landed(e_ao, e_ai[4], "a_out")
    (other_ao,) = swap_sibling([sum_ao], "swap_a_out")
    ib_part = _deferred(inproj_bwd, dproj_a, wa_in, x0, dx1, mod0, norm_gain[0:1], seq, False, "a_inproj_bwd")
    ib_steps = math.prod(ib_part.kw["grid"])
    up_bo = _deferred(adamw_pair, sum_bo, other_bo, b_w_out[0], m_b_w_out[0], v_b_w_out[0], "adamw_b_out",
                      steps=ib_steps)
    up_ao = _deferred(adamw_pair, sum_ao, other_ao, a_w_out[0], m_a_w_out[0], v_a_w_out[0], "adamw_a_out",
                      steps=ib_steps)
    (dx0, dshift0, dscale0, dng0), upd_bo, upd_ao = fused([ib_part, up_bo, up_ao], "a_inproj_bwd_adamw_out")
    gb_out, db_out, mb_out, vb_out = [r.reshape(b_w_out.shape) for r in upd_bo]
    ga_out, da_out, ma_out, va_out = [r.reshape(a_w_out.shape) for r in upd_ao]
    grad_x = dx0.reshape(nb, seq, d)

    dmod = jnp.concatenate([dshift0, dscale0, dgate0, dshift1, dscale1, dgate1], axis=2)
    n_dmod = dmod.size
    small_g = [jnp.concatenate([dng0, dng1], axis=0), dlg, dlbias, dws, dbs, dlb, dfg, dgn]
    packed_g = _pack([dmod] + small_g + [loss_part])
    rows = packed_g.shape[0]
    s_small = gather_all_start(
        lax.dynamic_update_slice(jnp.zeros((N_DEV, rows, LANES), F32), packed_g[None], (dev, 0, 0)),
        "gather_small_start")

    sum_ai = landed(e_ai, s_small[3], "a_in")
    (other_ai,) = swap_sibling([sum_ai], "swap_a_in")
    ga_in, da_in, ma_in, va_in = [r.reshape(a_w_in.shape) for r in adamw_pair(
        sum_ai, other_ai, a_w_in[0], m_a_w_in[0], v_a_w_in[0], "adamw_a_in")]

    small_w = [norm_gain, a_ln_gain, a_ln_bias, a_w_s, a_b_s, b_lower_bounds, final_gain, b_gn_gain]
    small_m = [m_norm_gain, m_a_ln_gain, m_a_ln_bias, m_a_w_s, m_a_b_s, m_b_lower_bounds, m_final_gain, m_b_gn_gain]
    small_v = [v_norm_gain, v_a_ln_gain, v_a_ln_bias, v_a_w_s, v_a_b_s, v_b_lower_bounds, v_final_gain, v_b_gn_gain]
    rows_of = lambda a: a.reshape(-1, a.shape[-1])
    gathered = gather_all_wait(s_small[0], s_small[1], s_small[2], ga_in, "gather_small_wait")
    tail, small_res = small_update(
        gathered, n_dmod // LANES, [rows_of(a) for a in small_w], [rows_of(a) for a in small_m],
        [rows_of(a) for a in small_v], "small_update")
    loss = tail[0, 0]
    sg, sd, sm, sv = [[small_res[p][kind].reshape(w.shape) for p, w in enumerate(small_w)] for kind in range(4)]

    dmod_all = gathered[:, :n_dmod // LANES].reshape(N_DEV * nb, n_l, 3 * d)
    dmod_cols = lax.dynamic_slice_in_dim(dmod_all, chip * ada_cols, ada_cols, axis=2)
    dmod_cols = jnp.transpose(dmod_cols, (1, 0, 2))
    g_wada, d_wada, m_wada, v_wada = ada_bwd(c_all, dmod_cols, w_ada, m_w_ada, v_w_ada, "ada_bwd")
    flat = lambda a: a.reshape(1, -1)
    g_bada, d_bada, m_bada, v_bada = [
        r.reshape(b_ada.shape) for r in
        bias_update(dmod_all.reshape(N_DEV * nb, n_l * 3 * d), flat(b_ada), flat(m_b_ada), flat(v_b_ada), "bias_update")]

    def order(ng, wada, bada, ain, sm_rest, aout, bin_, bout):
        lg, lbi, ws_, bs_, lbd, fg_, gn_ = sm_rest
        return [ng, wada, bada, ain, lg, lbi, ws_, bs_, aout, bin_, lbd, gn_, bout, fg_]

    grads = order(sg[0], g_wada, g_bada, ga_in, sg[1:8], ga_out, gb_in, gb_out)
    deltas = order(sd[0], d_wada, d_bada, da_in, sd[1:8], da_out, db_in, db_out)
    new_m = order(sm[0], m_wada, m_bada, ma_in, sm[1:8], ma_out, mb_in, mb_out)
    new_v = order(sv[0], v_wada, v_bada, va_in, sv[1:8], va_out, vb_in, vb_out)
    return (loss, grad_x, *grads, *deltas, *new_m, *new_v)
```

```python
import math

import jax
import jax.numpy as jnp
from jax import lax
from jax.experimental import pallas as pl
from jax.experimental.pallas import tpu as pltpu

F32 = jnp.float32
BF16 = jnp.bfloat16
EPS = 1e-6
CHUNK = 64
SG_BLOCK = 128
SG_GROUPS = 8
HEAD_DIM = 128
HG_WIDE = 8
HG_ROWS = 256
N_CHIPS = 4
N_DEV = 8
GATHER_PIECES = 4
LANES = 128
ADAM_LR = 0.001
ADAM_B1 = 0.9
ADAM_B2 = 0.999
ADAM_EPS = 1e-08
ADAM_WD = 0.01
ADAM_STEP = 10
GELU_C0 = 0.7978845608028654
GELU_C1 = 0.044715
MESH = pl.DeviceIdType.MESH
VMEM_LIMIT = 56 * 1024 * 1024


ROW_TILE = 1024


def _col_tile(n):
    return next(t for t in (1024, 768, 512, 256) if n % t == 0)


def _run(body, **kw):
    return pl.pallas_call(body, **kw)


def _call(body, **kw):
    if _TRACE["defer"]:
        return _Deferred(body, kw)
    return _run(body, **kw)


_TRACE = {"defer": False, "ids": None, "phase": None}


def _pid(axis):
    return pl.program_id(axis) if _TRACE["ids"] is None else _TRACE["ids"][axis]


def _phase(k):
    return _TRACE["phase"] is None or _TRACE["phase"] == k


class _Deferred:
    def __init__(self, body, kw):
        self.body, self.kw, self.args = body, kw, None

    def __call__(self, *args):
        self.args = args
        return self


def _deferred(build, *args, **kw):
    _TRACE["defer"] = True
    try:
        return build(*args, **kw)
    finally:
        _TRACE["defer"] = False


def _as_list(v):
    return list(v) if isinstance(v, (list, tuple)) else [v]


def fused(parts, name):
    grids = [tuple(p.kw["grid"]) for p in parts]
    steps = [math.prod(g) for g in grids]
    assert len(set(steps)) == 1, steps

    def unravel(t, grid):
        ids, stride = [], 1
        for g in reversed(grid):
            ids.append((t // stride) % g)
            stride *= g
        return tuple(reversed(ids))

    def respec(spec, grid):
        if spec.index_map is None:
            return spec
        return pl.BlockSpec(spec.block_shape, lambda t, m=spec.index_map, g=grid: m(*unravel(t, g)))

    ins = [[respec(s, g) for s in _as_list(p.kw["in_specs"])] for p, g in zip(parts, grids)]
    outs = [[respec(s, g) for s in _as_list(p.kw["out_specs"])] for p, g in zip(parts, grids)]
    shapes = [_as_list(p.kw["out_shape"]) for p in parts]
    scratch = [_as_list(p.kw.get("scratch_shapes", [])) for p in parts]
    n_in, n_out, n_scr = [len(v) for v in ins], [len(v) for v in outs], [len(v) for v in scratch]

    def body(*refs):
        t = pl.program_id(0)
        for phase in range(3):
            off_in, off_out, off_scr = 0, sum(n_in), sum(n_in) + sum(n_out)
            for k, p in enumerate(parts):
                mine = (refs[off_in:off_in + n_in[k]] + refs[off_out:off_out + n_out[k]]
                        + refs[off_scr:off_scr + n_scr[k]])
                off_in, off_out, off_scr = off_in + n_in[k], off_out + n_out[k], off_scr + n_scr[k]
                _TRACE["ids"], _TRACE["phase"] = unravel(t, grids[k]), phase
                try:
                    p.body(*mine)
                finally:
                    _TRACE["ids"], _TRACE["phase"] = None, None

    res = _run(
        body, name=name, grid=(steps[0],),
        in_specs=sum(ins, []), out_specs=sum(outs, []), out_shape=sum(shapes, []),
        scratch_shapes=sum(scratch, []), compiler_params=_params(),
    )(*sum([list(p.args) for p in parts], []))
    split, off = [], 0
    for k in range(len(parts)):
        split.append(list(res[off:off + n_out[k]]))
        off += n_out[k]
    return split


def _params(**kw):
    return pltpu.CompilerParams(vmem_limit_bytes=VMEM_LIMIT, **kw)


def _sigmoid(x):
    return 0.5 * jnp.tanh(0.5 * x) + 0.5


def _sigmoid_small(x):
    return 1.0 / (1.0 + jnp.exp(-x))


def _silu_and_grad(x):
    s = _sigmoid(x)
    return x * s, s * (1.0 + x * (1.0 - s))


def _gelu(x):
    return 0.5 * x * (1.0 + jnp.tanh(GELU_C0 * (x + GELU_C1 * x * x * x)))


def _gelu_and_grad(x):
    t = jnp.tanh(GELU_C0 * (x + GELU_C1 * x * x * x))
    g = 0.5 * x * (1.0 + t)
    dg = 0.5 * (1.0 + t) + 0.5 * x * (1.0 - t * t) * (GELU_C0 * (1.0 + 3.0 * GELU_C1 * x * x))
    return g, dg


def _dot(a, b, dims, precision=None):
    return lax.dot_general(a, b, (dims, ((), ())), precision=precision, preferred_element_type=F32)


NN = ((1,), (0,))
NT = ((1,), (1,))
TN = ((0,), (0,))


def _adamw(w, g, m, v):
    m = ADAM_B1 * m + (1.0 - ADAM_B1) * g
    v = ADAM_B2 * v + (1.0 - ADAM_B2) * (g * g)
    m_hat = m / (1.0 - ADAM_B1 ** ADAM_STEP)
    v_hat = v / (1.0 - ADAM_B2 ** ADAM_STEP)
    delta = -ADAM_LR * (m_hat / (jnp.sqrt(v_hat) + ADAM_EPS) + ADAM_WD * w)
    return delta, m, v


def _chunk_mask():
    r = lax.broadcasted_iota(jnp.int32, (SG_BLOCK, SG_BLOCK), 0)
    c = lax.broadcasted_iota(jnp.int32, (SG_BLOCK, SG_BLOCK), 1)
    return (c // CHUNK) <= (r // CHUNK)


def _place():
    return lax.axis_index("x"), lax.axis_index("y"), lax.axis_index("c")


def _other_chips(x, y):
    return [(1 - x, y), (x, 1 - y), (1 - x, 1 - y)]


def allgather_small(v, name):
    m_per, n = v.shape

    def body(x_ref, out_ref, send_sems, recv_sems, local_sem):
        x, y, c = _place()
        me, sibling = (x, y, c), (x, y, 1 - c)
        chips = _other_chips(x, y)

        def rows(px, py, pc):
            return out_ref.at[pl.ds((4 * px + 2 * py + pc) * m_per, m_per), :]

        def copy(k, block, to, src=None):
            return pltpu.make_async_remote_copy(
                src_ref=rows(*block) if src is None else src, dst_ref=rows(*block),
                send_sem=send_sems.at[k], recv_sem=recv_sems.at[k], device_id=to, device_id_type=MESH)

        mine = pltpu.make_async_copy(x_ref, rows(*me), local_sem)
        mine.start()
        first = [copy(0, me, sibling, src=x_ref)]
        first += [copy(1 + j, me, (*chip, c), src=x_ref) for j, chip in enumerate(chips)]
        for cp in first:
            cp.start()
        passed = [copy(4 + j, (*chip, c), sibling) for j, chip in enumerate(chips)]
        for j, chip in enumerate(chips):
            copy(1 + j, (*chip, c), me).wait_recv()
            passed[j].start()
        copy(0, sibling, me).wait_recv()
        for j, chip in enumerate(chips):
            copy(4 + j, (*chip, 1 - c), me).wait_recv()
        for cp in first + passed:
            cp.wait_send()
        mine.wait()

    return _call(
        body, name=name,
        out_shape=jax.ShapeDtypeStruct((N_DEV * m_per, n), v.dtype),
        in_specs=[pl.BlockSpec(memory_space=pltpu.VMEM)],
        out_specs=pl.BlockSpec(memory_space=pltpu.VMEM),
        scratch_shapes=[pltpu.SemaphoreType.DMA((7,)), pltpu.SemaphoreType.DMA((7,)), pltpu.SemaphoreType.DMA],
    )(v)


def _hbm_spec():
    return pl.BlockSpec(memory_space=pltpu.HBM)


def _sem_spec():
    return pl.BlockSpec(memory_space=pltpu.SEMAPHORE)


def _split_params():
    return pltpu.CompilerParams(has_side_effects=pltpu.SideEffectType.DATAFLOW_SIDE_EFFECTING)


def _hbm(a):
    return pltpu.with_memory_space_constraint(a, pltpu.HBM)


def gather_inplace(lands, name):
    n = len(lands)
    per_land = 6 * GATHER_PIECES

    def body(*refs):
        land_refs, token = refs[n:2 * n], refs[2 * n]
        send_sems, recv_sems = refs[2 * n + 1:]
        x, y, c = _place()
        chips = _other_chips(x, y)

        def copy(w, k, q, chip_idx, core_half, to):
            piece = lands[w].shape[1] // (2 * GATHER_PIECES)
            rows = land_refs[w].at[chip_idx, pl.ds((core_half * GATHER_PIECES + q) * piece, piece), :]
            sem = per_land * w + GATHER_PIECES * k + q
            return pltpu.make_async_remote_copy(
                src_ref=rows, dst_ref=rows, send_sem=send_sems.at[sem], recv_sem=recv_sems.at[sem],
                device_id=to, device_id_type=MESH)

        order = [(q, w, j, px, py) for q in range(GATHER_PIECES) for w in range(n) for j, (px, py) in enumerate(chips)]
        first = [copy(w, j, q, 2 * x + y, c, (px, py, c)) for q, w, j, px, py in order]
        for cp in first:
            cp.start()
        passed = []
        for q, w, j, px, py in order:
            copy(w, j, q, 2 * px + py, c, (px, py, c)).wait_recv()
            passed.append(copy(w, 3 + j, q, 2 * px + py, c, (x, y, 1 - c)))
            passed[-1].start()
        for q, w, j, px, py in order:
            copy(w, 3 + j, q, 2 * px + py, 1 - c, (x, y, 1 - c)).wait_recv()
        for cp in first + passed:
            cp.wait_send()
        token[...] = jnp.zeros_like(token)

    res = _call(
        body, name=name,
        out_shape=[jax.ShapeDtypeStruct(a.shape, a.dtype) for a in lands] + [jax.ShapeDtypeStruct((8, LANES), F32)],
        in_specs=[_hbm_spec()] * n, out_specs=[_hbm_spec()] * n + [pl.BlockSpec(memory_space=pltpu.VMEM)],
        input_output_aliases={w: w for w in range(n)},
        scratch_shapes=[pltpu.SemaphoreType.DMA((per_land * n,)), pltpu.SemaphoreType.DMA((per_land * n,))],
    )(*lands)
    return res[:n], res[n]


def gather_start(land, name):
    def body(land_ref, send_sems, recv_sems, land_thru, token):
        del land_thru
        x, y, c = _place()
        for j, (px, py) in enumerate(_other_chips(x, y)):
            pltpu.make_async_remote_copy(
                src_ref=land_ref.at[2 * x + y], dst_ref=land_ref.at[2 * x + y],
                send_sem=send_sems.at[j], recv_sem=recv_sems.at[j], device_id=(px, py, c),
                device_id_type=MESH).start()
        token[...] = jnp.zeros_like(token)

    return _call(
        body, name=name,
        out_shape=(pltpu.SemaphoreType.DMA((3,)), pltpu.SemaphoreType.DMA((3,)),
                   pltpu.HBM(land.shape, land.dtype), jax.ShapeDtypeStruct((8, LANES), F32)),
        in_specs=(_hbm_spec(),),
        out_specs=(_sem_spec(), _sem_spec(), _hbm_spec(), pl.BlockSpec(memory_space=pltpu.VMEM)),
        input_output_aliases={0: 2}, compiler_params=_split_params(),
    )(_hbm(land))


def gather_wait(send_sems, recv_sems, land, after, name):
    def body(land_ref, send_sems, recv_sems, after_ref, land_out):
        del after_ref, land_out
        x, y, c = _place()
        for j, (px, py) in enumerate(_other_chips(x, y)):
            cp = pltpu.make_async_remote_copy(
                src_ref=land_ref.at[2 * x + y], dst_ref=land_ref.at[2 * px + py],
                send_sem=send_sems.at[j], recv_sem=recv_sems.at[j], device_id=(px, py, c), device_id_type=MESH)
            cp.wait_send()
            cp.wait_recv()

    return _call(
        body, name=name,
        out_shape=pltpu.HBM(land.shape, land.dtype),
        in_specs=(_hbm_spec(), _sem_spec(), _sem_spec(), pl.BlockSpec(memory_space=pl.ANY)),
        out_specs=_hbm_spec(), input_output_aliases={0: 0}, compiler_params=_split_params(),
    )(land, send_sems, recv_sems, after)


def _flips():
    return [(fx, fy, fc) for fx in (0, 1) for fy in (0, 1) for fc in (0, 1) if (fx, fy, fc) != (0, 0, 0)]


def _flipped(x, y, c, flip):
    fx, fy, fc = flip
    return (1 - x if fx else x, 1 - y if fy else y, 1 - c if fc else c)


def gather_all_start(land, name):
    def body(land_ref, send_sems, recv_sems, land_thru, token):
        del land_thru
        x, y, c = _place()
        for k, flip in enumerate(_flips()):
            pltpu.make_async_remote_copy(
                src_ref=land_ref.at[4 * x + 2 * y + c], dst_ref=land_ref.at[4 * x + 2 * y + c],
                send_sem=send_sems.at[k], recv_sem=recv_sems.at[k], device_id=_flipped(x, y, c, flip),
                device_id_type=MESH).start()
        token[...] = jnp.zeros_like(token)

    return _call(
        body, name=name,
        out_shape=(pltpu.SemaphoreType.DMA((7,)), pltpu.SemaphoreType.DMA((7,)),
                   pltpu.HBM(land.shape, land.dtype), jax.ShapeDtypeStruct((8, LANES), F32)),
        in_specs=(_hbm_spec(),),
        out_specs=(_sem_spec(), _sem_spec(), _hbm_spec(), pl.BlockSpec(memory_space=pltpu.VMEM)),
        input_output_aliases={0: 2}, compiler_params=_split_params(),
    )(_hbm(land))


def gather_all_wait(send_sems, recv_sems, land, after, name):
    def body(land_ref, send_sems, recv_sems, after_ref, land_out):
        del after_ref, land_out
        x, y, c = _place()
        for k, flip in enumerate(_flips()):
            px, py, pc = _flipped(x, y, c, flip)
            cp = pltpu.make_async_remote_copy(
                src_ref=land_ref.at[4 * x + 2 * y + c], dst_ref=land_ref.at[4 * px + 2 * py + pc],
                send_sem=send_sems.at[k], recv_sem=recv_sems.at[k], device_id=(px, py, pc), device_id_type=MESH)
            cp.wait_send()
            cp.wait_recv()

    return _call(
        body, name=name,
        out_shape=pltpu.HBM(land.shape, land.dtype),
        in_specs=(_hbm_spec(), _sem_spec(), _sem_spec(), pl.BlockSpec(memory_space=pl.ANY)),
        out_specs=_hbm_spec(), input_output_aliases={0: 0}, compiler_params=_split_params(),
    )(land, send_sems, recv_sems, after)


def exchange_start(parts, name):
    _, r, c_ = parts.shape

    def body(parts_ref, land_ref, send_sems, recv_sems, parts_thru, land_thru, token):
        del parts_thru, land_thru
        x, y, c = _place()
        for j, (px, py) in enumerate(_other_chips(x, y)):
            pltpu.make_async_remote_copy(
                src_ref=parts_ref.at[2 * px + py], dst_ref=land_ref.at[j],
                send_sem=send_sems.at[j], recv_sem=recv_sems.at[j], device_id=(px, py, c),
                device_id_type=MESH).start()
        token[...] = jnp.zeros_like(token)

    return _call(
        body, name=name,
        out_shape=(pltpu.SemaphoreType.DMA((3,)), pltpu.SemaphoreType.DMA((3,)),
                   pltpu.HBM(parts.shape, parts.dtype), pltpu.HBM((3, r, c_), parts.dtype),
                   jax.ShapeDtypeStruct((8, LANES), F32)),
        in_specs=(_hbm_spec(), _hbm_spec()),
        out_specs=(_sem_spec(), _sem_spec(), _hbm_spec(), _hbm_spec(), pl.BlockSpec(memory_space=pltpu.VMEM)),
        input_output_aliases={0: 2, 1: 3}, compiler_params=_split_params(),
    )(_hbm(parts), _hbm(lax.empty((3, r, c_), parts.dtype)))


def exchange_wait(send_sems, recv_sems, parts, land, after, name):
    def body(parts_ref, land_ref, send_sems, recv_sems, after_ref, parts_out, land_out):
        del after_ref, parts_out, land_out
        x, y, c = _place()
        for j, (px, py) in enumerate(_other_chips(x, y)):
            cp = pltpu.make_async_remote_copy(
                src_ref=parts_ref.at[2 * px + py], dst_ref=land_ref.at[j],
                send_sem=send_sems.at[j], recv_sem=recv_sems.at[j], device_id=(px, py, c), device_id_type=MESH)
            cp.wait_send()
            cp.wait_recv()

    return _call(
        body, name=name,
        out_shape=(pltpu.HBM(parts.shape, parts.dtype), pltpu.HBM(land.shape, land.dtype)),
        in_specs=(_hbm_spec(), _hbm_spec(), _sem_spec(), _sem_spec(), pl.BlockSpec(memory_space=pl.ANY)),
        out_specs=(_hbm_spec(), _hbm_spec()), input_output_aliases={0: 0, 1: 1},
        compiler_params=_split_params(),
    )(parts, land, send_sems, recv_sems, after)


def cast_into_slot(w, chip, after, name):
    r, c = w.shape
    tr = min(256, r)

    def body(s_ref, w_ref, after_ref, o_ref):
        del s_ref, after_ref
        o_ref[...] = w_ref[...].astype(BF16)

    return _call(
        body, name=name,
        grid_spec=pltpu.PrefetchScalarGridSpec(
            num_scalar_prefetch=1, grid=(r // tr,),
            in_specs=[pl.BlockSpec((tr, c), lambda i, s: (i, 0)), pl.BlockSpec(memory_space=pl.ANY)],
            out_specs=pl.BlockSpec((None, tr, c), lambda i, s: (s[0], i, 0))),
        out_shape=jax.ShapeDtypeStruct((N_CHIPS, r, c), BF16),
        compiler_params=_params(),
    )(chip.reshape(1).astype(jnp.int32), w, after)


def sum_parts(parts, land, chip, name):
    _, r, c = parts.shape
    tr = min(256, r)

    def body(s_ref, p_ref, l_ref, o_ref):
        del s_ref
        acc = p_ref[...].astype(F32) + l_ref[0].astype(F32)
        acc = acc + l_ref[1].astype(F32)
        o_ref[...] = (acc + l_ref[2].astype(F32)).astype(BF16)

    return _call(
        body, name=name,
        grid_spec=pltpu.PrefetchScalarGridSpec(
            num_scalar_prefetch=1, grid=(r // tr,),
            in_specs=[pl.BlockSpec((None, tr, c), lambda i, s: (s[0], i, 0)),
                      pl.BlockSpec((3, tr, c), lambda i, s: (0, i, 0))],
            out_specs=pl.BlockSpec((tr, c), lambda i, s: (i, 0))),
        out_shape=jax.ShapeDtypeStruct((r, c), BF16),
        compiler_params=_params(),
    )(chip.reshape(1).astype(jnp.int32), parts, land)


def swap_sibling(arrs, name):
    n = len(arrs)

    def body(*refs):
        ins, outs = refs[:n], refs[n:2 * n]
        send_sems, recv_sems = refs[2 * n:]
        x, y, c = _place()
        cps = []
        for w in range(n):
            cp = pltpu.make_async_remote_copy(
                src_ref=ins[w], dst_ref=outs[w], send_sem=send_sems.at[w], recv_sem=recv_sems.at[w],
                device_id=(x, y, 1 - c), device_id_type=MESH)
            cp.start()
            cps.append(cp)
        for cp in cps:
            cp.wait_recv()
        for cp in cps:
            cp.wait_send()

    return _call(
        body, name=name,
        out_shape=[jax.ShapeDtypeStruct(a.shape, a.dtype) for a in arrs],
        in_specs=[_hbm_spec()] * n, out_specs=[_hbm_spec()] * n,
        scratch_shapes=[pltpu.SemaphoreType.DMA((n,)), pltpu.SemaphoreType.DMA((n,))],
    )(*arrs)


def send_other_half(parts, name):
    n_s, r, c_ = parts.shape
    half = r // 2

    def body(parts_ref, got_ref, send_sem, recv_sem):
        x, y, c = _place()
        cp = pltpu.make_async_remote_copy(
            src_ref=parts_ref.at[:, pl.ds((1 - c) * half, half), :], dst_ref=got_ref,
            send_sem=send_sem, recv_sem=recv_sem, device_id=(x, y, 1 - c), device_id_type=MESH)
        cp.start()
        cp.wait()

    return _call(
        body, name=name,
        out_shape=jax.ShapeDtypeStruct((n_s, half, c_), parts.dtype),
        in_specs=[_hbm_spec()], out_specs=_hbm_spec(),
        scratch_shapes=[pltpu.SemaphoreType.DMA, pltpu.SemaphoreType.DMA],
    )(parts)


def add_kept_half(parts, got, core, name):
    n_s, r, c_ = parts.shape
    half = r // 2
    tr = min(256, half)
    n_t = half // tr

    def body(s_ref, p_ref, g_ref, o_ref):
        del s_ref
        o_ref[...] = (p_ref[...].astype(F32) + g_ref[...].astype(F32)).astype(BF16)

    return _call(
        body, name=name,
        grid_spec=pltpu.PrefetchScalarGridSpec(
            num_scalar_prefetch=1, grid=(n_s, n_t),
            in_specs=[pl.BlockSpec((None, tr, c_), lambda s, i, k: (s, k[0] * n_t + i, 0)),
                      pl.BlockSpec((None, tr, c_), lambda s, i, k: (s, i, 0))],
            out_specs=pl.BlockSpec((None, tr, c_), lambda s, i, k: (s, i, 0))),
        out_shape=jax.ShapeDtypeStruct((n_s, half, c_), BF16),
        compiler_params=_params(),
    )(core.reshape(1).astype(jnp.int32), parts, got)


def adamw_halves(mine, other, core, w, m, v, name):
    r, c = w.shape
    tr = min(128, r // 2)
    n_t = (r // 2) // tr

    def body(s_ref, mine_ref, other_ref, w_ref, m_ref, v_ref, g_ref, d_ref, nm_ref, nv_ref):
        is_mine = (_pid(0) // n_t) == s_ref[0]
        g = jnp.where(is_mine, mine_ref[...], other_ref[...]).astype(F32)
        d, nm, nv = _adamw(w_ref[...], g, m_ref[...], v_ref[...])
        g_ref[...] = g
        d_ref[...] = d
        nm_ref[...] = nm
        nv_ref[...] = nv

    half_spec = pl.BlockSpec((tr, c), lambda i, s: (i % n_t, 0))
    spec = pl.BlockSpec((tr, c), lambda i, s: (i, 0))
    return _call(
        body, name=name,
        grid_spec=pltpu.PrefetchScalarGridSpec(
            num_scalar_prefetch=1, grid=(2 * n_t,),
            in_specs=[half_spec, half_spec, spec, spec, spec], out_specs=[spec] * 4),
        out_shape=[jax.ShapeDtypeStruct((r, c), F32)] * 4,
        compiler_params=_params(),
    )(core.reshape(1).astype(jnp.int32), mine, other, w, m, v)


def adamw_pair(pa, pb, w, m, v, name, steps=None):
    r, c = w.shape
    tr = min(128, r) if steps is None else r // steps

    def body(pa_ref, pb_ref, w_ref, m_ref, v_ref, g_ref, d_ref, nm_ref, nv_ref):
        if not _phase(1):
            return
        g = pa_ref[...].astype(F32) + pb_ref[...].astype(F32)
        d, nm, nv = _adamw(w_ref[...], g, m_ref[...], v_ref[...])
        g_ref[...] = g
        d_ref[...] = d
        nm_ref[...] = nm
        nv_ref[...] = nv

    spec = pl.BlockSpec((tr, c), lambda i: (i, 0))
    return _call(
        body, name=name, grid=(r // tr,),
        out_shape=[jax.ShapeDtypeStruct((r, c), F32)] * 4,
        in_specs=[spec] * 5, out_specs=[spec] * 4,
        compiler_params=_params(),
    )(pa, pb, w, m, v)


def small_update(gathered, first_row, ws, ms, vs, name):
    n_w = len(ws)
    total_rows = gathered.shape[1]

    def body(*refs):
        g_ref = refs[0]
        w_refs, m_refs, v_refs = refs[1:1 + n_w], refs[1 + n_w:1 + 2 * n_w], refs[1 + 2 * n_w:1 + 3 * n_w]
        tail_ref = refs[1 + 3 * n_w]
        outs = refs[2 + 3 * n_w:2 + 7 * n_w]
        sum_ref = refs[2 + 7 * n_w]
        acc = g_ref[0]
        for k in range(1, N_DEV):
            acc = acc + g_ref[k]
        sum_ref[...] = acc
        row = first_row
        for p in range(n_w):
            a, b = ws[p].shape
            per = b // LANES
            g_out, d_out, m_out, v_out = outs[4 * p:4 * p + 4]
            if per == 1:
                g_out[...] = sum_ref[row:row + a, :]
            else:
                for i in range(a):
                    for jc in range(per):
                        g_out[i:i + 1, jc * LANES:(jc + 1) * LANES] = sum_ref[row + i * per + jc:row + i * per + jc + 1, :]
            row += a * per
            dl, nm, nv = _adamw(w_refs[p][...], g_out[...], m_refs[p][...], v_refs[p][...])
            d_out[...] = dl
            m_out[...] = nm
            v_out[...] = nv
        tail_ref[...] = sum_ref[row:row + 1, :]

    out_shape = [jax.ShapeDtypeStruct((1, LANES), F32)]
    for w in ws:
        out_shape += [jax.ShapeDtypeStruct(w.shape, F32)] * 4
    res = _call(
        body, name=name, out_shape=out_shape,
        scratch_shapes=[pltpu.VMEM((total_rows, LANES), F32)],
        compiler_params=_params(),
    )(gathered, *ws, *ms, *vs)
    return res[0], [res[1 + 4 * p:5 + 4 * p] for p in range(n_w)]


def ada_fwd(c_all, w_ada, b_cols, name):
    n_l, d, cols = w_ada.shape
    nb = c_all.shape[0]
    tn = 256

    def body(c_ref, w_ref, b_ref, o_ref):
        cv = c_ref[...]
        ca = (cv * _sigmoid(cv)).astype(BF16)
        o_ref[...] = _dot(ca, w_ref[...].astype(BF16), NN) + b_ref[...]

    return _call(
        body, name=name, grid=(n_l, cols // tn),
        out_shape=jax.ShapeDtypeStruct((n_l, nb, cols), F32),
        in_specs=[pl.BlockSpec((nb, d), lambda l, j: (0, 0)),
                  pl.BlockSpec((None, d, tn), lambda l, j: (l, 0, j)),
                  pl.BlockSpec((None, 1, tn), lambda l, j: (l, 0, j))],
        out_specs=pl.BlockSpec((None, nb, tn), lambda l, j: (l, 0, j)),
        compiler_params=_params(),
    )(c_all, w_ada, b_cols)


def ada_bwd(c_all, dmod_cols, w, m, v, name):
    n_l, d, cols = w.shape
    nb = c_all.shape[0]
    tn = 256

    def body(c_ref, dm_ref, w_ref, m_ref, v_ref, g_ref, d_ref, nm_ref, nv_ref):
        cv = c_ref[...]
        ca = (cv * _sigmoid(cv)).astype(BF16)
        g = _dot(ca, dm_ref[...].astype(BF16), TN)
        dl, nm, nv = _adamw(w_ref[...], g, m_ref[...], v_ref[...])
        g_ref[...] = g
        d_ref[...] = dl
        nm_ref[...] = nm
        nv_ref[...] = nv

    wspec = pl.BlockSpec((None, d, tn), lambda l, j: (l, 0, j))
    return _call(
        body, name=name, grid=(n_l, cols // tn),
        out_shape=[jax.ShapeDtypeStruct((n_l, d, cols), F32)] * 4,
        in_specs=[pl.BlockSpec((nb, d), lambda l, j: (0, 0)),
                  pl.BlockSpec((None, nb, tn), lambda l, j: (l, 0, j)),
                  wspec, wspec, wspec],
        out_specs=[wspec] * 4,
        compiler_params=_params(),
    )(c_all, dmod_cols, w, m, v)


def bias_update(dmod_all, w, m, v, name):
    def body(dm_ref, w_ref, m_ref, v_ref, g_ref, d_ref, nm_ref, nv_ref):
        g = jnp.sum(dm_ref[...], axis=0, keepdims=True)
        dl, nm, nv = _adamw(w_ref[...], g, m_ref[...], v_ref[...])
        g_ref[...] = g
        d_ref[...] = dl
        nm_ref[...] = nm
        nv_ref[...] = nv

    return _call(
        body, name=name,
        out_shape=[jax.ShapeDtypeStruct(w.shape, F32)] * 4,
        compiler_params=_params(),
    )(dmod_all, w, m, v)


def inproj_fwd(x, mod, ng, wg, seq, sectioned, name):
    m_rows, d = x.shape
    nsh, _, ns = wg.shape
    n = nsh * ns
    tm, tn = min(2 * ROW_TILE, seq), _col_tile(ns)
    per = ns // tn

    def body(x_ref, mod_ref, ng_ref, w_ref, proj_ref, h_ref):
        @pl.when(_pid(1) == 0)
        def _():
            xv = x_ref[...]
            r = lax.rsqrt(jnp.mean(xv * xv, axis=-1, keepdims=True) + EPS)
            md = mod_ref[0]
            h = (xv * r * ng_ref[...]) * (1.0 + md[:, d:2 * d]) + md[:, :d]
            h_ref[...] = h.astype(BF16)
        proj_ref[...] = _dot(h_ref[...], w_ref[...], NN)

    if sectioned:
        proj_shape = (nsh, m_rows, ns)
        proj_spec = pl.BlockSpec((None, tm, tn), lambda i, j: (j // per, i, j % per))
    else:
        proj_shape = (m_rows, n)
        proj_spec = pl.BlockSpec((tm, tn), lambda i, j: (i, j))
    return _call(
        body, name=name, grid=(m_rows // tm, n // tn),
        out_shape=[jax.ShapeDtypeStruct(proj_shape, F32), jax.ShapeDtypeStruct((m_rows, d), BF16)],
        in_specs=[pl.BlockSpec((tm, d), lambda i, j: (i, 0)),
                  pl.BlockSpec((1, 1, 3 * d), lambda i, j: ((i * tm) // seq, 0, 0)),
                  pl.BlockSpec((1, d), lambda i, j: (0, 0)),
                  pl.BlockSpec((None, d, tn), lambda i, j: (j // per, 0, j % per))],
        out_specs=[proj_spec, pl.BlockSpec((tm, d), lambda i, j: (i, 0))],
        compiler_params=_params(),
    )(x, mod, ng, wg)


def outproj_fwd(y, w, x, mod, seq, name):
    m_rows, di = y.shape
    d = w.shape[1]
    tm = min(ROW_TILE, seq)

    def body(y_ref, w_ref, x_ref, mod_ref, xn_ref, out_ref):
        acc = _dot(y_ref[...], w_ref[...], NN)
        out_ref[...] = acc.astype(BF16)
        xn_ref[...] = x_ref[...] + mod_ref[0][:, 2 * d:] * acc

    row = pl.BlockSpec((tm, d), lambda i: (i, 0))
    return _call(
        body, name=name, grid=(m_rows // tm,),
        out_shape=[jax.ShapeDtypeStruct((m_rows, d), F32), jax.ShapeDtypeStruct((m_rows, d), BF16)],
        in_specs=[pl.BlockSpec((tm, di), lambda i: (i, 0)),
                  pl.BlockSpec((di, d), lambda i: (0, 0)),
                  row,
                  pl.BlockSpec((1, 1, 3 * d), lambda i: ((i * tm) // seq, 0, 0))],
        out_specs=[row, row],
        compiler_params=_params(),
    )(y, w, x, mod)


def outproj_bwd(dxo, out, mod, w, seq, name):
    m_rows, d = dxo.shape
    di = w.shape[0]
    nb = m_rows // seq
    tm, tn = min(ROW_TILE, seq), _col_tile(di)

    def body(dxo_ref, out_ref, mod_ref, w_ref, dy_ref, dout_ref, dgate_ref):
        i = _pid(0)

        @pl.when(_pid(1) == 0)
        def _():
            dx = dxo_ref[...]
            dout_ref[...] = (mod_ref[0][:, 2 * d:] * dx).astype(BF16)
            part = jnp.sum(dx * out_ref[...].astype(F32), axis=0, keepdims=True)

            @pl.when((i * tm) % seq == 0)
            def _():
                dgate_ref[0] = part

            @pl.when((i * tm) % seq != 0)
            def _():
                dgate_ref[0] = dgate_ref[0] + part

        dy_ref[...] = _dot(dout_ref[...], w_ref[...], NT).astype(BF16)

    row = pl.BlockSpec((tm, d), lambda i, j: (i, 0))
    return _call(
        body, name=name, grid=(m_rows // tm, di // tn),
        out_shape=[jax.ShapeDtypeStruct((m_rows, di), BF16), jax.ShapeDtypeStruct((m_rows, d), BF16),
                   jax.ShapeDtypeStruct((nb, 1, d), F32)],
        in_specs=[row, row,
                  pl.BlockSpec((1, 1, 3 * d), lambda i, j: ((i * tm) // seq, 0, 0)),
                  pl.BlockSpec((tn, d), lambda i, j: (j, 0))],
        out_specs=[pl.BlockSpec((tm, tn), lambda i, j: (i, j)), row,
                   pl.BlockSpec((1, 1, d), lambda i, j: ((i * tm) // seq, 0, 0))],
        compiler_params=_params(),
    )(dxo, out, mod, w)


def grad_w_out(y, dout, name):
    m_rows, di = y.shape
    d = dout.shape[1]
    tm, tk = min(ROW_TILE, m_rows), _col_tile(di)
    n_m = m_rows // tm

    def body(y_ref, do_ref, o_ref, acc_ref):
        mi = _pid(1)

        @pl.when(mi == 0)
        def _():
            acc_ref[...] = jnp.zeros_like(acc_ref)

        acc_ref[...] += _dot(y_ref[...], do_ref[...], TN)

        @pl.when(mi == n_m - 1)
        def _():
            o_ref[...] = acc_ref[...].astype(BF16)

    return _call(
        body, name=name, grid=(di // tk, n_m),
        out_shape=jax.ShapeDtypeStruct((di, d), BF16),
        in_specs=[pl.BlockSpec((tm, tk), lambda j, mi: (mi, j)),
                  pl.BlockSpec((tm, d), lambda j, mi: (mi, 0))],
        out_specs=pl.BlockSpec((tk, d), lambda j, mi: (j, 0)),
        scratch_shapes=[pltpu.VMEM((tk, d), F32)],
        compiler_params=_params(),
    )(y, dout)


def grad_w_in(h, dproj, nsh, sectioned, name):
    m_rows, d = h.shape
    n = dproj.shape[0] * dproj.shape[2] if sectioned else dproj.shape[1]
    ns = n // nsh
    tm, tn = min(ROW_TILE, m_rows), ns
    per = ns // tn
    n_m = m_rows // tm

    def body(h_ref, dp_ref, o_ref, acc_ref):
        mi = _pid(1)
        if _phase(0):
            @pl.when(mi == 0)
            def _():
                acc_ref[...] = jnp.zeros_like(acc_ref)

        if _phase(1):
            acc_ref[...] += _dot(h_ref[...], dp_ref[...], TN)

        if _phase(2):
            @pl.when(mi == n_m - 1)
            def _():
                o_ref[...] = acc_ref[...].astype(BF16)

    if sectioned:
        dp_spec = pl.BlockSpec((None, tm, tn), lambda j, mi: (j // per, mi, j % per))
    else:
        dp_spec = pl.BlockSpec((tm, tn), lambda j, mi: (mi, j))
    return _call(
        body, name=name, grid=(n // tn, n_m),
        out_shape=jax.ShapeDtypeStruct((nsh, d, ns), BF16),
        in_specs=[pl.BlockSpec((tm, d), lambda j, mi: (mi, 0)), dp_spec],
        out_specs=pl.BlockSpec((None, d, tn), lambda j, mi: (j // per, 0, j % per)),
        scratch_shapes=[pltpu.VMEM((d, tn), F32)],
        compiler_params=_params(),
    )(h, dproj)


def inproj_bwd(dproj, wg, x, dxo, mod, ng, seq, sectioned, name):
    m_rows, d = x.shape
    nsh, _, ns = wg.shape
    n = nsh * ns
    nb = m_rows // seq
    tm, tk = min(ROW_TILE, seq), ns
    per = ns // tk
    n_k = n // tk

    def body(dp_ref, w_ref, x_ref, dxo_ref, mod_ref, ng_ref, dxi_ref, dsh_ref, dsc_ref, dng_ref, acc_ref):
        i, k = _pid(0), _pid(1)
        if _phase(0):
            @pl.when(k == 0)
            def _():
                acc_ref[...] = jnp.zeros_like(acc_ref)

        if _phase(1):
            acc_ref[...] += _dot(dp_ref[...], w_ref[...], NT)

        if not _phase(2):
            return

        @pl.when(k == n_k - 1)
        def _():
            dh = acc_ref[...]
            xv = x_ref[...]
            r = lax.rsqrt(jnp.mean(xv * xv, axis=-1, keepdims=True) + EPS)
            xn = xv * r
            md = mod_ref[0]
            gain = ng_ref[...]
            p_shift = jnp.sum(dh, axis=0, keepdims=True)
            p_scale = jnp.sum(dh * (xn * gain), axis=0, keepdims=True)
            drn = dh * (1.0 + md[:, d:2 * d])
            p_ng = jnp.sum(drn * xn, axis=0, keepdims=True)
            dxn = drn * gain
            dx = r * (dxn - xn * jnp.mean(dxn * xn, axis=-1, keepdims=True))
            dxi_ref[...] = dxo_ref[...] + dx

            @pl.when((i * tm) % seq == 0)
            def _():
                dsh_ref[0] = p_shift
                dsc_ref[0] = p_scale

            @pl.when((i * tm) % seq != 0)
            def _():
                dsh_ref[0] = dsh_ref[0] + p_shift
                dsc_ref[0] = dsc_ref[0] + p_scale

            @pl.when(i == 0)
            def _():
                dng_ref[...] = p_ng

            @pl.when(i != 0)
            def _():
                dng_ref[...] = dng_ref[...] + p_ng

    if sectioned:
        dp_spec = pl.BlockSpec((None, tm, tk), lambda i, k: (k // per, i, k % per))
    else:
        dp_spec = pl.BlockSpec((tm, tk), lambda i, k: (i, k))
    row = pl.BlockSpec((tm, d), lambda i, k: (i, 0))
    per_seq = pl.BlockSpec((1, 1, d), lambda i, k: ((i * tm) // seq, 0, 0))
    return _call(
        body, name=name, grid=(m_rows // tm, n_k),
        out_shape=[jax.ShapeDtypeStruct((m_rows, d), F32), jax.ShapeDtypeStruct((nb, 1, d), F32),
                   jax.ShapeDtypeStruct((nb, 1, d), F32), jax.ShapeDtypeStruct((1, d), F32)],
        in_specs=[dp_spec,
                  pl.BlockSpec((None, d, tk), lambda i, k: (k // per, 0, k % per)),
                  row, row,
                  pl.BlockSpec((1, 1, 3 * d), lambda i, k: ((i * tm) // seq, 0, 0)),
                  pl.BlockSpec((1, d), lambda i, k: (0, 0))],
        out_specs=[row, per_seq, per_seq, pl.BlockSpec((1, d), lambda i, k: (0, 0))],
        scratch_shapes=[pltpu.VMEM((tm, d), F32)],
        compiler_params=_params(),
    )(dproj, wg, x, dxo, mod, ng)


def _sgu_stats(proj_ref, vg_ref, di, gd, dgel_ref=None):
    s1 = jnp.zeros((SG_BLOCK, 1), F32)
    for g in range(SG_GROUPS):
        v_pre = proj_ref[:, di + g * gd:di + (g + 1) * gd]
        if dgel_ref is None:
            vg = _gelu(v_pre)
        else:
            vg, dgel_ref[:, g * gd:(g + 1) * gd] = _gelu_and_grad(v_pre)
        vg_ref[:, g * gd:(g + 1) * gd] = vg
        s1 = s1 + jnp.sum(vg, axis=1, keepdims=True)
    mu = s1 / di
    s2 = jnp.zeros((SG_BLOCK, 1), F32)
    for g in range(SG_GROUPS):
        dv = vg_ref[:, g * gd:(g + 1) * gd] - mu
        s2 = s2 + jnp.sum(dv * dv, axis=1, keepdims=True)
    return mu, lax.rsqrt(s2 / di + EPS)


def sgu_fwd(proj, ln_gain, ln_bias, ws, bs, name):
    m_rows, n3 = proj.shape
    di = n3 // 3
    gd = di // SG_GROUPS

    def body(proj_ref, lg_ref, lb_ref, ws_ref, bs_ref, y_ref, wsm_ref, vg_ref):
        @pl.when(_pid(0) == 0)
        def _():
            mask = _chunk_mask()
            for g in range(SG_GROUPS):
                wsm_ref[g] = jnp.where(mask, ws_ref[g], 0.0).astype(BF16)

        mu, rstd = _sgu_stats(proj_ref, vg_ref, di, gd)
        for g in range(SG_GROUPS):
            cs = slice(g * gd, (g + 1) * gd)
            vln = (vg_ref[:, cs] - mu) * rstd * lg_ref[:, cs] + lb_ref[:, cs]
            s = _dot(wsm_ref[g], vln.astype(BF16), NN) + bs_ref[g]
            u = _gelu(proj_ref[:, cs])
            gp = proj_ref[:, 2 * di + g * gd:2 * di + (g + 1) * gd]
            y_ref[:, cs] = (u * s * (gp * _sigmoid(gp))).astype(BF16)

    full = lambda shape: pl.BlockSpec(shape, lambda i: (0,) * len(shape))
    return _call(
        body, name=name, grid=(m_rows // SG_BLOCK,),
        out_shape=jax.ShapeDtypeStruct((m_rows, di), BF16),
        in_specs=[pl.BlockSpec((SG_BLOCK, n3), lambda i: (i, 0)),
                  full((1, di)), full((1, di)),
                  full((SG_GROUPS, SG_BLOCK, SG_BLOCK)), full((SG_GROUPS, SG_BLOCK, 1))],
        out_specs=pl.BlockSpec((SG_BLOCK, di), lambda i: (i, 0)),
        scratch_shapes=[pltpu.VMEM((SG_GROUPS, SG_BLOCK, SG_BLOCK), BF16), pltpu.VMEM((SG_BLOCK, di), F32)],
        compiler_params=_params(),
    )(proj, ln_gain, ln_bias, ws, bs)


def sgu_bwd(proj, dy, ln_gain, ln_bias, ws, bs, name):
    m_rows, n3 = proj.shape
    di = n3 // 3
    gd = di // SG_GROUPS
    n_i = m_rows // SG_BLOCK

    def body(proj_ref, dy_ref, lg_ref, lb_ref, ws_ref, bs_ref,
             dp_ref, dws_ref, dbs_ref, dlg_ref, dlb_ref, wsm_ref, vg_ref, dvh_ref, dgel_ref):
        i = _pid(0)

        def before():
            @pl.when(i == 0)
            def _():
                mask = _chunk_mask()
                for g in range(SG_GROUPS):
                    wsm_ref[g] = jnp.where(mask, ws_ref[g], 0.0).astype(BF16)
                dws_ref[...] = jnp.zeros_like(dws_ref)
                dbs_ref[...] = jnp.zeros_like(dbs_ref)
                dlg_ref[...] = jnp.zeros_like(dlg_ref)
                dlb_ref[...] = jnp.zeros_like(dlb_ref)

        def after():
            @pl.when(i == n_i - 1)
            def _():
                mask = _chunk_mask()
                for g in range(SG_GROUPS):
                    dws_ref[g] = jnp.where(mask, dws_ref[g], 0.0)

        before()
        mu, rstd = _sgu_stats(proj_ref, vg_ref, di, gd, dgel_ref)
        m1 = jnp.zeros((SG_BLOCK, 1), F32)
        m2 = jnp.zeros((SG_BLOCK, 1), F32)
        for g in range(SG_GROUPS):
            cs = slice(g * gd, (g + 1) * gd)
            gs = slice(2 * di + g * gd, 2 * di + (g + 1) * gd)
            gain = lg_ref[:, cs]
            vhat = (vg_ref[:, cs] - mu) * rstd
            vln_b = (vhat * gain + lb_ref[:, cs]).astype(BF16)
            s = _dot(wsm_ref[g], vln_b, NN) + bs_ref[g]
            u, du = _gelu_and_grad(proj_ref[:, cs])
            sg, dsg = _silu_and_grad(proj_ref[:, gs])
            dyv = dy_ref[:, cs].astype(F32)
            dp_ref[:, cs] = (dyv * s * sg * du).astype(BF16)
            dp_ref[:, gs] = (dyv * u * s * dsg).astype(BF16)
            ds = dyv * u * sg
            ds_b = ds.astype(BF16)
            dws_ref[g] = dws_ref[g] + _dot(ds_b, vln_b, NT)
            dbs_ref[g] = dbs_ref[g] + jnp.sum(ds, axis=1, keepdims=True)
            dvln = _dot(wsm_ref[g], ds_b, TN)
            dlg_ref[:, cs] = dlg_ref[:, cs] + jnp.sum(dvln * vhat, axis=0, keepdims=True)
            dlb_ref[:, cs] = dlb_ref[:, cs] + jnp.sum(dvln, axis=0, keepdims=True)
            dvh = dvln * gain
            dvh_ref[:, cs] = dvh
            m1 = m1 + jnp.sum(dvh, axis=1, keepdims=True)
            m2 = m2 + jnp.sum(dvh * vhat, axis=1, keepdims=True)
        m1 = m1 / di
        m2 = m2 / di
        for g in range(SG_GROUPS):
            cs = slice(g * gd, (g + 1) * gd)
            vs = slice(di + g * gd, di + (g + 1) * gd)
            vhat = (vg_ref[:, cs] - mu) * rstd
            dvg = rstd * (dvh_ref[:, cs] - m1 - vhat * m2)
            dp_ref[:, vs] = (dvg * dgel_ref[:, cs]).astype(BF16)

        after()

    full = lambda shape: pl.BlockSpec(shape, lambda i: (0,) * len(shape))
    return _call(
        body, name=name, grid=(n_i,),
        out_shape=[jax.ShapeDtypeStruct((m_rows, n3), BF16),
                   jax.ShapeDtypeStruct((SG_GROUPS, SG_BLOCK, SG_BLOCK), F32),
                   jax.ShapeDtypeStruct((SG_GROUPS, SG_BLOCK, 1), F32),
                   jax.ShapeDtypeStruct((1, di), F32), jax.ShapeDtypeStruct((1, di), F32)],
        in_specs=[pl.BlockSpec((SG_BLOCK, n3), lambda i: (i, 0)),
                  pl.BlockSpec((SG_BLOCK, di), lambda i: (i, 0)),
                  full((1, di)), full((1, di)),
                  full((SG_GROUPS, SG_BLOCK, SG_BLOCK)), full((SG_GROUPS, SG_BLOCK, 1))],
        out_specs=[pl.BlockSpec((SG_BLOCK, n3), lambda i: (i, 0)),
                   full((SG_GROUPS, SG_BLOCK, SG_BLOCK)), full((SG_GROUPS, SG_BLOCK, 1)),
                   full((1, di)), full((1, di))],
        scratch_shapes=[pltpu.VMEM((SG_GROUPS, SG_BLOCK, SG_BLOCK), BF16),
                        pltpu.VMEM((SG_BLOCK, di), F32), pltpu.VMEM((SG_BLOCK, di), F32),
                        pltpu.VMEM((SG_BLOCK, di), F32)],
        compiler_params=_params(),
    )(proj, dy, ln_gain, ln_bias, ws, bs)


def _lower_bound(lbraw):
    mx = jnp.maximum(lbraw[0:1, :], lbraw[1:2, :])
    e0 = jnp.exp(lbraw[0:1, :] - mx)
    e1 = jnp.exp(lbraw[1:2, :] - mx)
    p0 = e0 / (e0 + e1)
    p1 = e1 / (e0 + e1)
    return (p0 + p1) - p0, p0, p1


def _tri(lower):
    r = lax.broadcasted_iota(jnp.int32, (CHUNK, CHUNK), 0)
    c = lax.broadcasted_iota(jnp.int32, (CHUNK, CHUNK), 1)
    return ((r >= c) if lower else (c >= r)).astype(F32)


def _row(a, idx):
    r = lax.broadcasted_iota(jnp.int32, a.shape, 0)
    return jnp.sum(jnp.where(r == idx, a, 0.0), axis=0, keepdims=True)


def _hgrn_gates(qp, fp, lb, tri):
    sgm = _sigmoid_small(fp)
    f = lb + (1.0 - lb) * sgm
    k = 1.0 - f
    a = _dot(tri, jnp.log(f), NN, precision=lax.Precision.HIGHEST)
    a_mid = _row(a, CHUNK // 2 - 1)
    a_last = _row(a, CHUNK - 1)
    q, dq = _silu_and_grad(qp)
    e1, e2, e3, e4 = jnp.exp(a - a_mid), jnp.exp(a_mid - a), jnp.exp(a), jnp.exp(a_last - a)
    return dict(sgm=sgm, f=f, k=k, q=q, dq=dq, e1=e1, e2=e2, e3=e3, e4=e4, dec=jnp.exp(a_last),
                q_in=q * e1, k_in=k * e2, q_out=q * e3, k_out=k * e4)


def _causal():
    r = lax.broadcasted_iota(jnp.int32, (CHUNK, CHUNK), 0)
    c = lax.broadcasted_iota(jnp.int32, (CHUNK, CHUNK), 1)
    return r >= c


def hgrn_fwd(proj4, lbraw, gn, seq, name):
    _, m_rows, di = proj4.shape
    nb, nh, nc = m_rows // seq, di // HEAD_DIM, seq // CHUNK
    rows = min(HG_ROWS, seq)
    wide = HG_WIDE * HEAD_DIM
    ns, cpb = seq // rows, rows // CHUNK

    def body(p_ref, lb_ref, gn_ref, y_ref, sts_ref, st_ref):
        @pl.when(_pid(2) == 0)
        def _():
            st_ref[...] = jnp.zeros_like(st_ref)

        tri = _tri(True)
        causal = _causal()
        gain = gn_ref[...]
        lbs = [_lower_bound(lb_ref[:, j * HEAD_DIM:(j + 1) * HEAD_DIM])[0] for j in range(HG_WIDE)]

        units = [(n, j) for n in range(cpb) for j in range(HG_WIDE)]
        rs = lambda n: slice(n * CHUNK, (n + 1) * CHUNK)
        cs = lambda j: slice(j * HEAD_DIM, (j + 1) * HEAD_DIM)
        gates, v_b, sc_b, kv, o_in, o_x = {}, {}, {}, {}, {}, {}
        for n, j in units:
            gates[n, j] = _hgrn_gates(p_ref[0, rs(n), cs(j)], p_ref[1, rs(n), cs(j)], lbs[j], tri)
            v_b[n, j] = p_ref[2, rs(n), cs(j)].astype(BF16)
        for u in units:
            t = gates[u]
            sc_b[u] = jnp.where(causal, _dot(t["q_in"].astype(BF16), t["k_in"].astype(BF16), NT), 0.0).astype(BF16)
            kv[u] = _dot(v_b[u], t["k_out"].astype(BF16), TN)
        for u in units:
            o_in[u] = _dot(sc_b[u], v_b[u], NN)
        for j in range(HG_WIDE):
            st = st_ref[j]
            for n in range(cpb):
                sts_ref[n, :, cs(j)] = st
                o_x[n, j] = _dot(gates[n, j]["q_out"].astype(BF16), st.astype(BF16), NT)
                st = st * gates[n, j]["dec"] + kv[n, j]
            st_ref[j] = st
        for n, j in units:
            o = o_in[n, j] + o_x[n, j]
            r = lax.rsqrt(jnp.mean(o * o, axis=-1, keepdims=True) + EPS)
            gp = p_ref[3, rs(n), cs(j)]
            y_ref[rs(n), cs(j)] = ((o * r * gain) * (gp * _sigmoid(gp))).astype(BF16)

    return _call(
        body, name=name, grid=(nh // HG_WIDE, nb, ns),
        out_shape=[jax.ShapeDtypeStruct((m_rows, di), BF16),
                   jax.ShapeDtypeStruct((nb * nc, HEAD_DIM, di), F32)],
        in_specs=[pl.BlockSpec((4, rows, wide), lambda hg, b, s: (0, b * ns + s, hg)),
                  pl.BlockSpec((2, wide), lambda hg, b, s: (0, hg)),
                  pl.BlockSpec((1, HEAD_DIM), lambda hg, b, s: (0, 0))],
        out_specs=[pl.BlockSpec((rows, wide), lambda hg, b, s: (b * ns + s, hg)),
                   pl.BlockSpec((cpb, HEAD_DIM, wide), lambda hg, b, s: (b * ns + s, 0, hg))],
        scratch_shapes=[pltpu.VMEM((HG_WIDE, HEAD_DIM, HEAD_DIM), F32)],
        compiler_params=_params(),
    )(proj4, lbraw, gn)


def hgrn_bwd(proj4, dy, sts, lbraw, gn, seq, name):
    _, m_rows, di = proj4.shape
    nb, nh, nc = m_rows // seq, di // HEAD_DIM, seq // CHUNK
    rows = min(HG_ROWS, seq)
    wide = HG_WIDE * HEAD_DIM
    ns, cpb = seq // rows, rows // CHUNK
    n_hg = nh // HG_WIDE

    def body(p_ref, dy_ref, sts_ref, lb_ref, gn_ref, dp_ref, dlb_ref, dgn_ref, dst_ref, lbacc_ref, gnacc_ref):
        hg, b, s = _pid(0), _pid(1), _pid(2)
        tri, triu = _tri(True), _tri(False)
        causal = _causal()
        gain = gn_ref[...]
        first = (b == 0) & (s == 0)
        cs = lambda j: slice(j * HEAD_DIM, (j + 1) * HEAD_DIM)

        def before():
            @pl.when((hg == 0) & first)
            def _():
                gnacc_ref[...] = jnp.zeros_like(gnacc_ref)

            @pl.when(first)
            def _():
                lbacc_ref[...] = jnp.zeros_like(lbacc_ref)

            @pl.when(s == 0)
            def _():
                dst_ref[...] = jnp.zeros_like(dst_ref)

        def after():
            @pl.when((b == nb - 1) & (s == ns - 1))
            def _():
                for j in range(HG_WIDE):
                    _, p0, p1 = _lower_bound(lb_ref[:, cs(j)])
                    acc = lbacc_ref[:, cs(j)]
                    dlb_ref[0:1, cs(j)] = -acc * p0 * p1
                    dlb_ref[1:2, cs(j)] = acc * p1 * (1.0 - p1)

            @pl.when((hg == n_hg - 1) & (b == nb - 1) & (s == ns - 1))
            def _():
                tot = gnacc_ref[:, 0:HEAD_DIM]
                for j in range(1, HG_WIDE):
                    tot = tot + gnacc_ref[:, cs(j)]
                dgn_ref[...] = tot

        before()

        units = [(n, j) for n in range(cpb) for j in range(HG_WIDE)]
        rs = lambda n: slice(n * CHUNK, (n + 1) * CHUNK)
        lbs = [_lower_bound(lb_ref[:, cs(j)])[0] for j in range(HG_WIDE)]
        gates, v_b, st_b, sc_b, o, do_b = {}, {}, {}, {}, {}, {}
        dq_out, dsc_b, dv, g_st, dq_in, dk_in, dst_at, dk_out, ddec = {}, {}, {}, {}, {}, {}, {}, {}, {}
        for n, j in units:
            gates[n, j] = _hgrn_gates(p_ref[0, rs(n), cs(j)], p_ref[1, rs(n), cs(j)], lbs[j], tri)
            v_b[n, j] = p_ref[2, rs(n), cs(j)].astype(BF16)
            st_b[n, j] = sts_ref[n, :, cs(j)].astype(BF16)
        for u in units:
            t = gates[u]
            sc_b[u] = jnp.where(causal, _dot(t["q_in"].astype(BF16), t["k_in"].astype(BF16), NT), 0.0).astype(BF16)
        for u in units:
            o[u] = _dot(sc_b[u], v_b[u], NN) + _dot(gates[u]["q_out"].astype(BF16), st_b[u], NT)
        for n, j in units:
            ov = o[n, j]
            r = lax.rsqrt(jnp.mean(ov * ov, axis=-1, keepdims=True) + EPS)
            ohat = ov * r
            sg, dsg = _silu_and_grad(p_ref[3, rs(n), cs(j)])
            dyv = dy_ref[rs(n), cs(j)].astype(F32)
            dp_ref[3, rs(n), cs(j)] = (dyv * (ohat * gain) * dsg).astype(BF16)
            d_on = dyv * sg
            gnacc_ref[:, cs(j)] = gnacc_ref[:, cs(j)] + jnp.sum(d_on * ohat, axis=0, keepdims=True)
            dohat = d_on * gain
            do_b[n, j] = (r * (dohat - ohat * jnp.mean(dohat * ohat, axis=-1, keepdims=True))).astype(BF16)
        for u in units:
            dq_out[u] = _dot(do_b[u], st_b[u], NN)
            dsc_b[u] = jnp.where(causal, _dot(do_b[u], v_b[u], NT), 0.0).astype(BF16)
            dv[u] = _dot(sc_b[u], do_b[u], TN)
            g_st[u] = _dot(do_b[u], gates[u]["q_out"].astype(BF16), TN)
        for u in units:
            dq_in[u] = _dot(dsc_b[u], gates[u]["k_in"].astype(BF16), NN)
            dk_in[u] = _dot(dsc_b[u], gates[u]["q_in"].astype(BF16), TN)
        for j in range(HG_WIDE):
            dst = dst_ref[j]
            for n in reversed(range(cpb)):
                dst_at[n, j] = dst
                dst = dst * gates[n, j]["dec"] + g_st[n, j]
            dst_ref[j] = dst
        for n, j in units:
            dst = dst_at[n, j]
            dst_b = dst.astype(BF16)
            dk_out[n, j] = _dot(v_b[n, j], dst_b, NN)
            dv[n, j] = dv[n, j] + _dot(gates[n, j]["k_out"].astype(BF16), dst_b, NT)
            ddec[n, j] = jnp.sum(dst * sts_ref[n, :, cs(j)], axis=0, keepdims=True)
        for n, j in units:
            t = gates[n, j]
            dp_ref[2, rs(n), cs(j)] = dv[n, j].astype(BF16)
            dq = dq_in[n, j] * t["e1"] + dq_out[n, j] * t["e3"]
            dk = dk_in[n, j] * t["e2"] + dk_out[n, j] * t["e4"]
            w_in = dq_in[n, j] * t["q_in"] - dk_in[n, j] * t["k_in"]
            w_out = dk_out[n, j] * t["k_out"]
            da = w_in + dq_out[n, j] * t["q_out"] - w_out
            da_mid = -jnp.sum(w_in, axis=0, keepdims=True)
            da_last = jnp.sum(w_out, axis=0, keepdims=True) + ddec[n, j] * t["dec"]
            rid = lax.broadcasted_iota(jnp.int32, da.shape, 0)
            da = da + jnp.where(rid == CHUNK // 2 - 1, da_mid, 0.0) + jnp.where(rid == CHUNK - 1, da_last, 0.0)
            dlf = _dot(triu, da, NN, precision=lax.Precision.HIGHEST)
            df = dlf / t["f"] - dk
            sgm = t["sgm"]
            dp_ref[1, rs(n), cs(j)] = (df * (1.0 - lbs[j]) * sgm * (1.0 - sgm)).astype(BF16)
            lbacc_ref[:, cs(j)] = lbacc_ref[:, cs(j)] + jnp.sum(df * (1.0 - sgm), axis=0, keepdims=True)
            dp_ref[0, rs(n), cs(j)] = (dq * t["dq"]).astype(BF16)

        after()

    blk = lambda hg, b, s: b * ns + (ns - 1 - s)
    return _call(
        body, name=name, grid=(n_hg, nb, ns),
        out_shape=[jax.ShapeDtypeStruct((4, m_rows, di), BF16), jax.ShapeDtypeStruct((2, di), F32),
                   jax.ShapeDtypeStruct((1, HEAD_DIM), F32)],
        in_specs=[pl.BlockSpec((4, rows, wide), lambda hg, b, s: (0, blk(hg, b, s), hg)),
                  pl.BlockSpec((rows, wide), lambda hg, b, s: (blk(hg, b, s), hg)),
                  pl.BlockSpec((cpb, HEAD_DIM, wide), lambda hg, b, s: (blk(hg, b, s), 0, hg)),
                  pl.BlockSpec((2, wide), lambda hg, b, s: (0, hg)),
                  pl.BlockSpec((1, HEAD_DIM), lambda hg, b, s: (0, 0))],
        out_specs=[pl.BlockSpec((4, rows, wide), lambda hg, b, s: (0, blk(hg, b, s), hg)),
                   pl.BlockSpec((2, wide), lambda hg, b, s: (0, hg)),
                   pl.BlockSpec((1, HEAD_DIM), lambda hg, b, s: (0, 0))],
        scratch_shapes=[pltpu.VMEM((HG_WIDE, HEAD_DIM, HEAD_DIM), F32), pltpu.VMEM((1, wide), F32),
                        pltpu.VMEM((1, wide), F32)],
        compiler_params=_params(),
    )(proj4, dy, sts, lbraw, gn)


def outproj_loss(y, w, x, mod, fg, target, seq, name):
    m_rows, di = y.shape
    d = w.shape[1]
    tm = min(512, seq)

    def body(y_ref, w_ref, x_ref, mod_ref, fg_ref, t_ref, out_ref, loss_ref, dx_ref, dfg_ref):
        i = _pid(0)
        acc = _dot(y_ref[...], w_ref[...], NN)
        out_ref[...] = acc.astype(BF16)
        xv = x_ref[...] + mod_ref[0][:, 2 * d:] * acc
        gain = fg_ref[...]
        r = lax.rsqrt(jnp.mean(xv * xv, axis=-1, keepdims=True) + EPS)
        xn = xv * r
        e = xn * gain - t_ref[...]
        part = 0.5 * jnp.sum(jnp.mean(e * e, axis=-1, keepdims=True), axis=0, keepdims=True)
        dyv = e / d
        p_fg = jnp.sum(dyv * xn, axis=0, keepdims=True)
        dxn = dyv * gain
        dx_ref[...] = r * (dxn - xn * jnp.mean(dxn * xn, axis=-1, keepdims=True))

        @pl.when(i == 0)
        def _():
            loss_ref[...] = part
            dfg_ref[...] = p_fg

        @pl.when(i != 0)
        def _():
            loss_ref[...] = loss_ref[...] + part
            dfg_ref[...] = dfg_ref[...] + p_fg

    row = pl.BlockSpec((tm, d), lambda i: (i, 0))
    return _call(
        body, name=name, grid=(m_rows // tm,),
        out_shape=[jax.ShapeDtypeStruct((m_rows, d), BF16), jax.ShapeDtypeStruct((1, 1), F32),
                   jax.ShapeDtypeStruct((m_rows, d), F32), jax.ShapeDtypeStruct((1, d), F32)],
        in_specs=[pl.BlockSpec((tm, di), lambda i: (i, 0)),
                  pl.BlockSpec((di, d), lambda i: (0, 0)),
                  row,
                  pl.BlockSpec((1, 1, 3 * d), lambda i: ((i * tm) // seq, 0, 0)),
                  pl.BlockSpec((1, d), lambda i: (0, 0)), row],
        out_specs=[row, pl.BlockSpec((1, 1), lambda i: (0, 0)), row, pl.BlockSpec((1, d), lambda i: (0, 0))],
        compiler_params=_params(),
    )(y, w, x, mod, fg, target)


def _pack(parts):
    flat = jnp.concatenate([p.reshape(-1) for p in parts])
    pad = (-flat.shape[0]) % (8 * LANES)
    return jnp.pad(flat, (0, pad)).reshape(-1, LANES)


def kernel(x, c, norm_gain, w_ada, b_ada, a_w_in, a_ln_gain, a_ln_bias, a_w_s, a_b_s, a_w_out, b_w_in, b_lower_bounds, b_gn_gain, b_w_out, final_gain, loss_target, m_norm_gain, m_w_ada, m_b_ada, m_a_w_in, m_a_ln_gain, m_a_ln_bias, m_a_w_s, m_a_b_s, m_a_w_out, m_b_w_in, m_b_lower_bounds, m_b_gn_gain, m_b_w_out, m_final_gain, v_norm_gain, v_w_ada, v_b_ada, v_a_w_in, v_a_ln_gain, v_a_ln_bias, v_a_w_s, v_a_b_s, v_a_w_out, v_b_w_in, v_b_lower_bounds, v_b_gn_gain, v_b_w_out, v_final_gain):
    nb, seq, d = x.shape
    m_rows = nb * seq
    n_l = w_ada.shape[0]
    ada_cols = w_ada.shape[2]
    px, py, pc = _place()
    chip = 2 * px + py
    dev = 2 * chip + pc

    c_all = allgather_small(c.reshape(-1, LANES), "gather_c").reshape(N_DEV * nb, d)
    b_cols = lax.dynamic_slice_in_dim(b_ada, chip * ada_cols, ada_cols, axis=1).reshape(n_l, 1, ada_cols)
    mod_cols = ada_fwd(c_all, w_ada, b_cols, "ada_fwd")
    mod_g = allgather_small(mod_cols.reshape(-1, LANES), "gather_mod")
    mod_g = mod_g.reshape(N_CHIPS, 2, n_l, N_DEV * nb, ada_cols)[:, 0]
    mod_all = jnp.transpose(mod_g, (1, 2, 0, 3)).reshape(n_l, N_DEV * nb, 3 * d)
    mod_mine = lax.dynamic_slice_in_dim(mod_all, dev * nb, nb, axis=1)
    mod0 = mod_mine[0].reshape(nb, 1, 3 * d)
    mod1 = mod_mine[1].reshape(nb, 1, 3 * d)

    (wa_in, wa_out), tok_a = gather_inplace(
        [cast_into_slot(a_w_in[0], chip, mod_mine, "cast_a_in"), cast_into_slot(a_w_out[0], chip, mod_mine, "cast_a_out")],
        "gather_a")
    s_bi = gather_start(cast_into_slot(b_w_in[0], chip, tok_a, "cast_b_in"), "gather_b_in_start")
    s_bo = gather_start(cast_into_slot(b_w_out[0], chip, s_bi[3], "cast_b_out"), "gather_b_out_start")
    di = a_w_out.shape[1] * N_CHIPS
    wa_out = wa_out.reshape(di, d)

    x0 = x.reshape(m_rows, d)
    tgt = loss_target.reshape(m_rows, d)
    ng0 = norm_gain[0:1] + (s_bi[3][0, 0] + s_bo[3][0, 0])
    ng1 = norm_gain[1:2]
    bs_col = a_b_s[0].reshape(SG_GROUPS, SG_BLOCK, 1)
    proj_a, h_a = inproj_fwd(x0, mod0, ng0, wa_in, seq, False, "a_inproj")
    y_a = sgu_fwd(proj_a, a_ln_gain, a_ln_bias, a_w_s[0], bs_col, "a_sgu")
    x1, out_a = outproj_fwd(y_a, wa_out, x0, mod0, seq, "a_outproj")
    wb_in = gather_wait(*s_bi[:3], out_a, "gather_b_in_wait")
    proj_b, h_b = inproj_fwd(x1, mod1, ng1, wb_in, seq, True, "b_inproj")
    y_b, sts_b = hgrn_fwd(proj_b, b_lower_bounds, b_gn_gain, seq, "b_hgrn")
    wb_out = gather_wait(*s_bo[:3], y_b, "gather_b_out_wait").reshape(di, d)
    out_b, loss_part, dx2, dfg = outproj_loss(
        y_b, wb_out, x1, mod1, final_gain.reshape(1, d), tgt, seq, "b_outproj_loss")

    shard_rows = di // N_CHIPS
    dy_b, dout_b, dgate1 = outproj_bwd(dx2, out_b, mod1, wb_out, seq, "b_outproj_bwd")
    gwb_out = grad_w_out(y_b, dout_b, "b_grad_w_out").reshape(N_CHIPS, shard_rows, d)
    e_bo = exchange_start(gwb_out, "exchange_b_out_start")
    dproj_b, dlb, dgn = hgrn_bwd(
        proj_b, dy_b, sts_b, b_lower_bounds, b_gn_gain + e_bo[4][0, 0], seq, "b_hgrn_bwd")
    e_bi = exchange_start(grad_w_in(h_b, dproj_b, N_CHIPS, True, "b_grad_w_in"), "exchange_b_in_start")
    dx1, dshift1, dscale1, dng1 = inproj_bwd(
        dproj_b, wb_in, x1, dx2, mod1, ng1 + e_bi[4][0, 0], seq, True, "b_inproj_bwd")

    dy_a, dout_a, dgate0 = outproj_bwd(dx1, out_a, mod0, wa_out, seq, "a_outproj_bwd")
    gwa_out = grad_w_out(y_a, dout_a, "a_grad_w_out").reshape(N_CHIPS, shard_rows, d)
    e_ao = exchange_start(gwa_out, "exchange_a_out_start")
    dproj_a, dws, dbs, dlg, dlbias = sgu_bwd(
        proj_a, dy_a, a_ln_gain + e_ao[4][0, 0], a_ln_bias, a_w_s[0], bs_col, "a_sgu_bwd")
    def landed(ex, after, nm):
        parts_thru, land = exchange_wait(ex[0], ex[1], ex[2], ex[3], after, "exchange_" + nm + "_wait")
        return sum_parts(parts_thru, land, chip, "sum_" + nm)

    sum_bo = landed(e_bo, dproj_a, "b_out")
    sum_bi = landed(e_bi, sum_bo, "b_in")
    other_bo, other_bi = swap_sibling([sum_bo, sum_bi], "swap_b")
    gw_part = _deferred(grad_w_in, h_a, dproj_a, N_CHIPS, False, "a_grad_w_in")
    up_part = _deferred(adamw_pair, sum_bi, other_bi, b_w_in[0], m_b_w_in[0], v_b_w_in[0], "adamw_b_in",
                        steps=math.prod(gw_part.kw["grid"]))
    (gwa_in,), upd_bi = fused([gw_part, up_part], "a_grad_w_in_adamw_b_in")
    gb_in, db_in, mb_in, vb_in = [r.reshape(b_w_in.shape) for r in upd_bi]
    kept_ai = add_kept_half(gwa_in, send_other_half(gwa_in, "halves_a_in"), pc, "presum_a_in")
    e_ai = exchange_start(kept_ai, "exchange_a_in_start")
    sum_ao = landed(e_ao, e_ai[4], "a_out")
    (other_ao,) = swap_sibling([sum_ao], "swap_a_out")
    ib_part = _deferred(inproj_bwd, dproj_a, wa_in, x0, dx1, mod0, norm_gain[0:1], seq, False, "a_inproj_bwd")
    ib_steps = math.prod(ib_part.kw["grid"])
    up_bo = _deferred(adamw_pair, sum_bo, other_bo, b_w_out[0], m_b_w_out[0], v_b_w_out[0], "adamw_b_out",
                      steps=ib_steps)
    up_ao = _deferred(adamw_pair, sum_ao, other_ao, a_w_out[0], m_a_w_out[0], v_a_w_out[0], "adamw_a_out",
                      steps=ib_steps)
    (dx0, dshift0, dscale0, dng0), upd_bo, upd_ao = fused([ib_part, up_bo, up_ao], "a_inproj_bwd_adamw_out")
    gb_out, db_out, mb_out, vb_out = [r.reshape(b_w_out.shape) for r in upd_bo]
    ga_out, da_out, ma_out, va_out = [r.reshape(a_w_out.shape) for r in upd_ao]
    grad_x = dx0.reshape(nb, seq, d)

    dmod = jnp.concatenate([dshift0, dscale0, dgate0, dshift1, dscale1, dgate1], axis=2)
    n_dmod = dmod.size
    small_g = [jnp.concatenate([dng0, dng1], axis=0), dlg, dlbias, dws, dbs, dlb, dfg, dgn]
    packed_g = _pack([dmod] + small_g + [loss_part])
    rows = packed_g.shape[0]
    s_small = gather_all_start(
        lax.dynamic_update_slice(jnp.zeros((N_DEV, rows, LANES), F32), packed_g[None], (dev, 0, 0)),
        "gather_small_start")

    sum_ai = landed(e_ai, s_small[3], "a_in")
    (other_ai,) = swap_sibling([sum_ai], "swap_a_in")
    ga_in, da_in, ma_in, va_in = [r.reshape(a_w_in.shape) for r in adamw_halves(
        sum_ai, other_ai, pc, a_w_in[0], m_a_w_in[0], v_a_w_in[0], "adamw_a_in")]

    small_w = [norm_gain, a_ln_gain, a_ln_bias, a_w_s, a_b_s, b_lower_bounds, final_gain, b_gn_gain]
    small_m = [m_norm_gain, m_a_ln_gain, m_a_ln_bias, m_a_w_s, m_a_b_s, m_b_lower_bounds, m_final_gain, m_b_gn_gain]
    small_v = [v_norm_gain, v_a_ln_gain, v_a_ln_bias, v_a_w_s, v_a_b_s, v_b_lower_bounds, v_final_gain, v_b_gn_gain]
    rows_of = lambda a: a.reshape(-1, a.shape[-1])
    gathered = gather_all_wait(s_small[0], s_small[1], s_small[2], ga_in, "gather_small_wait")
    tail, small_res = small_update(
        gathered, n_dmod // LANES, [rows_of(a) for a in small_w], [rows_of(a) for a in small_m],
        [rows_of(a) for a in small_v], "small_update")
    loss = tail[0, 0]
    sg, sd, sm, sv = [[small_res[p][kind].reshape(w.shape) for p, w in enumerate(small_w)] for kind in range(4)]

    dmod_all = gathered[:, :n_dmod // LANES].reshape(N_DEV * nb, n_l, 3 * d)
    dmod_cols = lax.dynamic_slice_in_dim(dmod_all, chip * ada_cols, ada_cols, axis=2)
    dmod_cols = jnp.transpose(dmod_cols, (1, 0, 2))
    g_wada, d_wada, m_wada, v_wada = ada_bwd(c_all, dmod_cols, w_ada, m_w_ada, v_w_ada, "ada_bwd")
    flat = lambda a: a.reshape(1, -1)
    g_bada, d_bada, m_bada, v_bada = [
        r.reshape(b_ada.shape) for r in
        bias_update(dmod_all.reshape(N_DEV * nb, n_l * 3 * d), flat(b_ada), flat(m_b_ada), flat(v_b_ada), "bias_update")]

    def order(ng, wada, bada, ain, sm_rest, aout, bin_, bout):
        lg, lbi, ws_, bs_, lbd, fg_, gn_ = sm_rest
        return [ng, wada, bada, ain, lg, lbi, ws_, bs_, aout, bin_, lbd, gn_, bout, fg_]

    grads = order(sg[0], g_wada, g_bada, ga_in, sg[1:8], ga_out, gb_in, gb_out)
    deltas = order(sd[0], d_wada, d_bada, da_in, sd[1:8], da_out, db_in, db_out)
    new_m = order(sm[0], m_wada, m_bada, ma_in, sm[1:8], ma_out, mb_in, mb_out)
    new_v = order(sv[0], v_wada, v_bada, va_in, sv[1:8], va_out, vb_in, vb_out)
    return (loss, grad_x, *grads, *deltas, *new_m, *new_v)
```

```python
import jax
import jax.numpy as jnp
from jax import lax
from jax.experimental import pallas as pl
from jax.experimental.pallas import tpu as pltpu

F32 = jnp.float32
BF16 = jnp.bfloat16
EPS = 1e-6
CHUNK = 64
SG_BLOCK = 128
SG_GROUPS = 8
HEAD_DIM = 128
HG_WIDE = 8
HG_ROWS = 256
N_CHIPS = 4
N_DEV = 8
GATHER_PIECES = 4
LANES = 128
ADAM_LR = 0.001
ADAM_B1 = 0.9
ADAM_B2 = 0.999
ADAM_EPS = 1e-08
ADAM_WD = 0.01
ADAM_STEP = 10
GELU_C0 = 0.7978845608028654
GELU_C1 = 0.044715
MESH = pl.DeviceIdType.MESH
VMEM_LIMIT = 56 * 1024 * 1024


ROW_TILE = 1024


def _col_tile(n):
    return next(t for t in (1024, 768, 512, 256) if n % t == 0)


def _call(body, **kw):
    return pl.pallas_call(body, **kw)


def _params(**kw):
    return pltpu.CompilerParams(vmem_limit_bytes=VMEM_LIMIT, **kw)


def _sigmoid(x):
    return 0.5 * jnp.tanh(0.5 * x) + 0.5


def _sigmoid_small(x):
    return 1.0 / (1.0 + jnp.exp(-x))


def _silu_and_grad(x):
    s = _sigmoid(x)
    return x * s, s * (1.0 + x * (1.0 - s))


def _gelu(x):
    return 0.5 * x * (1.0 + jnp.tanh(GELU_C0 * (x + GELU_C1 * x * x * x)))


def _gelu_and_grad(x):
    t = jnp.tanh(GELU_C0 * (x + GELU_C1 * x * x * x))
    g = 0.5 * x * (1.0 + t)
    dg = 0.5 * (1.0 + t) + 0.5 * x * (1.0 - t * t) * (GELU_C0 * (1.0 + 3.0 * GELU_C1 * x * x))
    return g, dg


def _dot(a, b, dims, precision=None):
    return lax.dot_general(a, b, (dims, ((), ())), precision=precision, preferred_element_type=F32)


NN = ((1,), (0,))
NT = ((1,), (1,))
TN = ((0,), (0,))


def _adamw(w, g, m, v):
    m = ADAM_B1 * m + (1.0 - ADAM_B1) * g
    v = ADAM_B2 * v + (1.0 - ADAM_B2) * (g * g)
    m_hat = m / (1.0 - ADAM_B1 ** ADAM_STEP)
    v_hat = v / (1.0 - ADAM_B2 ** ADAM_STEP)
    delta = -ADAM_LR * (m_hat / (jnp.sqrt(v_hat) + ADAM_EPS) + ADAM_WD * w)
    return delta, m, v


def _chunk_mask():
    r = lax.broadcasted_iota(jnp.int32, (SG_BLOCK, SG_BLOCK), 0)
    c = lax.broadcasted_iota(jnp.int32, (SG_BLOCK, SG_BLOCK), 1)
    return (c // CHUNK) <= (r // CHUNK)


def _place():
    return lax.axis_index("x"), lax.axis_index("y"), lax.axis_index("c")


def _other_chips(x, y):
    return [(1 - x, y), (x, 1 - y), (1 - x, 1 - y)]


def allgather_small(v, name):
    m_per, n = v.shape

    def body(x_ref, out_ref, send_sems, recv_sems, local_sem):
        x, y, c = _place()
        me, sibling = (x, y, c), (x, y, 1 - c)
        chips = _other_chips(x, y)

        def rows(px, py, pc):
            return out_ref.at[pl.ds((4 * px + 2 * py + pc) * m_per, m_per), :]

        def copy(k, block, to, src=None):
            return pltpu.make_async_remote_copy(
                src_ref=rows(*block) if src is None else src, dst_ref=rows(*block),
                send_sem=send_sems.at[k], recv_sem=recv_sems.at[k], device_id=to, device_id_type=MESH)

        mine = pltpu.make_async_copy(x_ref, rows(*me), local_sem)
        mine.start()
        first = [copy(0, me, sibling, src=x_ref)]
        first += [copy(1 + j, me, (*chip, c), src=x_ref) for j, chip in enumerate(chips)]
        for cp in first:
            cp.start()
        passed = [copy(4 + j, (*chip, c), sibling) for j, chip in enumerate(chips)]
        for j, chip in enumerate(chips):
            copy(1 + j, (*chip, c), me).wait_recv()
            passed[j].start()
        copy(0, sibling, me).wait_recv()
        for j, chip in enumerate(chips):
            copy(4 + j, (*chip, 1 - c), me).wait_recv()
        for cp in first + passed:
            cp.wait_send()
        mine.wait()

    return _call(
        body, name=name,
        out_shape=jax.ShapeDtypeStruct((N_DEV * m_per, n), v.dtype),
        in_specs=[pl.BlockSpec(memory_space=pltpu.VMEM)],
        out_specs=pl.BlockSpec(memory_space=pltpu.VMEM),
        scratch_shapes=[pltpu.SemaphoreType.DMA((7,)), pltpu.SemaphoreType.DMA((7,)), pltpu.SemaphoreType.DMA],
    )(v)


def _hbm_spec():
    return pl.BlockSpec(memory_space=pltpu.HBM)


def _sem_spec():
    return pl.BlockSpec(memory_space=pltpu.SEMAPHORE)


def _split_params():
    return pltpu.CompilerParams(has_side_effects=pltpu.SideEffectType.DATAFLOW_SIDE_EFFECTING)


def _hbm(a):
    return pltpu.with_memory_space_constraint(a, pltpu.HBM)


def gather_inplace(lands, name):
    n = len(lands)
    per_land = 6 * GATHER_PIECES

    def body(*refs):
        land_refs, token = refs[n:2 * n], refs[2 * n]
        send_sems, recv_sems = refs[2 * n + 1:]
        x, y, c = _place()
        chips = _other_chips(x, y)

        def copy(w, k, q, chip_idx, core_half, to):
            piece = lands[w].shape[1] // (2 * GATHER_PIECES)
            rows = land_refs[w].at[chip_idx, pl.ds((core_half * GATHER_PIECES + q) * piece, piece), :]
            sem = per_land * w + GATHER_PIECES * k + q
            return pltpu.make_async_remote_copy(
                src_ref=rows, dst_ref=rows, send_sem=send_sems.at[sem], recv_sem=recv_sems.at[sem],
                device_id=to, device_id_type=MESH)

        order = [(q, w, j, px, py) for q in range(GATHER_PIECES) for w in range(n) for j, (px, py) in enumerate(chips)]
        first = [copy(w, j, q, 2 * x + y, c, (px, py, c)) for q, w, j, px, py in order]
        for cp in first:
            cp.start()
        passed = []
        for q, w, j, px, py in order:
            copy(w, j, q, 2 * px + py, c, (px, py, c)).wait_recv()
            passed.append(copy(w, 3 + j, q, 2 * px + py, c, (x, y, 1 - c)))
            passed[-1].start()
        for q, w, j, px, py in order:
            copy(w, 3 + j, q, 2 * px + py, 1 - c, (x, y, 1 - c)).wait_recv()
        for cp in first + passed:
            cp.wait_send()
        token[...] = jnp.zeros_like(token)

    res = _call(
        body, name=name,
        out_shape=[jax.ShapeDtypeStruct(a.shape, a.dtype) for a in lands] + [jax.ShapeDtypeStruct((8, LANES), F32)],
        in_specs=[_hbm_spec()] * n, out_specs=[_hbm_spec()] * n + [pl.BlockSpec(memory_space=pltpu.VMEM)],
        input_output_aliases={w: w for w in range(n)},
        scratch_shapes=[pltpu.SemaphoreType.DMA((per_land * n,)), pltpu.SemaphoreType.DMA((per_land * n,))],
    )(*lands)
    return res[:n], res[n]


def gather_start(land, name):
    def body(land_ref, send_sems, recv_sems, land_thru, token):
        del land_thru
        x, y, c = _place()
        for j, (px, py) in enumerate(_other_chips(x, y)):
            pltpu.make_async_remote_copy(
                src_ref=land_ref.at[2 * x + y], dst_ref=land_ref.at[2 * x + y],
                send_sem=send_sems.at[j], recv_sem=recv_sems.at[j], device_id=(px, py, c),
                device_id_type=MESH).start()
        token[...] = jnp.zeros_like(token)

    return _call(
        body, name=name,
        out_shape=(pltpu.SemaphoreType.DMA((3,)), pltpu.SemaphoreType.DMA((3,)),
                   pltpu.HBM(land.shape, land.dtype), jax.ShapeDtypeStruct((8, LANES), F32)),
        in_specs=(_hbm_spec(),),
        out_specs=(_sem_spec(), _sem_spec(), _hbm_spec(), pl.BlockSpec(memory_space=pltpu.VMEM)),
        input_output_aliases={0: 2}, compiler_params=_split_params(),
    )(_hbm(land))


def gather_wait(send_sems, recv_sems, land, after, name):
    def body(land_ref, send_sems, recv_sems, after_ref, land_out):
        del after_ref, land_out
        x, y, c = _place()
        for j, (px, py) in enumerate(_other_chips(x, y)):
            cp = pltpu.make_async_remote_copy(
                src_ref=land_ref.at[2 * x + y], dst_ref=land_ref.at[2 * px + py],
                send_sem=send_sems.at[j], recv_sem=recv_sems.at[j], device_id=(px, py, c), device_id_type=MESH)
            cp.wait_send()
            cp.wait_recv()

    return _call(
        body, name=name,
        out_shape=pltpu.HBM(land.shape, land.dtype),
        in_specs=(_hbm_spec(), _sem_spec(), _sem_spec(), pl.BlockSpec(memory_space=pl.ANY)),
        out_specs=_hbm_spec(), input_output_aliases={0: 0}, compiler_params=_split_params(),
    )(land, send_sems, recv_sems, after)


def _flips():
    return [(fx, fy, fc) for fx in (0, 1) for fy in (0, 1) for fc in (0, 1) if (fx, fy, fc) != (0, 0, 0)]


def _flipped(x, y, c, flip):
    fx, fy, fc = flip
    return (1 - x if fx else x, 1 - y if fy else y, 1 - c if fc else c)


def gather_all_start(land, name):
    def body(land_ref, send_sems, recv_sems, land_thru, token):
        del land_thru
        x, y, c = _place()
        for k, flip in enumerate(_flips()):
            pltpu.make_async_remote_copy(
                src_ref=land_ref.at[4 * x + 2 * y + c], dst_ref=land_ref.at[4 * x + 2 * y + c],
                send_sem=send_sems.at[k], recv_sem=recv_sems.at[k], device_id=_flipped(x, y, c, flip),
                device_id_type=MESH).start()
        token[...] = jnp.zeros_like(token)

    return _call(
        body, name=name,
        out_shape=(pltpu.SemaphoreType.DMA((7,)), pltpu.SemaphoreType.DMA((7,)),
                   pltpu.HBM(land.shape, land.dtype), jax.ShapeDtypeStruct((8, LANES), F32)),
        in_specs=(_hbm_spec(),),
        out_specs=(_sem_spec(), _sem_spec(), _hbm_spec(), pl.BlockSpec(memory_space=pltpu.VMEM)),
        input_output_aliases={0: 2}, compiler_params=_split_params(),
    )(_hbm(land))


def gather_all_wait(send_sems, recv_sems, land, after, name):
    def body(land_ref, send_sems, recv_sems, after_ref, land_out):
        del after_ref, land_out
        x, y, c = _place()
        for k, flip in enumerate(_flips()):
            px, py, pc = _flipped(x, y, c, flip)
            cp = pltpu.make_async_remote_copy(
                src_ref=land_ref.at[4 * x + 2 * y + c], dst_ref=land_ref.at[4 * px + 2 * py + pc],
                send_sem=send_sems.at[k], recv_sem=recv_sems.at[k], device_id=(px, py, pc), device_id_type=MESH)
            cp.wait_send()
            cp.wait_recv()

    return _call(
        body, name=name,
        out_shape=pltpu.HBM(land.shape, land.dtype),
        in_specs=(_hbm_spec(), _sem_spec(), _sem_spec(), pl.BlockSpec(memory_space=pl.ANY)),
        out_specs=_hbm_spec(), input_output_aliases={0: 0}, compiler_params=_split_params(),
    )(land, send_sems, recv_sems, after)


def exchange_start(parts, name):
    _, r, c_ = parts.shape

    def body(parts_ref, land_ref, send_sems, recv_sems, parts_thru, land_thru, token):
        del parts_thru, land_thru
        x, y, c = _place()
        for j, (px, py) in enumerate(_other_chips(x, y)):
            pltpu.make_async_remote_copy(
                src_ref=parts_ref.at[2 * px + py], dst_ref=land_ref.at[j],
                send_sem=send_sems.at[j], recv_sem=recv_sems.at[j], device_id=(px, py, c),
                device_id_type=MESH).start()
        token[...] = jnp.zeros_like(token)

    return _call(
        body, name=name,
        out_shape=(pltpu.SemaphoreType.DMA((3,)), pltpu.SemaphoreType.DMA((3,)),
                   pltpu.HBM(parts.shape, parts.dtype), pltpu.HBM((3, r, c_), parts.dtype),
                   jax.ShapeDtypeStruct((8, LANES), F32)),
        in_specs=(_hbm_spec(), _hbm_spec()),
        out_specs=(_sem_spec(), _sem_spec(), _hbm_spec(), _hbm_spec(), pl.BlockSpec(memory_space=pltpu.VMEM)),
        input_output_aliases={0: 2, 1: 3}, compiler_params=_split_params(),
    )(_hbm(parts), _hbm(lax.empty((3, r, c_), parts.dtype)))


def exchange_wait(send_sems, recv_sems, parts, land, after, name):
    def body(parts_ref, land_ref, send_sems, recv_sems, after_ref, parts_out, land_out):
        del after_ref, parts_out, land_out
        x, y, c = _place()
        for j, (px, py) in enumerate(_other_chips(x, y)):
            cp = pltpu.make_async_remote_copy(
                src_ref=parts_ref.at[2 * px + py], dst_ref=land_ref.at[j],
                send_sem=send_sems.at[j], recv_sem=recv_sems.at[j], device_id=(px, py, c), device_id_type=MESH)
            cp.wait_send()
            cp.wait_recv()

    return _call(
        body, name=name,
        out_shape=(pltpu.HBM(parts.shape, parts.dtype), pltpu.HBM(land.shape, land.dtype)),
        in_specs=(_hbm_spec(), _hbm_spec(), _sem_spec(), _sem_spec(), pl.BlockSpec(memory_space=pl.ANY)),
        out_specs=(_hbm_spec(), _hbm_spec()), input_output_aliases={0: 0, 1: 1},
        compiler_params=_split_params(),
    )(parts, land, send_sems, recv_sems, after)


def cast_into_slot(w, chip, after, name):
    r, c = w.shape
    tr = min(256, r)

    def body(s_ref, w_ref, after_ref, o_ref):
        del s_ref, after_ref
        o_ref[...] = w_ref[...].astype(BF16)

    return _call(
        body, name=name,
        grid_spec=pltpu.PrefetchScalarGridSpec(
            num_scalar_prefetch=1, grid=(r // tr,),
            in_specs=[pl.BlockSpec((tr, c), lambda i, s: (i, 0)), pl.BlockSpec(memory_space=pl.ANY)],
            out_specs=pl.BlockSpec((None, tr, c), lambda i, s: (s[0], i, 0))),
        out_shape=jax.ShapeDtypeStruct((N_CHIPS, r, c), BF16),
        compiler_params=_params(),
    )(chip.reshape(1).astype(jnp.int32), w, after)


def sum_parts(parts, land, chip, name):
    _, r, c = parts.shape
    tr = min(256, r)

    def body(s_ref, p_ref, l_ref, o_ref):
        del s_ref
        acc = p_ref[...].astype(F32) + l_ref[0].astype(F32)
        acc = acc + l_ref[1].astype(F32)
        o_ref[...] = (acc + l_ref[2].astype(F32)).astype(BF16)

    return _call(
        body, name=name,
        grid_spec=pltpu.PrefetchScalarGridSpec(
            num_scalar_prefetch=1, grid=(r // tr,),
            in_specs=[pl.BlockSpec((None, tr, c), lambda i, s: (s[0], i, 0)),
                      pl.BlockSpec((3, tr, c), lambda i, s: (0, i, 0))],
            out_specs=pl.BlockSpec((tr, c), lambda i, s: (i, 0))),
        out_shape=jax.ShapeDtypeStruct((r, c), BF16),
        compiler_params=_params(),
    )(chip.reshape(1).astype(jnp.int32), parts, land)


def swap_sibling(arrs, name):
    n = len(arrs)

    def body(*refs):
        ins, outs = refs[:n], refs[n:2 * n]
        send_sems, recv_sems = refs[2 * n:]
        x, y, c = _place()
        cps = []
        for w in range(n):
            cp = pltpu.make_async_remote_copy(
                src_ref=ins[w], dst_ref=outs[w], send_sem=send_sems.at[w], recv_sem=recv_sems.at[w],
                device_id=(x, y, 1 - c), device_id_type=MESH)
            cp.start()
            cps.append(cp)
        for cp in cps:
            cp.wait_recv()
        for cp in cps:
            cp.wait_send()

    return _call(
        body, name=name,
        out_shape=[jax.ShapeDtypeStruct(a.shape, a.dtype) for a in arrs],
        in_specs=[_hbm_spec()] * n, out_specs=[_hbm_spec()] * n,
        scratch_shapes=[pltpu.SemaphoreType.DMA((n,)), pltpu.SemaphoreType.DMA((n,))],
    )(*arrs)


def adamw_pair(pa, pb, w, m, v, name):
    r, c = w.shape
    tr = min(128, r)

    def body(pa_ref, pb_ref, w_ref, m_ref, v_ref, g_ref, d_ref, nm_ref, nv_ref):
        g = pa_ref[...].astype(F32) + pb_ref[...].astype(F32)
        d, nm, nv = _adamw(w_ref[...], g, m_ref[...], v_ref[...])
        g_ref[...] = g
        d_ref[...] = d
        nm_ref[...] = nm
        nv_ref[...] = nv

    spec = pl.BlockSpec((tr, c), lambda i: (i, 0))
    return _call(
        body, name=name, grid=(r // tr,),
        out_shape=[jax.ShapeDtypeStruct((r, c), F32)] * 4,
        in_specs=[spec] * 5, out_specs=[spec] * 4,
        compiler_params=_params(),
    )(pa, pb, w, m, v)


def small_update(gathered, first_row, ws, ms, vs, name):
    n_w = len(ws)
    total_rows = gathered.shape[1]

    def body(*refs):
        g_ref = refs[0]
        w_refs, m_refs, v_refs = refs[1:1 + n_w], refs[1 + n_w:1 + 2 * n_w], refs[1 + 2 * n_w:1 + 3 * n_w]
        tail_ref = refs[1 + 3 * n_w]
        outs = refs[2 + 3 * n_w:2 + 7 * n_w]
        sum_ref = refs[2 + 7 * n_w]
        acc = g_ref[0]
        for k in range(1, N_DEV):
            acc = acc + g_ref[k]
        sum_ref[...] = acc
        row = first_row
        for p in range(n_w):
            a, b = ws[p].shape
            per = b // LANES
            g_out, d_out, m_out, v_out = outs[4 * p:4 * p + 4]
            if per == 1:
                g_out[...] = sum_ref[row:row + a, :]
            else:
                for i in range(a):
                    for jc in range(per):
                        g_out[i:i + 1, jc * LANES:(jc + 1) * LANES] = sum_ref[row + i * per + jc:row + i * per + jc + 1, :]
            row += a * per
            dl, nm, nv = _adamw(w_refs[p][...], g_out[...], m_refs[p][...], v_refs[p][...])
            d_out[...] = dl
            m_out[...] = nm
            v_out[...] = nv
        tail_ref[...] = sum_ref[row:row + 1, :]

    out_shape = [jax.ShapeDtypeStruct((1, LANES), F32)]
    for w in ws:
        out_shape += [jax.ShapeDtypeStruct(w.shape, F32)] * 4
    res = _call(
        body, name=name, out_shape=out_shape,
        scratch_shapes=[pltpu.VMEM((total_rows, LANES), F32)],
        compiler_params=_params(),
    )(gathered, *ws, *ms, *vs)
    return res[0], [res[1 + 4 * p:5 + 4 * p] for p in range(n_w)]


def ada_fwd(c_all, w_ada, b_cols, name):
    n_l, d, cols = w_ada.shape
    nb = c_all.shape[0]
    tn = 256

    def body(c_ref, w_ref, b_ref, o_ref):
        cv = c_ref[...]
        ca = (cv * _sigmoid(cv)).astype(BF16)
        o_ref[...] = _dot(ca, w_ref[...].astype(BF16), NN) + b_ref[...]

    return _call(
        body, name=name, grid=(n_l, cols // tn),
        out_shape=jax.ShapeDtypeStruct((n_l, nb, cols), F32),
        in_specs=[pl.BlockSpec((nb, d), lambda l, j: (0, 0)),
                  pl.BlockSpec((None, d, tn), lambda l, j: (l, 0, j)),
                  pl.BlockSpec((None, 1, tn), lambda l, j: (l, 0, j))],
        out_specs=pl.BlockSpec((None, nb, tn), lambda l, j: (l, 0, j)),
        compiler_params=_params(),
    )(c_all, w_ada, b_cols)


def ada_bwd(c_all, dmod_cols, w, m, v, name):
    n_l, d, cols = w.shape
    nb = c_all.shape[0]
    tn = 256

    def body(c_ref, dm_ref, w_ref, m_ref, v_ref, g_ref, d_ref, nm_ref, nv_ref):
        cv = c_ref[...]
        ca = (cv * _sigmoid(cv)).astype(BF16)
        g = _dot(ca, dm_ref[...].astype(BF16), TN)
        dl, nm, nv = _adamw(w_ref[...], g, m_ref[...], v_ref[...])
        g_ref[...] = g
        d_ref[...] = dl
        nm_ref[...] = nm
        nv_ref[...] = nv

    wspec = pl.BlockSpec((None, d, tn), lambda l, j: (l, 0, j))
    return _call(
        body, name=name, grid=(n_l, cols // tn),
        out_shape=[jax.ShapeDtypeStruct((n_l, d, cols), F32)] * 4,
        in_specs=[pl.BlockSpec((nb, d), lambda l, j: (0, 0)),
                  pl.BlockSpec((None, nb, tn), lambda l, j: (l, 0, j)),
                  wspec, wspec, wspec],
        out_specs=[wspec] * 4,
        compiler_params=_params(),
    )(c_all, dmod_cols, w, m, v)


def bias_update(dmod_all, w, m, v, name):
    def body(dm_ref, w_ref, m_ref, v_ref, g_ref, d_ref, nm_ref, nv_ref):
        g = jnp.sum(dm_ref[...], axis=0, keepdims=True)
        dl, nm, nv = _adamw(w_ref[...], g, m_ref[...], v_ref[...])
        g_ref[...] = g
        d_ref[...] = dl
        nm_ref[...] = nm
        nv_ref[...] = nv

    return _call(
        body, name=name,
        out_shape=[jax.ShapeDtypeStruct(w.shape, F32)] * 4,
        compiler_params=_params(),
    )(dmod_all, w, m, v)


def inproj_fwd(x, mod, ng, wg, seq, sectioned, name):
    m_rows, d = x.shape
    nsh, _, ns = wg.shape
    n = nsh * ns
    tm, tn = min(2 * ROW_TILE, seq), _col_tile(ns)
    per = ns // tn

    def body(x_ref, mod_ref, ng_ref, w_ref, proj_ref, h_ref):
        @pl.when(pl.program_id(1) == 0)
        def _():
            xv = x_ref[...]
            r = lax.rsqrt(jnp.mean(xv * xv, axis=-1, keepdims=True) + EPS)
            md = mod_ref[0]
            h = (xv * r * ng_ref[...]) * (1.0 + md[:, d:2 * d]) + md[:, :d]
            h_ref[...] = h.astype(BF16)
        proj_ref[...] = _dot(h_ref[...], w_ref[...], NN)

    if sectioned:
        proj_shape = (nsh, m_rows, ns)
        proj_spec = pl.BlockSpec((None, tm, tn), lambda i, j: (j // per, i, j % per))
    else:
        proj_shape = (m_rows, n)
        proj_spec = pl.BlockSpec((tm, tn), lambda i, j: (i, j))
    return _call(
        body, name=name, grid=(m_rows // tm, n // tn),
        out_shape=[jax.ShapeDtypeStruct(proj_shape, F32), jax.ShapeDtypeStruct((m_rows, d), BF16)],
        in_specs=[pl.BlockSpec((tm, d), lambda i, j: (i, 0)),
                  pl.BlockSpec((1, 1, 3 * d), lambda i, j: ((i * tm) // seq, 0, 0)),
                  pl.BlockSpec((1, d), lambda i, j: (0, 0)),
                  pl.BlockSpec((None, d, tn), lambda i, j: (j // per, 0, j % per))],
        out_specs=[proj_spec, pl.BlockSpec((tm, d), lambda i, j: (i, 0))],
        compiler_params=_params(),
    )(x, mod, ng, wg)


def outproj_fwd(y, w, x, mod, seq, name):
    m_rows, di = y.shape
    d = w.shape[1]
    tm = min(ROW_TILE, seq)

    def body(y_ref, w_ref, x_ref, mod_ref, xn_ref, out_ref):
        acc = _dot(y_ref[...], w_ref[...], NN)
        out_ref[...] = acc.astype(BF16)
        xn_ref[...] = x_ref[...] + mod_ref[0][:, 2 * d:] * acc

    row = pl.BlockSpec((tm, d), lambda i: (i, 0))
    return _call(
        body, name=name, grid=(m_rows // tm,),
        out_shape=[jax.ShapeDtypeStruct((m_rows, d), F32), jax.ShapeDtypeStruct((m_rows, d), BF16)],
        in_specs=[pl.BlockSpec((tm, di), lambda i: (i, 0)),
                  pl.BlockSpec((di, d), lambda i: (0, 0)),
                  row,
                  pl.BlockSpec((1, 1, 3 * d), lambda i: ((i * tm) // seq, 0, 0))],
        out_specs=[row, row],
        compiler_params=_params(),
    )(y, w, x, mod)


def outproj_bwd(dxo, out, mod, w, seq, name):
    m_rows, d = dxo.shape
    di = w.shape[0]
    nb = m_rows // seq
    tm, tn = min(ROW_TILE, seq), _col_tile(di)

    def body(dxo_ref, out_ref, mod_ref, w_ref, dy_ref, dout_ref, dgate_ref):
        i = pl.program_id(0)

        @pl.when(pl.program_id(1) == 0)
        def _():
            dx = dxo_ref[...]
            dout_ref[...] = (mod_ref[0][:, 2 * d:] * dx).astype(BF16)
            part = jnp.sum(dx * out_ref[...].astype(F32), axis=0, keepdims=True)

            @pl.when((i * tm) % seq == 0)
            def _():
                dgate_ref[0] = part

            @pl.when((i * tm) % seq != 0)
            def _():
                dgate_ref[0] = dgate_ref[0] + part

        dy_ref[...] = _dot(dout_ref[...], w_ref[...], NT).astype(BF16)

    row = pl.BlockSpec((tm, d), lambda i, j: (i, 0))
    return _call(
        body, name=name, grid=(m_rows // tm, di // tn),
        out_shape=[jax.ShapeDtypeStruct((m_rows, di), BF16), jax.ShapeDtypeStruct((m_rows, d), BF16),
                   jax.ShapeDtypeStruct((nb, 1, d), F32)],
        in_specs=[row, row,
                  pl.BlockSpec((1, 1, 3 * d), lambda i, j: ((i * tm) // seq, 0, 0)),
                  pl.BlockSpec((tn, d), lambda i, j: (j, 0))],
        out_specs=[pl.BlockSpec((tm, tn), lambda i, j: (i, j)), row,
                   pl.BlockSpec((1, 1, d), lambda i, j: ((i * tm) // seq, 0, 0))],
        compiler_params=_params(),
    )(dxo, out, mod, w)


def grad_w_out(y, dout, name):
    m_rows, di = y.shape
    d = dout.shape[1]
    tm, tk = min(ROW_TILE, m_rows), _col_tile(di)
    n_m = m_rows // tm

    def body(y_ref, do_ref, o_ref, acc_ref):
        mi = pl.program_id(1)

        @pl.when(mi == 0)
        def _():
            acc_ref[...] = jnp.zeros_like(acc_ref)

        acc_ref[...] += _dot(y_ref[...], do_ref[...], TN)

        @pl.when(mi == n_m - 1)
        def _():
            o_ref[...] = acc_ref[...].astype(BF16)

    return _call(
        body, name=name, grid=(di // tk, n_m),
        out_shape=jax.ShapeDtypeStruct((di, d), BF16),
        in_specs=[pl.BlockSpec((tm, tk), lambda j, mi: (mi, j)),
                  pl.BlockSpec((tm, d), lambda j, mi: (mi, 0))],
        out_specs=pl.BlockSpec((tk, d), lambda j, mi: (j, 0)),
        scratch_shapes=[pltpu.VMEM((tk, d), F32)],
        compiler_params=_params(),
    )(y, dout)


def grad_w_in(h, dproj, nsh, sectioned, name):
    m_rows, d = h.shape
    n = dproj.shape[0] * dproj.shape[2] if sectioned else dproj.shape[1]
    ns = n // nsh
    tm, tn = min(ROW_TILE, m_rows), ns
    per = ns // tn
    n_m = m_rows // tm

    def body(h_ref, dp_ref, o_ref, acc_ref):
        mi = pl.program_id(1)
        @pl.when(mi == 0)
        def _():
            acc_ref[...] = jnp.zeros_like(acc_ref)

        acc_ref[...] += _dot(h_ref[...], dp_ref[...], TN)

        @pl.when(mi == n_m - 1)
        def _():
            o_ref[...] = acc_ref[...].astype(BF16)

    if sectioned:
        dp_spec = pl.BlockSpec((None, tm, tn), lambda j, mi: (j // per, mi, j % per))
    else:
        dp_spec = pl.BlockSpec((tm, tn), lambda j, mi: (mi, j))
    return _call(
        body, name=name, grid=(n // tn, n_m),
        out_shape=jax.ShapeDtypeStruct((nsh, d, ns), BF16),
        in_specs=[pl.BlockSpec((tm, d), lambda j, mi: (mi, 0)), dp_spec],
        out_specs=pl.BlockSpec((None, d, tn), lambda j, mi: (j // per, 0, j % per)),
        scratch_shapes=[pltpu.VMEM((d, tn), F32)],
        compiler_params=_params(),
    )(h, dproj)


def inproj_bwd(dproj, wg, x, dxo, mod, ng, seq, sectioned, name):
    m_rows, d = x.shape
    nsh, _, ns = wg.shape
    n = nsh * ns
    nb = m_rows // seq
    tm, tk = min(ROW_TILE, seq), ns
    per = ns // tk
    n_k = n // tk

    def body(dp_ref, w_ref, x_ref, dxo_ref, mod_ref, ng_ref, dxi_ref, dsh_ref, dsc_ref, dng_ref, acc_ref):
        i, k = pl.program_id(0), pl.program_id(1)
        @pl.when(k == 0)
        def _():
            acc_ref[...] = jnp.zeros_like(acc_ref)

        acc_ref[...] += _dot(dp_ref[...], w_ref[...], NT)

        @pl.when(k == n_k - 1)
        def _():
            dh = acc_ref[...]
            xv = x_ref[...]
            r = lax.rsqrt(jnp.mean(xv * xv, axis=-1, keepdims=True) + EPS)
            xn = xv * r
            md = mod_ref[0]
            gain = ng_ref[...]
            p_shift = jnp.sum(dh, axis=0, keepdims=True)
            p_scale = jnp.sum(dh * (xn * gain), axis=0, keepdims=True)
            drn = dh * (1.0 + md[:, d:2 * d])
            p_ng = jnp.sum(drn * xn, axis=0, keepdims=True)
            dxn = drn * gain
            dx = r * (dxn - xn * jnp.mean(dxn * xn, axis=-1, keepdims=True))
            dxi_ref[...] = dxo_ref[...] + dx

            @pl.when((i * tm) % seq == 0)
            def _():
                dsh_ref[0] = p_shift
                dsc_ref[0] = p_scale

            @pl.when((i * tm) % seq != 0)
            def _():
                dsh_ref[0] = dsh_ref[0] + p_shift
                dsc_ref[0] = dsc_ref[0] + p_scale

            @pl.when(i == 0)
            def _():
                dng_ref[...] = p_ng

            @pl.when(i != 0)
            def _():
                dng_ref[...] = dng_ref[...] + p_ng

    if sectioned:
        dp_spec = pl.BlockSpec((None, tm, tk), lambda i, k: (k // per, i, k % per))
    else:
        dp_spec = pl.BlockSpec((tm, tk), lambda i, k: (i, k))
    row = pl.BlockSpec((tm, d), lambda i, k: (i, 0))
    per_seq = pl.BlockSpec((1, 1, d), lambda i, k: ((i * tm) // seq, 0, 0))
    return _call(
        body, name=name, grid=(m_rows // tm, n_k),
        out_shape=[jax.ShapeDtypeStruct((m_rows, d), F32), jax.ShapeDtypeStruct((nb, 1, d), F32),
                   jax.ShapeDtypeStruct((nb, 1, d), F32), jax.ShapeDtypeStruct((1, d), F32)],
        in_specs=[dp_spec,
                  pl.BlockSpec((None, d, tk), lambda i, k: (k // per, 0, k % per)),
                  row, row,
                  pl.BlockSpec((1, 1, 3 * d), lambda i, k: ((i * tm) // seq, 0, 0)),
                  pl.BlockSpec((1, d), lambda i, k: (0, 0))],
        out_specs=[row, per_seq, per_seq, pl.BlockSpec((1, d), lambda i, k: (0, 0))],
        scratch_shapes=[pltpu.VMEM((tm, d), F32)],
        compiler_params=_params(),
    )(dproj, wg, x, dxo, mod, ng)


def _sgu_stats(proj_ref, vg_ref, di, gd, dgel_ref=None):
    s1 = jnp.zeros((SG_BLOCK, 1), F32)
    for g in range(SG_GROUPS):
        v_pre = proj_ref[:, di + g * gd:di + (g + 1) * gd]
        if dgel_ref is None:
            vg = _gelu(v_pre)
        else:
            vg, dgel_ref[:, g * gd:(g + 1) * gd] = _gelu_and_grad(v_pre)
        vg_ref[:, g * gd:(g + 1) * gd] = vg
        s1 = s1 + jnp.sum(vg, axis=1, keepdims=True)
    mu = s1 / di
    s2 = jnp.zeros((SG_BLOCK, 1), F32)
    for g in range(SG_GROUPS):
        dv = vg_ref[:, g * gd:(g + 1) * gd] - mu
        s2 = s2 + jnp.sum(dv * dv, axis=1, keepdims=True)
    return mu, lax.rsqrt(s2 / di + EPS)


def sgu_fwd(proj, ln_gain, ln_bias, ws, bs, name):
    m_rows, n3 = proj.shape
    di = n3 // 3
    gd = di // SG_GROUPS

    def body(proj_ref, lg_ref, lb_ref, ws_ref, bs_ref, y_ref, wsm_ref, vg_ref):
        @pl.when(pl.program_id(0) == 0)
        def _():
            mask = _chunk_mask()
            for g in range(SG_GROUPS):
                wsm_ref[g] = jnp.where(mask, ws_ref[g], 0.0).astype(BF16)

        mu, rstd = _sgu_stats(proj_ref, vg_ref, di, gd)
        for g in range(SG_GROUPS):
            cs = slice(g * gd, (g + 1) * gd)
            vln = (vg_ref[:, cs] - mu) * rstd * lg_ref[:, cs] + lb_ref[:, cs]
            s = _dot(wsm_ref[g], vln.astype(BF16), NN) + bs_ref[g]
            u = _gelu(proj_ref[:, cs])
            gp = proj_ref[:, 2 * di + g * gd:2 * di + (g + 1) * gd]
            y_ref[:, cs] = (u * s * (gp * _sigmoid(gp))).astype(BF16)

    full = lambda shape: pl.BlockSpec(shape, lambda i: (0,) * len(shape))
    return _call(
        body, name=name, grid=(m_rows // SG_BLOCK,),
        out_shape=jax.ShapeDtypeStruct((m_rows, di), BF16),
        in_specs=[pl.BlockSpec((SG_BLOCK, n3), lambda i: (i, 0)),
                  full((1, di)), full((1, di)),
                  full((SG_GROUPS, SG_BLOCK, SG_BLOCK)), full((SG_GROUPS, SG_BLOCK, 1))],
        out_specs=pl.BlockSpec((SG_BLOCK, di), lambda i: (i, 0)),
        scratch_shapes=[pltpu.VMEM((SG_GROUPS, SG_BLOCK, SG_BLOCK), BF16), pltpu.VMEM((SG_BLOCK, di), F32)],
        compiler_params=_params(),
    )(proj, ln_gain, ln_bias, ws, bs)


def sgu_bwd(proj, dy, ln_gain, ln_bias, ws, bs, name):
    m_rows, n3 = proj.shape
    di = n3 // 3
    gd = di // SG_GROUPS
    n_i = m_rows // SG_BLOCK

    def body(proj_ref, dy_ref, lg_ref, lb_ref, ws_ref, bs_ref,
             dp_ref, dws_ref, dbs_ref, dlg_ref, dlb_ref, wsm_ref, vg_ref, dvh_ref, dgel_ref):
        i = pl.program_id(0)

        def before():
            @pl.when(i == 0)
            def _():
                mask = _chunk_mask()
                for g in range(SG_GROUPS):
                    wsm_ref[g] = jnp.where(mask, ws_ref[g], 0.0).astype(BF16)
                dws_ref[...] = jnp.zeros_like(dws_ref)
                dbs_ref[...] = jnp.zeros_like(dbs_ref)
                dlg_ref[...] = jnp.zeros_like(dlg_ref)
                dlb_ref[...] = jnp.zeros_like(dlb_ref)

        def after():
            @pl.when(i == n_i - 1)
            def _():
                mask = _chunk_mask()
                for g in range(SG_GROUPS):
                    dws_ref[g] = jnp.where(mask, dws_ref[g], 0.0)

        before()
        mu, rstd = _sgu_stats(proj_ref, vg_ref, di, gd, dgel_ref)
        m1 = jnp.zeros((SG_BLOCK, 1), F32)
        m2 = jnp.zeros((SG_BLOCK, 1), F32)
        for g in range(SG_GROUPS):
            cs = slice(g * gd, (g + 1) * gd)
            gs = slice(2 * di + g * gd, 2 * di + (g + 1) * gd)
            gain = lg_ref[:, cs]
            vhat = (vg_ref[:, cs] - mu) * rstd
            vln_b = (vhat * gain + lb_ref[:, cs]).astype(BF16)
            s = _dot(wsm_ref[g], vln_b, NN) + bs_ref[g]
            u, du = _gelu_and_grad(proj_ref[:, cs])
            sg, dsg = _silu_and_grad(proj_ref[:, gs])
            dyv = dy_ref[:, cs].astype(F32)
            dp_ref[:, cs] = (dyv * s * sg * du).astype(BF16)
            dp_ref[:, gs] = (dyv * u * s * dsg).astype(BF16)
            ds = dyv * u * sg
            ds_b = ds.astype(BF16)
            dws_ref[g] = dws_ref[g] + _dot(ds_b, vln_b, NT)
            dbs_ref[g] = dbs_ref[g] + jnp.sum(ds, axis=1, keepdims=True)
            dvln = _dot(wsm_ref[g], ds_b, TN)
            dlg_ref[:, cs] = dlg_ref[:, cs] + jnp.sum(dvln * vhat, axis=0, keepdims=True)
            dlb_ref[:, cs] = dlb_ref[:, cs] + jnp.sum(dvln, axis=0, keepdims=True)
            dvh = dvln * gain
            dvh_ref[:, cs] = dvh
            m1 = m1 + jnp.sum(dvh, axis=1, keepdims=True)
            m2 = m2 + jnp.sum(dvh * vhat, axis=1, keepdims=True)
        m1 = m1 / di
        m2 = m2 / di
        for g in range(SG_GROUPS):
            cs = slice(g * gd, (g + 1) * gd)
            vs = slice(di + g * gd, di + (g + 1) * gd)
            vhat = (vg_ref[:, cs] - mu) * rstd
            dvg = rstd * (dvh_ref[:, cs] - m1 - vhat * m2)
            dp_ref[:, vs] = (dvg * dgel_ref[:, cs]).astype(BF16)

        after()

    full = lambda shape: pl.BlockSpec(shape, lambda i: (0,) * len(shape))
    return _call(
        body, name=name, grid=(n_i,),
        out_shape=[jax.ShapeDtypeStruct((m_rows, n3), BF16),
                   jax.ShapeDtypeStruct((SG_GROUPS, SG_BLOCK, SG_BLOCK), F32),
                   jax.ShapeDtypeStruct((SG_GROUPS, SG_BLOCK, 1), F32),
                   jax.ShapeDtypeStruct((1, di), F32), jax.ShapeDtypeStruct((1, di), F32)],
        in_specs=[pl.BlockSpec((SG_BLOCK, n3), lambda i: (i, 0)),
                  pl.BlockSpec((SG_BLOCK, di), lambda i: (i, 0)),
                  full((1, di)), full((1, di)),
                  full((SG_GROUPS, SG_BLOCK, SG_BLOCK)), full((SG_GROUPS, SG_BLOCK, 1))],
        out_specs=[pl.BlockSpec((SG_BLOCK, n3), lambda i: (i, 0)),
                   full((SG_GROUPS, SG_BLOCK, SG_BLOCK)), full((SG_GROUPS, SG_BLOCK, 1)),
                   full((1, di)), full((1, di))],
        scratch_shapes=[pltpu.VMEM((SG_GROUPS, SG_BLOCK, SG_BLOCK), BF16),
                        pltpu.VMEM((SG_BLOCK, di), F32), pltpu.VMEM((SG_BLOCK, di), F32),
                        pltpu.VMEM((SG_BLOCK, di), F32)],
        compiler_params=_params(),
    )(proj, dy, ln_gain, ln_bias, ws, bs)


def _lower_bound(lbraw):
    mx = jnp.maximum(lbraw[0:1, :], lbraw[1:2, :])
    e0 = jnp.exp(lbraw[0:1, :] - mx)
    e1 = jnp.exp(lbraw[1:2, :] - mx)
    p0 = e0 / (e0 + e1)
    p1 = e1 / (e0 + e1)
    return (p0 + p1) - p0, p0, p1


def _tri(lower):
    r = lax.broadcasted_iota(jnp.int32, (CHUNK, CHUNK), 0)
    c = lax.broadcasted_iota(jnp.int32, (CHUNK, CHUNK), 1)
    return ((r >= c) if lower else (c >= r)).astype(BF16)


def _running_sum(tri, x):
    x1 = x.astype(BF16)
    r1 = x - x1.astype(F32)
    x2 = r1.astype(BF16)
    x3 = (r1 - x2.astype(F32)).astype(BF16)
    return _dot(tri, x1, NN) + _dot(tri, x2, NN) + _dot(tri, x3, NN)


def _row(a, idx):
    r = lax.broadcasted_iota(jnp.int32, a.shape, 0)
    return jnp.sum(jnp.where(r == idx, a, 0.0), axis=0, keepdims=True)


def _hgrn_gates(qp, fp, lb, tri):
    sgm = _sigmoid_small(fp)
    f = lb + (1.0 - lb) * sgm
    k = 1.0 - f
    a = _running_sum(tri, jnp.log(f))
    a_mid = _row(a, CHUNK // 2 - 1)
    a_last = _row(a, CHUNK - 1)
    q, dq = _silu_and_grad(qp)
    e1, e2, e3, e4 = jnp.exp(a - a_mid), jnp.exp(a_mid - a), jnp.exp(a), jnp.exp(a_last - a)
    return dict(sgm=sgm, f=f, k=k, q=q, dq=dq, e1=e1, e2=e2, e3=e3, e4=e4, dec=jnp.exp(a_last),
                q_in=q * e1, k_in=k * e2, q_out=q * e3, k_out=k * e4)


def _causal():
    r = lax.broadcasted_iota(jnp.int32, (CHUNK, CHUNK), 0)
    c = lax.broadcasted_iota(jnp.int32, (CHUNK, CHUNK), 1)
    return r >= c


def hgrn_fwd(proj4, lbraw, gn, seq, name):
    _, m_rows, di = proj4.shape
    nb, nh, nc = m_rows // seq, di // HEAD_DIM, seq // CHUNK
    rows = min(HG_ROWS, seq)
    wide = HG_WIDE * HEAD_DIM
    ns, cpb = seq // rows, rows // CHUNK

    def body(p_ref, lb_ref, gn_ref, y_ref, sts_ref, st_ref):
        @pl.when(pl.program_id(2) == 0)
        def _():
            st_ref[...] = jnp.zeros_like(st_ref)

        tri = _tri(True)
        causal = _causal()
        gain = gn_ref[...]
        lbs = [_lower_bound(lb_ref[:, j * HEAD_DIM:(j + 1) * HEAD_DIM])[0] for j in range(HG_WIDE)]

        units = [(n, j) for n in range(cpb) for j in range(HG_WIDE)]
        rs = lambda n: slice(n * CHUNK, (n + 1) * CHUNK)
        cs = lambda j: slice(j * HEAD_DIM, (j + 1) * HEAD_DIM)
        gates, v_b, sc_b, kv, o_in, o_x = {}, {}, {}, {}, {}, {}
        for n, j in units:
            gates[n, j] = _hgrn_gates(p_ref[0, rs(n), cs(j)], p_ref[1, rs(n), cs(j)], lbs[j], tri)
            v_b[n, j] = p_ref[2, rs(n), cs(j)].astype(BF16)
        for u in units:
            t = gates[u]
            sc_b[u] = jnp.where(causal, _dot(t["q_in"].astype(BF16), t["k_in"].astype(BF16), NT), 0.0).astype(BF16)
            kv[u] = _dot(v_b[u], t["k_out"].astype(BF16), TN)
        for u in units:
            o_in[u] = _dot(sc_b[u], v_b[u], NN)
        for j in range(HG_WIDE):
            st = st_ref[j]
            for n in range(cpb):
                sts_ref[n, :, cs(j)] = st
                o_x[n, j] = _dot(gates[n, j]["q_out"].astype(BF16), st.astype(BF16), NT)
                st = st * gates[n, j]["dec"] + kv[n, j]
            st_ref[j] = st
        for n, j in units:
            o = o_in[n, j] + o_x[n, j]
            r = lax.rsqrt(jnp.mean(o * o, axis=-1, keepdims=True) + EPS)
            gp = p_ref[3, rs(n), cs(j)]
            y_ref[rs(n), cs(j)] = ((o * r * gain) * (gp * _sigmoid(gp))).astype(BF16)

    return _call(
        body, name=name, grid=(nh // HG_WIDE, nb, ns),
        out_shape=[jax.ShapeDtypeStruct((m_rows, di), BF16),
                   jax.ShapeDtypeStruct((nb * nc, HEAD_DIM, di), F32)],
        in_specs=[pl.BlockSpec((4, rows, wide), lambda hg, b, s: (0, b * ns + s, hg)),
                  pl.BlockSpec((2, wide), lambda hg, b, s: (0, hg)),
                  pl.BlockSpec((1, HEAD_DIM), lambda hg, b, s: (0, 0))],
        out_specs=[pl.BlockSpec((rows, wide), lambda hg, b, s: (b * ns + s, hg)),
                   pl.BlockSpec((cpb, HEAD_DIM, wide), lambda hg, b, s: (b * ns + s, 0, hg))],
        scratch_shapes=[pltpu.VMEM((HG_WIDE, HEAD_DIM, HEAD_DIM), F32)],
        compiler_params=_params(),
    )(proj4, lbraw, gn)


def hgrn_bwd(proj4, dy, sts, lbraw, gn, seq, name):
    _, m_rows, di = proj4.shape
    nb, nh, nc = m_rows // seq, di // HEAD_DIM, seq // CHUNK
    rows = min(HG_ROWS, seq)
    wide = HG_WIDE * HEAD_DIM
    ns, cpb = seq // rows, rows // CHUNK
    n_hg = nh // HG_WIDE

    def body(p_ref, dy_ref, sts_ref, lb_ref, gn_ref, dp_ref, dlb_ref, dgn_ref, dst_ref, lbacc_ref, gnacc_ref):
        hg, b, s = pl.program_id(0), pl.program_id(1), pl.program_id(2)
        tri, triu = _tri(True), _tri(False)
        causal = _causal()
        gain = gn_ref[...]
        first = (b == 0) & (s == 0)
        cs = lambda j: slice(j * HEAD_DIM, (j + 1) * HEAD_DIM)

        def before():
            @pl.when((hg == 0) & first)
            def _():
                gnacc_ref[...] = jnp.zeros_like(gnacc_ref)

            @pl.when(first)
            def _():
                lbacc_ref[...] = jnp.zeros_like(lbacc_ref)

            @pl.when(s == 0)
            def _():
                dst_ref[...] = jnp.zeros_like(dst_ref)

        def after():
            @pl.when((b == nb - 1) & (s == ns - 1))
            def _():
                for j in range(HG_WIDE):
                    _, p0, p1 = _lower_bound(lb_ref[:, cs(j)])
                    acc = lbacc_ref[:, cs(j)]
                    dlb_ref[0:1, cs(j)] = -acc * p0 * p1
                    dlb_ref[1:2, cs(j)] = acc * p1 * (1.0 - p1)

            @pl.when((hg == n_hg - 1) & (b == nb - 1) & (s == ns - 1))
            def _():
                tot = gnacc_ref[:, 0:HEAD_DIM]
                for j in range(1, HG_WIDE):
                    tot = tot + gnacc_ref[:, cs(j)]
                dgn_ref[...] = tot

        before()

        units = [(n, j) for n in range(cpb) for j in range(HG_WIDE)]
        rs = lambda n: slice(n * CHUNK, (n + 1) * CHUNK)
        lbs = [_lower_bound(lb_ref[:, cs(j)])[0] for j in range(HG_WIDE)]
        gates, v_b, st_b, sc_b, o, do_b = {}, {}, {}, {}, {}, {}
        dq_out, dsc_b, dv, g_st, dq_in, dk_in, dst_at, dk_out, ddec = {}, {}, {}, {}, {}, {}, {}, {}, {}
        for n, j in units:
            gates[n, j] = _hgrn_gates(p_ref[0, rs(n), cs(j)], p_ref[1, rs(n), cs(j)], lbs[j], tri)
            v_b[n, j] = p_ref[2, rs(n), cs(j)].astype(BF16)
            st_b[n, j] = sts_ref[n, :, cs(j)].astype(BF16)
        for u in units:
            t = gates[u]
            sc_b[u] = jnp.where(causal, _dot(t["q_in"].astype(BF16), t["k_in"].astype(BF16), NT), 0.0).astype(BF16)
        for u in units:
            o[u] = _dot(sc_b[u], v_b[u], NN) + _dot(gates[u]["q_out"].astype(BF16), st_b[u], NT)
        for n, j in units:
            ov = o[n, j]
            r = lax.rsqrt(jnp.mean(ov * ov, axis=-1, keepdims=True) + EPS)
            ohat = ov * r
            sg, dsg = _silu_and_grad(p_ref[3, rs(n), cs(j)])
            dyv = dy_ref[rs(n), cs(j)].astype(F32)
            dp_ref[3, rs(n), cs(j)] = (dyv * (ohat * gain) * dsg).astype(BF16)
            d_on = dyv * sg
            gnacc_ref[:, cs(j)] = gnacc_ref[:, cs(j)] + jnp.sum(d_on * ohat, axis=0, keepdims=True)
            dohat = d_on * gain
            do_b[n, j] = (r * (dohat - ohat * jnp.mean(dohat * ohat, axis=-1, keepdims=True))).astype(BF16)
        for u in units:
            dq_out[u] = _dot(do_b[u], st_b[u], NN)
            dsc_b[u] = jnp.where(causal, _dot(do_b[u], v_b[u], NT), 0.0).astype(BF16)
            dv[u] = _dot(sc_b[u], do_b[u], TN)
            g_st[u] = _dot(do_b[u], gates[u]["q_out"].astype(BF16), TN)
        for u in units:
            dq_in[u] = _dot(dsc_b[u], gates[u]["k_in"].astype(BF16), NN)
            dk_in[u] = _dot(dsc_b[u], gates[u]["q_in"].astype(BF16), TN)
        for j in range(HG_WIDE):
            dst = dst_ref[j]
            for n in reversed(range(cpb)):
                dst_at[n, j] = dst
                dst = dst * gates[n, j]["dec"] + g_st[n, j]
            dst_ref[j] = dst
        for n, j in units:
            dst = dst_at[n, j]
            dst_b = dst.astype(BF16)
            dk_out[n, j] = _dot(v_b[n, j], dst_b, NN)
            dv[n, j] = dv[n, j] + _dot(gates[n, j]["k_out"].astype(BF16), dst_b, NT)
            ddec[n, j] = jnp.sum(dst * sts_ref[n, :, cs(j)], axis=0, keepdims=True)
        for n, j in units:
            t = gates[n, j]
            dp_ref[2, rs(n), cs(j)] = dv[n, j].astype(BF16)
            dq = dq_in[n, j] * t["e1"] + dq_out[n, j] * t["e3"]
            dk = dk_in[n, j] * t["e2"] + dk_out[n, j] * t["e4"]
            w_in = dq_in[n, j] * t["q_in"] - dk_in[n, j] * t["k_in"]
            w_out = dk_out[n, j] * t["k_out"]
            da = w_in + dq_out[n, j] * t["q_out"] - w_out
            da_mid = -jnp.sum(w_in, axis=0, keepdims=True)
            da_last = jnp.sum(w_out, axis=0, keepdims=True) + ddec[n, j] * t["dec"]
            rid = lax.broadcasted_iota(jnp.int32, da.shape, 0)
            da = da + jnp.where(rid == CHUNK // 2 - 1, da_mid, 0.0) + jnp.where(rid == CHUNK - 1, da_last, 0.0)
            dlf = _running_sum(triu, da)
            df = dlf / t["f"] - dk
            sgm = t["sgm"]
            dp_ref[1, rs(n), cs(j)] = (df * (1.0 - lbs[j]) * sgm * (1.0 - sgm)).astype(BF16)
            lbacc_ref[:, cs(j)] = lbacc_ref[:, cs(j)] + jnp.sum(df * (1.0 - sgm), axis=0, keepdims=True)
            dp_ref[0, rs(n), cs(j)] = (dq * t["dq"]).astype(BF16)

        after()

    blk = lambda hg, b, s: b * ns + (ns - 1 - s)
    return _call(
        body, name=name, grid=(n_hg, nb, ns),
        out_shape=[jax.ShapeDtypeStruct((4, m_rows, di), BF16), jax.ShapeDtypeStruct((2, di), F32),
                   jax.ShapeDtypeStruct((1, HEAD_DIM), F32)],
        in_specs=[pl.BlockSpec((4, rows, wide), lambda hg, b, s: (0, blk(hg, b, s), hg)),
                  pl.BlockSpec((rows, wide), lambda hg, b, s: (blk(hg, b, s), hg)),
                  pl.BlockSpec((cpb, HEAD_DIM, wide), lambda hg, b, s: (blk(hg, b, s), 0, hg)),
                  pl.BlockSpec((2, wide), lambda hg, b, s: (0, hg)),
                  pl.BlockSpec((1, HEAD_DIM), lambda hg, b, s: (0, 0))],
        out_specs=[pl.BlockSpec((4, rows, wide), lambda hg, b, s: (0, blk(hg, b, s), hg)),
                   pl.BlockSpec((2, wide), lambda hg, b, s: (0, hg)),
                   pl.BlockSpec((1, HEAD_DIM), lambda hg, b, s: (0, 0))],
        scratch_shapes=[pltpu.VMEM((HG_WIDE, HEAD_DIM, HEAD_DIM), F32), pltpu.VMEM((1, wide), F32),
                        pltpu.VMEM((1, wide), F32)],
        compiler_params=_params(),
    )(proj4, dy, sts, lbraw, gn)


def outproj_loss(y, w, x, mod, fg, target, seq, name):
    m_rows, di = y.shape
    d = w.shape[1]
    tm = min(512, seq)

    def body(y_ref, w_ref, x_ref, mod_ref, fg_ref, t_ref, out_ref, loss_ref, dx_ref, dfg_ref):
        i = pl.program_id(0)
        acc = _dot(y_ref[...], w_ref[...], NN)
        out_ref[...] = acc.astype(BF16)
        xv = x_ref[...] + mod_ref[0][:, 2 * d:] * acc
        gain = fg_ref[...]
        r = lax.rsqrt(jnp.mean(xv * xv, axis=-1, keepdims=True) + EPS)
        xn = xv * r
        e = xn * gain - t_ref[...]
        part = 0.5 * jnp.sum(jnp.mean(e * e, axis=-1, keepdims=True), axis=0, keepdims=True)
        dyv = e / d
        p_fg = jnp.sum(dyv * xn, axis=0, keepdims=True)
        dxn = dyv * gain
        dx_ref[...] = r * (dxn - xn * jnp.mean(dxn * xn, axis=-1, keepdims=True))

        @pl.when(i == 0)
        def _():
            loss_ref[...] = part
            dfg_ref[...] = p_fg

        @pl.when(i != 0)
        def _():
            loss_ref[...] = loss_ref[...] + part
            dfg_ref[...] = dfg_ref[...] + p_fg

    row = pl.BlockSpec((tm, d), lambda i: (i, 0))
    return _call(
        body, name=name, grid=(m_rows // tm,),
        out_shape=[jax.ShapeDtypeStruct((m_rows, d), BF16), jax.ShapeDtypeStruct((1, 1), F32),
                   jax.ShapeDtypeStruct((m_rows, d), F32), jax.ShapeDtypeStruct((1, d), F32)],
        in_specs=[pl.BlockSpec((tm, di), lambda i: (i, 0)),
                  pl.BlockSpec((di, d), lambda i: (0, 0)),
                  row,
                  pl.BlockSpec((1, 1, 3 * d), lambda i: ((i * tm) // seq, 0, 0)),
                  pl.BlockSpec((1, d), lambda i: (0, 0)), row],
        out_specs=[row, pl.BlockSpec((1, 1), lambda i: (0, 0)), row, pl.BlockSpec((1, d), lambda i: (0, 0))],
        compiler_params=_params(),
    )(y, w, x, mod, fg, target)


def _pack(parts):
    flat = jnp.concatenate([p.reshape(-1) for p in parts])
    pad = (-flat.shape[0]) % (8 * LANES)
    return jnp.pad(flat, (0, pad)).reshape(-1, LANES)


def kernel(x, c, norm_gain, w_ada, b_ada, a_w_in, a_ln_gain, a_ln_bias, a_w_s, a_b_s, a_w_out, b_w_in, b_lower_bounds, b_gn_gain, b_w_out, final_gain, loss_target, m_norm_gain, m_w_ada, m_b_ada, m_a_w_in, m_a_ln_gain, m_a_ln_bias, m_a_w_s, m_a_b_s, m_a_w_out, m_b_w_in, m_b_lower_bounds, m_b_gn_gain, m_b_w_out, m_final_gain, v_norm_gain, v_w_ada, v_b_ada, v_a_w_in, v_a_ln_gain, v_a_ln_bias, v_a_w_s, v_a_b_s, v_a_w_out, v_b_w_in, v_b_lower_bounds, v_b_gn_gain, v_b_w_out, v_final_gain):
    nb, seq, d = x.shape
    m_rows = nb * seq
    n_l = w_ada.shape[0]
    ada_cols = w_ada.shape[2]
    px, py, pc = _place()
    chip = 2 * px + py
    dev = 2 * chip + pc

    c_all = allgather_small(c.reshape(-1, LANES), "gather_c").reshape(N_DEV * nb, d)
    b_cols = lax.dynamic_slice_in_dim(b_ada, chip * ada_cols, ada_cols, axis=1).reshape(n_l, 1, ada_cols)
    mod_cols = ada_fwd(c_all, w_ada, b_cols, "ada_fwd")
    mod_g = allgather_small(mod_cols.reshape(-1, LANES), "gather_mod")
    mod_g = mod_g.reshape(N_CHIPS, 2, n_l, N_DEV * nb, ada_cols)[:, 0]
    mod_all = jnp.transpose(mod_g, (1, 2, 0, 3)).reshape(n_l, N_DEV * nb, 3 * d)
    mod_mine = lax.dynamic_slice_in_dim(mod_all, dev * nb, nb, axis=1)
    mod0 = mod_mine[0].reshape(nb, 1, 3 * d)
    mod1 = mod_mine[1].reshape(nb, 1, 3 * d)

    (wa_in, wa_out), tok_a = gather_inplace(
        [cast_into_slot(a_w_in[0], chip, mod_mine, "cast_a_in"), cast_into_slot(a_w_out[0], chip, mod_mine, "cast_a_out")],
        "gather_a")
    s_bi = gather_start(cast_into_slot(b_w_in[0], chip, tok_a, "cast_b_in"), "gather_b_in_start")
    s_bo = gather_start(cast_into_slot(b_w_out[0], chip, s_bi[3], "cast_b_out"), "gather_b_out_start")
    di = a_w_out.shape[1] * N_CHIPS
    wa_out = wa_out.reshape(di, d)

    x0 = x.reshape(m_rows, d)
    tgt = loss_target.reshape(m_rows, d)
    ng0 = norm_gain[0:1] + (s_bi[3][0, 0] + s_bo[3][0, 0])
    ng1 = norm_gain[1:2]
    bs_col = a_b_s[0].reshape(SG_GROUPS, SG_BLOCK, 1)
    proj_a, h_a = inproj_fwd(x0, mod0, ng0, wa_in, seq, False, "a_inproj")
    y_a = sgu_fwd(proj_a, a_ln_gain, a_ln_bias, a_w_s[0], bs_col, "a_sgu")
    x1, out_a = outproj_fwd(y_a, wa_out, x0, mod0, seq, "a_outproj")
    wb_in = gather_wait(*s_bi[:3], out_a, "gather_b_in_wait")
    proj_b, h_b = inproj_fwd(x1, mod1, ng1, wb_in, seq, True, "b_inproj")
    y_b, sts_b = hgrn_fwd(proj_b, b_lower_bounds, b_gn_gain, seq, "b_hgrn")
    wb_out = gather_wait(*s_bo[:3], y_b, "gather_b_out_wait").reshape(di, d)
    out_b, loss_part, dx2, dfg = outproj_loss(
        y_b, wb_out, x1, mod1, final_gain.reshape(1, d), tgt, seq, "b_outproj_loss")

    shard_rows = di // N_CHIPS
    dy_b, dout_b, dgate1 = outproj_bwd(dx2, out_b, mod1, wb_out, seq, "b_outproj_bwd")
    gwb_out = grad_w_out(y_b, dout_b, "b_grad_w_out").reshape(N_CHIPS, shard_rows, d)
    e_bo = exchange_start(gwb_out, "exchange_b_out_start")
    dproj_b, dlb, dgn = hgrn_bwd(
        proj_b, dy_b, sts_b, b_lower_bounds, b_gn_gain + e_bo[4][0, 0], seq, "b_hgrn_bwd")
    e_bi = exchange_start(grad_w_in(h_b, dproj_b, N_CHIPS, True, "b_grad_w_in"), "exchange_b_in_start")
    dx1, dshift1, dscale1, dng1 = inproj_bwd(
        dproj_b, wb_in, x1, dx2, mod1, ng1 + e_bi[4][0, 0], seq, True, "b_inproj_bwd")

    dy_a, dout_a, dgate0 = outproj_bwd(dx1, out_a, mod0, wa_out, seq, "a_outproj_bwd")
    gwa_out = grad_w_out(y_a, dout_a, "a_grad_w_out").reshape(N_CHIPS, shard_rows, d)
    e_ao = exchange_start(gwa_out, "exchange_a_out_start")
    dproj_a, dws, dbs, dlg, dlbias = sgu_bwd(
        proj_a, dy_a, a_ln_gain + e_ao[4][0, 0], a_ln_bias, a_w_s[0], bs_col, "a_sgu_bwd")
    e_ai = exchange_start(grad_w_in(h_a, dproj_a, N_CHIPS, False, "a_grad_w_in"), "exchange_a_in_start")
    dx0, dshift0, dscale0, dng0 = inproj_bwd(
        dproj_a, wa_in, x0, dx1, mod0, norm_gain[0:1] + e_ai[4][0, 0], seq, False, "a_inproj_bwd")
    grad_x = dx0.reshape(nb, seq, d)

    dmod = jnp.concatenate([dshift0, dscale0, dgate0, dshift1, dscale1, dgate1], axis=2)
    n_dmod = dmod.size
    small_g = [jnp.concatenate([dng0, dng1], axis=0), dlg, dlbias, dws, dbs, dlb, dfg, dgn]
    packed_g = _pack([dmod] + small_g + [loss_part])
    rows = packed_g.shape[0]
    s_small = gather_all_start(
        lax.dynamic_update_slice(jnp.zeros((N_DEV, rows, LANES), F32), packed_g[None], (dev, 0, 0)),
        "gather_small_start")

    def finish(group, after):
        mine = []
        for ex, _, _, _, nm in group:
            parts_thru, land = exchange_wait(ex[0], ex[1], ex[2], ex[3], after, "exchange_" + nm + "_wait")
            mine.append(sum_parts(parts_thru, land, chip, "sum_" + nm))
            after = mine[-1]
        theirs = swap_sibling(mine, "swap_" + group[0][4])
        return [[r.reshape(w.shape) for r in adamw_pair(pa, pb, w[0], m[0], v[0], "adamw_" + nm)]
                for pa, pb, (_, w, m, v, nm) in zip(mine, theirs, group)]

    (gb_out, db_out, mb_out, vb_out), (gb_in, db_in, mb_in, vb_in), (ga_out, da_out, ma_out, va_out) = finish(
        [(e_bo, b_w_out, m_b_w_out, v_b_w_out, "b_out"), (e_bi, b_w_in, m_b_w_in, v_b_w_in, "b_in"),
         (e_ao, a_w_out, m_a_w_out, v_a_w_out, "a_out")], s_small[3])
    ((ga_in, da_in, ma_in, va_in),) = finish([(e_ai, a_w_in, m_a_w_in, v_a_w_in, "a_in")], ga_out)

    small_w = [norm_gain, a_ln_gain, a_ln_bias, a_w_s, a_b_s, b_lower_bounds, final_gain, b_gn_gain]
    small_m = [m_norm_gain, m_a_ln_gain, m_a_ln_bias, m_a_w_s, m_a_b_s, m_b_lower_bounds, m_final_gain, m_b_gn_gain]
    small_v = [v_norm_gain, v_a_ln_gain, v_a_ln_bias, v_a_w_s, v_a_b_s, v_b_lower_bounds, v_final_gain, v_b_gn_gain]
    rows_of = lambda a: a.reshape(-1, a.shape[-1])
    gathered = gather_all_wait(s_small[0], s_small[1], s_small[2], ga_in, "gather_small_wait")
    tail, small_res = small_update(
        gathered, n_dmod // LANES, [rows_of(a) for a in small_w], [rows_of(a) for a in small_m],
        [rows_of(a) for a in small_v], "small_update")
    loss = tail[0, 0]
    sg, sd, sm, sv = [[small_res[p][kind].reshape(w.shape) for p, w in enumerate(small_w)] for kind in range(4)]

    dmod_all = gathered[:, :n_dmod // LANES].reshape(N_DEV * nb, n_l, 3 * d)
    dmod_cols = lax.dynamic_slice_in_dim(dmod_all, chip * ada_cols, ada_cols, axis=2)
    dmod_cols = jnp.transpose(dmod_cols, (1, 0, 2))
    g_wada, d_wada, m_wada, v_wada = ada_bwd(c_all, dmod_cols, w_ada, m_w_ada, v_w_ada, "ada_bwd")
    flat = lambda a: a.reshape(1, -1)
    g_bada, d_bada, m_bada, v_bada = [
        r.reshape(b_ada.shape) for r in
        bias_update(dmod_all.reshape(N_DEV * nb, n_l * 3 * d), flat(b_ada), flat(m_b_ada), flat(v_b_ada), "bias_update")]

    def order(ng, wada, bada, ain, sm_rest, aout, bin_, bout):
        lg, lbi, ws_, bs_, lbd, fg_, gn_ = sm_rest
        return [ng, wada, bada, ain, lg, lbi, ws_, bs_, aout, bin_, lbd, gn_, bout, fg_]

    grads = order(sg[0], g_wada, g_bada, ga_in, sg[1:8], ga_out, gb_in, gb_out)
    deltas = order(sd[0], d_wada, d_bada, da_in, sd[1:8], da_out, db_in, db_out)
    new_m = order(sm[0], m_wada, m_bada, ma_in, sm[1:8], ma_out, mb_in, mb_out)
    new_v = order(sv[0], v_wada, v_bada, va_in, sv[1:8], va_out, vb_in, vb_out)
    return (loss, grad_x, *grads, *deltas, *new_m, *new_v)
```

```python
import jax
import jax.numpy as jnp
from jax import lax
from jax.experimental import pallas as pl
from jax.experimental.pallas import tpu as pltpu

F32 = jnp.float32
BF16 = jnp.bfloat16
EPS = 1e-6
CHUNK = 64
SG_BLOCK = 128
SG_GROUPS = 8
HEAD_DIM = 128
HG_WIDE = 8
HG_ROWS = 256
N_CHIPS = 4
N_DEV = 8
LANES = 128
ADAM_LR = 0.001
ADAM_B1 = 0.9
ADAM_B2 = 0.999
ADAM_EPS = 1e-08
ADAM_WD = 0.01
ADAM_STEP = 10
GELU_C0 = 0.7978845608028654
GELU_C1 = 0.044715
MESH = pl.DeviceIdType.MESH
VMEM_LIMIT = 56 * 1024 * 1024


ROW_TILE = 1024


def _col_tile(n):
    return next(t for t in (1024, 768, 512, 256) if n % t == 0)


def _call(body, **kw):
    return pl.pallas_call(body, **kw)


def _params(**kw):
    return pltpu.CompilerParams(vmem_limit_bytes=VMEM_LIMIT, **kw)


def _sigmoid(x):
    return 0.5 * jnp.tanh(0.5 * x) + 0.5


def _sigmoid_small(x):
    return 1.0 / (1.0 + jnp.exp(-x))


def _silu_and_grad(x):
    s = _sigmoid(x)
    return x * s, s * (1.0 + x * (1.0 - s))


def _gelu(x):
    return 0.5 * x * (1.0 + jnp.tanh(GELU_C0 * (x + GELU_C1 * x * x * x)))


def _gelu_and_grad(x):
    t = jnp.tanh(GELU_C0 * (x + GELU_C1 * x * x * x))
    g = 0.5 * x * (1.0 + t)
    dg = 0.5 * (1.0 + t) + 0.5 * x * (1.0 - t * t) * (GELU_C0 * (1.0 + 3.0 * GELU_C1 * x * x))
    return g, dg


def _dot(a, b, dims, precision=None):
    return lax.dot_general(a, b, (dims, ((), ())), precision=precision, preferred_element_type=F32)


NN = ((1,), (0,))
NT = ((1,), (1,))
TN = ((0,), (0,))


def _adamw(w, g, m, v):
    m = ADAM_B1 * m + (1.0 - ADAM_B1) * g
    v = ADAM_B2 * v + (1.0 - ADAM_B2) * (g * g)
    m_hat = m / (1.0 - ADAM_B1 ** ADAM_STEP)
    v_hat = v / (1.0 - ADAM_B2 ** ADAM_STEP)
    delta = -ADAM_LR * (m_hat / (jnp.sqrt(v_hat) + ADAM_EPS) + ADAM_WD * w)
    return delta, m, v


def _chunk_mask():
    r = lax.broadcasted_iota(jnp.int32, (SG_BLOCK, SG_BLOCK), 0)
    c = lax.broadcasted_iota(jnp.int32, (SG_BLOCK, SG_BLOCK), 1)
    return (c // CHUNK) <= (r // CHUNK)


def _place():
    return lax.axis_index("x"), lax.axis_index("y"), lax.axis_index("c")


def _other_chips(x, y):
    return [(1 - x, y), (x, 1 - y), (1 - x, 1 - y)]


def allgather_small(v, name):
    m_per, n = v.shape

    def body(x_ref, out_ref, send_sems, recv_sems, local_sem):
        x, y, c = _place()
        me, sibling = (x, y, c), (x, y, 1 - c)
        chips = _other_chips(x, y)

        def rows(px, py, pc):
            return out_ref.at[pl.ds((4 * px + 2 * py + pc) * m_per, m_per), :]

        def copy(k, block, to, src=None):
            return pltpu.make_async_remote_copy(
                src_ref=rows(*block) if src is None else src, dst_ref=rows(*block),
                send_sem=send_sems.at[k], recv_sem=recv_sems.at[k], device_id=to, device_id_type=MESH)

        mine = pltpu.make_async_copy(x_ref, rows(*me), local_sem)
        mine.start()
        first = [copy(0, me, sibling, src=x_ref)]
        first += [copy(1 + j, me, (*chip, c), src=x_ref) for j, chip in enumerate(chips)]
        for cp in first:
            cp.start()
        passed = [copy(4 + j, (*chip, c), sibling) for j, chip in enumerate(chips)]
        for j, chip in enumerate(chips):
            copy(1 + j, (*chip, c), me).wait_recv()
            passed[j].start()
        copy(0, sibling, me).wait_recv()
        for j, chip in enumerate(chips):
            copy(4 + j, (*chip, 1 - c), me).wait_recv()
        for cp in first + passed:
            cp.wait_send()
        mine.wait()

    return _call(
        body, name=name,
        out_shape=jax.ShapeDtypeStruct((N_DEV * m_per, n), v.dtype),
        in_specs=[pl.BlockSpec(memory_space=pltpu.VMEM)],
        out_specs=pl.BlockSpec(memory_space=pltpu.VMEM),
        scratch_shapes=[pltpu.SemaphoreType.DMA((7,)), pltpu.SemaphoreType.DMA((7,)), pltpu.SemaphoreType.DMA],
    )(v)


def _hbm_spec():
    return pl.BlockSpec(memory_space=pltpu.HBM)


def _sem_spec():
    return pl.BlockSpec(memory_space=pltpu.SEMAPHORE)


def _split_params():
    return pltpu.CompilerParams(has_side_effects=pltpu.SideEffectType.DATAFLOW_SIDE_EFFECTING)


def _hbm(a):
    return pltpu.with_memory_space_constraint(a, pltpu.HBM)


def _half_copy(land_ref, rows, chip_idx, core_half, send_sem, recv_sem, to):
    half = land_ref.at[chip_idx, pl.ds(core_half * (rows // 2), rows // 2), :]
    return pltpu.make_async_remote_copy(
        src_ref=half, dst_ref=half, send_sem=send_sem, recv_sem=recv_sem, device_id=to, device_id_type=MESH)


def halves_start(lands, name):
    n = len(lands)

    def body(*refs):
        land_refs, send_sems, recv_sems, token = refs[:n], refs[n], refs[n + 1], refs[-1]
        x, y, c = _place()
        for w in range(n):
            for j, (px, py) in enumerate(_other_chips(x, y)):
                _half_copy(land_refs[w], lands[w].shape[1], 2 * x + y, c,
                           send_sems.at[3 * w + j], recv_sems.at[3 * w + j], (px, py, c)).start()
        token[...] = jnp.zeros_like(token)

    res = _call(
        body, name=name,
        out_shape=(pltpu.SemaphoreType.DMA((3 * n,)), pltpu.SemaphoreType.DMA((3 * n,)),
                   *[pltpu.HBM(a.shape, a.dtype) for a in lands], jax.ShapeDtypeStruct((8, LANES), F32)),
        in_specs=(_hbm_spec(),) * n,
        out_specs=(_sem_spec(), _sem_spec(), *[_hbm_spec()] * n, pl.BlockSpec(memory_space=pltpu.VMEM)),
        input_output_aliases={w: 2 + w for w in range(n)}, compiler_params=_split_params(),
    )(*[_hbm(a) for a in lands])
    return res[0], res[1], list(res[2:2 + n]), res[2 + n]


def halves_wait(send_sems, recv_sems, lands, after, name):
    n = len(lands)

    def body(*refs):
        land_refs, send_sems, recv_sems = refs[:n], refs[n], refs[n + 1]
        x, y, c = _place()
        for w in range(n):
            for j, (px, py) in enumerate(_other_chips(x, y)):
                cp = _half_copy(land_refs[w], lands[w].shape[1], 2 * px + py, c,
                                send_sems.at[3 * w + j], recv_sems.at[3 * w + j], (px, py, c))
                cp.wait_send()
                cp.wait_recv()

    return _call(
        body, name=name,
        out_shape=tuple(pltpu.HBM(a.shape, a.dtype) for a in lands),
        in_specs=(*[_hbm_spec()] * n, _sem_spec(), _sem_spec(), pl.BlockSpec(memory_space=pl.ANY)),
        out_specs=tuple(_hbm_spec() for _ in lands), input_output_aliases={w: w for w in range(n)},
        compiler_params=_split_params(),
    )(*lands, send_sems, recv_sems, after)


def pass_halves(lands, name):
    n = len(lands)

    def body(*refs):
        land_refs, send_sems, recv_sems = refs[n:2 * n], refs[2 * n], refs[2 * n + 1]
        x, y, c = _place()
        sent = []
        for w in range(n):
            for j, (px, py) in enumerate(_other_chips(x, y)):
                cp = _half_copy(land_refs[w], lands[w].shape[1], 2 * px + py, c,
                                send_sems.at[3 * w + j], recv_sems.at[3 * w + j], (x, y, 1 - c))
                cp.start()
                sent.append(cp)
        for w in range(n):
            for j, (px, py) in enumerate(_other_chips(x, y)):
                _half_copy(land_refs[w], lands[w].shape[1], 2 * px + py, 1 - c,
                           send_sems.at[3 * w + j], recv_sems.at[3 * w + j], (x, y, 1 - c)).wait_recv()
        for cp in sent:
            cp.wait_send()

    return _call(
        body, name=name,
        out_shape=[jax.ShapeDtypeStruct(a.shape, a.dtype) for a in lands],
        in_specs=[_hbm_spec()] * n, out_specs=[_hbm_spec()] * n,
        input_output_aliases={w: w for w in range(n)},
        scratch_shapes=[pltpu.SemaphoreType.DMA((3 * n,)), pltpu.SemaphoreType.DMA((3 * n,))],
    )(*lands)


def gather_start(land, after, name):
    def body(land_ref, after_ref, send_sems, recv_sems, land_thru, token):
        del after_ref, land_thru
        x, y, c = _place()
        for j, (px, py) in enumerate(_other_chips(x, y)):
            pltpu.make_async_remote_copy(
                src_ref=land_ref.at[2 * x + y], dst_ref=land_ref.at[2 * x + y],
                send_sem=send_sems.at[j], recv_sem=recv_sems.at[j], device_id=(px, py, c),
                device_id_type=MESH).start()
        token[...] = jnp.zeros_like(token)

    return _call(
        body, name=name,
        out_shape=(pltpu.SemaphoreType.DMA((3,)), pltpu.SemaphoreType.DMA((3,)),
                   pltpu.HBM(land.shape, land.dtype), jax.ShapeDtypeStruct((8, LANES), F32)),
        in_specs=(_hbm_spec(), pl.BlockSpec(memory_space=pl.ANY)),
        out_specs=(_sem_spec(), _sem_spec(), _hbm_spec(), pl.BlockSpec(memory_space=pltpu.VMEM)),
        input_output_aliases={0: 2}, compiler_params=_split_params(),
    )(_hbm(land), after)


def gather_wait(send_sems, recv_sems, land, after, name):
    def body(land_ref, send_sems, recv_sems, after_ref, land_out):
        del after_ref, land_out
        x, y, c = _place()
        for j, (px, py) in enumerate(_other_chips(x, y)):
            cp = pltpu.make_async_remote_copy(
                src_ref=land_ref.at[2 * x + y], dst_ref=land_ref.at[2 * px + py],
                send_sem=send_sems.at[j], recv_sem=recv_sems.at[j], device_id=(px, py, c), device_id_type=MESH)
            cp.wait_send()
            cp.wait_recv()

    return _call(
        body, name=name,
        out_shape=pltpu.HBM(land.shape, land.dtype),
        in_specs=(_hbm_spec(), _sem_spec(), _sem_spec(), pl.BlockSpec(memory_space=pl.ANY)),
        out_specs=_hbm_spec(), input_output_aliases={0: 0}, compiler_params=_split_params(),
    )(land, send_sems, recv_sems, after)


def _flips():
    return [(fx, fy, fc) for fx in (0, 1) for fy in (0, 1) for fc in (0, 1) if (fx, fy, fc) != (0, 0, 0)]


def _flipped(x, y, c, flip):
    fx, fy, fc = flip
    return (1 - x if fx else x, 1 - y if fy else y, 1 - c if fc else c)


def gather_all_start(land, name):
    def body(land_ref, send_sems, recv_sems, land_thru, token):
        del land_thru
        x, y, c = _place()
        for k, flip in enumerate(_flips()):
            pltpu.make_async_remote_copy(
                src_ref=land_ref.at[4 * x + 2 * y + c], dst_ref=land_ref.at[4 * x + 2 * y + c],
                send_sem=send_sems.at[k], recv_sem=recv_sems.at[k], device_id=_flipped(x, y, c, flip),
                device_id_type=MESH).start()
        token[...] = jnp.zeros_like(token)

    return _call(
        body, name=name,
        out_shape=(pltpu.SemaphoreType.DMA((7,)), pltpu.SemaphoreType.DMA((7,)),
                   pltpu.HBM(land.shape, land.dtype), jax.ShapeDtypeStruct((8, LANES), F32)),
        in_specs=(_hbm_spec(),),
        out_specs=(_sem_spec(), _sem_spec(), _hbm_spec(), pl.BlockSpec(memory_space=pltpu.VMEM)),
        input_output_aliases={0: 2}, compiler_params=_split_params(),
    )(_hbm(land))


def gather_all_wait(send_sems, recv_sems, land, after, name):
    def body(land_ref, send_sems, recv_sems, after_ref, land_out):
        del after_ref, land_out
        x, y, c = _place()
        for k, flip in enumerate(_flips()):
            px, py, pc = _flipped(x, y, c, flip)
            cp = pltpu.make_async_remote_copy(
                src_ref=land_ref.at[4 * x + 2 * y + c], dst_ref=land_ref.at[4 * px + 2 * py + pc],
                send_sem=send_sems.at[k], recv_sem=recv_sems.at[k], device_id=(px, py, pc), device_id_type=MESH)
            cp.wait_send()
            cp.wait_recv()

    return _call(
        body, name=name,
        out_shape=pltpu.HBM(land.shape, land.dtype),
        in_specs=(_hbm_spec(), _sem_spec(), _sem_spec(), pl.BlockSpec(memory_space=pl.ANY)),
        out_specs=_hbm_spec(), input_output_aliases={0: 0}, compiler_params=_split_params(),
    )(land, send_sems, recv_sems, after)


def exchange_start(parts, name):
    _, r, c_ = parts.shape

    def body(parts_ref, land_ref, send_sems, recv_sems, parts_thru, land_thru, token):
        del parts_thru, land_thru
        x, y, c = _place()
        for j, (px, py) in enumerate(_other_chips(x, y)):
            pltpu.make_async_remote_copy(
                src_ref=parts_ref.at[2 * px + py], dst_ref=land_ref.at[j],
                send_sem=send_sems.at[j], recv_sem=recv_sems.at[j], device_id=(px, py, c),
                device_id_type=MESH).start()
        token[...] = jnp.zeros_like(token)

    return _call(
        body, name=name,
        out_shape=(pltpu.SemaphoreType.DMA((3,)), pltpu.SemaphoreType.DMA((3,)),
                   pltpu.HBM(parts.shape, parts.dtype), pltpu.HBM((3, r, c_), parts.dtype),
                   jax.ShapeDtypeStruct((8, LANES), F32)),
        in_specs=(_hbm_spec(), _hbm_spec()),
        out_specs=(_sem_spec(), _sem_spec(), _hbm_spec(), _hbm_spec(), pl.BlockSpec(memory_space=pltpu.VMEM)),
        input_output_aliases={0: 2, 1: 3}, compiler_params=_split_params(),
    )(_hbm(parts), _hbm(lax.empty((3, r, c_), parts.dtype)))


def exchange_wait(send_sems, recv_sems, parts, land, after, name):
    def body(parts_ref, land_ref, send_sems, recv_sems, after_ref, parts_out, land_out):
        del after_ref, parts_out, land_out
        x, y, c = _place()
        for j, (px, py) in enumerate(_other_chips(x, y)):
            cp = pltpu.make_async_remote_copy(
                src_ref=parts_ref.at[2 * px + py], dst_ref=land_ref.at[j],
                send_sem=send_sems.at[j], recv_sem=recv_sems.at[j], device_id=(px, py, c), device_id_type=MESH)
            cp.wait_send()
            cp.wait_recv()

    return _call(
        body, name=name,
        out_shape=(pltpu.HBM(parts.shape, parts.dtype), pltpu.HBM(land.shape, land.dtype)),
        in_specs=(_hbm_spec(), _hbm_spec(), _sem_spec(), _sem_spec(), pl.BlockSpec(memory_space=pl.ANY)),
        out_specs=(_hbm_spec(), _hbm_spec()), input_output_aliases={0: 0, 1: 1},
        compiler_params=_split_params(),
    )(parts, land, send_sems, recv_sems, after)


def cast_into_slot(w, chip, after, name):
    r, c = w.shape
    tr = min(256, r)

    def body(s_ref, w_ref, after_ref, o_ref):
        del s_ref, after_ref
        o_ref[...] = w_ref[...].astype(BF16)

    return _call(
        body, name=name,
        grid_spec=pltpu.PrefetchScalarGridSpec(
            num_scalar_prefetch=1, grid=(r // tr,),
            in_specs=[pl.BlockSpec((tr, c), lambda i, s: (i, 0)), pl.BlockSpec(memory_space=pl.ANY)],
            out_specs=pl.BlockSpec((None, tr, c), lambda i, s: (s[0], i, 0))),
        out_shape=jax.ShapeDtypeStruct((N_CHIPS, r, c), BF16),
        compiler_params=_params(),
    )(chip.reshape(1).astype(jnp.int32), w, after)


def sum_parts(parts, land, chip, name):
    _, r, c = parts.shape
    tr = min(256, r)

    def body(s_ref, p_ref, l_ref, o_ref):
        del s_ref
        acc = p_ref[...].astype(F32) + l_ref[0].astype(F32)
        acc = acc + l_ref[1].astype(F32)
        o_ref[...] = (acc + l_ref[2].astype(F32)).astype(BF16)

    return _call(
        body, name=name,
        grid_spec=pltpu.PrefetchScalarGridSpec(
            num_scalar_prefetch=1, grid=(r // tr,),
            in_specs=[pl.BlockSpec((None, tr, c), lambda i, s: (s[0], i, 0)),
                      pl.BlockSpec((3, tr, c), lambda i, s: (0, i, 0))],
            out_specs=pl.BlockSpec((tr, c), lambda i, s: (i, 0))),
        out_shape=jax.ShapeDtypeStruct((r, c), BF16),
        compiler_params=_params(),
    )(chip.reshape(1).astype(jnp.int32), parts, land)


def swap_sibling(arrs, name):
    n = len(arrs)

    def body(*refs):
        ins, outs = refs[:n], refs[n:2 * n]
        send_sems, recv_sems = refs[2 * n:]
        x, y, c = _place()
        cps = []
        for w in range(n):
            cp = pltpu.make_async_remote_copy(
                src_ref=ins[w], dst_ref=outs[w], send_sem=send_sems.at[w], recv_sem=recv_sems.at[w],
                device_id=(x, y, 1 - c), device_id_type=MESH)
            cp.start()
            cps.append(cp)
        for cp in cps:
            cp.wait_recv()
        for cp in cps:
            cp.wait_send()

    return _call(
        body, name=name,
        out_shape=[jax.ShapeDtypeStruct(a.shape, a.dtype) for a in arrs],
        in_specs=[_hbm_spec()] * n, out_specs=[_hbm_spec()] * n,
        scratch_shapes=[pltpu.SemaphoreType.DMA((n,)), pltpu.SemaphoreType.DMA((n,))],
    )(*arrs)


def adamw_pair(pa, pb, w, m, v, name):
    r, c = w.shape
    tr = min(128, r)

    def body(pa_ref, pb_ref, w_ref, m_ref, v_ref, g_ref, d_ref, nm_ref, nv_ref):
        g = pa_ref[...].astype(F32) + pb_ref[...].astype(F32)
        d, nm, nv = _adamw(w_ref[...], g, m_ref[...], v_ref[...])
        g_ref[...] = g
        d_ref[...] = d
        nm_ref[...] = nm
        nv_ref[...] = nv

    spec = pl.BlockSpec((tr, c), lambda i: (i, 0))
    return _call(
        body, name=name, grid=(r // tr,),
        out_shape=[jax.ShapeDtypeStruct((r, c), F32)] * 4,
        in_specs=[spec] * 5, out_specs=[spec] * 4,
        compiler_params=_params(),
    )(pa, pb, w, m, v)


def small_update(gathered, first_row, ws, ms, vs, name):
    n_w = len(ws)
    total_rows = gathered.shape[1]

    def body(*refs):
        g_ref = refs[0]
        w_refs, m_refs, v_refs = refs[1:1 + n_w], refs[1 + n_w:1 + 2 * n_w], refs[1 + 2 * n_w:1 + 3 * n_w]
        tail_ref = refs[1 + 3 * n_w]
        outs = refs[2 + 3 * n_w:2 + 7 * n_w]
        sum_ref = refs[2 + 7 * n_w]
        acc = g_ref[0]
        for k in range(1, N_DEV):
            acc = acc + g_ref[k]
        sum_ref[...] = acc
        row = first_row
        for p in range(n_w):
            a, b = ws[p].shape
            per = b // LANES
            g_out, d_out, m_out, v_out = outs[4 * p:4 * p + 4]
            if per == 1:
                g_out[...] = sum_ref[row:row + a, :]
            else:
                for i in range(a):
                    for jc in range(per):
                        g_out[i:i + 1, jc * LANES:(jc + 1) * LANES] = sum_ref[row + i * per + jc:row + i * per + jc + 1, :]
            row += a * per
            dl, nm, nv = _adamw(w_refs[p][...], g_out[...], m_refs[p][...], v_refs[p][...])
            d_out[...] = dl
            m_out[...] = nm
            v_out[...] = nv
        tail_ref[...] = sum_ref[row:row + 1, :]

    out_shape = [jax.ShapeDtypeStruct((1, LANES), F32)]
    for w in ws:
        out_shape += [jax.ShapeDtypeStruct(w.shape, F32)] * 4
    res = _call(
        body, name=name, out_shape=out_shape,
        scratch_shapes=[pltpu.VMEM((total_rows, LANES), F32)],
        compiler_params=_params(),
    )(gathered, *ws, *ms, *vs)
    return res[0], [res[1 + 4 * p:5 + 4 * p] for p in range(n_w)]


def ada_fwd(c_all, w_ada, b_cols, name):
    n_l, d, cols = w_ada.shape
    nb = c_all.shape[0]
    tn = 256

    def body(c_ref, w_ref, b_ref, o_ref):
        cv = c_ref[...]
        ca = (cv * _sigmoid(cv)).astype(BF16)
        o_ref[...] = _dot(ca, w_ref[...].astype(BF16), NN) + b_ref[...]

    return _call(
        body, name=name, grid=(n_l, cols // tn),
        out_shape=jax.ShapeDtypeStruct((n_l, nb, cols), F32),
        in_specs=[pl.BlockSpec((nb, d), lambda l, j: (0, 0)),
                  pl.BlockSpec((None, d, tn), lambda l, j: (l, 0, j)),
                  pl.BlockSpec((None, 1, tn), lambda l, j: (l, 0, j))],
        out_specs=pl.BlockSpec((None, nb, tn), lambda l, j: (l, 0, j)),
        compiler_params=_params(),
    )(c_all, w_ada, b_cols)


def ada_bwd(c_all, dmod_cols, w, m, v, name):
    n_l, d, cols = w.shape
    nb = c_all.shape[0]
    tn = 256

    def body(c_ref, dm_ref, w_ref, m_ref, v_ref, g_ref, d_ref, nm_ref, nv_ref):
        cv = c_ref[...]
        ca = (cv * _sigmoid(cv)).astype(BF16)
        g = _dot(ca, dm_ref[...].astype(BF16), TN)
        dl, nm, nv = _adamw(w_ref[...], g, m_ref[...], v_ref[...])
        g_ref[...] = g
        d_ref[...] = dl
        nm_ref[...] = nm
        nv_ref[...] = nv

    wspec = pl.BlockSpec((None, d, tn), lambda l, j: (l, 0, j))
    return _call(
        body, name=name, grid=(n_l, cols // tn),
        out_shape=[jax.ShapeDtypeStruct((n_l, d, cols), F32)] * 4,
        in_specs=[pl.BlockSpec((nb, d), lambda l, j: (0, 0)),
                  pl.BlockSpec((None, nb, tn), lambda l, j: (l, 0, j)),
                  wspec, wspec, wspec],
        out_specs=[wspec] * 4,
        compiler_params=_params(),
    )(c_all, dmod_cols, w, m, v)


def bias_update(dmod_all, w, m, v, name):
    def body(dm_ref, w_ref, m_ref, v_ref, g_ref, d_ref, nm_ref, nv_ref):
        g = jnp.sum(dm_ref[...], axis=0, keepdims=True)
        dl, nm, nv = _adamw(w_ref[...], g, m_ref[...], v_ref[...])
        g_ref[...] = g
        d_ref[...] = dl
        nm_ref[...] = nm
        nv_ref[...] = nv

    return _call(
        body, name=name,
        out_shape=[jax.ShapeDtypeStruct(w.shape, F32)] * 4,
        compiler_params=_params(),
    )(dmod_all, w, m, v)


def inproj_fwd(x, mod, ng, wg, seq, sectioned, name):
    m_rows, d = x.shape
    nsh, _, ns = wg.shape
    n = nsh * ns
    tm, tn = min(2 * ROW_TILE, seq), _col_tile(ns)
    per = ns // tn

    def body(x_ref, mod_ref, ng_ref, w_ref, proj_ref, h_ref):
        @pl.when(pl.program_id(1) == 0)
        def _():
            xv = x_ref[...]
            r = lax.rsqrt(jnp.mean(xv * xv, axis=-1, keepdims=True) + EPS)
            md = mod_ref[0]
            h = (xv * r * ng_ref[...]) * (1.0 + md[:, d:2 * d]) + md[:, :d]
            h_ref[...] = h.astype(BF16)
        proj_ref[...] = _dot(h_ref[...], w_ref[...], NN)

    if sectioned:
        proj_shape = (nsh, m_rows, ns)
        proj_spec = pl.BlockSpec((None, tm, tn), lambda i, j: (j // per, i, j % per))
    else:
        proj_shape = (m_rows, n)
        proj_spec = pl.BlockSpec((tm, tn), lambda i, j: (i, j))
    return _call(
        body, name=name, grid=(m_rows // tm, n // tn),
        out_shape=[jax.ShapeDtypeStruct(proj_shape, F32), jax.ShapeDtypeStruct((m_rows, d), BF16)],
        in_specs=[pl.BlockSpec((tm, d), lambda i, j: (i, 0)),
                  pl.BlockSpec((1, 1, 3 * d), lambda i, j: ((i * tm) // seq, 0, 0)),
                  pl.BlockSpec((1, d), lambda i, j: (0, 0)),
                  pl.BlockSpec((None, d, tn), lambda i, j: (j // per, 0, j % per))],
        out_specs=[proj_spec, pl.BlockSpec((tm, d), lambda i, j: (i, 0))],
        compiler_params=_params(),
    )(x, mod, ng, wg)


def outproj_fwd(y, w, x, mod, seq, name):
    m_rows, di = y.shape
    d = w.shape[1]
    tm = min(ROW_TILE, seq)

    def body(y_ref, w_ref, x_ref, mod_ref, xn_ref, out_ref):
        acc = _dot(y_ref[...], w_ref[...], NN)
        out_ref[...] = acc.astype(BF16)
        xn_ref[...] = x_ref[...] + mod_ref[0][:, 2 * d:] * acc

    row = pl.BlockSpec((tm, d), lambda i: (i, 0))
    return _call(
        body, name=name, grid=(m_rows // tm,),
        out_shape=[jax.ShapeDtypeStruct((m_rows, d), F32), jax.ShapeDtypeStruct((m_rows, d), BF16)],
        in_specs=[pl.BlockSpec((tm, di), lambda i: (i, 0)),
                  pl.BlockSpec((di, d), lambda i: (0, 0)),
                  row,
                  pl.BlockSpec((1, 1, 3 * d), lambda i: ((i * tm) // seq, 0, 0))],
        out_specs=[row, row],
        compiler_params=_params(),
    )(y, w, x, mod)


def outproj_bwd(dxo, out, mod, w, seq, name):
    m_rows, d = dxo.shape
    di = w.shape[0]
    nb = m_rows // seq
    tm, tn = min(ROW_TILE, seq), _col_tile(di)

    def body(dxo_ref, out_ref, mod_ref, w_ref, dy_ref, dout_ref, dgate_ref):
        i = pl.program_id(0)

        @pl.when(pl.program_id(1) == 0)
        def _():
            dx = dxo_ref[...]
            dout_ref[...] = (mod_ref[0][:, 2 * d:] * dx).astype(BF16)
            part = jnp.sum(dx * out_ref[...].astype(F32), axis=0, keepdims=True)

            @pl.when((i * tm) % seq == 0)
            def _():
                dgate_ref[0] = part

            @pl.when((i * tm) % seq != 0)
            def _():
                dgate_ref[0] = dgate_ref[0] + part

        dy_ref[...] = _dot(dout_ref[...], w_ref[...], NT).astype(BF16)

    row = pl.BlockSpec((tm, d), lambda i, j: (i, 0))
    return _call(
        body, name=name, grid=(m_rows // tm, di // tn),
        out_shape=[jax.ShapeDtypeStruct((m_rows, di), BF16), jax.ShapeDtypeStruct((m_rows, d), BF16),
                   jax.ShapeDtypeStruct((nb, 1, d), F32)],
        in_specs=[row, row,
                  pl.BlockSpec((1, 1, 3 * d), lambda i, j: ((i * tm) // seq, 0, 0)),
                  pl.BlockSpec((tn, d), lambda i, j: (j, 0))],
        out_specs=[pl.BlockSpec((tm, tn), lambda i, j: (i, j)), row,
                   pl.BlockSpec((1, 1, d), lambda i, j: ((i * tm) // seq, 0, 0))],
        compiler_params=_params(),
    )(dxo, out, mod, w)


def grad_w_out(y, dout, name):
    m_rows, di = y.shape
    d = dout.shape[1]
    tm, tk = min(ROW_TILE, m_rows), _col_tile(di)
    n_m = m_rows // tm

    def body(y_ref, do_ref, o_ref, acc_ref):
        mi = pl.program_id(1)

        @pl.when(mi == 0)
        def _():
            acc_ref[...] = jnp.zeros_like(acc_ref)

        acc_ref[...] += _dot(y_ref[...], do_ref[...], TN)

        @pl.when(mi == n_m - 1)
        def _():
            o_ref[...] = acc_ref[...].astype(BF16)

    return _call(
        body, name=name, grid=(di // tk, n_m),
        out_shape=jax.ShapeDtypeStruct((di, d), BF16),
        in_specs=[pl.BlockSpec((tm, tk), lambda j, mi: (mi, j)),
                  pl.BlockSpec((tm, d), lambda j, mi: (mi, 0))],
        out_specs=pl.BlockSpec((tk, d), lambda j, mi: (j, 0)),
        scratch_shapes=[pltpu.VMEM((tk, d), F32)],
        compiler_params=_params(),
    )(y, dout)


def grad_w_in(h, dproj, nsh, sectioned, name):
    m_rows, d = h.shape
    n = dproj.shape[0] * dproj.shape[2] if sectioned else dproj.shape[1]
    ns = n // nsh
    tm, tn = min(ROW_TILE, m_rows), ns
    per = ns // tn
    n_m = m_rows // tm

    def body(h_ref, dp_ref, o_ref, acc_ref):
        mi = pl.program_id(1)
        @pl.when(mi == 0)
        def _():
            acc_ref[...] = jnp.zeros_like(acc_ref)

        acc_ref[...] += _dot(h_ref[...], dp_ref[...], TN)

        @pl.when(mi == n_m - 1)
        def _():
            o_ref[...] = acc_ref[...].astype(BF16)

    if sectioned:
        dp_spec = pl.BlockSpec((None, tm, tn), lambda j, mi: (j // per, mi, j % per))
    else:
        dp_spec = pl.BlockSpec((tm, tn), lambda j, mi: (mi, j))
    return _call(
        body, name=name, grid=(n // tn, n_m),
        out_shape=jax.ShapeDtypeStruct((nsh, d, ns), BF16),
        in_specs=[pl.BlockSpec((tm, d), lambda j, mi: (mi, 0)), dp_spec],
        out_specs=pl.BlockSpec((None, d, tn), lambda j, mi: (j // per, 0, j % per)),
        scratch_shapes=[pltpu.VMEM((d, tn), F32)],
        compiler_params=_params(),
    )(h, dproj)


def inproj_bwd(dproj, wg, x, dxo, mod, ng, seq, sectioned, name):
    m_rows, d = x.shape
    nsh, _, ns = wg.shape
    n = nsh * ns
    nb = m_rows // seq
    tm, tk = min(ROW_TILE, seq), ns
    per = ns // tk
    n_k = n // tk

    def body(dp_ref, w_ref, x_ref, dxo_ref, mod_ref, ng_ref, dxi_ref, dsh_ref, dsc_ref, dng_ref, acc_ref):
        i, k = pl.program_id(0), pl.program_id(1)
        @pl.when(k == 0)
        def _():
            acc_ref[...] = jnp.zeros_like(acc_ref)

        acc_ref[...] += _dot(dp_ref[...], w_ref[...], NT)

        @pl.when(k == n_k - 1)
        def _():
            dh = acc_ref[...]
            xv = x_ref[...]
            r = lax.rsqrt(jnp.mean(xv * xv, axis=-1, keepdims=True) + EPS)
            xn = xv * r
            md = mod_ref[0]
            gain = ng_ref[...]
            p_shift = jnp.sum(dh, axis=0, keepdims=True)
            p_scale = jnp.sum(dh * (xn * gain), axis=0, keepdims=True)
            drn = dh * (1.0 + md[:, d:2 * d])
            p_ng = jnp.sum(drn * xn, axis=0, keepdims=True)
            dxn = drn * gain
            dx = r * (dxn - xn * jnp.mean(dxn * xn, axis=-1, keepdims=True))
            dxi_ref[...] = dxo_ref[...] + dx

            @pl.when((i * tm) % seq == 0)
            def _():
                dsh_ref[0] = p_shift
                dsc_ref[0] = p_scale

            @pl.when((i * tm) % seq != 0)
            def _():
                dsh_ref[0] = dsh_ref[0] + p_shift
                dsc_ref[0] = dsc_ref[0] + p_scale

            @pl.when(i == 0)
            def _():
                dng_ref[...] = p_ng

            @pl.when(i != 0)
            def _():
                dng_ref[...] = dng_ref[...] + p_ng

    if sectioned:
        dp_spec = pl.BlockSpec((None, tm, tk), lambda i, k: (k // per, i, k % per))
    else:
        dp_spec = pl.BlockSpec((tm, tk), lambda i, k: (i, k))
    row = pl.BlockSpec((tm, d), lambda i, k: (i, 0))
    per_seq = pl.BlockSpec((1, 1, d), lambda i, k: ((i * tm) // seq, 0, 0))
    return _call(
        body, name=name, grid=(m_rows // tm, n_k),
        out_shape=[jax.ShapeDtypeStruct((m_rows, d), F32), jax.ShapeDtypeStruct((nb, 1, d), F32),
                   jax.ShapeDtypeStruct((nb, 1, d), F32), jax.ShapeDtypeStruct((1, d), F32)],
        in_specs=[dp_spec,
                  pl.BlockSpec((None, d, tk), lambda i, k: (k // per, 0, k % per)),
                  row, row,
                  pl.BlockSpec((1, 1, 3 * d), lambda i, k: ((i * tm) // seq, 0, 0)),
                  pl.BlockSpec((1, d), lambda i, k: (0, 0))],
        out_specs=[row, per_seq, per_seq, pl.BlockSpec((1, d), lambda i, k: (0, 0))],
        scratch_shapes=[pltpu.VMEM((tm, d), F32)],
        compiler_params=_params(),
    )(dproj, wg, x, dxo, mod, ng)


def _sgu_stats(proj_ref, vg_ref, di, gd, dgel_ref=None):
    s1 = jnp.zeros((SG_BLOCK, 1), F32)
    for g in range(SG_GROUPS):
        v_pre = proj_ref[:, di + g * gd:di + (g + 1) * gd]
        if dgel_ref is None:
            vg = _gelu(v_pre)
        else:
            vg, dgel_ref[:, g * gd:(g + 1) * gd] = _gelu_and_grad(v_pre)
        vg_ref[:, g * gd:(g + 1) * gd] = vg
        s1 = s1 + jnp.sum(vg, axis=1, keepdims=True)
    mu = s1 / di
    s2 = jnp.zeros((SG_BLOCK, 1), F32)
    for g in range(SG_GROUPS):
        dv = vg_ref[:, g * gd:(g + 1) * gd] - mu
        s2 = s2 + jnp.sum(dv * dv, axis=1, keepdims=True)
    return mu, lax.rsqrt(s2 / di + EPS)


def sgu_fwd(proj, ln_gain, ln_bias, ws, bs, name):
    m_rows, n3 = proj.shape
    di = n3 // 3
    gd = di // SG_GROUPS

    def body(proj_ref, lg_ref, lb_ref, ws_ref, bs_ref, y_ref, wsm_ref, vg_ref):
        @pl.when(pl.program_id(0) == 0)
        def _():
            mask = _chunk_mask()
            for g in range(SG_GROUPS):
                wsm_ref[g] = jnp.where(mask, ws_ref[g], 0.0).astype(BF16)

        mu, rstd = _sgu_stats(proj_ref, vg_ref, di, gd)
        for g in range(SG_GROUPS):
            cs = slice(g * gd, (g + 1) * gd)
            vln = (vg_ref[:, cs] - mu) * rstd * lg_ref[:, cs] + lb_ref[:, cs]
            s = _dot(wsm_ref[g], vln.astype(BF16), NN) + bs_ref[g]
            u = _gelu(proj_ref[:, cs])
            gp = proj_ref[:, 2 * di + g * gd:2 * di + (g + 1) * gd]
            y_ref[:, cs] = (u * s * (gp * _sigmoid(gp))).astype(BF16)

    full = lambda shape: pl.BlockSpec(shape, lambda i: (0,) * len(shape))
    return _call(
        body, name=name, grid=(m_rows // SG_BLOCK,),
        out_shape=jax.ShapeDtypeStruct((m_rows, di), BF16),
        in_specs=[pl.BlockSpec((SG_BLOCK, n3), lambda i: (i, 0)),
                  full((1, di)), full((1, di)),
                  full((SG_GROUPS, SG_BLOCK, SG_BLOCK)), full((SG_GROUPS, SG_BLOCK, 1))],
        out_specs=pl.BlockSpec((SG_BLOCK, di), lambda i: (i, 0)),
        scratch_shapes=[pltpu.VMEM((SG_GROUPS, SG_BLOCK, SG_BLOCK), BF16), pltpu.VMEM((SG_BLOCK, di), F32)],
        compiler_params=_params(),
    )(proj, ln_gain, ln_bias, ws, bs)


def sgu_bwd(proj, dy, ln_gain, ln_bias, ws, bs, name):
    m_rows, n3 = proj.shape
    di = n3 // 3
    gd = di // SG_GROUPS
    n_i = m_rows // SG_BLOCK

    def body(proj_ref, dy_ref, lg_ref, lb_ref, ws_ref, bs_ref,
             dp_ref, dws_ref, dbs_ref, dlg_ref, dlb_ref, wsm_ref, vg_ref, dvh_ref, dgel_ref):
        i = pl.program_id(0)

        def before():
            @pl.when(i == 0)
            def _():
                mask = _chunk_mask()
                for g in range(SG_GROUPS):
                    wsm_ref[g] = jnp.where(mask, ws_ref[g], 0.0).astype(BF16)
                dws_ref[...] = jnp.zeros_like(dws_ref)
                dbs_ref[...] = jnp.zeros_like(dbs_ref)
                dlg_ref[...] = jnp.zeros_like(dlg_ref)
                dlb_ref[...] = jnp.zeros_like(dlb_ref)

        def after():
            @pl.when(i == n_i - 1)
            def _():
                mask = _chunk_mask()
                for g in range(SG_GROUPS):
                    dws_ref[g] = jnp.where(mask, dws_ref[g], 0.0)

        before()
        mu, rstd = _sgu_stats(proj_ref, vg_ref, di, gd, dgel_ref)
        m1 = jnp.zeros((SG_BLOCK, 1), F32)
        m2 = jnp.zeros((SG_BLOCK, 1), F32)
        for g in range(SG_GROUPS):
            cs = slice(g * gd, (g + 1) * gd)
            gs = slice(2 * di + g * gd, 2 * di + (g + 1) * gd)
            gain = lg_ref[:, cs]
            vhat = (vg_ref[:, cs] - mu) * rstd
            vln_b = (vhat * gain + lb_ref[:, cs]).astype(BF16)
            s = _dot(wsm_ref[g], vln_b, NN) + bs_ref[g]
            u, du = _gelu_and_grad(proj_ref[:, cs])
            sg, dsg = _silu_and_grad(proj_ref[:, gs])
            dyv = dy_ref[:, cs].astype(F32)
            dp_ref[:, cs] = (dyv * s * sg * du).astype(BF16)
            dp_ref[:, gs] = (dyv * u * s * dsg).astype(BF16)
            ds = dyv * u * sg
            ds_b = ds.astype(BF16)
            dws_ref[g] = dws_ref[g] + _dot(ds_b, vln_b, NT)
            dbs_ref[g] = dbs_ref[g] + jnp.sum(ds, axis=1, keepdims=True)
            dvln = _dot(wsm_ref[g], ds_b, TN)
            dlg_ref[:, cs] = dlg_ref[:, cs] + jnp.sum(dvln * vhat, axis=0, keepdims=True)
            dlb_ref[:, cs] = dlb_ref[:, cs] + jnp.sum(dvln, axis=0, keepdims=True)
            dvh = dvln * gain
            dvh_ref[:, cs] = dvh
            m1 = m1 + jnp.sum(dvh, axis=1, keepdims=True)
            m2 = m2 + jnp.sum(dvh * vhat, axis=1, keepdims=True)
        m1 = m1 / di
        m2 = m2 / di
        for g in range(SG_GROUPS):
            cs = slice(g * gd, (g + 1) * gd)
            vs = slice(di + g * gd, di + (g + 1) * gd)
            vhat = (vg_ref[:, cs] - mu) * rstd
            dvg = rstd * (dvh_ref[:, cs] - m1 - vhat * m2)
            dp_ref[:, vs] = (dvg * dgel_ref[:, cs]).astype(BF16)

        after()

    full = lambda shape: pl.BlockSpec(shape, lambda i: (0,) * len(shape))
    return _call(
        body, name=name, grid=(n_i,),
        out_shape=[jax.ShapeDtypeStruct((m_rows, n3), BF16),
                   jax.ShapeDtypeStruct((SG_GROUPS, SG_BLOCK, SG_BLOCK), F32),
                   jax.ShapeDtypeStruct((SG_GROUPS, SG_BLOCK, 1), F32),
                   jax.ShapeDtypeStruct((1, di), F32), jax.ShapeDtypeStruct((1, di), F32)],
        in_specs=[pl.BlockSpec((SG_BLOCK, n3), lambda i: (i, 0)),
                  pl.BlockSpec((SG_BLOCK, di), lambda i: (i, 0)),
                  full((1, di)), full((1, di)),
                  full((SG_GROUPS, SG_BLOCK, SG_BLOCK)), full((SG_GROUPS, SG_BLOCK, 1))],
        out_specs=[pl.BlockSpec((SG_BLOCK, n3), lambda i: (i, 0)),
                   full((SG_GROUPS, SG_BLOCK, SG_BLOCK)), full((SG_GROUPS, SG_BLOCK, 1)),
                   full((1, di)), full((1, di))],
        scratch_shapes=[pltpu.VMEM((SG_GROUPS, SG_BLOCK, SG_BLOCK), BF16),
                        pltpu.VMEM((SG_BLOCK, di), F32), pltpu.VMEM((SG_BLOCK, di), F32),
                        pltpu.VMEM((SG_BLOCK, di), F32)],
        compiler_params=_params(),
    )(proj, dy, ln_gain, ln_bias, ws, bs)


def _lower_bound(lbraw):
    mx = jnp.maximum(lbraw[0:1, :], lbraw[1:2, :])
    e0 = jnp.exp(lbraw[0:1, :] - mx)
    e1 = jnp.exp(lbraw[1:2, :] - mx)
    p0 = e0 / (e0 + e1)
    p1 = e1 / (e0 + e1)
    return (p0 + p1) - p0, p0, p1


def _tri(lower):
    r = lax.broadcasted_iota(jnp.int32, (CHUNK, CHUNK), 0)
    c = lax.broadcasted_iota(jnp.int32, (CHUNK, CHUNK), 1)
    return ((r >= c) if lower else (c >= r)).astype(BF16)


def _running_sum(tri, x):
    x1 = x.astype(BF16)
    r1 = x - x1.astype(F32)
    x2 = r1.astype(BF16)
    x3 = (r1 - x2.astype(F32)).astype(BF16)
    return _dot(tri, x1, NN) + _dot(tri, x2, NN) + _dot(tri, x3, NN)


def _row(a, idx):
    r = lax.broadcasted_iota(jnp.int32, a.shape, 0)
    return jnp.sum(jnp.where(r == idx, a, 0.0), axis=0, keepdims=True)


def _hgrn_gates(qp, fp, lb, tri):
    sgm = _sigmoid_small(fp)
    f = lb + (1.0 - lb) * sgm
    k = 1.0 - f
    a = _running_sum(tri, jnp.log(f))
    a_mid = _row(a, CHUNK // 2 - 1)
    a_last = _row(a, CHUNK - 1)
    q, dq = _silu_and_grad(qp)
    e1, e2, e3, e4 = jnp.exp(a - a_mid), jnp.exp(a_mid - a), jnp.exp(a), jnp.exp(a_last - a)
    return dict(sgm=sgm, f=f, k=k, q=q, dq=dq, e1=e1, e2=e2, e3=e3, e4=e4, dec=jnp.exp(a_last),
                q_in=q * e1, k_in=k * e2, q_out=q * e3, k_out=k * e4)


def _causal():
    r = lax.broadcasted_iota(jnp.int32, (CHUNK, CHUNK), 0)
    c = lax.broadcasted_iota(jnp.int32, (CHUNK, CHUNK), 1)
    return r >= c


def hgrn_fwd(proj4, lbraw, gn, seq, name):
    _, m_rows, di = proj4.shape
    nb, nh, nc = m_rows // seq, di // HEAD_DIM, seq // CHUNK
    rows = min(HG_ROWS, seq)
    wide = HG_WIDE * HEAD_DIM
    ns, cpb = seq // rows, rows // CHUNK

    def body(p_ref, lb_ref, gn_ref, y_ref, sts_ref, st_ref):
        @pl.when(pl.program_id(2) == 0)
        def _():
            st_ref[...] = jnp.zeros_like(st_ref)

        tri = _tri(True)
        causal = _causal()
        gain = gn_ref[...]
        lbs = [_lower_bound(lb_ref[:, j * HEAD_DIM:(j + 1) * HEAD_DIM])[0] for j in range(HG_WIDE)]

        units = [(n, j) for n in range(cpb) for j in range(HG_WIDE)]
        rs = lambda n: slice(n * CHUNK, (n + 1) * CHUNK)
        cs = lambda j: slice(j * HEAD_DIM, (j + 1) * HEAD_DIM)
        gates, v_b, sc_b, kv, o_in, o_x = {}, {}, {}, {}, {}, {}
        for n, j in units:
            gates[n, j] = _hgrn_gates(p_ref[0, rs(n), cs(j)], p_ref[1, rs(n), cs(j)], lbs[j], tri)
            v_b[n, j] = p_ref[2, rs(n), cs(j)].astype(BF16)
        for u in units:
            t = gates[u]
            sc_b[u] = jnp.where(causal, _dot(t["q_in"].astype(BF16), t["k_in"].astype(BF16), NT), 0.0).astype(BF16)
            kv[u] = _dot(v_b[u], t["k_out"].astype(BF16), TN)
        for u in units:
            o_in[u] = _dot(sc_b[u], v_b[u], NN)
        for j in range(HG_WIDE):
            st = st_ref[j]
            for n in range(cpb):
                sts_ref[n, :, cs(j)] = st
                o_x[n, j] = _dot(gates[n, j]["q_out"].astype(BF16), st.astype(BF16), NT)
                st = st * gates[n, j]["dec"] + kv[n, j]
            st_ref[j] = st
        for n, j in units:
            o = o_in[n, j] + o_x[n, j]
            r = lax.rsqrt(jnp.mean(o * o, axis=-1, keepdims=True) + EPS)
            gp = p_ref[3, rs(n), cs(j)]
            y_ref[rs(n), cs(j)] = ((o * r * gain) * (gp * _sigmoid(gp))).astype(BF16)

    return _call(
        body, name=name, grid=(nh // HG_WIDE, nb, ns),
        out_shape=[jax.ShapeDtypeStruct((m_rows, di), BF16),
                   jax.ShapeDtypeStruct((nb * nc, HEAD_DIM, di), F32)],
        in_specs=[pl.BlockSpec((4, rows, wide), lambda hg, b, s: (0, b * ns + s, hg)),
                  pl.BlockSpec((2, wide), lambda hg, b, s: (0, hg)),
                  pl.BlockSpec((1, HEAD_DIM), lambda hg, b, s: (0, 0))],
        out_specs=[pl.BlockSpec((rows, wide), lambda hg, b, s: (b * ns + s, hg)),
                   pl.BlockSpec((cpb, HEAD_DIM, wide), lambda hg, b, s: (b * ns + s, 0, hg))],
        scratch_shapes=[pltpu.VMEM((HG_WIDE, HEAD_DIM, HEAD_DIM), F32)],
        compiler_params=_params(),
    )(proj4, lbraw, gn)


def hgrn_bwd(proj4, dy, sts, lbraw, gn, seq, name):
    _, m_rows, di = proj4.shape
    nb, nh, nc = m_rows // seq, di // HEAD_DIM, seq // CHUNK
    rows = min(HG_ROWS, seq)
    wide = HG_WIDE * HEAD_DIM
    ns, cpb = seq // rows, rows // CHUNK
    n_hg = nh // HG_WIDE

    def body(p_ref, dy_ref, sts_ref, lb_ref, gn_ref, dp_ref, dlb_ref, dgn_ref, dst_ref, lbacc_ref, gnacc_ref):
        hg, b, s = pl.program_id(0), pl.program_id(1), pl.program_id(2)
        tri, triu = _tri(True), _tri(False)
        causal = _causal()
        gain = gn_ref[...]
        first = (b == 0) & (s == 0)
        cs = lambda j: slice(j * HEAD_DIM, (j + 1) * HEAD_DIM)

        def before():
            @pl.when((hg == 0) & first)
            def _():
                gnacc_ref[...] = jnp.zeros_like(gnacc_ref)

            @pl.when(first)
            def _():
                lbacc_ref[...] = jnp.zeros_like(lbacc_ref)

            @pl.when(s == 0)
            def _():
                dst_ref[...] = jnp.zeros_like(dst_ref)

        def after():
            @pl.when((b == nb - 1) & (s == ns - 1))
            def _():
                for j in range(HG_WIDE):
                    _, p0, p1 = _lower_bound(lb_ref[:, cs(j)])
                    acc = lbacc_ref[:, cs(j)]
                    dlb_ref[0:1, cs(j)] = -acc * p0 * p1
                    dlb_ref[1:2, cs(j)] = acc * p1 * (1.0 - p1)

            @pl.when((hg == n_hg - 1) & (b == nb - 1) & (s == ns - 1))
            def _():
                tot = gnacc_ref[:, 0:HEAD_DIM]
                for j in range(1, HG_WIDE):
                    tot = tot + gnacc_ref[:, cs(j)]
                dgn_ref[...] = tot

        before()

        units = [(n, j) for n in range(cpb) for j in range(HG_WIDE)]
        rs = lambda n: slice(n * CHUNK, (n + 1) * CHUNK)
        lbs = [_lower_bound(lb_ref[:, cs(j)])[0] for j in range(HG_WIDE)]
        gates, v_b, st_b, sc_b, o, do_b = {}, {}, {}, {}, {}, {}
        dq_out, dsc_b, dv, g_st, dq_in, dk_in, dst_at, dk_out, ddec = {}, {}, {}, {}, {}, {}, {}, {}, {}
        for n, j in units:
            gates[n, j] = _hgrn_gates(p_ref[0, rs(n), cs(j)], p_ref[1, rs(n), cs(j)], lbs[j], tri)
            v_b[n, j] = p_ref[2, rs(n), cs(j)].astype(BF16)
            st_b[n, j] = sts_ref[n, :, cs(j)].astype(BF16)
        for u in units:
            t = gates[u]
            sc_b[u] = jnp.where(causal, _dot(t["q_in"].astype(BF16), t["k_in"].astype(BF16), NT), 0.0).astype(BF16)
        for u in units:
            o[u] = _dot(sc_b[u], v_b[u], NN) + _dot(gates[u]["q_out"].astype(BF16), st_b[u], NT)
        for n, j in units:
            ov = o[n, j]
            r = lax.rsqrt(jnp.mean(ov * ov, axis=-1, keepdims=True) + EPS)
            ohat = ov * r
            sg, dsg = _silu_and_grad(p_ref[3, rs(n), cs(j)])
            dyv = dy_ref[rs(n), cs(j)].astype(F32)
            dp_ref[3, rs(n), cs(j)] = (dyv * (ohat * gain) * dsg).astype(BF16)
            d_on = dyv * sg
            gnacc_ref[:, cs(j)] = gnacc_ref[:, cs(j)] + jnp.sum(d_on * ohat, axis=0, keepdims=True)
            dohat = d_on * gain
            do_b[n, j] = (r * (dohat - ohat * jnp.mean(dohat * ohat, axis=-1, keepdims=True))).astype(BF16)
        for u in units:
            dq_out[u] = _dot(do_b[u], st_b[u], NN)
            dsc_b[u] = jnp.where(causal, _dot(do_b[u], v_b[u], NT), 0.0).astype(BF16)
            dv[u] = _dot(sc_b[u], do_b[u], TN)
            g_st[u] = _dot(do_b[u], gates[u]["q_out"].astype(BF16), TN)
        for u in units:
            dq_in[u] = _dot(dsc_b[u], gates[u]["k_in"].astype(BF16), NN)
            dk_in[u] = _dot(dsc_b[u], gates[u]["q_in"].astype(BF16), TN)
        for j in range(HG_WIDE):
            dst = dst_ref[j]
            for n in reversed(range(cpb)):
                dst_at[n, j] = dst
                dst = dst * gates[n, j]["dec"] + g_st[n, j]
            dst_ref[j] = dst
        for n, j in units:
            dst = dst_at[n, j]
            dst_b = dst.astype(BF16)
            dk_out[n, j] = _dot(v_b[n, j], dst_b, NN)
            dv[n, j] = dv[n, j] + _dot(gates[n, j]["k_out"].astype(BF16), dst_b, NT)
            ddec[n, j] = jnp.sum(dst * sts_ref[n, :, cs(j)], axis=0, keepdims=True)
        for n, j in units:
            t = gates[n, j]
            dp_ref[2, rs(n), cs(j)] = dv[n, j].astype(BF16)
            dq = dq_in[n, j] * t["e1"] + dq_out[n, j] * t["e3"]
            dk = dk_in[n, j] * t["e2"] + dk_out[n, j] * t["e4"]
            w_in = dq_in[n, j] * t["q_in"] - dk_in[n, j] * t["k_in"]
            w_out = dk_out[n, j] * t["k_out"]
            da = w_in + dq_out[n, j] * t["q_out"] - w_out
            da_mid = -jnp.sum(w_in, axis=0, keepdims=True)
            da_last = jnp.sum(w_out, axis=0, keepdims=True) + ddec[n, j] * t["dec"]
            rid = lax.broadcasted_iota(jnp.int32, da.shape, 0)
            da = da + jnp.where(rid == CHUNK // 2 - 1, da_mid, 0.0) + jnp.where(rid == CHUNK - 1, da_last, 0.0)
            dlf = _running_sum(triu, da)
            df = dlf / t["f"] - dk
            sgm = t["sgm"]
            dp_ref[1, rs(n), cs(j)] = (df * (1.0 - lbs[j]) * sgm * (1.0 - sgm)).astype(BF16)
            lbacc_ref[:, cs(j)] = lbacc_ref[:, cs(j)] + jnp.sum(df * (1.0 - sgm), axis=0, keepdims=True)
            dp_ref[0, rs(n), cs(j)] = (dq * t["dq"]).astype(BF16)

        after()

    blk = lambda hg, b, s: b * ns + (ns - 1 - s)
    return _call(
        body, name=name, grid=(n_hg, nb, ns),
        out_shape=[jax.ShapeDtypeStruct((4, m_rows, di), BF16), jax.ShapeDtypeStruct((2, di), F32),
                   jax.ShapeDtypeStruct((1, HEAD_DIM), F32)],
        in_specs=[pl.BlockSpec((4, rows, wide), lambda hg, b, s: (0, blk(hg, b, s), hg)),
                  pl.BlockSpec((rows, wide), lambda hg, b, s: (blk(hg, b, s), hg)),
                  pl.BlockSpec((cpb, HEAD_DIM, wide), lambda hg, b, s: (blk(hg, b, s), 0, hg)),
                  pl.BlockSpec((2, wide), lambda hg, b, s: (0, hg)),
                  pl.BlockSpec((1, HEAD_DIM), lambda hg, b, s: (0, 0))],
        out_specs=[pl.BlockSpec((4, rows, wide), lambda hg, b, s: (0, blk(hg, b, s), hg)),
                   pl.BlockSpec((2, wide), lambda hg, b, s: (0, hg)),
                   pl.BlockSpec((1, HEAD_DIM), lambda hg, b, s: (0, 0))],
        scratch_shapes=[pltpu.VMEM((HG_WIDE, HEAD_DIM, HEAD_DIM), F32), pltpu.VMEM((1, wide), F32),
                        pltpu.VMEM((1, wide), F32)],
        compiler_params=_params(),
    )(proj4, dy, sts, lbraw, gn)


def outproj_loss(y, w, x, mod, fg, target, seq, name):
    m_rows, di = y.shape
    d = w.shape[1]
    tm = min(512, seq)

    def body(y_ref, w_ref, x_ref, mod_ref, fg_ref, t_ref, out_ref, loss_ref, dx_ref, dfg_ref):
        i = pl.program_id(0)
        acc = _dot(y_ref[...], w_ref[...], NN)
        out_ref[...] = acc.astype(BF16)
        xv = x_ref[...] + mod_ref[0][:, 2 * d:] * acc
        gain = fg_ref[...]
        r = lax.rsqrt(jnp.mean(xv * xv, axis=-1, keepdims=True) + EPS)
        xn = xv * r
        e = xn * gain - t_ref[...]
        part = 0.5 * jnp.sum(jnp.mean(e * e, axis=-1, keepdims=True), axis=0, keepdims=True)
        dyv = e / d
        p_fg = jnp.sum(dyv * xn, axis=0, keepdims=True)
        dxn = dyv * gain
        dx_ref[...] = r * (dxn - xn * jnp.mean(dxn * xn, axis=-1, keepdims=True))

        @pl.when(i == 0)
        def _():
            loss_ref[...] = part
            dfg_ref[...] = p_fg

        @pl.when(i != 0)
        def _():
            loss_ref[...] = loss_ref[...] + part
            dfg_ref[...] = dfg_ref[...] + p_fg

    row = pl.BlockSpec((tm, d), lambda i: (i, 0))
    return _call(
        body, name=name, grid=(m_rows // tm,),
        out_shape=[jax.ShapeDtypeStruct((m_rows, d), BF16), jax.ShapeDtypeStruct((1, 1), F32),
                   jax.ShapeDtypeStruct((m_rows, d), F32), jax.ShapeDtypeStruct((1, d), F32)],
        in_specs=[pl.BlockSpec((tm, di), lambda i: (i, 0)),
                  pl.BlockSpec((di, d), lambda i: (0, 0)),
                  row,
                  pl.BlockSpec((1, 1, 3 * d), lambda i: ((i * tm) // seq, 0, 0)),
                  pl.BlockSpec((1, d), lambda i: (0, 0)), row],
        out_specs=[row, pl.BlockSpec((1, 1), lambda i: (0, 0)), row, pl.BlockSpec((1, d), lambda i: (0, 0))],
        compiler_params=_params(),
    )(y, w, x, mod, fg, target)


def _pack(parts):
    flat = jnp.concatenate([p.reshape(-1) for p in parts])
    pad = (-flat.shape[0]) % (8 * LANES)
    return jnp.pad(flat, (0, pad)).reshape(-1, LANES)


def kernel(x, c, norm_gain, w_ada, b_ada, a_w_in, a_ln_gain, a_ln_bias, a_w_s, a_b_s, a_w_out, b_w_in, b_lower_bounds, b_gn_gain, b_w_out, final_gain, loss_target, m_norm_gain, m_w_ada, m_b_ada, m_a_w_in, m_a_ln_gain, m_a_ln_bias, m_a_w_s, m_a_b_s, m_a_w_out, m_b_w_in, m_b_lower_bounds, m_b_gn_gain, m_b_w_out, m_final_gain, v_norm_gain, v_w_ada, v_b_ada, v_a_w_in, v_a_ln_gain, v_a_ln_bias, v_a_w_s, v_a_b_s, v_a_w_out, v_b_w_in, v_b_lower_bounds, v_b_gn_gain, v_b_w_out, v_final_gain):
    nb, seq, d = x.shape
    m_rows = nb * seq
    n_l = w_ada.shape[0]
    ada_cols = w_ada.shape[2]
    px, py, pc = _place()
    chip = 2 * px + py
    dev = 2 * chip + pc

    c_all = allgather_small(c.reshape(-1, LANES), "gather_c").reshape(N_DEV * nb, d)
    s_a = halves_start([cast_into_slot(a_w_in[0], chip, c_all, "cast_a_in"),
                        cast_into_slot(a_w_out[0], chip, c_all, "cast_a_out")], "gather_a_start")
    land_b_in = cast_into_slot(b_w_in[0], chip, s_a[3], "cast_b_in")
    land_b_out = cast_into_slot(b_w_out[0], chip, s_a[3], "cast_b_out")
    b_cols = lax.dynamic_slice_in_dim(b_ada, chip * ada_cols, ada_cols, axis=1).reshape(n_l, 1, ada_cols)
    mod_cols = ada_fwd(c_all, w_ada, b_cols + s_a[3][0, 0], "ada_fwd")
    mod_g = allgather_small(mod_cols.reshape(-1, LANES), "gather_mod")
    mod_g = mod_g.reshape(N_CHIPS, 2, n_l, N_DEV * nb, ada_cols)[:, 0]
    mod_all = jnp.transpose(mod_g, (1, 2, 0, 3)).reshape(n_l, N_DEV * nb, 3 * d)
    mod_mine = lax.dynamic_slice_in_dim(mod_all, dev * nb, nb, axis=1)
    mod0 = mod_mine[0].reshape(nb, 1, 3 * d)
    mod1 = mod_mine[1].reshape(nb, 1, 3 * d)

    landed_a = halves_wait(s_a[0], s_a[1], s_a[2], mod_mine, "gather_a_wait")
    s_bi = gather_start(land_b_in, landed_a[0], "gather_b_in_start")
    s_bo = gather_start(land_b_out, s_bi[3], "gather_b_out_start")
    wa_in, wa_out = pass_halves(list(landed_a), "gather_a_pass")
    di = a_w_out.shape[1] * N_CHIPS
    wa_out = wa_out.reshape(di, d)

    x0 = x.reshape(m_rows, d)
    tgt = loss_target.reshape(m_rows, d)
    ng0 = norm_gain[0:1] + (s_bi[3][0, 0] + s_bo[3][0, 0])
    ng1 = norm_gain[1:2]
    bs_col = a_b_s[0].reshape(SG_GROUPS, SG_BLOCK, 1)
    proj_a, h_a = inproj_fwd(x0, mod0, ng0, wa_in, seq, False, "a_inproj")
    y_a = sgu_fwd(proj_a, a_ln_gain, a_ln_bias, a_w_s[0], bs_col, "a_sgu")
    x1, out_a = outproj_fwd(y_a, wa_out, x0, mod0, seq, "a_outproj")
    wb_in = gather_wait(*s_bi[:3], out_a, "gather_b_in_wait")
    proj_b, h_b = inproj_fwd(x1, mod1, ng1, wb_in, seq, True, "b_inproj")
    y_b, sts_b = hgrn_fwd(proj_b, b_lower_bounds, b_gn_gain, seq, "b_hgrn")
    wb_out = gather_wait(*s_bo[:3], y_b, "gather_b_out_wait").reshape(di, d)
    out_b, loss_part, dx2, dfg = outproj_loss(
        y_b, wb_out, x1, mod1, final_gain.reshape(1, d), tgt, seq, "b_outproj_loss")

    shard_rows = di // N_CHIPS
    dy_b, dout_b, dgate1 = outproj_bwd(dx2, out_b, mod1, wb_out, seq, "b_outproj_bwd")
    gwb_out = grad_w_out(y_b, dout_b, "b_grad_w_out").reshape(N_CHIPS, shard_rows, d)
    e_bo = exchange_start(gwb_out, "exchange_b_out_start")
    dproj_b, dlb, dgn = hgrn_bwd(
        proj_b, dy_b, sts_b, b_lower_bounds, b_gn_gain + e_bo[4][0, 0], seq, "b_hgrn_bwd")
    e_bi = exchange_start(grad_w_in(h_b, dproj_b, N_CHIPS, True, "b_grad_w_in"), "exchange_b_in_start")
    dx1, dshift1, dscale1, dng1 = inproj_bwd(
        dproj_b, wb_in, x1, dx2, mod1, ng1 + e_bi[4][0, 0], seq, True, "b_inproj_bwd")

    dy_a, dout_a, dgate0 = outproj_bwd(dx1, out_a, mod0, wa_out, seq, "a_outproj_bwd")
    gwa_out = grad_w_out(y_a, dout_a, "a_grad_w_out").reshape(N_CHIPS, shard_rows, d)
    e_ao = exchange_start(gwa_out, "exchange_a_out_start")
    dproj_a, dws, dbs, dlg, dlbias = sgu_bwd(
        proj_a, dy_a, a_ln_gain + e_ao[4][0, 0], a_ln_bias, a_w_s[0], bs_col, "a_sgu_bwd")
    e_ai = exchange_start(grad_w_in(h_a, dproj_a, N_CHIPS, False, "a_grad_w_in"), "exchange_a_in_start")
    dx0, dshift0, dscale0, dng0 = inproj_bwd(
        dproj_a, wa_in, x0, dx1, mod0, norm_gain[0:1] + e_ai[4][0, 0], seq, False, "a_inproj_bwd")
    grad_x = dx0.reshape(nb, seq, d)

    dmod = jnp.concatenate([dshift0, dscale0, dgate0, dshift1, dscale1, dgate1], axis=2)
    n_dmod = dmod.size
    small_g = [jnp.concatenate([dng0, dng1], axis=0), dlg, dlbias, dws, dbs, dlb, dfg, dgn]
    packed_g = _pack([dmod] + small_g + [loss_part])
    rows = packed_g.shape[0]
    s_small = gather_all_start(
        lax.dynamic_update_slice(jnp.zeros((N_DEV, rows, LANES), F32), packed_g[None], (dev, 0, 0)),
        "gather_small_start")

    def finish(group, after):
        mine = []
        for ex, _, _, _, nm in group:
            parts_thru, land = exchange_wait(ex[0], ex[1], ex[2], ex[3], after, "exchange_" + nm + "_wait")
            mine.append(sum_parts(parts_thru, land, chip, "sum_" + nm))
            after = mine[-1]
        theirs = swap_sibling(mine, "swap_" + group[0][4])
        return [[r.reshape(w.shape) for r in adamw_pair(pa, pb, w[0], m[0], v[0], "adamw_" + nm)]
                for pa, pb, (_, w, m, v, nm) in zip(mine, theirs, group)]

    (gb_out, db_out, mb_out, vb_out), (gb_in, db_in, mb_in, vb_in), (ga_out, da_out, ma_out, va_out) = finish(
        [(e_bo, b_w_out, m_b_w_out, v_b_w_out, "b_out"), (e_bi, b_w_in, m_b_w_in, v_b_w_in, "b_in"),
         (e_ao, a_w_out, m_a_w_out, v_a_w_out, "a_out")], s_small[3])
    ((ga_in, da_in, ma_in, va_in),) = finish([(e_ai, a_w_in, m_a_w_in, v_a_w_in, "a_in")], ga_out)

    small_w = [norm_gain, a_ln_gain, a_ln_bias, a_w_s, a_b_s, b_lower_bounds, final_gain, b_gn_gain]
    small_m = [m_norm_gain, m_a_ln_gain, m_a_ln_bias, m_a_w_s, m_a_b_s, m_b_lower_bounds, m_final_gain, m_b_gn_gain]
    small_v = [v_norm_gain, v_a_ln_gain, v_a_ln_bias, v_a_w_s, v_a_b_s, v_b_lower_bounds, v_final_gain, v_b_gn_gain]
    rows_of = lambda a: a.reshape(-1, a.shape[-1])
    gathered = gather_all_wait(s_small[0], s_small[1], s_small[2], ga_in, "gather_small_wait")
    tail, small_res = small_update(
        gathered, n_dmod // LANES, [rows_of(a) for a in small_w], [rows_of(a) for a in small_m],
        [rows_of(a) for a in small_v], "small_update")
    loss = tail[0, 0]
    sg, sd, sm, sv = [[small_res[p][kind].reshape(w.shape) for p, w in enumerate(small_w)] for kind in range(4)]

    dmod_all = gathered[:, :n_dmod // LANES].reshape(N_DEV * nb, n_l, 3 * d)
    dmod_cols = lax.dynamic_slice_in_dim(dmod_all, chip * ada_cols, ada_cols, axis=2)
    dmod_cols = jnp.transpose(dmod_cols, (1, 0, 2))
    g_wada, d_wada, m_wada, v_wada = ada_bwd(c_all, dmod_cols, w_ada, m_w_ada, v_w_ada, "ada_bwd")
    flat = lambda a: a.reshape(1, -1)
    g_bada, d_bada, m_bada, v_bada = [
        r.reshape(b_ada.shape) for r in
        bias_update(dmod_all.reshape(N_DEV * nb, n_l * 3 * d), flat(b_ada), flat(m_b_ada), flat(v_b_ada), "bias_update")]

    def order(ng, wada, bada, ain, sm_rest, aout, bin_, bout):
        lg, lbi, ws_, bs_, lbd, fg_, gn_ = sm_rest
        return [ng, wada, bada, ain, lg, lbi, ws_, bs_, aout, bin_, lbd, gn_, bout, fg_]

    grads = order(sg[0], g_wada, g_bada, ga_in, sg[1:8], ga_out, gb_in, gb_out)
    deltas = order(sd[0], d_wada, d_bada, da_in, sd[1:8], da_out, db_in, db_out)
    new_m = order(sm[0], m_wada, m_bada, ma_in, sm[1:8], ma_out, mb_in, mb_out)
    new_v = order(sv[0], v_wada, v_bada, va_in, sv[1:8], va_out, vb_in, vb_out)
    return (loss, grad_x, *grads, *deltas, *new_m, *new_v)
```

```python
import jax
import jax.numpy as jnp
from jax import lax
from jax.experimental import pallas as pl
from jax.experimental.pallas import tpu as pltpu

F32 = jnp.float32
BF16 = jnp.bfloat16
EPS = 1e-6
CHUNK = 64
SG_BLOCK = 128
SG_GROUPS = 8
HEAD_DIM = 128
HG_WIDE = 8
HG_ROWS = 256
N_CHIPS = 4
N_DEV = 8
LANES = 128
ADAM_LR = 0.001
ADAM_B1 = 0.9
ADAM_B2 = 0.999
ADAM_EPS = 1e-08
ADAM_WD = 0.01
ADAM_STEP = 10
GELU_C0 = 0.7978845608028654
GELU_C1 = 0.044715
MESH = pl.DeviceIdType.MESH
VMEM_LIMIT = 56 * 1024 * 1024


ROW_TILE = 1024


def _col_tile(n):
    return next(t for t in (1024, 768, 512, 256) if n % t == 0)


def _call(body, **kw):
    return pl.pallas_call(body, **kw)


def _params(**kw):
    return pltpu.CompilerParams(vmem_limit_bytes=VMEM_LIMIT, **kw)


def _sigmoid(x):
    return 0.5 * jnp.tanh(0.5 * x) + 0.5


def _sigmoid_small(x):
    return 1.0 / (1.0 + jnp.exp(-x))


def _silu_and_grad(x):
    s = _sigmoid(x)
    return x * s, s * (1.0 + x * (1.0 - s))


def _gelu(x):
    return 0.5 * x * (1.0 + jnp.tanh(GELU_C0 * (x + GELU_C1 * x * x * x)))


def _gelu_and_grad(x):
    t = jnp.tanh(GELU_C0 * (x + GELU_C1 * x * x * x))
    g = 0.5 * x * (1.0 + t)
    dg = 0.5 * (1.0 + t) + 0.5 * x * (1.0 - t * t) * (GELU_C0 * (1.0 + 3.0 * GELU_C1 * x * x))
    return g, dg


def _dot(a, b, dims, precision=None):
    return lax.dot_general(a, b, (dims, ((), ())), precision=precision, preferred_element_type=F32)


NN = ((1,), (0,))
NT = ((1,), (1,))
TN = ((0,), (0,))


def _adamw(w, g, m, v):
    m = ADAM_B1 * m + (1.0 - ADAM_B1) * g
    v = ADAM_B2 * v + (1.0 - ADAM_B2) * (g * g)
    m_hat = m / (1.0 - ADAM_B1 ** ADAM_STEP)
    v_hat = v / (1.0 - ADAM_B2 ** ADAM_STEP)
    delta = -ADAM_LR * (m_hat / (jnp.sqrt(v_hat) + ADAM_EPS) + ADAM_WD * w)
    return delta, m, v


def _chunk_mask():
    r = lax.broadcasted_iota(jnp.int32, (SG_BLOCK, SG_BLOCK), 0)
    c = lax.broadcasted_iota(jnp.int32, (SG_BLOCK, SG_BLOCK), 1)
    return (c // CHUNK) <= (r // CHUNK)


def _place():
    return lax.axis_index("x"), lax.axis_index("y"), lax.axis_index("c")


def _other_chips(x, y):
    return [(1 - x, y), (x, 1 - y), (1 - x, 1 - y)]


def allgather_small(v, name):
    m_per, n = v.shape

    def body(x_ref, out_ref, send_sems, recv_sems, local_sem):
        x, y, c = _place()
        me, sibling = (x, y, c), (x, y, 1 - c)
        chips = _other_chips(x, y)

        def rows(px, py, pc):
            return out_ref.at[pl.ds((4 * px + 2 * py + pc) * m_per, m_per), :]

        def copy(k, block, to, src=None):
            return pltpu.make_async_remote_copy(
                src_ref=rows(*block) if src is None else src, dst_ref=rows(*block),
                send_sem=send_sems.at[k], recv_sem=recv_sems.at[k], device_id=to, device_id_type=MESH)

        mine = pltpu.make_async_copy(x_ref, rows(*me), local_sem)
        mine.start()
        first = [copy(0, me, sibling, src=x_ref)]
        first += [copy(1 + j, me, (*chip, c), src=x_ref) for j, chip in enumerate(chips)]
        for cp in first:
            cp.start()
        passed = [copy(4 + j, (*chip, c), sibling) for j, chip in enumerate(chips)]
        for j, chip in enumerate(chips):
            copy(1 + j, (*chip, c), me).wait_recv()
            passed[j].start()
        copy(0, sibling, me).wait_recv()
        for j, chip in enumerate(chips):
            copy(4 + j, (*chip, 1 - c), me).wait_recv()
        for cp in first + passed:
            cp.wait_send()
        mine.wait()

    return _call(
        body, name=name,
        out_shape=jax.ShapeDtypeStruct((N_DEV * m_per, n), v.dtype),
        in_specs=[pl.BlockSpec(memory_space=pltpu.VMEM)],
        out_specs=pl.BlockSpec(memory_space=pltpu.VMEM),
        scratch_shapes=[pltpu.SemaphoreType.DMA((7,)), pltpu.SemaphoreType.DMA((7,)), pltpu.SemaphoreType.DMA],
    )(v)


def _hbm_spec():
    return pl.BlockSpec(memory_space=pltpu.HBM)


def _sem_spec():
    return pl.BlockSpec(memory_space=pltpu.SEMAPHORE)


def _split_params():
    return pltpu.CompilerParams(has_side_effects=pltpu.SideEffectType.DATAFLOW_SIDE_EFFECTING)


def _hbm(a):
    return pltpu.with_memory_space_constraint(a, pltpu.HBM)


def _half_copy(land_ref, rows, chip_idx, core_half, send_sem, recv_sem, to):
    half = land_ref.at[chip_idx, pl.ds(core_half * (rows // 2), rows // 2), :]
    return pltpu.make_async_remote_copy(
        src_ref=half, dst_ref=half, send_sem=send_sem, recv_sem=recv_sem, device_id=to, device_id_type=MESH)


def halves_start(lands, name):
    n = len(lands)

    def body(*refs):
        land_refs, send_sems, recv_sems, token = refs[:n], refs[n], refs[n + 1], refs[-1]
        x, y, c = _place()
        for w in range(n):
            for j, (px, py) in enumerate(_other_chips(x, y)):
                _half_copy(land_refs[w], lands[w].shape[1], 2 * x + y, c,
                           send_sems.at[3 * w + j], recv_sems.at[3 * w + j], (px, py, c)).start()
        token[...] = jnp.zeros_like(token)

    res = _call(
        body, name=name,
        out_shape=(pltpu.SemaphoreType.DMA((3 * n,)), pltpu.SemaphoreType.DMA((3 * n,)),
                   *[pltpu.HBM(a.shape, a.dtype) for a in lands], jax.ShapeDtypeStruct((8, LANES), F32)),
        in_specs=(_hbm_spec(),) * n,
        out_specs=(_sem_spec(), _sem_spec(), *[_hbm_spec()] * n, pl.BlockSpec(memory_space=pltpu.VMEM)),
        input_output_aliases={w: 2 + w for w in range(n)}, compiler_params=_split_params(),
    )(*[_hbm(a) for a in lands])
    return res[0], res[1], list(res[2:2 + n]), res[2 + n]


def halves_wait(send_sems, recv_sems, lands, after, name):
    n = len(lands)

    def body(*refs):
        land_refs, send_sems, recv_sems = refs[:n], refs[n], refs[n + 1]
        x, y, c = _place()
        for w in range(n):
            for j, (px, py) in enumerate(_other_chips(x, y)):
                cp = _half_copy(land_refs[w], lands[w].shape[1], 2 * px + py, c,
                                send_sems.at[3 * w + j], recv_sems.at[3 * w + j], (px, py, c))
                cp.wait_send()
                cp.wait_recv()

    return _call(
        body, name=name,
        out_shape=tuple(pltpu.HBM(a.shape, a.dtype) for a in lands),
        in_specs=(*[_hbm_spec()] * n, _sem_spec(), _sem_spec(), pl.BlockSpec(memory_space=pl.ANY)),
        out_specs=tuple(_hbm_spec() for _ in lands), input_output_aliases={w: w for w in range(n)},
        compiler_params=_split_params(),
    )(*lands, send_sems, recv_sems, after)


def pass_halves(lands, name):
    n = len(lands)

    def body(*refs):
        land_refs, send_sems, recv_sems = refs[n:2 * n], refs[2 * n], refs[2 * n + 1]
        x, y, c = _place()
        sent = []
        for w in range(n):
            for j, (px, py) in enumerate(_other_chips(x, y)):
                cp = _half_copy(land_refs[w], lands[w].shape[1], 2 * px + py, c,
                                send_sems.at[3 * w + j], recv_sems.at[3 * w + j], (x, y, 1 - c))
                cp.start()
                sent.append(cp)
        for w in range(n):
            for j, (px, py) in enumerate(_other_chips(x, y)):
                _half_copy(land_refs[w], lands[w].shape[1], 2 * px + py, 1 - c,
                           send_sems.at[3 * w + j], recv_sems.at[3 * w + j], (x, y, 1 - c)).wait_recv()
        for cp in sent:
            cp.wait_send()

    return _call(
        body, name=name,
        out_shape=[jax.ShapeDtypeStruct(a.shape, a.dtype) for a in lands],
        in_specs=[_hbm_spec()] * n, out_specs=[_hbm_spec()] * n,
        input_output_aliases={w: w for w in range(n)},
        scratch_shapes=[pltpu.SemaphoreType.DMA((3 * n,)), pltpu.SemaphoreType.DMA((3 * n,))],
    )(*lands)


def gather_start(land, after, name):
    def body(land_ref, after_ref, send_sems, recv_sems, land_thru, token):
        del after_ref, land_thru
        x, y, c = _place()
        for j, (px, py) in enumerate(_other_chips(x, y)):
            pltpu.make_async_remote_copy(
                src_ref=land_ref.at[2 * x + y], dst_ref=land_ref.at[2 * x + y],
                send_sem=send_sems.at[j], recv_sem=recv_sems.at[j], device_id=(px, py, c),
                device_id_type=MESH).start()
        token[...] = jnp.zeros_like(token)

    return _call(
        body, name=name,
        out_shape=(pltpu.SemaphoreType.DMA((3,)), pltpu.SemaphoreType.DMA((3,)),
                   pltpu.HBM(land.shape, land.dtype), jax.ShapeDtypeStruct((8, LANES), F32)),
        in_specs=(_hbm_spec(), pl.BlockSpec(memory_space=pl.ANY)),
        out_specs=(_sem_spec(), _sem_spec(), _hbm_spec(), pl.BlockSpec(memory_space=pltpu.VMEM)),
        input_output_aliases={0: 2}, compiler_params=_split_params(),
    )(_hbm(land), after)


def gather_wait(send_sems, recv_sems, land, after, name):
    def body(land_ref, send_sems, recv_sems, after_ref, land_out):
        del after_ref, land_out
        x, y, c = _place()
        for j, (px, py) in enumerate(_other_chips(x, y)):
            cp = pltpu.make_async_remote_copy(
                src_ref=land_ref.at[2 * x + y], dst_ref=land_ref.at[2 * px + py],
                send_sem=send_sems.at[j], recv_sem=recv_sems.at[j], device_id=(px, py, c), device_id_type=MESH)
            cp.wait_send()
            cp.wait_recv()

    return _call(
        body, name=name,
        out_shape=pltpu.HBM(land.shape, land.dtype),
        in_specs=(_hbm_spec(), _sem_spec(), _sem_spec(), pl.BlockSpec(memory_space=pl.ANY)),
        out_specs=_hbm_spec(), input_output_aliases={0: 0}, compiler_params=_split_params(),
    )(land, send_sems, recv_sems, after)


def _flips():
    return [(fx, fy, fc) for fx in (0, 1) for fy in (0, 1) for fc in (0, 1) if (fx, fy, fc) != (0, 0, 0)]


def _flipped(x, y, c, flip):
    fx, fy, fc = flip
    return (1 - x if fx else x, 1 - y if fy else y, 1 - c if fc else c)


def gather_all_start(land, name):
    def body(land_ref, send_sems, recv_sems, land_thru, token):
        del land_thru
        x, y, c = _place()
        for k, flip in enumerate(_flips()):
            pltpu.make_async_remote_copy(
                src_ref=land_ref.at[4 * x + 2 * y + c], dst_ref=land_ref.at[4 * x + 2 * y + c],
                send_sem=send_sems.at[k], recv_sem=recv_sems.at[k], device_id=_flipped(x, y, c, flip),
                device_id_type=MESH).start()
        token[...] = jnp.zeros_like(token)

    return _call(
        body, name=name,
        out_shape=(pltpu.SemaphoreType.DMA((7,)), pltpu.SemaphoreType.DMA((7,)),
                   pltpu.HBM(land.shape, land.dtype), jax.ShapeDtypeStruct((8, LANES), F32)),
        in_specs=(_hbm_spec(),),
        out_specs=(_sem_spec(), _sem_spec(), _hbm_spec(), pl.BlockSpec(memory_space=pltpu.VMEM)),
        input_output_aliases={0: 2}, compiler_params=_split_params(),
    )(_hbm(land))


def gather_all_wait(send_sems, recv_sems, land, after, name):
    def body(land_ref, send_sems, recv_sems, after_ref, land_out):
        del after_ref, land_out
        x, y, c = _place()
        for k, flip in enumerate(_flips()):
            px, py, pc = _flipped(x, y, c, flip)
            cp = pltpu.make_async_remote_copy(
                src_ref=land_ref.at[4 * x + 2 * y + c], dst_ref=land_ref.at[4 * px + 2 * py + pc],
                send_sem=send_sems.at[k], recv_sem=recv_sems.at[k], device_id=(px, py, pc), device_id_type=MESH)
            cp.wait_send()
            cp.wait_recv()

    return _call(
        body, name=name,
        out_shape=pltpu.HBM(land.shape, land.dtype),
        in_specs=(_hbm_spec(), _sem_spec(), _sem_spec(), pl.BlockSpec(memory_space=pl.ANY)),
        out_specs=_hbm_spec(), input_output_aliases={0: 0}, compiler_params=_split_params(),
    )(land, send_sems, recv_sems, after)


def exchange_start(parts, name):
    _, r, c_ = parts.shape

    def body(parts_ref, land_ref, send_sems, recv_sems, parts_thru, land_thru, token):
        del parts_thru, land_thru
        x, y, c = _place()
        for j, (px, py) in enumerate(_other_chips(x, y)):
            pltpu.make_async_remote_copy(
                src_ref=parts_ref.at[2 * px + py], dst_ref=land_ref.at[j],
                send_sem=send_sems.at[j], recv_sem=recv_sems.at[j], device_id=(px, py, c),
                device_id_type=MESH).start()
        token[...] = jnp.zeros_like(token)

    return _call(
        body, name=name,
        out_shape=(pltpu.SemaphoreType.DMA((3,)), pltpu.SemaphoreType.DMA((3,)),
                   pltpu.HBM(parts.shape, parts.dtype), pltpu.HBM((3, r, c_), parts.dtype),
                   jax.ShapeDtypeStruct((8, LANES), F32)),
        in_specs=(_hbm_spec(), _hbm_spec()),
        out_specs=(_sem_spec(), _sem_spec(), _hbm_spec(), _hbm_spec(), pl.BlockSpec(memory_space=pltpu.VMEM)),
        input_output_aliases={0: 2, 1: 3}, compiler_params=_split_params(),
    )(_hbm(parts), _hbm(lax.empty((3, r, c_), parts.dtype)))


def exchange_wait(send_sems, recv_sems, parts, land, after, name):
    def body(parts_ref, land_ref, send_sems, recv_sems, after_ref, parts_out, land_out):
        del after_ref, parts_out, land_out
        x, y, c = _place()
        for j, (px, py) in enumerate(_other_chips(x, y)):
            cp = pltpu.make_async_remote_copy(
                src_ref=parts_ref.at[2 * px + py], dst_ref=land_ref.at[j],
                send_sem=send_sems.at[j], recv_sem=recv_sems.at[j], device_id=(px, py, c), device_id_type=MESH)
            cp.wait_send()
            cp.wait_recv()

    return _call(
        body, name=name,
        out_shape=(pltpu.HBM(parts.shape, parts.dtype), pltpu.HBM(land.shape, land.dtype)),
        in_specs=(_hbm_spec(), _hbm_spec(), _sem_spec(), _sem_spec(), pl.BlockSpec(memory_space=pl.ANY)),
        out_specs=(_hbm_spec(), _hbm_spec()), input_output_aliases={0: 0, 1: 1},
        compiler_params=_split_params(),
    )(parts, land, send_sems, recv_sems, after)


def cast_into_slot(w, chip, after, name):
    r, c = w.shape
    tr = min(256, r)

    def body(s_ref, w_ref, after_ref, o_ref):
        del s_ref, after_ref
        o_ref[...] = w_ref[...].astype(BF16)

    return _call(
        body, name=name,
        grid_spec=pltpu.PrefetchScalarGridSpec(
            num_scalar_prefetch=1, grid=(r // tr,),
            in_specs=[pl.BlockSpec((tr, c), lambda i, s: (i, 0)), pl.BlockSpec(memory_space=pl.ANY)],
            out_specs=pl.BlockSpec((None, tr, c), lambda i, s: (s[0], i, 0))),
        out_shape=jax.ShapeDtypeStruct((N_CHIPS, r, c), BF16),
        compiler_params=_params(),
    )(chip.reshape(1).astype(jnp.int32), w, after)


def sum_parts(parts, land, chip, name):
    _, r, c = parts.shape
    tr = min(256, r)

    def body(s_ref, p_ref, l_ref, o_ref):
        del s_ref
        acc = p_ref[...].astype(F32) + l_ref[0].astype(F32)
        acc = acc + l_ref[1].astype(F32)
        o_ref[...] = (acc + l_ref[2].astype(F32)).astype(BF16)

    return _call(
        body, name=name,
        grid_spec=pltpu.PrefetchScalarGridSpec(
            num_scalar_prefetch=1, grid=(r // tr,),
            in_specs=[pl.BlockSpec((None, tr, c), lambda i, s: (s[0], i, 0)),
                      pl.BlockSpec((3, tr, c), lambda i, s: (0, i, 0))],
            out_specs=pl.BlockSpec((tr, c), lambda i, s: (i, 0))),
        out_shape=jax.ShapeDtypeStruct((r, c), BF16),
        compiler_params=_params(),
    )(chip.reshape(1).astype(jnp.int32), parts, land)


def swap_sibling(arrs, name):
    n = len(arrs)

    def body(*refs):
        ins, outs = refs[:n], refs[n:2 * n]
        send_sems, recv_sems = refs[2 * n:]
        x, y, c = _place()
        cps = []
        for w in range(n):
            cp = pltpu.make_async_remote_copy(
                src_ref=ins[w], dst_ref=outs[w], send_sem=send_sems.at[w], recv_sem=recv_sems.at[w],
                device_id=(x, y, 1 - c), device_id_type=MESH)
            cp.start()
            cps.append(cp)
        for cp in cps:
            cp.wait_recv()
        for cp in cps:
            cp.wait_send()

    return _call(
        body, name=name,
        out_shape=[jax.ShapeDtypeStruct(a.shape, a.dtype) for a in arrs],
        in_specs=[_hbm_spec()] * n, out_specs=[_hbm_spec()] * n,
        scratch_shapes=[pltpu.SemaphoreType.DMA((n,)), pltpu.SemaphoreType.DMA((n,))],
    )(*arrs)


def adamw_pair(pa, pb, w, m, v, name):
    r, c = w.shape
    tr = min(128, r)

    def body(pa_ref, pb_ref, w_ref, m_ref, v_ref, g_ref, d_ref, nm_ref, nv_ref):
        g = pa_ref[...].astype(F32) + pb_ref[...].astype(F32)
        d, nm, nv = _adamw(w_ref[...], g, m_ref[...], v_ref[...])
        g_ref[...] = g
        d_ref[...] = d
        nm_ref[...] = nm
        nv_ref[...] = nv

    spec = pl.BlockSpec((tr, c), lambda i: (i, 0))
    return _call(
        body, name=name, grid=(r // tr,),
        out_shape=[jax.ShapeDtypeStruct((r, c), F32)] * 4,
        in_specs=[spec] * 5, out_specs=[spec] * 4,
        compiler_params=_params(),
    )(pa, pb, w, m, v)


def small_update(gathered, first_row, ws, ms, vs, name):
    n_w = len(ws)
    total_rows = gathered.shape[1]

    def body(*refs):
        g_ref = refs[0]
        w_refs, m_refs, v_refs = refs[1:1 + n_w], refs[1 + n_w:1 + 2 * n_w], refs[1 + 2 * n_w:1 + 3 * n_w]
        tail_ref = refs[1 + 3 * n_w]
        outs = refs[2 + 3 * n_w:2 + 7 * n_w]
        sum_ref = refs[2 + 7 * n_w]
        acc = g_ref[0]
        for k in range(1, N_DEV):
            acc = acc + g_ref[k]
        sum_ref[...] = acc
        row = first_row
        for p in range(n_w):
            a, b = ws[p].shape
            per = b // LANES
            g_out, d_out, m_out, v_out = outs[4 * p:4 * p + 4]
            if per == 1:
                g_out[...] = sum_ref[row:row + a, :]
            else:
                for i in range(a):
                    for jc in range(per):
                        g_out[i:i + 1, jc * LANES:(jc + 1) * LANES] = sum_ref[row + i * per + jc:row + i * per + jc + 1, :]
            row += a * per
            dl, nm, nv = _adamw(w_refs[p][...], g_out[...], m_refs[p][...], v_refs[p][...])
            d_out[...] = dl
            m_out[...] = nm
            v_out[...] = nv
        tail_ref[...] = sum_ref[row:row + 1, :]

    out_shape = [jax.ShapeDtypeStruct((1, LANES), F32)]
    for w in ws:
        out_shape += [jax.ShapeDtypeStruct(w.shape, F32)] * 4
    res = _call(
        body, name=name, out_shape=out_shape,
        scratch_shapes=[pltpu.VMEM((total_rows, LANES), F32)],
        compiler_params=_params(),
    )(gathered, *ws, *ms, *vs)
    return res[0], [res[1 + 4 * p:5 + 4 * p] for p in range(n_w)]


def ada_fwd(c_all, w_ada, b_cols, after, name):
    n_l, d, cols = w_ada.shape
    nb = c_all.shape[0]
    tn = 256

    def body(c_ref, w_ref, b_ref, after_ref, o_ref):
        del after_ref
        cv = c_ref[...]
        ca = (cv * _sigmoid(cv)).astype(BF16)
        o_ref[...] = _dot(ca, w_ref[...].astype(BF16), NN) + b_ref[...]

    return _call(
        body, name=name, grid=(n_l, cols // tn),
        out_shape=jax.ShapeDtypeStruct((n_l, nb, cols), F32),
        in_specs=[pl.BlockSpec((nb, d), lambda l, j: (0, 0)),
                  pl.BlockSpec((None, d, tn), lambda l, j: (l, 0, j)),
                  pl.BlockSpec((None, 1, tn), lambda l, j: (l, 0, j)),
                  pl.BlockSpec(memory_space=pl.ANY)],
        out_specs=pl.BlockSpec((None, nb, tn), lambda l, j: (l, 0, j)),
        compiler_params=_params(),
    )(c_all, w_ada, b_cols, after)


def ada_bwd(c_all, dmod_cols, w, m, v, name):
    n_l, d, cols = w.shape
    nb = c_all.shape[0]
    tn = 256

    def body(c_ref, dm_ref, w_ref, m_ref, v_ref, g_ref, d_ref, nm_ref, nv_ref):
        cv = c_ref[...]
        ca = (cv * _sigmoid(cv)).astype(BF16)
        g = _dot(ca, dm_ref[...].astype(BF16), TN)
        dl, nm, nv = _adamw(w_ref[...], g, m_ref[...], v_ref[...])
        g_ref[...] = g
        d_ref[...] = dl
        nm_ref[...] = nm
        nv_ref[...] = nv

    wspec = pl.BlockSpec((None, d, tn), lambda l, j: (l, 0, j))
    return _call(
        body, name=name, grid=(n_l, cols // tn),
        out_shape=[jax.ShapeDtypeStruct((n_l, d, cols), F32)] * 4,
        in_specs=[pl.BlockSpec((nb, d), lambda l, j: (0, 0)),
                  pl.BlockSpec((None, nb, tn), lambda l, j: (l, 0, j)),
                  wspec, wspec, wspec],
        out_specs=[wspec] * 4,
        compiler_params=_params(),
    )(c_all, dmod_cols, w, m, v)


def bias_update(dmod_all, w, m, v, name):
    def body(dm_ref, w_ref, m_ref, v_ref, g_ref, d_ref, nm_ref, nv_ref):
        g = jnp.sum(dm_ref[...], axis=0, keepdims=True)
        dl, nm, nv = _adamw(w_ref[...], g, m_ref[...], v_ref[...])
        g_ref[...] = g
        d_ref[...] = dl
        nm_ref[...] = nm
        nv_ref[...] = nv

    return _call(
        body, name=name,
        out_shape=[jax.ShapeDtypeStruct(w.shape, F32)] * 4,
        compiler_params=_params(),
    )(dmod_all, w, m, v)


def inproj_fwd(x, mod, ng, wg, seq, sectioned, name):
    m_rows, d = x.shape
    nsh, _, ns = wg.shape
    n = nsh * ns
    tm, tn = min(2 * ROW_TILE, seq), _col_tile(ns)
    per = ns // tn

    def body(x_ref, mod_ref, ng_ref, w_ref, proj_ref, h_ref):
        @pl.when(pl.program_id(1) == 0)
        def _():
            xv = x_ref[...]
            r = lax.rsqrt(jnp.mean(xv * xv, axis=-1, keepdims=True) + EPS)
            md = mod_ref[0]
            h = (xv * r * ng_ref[...]) * (1.0 + md[:, d:2 * d]) + md[:, :d]
            h_ref[...] = h.astype(BF16)
        proj_ref[...] = _dot(h_ref[...], w_ref[...], NN)

    if sectioned:
        proj_shape = (nsh, m_rows, ns)
        proj_spec = pl.BlockSpec((None, tm, tn), lambda i, j: (j // per, i, j % per))
    else:
        proj_shape = (m_rows, n)
        proj_spec = pl.BlockSpec((tm, tn), lambda i, j: (i, j))
    return _call(
        body, name=name, grid=(m_rows // tm, n // tn),
        out_shape=[jax.ShapeDtypeStruct(proj_shape, F32), jax.ShapeDtypeStruct((m_rows, d), BF16)],
        in_specs=[pl.BlockSpec((tm, d), lambda i, j: (i, 0)),
                  pl.BlockSpec((1, 1, 3 * d), lambda i, j: ((i * tm) // seq, 0, 0)),
                  pl.BlockSpec((1, d), lambda i, j: (0, 0)),
                  pl.BlockSpec((None, d, tn), lambda i, j: (j // per, 0, j % per))],
        out_specs=[proj_spec, pl.BlockSpec((tm, d), lambda i, j: (i, 0))],
        compiler_params=_params(),
    )(x, mod, ng, wg)


def outproj_fwd(y, w, x, mod, seq, name):
    m_rows, di = y.shape
    d = w.shape[1]
    tm = min(ROW_TILE, seq)

    def body(y_ref, w_ref, x_ref, mod_ref, xn_ref, out_ref):
        acc = _dot(y_ref[...], w_ref[...], NN)
        out_ref[...] = acc.astype(BF16)
        xn_ref[...] = x_ref[...] + mod_ref[0][:, 2 * d:] * acc

    row = pl.BlockSpec((tm, d), lambda i: (i, 0))
    return _call(
        body, name=name, grid=(m_rows // tm,),
        out_shape=[jax.ShapeDtypeStruct((m_rows, d), F32), jax.ShapeDtypeStruct((m_rows, d), BF16)],
        in_specs=[pl.BlockSpec((tm, di), lambda i: (i, 0)),
                  pl.BlockSpec((di, d), lambda i: (0, 0)),
                  row,
                  pl.BlockSpec((1, 1, 3 * d), lambda i: ((i * tm) // seq, 0, 0))],
        out_specs=[row, row],
        compiler_params=_params(),
    )(y, w, x, mod)


def outproj_bwd(dxo, out, mod, w, seq, name):
    m_rows, d = dxo.shape
    di = w.shape[0]
    nb = m_rows // seq
    tm, tn = min(ROW_TILE, seq), _col_tile(di)

    def body(dxo_ref, out_ref, mod_ref, w_ref, dy_ref, dout_ref, dgate_ref):
        i = pl.program_id(0)

        @pl.when(pl.program_id(1) == 0)
        def _():
            dx = dxo_ref[...]
            dout_ref[...] = (mod_ref[0][:, 2 * d:] * dx).astype(BF16)
            part = jnp.sum(dx * out_ref[...].astype(F32), axis=0, keepdims=True)

            @pl.when((i * tm) % seq == 0)
            def _():
                dgate_ref[0] = part

            @pl.when((i * tm) % seq != 0)
            def _():
                dgate_ref[0] = dgate_ref[0] + part

        dy_ref[...] = _dot(dout_ref[...], w_ref[...], NT).astype(BF16)

    row = pl.BlockSpec((tm, d), lambda i, j: (i, 0))
    return _call(
        body, name=name, grid=(m_rows // tm, di // tn),
        out_shape=[jax.ShapeDtypeStruct((m_rows, di), BF16), jax.ShapeDtypeStruct((m_rows, d), BF16),
                   jax.ShapeDtypeStruct((nb, 1, d), F32)],
        in_specs=[row, row,
                  pl.BlockSpec((1, 1, 3 * d), lambda i, j: ((i * tm) // seq, 0, 0)),
                  pl.BlockSpec((tn, d), lambda i, j: (j, 0))],
        out_specs=[pl.BlockSpec((tm, tn), lambda i, j: (i, j)), row,
                   pl.BlockSpec((1, 1, d), lambda i, j: ((i * tm) // seq, 0, 0))],
        compiler_params=_params(),
    )(dxo, out, mod, w)


def grad_w_out(y, dout, name):
    m_rows, di = y.shape
    d = dout.shape[1]
    tm, tk = min(ROW_TILE, m_rows), _col_tile(di)
    n_m = m_rows // tm

    def body(y_ref, do_ref, o_ref, acc_ref):
        mi = pl.program_id(1)

        @pl.when(mi == 0)
        def _():
            acc_ref[...] = jnp.zeros_like(acc_ref)

        acc_ref[...] += _dot(y_ref[...], do_ref[...], TN)

        @pl.when(mi == n_m - 1)
        def _():
            o_ref[...] = acc_ref[...].astype(BF16)

    return _call(
        body, name=name, grid=(di // tk, n_m),
        out_shape=jax.ShapeDtypeStruct((di, d), BF16),
        in_specs=[pl.BlockSpec((tm, tk), lambda j, mi: (mi, j)),
                  pl.BlockSpec((tm, d), lambda j, mi: (mi, 0))],
        out_specs=pl.BlockSpec((tk, d), lambda j, mi: (j, 0)),
        scratch_shapes=[pltpu.VMEM((tk, d), F32)],
        compiler_params=_params(),
    )(y, dout)


def grad_w_in(h, dproj, nsh, sectioned, name):
    m_rows, d = h.shape
    n = dproj.shape[0] * dproj.shape[2] if sectioned else dproj.shape[1]
    ns = n // nsh
    tm, tn = min(ROW_TILE, m_rows), ns
    per = ns // tn
    n_m = m_rows // tm

    def body(h_ref, dp_ref, o_ref, acc_ref):
        mi = pl.program_id(1)
        @pl.when(mi == 0)
        def _():
            acc_ref[...] = jnp.zeros_like(acc_ref)

        acc_ref[...] += _dot(h_ref[...], dp_ref[...], TN)

        @pl.when(mi == n_m - 1)
        def _():
            o_ref[...] = acc_ref[...].astype(BF16)

    if sectioned:
        dp_spec = pl.BlockSpec((None, tm, tn), lambda j, mi: (j // per, mi, j % per))
    else:
        dp_spec = pl.BlockSpec((tm, tn), lambda j, mi: (mi, j))
    return _call(
        body, name=name, grid=(n // tn, n_m),
        out_shape=jax.ShapeDtypeStruct((nsh, d, ns), BF16),
        in_specs=[pl.BlockSpec((tm, d), lambda j, mi: (mi, 0)), dp_spec],
        out_specs=pl.BlockSpec((None, d, tn), lambda j, mi: (j // per, 0, j % per)),
        scratch_shapes=[pltpu.VMEM((d, tn), F32)],
        compiler_params=_params(),
    )(h, dproj)


def inproj_bwd(dproj, wg, x, dxo, mod, ng, seq, sectioned, name):
    m_rows, d = x.shape
    nsh, _, ns = wg.shape
    n = nsh * ns
    nb = m_rows // seq
    tm, tk = min(ROW_TILE, seq), ns
    per = ns // tk
    n_k = n // tk

    def body(dp_ref, w_ref, x_ref, dxo_ref, mod_ref, ng_ref, dxi_ref, dsh_ref, dsc_ref, dng_ref, acc_ref):
        i, k = pl.program_id(0), pl.program_id(1)
        @pl.when(k == 0)
        def _():
            acc_ref[...] = jnp.zeros_like(acc_ref)

        acc_ref[...] += _dot(dp_ref[...], w_ref[...], NT)

        @pl.when(k == n_k - 1)
        def _():
            dh = acc_ref[...]
            xv = x_ref[...]
            r = lax.rsqrt(jnp.mean(xv * xv, axis=-1, keepdims=True) + EPS)
            xn = xv * r
            md = mod_ref[0]
            gain = ng_ref[...]
            p_shift = jnp.sum(dh, axis=0, keepdims=True)
            p_scale = jnp.sum(dh * (xn * gain), axis=0, keepdims=True)
            drn = dh * (1.0 + md[:, d:2 * d])
            p_ng = jnp.sum(drn * xn, axis=0, keepdims=True)
            dxn = drn * gain
            dx = r * (dxn - xn * jnp.mean(dxn * xn, axis=-1, keepdims=True))
            dxi_ref[...] = dxo_ref[...] + dx

            @pl.when((i * tm) % seq == 0)
            def _():
                dsh_ref[0] = p_shift
                dsc_ref[0] = p_scale

            @pl.when((i * tm) % seq != 0)
            def _():
                dsh_ref[0] = dsh_ref[0] + p_shift
                dsc_ref[0] = dsc_ref[0] + p_scale

            @pl.when(i == 0)
            def _():
                dng_ref[...] = p_ng

            @pl.when(i != 0)
            def _():
                dng_ref[...] = dng_ref[...] + p_ng

    if sectioned:
        dp_spec = pl.BlockSpec((None, tm, tk), lambda i, k: (k // per, i, k % per))
    else:
        dp_spec = pl.BlockSpec((tm, tk), lambda i, k: (i, k))
    row = pl.BlockSpec((tm, d), lambda i, k: (i, 0))
    per_seq = pl.BlockSpec((1, 1, d), lambda i, k: ((i * tm) // seq, 0, 0))
    return _call(
        body, name=name, grid=(m_rows // tm, n_k),
        out_shape=[jax.ShapeDtypeStruct((m_rows, d), F32), jax.ShapeDtypeStruct((nb, 1, d), F32),
                   jax.ShapeDtypeStruct((nb, 1, d), F32), jax.ShapeDtypeStruct((1, d), F32)],
        in_specs=[dp_spec,
                  pl.BlockSpec((None, d, tk), lambda i, k: (k // per, 0, k % per)),
                  row, row,
                  pl.BlockSpec((1, 1, 3 * d), lambda i, k: ((i * tm) // seq, 0, 0)),
                  pl.BlockSpec((1, d), lambda i, k: (0, 0))],
        out_specs=[row, per_seq, per_seq, pl.BlockSpec((1, d), lambda i, k: (0, 0))],
        scratch_shapes=[pltpu.VMEM((tm, d), F32)],
        compiler_params=_params(),
    )(dproj, wg, x, dxo, mod, ng)


def _sgu_stats(proj_ref, vg_ref, di, gd, dgel_ref=None):
    s1 = jnp.zeros((SG_BLOCK, 1), F32)
    for g in range(SG_GROUPS):
        v_pre = proj_ref[:, di + g * gd:di + (g + 1) * gd]
        if dgel_ref is None:
            vg = _gelu(v_pre)
        else:
            vg, dgel_ref[:, g * gd:(g + 1) * gd] = _gelu_and_grad(v_pre)
        vg_ref[:, g * gd:(g + 1) * gd] = vg
        s1 = s1 + jnp.sum(vg, axis=1, keepdims=True)
    mu = s1 / di
    s2 = jnp.zeros((SG_BLOCK, 1), F32)
    for g in range(SG_GROUPS):
        dv = vg_ref[:, g * gd:(g + 1) * gd] - mu
        s2 = s2 + jnp.sum(dv * dv, axis=1, keepdims=True)
    return mu, lax.rsqrt(s2 / di + EPS)


def sgu_fwd(proj, ln_gain, ln_bias, ws, bs, name):
    m_rows, n3 = proj.shape
    di = n3 // 3
    gd = di // SG_GROUPS

    def body(proj_ref, lg_ref, lb_ref, ws_ref, bs_ref, y_ref, wsm_ref, vg_ref):
        @pl.when(pl.program_id(0) == 0)
        def _():
            mask = _chunk_mask()
            for g in range(SG_GROUPS):
                wsm_ref[g] = jnp.where(mask, ws_ref[g], 0.0).astype(BF16)

        mu, rstd = _sgu_stats(proj_ref, vg_ref, di, gd)
        for g in range(SG_GROUPS):
            cs = slice(g * gd, (g + 1) * gd)
            vln = (vg_ref[:, cs] - mu) * rstd * lg_ref[:, cs] + lb_ref[:, cs]
            s = _dot(wsm_ref[g], vln.astype(BF16), NN) + bs_ref[g]
            u = _gelu(proj_ref[:, cs])
            gp = proj_ref[:, 2 * di + g * gd:2 * di + (g + 1) * gd]
            y_ref[:, cs] = (u * s * (gp * _sigmoid(gp))).astype(BF16)

    full = lambda shape: pl.BlockSpec(shape, lambda i: (0,) * len(shape))
    return _call(
        body, name=name, grid=(m_rows // SG_BLOCK,),
        out_shape=jax.ShapeDtypeStruct((m_rows, di), BF16),
        in_specs=[pl.BlockSpec((SG_BLOCK, n3), lambda i: (i, 0)),
                  full((1, di)), full((1, di)),
                  full((SG_GROUPS, SG_BLOCK, SG_BLOCK)), full((SG_GROUPS, SG_BLOCK, 1))],
        out_specs=pl.BlockSpec((SG_BLOCK, di), lambda i: (i, 0)),
        scratch_shapes=[pltpu.VMEM((SG_GROUPS, SG_BLOCK, SG_BLOCK), BF16), pltpu.VMEM((SG_BLOCK, di), F32)],
        compiler_params=_params(),
    )(proj, ln_gain, ln_bias, ws, bs)


def sgu_bwd(proj, dy, ln_gain, ln_bias, ws, bs, name):
    m_rows, n3 = proj.shape
    di = n3 // 3
    gd = di // SG_GROUPS
    n_i = m_rows // SG_BLOCK

    def body(proj_ref, dy_ref, lg_ref, lb_ref, ws_ref, bs_ref,
             dp_ref, dws_ref, dbs_ref, dlg_ref, dlb_ref, wsm_ref, vg_ref, dvh_ref, dgel_ref):
        i = pl.program_id(0)

        def before():
            @pl.when(i == 0)
            def _():
                mask = _chunk_mask()
                for g in range(SG_GROUPS):
                    wsm_ref[g] = jnp.where(mask, ws_ref[g], 0.0).astype(BF16)
                dws_ref[...] = jnp.zeros_like(dws_ref)
                dbs_ref[...] = jnp.zeros_like(dbs_ref)
                dlg_ref[...] = jnp.zeros_like(dlg_ref)
                dlb_ref[...] = jnp.zeros_like(dlb_ref)

        def after():
            @pl.when(i == n_i - 1)
            def _():
                mask = _chunk_mask()
                for g in range(SG_GROUPS):
                    dws_ref[g] = jnp.where(mask, dws_ref[g], 0.0)

        before()
        mu, rstd = _sgu_stats(proj_ref, vg_ref, di, gd, dgel_ref)
        m1 = jnp.zeros((SG_BLOCK, 1), F32)
        m2 = jnp.zeros((SG_BLOCK, 1), F32)
        for g in range(SG_GROUPS):
            cs = slice(g * gd, (g + 1) * gd)
            gs = slice(2 * di + g * gd, 2 * di + (g + 1) * gd)
            gain = lg_ref[:, cs]
            vhat = (vg_ref[:, cs] - mu) * rstd
            vln_b = (vhat * gain + lb_ref[:, cs]).astype(BF16)
            s = _dot(wsm_ref[g], vln_b, NN) + bs_ref[g]
            u, du = _gelu_and_grad(proj_ref[:, cs])
            sg, dsg = _silu_and_grad(proj_ref[:, gs])
            dyv = dy_ref[:, cs].astype(F32)
            dp_ref[:, cs] = (dyv * s * sg * du).astype(BF16)
            dp_ref[:, gs] = (dyv * u * s * dsg).astype(BF16)
            ds = dyv * u * sg
            ds_b = ds.astype(BF16)
            dws_ref[g] = dws_ref[g] + _dot(ds_b, vln_b, NT)
            dbs_ref[g] = dbs_ref[g] + jnp.sum(ds, axis=1, keepdims=True)
            dvln = _dot(wsm_ref[g], ds_b, TN)
            dlg_ref[:, cs] = dlg_ref[:, cs] + jnp.sum(dvln * vhat, axis=0, keepdims=True)
            dlb_ref[:, cs] = dlb_ref[:, cs] + jnp.sum(dvln, axis=0, keepdims=True)
            dvh = dvln * gain
            dvh_ref[:, cs] = dvh
            m1 = m1 + jnp.sum(dvh, axis=1, keepdims=True)
            m2 = m2 + jnp.sum(dvh * vhat, axis=1, keepdims=True)
        m1 = m1 / di
        m2 = m2 / di
        for g in range(SG_GROUPS):
            cs = slice(g * gd, (g + 1) * gd)
            vs = slice(di + g * gd, di + (g + 1) * gd)
            vhat = (vg_ref[:, cs] - mu) * rstd
            dvg = rstd * (dvh_ref[:, cs] - m1 - vhat * m2)
            dp_ref[:, vs] = (dvg * dgel_ref[:, cs]).astype(BF16)

        after()

    full = lambda shape: pl.BlockSpec(shape, lambda i: (0,) * len(shape))
    return _call(
        body, name=name, grid=(n_i,),
        out_shape=[jax.ShapeDtypeStruct((m_rows, n3), BF16),
                   jax.ShapeDtypeStruct((SG_GROUPS, SG_BLOCK, SG_BLOCK), F32),
                   jax.ShapeDtypeStruct((SG_GROUPS, SG_BLOCK, 1), F32),
                   jax.ShapeDtypeStruct((1, di), F32), jax.ShapeDtypeStruct((1, di), F32)],
        in_specs=[pl.BlockSpec((SG_BLOCK, n3), lambda i: (i, 0)),
                  pl.BlockSpec((SG_BLOCK, di), lambda i: (i, 0)),
                  full((1, di)), full((1, di)),
                  full((SG_GROUPS, SG_BLOCK, SG_BLOCK)), full((SG_GROUPS, SG_BLOCK, 1))],
        out_specs=[pl.BlockSpec((SG_BLOCK, n3), lambda i: (i, 0)),
                   full((SG_GROUPS, SG_BLOCK, SG_BLOCK)), full((SG_GROUPS, SG_BLOCK, 1)),
                   full((1, di)), full((1, di))],
        scratch_shapes=[pltpu.VMEM((SG_GROUPS, SG_BLOCK, SG_BLOCK), BF16),
                        pltpu.VMEM((SG_BLOCK, di), F32), pltpu.VMEM((SG_BLOCK, di), F32),
                        pltpu.VMEM((SG_BLOCK, di), F32)],
        compiler_params=_params(),
    )(proj, dy, ln_gain, ln_bias, ws, bs)


def _lower_bound(lbraw):
    mx = jnp.maximum(lbraw[0:1, :], lbraw[1:2, :])
    e0 = jnp.exp(lbraw[0:1, :] - mx)
    e1 = jnp.exp(lbraw[1:2, :] - mx)
    p0 = e0 / (e0 + e1)
    p1 = e1 / (e0 + e1)
    return (p0 + p1) - p0, p0, p1


def _tri(lower):
    r = lax.broadcasted_iota(jnp.int32, (CHUNK, CHUNK), 0)
    c = lax.broadcasted_iota(jnp.int32, (CHUNK, CHUNK), 1)
    return ((r >= c) if lower else (c >= r)).astype(BF16)


def _running_sum(tri, x):
    x1 = x.astype(BF16)
    r1 = x - x1.astype(F32)
    x2 = r1.astype(BF16)
    x3 = (r1 - x2.astype(F32)).astype(BF16)
    return _dot(tri, x1, NN) + _dot(tri, x2, NN) + _dot(tri, x3, NN)


def _row(a, idx):
    r = lax.broadcasted_iota(jnp.int32, a.shape, 0)
    return jnp.sum(jnp.where(r == idx, a, 0.0), axis=0, keepdims=True)


def _hgrn_gates(qp, fp, lb, tri):
    sgm = _sigmoid_small(fp)
    f = lb + (1.0 - lb) * sgm
    k = 1.0 - f
    a = _running_sum(tri, jnp.log(f))
    a_mid = _row(a, CHUNK // 2 - 1)
    a_last = _row(a, CHUNK - 1)
    q, dq = _silu_and_grad(qp)
    e1, e2, e3, e4 = jnp.exp(a - a_mid), jnp.exp(a_mid - a), jnp.exp(a), jnp.exp(a_last - a)
    return dict(sgm=sgm, f=f, k=k, q=q, dq=dq, e1=e1, e2=e2, e3=e3, e4=e4, dec=jnp.exp(a_last),
                q_in=q * e1, k_in=k * e2, q_out=q * e3, k_out=k * e4)


def _causal():
    r = lax.broadcasted_iota(jnp.int32, (CHUNK, CHUNK), 0)
    c = lax.broadcasted_iota(jnp.int32, (CHUNK, CHUNK), 1)
    return r >= c


def hgrn_fwd(proj4, lbraw, gn, seq, name):
    _, m_rows, di = proj4.shape
    nb, nh, nc = m_rows // seq, di // HEAD_DIM, seq // CHUNK
    rows = min(HG_ROWS, seq)
    wide = HG_WIDE * HEAD_DIM
    ns, cpb = seq // rows, rows // CHUNK

    def body(p_ref, lb_ref, gn_ref, y_ref, sts_ref, st_ref):
        @pl.when(pl.program_id(2) == 0)
        def _():
            st_ref[...] = jnp.zeros_like(st_ref)

        tri = _tri(True)
        causal = _causal()
        gain = gn_ref[...]
        lbs = [_lower_bound(lb_ref[:, j * HEAD_DIM:(j + 1) * HEAD_DIM])[0] for j in range(HG_WIDE)]

        units = [(n, j) for n in range(cpb) for j in range(HG_WIDE)]
        rs = lambda n: slice(n * CHUNK, (n + 1) * CHUNK)
        cs = lambda j: slice(j * HEAD_DIM, (j + 1) * HEAD_DIM)
        gates, v_b, sc_b, kv, o_in, o_x = {}, {}, {}, {}, {}, {}
        for n, j in units:
            gates[n, j] = _hgrn_gates(p_ref[0, rs(n), cs(j)], p_ref[1, rs(n), cs(j)], lbs[j], tri)
            v_b[n, j] = p_ref[2, rs(n), cs(j)].astype(BF16)
        for u in units:
            t = gates[u]
            sc_b[u] = jnp.where(causal, _dot(t["q_in"].astype(BF16), t["k_in"].astype(BF16), NT), 0.0).astype(BF16)
            kv[u] = _dot(v_b[u], t["k_out"].astype(BF16), TN)
        for u in units:
            o_in[u] = _dot(sc_b[u], v_b[u], NN)
        for j in range(HG_WIDE):
            st = st_ref[j]
            for n in range(cpb):
                sts_ref[n, :, cs(j)] = st
                o_x[n, j] = _dot(gates[n, j]["q_out"].astype(BF16), st.astype(BF16), NT)
                st = st * gates[n, j]["dec"] + kv[n, j]
            st_ref[j] = st
        for n, j in units:
            o = o_in[n, j] + o_x[n, j]
            r = lax.rsqrt(jnp.mean(o * o, axis=-1, keepdims=True) + EPS)
            gp = p_ref[3, rs(n), cs(j)]
            y_ref[rs(n), cs(j)] = ((o * r * gain) * (gp * _sigmoid(gp))).astype(BF16)

    return _call(
        body, name=name, grid=(nh // HG_WIDE, nb, ns),
        out_shape=[jax.ShapeDtypeStruct((m_rows, di), BF16),
                   jax.ShapeDtypeStruct((nb * nc, HEAD_DIM, di), F32)],
        in_specs=[pl.BlockSpec((4, rows, wide), lambda hg, b, s: (0, b * ns + s, hg)),
                  pl.BlockSpec((2, wide), lambda hg, b, s: (0, hg)),
                  pl.BlockSpec((1, HEAD_DIM), lambda hg, b, s: (0, 0))],
        out_specs=[pl.BlockSpec((rows, wide), lambda hg, b, s: (b * ns + s, hg)),
                   pl.BlockSpec((cpb, HEAD_DIM, wide), lambda hg, b, s: (b * ns + s, 0, hg))],
        scratch_shapes=[pltpu.VMEM((HG_WIDE, HEAD_DIM, HEAD_DIM), F32)],
        compiler_params=_params(),
    )(proj4, lbraw, gn)


def hgrn_bwd(proj4, dy, sts, lbraw, gn, seq, name):
    _, m_rows, di = proj4.shape
    nb, nh, nc = m_rows // seq, di // HEAD_DIM, seq // CHUNK
    rows = min(HG_ROWS, seq)
    wide = HG_WIDE * HEAD_DIM
    ns, cpb = seq // rows, rows // CHUNK
    n_hg = nh // HG_WIDE

    def body(p_ref, dy_ref, sts_ref, lb_ref, gn_ref, dp_ref, dlb_ref, dgn_ref, dst_ref, lbacc_ref, gnacc_ref):
        hg, b, s = pl.program_id(0), pl.program_id(1), pl.program_id(2)
        tri, triu = _tri(True), _tri(False)
        causal = _causal()
        gain = gn_ref[...]
        first = (b == 0) & (s == 0)
        cs = lambda j: slice(j * HEAD_DIM, (j + 1) * HEAD_DIM)

        def before():
            @pl.when((hg == 0) & first)
            def _():
                gnacc_ref[...] = jnp.zeros_like(gnacc_ref)

            @pl.when(first)
            def _():
                lbacc_ref[...] = jnp.zeros_like(lbacc_ref)

            @pl.when(s == 0)
            def _():
                dst_ref[...] = jnp.zeros_like(dst_ref)

        def after():
            @pl.when((b == nb - 1) & (s == ns - 1))
            def _():
                for j in range(HG_WIDE):
                    _, p0, p1 = _lower_bound(lb_ref[:, cs(j)])
                    acc = lbacc_ref[:, cs(j)]
                    dlb_ref[0:1, cs(j)] = -acc * p0 * p1
                    dlb_ref[1:2, cs(j)] = acc * p1 * (1.0 - p1)

            @pl.when((hg == n_hg - 1) & (b == nb - 1) & (s == ns - 1))
            def _():
                tot = gnacc_ref[:, 0:HEAD_DIM]
                for j in range(1, HG_WIDE):
                    tot = tot + gnacc_ref[:, cs(j)]
                dgn_ref[...] = tot

        before()

        units = [(n, j) for n in range(cpb) for j in range(HG_WIDE)]
        rs = lambda n: slice(n * CHUNK, (n + 1) * CHUNK)
        lbs = [_lower_bound(lb_ref[:, cs(j)])[0] for j in range(HG_WIDE)]
        gates, v_b, st_b, sc_b, o, do_b = {}, {}, {}, {}, {}, {}
        dq_out, dsc_b, dv, g_st, dq_in, dk_in, dst_at, dk_out, ddec = {}, {}, {}, {}, {}, {}, {}, {}, {}
        for n, j in units:
            gates[n, j] = _hgrn_gates(p_ref[0, rs(n), cs(j)], p_ref[1, rs(n), cs(j)], lbs[j], tri)
            v_b[n, j] = p_ref[2, rs(n), cs(j)].astype(BF16)
            st_b[n, j] = sts_ref[n, :, cs(j)].astype(BF16)
        for u in units:
            t = gates[u]
            sc_b[u] = jnp.where(causal, _dot(t["q_in"].astype(BF16), t["k_in"].astype(BF16), NT), 0.0).astype(BF16)
        for u in units:
            o[u] = _dot(sc_b[u], v_b[u], NN) + _dot(gates[u]["q_out"].astype(BF16), st_b[u], NT)
        for n, j in units:
            ov = o[n, j]
            r = lax.rsqrt(jnp.mean(ov * ov, axis=-1, keepdims=True) + EPS)
            ohat = ov * r
            sg, dsg = _silu_and_grad(p_ref[3, rs(n), cs(j)])
            dyv = dy_ref[rs(n), cs(j)].astype(F32)
            dp_ref[3, rs(n), cs(j)] = (dyv * (ohat * gain) * dsg).astype(BF16)
            d_on = dyv * sg
            gnacc_ref[:, cs(j)] = gnacc_ref[:, cs(j)] + jnp.sum(d_on * ohat, axis=0, keepdims=True)
            dohat = d_on * gain
            do_b[n, j] = (r * (dohat - ohat * jnp.mean(dohat * ohat, axis=-1, keepdims=True))).astype(BF16)
        for u in units:
            dq_out[u] = _dot(do_b[u], st_b[u], NN)
            dsc_b[u] = jnp.where(causal, _dot(do_b[u], v_b[u], NT), 0.0).astype(BF16)
            dv[u] = _dot(sc_b[u], do_b[u], TN)
            g_st[u] = _dot(do_b[u], gates[u]["q_out"].astype(BF16), TN)
        for u in units:
            dq_in[u] = _dot(dsc_b[u], gates[u]["k_in"].astype(BF16), NN)
            dk_in[u] = _dot(dsc_b[u], gates[u]["q_in"].astype(BF16), TN)
        for j in range(HG_WIDE):
            dst = dst_ref[j]
            for n in reversed(range(cpb)):
                dst_at[n, j] = dst
                dst = dst * gates[n, j]["dec"] + g_st[n, j]
            dst_ref[j] = dst
        for n, j in units:
            dst = dst_at[n, j]
            dst_b = dst.astype(BF16)
            dk_out[n, j] = _dot(v_b[n, j], dst_b, NN)
            dv[n, j] = dv[n, j] + _dot(gates[n, j]["k_out"].astype(BF16), dst_b, NT)
            ddec[n, j] = jnp.sum(dst * sts_ref[n, :, cs(j)], axis=0, keepdims=True)
        for n, j in units:
            t = gates[n, j]
            dp_ref[2, rs(n), cs(j)] = dv[n, j].astype(BF16)
            dq = dq_in[n, j] * t["e1"] + dq_out[n, j] * t["e3"]
            dk = dk_in[n, j] * t["e2"] + dk_out[n, j] * t["e4"]
            w_in = dq_in[n, j] * t["q_in"] - dk_in[n, j] * t["k_in"]
            w_out = dk_out[n, j] * t["k_out"]
            da = w_in + dq_out[n, j] * t["q_out"] - w_out
            da_mid = -jnp.sum(w_in, axis=0, keepdims=True)
            da_last = jnp.sum(w_out, axis=0, keepdims=True) + ddec[n, j] * t["dec"]
            rid = lax.broadcasted_iota(jnp.int32, da.shape, 0)
            da = da + jnp.where(rid == CHUNK // 2 - 1, da_mid, 0.0) + jnp.where(rid == CHUNK - 1, da_last, 0.0)
            dlf = _running_sum(triu, da)
            df = dlf / t["f"] - dk
            sgm = t["sgm"]
            dp_ref[1, rs(n), cs(j)] = (df * (1.0 - lbs[j]) * sgm * (1.0 - sgm)).astype(BF16)
            lbacc_ref[:, cs(j)] = lbacc_ref[:, cs(j)] + jnp.sum(df * (1.0 - sgm), axis=0, keepdims=True)
            dp_ref[0, rs(n), cs(j)] = (dq * t["dq"]).astype(BF16)

        after()

    blk = lambda hg, b, s: b * ns + (ns - 1 - s)
    return _call(
        body, name=name, grid=(n_hg, nb, ns),
        out_shape=[jax.ShapeDtypeStruct((4, m_rows, di), BF16), jax.ShapeDtypeStruct((2, di), F32),
                   jax.ShapeDtypeStruct((1, HEAD_DIM), F32)],
        in_specs=[pl.BlockSpec((4, rows, wide), lambda hg, b, s: (0, blk(hg, b, s), hg)),
                  pl.BlockSpec((rows, wide), lambda hg, b, s: (blk(hg, b, s), hg)),
                  pl.BlockSpec((cpb, HEAD_DIM, wide), lambda hg, b, s: (blk(hg, b, s), 0, hg)),
                  pl.BlockSpec((2, wide), lambda hg, b, s: (0, hg)),
                  pl.BlockSpec((1, HEAD_DIM), lambda hg, b, s: (0, 0))],
        out_specs=[pl.BlockSpec((4, rows, wide), lambda hg, b, s: (0, blk(hg, b, s), hg)),
                   pl.BlockSpec((2, wide), lambda hg, b, s: (0, hg)),
                   pl.BlockSpec((1, HEAD_DIM), lambda hg, b, s: (0, 0))],
        scratch_shapes=[pltpu.VMEM((HG_WIDE, HEAD_DIM, HEAD_DIM), F32), pltpu.VMEM((1, wide), F32),
                        pltpu.VMEM((1, wide), F32)],
        compiler_params=_params(),
    )(proj4, dy, sts, lbraw, gn)


def outproj_loss(y, w, x, mod, fg, target, seq, name):
    m_rows, di = y.shape
    d = w.shape[1]
    tm = min(512, seq)

    def body(y_ref, w_ref, x_ref, mod_ref, fg_ref, t_ref, out_ref, loss_ref, dx_ref, dfg_ref):
        i = pl.program_id(0)
        acc = _dot(y_ref[...], w_ref[...], NN)
        out_ref[...] = acc.astype(BF16)
        xv = x_ref[...] + mod_ref[0][:, 2 * d:] * acc
        gain = fg_ref[...]
        r = lax.rsqrt(jnp.mean(xv * xv, axis=-1, keepdims=True) + EPS)
        xn = xv * r
        e = xn * gain - t_ref[...]
        part = 0.5 * jnp.sum(jnp.mean(e * e, axis=-1, keepdims=True), axis=0, keepdims=True)
        dyv = e / d
        p_fg = jnp.sum(dyv * xn, axis=0, keepdims=True)
        dxn = dyv * gain
        dx_ref[...] = r * (dxn - xn * jnp.mean(dxn * xn, axis=-1, keepdims=True))

        @pl.when(i == 0)
        def _():
            loss_ref[...] = part
            dfg_ref[...] = p_fg

        @pl.when(i != 0)
        def _():
            loss_ref[...] = loss_ref[...] + part
            dfg_ref[...] = dfg_ref[...] + p_fg

    row = pl.BlockSpec((tm, d), lambda i: (i, 0))
    return _call(
        body, name=name, grid=(m_rows // tm,),
        out_shape=[jax.ShapeDtypeStruct((m_rows, d), BF16), jax.ShapeDtypeStruct((1, 1), F32),
                   jax.ShapeDtypeStruct((m_rows, d), F32), jax.ShapeDtypeStruct((1, d), F32)],
        in_specs=[pl.BlockSpec((tm, di), lambda i: (i, 0)),
                  pl.BlockSpec((di, d), lambda i: (0, 0)),
                  row,
                  pl.BlockSpec((1, 1, 3 * d), lambda i: ((i * tm) // seq, 0, 0)),
                  pl.BlockSpec((1, d), lambda i: (0, 0)), row],
        out_specs=[row, pl.BlockSpec((1, 1), lambda i: (0, 0)), row, pl.BlockSpec((1, d), lambda i: (0, 0))],
        compiler_params=_params(),
    )(y, w, x, mod, fg, target)


def _pack(parts):
    flat = jnp.concatenate([p.reshape(-1) for p in parts])
    pad = (-flat.shape[0]) % (8 * LANES)
    return jnp.pad(flat, (0, pad)).reshape(-1, LANES)


def kernel(x, c, norm_gain, w_ada, b_ada, a_w_in, a_ln_gain, a_ln_bias, a_w_s, a_b_s, a_w_out, b_w_in, b_lower_bounds, b_gn_gain, b_w_out, final_gain, loss_target, m_norm_gain, m_w_ada, m_b_ada, m_a_w_in, m_a_ln_gain, m_a_ln_bias, m_a_w_s, m_a_b_s, m_a_w_out, m_b_w_in, m_b_lower_bounds, m_b_gn_gain, m_b_w_out, m_final_gain, v_norm_gain, v_w_ada, v_b_ada, v_a_w_in, v_a_ln_gain, v_a_ln_bias, v_a_w_s, v_a_b_s, v_a_w_out, v_b_w_in, v_b_lower_bounds, v_b_gn_gain, v_b_w_out, v_final_gain):
    nb, seq, d = x.shape
    m_rows = nb * seq
    n_l = w_ada.shape[0]
    ada_cols = w_ada.shape[2]
    px, py, pc = _place()
    chip = 2 * px + py
    dev = 2 * chip + pc

    c_all = allgather_small(c.reshape(-1, LANES), "gather_c").reshape(N_DEV * nb, d)
    s_a = halves_start([cast_into_slot(a_w_in[0], chip, c_all, "cast_a_in"),
                        cast_into_slot(a_w_out[0], chip, c_all, "cast_a_out")], "gather_a_start")
    land_b_in = cast_into_slot(b_w_in[0], chip, s_a[3], "cast_b_in")
    land_b_out = cast_into_slot(b_w_out[0], chip, land_b_in, "cast_b_out")
    b_cols = lax.dynamic_slice_in_dim(b_ada, chip * ada_cols, ada_cols, axis=1).reshape(n_l, 1, ada_cols)
    mod_cols = ada_fwd(c_all, w_ada, b_cols, land_b_out, "ada_fwd")
    mod_g = allgather_small(mod_cols.reshape(-1, LANES), "gather_mod")
    mod_g = mod_g.reshape(N_CHIPS, 2, n_l, N_DEV * nb, ada_cols)[:, 0]
    mod_all = jnp.transpose(mod_g, (1, 2, 0, 3)).reshape(n_l, N_DEV * nb, 3 * d)
    mod_mine = lax.dynamic_slice_in_dim(mod_all, dev * nb, nb, axis=1)
    mod0 = mod_mine[0].reshape(nb, 1, 3 * d)
    mod1 = mod_mine[1].reshape(nb, 1, 3 * d)

    landed_a = halves_wait(s_a[0], s_a[1], s_a[2], mod_mine, "gather_a_wait")
    s_bi = gather_start(land_b_in, landed_a[0], "gather_b_in_start")
    s_bo = gather_start(land_b_out, s_bi[3], "gather_b_out_start")
    wa_in, wa_out = pass_halves(list(landed_a), "gather_a_pass")
    di = a_w_out.shape[1] * N_CHIPS
    wa_out = wa_out.reshape(di, d)

    x0 = x.reshape(m_rows, d)
    tgt = loss_target.reshape(m_rows, d)
    ng0 = norm_gain[0:1] + (s_bi[3][0, 0] + s_bo[3][0, 0])
    ng1 = norm_gain[1:2]
    bs_col = a_b_s[0].reshape(SG_GROUPS, SG_BLOCK, 1)
    proj_a, h_a = inproj_fwd(x0, mod0, ng0, wa_in, seq, False, "a_inproj")
    y_a = sgu_fwd(proj_a, a_ln_gain, a_ln_bias, a_w_s[0], bs_col, "a_sgu")
    x1, out_a = outproj_fwd(y_a, wa_out, x0, mod0, seq, "a_outproj")
    wb_in = gather_wait(*s_bi[:3], out_a, "gather_b_in_wait")
    proj_b, h_b = inproj_fwd(x1, mod1, ng1, wb_in, seq, True, "b_inproj")
    y_b, sts_b = hgrn_fwd(proj_b, b_lower_bounds, b_gn_gain, seq, "b_hgrn")
    wb_out = gather_wait(*s_bo[:3], y_b, "gather_b_out_wait").reshape(di, d)
    out_b, loss_part, dx2, dfg = outproj_loss(
        y_b, wb_out, x1, mod1, final_gain.reshape(1, d), tgt, seq, "b_outproj_loss")

    shard_rows = di // N_CHIPS
    dy_b, dout_b, dgate1 = outproj_bwd(dx2, out_b, mod1, wb_out, seq, "b_outproj_bwd")
    gwb_out = grad_w_out(y_b, dout_b, "b_grad_w_out").reshape(N_CHIPS, shard_rows, d)
    e_bo = exchange_start(gwb_out, "exchange_b_out_start")
    dproj_b, dlb, dgn = hgrn_bwd(
        proj_b, dy_b, sts_b, b_lower_bounds, b_gn_gain + e_bo[4][0, 0], seq, "b_hgrn_bwd")
    e_bi = exchange_start(grad_w_in(h_b, dproj_b, N_CHIPS, True, "b_grad_w_in"), "exchange_b_in_start")
    dx1, dshift1, dscale1, dng1 = inproj_bwd(
        dproj_b, wb_in, x1, dx2, mod1, ng1 + e_bi[4][0, 0], seq, True, "b_inproj_bwd")

    dy_a, dout_a, dgate0 = outproj_bwd(dx1, out_a, mod0, wa_out, seq, "a_outproj_bwd")
    gwa_out = grad_w_out(y_a, dout_a, "a_grad_w_out").reshape(N_CHIPS, shard_rows, d)
    e_ao = exchange_start(gwa_out, "exchange_a_out_start")
    dproj_a, dws, dbs, dlg, dlbias = sgu_bwd(
        proj_a, dy_a, a_ln_gain + e_ao[4][0, 0], a_ln_bias, a_w_s[0], bs_col, "a_sgu_bwd")
    e_ai = exchange_start(grad_w_in(h_a, dproj_a, N_CHIPS, False, "a_grad_w_in"), "exchange_a_in_start")
    dx0, dshift0, dscale0, dng0 = inproj_bwd(
        dproj_a, wa_in, x0, dx1, mod0, norm_gain[0:1] + e_ai[4][0, 0], seq, False, "a_inproj_bwd")
    grad_x = dx0.reshape(nb, seq, d)

    dmod = jnp.concatenate([dshift0, dscale0, dgate0, dshift1, dscale1, dgate1], axis=2)
    n_dmod = dmod.size
    small_g = [jnp.concatenate([dng0, dng1], axis=0), dlg, dlbias, dws, dbs, dlb, dfg, dgn]
    packed_g = _pack([dmod] + small_g + [loss_part])
    rows = packed_g.shape[0]
    s_small = gather_all_start(
        lax.dynamic_update_slice(jnp.zeros((N_DEV, rows, LANES), F32), packed_g[None], (dev, 0, 0)),
        "gather_small_start")

    def finish(group, after):
        mine = []
        for ex, _, _, _, nm in group:
            parts_thru, land = exchange_wait(ex[0], ex[1], ex[2], ex[3], after, "exchange_" + nm + "_wait")
            mine.append(sum_parts(parts_thru, land, chip, "sum_" + nm))
            after = mine[-1]
        theirs = swap_sibling(mine, "swap_" + group[0][4])
        return [[r.reshape(w.shape) for r in adamw_pair(pa, pb, w[0], m[0], v[0], "adamw_" + nm)]
                for pa, pb, (_, w, m, v, nm) in zip(mine, theirs, group)]

    (gb_out, db_out, mb_out, vb_out), (gb_in, db_in, mb_in, vb_in), (ga_out, da_out, ma_out, va_out) = finish(
        [(e_bo, b_w_out, m_b_w_out, v_b_w_out, "b_out"), (e_bi, b_w_in, m_b_w_in, v_b_w_in, "b_in"),
         (e_ao, a_w_out, m_a_w_out, v_a_w_out, "a_out")], s_small[3])
    ((ga_in, da_in, ma_in, va_in),) = finish([(e_ai, a_w_in, m_a_w_in, v_a_w_in, "a_in")], ga_out)

    small_w = [norm_gain, a_ln_gain, a_ln_bias, a_w_s, a_b_s, b_lower_bounds, final_gain, b_gn_gain]
    small_m = [m_norm_gain, m_a_ln_gain, m_a_ln_bias, m_a_w_s, m_a_b_s, m_b_lower_bounds, m_final_gain, m_b_gn_gain]
    small_v = [v_norm_gain, v_a_ln_gain, v_a_ln_bias, v_a_w_s, v_a_b_s, v_b_lower_bounds, v_final_gain, v_b_gn_gain]
    rows_of = lambda a: a.reshape(-1, a.shape[-1])
    gathered = gather_all_wait(s_small[0], s_small[1], s_small[2], ga_in, "gather_small_wait")
    tail, small_res = small_update(
        gathered, n_dmod // LANES, [rows_of(a) for a in small_w], [rows_of(a) for a in small_m],
        [rows_of(a) for a in small_v], "small_update")
    loss = tail[0, 0]
    sg, sd, sm, sv = [[small_res[p][kind].reshape(w.shape) for p, w in enumerate(small_w)] for kind in range(4)]

    dmod_all = gathered[:, :n_dmod // LANES].reshape(N_DEV * nb, n_l, 3 * d)
    dmod_cols = lax.dynamic_slice_in_dim(dmod_all, chip * ada_cols, ada_cols, axis=2)
    dmod_cols = jnp.transpose(dmod_cols, (1, 0, 2))
    g_wada, d_wada, m_wada, v_wada = ada_bwd(c_all, dmod_cols, w_ada, m_w_ada, v_w_ada, "ada_bwd")
    flat = lambda a: a.reshape(1, -1)
    g_bada, d_bada, m_bada, v_bada = [
        r.reshape(b_ada.shape) for r in
        bias_update(dmod_all.reshape(N_DEV * nb, n_l * 3 * d), flat(b_ada), flat(m_b_ada), flat(v_b_ada), "bias_update")]

    def order(ng, wada, bada, ain, sm_rest, aout, bin_, bout):
        lg, lbi, ws_, bs_, lbd, fg_, gn_ = sm_rest
        return [ng, wada, bada, ain, lg, lbi, ws_, bs_, aout, bin_, lbd, gn_, bout, fg_]

    grads = order(sg[0], g_wada, g_bada, ga_in, sg[1:8], ga_out, gb_in, gb_out)
    deltas = order(sd[0], d_wada, d_bada, da_in, sd[1:8], da_out, db_in, db_out)
    new_m = order(sm[0], m_wada, m_bada, ma_in, sm[1:8], ma_out, mb_in, mb_out)
    new_v = order(sv[0], v_wada, v_bada, va_in, sv[1:8], va_out, vb_in, vb_out)
    return (loss, grad_x, *grads, *deltas, *new_m, *new_v)
```

```python
import jax
import jax.numpy as jnp
from jax import lax
from jax.experimental import pallas as pl
from jax.experimental.pallas import tpu as pltpu

F32 = jnp.float32
BF16 = jnp.bfloat16
EPS = 1e-6
CHUNK = 64
SG_BLOCK = 128
SG_GROUPS = 8
HEAD_DIM = 128
HG_WIDE = 8
HG_ROWS = 256
N_CHIPS = 4
N_DEV = 8
LANES = 128
ADAM_LR = 0.001
ADAM_B1 = 0.9
ADAM_B2 = 0.999
ADAM_EPS = 1e-08
ADAM_WD = 0.01
ADAM_STEP = 10
GELU_C0 = 0.7978845608028654
GELU_C1 = 0.044715
MESH = pl.DeviceIdType.MESH
VMEM_LIMIT = 56 * 1024 * 1024


ROW_TILE = 1024


def _col_tile(n):
    return next(t for t in (1024, 768, 512, 256) if n % t == 0)


def _call(body, **kw):
    return pl.pallas_call(body, **kw)


def _params(**kw):
    return pltpu.CompilerParams(vmem_limit_bytes=VMEM_LIMIT, **kw)


def _sigmoid(x):
    return 0.5 * jnp.tanh(0.5 * x) + 0.5


def _sigmoid_small(x):
    return 1.0 / (1.0 + jnp.exp(-x))


def _silu_and_grad(x):
    s = _sigmoid(x)
    return x * s, s * (1.0 + x * (1.0 - s))


def _gelu(x):
    return 0.5 * x * (1.0 + jnp.tanh(GELU_C0 * (x + GELU_C1 * x * x * x)))


def _gelu_and_grad(x):
    t = jnp.tanh(GELU_C0 * (x + GELU_C1 * x * x * x))
    g = 0.5 * x * (1.0 + t)
    dg = 0.5 * (1.0 + t) + 0.5 * x * (1.0 - t * t) * (GELU_C0 * (1.0 + 3.0 * GELU_C1 * x * x))
    return g, dg


def _dot(a, b, dims, precision=None):
    return lax.dot_general(a, b, (dims, ((), ())), precision=precision, preferred_element_type=F32)


NN = ((1,), (0,))
NT = ((1,), (1,))
TN = ((0,), (0,))


def _adamw(w, g, m, v):
    m = ADAM_B1 * m + (1.0 - ADAM_B1) * g
    v = ADAM_B2 * v + (1.0 - ADAM_B2) * (g * g)
    m_hat = m / (1.0 - ADAM_B1 ** ADAM_STEP)
    v_hat = v / (1.0 - ADAM_B2 ** ADAM_STEP)
    delta = -ADAM_LR * (m_hat / (jnp.sqrt(v_hat) + ADAM_EPS) + ADAM_WD * w)
    return delta, m, v


def _chunk_mask():
    r = lax.broadcasted_iota(jnp.int32, (SG_BLOCK, SG_BLOCK), 0)
    c = lax.broadcasted_iota(jnp.int32, (SG_BLOCK, SG_BLOCK), 1)
    return (c // CHUNK) <= (r // CHUNK)


def _place():
    return lax.axis_index("x"), lax.axis_index("y"), lax.axis_index("c")


def _other_chips(x, y):
    return [(1 - x, y), (x, 1 - y), (1 - x, 1 - y)]


def allgather_small(v, name):
    m_per, n = v.shape

    def body(x_ref, out_ref, send_sems, recv_sems, local_sem):
        x, y, c = _place()
        me, sibling = (x, y, c), (x, y, 1 - c)
        chips = _other_chips(x, y)

        def rows(px, py, pc):
            return out_ref.at[pl.ds((4 * px + 2 * py + pc) * m_per, m_per), :]

        def copy(k, block, to, src=None):
            return pltpu.make_async_remote_copy(
                src_ref=rows(*block) if src is None else src, dst_ref=rows(*block),
                send_sem=send_sems.at[k], recv_sem=recv_sems.at[k], device_id=to, device_id_type=MESH)

        mine = pltpu.make_async_copy(x_ref, rows(*me), local_sem)
        mine.start()
        first = [copy(0, me, sibling, src=x_ref)]
        first += [copy(1 + j, me, (*chip, c), src=x_ref) for j, chip in enumerate(chips)]
        for cp in first:
            cp.start()
        passed = [copy(4 + j, (*chip, c), sibling) for j, chip in enumerate(chips)]
        for j, chip in enumerate(chips):
            copy(1 + j, (*chip, c), me).wait_recv()
            passed[j].start()
        copy(0, sibling, me).wait_recv()
        for j, chip in enumerate(chips):
            copy(4 + j, (*chip, 1 - c), me).wait_recv()
        for cp in first + passed:
            cp.wait_send()
        mine.wait()

    return _call(
        body, name=name,
        out_shape=jax.ShapeDtypeStruct((N_DEV * m_per, n), v.dtype),
        in_specs=[pl.BlockSpec(memory_space=pltpu.VMEM)],
        out_specs=pl.BlockSpec(memory_space=pltpu.VMEM),
        scratch_shapes=[pltpu.SemaphoreType.DMA((7,)), pltpu.SemaphoreType.DMA((7,)), pltpu.SemaphoreType.DMA],
    )(v)


def _hbm_spec():
    return pl.BlockSpec(memory_space=pltpu.HBM)


def _sem_spec():
    return pl.BlockSpec(memory_space=pltpu.SEMAPHORE)


def _split_params():
    return pltpu.CompilerParams(has_side_effects=pltpu.SideEffectType.DATAFLOW_SIDE_EFFECTING)


def _hbm(a):
    return pltpu.with_memory_space_constraint(a, pltpu.HBM)


def _half_copy(land_ref, rows, chip_idx, core_half, send_sem, recv_sem, to):
    half = land_ref.at[chip_idx, pl.ds(core_half * (rows // 2), rows // 2), :]
    return pltpu.make_async_remote_copy(
        src_ref=half, dst_ref=half, send_sem=send_sem, recv_sem=recv_sem, device_id=to, device_id_type=MESH)


def halves_start(lands, name):
    n = len(lands)

    def body(*refs):
        land_refs, send_sems, recv_sems, token = refs[:n], refs[n], refs[n + 1], refs[-1]
        x, y, c = _place()
        for w in range(n):
            for j, (px, py) in enumerate(_other_chips(x, y)):
                _half_copy(land_refs[w], lands[w].shape[1], 2 * x + y, c,
                           send_sems.at[3 * w + j], recv_sems.at[3 * w + j], (px, py, c)).start()
        token[...] = jnp.zeros_like(token)

    res = _call(
        body, name=name,
        out_shape=(pltpu.SemaphoreType.DMA((3 * n,)), pltpu.SemaphoreType.DMA((3 * n,)),
                   *[pltpu.HBM(a.shape, a.dtype) for a in lands], jax.ShapeDtypeStruct((8, LANES), F32)),
        in_specs=(_hbm_spec(),) * n,
        out_specs=(_sem_spec(), _sem_spec(), *[_hbm_spec()] * n, pl.BlockSpec(memory_space=pltpu.VMEM)),
        input_output_aliases={w: 2 + w for w in range(n)}, compiler_params=_split_params(),
    )(*[_hbm(a) for a in lands])
    return res[0], res[1], list(res[2:2 + n]), res[2 + n]


def halves_wait(send_sems, recv_sems, lands, after, name):
    n = len(lands)

    def body(*refs):
        land_refs, send_sems, recv_sems = refs[:n], refs[n], refs[n + 1]
        x, y, c = _place()
        for w in range(n):
            for j, (px, py) in enumerate(_other_chips(x, y)):
                cp = _half_copy(land_refs[w], lands[w].shape[1], 2 * px + py, c,
                                send_sems.at[3 * w + j], recv_sems.at[3 * w + j], (px, py, c))
                cp.wait_send()
                cp.wait_recv()

    return _call(
        body, name=name,
        out_shape=tuple(pltpu.HBM(a.shape, a.dtype) for a in lands),
        in_specs=(*[_hbm_spec()] * n, _sem_spec(), _sem_spec(), pl.BlockSpec(memory_space=pl.ANY)),
        out_specs=tuple(_hbm_spec() for _ in lands), input_output_aliases={w: w for w in range(n)},
        compiler_params=_split_params(),
    )(*lands, send_sems, recv_sems, after)


def pass_halves(lands, name):
    n = len(lands)

    def body(*refs):
        land_refs, send_sems, recv_sems = refs[n:2 * n], refs[2 * n], refs[2 * n + 1]
        x, y, c = _place()
        sent = []
        for w in range(n):
            for j, (px, py) in enumerate(_other_chips(x, y)):
                cp = _half_copy(land_refs[w], lands[w].shape[1], 2 * px + py, c,
                                send_sems.at[3 * w + j], recv_sems.at[3 * w + j], (x, y, 1 - c))
                cp.start()
                sent.append(cp)
        for w in range(n):
            for j, (px, py) in enumerate(_other_chips(x, y)):
                _half_copy(land_refs[w], lands[w].shape[1], 2 * px + py, 1 - c,
                           send_sems.at[3 * w + j], recv_sems.at[3 * w + j], (x, y, 1 - c)).wait_recv()
        for cp in sent:
            cp.wait_send()

    return _call(
        body, name=name,
        out_shape=[jax.ShapeDtypeStruct(a.shape, a.dtype) for a in lands],
        in_specs=[_hbm_spec()] * n, out_specs=[_hbm_spec()] * n,
        input_output_aliases={w: w for w in range(n)},
        scratch_shapes=[pltpu.SemaphoreType.DMA((3 * n,)), pltpu.SemaphoreType.DMA((3 * n,))],
    )(*lands)


def gather_start(land, after, name):
    def body(land_ref, after_ref, send_sems, recv_sems, land_thru, token):
        del after_ref, land_thru
        x, y, c = _place()
        for j, (px, py) in enumerate(_other_chips(x, y)):
            pltpu.make_async_remote_copy(
                src_ref=land_ref.at[2 * x + y], dst_ref=land_ref.at[2 * x + y],
                send_sem=send_sems.at[j], recv_sem=recv_sems.at[j], device_id=(px, py, c),
                device_id_type=MESH).start()
        token[...] = jnp.zeros_like(token)

    return _call(
        body, name=name,
        out_shape=(pltpu.SemaphoreType.DMA((3,)), pltpu.SemaphoreType.DMA((3,)),
                   pltpu.HBM(land.shape, land.dtype), jax.ShapeDtypeStruct((8, LANES), F32)),
        in_specs=(_hbm_spec(), pl.BlockSpec(memory_space=pl.ANY)),
        out_specs=(_sem_spec(), _sem_spec(), _hbm_spec(), pl.BlockSpec(memory_space=pltpu.VMEM)),
        input_output_aliases={0: 2}, compiler_params=_split_params(),
    )(_hbm(land), after)


def gather_wait(send_sems, recv_sems, land, after, name):
    def body(land_ref, send_sems, recv_sems, after_ref, land_out):
        del after_ref, land_out
        x, y, c = _place()
        for j, (px, py) in enumerate(_other_chips(x, y)):
            cp = pltpu.make_async_remote_copy(
                src_ref=land_ref.at[2 * x + y], dst_ref=land_ref.at[2 * px + py],
                send_sem=send_sems.at[j], recv_sem=recv_sems.at[j], device_id=(px, py, c), device_id_type=MESH)
            cp.wait_send()
            cp.wait_recv()

    return _call(
        body, name=name,
        out_shape=pltpu.HBM(land.shape, land.dtype),
        in_specs=(_hbm_spec(), _sem_spec(), _sem_spec(), pl.BlockSpec(memory_space=pl.ANY)),
        out_specs=_hbm_spec(), input_output_aliases={0: 0}, compiler_params=_split_params(),
    )(land, send_sems, recv_sems, after)


def _flips():
    return [(fx, fy, fc) for fx in (0, 1) for fy in (0, 1) for fc in (0, 1) if (fx, fy, fc) != (0, 0, 0)]


def _flipped(x, y, c, flip):
    fx, fy, fc = flip
    return (1 - x if fx else x, 1 - y if fy else y, 1 - c if fc else c)


def gather_all_start(land, name):
    def body(land_ref, send_sems, recv_sems, land_thru, token):
        del land_thru
        x, y, c = _place()
        for k, flip in enumerate(_flips()):
            pltpu.make_async_remote_copy(
                src_ref=land_ref.at[4 * x + 2 * y + c], dst_ref=land_ref.at[4 * x + 2 * y + c],
                send_sem=send_sems.at[k], recv_sem=recv_sems.at[k], device_id=_flipped(x, y, c, flip),
                device_id_type=MESH).start()
        token[...] = jnp.zeros_like(token)

    return _call(
        body, name=name,
        out_shape=(pltpu.SemaphoreType.DMA((7,)), pltpu.SemaphoreType.DMA((7,)),
                   pltpu.HBM(land.shape, land.dtype), jax.ShapeDtypeStruct((8, LANES), F32)),
        in_specs=(_hbm_spec(),),
        out_specs=(_sem_spec(), _sem_spec(), _hbm_spec(), pl.BlockSpec(memory_space=pltpu.VMEM)),
        input_output_aliases={0: 2}, compiler_params=_split_params(),
    )(_hbm(land))


def gather_all_wait(send_sems, recv_sems, land, after, name):
    def body(land_ref, send_sems, recv_sems, after_ref, land_out):
        del after_ref, land_out
        x, y, c = _place()
        for k, flip in enumerate(_flips()):
            px, py, pc = _flipped(x, y, c, flip)
            cp = pltpu.make_async_remote_copy(
                src_ref=land_ref.at[4 * x + 2 * y + c], dst_ref=land_ref.at[4 * px + 2 * py + pc],
                send_sem=send_sems.at[k], recv_sem=recv_sems.at[k], device_id=(px, py, pc), device_id_type=MESH)
            cp.wait_send()
            cp.wait_recv()

    return _call(
        body, name=name,
        out_shape=pltpu.HBM(land.shape, land.dtype),
        in_specs=(_hbm_spec(), _sem_spec(), _sem_spec(), pl.BlockSpec(memory_space=pl.ANY)),
        out_specs=_hbm_spec(), input_output_aliases={0: 0}, compiler_params=_split_params(),
    )(land, send_sems, recv_sems, after)


def exchange_start(parts, name):
    _, r, c_ = parts.shape

    def body(parts_ref, land_ref, send_sems, recv_sems, parts_thru, land_thru, token):
        del parts_thru, land_thru
        x, y, c = _place()
        for j, (px, py) in enumerate(_other_chips(x, y)):
            pltpu.make_async_remote_copy(
                src_ref=parts_ref.at[2 * px + py], dst_ref=land_ref.at[j],
                send_sem=send_sems.at[j], recv_sem=recv_sems.at[j], device_id=(px, py, c),
                device_id_type=MESH).start()
        token[...] = jnp.zeros_like(token)

    return _call(
        body, name=name,
        out_shape=(pltpu.SemaphoreType.DMA((3,)), pltpu.SemaphoreType.DMA((3,)),
                   pltpu.HBM(parts.shape, parts.dtype), pltpu.HBM((3, r, c_), parts.dtype),
                   jax.ShapeDtypeStruct((8, LANES), F32)),
        in_specs=(_hbm_spec(), _hbm_spec()),
        out_specs=(_sem_spec(), _sem_spec(), _hbm_spec(), _hbm_spec(), pl.BlockSpec(memory_space=pltpu.VMEM)),
        input_output_aliases={0: 2, 1: 3}, compiler_params=_split_params(),
    )(_hbm(parts), _hbm(lax.empty((3, r, c_), parts.dtype)))


def exchange_wait(send_sems, recv_sems, parts, land, after, name):
    def body(parts_ref, land_ref, send_sems, recv_sems, after_ref, parts_out, land_out):
        del after_ref, parts_out, land_out
        x, y, c = _place()
        for j, (px, py) in enumerate(_other_chips(x, y)):
            cp = pltpu.make_async_remote_copy(
                src_ref=parts_ref.at[2 * px + py], dst_ref=land_ref.at[j],
                send_sem=send_sems.at[j], recv_sem=recv_sems.at[j], device_id=(px, py, c), device_id_type=MESH)
            cp.wait_send()
            cp.wait_recv()

    return _call(
        body, name=name,
        out_shape=(pltpu.HBM(parts.shape, parts.dtype), pltpu.HBM(land.shape, land.dtype)),
        in_specs=(_hbm_spec(), _hbm_spec(), _sem_spec(), _sem_spec(), pl.BlockSpec(memory_space=pl.ANY)),
        out_specs=(_hbm_spec(), _hbm_spec()), input_output_aliases={0: 0, 1: 1},
        compiler_params=_split_params(),
    )(parts, land, send_sems, recv_sems, after)


def cast_into_slot(w, chip, after, name):
    r, c = w.shape
    tr = min(256, r)

    def body(s_ref, w_ref, after_ref, o_ref):
        del s_ref, after_ref
        o_ref[...] = w_ref[...].astype(BF16)

    return _call(
        body, name=name,
        grid_spec=pltpu.PrefetchScalarGridSpec(
            num_scalar_prefetch=1, grid=(r // tr,),
            in_specs=[pl.BlockSpec((tr, c), lambda i, s: (i, 0)), pl.BlockSpec(memory_space=pl.ANY)],
            out_specs=pl.BlockSpec((None, tr, c), lambda i, s: (s[0], i, 0))),
        out_shape=jax.ShapeDtypeStruct((N_CHIPS, r, c), BF16),
        compiler_params=_params(),
    )(chip.reshape(1).astype(jnp.int32), w, after)


def sum_parts(parts, land, chip, name):
    _, r, c = parts.shape
    tr = min(256, r)

    def body(s_ref, p_ref, l_ref, o_ref):
        del s_ref
        acc = p_ref[...].astype(F32) + l_ref[0].astype(F32)
        acc = acc + l_ref[1].astype(F32)
        o_ref[...] = (acc + l_ref[2].astype(F32)).astype(BF16)

    return _call(
        body, name=name,
        grid_spec=pltpu.PrefetchScalarGridSpec(
            num_scalar_prefetch=1, grid=(r // tr,),
            in_specs=[pl.BlockSpec((None, tr, c), lambda i, s: (s[0], i, 0)),
                      pl.BlockSpec((3, tr, c), lambda i, s: (0, i, 0))],
            out_specs=pl.BlockSpec((tr, c), lambda i, s: (i, 0))),
        out_shape=jax.ShapeDtypeStruct((r, c), BF16),
        compiler_params=_params(),
    )(chip.reshape(1).astype(jnp.int32), parts, land)


def swap_sibling(arrs, name):
    n = len(arrs)

    def body(*refs):
        ins, outs = refs[:n], refs[n:2 * n]
        send_sems, recv_sems = refs[2 * n:]
        x, y, c = _place()
        cps = []
        for w in range(n):
            cp = pltpu.make_async_remote_copy(
                src_ref=ins[w], dst_ref=outs[w], send_sem=send_sems.at[w], recv_sem=recv_sems.at[w],
                device_id=(x, y, 1 - c), device_id_type=MESH)
            cp.start()
            cps.append(cp)
        for cp in cps:
            cp.wait_recv()
        for cp in cps:
            cp.wait_send()

    return _call(
        body, name=name,
        out_shape=[jax.ShapeDtypeStruct(a.shape, a.dtype) for a in arrs],
        in_specs=[_hbm_spec()] * n, out_specs=[_hbm_spec()] * n,
        scratch_shapes=[pltpu.SemaphoreType.DMA((n,)), pltpu.SemaphoreType.DMA((n,))],
    )(*arrs)


def adamw_pair(pa, pb, w, m, v, name):
    r, c = w.shape
    tr = min(128, r)

    def body(pa_ref, pb_ref, w_ref, m_ref, v_ref, g_ref, d_ref, nm_ref, nv_ref):
        g = pa_ref[...].astype(F32) + pb_ref[...].astype(F32)
        d, nm, nv = _adamw(w_ref[...], g, m_ref[...], v_ref[...])
        g_ref[...] = g
        d_ref[...] = d
        nm_ref[...] = nm
        nv_ref[...] = nv

    spec = pl.BlockSpec((tr, c), lambda i: (i, 0))
    return _call(
        body, name=name, grid=(r // tr,),
        out_shape=[jax.ShapeDtypeStruct((r, c), F32)] * 4,
        in_specs=[spec] * 5, out_specs=[spec] * 4,
        compiler_params=_params(),
    )(pa, pb, w, m, v)


def small_update(gathered, first_row, ws, ms, vs, name):
    n_w = len(ws)
    total_rows = gathered.shape[1]

    def body(*refs):
        g_ref = refs[0]
        w_refs, m_refs, v_refs = refs[1:1 + n_w], refs[1 + n_w:1 + 2 * n_w], refs[1 + 2 * n_w:1 + 3 * n_w]
        tail_ref = refs[1 + 3 * n_w]
        outs = refs[2 + 3 * n_w:2 + 7 * n_w]
        sum_ref = refs[2 + 7 * n_w]
        acc = g_ref[0]
        for k in range(1, N_DEV):
            acc = acc + g_ref[k]
        sum_ref[...] = acc
        row = first_row
        for p in range(n_w):
            a, b = ws[p].shape
            per = b // LANES
            g_out, d_out, m_out, v_out = outs[4 * p:4 * p + 4]
            if per == 1:
                g_out[...] = sum_ref[row:row + a, :]
            else:
                for i in range(a):
                    for jc in range(per):
                        g_out[i:i + 1, jc * LANES:(jc + 1) * LANES] = sum_ref[row + i * per + jc:row + i * per + jc + 1, :]
            row += a * per
            dl, nm, nv = _adamw(w_refs[p][...], g_out[...], m_refs[p][...], v_refs[p][...])
            d_out[...] = dl
            m_out[...] = nm
            v_out[...] = nv
        tail_ref[...] = sum_ref[row:row + 1, :]

    out_shape = [jax.ShapeDtypeStruct((1, LANES), F32)]
    for w in ws:
        out_shape += [jax.ShapeDtypeStruct(w.shape, F32)] * 4
    res = _call(
        body, name=name, out_shape=out_shape,
        scratch_shapes=[pltpu.VMEM((total_rows, LANES), F32)],
        compiler_params=_params(),
    )(gathered, *ws, *ms, *vs)
    return res[0], [res[1 + 4 * p:5 + 4 * p] for p in range(n_w)]


def ada_fwd(c_all, w_ada, b_cols, after, name):
    n_l, d, cols = w_ada.shape
    nb = c_all.shape[0]
    tn = 256

    def body(c_ref, w_ref, b_ref, after_ref, o_ref):
        del after_ref
        cv = c_ref[...]
        ca = (cv * _sigmoid(cv)).astype(BF16)
        o_ref[...] = _dot(ca, w_ref[...].astype(BF16), NN) + b_ref[...]

    return _call(
        body, name=name, grid=(n_l, cols // tn),
        out_shape=jax.ShapeDtypeStruct((n_l, nb, cols), F32),
        in_specs=[pl.BlockSpec((nb, d), lambda l, j: (0, 0)),
                  pl.BlockSpec((None, d, tn), lambda l, j: (l, 0, j)),
                  pl.BlockSpec((None, 1, tn), lambda l, j: (l, 0, j)),
                  pl.BlockSpec(memory_space=pl.ANY)],
        out_specs=pl.BlockSpec((None, nb, tn), lambda l, j: (l, 0, j)),
        compiler_params=_params(),
    )(c_all, w_ada, b_cols, after)


def ada_bwd(c_all, dmod_cols, w, m, v, name):
    n_l, d, cols = w.shape
    nb = c_all.shape[0]
    tn = 256

    def body(c_ref, dm_ref, w_ref, m_ref, v_ref, g_ref, d_ref, nm_ref, nv_ref):
        cv = c_ref[...]
        ca = (cv * _sigmoid(cv)).astype(BF16)
        g = _dot(ca, dm_ref[...].astype(BF16), TN)
        dl, nm, nv = _adamw(w_ref[...], g, m_ref[...], v_ref[...])
        g_ref[...] = g
        d_ref[...] = dl
        nm_ref[...] = nm
        nv_ref[...] = nv

    wspec = pl.BlockSpec((None, d, tn), lambda l, j: (l, 0, j))
    return _call(
        body, name=name, grid=(n_l, cols // tn),
        out_shape=[jax.ShapeDtypeStruct((n_l, d, cols), F32)] * 4,
        in_specs=[pl.BlockSpec((nb, d), lambda l, j: (0, 0)),
                  pl.BlockSpec((None, nb, tn), lambda l, j: (l, 0, j)),
                  wspec, wspec, wspec],
        out_specs=[wspec] * 4,
        compiler_params=_params(),
    )(c_all, dmod_cols, w, m, v)


def bias_update(dmod_all, w, m, v, name):
    def body(dm_ref, w_ref, m_ref, v_ref, g_ref, d_ref, nm_ref, nv_ref):
        g = jnp.sum(dm_ref[...], axis=0, keepdims=True)
        dl, nm, nv = _adamw(w_ref[...], g, m_ref[...], v_ref[...])
        g_ref[...] = g
        d_ref[...] = dl
        nm_ref[...] = nm
        nv_ref[...] = nv

    return _call(
        body, name=name,
        out_shape=[jax.ShapeDtypeStruct(w.shape, F32)] * 4,
        compiler_params=_params(),
    )(dmod_all, w, m, v)


def inproj_fwd(x, mod, ng, wg, seq, sectioned, name):
    m_rows, d = x.shape
    nsh, _, ns = wg.shape
    n = nsh * ns
    tm, tn = min(2 * ROW_TILE, seq), _col_tile(ns)
    per = ns // tn

    def body(x_ref, mod_ref, ng_ref, w_ref, proj_ref, h_ref):
        @pl.when(pl.program_id(1) == 0)
        def _():
            xv = x_ref[...]
            r = lax.rsqrt(jnp.mean(xv * xv, axis=-1, keepdims=True) + EPS)
            md = mod_ref[0]
            h = (xv * r * ng_ref[...]) * (1.0 + md[:, d:2 * d]) + md[:, :d]
            h_ref[...] = h.astype(BF16)
        proj_ref[...] = _dot(h_ref[...], w_ref[...], NN)

    if sectioned:
        proj_shape = (nsh, m_rows, ns)
        proj_spec = pl.BlockSpec((None, tm, tn), lambda i, j: (j // per, i, j % per))
    else:
        proj_shape = (m_rows, n)
        proj_spec = pl.BlockSpec((tm, tn), lambda i, j: (i, j))
    return _call(
        body, name=name, grid=(m_rows // tm, n // tn),
        out_shape=[jax.ShapeDtypeStruct(proj_shape, F32), jax.ShapeDtypeStruct((m_rows, d), BF16)],
        in_specs=[pl.BlockSpec((tm, d), lambda i, j: (i, 0)),
                  pl.BlockSpec((1, 1, 3 * d), lambda i, j: ((i * tm) // seq, 0, 0)),
                  pl.BlockSpec((1, d), lambda i, j: (0, 0)),
                  pl.BlockSpec((None, d, tn), lambda i, j: (j // per, 0, j % per))],
        out_specs=[proj_spec, pl.BlockSpec((tm, d), lambda i, j: (i, 0))],
        compiler_params=_params(),
    )(x, mod, ng, wg)


def outproj_fwd(y, w, x, mod, seq, name):
    m_rows, di = y.shape
    d = w.shape[1]
    tm = min(ROW_TILE, seq)

    def body(y_ref, w_ref, x_ref, mod_ref, xn_ref, out_ref):
        acc = _dot(y_ref[...], w_ref[...], NN)
        out_ref[...] = acc.astype(BF16)
        xn_ref[...] = x_ref[...] + mod_ref[0][:, 2 * d:] * acc

    row = pl.BlockSpec((tm, d), lambda i: (i, 0))
    return _call(
        body, name=name, grid=(m_rows // tm,),
        out_shape=[jax.ShapeDtypeStruct((m_rows, d), F32), jax.ShapeDtypeStruct((m_rows, d), BF16)],
        in_specs=[pl.BlockSpec((tm, di), lambda i: (i, 0)),
                  pl.BlockSpec((di, d), lambda i: (0, 0)),
                  row,
                  pl.BlockSpec((1, 1, 3 * d), lambda i: ((i * tm) // seq, 0, 0))],
        out_specs=[row, row],
        compiler_params=_params(),
    )(y, w, x, mod)


def outproj_bwd(dxo, out, mod, w, seq, name):
    m_rows, d = dxo.shape
    di = w.shape[0]
    nb = m_rows // seq
    tm, tn = min(ROW_TILE, seq), di

    def body(dxo_ref, out_ref, mod_ref, w_ref, dy_ref, dout_ref, dgate_ref):
        i = pl.program_id(0)

        @pl.when(pl.program_id(1) == 0)
        def _():
            dx = dxo_ref[...]
            dout_ref[...] = (mod_ref[0][:, 2 * d:] * dx).astype(BF16)
            part = jnp.sum(dx * out_ref[...].astype(F32), axis=0, keepdims=True)

            @pl.when((i * tm) % seq == 0)
            def _():
                dgate_ref[0] = part

            @pl.when((i * tm) % seq != 0)
            def _():
                dgate_ref[0] = dgate_ref[0] + part

        dy_ref[...] = _dot(dout_ref[...], w_ref[...], NT).astype(BF16)

    row = pl.BlockSpec((tm, d), lambda i, j: (i, 0))
    return _call(
        body, name=name, grid=(m_rows // tm, di // tn),
        out_shape=[jax.ShapeDtypeStruct((m_rows, di), BF16), jax.ShapeDtypeStruct((m_rows, d), BF16),
                   jax.ShapeDtypeStruct((nb, 1, d), F32)],
        in_specs=[row, row,
                  pl.BlockSpec((1, 1, 3 * d), lambda i, j: ((i * tm) // seq, 0, 0)),
                  pl.BlockSpec((tn, d), lambda i, j: (j, 0))],
        out_specs=[pl.BlockSpec((tm, tn), lambda i, j: (i, j)), row,
                   pl.BlockSpec((1, 1, d), lambda i, j: ((i * tm) // seq, 0, 0))],
        compiler_params=_params(),
    )(dxo, out, mod, w)


def grad_w_out(y, dout, name):
    m_rows, di = y.shape
    d = dout.shape[1]
    tm, tk = min(ROW_TILE, m_rows), di
    n_m = m_rows // tm

    def body(y_ref, do_ref, o_ref, acc_ref):
        mi = pl.program_id(1)

        @pl.when(mi == 0)
        def _():
            acc_ref[...] = jnp.zeros_like(acc_ref)

        acc_ref[...] += _dot(y_ref[...], do_ref[...], TN)

        @pl.when(mi == n_m - 1)
        def _():
            o_ref[...] = acc_ref[...].astype(BF16)

    return _call(
        body, name=name, grid=(di // tk, n_m),
        out_shape=jax.ShapeDtypeStruct((di, d), BF16),
        in_specs=[pl.BlockSpec((tm, tk), lambda j, mi: (mi, j)),
                  pl.BlockSpec((tm, d), lambda j, mi: (mi, 0))],
        out_specs=pl.BlockSpec((tk, d), lambda j, mi: (j, 0)),
        scratch_shapes=[pltpu.VMEM((tk, d), F32)],
        compiler_params=_params(),
    )(y, dout)


def grad_w_in(h, dproj, nsh, sectioned, name):
    m_rows, d = h.shape
    n = dproj.shape[0] * dproj.shape[2] if sectioned else dproj.shape[1]
    ns = n // nsh
    tm, tn = min(ROW_TILE, m_rows), ns
    per = ns // tn
    n_m = m_rows // tm

    def body(h_ref, dp_ref, o_ref, acc_ref):
        mi = pl.program_id(1)
        @pl.when(mi == 0)
        def _():
            acc_ref[...] = jnp.zeros_like(acc_ref)

        acc_ref[...] += _dot(h_ref[...], dp_ref[...], TN)

        @pl.when(mi == n_m - 1)
        def _():
            o_ref[...] = acc_ref[...].astype(BF16)

    if sectioned:
        dp_spec = pl.BlockSpec((None, tm, tn), lambda j, mi: (j // per, mi, j % per))
    else:
        dp_spec = pl.BlockSpec((tm, tn), lambda j, mi: (mi, j))
    return _call(
        body, name=name, grid=(n // tn, n_m),
        out_shape=jax.ShapeDtypeStruct((nsh, d, ns), BF16),
        in_specs=[pl.BlockSpec((tm, d), lambda j, mi: (mi, 0)), dp_spec],
        out_specs=pl.BlockSpec((None, d, tn), lambda j, mi: (j // per, 0, j % per)),
        scratch_shapes=[pltpu.VMEM((d, tn), F32)],
        compiler_params=_params(),
    )(h, dproj)


def inproj_bwd(dproj, wg, x, dxo, mod, ng, seq, sectioned, name):
    m_rows, d = x.shape
    nsh, _, ns = wg.shape
    n = nsh * ns
    nb = m_rows // seq
    tm, tk = min(ROW_TILE, seq), ns
    per = ns // tk
    n_k = n // tk

    def body(dp_ref, w_ref, x_ref, dxo_ref, mod_ref, ng_ref, dxi_ref, dsh_ref, dsc_ref, dng_ref, acc_ref):
        i, k = pl.program_id(0), pl.program_id(1)
        @pl.when(k == 0)
        def _():
            acc_ref[...] = jnp.zeros_like(acc_ref)

        acc_ref[...] += _dot(dp_ref[...], w_ref[...], NT)

        @pl.when(k == n_k - 1)
        def _():
            dh = acc_ref[...]
            xv = x_ref[...]
            r = lax.rsqrt(jnp.mean(xv * xv, axis=-1, keepdims=True) + EPS)
            xn = xv * r
            md = mod_ref[0]
            gain = ng_ref[...]
            p_shift = jnp.sum(dh, axis=0, keepdims=True)
            p_scale = jnp.sum(dh * (xn * gain), axis=0, keepdims=True)
            drn = dh * (1.0 + md[:, d:2 * d])
            p_ng = jnp.sum(drn * xn, axis=0, keepdims=True)
            dxn = drn * gain
            dx = r * (dxn - xn * jnp.mean(dxn * xn, axis=-1, keepdims=True))
            dxi_ref[...] = dxo_ref[...] + dx

            @pl.when((i * tm) % seq == 0)
            def _():
                dsh_ref[0] = p_shift
                dsc_ref[0] = p_scale

            @pl.when((i * tm) % seq != 0)
            def _():
                dsh_ref[0] = dsh_ref[0] + p_shift
                dsc_ref[0] = dsc_ref[0] + p_scale

            @pl.when(i == 0)
            def _():
                dng_ref[...] = p_ng

            @pl.when(i != 0)
            def _():
                dng_ref[...] = dng_ref[...] + p_ng

    if sectioned:
        dp_spec = pl.BlockSpec((None, tm, tk), lambda i, k: (k // per, i, k % per))
    else:
        dp_spec = pl.BlockSpec((tm, tk), lambda i, k: (i, k))
    row = pl.BlockSpec((tm, d), lambda i, k: (i, 0))
    per_seq = pl.BlockSpec((1, 1, d), lambda i, k: ((i * tm) // seq, 0, 0))
    return _call(
        body, name=name, grid=(m_rows // tm, n_k),
        out_shape=[jax.ShapeDtypeStruct((m_rows, d), F32), jax.ShapeDtypeStruct((nb, 1, d), F32),
                   jax.ShapeDtypeStruct((nb, 1, d), F32), jax.ShapeDtypeStruct((1, d), F32)],
        in_specs=[dp_spec,
                  pl.BlockSpec((None, d, tk), lambda i, k: (k // per, 0, k % per)),
                  row, row,
                  pl.BlockSpec((1, 1, 3 * d), lambda i, k: ((i * tm) // seq, 0, 0)),
                  pl.BlockSpec((1, d), lambda i, k: (0, 0))],
        out_specs=[row, per_seq, per_seq, pl.BlockSpec((1, d), lambda i, k: (0, 0))],
        scratch_shapes=[pltpu.VMEM((tm, d), F32)],
        compiler_params=_params(),
    )(dproj, wg, x, dxo, mod, ng)


def _sgu_stats(proj_ref, vg_ref, di, gd, dgel_ref=None):
    s1 = jnp.zeros((SG_BLOCK, 1), F32)
    for g in range(SG_GROUPS):
        v_pre = proj_ref[:, di + g * gd:di + (g + 1) * gd]
        if dgel_ref is None:
            vg = _gelu(v_pre)
        else:
            vg, dgel_ref[:, g * gd:(g + 1) * gd] = _gelu_and_grad(v_pre)
        vg_ref[:, g * gd:(g + 1) * gd] = vg
        s1 = s1 + jnp.sum(vg, axis=1, keepdims=True)
    mu = s1 / di
    s2 = jnp.zeros((SG_BLOCK, 1), F32)
    for g in range(SG_GROUPS):
        dv = vg_ref[:, g * gd:(g + 1) * gd] - mu
        s2 = s2 + jnp.sum(dv * dv, axis=1, keepdims=True)
    return mu, lax.rsqrt(s2 / di + EPS)


def sgu_fwd(proj, ln_gain, ln_bias, ws, bs, name):
    m_rows, n3 = proj.shape
    di = n3 // 3
    gd = di // SG_GROUPS

    def body(proj_ref, lg_ref, lb_ref, ws_ref, bs_ref, y_ref, wsm_ref, vg_ref):
        @pl.when(pl.program_id(0) == 0)
        def _():
            mask = _chunk_mask()
            for g in range(SG_GROUPS):
                wsm_ref[g] = jnp.where(mask, ws_ref[g], 0.0).astype(BF16)

        mu, rstd = _sgu_stats(proj_ref, vg_ref, di, gd)
        for g in range(SG_GROUPS):
            cs = slice(g * gd, (g + 1) * gd)
            vln = (vg_ref[:, cs] - mu) * rstd * lg_ref[:, cs] + lb_ref[:, cs]
            s = _dot(wsm_ref[g], vln.astype(BF16), NN) + bs_ref[g]
            u = _gelu(proj_ref[:, cs])
            gp = proj_ref[:, 2 * di + g * gd:2 * di + (g + 1) * gd]
            y_ref[:, cs] = (u * s * (gp * _sigmoid(gp))).astype(BF16)

    full = lambda shape: pl.BlockSpec(shape, lambda i: (0,) * len(shape))
    return _call(
        body, name=name, grid=(m_rows // SG_BLOCK,),
        out_shape=jax.ShapeDtypeStruct((m_rows, di), BF16),
        in_specs=[pl.BlockSpec((SG_BLOCK, n3), lambda i: (i, 0)),
                  full((1, di)), full((1, di)),
                  full((SG_GROUPS, SG_BLOCK, SG_BLOCK)), full((SG_GROUPS, SG_BLOCK, 1))],
        out_specs=pl.BlockSpec((SG_BLOCK, di), lambda i: (i, 0)),
        scratch_shapes=[pltpu.VMEM((SG_GROUPS, SG_BLOCK, SG_BLOCK), BF16), pltpu.VMEM((SG_BLOCK, di), F32)],
        compiler_params=_params(),
    )(proj, ln_gain, ln_bias, ws, bs)


def sgu_bwd(proj, dy, ln_gain, ln_bias, ws, bs, name):
    m_rows, n3 = proj.shape
    di = n3 // 3
    gd = di // SG_GROUPS
    n_i = m_rows // SG_BLOCK

    def body(proj_ref, dy_ref, lg_ref, lb_ref, ws_ref, bs_ref,
             dp_ref, dws_ref, dbs_ref, dlg_ref, dlb_ref, wsm_ref, vg_ref, dvh_ref, dgel_ref):
        i = pl.program_id(0)

        def before():
            @pl.when(i == 0)
            def _():
                mask = _chunk_mask()
                for g in range(SG_GROUPS):
                    wsm_ref[g] = jnp.where(mask, ws_ref[g], 0.0).astype(BF16)
                dws_ref[...] = jnp.zeros_like(dws_ref)
                dbs_ref[...] = jnp.zeros_like(dbs_ref)
                dlg_ref[...] = jnp.zeros_like(dlg_ref)
                dlb_ref[...] = jnp.zeros_like(dlb_ref)

        def after():
            @pl.when(i == n_i - 1)
            def _():
                mask = _chunk_mask()
                for g in range(SG_GROUPS):
                    dws_ref[g] = jnp.where(mask, dws_ref[g], 0.0)

        before()
        mu, rstd = _sgu_stats(proj_ref, vg_ref, di, gd, dgel_ref)
        m1 = jnp.zeros((SG_BLOCK, 1), F32)
        m2 = jnp.zeros((SG_BLOCK, 1), F32)
        for g in range(SG_GROUPS):
            cs = slice(g * gd, (g + 1) * gd)
            gs = slice(2 * di + g * gd, 2 * di + (g + 1) * gd)
            gain = lg_ref[:, cs]
            vhat = (vg_ref[:, cs] - mu) * rstd
            vln_b = (vhat * gain + lb_ref[:, cs]).astype(BF16)
            s = _dot(wsm_ref[g], vln_b, NN) + bs_ref[g]
            u, du = _gelu_and_grad(proj_ref[:, cs])
            sg, dsg = _silu_and_grad(proj_ref[:, gs])
            dyv = dy_ref[:, cs].astype(F32)
            dp_ref[:, cs] = (dyv * s * sg * du).astype(BF16)
            dp_ref[:, gs] = (dyv * u * s * dsg).astype(BF16)
            ds = dyv * u * sg
            ds_b = ds.astype(BF16)
            dws_ref[g] = dws_ref[g] + _dot(ds_b, vln_b, NT)
            dbs_ref[g] = dbs_ref[g] + jnp.sum(ds, axis=1, keepdims=True)
            dvln = _dot(wsm_ref[g], ds_b, TN)
            dlg_ref[:, cs] = dlg_ref[:, cs] + jnp.sum(dvln * vhat, axis=0, keepdims=True)
            dlb_ref[:, cs] = dlb_ref[:, cs] + jnp.sum(dvln, axis=0, keepdims=True)
            dvh = dvln * gain
            dvh_ref[:, cs] = dvh
            m1 = m1 + jnp.sum(dvh, axis=1, keepdims=True)
            m2 = m2 + jnp.sum(dvh * vhat, axis=1, keepdims=True)
        m1 = m1 / di
        m2 = m2 / di
        for g in range(SG_GROUPS):
            cs = slice(g * gd, (g + 1) * gd)
            vs = slice(di + g * gd, di + (g + 1) * gd)
            vhat = (vg_ref[:, cs] - mu) * rstd
            dvg = rstd * (dvh_ref[:, cs] - m1 - vhat * m2)
            dp_ref[:, vs] = (dvg * dgel_ref[:, cs]).astype(BF16)

        after()

    full = lambda shape: pl.BlockSpec(shape, lambda i: (0,) * len(shape))
    return _call(
        body, name=name, grid=(n_i,),
        out_shape=[jax.ShapeDtypeStruct((m_rows, n3), BF16),
                   jax.ShapeDtypeStruct((SG_GROUPS, SG_BLOCK, SG_BLOCK), F32),
                   jax.ShapeDtypeStruct((SG_GROUPS, SG_BLOCK, 1), F32),
                   jax.ShapeDtypeStruct((1, di), F32), jax.ShapeDtypeStruct((1, di), F32)],
        in_specs=[pl.BlockSpec((SG_BLOCK, n3), lambda i: (i, 0)),
                  pl.BlockSpec((SG_BLOCK, di), lambda i: (i, 0)),
                  full((1, di)), full((1, di)),
                  full((SG_GROUPS, SG_BLOCK, SG_BLOCK)), full((SG_GROUPS, SG_BLOCK, 1))],
        out_specs=[pl.BlockSpec((SG_BLOCK, n3), lambda i: (i, 0)),
                   full((SG_GROUPS, SG_BLOCK, SG_BLOCK)), full((SG_GROUPS, SG_BLOCK, 1)),
                   full((1, di)), full((1, di))],
        scratch_shapes=[pltpu.VMEM((SG_GROUPS, SG_BLOCK, SG_BLOCK), BF16),
                        pltpu.VMEM((SG_BLOCK, di), F32), pltpu.VMEM((SG_BLOCK, di), F32),
                        pltpu.VMEM((SG_BLOCK, di), F32)],
        compiler_params=_params(),
    )(proj, dy, ln_gain, ln_bias, ws, bs)


def _lower_bound(lbraw):
    mx = jnp.maximum(lbraw[0:1, :], lbraw[1:2, :])
    e0 = jnp.exp(lbraw[0:1, :] - mx)
    e1 = jnp.exp(lbraw[1:2, :] - mx)
    p0 = e0 / (e0 + e1)
    p1 = e1 / (e0 + e1)
    return (p0 + p1) - p0, p0, p1


def _tri(lower):
    r = lax.broadcasted_iota(jnp.int32, (CHUNK, CHUNK), 0)
    c = lax.broadcasted_iota(jnp.int32, (CHUNK, CHUNK), 1)
    return ((r >= c) if lower else (c >= r)).astype(BF16)


def _running_sum(tri, x):
    x1 = x.astype(BF16)
    r1 = x - x1.astype(F32)
    x2 = r1.astype(BF16)
    x3 = (r1 - x2.astype(F32)).astype(BF16)
    return _dot(tri, x1, NN) + _dot(tri, x2, NN) + _dot(tri, x3, NN)


def _row(a, idx):
    r = lax.broadcasted_iota(jnp.int32, a.shape, 0)
    return jnp.sum(jnp.where(r == idx, a, 0.0), axis=0, keepdims=True)


def _hgrn_gates(qp, fp, lb, tri):
    sgm = _sigmoid_small(fp)
    f = lb + (1.0 - lb) * sgm
    k = 1.0 - f
    a = _running_sum(tri, jnp.log(f))
    a_mid = _row(a, CHUNK // 2 - 1)
    a_last = _row(a, CHUNK - 1)
    q, dq = _silu_and_grad(qp)
    e1, e2, e3, e4 = jnp.exp(a - a_mid), jnp.exp(a_mid - a), jnp.exp(a), jnp.exp(a_last - a)
    return dict(sgm=sgm, f=f, k=k, q=q, dq=dq, e1=e1, e2=e2, e3=e3, e4=e4, dec=jnp.exp(a_last),
                q_in=q * e1, k_in=k * e2, q_out=q * e3, k_out=k * e4)


def _causal():
    r = lax.broadcasted_iota(jnp.int32, (CHUNK, CHUNK), 0)
    c = lax.broadcasted_iota(jnp.int32, (CHUNK, CHUNK), 1)
    return r >= c


def hgrn_fwd(proj4, lbraw, gn, seq, name):
    _, m_rows, di = proj4.shape
    nb, nh, nc = m_rows // seq, di // HEAD_DIM, seq // CHUNK
    rows = min(HG_ROWS, seq)
    wide = HG_WIDE * HEAD_DIM
    ns, cpb = seq // rows, rows // CHUNK

    def body(p_ref, lb_ref, gn_ref, y_ref, sts_ref, st_ref):
        @pl.when(pl.program_id(2) == 0)
        def _():
            st_ref[...] = jnp.zeros_like(st_ref)

        tri = _tri(True)
        causal = _causal()
        gain = gn_ref[...]
        lbs = [_lower_bound(lb_ref[:, j * HEAD_DIM:(j + 1) * HEAD_DIM])[0] for j in range(HG_WIDE)]

        units = [(n, j) for n in range(cpb) for j in range(HG_WIDE)]
        rs = lambda n: slice(n * CHUNK, (n + 1) * CHUNK)
        cs = lambda j: slice(j * HEAD_DIM, (j + 1) * HEAD_DIM)
        gates, v_b, sc_b, kv, o_in, o_x = {}, {}, {}, {}, {}, {}
        for n, j in units:
            gates[n, j] = _hgrn_gates(p_ref[0, rs(n), cs(j)], p_ref[1, rs(n), cs(j)], lbs[j], tri)
            v_b[n, j] = p_ref[2, rs(n), cs(j)].astype(BF16)
        for u in units:
            t = gates[u]
            sc_b[u] = jnp.where(causal, _dot(t["q_in"].astype(BF16), t["k_in"].astype(BF16), NT), 0.0).astype(BF16)
            kv[u] = _dot(v_b[u], t["k_out"].astype(BF16), TN)
        for u in units:
            o_in[u] = _dot(sc_b[u], v_b[u], NN)
        for j in range(HG_WIDE):
            st = st_ref[j]
            for n in range(cpb):
                sts_ref[n, :, cs(j)] = st
                o_x[n, j] = _dot(gates[n, j]["q_out"].astype(BF16), st.astype(BF16), NT)
                st = st * gates[n, j]["dec"] + kv[n, j]
            st_ref[j] = st
        for n, j in units:
            o = o_in[n, j] + o_x[n, j]
            r = lax.rsqrt(jnp.mean(o * o, axis=-1, keepdims=True) + EPS)
            gp = p_ref[3, rs(n), cs(j)]
            y_ref[rs(n), cs(j)] = ((o * r * gain) * (gp * _sigmoid(gp))).astype(BF16)

    return _call(
        body, name=name, grid=(nh // HG_WIDE, nb, ns),
        out_shape=[jax.ShapeDtypeStruct((m_rows, di), BF16),
                   jax.ShapeDtypeStruct((nb * nc, HEAD_DIM, di), F32)],
        in_specs=[pl.BlockSpec((4, rows, wide), lambda hg, b, s: (0, b * ns + s, hg)),
                  pl.BlockSpec((2, wide), lambda hg, b, s: (0, hg)),
                  pl.BlockSpec((1, HEAD_DIM), lambda hg, b, s: (0, 0))],
        out_specs=[pl.BlockSpec((rows, wide), lambda hg, b, s: (b * ns + s, hg)),
                   pl.BlockSpec((cpb, HEAD_DIM, wide), lambda hg, b, s: (b * ns + s, 0, hg))],
        scratch_shapes=[pltpu.VMEM((HG_WIDE, HEAD_DIM, HEAD_DIM), F32)],
        compiler_params=_params(),
    )(proj4, lbraw, gn)


def hgrn_bwd(proj4, dy, sts, lbraw, gn, seq, name):
    _, m_rows, di = proj4.shape
    nb, nh, nc = m_rows // seq, di // HEAD_DIM, seq // CHUNK
    rows = min(HG_ROWS, seq)
    wide = HG_WIDE * HEAD_DIM
    ns, cpb = seq // rows, rows // CHUNK
    n_hg = nh // HG_WIDE

    def body(p_ref, dy_ref, sts_ref, lb_ref, gn_ref, dp_ref, dlb_ref, dgn_ref, dst_ref, lbacc_ref, gnacc_ref):
        hg, b, s = pl.program_id(0), pl.program_id(1), pl.program_id(2)
        tri, triu = _tri(True), _tri(False)
        causal = _causal()
        gain = gn_ref[...]
        first = (b == 0) & (s == 0)
        cs = lambda j: slice(j * HEAD_DIM, (j + 1) * HEAD_DIM)

        def before():
            @pl.when((hg == 0) & first)
            def _():
                gnacc_ref[...] = jnp.zeros_like(gnacc_ref)

            @pl.when(first)
            def _():
                lbacc_ref[...] = jnp.zeros_like(lbacc_ref)

            @pl.when(s == 0)
            def _():
                dst_ref[...] = jnp.zeros_like(dst_ref)

        def after():
            @pl.when((b == nb - 1) & (s == ns - 1))
            def _():
                for j in range(HG_WIDE):
                    _, p0, p1 = _lower_bound(lb_ref[:, cs(j)])
                    acc = lbacc_ref[:, cs(j)]
                    dlb_ref[0:1, cs(j)] = -acc * p0 * p1
                    dlb_ref[1:2, cs(j)] = acc * p1 * (1.0 - p1)

            @pl.when((hg == n_hg - 1) & (b == nb - 1) & (s == ns - 1))
            def _():
                tot = gnacc_ref[:, 0:HEAD_DIM]
                for j in range(1, HG_WIDE):
                    tot = tot + gnacc_ref[:, cs(j)]
                dgn_ref[...] = tot

        before()

        units = [(n, j) for n in range(cpb) for j in range(HG_WIDE)]
        rs = lambda n: slice(n * CHUNK, (n + 1) * CHUNK)
        lbs = [_lower_bound(lb_ref[:, cs(j)])[0] for j in range(HG_WIDE)]
        gates, v_b, st_b, sc_b, o, do_b = {}, {}, {}, {}, {}, {}
        dq_out, dsc_b, dv, g_st, dq_in, dk_in, dst_at, dk_out, ddec = {}, {}, {}, {}, {}, {}, {}, {}, {}
        for n, j in units:
            gates[n, j] = _hgrn_gates(p_ref[0, rs(n), cs(j)], p_ref[1, rs(n), cs(j)], lbs[j], tri)
            v_b[n, j] = p_ref[2, rs(n), cs(j)].astype(BF16)
            st_b[n, j] = sts_ref[n, :, cs(j)].astype(BF16)
        for u in units:
            t = gates[u]
            sc_b[u] = jnp.where(causal, _dot(t["q_in"].astype(BF16), t["k_in"].astype(BF16), NT), 0.0).astype(BF16)
        for u in units:
            o[u] = _dot(sc_b[u], v_b[u], NN) + _dot(gates[u]["q_out"].astype(BF16), st_b[u], NT)
        for n, j in units:
            ov = o[n, j]
            r = lax.rsqrt(jnp.mean(ov * ov, axis=-1, keepdims=True) + EPS)
            ohat = ov * r
            sg, dsg = _silu_and_grad(p_ref[3, rs(n), cs(j)])
            dyv = dy_ref[rs(n), cs(j)].astype(F32)
            dp_ref[3, rs(n), cs(j)] = (dyv * (ohat * gain) * dsg).astype(BF16)
            d_on = dyv * sg
            gnacc_ref[:, cs(j)] = gnacc_ref[:, cs(j)] + jnp.sum(d_on * ohat, axis=0, keepdims=True)
            dohat = d_on * gain
            do_b[n, j] = (r * (dohat - ohat * jnp.mean(dohat * ohat, axis=-1, keepdims=True))).astype(BF16)
        for u in units:
            dq_out[u] = _dot(do_b[u], st_b[u], NN)
            dsc_b[u] = jnp.where(causal, _dot(do_b[u], v_b[u], NT), 0.0).astype(BF16)
            dv[u] = _dot(sc_b[u], do_b[u], TN)
            g_st[u] = _dot(do_b[u], gates[u]["q_out"].astype(BF16), TN)
        for u in units:
            dq_in[u] = _dot(dsc_b[u], gates[u]["k_in"].astype(BF16), NN)
            dk_in[u] = _dot(dsc_b[u], gates[u]["q_in"].astype(BF16), TN)
        for j in range(HG_WIDE):
            dst = dst_ref[j]
            for n in reversed(range(cpb)):
                dst_at[n, j] = dst
                dst = dst * gates[n, j]["dec"] + g_st[n, j]
            dst_ref[j] = dst
        for n, j in units:
            dst = dst_at[n, j]
            dst_b = dst.astype(BF16)
            dk_out[n, j] = _dot(v_b[n, j], dst_b, NN)
            dv[n, j] = dv[n, j] + _dot(gates[n, j]["k_out"].astype(BF16), dst_b, NT)
            ddec[n, j] = jnp.sum(dst * sts_ref[n, :, cs(j)], axis=0, keepdims=True)
        for n, j in units:
            t = gates[n, j]
            dp_ref[2, rs(n), cs(j)] = dv[n, j].astype(BF16)
            dq = dq_in[n, j] * t["e1"] + dq_out[n, j] * t["e3"]
            dk = dk_in[n, j] * t["e2"] + dk_out[n, j] * t["e4"]
            w_in = dq_in[n, j] * t["q_in"] - dk_in[n, j] * t["k_in"]
            w_out = dk_out[n, j] * t["k_out"]
            da = w_in + dq_out[n, j] * t["q_out"] - w_out
            da_mid = -jnp.sum(w_in, axis=0, keepdims=True)
            da_last = jnp.sum(w_out, axis=0, keepdims=True) + ddec[n, j] * t["dec"]
            rid = lax.broadcasted_iota(jnp.int32, da.shape, 0)
            da = da + jnp.where(rid == CHUNK // 2 - 1, da_mid, 0.0) + jnp.where(rid == CHUNK - 1, da_last, 0.0)
            dlf = _running_sum(triu, da)
            df = dlf / t["f"] - dk
            sgm = t["sgm"]
            dp_ref[1, rs(n), cs(j)] = (df * (1.0 - lbs[j]) * sgm * (1.0 - sgm)).astype(BF16)
            lbacc_ref[:, cs(j)] = lbacc_ref[:, cs(j)] + jnp.sum(df * (1.0 - sgm), axis=0, keepdims=True)
            dp_ref[0, rs(n), cs(j)] = (dq * t["dq"]).astype(BF16)

        after()

    blk = lambda hg, b, s: b * ns + (ns - 1 - s)
    return _call(
        body, name=name, grid=(n_hg, nb, ns),
        out_shape=[jax.ShapeDtypeStruct((4, m_rows, di), BF16), jax.ShapeDtypeStruct((2, di), F32),
                   jax.ShapeDtypeStruct((1, HEAD_DIM), F32)],
        in_specs=[pl.BlockSpec((4, rows, wide), lambda hg, b, s: (0, blk(hg, b, s), hg)),
                  pl.BlockSpec((rows, wide), lambda hg, b, s: (blk(hg, b, s), hg)),
                  pl.BlockSpec((cpb, HEAD_DIM, wide), lambda hg, b, s: (blk(hg, b, s), 0, hg)),
                  pl.BlockSpec((2, wide), lambda hg, b, s: (0, hg)),
                  pl.BlockSpec((1, HEAD_DIM), lambda hg, b, s: (0, 0))],
        out_specs=[pl.BlockSpec((4, rows, wide), lambda hg, b, s: (0, blk(hg, b, s), hg)),
                   pl.BlockSpec((2, wide), lambda hg, b, s: (0, hg)),
                   pl.BlockSpec((1, HEAD_DIM), lambda hg, b, s: (0, 0))],
        scratch_shapes=[pltpu.VMEM((HG_WIDE, HEAD_DIM, HEAD_DIM), F32), pltpu.VMEM((1, wide), F32),
                        pltpu.VMEM((1, wide), F32)],
        compiler_params=_params(),
    )(proj4, dy, sts, lbraw, gn)


def outproj_loss(y, w, x, mod, fg, target, seq, name):
    m_rows, di = y.shape
    d = w.shape[1]
    tm = min(512, seq)

    def body(y_ref, w_ref, x_ref, mod_ref, fg_ref, t_ref, out_ref, loss_ref, dx_ref, dfg_ref):
        i = pl.program_id(0)
        acc = _dot(y_ref[...], w_ref[...], NN)
        out_ref[...] = acc.astype(BF16)
        xv = x_ref[...] + mod_ref[0][:, 2 * d:] * acc
        gain = fg_ref[...]
        r = lax.rsqrt(jnp.mean(xv * xv, axis=-1, keepdims=True) + EPS)
        xn = xv * r
        e = xn * gain - t_ref[...]
        part = 0.5 * jnp.sum(jnp.mean(e * e, axis=-1, keepdims=True), axis=0, keepdims=True)
        dyv = e / d
        p_fg = jnp.sum(dyv * xn, axis=0, keepdims=True)
        dxn = dyv * gain
        dx_ref[...] = r * (dxn - xn * jnp.mean(dxn * xn, axis=-1, keepdims=True))

        @pl.when(i == 0)
        def _():
            loss_ref[...] = part
            dfg_ref[...] = p_fg

        @pl.when(i != 0)
        def _():
            loss_ref[...] = loss_ref[...] + part
            dfg_ref[...] = dfg_ref[...] + p_fg

    row = pl.BlockSpec((tm, d), lambda i: (i, 0))
    return _call(
        body, name=name, grid=(m_rows // tm,),
        out_shape=[jax.ShapeDtypeStruct((m_rows, d), BF16), jax.ShapeDtypeStruct((1, 1), F32),
                   jax.ShapeDtypeStruct((m_rows, d), F32), jax.ShapeDtypeStruct((1, d), F32)],
        in_specs=[pl.BlockSpec((tm, di), lambda i: (i, 0)),
                  pl.BlockSpec((di, d), lambda i: (0, 0)),
                  row,
                  pl.BlockSpec((1, 1, 3 * d), lambda i: ((i * tm) // seq, 0, 0)),
                  pl.BlockSpec((1, d), lambda i: (0, 0)), row],
        out_specs=[row, pl.BlockSpec((1, 1), lambda i: (0, 0)), row, pl.BlockSpec((1, d), lambda i: (0, 0))],
        compiler_params=_params(),
    )(y, w, x, mod, fg, target)


def _pack(parts):
    flat = jnp.concatenate([p.reshape(-1) for p in parts])
    pad = (-flat.shape[0]) % (8 * LANES)
    return jnp.pad(flat, (0, pad)).reshape(-1, LANES)


def kernel(x, c, norm_gain, w_ada, b_ada, a_w_in, a_ln_gain, a_ln_bias, a_w_s, a_b_s, a_w_out, b_w_in, b_lower_bounds, b_gn_gain, b_w_out, final_gain, loss_target, m_norm_gain, m_w_ada, m_b_ada, m_a_w_in, m_a_ln_gain, m_a_ln_bias, m_a_w_s, m_a_b_s, m_a_w_out, m_b_w_in, m_b_lower_bounds, m_b_gn_gain, m_b_w_out, m_final_gain, v_norm_gain, v_w_ada, v_b_ada, v_a_w_in, v_a_ln_gain, v_a_ln_bias, v_a_w_s, v_a_b_s, v_a_w_out, v_b_w_in, v_b_lower_bounds, v_b_gn_gain, v_b_w_out, v_final_gain):
    nb, seq, d = x.shape
    m_rows = nb * seq
    n_l = w_ada.shape[0]
    ada_cols = w_ada.shape[2]
    px, py, pc = _place()
    chip = 2 * px + py
    dev = 2 * chip + pc

    c_all = allgather_small(c.reshape(-1, LANES), "gather_c").reshape(N_DEV * nb, d)
    s_a = halves_start([cast_into_slot(a_w_in[0], chip, c_all, "cast_a_in"),
                        cast_into_slot(a_w_out[0], chip, c_all, "cast_a_out")], "gather_a_start")
    land_b_in = cast_into_slot(b_w_in[0], chip, s_a[3], "cast_b_in")
    land_b_out = cast_into_slot(b_w_out[0], chip, land_b_in, "cast_b_out")
    b_cols = lax.dynamic_slice_in_dim(b_ada, chip * ada_cols, ada_cols, axis=1).reshape(n_l, 1, ada_cols)
    mod_cols = ada_fwd(c_all, w_ada, b_cols, land_b_out, "ada_fwd")
    mod_g = allgather_small(mod_cols.reshape(-1, LANES), "gather_mod")
    mod_g = mod_g.reshape(N_CHIPS, 2, n_l, N_DEV * nb, ada_cols)[:, 0]
    mod_all = jnp.transpose(mod_g, (1, 2, 0, 3)).reshape(n_l, N_DEV * nb, 3 * d)
    mod_mine = lax.dynamic_slice_in_dim(mod_all, dev * nb, nb, axis=1)
    mod0 = mod_mine[0].reshape(nb, 1, 3 * d)
    mod1 = mod_mine[1].reshape(nb, 1, 3 * d)

    landed_a = halves_wait(s_a[0], s_a[1], s_a[2], mod_mine, "gather_a_wait")
    s_bi = gather_start(land_b_in, landed_a[0], "gather_b_in_start")
    s_bo = gather_start(land_b_out, s_bi[3], "gather_b_out_start")
    wa_in, wa_out = pass_halves(list(landed_a), "gather_a_pass")
    di = a_w_out.shape[1] * N_CHIPS
    wa_out = wa_out.reshape(di, d)

    x0 = x.reshape(m_rows, d)
    tgt = loss_target.reshape(m_rows, d)
    ng0 = norm_gain[0:1] + (s_bi[3][0, 0] + s_bo[3][0, 0])
    ng1 = norm_gain[1:2]
    bs_col = a_b_s[0].reshape(SG_GROUPS, SG_BLOCK, 1)
    proj_a, h_a = inproj_fwd(x0, mod0, ng0, wa_in, seq, False, "a_inproj")
    y_a = sgu_fwd(proj_a, a_ln_gain, a_ln_bias, a_w_s[0], bs_col, "a_sgu")
    x1, out_a = outproj_fwd(y_a, wa_out, x0, mod0, seq, "a_outproj")
    wb_in = gather_wait(*s_bi[:3], out_a, "gather_b_in_wait")
    proj_b, h_b = inproj_fwd(x1, mod1, ng1, wb_in, seq, True, "b_inproj")
    y_b, sts_b = hgrn_fwd(proj_b, b_lower_bounds, b_gn_gain, seq, "b_hgrn")
    wb_out = gather_wait(*s_bo[:3], y_b, "gather_b_out_wait").reshape(di, d)
    out_b, loss_part, dx2, dfg = outproj_loss(
        y_b, wb_out, x1, mod1, final_gain.reshape(1, d), tgt, seq, "b_outproj_loss")

    shard_rows = di // N_CHIPS
    dy_b, dout_b, dgate1 = outproj_bwd(dx2, out_b, mod1, wb_out, seq, "b_outproj_bwd")
    gwb_out = grad_w_out(y_b, dout_b, "b_grad_w_out").reshape(N_CHIPS, shard_rows, d)
    e_bo = exchange_start(gwb_out, "exchange_b_out_start")
    dproj_b, dlb, dgn = hgrn_bwd(
        proj_b, dy_b, sts_b, b_lower_bounds, b_gn_gain + e_bo[4][0, 0], seq, "b_hgrn_bwd")
    e_bi = exchange_start(grad_w_in(h_b, dproj_b, N_CHIPS, True, "b_grad_w_in"), "exchange_b_in_start")
    dx1, dshift1, dscale1, dng1 = inproj_bwd(
        dproj_b, wb_in, x1, dx2, mod1, ng1 + e_bi[4][0, 0], seq, True, "b_inproj_bwd")

    dy_a, dout_a, dgate0 = outproj_bwd(dx1, out_a, mod0, wa_out, seq, "a_outproj_bwd")
    gwa_out = grad_w_out(y_a, dout_a, "a_grad_w_out").reshape(N_CHIPS, shard_rows, d)
    e_ao = exchange_start(gwa_out, "exchange_a_out_start")
    dproj_a, dws, dbs, dlg, dlbias = sgu_bwd(
        proj_a, dy_a, a_ln_gain + e_ao[4][0, 0], a_ln_bias, a_w_s[0], bs_col, "a_sgu_bwd")
    e_ai = exchange_start(grad_w_in(h_a, dproj_a, N_CHIPS, False, "a_grad_w_in"), "exchange_a_in_start")
    dx0, dshift0, dscale0, dng0 = inproj_bwd(
        dproj_a, wa_in, x0, dx1, mod0, norm_gain[0:1] + e_ai[4][0, 0], seq, False, "a_inproj_bwd")
    grad_x = dx0.reshape(nb, seq, d)

    dmod = jnp.concatenate([dshift0, dscale0, dgate0, dshift1, dscale1, dgate1], axis=2)
    n_dmod = dmod.size
    small_g = [jnp.concatenate([dng0, dng1], axis=0), dlg, dlbias, dws, dbs, dlb, dfg, dgn]
    packed_g = _pack([dmod] + small_g + [loss_part])
    rows = packed_g.shape[0]
    s_small = gather_all_start(
        lax.dynamic_update_slice(jnp.zeros((N_DEV, rows, LANES), F32), packed_g[None], (dev, 0, 0)),
        "gather_small_start")

    def finish(group, after):
        mine = []
        for ex, _, _, _, nm in group:
            parts_thru, land = exchange_wait(ex[0], ex[1], ex[2], ex[3], after, "exchange_" + nm + "_wait")
            mine.append(sum_parts(parts_thru, land, chip, "sum_" + nm))
            after = mine[-1]
        theirs = swap_sibling(mine, "swap_" + group[0][4])
        return [[r.reshape(w.shape) for r in adamw_pair(pa, pb, w[0], m[0], v[0], "adamw_" + nm)]
                for pa, pb, (_, w, m, v, nm) in zip(mine, theirs, group)]

    (gb_out, db_out, mb_out, vb_out), (gb_in, db_in, mb_in, vb_in), (ga_out, da_out, ma_out, va_out) = finish(
        [(e_bo, b_w_out, m_b_w_out, v_b_w_out, "b_out"), (e_bi, b_w_in, m_b_w_in, v_b_w_in, "b_in"),
         (e_ao, a_w_out, m_a_w_out, v_a_w_out, "a_out")], s_small[3])
    ((ga_in, da_in, ma_in, va_in),) = finish([(e_ai, a_w_in, m_a_w_in, v_a_w_in, "a_in")], ga_out)

    small_w = [norm_gain, a_ln_gain, a_ln_bias, a_w_s, a_b_s, b_lower_bounds, final_gain, b_gn_gain]
    small_m = [m_norm_gain, m_a_ln_gain, m_a_ln_bias, m_a_w_s, m_a_b_s, m_b_lower_bounds, m_final_gain, m_b_gn_gain]
    small_v = [v_norm_gain, v_a_ln_gain, v_a_ln_bias, v_a_w_s, v_a_b_s, v_b_lower_bounds, v_final_gain, v_b_gn_gain]
    rows_of = lambda a: a.reshape(-1, a.shape[-1])
    gathered = gather_all_wait(s_small[0], s_small[1], s_small[2], ga_in, "gather_small_wait")
    tail, small_res = small_update(
        gathered, n_dmod // LANES, [rows_of(a) for a in small_w], [rows_of(a) for a in small_m],
        [rows_of(a) for a in small_v], "small_update")
    loss = tail[0, 0]
    sg, sd, sm, sv = [[small_res[p][kind].reshape(w.shape) for p, w in enumerate(small_w)] for kind in range(4)]

    dmod_all = gathered[:, :n_dmod // LANES].reshape(N_DEV * nb, n_l, 3 * d)
    dmod_cols = lax.dynamic_slice_in_dim(dmod_all, chip * ada_cols, ada_cols, axis=2)
    dmod_cols = jnp.transpose(dmod_cols, (1, 0, 2))
    g_wada, d_wada, m_wada, v_wada = ada_bwd(c_all, dmod_cols, w_ada, m_w_ada, v_w_ada, "ada_bwd")
    flat = lambda a: a.reshape(1, -1)
    g_bada, d_bada, m_bada, v_bada = [
        r.reshape(b_ada.shape) for r in
        bias_update(dmod_all.reshape(N_DEV * nb, n_l * 3 * d), flat(b_ada), flat(m_b_ada), flat(v_b_ada), "bias_update")]

    def order(ng, wada, bada, ain, sm_rest, aout, bin_, bout):
        lg, lbi, ws_, bs_, lbd, fg_, gn_ = sm_rest
        return [ng, wada, bada, ain, lg, lbi, ws_, bs_, aout, bin_, lbd, gn_, bout, fg_]

    grads = order(sg[0], g_wada, g_bada, ga_in, sg[1:8], ga_out, gb_in, gb_out)
    deltas = order(sd[0], d_wada, d_bada, da_in, sd[1:8], da_out, db_in, db_out)
    new_m = order(sm[0], m_wada, m_bada, ma_in, sm[1:8], ma_out, mb_in, mb_out)
    new_v = order(sv[0], v_wada, v_bada, va_in, sv[1:8], va_out, vb_in, vb_out)
    return (loss, grad_x, *grads, *deltas, *new_m, *new_v)
```

```python
import jax
import jax.numpy as jnp
from jax import lax
from jax.experimental import pallas as pl
from jax.experimental.pallas import tpu as pltpu

F32 = jnp.float32
BF16 = jnp.bfloat16
EPS = 1e-6
CHUNK = 64
SG_BLOCK = 128
SG_GROUPS = 8
SG_STEP_BLOCKS = 2
HEAD_DIM = 128
HG_WIDE = 8
HG_ROWS = 256
N_CHIPS = 4
N_DEV = 8
LANES = 128
ADAM_LR = 0.001
ADAM_B1 = 0.9
ADAM_B2 = 0.999
ADAM_EPS = 1e-08
ADAM_WD = 0.01
ADAM_STEP = 10
GELU_C0 = 0.7978845608028654
GELU_C1 = 0.044715
MESH = pl.DeviceIdType.MESH
VMEM_LIMIT = 56 * 1024 * 1024


ROW_TILE = 1024


def _col_tile(n):
    return next(t for t in (1024, 768, 512, 256) if n % t == 0)


def _call(body, **kw):
    return pl.pallas_call(body, **kw)


def _params(**kw):
    return pltpu.CompilerParams(vmem_limit_bytes=VMEM_LIMIT, **kw)


def _sigmoid(x):
    return 0.5 * jnp.tanh(0.5 * x) + 0.5


def _sigmoid_small(x):
    return 1.0 / (1.0 + jnp.exp(-x))


def _silu_and_grad(x):
    s = _sigmoid(x)
    return x * s, s * (1.0 + x * (1.0 - s))


def _gelu(x):
    return 0.5 * x * (1.0 + jnp.tanh(GELU_C0 * (x + GELU_C1 * x * x * x)))


def _gelu_and_grad(x):
    t = jnp.tanh(GELU_C0 * (x + GELU_C1 * x * x * x))
    g = 0.5 * x * (1.0 + t)
    dg = 0.5 * (1.0 + t) + 0.5 * x * (1.0 - t * t) * (GELU_C0 * (1.0 + 3.0 * GELU_C1 * x * x))
    return g, dg


def _dot(a, b, dims, precision=None):
    return lax.dot_general(a, b, (dims, ((), ())), precision=precision, preferred_element_type=F32)


NN = ((1,), (0,))
NT = ((1,), (1,))
TN = ((0,), (0,))


def _adamw(w, g, m, v):
    m = ADAM_B1 * m + (1.0 - ADAM_B1) * g
    v = ADAM_B2 * v + (1.0 - ADAM_B2) * (g * g)
    m_hat = m / (1.0 - ADAM_B1 ** ADAM_STEP)
    v_hat = v / (1.0 - ADAM_B2 ** ADAM_STEP)
    delta = -ADAM_LR * (m_hat / (jnp.sqrt(v_hat) + ADAM_EPS) + ADAM_WD * w)
    return delta, m, v


def _chunk_mask():
    r = lax.broadcasted_iota(jnp.int32, (SG_BLOCK, SG_BLOCK), 0)
    c = lax.broadcasted_iota(jnp.int32, (SG_BLOCK, SG_BLOCK), 1)
    return (c // CHUNK) <= (r // CHUNK)


def _place():
    return lax.axis_index("x"), lax.axis_index("y"), lax.axis_index("c")


def _other_chips(x, y):
    return [(1 - x, y), (x, 1 - y), (1 - x, 1 - y)]


def allgather_small(v, name):
    m_per, n = v.shape

    def body(x_ref, out_ref, send_sems, recv_sems, local_sem):
        x, y, c = _place()
        me, sibling = (x, y, c), (x, y, 1 - c)
        chips = _other_chips(x, y)

        def rows(px, py, pc):
            return out_ref.at[pl.ds((4 * px + 2 * py + pc) * m_per, m_per), :]

        def copy(k, block, to, src=None):
            return pltpu.make_async_remote_copy(
                src_ref=rows(*block) if src is None else src, dst_ref=rows(*block),
                send_sem=send_sems.at[k], recv_sem=recv_sems.at[k], device_id=to, device_id_type=MESH)

        mine = pltpu.make_async_copy(x_ref, rows(*me), local_sem)
        mine.start()
        first = [copy(0, me, sibling, src=x_ref)]
        first += [copy(1 + j, me, (*chip, c), src=x_ref) for j, chip in enumerate(chips)]
        for cp in first:
            cp.start()
        passed = [copy(4 + j, (*chip, c), sibling) for j, chip in enumerate(chips)]
        for j, chip in enumerate(chips):
            copy(1 + j, (*chip, c), me).wait_recv()
            passed[j].start()
        copy(0, sibling, me).wait_recv()
        for j, chip in enumerate(chips):
            copy(4 + j, (*chip, 1 - c), me).wait_recv()
        for cp in first + passed:
            cp.wait_send()
        mine.wait()

    return _call(
        body, name=name,
        out_shape=jax.ShapeDtypeStruct((N_DEV * m_per, n), v.dtype),
        in_specs=[pl.BlockSpec(memory_space=pltpu.VMEM)],
        out_specs=pl.BlockSpec(memory_space=pltpu.VMEM),
        scratch_shapes=[pltpu.SemaphoreType.DMA((7,)), pltpu.SemaphoreType.DMA((7,)), pltpu.SemaphoreType.DMA],
    )(v)


def _hbm_spec():
    return pl.BlockSpec(memory_space=pltpu.HBM)


def _sem_spec():
    return pl.BlockSpec(memory_space=pltpu.SEMAPHORE)


def _split_params():
    return pltpu.CompilerParams(has_side_effects=pltpu.SideEffectType.DATAFLOW_SIDE_EFFECTING)


def _hbm(a):
    return pltpu.with_memory_space_constraint(a, pltpu.HBM)


def _half_copy(land_ref, rows, chip_idx, core_half, send_sem, recv_sem, to):
    half = land_ref.at[chip_idx, pl.ds(core_half * (rows // 2), rows // 2), :]
    return pltpu.make_async_remote_copy(
        src_ref=half, dst_ref=half, send_sem=send_sem, recv_sem=recv_sem, device_id=to, device_id_type=MESH)


def halves_start(lands, name):
    n = len(lands)

    def body(*refs):
        land_refs, send_sems, recv_sems, token = refs[:n], refs[n], refs[n + 1], refs[-1]
        x, y, c = _place()
        for w in range(n):
            for j, (px, py) in enumerate(_other_chips(x, y)):
                _half_copy(land_refs[w], lands[w].shape[1], 2 * x + y, c,
                           send_sems.at[3 * w + j], recv_sems.at[3 * w + j], (px, py, c)).start()
        token[...] = jnp.zeros_like(token)

    res = _call(
        body, name=name,
        out_shape=(pltpu.SemaphoreType.DMA((3 * n,)), pltpu.SemaphoreType.DMA((3 * n,)),
                   *[pltpu.HBM(a.shape, a.dtype) for a in lands], jax.ShapeDtypeStruct((8, LANES), F32)),
        in_specs=(_hbm_spec(),) * n,
        out_specs=(_sem_spec(), _sem_spec(), *[_hbm_spec()] * n, pl.BlockSpec(memory_space=pltpu.VMEM)),
        input_output_aliases={w: 2 + w for w in range(n)}, compiler_params=_split_params(),
    )(*[_hbm(a) for a in lands])
    return res[0], res[1], list(res[2:2 + n]), res[2 + n]


def halves_wait(send_sems, recv_sems, lands, after, name):
    n = len(lands)

    def body(*refs):
        land_refs, send_sems, recv_sems = refs[:n], refs[n], refs[n + 1]
        x, y, c = _place()
        for w in range(n):
            for j, (px, py) in enumerate(_other_chips(x, y)):
                cp = _half_copy(land_refs[w], lands[w].shape[1], 2 * px + py, c,
                                send_sems.at[3 * w + j], recv_sems.at[3 * w + j], (px, py, c))
                cp.wait_send()
                cp.wait_recv()

    return _call(
        body, name=name,
        out_shape=tuple(pltpu.HBM(a.shape, a.dtype) for a in lands),
        in_specs=(*[_hbm_spec()] * n, _sem_spec(), _sem_spec(), pl.BlockSpec(memory_space=pl.ANY)),
        out_specs=tuple(_hbm_spec() for _ in lands), input_output_aliases={w: w for w in range(n)},
        compiler_params=_split_params(),
    )(*lands, send_sems, recv_sems, after)


def pass_halves(lands, name):
    n = len(lands)

    def body(*refs):
        land_refs, send_sems, recv_sems = refs[n:2 * n], refs[2 * n], refs[2 * n + 1]
        x, y, c = _place()
        sent = []
        for w in range(n):
            for j, (px, py) in enumerate(_other_chips(x, y)):
                cp = _half_copy(land_refs[w], lands[w].shape[1], 2 * px + py, c,
                                send_sems.at[3 * w + j], recv_sems.at[3 * w + j], (x, y, 1 - c))
                cp.start()
                sent.append(cp)
        for w in range(n):
            for j, (px, py) in enumerate(_other_chips(x, y)):
                _half_copy(land_refs[w], lands[w].shape[1], 2 * px + py, 1 - c,
                           send_sems.at[3 * w + j], recv_sems.at[3 * w + j], (x, y, 1 - c)).wait_recv()
        for cp in sent:
            cp.wait_send()

    return _call(
        body, name=name,
        out_shape=[jax.ShapeDtypeStruct(a.shape, a.dtype) for a in lands],
        in_specs=[_hbm_spec()] * n, out_specs=[_hbm_spec()] * n,
        input_output_aliases={w: w for w in range(n)},
        scratch_shapes=[pltpu.SemaphoreType.DMA((3 * n,)), pltpu.SemaphoreType.DMA((3 * n,))],
    )(*lands)


def gather_start(land, after, name):
    def body(land_ref, after_ref, send_sems, recv_sems, land_thru, token):
        del after_ref, land_thru
        x, y, c = _place()
        for j, (px, py) in enumerate(_other_chips(x, y)):
            pltpu.make_async_remote_copy(
                src_ref=land_ref.at[2 * x + y], dst_ref=land_ref.at[2 * x + y],
                send_sem=send_sems.at[j], recv_sem=recv_sems.at[j], device_id=(px, py, c),
                device_id_type=MESH).start()
        token[...] = jnp.zeros_like(token)

    return _call(
        body, name=name,
        out_shape=(pltpu.SemaphoreType.DMA((3,)), pltpu.SemaphoreType.DMA((3,)),
                   pltpu.HBM(land.shape, land.dtype), jax.ShapeDtypeStruct((8, LANES), F32)),
        in_specs=(_hbm_spec(), pl.BlockSpec(memory_space=pl.ANY)),
        out_specs=(_sem_spec(), _sem_spec(), _hbm_spec(), pl.BlockSpec(memory_space=pltpu.VMEM)),
        input_output_aliases={0: 2}, compiler_params=_split_params(),
    )(_hbm(land), after)


def gather_wait(send_sems, recv_sems, land, after, name):
    def body(land_ref, send_sems, recv_sems, after_ref, land_out):
        del after_ref, land_out
        x, y, c = _place()
        for j, (px, py) in enumerate(_other_chips(x, y)):
            cp = pltpu.make_async_remote_copy(
                src_ref=land_ref.at[2 * x + y], dst_ref=land_ref.at[2 * px + py],
                send_sem=send_sems.at[j], recv_sem=recv_sems.at[j], device_id=(px, py, c), device_id_type=MESH)
            cp.wait_send()
            cp.wait_recv()

    return _call(
        body, name=name,
        out_shape=pltpu.HBM(land.shape, land.dtype),
        in_specs=(_hbm_spec(), _sem_spec(), _sem_spec(), pl.BlockSpec(memory_space=pl.ANY)),
        out_specs=_hbm_spec(), input_output_aliases={0: 0}, compiler_params=_split_params(),
    )(land, send_sems, recv_sems, after)


def _flips():
    return [(fx, fy, fc) for fx in (0, 1) for fy in (0, 1) for fc in (0, 1) if (fx, fy, fc) != (0, 0, 0)]


def _flipped(x, y, c, flip):
    fx, fy, fc = flip
    return (1 - x if fx else x, 1 - y if fy else y, 1 - c if fc else c)


def gather_all_start(land, name):
    def body(land_ref, send_sems, recv_sems, land_thru, token):
        del land_thru
        x, y, c = _place()
        for k, flip in enumerate(_flips()):
            pltpu.make_async_remote_copy(
                src_ref=land_ref.at[4 * x + 2 * y + c], dst_ref=land_ref.at[4 * x + 2 * y + c],
                send_sem=send_sems.at[k], recv_sem=recv_sems.at[k], device_id=_flipped(x, y, c, flip),
                device_id_type=MESH).start()
        token[...] = jnp.zeros_like(token)

    return _call(
        body, name=name,
        out_shape=(pltpu.SemaphoreType.DMA((7,)), pltpu.SemaphoreType.DMA((7,)),
                   pltpu.HBM(land.shape, land.dtype), jax.ShapeDtypeStruct((8, LANES), F32)),
        in_specs=(_hbm_spec(),),
        out_specs=(_sem_spec(), _sem_spec(), _hbm_spec(), pl.BlockSpec(memory_space=pltpu.VMEM)),
        input_output_aliases={0: 2}, compiler_params=_split_params(),
    )(_hbm(land))


def gather_all_wait(send_sems, recv_sems, land, after, name):
    def body(land_ref, send_sems, recv_sems, after_ref, land_out):
        del after_ref, land_out
        x, y, c = _place()
        for k, flip in enumerate(_flips()):
            px, py, pc = _flipped(x, y, c, flip)
            cp = pltpu.make_async_remote_copy(
                src_ref=land_ref.at[4 * x + 2 * y + c], dst_ref=land_ref.at[4 * px + 2 * py + pc],
                send_sem=send_sems.at[k], recv_sem=recv_sems.at[k], device_id=(px, py, pc), device_id_type=MESH)
            cp.wait_send()
            cp.wait_recv()

    return _call(
        body, name=name,
        out_shape=pltpu.HBM(land.shape, land.dtype),
        in_specs=(_hbm_spec(), _sem_spec(), _sem_spec(), pl.BlockSpec(memory_space=pl.ANY)),
        out_specs=_hbm_spec(), input_output_aliases={0: 0}, compiler_params=_split_params(),
    )(land, send_sems, recv_sems, after)


def exchange_start(parts, name):
    _, r, c_ = parts.shape

    def body(parts_ref, land_ref, send_sems, recv_sems, parts_thru, land_thru, token):
        del parts_thru, land_thru
        x, y, c = _place()
        for j, (px, py) in enumerate(_other_chips(x, y)):
            pltpu.make_async_remote_copy(
                src_ref=parts_ref.at[2 * px + py], dst_ref=land_ref.at[j],
                send_sem=send_sems.at[j], recv_sem=recv_sems.at[j], device_id=(px, py, c),
                device_id_type=MESH).start()
        token[...] = jnp.zeros_like(token)

    return _call(
        body, name=name,
        out_shape=(pltpu.SemaphoreType.DMA((3,)), pltpu.SemaphoreType.DMA((3,)),
                   pltpu.HBM(parts.shape, parts.dtype), pltpu.HBM((3, r, c_), parts.dtype),
                   jax.ShapeDtypeStruct((8, LANES), F32)),
        in_specs=(_hbm_spec(), _hbm_spec()),
        out_specs=(_sem_spec(), _sem_spec(), _hbm_spec(), _hbm_spec(), pl.BlockSpec(memory_space=pltpu.VMEM)),
        input_output_aliases={0: 2, 1: 3}, compiler_params=_split_params(),
    )(_hbm(parts), _hbm(lax.empty((3, r, c_), parts.dtype)))


def exchange_wait(send_sems, recv_sems, parts, land, after, name):
    def body(parts_ref, land_ref, send_sems, recv_sems, after_ref, parts_out, land_out):
        del after_ref, parts_out, land_out
        x, y, c = _place()
        for j, (px, py) in enumerate(_other_chips(x, y)):
            cp = pltpu.make_async_remote_copy(
                src_ref=parts_ref.at[2 * px + py], dst_ref=land_ref.at[j],
                send_sem=send_sems.at[j], recv_sem=recv_sems.at[j], device_id=(px, py, c), device_id_type=MESH)
            cp.wait_send()
            cp.wait_recv()

    return _call(
        body, name=name,
        out_shape=(pltpu.HBM(parts.shape, parts.dtype), pltpu.HBM(land.shape, land.dtype)),
        in_specs=(_hbm_spec(), _hbm_spec(), _sem_spec(), _sem_spec(), pl.BlockSpec(memory_space=pl.ANY)),
        out_specs=(_hbm_spec(), _hbm_spec()), input_output_aliases={0: 0, 1: 1},
        compiler_params=_split_params(),
    )(parts, land, send_sems, recv_sems, after)


def cast_into_slot(w, chip, after, name):
    r, c = w.shape
    tr = min(256, r)

    def body(s_ref, w_ref, after_ref, o_ref):
        del s_ref, after_ref
        o_ref[...] = w_ref[...].astype(BF16)

    return _call(
        body, name=name,
        grid_spec=pltpu.PrefetchScalarGridSpec(
            num_scalar_prefetch=1, grid=(r // tr,),
            in_specs=[pl.BlockSpec((tr, c), lambda i, s: (i, 0)), pl.BlockSpec(memory_space=pl.ANY)],
            out_specs=pl.BlockSpec((None, tr, c), lambda i, s: (s[0], i, 0))),
        out_shape=jax.ShapeDtypeStruct((N_CHIPS, r, c), BF16),
        compiler_params=_params(),
    )(chip.reshape(1).astype(jnp.int32), w, after)


def sum_parts(parts, land, chip, name):
    _, r, c = parts.shape
    tr = min(256, r)

    def body(s_ref, p_ref, l_ref, o_ref):
        del s_ref
        acc = p_ref[...].astype(F32) + l_ref[0].astype(F32)
        acc = acc + l_ref[1].astype(F32)
        o_ref[...] = (acc + l_ref[2].astype(F32)).astype(BF16)

    return _call(
        body, name=name,
        grid_spec=pltpu.PrefetchScalarGridSpec(
            num_scalar_prefetch=1, grid=(r // tr,),
            in_specs=[pl.BlockSpec((None, tr, c), lambda i, s: (s[0], i, 0)),
                      pl.BlockSpec((3, tr, c), lambda i, s: (0, i, 0))],
            out_specs=pl.BlockSpec((tr, c), lambda i, s: (i, 0))),
        out_shape=jax.ShapeDtypeStruct((r, c), BF16),
        compiler_params=_params(),
    )(chip.reshape(1).astype(jnp.int32), parts, land)


def swap_sibling(arrs, name):
    n = len(arrs)

    def body(*refs):
        ins, outs = refs[:n], refs[n:2 * n]
        send_sems, recv_sems = refs[2 * n:]
        x, y, c = _place()
        cps = []
        for w in range(n):
            cp = pltpu.make_async_remote_copy(
                src_ref=ins[w], dst_ref=outs[w], send_sem=send_sems.at[w], recv_sem=recv_sems.at[w],
                device_id=(x, y, 1 - c), device_id_type=MESH)
            cp.start()
            cps.append(cp)
        for cp in cps:
            cp.wait_recv()
        for cp in cps:
            cp.wait_send()

    return _call(
        body, name=name,
        out_shape=[jax.ShapeDtypeStruct(a.shape, a.dtype) for a in arrs],
        in_specs=[_hbm_spec()] * n, out_specs=[_hbm_spec()] * n,
        scratch_shapes=[pltpu.SemaphoreType.DMA((n,)), pltpu.SemaphoreType.DMA((n,))],
    )(*arrs)


def adamw_pair(pa, pb, w, m, v, name):
    r, c = w.shape
    tr = min(128, r)

    def body(pa_ref, pb_ref, w_ref, m_ref, v_ref, g_ref, d_ref, nm_ref, nv_ref):
        g = pa_ref[...].astype(F32) + pb_ref[...].astype(F32)
        d, nm, nv = _adamw(w_ref[...], g, m_ref[...], v_ref[...])
        g_ref[...] = g
        d_ref[...] = d
        nm_ref[...] = nm
        nv_ref[...] = nv

    spec = pl.BlockSpec((tr, c), lambda i: (i, 0))
    return _call(
        body, name=name, grid=(r // tr,),
        out_shape=[jax.ShapeDtypeStruct((r, c), F32)] * 4,
        in_specs=[spec] * 5, out_specs=[spec] * 4,
        compiler_params=_params(),
    )(pa, pb, w, m, v)


def small_update(gathered, first_row, ws, ms, vs, name):
    n_w = len(ws)
    total_rows = gathered.shape[1]

    def body(*refs):
        g_ref = refs[0]
        w_refs, m_refs, v_refs = refs[1:1 + n_w], refs[1 + n_w:1 + 2 * n_w], refs[1 + 2 * n_w:1 + 3 * n_w]
        tail_ref = refs[1 + 3 * n_w]
        outs = refs[2 + 3 * n_w:2 + 7 * n_w]
        sum_ref = refs[2 + 7 * n_w]
        acc = g_ref[0]
        for k in range(1, N_DEV):
            acc = acc + g_ref[k]
        sum_ref[...] = acc
        row = first_row
        for p in range(n_w):
            a, b = ws[p].shape
            per = b // LANES
            g_out, d_out, m_out, v_out = outs[4 * p:4 * p + 4]
            if per == 1:
                g_out[...] = sum_ref[row:row + a, :]
            else:
                for i in range(a):
                    for jc in range(per):
                        g_out[i:i + 1, jc * LANES:(jc + 1) * LANES] = sum_ref[row + i * per + jc:row + i * per + jc + 1, :]
            row += a * per
            dl, nm, nv = _adamw(w_refs[p][...], g_out[...], m_refs[p][...], v_refs[p][...])
            d_out[...] = dl
            m_out[...] = nm
            v_out[...] = nv
        tail_ref[...] = sum_ref[row:row + 1, :]

    out_shape = [jax.ShapeDtypeStruct((1, LANES), F32)]
    for w in ws:
        out_shape += [jax.ShapeDtypeStruct(w.shape, F32)] * 4
    res = _call(
        body, name=name, out_shape=out_shape,
        scratch_shapes=[pltpu.VMEM((total_rows, LANES), F32)],
        compiler_params=_params(),
    )(gathered, *ws, *ms, *vs)
    return res[0], [res[1 + 4 * p:5 + 4 * p] for p in range(n_w)]


def ada_fwd(c_all, w_ada, b_cols, after, name):
    n_l, d, cols = w_ada.shape
    nb = c_all.shape[0]
    tn = 256

    def body(c_ref, w_ref, b_ref, after_ref, o_ref):
        del after_ref
        cv = c_ref[...]
        ca = (cv * _sigmoid(cv)).astype(BF16)
        o_ref[...] = _dot(ca, w_ref[...].astype(BF16), NN) + b_ref[...]

    return _call(
        body, name=name, grid=(n_l, cols // tn),
        out_shape=jax.ShapeDtypeStruct((n_l, nb, cols), F32),
        in_specs=[pl.BlockSpec((nb, d), lambda l, j: (0, 0)),
                  pl.BlockSpec((None, d, tn), lambda l, j: (l, 0, j)),
                  pl.BlockSpec((None, 1, tn), lambda l, j: (l, 0, j)),
                  pl.BlockSpec(memory_space=pl.ANY)],
        out_specs=pl.BlockSpec((None, nb, tn), lambda l, j: (l, 0, j)),
        compiler_params=_params(),
    )(c_all, w_ada, b_cols, after)


def ada_bwd(c_all, dmod_cols, w, m, v, name):
    n_l, d, cols = w.shape
    nb = c_all.shape[0]
    tn = 256

    def body(c_ref, dm_ref, w_ref, m_ref, v_ref, g_ref, d_ref, nm_ref, nv_ref):
        cv = c_ref[...]
        ca = (cv * _sigmoid(cv)).astype(BF16)
        g = _dot(ca, dm_ref[...].astype(BF16), TN)
        dl, nm, nv = _adamw(w_ref[...], g, m_ref[...], v_ref[...])
        g_ref[...] = g
        d_ref[...] = dl
        nm_ref[...] = nm
        nv_ref[...] = nv

    wspec = pl.BlockSpec((None, d, tn), lambda l, j: (l, 0, j))
    return _call(
        body, name=name, grid=(n_l, cols // tn),
        out_shape=[jax.ShapeDtypeStruct((n_l, d, cols), F32)] * 4,
        in_specs=[pl.BlockSpec((nb, d), lambda l, j: (0, 0)),
                  pl.BlockSpec((None, nb, tn), lambda l, j: (l, 0, j)),
                  wspec, wspec, wspec],
        out_specs=[wspec] * 4,
        compiler_params=_params(),
    )(c_all, dmod_cols, w, m, v)


def bias_update(dmod_all, w, m, v, name):
    def body(dm_ref, w_ref, m_ref, v_ref, g_ref, d_ref, nm_ref, nv_ref):
        g = jnp.sum(dm_ref[...], axis=0, keepdims=True)
        dl, nm, nv = _adamw(w_ref[...], g, m_ref[...], v_ref[...])
        g_ref[...] = g
        d_ref[...] = dl
        nm_ref[...] = nm
        nv_ref[...] = nv

    return _call(
        body, name=name,
        out_shape=[jax.ShapeDtypeStruct(w.shape, F32)] * 4,
        compiler_params=_params(),
    )(dmod_all, w, m, v)


def inproj_fwd(x, mod, ng, wg, seq, sectioned, name):
    m_rows, d = x.shape
    nsh, _, ns = wg.shape
    n = nsh * ns
    tm, tn = min(2 * ROW_TILE, seq), _col_tile(ns)
    per = ns // tn

    def body(x_ref, mod_ref, ng_ref, w_ref, proj_ref, h_ref):
        @pl.when(pl.program_id(1) == 0)
        def _():
            xv = x_ref[...]
            r = lax.rsqrt(jnp.mean(xv * xv, axis=-1, keepdims=True) + EPS)
            md = mod_ref[0]
            h = (xv * r * ng_ref[...]) * (1.0 + md[:, d:2 * d]) + md[:, :d]
            h_ref[...] = h.astype(BF16)
        proj_ref[...] = _dot(h_ref[...], w_ref[...], NN)

    if sectioned:
        proj_shape = (nsh, m_rows, ns)
        proj_spec = pl.BlockSpec((None, tm, tn), lambda i, j: (j // per, i, j % per))
    else:
        proj_shape = (m_rows, n)
        proj_spec = pl.BlockSpec((tm, tn), lambda i, j: (i, j))
    return _call(
        body, name=name, grid=(m_rows // tm, n // tn),
        out_shape=[jax.ShapeDtypeStruct(proj_shape, F32), jax.ShapeDtypeStruct((m_rows, d), BF16)],
        in_specs=[pl.BlockSpec((tm, d), lambda i, j: (i, 0)),
                  pl.BlockSpec((1, 1, 3 * d), lambda i, j: ((i * tm) // seq, 0, 0)),
                  pl.BlockSpec((1, d), lambda i, j: (0, 0)),
                  pl.BlockSpec((None, d, tn), lambda i, j: (j // per, 0, j % per))],
        out_specs=[proj_spec, pl.BlockSpec((tm, d), lambda i, j: (i, 0))],
        compiler_params=_params(),
    )(x, mod, ng, wg)


def outproj_fwd(y, w, x, mod, seq, name):
    m_rows, di = y.shape
    d = w.shape[1]
    tm = min(ROW_TILE, seq)

    def body(y_ref, w_ref, x_ref, mod_ref, xn_ref, out_ref):
        acc = _dot(y_ref[...], w_ref[...], NN)
        out_ref[...] = acc.astype(BF16)
        xn_ref[...] = x_ref[...] + mod_ref[0][:, 2 * d:] * acc

    row = pl.BlockSpec((tm, d), lambda i: (i, 0))
    return _call(
        body, name=name, grid=(m_rows // tm,),
        out_shape=[jax.ShapeDtypeStruct((m_rows, d), F32), jax.ShapeDtypeStruct((m_rows, d), BF16)],
        in_specs=[pl.BlockSpec((tm, di), lambda i: (i, 0)),
                  pl.BlockSpec((di, d), lambda i: (0, 0)),
                  row,
                  pl.BlockSpec((1, 1, 3 * d), lambda i: ((i * tm) // seq, 0, 0))],
        out_specs=[row, row],
        compiler_params=_params(),
    )(y, w, x, mod)


def outproj_bwd(dxo, out, mod, w, seq, name):
    m_rows, d = dxo.shape
    di = w.shape[0]
    nb = m_rows // seq
    tm, tn = min(ROW_TILE, seq), di

    def body(dxo_ref, out_ref, mod_ref, w_ref, dy_ref, dout_ref, dgate_ref):
        i = pl.program_id(0)

        @pl.when(pl.program_id(1) == 0)
        def _():
            dx = dxo_ref[...]
            dout_ref[...] = (mod_ref[0][:, 2 * d:] * dx).astype(BF16)
            part = jnp.sum(dx * out_ref[...].astype(F32), axis=0, keepdims=True)

            @pl.when((i * tm) % seq == 0)
            def _():
                dgate_ref[0] = part

            @pl.when((i * tm) % seq != 0)
            def _():
                dgate_ref[0] = dgate_ref[0] + part

        dy_ref[...] = _dot(dout_ref[...], w_ref[...], NT).astype(BF16)

    row = pl.BlockSpec((tm, d), lambda i, j: (i, 0))
    return _call(
        body, name=name, grid=(m_rows // tm, di // tn),
        out_shape=[jax.ShapeDtypeStruct((m_rows, di), BF16), jax.ShapeDtypeStruct((m_rows, d), BF16),
                   jax.ShapeDtypeStruct((nb, 1, d), F32)],
        in_specs=[row, row,
                  pl.BlockSpec((1, 1, 3 * d), lambda i, j: ((i * tm) // seq, 0, 0)),
                  pl.BlockSpec((tn, d), lambda i, j: (j, 0))],
        out_specs=[pl.BlockSpec((tm, tn), lambda i, j: (i, j)), row,
                   pl.BlockSpec((1, 1, d), lambda i, j: ((i * tm) // seq, 0, 0))],
        compiler_params=_params(),
    )(dxo, out, mod, w)


def grad_w_out(y, dout, name):
    m_rows, di = y.shape
    d = dout.shape[1]
    tm, tk = min(ROW_TILE, m_rows), di
    n_m = m_rows // tm

    def body(y_ref, do_ref, o_ref, acc_ref):
        mi = pl.program_id(1)

        @pl.when(mi == 0)
        def _():
            acc_ref[...] = jnp.zeros_like(acc_ref)

        acc_ref[...] += _dot(y_ref[...], do_ref[...], TN)

        @pl.when(mi == n_m - 1)
        def _():
            o_ref[...] = acc_ref[...].astype(BF16)

    return _call(
        body, name=name, grid=(di // tk, n_m),
        out_shape=jax.ShapeDtypeStruct((di, d), BF16),
        in_specs=[pl.BlockSpec((tm, tk), lambda j, mi: (mi, j)),
                  pl.BlockSpec((tm, d), lambda j, mi: (mi, 0))],
        out_specs=pl.BlockSpec((tk, d), lambda j, mi: (j, 0)),
        scratch_shapes=[pltpu.VMEM((tk, d), F32)],
        compiler_params=_params(),
    )(y, dout)


def grad_w_in(h, dproj, nsh, sectioned, name):
    m_rows, d = h.shape
    n = dproj.shape[0] * dproj.shape[2] if sectioned else dproj.shape[1]
    ns = n // nsh
    tm, tn = min(ROW_TILE, m_rows), ns
    per = ns // tn
    n_m = m_rows // tm

    def body(h_ref, dp_ref, o_ref, acc_ref):
        mi = pl.program_id(1)
        @pl.when(mi == 0)
        def _():
            acc_ref[...] = jnp.zeros_like(acc_ref)

        acc_ref[...] += _dot(h_ref[...], dp_ref[...], TN)

        @pl.when(mi == n_m - 1)
        def _():
            o_ref[...] = acc_ref[...].astype(BF16)

    if sectioned:
        dp_spec = pl.BlockSpec((None, tm, tn), lambda j, mi: (j // per, mi, j % per))
    else:
        dp_spec = pl.BlockSpec((tm, tn), lambda j, mi: (mi, j))
    return _call(
        body, name=name, grid=(n // tn, n_m),
        out_shape=jax.ShapeDtypeStruct((nsh, d, ns), BF16),
        in_specs=[pl.BlockSpec((tm, d), lambda j, mi: (mi, 0)), dp_spec],
        out_specs=pl.BlockSpec((None, d, tn), lambda j, mi: (j // per, 0, j % per)),
        scratch_shapes=[pltpu.VMEM((d, tn), F32)],
        compiler_params=_params(),
    )(h, dproj)


def inproj_bwd(dproj, wg, x, dxo, mod, ng, seq, sectioned, name):
    m_rows, d = x.shape
    nsh, _, ns = wg.shape
    n = nsh * ns
    nb = m_rows // seq
    tm, tk = min(ROW_TILE, seq), ns
    per = ns // tk
    n_k = n // tk

    def body(dp_ref, w_ref, x_ref, dxo_ref, mod_ref, ng_ref, dxi_ref, dsh_ref, dsc_ref, dng_ref, acc_ref):
        i, k = pl.program_id(0), pl.program_id(1)
        @pl.when(k == 0)
        def _():
            acc_ref[...] = jnp.zeros_like(acc_ref)

        acc_ref[...] += _dot(dp_ref[...], w_ref[...], NT)

        @pl.when(k == n_k - 1)
        def _():
            dh = acc_ref[...]
            xv = x_ref[...]
            r = lax.rsqrt(jnp.mean(xv * xv, axis=-1, keepdims=True) + EPS)
            xn = xv * r
            md = mod_ref[0]
            gain = ng_ref[...]
            p_shift = jnp.sum(dh, axis=0, keepdims=True)
            p_scale = jnp.sum(dh * (xn * gain), axis=0, keepdims=True)
            drn = dh * (1.0 + md[:, d:2 * d])
            p_ng = jnp.sum(drn * xn, axis=0, keepdims=True)
            dxn = drn * gain
            dx = r * (dxn - xn * jnp.mean(dxn * xn, axis=-1, keepdims=True))
            dxi_ref[...] = dxo_ref[...] + dx

            @pl.when((i * tm) % seq == 0)
            def _():
                dsh_ref[0] = p_shift
                dsc_ref[0] = p_scale

            @pl.when((i * tm) % seq != 0)
            def _():
                dsh_ref[0] = dsh_ref[0] + p_shift
                dsc_ref[0] = dsc_ref[0] + p_scale

            @pl.when(i == 0)
            def _():
                dng_ref[...] = p_ng

            @pl.when(i != 0)
            def _():
                dng_ref[...] = dng_ref[...] + p_ng

    if sectioned:
        dp_spec = pl.BlockSpec((None, tm, tk), lambda i, k: (k // per, i, k % per))
    else:
        dp_spec = pl.BlockSpec((tm, tk), lambda i, k: (i, k))
    row = pl.BlockSpec((tm, d), lambda i, k: (i, 0))
    per_seq = pl.BlockSpec((1, 1, d), lambda i, k: ((i * tm) // seq, 0, 0))
    return _call(
        body, name=name, grid=(m_rows // tm, n_k),
        out_shape=[jax.ShapeDtypeStruct((m_rows, d), F32), jax.ShapeDtypeStruct((nb, 1, d), F32),
                   jax.ShapeDtypeStruct((nb, 1, d), F32), jax.ShapeDtypeStruct((1, d), F32)],
        in_specs=[dp_spec,
                  pl.BlockSpec((None, d, tk), lambda i, k: (k // per, 0, k % per)),
                  row, row,
                  pl.BlockSpec((1, 1, 3 * d), lambda i, k: ((i * tm) // seq, 0, 0)),
                  pl.BlockSpec((1, d), lambda i, k: (0, 0))],
        out_specs=[row, per_seq, per_seq, pl.BlockSpec((1, d), lambda i, k: (0, 0))],
        scratch_shapes=[pltpu.VMEM((tm, d), F32)],
        compiler_params=_params(),
    )(dproj, wg, x, dxo, mod, ng)


def _sgu_stats(proj_ref, vg_ref, di, gd, dgel_ref=None):
    s1 = jnp.zeros((SG_BLOCK, 1), F32)
    for g in range(SG_GROUPS):
        v_pre = proj_ref[:, di + g * gd:di + (g + 1) * gd]
        if dgel_ref is None:
            vg = _gelu(v_pre)
        else:
            vg, dgel_ref[:, g * gd:(g + 1) * gd] = _gelu_and_grad(v_pre)
        vg_ref[:, g * gd:(g + 1) * gd] = vg
        s1 = s1 + jnp.sum(vg, axis=1, keepdims=True)
    mu = s1 / di
    s2 = jnp.zeros((SG_BLOCK, 1), F32)
    for g in range(SG_GROUPS):
        dv = vg_ref[:, g * gd:(g + 1) * gd] - mu
        s2 = s2 + jnp.sum(dv * dv, axis=1, keepdims=True)
    return mu, lax.rsqrt(s2 / di + EPS)


def sgu_fwd(proj, ln_gain, ln_bias, ws, bs, name):
    m_rows, n3 = proj.shape
    di = n3 // 3
    gd = di // SG_GROUPS
    n_blocks = m_rows // SG_BLOCK
    per_step = SG_STEP_BLOCKS if n_blocks % SG_STEP_BLOCKS == 0 else 1

    def body(proj_ref, lg_ref, lb_ref, ws_ref, bs_ref, y_ref, wsm_ref, vg_ref):
        @pl.when(pl.program_id(0) == 0)
        def _():
            mask = _chunk_mask()
            for g in range(SG_GROUPS):
                wsm_ref[g] = jnp.where(mask, ws_ref[g], 0.0).astype(BF16)

        for blk in range(per_step):
            p_ref, o_ref = proj_ref.at[blk], y_ref.at[blk]
            mu, rstd = _sgu_stats(p_ref, vg_ref, di, gd)
            for g in range(SG_GROUPS):
                cs = slice(g * gd, (g + 1) * gd)
                vln = (vg_ref[:, cs] - mu) * rstd * lg_ref[:, cs] + lb_ref[:, cs]
                s = _dot(wsm_ref[g], vln.astype(BF16), NN) + bs_ref[g]
                u = _gelu(p_ref[:, cs])
                gp = p_ref[:, 2 * di + g * gd:2 * di + (g + 1) * gd]
                o_ref[:, cs] = (u * s * (gp * _sigmoid(gp))).astype(BF16)

    full = lambda shape: pl.BlockSpec(shape, lambda i: (0,) * len(shape))
    return _call(
        body, name=name, grid=(n_blocks // per_step,),
        out_shape=jax.ShapeDtypeStruct((n_blocks, SG_BLOCK, di), BF16),
        in_specs=[pl.BlockSpec((per_step, SG_BLOCK, n3), lambda i: (i, 0, 0)),
                  full((1, di)), full((1, di)),
                  full((SG_GROUPS, SG_BLOCK, SG_BLOCK)), full((SG_GROUPS, SG_BLOCK, 1))],
        out_specs=pl.BlockSpec((per_step, SG_BLOCK, di), lambda i: (i, 0, 0)),
        scratch_shapes=[pltpu.VMEM((SG_GROUPS, SG_BLOCK, SG_BLOCK), BF16), pltpu.VMEM((SG_BLOCK, di), F32)],
        compiler_params=_params(),
    )(proj.reshape(n_blocks, SG_BLOCK, n3), ln_gain, ln_bias, ws, bs).reshape(m_rows, di)


def sgu_bwd(proj, dy, ln_gain, ln_bias, ws, bs, name):
    m_rows, n3 = proj.shape
    di = n3 // 3
    gd = di // SG_GROUPS
    n_i = m_rows // SG_BLOCK

    def body(proj_ref, dy_ref, lg_ref, lb_ref, ws_ref, bs_ref,
             dp_ref, dws_ref, dbs_ref, dlg_ref, dlb_ref, wsm_ref, vg_ref, dvh_ref, dgel_ref):
        i = pl.program_id(0)

        def before():
            @pl.when(i == 0)
            def _():
                mask = _chunk_mask()
                for g in range(SG_GROUPS):
                    wsm_ref[g] = jnp.where(mask, ws_ref[g], 0.0).astype(BF16)
                dws_ref[...] = jnp.zeros_like(dws_ref)
                dbs_ref[...] = jnp.zeros_like(dbs_ref)
                dlg_ref[...] = jnp.zeros_like(dlg_ref)
                dlb_ref[...] = jnp.zeros_like(dlb_ref)

        def after():
            @pl.when(i == n_i - 1)
            def _():
                mask = _chunk_mask()
                for g in range(SG_GROUPS):
                    dws_ref[g] = jnp.where(mask, dws_ref[g], 0.0)

        before()
        mu, rstd = _sgu_stats(proj_ref, vg_ref, di, gd, dgel_ref)
        m1 = jnp.zeros((SG_BLOCK, 1), F32)
        m2 = jnp.zeros((SG_BLOCK, 1), F32)
        for g in range(SG_GROUPS):
            cs = slice(g * gd, (g + 1) * gd)
            gs = slice(2 * di + g * gd, 2 * di + (g + 1) * gd)
            gain = lg_ref[:, cs]
            vhat = (vg_ref[:, cs] - mu) * rstd
            vln_b = (vhat * gain + lb_ref[:, cs]).astype(BF16)
            s = _dot(wsm_ref[g], vln_b, NN) + bs_ref[g]
            u, du = _gelu_and_grad(proj_ref[:, cs])
            sg, dsg = _silu_and_grad(proj_ref[:, gs])
            dyv = dy_ref[:, cs].astype(F32)
            dp_ref[:, cs] = (dyv * s * sg * du).astype(BF16)
            dp_ref[:, gs] = (dyv * u * s * dsg).astype(BF16)
            ds = dyv * u * sg
            ds_b = ds.astype(BF16)
            dws_ref[g] = dws_ref[g] + _dot(ds_b, vln_b, NT)
            dbs_ref[g] = dbs_ref[g] + jnp.sum(ds, axis=1, keepdims=True)
            dvln = _dot(wsm_ref[g], ds_b, TN)
            dlg_ref[:, cs] = dlg_ref[:, cs] + jnp.sum(dvln * vhat, axis=0, keepdims=True)
            dlb_ref[:, cs] = dlb_ref[:, cs] + jnp.sum(dvln, axis=0, keepdims=True)
            dvh = dvln * gain
            dvh_ref[:, cs] = dvh
            m1 = m1 + jnp.sum(dvh, axis=1, keepdims=True)
            m2 = m2 + jnp.sum(dvh * vhat, axis=1, keepdims=True)
        m1 = m1 / di
        m2 = m2 / di
        for g in range(SG_GROUPS):
            cs = slice(g * gd, (g + 1) * gd)
            vs = slice(di + g * gd, di + (g + 1) * gd)
            vhat = (vg_ref[:, cs] - mu) * rstd
            dvg = rstd * (dvh_ref[:, cs] - m1 - vhat * m2)
            dp_ref[:, vs] = (dvg * dgel_ref[:, cs]).astype(BF16)

        after()

    full = lambda shape: pl.BlockSpec(shape, lambda i: (0,) * len(shape))
    return _call(
        body, name=name, grid=(n_i,),
        out_shape=[jax.ShapeDtypeStruct((m_rows, n3), BF16),
                   jax.ShapeDtypeStruct((SG_GROUPS, SG_BLOCK, SG_BLOCK), F32),
                   jax.ShapeDtypeStruct((SG_GROUPS, SG_BLOCK, 1), F32),
                   jax.ShapeDtypeStruct((1, di), F32), jax.ShapeDtypeStruct((1, di), F32)],
        in_specs=[pl.BlockSpec((SG_BLOCK, n3), lambda i: (i, 0)),
                  pl.BlockSpec((SG_BLOCK, di), lambda i: (i, 0)),
                  full((1, di)), full((1, di)),
                  full((SG_GROUPS, SG_BLOCK, SG_BLOCK)), full((SG_GROUPS, SG_BLOCK, 1))],
        out_specs=[pl.BlockSpec((SG_BLOCK, n3), lambda i: (i, 0)),
                   full((SG_GROUPS, SG_BLOCK, SG_BLOCK)), full((SG_GROUPS, SG_BLOCK, 1)),
                   full((1, di)), full((1, di))],
        scratch_shapes=[pltpu.VMEM((SG_GROUPS, SG_BLOCK, SG_BLOCK), BF16),
                        pltpu.VMEM((SG_BLOCK, di), F32), pltpu.VMEM((SG_BLOCK, di), F32),
                        pltpu.VMEM((SG_BLOCK, di), F32)],
        compiler_params=_params(),
    )(proj, dy, ln_gain, ln_bias, ws, bs)


def _lower_bound(lbraw):
    mx = jnp.maximum(lbraw[0:1, :], lbraw[1:2, :])
    e0 = jnp.exp(lbraw[0:1, :] - mx)
    e1 = jnp.exp(lbraw[1:2, :] - mx)
    p0 = e0 / (e0 + e1)
    p1 = e1 / (e0 + e1)
    return (p0 + p1) - p0, p0, p1


def _tri(lower):
    r = lax.broadcasted_iota(jnp.int32, (CHUNK, CHUNK), 0)
    c = lax.broadcasted_iota(jnp.int32, (CHUNK, CHUNK), 1)
    return ((r >= c) if lower else (c >= r)).astype(BF16)


def _running_sum(tri, x):
    x1 = x.astype(BF16)
    r1 = x - x1.astype(F32)
    x2 = r1.astype(BF16)
    x3 = (r1 - x2.astype(F32)).astype(BF16)
    return _dot(tri, x1, NN) + _dot(tri, x2, NN) + _dot(tri, x3, NN)


def _row(a, idx):
    r = lax.broadcasted_iota(jnp.int32, a.shape, 0)
    return jnp.sum(jnp.where(r == idx, a, 0.0), axis=0, keepdims=True)


def _hgrn_gates(qp, fp, lb, tri):
    sgm = _sigmoid_small(fp)
    f = lb + (1.0 - lb) * sgm
    k = 1.0 - f
    a = _running_sum(tri, jnp.log(f))
    a_mid = _row(a, CHUNK // 2 - 1)
    a_last = _row(a, CHUNK - 1)
    q, dq = _silu_and_grad(qp)
    e1, e2, e3, e4 = jnp.exp(a - a_mid), jnp.exp(a_mid - a), jnp.exp(a), jnp.exp(a_last - a)
    return dict(sgm=sgm, f=f, k=k, q=q, dq=dq, e1=e1, e2=e2, e3=e3, e4=e4, dec=jnp.exp(a_last),
                q_in=q * e1, k_in=k * e2, q_out=q * e3, k_out=k * e4)


def _causal():
    r = lax.broadcasted_iota(jnp.int32, (CHUNK, CHUNK), 0)
    c = lax.broadcasted_iota(jnp.int32, (CHUNK, CHUNK), 1)
    return r >= c


def hgrn_fwd(proj4, lbraw, gn, seq, name):
    _, m_rows, di = proj4.shape
    nb, nh, nc = m_rows // seq, di // HEAD_DIM, seq // CHUNK
    rows = min(HG_ROWS, seq)
    wide = HG_WIDE * HEAD_DIM
    ns, cpb = seq // rows, rows // CHUNK

    def body(p_ref, lb_ref, gn_ref, y_ref, sts_ref, st_ref):
        @pl.when(pl.program_id(2) == 0)
        def _():
            st_ref[...] = jnp.zeros_like(st_ref)

        tri = _tri(True)
        causal = _causal()
        gain = gn_ref[...]
        lbs = [_lower_bound(lb_ref[:, j * HEAD_DIM:(j + 1) * HEAD_DIM])[0] for j in range(HG_WIDE)]

        units = [(n, j) for n in range(cpb) for j in range(HG_WIDE)]
        rs = lambda n: slice(n * CHUNK, (n + 1) * CHUNK)
        cs = lambda j: slice(j * HEAD_DIM, (j + 1) * HEAD_DIM)
        gates, v_b, sc_b, kv, o_in, o_x = {}, {}, {}, {}, {}, {}
        for n, j in units:
            gates[n, j] = _hgrn_gates(p_ref[0, rs(n), cs(j)], p_ref[1, rs(n), cs(j)], lbs[j], tri)
            v_b[n, j] = p_ref[2, rs(n), cs(j)].astype(BF16)
        for u in units:
            t = gates[u]
            sc_b[u] = jnp.where(causal, _dot(t["q_in"].astype(BF16), t["k_in"].astype(BF16), NT), 0.0).astype(BF16)
            kv[u] = _dot(v_b[u], t["k_out"].astype(BF16), TN)
        for u in units:
            o_in[u] = _dot(sc_b[u], v_b[u], NN)
        for j in range(HG_WIDE):
            st = st_ref[j]
            for n in range(cpb):
                sts_ref[n, :, cs(j)] = st
                o_x[n, j] = _dot(gates[n, j]["q_out"].astype(BF16), st.astype(BF16), NT)
                st = st * gates[n, j]["dec"] + kv[n, j]
            st_ref[j] = st
        for n, j in units:
            o = o_in[n, j] + o_x[n, j]
            r = lax.rsqrt(jnp.mean(o * o, axis=-1, keepdims=True) + EPS)
            gp = p_ref[3, rs(n), cs(j)]
            y_ref[rs(n), cs(j)] = ((o * r * gain) * (gp * _sigmoid(gp))).astype(BF16)

    return _call(
        body, name=name, grid=(nh // HG_WIDE, nb, ns),
        out_shape=[jax.ShapeDtypeStruct((m_rows, di), BF16),
                   jax.ShapeDtypeStruct((nb * nc, HEAD_DIM, di), F32)],
        in_specs=[pl.BlockSpec((4, rows, wide), lambda hg, b, s: (0, b * ns + s, hg)),
                  pl.BlockSpec((2, wide), lambda hg, b, s: (0, hg)),
                  pl.BlockSpec((1, HEAD_DIM), lambda hg, b, s: (0, 0))],
        out_specs=[pl.BlockSpec((rows, wide), lambda hg, b, s: (b * ns + s, hg)),
                   pl.BlockSpec((cpb, HEAD_DIM, wide), lambda hg, b, s: (b * ns + s, 0, hg))],
        scratch_shapes=[pltpu.VMEM((HG_WIDE, HEAD_DIM, HEAD_DIM), F32)],
        compiler_params=_params(),
    )(proj4, lbraw, gn)


def hgrn_bwd(proj4, dy, sts, lbraw, gn, seq, name):
    _, m_rows, di = proj4.shape
    nb, nh, nc = m_rows // seq, di // HEAD_DIM, seq // CHUNK
    rows = min(HG_ROWS, seq)
    wide = HG_WIDE * HEAD_DIM
    ns, cpb = seq // rows, rows // CHUNK
    n_hg = nh // HG_WIDE

    def body(p_ref, dy_ref, sts_ref, lb_ref, gn_ref, dp_ref, dlb_ref, dgn_ref, dst_ref, lbacc_ref, gnacc_ref):
        hg, b, s = pl.program_id(0), pl.program_id(1), pl.program_id(2)
        tri, triu = _tri(True), _tri(False)
        causal = _causal()
        gain = gn_ref[...]
        first = (b == 0) & (s == 0)
        cs = lambda j: slice(j * HEAD_DIM, (j + 1) * HEAD_DIM)

        def before():
            @pl.when((hg == 0) & first)
            def _():
                gnacc_ref[...] = jnp.zeros_like(gnacc_ref)

            @pl.when(first)
            def _():
                lbacc_ref[...] = jnp.zeros_like(lbacc_ref)

            @pl.when(s == 0)
            def _():
                dst_ref[...] = jnp.zeros_like(dst_ref)

        def after():
            @pl.when((b == nb - 1) & (s == ns - 1))
            def _():
                for j in range(HG_WIDE):
                    _, p0, p1 = _lower_bound(lb_ref[:, cs(j)])
                    acc = lbacc_ref[:, cs(j)]
                    dlb_ref[0:1, cs(j)] = -acc * p0 * p1
                    dlb_ref[1:2, cs(j)] = acc * p1 * (1.0 - p1)

            @pl.when((hg == n_hg - 1) & (b == nb - 1) & (s == ns - 1))
            def _():
                tot = gnacc_ref[:, 0:HEAD_DIM]
                for j in range(1, HG_WIDE):
                    tot = tot + gnacc_ref[:, cs(j)]
                dgn_ref[...] = tot

        before()

        units = [(n, j) for n in range(cpb) for j in range(HG_WIDE)]
        rs = lambda n: slice(n * CHUNK, (n + 1) * CHUNK)
        lbs = [_lower_bound(lb_ref[:, cs(j)])[0] for j in range(HG_WIDE)]
        gates, v_b, st_b, sc_b, o, do_b = {}, {}, {}, {}, {}, {}
        dq_out, dsc_b, dv, g_st, dq_in, dk_in, dst_at, dk_out, ddec = {}, {}, {}, {}, {}, {}, {}, {}, {}
        for n, j in units:
            gates[n, j] = _hgrn_gates(p_ref[0, rs(n), cs(j)], p_ref[1, rs(n), cs(j)], lbs[j], tri)
            v_b[n, j] = p_ref[2, rs(n), cs(j)].astype(BF16)
            st_b[n, j] = sts_ref[n, :, cs(j)].astype(BF16)
        for u in units:
            t = gates[u]
            sc_b[u] = jnp.where(causal, _dot(t["q_in"].astype(BF16), t["k_in"].astype(BF16), NT), 0.0).astype(BF16)
        for u in units:
            o[u] = _dot(sc_b[u], v_b[u], NN) + _dot(gates[u]["q_out"].astype(BF16), st_b[u], NT)
        for n, j in units:
            ov = o[n, j]
            r = lax.rsqrt(jnp.mean(ov * ov, axis=-1, keepdims=True) + EPS)
            ohat = ov * r
            sg, dsg = _silu_and_grad(p_ref[3, rs(n), cs(j)])
            dyv = dy_ref[rs(n), cs(j)].astype(F32)
            dp_ref[3, rs(n), cs(j)] = (dyv * (ohat * gain) * dsg).astype(BF16)
            d_on = dyv * sg
            gnacc_ref[:, cs(j)] = gnacc_ref[:, cs(j)] + jnp.sum(d_on * ohat, axis=0, keepdims=True)
            dohat = d_on * gain
            do_b[n, j] = (r * (dohat - ohat * jnp.mean(dohat * ohat, axis=-1, keepdims=True))).astype(BF16)
        for u in units:
            dq_out[u] = _dot(do_b[u], st_b[u], NN)
            dsc_b[u] = jnp.where(causal, _dot(do_b[u], v_b[u], NT), 0.0).astype(BF16)
            dv[u] = _dot(sc_b[u], do_b[u], TN)
            g_st[u] = _dot(do_b[u], gates[u]["q_out"].astype(BF16), TN)
        for u in units:
            dq_in[u] = _dot(dsc_b[u], gates[u]["k_in"].astype(BF16), NN)
            dk_in[u] = _dot(dsc_b[u], gates[u]["q_in"].astype(BF16), TN)
        for j in range(HG_WIDE):
            dst = dst_ref[j]
            for n in reversed(range(cpb)):
                dst_at[n, j] = dst
                dst = dst * gates[n, j]["dec"] + g_st[n, j]
            dst_ref[j] = dst
        for n, j in units:
            dst = dst_at[n, j]
            dst_b = dst.astype(BF16)
            dk_out[n, j] = _dot(v_b[n, j], dst_b, NN)
            dv[n, j] = dv[n, j] + _dot(gates[n, j]["k_out"].astype(BF16), dst_b, NT)
            ddec[n, j] = jnp.sum(dst * sts_ref[n, :, cs(j)], axis=0, keepdims=True)
        for n, j in units:
            t = gates[n, j]
            dp_ref[2, rs(n), cs(j)] = dv[n, j].astype(BF16)
            dq = dq_in[n, j] * t["e1"] + dq_out[n, j] * t["e3"]
            dk = dk_in[n, j] * t["e2"] + dk_out[n, j] * t["e4"]
            w_in = dq_in[n, j] * t["q_in"] - dk_in[n, j] * t["k_in"]
            w_out = dk_out[n, j] * t["k_out"]
            da = w_in + dq_out[n, j] * t["q_out"] - w_out
            da_mid = -jnp.sum(w_in, axis=0, keepdims=True)
            da_last = jnp.sum(w_out, axis=0, keepdims=True) + ddec[n, j] * t["dec"]
            rid = lax.broadcasted_iota(jnp.int32, da.shape, 0)
            da = da + jnp.where(rid == CHUNK // 2 - 1, da_mid, 0.0) + jnp.where(rid == CHUNK - 1, da_last, 0.0)
            dlf = _running_sum(triu, da)
            df = dlf / t["f"] - dk
            sgm = t["sgm"]
            dp_ref[1, rs(n), cs(j)] = (df * (1.0 - lbs[j]) * sgm * (1.0 - sgm)).astype(BF16)
            lbacc_ref[:, cs(j)] = lbacc_ref[:, cs(j)] + jnp.sum(df * (1.0 - sgm), axis=0, keepdims=True)
            dp_ref[0, rs(n), cs(j)] = (dq * t["dq"]).astype(BF16)

        after()

    blk = lambda hg, b, s: b * ns + (ns - 1 - s)
    return _call(
        body, name=name, grid=(n_hg, nb, ns),
        out_shape=[jax.ShapeDtypeStruct((4, m_rows, di), BF16), jax.ShapeDtypeStruct((2, di), F32),
                   jax.ShapeDtypeStruct((1, HEAD_DIM), F32)],
        in_specs=[pl.BlockSpec((4, rows, wide), lambda hg, b, s: (0, blk(hg, b, s), hg)),
                  pl.BlockSpec((rows, wide), lambda hg, b, s: (blk(hg, b, s), hg)),
                  pl.BlockSpec((cpb, HEAD_DIM, wide), lambda hg, b, s: (blk(hg, b, s), 0, hg)),
                  pl.BlockSpec((2, wide), lambda hg, b, s: (0, hg)),
                  pl.BlockSpec((1, HEAD_DIM), lambda hg, b, s: (0, 0))],
        out_specs=[pl.BlockSpec((4, rows, wide), lambda hg, b, s: (0, blk(hg, b, s), hg)),
                   pl.BlockSpec((2, wide), lambda hg, b, s: (0, hg)),
                   pl.BlockSpec((1, HEAD_DIM), lambda hg, b, s: (0, 0))],
        scratch_shapes=[pltpu.VMEM((HG_WIDE, HEAD_DIM, HEAD_DIM), F32), pltpu.VMEM((1, wide), F32),
                        pltpu.VMEM((1, wide), F32)],
        compiler_params=_params(),
    )(proj4, dy, sts, lbraw, gn)


def outproj_loss(y, w, x, mod, fg, target, seq, name):
    m_rows, di = y.shape
    d = w.shape[1]
    tm = min(512, seq)

    def body(y_ref, w_ref, x_ref, mod_ref, fg_ref, t_ref, out_ref, loss_ref, dx_ref, dfg_ref):
        i = pl.program_id(0)
        acc = _dot(y_ref[...], w_ref[...], NN)
        out_ref[...] = acc.astype(BF16)
        xv = x_ref[...] + mod_ref[0][:, 2 * d:] * acc
        gain = fg_ref[...]
        r = lax.rsqrt(jnp.mean(xv * xv, axis=-1, keepdims=True) + EPS)
        xn = xv * r
        e = xn * gain - t_ref[...]
        part = 0.5 * jnp.sum(jnp.mean(e * e, axis=-1, keepdims=True), axis=0, keepdims=True)
        dyv = e / d
        p_fg = jnp.sum(dyv * xn, axis=0, keepdims=True)
        dxn = dyv * gain
        dx_ref[...] = r * (dxn - xn * jnp.mean(dxn * xn, axis=-1, keepdims=True))

        @pl.when(i == 0)
        def _():
            loss_ref[...] = part
            dfg_ref[...] = p_fg

        @pl.when(i != 0)
        def _():
            loss_ref[...] = loss_ref[...] + part
            dfg_ref[...] = dfg_ref[...] + p_fg

    row = pl.BlockSpec((tm, d), lambda i: (i, 0))
    return _call(
        body, name=name, grid=(m_rows // tm,),
        out_shape=[jax.ShapeDtypeStruct((m_rows, d), BF16), jax.ShapeDtypeStruct((1, 1), F32),
                   jax.ShapeDtypeStruct((m_rows, d), F32), jax.ShapeDtypeStruct((1, d), F32)],
        in_specs=[pl.BlockSpec((tm, di), lambda i: (i, 0)),
                  pl.BlockSpec((di, d), lambda i: (0, 0)),
                  row,
                  pl.BlockSpec((1, 1, 3 * d), lambda i: ((i * tm) // seq, 0, 0)),
                  pl.BlockSpec((1, d), lambda i: (0, 0)), row],
        out_specs=[row, pl.BlockSpec((1, 1), lambda i: (0, 0)), row, pl.BlockSpec((1, d), lambda i: (0, 0))],
        compiler_params=_params(),
    )(y, w, x, mod, fg, target)


def _pack(parts):
    flat = jnp.concatenate([p.reshape(-1) for p in parts])
    pad = (-flat.shape[0]) % (8 * LANES)
    return jnp.pad(flat, (0, pad)).reshape(-1, LANES)


def kernel(x, c, norm_gain, w_ada, b_ada, a_w_in, a_ln_gain, a_ln_bias, a_w_s, a_b_s, a_w_out, b_w_in, b_lower_bounds, b_gn_gain, b_w_out, final_gain, loss_target, m_norm_gain, m_w_ada, m_b_ada, m_a_w_in, m_a_ln_gain, m_a_ln_bias, m_a_w_s, m_a_b_s, m_a_w_out, m_b_w_in, m_b_lower_bounds, m_b_gn_gain, m_b_w_out, m_final_gain, v_norm_gain, v_w_ada, v_b_ada, v_a_w_in, v_a_ln_gain, v_a_ln_bias, v_a_w_s, v_a_b_s, v_a_w_out, v_b_w_in, v_b_lower_bounds, v_b_gn_gain, v_b_w_out, v_final_gain):
    nb, seq, d = x.shape
    m_rows = nb * seq
    n_l = w_ada.shape[0]
    ada_cols = w_ada.shape[2]
    px, py, pc = _place()
    chip = 2 * px + py
    dev = 2 * chip + pc

    c_all = allgather_small(c.reshape(-1, LANES), "gather_c").reshape(N_DEV * nb, d)
    s_a = halves_start([cast_into_slot(a_w_in[0], chip, c_all, "cast_a_in"),
                        cast_into_slot(a_w_out[0], chip, c_all, "cast_a_out")], "gather_a_start")
    land_b_in = cast_into_slot(b_w_in[0], chip, s_a[3], "cast_b_in")
    land_b_out = cast_into_slot(b_w_out[0], chip, land_b_in, "cast_b_out")
    b_cols = lax.dynamic_slice_in_dim(b_ada, chip * ada_cols, ada_cols, axis=1).reshape(n_l, 1, ada_cols)
    mod_cols = ada_fwd(c_all, w_ada, b_cols, land_b_out, "ada_fwd")
    mod_g = allgather_small(mod_cols.reshape(-1, LANES), "gather_mod")
    mod_g = mod_g.reshape(N_CHIPS, 2, n_l, N_DEV * nb, ada_cols)[:, 0]
    mod_all = jnp.transpose(mod_g, (1, 2, 0, 3)).reshape(n_l, N_DEV * nb, 3 * d)
    mod_mine = lax.dynamic_slice_in_dim(mod_all, dev * nb, nb, axis=1)
    mod0 = mod_mine[0].reshape(nb, 1, 3 * d)
    mod1 = mod_mine[1].reshape(nb, 1, 3 * d)

    landed_a = halves_wait(s_a[0], s_a[1], s_a[2], mod_mine, "gather_a_wait")
    s_bi = gather_start(land_b_in, landed_a[0], "gather_b_in_start")
    s_bo = gather_start(land_b_out, s_bi[3], "gather_b_out_start")
    wa_in, wa_out = pass_halves(list(landed_a), "gather_a_pass")
    di = a_w_out.shape[1] * N_CHIPS
    wa_out = wa_out.reshape(di, d)

    x0 = x.reshape(m_rows, d)
    tgt = loss_target.reshape(m_rows, d)
    ng0 = norm_gain[0:1] + (s_bi[3][0, 0] + s_bo[3][0, 0])
    ng1 = norm_gain[1:2]
    bs_col = a_b_s[0].reshape(SG_GROUPS, SG_BLOCK, 1)
    proj_a, h_a = inproj_fwd(x0, mod0, ng0, wa_in, seq, False, "a_inproj")
    y_a = sgu_fwd(proj_a, a_ln_gain, a_ln_bias, a_w_s[0], bs_col, "a_sgu")
    x1, out_a = outproj_fwd(y_a, wa_out, x0, mod0, seq, "a_outproj")
    wb_in = gather_wait(*s_bi[:3], out_a, "gather_b_in_wait")
    proj_b, h_b = inproj_fwd(x1, mod1, ng1, wb_in, seq, True, "b_inproj")
    y_b, sts_b = hgrn_fwd(proj_b, b_lower_bounds, b_gn_gain, seq, "b_hgrn")
    wb_out = gather_wait(*s_bo[:3], y_b, "gather_b_out_wait").reshape(di, d)
    out_b, loss_part, dx2, dfg = outproj_loss(
        y_b, wb_out, x1, mod1, final_gain.reshape(1, d), tgt, seq, "b_outproj_loss")

    shard_rows = di // N_CHIPS
    dy_b, dout_b, dgate1 = outproj_bwd(dx2, out_b, mod1, wb_out, seq, "b_outproj_bwd")
    gwb_out = grad_w_out(y_b, dout_b, "b_grad_w_out").reshape(N_CHIPS, shard_rows, d)
    e_bo = exchange_start(gwb_out, "exchange_b_out_start")
    dproj_b, dlb, dgn = hgrn_bwd(
        proj_b, dy_b, sts_b, b_lower_bounds, b_gn_gain + e_bo[4][0, 0], seq, "b_hgrn_bwd")
    e_bi = exchange_start(grad_w_in(h_b, dproj_b, N_CHIPS, True, "b_grad_w_in"), "exchange_b_in_start")
    dx1, dshift1, dscale1, dng1 = inproj_bwd(
        dproj_b, wb_in, x1, dx2, mod1, ng1 + e_bi[4][0, 0], seq, True, "b_inproj_bwd")

    dy_a, dout_a, dgate0 = outproj_bwd(dx1, out_a, mod0, wa_out, seq, "a_outproj_bwd")
    gwa_out = grad_w_out(y_a, dout_a, "a_grad_w_out").reshape(N_CHIPS, shard_rows, d)
    e_ao = exchange_start(gwa_out, "exchange_a_out_start")
    dproj_a, dws, dbs, dlg, dlbias = sgu_bwd(
        proj_a, dy_a, a_ln_gain + e_ao[4][0, 0], a_ln_bias, a_w_s[0], bs_col, "a_sgu_bwd")
    e_ai = exchange_start(grad_w_in(h_a, dproj_a, N_CHIPS, False, "a_grad_w_in"), "exchange_a_in_start")
    dx0, dshift0, dscale0, dng0 = inproj_bwd(
        dproj_a, wa_in, x0, dx1, mod0, norm_gain[0:1] + e_ai[4][0, 0], seq, False, "a_inproj_bwd")
    grad_x = dx0.reshape(nb, seq, d)

    dmod = jnp.concatenate([dshift0, dscale0, dgate0, dshift1, dscale1, dgate1], axis=2)
    n_dmod = dmod.size
    small_g = [jnp.concatenate([dng0, dng1], axis=0), dlg, dlbias, dws, dbs, dlb, dfg, dgn]
    packed_g = _pack([dmod] + small_g + [loss_part])
    rows = packed_g.shape[0]
    s_small = gather_all_start(
        lax.dynamic_update_slice(jnp.zeros((N_DEV, rows, LANES), F32), packed_g[None], (dev, 0, 0)),
        "gather_small_start")

    def finish(group, after):
        mine = []
        for ex, _, _, _, nm in group:
            parts_thru, land = exchange_wait(ex[0], ex[1], ex[2], ex[3], after, "exchange_" + nm + "_wait")
            mine.append(sum_parts(parts_thru, land, chip, "sum_" + nm))
            after = mine[-1]
        theirs = swap_sibling(mine, "swap_" + group[0][4])
        return [[r.reshape(w.shape) for r in adamw_pair(pa, pb, w[0], m[0], v[0], "adamw_" + nm)]
                for pa, pb, (_, w, m, v, nm) in zip(mine, theirs, group)]

    (gb_out, db_out, mb_out, vb_out), (gb_in, db_in, mb_in, vb_in), (ga_out, da_out, ma_out, va_out) = finish(
        [(e_bo, b_w_out, m_b_w_out, v_b_w_out, "b_out"), (e_bi, b_w_in, m_b_w_in, v_b_w_in, "b_in"),
         (e_ao, a_w_out, m_a_w_out, v_a_w_out, "a_out")], s_small[3])
    ((ga_in, da_in, ma_in, va_in),) = finish([(e_ai, a_w_in, m_a_w_in, v_a_w_in, "a_in")], ga_out)

    small_w = [norm_gain, a_ln_gain, a_ln_bias, a_w_s, a_b_s, b_lower_bounds, final_gain, b_gn_gain]
    small_m = [m_norm_gain, m_a_ln_gain, m_a_ln_bias, m_a_w_s, m_a_b_s, m_b_lower_bounds, m_final_gain, m_b_gn_gain]
    small_v = [v_norm_gain, v_a_ln_gain, v_a_ln_bias, v_a_w_s, v_a_b_s, v_b_lower_bounds, v_final_gain, v_b_gn_gain]
    rows_of = lambda a: a.reshape(-1, a.shape[-1])
    gathered = gather_all_wait(s_small[0], s_small[1], s_small[2], ga_in, "gather_small_wait")
    tail, small_res = small_update(
        gathered, n_dmod // LANES, [rows_of(a) for a in small_w], [rows_of(a) for a in small_m],
        [rows_of(a) for a in small_v], "small_update")
    loss = tail[0, 0]
    sg, sd, sm, sv = [[small_res[p][kind].reshape(w.shape) for p, w in enumerate(small_w)] for kind in range(4)]

    dmod_all = gathered[:, :n_dmod // LANES].reshape(N_DEV * nb, n_l, 3 * d)
    dmod_cols = lax.dynamic_slice_in_dim(dmod_all, chip * ada_cols, ada_cols, axis=2)
    dmod_cols = jnp.transpose(dmod_cols, (1, 0, 2))
    g_wada, d_wada, m_wada, v_wada = ada_bwd(c_all, dmod_cols, w_ada, m_w_ada, v_w_ada, "ada_bwd")
    flat = lambda a: a.reshape(1, -1)
    g_bada, d_bada, m_bada, v_bada = [
        r.reshape(b_ada.shape) for r in
        bias_update(dmod_all.reshape(N_DEV * nb, n_l * 3 * d), flat(b_ada), flat(m_b_ada), flat(v_b_ada), "bias_update")]

    def order(ng, wada, bada, ain, sm_rest, aout, bin_, bout):
        lg, lbi, ws_, bs_, lbd, fg_, gn_ = sm_rest
        return [ng, wada, bada, ain, lg, lbi, ws_, bs_, aout, bin_, lbd, gn_, bout, fg_]

    grads = order(sg[0], g_wada, g_bada, ga_in, sg[1:8], ga_out, gb_in, gb_out)
    deltas = order(sd[0], d_wada, d_bada, da_in, sd[1:8], da_out, db_in, db_out)
    new_m = order(sm[0], m_wada, m_bada, ma_in, sm[1:8], ma_out, mb_in, mb_out)
    new_v = order(sv[0], v_wada, v_bada, va_in, sv[1:8], va_out, vb_in, vb_out)
    return (loss, grad_x, *grads, *deltas, *new_m, *new_v)
```

```python
import jax
import jax.numpy as jnp
from jax import lax
from jax.experimental import pallas as pl
from jax.experimental.pallas import tpu as pltpu

F32 = jnp.float32
BF16 = jnp.bfloat16
EPS = 1e-6
CHUNK = 64
SG_BLOCK = 128
SG_GROUPS = 8
SG_STEP_BLOCKS = 4
HEAD_DIM = 128
HG_WIDE = 8
HG_ROWS = 256
N_CHIPS = 4
N_DEV = 8
LANES = 128
ADAM_LR = 0.001
ADAM_B1 = 0.9
ADAM_B2 = 0.999
ADAM_EPS = 1e-08
ADAM_WD = 0.01
ADAM_STEP = 10
GELU_C0 = 0.7978845608028654
GELU_C1 = 0.044715
MESH = pl.DeviceIdType.MESH
VMEM_LIMIT = 56 * 1024 * 1024


ROW_TILE = 1024


def _col_tile(n):
    return next(t for t in (1024, 768, 512, 256) if n % t == 0)


def _call(body, **kw):
    return pl.pallas_call(body, **kw)


def _params(**kw):
    return pltpu.CompilerParams(vmem_limit_bytes=VMEM_LIMIT, **kw)


def _sigmoid(x):
    return 0.5 * jnp.tanh(0.5 * x) + 0.5


def _sigmoid_small(x):
    return 1.0 / (1.0 + jnp.exp(-x))


def _silu_and_grad(x):
    s = _sigmoid(x)
    return x * s, s * (1.0 + x * (1.0 - s))


def _gelu(x):
    return 0.5 * x * (1.0 + jnp.tanh(GELU_C0 * (x + GELU_C1 * x * x * x)))


def _gelu_and_grad(x):
    t = jnp.tanh(GELU_C0 * (x + GELU_C1 * x * x * x))
    g = 0.5 * x * (1.0 + t)
    dg = 0.5 * (1.0 + t) + 0.5 * x * (1.0 - t * t) * (GELU_C0 * (1.0 + 3.0 * GELU_C1 * x * x))
    return g, dg


def _dot(a, b, dims, precision=None):
    return lax.dot_general(a, b, (dims, ((), ())), precision=precision, preferred_element_type=F32)


NN = ((1,), (0,))
NT = ((1,), (1,))
TN = ((0,), (0,))


def _adamw(w, g, m, v):
    m = ADAM_B1 * m + (1.0 - ADAM_B1) * g
    v = ADAM_B2 * v + (1.0 - ADAM_B2) * (g * g)
    m_hat = m / (1.0 - ADAM_B1 ** ADAM_STEP)
    v_hat = v / (1.0 - ADAM_B2 ** ADAM_STEP)
    delta = -ADAM_LR * (m_hat / (jnp.sqrt(v_hat) + ADAM_EPS) + ADAM_WD * w)
    return delta, m, v


def _chunk_mask():
    r = lax.broadcasted_iota(jnp.int32, (SG_BLOCK, SG_BLOCK), 0)
    c = lax.broadcasted_iota(jnp.int32, (SG_BLOCK, SG_BLOCK), 1)
    return (c // CHUNK) <= (r // CHUNK)


def _place():
    return lax.axis_index("x"), lax.axis_index("y"), lax.axis_index("c")


def _other_chips(x, y):
    return [(1 - x, y), (x, 1 - y), (1 - x, 1 - y)]


def allgather_small(v, name):
    m_per, n = v.shape

    def body(x_ref, out_ref, send_sems, recv_sems, local_sem):
        x, y, c = _place()
        me, sibling = (x, y, c), (x, y, 1 - c)
        chips = _other_chips(x, y)

        def rows(px, py, pc):
            return out_ref.at[pl.ds((4 * px + 2 * py + pc) * m_per, m_per), :]

        def copy(k, block, to, src=None):
            return pltpu.make_async_remote_copy(
                src_ref=rows(*block) if src is None else src, dst_ref=rows(*block),
                send_sem=send_sems.at[k], recv_sem=recv_sems.at[k], device_id=to, device_id_type=MESH)

        mine = pltpu.make_async_copy(x_ref, rows(*me), local_sem)
        mine.start()
        first = [copy(0, me, sibling, src=x_ref)]
        first += [copy(1 + j, me, (*chip, c), src=x_ref) for j, chip in enumerate(chips)]
        for cp in first:
            cp.start()
        passed = [copy(4 + j, (*chip, c), sibling) for j, chip in enumerate(chips)]
        for j, chip in enumerate(chips):
            copy(1 + j, (*chip, c), me).wait_recv()
            passed[j].start()
        copy(0, sibling, me).wait_recv()
        for j, chip in enumerate(chips):
            copy(4 + j, (*chip, 1 - c), me).wait_recv()
        for cp in first + passed:
            cp.wait_send()
        mine.wait()

    return _call(
        body, name=name,
        out_shape=jax.ShapeDtypeStruct((N_DEV * m_per, n), v.dtype),
        in_specs=[pl.BlockSpec(memory_space=pltpu.VMEM)],
        out_specs=pl.BlockSpec(memory_space=pltpu.VMEM),
        scratch_shapes=[pltpu.SemaphoreType.DMA((7,)), pltpu.SemaphoreType.DMA((7,)), pltpu.SemaphoreType.DMA],
    )(v)


def _hbm_spec():
    return pl.BlockSpec(memory_space=pltpu.HBM)


def _sem_spec():
    return pl.BlockSpec(memory_space=pltpu.SEMAPHORE)


def _split_params():
    return pltpu.CompilerParams(has_side_effects=pltpu.SideEffectType.DATAFLOW_SIDE_EFFECTING)


def _hbm(a):
    return pltpu.with_memory_space_constraint(a, pltpu.HBM)


def _half_copy(land_ref, rows, chip_idx, core_half, send_sem, recv_sem, to):
    half = land_ref.at[chip_idx, pl.ds(core_half * (rows // 2), rows // 2), :]
    return pltpu.make_async_remote_copy(
        src_ref=half, dst_ref=half, send_sem=send_sem, recv_sem=recv_sem, device_id=to, device_id_type=MESH)


def halves_start(lands, name):
    n = len(lands)

    def body(*refs):
        land_refs, send_sems, recv_sems, token = refs[:n], refs[n], refs[n + 1], refs[-1]
        x, y, c = _place()
        for w in range(n):
            for j, (px, py) in enumerate(_other_chips(x, y)):
                _half_copy(land_refs[w], lands[w].shape[1], 2 * x + y, c,
                           send_sems.at[3 * w + j], recv_sems.at[3 * w + j], (px, py, c)).start()
        token[...] = jnp.zeros_like(token)

    res = _call(
        body, name=name,
        out_shape=(pltpu.SemaphoreType.DMA((3 * n,)), pltpu.SemaphoreType.DMA((3 * n,)),
                   *[pltpu.HBM(a.shape, a.dtype) for a in lands], jax.ShapeDtypeStruct((8, LANES), F32)),
        in_specs=(_hbm_spec(),) * n,
        out_specs=(_sem_spec(), _sem_spec(), *[_hbm_spec()] * n, pl.BlockSpec(memory_space=pltpu.VMEM)),
        input_output_aliases={w: 2 + w for w in range(n)}, compiler_params=_split_params(),
    )(*[_hbm(a) for a in lands])
    return res[0], res[1], list(res[2:2 + n]), res[2 + n]


def halves_wait(send_sems, recv_sems, lands, after, name):
    n = len(lands)

    def body(*refs):
        land_refs, send_sems, recv_sems = refs[:n], refs[n], refs[n + 1]
        x, y, c = _place()
        for w in range(n):
            for j, (px, py) in enumerate(_other_chips(x, y)):
                cp = _half_copy(land_refs[w], lands[w].shape[1], 2 * px + py, c,
                                send_sems.at[3 * w + j], recv_sems.at[3 * w + j], (px, py, c))
                cp.wait_send()
                cp.wait_recv()

    return _call(
        body, name=name,
        out_shape=tuple(pltpu.HBM(a.shape, a.dtype) for a in lands),
        in_specs=(*[_hbm_spec()] * n, _sem_spec(), _sem_spec(), pl.BlockSpec(memory_space=pl.ANY)),
        out_specs=tuple(_hbm_spec() for _ in lands), input_output_aliases={w: w for w in range(n)},
        compiler_params=_split_params(),
    )(*lands, send_sems, recv_sems, after)


def pass_halves(lands, name):
    n = len(lands)

    def body(*refs):
        land_refs, send_sems, recv_sems = refs[n:2 * n], refs[2 * n], refs[2 * n + 1]
        x, y, c = _place()
        sent = []
        for w in range(n):
            for j, (px, py) in enumerate(_other_chips(x, y)):
                cp = _half_copy(land_refs[w], lands[w].shape[1], 2 * px + py, c,
                                send_sems.at[3 * w + j], recv_sems.at[3 * w + j], (x, y, 1 - c))
                cp.start()
                sent.append(cp)
        for w in range(n):
            for j, (px, py) in enumerate(_other_chips(x, y)):
                _half_copy(land_refs[w], lands[w].shape[1], 2 * px + py, 1 - c,
                           send_sems.at[3 * w + j], recv_sems.at[3 * w + j], (x, y, 1 - c)).wait_recv()
        for cp in sent:
            cp.wait_send()

    return _call(
        body, name=name,
        out_shape=[jax.ShapeDtypeStruct(a.shape, a.dtype) for a in lands],
        in_specs=[_hbm_spec()] * n, out_specs=[_hbm_spec()] * n,
        input_output_aliases={w: w for w in range(n)},
        scratch_shapes=[pltpu.SemaphoreType.DMA((3 * n,)), pltpu.SemaphoreType.DMA((3 * n,))],
    )(*lands)


def gather_start(land, after, name):
    def body(land_ref, after_ref, send_sems, recv_sems, land_thru, token):
        del after_ref, land_thru
        x, y, c = _place()
        for j, (px, py) in enumerate(_other_chips(x, y)):
            pltpu.make_async_remote_copy(
                src_ref=land_ref.at[2 * x + y], dst_ref=land_ref.at[2 * x + y],
                send_sem=send_sems.at[j], recv_sem=recv_sems.at[j], device_id=(px, py, c),
                device_id_type=MESH).start()
        token[...] = jnp.zeros_like(token)

    return _call(
        body, name=name,
        out_shape=(pltpu.SemaphoreType.DMA((3,)), pltpu.SemaphoreType.DMA((3,)),
                   pltpu.HBM(land.shape, land.dtype), jax.ShapeDtypeStruct((8, LANES), F32)),
        in_specs=(_hbm_spec(), pl.BlockSpec(memory_space=pl.ANY)),
        out_specs=(_sem_spec(), _sem_spec(), _hbm_spec(), pl.BlockSpec(memory_space=pltpu.VMEM)),
        input_output_aliases={0: 2}, compiler_params=_split_params(),
    )(_hbm(land), after)


def gather_wait(send_sems, recv_sems, land, after, name):
    def body(land_ref, send_sems, recv_sems, after_ref, land_out):
        del after_ref, land_out
        x, y, c = _place()
        for j, (px, py) in enumerate(_other_chips(x, y)):
            cp = pltpu.make_async_remote_copy(
                src_ref=land_ref.at[2 * x + y], dst_ref=land_ref.at[2 * px + py],
                send_sem=send_sems.at[j], recv_sem=recv_sems.at[j], device_id=(px, py, c), device_id_type=MESH)
            cp.wait_send()
            cp.wait_recv()

    return _call(
        body, name=name,
        out_shape=pltpu.HBM(land.shape, land.dtype),
        in_specs=(_hbm_spec(), _sem_spec(), _sem_spec(), pl.BlockSpec(memory_space=pl.ANY)),
        out_specs=_hbm_spec(), input_output_aliases={0: 0}, compiler_params=_split_params(),
    )(land, send_sems, recv_sems, after)


def _flips():
    return [(fx, fy, fc) for fx in (0, 1) for fy in (0, 1) for fc in (0, 1) if (fx, fy, fc) != (0, 0, 0)]


def _flipped(x, y, c, flip):
    fx, fy, fc = flip
    return (1 - x if fx else x, 1 - y if fy else y, 1 - c if fc else c)


def gather_all_start(land, name):
    def body(land_ref, send_sems, recv_sems, land_thru, token):
        del land_thru
        x, y, c = _place()
        for k, flip in enumerate(_flips()):
            pltpu.make_async_remote_copy(
                src_ref=land_ref.at[4 * x + 2 * y + c], dst_ref=land_ref.at[4 * x + 2 * y + c],
                send_sem=send_sems.at[k], recv_sem=recv_sems.at[k], device_id=_flipped(x, y, c, flip),
                device_id_type=MESH).start()
        token[...] = jnp.zeros_like(token)

    return _call(
        body, name=name,
        out_shape=(pltpu.SemaphoreType.DMA((7,)), pltpu.SemaphoreType.DMA((7,)),
                   pltpu.HBM(land.shape, land.dtype), jax.ShapeDtypeStruct((8, LANES), F32)),
        in_specs=(_hbm_spec(),),
        out_specs=(_sem_spec(), _sem_spec(), _hbm_spec(), pl.BlockSpec(memory_space=pltpu.VMEM)),
        input_output_aliases={0: 2}, compiler_params=_split_params(),
    )(_hbm(land))


def gather_all_wait(send_sems, recv_sems, land, after, name):
    def body(land_ref, send_sems, recv_sems, after_ref, land_out):
        del after_ref, land_out
        x, y, c = _place()
        for k, flip in enumerate(_flips()):
            px, py, pc = _flipped(x, y, c, flip)
            cp = pltpu.make_async_remote_copy(
                src_ref=land_ref.at[4 * x + 2 * y + c], dst_ref=land_ref.at[4 * px + 2 * py + pc],
                send_sem=send_sems.at[k], recv_sem=recv_sems.at[k], device_id=(px, py, pc), device_id_type=MESH)
            cp.wait_send()
            cp.wait_recv()

    return _call(
        body, name=name,
        out_shape=pltpu.HBM(land.shape, land.dtype),
        in_specs=(_hbm_spec(), _sem_spec(), _sem_spec(), pl.BlockSpec(memory_space=pl.ANY)),
        out_specs=_hbm_spec(), input_output_aliases={0: 0}, compiler_params=_split_params(),
    )(land, send_sems, recv_sems, after)


def exchange_start(parts, name):
    _, r, c_ = parts.shape

    def body(parts_ref, land_ref, send_sems, recv_sems, parts_thru, land_thru, token):
        del parts_thru, land_thru
        x, y, c = _place()
        for j, (px, py) in enumerate(_other_chips(x, y)):
            pltpu.make_async_remote_copy(
                src_ref=parts_ref.at[2 * px + py], dst_ref=land_ref.at[j],
                send_sem=send_sems.at[j], recv_sem=recv_sems.at[j], device_id=(px, py, c),
                device_id_type=MESH).start()
        token[...] = jnp.zeros_like(token)

    return _call(
        body, name=name,
        out_shape=(pltpu.SemaphoreType.DMA((3,)), pltpu.SemaphoreType.DMA((3,)),
                   pltpu.HBM(parts.shape, parts.dtype), pltpu.HBM((3, r, c_), parts.dtype),
                   jax.ShapeDtypeStruct((8, LANES), F32)),
        in_specs=(_hbm_spec(), _hbm_spec()),
        out_specs=(_sem_spec(), _sem_spec(), _hbm_spec(), _hbm_spec(), pl.BlockSpec(memory_space=pltpu.VMEM)),
        input_output_aliases={0: 2, 1: 3}, compiler_params=_split_params(),
    )(_hbm(parts), _hbm(lax.empty((3, r, c_), parts.dtype)))


def exchange_wait(send_sems, recv_sems, parts, land, after, name):
    def body(parts_ref, land_ref, send_sems, recv_sems, after_ref, parts_out, land_out):
        del after_ref, parts_out, land_out
        x, y, c = _place()
        for j, (px, py) in enumerate(_other_chips(x, y)):
            cp = pltpu.make_async_remote_copy(
                src_ref=parts_ref.at[2 * px + py], dst_ref=land_ref.at[j],
                send_sem=send_sems.at[j], recv_sem=recv_sems.at[j], device_id=(px, py, c), device_id_type=MESH)
            cp.wait_send()
            cp.wait_recv()

    return _call(
        body, name=name,
        out_shape=(pltpu.HBM(parts.shape, parts.dtype), pltpu.HBM(land.shape, land.dtype)),
        in_specs=(_hbm_spec(), _hbm_spec(), _sem_spec(), _sem_spec(), pl.BlockSpec(memory_space=pl.ANY)),
        out_specs=(_hbm_spec(), _hbm_spec()), input_output_aliases={0: 0, 1: 1},
        compiler_params=_split_params(),
    )(parts, land, send_sems, recv_sems, after)


def cast_into_slot(w, chip, after, name):
    r, c = w.shape
    tr = min(256, r)

    def body(s_ref, w_ref, after_ref, o_ref):
        del s_ref, after_ref
        o_ref[...] = w_ref[...].astype(BF16)

    return _call(
        body, name=name,
        grid_spec=pltpu.PrefetchScalarGridSpec(
            num_scalar_prefetch=1, grid=(r // tr,),
            in_specs=[pl.BlockSpec((tr, c), lambda i, s: (i, 0)), pl.BlockSpec(memory_space=pl.ANY)],
            out_specs=pl.BlockSpec((None, tr, c), lambda i, s: (s[0], i, 0))),
        out_shape=jax.ShapeDtypeStruct((N_CHIPS, r, c), BF16),
        compiler_params=_params(),
    )(chip.reshape(1).astype(jnp.int32), w, after)


def sum_parts(parts, land, chip, name):
    _, r, c = parts.shape
    tr = min(256, r)

    def body(s_ref, p_ref, l_ref, o_ref):
        del s_ref
        acc = p_ref[...].astype(F32) + l_ref[0].astype(F32)
        acc = acc + l_ref[1].astype(F32)
        o_ref[...] = (acc + l_ref[2].astype(F32)).astype(BF16)

    return _call(
        body, name=name,
        grid_spec=pltpu.PrefetchScalarGridSpec(
            num_scalar_prefetch=1, grid=(r // tr,),
            in_specs=[pl.BlockSpec((None, tr, c), lambda i, s: (s[0], i, 0)),
                      pl.BlockSpec((3, tr, c), lambda i, s: (0, i, 0))],
            out_specs=pl.BlockSpec((tr, c), lambda i, s: (i, 0))),
        out_shape=jax.ShapeDtypeStruct((r, c), BF16),
        compiler_params=_params(),
    )(chip.reshape(1).astype(jnp.int32), parts, land)


def swap_sibling(arrs, name):
    n = len(arrs)

    def body(*refs):
        ins, outs = refs[:n], refs[n:2 * n]
        send_sems, recv_sems = refs[2 * n:]
        x, y, c = _place()
        cps = []
        for w in range(n):
            cp = pltpu.make_async_remote_copy(
                src_ref=ins[w], dst_ref=outs[w], send_sem=send_sems.at[w], recv_sem=recv_sems.at[w],
                device_id=(x, y, 1 - c), device_id_type=MESH)
            cp.start()
            cps.append(cp)
        for cp in cps:
            cp.wait_recv()
        for cp in cps:
            cp.wait_send()

    return _call(
        body, name=name,
        out_shape=[jax.ShapeDtypeStruct(a.shape, a.dtype) for a in arrs],
        in_specs=[_hbm_spec()] * n, out_specs=[_hbm_spec()] * n,
        scratch_shapes=[pltpu.SemaphoreType.DMA((n,)), pltpu.SemaphoreType.DMA((n,))],
    )(*arrs)


def adamw_pair(pa, pb, w, m, v, name):
    r, c = w.shape
    tr = min(256, r)

    def body(pa_ref, pb_ref, w_ref, m_ref, v_ref, g_ref, d_ref, nm_ref, nv_ref):
        g = pa_ref[...].astype(F32) + pb_ref[...].astype(F32)
        d, nm, nv = _adamw(w_ref[...], g, m_ref[...], v_ref[...])
        g_ref[...] = g
        d_ref[...] = d
        nm_ref[...] = nm
        nv_ref[...] = nv

    spec = pl.BlockSpec((tr, c), lambda i: (i, 0))
    return _call(
        body, name=name, grid=(r // tr,),
        out_shape=[jax.ShapeDtypeStruct((r, c), F32)] * 4,
        in_specs=[spec] * 5, out_specs=[spec] * 4,
        compiler_params=_params(),
    )(pa, pb, w, m, v)


def small_update(gathered, first_row, ws, ms, vs, name):
    n_w = len(ws)
    total_rows = gathered.shape[1]

    def body(*refs):
        g_ref = refs[0]
        w_refs, m_refs, v_refs = refs[1:1 + n_w], refs[1 + n_w:1 + 2 * n_w], refs[1 + 2 * n_w:1 + 3 * n_w]
        tail_ref = refs[1 + 3 * n_w]
        outs = refs[2 + 3 * n_w:2 + 7 * n_w]
        sum_ref = refs[2 + 7 * n_w]
        acc = g_ref[0]
        for k in range(1, N_DEV):
            acc = acc + g_ref[k]
        sum_ref[...] = acc
        row = first_row
        for p in range(n_w):
            a, b = ws[p].shape
            per = b // LANES
            g_out, d_out, m_out, v_out = outs[4 * p:4 * p + 4]
            if per == 1:
                g_out[...] = sum_ref[row:row + a, :]
            else:
                for i in range(a):
                    for jc in range(per):
                        g_out[i:i + 1, jc * LANES:(jc + 1) * LANES] = sum_ref[row + i * per + jc:row + i * per + jc + 1, :]
            row += a * per
            dl, nm, nv = _adamw(w_refs[p][...], g_out[...], m_refs[p][...], v_refs[p][...])
            d_out[...] = dl
            m_out[...] = nm
            v_out[...] = nv
        tail_ref[...] = sum_ref[row:row + 1, :]

    out_shape = [jax.ShapeDtypeStruct((1, LANES), F32)]
    for w in ws:
        out_shape += [jax.ShapeDtypeStruct(w.shape, F32)] * 4
    res = _call(
        body, name=name, out_shape=out_shape,
        scratch_shapes=[pltpu.VMEM((total_rows, LANES), F32)],
        compiler_params=_params(),
    )(gathered, *ws, *ms, *vs)
    return res[0], [res[1 + 4 * p:5 + 4 * p] for p in range(n_w)]


def ada_fwd(c_all, w_ada, b_cols, after, name):
    n_l, d, cols = w_ada.shape
    nb = c_all.shape[0]
    tn = 256

    def body(c_ref, w_ref, b_ref, after_ref, o_ref):
        del after_ref
        cv = c_ref[...]
        ca = (cv * _sigmoid(cv)).astype(BF16)
        o_ref[...] = _dot(ca, w_ref[...].astype(BF16), NN) + b_ref[...]

    return _call(
        body, name=name, grid=(n_l, cols // tn),
        out_shape=jax.ShapeDtypeStruct((n_l, nb, cols), F32),
        in_specs=[pl.BlockSpec((nb, d), lambda l, j: (0, 0)),
                  pl.BlockSpec((None, d, tn), lambda l, j: (l, 0, j)),
                  pl.BlockSpec((None, 1, tn), lambda l, j: (l, 0, j)),
                  pl.BlockSpec(memory_space=pl.ANY)],
        out_specs=pl.BlockSpec((None, nb, tn), lambda l, j: (l, 0, j)),
        compiler_params=_params(),
    )(c_all, w_ada, b_cols, after)


def ada_bwd(c_all, dmod_cols, w, m, v, name):
    n_l, d, cols = w.shape
    nb = c_all.shape[0]
    tn = 256

    def body(c_ref, dm_ref, w_ref, m_ref, v_ref, g_ref, d_ref, nm_ref, nv_ref):
        cv = c_ref[...]
        ca = (cv * _sigmoid(cv)).astype(BF16)
        g = _dot(ca, dm_ref[...].astype(BF16), TN)
        dl, nm, nv = _adamw(w_ref[...], g, m_ref[...], v_ref[...])
        g_ref[...] = g
        d_ref[...] = dl
        nm_ref[...] = nm
        nv_ref[...] = nv

    wspec = pl.BlockSpec((None, d, tn), lambda l, j: (l, 0, j))
    return _call(
        body, name=name, grid=(n_l, cols // tn),
        out_shape=[jax.ShapeDtypeStruct((n_l, d, cols), F32)] * 4,
        in_specs=[pl.BlockSpec((nb, d), lambda l, j: (0, 0)),
                  pl.BlockSpec((None, nb, tn), lambda l, j: (l, 0, j)),
                  wspec, wspec, wspec],
        out_specs=[wspec] * 4,
        compiler_params=_params(),
    )(c_all, dmod_cols, w, m, v)


def bias_update(dmod_all, w, m, v, name):
    def body(dm_ref, w_ref, m_ref, v_ref, g_ref, d_ref, nm_ref, nv_ref):
        g = jnp.sum(dm_ref[...], axis=0, keepdims=True)
        dl, nm, nv = _adamw(w_ref[...], g, m_ref[...], v_ref[...])
        g_ref[...] = g
        d_ref[...] = dl
        nm_ref[...] = nm
        nv_ref[...] = nv

    return _call(
        body, name=name,
        out_shape=[jax.ShapeDtypeStruct(w.shape, F32)] * 4,
        compiler_params=_params(),
    )(dmod_all, w, m, v)


def inproj_fwd(x, mod, ng, wg, seq, sectioned, name):
    m_rows, d = x.shape
    nsh, _, ns = wg.shape
    n = nsh * ns
    tm, tn = min(2 * ROW_TILE, seq), _col_tile(ns)
    per = ns // tn

    def body(x_ref, mod_ref, ng_ref, w_ref, proj_ref, h_ref):
        @pl.when(pl.program_id(1) == 0)
        def _():
            xv = x_ref[...]
            r = lax.rsqrt(jnp.mean(xv * xv, axis=-1, keepdims=True) + EPS)
            md = mod_ref[0]
            h = (xv * r * ng_ref[...]) * (1.0 + md[:, d:2 * d]) + md[:, :d]
            h_ref[...] = h.astype(BF16)
        proj_ref[...] = _dot(h_ref[...], w_ref[...], NN)

    if sectioned:
        proj_shape = (nsh, m_rows, ns)
        proj_spec = pl.BlockSpec((None, tm, tn), lambda i, j: (j // per, i, j % per))
    else:
        proj_shape = (m_rows, n)
        proj_spec = pl.BlockSpec((tm, tn), lambda i, j: (i, j))
    return _call(
        body, name=name, grid=(m_rows // tm, n // tn),
        out_shape=[jax.ShapeDtypeStruct(proj_shape, F32), jax.ShapeDtypeStruct((m_rows, d), BF16)],
        in_specs=[pl.BlockSpec((tm, d), lambda i, j: (i, 0)),
                  pl.BlockSpec((1, 1, 3 * d), lambda i, j: ((i * tm) // seq, 0, 0)),
                  pl.BlockSpec((1, d), lambda i, j: (0, 0)),
                  pl.BlockSpec((None, d, tn), lambda i, j: (j // per, 0, j % per))],
        out_specs=[proj_spec, pl.BlockSpec((tm, d), lambda i, j: (i, 0))],
        compiler_params=_params(),
    )(x, mod, ng, wg)


def outproj_fwd(y, w, x, mod, seq, name):
    m_rows, di = y.shape
    d = w.shape[1]
    tm = min(ROW_TILE, seq)

    def body(y_ref, w_ref, x_ref, mod_ref, xn_ref, out_ref):
        acc = _dot(y_ref[...], w_ref[...], NN)
        out_ref[...] = acc.astype(BF16)
        xn_ref[...] = x_ref[...] + mod_ref[0][:, 2 * d:] * acc

    row = pl.BlockSpec((tm, d), lambda i: (i, 0))
    return _call(
        body, name=name, grid=(m_rows // tm,),
        out_shape=[jax.ShapeDtypeStruct((m_rows, d), F32), jax.ShapeDtypeStruct((m_rows, d), BF16)],
        in_specs=[pl.BlockSpec((tm, di), lambda i: (i, 0)),
                  pl.BlockSpec((di, d), lambda i: (0, 0)),
                  row,
                  pl.BlockSpec((1, 1, 3 * d), lambda i: ((i * tm) // seq, 0, 0))],
        out_specs=[row, row],
        compiler_params=_params(),
    )(y, w, x, mod)


def outproj_bwd(dxo, out, mod, w, seq, name):
    m_rows, d = dxo.shape
    di = w.shape[0]
    nb = m_rows // seq
    tm, tn = min(ROW_TILE, seq), di

    def body(dxo_ref, out_ref, mod_ref, w_ref, dy_ref, dout_ref, dgate_ref):
        i = pl.program_id(0)

        @pl.when(pl.program_id(1) == 0)
        def _():
            dx = dxo_ref[...]
            dout_ref[...] = (mod_ref[0][:, 2 * d:] * dx).astype(BF16)
            part = jnp.sum(dx * out_ref[...].astype(F32), axis=0, keepdims=True)

            @pl.when((i * tm) % seq == 0)
            def _():
                dgate_ref[0] = part

            @pl.when((i * tm) % seq != 0)
            def _():
                dgate_ref[0] = dgate_ref[0] + part

        dy_ref[...] = _dot(dout_ref[...], w_ref[...], NT).astype(BF16)

    row = pl.BlockSpec((tm, d), lambda i, j: (i, 0))
    return _call(
        body, name=name, grid=(m_rows // tm, di // tn),
        out_shape=[jax.ShapeDtypeStruct((m_rows, di), BF16), jax.ShapeDtypeStruct((m_rows, d), BF16),
                   jax.ShapeDtypeStruct((nb, 1, d), F32)],
        in_specs=[row, row,
                  pl.BlockSpec((1, 1, 3 * d), lambda i, j: ((i * tm) // seq, 0, 0)),
                  pl.BlockSpec((tn, d), lambda i, j: (j, 0))],
        out_specs=[pl.BlockSpec((tm, tn), lambda i, j: (i, j)), row,
                   pl.BlockSpec((1, 1, d), lambda i, j: ((i * tm) // seq, 0, 0))],
        compiler_params=_params(),
    )(dxo, out, mod, w)


def grad_w_out(y, dout, name):
    m_rows, di = y.shape
    d = dout.shape[1]
    tm, tk = min(ROW_TILE, m_rows), di
    n_m = m_rows // tm

    def body(y_ref, do_ref, o_ref, acc_ref):
        mi = pl.program_id(1)

        @pl.when(mi == 0)
        def _():
            acc_ref[...] = jnp.zeros_like(acc_ref)

        acc_ref[...] += _dot(y_ref[...], do_ref[...], TN)

        @pl.when(mi == n_m - 1)
        def _():
            o_ref[...] = acc_ref[...].astype(BF16)

    return _call(
        body, name=name, grid=(di // tk, n_m),
        out_shape=jax.ShapeDtypeStruct((di, d), BF16),
        in_specs=[pl.BlockSpec((tm, tk), lambda j, mi: (mi, j)),
                  pl.BlockSpec((tm, d), lambda j, mi: (mi, 0))],
        out_specs=pl.BlockSpec((tk, d), lambda j, mi: (j, 0)),
        scratch_shapes=[pltpu.VMEM((tk, d), F32)],
        compiler_params=_params(),
    )(y, dout)


def grad_w_in(h, dproj, nsh, sectioned, name):
    m_rows, d = h.shape
    n = dproj.shape[0] * dproj.shape[2] if sectioned else dproj.shape[1]
    ns = n // nsh
    tm, tn = min(ROW_TILE, m_rows), ns
    per = ns // tn
    n_m = m_rows // tm

    def body(h_ref, dp_ref, o_ref, acc_ref):
        mi = pl.program_id(1)
        @pl.when(mi == 0)
        def _():
            acc_ref[...] = jnp.zeros_like(acc_ref)

        acc_ref[...] += _dot(h_ref[...], dp_ref[...], TN)

        @pl.when(mi == n_m - 1)
        def _():
            o_ref[...] = acc_ref[...].astype(BF16)

    if sectioned:
        dp_spec = pl.BlockSpec((None, tm, tn), lambda j, mi: (j // per, mi, j % per))
    else:
        dp_spec = pl.BlockSpec((tm, tn), lambda j, mi: (mi, j))
    return _call(
        body, name=name, grid=(n // tn, n_m),
        out_shape=jax.ShapeDtypeStruct((nsh, d, ns), BF16),
        in_specs=[pl.BlockSpec((tm, d), lambda j, mi: (mi, 0)), dp_spec],
        out_specs=pl.BlockSpec((None, d, tn), lambda j, mi: (j // per, 0, j % per)),
        scratch_shapes=[pltpu.VMEM((d, tn), F32)],
        compiler_params=_params(),
    )(h, dproj)


def inproj_bwd(dproj, wg, x, dxo, mod, ng, seq, sectioned, name):
    m_rows, d = x.shape
    nsh, _, ns = wg.shape
    n = nsh * ns
    nb = m_rows // seq
    tm, tk = min(ROW_TILE, seq), ns
    per = ns // tk
    n_k = n // tk

    def body(dp_ref, w_ref, x_ref, dxo_ref, mod_ref, ng_ref, dxi_ref, dsh_ref, dsc_ref, dng_ref, acc_ref):
        i, k = pl.program_id(0), pl.program_id(1)
        @pl.when(k == 0)
        def _():
            acc_ref[...] = jnp.zeros_like(acc_ref)

        acc_ref[...] += _dot(dp_ref[...], w_ref[...], NT)

        @pl.when(k == n_k - 1)
        def _():
            dh = acc_ref[...]
            xv = x_ref[...]
            r = lax.rsqrt(jnp.mean(xv * xv, axis=-1, keepdims=True) + EPS)
            xn = xv * r
            md = mod_ref[0]
            gain = ng_ref[...]
            p_shift = jnp.sum(dh, axis=0, keepdims=True)
            p_scale = jnp.sum(dh * (xn * gain), axis=0, keepdims=True)
            drn = dh * (1.0 + md[:, d:2 * d])
            p_ng = jnp.sum(drn * xn, axis=0, keepdims=True)
            dxn = drn * gain
            dx = r * (dxn - xn * jnp.mean(dxn * xn, axis=-1, keepdims=True))
            dxi_ref[...] = dxo_ref[...] + dx

            @pl.when((i * tm) % seq == 0)
            def _():
                dsh_ref[0] = p_shift
                dsc_ref[0] = p_scale

            @pl.when((i * tm) % seq != 0)
            def _():
                dsh_ref[0] = dsh_ref[0] + p_shift
                dsc_ref[0] = dsc_ref[0] + p_scale

            @pl.when(i == 0)
            def _():
                dng_ref[...] = p_ng

            @pl.when(i != 0)
            def _():
                dng_ref[...] = dng_ref[...] + p_ng

    if sectioned:
        dp_spec = pl.BlockSpec((None, tm, tk), lambda i, k: (k // per, i, k % per))
    else:
        dp_spec = pl.BlockSpec((tm, tk), lambda i, k: (i, k))
    row = pl.BlockSpec((tm, d), lambda i, k: (i, 0))
    per_seq = pl.BlockSpec((1, 1, d), lambda i, k: ((i * tm) // seq, 0, 0))
    return _call(
        body, name=name, grid=(m_rows // tm, n_k),
        out_shape=[jax.ShapeDtypeStruct((m_rows, d), F32), jax.ShapeDtypeStruct((nb, 1, d), F32),
                   jax.ShapeDtypeStruct((nb, 1, d), F32), jax.ShapeDtypeStruct((1, d), F32)],
        in_specs=[dp_spec,
                  pl.BlockSpec((None, d, tk), lambda i, k: (k // per, 0, k % per)),
                  row, row,
                  pl.BlockSpec((1, 1, 3 * d), lambda i, k: ((i * tm) // seq, 0, 0)),
                  pl.BlockSpec((1, d), lambda i, k: (0, 0))],
        out_specs=[row, per_seq, per_seq, pl.BlockSpec((1, d), lambda i, k: (0, 0))],
        scratch_shapes=[pltpu.VMEM((tm, d), F32)],
        compiler_params=_params(),
    )(dproj, wg, x, dxo, mod, ng)


def _sgu_stats(proj_ref, vg_ref, di, gd, dgel_ref=None):
    s1 = jnp.zeros((SG_BLOCK, 1), F32)
    for g in range(SG_GROUPS):
        v_pre = proj_ref[:, di + g * gd:di + (g + 1) * gd]
        if dgel_ref is None:
            vg = _gelu(v_pre)
        else:
            vg, dgel_ref[:, g * gd:(g + 1) * gd] = _gelu_and_grad(v_pre)
        vg_ref[:, g * gd:(g + 1) * gd] = vg
        s1 = s1 + jnp.sum(vg, axis=1, keepdims=True)
    mu = s1 / di
    s2 = jnp.zeros((SG_BLOCK, 1), F32)
    for g in range(SG_GROUPS):
        dv = vg_ref[:, g * gd:(g + 1) * gd] - mu
        s2 = s2 + jnp.sum(dv * dv, axis=1, keepdims=True)
    return mu, lax.rsqrt(s2 / di + EPS)


def sgu_fwd(proj, ln_gain, ln_bias, ws, bs, name):
    m_rows, n3 = proj.shape
    di = n3 // 3
    gd = di // SG_GROUPS
    n_blocks = m_rows // SG_BLOCK
    per_step = SG_STEP_BLOCKS if n_blocks % SG_STEP_BLOCKS == 0 else 1

    def body(proj_ref, lg_ref, lb_ref, ws_ref, bs_ref, y_ref, wsm_ref, vg_ref):
        @pl.when(pl.program_id(0) == 0)
        def _():
            mask = _chunk_mask()
            for g in range(SG_GROUPS):
                wsm_ref[g] = jnp.where(mask, ws_ref[g], 0.0).astype(BF16)

        for blk in range(per_step):
            p_ref, o_ref = proj_ref.at[blk], y_ref.at[blk]
            mu, rstd = _sgu_stats(p_ref, vg_ref, di, gd)
            for g in range(SG_GROUPS):
                cs = slice(g * gd, (g + 1) * gd)
                vln = (vg_ref[:, cs] - mu) * rstd * lg_ref[:, cs] + lb_ref[:, cs]
                s = _dot(wsm_ref[g], vln.astype(BF16), NN) + bs_ref[g]
                u = _gelu(p_ref[:, cs])
                gp = p_ref[:, 2 * di + g * gd:2 * di + (g + 1) * gd]
                o_ref[:, cs] = (u * s * (gp * _sigmoid(gp))).astype(BF16)

    full = lambda shape: pl.BlockSpec(shape, lambda i: (0,) * len(shape))
    return _call(
        body, name=name, grid=(n_blocks // per_step,),
        out_shape=jax.ShapeDtypeStruct((n_blocks, SG_BLOCK, di), BF16),
        in_specs=[pl.BlockSpec((per_step, SG_BLOCK, n3), lambda i: (i, 0, 0)),
                  full((1, di)), full((1, di)),
                  full((SG_GROUPS, SG_BLOCK, SG_BLOCK)), full((SG_GROUPS, SG_BLOCK, 1))],
        out_specs=pl.BlockSpec((per_step, SG_BLOCK, di), lambda i: (i, 0, 0)),
        scratch_shapes=[pltpu.VMEM((SG_GROUPS, SG_BLOCK, SG_BLOCK), BF16), pltpu.VMEM((SG_BLOCK, di), F32)],
        compiler_params=_params(),
    )(proj.reshape(n_blocks, SG_BLOCK, n3), ln_gain, ln_bias, ws, bs).reshape(m_rows, di)


def sgu_bwd(proj, dy, ln_gain, ln_bias, ws, bs, name):
    m_rows, n3 = proj.shape
    di = n3 // 3
    gd = di // SG_GROUPS
    n_i = m_rows // SG_BLOCK

    def body(proj_ref, dy_ref, lg_ref, lb_ref, ws_ref, bs_ref,
             dp_ref, dws_ref, dbs_ref, dlg_ref, dlb_ref, wsm_ref, vg_ref, dvh_ref, dgel_ref):
        i = pl.program_id(0)

        def before():
            @pl.when(i == 0)
            def _():
                mask = _chunk_mask()
                for g in range(SG_GROUPS):
                    wsm_ref[g] = jnp.where(mask, ws_ref[g], 0.0).astype(BF16)
                dws_ref[...] = jnp.zeros_like(dws_ref)
                dbs_ref[...] = jnp.zeros_like(dbs_ref)
                dlg_ref[...] = jnp.zeros_like(dlg_ref)
                dlb_ref[...] = jnp.zeros_like(dlb_ref)

        def after():
            @pl.when(i == n_i - 1)
            def _():
                mask = _chunk_mask()
                for g in range(SG_GROUPS):
                    dws_ref[g] = jnp.where(mask, dws_ref[g], 0.0)

        before()
        mu, rstd = _sgu_stats(proj_ref, vg_ref, di, gd, dgel_ref)
        m1 = jnp.zeros((SG_BLOCK, 1), F32)
        m2 = jnp.zeros((SG_BLOCK, 1), F32)
        for g in range(SG_GROUPS):
            cs = slice(g * gd, (g + 1) * gd)
            gs = slice(2 * di + g * gd, 2 * di + (g + 1) * gd)
            gain = lg_ref[:, cs]
            vhat = (vg_ref[:, cs] - mu) * rstd
            vln_b = (vhat * gain + lb_ref[:, cs]).astype(BF16)
            s = _dot(wsm_ref[g], vln_b, NN) + bs_ref[g]
            u, du = _gelu_and_grad(proj_ref[:, cs])
            sg, dsg = _silu_and_grad(proj_ref[:, gs])
            dyv = dy_ref[:, cs].astype(F32)
            dp_ref[:, cs] = (dyv * s * sg * du).astype(BF16)
            dp_ref[:, gs] = (dyv * u * s * dsg).astype(BF16)
            ds = dyv * u * sg
            ds_b = ds.astype(BF16)
            dws_ref[g] = dws_ref[g] + _dot(ds_b, vln_b, NT)
            dbs_ref[g] = dbs_ref[g] + jnp.sum(ds, axis=1, keepdims=True)
            dvln = _dot(wsm_ref[g], ds_b, TN)
            dlg_ref[:, cs] = dlg_ref[:, cs] + jnp.sum(dvln * vhat, axis=0, keepdims=True)
            dlb_ref[:, cs] = dlb_ref[:, cs] + jnp.sum(dvln, axis=0, keepdims=True)
            dvh = dvln * gain
            dvh_ref[:, cs] = dvh
            m1 = m1 + jnp.sum(dvh, axis=1, keepdims=True)
            m2 = m2 + jnp.sum(dvh * vhat, axis=1, keepdims=True)
        m1 = m1 / di
        m2 = m2 / di
        for g in range(SG_GROUPS):
            cs = slice(g * gd, (g + 1) * gd)
            vs = slice(di + g * gd, di + (g + 1) * gd)
            vhat = (vg_ref[:, cs] - mu) * rstd
            dvg = rstd * (dvh_ref[:, cs] - m1 - vhat * m2)
            dp_ref[:, vs] = (dvg * dgel_ref[:, cs]).astype(BF16)

        after()

    full = lambda shape: pl.BlockSpec(shape, lambda i: (0,) * len(shape))
    return _call(
        body, name=name, grid=(n_i,),
        out_shape=[jax.ShapeDtypeStruct((m_rows, n3), BF16),
                   jax.ShapeDtypeStruct((SG_GROUPS, SG_BLOCK, SG_BLOCK), F32),
                   jax.ShapeDtypeStruct((SG_GROUPS, SG_BLOCK, 1), F32),
                   jax.ShapeDtypeStruct((1, di), F32), jax.ShapeDtypeStruct((1, di), F32)],
        in_specs=[pl.BlockSpec((SG_BLOCK, n3), lambda i: (i, 0)),
                  pl.BlockSpec((SG_BLOCK, di), lambda i: (i, 0)),
                  full((1, di)), full((1, di)),
                  full((SG_GROUPS, SG_BLOCK, SG_BLOCK)), full((SG_GROUPS, SG_BLOCK, 1))],
        out_specs=[pl.BlockSpec((SG_BLOCK, n3), lambda i: (i, 0)),
                   full((SG_GROUPS, SG_BLOCK, SG_BLOCK)), full((SG_GROUPS, SG_BLOCK, 1)),
                   full((1, di)), full((1, di))],
        scratch_shapes=[pltpu.VMEM((SG_GROUPS, SG_BLOCK, SG_BLOCK), BF16),
                        pltpu.VMEM((SG_BLOCK, di), F32), pltpu.VMEM((SG_BLOCK, di), F32),
                        pltpu.VMEM((SG_BLOCK, di), F32)],
        compiler_params=_params(),
    )(proj, dy, ln_gain, ln_bias, ws, bs)


def _lower_bound(lbraw):
    mx = jnp.maximum(lbraw[0:1, :], lbraw[1:2, :])
    e0 = jnp.exp(lbraw[0:1, :] - mx)
    e1 = jnp.exp(lbraw[1:2, :] - mx)
    p0 = e0 / (e0 + e1)
    p1 = e1 / (e0 + e1)
    return (p0 + p1) - p0, p0, p1


def _tri(lower):
    r = lax.broadcasted_iota(jnp.int32, (CHUNK, CHUNK), 0)
    c = lax.broadcasted_iota(jnp.int32, (CHUNK, CHUNK), 1)
    return ((r >= c) if lower else (c >= r)).astype(BF16)


def _running_sum(tri, x):
    x1 = x.astype(BF16)
    r1 = x - x1.astype(F32)
    x2 = r1.astype(BF16)
    x3 = (r1 - x2.astype(F32)).astype(BF16)
    return _dot(tri, x1, NN) + _dot(tri, x2, NN) + _dot(tri, x3, NN)


def _row(a, idx):
    r = lax.broadcasted_iota(jnp.int32, a.shape, 0)
    return jnp.sum(jnp.where(r == idx, a, 0.0), axis=0, keepdims=True)


def _hgrn_gates(qp, fp, lb, tri):
    sgm = _sigmoid_small(fp)
    f = lb + (1.0 - lb) * sgm
    k = 1.0 - f
    a = _running_sum(tri, jnp.log(f))
    a_mid = _row(a, CHUNK // 2 - 1)
    a_last = _row(a, CHUNK - 1)
    q, dq = _silu_and_grad(qp)
    e1, e2, e3, e4 = jnp.exp(a - a_mid), jnp.exp(a_mid - a), jnp.exp(a), jnp.exp(a_last - a)
    return dict(sgm=sgm, f=f, k=k, q=q, dq=dq, e1=e1, e2=e2, e3=e3, e4=e4, dec=jnp.exp(a_last),
                q_in=q * e1, k_in=k * e2, q_out=q * e3, k_out=k * e4)


def _causal():
    r = lax.broadcasted_iota(jnp.int32, (CHUNK, CHUNK), 0)
    c = lax.broadcasted_iota(jnp.int32, (CHUNK, CHUNK), 1)
    return r >= c


def hgrn_fwd(proj4, lbraw, gn, seq, name):
    _, m_rows, di = proj4.shape
    nb, nh, nc = m_rows // seq, di // HEAD_DIM, seq // CHUNK
    rows = min(HG_ROWS, seq)
    wide = HG_WIDE * HEAD_DIM
    ns, cpb = seq // rows, rows // CHUNK

    def body(p_ref, lb_ref, gn_ref, y_ref, sts_ref, st_ref):
        @pl.when(pl.program_id(2) == 0)
        def _():
            st_ref[...] = jnp.zeros_like(st_ref)

        tri = _tri(True)
        causal = _causal()
        gain = gn_ref[...]
        lbs = [_lower_bound(lb_ref[:, j * HEAD_DIM:(j + 1) * HEAD_DIM])[0] for j in range(HG_WIDE)]

        units = [(n, j) for n in range(cpb) for j in range(HG_WIDE)]
        rs = lambda n: slice(n * CHUNK, (n + 1) * CHUNK)
        cs = lambda j: slice(j * HEAD_DIM, (j + 1) * HEAD_DIM)
        gates, v_b, sc_b, kv, o_in, o_x = {}, {}, {}, {}, {}, {}
        for n, j in units:
            gates[n, j] = _hgrn_gates(p_ref[0, rs(n), cs(j)], p_ref[1, rs(n), cs(j)], lbs[j], tri)
            v_b[n, j] = p_ref[2, rs(n), cs(j)].astype(BF16)
        for u in units:
            t = gates[u]
            sc_b[u] = jnp.where(causal, _dot(t["q_in"].astype(BF16), t["k_in"].astype(BF16), NT), 0.0).astype(BF16)
            kv[u] = _dot(v_b[u], t["k_out"].astype(BF16), TN)
        for u in units:
            o_in[u] = _dot(sc_b[u], v_b[u], NN)
        for j in range(HG_WIDE):
            st = st_ref[j]
            for n in range(cpb):
                sts_ref[n, :, cs(j)] = st
                o_x[n, j] = _dot(gates[n, j]["q_out"].astype(BF16), st.astype(BF16), NT)
                st = st * gates[n, j]["dec"] + kv[n, j]
            st_ref[j] = st
        for n, j in units:
            o = o_in[n, j] + o_x[n, j]
            r = lax.rsqrt(jnp.mean(o * o, axis=-1, keepdims=True) + EPS)
            gp = p_ref[3, rs(n), cs(j)]
            y_ref[rs(n), cs(j)] = ((o * r * gain) * (gp * _sigmoid(gp))).astype(BF16)

    return _call(
        body, name=name, grid=(nh // HG_WIDE, nb, ns),
        out_shape=[jax.ShapeDtypeStruct((m_rows, di), BF16),
                   jax.ShapeDtypeStruct((nb * nc, HEAD_DIM, di), F32)],
        in_specs=[pl.BlockSpec((4, rows, wide), lambda hg, b, s: (0, b * ns + s, hg)),
                  pl.BlockSpec((2, wide), lambda hg, b, s: (0, hg)),
                  pl.BlockSpec((1, HEAD_DIM), lambda hg, b, s: (0, 0))],
        out_specs=[pl.BlockSpec((rows, wide), lambda hg, b, s: (b * ns + s, hg)),
                   pl.BlockSpec((cpb, HEAD_DIM, wide), lambda hg, b, s: (b * ns + s, 0, hg))],
        scratch_shapes=[pltpu.VMEM((HG_WIDE, HEAD_DIM, HEAD_DIM), F32)],
        compiler_params=_params(),
    )(proj4, lbraw, gn)


def hgrn_bwd(proj4, dy, sts, lbraw, gn, seq, name):
    _, m_rows, di = proj4.shape
    nb, nh, nc = m_rows // seq, di // HEAD_DIM, seq // CHUNK
    rows = min(HG_ROWS, seq)
    wide = HG_WIDE * HEAD_DIM
    ns, cpb = seq // rows, rows // CHUNK
    n_hg = nh // HG_WIDE

    def body(p_ref, dy_ref, sts_ref, lb_ref, gn_ref, dp_ref, dlb_ref, dgn_ref, dst_ref, lbacc_ref, gnacc_ref):
        hg, b, s = pl.program_id(0), pl.program_id(1), pl.program_id(2)
        tri, triu = _tri(True), _tri(False)
        causal = _causal()
        gain = gn_ref[...]
        first = (b == 0) & (s == 0)
        cs = lambda j: slice(j * HEAD_DIM, (j + 1) * HEAD_DIM)

        def before():
            @pl.when((hg == 0) & first)
            def _():
                gnacc_ref[...] = jnp.zeros_like(gnacc_ref)

            @pl.when(first)
            def _():
                lbacc_ref[...] = jnp.zeros_like(lbacc_ref)

            @pl.when(s == 0)
            def _():
                dst_ref[...] = jnp.zeros_like(dst_ref)

        def after():
            @pl.when((b == nb - 1) & (s == ns - 1))
            def _():
                for j in range(HG_WIDE):
                    _, p0, p1 = _lower_bound(lb_ref[:, cs(j)])
                    acc = lbacc_ref[:, cs(j)]
                    dlb_ref[0:1, cs(j)] = -acc * p0 * p1
                    dlb_ref[1:2, cs(j)] = acc * p1 * (1.0 - p1)

            @pl.when((hg == n_hg - 1) & (b == nb - 1) & (s == ns - 1))
            def _():
                tot = gnacc_ref[:, 0:HEAD_DIM]
                for j in range(1, HG_WIDE):
                    tot = tot + gnacc_ref[:, cs(j)]
                dgn_ref[...] = tot

        before()

        units = [(n, j) for n in range(cpb) for j in range(HG_WIDE)]
        rs = lambda n: slice(n * CHUNK, (n + 1) * CHUNK)
        lbs = [_lower_bound(lb_ref[:, cs(j)])[0] for j in range(HG_WIDE)]
        gates, v_b, st_b, sc_b, o, do_b = {}, {}, {}, {}, {}, {}
        dq_out, dsc_b, dv, g_st, dq_in, dk_in, dst_at, dk_out, ddec = {}, {}, {}, {}, {}, {}, {}, {}, {}
        for n, j in units:
            gates[n, j] = _hgrn_gates(p_ref[0, rs(n), cs(j)], p_ref[1, rs(n), cs(j)], lbs[j], tri)
            v_b[n, j] = p_ref[2, rs(n), cs(j)].astype(BF16)
            st_b[n, j] = sts_ref[n, :, cs(j)].astype(BF16)
        for u in units:
            t = gates[u]
            sc_b[u] = jnp.where(causal, _dot(t["q_in"].astype(BF16), t["k_in"].astype(BF16), NT), 0.0).astype(BF16)
        for u in units:
            o[u] = _dot(sc_b[u], v_b[u], NN) + _dot(gates[u]["q_out"].astype(BF16), st_b[u], NT)
        for n, j in units:
            ov = o[n, j]
            r = lax.rsqrt(jnp.mean(ov * ov, axis=-1, keepdims=True) + EPS)
            ohat = ov * r
            sg, dsg = _silu_and_grad(p_ref[3, rs(n), cs(j)])
            dyv = dy_ref[rs(n), cs(j)].astype(F32)
            dp_ref[3, rs(n), cs(j)] = (dyv * (ohat * gain) * dsg).astype(BF16)
            d_on = dyv * sg
            gnacc_ref[:, cs(j)] = gnacc_ref[:, cs(j)] + jnp.sum(d_on * ohat, axis=0, keepdims=True)
            dohat = d_on * gain
            do_b[n, j] = (r * (dohat - ohat * jnp.mean(dohat * ohat, axis=-1, keepdims=True))).astype(BF16)
        for u in units:
            dq_out[u] = _dot(do_b[u], st_b[u], NN)
            dsc_b[u] = jnp.where(causal, _dot(do_b[u], v_b[u], NT), 0.0).astype(BF16)
            dv[u] = _dot(sc_b[u], do_b[u], TN)
            g_st[u] = _dot(do_b[u], gates[u]["q_out"].astype(BF16), TN)
        for u in units:
            dq_in[u] = _dot(dsc_b[u], gates[u]["k_in"].astype(BF16), NN)
            dk_in[u] = _dot(dsc_b[u], gates[u]["q_in"].astype(BF16), TN)
        for j in range(HG_WIDE):
            dst = dst_ref[j]
            for n in reversed(range(cpb)):
                dst_at[n, j] = dst
                dst = dst * gates[n, j]["dec"] + g_st[n, j]
            dst_ref[j] = dst
        for n, j in units:
            dst = dst_at[n, j]
            dst_b = dst.astype(BF16)
            dk_out[n, j] = _dot(v_b[n, j], dst_b, NN)
            dv[n, j] = dv[n, j] + _dot(gates[n, j]["k_out"].astype(BF16), dst_b, NT)
            ddec[n, j] = jnp.sum(dst * sts_ref[n, :, cs(j)], axis=0, keepdims=True)
        for n, j in units:
            t = gates[n, j]
            dp_ref[2, rs(n), cs(j)] = dv[n, j].astype(BF16)
            dq = dq_in[n, j] * t["e1"] + dq_out[n, j] * t["e3"]
            dk = dk_in[n, j] * t["e2"] + dk_out[n, j] * t["e4"]
            w_in = dq_in[n, j] * t["q_in"] - dk_in[n, j] * t["k_in"]
            w_out = dk_out[n, j] * t["k_out"]
            da = w_in + dq_out[n, j] * t["q_out"] - w_out
            da_mid = -jnp.sum(w_in, axis=0, keepdims=True)
            da_last = jnp.sum(w_out, axis=0, keepdims=True) + ddec[n, j] * t["dec"]
            rid = lax.broadcasted_iota(jnp.int32, da.shape, 0)
            da = da + jnp.where(rid == CHUNK // 2 - 1, da_mid, 0.0) + jnp.where(rid == CHUNK - 1, da_last, 0.0)
            dlf = _running_sum(triu, da)
            df = dlf / t["f"] - dk
            sgm = t["sgm"]
            dp_ref[1, rs(n), cs(j)] = (df * (1.0 - lbs[j]) * sgm * (1.0 - sgm)).astype(BF16)
            lbacc_ref[:, cs(j)] = lbacc_ref[:, cs(j)] + jnp.sum(df * (1.0 - sgm), axis=0, keepdims=True)
            dp_ref[0, rs(n), cs(j)] = (dq * t["dq"]).astype(BF16)

        after()

    blk = lambda hg, b, s: b * ns + (ns - 1 - s)
    return _call(
        body, name=name, grid=(n_hg, nb, ns),
        out_shape=[jax.ShapeDtypeStruct((4, m_rows, di), BF16), jax.ShapeDtypeStruct((2, di), F32),
                   jax.ShapeDtypeStruct((1, HEAD_DIM), F32)],
        in_specs=[pl.BlockSpec((4, rows, wide), lambda hg, b, s: (0, blk(hg, b, s), hg)),
                  pl.BlockSpec((rows, wide), lambda hg, b, s: (blk(hg, b, s), hg)),
                  pl.BlockSpec((cpb, HEAD_DIM, wide), lambda hg, b, s: (blk(hg, b, s), 0, hg)),
                  pl.BlockSpec((2, wide), lambda hg, b, s: (0, hg)),
                  pl.BlockSpec((1, HEAD_DIM), lambda hg, b, s: (0, 0))],
        out_specs=[pl.BlockSpec((4, rows, wide), lambda hg, b, s: (0, blk(hg, b, s), hg)),
                   pl.BlockSpec((2, wide), lambda hg, b, s: (0, hg)),
                   pl.BlockSpec((1, HEAD_DIM), lambda hg, b, s: (0, 0))],
        scratch_shapes=[pltpu.VMEM((HG_WIDE, HEAD_DIM, HEAD_DIM), F32), pltpu.VMEM((1, wide), F32),
                        pltpu.VMEM((1, wide), F32)],
        compiler_params=_params(),
    )(proj4, dy, sts, lbraw, gn)


def outproj_loss(y, w, x, mod, fg, target, seq, name):
    m_rows, di = y.shape
    d = w.shape[1]
    tm = min(512, seq)

    def body(y_ref, w_ref, x_ref, mod_ref, fg_ref, t_ref, out_ref, loss_ref, dx_ref, dfg_ref):
        i = pl.program_id(0)
        acc = _dot(y_ref[...], w_ref[...], NN)
        out_ref[...] = acc.astype(BF16)
        xv = x_ref[...] + mod_ref[0][:, 2 * d:] * acc
        gain = fg_ref[...]
        r = lax.rsqrt(jnp.mean(xv * xv, axis=-1, keepdims=True) + EPS)
        xn = xv * r
        e = xn * gain - t_ref[...]
        part = 0.5 * jnp.sum(jnp.mean(e * e, axis=-1, keepdims=True), axis=0, keepdims=True)
        dyv = e / d
        p_fg = jnp.sum(dyv * xn, axis=0, keepdims=True)
        dxn = dyv * gain
        dx_ref[...] = r * (dxn - xn * jnp.mean(dxn * xn, axis=-1, keepdims=True))

        @pl.when(i == 0)
        def _():
            loss_ref[...] = part
            dfg_ref[...] = p_fg

        @pl.when(i != 0)
        def _():
            loss_ref[...] = loss_ref[...] + part
            dfg_ref[...] = dfg_ref[...] + p_fg

    row = pl.BlockSpec((tm, d), lambda i: (i, 0))
    return _call(
        body, name=name, grid=(m_rows // tm,),
        out_shape=[jax.ShapeDtypeStruct((m_rows, d), BF16), jax.ShapeDtypeStruct((1, 1), F32),
                   jax.ShapeDtypeStruct((m_rows, d), F32), jax.ShapeDtypeStruct((1, d), F32)],
        in_specs=[pl.BlockSpec((tm, di), lambda i: (i, 0)),
                  pl.BlockSpec((di, d), lambda i: (0, 0)),
                  row,
                  pl.BlockSpec((1, 1, 3 * d), lambda i: ((i * tm) // seq, 0, 0)),
                  pl.BlockSpec((1, d), lambda i: (0, 0)), row],
        out_specs=[row, pl.BlockSpec((1, 1), lambda i: (0, 0)), row, pl.BlockSpec((1, d), lambda i: (0, 0))],
        compiler_params=_params(),
    )(y, w, x, mod, fg, target)


def _pack(parts):
    flat = jnp.concatenate([p.reshape(-1) for p in parts])
    pad = (-flat.shape[0]) % (8 * LANES)
    return jnp.pad(flat, (0, pad)).reshape(-1, LANES)


def kernel(x, c, norm_gain, w_ada, b_ada, a_w_in, a_ln_gain, a_ln_bias, a_w_s, a_b_s, a_w_out, b_w_in, b_lower_bounds, b_gn_gain, b_w_out, final_gain, loss_target, m_norm_gain, m_w_ada, m_b_ada, m_a_w_in, m_a_ln_gain, m_a_ln_bias, m_a_w_s, m_a_b_s, m_a_w_out, m_b_w_in, m_b_lower_bounds, m_b_gn_gain, m_b_w_out, m_final_gain, v_norm_gain, v_w_ada, v_b_ada, v_a_w_in, v_a_ln_gain, v_a_ln_bias, v_a_w_s, v_a_b_s, v_a_w_out, v_b_w_in, v_b_lower_bounds, v_b_gn_gain, v_b_w_out, v_final_gain):
    nb, seq, d = x.shape
    m_rows = nb * seq
    n_l = w_ada.shape[0]
    ada_cols = w_ada.shape[2]
    px, py, pc = _place()
    chip = 2 * px + py
    dev = 2 * chip + pc

    c_all = allgather_small(c.reshape(-1, LANES), "gather_c").reshape(N_DEV * nb, d)
    s_a = halves_start([cast_into_slot(a_w_in[0], chip, c_all, "cast_a_in"),
                        cast_into_slot(a_w_out[0], chip, c_all, "cast_a_out")], "gather_a_start")
    land_b_in = cast_into_slot(b_w_in[0], chip, s_a[3], "cast_b_in")
    land_b_out = cast_into_slot(b_w_out[0], chip, land_b_in, "cast_b_out")
    b_cols = lax.dynamic_slice_in_dim(b_ada, chip * ada_cols, ada_cols, axis=1).reshape(n_l, 1, ada_cols)
    mod_cols = ada_fwd(c_all, w_ada, b_cols, land_b_out, "ada_fwd")
    mod_g = allgather_small(mod_cols.reshape(-1, LANES), "gather_mod")
    mod_g = mod_g.reshape(N_CHIPS, 2, n_l, N_DEV * nb, ada_cols)[:, 0]
    mod_all = jnp.transpose(mod_g, (1, 2, 0, 3)).reshape(n_l, N_DEV * nb, 3 * d)
    mod_mine = lax.dynamic_slice_in_dim(mod_all, dev * nb, nb, axis=1)
    mod0 = mod_mine[0].reshape(nb, 1, 3 * d)
    mod1 = mod_mine[1].reshape(nb, 1, 3 * d)

    landed_a = halves_wait(s_a[0], s_a[1], s_a[2], mod_mine, "gather_a_wait")
    s_bi = gather_start(land_b_in, landed_a[0], "gather_b_in_start")
    s_bo = gather_start(land_b_out, s_bi[3], "gather_b_out_start")
    wa_in, wa_out = pass_halves(list(landed_a), "gather_a_pass")
    di = a_w_out.shape[1] * N_CHIPS
    wa_out = wa_out.reshape(di, d)

    x0 = x.reshape(m_rows, d)
    tgt = loss_target.reshape(m_rows, d)
    ng0 = norm_gain[0:1] + (s_bi[3][0, 0] + s_bo[3][0, 0])
    ng1 = norm_gain[1:2]
    bs_col = a_b_s[0].reshape(SG_GROUPS, SG_BLOCK, 1)
    proj_a, h_a = inproj_fwd(x0, mod0, ng0, wa_in, seq, False, "a_inproj")
    y_a = sgu_fwd(proj_a, a_ln_gain, a_ln_bias, a_w_s[0], bs_col, "a_sgu")
    x1, out_a = outproj_fwd(y_a, wa_out, x0, mod0, seq, "a_outproj")
    wb_in = gather_wait(*s_bi[:3], out_a, "gather_b_in_wait")
    proj_b, h_b = inproj_fwd(x1, mod1, ng1, wb_in, seq, True, "b_inproj")
    y_b, sts_b = hgrn_fwd(proj_b, b_lower_bounds, b_gn_gain, seq, "b_hgrn")
    wb_out = gather_wait(*s_bo[:3], y_b, "gather_b_out_wait").reshape(di, d)
    out_b, loss_part, dx2, dfg = outproj_loss(
        y_b, wb_out, x1, mod1, final_gain.reshape(1, d), tgt, seq, "b_outproj_loss")

    shard_rows = di // N_CHIPS
    dy_b, dout_b, dgate1 = outproj_bwd(dx2, out_b, mod1, wb_out, seq, "b_outproj_bwd")
    gwb_out = grad_w_out(y_b, dout_b, "b_grad_w_out").reshape(N_CHIPS, shard_rows, d)
    e_bo = exchange_start(gwb_out, "exchange_b_out_start")
    dproj_b, dlb, dgn = hgrn_bwd(
        proj_b, dy_b, sts_b, b_lower_bounds, b_gn_gain + e_bo[4][0, 0], seq, "b_hgrn_bwd")
    e_bi = exchange_start(grad_w_in(h_b, dproj_b, N_CHIPS, True, "b_grad_w_in"), "exchange_b_in_start")
    dx1, dshift1, dscale1, dng1 = inproj_bwd(
        dproj_b, wb_in, x1, dx2, mod1, ng1 + e_bi[4][0, 0], seq, True, "b_inproj_bwd")

    dy_a, dout_a, dgate0 = outproj_bwd(dx1, out_a, mod0, wa_out, seq, "a_outproj_bwd")
    gwa_out = grad_w_out(y_a, dout_a, "a_grad_w_out").reshape(N_CHIPS, shard_rows, d)
    e_ao = exchange_start(gwa_out, "exchange_a_out_start")
    dproj_a, dws, dbs, dlg, dlbias = sgu_bwd(
        proj_a, dy_a, a_ln_gain + e_ao[4][0, 0], a_ln_bias, a_w_s[0], bs_col, "a_sgu_bwd")
    e_ai = exchange_start(grad_w_in(h_a, dproj_a, N_CHIPS, False, "a_grad_w_in"), "exchange_a_in_start")
    dx0, dshift0, dscale0, dng0 = inproj_bwd(
        dproj_a, wa_in, x0, dx1, mod0, norm_gain[0:1] + e_ai[4][0, 0], seq, False, "a_inproj_bwd")
    grad_x = dx0.reshape(nb, seq, d)

    dmod = jnp.concatenate([dshift0, dscale0, dgate0, dshift1, dscale1, dgate1], axis=2)
    n_dmod = dmod.size
    small_g = [jnp.concatenate([dng0, dng1], axis=0), dlg, dlbias, dws, dbs, dlb, dfg, dgn]
    packed_g = _pack([dmod] + small_g + [loss_part])
    rows = packed_g.shape[0]
    s_small = gather_all_start(
        lax.dynamic_update_slice(jnp.zeros((N_DEV, rows, LANES), F32), packed_g[None], (dev, 0, 0)),
        "gather_small_start")

    def finish(group, after):
        mine = []
        for ex, _, _, _, nm in group:
            parts_thru, land = exchange_wait(ex[0], ex[1], ex[2], ex[3], after, "exchange_" + nm + "_wait")
            mine.append(sum_parts(parts_thru, land, chip, "sum_" + nm))
            after = mine[-1]
        theirs = swap_sibling(mine, "swap_" + group[0][4])
        return [[r.reshape(w.shape) for r in adamw_pair(pa, pb, w[0], m[0], v[0], "adamw_" + nm)]
                for pa, pb, (_, w, m, v, nm) in zip(mine, theirs, group)]

    (gb_out, db_out, mb_out, vb_out), (gb_in, db_in, mb_in, vb_in), (ga_out, da_out, ma_out, va_out) = finish(
        [(e_bo, b_w_out, m_b_w_out, v_b_w_out, "b_out"), (e_bi, b_w_in, m_b_w_in, v_b_w_in, "b_in"),
         (e_ao, a_w_out, m_a_w_out, v_a_w_out, "a_out")], s_small[3])
    ((ga_in, da_in, ma_in, va_in),) = finish([(e_ai, a_w_in, m_a_w_in, v_a_w_in, "a_in")], ga_out)

    small_w = [norm_gain, a_ln_gain, a_ln_bias, a_w_s, a_b_s, b_lower_bounds, final_gain, b_gn_gain]
    small_m = [m_norm_gain, m_a_ln_gain, m_a_ln_bias, m_a_w_s, m_a_b_s, m_b_lower_bounds, m_final_gain, m_b_gn_gain]
    small_v = [v_norm_gain, v_a_ln_gain, v_a_ln_bias, v_a_w_s, v_a_b_s, v_b_lower_bounds, v_final_gain, v_b_gn_gain]
    rows_of = lambda a: a.reshape(-1, a.shape[-1])
    gathered = gather_all_wait(s_small[0], s_small[1], s_small[2], ga_in, "gather_small_wait")
    tail, small_res = small_update(
        gathered, n_dmod // LANES, [rows_of(a) for a in small_w], [rows_of(a) for a in small_m],
        [rows_of(a) for a in small_v], "small_update")
    loss = tail[0, 0]
    sg, sd, sm, sv = [[small_res[p][kind].reshape(w.shape) for p, w in enumerate(small_w)] for kind in range(4)]

    dmod_all = gathered[:, :n_dmod // LANES].reshape(N_DEV * nb, n_l, 3 * d)
    dmod_cols = lax.dynamic_slice_in_dim(dmod_all, chip * ada_cols, ada_cols, axis=2)
    dmod_cols = jnp.transpose(dmod_cols, (1, 0, 2))
    g_wada, d_wada, m_wada, v_wada = ada_bwd(c_all, dmod_cols, w_ada, m_w_ada, v_w_ada, "ada_bwd")
    flat = lambda a: a.reshape(1, -1)
    g_bada, d_bada, m_bada, v_bada = [
        r.reshape(b_ada.shape) for r in
        bias_update(dmod_all.reshape(N_DEV * nb, n_l * 3 * d), flat(b_ada), flat(m_b_ada), flat(v_b_ada), "bias_update")]

    def order(ng, wada, bada, ain, sm_rest, aout, bin_, bout):
        lg, lbi, ws_, bs_, lbd, fg_, gn_ = sm_rest
        return [ng, wada, bada, ain, lg, lbi, ws_, bs_, aout, bin_, lbd, gn_, bout, fg_]

    grads = order(sg[0], g_wada, g_bada, ga_in, sg[1:8], ga_out, gb_in, gb_out)
    deltas = order(sd[0], d_wada, d_bada, da_in, sd[1:8], da_out, db_in, db_out)
    new_m = order(sm[0], m_wada, m_bada, ma_in, sm[1:8], ma_out, mb_in, mb_out)
    new_v = order(sv[0], v_wada, v_bada, va_in, sv[1:8], va_out, vb_in, vb_out)
    return (loss, grad_x, *grads, *deltas, *new_m, *new_v)
```

```python
import jax
import jax.numpy as jnp
from jax import lax
from jax.experimental import pallas as pl
from jax.experimental.pallas import tpu as pltpu

F32 = jnp.float32
BF16 = jnp.bfloat16
EPS = 1e-6
CHUNK = 64
SG_BLOCK = 128
SG_GROUPS = 8
SG_STEP_BLOCKS = 4
HEAD_DIM = 128
HG_WIDE = 8
HG_ROWS = 256
N_CHIPS = 4
N_DEV = 8
LANES = 128
ADAM_LR = 0.001
ADAM_B1 = 0.9
ADAM_B2 = 0.999
ADAM_EPS = 1e-08
ADAM_WD = 0.01
ADAM_STEP = 10
GELU_C0 = 0.7978845608028654
GELU_C1 = 0.044715
MESH = pl.DeviceIdType.MESH
VMEM_LIMIT = 56 * 1024 * 1024


ROW_TILE = 1024


def _col_tile(n):
    return next(t for t in (1024, 768, 512, 256) if n % t == 0)


def _call(body, **kw):
    return pl.pallas_call(body, **kw)


def _params(**kw):
    return pltpu.CompilerParams(vmem_limit_bytes=VMEM_LIMIT, **kw)


def _sigmoid(x):
    return 0.5 * jnp.tanh(0.5 * x) + 0.5


def _sigmoid_small(x):
    return 1.0 / (1.0 + jnp.exp(-x))


def _silu_and_grad(x):
    s = _sigmoid(x)
    return x * s, s * (1.0 + x * (1.0 - s))


def _gelu(x):
    return 0.5 * x * (1.0 + jnp.tanh(GELU_C0 * (x + GELU_C1 * x * x * x)))


def _gelu_and_grad(x):
    t = jnp.tanh(GELU_C0 * (x + GELU_C1 * x * x * x))
    g = 0.5 * x * (1.0 + t)
    dg = 0.5 * (1.0 + t) + 0.5 * x * (1.0 - t * t) * (GELU_C0 * (1.0 + 3.0 * GELU_C1 * x * x))
    return g, dg


def _dot(a, b, dims, precision=None):
    return lax.dot_general(a, b, (dims, ((), ())), precision=precision, preferred_element_type=F32)


NN = ((1,), (0,))
NT = ((1,), (1,))
TN = ((0,), (0,))


def _adamw(w, g, m, v):
    m = ADAM_B1 * m + (1.0 - ADAM_B1) * g
    v = ADAM_B2 * v + (1.0 - ADAM_B2) * (g * g)
    m_hat = m / (1.0 - ADAM_B1 ** ADAM_STEP)
    v_hat = v / (1.0 - ADAM_B2 ** ADAM_STEP)
    delta = -ADAM_LR * (m_hat / (jnp.sqrt(v_hat) + ADAM_EPS) + ADAM_WD * w)
    return delta, m, v


def _chunk_mask():
    r = lax.broadcasted_iota(jnp.int32, (SG_BLOCK, SG_BLOCK), 0)
    c = lax.broadcasted_iota(jnp.int32, (SG_BLOCK, SG_BLOCK), 1)
    return (c // CHUNK) <= (r // CHUNK)


def _place():
    return lax.axis_index("x"), lax.axis_index("y"), lax.axis_index("c")


def _other_chips(x, y):
    return [(1 - x, y), (x, 1 - y), (1 - x, 1 - y)]


def allgather_small(v, name):
    m_per, n = v.shape

    def body(x_ref, out_ref, send_sems, recv_sems, local_sem):
        x, y, c = _place()
        me, sibling = (x, y, c), (x, y, 1 - c)
        chips = _other_chips(x, y)

        def rows(px, py, pc):
            return out_ref.at[pl.ds((4 * px + 2 * py + pc) * m_per, m_per), :]

        def copy(k, block, to, src=None):
            return pltpu.make_async_remote_copy(
                src_ref=rows(*block) if src is None else src, dst_ref=rows(*block),
                send_sem=send_sems.at[k], recv_sem=recv_sems.at[k], device_id=to, device_id_type=MESH)

        mine = pltpu.make_async_copy(x_ref, rows(*me), local_sem)
        mine.start()
        first = [copy(0, me, sibling, src=x_ref)]
        first += [copy(1 + j, me, (*chip, c), src=x_ref) for j, chip in enumerate(chips)]
        for cp in first:
            cp.start()
        passed = [copy(4 + j, (*chip, c), sibling) for j, chip in enumerate(chips)]
        for j, chip in enumerate(chips):
            copy(1 + j, (*chip, c), me).wait_recv()
            passed[j].start()
        copy(0, sibling, me).wait_recv()
        for j, chip in enumerate(chips):
            copy(4 + j, (*chip, 1 - c), me).wait_recv()
        for cp in first + passed:
            cp.wait_send()
        mine.wait()

    return _call(
        body, name=name,
        out_shape=jax.ShapeDtypeStruct((N_DEV * m_per, n), v.dtype),
        in_specs=[pl.BlockSpec(memory_space=pltpu.VMEM)],
        out_specs=pl.BlockSpec(memory_space=pltpu.VMEM),
        scratch_shapes=[pltpu.SemaphoreType.DMA((7,)), pltpu.SemaphoreType.DMA((7,)), pltpu.SemaphoreType.DMA],
    )(v)


def _hbm_spec():
    return pl.BlockSpec(memory_space=pltpu.HBM)


def _sem_spec():
    return pl.BlockSpec(memory_space=pltpu.SEMAPHORE)


def _split_params():
    return pltpu.CompilerParams(has_side_effects=pltpu.SideEffectType.DATAFLOW_SIDE_EFFECTING)


def _hbm(a):
    return pltpu.with_memory_space_constraint(a, pltpu.HBM)


def _half_copy(land_ref, rows, chip_idx, core_half, send_sem, recv_sem, to):
    half = land_ref.at[chip_idx, pl.ds(core_half * (rows // 2), rows // 2), :]
    return pltpu.make_async_remote_copy(
        src_ref=half, dst_ref=half, send_sem=send_sem, recv_sem=recv_sem, device_id=to, device_id_type=MESH)


def halves_start(lands, name):
    n = len(lands)

    def body(*refs):
        land_refs, send_sems, recv_sems, token = refs[:n], refs[n], refs[n + 1], refs[-1]
        x, y, c = _place()
        for w in range(n):
            for j, (px, py) in enumerate(_other_chips(x, y)):
                _half_copy(land_refs[w], lands[w].shape[1], 2 * x + y, c,
                           send_sems.at[3 * w + j], recv_sems.at[3 * w + j], (px, py, c)).start()
        token[...] = jnp.zeros_like(token)

    res = _call(
        body, name=name,
        out_shape=(pltpu.SemaphoreType.DMA((3 * n,)), pltpu.SemaphoreType.DMA((3 * n,)),
                   *[pltpu.HBM(a.shape, a.dtype) for a in lands], jax.ShapeDtypeStruct((8, LANES), F32)),
        in_specs=(_hbm_spec(),) * n,
        out_specs=(_sem_spec(), _sem_spec(), *[_hbm_spec()] * n, pl.BlockSpec(memory_space=pltpu.VMEM)),
        input_output_aliases={w: 2 + w for w in range(n)}, compiler_params=_split_params(),
    )(*[_hbm(a) for a in lands])
    return res[0], res[1], list(res[2:2 + n]), res[2 + n]


def halves_wait(send_sems, recv_sems, lands, after, name):
    n = len(lands)

    def body(*refs):
        land_refs, send_sems, recv_sems = refs[:n], refs[n], refs[n + 1]
        x, y, c = _place()
        for w in range(n):
            for j, (px, py) in enumerate(_other_chips(x, y)):
                cp = _half_copy(land_refs[w], lands[w].shape[1], 2 * px + py, c,
                                send_sems.at[3 * w + j], recv_sems.at[3 * w + j], (px, py, c))
                cp.wait_send()
                cp.wait_recv()

    return _call(
        body, name=name,
        out_shape=tuple(pltpu.HBM(a.shape, a.dtype) for a in lands),
        in_specs=(*[_hbm_spec()] * n, _sem_spec(), _sem_spec(), pl.BlockSpec(memory_space=pl.ANY)),
        out_specs=tuple(_hbm_spec() for _ in lands), input_output_aliases={w: w for w in range(n)},
        compiler_params=_split_params(),
    )(*lands, send_sems, recv_sems, after)


def pass_halves(lands, name):
    n = len(lands)

    def body(*refs):
        land_refs, send_sems, recv_sems = refs[n:2 * n], refs[2 * n], refs[2 * n + 1]
        x, y, c = _place()
        sent = []
        for w in range(n):
            for j, (px, py) in enumerate(_other_chips(x, y)):
                cp = _half_copy(land_refs[w], lands[w].shape[1], 2 * px + py, c,
                                send_sems.at[3 * w + j], recv_sems.at[3 * w + j], (x, y, 1 - c))
                cp.start()
                sent.append(cp)
        for w in range(n):
            for j, (px, py) in enumerate(_other_chips(x, y)):
                _half_copy(land_refs[w], lands[w].shape[1], 2 * px + py, 1 - c,
                           send_sems.at[3 * w + j], recv_sems.at[3 * w + j], (x, y, 1 - c)).wait_recv()
        for cp in sent:
            cp.wait_send()

    return _call(
        body, name=name,
        out_shape=[jax.ShapeDtypeStruct(a.shape, a.dtype) for a in lands],
        in_specs=[_hbm_spec()] * n, out_specs=[_hbm_spec()] * n,
        input_output_aliases={w: w for w in range(n)},
        scratch_shapes=[pltpu.SemaphoreType.DMA((3 * n,)), pltpu.SemaphoreType.DMA((3 * n,))],
    )(*lands)


def gather_start(land, after, name):
    def body(land_ref, after_ref, send_sems, recv_sems, land_thru, token):
        del after_ref, land_thru
        x, y, c = _place()
        for j, (px, py) in enumerate(_other_chips(x, y)):
            pltpu.make_async_remote_copy(
                src_ref=land_ref.at[2 * x + y], dst_ref=land_ref.at[2 * x + y],
                send_sem=send_sems.at[j], recv_sem=recv_sems.at[j], device_id=(px, py, c),
                device_id_type=MESH).start()
        token[...] = jnp.zeros_like(token)

    return _call(
        body, name=name,
        out_shape=(pltpu.SemaphoreType.DMA((3,)), pltpu.SemaphoreType.DMA((3,)),
                   pltpu.HBM(land.shape, land.dtype), jax.ShapeDtypeStruct((8, LANES), F32)),
        in_specs=(_hbm_spec(), pl.BlockSpec(memory_space=pl.ANY)),
        out_specs=(_sem_spec(), _sem_spec(), _hbm_spec(), pl.BlockSpec(memory_space=pltpu.VMEM)),
        input_output_aliases={0: 2}, compiler_params=_split_params(),
    )(_hbm(land), after)


def gather_wait(send_sems, recv_sems, land, after, name):
    def body(land_ref, send_sems, recv_sems, after_ref, land_out):
        del after_ref, land_out
        x, y, c = _place()
        for j, (px, py) in enumerate(_other_chips(x, y)):
            cp = pltpu.make_async_remote_copy(
                src_ref=land_ref.at[2 * x + y], dst_ref=land_ref.at[2 * px + py],
                send_sem=send_sems.at[j], recv_sem=recv_sems.at[j], device_id=(px, py, c), device_id_type=MESH)
            cp.wait_send()
            cp.wait_recv()

    return _call(
        body, name=name,
        out_shape=pltpu.HBM(land.shape, land.dtype),
        in_specs=(_hbm_spec(), _sem_spec(), _sem_spec(), pl.BlockSpec(memory_space=pl.ANY)),
        out_specs=_hbm_spec(), input_output_aliases={0: 0}, compiler_params=_split_params(),
    )(land, send_sems, recv_sems, after)


def _flips():
    return [(fx, fy, fc) for fx in (0, 1) for fy in (0, 1) for fc in (0, 1) if (fx, fy, fc) != (0, 0, 0)]


def _flipped(x, y, c, flip):
    fx, fy, fc = flip
    return (1 - x if fx else x, 1 - y if fy else y, 1 - c if fc else c)


def gather_all_start(land, name):
    def body(land_ref, send_sems, recv_sems, land_thru, token):
        del land_thru
        x, y, c = _place()
        for k, flip in enumerate(_flips()):
            pltpu.make_async_remote_copy(
                src_ref=land_ref.at[4 * x + 2 * y + c], dst_ref=land_ref.at[4 * x + 2 * y + c],
                send_sem=send_sems.at[k], recv_sem=recv_sems.at[k], device_id=_flipped(x, y, c, flip),
                device_id_type=MESH).start()
        token[...] = jnp.zeros_like(token)

    return _call(
        body, name=name,
        out_shape=(pltpu.SemaphoreType.DMA((7,)), pltpu.SemaphoreType.DMA((7,)),
                   pltpu.HBM(land.shape, land.dtype), jax.ShapeDtypeStruct((8, LANES), F32)),
        in_specs=(_hbm_spec(),),
        out_specs=(_sem_spec(), _sem_spec(), _hbm_spec(), pl.BlockSpec(memory_space=pltpu.VMEM)),
        input_output_aliases={0: 2}, compiler_params=_split_params(),
    )(_hbm(land))


def gather_all_wait(send_sems, recv_sems, land, after, name):
    def body(land_ref, send_sems, recv_sems, after_ref, land_out):
        del after_ref, land_out
        x, y, c = _place()
        for k, flip in enumerate(_flips()):
            px, py, pc = _flipped(x, y, c, flip)
            cp = pltpu.make_async_remote_copy(
                src_ref=land_ref.at[4 * x + 2 * y + c], dst_ref=land_ref.at[4 * px + 2 * py + pc],
                send_sem=send_sems.at[k], recv_sem=recv_sems.at[k], device_id=(px, py, pc), device_id_type=MESH)
            cp.wait_send()
            cp.wait_recv()

    return _call(
        body, name=name,
        out_shape=pltpu.HBM(land.shape, land.dtype),
        in_specs=(_hbm_spec(), _sem_spec(), _sem_spec(), pl.BlockSpec(memory_space=pl.ANY)),
        out_specs=_hbm_spec(), input_output_aliases={0: 0}, compiler_params=_split_params(),
    )(land, send_sems, recv_sems, after)


def exchange_start(parts, name):
    _, r, c_ = parts.shape

    def body(parts_ref, land_ref, send_sems, recv_sems, parts_thru, land_thru, token):
        del parts_thru, land_thru
        x, y, c = _place()
        for j, (px, py) in enumerate(_other_chips(x, y)):
            pltpu.make_async_remote_copy(
                src_ref=parts_ref.at[2 * px + py], dst_ref=land_ref.at[j],
                send_sem=send_sems.at[j], recv_sem=recv_sems.at[j], device_id=(px, py, c),
                device_id_type=MESH).start()
        token[...] = jnp.zeros_like(token)

    return _call(
        body, name=name,
        out_shape=(pltpu.SemaphoreType.DMA((3,)), pltpu.SemaphoreType.DMA((3,)),
                   pltpu.HBM(parts.shape, parts.dtype), pltpu.HBM((3, r, c_), parts.dtype),
                   jax.ShapeDtypeStruct((8, LANES), F32)),
        in_specs=(_hbm_spec(), _hbm_spec()),
        out_specs=(_sem_spec(), _sem_spec(), _hbm_spec(), _hbm_spec(), pl.BlockSpec(memory_space=pltpu.VMEM)),
        input_output_aliases={0: 2, 1: 3}, compiler_params=_split_params(),
    )(_hbm(parts), _hbm(lax.empty((3, r, c_), parts.dtype)))


def exchange_wait(send_sems, recv_sems, parts, land, after, name):
    def body(parts_ref, land_ref, send_sems, recv_sems, after_ref, parts_out, land_out):
        del after_ref, parts_out, land_out
        x, y, c = _place()
        for j, (px, py) in enumerate(_other_chips(x, y)):
            cp = pltpu.make_async_remote_copy(
                src_ref=parts_ref.at[2 * px + py], dst_ref=land_ref.at[j],
                send_sem=send_sems.at[j], recv_sem=recv_sems.at[j], device_id=(px, py, c), device_id_type=MESH)
            cp.wait_send()
            cp.wait_recv()

    return _call(
        body, name=name,
        out_shape=(pltpu.HBM(parts.shape, parts.dtype), pltpu.HBM(land.shape, land.dtype)),
        in_specs=(_hbm_spec(), _hbm_spec(), _sem_spec(), _sem_spec(), pl.BlockSpec(memory_space=pl.ANY)),
        out_specs=(_hbm_spec(), _hbm_spec()), input_output_aliases={0: 0, 1: 1},
        compiler_params=_split_params(),
    )(parts, land, send_sems, recv_sems, after)


def cast_into_slot(w, chip, after, name):
    r, c = w.shape
    tr = min(256, r)

    def body(s_ref, w_ref, after_ref, o_ref):
        del s_ref, after_ref
        o_ref[...] = w_ref[...].astype(BF16)

    return _call(
        body, name=name,
        grid_spec=pltpu.PrefetchScalarGridSpec(
            num_scalar_prefetch=1, grid=(r // tr,),
            in_specs=[pl.BlockSpec((tr, c), lambda i, s: (i, 0)), pl.BlockSpec(memory_space=pl.ANY)],
            out_specs=pl.BlockSpec((None, tr, c), lambda i, s: (s[0], i, 0))),
        out_shape=jax.ShapeDtypeStruct((N_CHIPS, r, c), BF16),
        compiler_params=_params(),
    )(chip.reshape(1).astype(jnp.int32), w, after)


def sum_parts(parts, land, chip, name):
    _, r, c = parts.shape
    tr = min(256, r)

    def body(s_ref, p_ref, l_ref, o_ref):
        del s_ref
        acc = p_ref[...].astype(F32) + l_ref[0].astype(F32)
        acc = acc + l_ref[1].astype(F32)
        o_ref[...] = (acc + l_ref[2].astype(F32)).astype(BF16)

    return _call(
        body, name=name,
        grid_spec=pltpu.PrefetchScalarGridSpec(
            num_scalar_prefetch=1, grid=(r // tr,),
            in_specs=[pl.BlockSpec((None, tr, c), lambda i, s: (s[0], i, 0)),
                      pl.BlockSpec((3, tr, c), lambda i, s: (0, i, 0))],
            out_specs=pl.BlockSpec((tr, c), lambda i, s: (i, 0))),
        out_shape=jax.ShapeDtypeStruct((r, c), BF16),
        compiler_params=_params(),
    )(chip.reshape(1).astype(jnp.int32), parts, land)


def swap_sibling(arrs, name):
    n = len(arrs)

    def body(*refs):
        ins, outs = refs[:n], refs[n:2 * n]
        send_sems, recv_sems = refs[2 * n:]
        x, y, c = _place()
        cps = []
        for w in range(n):
            cp = pltpu.make_async_remote_copy(
                src_ref=ins[w], dst_ref=outs[w], send_sem=send_sems.at[w], recv_sem=recv_sems.at[w],
                device_id=(x, y, 1 - c), device_id_type=MESH)
            cp.start()
            cps.append(cp)
        for cp in cps:
            cp.wait_recv()
        for cp in cps:
            cp.wait_send()

    return _call(
        body, name=name,
        out_shape=[jax.ShapeDtypeStruct(a.shape, a.dtype) for a in arrs],
        in_specs=[_hbm_spec()] * n, out_specs=[_hbm_spec()] * n,
        scratch_shapes=[pltpu.SemaphoreType.DMA((n,)), pltpu.SemaphoreType.DMA((n,))],
    )(*arrs)


def adamw_pair(pa, pb, w, m, v, name):
    r, c = w.shape
    tr = min(256, r)

    def body(pa_ref, pb_ref, w_ref, m_ref, v_ref, g_ref, d_ref, nm_ref, nv_ref):
        g = pa_ref[...].astype(F32) + pb_ref[...].astype(F32)
        d, nm, nv = _adamw(w_ref[...], g, m_ref[...], v_ref[...])
        g_ref[...] = g
        d_ref[...] = d
        nm_ref[...] = nm
        nv_ref[...] = nv

    spec = pl.BlockSpec((tr, c), lambda i: (i, 0))
    return _call(
        body, name=name, grid=(r // tr,),
        out_shape=[jax.ShapeDtypeStruct((r, c), F32)] * 4,
        in_specs=[spec] * 5, out_specs=[spec] * 4,
        compiler_params=_params(),
    )(pa, pb, w, m, v)


def small_update(gathered, first_row, ws, ms, vs, name):
    n_w = len(ws)
    total_rows = gathered.shape[1]

    def body(*refs):
        g_ref = refs[0]
        w_refs, m_refs, v_refs = refs[1:1 + n_w], refs[1 + n_w:1 + 2 * n_w], refs[1 + 2 * n_w:1 + 3 * n_w]
        tail_ref = refs[1 + 3 * n_w]
        outs = refs[2 + 3 * n_w:2 + 7 * n_w]
        sum_ref = refs[2 + 7 * n_w]
        acc = g_ref[0]
        for k in range(1, N_DEV):
            acc = acc + g_ref[k]
        sum_ref[...] = acc
        row = first_row
        for p in range(n_w):
            a, b = ws[p].shape
            per = b // LANES
            g_out, d_out, m_out, v_out = outs[4 * p:4 * p + 4]
            if per == 1:
                g_out[...] = sum_ref[row:row + a, :]
            else:
                for i in range(a):
                    for jc in range(per):
                        g_out[i:i + 1, jc * LANES:(jc + 1) * LANES] = sum_ref[row + i * per + jc:row + i * per + jc + 1, :]
            row += a * per
            dl, nm, nv = _adamw(w_refs[p][...], g_out[...], m_refs[p][...], v_refs[p][...])
            d_out[...] = dl
            m_out[...] = nm
            v_out[...] = nv
        tail_ref[...] = sum_ref[row:row + 1, :]

    out_shape = [jax.ShapeDtypeStruct((1, LANES), F32)]
    for w in ws:
        out_shape += [jax.ShapeDtypeStruct(w.shape, F32)] * 4
    res = _call(
        body, name=name, out_shape=out_shape,
        scratch_shapes=[pltpu.VMEM((total_rows, LANES), F32)],
        compiler_params=_params(),
    )(gathered, *ws, *ms, *vs)
    return res[0], [res[1 + 4 * p:5 + 4 * p] for p in range(n_w)]


def ada_fwd(c_all, w_ada, b_cols, after, name):
    n_l, d, cols = w_ada.shape
    nb = c_all.shape[0]
    tn = cols

    def body(c_ref, w_ref, b_ref, after_ref, o_ref):
        del after_ref
        cv = c_ref[...]
        ca = (cv * _sigmoid(cv)).astype(BF16)
        o_ref[...] = _dot(ca, w_ref[...].astype(BF16), NN) + b_ref[...]

    return _call(
        body, name=name, grid=(n_l, cols // tn),
        out_shape=jax.ShapeDtypeStruct((n_l, nb, cols), F32),
        in_specs=[pl.BlockSpec((nb, d), lambda l, j: (0, 0)),
                  pl.BlockSpec((None, d, tn), lambda l, j: (l, 0, j)),
                  pl.BlockSpec((None, 1, tn), lambda l, j: (l, 0, j)),
                  pl.BlockSpec(memory_space=pl.ANY)],
        out_specs=pl.BlockSpec((None, nb, tn), lambda l, j: (l, 0, j)),
        compiler_params=_params(),
    )(c_all, w_ada, b_cols, after)


def ada_bwd(c_all, dmod_cols, w, m, v, name):
    n_l, d, cols = w.shape
    nb = c_all.shape[0]
    tn = cols

    def body(c_ref, dm_ref, w_ref, m_ref, v_ref, g_ref, d_ref, nm_ref, nv_ref):
        cv = c_ref[...]
        ca = (cv * _sigmoid(cv)).astype(BF16)
        g = _dot(ca, dm_ref[...].astype(BF16), TN)
        dl, nm, nv = _adamw(w_ref[...], g, m_ref[...], v_ref[...])
        g_ref[...] = g
        d_ref[...] = dl
        nm_ref[...] = nm
        nv_ref[...] = nv

    wspec = pl.BlockSpec((None, d, tn), lambda l, j: (l, 0, j))
    return _call(
        body, name=name, grid=(n_l, cols // tn),
        out_shape=[jax.ShapeDtypeStruct((n_l, d, cols), F32)] * 4,
        in_specs=[pl.BlockSpec((nb, d), lambda l, j: (0, 0)),
                  pl.BlockSpec((None, nb, tn), lambda l, j: (l, 0, j)),
                  wspec, wspec, wspec],
        out_specs=[wspec] * 4,
        compiler_params=_params(),
    )(c_all, dmod_cols, w, m, v)


def bias_update(dmod_all, w, m, v, name):
    def body(dm_ref, w_ref, m_ref, v_ref, g_ref, d_ref, nm_ref, nv_ref):
        g = jnp.sum(dm_ref[...], axis=0, keepdims=True)
        dl, nm, nv = _adamw(w_ref[...], g, m_ref[...], v_ref[...])
        g_ref[...] = g
        d_ref[...] = dl
        nm_ref[...] = nm
        nv_ref[...] = nv

    return _call(
        body, name=name,
        out_shape=[jax.ShapeDtypeStruct(w.shape, F32)] * 4,
        compiler_params=_params(),
    )(dmod_all, w, m, v)


def inproj_fwd(x, mod, ng, wg, seq, sectioned, name):
    m_rows, d = x.shape
    nsh, _, ns = wg.shape
    n = nsh * ns
    tm, tn = min(2 * ROW_TILE, seq), _col_tile(ns)
    per = ns // tn

    def body(x_ref, mod_ref, ng_ref, w_ref, proj_ref, h_ref):
        @pl.when(pl.program_id(1) == 0)
        def _():
            xv = x_ref[...]
            r = lax.rsqrt(jnp.mean(xv * xv, axis=-1, keepdims=True) + EPS)
            md = mod_ref[0]
            h = (xv * r * ng_ref[...]) * (1.0 + md[:, d:2 * d]) + md[:, :d]
            h_ref[...] = h.astype(BF16)
        proj_ref[...] = _dot(h_ref[...], w_ref[...], NN)

    if sectioned:
        proj_shape = (nsh, m_rows, ns)
        proj_spec = pl.BlockSpec((None, tm, tn), lambda i, j: (j // per, i, j % per))
    else:
        proj_shape = (m_rows, n)
        proj_spec = pl.BlockSpec((tm, tn), lambda i, j: (i, j))
    return _call(
        body, name=name, grid=(m_rows // tm, n // tn),
        out_shape=[jax.ShapeDtypeStruct(proj_shape, F32), jax.ShapeDtypeStruct((m_rows, d), BF16)],
        in_specs=[pl.BlockSpec((tm, d), lambda i, j: (i, 0)),
                  pl.BlockSpec((1, 1, 3 * d), lambda i, j: ((i * tm) // seq, 0, 0)),
                  pl.BlockSpec((1, d), lambda i, j: (0, 0)),
                  pl.BlockSpec((None, d, tn), lambda i, j: (j // per, 0, j % per))],
        out_specs=[proj_spec, pl.BlockSpec((tm, d), lambda i, j: (i, 0))],
        compiler_params=_params(),
    )(x, mod, ng, wg)


def outproj_fwd(y, w, x, mod, seq, name):
    m_rows, di = y.shape
    d = w.shape[1]
    tm = min(ROW_TILE, seq)

    def body(y_ref, w_ref, x_ref, mod_ref, xn_ref, out_ref):
        acc = _dot(y_ref[...], w_ref[...], NN)
        out_ref[...] = acc.astype(BF16)
        xn_ref[...] = x_ref[...] + mod_ref[0][:, 2 * d:] * acc

    row = pl.BlockSpec((tm, d), lambda i: (i, 0))
    return _call(
        body, name=name, grid=(m_rows // tm,),
        out_shape=[jax.ShapeDtypeStruct((m_rows, d), F32), jax.ShapeDtypeStruct((m_rows, d), BF16)],
        in_specs=[pl.BlockSpec((tm, di), lambda i: (i, 0)),
                  pl.BlockSpec((di, d), lambda i: (0, 0)),
                  row,
                  pl.BlockSpec((1, 1, 3 * d), lambda i: ((i * tm) // seq, 0, 0))],
        out_specs=[row, row],
        compiler_params=_params(),
    )(y, w, x, mod)


def outproj_bwd(dxo, out, mod, w, seq, name):
    m_rows, d = dxo.shape
    di = w.shape[0]
    nb = m_rows // seq
    tm, tn = min(ROW_TILE, seq), di

    def body(dxo_ref, out_ref, mod_ref, w_ref, dy_ref, dout_ref, dgate_ref):
        i = pl.program_id(0)

        @pl.when(pl.program_id(1) == 0)
        def _():
            dx = dxo_ref[...]
            dout_ref[...] = (mod_ref[0][:, 2 * d:] * dx).astype(BF16)
            part = jnp.sum(dx * out_ref[...].astype(F32), axis=0, keepdims=True)

            @pl.when((i * tm) % seq == 0)
            def _():
                dgate_ref[0] = part

            @pl.when((i * tm) % seq != 0)
            def _():
                dgate_ref[0] = dgate_ref[0] + part

        dy_ref[...] = _dot(dout_ref[...], w_ref[...], NT).astype(BF16)

    row = pl.BlockSpec((tm, d), lambda i, j: (i, 0))
    return _call(
        body, name=name, grid=(m_rows // tm, di // tn),
        out_shape=[jax.ShapeDtypeStruct((m_rows, di), BF16), jax.ShapeDtypeStruct((m_rows, d), BF16),
                   jax.ShapeDtypeStruct((nb, 1, d), F32)],
        in_specs=[row, row,
                  pl.BlockSpec((1, 1, 3 * d), lambda i, j: ((i * tm) // seq, 0, 0)),
                  pl.BlockSpec((tn, d), lambda i, j: (j, 0))],
        out_specs=[pl.BlockSpec((tm, tn), lambda i, j: (i, j)), row,
                   pl.BlockSpec((1, 1, d), lambda i, j: ((i * tm) // seq, 0, 0))],
        compiler_params=_params(),
    )(dxo, out, mod, w)


def grad_w_out(y, dout, name):
    m_rows, di = y.shape
    d = dout.shape[1]
    tm, tk = min(ROW_TILE, m_rows), di
    n_m = m_rows // tm

    def body(y_ref, do_ref, o_ref, acc_ref):
        mi = pl.program_id(1)

        @pl.when(mi == 0)
        def _():
            acc_ref[...] = jnp.zeros_like(acc_ref)

        acc_ref[...] += _dot(y_ref[...], do_ref[...], TN)

        @pl.when(mi == n_m - 1)
        def _():
            o_ref[...] = acc_ref[...].astype(BF16)

    return _call(
        body, name=name, grid=(di // tk, n_m),
        out_shape=jax.ShapeDtypeStruct((di, d), BF16),
        in_specs=[pl.BlockSpec((tm, tk), lambda j, mi: (mi, j)),
                  pl.BlockSpec((tm, d), lambda j, mi: (mi, 0))],
        out_specs=pl.BlockSpec((tk, d), lambda j, mi: (j, 0)),
        scratch_shapes=[pltpu.VMEM((tk, d), F32)],
        compiler_params=_params(),
    )(y, dout)


def grad_w_in(h, dproj, nsh, sectioned, name):
    m_rows, d = h.shape
    n = dproj.shape[0] * dproj.shape[2] if sectioned else dproj.shape[1]
    ns = n // nsh
    tm, tn = min(ROW_TILE, m_rows), ns
    per = ns // tn
    n_m = m_rows // tm

    def body(h_ref, dp_ref, o_ref, acc_ref):
        mi = pl.program_id(1)
        @pl.when(mi == 0)
        def _():
            acc_ref[...] = jnp.zeros_like(acc_ref)

        acc_ref[...] += _dot(h_ref[...], dp_ref[...], TN)

        @pl.when(mi == n_m - 1)
        def _():
            o_ref[...] = acc_ref[...].astype(BF16)

    if sectioned:
        dp_spec = pl.BlockSpec((None, tm, tn), lambda j, mi: (j // per, mi, j % per))
    else:
        dp_spec = pl.BlockSpec((tm, tn), lambda j, mi: (mi, j))
    return _call(
        body, name=name, grid=(n // tn, n_m),
        out_shape=jax.ShapeDtypeStruct((nsh, d, ns), BF16),
        in_specs=[pl.BlockSpec((tm, d), lambda j, mi: (mi, 0)), dp_spec],
        out_specs=pl.BlockSpec((None, d, tn), lambda j, mi: (j // per, 0, j % per)),
        scratch_shapes=[pltpu.VMEM((d, tn), F32)],
        compiler_params=_params(),
    )(h, dproj)


def inproj_bwd(dproj, wg, x, dxo, mod, ng, seq, sectioned, name):
    m_rows, d = x.shape
    nsh, _, ns = wg.shape
    n = nsh * ns
    nb = m_rows // seq
    tm, tk = min(ROW_TILE, seq), ns
    per = ns // tk
    n_k = n // tk

    def body(dp_ref, w_ref, x_ref, dxo_ref, mod_ref, ng_ref, dxi_ref, dsh_ref, dsc_ref, dng_ref, acc_ref):
        i, k = pl.program_id(0), pl.program_id(1)
        @pl.when(k == 0)
        def _():
            acc_ref[...] = jnp.zeros_like(acc_ref)

        acc_ref[...] += _dot(dp_ref[...], w_ref[...], NT)

        @pl.when(k == n_k - 1)
        def _():
            dh = acc_ref[...]
            xv = x_ref[...]
            r = lax.rsqrt(jnp.mean(xv * xv, axis=-1, keepdims=True) + EPS)
            xn = xv * r
            md = mod_ref[0]
            gain = ng_ref[...]
            p_shift = jnp.sum(dh, axis=0, keepdims=True)
            p_scale = jnp.sum(dh * (xn * gain), axis=0, keepdims=True)
            drn = dh * (1.0 + md[:, d:2 * d])
            p_ng = jnp.sum(drn * xn, axis=0, keepdims=True)
            dxn = drn * gain
            dx = r * (dxn - xn * jnp.mean(dxn * xn, axis=-1, keepdims=True))
            dxi_ref[...] = dxo_ref[...] + dx

            @pl.when((i * tm) % seq == 0)
            def _():
                dsh_ref[0] = p_shift
                dsc_ref[0] = p_scale

            @pl.when((i * tm) % seq != 0)
            def _():
                dsh_ref[0] = dsh_ref[0] + p_shift
                dsc_ref[0] = dsc_ref[0] + p_scale

            @pl.when(i == 0)
            def _():
                dng_ref[...] = p_ng

            @pl.when(i != 0)
            def _():
                dng_ref[...] = dng_ref[...] + p_ng

    if sectioned:
        dp_spec = pl.BlockSpec((None, tm, tk), lambda i, k: (k // per, i, k % per))
    else:
        dp_spec = pl.BlockSpec((tm, tk), lambda i, k: (i, k))
    row = pl.BlockSpec((tm, d), lambda i, k: (i, 0))
    per_seq = pl.BlockSpec((1, 1, d), lambda i, k: ((i * tm) // seq, 0, 0))
    return _call(
        body, name=name, grid=(m_rows // tm, n_k),
        out_shape=[jax.ShapeDtypeStruct((m_rows, d), F32), jax.ShapeDtypeStruct((nb, 1, d), F32),
                   jax.ShapeDtypeStruct((nb, 1, d), F32), jax.ShapeDtypeStruct((1, d), F32)],
        in_specs=[dp_spec,
                  pl.BlockSpec((None, d, tk), lambda i, k: (k // per, 0, k % per)),
                  row, row,
                  pl.BlockSpec((1, 1, 3 * d), lambda i, k: ((i * tm) // seq, 0, 0)),
                  pl.BlockSpec((1, d), lambda i, k: (0, 0))],
        out_specs=[row, per_seq, per_seq, pl.BlockSpec((1, d), lambda i, k: (0, 0))],
        scratch_shapes=[pltpu.VMEM((tm, d), F32)],
        compiler_params=_params(),
    )(dproj, wg, x, dxo, mod, ng)


def _sgu_stats(proj_ref, vg_ref, di, gd, dgel_ref=None):
    s1 = jnp.zeros((SG_BLOCK, 1), F32)
    for g in range(SG_GROUPS):
        v_pre = proj_ref[:, di + g * gd:di + (g + 1) * gd]
        if dgel_ref is None:
            vg = _gelu(v_pre)
        else:
            vg, dgel_ref[:, g * gd:(g + 1) * gd] = _gelu_and_grad(v_pre)
        vg_ref[:, g * gd:(g + 1) * gd] = vg
        s1 = s1 + jnp.sum(vg, axis=1, keepdims=True)
    mu = s1 / di
    s2 = jnp.zeros((SG_BLOCK, 1), F32)
    for g in range(SG_GROUPS):
        dv = vg_ref[:, g * gd:(g + 1) * gd] - mu
        s2 = s2 + jnp.sum(dv * dv, axis=1, keepdims=True)
    return mu, lax.rsqrt(s2 / di + EPS)


def sgu_fwd(proj, ln_gain, ln_bias, ws, bs, name):
    m_rows, n3 = proj.shape
    di = n3 // 3
    gd = di // SG_GROUPS
    n_blocks = m_rows // SG_BLOCK
    per_step = SG_STEP_BLOCKS if n_blocks % SG_STEP_BLOCKS == 0 else 1

    def body(proj_ref, lg_ref, lb_ref, ws_ref, bs_ref, y_ref, wsm_ref, vg_ref):
        @pl.when(pl.program_id(0) == 0)
        def _():
            mask = _chunk_mask()
            for g in range(SG_GROUPS):
                wsm_ref[g] = jnp.where(mask, ws_ref[g], 0.0).astype(BF16)

        for blk in range(per_step):
            p_ref, o_ref = proj_ref.at[blk], y_ref.at[blk]
            mu, rstd = _sgu_stats(p_ref, vg_ref, di, gd)
            for g in range(SG_GROUPS):
                cs = slice(g * gd, (g + 1) * gd)
                vln = (vg_ref[:, cs] - mu) * rstd * lg_ref[:, cs] + lb_ref[:, cs]
                s = _dot(wsm_ref[g], vln.astype(BF16), NN) + bs_ref[g]
                u = _gelu(p_ref[:, cs])
                gp = p_ref[:, 2 * di + g * gd:2 * di + (g + 1) * gd]
                o_ref[:, cs] = (u * s * (gp * _sigmoid(gp))).astype(BF16)

    full = lambda shape: pl.BlockSpec(shape, lambda i: (0,) * len(shape))
    return _call(
        body, name=name, grid=(n_blocks // per_step,),
        out_shape=jax.ShapeDtypeStruct((n_blocks, SG_BLOCK, di), BF16),
        in_specs=[pl.BlockSpec((per_step, SG_BLOCK, n3), lambda i: (i, 0, 0)),
                  full((1, di)), full((1, di)),
                  full((SG_GROUPS, SG_BLOCK, SG_BLOCK)), full((SG_GROUPS, SG_BLOCK, 1))],
        out_specs=pl.BlockSpec((per_step, SG_BLOCK, di), lambda i: (i, 0, 0)),
        scratch_shapes=[pltpu.VMEM((SG_GROUPS, SG_BLOCK, SG_BLOCK), BF16), pltpu.VMEM((SG_BLOCK, di), F32)],
        compiler_params=_params(),
    )(proj.reshape(n_blocks, SG_BLOCK, n3), ln_gain, ln_bias, ws, bs).reshape(m_rows, di)


def sgu_bwd(proj, dy, ln_gain, ln_bias, ws, bs, name):
    m_rows, n3 = proj.shape
    di = n3 // 3
    gd = di // SG_GROUPS
    n_i = m_rows // SG_BLOCK

    def body(proj_ref, dy_ref, lg_ref, lb_ref, ws_ref, bs_ref,
             dp_ref, dws_ref, dbs_ref, dlg_ref, dlb_ref, wsm_ref, vg_ref, dvh_ref, dgel_ref):
        i = pl.program_id(0)

        def before():
            @pl.when(i == 0)
            def _():
                mask = _chunk_mask()
                for g in range(SG_GROUPS):
                    wsm_ref[g] = jnp.where(mask, ws_ref[g], 0.0).astype(BF16)
                dws_ref[...] = jnp.zeros_like(dws_ref)
                dbs_ref[...] = jnp.zeros_like(dbs_ref)
                dlg_ref[...] = jnp.zeros_like(dlg_ref)
                dlb_ref[...] = jnp.zeros_like(dlb_ref)

        def after():
            @pl.when(i == n_i - 1)
            def _():
                mask = _chunk_mask()
                for g in range(SG_GROUPS):
                    dws_ref[g] = jnp.where(mask, dws_ref[g], 0.0)

        before()
        mu, rstd = _sgu_stats(proj_ref, vg_ref, di, gd, dgel_ref)
        m1 = jnp.zeros((SG_BLOCK, 1), F32)
        m2 = jnp.zeros((SG_BLOCK, 1), F32)
        for g in range(SG_GROUPS):
            cs = slice(g * gd, (g + 1) * gd)
            gs = slice(2 * di + g * gd, 2 * di + (g + 1) * gd)
            gain = lg_ref[:, cs]
            vhat = (vg_ref[:, cs] - mu) * rstd
            vln_b = (vhat * gain + lb_ref[:, cs]).astype(BF16)
            s = _dot(wsm_ref[g], vln_b, NN) + bs_ref[g]
            u, du = _gelu_and_grad(proj_ref[:, cs])
            sg, dsg = _silu_and_grad(proj_ref[:, gs])
            dyv = dy_ref[:, cs].astype(F32)
            dp_ref[:, cs] = (dyv * s * sg * du).astype(BF16)
            dp_ref[:, gs] = (dyv * u * s * dsg).astype(BF16)
            ds = dyv * u * sg
            ds_b = ds.astype(BF16)
            dws_ref[g] = dws_ref[g] + _dot(ds_b, vln_b, NT)
            dbs_ref[g] = dbs_ref[g] + jnp.sum(ds, axis=1, keepdims=True)
            dvln = _dot(wsm_ref[g], ds_b, TN)
            dlg_ref[:, cs] = dlg_ref[:, cs] + jnp.sum(dvln * vhat, axis=0, keepdims=True)
            dlb_ref[:, cs] = dlb_ref[:, cs] + jnp.sum(dvln, axis=0, keepdims=True)
            dvh = dvln * gain
            dvh_ref[:, cs] = dvh
            m1 = m1 + jnp.sum(dvh, axis=1, keepdims=True)
            m2 = m2 + jnp.sum(dvh * vhat, axis=1, keepdims=True)
        m1 = m1 / di
        m2 = m2 / di
        for g in range(SG_GROUPS):
            cs = slice(g * gd, (g + 1) * gd)
            vs = slice(di + g * gd, di + (g + 1) * gd)
            vhat = (vg_ref[:, cs] - mu) * rstd
            dvg = rstd * (dvh_ref[:, cs] - m1 - vhat * m2)
            dp_ref[:, vs] = (dvg * dgel_ref[:, cs]).astype(BF16)

        after()

    full = lambda shape: pl.BlockSpec(shape, lambda i: (0,) * len(shape))
    return _call(
        body, name=name, grid=(n_i,),
        out_shape=[jax.ShapeDtypeStruct((m_rows, n3), BF16),
                   jax.ShapeDtypeStruct((SG_GROUPS, SG_BLOCK, SG_BLOCK), F32),
                   jax.ShapeDtypeStruct((SG_GROUPS, SG_BLOCK, 1), F32),
                   jax.ShapeDtypeStruct((1, di), F32), jax.ShapeDtypeStruct((1, di), F32)],
        in_specs=[pl.BlockSpec((SG_BLOCK, n3), lambda i: (i, 0)),
                  pl.BlockSpec((SG_BLOCK, di), lambda i: (i, 0)),
                  full((1, di)), full((1, di)),
                  full((SG_GROUPS, SG_BLOCK, SG_BLOCK)), full((SG_GROUPS, SG_BLOCK, 1))],
        out_specs=[pl.BlockSpec((SG_BLOCK, n3), lambda i: (i, 0)),
                   full((SG_GROUPS, SG_BLOCK, SG_BLOCK)), full((SG_GROUPS, SG_BLOCK, 1)),
                   full((1, di)), full((1, di))],
        scratch_shapes=[pltpu.VMEM((SG_GROUPS, SG_BLOCK, SG_BLOCK), BF16),
                        pltpu.VMEM((SG_BLOCK, di), F32), pltpu.VMEM((SG_BLOCK, di), F32),
                        pltpu.VMEM((SG_BLOCK, di), F32)],
        compiler_params=_params(),
    )(proj, dy, ln_gain, ln_bias, ws, bs)


def _lower_bound(lbraw):
    mx = jnp.maximum(lbraw[0:1, :], lbraw[1:2, :])
    e0 = jnp.exp(lbraw[0:1, :] - mx)
    e1 = jnp.exp(lbraw[1:2, :] - mx)
    p0 = e0 / (e0 + e1)
    p1 = e1 / (e0 + e1)
    return (p0 + p1) - p0, p0, p1


def _tri(lower):
    r = lax.broadcasted_iota(jnp.int32, (CHUNK, CHUNK), 0)
    c = lax.broadcasted_iota(jnp.int32, (CHUNK, CHUNK), 1)
    return ((r >= c) if lower else (c >= r)).astype(BF16)


def _running_sum(tri, x):
    x1 = x.astype(BF16)
    r1 = x - x1.astype(F32)
    x2 = r1.astype(BF16)
    x3 = (r1 - x2.astype(F32)).astype(BF16)
    return _dot(tri, x1, NN) + _dot(tri, x2, NN) + _dot(tri, x3, NN)


def _row(a, idx):
    r = lax.broadcasted_iota(jnp.int32, a.shape, 0)
    return jnp.sum(jnp.where(r == idx, a, 0.0), axis=0, keepdims=True)


def _hgrn_gates(qp, fp, lb, tri):
    sgm = _sigmoid_small(fp)
    f = lb + (1.0 - lb) * sgm
    k = 1.0 - f
    a = _running_sum(tri, jnp.log(f))
    a_mid = _row(a, CHUNK // 2 - 1)
    a_last = _row(a, CHUNK - 1)
    q, dq = _silu_and_grad(qp)
    e1, e2, e3, e4 = jnp.exp(a - a_mid), jnp.exp(a_mid - a), jnp.exp(a), jnp.exp(a_last - a)
    return dict(sgm=sgm, f=f, k=k, q=q, dq=dq, e1=e1, e2=e2, e3=e3, e4=e4, dec=jnp.exp(a_last),
                q_in=q * e1, k_in=k * e2, q_out=q * e3, k_out=k * e4)


def _causal():
    r = lax.broadcasted_iota(jnp.int32, (CHUNK, CHUNK), 0)
    c = lax.broadcasted_iota(jnp.int32, (CHUNK, CHUNK), 1)
    return r >= c


def hgrn_fwd(proj4, lbraw, gn, seq, name):
    _, m_rows, di = proj4.shape
    nb, nh, nc = m_rows // seq, di // HEAD_DIM, seq // CHUNK
    rows = min(HG_ROWS, seq)
    wide = HG_WIDE * HEAD_DIM
    ns, cpb = seq // rows, rows // CHUNK

    def body(p_ref, lb_ref, gn_ref, y_ref, sts_ref, st_ref):
        @pl.when(pl.program_id(2) == 0)
        def _():
            st_ref[...] = jnp.zeros_like(st_ref)

        tri = _tri(True)
        causal = _causal()
        gain = gn_ref[...]
        lbs = [_lower_bound(lb_ref[:, j * HEAD_DIM:(j + 1) * HEAD_DIM])[0] for j in range(HG_WIDE)]

        units = [(n, j) for n in range(cpb) for j in range(HG_WIDE)]
        rs = lambda n: slice(n * CHUNK, (n + 1) * CHUNK)
        cs = lambda j: slice(j * HEAD_DIM, (j + 1) * HEAD_DIM)
        gates, v_b, sc_b, kv, o_in, o_x = {}, {}, {}, {}, {}, {}
        for n, j in units:
            gates[n, j] = _hgrn_gates(p_ref[0, rs(n), cs(j)], p_ref[1, rs(n), cs(j)], lbs[j], tri)
            v_b[n, j] = p_ref[2, rs(n), cs(j)].astype(BF16)
        for u in units:
            t = gates[u]
            sc_b[u] = jnp.where(causal, _dot(t["q_in"].astype(BF16), t["k_in"].astype(BF16), NT), 0.0).astype(BF16)
            kv[u] = _dot(v_b[u], t["k_out"].astype(BF16), TN)
        for u in units:
            o_in[u] = _dot(sc_b[u], v_b[u], NN)
        for j in range(HG_WIDE):
            st = st_ref[j]
            for n in range(cpb):
                sts_ref[n, :, cs(j)] = st
                o_x[n, j] = _dot(gates[n, j]["q_out"].astype(BF16), st.astype(BF16), NT)
                st = st * gates[n, j]["dec"] + kv[n, j]
            st_ref[j] = st
        for n, j in units:
            o = o_in[n, j] + o_x[n, j]
            r = lax.rsqrt(jnp.mean(o * o, axis=-1, keepdims=True) + EPS)
            gp = p_ref[3, rs(n), cs(j)]
            y_ref[rs(n), cs(j)] = ((o * r * gain) * (gp * _sigmoid(gp))).astype(BF16)

    return _call(
        body, name=name, grid=(nh // HG_WIDE, nb, ns),
        out_shape=[jax.ShapeDtypeStruct((m_rows, di), BF16),
                   jax.ShapeDtypeStruct((nb * nc, HEAD_DIM, di), F32)],
        in_specs=[pl.BlockSpec((4, rows, wide), lambda hg, b, s: (0, b * ns + s, hg)),
                  pl.BlockSpec((2, wide), lambda hg, b, s: (0, hg)),
                  pl.BlockSpec((1, HEAD_DIM), lambda hg, b, s: (0, 0))],
        out_specs=[pl.BlockSpec((rows, wide), lambda hg, b, s: (b * ns + s, hg)),
                   pl.BlockSpec((cpb, HEAD_DIM, wide), lambda hg, b, s: (b * ns + s, 0, hg))],
        scratch_shapes=[pltpu.VMEM((HG_WIDE, HEAD_DIM, HEAD_DIM), F32)],
        compiler_params=_params(),
    )(proj4, lbraw, gn)


def hgrn_bwd(proj4, dy, sts, lbraw, gn, seq, name):
    _, m_rows, di = proj4.shape
    nb, nh, nc = m_rows // seq, di // HEAD_DIM, seq // CHUNK
    rows = min(HG_ROWS, seq)
    wide = HG_WIDE * HEAD_DIM
    ns, cpb = seq // rows, rows // CHUNK
    n_hg = nh // HG_WIDE

    def body(p_ref, dy_ref, sts_ref, lb_ref, gn_ref, dp_ref, dlb_ref, dgn_ref, dst_ref, lbacc_ref, gnacc_ref):
        hg, b, s = pl.program_id(0), pl.program_id(1), pl.program_id(2)
        tri, triu = _tri(True), _tri(False)
        causal = _causal()
        gain = gn_ref[...]
        first = (b == 0) & (s == 0)
        cs = lambda j: slice(j * HEAD_DIM, (j + 1) * HEAD_DIM)

        def before():
            @pl.when((hg == 0) & first)
            def _():
                gnacc_ref[...] = jnp.zeros_like(gnacc_ref)

            @pl.when(first)
            def _():
                lbacc_ref[...] = jnp.zeros_like(lbacc_ref)

            @pl.when(s == 0)
            def _():
                dst_ref[...] = jnp.zeros_like(dst_ref)

        def after():
            @pl.when((b == nb - 1) & (s == ns - 1))
            def _():
                for j in range(HG_WIDE):
                    _, p0, p1 = _lower_bound(lb_ref[:, cs(j)])
                    acc = lbacc_ref[:, cs(j)]
                    dlb_ref[0:1, cs(j)] = -acc * p0 * p1
                    dlb_ref[1:2, cs(j)] = acc * p1 * (1.0 - p1)

            @pl.when((hg == n_hg - 1) & (b == nb - 1) & (s == ns - 1))
            def _():
                tot = gnacc_ref[:, 0:HEAD_DIM]
                for j in range(1, HG_WIDE):
                    tot = tot + gnacc_ref[:, cs(j)]
                dgn_ref[...] = tot

        before()

        units = [(n, j) for n in range(cpb) for j in range(HG_WIDE)]
        rs = lambda n: slice(n * CHUNK, (n + 1) * CHUNK)
        lbs = [_lower_bound(lb_ref[:, cs(j)])[0] for j in range(HG_WIDE)]
        gates, v_b, st_b, sc_b, o, do_b = {}, {}, {}, {}, {}, {}
        dq_out, dsc_b, dv, g_st, dq_in, dk_in, dst_at, dk_out, ddec = {}, {}, {}, {}, {}, {}, {}, {}, {}
        for n, j in units:
            gates[n, j] = _hgrn_gates(p_ref[0, rs(n), cs(j)], p_ref[1, rs(n), cs(j)], lbs[j], tri)
            v_b[n, j] = p_ref[2, rs(n), cs(j)].astype(BF16)
            st_b[n, j] = sts_ref[n, :, cs(j)].astype(BF16)
        for u in units:
            t = gates[u]
            sc_b[u] = jnp.where(causal, _dot(t["q_in"].astype(BF16), t["k_in"].astype(BF16), NT), 0.0).astype(BF16)
        for u in units:
            o[u] = _dot(sc_b[u], v_b[u], NN) + _dot(gates[u]["q_out"].astype(BF16), st_b[u], NT)
        for n, j in units:
            ov = o[n, j]
            r = lax.rsqrt(jnp.mean(ov * ov, axis=-1, keepdims=True) + EPS)
            ohat = ov * r
            sg, dsg = _silu_and_grad(p_ref[3, rs(n), cs(j)])
            dyv = dy_ref[rs(n), cs(j)].astype(F32)
            dp_ref[3, rs(n), cs(j)] = (dyv * (ohat * gain) * dsg).astype(BF16)
            d_on = dyv * sg
            gnacc_ref[:, cs(j)] = gnacc_ref[:, cs(j)] + jnp.sum(d_on * ohat, axis=0, keepdims=True)
            dohat = d_on * gain
            do_b[n, j] = (r * (dohat - ohat * jnp.mean(dohat * ohat, axis=-1, keepdims=True))).astype(BF16)
        for u in units:
            dq_out[u] = _dot(do_b[u], st_b[u], NN)
            dsc_b[u] = jnp.where(causal, _dot(do_b[u], v_b[u], NT), 0.0).astype(BF16)
            dv[u] = _dot(sc_b[u], do_b[u], TN)
            g_st[u] = _dot(do_b[u], gates[u]["q_out"].astype(BF16), TN)
        for u in units:
            dq_in[u] = _dot(dsc_b[u], gates[u]["k_in"].astype(BF16), NN)
            dk_in[u] = _dot(dsc_b[u], gates[u]["q_in"].astype(BF16), TN)
        for j in range(HG_WIDE):
            dst = dst_ref[j]
            for n in reversed(range(cpb)):
                dst_at[n, j] = dst
                dst = dst * gates[n, j]["dec"] + g_st[n, j]
            dst_ref[j] = dst
        for n, j in units:
            dst = dst_at[n, j]
            dst_b = dst.astype(BF16)
            dk_out[n, j] = _dot(v_b[n, j], dst_b, NN)
            dv[n, j] = dv[n, j] + _dot(gates[n, j]["k_out"].astype(BF16), dst_b, NT)
            ddec[n, j] = jnp.sum(dst * sts_ref[n, :, cs(j)], axis=0, keepdims=True)
        for n, j in units:
            t = gates[n, j]
            dp_ref[2, rs(n), cs(j)] = dv[n, j].astype(BF16)
            dq = dq_in[n, j] * t["e1"] + dq_out[n, j] * t["e3"]
            dk = dk_in[n, j] * t["e2"] + dk_out[n, j] * t["e4"]
            w_in = dq_in[n, j] * t["q_in"] - dk_in[n, j] * t["k_in"]
            w_out = dk_out[n, j] * t["k_out"]
            da = w_in + dq_out[n, j] * t["q_out"] - w_out
            da_mid = -jnp.sum(w_in, axis=0, keepdims=True)
            da_last = jnp.sum(w_out, axis=0, keepdims=True) + ddec[n, j] * t["dec"]
            rid = lax.broadcasted_iota(jnp.int32, da.shape, 0)
            da = da + jnp.where(rid == CHUNK // 2 - 1, da_mid, 0.0) + jnp.where(rid == CHUNK - 1, da_last, 0.0)
            dlf = _running_sum(triu, da)
            df = dlf / t["f"] - dk
            sgm = t["sgm"]
            dp_ref[1, rs(n), cs(j)] = (df * (1.0 - lbs[j]) * sgm * (1.0 - sgm)).astype(BF16)
            lbacc_ref[:, cs(j)] = lbacc_ref[:, cs(j)] + jnp.sum(df * (1.0 - sgm), axis=0, keepdims=True)
            dp_ref[0, rs(n), cs(j)] = (dq * t["dq"]).astype(BF16)

        after()

    blk = lambda hg, b, s: b * ns + (ns - 1 - s)
    return _call(
        body, name=name, grid=(n_hg, nb, ns),
        out_shape=[jax.ShapeDtypeStruct((4, m_rows, di), BF16), jax.ShapeDtypeStruct((2, di), F32),
                   jax.ShapeDtypeStruct((1, HEAD_DIM), F32)],
        in_specs=[pl.BlockSpec((4, rows, wide), lambda hg, b, s: (0, blk(hg, b, s), hg)),
                  pl.BlockSpec((rows, wide), lambda hg, b, s: (blk(hg, b, s), hg)),
                  pl.BlockSpec((cpb, HEAD_DIM, wide), lambda hg, b, s: (blk(hg, b, s), 0, hg)),
                  pl.BlockSpec((2, wide), lambda hg, b, s: (0, hg)),
                  pl.BlockSpec((1, HEAD_DIM), lambda hg, b, s: (0, 0))],
        out_specs=[pl.BlockSpec((4, rows, wide), lambda hg, b, s: (0, blk(hg, b, s), hg)),
                   pl.BlockSpec((2, wide), lambda hg, b, s: (0, hg)),
                   pl.BlockSpec((1, HEAD_DIM), lambda hg, b, s: (0, 0))],
        scratch_shapes=[pltpu.VMEM((HG_WIDE, HEAD_DIM, HEAD_DIM), F32), pltpu.VMEM((1, wide), F32),
                        pltpu.VMEM((1, wide), F32)],
        compiler_params=_params(),
    )(proj4, dy, sts, lbraw, gn)


def outproj_loss(y, w, x, mod, fg, target, seq, name):
    m_rows, di = y.shape
    d = w.shape[1]
    tm = min(512, seq)

    def body(y_ref, w_ref, x_ref, mod_ref, fg_ref, t_ref, out_ref, loss_ref, dx_ref, dfg_ref):
        i = pl.program_id(0)
        acc = _dot(y_ref[...], w_ref[...], NN)
        out_ref[...] = acc.astype(BF16)
        xv = x_ref[...] + mod_ref[0][:, 2 * d:] * acc
        gain = fg_ref[...]
        r = lax.rsqrt(jnp.mean(xv * xv, axis=-1, keepdims=True) + EPS)
        xn = xv * r
        e = xn * gain - t_ref[...]
        part = 0.5 * jnp.sum(jnp.mean(e * e, axis=-1, keepdims=True), axis=0, keepdims=True)
        dyv = e / d
        p_fg = jnp.sum(dyv * xn, axis=0, keepdims=True)
        dxn = dyv * gain
        dx_ref[...] = r * (dxn - xn * jnp.mean(dxn * xn, axis=-1, keepdims=True))

        @pl.when(i == 0)
        def _():
            loss_ref[...] = part
            dfg_ref[...] = p_fg

        @pl.when(i != 0)
        def _():
            loss_ref[...] = loss_ref[...] + part
            dfg_ref[...] = dfg_ref[...] + p_fg

    row = pl.BlockSpec((tm, d), lambda i: (i, 0))
    return _call(
        body, name=name, grid=(m_rows // tm,),
        out_shape=[jax.ShapeDtypeStruct((m_rows, d), BF16), jax.ShapeDtypeStruct((1, 1), F32),
                   jax.ShapeDtypeStruct((m_rows, d), F32), jax.ShapeDtypeStruct((1, d), F32)],
        in_specs=[pl.BlockSpec((tm, di), lambda i: (i, 0)),
                  pl.BlockSpec((di, d), lambda i: (0, 0)),
                  row,
                  pl.BlockSpec((1, 1, 3 * d), lambda i: ((i * tm) // seq, 0, 0)),
                  pl.BlockSpec((1, d), lambda i: (0, 0)), row],
        out_specs=[row, pl.BlockSpec((1, 1), lambda i: (0, 0)), row, pl.BlockSpec((1, d), lambda i: (0, 0))],
        compiler_params=_params(),
    )(y, w, x, mod, fg, target)


def _pack(parts):
    flat = jnp.concatenate([p.reshape(-1) for p in parts])
    pad = (-flat.shape[0]) % (8 * LANES)
    return jnp.pad(flat, (0, pad)).reshape(-1, LANES)


def kernel(x, c, norm_gain, w_ada, b_ada, a_w_in, a_ln_gain, a_ln_bias, a_w_s, a_b_s, a_w_out, b_w_in, b_lower_bounds, b_gn_gain, b_w_out, final_gain, loss_target, m_norm_gain, m_w_ada, m_b_ada, m_a_w_in, m_a_ln_gain, m_a_ln_bias, m_a_w_s, m_a_b_s, m_a_w_out, m_b_w_in, m_b_lower_bounds, m_b_gn_gain, m_b_w_out, m_final_gain, v_norm_gain, v_w_ada, v_b_ada, v_a_w_in, v_a_ln_gain, v_a_ln_bias, v_a_w_s, v_a_b_s, v_a_w_out, v_b_w_in, v_b_lower_bounds, v_b_gn_gain, v_b_w_out, v_final_gain):
    nb, seq, d = x.shape
    m_rows = nb * seq
    n_l = w_ada.shape[0]
    ada_cols = w_ada.shape[2]
    px, py, pc = _place()
    chip = 2 * px + py
    dev = 2 * chip + pc

    c_all = allgather_small(c.reshape(-1, LANES), "gather_c").reshape(N_DEV * nb, d)
    s_a = halves_start([cast_into_slot(a_w_in[0], chip, c_all, "cast_a_in")], "gather_a_in_start")
    land_b_in = cast_into_slot(b_w_in[0], chip, s_a[3], "cast_b_in")
    land_b_out = cast_into_slot(b_w_out[0], chip, land_b_in, "cast_b_out")
    b_cols = lax.dynamic_slice_in_dim(b_ada, chip * ada_cols, ada_cols, axis=1).reshape(n_l, 1, ada_cols)
    mod_cols = ada_fwd(c_all, w_ada, b_cols, land_b_out, "ada_fwd")
    mod_g = allgather_small(mod_cols.reshape(-1, LANES), "gather_mod")
    mod_g = mod_g.reshape(N_CHIPS, 2, n_l, N_DEV * nb, ada_cols)[:, 0]
    mod_all = jnp.transpose(mod_g, (1, 2, 0, 3)).reshape(n_l, N_DEV * nb, 3 * d)
    mod_mine = lax.dynamic_slice_in_dim(mod_all, dev * nb, nb, axis=1)
    mod0 = mod_mine[0].reshape(nb, 1, 3 * d)
    mod1 = mod_mine[1].reshape(nb, 1, 3 * d)

    s_ao = halves_start([cast_into_slot(a_w_out[0], chip, mod_mine, "cast_a_out")], "gather_a_out_start")
    landed_a = halves_wait(s_a[0], s_a[1], s_a[2], s_ao[3], "gather_a_in_wait")
    s_bi = gather_start(land_b_in, landed_a[0], "gather_b_in_start")
    s_bo = gather_start(land_b_out, s_bi[3], "gather_b_out_start")
    (wa_in,) = pass_halves(list(landed_a), "gather_a_in_pass")
    di = a_w_out.shape[1] * N_CHIPS

    x0 = x.reshape(m_rows, d)
    tgt = loss_target.reshape(m_rows, d)
    ng0 = norm_gain[0:1] + (s_bi[3][0, 0] + s_bo[3][0, 0])
    ng1 = norm_gain[1:2]
    bs_col = a_b_s[0].reshape(SG_GROUPS, SG_BLOCK, 1)
    proj_a, h_a = inproj_fwd(x0, mod0, ng0, wa_in, seq, False, "a_inproj")
    y_a = sgu_fwd(proj_a, a_ln_gain, a_ln_bias, a_w_s[0], bs_col, "a_sgu")
    (wa_out,) = pass_halves(list(halves_wait(s_ao[0], s_ao[1], s_ao[2], y_a, "gather_a_out_wait")), "gather_a_out_pass")
    wa_out = wa_out.reshape(di, d)
    x1, out_a = outproj_fwd(y_a, wa_out, x0, mod0, seq, "a_outproj")
    wb_in = gather_wait(*s_bi[:3], out_a, "gather_b_in_wait")
    proj_b, h_b = inproj_fwd(x1, mod1, ng1, wb_in, seq, True, "b_inproj")
    y_b, sts_b = hgrn_fwd(proj_b, b_lower_bounds, b_gn_gain, seq, "b_hgrn")
    wb_out = gather_wait(*s_bo[:3], y_b, "gather_b_out_wait").reshape(di, d)
    out_b, loss_part, dx2, dfg = outproj_loss(
        y_b, wb_out, x1, mod1, final_gain.reshape(1, d), tgt, seq, "b_outproj_loss")

    shard_rows = di // N_CHIPS
    dy_b, dout_b, dgate1 = outproj_bwd(dx2, out_b, mod1, wb_out, seq, "b_outproj_bwd")
    gwb_out = grad_w_out(y_b, dout_b, "b_grad_w_out").reshape(N_CHIPS, shard_rows, d)
    e_bo = exchange_start(gwb_out, "exchange_b_out_start")
    dproj_b, dlb, dgn = hgrn_bwd(
        proj_b, dy_b, sts_b, b_lower_bounds, b_gn_gain + e_bo[4][0, 0], seq, "b_hgrn_bwd")
    e_bi = exchange_start(grad_w_in(h_b, dproj_b, N_CHIPS, True, "b_grad_w_in"), "exchange_b_in_start")
    dx1, dshift1, dscale1, dng1 = inproj_bwd(
        dproj_b, wb_in, x1, dx2, mod1, ng1 + e_bi[4][0, 0], seq, True, "b_inproj_bwd")

    dy_a, dout_a, dgate0 = outproj_bwd(dx1, out_a, mod0, wa_out, seq, "a_outproj_bwd")
    gwa_out = grad_w_out(y_a, dout_a, "a_grad_w_out").reshape(N_CHIPS, shard_rows, d)
    e_ao = exchange_start(gwa_out, "exchange_a_out_start")
    dproj_a, dws, dbs, dlg, dlbias = sgu_bwd(
        proj_a, dy_a, a_ln_gain + e_ao[4][0, 0], a_ln_bias, a_w_s[0], bs_col, "a_sgu_bwd")
    e_ai = exchange_start(grad_w_in(h_a, dproj_a, N_CHIPS, False, "a_grad_w_in"), "exchange_a_in_start")
    dx0, dshift0, dscale0, dng0 = inproj_bwd(
        dproj_a, wa_in, x0, dx1, mod0, norm_gain[0:1] + e_ai[4][0, 0], seq, False, "a_inproj_bwd")
    grad_x = dx0.reshape(nb, seq, d)

    dmod = jnp.concatenate([dshift0, dscale0, dgate0, dshift1, dscale1, dgate1], axis=2)
    n_dmod = dmod.size
    small_g = [jnp.concatenate([dng0, dng1], axis=0), dlg, dlbias, dws, dbs, dlb, dfg, dgn]
    packed_g = _pack([dmod] + small_g + [loss_part])
    rows = packed_g.shape[0]
    s_small = gather_all_start(
        lax.dynamic_update_slice(jnp.zeros((N_DEV, rows, LANES), F32), packed_g[None], (dev, 0, 0)),
        "gather_small_start")

    def finish(group, after):
        mine = []
        for ex, _, _, _, nm in group:
            parts_thru, land = exchange_wait(ex[0], ex[1], ex[2], ex[3], after, "exchange_" + nm + "_wait")
            mine.append(sum_parts(parts_thru, land, chip, "sum_" + nm))
            after = mine[-1]
        theirs = swap_sibling(mine, "swap_" + group[0][4])
        return [[r.reshape(w.shape) for r in adamw_pair(pa, pb, w[0], m[0], v[0], "adamw_" + nm)]
                for pa, pb, (_, w, m, v, nm) in zip(mine, theirs, group)]

    (gb_out, db_out, mb_out, vb_out), (gb_in, db_in, mb_in, vb_in), (ga_out, da_out, ma_out, va_out) = finish(
        [(e_bo, b_w_out, m_b_w_out, v_b_w_out, "b_out"), (e_bi, b_w_in, m_b_w_in, v_b_w_in, "b_in"),
         (e_ao, a_w_out, m_a_w_out, v_a_w_out, "a_out")], s_small[3])
    ((ga_in, da_in, ma_in, va_in),) = finish([(e_ai, a_w_in, m_a_w_in, v_a_w_in, "a_in")], ga_out)

    small_w = [norm_gain, a_ln_gain, a_ln_bias, a_w_s, a_b_s, b_lower_bounds, final_gain, b_gn_gain]
    small_m = [m_norm_gain, m_a_ln_gain, m_a_ln_bias, m_a_w_s, m_a_b_s, m_b_lower_bounds, m_final_gain, m_b_gn_gain]
    small_v = [v_norm_gain, v_a_ln_gain, v_a_ln_bias, v_a_w_s, v_a_b_s, v_b_lower_bounds, v_final_gain, v_b_gn_gain]
    rows_of = lambda a: a.reshape(-1, a.shape[-1])
    gathered = gather_all_wait(s_small[0], s_small[1], s_small[2], ga_in, "gather_small_wait")
    tail, small_res = small_update(
        gathered, n_dmod // LANES, [rows_of(a) for a in small_w], [rows_of(a) for a in small_m],
        [rows_of(a) for a in small_v], "small_update")
    loss = tail[0, 0]
    sg, sd, sm, sv = [[small_res[p][kind].reshape(w.shape) for p, w in enumerate(small_w)] for kind in range(4)]

    dmod_all = gathered[:, :n_dmod // LANES].reshape(N_DEV * nb, n_l, 3 * d)
    dmod_cols = lax.dynamic_slice_in_dim(dmod_all, chip * ada_cols, ada_cols, axis=2)
    dmod_cols = jnp.transpose(dmod_cols, (1, 0, 2))
    g_wada, d_wada, m_wada, v_wada = ada_bwd(c_all, dmod_cols, w_ada, m_w_ada, v_w_ada, "ada_bwd")
    flat = lambda a: a.reshape(1, -1)
    g_bada, d_bada, m_bada, v_bada = [
        r.reshape(b_ada.shape) for r in
        bias_update(dmod_all.reshape(N_DEV * nb, n_l * 3 * d), flat(b_ada), flat(m_b_ada), flat(v_b_ada), "bias_update")]

    def order(ng, wada, bada, ain, sm_rest, aout, bin_, bout):
        lg, lbi, ws_, bs_, lbd, fg_, gn_ = sm_rest
        return [ng, wada, bada, ain, lg, lbi, ws_, bs_, aout, bin_, lbd, gn_, bout, fg_]

    grads = order(sg[0], g_wada, g_bada, ga_in, sg[1:8], ga_out, gb_in, gb_out)
    deltas = order(sd[0], d_wada, d_bada, da_in, sd[1:8], da_out, db_in, db_out)
    new_m = order(sm[0], m_wada, m_bada, ma_in, sm[1:8], ma_out, mb_in, mb_out)
    new_v = order(sv[0], v_wada, v_bada, va_in, sv[1:8], va_out, vb_in, vb_out)
    return (loss, grad_x, *grads, *deltas, *new_m, *new_v)
```

```python
import jax
import jax.numpy as jnp
from jax import lax
from jax.experimental import pallas as pl
from jax.experimental.pallas import tpu as pltpu

F32 = jnp.float32
BF16 = jnp.bfloat16
EPS = 1e-6
CHUNK = 64
SG_BLOCK = 128
SG_GROUPS = 8
SG_STEP_BLOCKS = 4
HEAD_DIM = 128
HG_WIDE = 8
HG_ROWS = 256
N_CHIPS = 4
N_DEV = 8
LANES = 128
ADAM_LR = 0.001
ADAM_B1 = 0.9
ADAM_B2 = 0.999
ADAM_EPS = 1e-08
ADAM_WD = 0.01
ADAM_STEP = 10
GELU_C0 = 0.7978845608028654
GELU_C1 = 0.044715
MESH = pl.DeviceIdType.MESH
VMEM_LIMIT = 56 * 1024 * 1024


ROW_TILE = 1024


def _col_tile(n):
    return next(t for t in (1024, 768, 512, 256) if n % t == 0)


def _call(body, **kw):
    return pl.pallas_call(body, **kw)


def _params(**kw):
    return pltpu.CompilerParams(vmem_limit_bytes=VMEM_LIMIT, **kw)


def _sigmoid(x):
    return 0.5 * jnp.tanh(0.5 * x) + 0.5


def _sigmoid_small(x):
    return 1.0 / (1.0 + jnp.exp(-x))


def _silu_and_grad(x):
    s = _sigmoid(x)
    return x * s, s * (1.0 + x * (1.0 - s))


def _gelu(x):
    return 0.5 * x * (1.0 + jnp.tanh(GELU_C0 * (x + GELU_C1 * x * x * x)))


def _gelu_and_grad(x):
    t = jnp.tanh(GELU_C0 * (x + GELU_C1 * x * x * x))
    g = 0.5 * x * (1.0 + t)
    dg = 0.5 * (1.0 + t) + 0.5 * x * (1.0 - t * t) * (GELU_C0 * (1.0 + 3.0 * GELU_C1 * x * x))
    return g, dg


def _dot(a, b, dims, precision=None):
    return lax.dot_general(a, b, (dims, ((), ())), precision=precision, preferred_element_type=F32)


NN = ((1,), (0,))
NT = ((1,), (1,))
TN = ((0,), (0,))


def _adamw(w, g, m, v):
    m = ADAM_B1 * m + (1.0 - ADAM_B1) * g
    v = ADAM_B2 * v + (1.0 - ADAM_B2) * (g * g)
    m_hat = m / (1.0 - ADAM_B1 ** ADAM_STEP)
    v_hat = v / (1.0 - ADAM_B2 ** ADAM_STEP)
    delta = -ADAM_LR * (m_hat / (jnp.sqrt(v_hat) + ADAM_EPS) + ADAM_WD * w)
    return delta, m, v


def _chunk_mask():
    r = lax.broadcasted_iota(jnp.int32, (SG_BLOCK, SG_BLOCK), 0)
    c = lax.broadcasted_iota(jnp.int32, (SG_BLOCK, SG_BLOCK), 1)
    return (c // CHUNK) <= (r // CHUNK)


def _place():
    return lax.axis_index("x"), lax.axis_index("y"), lax.axis_index("c")


def _other_chips(x, y):
    return [(1 - x, y), (x, 1 - y), (1 - x, 1 - y)]


def allgather_small(v, name):
    m_per, n = v.shape

    def body(x_ref, out_ref, send_sems, recv_sems, local_sem):
        x, y, c = _place()
        me, sibling = (x, y, c), (x, y, 1 - c)
        chips = _other_chips(x, y)

        def rows(px, py, pc):
            return out_ref.at[pl.ds((4 * px + 2 * py + pc) * m_per, m_per), :]

        def copy(k, block, to, src=None):
            return pltpu.make_async_remote_copy(
                src_ref=rows(*block) if src is None else src, dst_ref=rows(*block),
                send_sem=send_sems.at[k], recv_sem=recv_sems.at[k], device_id=to, device_id_type=MESH)

        mine = pltpu.make_async_copy(x_ref, rows(*me), local_sem)
        mine.start()
        first = [copy(0, me, sibling, src=x_ref)]
        first += [copy(1 + j, me, (*chip, c), src=x_ref) for j, chip in enumerate(chips)]
        for cp in first:
            cp.start()
        passed = [copy(4 + j, (*chip, c), sibling) for j, chip in enumerate(chips)]
        for j, chip in enumerate(chips):
            copy(1 + j, (*chip, c), me).wait_recv()
            passed[j].start()
        copy(0, sibling, me).wait_recv()
        for j, chip in enumerate(chips):
            copy(4 + j, (*chip, 1 - c), me).wait_recv()
        for cp in first + passed:
            cp.wait_send()
        mine.wait()

    return _call(
        body, name=name,
        out_shape=jax.ShapeDtypeStruct((N_DEV * m_per, n), v.dtype),
        in_specs=[pl.BlockSpec(memory_space=pltpu.VMEM)],
        out_specs=pl.BlockSpec(memory_space=pltpu.VMEM),
        scratch_shapes=[pltpu.SemaphoreType.DMA((7,)), pltpu.SemaphoreType.DMA((7,)), pltpu.SemaphoreType.DMA],
    )(v)


def _hbm_spec():
    return pl.BlockSpec(memory_space=pltpu.HBM)


def _sem_spec():
    return pl.BlockSpec(memory_space=pltpu.SEMAPHORE)


def _split_params():
    return pltpu.CompilerParams(has_side_effects=pltpu.SideEffectType.DATAFLOW_SIDE_EFFECTING)


def _hbm(a):
    return pltpu.with_memory_space_constraint(a, pltpu.HBM)


def _half_copy(land_ref, rows, chip_idx, core_half, send_sem, recv_sem, to):
    half = land_ref.at[chip_idx, pl.ds(core_half * (rows // 2), rows // 2), :]
    return pltpu.make_async_remote_copy(
        src_ref=half, dst_ref=half, send_sem=send_sem, recv_sem=recv_sem, device_id=to, device_id_type=MESH)


def halves_start(lands, name):
    n = len(lands)

    def body(*refs):
        land_refs, send_sems, recv_sems, token = refs[:n], refs[n], refs[n + 1], refs[-1]
        x, y, c = _place()
        for w in range(n):
            for j, (px, py) in enumerate(_other_chips(x, y)):
                _half_copy(land_refs[w], lands[w].shape[1], 2 * x + y, c,
                           send_sems.at[3 * w + j], recv_sems.at[3 * w + j], (px, py, c)).start()
        token[...] = jnp.zeros_like(token)

    res = _call(
        body, name=name,
        out_shape=(pltpu.SemaphoreType.DMA((3 * n,)), pltpu.SemaphoreType.DMA((3 * n,)),
                   *[pltpu.HBM(a.shape, a.dtype) for a in lands], jax.ShapeDtypeStruct((8, LANES), F32)),
        in_specs=(_hbm_spec(),) * n,
        out_specs=(_sem_spec(), _sem_spec(), *[_hbm_spec()] * n, pl.BlockSpec(memory_space=pltpu.VMEM)),
        input_output_aliases={w: 2 + w for w in range(n)}, compiler_params=_split_params(),
    )(*[_hbm(a) for a in lands])
    return res[0], res[1], list(res[2:2 + n]), res[2 + n]


def halves_wait(send_sems, recv_sems, lands, after, name):
    n = len(lands)

    def body(*refs):
        land_refs, send_sems, recv_sems = refs[:n], refs[n], refs[n + 1]
        x, y, c = _place()
        for w in range(n):
            for j, (px, py) in enumerate(_other_chips(x, y)):
                cp = _half_copy(land_refs[w], lands[w].shape[1], 2 * px + py, c,
                                send_sems.at[3 * w + j], recv_sems.at[3 * w + j], (px, py, c))
                cp.wait_send()
                cp.wait_recv()

    return _call(
        body, name=name,
        out_shape=tuple(pltpu.HBM(a.shape, a.dtype) for a in lands),
        in_specs=(*[_hbm_spec()] * n, _sem_spec(), _sem_spec(), pl.BlockSpec(memory_space=pl.ANY)),
        out_specs=tuple(_hbm_spec() for _ in lands), input_output_aliases={w: w for w in range(n)},
        compiler_params=_split_params(),
    )(*lands, send_sems, recv_sems, after)


def pass_halves(lands, name):
    n = len(lands)

    def body(*refs):
        land_refs, send_sems, recv_sems = refs[n:2 * n], refs[2 * n], refs[2 * n + 1]
        x, y, c = _place()
        sent = []
        for w in range(n):
            for j, (px, py) in enumerate(_other_chips(x, y)):
                cp = _half_copy(land_refs[w], lands[w].shape[1], 2 * px + py, c,
                                send_sems.at[3 * w + j], recv_sems.at[3 * w + j], (x, y, 1 - c))
                cp.start()
                sent.append(cp)
        for w in range(n):
            for j, (px, py) in enumerate(_other_chips(x, y)):
                _half_copy(land_refs[w], lands[w].shape[1], 2 * px + py, 1 - c,
                           send_sems.at[3 * w + j], recv_sems.at[3 * w + j], (x, y, 1 - c)).wait_recv()
        for cp in sent:
            cp.wait_send()

    return _call(
        body, name=name,
        out_shape=[jax.ShapeDtypeStruct(a.shape, a.dtype) for a in lands],
        in_specs=[_hbm_spec()] * n, out_specs=[_hbm_spec()] * n,
        input_output_aliases={w: w for w in range(n)},
        scratch_shapes=[pltpu.SemaphoreType.DMA((3 * n,)), pltpu.SemaphoreType.DMA((3 * n,))],
    )(*lands)


def gather_start(land, after, name):
    def body(land_ref, after_ref, send_sems, recv_sems, land_thru, token):
        del after_ref, land_thru
        x, y, c = _place()
        for j, (px, py) in enumerate(_other_chips(x, y)):
            pltpu.make_async_remote_copy(
                src_ref=land_ref.at[2 * x + y], dst_ref=land_ref.at[2 * x + y],
                send_sem=send_sems.at[j], recv_sem=recv_sems.at[j], device_id=(px, py, c),
                device_id_type=MESH).start()
        token[...] = jnp.zeros_like(token)

    return _call(
        body, name=name,
        out_shape=(pltpu.SemaphoreType.DMA((3,)), pltpu.SemaphoreType.DMA((3,)),
                   pltpu.HBM(land.shape, land.dtype), jax.ShapeDtypeStruct((8, LANES), F32)),
        in_specs=(_hbm_spec(), pl.BlockSpec(memory_space=pl.ANY)),
        out_specs=(_sem_spec(), _sem_spec(), _hbm_spec(), pl.BlockSpec(memory_space=pltpu.VMEM)),
        input_output_aliases={0: 2}, compiler_params=_split_params(),
    )(_hbm(land), after)


def gather_wait(send_sems, recv_sems, land, after, name):
    def body(land_ref, send_sems, recv_sems, after_ref, land_out):
        del after_ref, land_out
        x, y, c = _place()
        for j, (px, py) in enumerate(_other_chips(x, y)):
            cp = pltpu.make_async_remote_copy(
                src_ref=land_ref.at[2 * x + y], dst_ref=land_ref.at[2 * px + py],
                send_sem=send_sems.at[j], recv_sem=recv_sems.at[j], device_id=(px, py, c), device_id_type=MESH)
            cp.wait_send()
            cp.wait_recv()

    return _call(
        body, name=name,
        out_shape=pltpu.HBM(land.shape, land.dtype),
        in_specs=(_hbm_spec(), _sem_spec(), _sem_spec(), pl.BlockSpec(memory_space=pl.ANY)),
        out_specs=_hbm_spec(), input_output_aliases={0: 0}, compiler_params=_split_params(),
    )(land, send_sems, recv_sems, after)


def _flips():
    return [(fx, fy, fc) for fx in (0, 1) for fy in (0, 1) for fc in (0, 1) if (fx, fy, fc) != (0, 0, 0)]


def _flipped(x, y, c, flip):
    fx, fy, fc = flip
    return (1 - x if fx else x, 1 - y if fy else y, 1 - c if fc else c)


def gather_all_start(land, name):
    def body(land_ref, send_sems, recv_sems, land_thru, token):
        del land_thru
        x, y, c = _place()
        for k, flip in enumerate(_flips()):
            pltpu.make_async_remote_copy(
                src_ref=land_ref.at[4 * x + 2 * y + c], dst_ref=land_ref.at[4 * x + 2 * y + c],
                send_sem=send_sems.at[k], recv_sem=recv_sems.at[k], device_id=_flipped(x, y, c, flip),
                device_id_type=MESH).start()
        token[...] = jnp.zeros_like(token)

    return _call(
        body, name=name,
        out_shape=(pltpu.SemaphoreType.DMA((7,)), pltpu.SemaphoreType.DMA((7,)),
                   pltpu.HBM(land.shape, land.dtype), jax.ShapeDtypeStruct((8, LANES), F32)),
        in_specs=(_hbm_spec(),),
        out_specs=(_sem_spec(), _sem_spec(), _hbm_spec(), pl.BlockSpec(memory_space=pltpu.VMEM)),
        input_output_aliases={0: 2}, compiler_params=_split_params(),
    )(_hbm(land))


def gather_all_wait(send_sems, recv_sems, land, after, name):
    def body(land_ref, send_sems, recv_sems, after_ref, land_out):
        del after_ref, land_out
        x, y, c = _place()
        for k, flip in enumerate(_flips()):
            px, py, pc = _flipped(x, y, c, flip)
            cp = pltpu.make_async_remote_copy(
                src_ref=land_ref.at[4 * x + 2 * y + c], dst_ref=land_ref.at[4 * px + 2 * py + pc],
                send_sem=send_sems.at[k], recv_sem=recv_sems.at[k], device_id=(px, py, pc), device_id_type=MESH)
            cp.wait_send()
            cp.wait_recv()

    return _call(
        body, name=name,
        out_shape=pltpu.HBM(land.shape, land.dtype),
        in_specs=(_hbm_spec(), _sem_spec(), _sem_spec(), pl.BlockSpec(memory_space=pl.ANY)),
        out_specs=_hbm_spec(), input_output_aliases={0: 0}, compiler_params=_split_params(),
    )(land, send_sems, recv_sems, after)


def exchange_start(parts, name):
    _, r, c_ = parts.shape

    def body(parts_ref, land_ref, send_sems, recv_sems, parts_thru, land_thru, token):
        del parts_thru, land_thru
        x, y, c = _place()
        for j, (px, py) in enumerate(_other_chips(x, y)):
            pltpu.make_async_remote_copy(
                src_ref=parts_ref.at[2 * px + py], dst_ref=land_ref.at[j],
                send_sem=send_sems.at[j], recv_sem=recv_sems.at[j], device_id=(px, py, c),
                device_id_type=MESH).start()
        token[...] = jnp.zeros_like(token)

    return _call(
        body, name=name,
        out_shape=(pltpu.SemaphoreType.DMA((3,)), pltpu.SemaphoreType.DMA((3,)),
                   pltpu.HBM(parts.shape, parts.dtype), pltpu.HBM((3, r, c_), parts.dtype),
                   jax.ShapeDtypeStruct((8, LANES), F32)),
        in_specs=(_hbm_spec(), _hbm_spec()),
        out_specs=(_sem_spec(), _sem_spec(), _hbm_spec(), _hbm_spec(), pl.BlockSpec(memory_space=pltpu.VMEM)),
        input_output_aliases={0: 2, 1: 3}, compiler_params=_split_params(),
    )(_hbm(parts), _hbm(lax.empty((3, r, c_), parts.dtype)))


def exchange_wait(send_sems, recv_sems, parts, land, after, name):
    def body(parts_ref, land_ref, send_sems, recv_sems, after_ref, parts_out, land_out):
        del after_ref, parts_out, land_out
        x, y, c = _place()
        for j, (px, py) in enumerate(_other_chips(x, y)):
            cp = pltpu.make_async_remote_copy(
                src_ref=parts_ref.at[2 * px + py], dst_ref=land_ref.at[j],
                send_sem=send_sems.at[j], recv_sem=recv_sems.at[j], device_id=(px, py, c), device_id_type=MESH)
            cp.wait_send()
            cp.wait_recv()

    return _call(
        body, name=name,
        out_shape=(pltpu.HBM(parts.shape, parts.dtype), pltpu.HBM(land.shape, land.dtype)),
        in_specs=(_hbm_spec(), _hbm_spec(), _sem_spec(), _sem_spec(), pl.BlockSpec(memory_space=pl.ANY)),
        out_specs=(_hbm_spec(), _hbm_spec()), input_output_aliases={0: 0, 1: 1},
        compiler_params=_split_params(),
    )(parts, land, send_sems, recv_sems, after)


def cast_into_slot(w, chip, after, name):
    r, c = w.shape
    tr = min(256, r)

    def body(s_ref, w_ref, after_ref, o_ref):
        del s_ref, after_ref
        o_ref[...] = w_ref[...].astype(BF16)

    return _call(
        body, name=name,
        grid_spec=pltpu.PrefetchScalarGridSpec(
            num_scalar_prefetch=1, grid=(r // tr,),
            in_specs=[pl.BlockSpec((tr, c), lambda i, s: (i, 0)), pl.BlockSpec(memory_space=pl.ANY)],
            out_specs=pl.BlockSpec((None, tr, c), lambda i, s: (s[0], i, 0))),
        out_shape=jax.ShapeDtypeStruct((N_CHIPS, r, c), BF16),
        compiler_params=_params(),
    )(chip.reshape(1).astype(jnp.int32), w, after)


def sum_parts(parts, land, chip, name):
    _, r, c = parts.shape
    tr = min(256, r)

    def body(s_ref, p_ref, l_ref, o_ref):
        del s_ref
        acc = p_ref[...].astype(F32) + l_ref[0].astype(F32)
        acc = acc + l_ref[1].astype(F32)
        o_ref[...] = (acc + l_ref[2].astype(F32)).astype(BF16)

    return _call(
        body, name=name,
        grid_spec=pltpu.PrefetchScalarGridSpec(
            num_scalar_prefetch=1, grid=(r // tr,),
            in_specs=[pl.BlockSpec((None, tr, c), lambda i, s: (s[0], i, 0)),
                      pl.BlockSpec((3, tr, c), lambda i, s: (0, i, 0))],
            out_specs=pl.BlockSpec((tr, c), lambda i, s: (i, 0))),
        out_shape=jax.ShapeDtypeStruct((r, c), BF16),
        compiler_params=_params(),
    )(chip.reshape(1).astype(jnp.int32), parts, land)


def swap_sibling(arrs, name):
    n = len(arrs)

    def body(*refs):
        ins, outs = refs[:n], refs[n:2 * n]
        send_sems, recv_sems = refs[2 * n:]
        x, y, c = _place()
        cps = []
        for w in range(n):
            cp = pltpu.make_async_remote_copy(
                src_ref=ins[w], dst_ref=outs[w], send_sem=send_sems.at[w], recv_sem=recv_sems.at[w],
                device_id=(x, y, 1 - c), device_id_type=MESH)
            cp.start()
            cps.append(cp)
        for cp in cps:
            cp.wait_recv()
        for cp in cps:
            cp.wait_send()

    return _call(
        body, name=name,
        out_shape=[jax.ShapeDtypeStruct(a.shape, a.dtype) for a in arrs],
        in_specs=[_hbm_spec()] * n, out_specs=[_hbm_spec()] * n,
        scratch_shapes=[pltpu.SemaphoreType.DMA((n,)), pltpu.SemaphoreType.DMA((n,))],
    )(*arrs)


def adamw_pair(pa, pb, w, m, v, name):
    r, c = w.shape
    tr = min(256, r)

    def body(pa_ref, pb_ref, w_ref, m_ref, v_ref, g_ref, d_ref, nm_ref, nv_ref):
        g = pa_ref[...].astype(F32) + pb_ref[...].astype(F32)
        d, nm, nv = _adamw(w_ref[...], g, m_ref[...], v_ref[...])
        g_ref[...] = g
        d_ref[...] = d
        nm_ref[...] = nm
        nv_ref[...] = nv

    spec = pl.BlockSpec((tr, c), lambda i: (i, 0))
    return _call(
        body, name=name, grid=(r // tr,),
        out_shape=[jax.ShapeDtypeStruct((r, c), F32)] * 4,
        in_specs=[spec] * 5, out_specs=[spec] * 4,
        compiler_params=_params(),
    )(pa, pb, w, m, v)


def small_update(gathered, first_row, ws, ms, vs, name):
    n_w = len(ws)
    total_rows = gathered.shape[1]

    def body(*refs):
        g_ref = refs[0]
        w_refs, m_refs, v_refs = refs[1:1 + n_w], refs[1 + n_w:1 + 2 * n_w], refs[1 + 2 * n_w:1 + 3 * n_w]
        tail_ref = refs[1 + 3 * n_w]
        outs = refs[2 + 3 * n_w:2 + 7 * n_w]
        sum_ref = refs[2 + 7 * n_w]
        acc = g_ref[0]
        for k in range(1, N_DEV):
            acc = acc + g_ref[k]
        sum_ref[...] = acc
        row = first_row
        for p in range(n_w):
            a, b = ws[p].shape
            per = b // LANES
            g_out, d_out, m_out, v_out = outs[4 * p:4 * p + 4]
            if per == 1:
                g_out[...] = sum_ref[row:row + a, :]
            else:
                for i in range(a):
                    for jc in range(per):
                        g_out[i:i + 1, jc * LANES:(jc + 1) * LANES] = sum_ref[row + i * per + jc:row + i * per + jc + 1, :]
            row += a * per
            dl, nm, nv = _adamw(w_refs[p][...], g_out[...], m_refs[p][...], v_refs[p][...])
            d_out[...] = dl
            m_out[...] = nm
            v_out[...] = nv
        tail_ref[...] = sum_ref[row:row + 1, :]

    out_shape = [jax.ShapeDtypeStruct((1, LANES), F32)]
    for w in ws:
        out_shape += [jax.ShapeDtypeStruct(w.shape, F32)] * 4
    res = _call(
        body, name=name, out_shape=out_shape,
        scratch_shapes=[pltpu.VMEM((total_rows, LANES), F32)],
        compiler_params=_params(),
    )(gathered, *ws, *ms, *vs)
    return res[0], [res[1 + 4 * p:5 + 4 * p] for p in range(n_w)]


def ada_fwd(c_all, w_ada, b_cols, after, name):
    n_l, d, cols = w_ada.shape
    nb = c_all.shape[0]
    tn = cols

    def body(c_ref, w_ref, b_ref, after_ref, o_ref):
        del after_ref
        cv = c_ref[...]
        ca = (cv * _sigmoid(cv)).astype(BF16)
        o_ref[...] = _dot(ca, w_ref[...].astype(BF16), NN) + b_ref[...]

    return _call(
        body, name=name, grid=(n_l, cols // tn),
        out_shape=jax.ShapeDtypeStruct((n_l, nb, cols), F32),
        in_specs=[pl.BlockSpec((nb, d), lambda l, j: (0, 0)),
                  pl.BlockSpec((None, d, tn), lambda l, j: (l, 0, j)),
                  pl.BlockSpec((None, 1, tn), lambda l, j: (l, 0, j)),
                  pl.BlockSpec(memory_space=pl.ANY)],
        out_specs=pl.BlockSpec((None, nb, tn), lambda l, j: (l, 0, j)),
        compiler_params=_params(),
    )(c_all, w_ada, b_cols, after)


def ada_bwd(c_all, dmod_cols, w, m, v, name):
    n_l, d, cols = w.shape
    nb = c_all.shape[0]
    tn = cols

    def body(c_ref, dm_ref, w_ref, m_ref, v_ref, g_ref, d_ref, nm_ref, nv_ref):
        cv = c_ref[...]
        ca = (cv * _sigmoid(cv)).astype(BF16)
        g = _dot(ca, dm_ref[...].astype(BF16), TN)
        dl, nm, nv = _adamw(w_ref[...], g, m_ref[...], v_ref[...])
        g_ref[...] = g
        d_ref[...] = dl
        nm_ref[...] = nm
        nv_ref[...] = nv

    wspec = pl.BlockSpec((None, d, tn), lambda l, j: (l, 0, j))
    return _call(
        body, name=name, grid=(n_l, cols // tn),
        out_shape=[jax.ShapeDtypeStruct((n_l, d, cols), F32)] * 4,
        in_specs=[pl.BlockSpec((nb, d), lambda l, j: (0, 0)),
                  pl.BlockSpec((None, nb, tn), lambda l, j: (l, 0, j)),
                  wspec, wspec, wspec],
        out_specs=[wspec] * 4,
        compiler_params=_params(),
    )(c_all, dmod_cols, w, m, v)


def bias_update(dmod_all, w, m, v, name):
    def body(dm_ref, w_ref, m_ref, v_ref, g_ref, d_ref, nm_ref, nv_ref):
        g = jnp.sum(dm_ref[...], axis=0, keepdims=True)
        dl, nm, nv = _adamw(w_ref[...], g, m_ref[...], v_ref[...])
        g_ref[...] = g
        d_ref[...] = dl
        nm_ref[...] = nm
        nv_ref[...] = nv

    return _call(
        body, name=name,
        out_shape=[jax.ShapeDtypeStruct(w.shape, F32)] * 4,
        compiler_params=_params(),
    )(dmod_all, w, m, v)


def inproj_fwd(x, mod, ng, wg, seq, sectioned, name):
    m_rows, d = x.shape
    nsh, _, ns = wg.shape
    n = nsh * ns
    tm, tn = min(2 * ROW_TILE, seq), _col_tile(ns)
    per = ns // tn

    def body(x_ref, mod_ref, ng_ref, w_ref, proj_ref, h_ref):
        @pl.when(pl.program_id(1) == 0)
        def _():
            xv = x_ref[...]
            r = lax.rsqrt(jnp.mean(xv * xv, axis=-1, keepdims=True) + EPS)
            md = mod_ref[0]
            h = (xv * r * ng_ref[...]) * (1.0 + md[:, d:2 * d]) + md[:, :d]
            h_ref[...] = h.astype(BF16)
        proj_ref[...] = _dot(h_ref[...], w_ref[...], NN)

    if sectioned:
        proj_shape = (nsh, m_rows, ns)
        proj_spec = pl.BlockSpec((None, tm, tn), lambda i, j: (j // per, i, j % per))
    else:
        proj_shape = (m_rows, n)
        proj_spec = pl.BlockSpec((tm, tn), lambda i, j: (i, j))
    return _call(
        body, name=name, grid=(m_rows // tm, n // tn),
        out_shape=[jax.ShapeDtypeStruct(proj_shape, F32), jax.ShapeDtypeStruct((m_rows, d), BF16)],
        in_specs=[pl.BlockSpec((tm, d), lambda i, j: (i, 0)),
                  pl.BlockSpec((1, 1, 3 * d), lambda i, j: ((i * tm) // seq, 0, 0)),
                  pl.BlockSpec((1, d), lambda i, j: (0, 0)),
                  pl.BlockSpec((None, d, tn), lambda i, j: (j // per, 0, j % per))],
        out_specs=[proj_spec, pl.BlockSpec((tm, d), lambda i, j: (i, 0))],
        compiler_params=_params(),
    )(x, mod, ng, wg)


def outproj_fwd(y, w, x, mod, seq, name):
    m_rows, di = y.shape
    d = w.shape[1]
    tm = min(ROW_TILE, seq)

    def body(y_ref, w_ref, x_ref, mod_ref, xn_ref, out_ref):
        acc = _dot(y_ref[...], w_ref[...], NN)
        out_ref[...] = acc.astype(BF16)
        xn_ref[...] = x_ref[...] + mod_ref[0][:, 2 * d:] * acc

    row = pl.BlockSpec((tm, d), lambda i: (i, 0))
    return _call(
        body, name=name, grid=(m_rows // tm,),
        out_shape=[jax.ShapeDtypeStruct((m_rows, d), F32), jax.ShapeDtypeStruct((m_rows, d), BF16)],
        in_specs=[pl.BlockSpec((tm, di), lambda i: (i, 0)),
                  pl.BlockSpec((di, d), lambda i: (0, 0)),
                  row,
                  pl.BlockSpec((1, 1, 3 * d), lambda i: ((i * tm) // seq, 0, 0))],
        out_specs=[row, row],
        compiler_params=_params(),
    )(y, w, x, mod)


def outproj_bwd(dxo, out, mod, w, seq, name):
    m_rows, d = dxo.shape
    di = w.shape[0]
    nb = m_rows // seq
    tm, tn = min(ROW_TILE, seq), di

    def body(dxo_ref, out_ref, mod_ref, w_ref, dy_ref, dout_ref, dgate_ref):
        i = pl.program_id(0)

        @pl.when(pl.program_id(1) == 0)
        def _():
            dx = dxo_ref[...]
            dout_ref[...] = (mod_ref[0][:, 2 * d:] * dx).astype(BF16)
            part = jnp.sum(dx * out_ref[...].astype(F32), axis=0, keepdims=True)

            @pl.when((i * tm) % seq == 0)
            def _():
                dgate_ref[0] = part

            @pl.when((i * tm) % seq != 0)
            def _():
                dgate_ref[0] = dgate_ref[0] + part

        dy_ref[...] = _dot(dout_ref[...], w_ref[...], NT).astype(BF16)

    row = pl.BlockSpec((tm, d), lambda i, j: (i, 0))
    return _call(
        body, name=name, grid=(m_rows // tm, di // tn),
        out_shape=[jax.ShapeDtypeStruct((m_rows, di), BF16), jax.ShapeDtypeStruct((m_rows, d), BF16),
                   jax.ShapeDtypeStruct((nb, 1, d), F32)],
        in_specs=[row, row,
                  pl.BlockSpec((1, 1, 3 * d), lambda i, j: ((i * tm) // seq, 0, 0)),
                  pl.BlockSpec((tn, d), lambda i, j: (j, 0))],
        out_specs=[pl.BlockSpec((tm, tn), lambda i, j: (i, j)), row,
                   pl.BlockSpec((1, 1, d), lambda i, j: ((i * tm) // seq, 0, 0))],
        compiler_params=_params(),
    )(dxo, out, mod, w)


def grad_w_out(y, dout, name):
    m_rows, di = y.shape
    d = dout.shape[1]
    tm, tk = min(ROW_TILE, m_rows), di
    n_m = m_rows // tm

    def body(y_ref, do_ref, o_ref, acc_ref):
        mi = pl.program_id(1)

        @pl.when(mi == 0)
        def _():
            acc_ref[...] = jnp.zeros_like(acc_ref)

        acc_ref[...] += _dot(y_ref[...], do_ref[...], TN)

        @pl.when(mi == n_m - 1)
        def _():
            o_ref[...] = acc_ref[...].astype(BF16)

    return _call(
        body, name=name, grid=(di // tk, n_m),
        out_shape=jax.ShapeDtypeStruct((di, d), BF16),
        in_specs=[pl.BlockSpec((tm, tk), lambda j, mi: (mi, j)),
                  pl.BlockSpec((tm, d), lambda j, mi: (mi, 0))],
        out_specs=pl.BlockSpec((tk, d), lambda j, mi: (j, 0)),
        scratch_shapes=[pltpu.VMEM((tk, d), F32)],
        compiler_params=_params(),
    )(y, dout)


def grad_w_in(h, dproj, nsh, sectioned, name):
    m_rows, d = h.shape
    n = dproj.shape[0] * dproj.shape[2] if sectioned else dproj.shape[1]
    ns = n // nsh
    tm, tn = min(ROW_TILE, m_rows), ns
    per = ns // tn
    n_m = m_rows // tm

    def body(h_ref, dp_ref, o_ref, acc_ref):
        mi = pl.program_id(1)
        @pl.when(mi == 0)
        def _():
            acc_ref[...] = jnp.zeros_like(acc_ref)

        acc_ref[...] += _dot(h_ref[...], dp_ref[...], TN)

        @pl.when(mi == n_m - 1)
        def _():
            o_ref[...] = acc_ref[...].astype(BF16)

    if sectioned:
        dp_spec = pl.BlockSpec((None, tm, tn), lambda j, mi: (j // per, mi, j % per))
    else:
        dp_spec = pl.BlockSpec((tm, tn), lambda j, mi: (mi, j))
    return _call(
        body, name=name, grid=(n // tn, n_m),
        out_shape=jax.ShapeDtypeStruct((nsh, d, ns), BF16),
        in_specs=[pl.BlockSpec((tm, d), lambda j, mi: (mi, 0)), dp_spec],
        out_specs=pl.BlockSpec((None, d, tn), lambda j, mi: (j // per, 0, j % per)),
        scratch_shapes=[pltpu.VMEM((d, tn), F32)],
        compiler_params=_params(),
    )(h, dproj)


def inproj_bwd(dproj, wg, x, dxo, mod, ng, seq, sectioned, name):
    m_rows, d = x.shape
    nsh, _, ns = wg.shape
    n = nsh * ns
    nb = m_rows // seq
    tm, tk = min(ROW_TILE, seq), ns
    per = ns // tk
    n_k = n // tk

    def body(dp_ref, w_ref, x_ref, dxo_ref, mod_ref, ng_ref, dxi_ref, dsh_ref, dsc_ref, dng_ref, acc_ref):
        i, k = pl.program_id(0), pl.program_id(1)
        @pl.when(k == 0)
        def _():
            acc_ref[...] = jnp.zeros_like(acc_ref)

        acc_ref[...] += _dot(dp_ref[...], w_ref[...], NT)

        @pl.when(k == n_k - 1)
        def _():
            dh = acc_ref[...]
            xv = x_ref[...]
            r = lax.rsqrt(jnp.mean(xv * xv, axis=-1, keepdims=True) + EPS)
            xn = xv * r
            md = mod_ref[0]
            gain = ng_ref[...]
            p_shift = jnp.sum(dh, axis=0, keepdims=True)
            p_scale = jnp.sum(dh * (xn * gain), axis=0, keepdims=True)
            drn = dh * (1.0 + md[:, d:2 * d])
            p_ng = jnp.sum(drn * xn, axis=0, keepdims=True)
            dxn = drn * gain
            dx = r * (dxn - xn * jnp.mean(dxn * xn, axis=-1, keepdims=True))
            dxi_ref[...] = dxo_ref[...] + dx

            @pl.when((i * tm) % seq == 0)
            def _():
                dsh_ref[0] = p_shift
                dsc_ref[0] = p_scale

            @pl.when((i * tm) % seq != 0)
            def _():
                dsh_ref[0] = dsh_ref[0] + p_shift
                dsc_ref[0] = dsc_ref[0] + p_scale

            @pl.when(i == 0)
            def _():
                dng_ref[...] = p_ng

            @pl.when(i != 0)
            def _():
                dng_ref[...] = dng_ref[...] + p_ng

    if sectioned:
        dp_spec = pl.BlockSpec((None, tm, tk), lambda i, k: (k // per, i, k % per))
    else:
        dp_spec = pl.BlockSpec((tm, tk), lambda i, k: (i, k))
    row = pl.BlockSpec((tm, d), lambda i, k: (i, 0))
    per_seq = pl.BlockSpec((1, 1, d), lambda i, k: ((i * tm) // seq, 0, 0))
    return _call(
        body, name=name, grid=(m_rows // tm, n_k),
        out_shape=[jax.ShapeDtypeStruct((m_rows, d), F32), jax.ShapeDtypeStruct((nb, 1, d), F32),
                   jax.ShapeDtypeStruct((nb, 1, d), F32), jax.ShapeDtypeStruct((1, d), F32)],
        in_specs=[dp_spec,
                  pl.BlockSpec((None, d, tk), lambda i, k: (k // per, 0, k % per)),
                  row, row,
                  pl.BlockSpec((1, 1, 3 * d), lambda i, k: ((i * tm) // seq, 0, 0)),
                  pl.BlockSpec((1, d), lambda i, k: (0, 0))],
        out_specs=[row, per_seq, per_seq, pl.BlockSpec((1, d), lambda i, k: (0, 0))],
        scratch_shapes=[pltpu.VMEM((tm, d), F32)],
        compiler_params=_params(),
    )(dproj, wg, x, dxo, mod, ng)


def _sgu_stats(proj_ref, vg_ref, di, gd, dgel_ref=None):
    s1 = jnp.zeros((SG_BLOCK, 1), F32)
    for g in range(SG_GROUPS):
        v_pre = proj_ref[:, di + g * gd:di + (g + 1) * gd]
        if dgel_ref is None:
            vg = _gelu(v_pre)
        else:
            vg, dgel_ref[:, g * gd:(g + 1) * gd] = _gelu_and_grad(v_pre)
        vg_ref[:, g * gd:(g + 1) * gd] = vg
        s1 = s1 + jnp.sum(vg, axis=1, keepdims=True)
    mu = s1 / di
    s2 = jnp.zeros((SG_BLOCK, 1), F32)
    for g in range(SG_GROUPS):
        dv = vg_ref[:, g * gd:(g + 1) * gd] - mu
        s2 = s2 + jnp.sum(dv * dv, axis=1, keepdims=True)
    return mu, lax.rsqrt(s2 / di + EPS)


def sgu_fwd(proj, ln_gain, ln_bias, ws, bs, name):
    m_rows, n3 = proj.shape
    di = n3 // 3
    gd = di // SG_GROUPS
    n_blocks = m_rows // SG_BLOCK
    per_step = SG_STEP_BLOCKS if n_blocks % SG_STEP_BLOCKS == 0 else 1

    def body(proj_ref, lg_ref, lb_ref, ws_ref, bs_ref, y_ref, wsm_ref, vg_ref):
        @pl.when(pl.program_id(0) == 0)
        def _():
            mask = _chunk_mask()
            for g in range(SG_GROUPS):
                wsm_ref[g] = jnp.where(mask, ws_ref[g], 0.0).astype(BF16)

        for blk in range(per_step):
            p_ref, o_ref = proj_ref.at[blk], y_ref.at[blk]
            mu, rstd = _sgu_stats(p_ref, vg_ref, di, gd)
            for g in range(SG_GROUPS):
                cs = slice(g * gd, (g + 1) * gd)
                vln = (vg_ref[:, cs] - mu) * rstd * lg_ref[:, cs] + lb_ref[:, cs]
                s = _dot(wsm_ref[g], vln.astype(BF16), NN) + bs_ref[g]
                u = _gelu(p_ref[:, cs])
                gp = p_ref[:, 2 * di + g * gd:2 * di + (g + 1) * gd]
                o_ref[:, cs] = (u * s * (gp * _sigmoid(gp))).astype(BF16)

    full = lambda shape: pl.BlockSpec(shape, lambda i: (0,) * len(shape))
    return _call(
        body, name=name, grid=(n_blocks // per_step,),
        out_shape=jax.ShapeDtypeStruct((n_blocks, SG_BLOCK, di), BF16),
        in_specs=[pl.BlockSpec((per_step, SG_BLOCK, n3), lambda i: (i, 0, 0)),
                  full((1, di)), full((1, di)),
                  full((SG_GROUPS, SG_BLOCK, SG_BLOCK)), full((SG_GROUPS, SG_BLOCK, 1))],
        out_specs=pl.BlockSpec((per_step, SG_BLOCK, di), lambda i: (i, 0, 0)),
        scratch_shapes=[pltpu.VMEM((SG_GROUPS, SG_BLOCK, SG_BLOCK), BF16), pltpu.VMEM((SG_BLOCK, di), F32)],
        compiler_params=_params(),
    )(proj.reshape(n_blocks, SG_BLOCK, n3), ln_gain, ln_bias, ws, bs).reshape(m_rows, di)


def sgu_bwd(proj, dy, ln_gain, ln_bias, ws, bs, name):
    m_rows, n3 = proj.shape
    di = n3 // 3
    gd = di // SG_GROUPS
    n_i = m_rows // SG_BLOCK

    def body(proj_ref, dy_ref, lg_ref, lb_ref, ws_ref, bs_ref,
             dp_ref, dws_ref, dbs_ref, dlg_ref, dlb_ref, wsm_ref, vg_ref, dvh_ref, dgel_ref):
        i = pl.program_id(0)

        def before():
            @pl.when(i == 0)
            def _():
                mask = _chunk_mask()
                for g in range(SG_GROUPS):
                    wsm_ref[g] = jnp.where(mask, ws_ref[g], 0.0).astype(BF16)
                dws_ref[...] = jnp.zeros_like(dws_ref)
                dbs_ref[...] = jnp.zeros_like(dbs_ref)
                dlg_ref[...] = jnp.zeros_like(dlg_ref)
                dlb_ref[...] = jnp.zeros_like(dlb_ref)

        def after():
            @pl.when(i == n_i - 1)
            def _():
                mask = _chunk_mask()
                for g in range(SG_GROUPS):
                    dws_ref[g] = jnp.where(mask, dws_ref[g], 0.0)

        before()
        mu, rstd = _sgu_stats(proj_ref, vg_ref, di, gd, dgel_ref)
        m1 = jnp.zeros((SG_BLOCK, 1), F32)
        m2 = jnp.zeros((SG_BLOCK, 1), F32)
        for g in range(SG_GROUPS):
            cs = slice(g * gd, (g + 1) * gd)
            gs = slice(2 * di + g * gd, 2 * di + (g + 1) * gd)
            gain = lg_ref[:, cs]
            vhat = (vg_ref[:, cs] - mu) * rstd
            vln_b = (vhat * gain + lb_ref[:, cs]).astype(BF16)
            s = _dot(wsm_ref[g], vln_b, NN) + bs_ref[g]
            u, du = _gelu_and_grad(proj_ref[:, cs])
            sg, dsg = _silu_and_grad(proj_ref[:, gs])
            dyv = dy_ref[:, cs].astype(F32)
            dp_ref[:, cs] = (dyv * s * sg * du).astype(BF16)
            dp_ref[:, gs] = (dyv * u * s * dsg).astype(BF16)
            ds = dyv * u * sg
            ds_b = ds.astype(BF16)
            dws_ref[g] = dws_ref[g] + _dot(ds_b, vln_b, NT)
            dbs_ref[g] = dbs_ref[g] + jnp.sum(ds, axis=1, keepdims=True)
            dvln = _dot(wsm_ref[g], ds_b, TN)
            dlg_ref[:, cs] = dlg_ref[:, cs] + jnp.sum(dvln * vhat, axis=0, keepdims=True)
            dlb_ref[:, cs] = dlb_ref[:, cs] + jnp.sum(dvln, axis=0, keepdims=True)
            dvh = dvln * gain
            dvh_ref[:, cs] = dvh
            m1 = m1 + jnp.sum(dvh, axis=1, keepdims=True)
            m2 = m2 + jnp.sum(dvh * vhat, axis=1, keepdims=True)
        m1 = m1 / di
        m2 = m2 / di
        for g in range(SG_GROUPS):
            cs = slice(g * gd, (g + 1) * gd)
            vs = slice(di + g * gd, di + (g + 1) * gd)
            vhat = (vg_ref[:, cs] - mu) * rstd
            dvg = rstd * (dvh_ref[:, cs] - m1 - vhat * m2)
            dp_ref[:, vs] = (dvg * dgel_ref[:, cs]).astype(BF16)

        after()

    full = lambda shape: pl.BlockSpec(shape, lambda i: (0,) * len(shape))
    return _call(
        body, name=name, grid=(n_i,),
        out_shape=[jax.ShapeDtypeStruct((m_rows, n3), BF16),
                   jax.ShapeDtypeStruct((SG_GROUPS, SG_BLOCK, SG_BLOCK), F32),
                   jax.ShapeDtypeStruct((SG_GROUPS, SG_BLOCK, 1), F32),
                   jax.ShapeDtypeStruct((1, di), F32), jax.ShapeDtypeStruct((1, di), F32)],
        in_specs=[pl.BlockSpec((SG_BLOCK, n3), lambda i: (i, 0)),
                  pl.BlockSpec((SG_BLOCK, di), lambda i: (i, 0)),
                  full((1, di)), full((1, di)),
                  full((SG_GROUPS, SG_BLOCK, SG_BLOCK)), full((SG_GROUPS, SG_BLOCK, 1))],
        out_specs=[pl.BlockSpec((SG_BLOCK, n3), lambda i: (i, 0)),
                   full((SG_GROUPS, SG_BLOCK, SG_BLOCK)), full((SG_GROUPS, SG_BLOCK, 1)),
                   full((1, di)), full((1, di))],
        scratch_shapes=[pltpu.VMEM((SG_GROUPS, SG_BLOCK, SG_BLOCK), BF16),
                        pltpu.VMEM((SG_BLOCK, di), F32), pltpu.VMEM((SG_BLOCK, di), F32),
                        pltpu.VMEM((SG_BLOCK, di), F32)],
        compiler_params=_params(),
    )(proj, dy, ln_gain, ln_bias, ws, bs)


def _lower_bound(lbraw):
    mx = jnp.maximum(lbraw[0:1, :], lbraw[1:2, :])
    e0 = jnp.exp(lbraw[0:1, :] - mx)
    e1 = jnp.exp(lbraw[1:2, :] - mx)
    p0 = e0 / (e0 + e1)
    p1 = e1 / (e0 + e1)
    return (p0 + p1) - p0, p0, p1


def _tri(lower):
    r = lax.broadcasted_iota(jnp.int32, (CHUNK, CHUNK), 0)
    c = lax.broadcasted_iota(jnp.int32, (CHUNK, CHUNK), 1)
    return ((r >= c) if lower else (c >= r)).astype(BF16)


def _running_sum(tri, x):
    x1 = x.astype(BF16)
    r1 = x - x1.astype(F32)
    x2 = r1.astype(BF16)
    x3 = (r1 - x2.astype(F32)).astype(BF16)
    return _dot(tri, x1, NN) + _dot(tri, x2, NN) + _dot(tri, x3, NN)


def _row(a, idx):
    r = lax.broadcasted_iota(jnp.int32, a.shape, 0)
    return jnp.sum(jnp.where(r == idx, a, 0.0), axis=0, keepdims=True)


def _hgrn_gates(qp, fp, lb, tri):
    sgm = _sigmoid_small(fp)
    f = lb + (1.0 - lb) * sgm
    k = 1.0 - f
    a = _running_sum(tri, jnp.log(f))
    a_mid = _row(a, CHUNK // 2 - 1)
    a_last = _row(a, CHUNK - 1)
    q, dq = _silu_and_grad(qp)
    e1, e2, e3, e4 = jnp.exp(a - a_mid), jnp.exp(a_mid - a), jnp.exp(a), jnp.exp(a_last - a)
    return dict(sgm=sgm, f=f, k=k, q=q, dq=dq, e1=e1, e2=e2, e3=e3, e4=e4, dec=jnp.exp(a_last),
                q_in=q * e1, k_in=k * e2, q_out=q * e3, k_out=k * e4)


def _causal():
    r = lax.broadcasted_iota(jnp.int32, (CHUNK, CHUNK), 0)
    c = lax.broadcasted_iota(jnp.int32, (CHUNK, CHUNK), 1)
    return r >= c


def hgrn_fwd(proj4, lbraw, gn, seq, name):
    _, m_rows, di = proj4.shape
    nb, nh, nc = m_rows // seq, di // HEAD_DIM, seq // CHUNK
    rows = min(HG_ROWS, seq)
    wide = HG_WIDE * HEAD_DIM
    ns, cpb = seq // rows, rows // CHUNK

    def body(p_ref, lb_ref, gn_ref, y_ref, sts_ref, st_ref):
        @pl.when(pl.program_id(2) == 0)
        def _():
            st_ref[...] = jnp.zeros_like(st_ref)

        tri = _tri(True)
        causal = _causal()
        gain = gn_ref[...]
        lbs = [_lower_bound(lb_ref[:, j * HEAD_DIM:(j + 1) * HEAD_DIM])[0] for j in range(HG_WIDE)]

        units = [(n, j) for n in range(cpb) for j in range(HG_WIDE)]
        rs = lambda n: slice(n * CHUNK, (n + 1) * CHUNK)
        cs = lambda j: slice(j * HEAD_DIM, (j + 1) * HEAD_DIM)
        gates, v_b, sc_b, kv, o_in, o_x = {}, {}, {}, {}, {}, {}
        for n, j in units:
            gates[n, j] = _hgrn_gates(p_ref[0, rs(n), cs(j)], p_ref[1, rs(n), cs(j)], lbs[j], tri)
            v_b[n, j] = p_ref[2, rs(n), cs(j)].astype(BF16)
        for u in units:
            t = gates[u]
            sc_b[u] = jnp.where(causal, _dot(t["q_in"].astype(BF16), t["k_in"].astype(BF16), NT), 0.0).astype(BF16)
            kv[u] = _dot(v_b[u], t["k_out"].astype(BF16), TN)
        for u in units:
            o_in[u] = _dot(sc_b[u], v_b[u], NN)
        for j in range(HG_WIDE):
            st = st_ref[j]
            for n in range(cpb):
                sts_ref[n, :, cs(j)] = st
                o_x[n, j] = _dot(gates[n, j]["q_out"].astype(BF16), st.astype(BF16), NT)
                st = st * gates[n, j]["dec"] + kv[n, j]
            st_ref[j] = st
        for n, j in units:
            o = o_in[n, j] + o_x[n, j]
            r = lax.rsqrt(jnp.mean(o * o, axis=-1, keepdims=True) + EPS)
            gp = p_ref[3, rs(n), cs(j)]
            y_ref[rs(n), cs(j)] = ((o * r * gain) * (gp * _sigmoid(gp))).astype(BF16)

    return _call(
        body, name=name, grid=(nh // HG_WIDE, nb, ns),
        out_shape=[jax.ShapeDtypeStruct((m_rows, di), BF16),
                   jax.ShapeDtypeStruct((nb * nc, HEAD_DIM, di), F32)],
        in_specs=[pl.BlockSpec((4, rows, wide), lambda hg, b, s: (0, b * ns + s, hg)),
                  pl.BlockSpec((2, wide), lambda hg, b, s: (0, hg)),
                  pl.BlockSpec((1, HEAD_DIM), lambda hg, b, s: (0, 0))],
        out_specs=[pl.BlockSpec((rows, wide), lambda hg, b, s: (b * ns + s, hg)),
                   pl.BlockSpec((cpb, HEAD_DIM, wide), lambda hg, b, s: (b * ns + s, 0, hg))],
        scratch_shapes=[pltpu.VMEM((HG_WIDE, HEAD_DIM, HEAD_DIM), F32)],
        compiler_params=_params(),
    )(proj4, lbraw, gn)


def hgrn_bwd(proj4, dy, sts, lbraw, gn, seq, name):
    _, m_rows, di = proj4.shape
    nb, nh, nc = m_rows // seq, di // HEAD_DIM, seq // CHUNK
    rows = min(HG_ROWS, seq)
    wide = HG_WIDE * HEAD_DIM
    ns, cpb = seq // rows, rows // CHUNK
    n_hg = nh // HG_WIDE

    def body(p_ref, dy_ref, sts_ref, lb_ref, gn_ref, dp_ref, dlb_ref, dgn_ref, dst_ref, lbacc_ref, gnacc_ref):
        hg, b, s = pl.program_id(0), pl.program_id(1), pl.program_id(2)
        tri, triu = _tri(True), _tri(False)
        causal = _causal()
        gain = gn_ref[...]
        first = (b == 0) & (s == 0)
        cs = lambda j: slice(j * HEAD_DIM, (j + 1) * HEAD_DIM)

        def before():
            @pl.when((hg == 0) & first)
            def _():
                gnacc_ref[...] = jnp.zeros_like(gnacc_ref)

            @pl.when(first)
            def _():
                lbacc_ref[...] = jnp.zeros_like(lbacc_ref)

            @pl.when(s == 0)
            def _():
                dst_ref[...] = jnp.zeros_like(dst_ref)

        def after():
            @pl.when((b == nb - 1) & (s == ns - 1))
            def _():
                for j in range(HG_WIDE):
                    _, p0, p1 = _lower_bound(lb_ref[:, cs(j)])
                    acc = lbacc_ref[:, cs(j)]
                    dlb_ref[0:1, cs(j)] = -acc * p0 * p1
                    dlb_ref[1:2, cs(j)] = acc * p1 * (1.0 - p1)

            @pl.when((hg == n_hg - 1) & (b == nb - 1) & (s == ns - 1))
            def _():
                tot = gnacc_ref[:, 0:HEAD_DIM]
                for j in range(1, HG_WIDE):
                    tot = tot + gnacc_ref[:, cs(j)]
                dgn_ref[...] = tot

        before()

        units = [(n, j) for n in range(cpb) for j in range(HG_WIDE)]
        rs = lambda n: slice(n * CHUNK, (n + 1) * CHUNK)
        lbs = [_lower_bound(lb_ref[:, cs(j)])[0] for j in range(HG_WIDE)]
        gates, v_b, st_b, sc_b, o, do_b = {}, {}, {}, {}, {}, {}
        dq_out, dsc_b, dv, g_st, dq_in, dk_in, dst_at, dk_out, ddec = {}, {}, {}, {}, {}, {}, {}, {}, {}
        for n, j in units:
            gates[n, j] = _hgrn_gates(p_ref[0, rs(n), cs(j)], p_ref[1, rs(n), cs(j)], lbs[j], tri)
            v_b[n, j] = p_ref[2, rs(n), cs(j)].astype(BF16)
            st_b[n, j] = sts_ref[n, :, cs(j)].astype(BF16)
        for u in units:
            t = gates[u]
            sc_b[u] = jnp.where(causal, _dot(t["q_in"].astype(BF16), t["k_in"].astype(BF16), NT), 0.0).astype(BF16)
        for u in units:
            o[u] = _dot(sc_b[u], v_b[u], NN) + _dot(gates[u]["q_out"].astype(BF16), st_b[u], NT)
        for n, j in units:
            ov = o[n, j]
            r = lax.rsqrt(jnp.mean(ov * ov, axis=-1, keepdims=True) + EPS)
            ohat = ov * r
            sg, dsg = _silu_and_grad(p_ref[3, rs(n), cs(j)])
            dyv = dy_ref[rs(n), cs(j)].astype(F32)
            dp_ref[3, rs(n), cs(j)] = (dyv * (ohat * gain) * dsg).astype(BF16)
            d_on = dyv * sg
            gnacc_ref[:, cs(j)] = gnacc_ref[:, cs(j)] + jnp.sum(d_on * ohat, axis=0, keepdims=True)
            dohat = d_on * gain
            do_b[n, j] = (r * (dohat - ohat * jnp.mean(dohat * ohat, axis=-1, keepdims=True))).astype(BF16)
        for u in units:
            dq_out[u] = _dot(do_b[u], st_b[u], NN)
            dsc_b[u] = jnp.where(causal, _dot(do_b[u], v_b[u], NT), 0.0).astype(BF16)
            dv[u] = _dot(sc_b[u], do_b[u], TN)
            g_st[u] = _dot(do_b[u], gates[u]["q_out"].astype(BF16), TN)
        for u in units:
            dq_in[u] = _dot(dsc_b[u], gates[u]["k_in"].astype(BF16), NN)
            dk_in[u] = _dot(dsc_b[u], gates[u]["q_in"].astype(BF16), TN)
        for j in range(HG_WIDE):
            dst = dst_ref[j]
            for n in reversed(range(cpb)):
                dst_at[n, j] = dst
                dst = dst * gates[n, j]["dec"] + g_st[n, j]
            dst_ref[j] = dst
        for n, j in units:
            dst = dst_at[n, j]
            dst_b = dst.astype(BF16)
            dk_out[n, j] = _dot(v_b[n, j], dst_b, NN)
            dv[n, j] = dv[n, j] + _dot(gates[n, j]["k_out"].astype(BF16), dst_b, NT)
            ddec[n, j] = jnp.sum(dst * sts_ref[n, :, cs(j)], axis=0, keepdims=True)
        for n, j in units:
            t = gates[n, j]
            dp_ref[2, rs(n), cs(j)] = dv[n, j].astype(BF16)
            dq = dq_in[n, j] * t["e1"] + dq_out[n, j] * t["e3"]
            dk = dk_in[n, j] * t["e2"] + dk_out[n, j] * t["e4"]
            w_in = dq_in[n, j] * t["q_in"] - dk_in[n, j] * t["k_in"]
            w_out = dk_out[n, j] * t["k_out"]
            da = w_in + dq_out[n, j] * t["q_out"] - w_out
            da_mid = -jnp.sum(w_in, axis=0, keepdims=True)
            da_last = jnp.sum(w_out, axis=0, keepdims=True) + ddec[n, j] * t["dec"]
            rid = lax.broadcasted_iota(jnp.int32, da.shape, 0)
            da = da + jnp.where(rid == CHUNK // 2 - 1, da_mid, 0.0) + jnp.where(rid == CHUNK - 1, da_last, 0.0)
            dlf = _running_sum(triu, da)
            df = dlf / t["f"] - dk
            sgm = t["sgm"]
            dp_ref[1, rs(n), cs(j)] = (df * (1.0 - lbs[j]) * sgm * (1.0 - sgm)).astype(BF16)
            lbacc_ref[:, cs(j)] = lbacc_ref[:, cs(j)] + jnp.sum(df * (1.0 - sgm), axis=0, keepdims=True)
            dp_ref[0, rs(n), cs(j)] = (dq * t["dq"]).astype(BF16)

        after()

    blk = lambda hg, b, s: b * ns + (ns - 1 - s)
    return _call(
        body, name=name, grid=(n_hg, nb, ns),
        out_shape=[jax.ShapeDtypeStruct((4, m_rows, di), BF16), jax.ShapeDtypeStruct((2, di), F32),
                   jax.ShapeDtypeStruct((1, HEAD_DIM), F32)],
        in_specs=[pl.BlockSpec((4, rows, wide), lambda hg, b, s: (0, blk(hg, b, s), hg)),
                  pl.BlockSpec((rows, wide), lambda hg, b, s: (blk(hg, b, s), hg)),
                  pl.BlockSpec((cpb, HEAD_DIM, wide), lambda hg, b, s: (blk(hg, b, s), 0, hg)),
                  pl.BlockSpec((2, wide), lambda hg, b, s: (0, hg)),
                  pl.BlockSpec((1, HEAD_DIM), lambda hg, b, s: (0, 0))],
        out_specs=[pl.BlockSpec((4, rows, wide), lambda hg, b, s: (0, blk(hg, b, s), hg)),
                   pl.BlockSpec((2, wide), lambda hg, b, s: (0, hg)),
                   pl.BlockSpec((1, HEAD_DIM), lambda hg, b, s: (0, 0))],
        scratch_shapes=[pltpu.VMEM((HG_WIDE, HEAD_DIM, HEAD_DIM), F32), pltpu.VMEM((1, wide), F32),
                        pltpu.VMEM((1, wide), F32)],
        compiler_params=_params(),
    )(proj4, dy, sts, lbraw, gn)


def outproj_loss(y, w, x, mod, fg, target, seq, name):
    m_rows, di = y.shape
    d = w.shape[1]
    tm = min(512, seq)

    def body(y_ref, w_ref, x_ref, mod_ref, fg_ref, t_ref, out_ref, loss_ref, dx_ref, dfg_ref):
        i = pl.program_id(0)
        acc = _dot(y_ref[...], w_ref[...], NN)
        out_ref[...] = acc.astype(BF16)
        xv = x_ref[...] + mod_ref[0][:, 2 * d:] * acc
        gain = fg_ref[...]
        r = lax.rsqrt(jnp.mean(xv * xv, axis=-1, keepdims=True) + EPS)
        xn = xv * r
        e = xn * gain - t_ref[...]
        part = 0.5 * jnp.sum(jnp.mean(e * e, axis=-1, keepdims=True), axis=0, keepdims=True)
        dyv = e / d
        p_fg = jnp.sum(dyv * xn, axis=0, keepdims=True)
        dxn = dyv * gain
        dx_ref[...] = r * (dxn - xn * jnp.mean(dxn * xn, axis=-1, keepdims=True))

        @pl.when(i == 0)
        def _():
            loss_ref[...] = part
            dfg_ref[...] = p_fg

        @pl.when(i != 0)
        def _():
            loss_ref[...] = loss_ref[...] + part
            dfg_ref[...] = dfg_ref[...] + p_fg

    row = pl.BlockSpec((tm, d), lambda i: (i, 0))
    return _call(
        body, name=name, grid=(m_rows // tm,),
        out_shape=[jax.ShapeDtypeStruct((m_rows, d), BF16), jax.ShapeDtypeStruct((1, 1), F32),
                   jax.ShapeDtypeStruct((m_rows, d), F32), jax.ShapeDtypeStruct((1, d), F32)],
        in_specs=[pl.BlockSpec((tm, di), lambda i: (i, 0)),
                  pl.BlockSpec((di, d), lambda i: (0, 0)),
                  row,
                  pl.BlockSpec((1, 1, 3 * d), lambda i: ((i * tm) // seq, 0, 0)),
                  pl.BlockSpec((1, d), lambda i: (0, 0)), row],
        out_specs=[row, pl.BlockSpec((1, 1), lambda i: (0, 0)), row, pl.BlockSpec((1, d), lambda i: (0, 0))],
        compiler_params=_params(),
    )(y, w, x, mod, fg, target)


def _pack(parts):
    flat = jnp.concatenate([p.reshape(-1) for p in parts])
    pad = (-flat.shape[0]) % (8 * LANES)
    return jnp.pad(flat, (0, pad)).reshape(-1, LANES)


def kernel(x, c, norm_gain, w_ada, b_ada, a_w_in, a_ln_gain, a_ln_bias, a_w_s, a_b_s, a_w_out, b_w_in, b_lower_bounds, b_gn_gain, b_w_out, final_gain, loss_target, m_norm_gain, m_w_ada, m_b_ada, m_a_w_in, m_a_ln_gain, m_a_ln_bias, m_a_w_s, m_a_b_s, m_a_w_out, m_b_w_in, m_b_lower_bounds, m_b_gn_gain, m_b_w_out, m_final_gain, v_norm_gain, v_w_ada, v_b_ada, v_a_w_in, v_a_ln_gain, v_a_ln_bias, v_a_w_s, v_a_b_s, v_a_w_out, v_b_w_in, v_b_lower_bounds, v_b_gn_gain, v_b_w_out, v_final_gain):
    nb, seq, d = x.shape
    m_rows = nb * seq
    n_l = w_ada.shape[0]
    ada_cols = w_ada.shape[2]
    px, py, pc = _place()
    chip = 2 * px + py
    dev = 2 * chip + pc

    c_all = allgather_small(c.reshape(-1, LANES), "gather_c").reshape(N_DEV * nb, d)
    s_a = halves_start([cast_into_slot(a_w_in[0], chip, c_all, "cast_a_in")], "gather_a_in_start")
    land_b_in = cast_into_slot(b_w_in[0], chip, s_a[3], "cast_b_in")
    land_b_out = cast_into_slot(b_w_out[0], chip, land_b_in, "cast_b_out")
    b_cols = lax.dynamic_slice_in_dim(b_ada, chip * ada_cols, ada_cols, axis=1).reshape(n_l, 1, ada_cols)
    mod_cols = ada_fwd(c_all, w_ada, b_cols, land_b_out, "ada_fwd")
    mod_rows = mod_cols.size // LANES
    s_mod = gather_all_start(
        lax.dynamic_update_slice(jnp.zeros((N_DEV, mod_rows, LANES), F32),
                                 mod_cols.reshape(1, mod_rows, LANES), (dev, 0, 0)), "gather_mod_start")
    s_ao = halves_start([cast_into_slot(a_w_out[0], chip, s_mod[3], "cast_a_out")], "gather_a_out_start")
    s_bi = gather_start(land_b_in, s_ao[3], "gather_b_in_start")
    s_bo = gather_start(land_b_out, s_bi[3], "gather_b_out_start")
    mod_g = gather_all_wait(s_mod[0], s_mod[1], s_mod[2], s_bo[3], "gather_mod_wait")
    mod_g = mod_g.reshape(N_CHIPS, 2, n_l, N_DEV * nb, ada_cols)[:, 0]
    mod_all = jnp.transpose(mod_g, (1, 2, 0, 3)).reshape(n_l, N_DEV * nb, 3 * d)
    mod_mine = lax.dynamic_slice_in_dim(mod_all, dev * nb, nb, axis=1)
    mod0 = mod_mine[0].reshape(nb, 1, 3 * d)
    mod1 = mod_mine[1].reshape(nb, 1, 3 * d)

    landed_a = halves_wait(s_a[0], s_a[1], s_a[2], mod_mine, "gather_a_in_wait")
    (wa_in,) = pass_halves(list(landed_a), "gather_a_in_pass")
    di = a_w_out.shape[1] * N_CHIPS

    x0 = x.reshape(m_rows, d)
    tgt = loss_target.reshape(m_rows, d)
    ng0 = norm_gain[0:1] + (s_bi[3][0, 0] + s_bo[3][0, 0])
    ng1 = norm_gain[1:2]
    bs_col = a_b_s[0].reshape(SG_GROUPS, SG_BLOCK, 1)
    proj_a, h_a = inproj_fwd(x0, mod0, ng0, wa_in, seq, False, "a_inproj")
    y_a = sgu_fwd(proj_a, a_ln_gain, a_ln_bias, a_w_s[0], bs_col, "a_sgu")
    (wa_out,) = pass_halves(list(halves_wait(s_ao[0], s_ao[1], s_ao[2], y_a, "gather_a_out_wait")), "gather_a_out_pass")
    wa_out = wa_out.reshape(di, d)
    x1, out_a = outproj_fwd(y_a, wa_out, x0, mod0, seq, "a_outproj")
    wb_in = gather_wait(*s_bi[:3], out_a, "gather_b_in_wait")
    proj_b, h_b = inproj_fwd(x1, mod1, ng1, wb_in, seq, True, "b_inproj")
    y_b, sts_b = hgrn_fwd(proj_b, b_lower_bounds, b_gn_gain, seq, "b_hgrn")
    wb_out = gather_wait(*s_bo[:3], y_b, "gather_b_out_wait").reshape(di, d)
    out_b, loss_part, dx2, dfg = outproj_loss(
        y_b, wb_out, x1, mod1, final_gain.reshape(1, d), tgt, seq, "b_outproj_loss")

    shard_rows = di // N_CHIPS
    dy_b, dout_b, dgate1 = outproj_bwd(dx2, out_b, mod1, wb_out, seq, "b_outproj_bwd")
    gwb_out = grad_w_out(y_b, dout_b, "b_grad_w_out").reshape(N_CHIPS, shard_rows, d)
    e_bo = exchange_start(gwb_out, "exchange_b_out_start")
    dproj_b, dlb, dgn = hgrn_bwd(
        proj_b, dy_b, sts_b, b_lower_bounds, b_gn_gain + e_bo[4][0, 0], seq, "b_hgrn_bwd")
    e_bi = exchange_start(grad_w_in(h_b, dproj_b, N_CHIPS, True, "b_grad_w_in"), "exchange_b_in_start")
    dx1, dshift1, dscale1, dng1 = inproj_bwd(
        dproj_b, wb_in, x1, dx2, mod1, ng1 + e_bi[4][0, 0], seq, True, "b_inproj_bwd")

    dy_a, dout_a, dgate0 = outproj_bwd(dx1, out_a, mod0, wa_out, seq, "a_outproj_bwd")
    gwa_out = grad_w_out(y_a, dout_a, "a_grad_w_out").reshape(N_CHIPS, shard_rows, d)
    e_ao = exchange_start(gwa_out, "exchange_a_out_start")
    dproj_a, dws, dbs, dlg, dlbias = sgu_bwd(
        proj_a, dy_a, a_ln_gain + e_ao[4][0, 0], a_ln_bias, a_w_s[0], bs_col, "a_sgu_bwd")
    e_ai = exchange_start(grad_w_in(h_a, dproj_a, N_CHIPS, False, "a_grad_w_in"), "exchange_a_in_start")
    dx0, dshift0, dscale0, dng0 = inproj_bwd(
        dproj_a, wa_in, x0, dx1, mod0, norm_gain[0:1] + e_ai[4][0, 0], seq, False, "a_inproj_bwd")
    grad_x = dx0.reshape(nb, seq, d)

    dmod = jnp.concatenate([dshift0, dscale0, dgate0, dshift1, dscale1, dgate1], axis=2)
    n_dmod = dmod.size
    small_g = [jnp.concatenate([dng0, dng1], axis=0), dlg, dlbias, dws, dbs, dlb, dfg, dgn]
    packed_g = _pack([dmod] + small_g + [loss_part])
    rows = packed_g.shape[0]
    s_small = gather_all_start(
        lax.dynamic_update_slice(jnp.zeros((N_DEV, rows, LANES), F32), packed_g[None], (dev, 0, 0)),
        "gather_small_start")

    def finish(group, after):
        mine = []
        for ex, _, _, _, nm in group:
            parts_thru, land = exchange_wait(ex[0], ex[1], ex[2], ex[3], after, "exchange_" + nm + "_wait")
            mine.append(sum_parts(parts_thru, land, chip, "sum_" + nm))
            after = mine[-1]
        theirs = swap_sibling(mine, "swap_" + group[0][4])
        return [[r.reshape(w.shape) for r in adamw_pair(pa, pb, w[0], m[0], v[0], "adamw_" + nm)]
                for pa, pb, (_, w, m, v, nm) in zip(mine, theirs, group)]

    (gb_out, db_out, mb_out, vb_out), (gb_in, db_in, mb_in, vb_in), (ga_out, da_out, ma_out, va_out) = finish(
        [(e_bo, b_w_out, m_b_w_out, v_b_w_out, "b_out"), (e_bi, b_w_in, m_b_w_in, v_b_w_in, "b_in"),
         (e_ao, a_w_out, m_a_w_out, v_a_w_out, "a_out")], s_small[3])
    ((ga_in, da_in, ma_in, va_in),) = finish([(e_ai, a_w_in, m_a_w_in, v_a_w_in, "a_in")], ga_out)

    small_w = [norm_gain, a_ln_gain, a_ln_bias, a_w_s, a_b_s, b_lower_bounds, final_gain, b_gn_gain]
    small_m = [m_norm_gain, m_a_ln_gain, m_a_ln_bias, m_a_w_s, m_a_b_s, m_b_lower_bounds, m_final_gain, m_b_gn_gain]
    small_v = [v_norm_gain, v_a_ln_gain, v_a_ln_bias, v_a_w_s, v_a_b_s, v_b_lower_bounds, v_final_gain, v_b_gn_gain]
    rows_of = lambda a: a.reshape(-1, a.shape[-1])
    gathered = gather_all_wait(s_small[0], s_small[1], s_small[2], ga_in, "gather_small_wait")
    tail, small_res = small_update(
        gathered, n_dmod // LANES, [rows_of(a) for a in small_w], [rows_of(a) for a in small_m],
        [rows_of(a) for a in small_v], "small_update")
    loss = tail[0, 0]
    sg, sd, sm, sv = [[small_res[p][kind].reshape(w.shape) for p, w in enumerate(small_w)] for kind in range(4)]

    dmod_all = gathered[:, :n_dmod // LANES].reshape(N_DEV * nb, n_l, 3 * d)
    dmod_cols = lax.dynamic_slice_in_dim(dmod_all, chip * ada_cols, ada_cols, axis=2)
    dmod_cols = jnp.transpose(dmod_cols, (1, 0, 2))
    g_wada, d_wada, m_wada, v_wada = ada_bwd(c_all, dmod_cols, w_ada, m_w_ada, v_w_ada, "ada_bwd")
    flat = lambda a: a.reshape(1, -1)
    g_bada, d_bada, m_bada, v_bada = [
        r.reshape(b_ada.shape) for r in
        bias_update(dmod_all.reshape(N_DEV * nb, n_l * 3 * d), flat(b_ada), flat(m_b_ada), flat(v_b_ada), "bias_update")]

    def order(ng, wada, bada, ain, sm_rest, aout, bin_, bout):
        lg, lbi, ws_, bs_, lbd, fg_, gn_ = sm_rest
        return [ng, wada, bada, ain, lg, lbi, ws_, bs_, aout, bin_, lbd, gn_, bout, fg_]

    grads = order(sg[0], g_wada, g_bada, ga_in, sg[1:8], ga_out, gb_in, gb_out)
    deltas = order(sd[0], d_wada, d_bada, da_in, sd[1:8], da_out, db_in, db_out)
    new_m = order(sm[0], m_wada, m_bada, ma_in, sm[1:8], ma_out, mb_in, mb_out)
    new_v = order(sv[0], v_wada, v_bada, va_in, sv[1:8], va_out, vb_in, vb_out)
    return (loss, grad_x, *grads, *deltas, *new_m, *new_v)
```

```python
import jax
import jax.numpy as jnp
from jax import lax
from jax.experimental import pallas as pl
from jax.experimental.pallas import tpu as pltpu

F32 = jnp.float32
BF16 = jnp.bfloat16
EPS = 1e-6
CHUNK = 64
SG_BLOCK = 128
SG_GROUPS = 8
SG_STEP_BLOCKS = 4
HEAD_DIM = 128
HG_WIDE = 8
HG_ROWS = 256
N_CHIPS = 4
N_DEV = 8
LANES = 128
ADAM_LR = 0.001
ADAM_B1 = 0.9
ADAM_B2 = 0.999
ADAM_EPS = 1e-08
ADAM_WD = 0.01
ADAM_STEP = 10
GELU_C0 = 0.7978845608028654
GELU_C1 = 0.044715
MESH = pl.DeviceIdType.MESH
VMEM_LIMIT = 56 * 1024 * 1024


ROW_TILE = 1024


def _col_tile(n):
    return next(t for t in (1024, 768, 512, 256) if n % t == 0)


def _call(body, **kw):
    return pl.pallas_call(body, **kw)


def _params(**kw):
    return pltpu.CompilerParams(vmem_limit_bytes=VMEM_LIMIT, **kw)


def _sigmoid(x):
    return 0.5 * jnp.tanh(0.5 * x) + 0.5


def _sigmoid_small(x):
    return 1.0 / (1.0 + jnp.exp(-x))


def _silu_and_grad(x):
    s = _sigmoid(x)
    return x * s, s * (1.0 + x * (1.0 - s))


def _gelu(x):
    return 0.5 * x * (1.0 + jnp.tanh(GELU_C0 * (x + GELU_C1 * x * x * x)))


def _gelu_and_grad(x):
    t = jnp.tanh(GELU_C0 * (x + GELU_C1 * x * x * x))
    g = 0.5 * x * (1.0 + t)
    dg = 0.5 * (1.0 + t) + 0.5 * x * (1.0 - t * t) * (GELU_C0 * (1.0 + 3.0 * GELU_C1 * x * x))
    return g, dg


def _dot(a, b, dims, precision=None):
    return lax.dot_general(a, b, (dims, ((), ())), precision=precision, preferred_element_type=F32)


NN = ((1,), (0,))
NT = ((1,), (1,))
TN = ((0,), (0,))


def _adamw(w, g, m, v):
    m = ADAM_B1 * m + (1.0 - ADAM_B1) * g
    v = ADAM_B2 * v + (1.0 - ADAM_B2) * (g * g)
    m_hat = m / (1.0 - ADAM_B1 ** ADAM_STEP)
    v_hat = v / (1.0 - ADAM_B2 ** ADAM_STEP)
    delta = -ADAM_LR * (m_hat / (jnp.sqrt(v_hat) + ADAM_EPS) + ADAM_WD * w)
    return delta, m, v


def _chunk_mask():
    r = lax.broadcasted_iota(jnp.int32, (SG_BLOCK, SG_BLOCK), 0)
    c = lax.broadcasted_iota(jnp.int32, (SG_BLOCK, SG_BLOCK), 1)
    return (c // CHUNK) <= (r // CHUNK)


def _place():
    return lax.axis_index("x"), lax.axis_index("y"), lax.axis_index("c")


def _other_chips(x, y):
    return [(1 - x, y), (x, 1 - y), (1 - x, 1 - y)]


def allgather_small(v, name):
    m_per, n = v.shape

    def body(x_ref, out_ref, send_sems, recv_sems, local_sem):
        x, y, c = _place()
        me, sibling = (x, y, c), (x, y, 1 - c)
        chips = _other_chips(x, y)

        def rows(px, py, pc):
            return out_ref.at[pl.ds((4 * px + 2 * py + pc) * m_per, m_per), :]

        def copy(k, block, to, src=None):
            return pltpu.make_async_remote_copy(
                src_ref=rows(*block) if src is None else src, dst_ref=rows(*block),
                send_sem=send_sems.at[k], recv_sem=recv_sems.at[k], device_id=to, device_id_type=MESH)

        mine = pltpu.make_async_copy(x_ref, rows(*me), local_sem)
        mine.start()
        first = [copy(0, me, sibling, src=x_ref)]
        first += [copy(1 + j, me, (*chip, c), src=x_ref) for j, chip in enumerate(chips)]
        for cp in first:
            cp.start()
        passed = [copy(4 + j, (*chip, c), sibling) for j, chip in enumerate(chips)]
        for j, chip in enumerate(chips):
            copy(1 + j, (*chip, c), me).wait_recv()
            passed[j].start()
        copy(0, sibling, me).wait_recv()
        for j, chip in enumerate(chips):
            copy(4 + j, (*chip, 1 - c), me).wait_recv()
        for cp in first + passed:
            cp.wait_send()
        mine.wait()

    return _call(
        body, name=name,
        out_shape=jax.ShapeDtypeStruct((N_DEV * m_per, n), v.dtype),
        in_specs=[pl.BlockSpec(memory_space=pltpu.VMEM)],
        out_specs=pl.BlockSpec(memory_space=pltpu.VMEM),
        scratch_shapes=[pltpu.SemaphoreType.DMA((7,)), pltpu.SemaphoreType.DMA((7,)), pltpu.SemaphoreType.DMA],
    )(v)


def _hbm_spec():
    return pl.BlockSpec(memory_space=pltpu.HBM)


def _sem_spec():
    return pl.BlockSpec(memory_space=pltpu.SEMAPHORE)


def _split_params():
    return pltpu.CompilerParams(has_side_effects=pltpu.SideEffectType.DATAFLOW_SIDE_EFFECTING)


def _hbm(a):
    return pltpu.with_memory_space_constraint(a, pltpu.HBM)


def _half_copy(land_ref, rows, chip_idx, core_half, send_sem, recv_sem, to):
    half = land_ref.at[chip_idx, pl.ds(core_half * (rows // 2), rows // 2), :]
    return pltpu.make_async_remote_copy(
        src_ref=half, dst_ref=half, send_sem=send_sem, recv_sem=recv_sem, device_id=to, device_id_type=MESH)


def halves_start(lands, after, name):
    n = len(lands)

    def body(*refs):
        land_refs, send_sems, recv_sems, token = refs[:n], refs[n + 1], refs[n + 2], refs[-1]
        x, y, c = _place()
        for w in range(n):
            for j, (px, py) in enumerate(_other_chips(x, y)):
                _half_copy(land_refs[w], lands[w].shape[1], 2 * x + y, c,
                           send_sems.at[3 * w + j], recv_sems.at[3 * w + j], (px, py, c)).start()
        token[...] = jnp.zeros_like(token)

    res = _call(
        body, name=name,
        out_shape=(pltpu.SemaphoreType.DMA((3 * n,)), pltpu.SemaphoreType.DMA((3 * n,)),
                   *[pltpu.HBM(a.shape, a.dtype) for a in lands], jax.ShapeDtypeStruct((8, LANES), F32)),
        in_specs=(*[_hbm_spec()] * n, pl.BlockSpec(memory_space=pl.ANY)),
        out_specs=(_sem_spec(), _sem_spec(), *[_hbm_spec()] * n, pl.BlockSpec(memory_space=pltpu.VMEM)),
        input_output_aliases={w: 2 + w for w in range(n)}, compiler_params=_split_params(),
    )(*[_hbm(a) for a in lands], after)
    return res[0], res[1], list(res[2:2 + n]), res[2 + n]


def halves_wait(send_sems, recv_sems, lands, after, name):
    n = len(lands)

    def body(*refs):
        land_refs, send_sems, recv_sems = refs[:n], refs[n], refs[n + 1]
        x, y, c = _place()
        for w in range(n):
            for j, (px, py) in enumerate(_other_chips(x, y)):
                cp = _half_copy(land_refs[w], lands[w].shape[1], 2 * px + py, c,
                                send_sems.at[3 * w + j], recv_sems.at[3 * w + j], (px, py, c))
                cp.wait_send()
                cp.wait_recv()

    return _call(
        body, name=name,
        out_shape=tuple(pltpu.HBM(a.shape, a.dtype) for a in lands),
        in_specs=(*[_hbm_spec()] * n, _sem_spec(), _sem_spec(), pl.BlockSpec(memory_space=pl.ANY)),
        out_specs=tuple(_hbm_spec() for _ in lands), input_output_aliases={w: w for w in range(n)},
        compiler_params=_split_params(),
    )(*lands, send_sems, recv_sems, after)


def pass_halves(lands, name):
    n = len(lands)

    def body(*refs):
        land_refs, send_sems, recv_sems = refs[n:2 * n], refs[2 * n], refs[2 * n + 1]
        x, y, c = _place()
        sent = []
        for w in range(n):
            for j, (px, py) in enumerate(_other_chips(x, y)):
                cp = _half_copy(land_refs[w], lands[w].shape[1], 2 * px + py, c,
                                send_sems.at[3 * w + j], recv_sems.at[3 * w + j], (x, y, 1 - c))
                cp.start()
                sent.append(cp)
        for w in range(n):
            for j, (px, py) in enumerate(_other_chips(x, y)):
                _half_copy(land_refs[w], lands[w].shape[1], 2 * px + py, 1 - c,
                           send_sems.at[3 * w + j], recv_sems.at[3 * w + j], (x, y, 1 - c)).wait_recv()
        for cp in sent:
            cp.wait_send()

    return _call(
        body, name=name,
        out_shape=[jax.ShapeDtypeStruct(a.shape, a.dtype) for a in lands],
        in_specs=[_hbm_spec()] * n, out_specs=[_hbm_spec()] * n,
        input_output_aliases={w: w for w in range(n)},
        scratch_shapes=[pltpu.SemaphoreType.DMA((3 * n,)), pltpu.SemaphoreType.DMA((3 * n,))],
    )(*lands)


def gather_start(land, after, name):
    def body(land_ref, after_ref, send_sems, recv_sems, land_thru, token):
        del after_ref, land_thru
        x, y, c = _place()
        for j, (px, py) in enumerate(_other_chips(x, y)):
            pltpu.make_async_remote_copy(
                src_ref=land_ref.at[2 * x + y], dst_ref=land_ref.at[2 * x + y],
                send_sem=send_sems.at[j], recv_sem=recv_sems.at[j], device_id=(px, py, c),
                device_id_type=MESH).start()
        token[...] = jnp.zeros_like(token)

    return _call(
        body, name=name,
        out_shape=(pltpu.SemaphoreType.DMA((3,)), pltpu.SemaphoreType.DMA((3,)),
                   pltpu.HBM(land.shape, land.dtype), jax.ShapeDtypeStruct((8, LANES), F32)),
        in_specs=(_hbm_spec(), pl.BlockSpec(memory_space=pl.ANY)),
        out_specs=(_sem_spec(), _sem_spec(), _hbm_spec(), pl.BlockSpec(memory_space=pltpu.VMEM)),
        input_output_aliases={0: 2}, compiler_params=_split_params(),
    )(_hbm(land), after)


def gather_wait(send_sems, recv_sems, land, after, name):
    def body(land_ref, send_sems, recv_sems, after_ref, land_out):
        del after_ref, land_out
        x, y, c = _place()
        for j, (px, py) in enumerate(_other_chips(x, y)):
            cp = pltpu.make_async_remote_copy(
                src_ref=land_ref.at[2 * x + y], dst_ref=land_ref.at[2 * px + py],
                send_sem=send_sems.at[j], recv_sem=recv_sems.at[j], device_id=(px, py, c), device_id_type=MESH)
            cp.wait_send()
            cp.wait_recv()

    return _call(
        body, name=name,
        out_shape=pltpu.HBM(land.shape, land.dtype),
        in_specs=(_hbm_spec(), _sem_spec(), _sem_spec(), pl.BlockSpec(memory_space=pl.ANY)),
        out_specs=_hbm_spec(), input_output_aliases={0: 0}, compiler_params=_split_params(),
    )(land, send_sems, recv_sems, after)


def _flips():
    return [(fx, fy, fc) for fx in (0, 1) for fy in (0, 1) for fc in (0, 1) if (fx, fy, fc) != (0, 0, 0)]


def _flipped(x, y, c, flip):
    fx, fy, fc = flip
    return (1 - x if fx else x, 1 - y if fy else y, 1 - c if fc else c)


def gather_all_start(land, name):
    def body(land_ref, send_sems, recv_sems, land_thru, token):
        del land_thru
        x, y, c = _place()
        for k, flip in enumerate(_flips()):
            pltpu.make_async_remote_copy(
                src_ref=land_ref.at[4 * x + 2 * y + c], dst_ref=land_ref.at[4 * x + 2 * y + c],
                send_sem=send_sems.at[k], recv_sem=recv_sems.at[k], device_id=_flipped(x, y, c, flip),
                device_id_type=MESH).start()
        token[...] = jnp.zeros_like(token)

    return _call(
        body, name=name,
        out_shape=(pltpu.SemaphoreType.DMA((7,)), pltpu.SemaphoreType.DMA((7,)),
                   pltpu.HBM(land.shape, land.dtype), jax.ShapeDtypeStruct((8, LANES), F32)),
        in_specs=(_hbm_spec(),),
        out_specs=(_sem_spec(), _sem_spec(), _hbm_spec(), pl.BlockSpec(memory_space=pltpu.VMEM)),
        input_output_aliases={0: 2}, compiler_params=_split_params(),
    )(_hbm(land))


def gather_all_wait(send_sems, recv_sems, land, after, name):
    def body(land_ref, send_sems, recv_sems, after_ref, land_out):
        del after_ref, land_out
        x, y, c = _place()
        for k, flip in enumerate(_flips()):
            px, py, pc = _flipped(x, y, c, flip)
            cp = pltpu.make_async_remote_copy(
                src_ref=land_ref.at[4 * x + 2 * y + c], dst_ref=land_ref.at[4 * px + 2 * py + pc],
                send_sem=send_sems.at[k], recv_sem=recv_sems.at[k], device_id=(px, py, pc), device_id_type=MESH)
            cp.wait_send()
            cp.wait_recv()

    return _call(
        body, name=name,
        out_shape=pltpu.HBM(land.shape, land.dtype),
        in_specs=(_hbm_spec(), _sem_spec(), _sem_spec(), pl.BlockSpec(memory_space=pl.ANY)),
        out_specs=_hbm_spec(), input_output_aliases={0: 0}, compiler_params=_split_params(),
    )(land, send_sems, recv_sems, after)


def exchange_start(parts, name):
    _, r, c_ = parts.shape

    def body(parts_ref, land_ref, send_sems, recv_sems, parts_thru, land_thru, token):
        del parts_thru, land_thru
        x, y, c = _place()
        for j, (px, py) in enumerate(_other_chips(x, y)):
            pltpu.make_async_remote_copy(
                src_ref=parts_ref.at[2 * px + py], dst_ref=land_ref.at[j],
                send_sem=send_sems.at[j], recv_sem=recv_sems.at[j], device_id=(px, py, c),
                device_id_type=MESH).start()
        token[...] = jnp.zeros_like(token)

    return _call(
        body, name=name,
        out_shape=(pltpu.SemaphoreType.DMA((3,)), pltpu.SemaphoreType.DMA((3,)),
                   pltpu.HBM(parts.shape, parts.dtype), pltpu.HBM((3, r, c_), parts.dtype),
                   jax.ShapeDtypeStruct((8, LANES), F32)),
        in_specs=(_hbm_spec(), _hbm_spec()),
        out_specs=(_sem_spec(), _sem_spec(), _hbm_spec(), _hbm_spec(), pl.BlockSpec(memory_space=pltpu.VMEM)),
        input_output_aliases={0: 2, 1: 3}, compiler_params=_split_params(),
    )(_hbm(parts), _hbm(lax.empty((3, r, c_), parts.dtype)))


def exchange_wait(send_sems, recv_sems, parts, land, after, name):
    def body(parts_ref, land_ref, send_sems, recv_sems, after_ref, parts_out, land_out):
        del after_ref, parts_out, land_out
        x, y, c = _place()
        for j, (px, py) in enumerate(_other_chips(x, y)):
            cp = pltpu.make_async_remote_copy(
                src_ref=parts_ref.at[2 * px + py], dst_ref=land_ref.at[j],
                send_sem=send_sems.at[j], recv_sem=recv_sems.at[j], device_id=(px, py, c), device_id_type=MESH)
            cp.wait_send()
            cp.wait_recv()

    return _call(
        body, name=name,
        out_shape=(pltpu.HBM(parts.shape, parts.dtype), pltpu.HBM(land.shape, land.dtype)),
        in_specs=(_hbm_spec(), _hbm_spec(), _sem_spec(), _sem_spec(), pl.BlockSpec(memory_space=pl.ANY)),
        out_specs=(_hbm_spec(), _hbm_spec()), input_output_aliases={0: 0, 1: 1},
        compiler_params=_split_params(),
    )(parts, land, send_sems, recv_sems, after)


def cast_into_slot(w, chip, after, name):
    r, c = w.shape
    tr = min(256, r)

    def body(s_ref, w_ref, after_ref, o_ref):
        del s_ref, after_ref
        o_ref[...] = w_ref[...].astype(BF16)

    return _call(
        body, name=name,
        grid_spec=pltpu.PrefetchScalarGridSpec(
            num_scalar_prefetch=1, grid=(r // tr,),
            in_specs=[pl.BlockSpec((tr, c), lambda i, s: (i, 0)), pl.BlockSpec(memory_space=pl.ANY)],
            out_specs=pl.BlockSpec((None, tr, c), lambda i, s: (s[0], i, 0))),
        out_shape=jax.ShapeDtypeStruct((N_CHIPS, r, c), BF16),
        compiler_params=_params(),
    )(chip.reshape(1).astype(jnp.int32), w, after)


def sum_parts(parts, land, chip, name):
    _, r, c = parts.shape
    tr = min(256, r)

    def body(s_ref, p_ref, l_ref, o_ref):
        del s_ref
        acc = p_ref[...].astype(F32) + l_ref[0].astype(F32)
        acc = acc + l_ref[1].astype(F32)
        o_ref[...] = (acc + l_ref[2].astype(F32)).astype(BF16)

    return _call(
        body, name=name,
        grid_spec=pltpu.PrefetchScalarGridSpec(
            num_scalar_prefetch=1, grid=(r // tr,),
            in_specs=[pl.BlockSpec((None, tr, c), lambda i, s: (s[0], i, 0)),
                      pl.BlockSpec((3, tr, c), lambda i, s: (0, i, 0))],
            out_specs=pl.BlockSpec((tr, c), lambda i, s: (i, 0))),
        out_shape=jax.ShapeDtypeStruct((r, c), BF16),
        compiler_params=_params(),
    )(chip.reshape(1).astype(jnp.int32), parts, land)


def swap_sibling(arrs, name):
    n = len(arrs)

    def body(*refs):
        ins, outs = refs[:n], refs[n:2 * n]
        send_sems, recv_sems = refs[2 * n:]
        x, y, c = _place()
        cps = []
        for w in range(n):
            cp = pltpu.make_async_remote_copy(
                src_ref=ins[w], dst_ref=outs[w], send_sem=send_sems.at[w], recv_sem=recv_sems.at[w],
                device_id=(x, y, 1 - c), device_id_type=MESH)
            cp.start()
            cps.append(cp)
        for cp in cps:
            cp.wait_recv()
        for cp in cps:
            cp.wait_send()

    return _call(
        body, name=name,
        out_shape=[jax.ShapeDtypeStruct(a.shape, a.dtype) for a in arrs],
        in_specs=[_hbm_spec()] * n, out_specs=[_hbm_spec()] * n,
        scratch_shapes=[pltpu.SemaphoreType.DMA((n,)), pltpu.SemaphoreType.DMA((n,))],
    )(*arrs)


def adamw_pair(pa, pb, w, m, v, name):
    r, c = w.shape
    tr = min(256, r)

    def body(pa_ref, pb_ref, w_ref, m_ref, v_ref, g_ref, d_ref, nm_ref, nv_ref):
        g = pa_ref[...].astype(F32) + pb_ref[...].astype(F32)
        d, nm, nv = _adamw(w_ref[...], g, m_ref[...], v_ref[...])
        g_ref[...] = g
        d_ref[...] = d
        nm_ref[...] = nm
        nv_ref[...] = nv

    spec = pl.BlockSpec((tr, c), lambda i: (i, 0))
    return _call(
        body, name=name, grid=(r // tr,),
        out_shape=[jax.ShapeDtypeStruct((r, c), F32)] * 4,
        in_specs=[spec] * 5, out_specs=[spec] * 4,
        compiler_params=_params(),
    )(pa, pb, w, m, v)


def small_update(gathered, first_row, ws, ms, vs, name):
    n_w = len(ws)
    total_rows = gathered.shape[1]

    def body(*refs):
        g_ref = refs[0]
        w_refs, m_refs, v_refs = refs[1:1 + n_w], refs[1 + n_w:1 + 2 * n_w], refs[1 + 2 * n_w:1 + 3 * n_w]
        tail_ref = refs[1 + 3 * n_w]
        outs = refs[2 + 3 * n_w:2 + 7 * n_w]
        sum_ref = refs[2 + 7 * n_w]
        acc = g_ref[0]
        for k in range(1, N_DEV):
            acc = acc + g_ref[k]
        sum_ref[...] = acc
        row = first_row
        for p in range(n_w):
            a, b = ws[p].shape
            per = b // LANES
            g_out, d_out, m_out, v_out = outs[4 * p:4 * p + 4]
            if per == 1:
                g_out[...] = sum_ref[row:row + a, :]
            else:
                for i in range(a):
                    for jc in range(per):
                        g_out[i:i + 1, jc * LANES:(jc + 1) * LANES] = sum_ref[row + i * per + jc:row + i * per + jc + 1, :]
            row += a * per
            dl, nm, nv = _adamw(w_refs[p][...], g_out[...], m_refs[p][...], v_refs[p][...])
            d_out[...] = dl
            m_out[...] = nm
            v_out[...] = nv
        tail_ref[...] = sum_ref[row:row + 1, :]

    out_shape = [jax.ShapeDtypeStruct((1, LANES), F32)]
    for w in ws:
        out_shape += [jax.ShapeDtypeStruct(w.shape, F32)] * 4
    res = _call(
        body, name=name, out_shape=out_shape,
        scratch_shapes=[pltpu.VMEM((total_rows, LANES), F32)],
        compiler_params=_params(),
    )(gathered, *ws, *ms, *vs)
    return res[0], [res[1 + 4 * p:5 + 4 * p] for p in range(n_w)]


def ada_fwd(c_all, w_ada, b_cols, after, name):
    n_l, d, cols = w_ada.shape
    nb = c_all.shape[0]
    tn = cols

    def body(c_ref, w_ref, b_ref, after_ref, o_ref):
        del after_ref
        cv = c_ref[...]
        ca = (cv * _sigmoid(cv)).astype(BF16)
        o_ref[...] = _dot(ca, w_ref[...].astype(BF16), NN) + b_ref[...]

    return _call(
        body, name=name, grid=(n_l, cols // tn),
        out_shape=jax.ShapeDtypeStruct((n_l, nb, cols), F32),
        in_specs=[pl.BlockSpec((nb, d), lambda l, j: (0, 0)),
                  pl.BlockSpec((None, d, tn), lambda l, j: (l, 0, j)),
                  pl.BlockSpec((None, 1, tn), lambda l, j: (l, 0, j)),
                  pl.BlockSpec(memory_space=pl.ANY)],
        out_specs=pl.BlockSpec((None, nb, tn), lambda l, j: (l, 0, j)),
        compiler_params=_params(),
    )(c_all, w_ada, b_cols, after)


def ada_bwd(c_all, dmod_cols, w, m, v, name):
    n_l, d, cols = w.shape
    nb = c_all.shape[0]
    tn = cols

    def body(c_ref, dm_ref, w_ref, m_ref, v_ref, g_ref, d_ref, nm_ref, nv_ref):
        cv = c_ref[...]
        ca = (cv * _sigmoid(cv)).astype(BF16)
        g = _dot(ca, dm_ref[...].astype(BF16), TN)
        dl, nm, nv = _adamw(w_ref[...], g, m_ref[...], v_ref[...])
        g_ref[...] = g
        d_ref[...] = dl
        nm_ref[...] = nm
        nv_ref[...] = nv

    wspec = pl.BlockSpec((None, d, tn), lambda l, j: (l, 0, j))
    return _call(
        body, name=name, grid=(n_l, cols // tn),
        out_shape=[jax.ShapeDtypeStruct((n_l, d, cols), F32)] * 4,
        in_specs=[pl.BlockSpec((nb, d), lambda l, j: (0, 0)),
                  pl.BlockSpec((None, nb, tn), lambda l, j: (l, 0, j)),
                  wspec, wspec, wspec],
        out_specs=[wspec] * 4,
        compiler_params=_params(),
    )(c_all, dmod_cols, w, m, v)


def bias_update(dmod_all, w, m, v, name):
    def body(dm_ref, w_ref, m_ref, v_ref, g_ref, d_ref, nm_ref, nv_ref):
        g = jnp.sum(dm_ref[...], axis=0, keepdims=True)
        dl, nm, nv = _adamw(w_ref[...], g, m_ref[...], v_ref[...])
        g_ref[...] = g
        d_ref[...] = dl
        nm_ref[...] = nm
        nv_ref[...] = nv

    return _call(
        body, name=name,
        out_shape=[jax.ShapeDtypeStruct(w.shape, F32)] * 4,
        compiler_params=_params(),
    )(dmod_all, w, m, v)


def inproj_fwd(x, mod, ng, wg, seq, sectioned, name):
    m_rows, d = x.shape
    nsh, _, ns = wg.shape
    n = nsh * ns
    tm, tn = min(2 * ROW_TILE, seq), _col_tile(ns)
    per = ns // tn

    def body(x_ref, mod_ref, ng_ref, w_ref, proj_ref, h_ref):
        @pl.when(pl.program_id(1) == 0)
        def _():
            xv = x_ref[...]
            r = lax.rsqrt(jnp.mean(xv * xv, axis=-1, keepdims=True) + EPS)
            md = mod_ref[0]
            h = (xv * r * ng_ref[...]) * (1.0 + md[:, d:2 * d]) + md[:, :d]
            h_ref[...] = h.astype(BF16)
        proj_ref[...] = _dot(h_ref[...], w_ref[...], NN)

    if sectioned:
        proj_shape = (nsh, m_rows, ns)
        proj_spec = pl.BlockSpec((None, tm, tn), lambda i, j: (j // per, i, j % per))
    else:
        proj_shape = (m_rows, n)
        proj_spec = pl.BlockSpec((tm, tn), lambda i, j: (i, j))
    return _call(
        body, name=name, grid=(m_rows // tm, n // tn),
        out_shape=[jax.ShapeDtypeStruct(proj_shape, F32), jax.ShapeDtypeStruct((m_rows, d), BF16)],
        in_specs=[pl.BlockSpec((tm, d), lambda i, j: (i, 0)),
                  pl.BlockSpec((1, 1, 3 * d), lambda i, j: ((i * tm) // seq, 0, 0)),
                  pl.BlockSpec((1, d), lambda i, j: (0, 0)),
                  pl.BlockSpec((None, d, tn), lambda i, j: (j // per, 0, j % per))],
        out_specs=[proj_spec, pl.BlockSpec((tm, d), lambda i, j: (i, 0))],
        compiler_params=_params(),
    )(x, mod, ng, wg)


def outproj_fwd(y, w, x, mod, seq, name):
    m_rows, di = y.shape
    d = w.shape[1]
    tm = min(ROW_TILE, seq)

    def body(y_ref, w_ref, x_ref, mod_ref, xn_ref, out_ref):
        acc = _dot(y_ref[...], w_ref[...], NN)
        out_ref[...] = acc.astype(BF16)
        xn_ref[...] = x_ref[...] + mod_ref[0][:, 2 * d:] * acc

    row = pl.BlockSpec((tm, d), lambda i: (i, 0))
    return _call(
        body, name=name, grid=(m_rows // tm,),
        out_shape=[jax.ShapeDtypeStruct((m_rows, d), F32), jax.ShapeDtypeStruct((m_rows, d), BF16)],
        in_specs=[pl.BlockSpec((tm, di), lambda i: (i, 0)),
                  pl.BlockSpec((di, d), lambda i: (0, 0)),
                  row,
                  pl.BlockSpec((1, 1, 3 * d), lambda i: ((i * tm) // seq, 0, 0))],
        out_specs=[row, row],
        compiler_params=_params(),
    )(y, w, x, mod)


def outproj_bwd(dxo, out, mod, w, seq, name):
    m_rows, d = dxo.shape
    di = w.shape[0]
    nb = m_rows // seq
    tm, tn = min(ROW_TILE, seq), di

    def body(dxo_ref, out_ref, mod_ref, w_ref, dy_ref, dout_ref, dgate_ref):
        i = pl.program_id(0)

        @pl.when(pl.program_id(1) == 0)
        def _():
            dx = dxo_ref[...]
            dout_ref[...] = (mod_ref[0][:, 2 * d:] * dx).astype(BF16)
            part = jnp.sum(dx * out_ref[...].astype(F32), axis=0, keepdims=True)

            @pl.when((i * tm) % seq == 0)
            def _():
                dgate_ref[0] = part

            @pl.when((i * tm) % seq != 0)
            def _():
                dgate_ref[0] = dgate_ref[0] + part

        dy_ref[...] = _dot(dout_ref[...], w_ref[...], NT).astype(BF16)

    row = pl.BlockSpec((tm, d), lambda i, j: (i, 0))
    return _call(
        body, name=name, grid=(m_rows // tm, di // tn),
        out_shape=[jax.ShapeDtypeStruct((m_rows, di), BF16), jax.ShapeDtypeStruct((m_rows, d), BF16),
                   jax.ShapeDtypeStruct((nb, 1, d), F32)],
        in_specs=[row, row,
                  pl.BlockSpec((1, 1, 3 * d), lambda i, j: ((i * tm) // seq, 0, 0)),
                  pl.BlockSpec((tn, d), lambda i, j: (j, 0))],
        out_specs=[pl.BlockSpec((tm, tn), lambda i, j: (i, j)), row,
                   pl.BlockSpec((1, 1, d), lambda i, j: ((i * tm) // seq, 0, 0))],
        compiler_params=_params(),
    )(dxo, out, mod, w)


def grad_w_out(y, dout, name):
    m_rows, di = y.shape
    d = dout.shape[1]
    tm, tk = min(ROW_TILE, m_rows), di
    n_m = m_rows // tm

    def body(y_ref, do_ref, o_ref, acc_ref):
        mi = pl.program_id(1)

        @pl.when(mi == 0)
        def _():
            acc_ref[...] = jnp.zeros_like(acc_ref)

        acc_ref[...] += _dot(y_ref[...], do_ref[...], TN)

        @pl.when(mi == n_m - 1)
        def _():
            o_ref[...] = acc_ref[...].astype(BF16)

    return _call(
        body, name=name, grid=(di // tk, n_m),
        out_shape=jax.ShapeDtypeStruct((di, d), BF16),
        in_specs=[pl.BlockSpec((tm, tk), lambda j, mi: (mi, j)),
                  pl.BlockSpec((tm, d), lambda j, mi: (mi, 0))],
        out_specs=pl.BlockSpec((tk, d), lambda j, mi: (j, 0)),
        scratch_shapes=[pltpu.VMEM((tk, d), F32)],
        compiler_params=_params(),
    )(y, dout)


def grad_w_in(h, dproj, nsh, sectioned, name):
    m_rows, d = h.shape
    n = dproj.shape[0] * dproj.shape[2] if sectioned else dproj.shape[1]
    ns = n // nsh
    tm, tn = min(ROW_TILE, m_rows), ns
    per = ns // tn
    n_m = m_rows // tm

    def body(h_ref, dp_ref, o_ref, acc_ref):
        mi = pl.program_id(1)
        @pl.when(mi == 0)
        def _():
            acc_ref[...] = jnp.zeros_like(acc_ref)

        acc_ref[...] += _dot(h_ref[...], dp_ref[...], TN)

        @pl.when(mi == n_m - 1)
        def _():
            o_ref[...] = acc_ref[...].astype(BF16)

    if sectioned:
        dp_spec = pl.BlockSpec((None, tm, tn), lambda j, mi: (j // per, mi, j % per))
    else:
        dp_spec = pl.BlockSpec((tm, tn), lambda j, mi: (mi, j))
    return _call(
        body, name=name, grid=(n // tn, n_m),
        out_shape=jax.ShapeDtypeStruct((nsh, d, ns), BF16),
        in_specs=[pl.BlockSpec((tm, d), lambda j, mi: (mi, 0)), dp_spec],
        out_specs=pl.BlockSpec((None, d, tn), lambda j, mi: (j // per, 0, j % per)),
        scratch_shapes=[pltpu.VMEM((d, tn), F32)],
        compiler_params=_params(),
    )(h, dproj)


def inproj_bwd(dproj, wg, x, dxo, mod, ng, seq, sectioned, name):
    m_rows, d = x.shape
    nsh, _, ns = wg.shape
    n = nsh * ns
    nb = m_rows // seq
    tm, tk = min(ROW_TILE, seq), ns
    per = ns // tk
    n_k = n // tk

    def body(dp_ref, w_ref, x_ref, dxo_ref, mod_ref, ng_ref, dxi_ref, dsh_ref, dsc_ref, dng_ref, acc_ref):
        i, k = pl.program_id(0), pl.program_id(1)
        @pl.when(k == 0)
        def _():
            acc_ref[...] = jnp.zeros_like(acc_ref)

        acc_ref[...] += _dot(dp_ref[...], w_ref[...], NT)

        @pl.when(k == n_k - 1)
        def _():
            dh = acc_ref[...]
            xv = x_ref[...]
            r = lax.rsqrt(jnp.mean(xv * xv, axis=-1, keepdims=True) + EPS)
            xn = xv * r
            md = mod_ref[0]
            gain = ng_ref[...]
            p_shift = jnp.sum(dh, axis=0, keepdims=True)
            p_scale = jnp.sum(dh * (xn * gain), axis=0, keepdims=True)
            drn = dh * (1.0 + md[:, d:2 * d])
            p_ng = jnp.sum(drn * xn, axis=0, keepdims=True)
            dxn = drn * gain
            dx = r * (dxn - xn * jnp.mean(dxn * xn, axis=-1, keepdims=True))
            dxi_ref[...] = dxo_ref[...] + dx

            @pl.when((i * tm) % seq == 0)
            def _():
                dsh_ref[0] = p_shift
                dsc_ref[0] = p_scale

            @pl.when((i * tm) % seq != 0)
            def _():
                dsh_ref[0] = dsh_ref[0] + p_shift
                dsc_ref[0] = dsc_ref[0] + p_scale

            @pl.when(i == 0)
            def _():
                dng_ref[...] = p_ng

            @pl.when(i != 0)
            def _():
                dng_ref[...] = dng_ref[...] + p_ng

    if sectioned:
        dp_spec = pl.BlockSpec((None, tm, tk), lambda i, k: (k // per, i, k % per))
    else:
        dp_spec = pl.BlockSpec((tm, tk), lambda i, k: (i, k))
    row = pl.BlockSpec((tm, d), lambda i, k: (i, 0))
    per_seq = pl.BlockSpec((1, 1, d), lambda i, k: ((i * tm) // seq, 0, 0))
    return _call(
        body, name=name, grid=(m_rows // tm, n_k),
        out_shape=[jax.ShapeDtypeStruct((m_rows, d), F32), jax.ShapeDtypeStruct((nb, 1, d), F32),
                   jax.ShapeDtypeStruct((nb, 1, d), F32), jax.ShapeDtypeStruct((1, d), F32)],
        in_specs=[dp_spec,
                  pl.BlockSpec((None, d, tk), lambda i, k: (k // per, 0, k % per)),
                  row, row,
                  pl.BlockSpec((1, 1, 3 * d), lambda i, k: ((i * tm) // seq, 0, 0)),
                  pl.BlockSpec((1, d), lambda i, k: (0, 0))],
        out_specs=[row, per_seq, per_seq, pl.BlockSpec((1, d), lambda i, k: (0, 0))],
        scratch_shapes=[pltpu.VMEM((tm, d), F32)],
        compiler_params=_params(),
    )(dproj, wg, x, dxo, mod, ng)


def _sgu_stats(proj_ref, vg_ref, di, gd, dgel_ref=None):
    s1 = jnp.zeros((SG_BLOCK, 1), F32)
    for g in range(SG_GROUPS):
        v_pre = proj_ref[:, di + g * gd:di + (g + 1) * gd]
        if dgel_ref is None:
            vg = _gelu(v_pre)
        else:
            vg, dgel_ref[:, g * gd:(g + 1) * gd] = _gelu_and_grad(v_pre)
        vg_ref[:, g * gd:(g + 1) * gd] = vg
        s1 = s1 + jnp.sum(vg, axis=1, keepdims=True)
    mu = s1 / di
    s2 = jnp.zeros((SG_BLOCK, 1), F32)
    for g in range(SG_GROUPS):
        dv = vg_ref[:, g * gd:(g + 1) * gd] - mu
        s2 = s2 + jnp.sum(dv * dv, axis=1, keepdims=True)
    return mu, lax.rsqrt(s2 / di + EPS)


def sgu_fwd(proj, ln_gain, ln_bias, ws, bs, name):
    m_rows, n3 = proj.shape
    di = n3 // 3
    gd = di // SG_GROUPS
    n_blocks = m_rows // SG_BLOCK
    per_step = SG_STEP_BLOCKS if n_blocks % SG_STEP_BLOCKS == 0 else 1

    def body(proj_ref, lg_ref, lb_ref, ws_ref, bs_ref, y_ref, wsm_ref, vg_ref):
        @pl.when(pl.program_id(0) == 0)
        def _():
            mask = _chunk_mask()
            for g in range(SG_GROUPS):
                wsm_ref[g] = jnp.where(mask, ws_ref[g], 0.0).astype(BF16)

        for blk in range(per_step):
            p_ref, o_ref = proj_ref.at[blk], y_ref.at[blk]
            mu, rstd = _sgu_stats(p_ref, vg_ref, di, gd)
            for g in range(SG_GROUPS):
                cs = slice(g * gd, (g + 1) * gd)
                vln = (vg_ref[:, cs] - mu) * rstd * lg_ref[:, cs] + lb_ref[:, cs]
                s = _dot(wsm_ref[g], vln.astype(BF16), NN) + bs_ref[g]
                u = _gelu(p_ref[:, cs])
                gp = p_ref[:, 2 * di + g * gd:2 * di + (g + 1) * gd]
                o_ref[:, cs] = (u * s * (gp * _sigmoid(gp))).astype(BF16)

    full = lambda shape: pl.BlockSpec(shape, lambda i: (0,) * len(shape))
    return _call(
        body, name=name, grid=(n_blocks // per_step,),
        out_shape=jax.ShapeDtypeStruct((n_blocks, SG_BLOCK, di), BF16),
        in_specs=[pl.BlockSpec((per_step, SG_BLOCK, n3), lambda i: (i, 0, 0)),
                  full((1, di)), full((1, di)),
                  full((SG_GROUPS, SG_BLOCK, SG_BLOCK)), full((SG_GROUPS, SG_BLOCK, 1))],
        out_specs=pl.BlockSpec((per_step, SG_BLOCK, di), lambda i: (i, 0, 0)),
        scratch_shapes=[pltpu.VMEM((SG_GROUPS, SG_BLOCK, SG_BLOCK), BF16), pltpu.VMEM((SG_BLOCK, di), F32)],
        compiler_params=_params(),
    )(proj.reshape(n_blocks, SG_BLOCK, n3), ln_gain, ln_bias, ws, bs).reshape(m_rows, di)


def sgu_bwd(proj, dy, ln_gain, ln_bias, ws, bs, name):
    m_rows, n3 = proj.shape
    di = n3 // 3
    gd = di // SG_GROUPS
    n_i = m_rows // SG_BLOCK

    def body(proj_ref, dy_ref, lg_ref, lb_ref, ws_ref, bs_ref,
             dp_ref, dws_ref, dbs_ref, dlg_ref, dlb_ref, wsm_ref, vg_ref, dvh_ref, dgel_ref):
        i = pl.program_id(0)

        def before():
            @pl.when(i == 0)
            def _():
                mask = _chunk_mask()
                for g in range(SG_GROUPS):
                    wsm_ref[g] = jnp.where(mask, ws_ref[g], 0.0).astype(BF16)
                dws_ref[...] = jnp.zeros_like(dws_ref)
                dbs_ref[...] = jnp.zeros_like(dbs_ref)
                dlg_ref[...] = jnp.zeros_like(dlg_ref)
                dlb_ref[...] = jnp.zeros_like(dlb_ref)

        def after():
            @pl.when(i == n_i - 1)
            def _():
                mask = _chunk_mask()
                for g in range(SG_GROUPS):
                    dws_ref[g] = jnp.where(mask, dws_ref[g], 0.0)

        before()
        mu, rstd = _sgu_stats(proj_ref, vg_ref, di, gd, dgel_ref)
        m1 = jnp.zeros((SG_BLOCK, 1), F32)
        m2 = jnp.zeros((SG_BLOCK, 1), F32)
        for g in range(SG_GROUPS):
            cs = slice(g * gd, (g + 1) * gd)
            gs = slice(2 * di + g * gd, 2 * di + (g + 1) * gd)
            gain = lg_ref[:, cs]
            vhat = (vg_ref[:, cs] - mu) * rstd
            vln_b = (vhat * gain + lb_ref[:, cs]).astype(BF16)
            s = _dot(wsm_ref[g], vln_b, NN) + bs_ref[g]
            u, du = _gelu_and_grad(proj_ref[:, cs])
            sg, dsg = _silu_and_grad(proj_ref[:, gs])
            dyv = dy_ref[:, cs].astype(F32)
            dp_ref[:, cs] = (dyv * s * sg * du).astype(BF16)
            dp_ref[:, gs] = (dyv * u * s * dsg).astype(BF16)
            ds = dyv * u * sg
            ds_b = ds.astype(BF16)
            dws_ref[g] = dws_ref[g] + _dot(ds_b, vln_b, NT)
            dbs_ref[g] = dbs_ref[g] + jnp.sum(ds, axis=1, keepdims=True)
            dvln = _dot(wsm_ref[g], ds_b, TN)
            dlg_ref[:, cs] = dlg_ref[:, cs] + jnp.sum(dvln * vhat, axis=0, keepdims=True)
            dlb_ref[:, cs] = dlb_ref[:, cs] + jnp.sum(dvln, axis=0, keepdims=True)
            dvh = dvln * gain
            dvh_ref[:, cs] = dvh
            m1 = m1 + jnp.sum(dvh, axis=1, keepdims=True)
            m2 = m2 + jnp.sum(dvh * vhat, axis=1, keepdims=True)
        m1 = m1 / di
        m2 = m2 / di
        for g in range(SG_GROUPS):
            cs = slice(g * gd, (g + 1) * gd)
            vs = slice(di + g * gd, di + (g + 1) * gd)
            vhat = (vg_ref[:, cs] - mu) * rstd
            dvg = rstd * (dvh_ref[:, cs] - m1 - vhat * m2)
            dp_ref[:, vs] = (dvg * dgel_ref[:, cs]).astype(BF16)

        after()

    full = lambda shape: pl.BlockSpec(shape, lambda i: (0,) * len(shape))
    return _call(
        body, name=name, grid=(n_i,),
        out_shape=[jax.ShapeDtypeStruct((m_rows, n3), BF16),
                   jax.ShapeDtypeStruct((SG_GROUPS, SG_BLOCK, SG_BLOCK), F32),
                   jax.ShapeDtypeStruct((SG_GROUPS, SG_BLOCK, 1), F32),
                   jax.ShapeDtypeStruct((1, di), F32), jax.ShapeDtypeStruct((1, di), F32)],
        in_specs=[pl.BlockSpec((SG_BLOCK, n3), lambda i: (i, 0)),
                  pl.BlockSpec((SG_BLOCK, di), lambda i: (i, 0)),
                  full((1, di)), full((1, di)),
                  full((SG_GROUPS, SG_BLOCK, SG_BLOCK)), full((SG_GROUPS, SG_BLOCK, 1))],
        out_specs=[pl.BlockSpec((SG_BLOCK, n3), lambda i: (i, 0)),
                   full((SG_GROUPS, SG_BLOCK, SG_BLOCK)), full((SG_GROUPS, SG_BLOCK, 1)),
                   full((1, di)), full((1, di))],
        scratch_shapes=[pltpu.VMEM((SG_GROUPS, SG_BLOCK, SG_BLOCK), BF16),
                        pltpu.VMEM((SG_BLOCK, di), F32), pltpu.VMEM((SG_BLOCK, di), F32),
                        pltpu.VMEM((SG_BLOCK, di), F32)],
        compiler_params=_params(),
    )(proj, dy, ln_gain, ln_bias, ws, bs)


def _lower_bound(lbraw):
    mx = jnp.maximum(lbraw[0:1, :], lbraw[1:2, :])
    e0 = jnp.exp(lbraw[0:1, :] - mx)
    e1 = jnp.exp(lbraw[1:2, :] - mx)
    p0 = e0 / (e0 + e1)
    p1 = e1 / (e0 + e1)
    return (p0 + p1) - p0, p0, p1


def _tri(lower):
    r = lax.broadcasted_iota(jnp.int32, (CHUNK, CHUNK), 0)
    c = lax.broadcasted_iota(jnp.int32, (CHUNK, CHUNK), 1)
    return ((r >= c) if lower else (c >= r)).astype(BF16)


def _running_sum(tri, x):
    x1 = x.astype(BF16)
    r1 = x - x1.astype(F32)
    x2 = r1.astype(BF16)
    x3 = (r1 - x2.astype(F32)).astype(BF16)
    return _dot(tri, x1, NN) + _dot(tri, x2, NN) + _dot(tri, x3, NN)


def _row(a, idx):
    r = lax.broadcasted_iota(jnp.int32, a.shape, 0)
    return jnp.sum(jnp.where(r == idx, a, 0.0), axis=0, keepdims=True)


def _hgrn_gates(qp, fp, lb, tri):
    sgm = _sigmoid_small(fp)
    f = lb + (1.0 - lb) * sgm
    k = 1.0 - f
    a = _running_sum(tri, jnp.log(f))
    a_mid = _row(a, CHUNK // 2 - 1)
    a_last = _row(a, CHUNK - 1)
    q, dq = _silu_and_grad(qp)
    e1, e2, e3, e4 = jnp.exp(a - a_mid), jnp.exp(a_mid - a), jnp.exp(a), jnp.exp(a_last - a)
    return dict(sgm=sgm, f=f, k=k, q=q, dq=dq, e1=e1, e2=e2, e3=e3, e4=e4, dec=jnp.exp(a_last),
                q_in=q * e1, k_in=k * e2, q_out=q * e3, k_out=k * e4)


def _causal():
    r = lax.broadcasted_iota(jnp.int32, (CHUNK, CHUNK), 0)
    c = lax.broadcasted_iota(jnp.int32, (CHUNK, CHUNK), 1)
    return r >= c


def hgrn_fwd(proj4, lbraw, gn, seq, name):
    _, m_rows, di = proj4.shape
    nb, nh, nc = m_rows // seq, di // HEAD_DIM, seq // CHUNK
    rows = min(HG_ROWS, seq)
    wide = HG_WIDE * HEAD_DIM
    ns, cpb = seq // rows, rows // CHUNK

    def body(p_ref, lb_ref, gn_ref, y_ref, sts_ref, st_ref):
        @pl.when(pl.program_id(2) == 0)
        def _():
            st_ref[...] = jnp.zeros_like(st_ref)

        tri = _tri(True)
        causal = _causal()
        gain = gn_ref[...]
        lbs = [_lower_bound(lb_ref[:, j * HEAD_DIM:(j + 1) * HEAD_DIM])[0] for j in range(HG_WIDE)]

        units = [(n, j) for n in range(cpb) for j in range(HG_WIDE)]
        rs = lambda n: slice(n * CHUNK, (n + 1) * CHUNK)
        cs = lambda j: slice(j * HEAD_DIM, (j + 1) * HEAD_DIM)
        gates, v_b, sc_b, kv, o_in, o_x = {}, {}, {}, {}, {}, {}
        for n, j in units:
            gates[n, j] = _hgrn_gates(p_ref[0, rs(n), cs(j)], p_ref[1, rs(n), cs(j)], lbs[j], tri)
            v_b[n, j] = p_ref[2, rs(n), cs(j)].astype(BF16)
        for u in units:
            t = gates[u]
            sc_b[u] = jnp.where(causal, _dot(t["q_in"].astype(BF16), t["k_in"].astype(BF16), NT), 0.0).astype(BF16)
            kv[u] = _dot(v_b[u], t["k_out"].astype(BF16), TN)
        for u in units:
            o_in[u] = _dot(sc_b[u], v_b[u], NN)
        for j in range(HG_WIDE):
            st = st_ref[j]
            for n in range(cpb):
                sts_ref[n, :, cs(j)] = st
                o_x[n, j] = _dot(gates[n, j]["q_out"].astype(BF16), st.astype(BF16), NT)
                st = st * gates[n, j]["dec"] + kv[n, j]
            st_ref[j] = st
        for n, j in units:
            o = o_in[n, j] + o_x[n, j]
            r = lax.rsqrt(jnp.mean(o * o, axis=-1, keepdims=True) + EPS)
            gp = p_ref[3, rs(n), cs(j)]
            y_ref[rs(n), cs(j)] = ((o * r * gain) * (gp * _sigmoid(gp))).astype(BF16)

    return _call(
        body, name=name, grid=(nh // HG_WIDE, nb, ns),
        out_shape=[jax.ShapeDtypeStruct((m_rows, di), BF16),
                   jax.ShapeDtypeStruct((nb * nc, HEAD_DIM, di), F32)],
        in_specs=[pl.BlockSpec((4, rows, wide), lambda hg, b, s: (0, b * ns + s, hg)),
                  pl.BlockSpec((2, wide), lambda hg, b, s: (0, hg)),
                  pl.BlockSpec((1, HEAD_DIM), lambda hg, b, s: (0, 0))],
        out_specs=[pl.BlockSpec((rows, wide), lambda hg, b, s: (b * ns + s, hg)),
                   pl.BlockSpec((cpb, HEAD_DIM, wide), lambda hg, b, s: (b * ns + s, 0, hg))],
        scratch_shapes=[pltpu.VMEM((HG_WIDE, HEAD_DIM, HEAD_DIM), F32)],
        compiler_params=_params(),
    )(proj4, lbraw, gn)


def hgrn_bwd(proj4, dy, sts, lbraw, gn, seq, name):
    _, m_rows, di = proj4.shape
    nb, nh, nc = m_rows // seq, di // HEAD_DIM, seq // CHUNK
    rows = min(HG_ROWS, seq)
    wide = HG_WIDE * HEAD_DIM
    ns, cpb = seq // rows, rows // CHUNK
    n_hg = nh // HG_WIDE

    def body(p_ref, dy_ref, sts_ref, lb_ref, gn_ref, dp_ref, dlb_ref, dgn_ref, dst_ref, lbacc_ref, gnacc_ref):
        hg, b, s = pl.program_id(0), pl.program_id(1), pl.program_id(2)
        tri, triu = _tri(True), _tri(False)
        causal = _causal()
        gain = gn_ref[...]
        first = (b == 0) & (s == 0)
        cs = lambda j: slice(j * HEAD_DIM, (j + 1) * HEAD_DIM)

        def before():
            @pl.when((hg == 0) & first)
            def _():
                gnacc_ref[...] = jnp.zeros_like(gnacc_ref)

            @pl.when(first)
            def _():
                lbacc_ref[...] = jnp.zeros_like(lbacc_ref)

            @pl.when(s == 0)
            def _():
                dst_ref[...] = jnp.zeros_like(dst_ref)

        def after():
            @pl.when((b == nb - 1) & (s == ns - 1))
            def _():
                for j in range(HG_WIDE):
                    _, p0, p1 = _lower_bound(lb_ref[:, cs(j)])
                    acc = lbacc_ref[:, cs(j)]
                    dlb_ref[0:1, cs(j)] = -acc * p0 * p1
                    dlb_ref[1:2, cs(j)] = acc * p1 * (1.0 - p1)

            @pl.when((hg == n_hg - 1) & (b == nb - 1) & (s == ns - 1))
            def _():
                tot = gnacc_ref[:, 0:HEAD_DIM]
                for j in range(1, HG_WIDE):
                    tot = tot + gnacc_ref[:, cs(j)]
                dgn_ref[...] = tot

        before()

        units = [(n, j) for n in range(cpb) for j in range(HG_WIDE)]
        rs = lambda n: slice(n * CHUNK, (n + 1) * CHUNK)
        lbs = [_lower_bound(lb_ref[:, cs(j)])[0] for j in range(HG_WIDE)]
        gates, v_b, st_b, sc_b, o, do_b = {}, {}, {}, {}, {}, {}
        dq_out, dsc_b, dv, g_st, dq_in, dk_in, dst_at, dk_out, ddec = {}, {}, {}, {}, {}, {}, {}, {}, {}
        for n, j in units:
            gates[n, j] = _hgrn_gates(p_ref[0, rs(n), cs(j)], p_ref[1, rs(n), cs(j)], lbs[j], tri)
            v_b[n, j] = p_ref[2, rs(n), cs(j)].astype(BF16)
            st_b[n, j] = sts_ref[n, :, cs(j)].astype(BF16)
        for u in units:
            t = gates[u]
            sc_b[u] = jnp.where(causal, _dot(t["q_in"].astype(BF16), t["k_in"].astype(BF16), NT), 0.0).astype(BF16)
        for u in units:
            o[u] = _dot(sc_b[u], v_b[u], NN) + _dot(gates[u]["q_out"].astype(BF16), st_b[u], NT)
        for n, j in units:
            ov = o[n, j]
            r = lax.rsqrt(jnp.mean(ov * ov, axis=-1, keepdims=True) + EPS)
            ohat = ov * r
            sg, dsg = _silu_and_grad(p_ref[3, rs(n), cs(j)])
            dyv = dy_ref[rs(n), cs(j)].astype(F32)
            dp_ref[3, rs(n), cs(j)] = (dyv * (ohat * gain) * dsg).astype(BF16)
            d_on = dyv * sg
            gnacc_ref[:, cs(j)] = gnacc_ref[:, cs(j)] + jnp.sum(d_on * ohat, axis=0, keepdims=True)
            dohat = d_on * gain
            do_b[n, j] = (r * (dohat - ohat * jnp.mean(dohat * ohat, axis=-1, keepdims=True))).astype(BF16)
        for u in units:
            dq_out[u] = _dot(do_b[u], st_b[u], NN)
            dsc_b[u] = jnp.where(causal, _dot(do_b[u], v_b[u], NT), 0.0).astype(BF16)
            dv[u] = _dot(sc_b[u], do_b[u], TN)
            g_st[u] = _dot(do_b[u], gates[u]["q_out"].astype(BF16), TN)
        for u in units:
            dq_in[u] = _dot(dsc_b[u], gates[u]["k_in"].astype(BF16), NN)
            dk_in[u] = _dot(dsc_b[u], gates[u]["q_in"].astype(BF16), TN)
        for j in range(HG_WIDE):
            dst = dst_ref[j]
            for n in reversed(range(cpb)):
                dst_at[n, j] = dst
                dst = dst * gates[n, j]["dec"] + g_st[n, j]
            dst_ref[j] = dst
        for n, j in units:
            dst = dst_at[n, j]
            dst_b = dst.astype(BF16)
            dk_out[n, j] = _dot(v_b[n, j], dst_b, NN)
            dv[n, j] = dv[n, j] + _dot(gates[n, j]["k_out"].astype(BF16), dst_b, NT)
            ddec[n, j] = jnp.sum(dst * sts_ref[n, :, cs(j)], axis=0, keepdims=True)
        for n, j in units:
            t = gates[n, j]
            dp_ref[2, rs(n), cs(j)] = dv[n, j].astype(BF16)
            dq = dq_in[n, j] * t["e1"] + dq_out[n, j] * t["e3"]
            dk = dk_in[n, j] * t["e2"] + dk_out[n, j] * t["e4"]
            w_in = dq_in[n, j] * t["q_in"] - dk_in[n, j] * t["k_in"]
            w_out = dk_out[n, j] * t["k_out"]
            da = w_in + dq_out[n, j] * t["q_out"] - w_out
            da_mid = -jnp.sum(w_in, axis=0, keepdims=True)
            da_last = jnp.sum(w_out, axis=0, keepdims=True) + ddec[n, j] * t["dec"]
            rid = lax.broadcasted_iota(jnp.int32, da.shape, 0)
            da = da + jnp.where(rid == CHUNK // 2 - 1, da_mid, 0.0) + jnp.where(rid == CHUNK - 1, da_last, 0.0)
            dlf = _running_sum(triu, da)
            df = dlf / t["f"] - dk
            sgm = t["sgm"]
            dp_ref[1, rs(n), cs(j)] = (df * (1.0 - lbs[j]) * sgm * (1.0 - sgm)).astype(BF16)
            lbacc_ref[:, cs(j)] = lbacc_ref[:, cs(j)] + jnp.sum(df * (1.0 - sgm), axis=0, keepdims=True)
            dp_ref[0, rs(n), cs(j)] = (dq * t["dq"]).astype(BF16)

        after()

    blk = lambda hg, b, s: b * ns + (ns - 1 - s)
    return _call(
        body, name=name, grid=(n_hg, nb, ns),
        out_shape=[jax.ShapeDtypeStruct((4, m_rows, di), BF16), jax.ShapeDtypeStruct((2, di), F32),
                   jax.ShapeDtypeStruct((1, HEAD_DIM), F32)],
        in_specs=[pl.BlockSpec((4, rows, wide), lambda hg, b, s: (0, blk(hg, b, s), hg)),
                  pl.BlockSpec((rows, wide), lambda hg, b, s: (blk(hg, b, s), hg)),
                  pl.BlockSpec((cpb, HEAD_DIM, wide), lambda hg, b, s: (blk(hg, b, s), 0, hg)),
                  pl.BlockSpec((2, wide), lambda hg, b, s: (0, hg)),
                  pl.BlockSpec((1, HEAD_DIM), lambda hg, b, s: (0, 0))],
        out_specs=[pl.BlockSpec((4, rows, wide), lambda hg, b, s: (0, blk(hg, b, s), hg)),
                   pl.BlockSpec((2, wide), lambda hg, b, s: (0, hg)),
                   pl.BlockSpec((1, HEAD_DIM), lambda hg, b, s: (0, 0))],
        scratch_shapes=[pltpu.VMEM((HG_WIDE, HEAD_DIM, HEAD_DIM), F32), pltpu.VMEM((1, wide), F32),
                        pltpu.VMEM((1, wide), F32)],
        compiler_params=_params(),
    )(proj4, dy, sts, lbraw, gn)


def outproj_loss(y, w, x, mod, fg, target, seq, name):
    m_rows, di = y.shape
    d = w.shape[1]
    tm = min(512, seq)

    def body(y_ref, w_ref, x_ref, mod_ref, fg_ref, t_ref, out_ref, loss_ref, dx_ref, dfg_ref):
        i = pl.program_id(0)
        acc = _dot(y_ref[...], w_ref[...], NN)
        out_ref[...] = acc.astype(BF16)
        xv = x_ref[...] + mod_ref[0][:, 2 * d:] * acc
        gain = fg_ref[...]
        r = lax.rsqrt(jnp.mean(xv * xv, axis=-1, keepdims=True) + EPS)
        xn = xv * r
        e = xn * gain - t_ref[...]
        part = 0.5 * jnp.sum(jnp.mean(e * e, axis=-1, keepdims=True), axis=0, keepdims=True)
        dyv = e / d
        p_fg = jnp.sum(dyv * xn, axis=0, keepdims=True)
        dxn = dyv * gain
        dx_ref[...] = r * (dxn - xn * jnp.mean(dxn * xn, axis=-1, keepdims=True))

        @pl.when(i == 0)
        def _():
            loss_ref[...] = part
            dfg_ref[...] = p_fg

        @pl.when(i != 0)
        def _():
            loss_ref[...] = loss_ref[...] + part
            dfg_ref[...] = dfg_ref[...] + p_fg

    row = pl.BlockSpec((tm, d), lambda i: (i, 0))
    return _call(
        body, name=name, grid=(m_rows // tm,),
        out_shape=[jax.ShapeDtypeStruct((m_rows, d), BF16), jax.ShapeDtypeStruct((1, 1), F32),
                   jax.ShapeDtypeStruct((m_rows, d), F32), jax.ShapeDtypeStruct((1, d), F32)],
        in_specs=[pl.BlockSpec((tm, di), lambda i: (i, 0)),
                  pl.BlockSpec((di, d), lambda i: (0, 0)),
                  row,
                  pl.BlockSpec((1, 1, 3 * d), lambda i: ((i * tm) // seq, 0, 0)),
                  pl.BlockSpec((1, d), lambda i: (0, 0)), row],
        out_specs=[row, pl.BlockSpec((1, 1), lambda i: (0, 0)), row, pl.BlockSpec((1, d), lambda i: (0, 0))],
        compiler_params=_params(),
    )(y, w, x, mod, fg, target)


def _pack(parts):
    flat = jnp.concatenate([p.reshape(-1) for p in parts])
    pad = (-flat.shape[0]) % (8 * LANES)
    return jnp.pad(flat, (0, pad)).reshape(-1, LANES)


def kernel(x, c, norm_gain, w_ada, b_ada, a_w_in, a_ln_gain, a_ln_bias, a_w_s, a_b_s, a_w_out, b_w_in, b_lower_bounds, b_gn_gain, b_w_out, final_gain, loss_target, m_norm_gain, m_w_ada, m_b_ada, m_a_w_in, m_a_ln_gain, m_a_ln_bias, m_a_w_s, m_a_b_s, m_a_w_out, m_b_w_in, m_b_lower_bounds, m_b_gn_gain, m_b_w_out, m_final_gain, v_norm_gain, v_w_ada, v_b_ada, v_a_w_in, v_a_ln_gain, v_a_ln_bias, v_a_w_s, v_a_b_s, v_a_w_out, v_b_w_in, v_b_lower_bounds, v_b_gn_gain, v_b_w_out, v_final_gain):
    nb, seq, d = x.shape
    m_rows = nb * seq
    n_l = w_ada.shape[0]
    ada_cols = w_ada.shape[2]
    px, py, pc = _place()
    chip = 2 * px + py
    dev = 2 * chip + pc

    c_all = allgather_small(c.reshape(-1, LANES), "gather_c").reshape(N_DEV * nb, d)
    s_a = halves_start([cast_into_slot(a_w_in[0], chip, c_all, "cast_a_in")], c_all, "gather_a_in_start")
    land_b_in = cast_into_slot(b_w_in[0], chip, s_a[3], "cast_b_in")
    land_b_out = cast_into_slot(b_w_out[0], chip, land_b_in, "cast_b_out")
    b_cols = lax.dynamic_slice_in_dim(b_ada, chip * ada_cols, ada_cols, axis=1).reshape(n_l, 1, ada_cols)
    mod_cols = ada_fwd(c_all, w_ada, b_cols, land_b_out, "ada_fwd")
    mod_g = allgather_small(mod_cols.reshape(-1, LANES), "gather_mod")
    mod_g = mod_g.reshape(N_CHIPS, 2, n_l, N_DEV * nb, ada_cols)[:, 0]
    mod_all = jnp.transpose(mod_g, (1, 2, 0, 3)).reshape(n_l, N_DEV * nb, 3 * d)
    mod_mine = lax.dynamic_slice_in_dim(mod_all, dev * nb, nb, axis=1)
    mod0 = mod_mine[0].reshape(nb, 1, 3 * d)
    mod1 = mod_mine[1].reshape(nb, 1, 3 * d)

    s_ao = halves_start([cast_into_slot(a_w_out[0], chip, mod_mine, "cast_a_out")], mod_mine, "gather_a_out_start")
    landed_a = halves_wait(s_a[0], s_a[1], s_a[2], s_ao[3], "gather_a_in_wait")
    s_bi = halves_start([land_b_in], landed_a[0], "gather_b_in_start")
    s_bo = gather_start(land_b_out, s_bi[3], "gather_b_out_start")
    (wa_in,) = pass_halves(list(landed_a), "gather_a_in_pass")
    di = a_w_out.shape[1] * N_CHIPS

    x0 = x.reshape(m_rows, d)
    tgt = loss_target.reshape(m_rows, d)
    ng0 = norm_gain[0:1] + (s_bi[3][0, 0] + s_bo[3][0, 0])
    ng1 = norm_gain[1:2]
    bs_col = a_b_s[0].reshape(SG_GROUPS, SG_BLOCK, 1)
    proj_a, h_a = inproj_fwd(x0, mod0, ng0, wa_in, seq, False, "a_inproj")
    y_a = sgu_fwd(proj_a, a_ln_gain, a_ln_bias, a_w_s[0], bs_col, "a_sgu")
    (wa_out,) = pass_halves(list(halves_wait(s_ao[0], s_ao[1], s_ao[2], y_a, "gather_a_out_wait")), "gather_a_out_pass")
    wa_out = wa_out.reshape(di, d)
    x1, out_a = outproj_fwd(y_a, wa_out, x0, mod0, seq, "a_outproj")
    (wb_in,) = pass_halves(list(halves_wait(s_bi[0], s_bi[1], s_bi[2], out_a, "gather_b_in_wait")), "gather_b_in_pass")
    proj_b, h_b = inproj_fwd(x1, mod1, ng1, wb_in, seq, True, "b_inproj")
    y_b, sts_b = hgrn_fwd(proj_b, b_lower_bounds, b_gn_gain, seq, "b_hgrn")
    wb_out = gather_wait(*s_bo[:3], y_b, "gather_b_out_wait").reshape(di, d)
    out_b, loss_part, dx2, dfg = outproj_loss(
        y_b, wb_out, x1, mod1, final_gain.reshape(1, d), tgt, seq, "b_outproj_loss")

    shard_rows = di // N_CHIPS
    dy_b, dout_b, dgate1 = outproj_bwd(dx2, out_b, mod1, wb_out, seq, "b_outproj_bwd")
    gwb_out = grad_w_out(y_b, dout_b, "b_grad_w_out").reshape(N_CHIPS, shard_rows, d)
    e_bo = exchange_start(gwb_out, "exchange_b_out_start")
    dproj_b, dlb, dgn = hgrn_bwd(
        proj_b, dy_b, sts_b, b_lower_bounds, b_gn_gain + e_bo[4][0, 0], seq, "b_hgrn_bwd")
    e_bi = exchange_start(grad_w_in(h_b, dproj_b, N_CHIPS, True, "b_grad_w_in"), "exchange_b_in_start")
    dx1, dshift1, dscale1, dng1 = inproj_bwd(
        dproj_b, wb_in, x1, dx2, mod1, ng1 + e_bi[4][0, 0], seq, True, "b_inproj_bwd")

    dy_a, dout_a, dgate0 = outproj_bwd(dx1, out_a, mod0, wa_out, seq, "a_outproj_bwd")
    gwa_out = grad_w_out(y_a, dout_a, "a_grad_w_out").reshape(N_CHIPS, shard_rows, d)
    e_ao = exchange_start(gwa_out, "exchange_a_out_start")
    dproj_a, dws, dbs, dlg, dlbias = sgu_bwd(
        proj_a, dy_a, a_ln_gain + e_ao[4][0, 0], a_ln_bias, a_w_s[0], bs_col, "a_sgu_bwd")
    e_ai = exchange_start(grad_w_in(h_a, dproj_a, N_CHIPS, False, "a_grad_w_in"), "exchange_a_in_start")
    dx0, dshift0, dscale0, dng0 = inproj_bwd(
        dproj_a, wa_in, x0, dx1, mod0, norm_gain[0:1] + e_ai[4][0, 0], seq, False, "a_inproj_bwd")
    grad_x = dx0.reshape(nb, seq, d)

    dmod = jnp.concatenate([dshift0, dscale0, dgate0, dshift1, dscale1, dgate1], axis=2)
    n_dmod = dmod.size
    small_g = [jnp.concatenate([dng0, dng1], axis=0), dlg, dlbias, dws, dbs, dlb, dfg, dgn]
    packed_g = _pack([dmod] + small_g + [loss_part])
    rows = packed_g.shape[0]
    s_small = gather_all_start(
        lax.dynamic_update_slice(jnp.zeros((N_DEV, rows, LANES), F32), packed_g[None], (dev, 0, 0)),
        "gather_small_start")

    def finish(group, after):
        mine = []
        for ex, _, _, _, nm in group:
            parts_thru, land = exchange_wait(ex[0], ex[1], ex[2], ex[3], after, "exchange_" + nm + "_wait")
            mine.append(sum_parts(parts_thru, land, chip, "sum_" + nm))
            after = mine[-1]
        theirs = swap_sibling(mine, "swap_" + group[0][4])
        return [[r.reshape(w.shape) for r in adamw_pair(pa, pb, w[0], m[0], v[0], "adamw_" + nm)]
                for pa, pb, (_, w, m, v, nm) in zip(mine, theirs, group)]

    (gb_out, db_out, mb_out, vb_out), (gb_in, db_in, mb_in, vb_in), (ga_out, da_out, ma_out, va_out) = finish(
        [(e_bo, b_w_out, m_b_w_out, v_b_w_out, "b_out"), (e_bi, b_w_in, m_b_w_in, v_b_w_in, "b_in"),
         (e_ao, a_w_out, m_a_w_out, v_a_w_out, "a_out")], s_small[3])
    ((ga_in, da_in, ma_in, va_in),) = finish([(e_ai, a_w_in, m_a_w_in, v_a_w_in, "a_in")], ga_out)

    small_w = [norm_gain, a_ln_gain, a_ln_bias, a_w_s, a_b_s, b_lower_bounds, final_gain, b_gn_gain]
    small_m = [m_norm_gain, m_a_ln_gain, m_a_ln_bias, m_a_w_s, m_a_b_s, m_b_lower_bounds, m_final_gain, m_b_gn_gain]
    small_v = [v_norm_gain, v_a_ln_gain, v_a_ln_bias, v_a_w_s, v_a_b_s, v_b_lower_bounds, v_final_gain, v_b_gn_gain]
    rows_of = lambda a: a.reshape(-1, a.shape[-1])
    gathered = gather_all_wait(s_small[0], s_small[1], s_small[2], ga_in, "gather_small_wait")
    tail, small_res = small_update(
        gathered, n_dmod // LANES, [rows_of(a) for a in small_w], [rows_of(a) for a in small_m],
        [rows_of(a) for a in small_v], "small_update")
    loss = tail[0, 0]
    sg, sd, sm, sv = [[small_res[p][kind].reshape(w.shape) for p, w in enumerate(small_w)] for kind in range(4)]

    dmod_all = gathered[:, :n_dmod // LANES].reshape(N_DEV * nb, n_l, 3 * d)
    dmod_cols = lax.dynamic_slice_in_dim(dmod_all, chip * ada_cols, ada_cols, axis=2)
    dmod_cols = jnp.transpose(dmod_cols, (1, 0, 2))
    g_wada, d_wada, m_wada, v_wada = ada_bwd(c_all, dmod_cols, w_ada, m_w_ada, v_w_ada, "ada_bwd")
    flat = lambda a: a.reshape(1, -1)
    g_bada, d_bada, m_bada, v_bada = [
        r.reshape(b_ada.shape) for r in
        bias_update(dmod_all.reshape(N_DEV * nb, n_l * 3 * d), flat(b_ada), flat(m_b_ada), flat(v_b_ada), "bias_update")]

    def order(ng, wada, bada, ain, sm_rest, aout, bin_, bout):
        lg, lbi, ws_, bs_, lbd, fg_, gn_ = sm_rest
        return [ng, wada, bada, ain, lg, lbi, ws_, bs_, aout, bin_, lbd, gn_, bout, fg_]

    grads = order(sg[0], g_wada, g_bada, ga_in, sg[1:8], ga_out, gb_in, gb_out)
    deltas = order(sd[0], d_wada, d_bada, da_in, sd[1:8], da_out, db_in, db_out)
    new_m = order(sm[0], m_wada, m_bada, ma_in, sm[1:8], ma_out, mb_in, mb_out)
    new_v = order(sv[0], v_wada, v_bada, va_in, sv[1:8], va_out, vb_in, vb_out)
    return (loss, grad_x, *grads, *deltas, *new_m, *new_v)
```

```python
import jax
import jax.numpy as jnp
from jax import lax
from jax.experimental import pallas as pl
from jax.experimental.pallas import tpu as pltpu

F32 = jnp.float32
BF16 = jnp.bfloat16
EPS = 1e-6
CHUNK = 64
SG_BLOCK = 128
SG_GROUPS = 8
SG_STEP_BLOCKS = 4
HEAD_DIM = 128
HG_WIDE = 8
HG_ROWS = 256
N_CHIPS = 4
N_DEV = 8
LANES = 128
ADAM_LR = 0.001
ADAM_B1 = 0.9
ADAM_B2 = 0.999
ADAM_EPS = 1e-08
ADAM_WD = 0.01
ADAM_STEP = 10
GELU_C0 = 0.7978845608028654
GELU_C1 = 0.044715
MESH = pl.DeviceIdType.MESH
VMEM_LIMIT = 56 * 1024 * 1024


ROW_TILE = 1024


def _col_tile(n):
    return next(t for t in (1024, 768, 512, 256) if n % t == 0)


def _call(body, **kw):
    return pl.pallas_call(body, **kw)


def _params(**kw):
    return pltpu.CompilerParams(vmem_limit_bytes=VMEM_LIMIT, **kw)


def _sigmoid(x):
    return 0.5 * jnp.tanh(0.5 * x) + 0.5


def _sigmoid_small(x):
    return 1.0 / (1.0 + jnp.exp(-x))


def _silu_and_grad(x):
    s = _sigmoid(x)
    return x * s, s * (1.0 + x * (1.0 - s))


def _gelu(x):
    return 0.5 * x * (1.0 + jnp.tanh(GELU_C0 * (x + GELU_C1 * x * x * x)))


def _gelu_and_grad(x):
    t = jnp.tanh(GELU_C0 * (x + GELU_C1 * x * x * x))
    g = 0.5 * x * (1.0 + t)
    dg = 0.5 * (1.0 + t) + 0.5 * x * (1.0 - t * t) * (GELU_C0 * (1.0 + 3.0 * GELU_C1 * x * x))
    return g, dg


def _dot(a, b, dims, precision=None):
    return lax.dot_general(a, b, (dims, ((), ())), precision=precision, preferred_element_type=F32)


NN = ((1,), (0,))
NT = ((1,), (1,))
TN = ((0,), (0,))


def _adamw(w, g, m, v):
    m = ADAM_B1 * m + (1.0 - ADAM_B1) * g
    v = ADAM_B2 * v + (1.0 - ADAM_B2) * (g * g)
    m_hat = m / (1.0 - ADAM_B1 ** ADAM_STEP)
    v_hat = v / (1.0 - ADAM_B2 ** ADAM_STEP)
    delta = -ADAM_LR * (m_hat / (jnp.sqrt(v_hat) + ADAM_EPS) + ADAM_WD * w)
    return delta, m, v


def _chunk_mask():
    r = lax.broadcasted_iota(jnp.int32, (SG_BLOCK, SG_BLOCK), 0)
    c = lax.broadcasted_iota(jnp.int32, (SG_BLOCK, SG_BLOCK), 1)
    return (c // CHUNK) <= (r // CHUNK)


def _place():
    return lax.axis_index("x"), lax.axis_index("y"), lax.axis_index("c")


def _other_chips(x, y):
    return [(1 - x, y), (x, 1 - y), (1 - x, 1 - y)]


def allgather_small(v, name):
    m_per, n = v.shape

    def body(x_ref, out_ref, send_sems, recv_sems, local_sem):
        x, y, c = _place()
        me, sibling = (x, y, c), (x, y, 1 - c)
        chips = _other_chips(x, y)

        def rows(px, py, pc):
            return out_ref.at[pl.ds((4 * px + 2 * py + pc) * m_per, m_per), :]

        def copy(k, block, to, src=None):
            return pltpu.make_async_remote_copy(
                src_ref=rows(*block) if src is None else src, dst_ref=rows(*block),
                send_sem=send_sems.at[k], recv_sem=recv_sems.at[k], device_id=to, device_id_type=MESH)

        mine = pltpu.make_async_copy(x_ref, rows(*me), local_sem)
        mine.start()
        first = [copy(0, me, sibling, src=x_ref)]
        first += [copy(1 + j, me, (*chip, c), src=x_ref) for j, chip in enumerate(chips)]
        for cp in first:
            cp.start()
        passed = [copy(4 + j, (*chip, c), sibling) for j, chip in enumerate(chips)]
        for j, chip in enumerate(chips):
            copy(1 + j, (*chip, c), me).wait_recv()
            passed[j].start()
        copy(0, sibling, me).wait_recv()
        for j, chip in enumerate(chips):
            copy(4 + j, (*chip, 1 - c), me).wait_recv()
        for cp in first + passed:
            cp.wait_send()
        mine.wait()

    return _call(
        body, name=name,
        out_shape=jax.ShapeDtypeStruct((N_DEV * m_per, n), v.dtype),
        in_specs=[pl.BlockSpec(memory_space=pltpu.VMEM)],
        out_specs=pl.BlockSpec(memory_space=pltpu.VMEM),
        scratch_shapes=[pltpu.SemaphoreType.DMA((7,)), pltpu.SemaphoreType.DMA((7,)), pltpu.SemaphoreType.DMA],
    )(v)


def _hbm_spec():
    return pl.BlockSpec(memory_space=pltpu.HBM)


def _sem_spec():
    return pl.BlockSpec(memory_space=pltpu.SEMAPHORE)


def _split_params():
    return pltpu.CompilerParams(has_side_effects=pltpu.SideEffectType.DATAFLOW_SIDE_EFFECTING)


def _hbm(a):
    return pltpu.with_memory_space_constraint(a, pltpu.HBM)


def _half_copy(land_ref, rows, chip_idx, core_half, send_sem, recv_sem, to):
    half = land_ref.at[chip_idx, pl.ds(core_half * (rows // 2), rows // 2), :]
    return pltpu.make_async_remote_copy(
        src_ref=half, dst_ref=half, send_sem=send_sem, recv_sem=recv_sem, device_id=to, device_id_type=MESH)


def halves_start(lands, after, name):
    n = len(lands)

    def body(*refs):
        land_refs, send_sems, recv_sems, token = refs[:n], refs[n + 1], refs[n + 2], refs[-1]
        x, y, c = _place()
        for w in range(n):
            for j, (px, py) in enumerate(_other_chips(x, y)):
                _half_copy(land_refs[w], lands[w].shape[1], 2 * x + y, c,
                           send_sems.at[3 * w + j], recv_sems.at[3 * w + j], (px, py, c)).start()
        token[...] = jnp.zeros_like(token)

    res = _call(
        body, name=name,
        out_shape=(pltpu.SemaphoreType.DMA((3 * n,)), pltpu.SemaphoreType.DMA((3 * n,)),
                   *[pltpu.HBM(a.shape, a.dtype) for a in lands], jax.ShapeDtypeStruct((8, LANES), F32)),
        in_specs=(*[_hbm_spec()] * n, pl.BlockSpec(memory_space=pl.ANY)),
        out_specs=(_sem_spec(), _sem_spec(), *[_hbm_spec()] * n, pl.BlockSpec(memory_space=pltpu.VMEM)),
        input_output_aliases={w: 2 + w for w in range(n)}, compiler_params=_split_params(),
    )(*[_hbm(a) for a in lands], after)
    return res[0], res[1], list(res[2:2 + n]), res[2 + n]


def halves_wait(send_sems, recv_sems, lands, after, name):
    n = len(lands)

    def body(*refs):
        land_refs, send_sems, recv_sems = refs[:n], refs[n], refs[n + 1]
        x, y, c = _place()
        for w in range(n):
            for j, (px, py) in enumerate(_other_chips(x, y)):
                cp = _half_copy(land_refs[w], lands[w].shape[1], 2 * px + py, c,
                                send_sems.at[3 * w + j], recv_sems.at[3 * w + j], (px, py, c))
                cp.wait_send()
                cp.wait_recv()

    return _call(
        body, name=name,
        out_shape=tuple(pltpu.HBM(a.shape, a.dtype) for a in lands),
        in_specs=(*[_hbm_spec()] * n, _sem_spec(), _sem_spec(), pl.BlockSpec(memory_space=pl.ANY)),
        out_specs=tuple(_hbm_spec() for _ in lands), input_output_aliases={w: w for w in range(n)},
        compiler_params=_split_params(),
    )(*lands, send_sems, recv_sems, after)


def pass_halves(lands, name):
    n = len(lands)

    def body(*refs):
        land_refs, send_sems, recv_sems = refs[n:2 * n], refs[2 * n], refs[2 * n + 1]
        x, y, c = _place()
        sent = []
        for w in range(n):
            for j, (px, py) in enumerate(_other_chips(x, y)):
                cp = _half_copy(land_refs[w], lands[w].shape[1], 2 * px + py, c,
                                send_sems.at[3 * w + j], recv_sems.at[3 * w + j], (x, y, 1 - c))
                cp.start()
                sent.append(cp)
        for w in range(n):
            for j, (px, py) in enumerate(_other_chips(x, y)):
                _half_copy(land_refs[w], lands[w].shape[1], 2 * px + py, 1 - c,
                           send_sems.at[3 * w + j], recv_sems.at[3 * w + j], (x, y, 1 - c)).wait_recv()
        for cp in sent:
            cp.wait_send()

    return _call(
        body, name=name,
        out_shape=[jax.ShapeDtypeStruct(a.shape, a.dtype) for a in lands],
        in_specs=[_hbm_spec()] * n, out_specs=[_hbm_spec()] * n,
        input_output_aliases={w: w for w in range(n)},
        scratch_shapes=[pltpu.SemaphoreType.DMA((3 * n,)), pltpu.SemaphoreType.DMA((3 * n,))],
    )(*lands)


def gather_start(land, after, name):
    def body(land_ref, after_ref, send_sems, recv_sems, land_thru, token):
        del after_ref, land_thru
        x, y, c = _place()
        for j, (px, py) in enumerate(_other_chips(x, y)):
            pltpu.make_async_remote_copy(
                src_ref=land_ref.at[2 * x + y], dst_ref=land_ref.at[2 * x + y],
                send_sem=send_sems.at[j], recv_sem=recv_sems.at[j], device_id=(px, py, c),
                device_id_type=MESH).start()
        token[...] = jnp.zeros_like(token)

    return _call(
        body, name=name,
        out_shape=(pltpu.SemaphoreType.DMA((3,)), pltpu.SemaphoreType.DMA((3,)),
                   pltpu.HBM(land.shape, land.dtype), jax.ShapeDtypeStruct((8, LANES), F32)),
        in_specs=(_hbm_spec(), pl.BlockSpec(memory_space=pl.ANY)),
        out_specs=(_sem_spec(), _sem_spec(), _hbm_spec(), pl.BlockSpec(memory_space=pltpu.VMEM)),
        input_output_aliases={0: 2}, compiler_params=_split_params(),
    )(_hbm(land), after)


def gather_wait(send_sems, recv_sems, land, after, name):
    def body(land_ref, send_sems, recv_sems, after_ref, land_out):
        del after_ref, land_out
        x, y, c = _place()
        for j, (px, py) in enumerate(_other_chips(x, y)):
            cp = pltpu.make_async_remote_copy(
                src_ref=land_ref.at[2 * x + y], dst_ref=land_ref.at[2 * px + py],
                send_sem=send_sems.at[j], recv_sem=recv_sems.at[j], device_id=(px, py, c), device_id_type=MESH)
            cp.wait_send()
            cp.wait_recv()

    return _call(
        body, name=name,
        out_shape=pltpu.HBM(land.shape, land.dtype),
        in_specs=(_hbm_spec(), _sem_spec(), _sem_spec(), pl.BlockSpec(memory_space=pl.ANY)),
        out_specs=_hbm_spec(), input_output_aliases={0: 0}, compiler_params=_split_params(),
    )(land, send_sems, recv_sems, after)


def _flips():
    return [(fx, fy, fc) for fx in (0, 1) for fy in (0, 1) for fc in (0, 1) if (fx, fy, fc) != (0, 0, 0)]


def _flipped(x, y, c, flip):
    fx, fy, fc = flip
    return (1 - x if fx else x, 1 - y if fy else y, 1 - c if fc else c)


def gather_all_start(land, name):
    def body(land_ref, send_sems, recv_sems, land_thru, token):
        del land_thru
        x, y, c = _place()
        for k, flip in enumerate(_flips()):
            pltpu.make_async_remote_copy(
                src_ref=land_ref.at[4 * x + 2 * y + c], dst_ref=land_ref.at[4 * x + 2 * y + c],
                send_sem=send_sems.at[k], recv_sem=recv_sems.at[k], device_id=_flipped(x, y, c, flip),
                device_id_type=MESH).start()
        token[...] = jnp.zeros_like(token)

    return _call(
        body, name=name,
        out_shape=(pltpu.SemaphoreType.DMA((7,)), pltpu.SemaphoreType.DMA((7,)),
                   pltpu.HBM(land.shape, land.dtype), jax.ShapeDtypeStruct((8, LANES), F32)),
        in_specs=(_hbm_spec(),),
        out_specs=(_sem_spec(), _sem_spec(), _hbm_spec(), pl.BlockSpec(memory_space=pltpu.VMEM)),
        input_output_aliases={0: 2}, compiler_params=_split_params(),
    )(_hbm(land))


def gather_all_wait(send_sems, recv_sems, land, after, name):
    def body(land_ref, send_sems, recv_sems, after_ref, land_out):
        del after_ref, land_out
        x, y, c = _place()
        for k, flip in enumerate(_flips()):
            px, py, pc = _flipped(x, y, c, flip)
            cp = pltpu.make_async_remote_copy(
                src_ref=land_ref.at[4 * x + 2 * y + c], dst_ref=land_ref.at[4 * px + 2 * py + pc],
                send_sem=send_sems.at[k], recv_sem=recv_sems.at[k], device_id=(px, py, pc), device_id_type=MESH)
            cp.wait_send()
            cp.wait_recv()

    return _call(
        body, name=name,
        out_shape=pltpu.HBM(land.shape, land.dtype),
        in_specs=(_hbm_spec(), _sem_spec(), _sem_spec(), pl.BlockSpec(memory_space=pl.ANY)),
        out_specs=_hbm_spec(), input_output_aliases={0: 0}, compiler_params=_split_params(),
    )(land, send_sems, recv_sems, after)


def exchange_start(parts, name):
    _, r, c_ = parts.shape

    def body(parts_ref, land_ref, send_sems, recv_sems, parts_thru, land_thru, token):
        del parts_thru, land_thru
        x, y, c = _place()
        for j, (px, py) in enumerate(_other_chips(x, y)):
            pltpu.make_async_remote_copy(
                src_ref=parts_ref.at[2 * px + py], dst_ref=land_ref.at[j],
                send_sem=send_sems.at[j], recv_sem=recv_sems.at[j], device_id=(px, py, c),
                device_id_type=MESH).start()
        token[...] = jnp.zeros_like(token)

    return _call(
        body, name=name,
        out_shape=(pltpu.SemaphoreType.DMA((3,)), pltpu.SemaphoreType.DMA((3,)),
                   pltpu.HBM(parts.shape, parts.dtype), pltpu.HBM((3, r, c_), parts.dtype),
                   jax.ShapeDtypeStruct((8, LANES), F32)),
        in_specs=(_hbm_spec(), _hbm_spec()),
        out_specs=(_sem_spec(), _sem_spec(), _hbm_spec(), _hbm_spec(), pl.BlockSpec(memory_space=pltpu.VMEM)),
        input_output_aliases={0: 2, 1: 3}, compiler_params=_split_params(),
    )(_hbm(parts), _hbm(lax.empty((3, r, c_), parts.dtype)))


def exchange_wait(send_sems, recv_sems, parts, land, after, name):
    def body(parts_ref, land_ref, send_sems, recv_sems, after_ref, parts_out, land_out):
        del after_ref, parts_out, land_out
        x, y, c = _place()
        for j, (px, py) in enumerate(_other_chips(x, y)):
            cp = pltpu.make_async_remote_copy(
                src_ref=parts_ref.at[2 * px + py], dst_ref=land_ref.at[j],
                send_sem=send_sems.at[j], recv_sem=recv_sems.at[j], device_id=(px, py, c), device_id_type=MESH)
            cp.wait_send()
            cp.wait_recv()

    return _call(
        body, name=name,
        out_shape=(pltpu.HBM(parts.shape, parts.dtype), pltpu.HBM(land.shape, land.dtype)),
        in_specs=(_hbm_spec(), _hbm_spec(), _sem_spec(), _sem_spec(), pl.BlockSpec(memory_space=pl.ANY)),
        out_specs=(_hbm_spec(), _hbm_spec()), input_output_aliases={0: 0, 1: 1},
        compiler_params=_split_params(),
    )(parts, land, send_sems, recv_sems, after)


def cast_into_slot(w, chip, after, name):
    r, c = w.shape
    tr = min(256, r)

    def body(s_ref, w_ref, after_ref, o_ref):
        del s_ref, after_ref
        o_ref[...] = w_ref[...].astype(BF16)

    return _call(
        body, name=name,
        grid_spec=pltpu.PrefetchScalarGridSpec(
            num_scalar_prefetch=1, grid=(r // tr,),
            in_specs=[pl.BlockSpec((tr, c), lambda i, s: (i, 0)), pl.BlockSpec(memory_space=pl.ANY)],
            out_specs=pl.BlockSpec((None, tr, c), lambda i, s: (s[0], i, 0))),
        out_shape=jax.ShapeDtypeStruct((N_CHIPS, r, c), BF16),
        compiler_params=_params(),
    )(chip.reshape(1).astype(jnp.int32), w, after)


def sum_parts(parts, land, chip, name):
    _, r, c = parts.shape
    tr = min(256, r)

    def body(s_ref, p_ref, l_ref, o_ref):
        del s_ref
        acc = p_ref[...].astype(F32) + l_ref[0].astype(F32)
        acc = acc + l_ref[1].astype(F32)
        o_ref[...] = (acc + l_ref[2].astype(F32)).astype(BF16)

    return _call(
        body, name=name,
        grid_spec=pltpu.PrefetchScalarGridSpec(
            num_scalar_prefetch=1, grid=(r // tr,),
            in_specs=[pl.BlockSpec((None, tr, c), lambda i, s: (s[0], i, 0)),
                      pl.BlockSpec((3, tr, c), lambda i, s: (0, i, 0))],
            out_specs=pl.BlockSpec((tr, c), lambda i, s: (i, 0))),
        out_shape=jax.ShapeDtypeStruct((r, c), BF16),
        compiler_params=_params(),
    )(chip.reshape(1).astype(jnp.int32), parts, land)


def swap_sibling(arrs, name):
    n = len(arrs)

    def body(*refs):
        ins, outs = refs[:n], refs[n:2 * n]
        send_sems, recv_sems = refs[2 * n:]
        x, y, c = _place()
        cps = []
        for w in range(n):
            cp = pltpu.make_async_remote_copy(
                src_ref=ins[w], dst_ref=outs[w], send_sem=send_sems.at[w], recv_sem=recv_sems.at[w],
                device_id=(x, y, 1 - c), device_id_type=MESH)
            cp.start()
            cps.append(cp)
        for cp in cps:
            cp.wait_recv()
        for cp in cps:
            cp.wait_send()

    return _call(
        body, name=name,
        out_shape=[jax.ShapeDtypeStruct(a.shape, a.dtype) for a in arrs],
        in_specs=[_hbm_spec()] * n, out_specs=[_hbm_spec()] * n,
        scratch_shapes=[pltpu.SemaphoreType.DMA((n,)), pltpu.SemaphoreType.DMA((n,))],
    )(*arrs)


def adamw_pair(pa, pb, w, m, v, name):
    r, c = w.shape
    tr = min(256, r)

    def body(pa_ref, pb_ref, w_ref, m_ref, v_ref, g_ref, d_ref, nm_ref, nv_ref):
        g = pa_ref[...].astype(F32) + pb_ref[...].astype(F32)
        d, nm, nv = _adamw(w_ref[...], g, m_ref[...], v_ref[...])
        g_ref[...] = g
        d_ref[...] = d
        nm_ref[...] = nm
        nv_ref[...] = nv

    spec = pl.BlockSpec((tr, c), lambda i: (i, 0))
    return _call(
        body, name=name, grid=(r // tr,),
        out_shape=[jax.ShapeDtypeStruct((r, c), F32)] * 4,
        in_specs=[spec] * 5, out_specs=[spec] * 4,
        compiler_params=_params(),
    )(pa, pb, w, m, v)


def small_update(gathered, first_row, ws, ms, vs, name):
    n_w = len(ws)
    total_rows = gathered.shape[1]

    def body(*refs):
        g_ref = refs[0]
        w_refs, m_refs, v_refs = refs[1:1 + n_w], refs[1 + n_w:1 + 2 * n_w], refs[1 + 2 * n_w:1 + 3 * n_w]
        tail_ref = refs[1 + 3 * n_w]
        outs = refs[2 + 3 * n_w:2 + 7 * n_w]
        sum_ref = refs[2 + 7 * n_w]
        acc = g_ref[0]
        for k in range(1, N_DEV):
            acc = acc + g_ref[k]
        sum_ref[...] = acc
        row = first_row
        for p in range(n_w):
            a, b = ws[p].shape
            per = b // LANES
            g_out, d_out, m_out, v_out = outs[4 * p:4 * p + 4]
            if per == 1:
                g_out[...] = sum_ref[row:row + a, :]
            else:
                for i in range(a):
                    for jc in range(per):
                        g_out[i:i + 1, jc * LANES:(jc + 1) * LANES] = sum_ref[row + i * per + jc:row + i * per + jc + 1, :]
            row += a * per
            dl, nm, nv = _adamw(w_refs[p][...], g_out[...], m_refs[p][...], v_refs[p][...])
            d_out[...] = dl
            m_out[...] = nm
            v_out[...] = nv
        tail_ref[...] = sum_ref[row:row + 1, :]

    out_shape = [jax.ShapeDtypeStruct((1, LANES), F32)]
    for w in ws:
        out_shape += [jax.ShapeDtypeStruct(w.shape, F32)] * 4
    res = _call(
        body, name=name, out_shape=out_shape,
        scratch_shapes=[pltpu.VMEM((total_rows, LANES), F32)],
        compiler_params=_params(),
    )(gathered, *ws, *ms, *vs)
    return res[0], [res[1 + 4 * p:5 + 4 * p] for p in range(n_w)]


def ada_fwd(c_all, w_ada, b_cols, after, name):
    n_l, d, cols = w_ada.shape
    nb = c_all.shape[0]
    tn = cols

    def body(c_ref, w_ref, b_ref, after_ref, o_ref):
        del after_ref
        cv = c_ref[...]
        ca = (cv * _sigmoid(cv)).astype(BF16)
        o_ref[...] = _dot(ca, w_ref[...].astype(BF16), NN) + b_ref[...]

    return _call(
        body, name=name, grid=(n_l, cols // tn),
        out_shape=jax.ShapeDtypeStruct((n_l, nb, cols), F32),
        in_specs=[pl.BlockSpec((nb, d), lambda l, j: (0, 0)),
                  pl.BlockSpec((None, d, tn), lambda l, j: (l, 0, j)),
                  pl.BlockSpec((None, 1, tn), lambda l, j: (l, 0, j)),
                  pl.BlockSpec(memory_space=pl.ANY)],
        out_specs=pl.BlockSpec((None, nb, tn), lambda l, j: (l, 0, j)),
        compiler_params=_params(),
    )(c_all, w_ada, b_cols, after)


def ada_bwd(c_all, dmod_cols, w, m, v, name):
    n_l, d, cols = w.shape
    nb = c_all.shape[0]
    tn = cols

    def body(c_ref, dm_ref, w_ref, m_ref, v_ref, g_ref, d_ref, nm_ref, nv_ref):
        cv = c_ref[...]
        ca = (cv * _sigmoid(cv)).astype(BF16)
        g = _dot(ca, dm_ref[...].astype(BF16), TN)
        dl, nm, nv = _adamw(w_ref[...], g, m_ref[...], v_ref[...])
        g_ref[...] = g
        d_ref[...] = dl
        nm_ref[...] = nm
        nv_ref[...] = nv

    wspec = pl.BlockSpec((None, d, tn), lambda l, j: (l, 0, j))
    return _call(
        body, name=name, grid=(n_l, cols // tn),
        out_shape=[jax.ShapeDtypeStruct((n_l, d, cols), F32)] * 4,
        in_specs=[pl.BlockSpec((nb, d), lambda l, j: (0, 0)),
                  pl.BlockSpec((None, nb, tn), lambda l, j: (l, 0, j)),
                  wspec, wspec, wspec],
        out_specs=[wspec] * 4,
        compiler_params=_params(),
    )(c_all, dmod_cols, w, m, v)


def bias_update(dmod_all, w, m, v, name):
    def body(dm_ref, w_ref, m_ref, v_ref, g_ref, d_ref, nm_ref, nv_ref):
        g = jnp.sum(dm_ref[...], axis=0, keepdims=True)
        dl, nm, nv = _adamw(w_ref[...], g, m_ref[...], v_ref[...])
        g_ref[...] = g
        d_ref[...] = dl
        nm_ref[...] = nm
        nv_ref[...] = nv

    return _call(
        body, name=name,
        out_shape=[jax.ShapeDtypeStruct(w.shape, F32)] * 4,
        compiler_params=_params(),
    )(dmod_all, w, m, v)


def inproj_fwd(x, mod, ng, wg, seq, sectioned, name):
    m_rows, d = x.shape
    nsh, _, ns = wg.shape
    n = nsh * ns
    tm, tn = min(2 * ROW_TILE, seq), _col_tile(ns)
    per = ns // tn

    def body(x_ref, mod_ref, ng_ref, w_ref, proj_ref, h_ref):
        @pl.when(pl.program_id(1) == 0)
        def _():
            xv = x_ref[...]
            r = lax.rsqrt(jnp.mean(xv * xv, axis=-1, keepdims=True) + EPS)
            md = mod_ref[0]
            h = (xv * r * ng_ref[...]) * (1.0 + md[:, d:2 * d]) + md[:, :d]
            h_ref[...] = h.astype(BF16)
        proj_ref[...] = _dot(h_ref[...], w_ref[...], NN)

    if sectioned:
        proj_shape = (nsh, m_rows, ns)
        proj_spec = pl.BlockSpec((None, tm, tn), lambda i, j: (j // per, i, j % per))
    else:
        proj_shape = (m_rows, n)
        proj_spec = pl.BlockSpec((tm, tn), lambda i, j: (i, j))
    return _call(
        body, name=name, grid=(m_rows // tm, n // tn),
        out_shape=[jax.ShapeDtypeStruct(proj_shape, F32), jax.ShapeDtypeStruct((m_rows, d), BF16)],
        in_specs=[pl.BlockSpec((tm, d), lambda i, j: (i, 0)),
                  pl.BlockSpec((1, 1, 3 * d), lambda i, j: ((i * tm) // seq, 0, 0)),
                  pl.BlockSpec((1, d), lambda i, j: (0, 0)),
                  pl.BlockSpec((None, d, tn), lambda i, j: (j // per, 0, j % per))],
        out_specs=[proj_spec, pl.BlockSpec((tm, d), lambda i, j: (i, 0))],
        compiler_params=_params(),
    )(x, mod, ng, wg)


def outproj_fwd(y, w, x, mod, seq, name):
    m_rows, di = y.shape
    d = w.shape[1]
    tm = min(ROW_TILE, seq)

    def body(y_ref, w_ref, x_ref, mod_ref, xn_ref, out_ref):
        acc = _dot(y_ref[...], w_ref[...], NN)
        out_ref[...] = acc.astype(BF16)
        xn_ref[...] = x_ref[...] + mod_ref[0][:, 2 * d:] * acc

    row = pl.BlockSpec((tm, d), lambda i: (i, 0))
    return _call(
        body, name=name, grid=(m_rows // tm,),
        out_shape=[jax.ShapeDtypeStruct((m_rows, d), F32), jax.ShapeDtypeStruct((m_rows, d), BF16)],
        in_specs=[pl.BlockSpec((tm, di), lambda i: (i, 0)),
                  pl.BlockSpec((di, d), lambda i: (0, 0)),
                  row,
                  pl.BlockSpec((1, 1, 3 * d), lambda i: ((i * tm) // seq, 0, 0))],
        out_specs=[row, row],
        compiler_params=_params(),
    )(y, w, x, mod)


def outproj_bwd(dxo, out, mod, w, seq, name):
    m_rows, d = dxo.shape
    di = w.shape[0]
    nb = m_rows // seq
    tm, tn = min(ROW_TILE, seq), di

    def body(dxo_ref, out_ref, mod_ref, w_ref, dy_ref, dout_ref, dgate_ref):
        i = pl.program_id(0)

        @pl.when(pl.program_id(1) == 0)
        def _():
            dx = dxo_ref[...]
            dout_ref[...] = (mod_ref[0][:, 2 * d:] * dx).astype(BF16)
            part = jnp.sum(dx * out_ref[...].astype(F32), axis=0, keepdims=True)

            @pl.when((i * tm) % seq == 0)
            def _():
                dgate_ref[0] = part

            @pl.when((i * tm) % seq != 0)
            def _():
                dgate_ref[0] = dgate_ref[0] + part

        dy_ref[...] = _dot(dout_ref[...], w_ref[...], NT).astype(BF16)

    row = pl.BlockSpec((tm, d), lambda i, j: (i, 0))
    return _call(
        body, name=name, grid=(m_rows // tm, di // tn),
        out_shape=[jax.ShapeDtypeStruct((m_rows, di), BF16), jax.ShapeDtypeStruct((m_rows, d), BF16),
                   jax.ShapeDtypeStruct((nb, 1, d), F32)],
        in_specs=[row, row,
                  pl.BlockSpec((1, 1, 3 * d), lambda i, j: ((i * tm) // seq, 0, 0)),
                  pl.BlockSpec((tn, d), lambda i, j: (j, 0))],
        out_specs=[pl.BlockSpec((tm, tn), lambda i, j: (i, j)), row,
                   pl.BlockSpec((1, 1, d), lambda i, j: ((i * tm) // seq, 0, 0))],
        compiler_params=_params(),
    )(dxo, out, mod, w)


def grad_w_out(y, dout, name):
    m_rows, di = y.shape
    d = dout.shape[1]
    tm, tk = min(ROW_TILE, m_rows), di
    n_m = m_rows // tm

    def body(y_ref, do_ref, o_ref, acc_ref):
        mi = pl.program_id(1)

        @pl.when(mi == 0)
        def _():
            acc_ref[...] = jnp.zeros_like(acc_ref)

        acc_ref[...] += _dot(y_ref[...], do_ref[...], TN)

        @pl.when(mi == n_m - 1)
        def _():
            o_ref[...] = acc_ref[...].astype(BF16)

    return _call(
        body, name=name, grid=(di // tk, n_m),
        out_shape=jax.ShapeDtypeStruct((di, d), BF16),
        in_specs=[pl.BlockSpec((tm, tk), lambda j, mi: (mi, j)),
                  pl.BlockSpec((tm, d), lambda j, mi: (mi, 0))],
        out_specs=pl.BlockSpec((tk, d), lambda j, mi: (j, 0)),
        scratch_shapes=[pltpu.VMEM((tk, d), F32)],
        compiler_params=_params(),
    )(y, dout)


def grad_w_in(h, dproj, nsh, sectioned, name):
    m_rows, d = h.shape
    n = dproj.shape[0] * dproj.shape[2] if sectioned else dproj.shape[1]
    ns = n // nsh
    tm, tn = min(ROW_TILE, m_rows), ns
    per = ns // tn
    n_m = m_rows // tm

    def body(h_ref, dp_ref, o_ref, acc_ref):
        mi = pl.program_id(1)
        @pl.when(mi == 0)
        def _():
            acc_ref[...] = jnp.zeros_like(acc_ref)

        acc_ref[...] += _dot(h_ref[...], dp_ref[...], TN)

        @pl.when(mi == n_m - 1)
        def _():
            o_ref[...] = acc_ref[...].astype(BF16)

    if sectioned:
        dp_spec = pl.BlockSpec((None, tm, tn), lambda j, mi: (j // per, mi, j % per))
    else:
        dp_spec = pl.BlockSpec((tm, tn), lambda j, mi: (mi, j))
    return _call(
        body, name=name, grid=(n // tn, n_m),
        out_shape=jax.ShapeDtypeStruct((nsh, d, ns), BF16),
        in_specs=[pl.BlockSpec((tm, d), lambda j, mi: (mi, 0)), dp_spec],
        out_specs=pl.BlockSpec((None, d, tn), lambda j, mi: (j // per, 0, j % per)),
        scratch_shapes=[pltpu.VMEM((d, tn), F32)],
        compiler_params=_params(),
    )(h, dproj)


def inproj_bwd(dproj, wg, x, dxo, mod, ng, seq, sectioned, name):
    m_rows, d = x.shape
    nsh, _, ns = wg.shape
    n = nsh * ns
    nb = m_rows // seq
    tm, tk = min(ROW_TILE, seq), ns
    per = ns // tk
    n_k = n // tk

    def body(dp_ref, w_ref, x_ref, dxo_ref, mod_ref, ng_ref, dxi_ref, dsh_ref, dsc_ref, dng_ref, acc_ref):
        i, k = pl.program_id(0), pl.program_id(1)
        @pl.when(k == 0)
        def _():
            acc_ref[...] = jnp.zeros_like(acc_ref)

        acc_ref[...] += _dot(dp_ref[...], w_ref[...], NT)

        @pl.when(k == n_k - 1)
        def _():
            dh = acc_ref[...]
            xv = x_ref[...]
            r = lax.rsqrt(jnp.mean(xv * xv, axis=-1, keepdims=True) + EPS)
            xn = xv * r
            md = mod_ref[0]
            gain = ng_ref[...]
            p_shift = jnp.sum(dh, axis=0, keepdims=True)
            p_scale = jnp.sum(dh * (xn * gain), axis=0, keepdims=True)
            drn = dh * (1.0 + md[:, d:2 * d])
            p_ng = jnp.sum(drn * xn, axis=0, keepdims=True)
            dxn = drn * gain
            dx = r * (dxn - xn * jnp.mean(dxn * xn, axis=-1, keepdims=True))
            dxi_ref[...] = dxo_ref[...] + dx

            @pl.when((i * tm) % seq == 0)
            def _():
                dsh_ref[0] = p_shift
                dsc_ref[0] = p_scale

            @pl.when((i * tm) % seq != 0)
            def _():
                dsh_ref[0] = dsh_ref[0] + p_shift
                dsc_ref[0] = dsc_ref[0] + p_scale

            @pl.when(i == 0)
            def _():
                dng_ref[...] = p_ng

            @pl.when(i != 0)
            def _():
                dng_ref[...] = dng_ref[...] + p_ng

    if sectioned:
        dp_spec = pl.BlockSpec((None, tm, tk), lambda i, k: (k // per, i, k % per))
    else:
        dp_spec = pl.BlockSpec((tm, tk), lambda i, k: (i, k))
    row = pl.BlockSpec((tm, d), lambda i, k: (i, 0))
    per_seq = pl.BlockSpec((1, 1, d), lambda i, k: ((i * tm) // seq, 0, 0))
    return _call(
        body, name=name, grid=(m_rows // tm, n_k),
        out_shape=[jax.ShapeDtypeStruct((m_rows, d), F32), jax.ShapeDtypeStruct((nb, 1, d), F32),
                   jax.ShapeDtypeStruct((nb, 1, d), F32), jax.ShapeDtypeStruct((1, d), F32)],
        in_specs=[dp_spec,
                  pl.BlockSpec((None, d, tk), lambda i, k: (k // per, 0, k % per)),
                  row, row,
                  pl.BlockSpec((1, 1, 3 * d), lambda i, k: ((i * tm) // seq, 0, 0)),
                  pl.BlockSpec((1, d), lambda i, k: (0, 0))],
        out_specs=[row, per_seq, per_seq, pl.BlockSpec((1, d), lambda i, k: (0, 0))],
        scratch_shapes=[pltpu.VMEM((tm, d), F32)],
        compiler_params=_params(),
    )(dproj, wg, x, dxo, mod, ng)


def _sgu_stats(proj_ref, vg_ref, di, gd, dgel_ref=None):
    s1 = jnp.zeros((SG_BLOCK, 1), F32)
    for g in range(SG_GROUPS):
        v_pre = proj_ref[:, di + g * gd:di + (g + 1) * gd]
        if dgel_ref is None:
            vg = _gelu(v_pre)
        else:
            vg, dgel_ref[:, g * gd:(g + 1) * gd] = _gelu_and_grad(v_pre)
        vg_ref[:, g * gd:(g + 1) * gd] = vg
        s1 = s1 + jnp.sum(vg, axis=1, keepdims=True)
    mu = s1 / di
    s2 = jnp.zeros((SG_BLOCK, 1), F32)
    for g in range(SG_GROUPS):
        dv = vg_ref[:, g * gd:(g + 1) * gd] - mu
        s2 = s2 + jnp.sum(dv * dv, axis=1, keepdims=True)
    return mu, lax.rsqrt(s2 / di + EPS)


def sgu_fwd(proj, ln_gain, ln_bias, ws, bs, name):
    m_rows, n3 = proj.shape
    di = n3 // 3
    gd = di // SG_GROUPS
    n_blocks = m_rows // SG_BLOCK
    per_step = SG_STEP_BLOCKS if n_blocks % SG_STEP_BLOCKS == 0 else 1

    def body(proj_ref, lg_ref, lb_ref, ws_ref, bs_ref, y_ref, wsm_ref, vg_ref):
        @pl.when(pl.program_id(0) == 0)
        def _():
            mask = _chunk_mask()
            for g in range(SG_GROUPS):
                wsm_ref[g] = jnp.where(mask, ws_ref[g], 0.0).astype(BF16)

        for blk in range(per_step):
            p_ref, o_ref = proj_ref.at[blk], y_ref.at[blk]
            mu, rstd = _sgu_stats(p_ref, vg_ref, di, gd)
            for g in range(SG_GROUPS):
                cs = slice(g * gd, (g + 1) * gd)
                vln = (vg_ref[:, cs] - mu) * rstd * lg_ref[:, cs] + lb_ref[:, cs]
                s = _dot(wsm_ref[g], vln.astype(BF16), NN) + bs_ref[g]
                u = _gelu(p_ref[:, cs])
                gp = p_ref[:, 2 * di + g * gd:2 * di + (g + 1) * gd]
                o_ref[:, cs] = (u * s * (gp * _sigmoid(gp))).astype(BF16)

    full = lambda shape: pl.BlockSpec(shape, lambda i: (0,) * len(shape))
    return _call(
        body, name=name, grid=(n_blocks // per_step,),
        out_shape=jax.ShapeDtypeStruct((n_blocks, SG_BLOCK, di), BF16),
        in_specs=[pl.BlockSpec((per_step, SG_BLOCK, n3), lambda i: (i, 0, 0)),
                  full((1, di)), full((1, di)),
                  full((SG_GROUPS, SG_BLOCK, SG_BLOCK)), full((SG_GROUPS, SG_BLOCK, 1))],
        out_specs=pl.BlockSpec((per_step, SG_BLOCK, di), lambda i: (i, 0, 0)),
        scratch_shapes=[pltpu.VMEM((SG_GROUPS, SG_BLOCK, SG_BLOCK), BF16), pltpu.VMEM((SG_BLOCK, di), F32)],
        compiler_params=_params(),
    )(proj.reshape(n_blocks, SG_BLOCK, n3), ln_gain, ln_bias, ws, bs).reshape(m_rows, di)


def sgu_bwd(proj, dy, ln_gain, ln_bias, ws, bs, name):
    m_rows, n3 = proj.shape
    di = n3 // 3
    gd = di // SG_GROUPS
    n_i = m_rows // SG_BLOCK

    def body(proj_ref, dy_ref, lg_ref, lb_ref, ws_ref, bs_ref,
             dp_ref, dws_ref, dbs_ref, dlg_ref, dlb_ref, wsm_ref, vg_ref, dvh_ref, dgel_ref):
        i = pl.program_id(0)

        def before():
            @pl.when(i == 0)
            def _():
                mask = _chunk_mask()
                for g in range(SG_GROUPS):
                    wsm_ref[g] = jnp.where(mask, ws_ref[g], 0.0).astype(BF16)
                dws_ref[...] = jnp.zeros_like(dws_ref)
                dbs_ref[...] = jnp.zeros_like(dbs_ref)
                dlg_ref[...] = jnp.zeros_like(dlg_ref)
                dlb_ref[...] = jnp.zeros_like(dlb_ref)

        def after():
            @pl.when(i == n_i - 1)
            def _():
                mask = _chunk_mask()
                for g in range(SG_GROUPS):
                    dws_ref[g] = jnp.where(mask, dws_ref[g], 0.0)

        before()
        mu, rstd = _sgu_stats(proj_ref, vg_ref, di, gd, dgel_ref)
        m1 = jnp.zeros((SG_BLOCK, 1), F32)
        m2 = jnp.zeros((SG_BLOCK, 1), F32)
        for g in range(SG_GROUPS):
            cs = slice(g * gd, (g + 1) * gd)
            gs = slice(2 * di + g * gd, 2 * di + (g + 1) * gd)
            gain = lg_ref[:, cs]
            vhat = (vg_ref[:, cs] - mu) * rstd
            vln_b = (vhat * gain + lb_ref[:, cs]).astype(BF16)
            s = _dot(wsm_ref[g], vln_b, NN) + bs_ref[g]
            u, du = _gelu_and_grad(proj_ref[:, cs])
            sg, dsg = _silu_and_grad(proj_ref[:, gs])
            dyv = dy_ref[:, cs].astype(F32)
            dp_ref[:, cs] = (dyv * s * sg * du).astype(BF16)
            dp_ref[:, gs] = (dyv * u * s * dsg).astype(BF16)
            ds = dyv * u * sg
            ds_b = ds.astype(BF16)
            dws_ref[g] = dws_ref[g] + _dot(ds_b, vln_b, NT)
            dbs_ref[g] = dbs_ref[g] + jnp.sum(ds, axis=1, keepdims=True)
            dvln = _dot(wsm_ref[g], ds_b, TN)
            dlg_ref[:, cs] = dlg_ref[:, cs] + jnp.sum(dvln * vhat, axis=0, keepdims=True)
            dlb_ref[:, cs] = dlb_ref[:, cs] + jnp.sum(dvln, axis=0, keepdims=True)
            dvh = dvln * gain
            dvh_ref[:, cs] = dvh
            m1 = m1 + jnp.sum(dvh, axis=1, keepdims=True)
            m2 = m2 + jnp.sum(dvh * vhat, axis=1, keepdims=True)
        m1 = m1 / di
        m2 = m2 / di
        for g in range(SG_GROUPS):
            cs = slice(g * gd, (g + 1) * gd)
            vs = slice(di + g * gd, di + (g + 1) * gd)
            vhat = (vg_ref[:, cs] - mu) * rstd
            dvg = rstd * (dvh_ref[:, cs] - m1 - vhat * m2)
            dp_ref[:, vs] = (dvg * dgel_ref[:, cs]).astype(BF16)

        after()

    full = lambda shape: pl.BlockSpec(shape, lambda i: (0,) * len(shape))
    return _call(
        body, name=name, grid=(n_i,),
        out_shape=[jax.ShapeDtypeStruct((m_rows, n3), BF16),
                   jax.ShapeDtypeStruct((SG_GROUPS, SG_BLOCK, SG_BLOCK), F32),
                   jax.ShapeDtypeStruct((SG_GROUPS, SG_BLOCK, 1), F32),
                   jax.ShapeDtypeStruct((1, di), F32), jax.ShapeDtypeStruct((1, di), F32)],
        in_specs=[pl.BlockSpec((SG_BLOCK, n3), lambda i: (i, 0)),
                  pl.BlockSpec((SG_BLOCK, di), lambda i: (i, 0)),
                  full((1, di)), full((1, di)),
                  full((SG_GROUPS, SG_BLOCK, SG_BLOCK)), full((SG_GROUPS, SG_BLOCK, 1))],
        out_specs=[pl.BlockSpec((SG_BLOCK, n3), lambda i: (i, 0)),
                   full((SG_GROUPS, SG_BLOCK, SG_BLOCK)), full((SG_GROUPS, SG_BLOCK, 1)),
                   full((1, di)), full((1, di))],
        scratch_shapes=[pltpu.VMEM((SG_GROUPS, SG_BLOCK, SG_BLOCK), BF16),
                        pltpu.VMEM((SG_BLOCK, di), F32), pltpu.VMEM((SG_BLOCK, di), F32),
                        pltpu.VMEM((SG_BLOCK, di), F32)],
        compiler_params=_params(),
    )(proj, dy, ln_gain, ln_bias, ws, bs)


def _lower_bound(lbraw):
    mx = jnp.maximum(lbraw[0:1, :], lbraw[1:2, :])
    e0 = jnp.exp(lbraw[0:1, :] - mx)
    e1 = jnp.exp(lbraw[1:2, :] - mx)
    p0 = e0 / (e0 + e1)
    p1 = e1 / (e0 + e1)
    return (p0 + p1) - p0, p0, p1


def _tri(lower):
    r = lax.broadcasted_iota(jnp.int32, (CHUNK, CHUNK), 0)
    c = lax.broadcasted_iota(jnp.int32, (CHUNK, CHUNK), 1)
    return ((r >= c) if lower else (c >= r)).astype(BF16)


def _running_sum(tri, x):
    x1 = x.astype(BF16)
    r1 = x - x1.astype(F32)
    x2 = r1.astype(BF16)
    x3 = (r1 - x2.astype(F32)).astype(BF16)
    return _dot(tri, x1, NN) + _dot(tri, x2, NN) + _dot(tri, x3, NN)


def _row(a, idx):
    r = lax.broadcasted_iota(jnp.int32, a.shape, 0)
    return jnp.sum(jnp.where(r == idx, a, 0.0), axis=0, keepdims=True)


def _hgrn_gates(qp, fp, lb, tri):
    sgm = _sigmoid_small(fp)
    f = lb + (1.0 - lb) * sgm
    k = 1.0 - f
    a = _running_sum(tri, jnp.log(f))
    a_mid = _row(a, CHUNK // 2 - 1)
    a_last = _row(a, CHUNK - 1)
    q, dq = _silu_and_grad(qp)
    e1, e2, e3, e4 = jnp.exp(a - a_mid), jnp.exp(a_mid - a), jnp.exp(a), jnp.exp(a_last - a)
    return dict(sgm=sgm, f=f, k=k, q=q, dq=dq, e1=e1, e2=e2, e3=e3, e4=e4, dec=jnp.exp(a_last),
                q_in=q * e1, k_in=k * e2, q_out=q * e3, k_out=k * e4)


def _causal():
    r = lax.broadcasted_iota(jnp.int32, (CHUNK, CHUNK), 0)
    c = lax.broadcasted_iota(jnp.int32, (CHUNK, CHUNK), 1)
    return r >= c


def hgrn_fwd(proj4, lbraw, gn, seq, name):
    _, m_rows, di = proj4.shape
    nb, nh, nc = m_rows // seq, di // HEAD_DIM, seq // CHUNK
    rows = min(2 * HG_ROWS, seq)
    wide = HG_WIDE * HEAD_DIM
    ns, cpb = seq // rows, rows // CHUNK

    def body(p_ref, lb_ref, gn_ref, y_ref, sts_ref, st_ref):
        @pl.when(pl.program_id(2) == 0)
        def _():
            st_ref[...] = jnp.zeros_like(st_ref)

        tri = _tri(True)
        causal = _causal()
        gain = gn_ref[...]
        lbs = [_lower_bound(lb_ref[:, j * HEAD_DIM:(j + 1) * HEAD_DIM])[0] for j in range(HG_WIDE)]

        units = [(n, j) for n in range(cpb) for j in range(HG_WIDE)]
        rs = lambda n: slice(n * CHUNK, (n + 1) * CHUNK)
        cs = lambda j: slice(j * HEAD_DIM, (j + 1) * HEAD_DIM)
        gates, v_b, sc_b, kv, o_in, o_x = {}, {}, {}, {}, {}, {}
        for n, j in units:
            gates[n, j] = _hgrn_gates(p_ref[0, rs(n), cs(j)], p_ref[1, rs(n), cs(j)], lbs[j], tri)
            v_b[n, j] = p_ref[2, rs(n), cs(j)].astype(BF16)
        for u in units:
            t = gates[u]
            sc_b[u] = jnp.where(causal, _dot(t["q_in"].astype(BF16), t["k_in"].astype(BF16), NT), 0.0).astype(BF16)
            kv[u] = _dot(v_b[u], t["k_out"].astype(BF16), TN)
        for u in units:
            o_in[u] = _dot(sc_b[u], v_b[u], NN)
        for j in range(HG_WIDE):
            st = st_ref[j]
            for n in range(cpb):
                sts_ref[n, :, cs(j)] = st
                o_x[n, j] = _dot(gates[n, j]["q_out"].astype(BF16), st.astype(BF16), NT)
                st = st * gates[n, j]["dec"] + kv[n, j]
            st_ref[j] = st
        for n, j in units:
            o = o_in[n, j] + o_x[n, j]
            r = lax.rsqrt(jnp.mean(o * o, axis=-1, keepdims=True) + EPS)
            gp = p_ref[3, rs(n), cs(j)]
            y_ref[rs(n), cs(j)] = ((o * r * gain) * (gp * _sigmoid(gp))).astype(BF16)

    return _call(
        body, name=name, grid=(nh // HG_WIDE, nb, ns),
        out_shape=[jax.ShapeDtypeStruct((m_rows, di), BF16),
                   jax.ShapeDtypeStruct((nb * nc, HEAD_DIM, di), F32)],
        in_specs=[pl.BlockSpec((4, rows, wide), lambda hg, b, s: (0, b * ns + s, hg)),
                  pl.BlockSpec((2, wide), lambda hg, b, s: (0, hg)),
                  pl.BlockSpec((1, HEAD_DIM), lambda hg, b, s: (0, 0))],
        out_specs=[pl.BlockSpec((rows, wide), lambda hg, b, s: (b * ns + s, hg)),
                   pl.BlockSpec((cpb, HEAD_DIM, wide), lambda hg, b, s: (b * ns + s, 0, hg))],
        scratch_shapes=[pltpu.VMEM((HG_WIDE, HEAD_DIM, HEAD_DIM), F32)],
        compiler_params=_params(),
    )(proj4, lbraw, gn)


def hgrn_bwd(proj4, dy, sts, lbraw, gn, seq, name):
    _, m_rows, di = proj4.shape
    nb, nh, nc = m_rows // seq, di // HEAD_DIM, seq // CHUNK
    rows = min(HG_ROWS, seq)
    wide = HG_WIDE * HEAD_DIM
    ns, cpb = seq // rows, rows // CHUNK
    n_hg = nh // HG_WIDE

    def body(p_ref, dy_ref, sts_ref, lb_ref, gn_ref, dp_ref, dlb_ref, dgn_ref, dst_ref, lbacc_ref, gnacc_ref):
        hg, b, s = pl.program_id(0), pl.program_id(1), pl.program_id(2)
        tri, triu = _tri(True), _tri(False)
        causal = _causal()
        gain = gn_ref[...]
        first = (b == 0) & (s == 0)
        cs = lambda j: slice(j * HEAD_DIM, (j + 1) * HEAD_DIM)

        def before():
            @pl.when((hg == 0) & first)
            def _():
                gnacc_ref[...] = jnp.zeros_like(gnacc_ref)

            @pl.when(first)
            def _():
                lbacc_ref[...] = jnp.zeros_like(lbacc_ref)

            @pl.when(s == 0)
            def _():
                dst_ref[...] = jnp.zeros_like(dst_ref)

        def after():
            @pl.when((b == nb - 1) & (s == ns - 1))
            def _():
                for j in range(HG_WIDE):
                    _, p0, p1 = _lower_bound(lb_ref[:, cs(j)])
                    acc = lbacc_ref[:, cs(j)]
                    dlb_ref[0:1, cs(j)] = -acc * p0 * p1
                    dlb_ref[1:2, cs(j)] = acc * p1 * (1.0 - p1)

            @pl.when((hg == n_hg - 1) & (b == nb - 1) & (s == ns - 1))
            def _():
                tot = gnacc_ref[:, 0:HEAD_DIM]
                for j in range(1, HG_WIDE):
                    tot = tot + gnacc_ref[:, cs(j)]
                dgn_ref[...] = tot

        before()

        units = [(n, j) for n in range(cpb) for j in range(HG_WIDE)]
        rs = lambda n: slice(n * CHUNK, (n + 1) * CHUNK)
        lbs = [_lower_bound(lb_ref[:, cs(j)])[0] for j in range(HG_WIDE)]
        gates, v_b, st_b, sc_b, o, do_b = {}, {}, {}, {}, {}, {}
        dq_out, dsc_b, dv, g_st, dq_in, dk_in, dst_at, dk_out, ddec = {}, {}, {}, {}, {}, {}, {}, {}, {}
        for n, j in units:
            gates[n, j] = _hgrn_gates(p_ref[0, rs(n), cs(j)], p_ref[1, rs(n), cs(j)], lbs[j], tri)
            v_b[n, j] = p_ref[2, rs(n), cs(j)].astype(BF16)
            st_b[n, j] = sts_ref[n, :, cs(j)].astype(BF16)
        for u in units:
            t = gates[u]
            sc_b[u] = jnp.where(causal, _dot(t["q_in"].astype(BF16), t["k_in"].astype(BF16), NT), 0.0).astype(BF16)
        for u in units:
            o[u] = _dot(sc_b[u], v_b[u], NN) + _dot(gates[u]["q_out"].astype(BF16), st_b[u], NT)
        for n, j in units:
            ov = o[n, j]
            r = lax.rsqrt(jnp.mean(ov * ov, axis=-1, keepdims=True) + EPS)
            ohat = ov * r
            sg, dsg = _silu_and_grad(p_ref[3, rs(n), cs(j)])
            dyv = dy_ref[rs(n), cs(j)].astype(F32)
            dp_ref[3, rs(n), cs(j)] = (dyv * (ohat * gain) * dsg).astype(BF16)
            d_on = dyv * sg
            gnacc_ref[:, cs(j)] = gnacc_ref[:, cs(j)] + jnp.sum(d_on * ohat, axis=0, keepdims=True)
            dohat = d_on * gain
            do_b[n, j] = (r * (dohat - ohat * jnp.mean(dohat * ohat, axis=-1, keepdims=True))).astype(BF16)
        for u in units:
            dq_out[u] = _dot(do_b[u], st_b[u], NN)
            dsc_b[u] = jnp.where(causal, _dot(do_b[u], v_b[u], NT), 0.0).astype(BF16)
            dv[u] = _dot(sc_b[u], do_b[u], TN)
            g_st[u] = _dot(do_b[u], gates[u]["q_out"].astype(BF16), TN)
        for u in units:
            dq_in[u] = _dot(dsc_b[u], gates[u]["k_in"].astype(BF16), NN)
            dk_in[u] = _dot(dsc_b[u], gates[u]["q_in"].astype(BF16), TN)
        for j in range(HG_WIDE):
            dst = dst_ref[j]
            for n in reversed(range(cpb)):
                dst_at[n, j] = dst
                dst = dst * gates[n, j]["dec"] + g_st[n, j]
            dst_ref[j] = dst
        for n, j in units:
            dst = dst_at[n, j]
            dst_b = dst.astype(BF16)
            dk_out[n, j] = _dot(v_b[n, j], dst_b, NN)
            dv[n, j] = dv[n, j] + _dot(gates[n, j]["k_out"].astype(BF16), dst_b, NT)
            ddec[n, j] = jnp.sum(dst * sts_ref[n, :, cs(j)], axis=0, keepdims=True)
        for n, j in units:
            t = gates[n, j]
            dp_ref[2, rs(n), cs(j)] = dv[n, j].astype(BF16)
            dq = dq_in[n, j] * t["e1"] + dq_out[n, j] * t["e3"]
            dk = dk_in[n, j] * t["e2"] + dk_out[n, j] * t["e4"]
            w_in = dq_in[n, j] * t["q_in"] - dk_in[n, j] * t["k_in"]
            w_out = dk_out[n, j] * t["k_out"]
            da = w_in + dq_out[n, j] * t["q_out"] - w_out
            da_mid = -jnp.sum(w_in, axis=0, keepdims=True)
            da_last = jnp.sum(w_out, axis=0, keepdims=True) + ddec[n, j] * t["dec"]
            rid = lax.broadcasted_iota(jnp.int32, da.shape, 0)
            da = da + jnp.where(rid == CHUNK // 2 - 1, da_mid, 0.0) + jnp.where(rid == CHUNK - 1, da_last, 0.0)
            dlf = _running_sum(triu, da)
            df = dlf / t["f"] - dk
            sgm = t["sgm"]
            dp_ref[1, rs(n), cs(j)] = (df * (1.0 - lbs[j]) * sgm * (1.0 - sgm)).astype(BF16)
            lbacc_ref[:, cs(j)] = lbacc_ref[:, cs(j)] + jnp.sum(df * (1.0 - sgm), axis=0, keepdims=True)
            dp_ref[0, rs(n), cs(j)] = (dq * t["dq"]).astype(BF16)

        after()

    blk = lambda hg, b, s: b * ns + (ns - 1 - s)
    return _call(
        body, name=name, grid=(n_hg, nb, ns),
        out_shape=[jax.ShapeDtypeStruct((4, m_rows, di), BF16), jax.ShapeDtypeStruct((2, di), F32),
                   jax.ShapeDtypeStruct((1, HEAD_DIM), F32)],
        in_specs=[pl.BlockSpec((4, rows, wide), lambda hg, b, s: (0, blk(hg, b, s), hg)),
                  pl.BlockSpec((rows, wide), lambda hg, b, s: (blk(hg, b, s), hg)),
                  pl.BlockSpec((cpb, HEAD_DIM, wide), lambda hg, b, s: (blk(hg, b, s), 0, hg)),
                  pl.BlockSpec((2, wide), lambda hg, b, s: (0, hg)),
                  pl.BlockSpec((1, HEAD_DIM), lambda hg, b, s: (0, 0))],
        out_specs=[pl.BlockSpec((4, rows, wide), lambda hg, b, s: (0, blk(hg, b, s), hg)),
                   pl.BlockSpec((2, wide), lambda hg, b, s: (0, hg)),
                   pl.BlockSpec((1, HEAD_DIM), lambda hg, b, s: (0, 0))],
        scratch_shapes=[pltpu.VMEM((HG_WIDE, HEAD_DIM, HEAD_DIM), F32), pltpu.VMEM((1, wide), F32),
                        pltpu.VMEM((1, wide), F32)],
        compiler_params=_params(),
    )(proj4, dy, sts, lbraw, gn)


def outproj_loss(y, w, x, mod, fg, target, seq, name):
    m_rows, di = y.shape
    d = w.shape[1]
    tm = min(512, seq)

    def body(y_ref, w_ref, x_ref, mod_ref, fg_ref, t_ref, out_ref, loss_ref, dx_ref, dfg_ref):
        i = pl.program_id(0)
        acc = _dot(y_ref[...], w_ref[...], NN)
        out_ref[...] = acc.astype(BF16)
        xv = x_ref[...] + mod_ref[0][:, 2 * d:] * acc
        gain = fg_ref[...]
        r = lax.rsqrt(jnp.mean(xv * xv, axis=-1, keepdims=True) + EPS)
        xn = xv * r
        e = xn * gain - t_ref[...]
        part = 0.5 * jnp.sum(jnp.mean(e * e, axis=-1, keepdims=True), axis=0, keepdims=True)
        dyv = e / d
        p_fg = jnp.sum(dyv * xn, axis=0, keepdims=True)
        dxn = dyv * gain
        dx_ref[...] = r * (dxn - xn * jnp.mean(dxn * xn, axis=-1, keepdims=True))

        @pl.when(i == 0)
        def _():
            loss_ref[...] = part
            dfg_ref[...] = p_fg

        @pl.when(i != 0)
        def _():
            loss_ref[...] = loss_ref[...] + part
            dfg_ref[...] = dfg_ref[...] + p_fg

    row = pl.BlockSpec((tm, d), lambda i: (i, 0))
    return _call(
        body, name=name, grid=(m_rows // tm,),
        out_shape=[jax.ShapeDtypeStruct((m_rows, d), BF16), jax.ShapeDtypeStruct((1, 1), F32),
                   jax.ShapeDtypeStruct((m_rows, d), F32), jax.ShapeDtypeStruct((1, d), F32)],
        in_specs=[pl.BlockSpec((tm, di), lambda i: (i, 0)),
                  pl.BlockSpec((di, d), lambda i: (0, 0)),
                  row,
                  pl.BlockSpec((1, 1, 3 * d), lambda i: ((i * tm) // seq, 0, 0)),
                  pl.BlockSpec((1, d), lambda i: (0, 0)), row],
        out_specs=[row, pl.BlockSpec((1, 1), lambda i: (0, 0)), row, pl.BlockSpec((1, d), lambda i: (0, 0))],
        compiler_params=_params(),
    )(y, w, x, mod, fg, target)


def _pack(parts):
    flat = jnp.concatenate([p.reshape(-1) for p in parts])
    pad = (-flat.shape[0]) % (8 * LANES)
    return jnp.pad(flat, (0, pad)).reshape(-1, LANES)


def kernel(x, c, norm_gain, w_ada, b_ada, a_w_in, a_ln_gain, a_ln_bias, a_w_s, a_b_s, a_w_out, b_w_in, b_lower_bounds, b_gn_gain, b_w_out, final_gain, loss_target, m_norm_gain, m_w_ada, m_b_ada, m_a_w_in, m_a_ln_gain, m_a_ln_bias, m_a_w_s, m_a_b_s, m_a_w_out, m_b_w_in, m_b_lower_bounds, m_b_gn_gain, m_b_w_out, m_final_gain, v_norm_gain, v_w_ada, v_b_ada, v_a_w_in, v_a_ln_gain, v_a_ln_bias, v_a_w_s, v_a_b_s, v_a_w_out, v_b_w_in, v_b_lower_bounds, v_b_gn_gain, v_b_w_out, v_final_gain):
    nb, seq, d = x.shape
    m_rows = nb * seq
    n_l = w_ada.shape[0]
    ada_cols = w_ada.shape[2]
    px, py, pc = _place()
    chip = 2 * px + py
    dev = 2 * chip + pc

    c_all = allgather_small(c.reshape(-1, LANES), "gather_c").reshape(N_DEV * nb, d)
    s_a = halves_start([cast_into_slot(a_w_in[0], chip, c_all, "cast_a_in")], c_all, "gather_a_in_start")
    land_b_in = cast_into_slot(b_w_in[0], chip, s_a[3], "cast_b_in")
    land_b_out = cast_into_slot(b_w_out[0], chip, land_b_in, "cast_b_out")
    b_cols = lax.dynamic_slice_in_dim(b_ada, chip * ada_cols, ada_cols, axis=1).reshape(n_l, 1, ada_cols)
    mod_cols = ada_fwd(c_all, w_ada, b_cols, land_b_out, "ada_fwd")
    mod_g = allgather_small(mod_cols.reshape(-1, LANES), "gather_mod")
    mod_g = mod_g.reshape(N_CHIPS, 2, n_l, N_DEV * nb, ada_cols)[:, 0]
    mod_all = jnp.transpose(mod_g, (1, 2, 0, 3)).reshape(n_l, N_DEV * nb, 3 * d)
    mod_mine = lax.dynamic_slice_in_dim(mod_all, dev * nb, nb, axis=1)
    mod0 = mod_mine[0].reshape(nb, 1, 3 * d)
    mod1 = mod_mine[1].reshape(nb, 1, 3 * d)

    s_ao = halves_start([cast_into_slot(a_w_out[0], chip, mod_mine, "cast_a_out")], mod_mine, "gather_a_out_start")
    landed_a = halves_wait(s_a[0], s_a[1], s_a[2], s_ao[3], "gather_a_in_wait")
    s_bi = halves_start([land_b_in], landed_a[0], "gather_b_in_start")
    s_bo = gather_start(land_b_out, s_bi[3], "gather_b_out_start")
    (wa_in,) = pass_halves(list(landed_a), "gather_a_in_pass")
    di = a_w_out.shape[1] * N_CHIPS

    x0 = x.reshape(m_rows, d)
    tgt = loss_target.reshape(m_rows, d)
    ng0 = norm_gain[0:1] + (s_bi[3][0, 0] + s_bo[3][0, 0])
    ng1 = norm_gain[1:2]
    bs_col = a_b_s[0].reshape(SG_GROUPS, SG_BLOCK, 1)
    proj_a, h_a = inproj_fwd(x0, mod0, ng0, wa_in, seq, False, "a_inproj")
    y_a = sgu_fwd(proj_a, a_ln_gain, a_ln_bias, a_w_s[0], bs_col, "a_sgu")
    (wa_out,) = pass_halves(list(halves_wait(s_ao[0], s_ao[1], s_ao[2], y_a, "gather_a_out_wait")), "gather_a_out_pass")
    wa_out = wa_out.reshape(di, d)
    x1, out_a = outproj_fwd(y_a, wa_out, x0, mod0, seq, "a_outproj")
    (wb_in,) = pass_halves(list(halves_wait(s_bi[0], s_bi[1], s_bi[2], out_a, "gather_b_in_wait")), "gather_b_in_pass")
    proj_b, h_b = inproj_fwd(x1, mod1, ng1, wb_in, seq, True, "b_inproj")
    y_b, sts_b = hgrn_fwd(proj_b, b_lower_bounds, b_gn_gain, seq, "b_hgrn")
    wb_out = gather_wait(*s_bo[:3], y_b, "gather_b_out_wait").reshape(di, d)
    out_b, loss_part, dx2, dfg = outproj_loss(
        y_b, wb_out, x1, mod1, final_gain.reshape(1, d), tgt, seq, "b_outproj_loss")

    shard_rows = di // N_CHIPS
    dy_b, dout_b, dgate1 = outproj_bwd(dx2, out_b, mod1, wb_out, seq, "b_outproj_bwd")
    gwb_out = grad_w_out(y_b, dout_b, "b_grad_w_out").reshape(N_CHIPS, shard_rows, d)
    e_bo = exchange_start(gwb_out, "exchange_b_out_start")
    dproj_b, dlb, dgn = hgrn_bwd(
        proj_b, dy_b, sts_b, b_lower_bounds, b_gn_gain + e_bo[4][0, 0], seq, "b_hgrn_bwd")
    e_bi = exchange_start(grad_w_in(h_b, dproj_b, N_CHIPS, True, "b_grad_w_in"), "exchange_b_in_start")
    dx1, dshift1, dscale1, dng1 = inproj_bwd(
        dproj_b, wb_in, x1, dx2, mod1, ng1 + e_bi[4][0, 0], seq, True, "b_inproj_bwd")

    dy_a, dout_a, dgate0 = outproj_bwd(dx1, out_a, mod0, wa_out, seq, "a_outproj_bwd")
    gwa_out = grad_w_out(y_a, dout_a, "a_grad_w_out").reshape(N_CHIPS, shard_rows, d)
    e_ao = exchange_start(gwa_out, "exchange_a_out_start")
    dproj_a, dws, dbs, dlg, dlbias = sgu_bwd(
        proj_a, dy_a, a_ln_gain + e_ao[4][0, 0], a_ln_bias, a_w_s[0], bs_col, "a_sgu_bwd")
    e_ai = exchange_start(grad_w_in(h_a, dproj_a, N_CHIPS, False, "a_grad_w_in"), "exchange_a_in_start")
    dx0, dshift0, dscale0, dng0 = inproj_bwd(
        dproj_a, wa_in, x0, dx1, mod0, norm_gain[0:1] + e_ai[4][0, 0], seq, False, "a_inproj_bwd")
    grad_x = dx0.reshape(nb, seq, d)

    dmod = jnp.concatenate([dshift0, dscale0, dgate0, dshift1, dscale1, dgate1], axis=2)
    n_dmod = dmod.size
    small_g = [jnp.concatenate([dng0, dng1], axis=0), dlg, dlbias, dws, dbs, dlb, dfg, dgn]
    packed_g = _pack([dmod] + small_g + [loss_part])
    rows = packed_g.shape[0]
    s_small = gather_all_start(
        lax.dynamic_update_slice(jnp.zeros((N_DEV, rows, LANES), F32), packed_g[None], (dev, 0, 0)),
        "gather_small_start")

    def finish(group, after):
        mine = []
        for ex, _, _, _, nm in group:
            parts_thru, land = exchange_wait(ex[0], ex[1], ex[2], ex[3], after, "exchange_" + nm + "_wait")
            mine.append(sum_parts(parts_thru, land, chip, "sum_" + nm))
            after = mine[-1]
        theirs = swap_sibling(mine, "swap_" + group[0][4])
        return [[r.reshape(w.shape) for r in adamw_pair(pa, pb, w[0], m[0], v[0], "adamw_" + nm)]
                for pa, pb, (_, w, m, v, nm) in zip(mine, theirs, group)]

    (gb_out, db_out, mb_out, vb_out), (gb_in, db_in, mb_in, vb_in), (ga_out, da_out, ma_out, va_out) = finish(
        [(e_bo, b_w_out, m_b_w_out, v_b_w_out, "b_out"), (e_bi, b_w_in, m_b_w_in, v_b_w_in, "b_in"),
         (e_ao, a_w_out, m_a_w_out, v_a_w_out, "a_out")], s_small[3])
    ((ga_in, da_in, ma_in, va_in),) = finish([(e_ai, a_w_in, m_a_w_in, v_a_w_in, "a_in")], ga_out)

    small_w = [norm_gain, a_ln_gain, a_ln_bias, a_w_s, a_b_s, b_lower_bounds, final_gain, b_gn_gain]
    small_m = [m_norm_gain, m_a_ln_gain, m_a_ln_bias, m_a_w_s, m_a_b_s, m_b_lower_bounds, m_final_gain, m_b_gn_gain]
    small_v = [v_norm_gain, v_a_ln_gain, v_a_ln_bias, v_a_w_s, v_a_b_s, v_b_lower_bounds, v_final_gain, v_b_gn_gain]
    rows_of = lambda a: a.reshape(-1, a.shape[-1])
    gathered = gather_all_wait(s_small[0], s_small[1], s_small[2], ga_in, "gather_small_wait")
    tail, small_res = small_update(
        gathered, n_dmod // LANES, [rows_of(a) for a in small_w], [rows_of(a) for a in small_m],
        [rows_of(a) for a in small_v], "small_update")
    loss = tail[0, 0]
    sg, sd, sm, sv = [[small_res[p][kind].reshape(w.shape) for p, w in enumerate(small_w)] for kind in range(4)]

    dmod_all = gathered[:, :n_dmod // LANES].reshape(N_DEV * nb, n_l, 3 * d)
    dmod_cols = lax.dynamic_slice_in_dim(dmod_all, chip * ada_cols, ada_cols, axis=2)
    dmod_cols = jnp.transpose(dmod_cols, (1, 0, 2))
    g_wada, d_wada, m_wada, v_wada = ada_bwd(c_all, dmod_cols, w_ada, m_w_ada, v_w_ada, "ada_bwd")
    flat = lambda a: a.reshape(1, -1)
    g_bada, d_bada, m_bada, v_bada = [
        r.reshape(b_ada.shape) for r in
        bias_update(dmod_all.reshape(N_DEV * nb, n_l * 3 * d), flat(b_ada), flat(m_b_ada), flat(v_b_ada), "bias_update")]

    def order(ng, wada, bada, ain, sm_rest, aout, bin_, bout):
        lg, lbi, ws_, bs_, lbd, fg_, gn_ = sm_rest
        return [ng, wada, bada, ain, lg, lbi, ws_, bs_, aout, bin_, lbd, gn_, bout, fg_]

    grads = order(sg[0], g_wada, g_bada, ga_in, sg[1:8], ga_out, gb_in, gb_out)
    deltas = order(sd[0], d_wada, d_bada, da_in, sd[1:8], da_out, db_in, db_out)
    new_m = order(sm[0], m_wada, m_bada, ma_in, sm[1:8], ma_out, mb_in, mb_out)
    new_v = order(sv[0], v_wada, v_bada, va_in, sv[1:8], va_out, vb_in, vb_out)
    return (loss, grad_x, *grads, *deltas, *new_m, *new_v)
```
